```python
import jax, jax.numpy as jnp
from jax import lax
import numpy as np

D_MODEL = 1024
BATCH = 8
SEQ = 4096
DEPTH = 2

CHUNK = 128
RET_HEADS = 4
RET_HEAD_DIM = 128
RET_W = RET_HEADS * RET_HEAD_DIM
SB_HEADS = 8
SB_HEAD_DIM = 64
SB_W = SB_HEADS * SB_HEAD_DIM
SGU_GROUPS = 4
SGU_GROUP_DIM = 128
SGU_W = SGU_GROUPS * SGU_GROUP_DIM
D_FF = 4 * D_MODEL
ROPE_BASE = 10000.0
LN_EPS = 1e-5
DEEPNORM_ALPHA = (2 * DEPTH) ** 0.25
DEEPNORM_BETA = (8 * DEPTH) ** -0.25
SPLITS = (RET_W, RET_W, RET_W, RET_W, SB_W, SB_W, SB_W, SGU_W, SGU_W, D_MODEL, D_MODEL, D_MODEL)
N_IN = sum(SPLITS)

kernel_name = "hybrid_ret_sb_sgu_deepnorm"


def layer_norm(x, g, b):
    xf = x.astype(jnp.float32)
    mu = jnp.mean(xf, axis=-1, keepdims=True)
    var = jnp.mean(jnp.square(xf - mu), axis=-1, keepdims=True)
    y = (xf - mu) * lax.rsqrt(var + LN_EPS)
    return (y * g.astype(jnp.float32) + b.astype(jnp.float32)).astype(x.dtype)


def head_group_norm(o, g, b):
    B, S, H, D = o.shape
    of = o.astype(jnp.float32)
    mu = jnp.mean(of, axis=-1, keepdims=True)
    var = jnp.mean(jnp.square(of - mu), axis=-1, keepdims=True)
    y = ((of - mu) * lax.rsqrt(var + LN_EPS)).reshape(B, S, H * D)
    return (y * g.astype(jnp.float32) + b.astype(jnp.float32)).astype(o.dtype)


def rotary(x, pos):
    half = x.shape[-1] // 2
    inv_freq = ROPE_BASE ** (-jnp.arange(half, dtype=jnp.float32) / half)
    ang = pos.astype(jnp.float32)[:, None] * inv_freq[None, :]
    cos = jnp.cos(ang)[None, :, None, :].astype(x.dtype)
    sin = jnp.sin(ang)[None, :, None, :].astype(x.dtype)
    x1, x2 = x[..., :half], x[..., half:]
    return jnp.concatenate([x1 * cos - x2 * sin, x2 * cos + x1 * sin], axis=-1)


def retention(q, k, v):
    B, S, H, D = q.shape
    N = S // CHUNK
    dt = q.dtype
    log_g = jnp.log(1.0 - 2.0 ** (-5.0 - jnp.arange(H, dtype=jnp.float32)))
    idx = jnp.arange(CHUNK, dtype=jnp.float32)
    diff = idx[:, None] - idx[None, :]
    intra_decay = jnp.where(diff[None] >= 0, jnp.exp(log_g[:, None, None] * diff[None]), 0.0).astype(dt)
    k_decay = jnp.exp(log_g[:, None] * (CHUNK - 1 - idx)[None, :]).T.astype(dt)
    q_decay = jnp.exp(log_g[:, None] * (idx + 1.0)[None, :]).T.astype(dt)
    chunk_decay = jnp.exp(log_g * CHUNK).astype(dt)

    qc = q.reshape(B, N, CHUNK, H, D)
    kc = k.reshape(B, N, CHUNK, H, D)
    vc = v.reshape(B, N, CHUNK, H, D)

    scores = jnp.einsum('bnihd,bnjhd->bnhij', qc, kc) * intra_decay
    intra = jnp.einsum('bnhij,bnjhe->bnihe', scores, vc)

    kv = jnp.einsum('bnjhd,bnjhe->nbhde', kc * k_decay[:, :, None], vc)

    def step(state, kv_n):
        new_state = state * chunk_decay[None, :, None, None] + kv_n
        return new_state, state

    state0 = jnp.zeros((B, H, D, D), dt)
    _, prev_states = lax.scan(step, state0, kv)
    inter = jnp.einsum('bnihd,nbhde->bnihe', qc * q_decay[:, :, None], prev_states)
    return (intra + inter).reshape(B, S, H, D)


def stick_breaking(q, k, v):
    B, S, H, D = q.shape
    NB = S // CHUNK
    scale = D ** -0.5
    qb = q.reshape(B, NB, CHUNK, H, D).transpose(1, 0, 2, 3, 4)
    s_pos = jnp.arange(S)

    def block(args):
        qn, n = args
        z = jnp.einsum('bihd,bshd->bhis', qn, k).astype(jnp.float32) * scale
        t_pos = n * CHUNK + jnp.arange(CHUNK)
        mask = s_pos[None, :] < t_pos[:, None]
        log_1m_beta = jnp.where(mask, jax.nn.log_sigmoid(-z), 0.0)
        later = lax.cumsum(log_1m_beta, axis=3, reverse=True) - log_1m_beta
        a = jnp.where(mask, jnp.exp(jax.nn.log_sigmoid(z) + later), 0.0)
        return jnp.einsum('bhis,bshd->bihd', a.astype(v.dtype), v)

    out = lax.map(block, (qb, jnp.arange(NB)))
    return out.transpose(1, 0, 2, 3, 4).reshape(B, S, H * D)


def chunked_sgu(u, v, ln_g, ln_b, w_s, b_s):
    B, S, _ = v.shape
    N = S // CHUNK
    v = layer_norm(v, ln_g, ln_b)
    vg = v.reshape(B, N, CHUNK, SGU_GROUPS, SGU_GROUP_DIM)
    causal = jnp.tril(jnp.ones((CHUNK, CHUNK), dtype=w_s.dtype))
    w = w_s * causal[None]
    sv = jnp.einsum('gij,bnjgc->bnigc', w, vg) + b_s.T[None, None, :, :, None]
    return u * sv.reshape(B, S, SGU_W)


def mixer(x, w_in, ret_gn_g, ret_gn_b, sgu_ln_g, sgu_ln_b, sgu_w, sgu_b, p_ret, p_sb, p_sgu, w_out):
    B, S, _ = x.shape
    proj = x @ w_in
    points = [int(p) for p in np.cumsum(SPLITS)[:-1]]
    (rq, rk, rv, rg, sq, sk, sv, gu, gv, gate_ret, gate_sb, gate_sgu) = jnp.split(proj, points, axis=-1)
    pos = jnp.arange(S, dtype=jnp.int32)

    rq = rotary(rq.reshape(B, S, RET_HEADS, RET_HEAD_DIM), pos)
    rk = rotary(rk.reshape(B, S, RET_HEADS, RET_HEAD_DIM), pos) * (RET_HEAD_DIM ** -0.5)
    ret = retention(rq, rk, rv.reshape(B, S, RET_HEADS, RET_HEAD_DIM))
    ret = jax.nn.silu(rg) * head_group_norm(ret, ret_gn_g, ret_gn_b)

    sb = stick_breaking(sq.reshape(B, S, SB_HEADS, SB_HEAD_DIM),
                        sk.reshape(B, S, SB_HEADS, SB_HEAD_DIM),
                        sv.reshape(B, S, SB_HEADS, SB_HEAD_DIM))

    sg = chunked_sgu(jax.nn.gelu(gu), jax.nn.gelu(gv), sgu_ln_g, sgu_ln_b, sgu_w, sgu_b)

    merged = (jax.nn.sigmoid(gate_ret) * (ret @ p_ret)
              + jax.nn.sigmoid(gate_sb) * (sb @ p_sb)
              + jax.nn.sigmoid(gate_sgu) * (sg @ p_sgu))
    return merged @ w_out


def _fwd_setup_inputs(seed: int = 0) -> dict:
    key = jax.random.key(seed)
    ks = jax.random.split(key, 20)
    L = DEPTH
    f32 = jnp.float32

    def nrm(k, shape, scale):
        return jax.random.normal(k, shape, f32) * scale

    return {
        "x": jax.random.normal(ks[0], (BATCH, SEQ, D_MODEL), f32),
        "w_in": nrm(ks[1], (L, D_MODEL, N_IN), D_MODEL ** -0.5),
        "ret_gn_g": 1.0 + nrm(ks[2], (L, RET_W), 0.02),
        "ret_gn_b": nrm(ks[3], (L, RET_W), 0.02),
        "sgu_ln_g": 1.0 + nrm(ks[4], (L, SGU_W), 0.02),
        "sgu_ln_b": nrm(ks[5], (L, SGU_W), 0.02),
        "sgu_w": nrm(ks[6], (L, SGU_GROUPS, CHUNK, CHUNK), CHUNK ** -0.5),
        "sgu_b": 1.0 + nrm(ks[7], (L, SGU_GROUPS, CHUNK), 0.01),
        "p_ret": nrm(ks[8], (L, RET_W, D_MODEL), RET_W ** -0.5 * DEEPNORM_BETA),
        "p_sb": nrm(ks[9], (L, SB_W, D_MODEL), SB_W ** -0.5 * DEEPNORM_BETA),
        "p_sgu": nrm(ks[10], (L, SGU_W, D_MODEL), SGU_W ** -0.5 * DEEPNORM_BETA),
        "w_out": nrm(ks[11], (L, D_MODEL, D_MODEL), D_MODEL ** -0.5 * DEEPNORM_BETA),
        "ln1_g": 1.0 + nrm(ks[12], (L, D_MODEL), 0.02),
        "ln1_b": nrm(ks[13], (L, D_MODEL), 0.02),
        "w_up": nrm(ks[14], (L, D_MODEL, D_FF), D_MODEL ** -0.5 * DEEPNORM_BETA),
        "w_down": nrm(ks[15], (L, D_FF, D_MODEL), D_FF ** -0.5 * DEEPNORM_BETA),
        "ln2_g": 1.0 + nrm(ks[16], (L, D_MODEL), 0.02),
        "ln2_b": nrm(ks[17], (L, D_MODEL), 0.02),
    }


def _fwd_reference(x, w_in, ret_gn_g, ret_gn_b, sgu_ln_g, sgu_ln_b, sgu_w, sgu_b, p_ret, p_sb, p_sgu,
              w_out, ln1_g, ln1_b, w_up, w_down, ln2_g, ln2_b):
    for l in range(DEPTH):
        y = mixer(x, w_in[l], ret_gn_g[l], ret_gn_b[l], sgu_ln_g[l], sgu_ln_b[l], sgu_w[l], sgu_b[l],
                  p_ret[l], p_sb[l], p_sgu[l], w_out[l])
        x = layer_norm(DEEPNORM_ALPHA * x + y, ln1_g[l], ln1_b[l])
        h = jnp.square(jax.nn.relu(x @ w_up[l])) @ w_down[l]
        x = layer_norm(DEEPNORM_ALPHA * x + h, ln2_g[l], ln2_b[l])
    return x


import jax as _jax
import jax.numpy as _jnp

TWIN_FORMAT = 'train_step'
FWD_PARAMS = ['x', 'w_in', 'ret_gn_g', 'ret_gn_b', 'sgu_ln_g', 'sgu_ln_b', 'sgu_w', 'sgu_b', 'p_ret', 'p_sb', 'p_sgu', 'w_out', 'ln1_g', 'ln1_b', 'w_up', 'w_down', 'ln2_g', 'ln2_b']
TWIN_WEIGHTS = ['w_in', 'ret_gn_g', 'ret_gn_b', 'sgu_ln_g', 'sgu_ln_b', 'sgu_w', 'sgu_b', 'p_ret', 'p_sb', 'p_sgu', 'w_out', 'ln1_g', 'ln1_b', 'w_up', 'w_down', 'ln2_g', 'ln2_b']
TWIN_DIFF_INPUT = 'x'
TWIN_INPUTS = ['x', 'w_in', 'ret_gn_g', 'ret_gn_b', 'sgu_ln_g', 'sgu_ln_b', 'sgu_w', 'sgu_b', 'p_ret', 'p_sb', 'p_sgu', 'w_out', 'ln1_g', 'ln1_b', 'w_up', 'w_down', 'ln2_g', 'ln2_b', 'loss_target', 'm_w_in', 'm_ret_gn_g', 'm_ret_gn_b', 'm_sgu_ln_g', 'm_sgu_ln_b', 'm_sgu_w', 'm_sgu_b', 'm_p_ret', 'm_p_sb', 'm_p_sgu', 'm_w_out', 'm_ln1_g', 'm_ln1_b', 'm_w_up', 'm_w_down', 'm_ln2_g', 'm_ln2_b', 'v_w_in', 'v_ret_gn_g', 'v_ret_gn_b', 'v_sgu_ln_g', 'v_sgu_ln_b', 'v_sgu_w', 'v_sgu_b', 'v_p_ret', 'v_p_sb', 'v_p_sgu', 'v_w_out', 'v_ln1_g', 'v_ln1_b', 'v_w_up', 'v_w_down', 'v_ln2_g', 'v_ln2_b']
TWIN_OUTPUTS = ['loss', 'grad_x', 'grad_w_in', 'grad_ret_gn_g', 'grad_ret_gn_b', 'grad_sgu_ln_g', 'grad_sgu_ln_b', 'grad_sgu_w', 'grad_sgu_b', 'grad_p_ret', 'grad_p_sb', 'grad_p_sgu', 'grad_w_out', 'grad_ln1_g', 'grad_ln1_b', 'grad_w_up', 'grad_w_down', 'grad_ln2_g', 'grad_ln2_b', 'delta_w_in', 'delta_ret_gn_g', 'delta_ret_gn_b', 'delta_sgu_ln_g', 'delta_sgu_ln_b', 'delta_sgu_w', 'delta_sgu_b', 'delta_p_ret', 'delta_p_sb', 'delta_p_sgu', 'delta_w_out', 'delta_ln1_g', 'delta_ln1_b', 'delta_w_up', 'delta_w_down', 'delta_ln2_g', 'delta_ln2_b', 'new_m_w_in', 'new_m_ret_gn_g', 'new_m_ret_gn_b', 'new_m_sgu_ln_g', 'new_m_sgu_ln_b', 'new_m_sgu_w', 'new_m_sgu_b', 'new_m_p_ret', 'new_m_p_sb', 'new_m_p_sgu', 'new_m_w_out', 'new_m_ln1_g', 'new_m_ln1_b', 'new_m_w_up', 'new_m_w_down', 'new_m_ln2_g', 'new_m_ln2_b', 'new_v_w_in', 'new_v_ret_gn_g', 'new_v_ret_gn_b', 'new_v_sgu_ln_g', 'new_v_sgu_ln_b', 'new_v_sgu_w', 'new_v_sgu_b', 'new_v_p_ret', 'new_v_p_sb', 'new_v_p_sgu', 'new_v_w_out', 'new_v_ln1_g', 'new_v_ln1_b', 'new_v_w_up', 'new_v_w_down', 'new_v_ln2_g', 'new_v_ln2_b']
TWIN_LEAF_KINDS = {'loss': 'loss', 'grad_x': 'grad_x', 'grad_w_in': 'grad_w', 'grad_ret_gn_g': 'grad_w', 'grad_ret_gn_b': 'grad_w', 'grad_sgu_ln_g': 'grad_w', 'grad_sgu_ln_b': 'grad_w', 'grad_sgu_w': 'grad_w', 'grad_sgu_b': 'grad_w', 'grad_p_ret': 'grad_w', 'grad_p_sb': 'grad_w', 'grad_p_sgu': 'grad_w', 'grad_w_out': 'grad_w', 'grad_ln1_g': 'grad_w', 'grad_ln1_b': 'grad_w', 'grad_w_up': 'grad_w', 'grad_w_down': 'grad_w', 'grad_ln2_g': 'grad_w', 'grad_ln2_b': 'grad_w', 'delta_w_in': 'delta_w', 'delta_ret_gn_g': 'delta_w', 'delta_ret_gn_b': 'delta_w', 'delta_sgu_ln_g': 'delta_w', 'delta_sgu_ln_b': 'delta_w', 'delta_sgu_w': 'delta_w', 'delta_sgu_b': 'delta_w', 'delta_p_ret': 'delta_w', 'delta_p_sb': 'delta_w', 'delta_p_sgu': 'delta_w', 'delta_w_out': 'delta_w', 'delta_ln1_g': 'delta_w', 'delta_ln1_b': 'delta_w', 'delta_w_up': 'delta_w', 'delta_w_down': 'delta_w', 'delta_ln2_g': 'delta_w', 'delta_ln2_b': 'delta_w', 'new_m_w_in': 'new_m', 'new_m_ret_gn_g': 'new_m', 'new_m_ret_gn_b': 'new_m', 'new_m_sgu_ln_g': 'new_m', 'new_m_sgu_ln_b': 'new_m', 'new_m_sgu_w': 'new_m', 'new_m_sgu_b': 'new_m', 'new_m_p_ret': 'new_m', 'new_m_p_sb': 'new_m', 'new_m_p_sgu': 'new_m', 'new_m_w_out': 'new_m', 'new_m_ln1_g': 'new_m', 'new_m_ln1_b': 'new_m', 'new_m_w_up': 'new_m', 'new_m_w_down': 'new_m', 'new_m_ln2_g': 'new_m', 'new_m_ln2_b': 'new_m', 'new_v_w_in': 'new_v', 'new_v_ret_gn_g': 'new_v', 'new_v_ret_gn_b': 'new_v', 'new_v_sgu_ln_g': 'new_v', 'new_v_sgu_ln_b': 'new_v', 'new_v_sgu_w': 'new_v', 'new_v_sgu_b': 'new_v', 'new_v_p_ret': 'new_v', 'new_v_p_sb': 'new_v', 'new_v_p_sgu': 'new_v', 'new_v_w_out': 'new_v', 'new_v_ln1_g': 'new_v', 'new_v_ln1_b': 'new_v', 'new_v_w_up': 'new_v', 'new_v_w_down': 'new_v', 'new_v_ln2_g': 'new_v', 'new_v_ln2_b': 'new_v'}


def _forward(args):
    return _fwd_reference(*[args[k] for k in FWD_PARAMS])


def _output_shape():
    def fwd():
        inp = _fwd_setup_inputs(0)
        return _fwd_reference(*[inp[k] for k in FWD_PARAMS])
    out = _jax.eval_shape(fwd)
    return out.shape, out.dtype

N_MICROBATCH = 1
ADAM_LR = 0.001
ADAM_B1 = 0.9
ADAM_B2 = 0.999
ADAM_EPS = 1e-08
ADAM_WD = 0.01
ADAM_STEP = 10
PER_EXAMPLE_BATCH_AXIS = {'x': 0, 'loss_target': 0}
SHARED_INPUTS = []
_WEIGHT_DTYPES = {'w_in': _jnp.float32, 'ret_gn_g': _jnp.float32, 'ret_gn_b': _jnp.float32, 'sgu_ln_g': _jnp.float32, 'sgu_ln_b': _jnp.float32, 'sgu_w': _jnp.float32, 'sgu_b': _jnp.float32, 'p_ret': _jnp.float32, 'p_sb': _jnp.float32, 'p_sgu': _jnp.float32, 'w_out': _jnp.float32, 'ln1_g': _jnp.float32, 'ln1_b': _jnp.float32, 'w_up': _jnp.float32, 'w_down': _jnp.float32, 'ln2_g': _jnp.float32, 'ln2_b': _jnp.float32}
MOMENT_SCALE = {'w_in': 1.442608e-02, 'ret_gn_g': 1.446406e-02, 'ret_gn_b': 1.719321e-02, 'sgu_ln_g': 1.181612e-02, 'sgu_ln_b': 1.196979e-02, 'sgu_w': 1.107541e-02, 'sgu_b': 1.620256e-02, 'p_ret': 2.044835e-02, 'p_sb': 2.201698e-02, 'p_sgu': 3.545497e-02, 'w_out': 4.590889e-02, 'ln1_g': 1.151714e+00, 'ln1_b': 5.813955e-01, 'w_up': 2.247607e-02, 'w_down': 5.689541e-02, 'ln2_g': 2.267865e+01, 'ln2_b': 1.794330e+00}


def _to_microbatches(a, axis):
    t = _jnp.moveaxis(a, axis, 0)
    t = t.reshape((N_MICROBATCH, t.shape[0] // N_MICROBATCH) + t.shape[1:])
    return _jnp.moveaxis(t, 1, axis + 1)


def setup_inputs(seed: int = 0) -> dict:
    inp = _fwd_setup_inputs(seed)
    key = _jax.random.fold_in(_jax.random.key(seed), 7919)
    shape, _ = _output_shape()
    out = dict(inp)
    out["loss_target"] = _jax.random.normal(_jax.random.fold_in(key, 0), shape, _jnp.float32)
    for i, name in enumerate(TWIN_WEIGHTS):
        w = inp[name].astype(_jnp.float32)
        if MOMENT_SCALE is None:
            s = _jnp.sqrt(_jnp.mean(_jnp.square(w)) + 1e-30)
        else:
            s = MOMENT_SCALE[name]
        km, kv = _jax.random.split(_jax.random.fold_in(key, i + 1))
        out[name] = w
        out["m_" + name] = s * _jax.random.normal(km, w.shape, _jnp.float32)
        out["v_" + name] = (s * s) * _jax.random.uniform(kv, w.shape, _jnp.float32, 0.5, 1.5)
    if N_MICROBATCH > 1:
        for name, axis in PER_EXAMPLE_BATCH_AXIS.items():
            out[name] = _to_microbatches(out[name], axis)
    return {'x': out['x'], 'w_in': out['w_in'], 'ret_gn_g': out['ret_gn_g'], 'ret_gn_b': out['ret_gn_b'], 'sgu_ln_g': out['sgu_ln_g'], 'sgu_ln_b': out['sgu_ln_b'], 'sgu_w': out['sgu_w'], 'sgu_b': out['sgu_b'], 'p_ret': out['p_ret'], 'p_sb': out['p_sb'], 'p_sgu': out['p_sgu'], 'w_out': out['w_out'], 'ln1_g': out['ln1_g'], 'ln1_b': out['ln1_b'], 'w_up': out['w_up'], 'w_down': out['w_down'], 'ln2_g': out['ln2_g'], 'ln2_b': out['ln2_b'], 'loss_target': out['loss_target'], 'm_w_in': out['m_w_in'], 'm_ret_gn_g': out['m_ret_gn_g'], 'm_ret_gn_b': out['m_ret_gn_b'], 'm_sgu_ln_g': out['m_sgu_ln_g'], 'm_sgu_ln_b': out['m_sgu_ln_b'], 'm_sgu_w': out['m_sgu_w'], 'm_sgu_b': out['m_sgu_b'], 'm_p_ret': out['m_p_ret'], 'm_p_sb': out['m_p_sb'], 'm_p_sgu': out['m_p_sgu'], 'm_w_out': out['m_w_out'], 'm_ln1_g': out['m_ln1_g'], 'm_ln1_b': out['m_ln1_b'], 'm_w_up': out['m_w_up'], 'm_w_down': out['m_w_down'], 'm_ln2_g': out['m_ln2_g'], 'm_ln2_b': out['m_ln2_b'], 'v_w_in': out['v_w_in'], 'v_ret_gn_g': out['v_ret_gn_g'], 'v_ret_gn_b': out['v_ret_gn_b'], 'v_sgu_ln_g': out['v_sgu_ln_g'], 'v_sgu_ln_b': out['v_sgu_ln_b'], 'v_sgu_w': out['v_sgu_w'], 'v_sgu_b': out['v_sgu_b'], 'v_p_ret': out['v_p_ret'], 'v_p_sb': out['v_p_sb'], 'v_p_sgu': out['v_p_sgu'], 'v_w_out': out['v_w_out'], 'v_ln1_g': out['v_ln1_g'], 'v_ln1_b': out['v_ln1_b'], 'v_w_up': out['v_w_up'], 'v_w_down': out['v_w_down'], 'v_ln2_g': out['v_ln2_g'], 'v_ln2_b': out['v_ln2_b']}


def _loss(weights, diff, rest, loss_target):
    with _jax.named_scope("forward"):
        args = {**rest, TWIN_DIFF_INPUT: diff, **{k: w.astype(_WEIGHT_DTYPES[k]) for k, w in weights.items()}}
        y = _forward(args)
    with _jax.named_scope("loss_head"):
        err = _jnp.square(y.astype(_jnp.float32) - loss_target)
        return 0.5 * _jnp.sum(_jnp.mean(err, axis=-1)) if err.ndim else 0.5 * err


def _adamw(w, g, m, v):
    m = ADAM_B1 * m + (1.0 - ADAM_B1) * g
    v = ADAM_B2 * v + (1.0 - ADAM_B2) * _jnp.square(g)
    m_hat = m / (1.0 - ADAM_B1 ** ADAM_STEP)
    v_hat = v / (1.0 - ADAM_B2 ** ADAM_STEP)
    delta = -ADAM_LR * (m_hat / (_jnp.sqrt(v_hat) + ADAM_EPS) + ADAM_WD * w)
    return delta, m, v


def reference(x, w_in, ret_gn_g, ret_gn_b, sgu_ln_g, sgu_ln_b, sgu_w, sgu_b, p_ret, p_sb, p_sgu, w_out, ln1_g, ln1_b, w_up, w_down, ln2_g, ln2_b, loss_target, m_w_in, m_ret_gn_g, m_ret_gn_b, m_sgu_ln_g, m_sgu_ln_b, m_sgu_w, m_sgu_b, m_p_ret, m_p_sb, m_p_sgu, m_w_out, m_ln1_g, m_ln1_b, m_w_up, m_w_down, m_ln2_g, m_ln2_b, v_w_in, v_ret_gn_g, v_ret_gn_b, v_sgu_ln_g, v_sgu_ln_b, v_sgu_w, v_sgu_b, v_p_ret, v_p_sb, v_p_sgu, v_w_out, v_ln1_g, v_ln1_b, v_w_up, v_w_down, v_ln2_g, v_ln2_b):
    given = dict(x=x, w_in=w_in, ret_gn_g=ret_gn_g, ret_gn_b=ret_gn_b, sgu_ln_g=sgu_ln_g, sgu_ln_b=sgu_ln_b, sgu_w=sgu_w, sgu_b=sgu_b, p_ret=p_ret, p_sb=p_sb, p_sgu=p_sgu, w_out=w_out, ln1_g=ln1_g, ln1_b=ln1_b, w_up=w_up, w_down=w_down, ln2_g=ln2_g, ln2_b=ln2_b, loss_target=loss_target, m_w_in=m_w_in, m_ret_gn_g=m_ret_gn_g, m_ret_gn_b=m_ret_gn_b, m_sgu_ln_g=m_sgu_ln_g, m_sgu_ln_b=m_sgu_ln_b, m_sgu_w=m_sgu_w, m_sgu_b=m_sgu_b, m_p_ret=m_p_ret, m_p_sb=m_p_sb, m_p_sgu=m_p_sgu, m_w_out=m_w_out, m_ln1_g=m_ln1_g, m_ln1_b=m_ln1_b, m_w_up=m_w_up, m_w_down=m_w_down, m_ln2_g=m_ln2_g, m_ln2_b=m_ln2_b, v_w_in=v_w_in, v_ret_gn_g=v_ret_gn_g, v_ret_gn_b=v_ret_gn_b, v_sgu_ln_g=v_sgu_ln_g, v_sgu_ln_b=v_sgu_ln_b, v_sgu_w=v_sgu_w, v_sgu_b=v_sgu_b, v_p_ret=v_p_ret, v_p_sb=v_p_sb, v_p_sgu=v_p_sgu, v_w_out=v_w_out, v_ln1_g=v_ln1_g, v_ln1_b=v_ln1_b, v_w_up=v_w_up, v_w_down=v_w_down, v_ln2_g=v_ln2_g, v_ln2_b=v_ln2_b)
    weights = {n: given[n] for n in TWIN_WEIGHTS}
    shared = {n: given[n] for n in SHARED_INPUTS}
    per_example = {n: given[n] for n in ['x']}
    grad_fn = _jax.value_and_grad(_loss, argnums=(0, 1))

    def one_microbatch(ex, loss_target):
        ex = dict(ex)
        diff = ex.pop(TWIN_DIFF_INPUT)
        return grad_fn(weights, diff, {**shared, **ex}, loss_target)

    if N_MICROBATCH == 1:
        loss, (grad_w, grad_x) = one_microbatch(per_example, given["loss_target"])
    else:
        def body(carry, xs):
            loss_sum, grad_sum = carry
            l_k, (gw_k, gx_k) = one_microbatch(xs[0], xs[1])
            with _jax.named_scope("update"):
                return (loss_sum + l_k, _jax.tree.map(_jnp.add, grad_sum, gw_k)), gx_k

        init = (_jnp.zeros((), _jnp.float32), _jax.tree.map(_jnp.zeros_like, weights))
        (loss, grad_w), grad_x = _jax.lax.scan(body, init, (per_example, given["loss_target"]))
    with _jax.named_scope("update"):
        delta_w, new_m, new_v = {}, {}, {}
        for n in TWIN_WEIGHTS:
            delta_w[n], new_m[n], new_v[n] = _adamw(weights[n], grad_w[n], given["m_" + n], given["v_" + n])
    return (loss, grad_x, *[grad_w[n] for n in TWIN_WEIGHTS], *[delta_w[n] for n in TWIN_WEIGHTS],
            *[new_m[n] for n in TWIN_WEIGHTS], *[new_v[n] for n in TWIN_WEIGHTS])
```

```python
import functools
import math

import jax
import jax.numpy as jnp
from jax import lax
from jax.experimental import pallas as pl
from jax.experimental.pallas import tpu as pltpu

F32 = jnp.float32
BF16 = jnp.bfloat16

D_MODEL = 1024
SEQ = 4096
DEPTH = 2
CHUNK = 128
RET_HEADS = 4
BRANCH_W = 512
N_IN = 7680
D_FF = 4096
LN_EPS = 1e-5
ROPE_BASE = 10000.0
ALPHA = (2 * DEPTH) ** 0.25
RET_SCALE = 128 ** -0.5
SB_SCALE = 64 ** -0.5
C_RET, C_SB, C_SGU, C_GATE = 0, 2048, 3584, 4608

ADAM_LR, ADAM_B1, ADAM_B2, ADAM_EPS, ADAM_WD, ADAM_STEP = 0.001, 0.9, 0.999, 1e-08, 0.01, 10

N_CHIPS = 4
VMEM_LIMIT = 56 * 1024 * 1024
MESH = pl.DeviceIdType.MESH

NN = ((1,), (0,))
NT = ((1,), (1,))
TN = ((0,), (0,))


def _dot(a, b, dims):
    return lax.dot_general(a, b, (dims, ((), ())), preferred_element_type=F32)


def _params(sem):
    return pltpu.CompilerParams(dimension_semantics=sem, vmem_limit_bytes=VMEM_LIMIT)


def _relu2(h):
    r = jnp.maximum(h, 0.0)
    return r * r


def matmul(a, b, *, mode, tm, tn, tk, outs=((F32, None),), pro=None, epi=None, tiles=(), rows=(), name):
    if mode == "nn":
        (M, K), N = a.shape, b.shape[1]
    elif mode == "nt":
        (M, K), N = a.shape, b.shape[0]
    else:
        (K, M), N = a.shape, b.shape[1]
    tm, tn, tk = min(tm, M), min(tn, N), min(tk, K)
    assert M % tm == 0 and N % tn == 0 and K % tk == 0, (name, M, N, K, tm, tn, tk)
    if mode == "nn":
        a_spec = pl.BlockSpec((tm, tk), lambda i, j, k: (i, k))
        b_spec = pl.BlockSpec((tk, tn), lambda i, j, k: (k, j))
        dims = NN
    elif mode == "nt":
        a_spec = pl.BlockSpec((tm, tk), lambda i, j, k: (i, k))
        b_spec = pl.BlockSpec((tn, tk), lambda i, j, k: (j, k))
        dims = NT
    else:
        a_spec = pl.BlockSpec((tk, tm), lambda i, j, k: (k, i))
        b_spec = pl.BlockSpec((tk, tn), lambda i, j, k: (k, j))
        dims = TN
    nk = K // tk
    nt_, nr, no = len(tiles), len(rows), len(outs)

    def body(a_ref, b_ref, *rest):
        tile_refs = rest[:nt_]
        row_refs = rest[nt_:nt_ + nr]
        out_refs = rest[nt_ + nr:nt_ + nr + no]
        av = a_ref[...]
        if pro is not None:
            av = pro(av)
        p = _dot(av.astype(BF16), b_ref[...].astype(BF16), dims)

        def finish(acc):
            vals = (acc,) * no if epi is None else epi(acc, *[r[...] for r in tile_refs], *[r[...] for r in row_refs])
            for o_ref, v in zip(out_refs, vals):
                o_ref[...] = v.astype(o_ref.dtype)

        if nk == 1:
            finish(p)
        else:
            acc_ref = rest[-1]
            k = pl.program_id(2)

            @pl.when(k == 0)
            def _():
                acc_ref[...] = p

            @pl.when(k > 0)
            def _():
                acc_ref[...] += p

            @pl.when(k == nk - 1)
            def _():
                finish(acc_ref[...])

    out_shape, out_specs = [], []
    for dt, width in outs:
        if width is None:
            out_shape.append(jax.ShapeDtypeStruct((M, N), dt))
            out_specs.append(pl.BlockSpec((tm, tn), lambda i, j, k: (i, j)))
        else:
            assert N == tn
            out_shape.append(jax.ShapeDtypeStruct((M, width), dt))
            out_specs.append(pl.BlockSpec((tm, width), lambda i, j, k: (i, 0)))
    in_specs = [a_spec, b_spec]
    in_specs += [pl.BlockSpec((tm, tn), lambda i, j, k: (i, j)) for _ in tiles]
    in_specs += [pl.BlockSpec((1, tn), lambda i, j, k: (0, j)) for _ in rows]
    res = pl.pallas_call(
        body, name=name, grid=(M // tm, N // tn, nk),
        in_specs=in_specs, out_specs=out_specs, out_shape=out_shape,
        scratch_shapes=[pltpu.VMEM((tm, tn), F32)] if nk > 1 else [],
        compiler_params=_params(("parallel", "parallel", "arbitrary")),
    )(a, b, *tiles, *rows)
    return res[0] if no == 1 else res


def _ln_epi(acc, res, g, b):
    u = ALPHA * res + acc
    mu = jnp.mean(u, axis=-1, keepdims=True)
    xc = u - mu
    var = jnp.mean(xc * xc, axis=-1, keepdims=True)
    rstd = lax.rsqrt(var + LN_EPS)
    xhat = xc * rstd
    return xhat * g + b, xhat, jnp.broadcast_to(rstd, (u.shape[0], 128))


def matmul_ln(a, w, res, g, b, *, pro=None, tk, name):
    n = w.shape[1]
    return matmul(a, w, mode="nn", tm=512, tn=n, tk=tk, pro=pro, epi=_ln_epi, tiles=(res,), rows=(g, b),
                  outs=((F32, None), (F32, None), (F32, 128)), name=name)


def ln_bwd(dy, xhat, rstd, g, *, name):
    T, D = dy.shape
    tm = min(512, T)

    def body(dy_ref, xh_ref, rs_ref, g_ref, du_ref, dg_ref, db_ref):
        dyv, xh = dy_ref[...], xh_ref[...]
        r = rs_ref[:, 0:1]
        dxh = dyv * g_ref[...]
        m1 = jnp.mean(dxh, axis=-1, keepdims=True)
        m2 = jnp.mean(dxh * xh, axis=-1, keepdims=True)
        du_ref[...] = r * (dxh - m1 - xh * m2)

        @pl.when(pl.program_id(0) == 0)
        def _():
            dg_ref[...] = jnp.zeros_like(dg_ref)
            db_ref[...] = jnp.zeros_like(db_ref)

        dg_ref[...] += jnp.sum(dyv * xh, axis=0, keepdims=True)
        db_ref[...] += jnp.sum(dyv, axis=0, keepdims=True)

    row = pl.BlockSpec((tm, D), lambda i: (i, 0))
    vec = pl.BlockSpec((1, D), lambda i: (0, 0))
    return pl.pallas_call(
        body, name=name, grid=(T // tm,),
        in_specs=[row, row, pl.BlockSpec((tm, 128), lambda i: (i, 0)), vec],
        out_specs=[row, vec, vec],
        out_shape=[jax.ShapeDtypeStruct((T, D), F32), jax.ShapeDtypeStruct((1, D), F32), jax.ShapeDtypeStruct((1, D), F32)],
        compiler_params=_params(("arbitrary",)),
    )(dy, xhat, rstd, g)


def loss_head(y, target):
    T, D = y.shape
    tm = min(512, T)

    def body(y_ref, t_ref, dy_ref, s_ref):
        e = y_ref[...] - t_ref[...]
        dy_ref[...] = e * (1.0 / D)

        @pl.when(pl.program_id(0) == 0)
        def _():
            s_ref[...] = jnp.zeros_like(s_ref)

        s_ref[...] += jnp.sum(jnp.mean(e * e, axis=-1, keepdims=True))

    row = pl.BlockSpec((tm, D), lambda i: (i, 0))
    return pl.pallas_call(
        body, name="loss_head", grid=(T // tm,),
        in_specs=[row, row], out_specs=[row, pl.BlockSpec((8, 128), lambda i: (0, 0))],
        out_shape=[jax.ShapeDtypeStruct((T, D), F32), jax.ShapeDtypeStruct((8, 128), F32)],
        compiler_params=_params(("arbitrary",)),
    )(y, target)


def _rope_tables(T):
    half = 64
    inv_freq = ROPE_BASE ** (-jnp.arange(half, dtype=F32) / half)
    ang = jnp.arange(T, dtype=jnp.int32).astype(F32)[:, None] * inv_freq[None, :]
    cos, sin = jnp.cos(ang), jnp.sin(ang)
    return jnp.concatenate([cos, cos], axis=1), jnp.concatenate([-sin, sin], axis=1)


def _ret_consts():
    H = RET_HEADS
    log_g = jnp.log(1.0 - 2.0 ** (-5.0 - jnp.arange(H, dtype=F32)))
    idx = jnp.arange(CHUNK, dtype=F32)
    diff = idx[:, None] - idx[None, :]
    dmat = jnp.where(diff[None] >= 0, jnp.exp(log_g[:, None, None] * diff[None]), 0.0)
    kd = jnp.exp(log_g[:, None] * (CHUNK - 1 - idx)[None, :])
    qd = jnp.exp(log_g[:, None] * (idx + 1.0)[None, :])
    cd = jnp.exp(log_g * CHUNK)
    full = (H, CHUNK, CHUNK)
    return (dmat.astype(F32), jnp.broadcast_to(kd[:, :, None], full), jnp.broadcast_to(qd[:, :, None], full),
            jnp.broadcast_to(cd[:, None, None], full))


def _swap_halves(v):
    return pltpu.roll(v, 64, 1)


def _group_norm(o):
    mu = jnp.mean(o, axis=-1, keepdims=True)
    xc = o - mu
    var = jnp.mean(xc * xc, axis=-1, keepdims=True)
    rstd = lax.rsqrt(var + LN_EPS)
    return xc * rstd, rstd


def ret_fwd(proj, cosf, sinf, consts, gn_g, gn_b, *, name):
    T = proj.shape[0]
    tb = min(512, T)
    nch = tb // CHUNK
    H = RET_HEADS

    def body(p_ref, cos_ref, sin_ref, dm_ref, kd_ref, qd_ref, cd_ref, g_ref, b_ref, out_ref, raw_ref, st_ref, s_ref):
        @pl.when(pl.program_id(0) == 0)
        def _():
            s_ref[...] = jnp.zeros_like(s_ref)

        for c in range(nch):
            r = slice(c * CHUNK, (c + 1) * CHUNK)
            cs, sn = cos_ref[r, :], sin_ref[r, :]
            for h in range(H):
                hc = slice(h * 128, (h + 1) * 128)
                q = p_ref[r, h * 128:(h + 1) * 128]
                k = p_ref[r, 512 + h * 128:512 + (h + 1) * 128]
                v = p_ref[r, 1024 + h * 128:1024 + (h + 1) * 128]
                gt = p_ref[r, 1536 + h * 128:1536 + (h + 1) * 128]
                qr = q * cs + _swap_halves(q) * sn
                kr = (k * cs + _swap_halves(k) * sn) * RET_SCALE
                sprev = s_ref[h]
                st_ref[c, h] = sprev
                qb, kb, vb = qr.astype(BF16), kr.astype(BF16), v.astype(BF16)
                s = _dot(qb, kb, NT) * dm_ref[h]
                o = _dot(s.astype(BF16), vb, NN) + _dot((qr * qd_ref[h]).astype(BF16), sprev.astype(BF16), NN)
                s_ref[h] = sprev * cd_ref[h] + _dot((kr * kd_ref[h]).astype(BF16), vb, TN)
                raw_ref[r, hc] = o
                y, _ = _group_norm(o)
                out_ref[r, hc] = (gt * jax.nn.sigmoid(gt)) * (y * g_ref[:, hc] + b_ref[:, hc])

    cmat = pl.BlockSpec((H, CHUNK, CHUNK), lambda i: (0, 0, 0))
    vec = pl.BlockSpec((1, BRANCH_W), lambda i: (0, 0))
    rope = pl.BlockSpec((tb, 128), lambda i: (i, 0))
    blk = pl.BlockSpec((tb, BRANCH_W), lambda i: (i, 0))
    return pl.pallas_call(
        body, name=name, grid=(T // tb,),
        in_specs=[pl.BlockSpec((tb, 2048), lambda i: (i, 0)), rope, rope, cmat, cmat, cmat, cmat, vec, vec],
        out_specs=[blk, blk, pl.BlockSpec((nch, H, CHUNK, CHUNK), lambda i: (i, 0, 0, 0))],
        out_shape=[jax.ShapeDtypeStruct((T, BRANCH_W), F32), jax.ShapeDtypeStruct((T, BRANCH_W), F32),
                   jax.ShapeDtypeStruct((T // CHUNK, H, CHUNK, CHUNK), F32)],
        scratch_shapes=[pltpu.VMEM((H, CHUNK, CHUNK), F32)],
        compiler_params=_params(("arbitrary",)),
    )(proj, cosf, sinf, *consts, gn_g, gn_b)


def ret_bwd(proj, cosf, sinf, consts, gn_g, gn_b, raw, states, dout, *, name):
    T = proj.shape[0]
    tb = min(512, T)
    nch = tb // CHUNK
    nb = T // tb
    H = RET_HEADS

    def body(p_ref, cos_ref, sin_ref, dm_ref, kd_ref, qd_ref, cd_ref, g_ref, b_ref, raw_ref, st_ref, do_ref,
             dp_ref, dg_ref, db_ref, ds_ref):
        @pl.when(pl.program_id(0) == 0)
        def _():
            ds_ref[...] = jnp.zeros_like(ds_ref)
            dg_ref[...] = jnp.zeros_like(dg_ref)
            db_ref[...] = jnp.zeros_like(db_ref)

        for c in reversed(range(nch)):
            r = slice(c * CHUNK, (c + 1) * CHUNK)
            cs, sn = cos_ref[r, :], sin_ref[r, :]
            for h in range(H):
                hc = slice(h * 128, (h + 1) * 128)
                q = p_ref[r, h * 128:(h + 1) * 128]
                k = p_ref[r, 512 + h * 128:512 + (h + 1) * 128]
                v = p_ref[r, 1024 + h * 128:1024 + (h + 1) * 128]
                gt = p_ref[r, 1536 + h * 128:1536 + (h + 1) * 128]
                qr = q * cs + _swap_halves(q) * sn
                kr = (k * cs + _swap_halves(k) * sn) * RET_SCALE
                sprev = st_ref[c, h]
                gv = g_ref[:, hc]
                y, rstd = _group_norm(raw_ref[r, hc])
                d_out = do_ref[r, hc]
                sg = jax.nn.sigmoid(gt)
                d_gate = d_out * (y * gv + b_ref[:, hc]) * (sg * (1.0 + gt * (1.0 - sg)))
                d_aff = d_out * (gt * sg)
                dg_ref[:, hc] += jnp.sum(d_aff * y, axis=0, keepdims=True)
                db_ref[:, hc] += jnp.sum(d_aff, axis=0, keepdims=True)
                dxh = d_aff * gv
                m1 = jnp.mean(dxh, axis=-1, keepdims=True)
                m2 = jnp.mean(dxh * y, axis=-1, keepdims=True)
                d_o = (rstd * (dxh - m1 - y * m2)).astype(BF16)
                qb, kb, vb = qr.astype(BF16), kr.astype(BF16), v.astype(BF16)
                dm, kd, qd = dm_ref[h], kd_ref[h], qd_ref[h]
                p = (_dot(qb, kb, NT) * dm).astype(BF16)
                dp = (_dot(d_o, vb, NT) * dm).astype(BF16)
                dsn = ds_ref[h]
                dsb = dsn.astype(BF16)
                dq_r = _dot(dp, kb, NN) + _dot(d_o, sprev.astype(BF16), NT) * qd
                dk_r = (_dot(dp, qb, TN) + _dot(vb, dsb, NT) * kd) * RET_SCALE
                d_v = _dot(p, d_o, TN) + _dot((kr * kd).astype(BF16), dsb, NN)
                ds_ref[h] = dsn * cd_ref[h] + _dot((qr * qd).astype(BF16), d_o, TN)
                dp_ref[r, h * 128:(h + 1) * 128] = (dq_r * cs - _swap_halves(dq_r) * sn).astype(BF16)
                dp_ref[r, 512 + h * 128:512 + (h + 1) * 128] = (dk_r * cs - _swap_halves(dk_r) * sn).astype(BF16)
                dp_ref[r, 1024 + h * 128:1024 + (h + 1) * 128] = d_v.astype(BF16)
                dp_ref[r, 1536 + h * 128:1536 + (h + 1) * 128] = d_gate.astype(BF16)

    cmat = pl.BlockSpec((H, CHUNK, CHUNK), lambda i: (0, 0, 0))
    vec = pl.BlockSpec((1, BRANCH_W), lambda i: (0, 0))
    rope = pl.BlockSpec((tb, 128), lambda i: (nb - 1 - i, 0))
    blk = pl.BlockSpec((tb, BRANCH_W), lambda i: (nb - 1 - i, 0))
    wide = pl.BlockSpec((tb, 2048), lambda i: (nb - 1 - i, 0))
    return pl.pallas_call(
        body, name=name, grid=(nb,),
        in_specs=[wide, rope, rope, cmat, cmat, cmat, cmat, vec, vec, blk,
                  pl.BlockSpec((nch, H, CHUNK, CHUNK), lambda i: (nb - 1 - i, 0, 0, 0)), blk],
        out_specs=[wide, vec, vec],
        out_shape=[jax.ShapeDtypeStruct((T, 2048), BF16), jax.ShapeDtypeStruct((1, BRANCH_W), F32),
                   jax.ShapeDtypeStruct((1, BRANCH_W), F32)],
        scratch_shapes=[pltpu.VMEM((H, CHUNK, CHUNK), F32)],
        compiler_params=_params(("arbitrary",)),
    )(proj, cosf, sinf, *consts, gn_g, gn_b, raw, states, dout)


def _sb_masks():
    row = lax.broadcasted_iota(jnp.int32, (CHUNK, CHUNK), 0)
    lane = lax.broadcasted_iota(jnp.int32, (CHUNK, CHUNK), 1)
    return row, lane


def _split_dot(v, u):
    hi = v.astype(BF16)
    lo = (v - hi.astype(F32)).astype(BF16)
    return _dot(hi, u, NN) + _dot(lo, u, NN)


def _sb_logits(qh, kb, tri):
    z = _dot(qh, kb, NT)
    l1p = jnp.log(1.0 + jnp.exp(-jnp.abs(z)))
    lsp = jnp.minimum(z, 0.0) - l1p
    lsn = lsp - z
    if tri is not None:
        lsn = jnp.where(tri, lsn, 0.0)
    return lsp, lsn


def sb_fwd(proj, *, name):
    T = proj.shape[0]
    nq = T // CHUNK
    cb = C_SB // 128

    def body(q_ref, k_ref, v_ref, o_ref):
        row, lane = _sb_masks()
        u_gt = (row > lane).astype(BF16)
        tri = lane < row
        hms = (lane < 64, lane >= 64)

        def qblock(i, _):
            rq = pl.ds(pl.multiple_of(i * CHUNK, CHUNK), CHUNK)
            qv = q_ref[rq, :] * SB_SCALE
            acc = jnp.zeros((CHUNK, CHUNK), F32)
            for hm in hms:
                qh = jnp.where(hm, qv, 0.0).astype(BF16)

                def step(j, carry, acc, tri_):
                    rk = pl.ds(pl.multiple_of(j * CHUNK, CHUNK), CHUNK)
                    lsp, lsn = _sb_logits(qh, k_ref[rk, :].astype(BF16), tri_)
                    a = jnp.exp(lsp + _split_dot(lsn, u_gt) + carry)
                    if tri_ is not None:
                        a = jnp.where(tri_, a, 0.0)
                    vh = jnp.where(hm, v_ref[rk, :], 0.0).astype(BF16)
                    return carry + jnp.sum(lsn, axis=1, keepdims=True), acc + _split_dot(a, vh)

                carry, acc = step(i, jnp.zeros((CHUNK, 1), F32), acc, tri)
                carry, acc = lax.fori_loop(0, i, lambda jj, ca: step(i - 1 - jj, ca[0], ca[1], None), (carry, acc))
            o_ref[rq, :] = acc
            return 0

        lax.fori_loop(0, nq, qblock, 0)

    def col(off):
        return pl.BlockSpec((T, 128), lambda hp: (0, off + hp))

    return pl.pallas_call(
        body, name=name, grid=(BRANCH_W // 128,),
        in_specs=[col(cb), col(cb + 4), col(cb + 8)], out_specs=col(0),
        out_shape=jax.ShapeDtypeStruct((T, BRANCH_W), F32),
        compiler_params=_params(("parallel",)),
    )(proj, proj, proj)


def sb_bwd(proj, out, dout, *, name):
    T = proj.shape[0]
    nq = T // CHUNK
    cb = C_SB // 128

    def body(q_ref, k_ref, v_ref, o_ref, do_ref, dq_ref, dk_ref, dv_ref, dka_ref, dva_ref):
        row, lane = _sb_masks()
        u_gt = (row > lane).astype(BF16)
        u_ge = (row >= lane).astype(BF16)
        tri = lane < row
        hms = (lane < 64, lane >= 64)
        dka_ref[...] = jnp.zeros_like(dka_ref)
        dva_ref[...] = jnp.zeros_like(dva_ref)

        def qblock(i, _):
            rq = pl.ds(pl.multiple_of(i * CHUNK, CHUNK), CHUNK)
            qv = q_ref[rq, :] * SB_SCALE
            dov, ov = do_ref[rq, :], o_ref[rq, :]
            dq = jnp.zeros((CHUNK, CHUNK), F32)
            for hm in hms:
                qh = jnp.where(hm, qv, 0.0).astype(BF16)
                doh_f = jnp.where(hm, dov, 0.0)
                doh = doh_f.astype(BF16)
                total = jnp.sum(doh.astype(F32) * ov, axis=1, keepdims=True)

                def step(j, c_l, c_w, dq, tri_):
                    rk = pl.ds(pl.multiple_of(j * CHUNK, CHUNK), CHUNK)
                    kf = k_ref[rk, :]
                    vb = v_ref[rk, :].astype(BF16)
                    lsp, lsn = _sb_logits(qh, kf.astype(BF16), tri_)
                    a = jnp.exp(lsp + _split_dot(lsn, u_gt) + c_l)
                    if tri_ is not None:
                        a = jnp.where(tri_, a, 0.0)
                    w = a * _dot(doh, vb, NT)
                    sfx = _split_dot(w, u_ge) + c_w
                    sp = jnp.exp(lsp)
                    dz = w * (1.0 - sp) - sp * (total - sfx)
                    if tri_ is not None:
                        dz = jnp.where(tri_, dz, 0.0)
                    dzb = dz.astype(BF16)
                    kh = jnp.where(hm, kf, 0.0).astype(BF16)
                    dka_ref[rk, :] += _dot(dzb, qh, TN)
                    dva_ref[rk, :] += _dot(a.astype(BF16), doh, TN)
                    return (c_l + jnp.sum(lsn, axis=1, keepdims=True), c_w + jnp.sum(w, axis=1, keepdims=True),
                            dq + _dot(dzb, kh, NN))

                zero = jnp.zeros((CHUNK, 1), F32)
                c_l, c_w, dq = step(i, zero, zero, dq, tri)
                c_l, c_w, dq = lax.fori_loop(
                    0, i, lambda jj, c: step(i - 1 - jj, c[0], c[1], c[2], None), (c_l, c_w, dq))
            dq_ref[rq, :] = (dq * SB_SCALE).astype(BF16)
            return 0

        lax.fori_loop(0, nq, qblock, 0)
        dk_ref[...] = dka_ref[...].astype(BF16)
        dv_ref[...] = dva_ref[...].astype(BF16)

    def col(off):
        return pl.BlockSpec((T, 128), lambda hp: (0, off + hp))

    o16 = jax.ShapeDtypeStruct((T, BRANCH_W), BF16)
    return pl.pallas_call(
        body, name=name, grid=(BRANCH_W // 128,),
        in_specs=[col(cb), col(cb + 4), col(cb + 8), col(0), col(0)], out_specs=[col(0), col(0), col(0)],
        out_shape=[o16, o16, o16],
        scratch_shapes=[pltpu.VMEM((T, 128), F32), pltpu.VMEM((T, 128), F32)],
        compiler_params=_params(("parallel",)),
    )(proj, proj, proj, out, dout)


_G0 = math.sqrt(2.0 / math.pi)
_G1 = 0.044715


def _gelu(x):
    return 0.5 * x * (1.0 + jnp.tanh(_G0 * (x + _G1 * x * x * x)))


def _gelu_grad(x):
    t = jnp.tanh(_G0 * (x + _G1 * x * x * x))
    return 0.5 * (1.0 + t) + 0.5 * x * (1.0 - t * t) * (_G0 * (1.0 + 3.0 * _G1 * x * x))


def _tril():
    row, lane = _sb_masks()
    return row >= lane


def sgu_fwd(proj, ln_g, ln_b, w, bias, *, name):
    T = proj.shape[0]
    tb = min(512, T)
    G = BRANCH_W // 128

    def body(u_ref, v_ref, g_ref, b_ref, w_ref, bias_ref, o_ref):
        vv = _gelu(v_ref[...])
        xh, _ = _group_norm(vv)
        vn = (xh * g_ref[...] + b_ref[...]).astype(BF16)
        tril = _tril()
        for g in range(G):
            wg = jnp.where(tril, w_ref[g], 0.0).astype(BF16)
            gc = slice(g * 128, (g + 1) * 128)
            for c in range(tb // CHUNK):
                r = slice(c * CHUNK, (c + 1) * CHUNK)
                sv = _dot(wg, vn[r, gc], NN) + bias_ref[g]
                o_ref[r, gc] = _gelu(u_ref[r, gc]) * sv

    cu, cv = C_SGU // BRANCH_W, C_SGU // BRANCH_W + 1
    vec = pl.BlockSpec((1, BRANCH_W), lambda i: (0, 0))
    mat = pl.BlockSpec((G, CHUNK, CHUNK), lambda i: (0, 0, 0))
    return pl.pallas_call(
        body, name=name, grid=(T // tb,),
        in_specs=[pl.BlockSpec((tb, BRANCH_W), lambda i: (i, cu)), pl.BlockSpec((tb, BRANCH_W), lambda i: (i, cv)),
                  vec, vec, mat, mat],
        out_specs=pl.BlockSpec((tb, BRANCH_W), lambda i: (i, 0)),
        out_shape=jax.ShapeDtypeStruct((T, BRANCH_W), F32),
        compiler_params=_params(("parallel",)),
    )(proj, proj, ln_g, ln_b, w, bias)


def sgu_bwd(proj, ln_g, ln_b, w, bias, dout, *, name):
    T = proj.shape[0]
    tb = min(512, T)
    G = BRANCH_W // 128

    def body(u_ref, v_ref, g_ref, b_ref, w_ref, bias_ref, do_ref, dp_ref, dw_ref, dbias_ref, dg_ref, db_ref, dvn_ref):
        @pl.when(pl.program_id(0) == 0)
        def _():
            dw_ref[...] = jnp.zeros_like(dw_ref)
            dbias_ref[...] = jnp.zeros_like(dbias_ref)
            dg_ref[...] = jnp.zeros_like(dg_ref)
            db_ref[...] = jnp.zeros_like(db_ref)

        gv = v_ref[...]
        vv = _gelu(gv)
        xh, rstd = _group_norm(vv)
        vn = (xh * g_ref[...] + b_ref[...]).astype(BF16)
        tril = _tril()
        for g in range(G):
            wg = jnp.where(tril, w_ref[g], 0.0).astype(BF16)
            gc = slice(g * 128, (g + 1) * 128)
            for c in range(tb // CHUNK):
                r = slice(c * CHUNK, (c + 1) * CHUNK)
                vn_c = vn[r, gc]
                sv = _dot(wg, vn_c, NN) + bias_ref[g]
                gu = u_ref[r, gc]
                d_o = do_ref[r, gc]
                dp_ref[r, gc] = (d_o * sv * _gelu_grad(gu)).astype(BF16)
                dsv = d_o * _gelu(gu)
                dsv_b = dsv.astype(BF16)
                dvn_ref[r, gc] = _dot(wg, dsv_b, TN)
                dw_ref[g] += jnp.where(tril, _dot(dsv_b, vn_c, NT), 0.0)
                dbias_ref[g] += jnp.broadcast_to(jnp.sum(dsv, axis=1, keepdims=True), (CHUNK, CHUNK))
        dvn = dvn_ref[...]
        dg_ref[...] += jnp.sum(dvn * xh, axis=0, keepdims=True)
        db_ref[...] += jnp.sum(dvn, axis=0, keepdims=True)
        dxh = dvn * g_ref[...]
        m1 = jnp.mean(dxh, axis=-1, keepdims=True)
        m2 = jnp.mean(dxh * xh, axis=-1, keepdims=True)
        dp_ref[:, BRANCH_W:2 * BRANCH_W] = (rstd * (dxh - m1 - xh * m2) * _gelu_grad(gv)).astype(BF16)

    cu, cv = C_SGU // BRANCH_W, C_SGU // BRANCH_W + 1
    vec = pl.BlockSpec((1, BRANCH_W), lambda i: (0, 0))
    mat = pl.BlockSpec((G, CHUNK, CHUNK), lambda i: (0, 0, 0))
    blk = pl.BlockSpec((tb, BRANCH_W), lambda i: (i, 0))
    msh = jax.ShapeDtypeStruct((G, CHUNK, CHUNK), F32)
    vsh = jax.ShapeDtypeStruct((1, BRANCH_W), F32)
    return pl.pallas_call(
        body, name=name, grid=(T // tb,),
        in_specs=[pl.BlockSpec((tb, BRANCH_W), lambda i: (i, cu)), pl.BlockSpec((tb, BRANCH_W), lambda i: (i, cv)),
                  vec, vec, mat, mat, blk],
        out_specs=[pl.BlockSpec((tb, 2 * BRANCH_W), lambda i: (i, 0)), mat, mat, vec, vec],
        out_shape=[jax.ShapeDtypeStruct((T, 2 * BRANCH_W), BF16), msh, msh, vsh, vsh],
        scratch_shapes=[pltpu.VMEM((tb, BRANCH_W), F32)],
        compiler_params=_params(("arbitrary",)),
    )(proj, proj, ln_g, ln_b, w, bias, dout)


def merge_fwd(a1, a2, a3, p1, p2, p3, proj, *, name):
    T = a1.shape[0]
    tm, tn = min(1024, T), 512
    gb = C_GATE // tn

    def body(a1_ref, a2_ref, a3_ref, p1_ref, p2_ref, p3_ref, g1_ref, g2_ref, g3_ref, m_ref, r1_ref, r2_ref, r3_ref):
        m = None
        for a_ref, p_ref, g_ref, r_ref in ((a1_ref, p1_ref, g1_ref, r1_ref), (a2_ref, p2_ref, g2_ref, r2_ref),
                                           (a3_ref, p3_ref, g3_ref, r3_ref)):
            r = _dot(a_ref[...].astype(BF16), p_ref[...], NN)
            r_ref[...] = r
            t = jax.nn.sigmoid(g_ref[...]) * r
            m = t if m is None else m + t
        m_ref[...] = m

    a_spec = pl.BlockSpec((tm, BRANCH_W), lambda i, j: (i, 0))
    p_spec = pl.BlockSpec((BRANCH_W, tn), lambda i, j: (0, j))
    o_spec = pl.BlockSpec((tm, tn), lambda i, j: (i, j))
    osh = jax.ShapeDtypeStruct((T, D_MODEL), F32)
    gates = [pl.BlockSpec((tm, tn), functools.partial(lambda i, j, o: (i, o + j), o=gb + 2 * n)) for n in range(3)]
    return pl.pallas_call(
        body, name=name, grid=(T // tm, D_MODEL // tn),
        in_specs=[a_spec, a_spec, a_spec, p_spec, p_spec, p_spec, *gates],
        out_specs=[o_spec] * 4, out_shape=[osh] * 4,
        compiler_params=_params(("parallel", "parallel")),
    )(a1, a2, a3, p1, p2, p3, proj, proj, proj)


def merge_bwd(dm, r1, r2, r3, proj, *, name):
    T = dm.shape[0]
    tm, tn = min(512, T), 512
    gb = C_GATE // tn

    def body(dm_ref, r1_ref, r2_ref, r3_ref, g1_ref, g2_ref, g3_ref, dr1_ref, dr2_ref, dr3_ref, dg1_ref, dg2_ref, dg3_ref):
        d = dm_ref[...]
        for r_ref, g_ref, dr_ref, dg_ref in ((r1_ref, g1_ref, dr1_ref, dg1_ref), (r2_ref, g2_ref, dr2_ref, dg2_ref),
                                             (r3_ref, g3_ref, dr3_ref, dg3_ref)):
            s = jax.nn.sigmoid(g_ref[...])
            dr_ref[...] = (d * s).astype(BF16)
            dg_ref[...] = (d * r_ref[...] * (s * (1.0 - s))).astype(BF16)

    o_spec = pl.BlockSpec((tm, tn), lambda i, j: (i, j))
    osh = jax.ShapeDtypeStruct((T, D_MODEL), BF16)
    gates = [pl.BlockSpec((tm, tn), functools.partial(lambda i, j, o: (i, o + j), o=gb + 2 * n)) for n in range(3)]
    return pl.pallas_call(
        body, name=name, grid=(T // tm, D_MODEL // tn),
        in_specs=[o_spec] * 4 + gates, out_specs=[o_spec] * 6, out_shape=[osh] * 6,
        compiler_params=_params(("parallel", "parallel")),
    )(dm, r1, r2, r3, proj, proj, proj)


def _rows_call(fn, ins, out_dtypes, *, name, tr=256):
    first = ins[0][0] if isinstance(ins[0], tuple) else ins[0]
    R, C = first.shape[-2:]
    tr = min(tr, R)
    assert R % tr == 0, (name, R, tr)
    arrs, specs = [], []
    for x in ins:
        if isinstance(x, tuple):
            arrs.append(x[0])
            specs.append(pl.BlockSpec((None, tr, C), functools.partial(lambda i, n: (n, i, 0), n=x[1])))
        else:
            arrs.append(x)
            specs.append(pl.BlockSpec((tr, C), lambda i: (i, 0)))
    ni = len(arrs)

    def body(*refs):
        vals = fn(*[r[...] for r in refs[:ni]])
        for o_ref, v in zip(refs[ni:], vals):
            o_ref[...] = v.astype(o_ref.dtype)

    res = pl.pallas_call(
        body, name=name, grid=(R // tr,), in_specs=specs,
        out_specs=[pl.BlockSpec((tr, C), lambda i: (i, 0)) for _ in out_dtypes],
        out_shape=[jax.ShapeDtypeStruct((R, C), dt) for dt in out_dtypes],
        compiler_params=_params(("parallel",)),
    )(*arrs)
    return res


def _adamw(w, g, m, v):
    m2 = ADAM_B1 * m + (1.0 - ADAM_B1) * g
    v2 = ADAM_B2 * v + (1.0 - ADAM_B2) * (g * g)
    m_hat = m2 / (1.0 - ADAM_B1 ** ADAM_STEP)
    v_hat = v2 / (1.0 - ADAM_B2 ** ADAM_STEP)
    delta = -ADAM_LR * (m_hat / (jnp.sqrt(v_hat) + ADAM_EPS) + ADAM_WD * w)
    return delta, m2, v2


def _place():
    return lax.axis_index("x"), lax.axis_index("y"), lax.axis_index("c")


def _chip_peers(x, y, c):
    return [((1 - x, y, c), 2 * (1 - x) + y), ((x, 1 - y, c), 2 * x + 1 - y), ((1 - x, 1 - y, c), 2 * (1 - x) + 1 - y)]


def _shard_of(ref, axis, k, n):
    start = pl.multiple_of(k * n, 128)
    return ref.at[pl.ds(start, n), :] if axis == 0 else ref.at[:, pl.ds(start, n)]


ANY = pl.BlockSpec(memory_space=pl.ANY)


def gather_weights(shards, axes):
    na = len(shards)
    L = shards[0].shape[0]

    def body(*refs):
        ins, outs = refs[:na], refs[na:na + na * L]
        send_sems, recv_sems, loc_sems = refs[na + na * L:]
        x, y, c = _place()
        k = 2 * x + y
        peers = _chip_peers(x, y, c)
        local, remote = [], []
        for a in range(na):
            n = ins[a].shape[1 + axes[a]]
            for l in range(L):
                src = ins[a].at[l]
                full = outs[a * L + l]
                lc = pltpu.make_async_copy(src, _shard_of(full, axes[a], k, n), loc_sems.at[a * L + l])
                lc.start()
                local.append(lc)
                for r, (peer, _) in enumerate(peers):
                    s = (a * L + l) * 3 + r
                    cp = pltpu.make_async_remote_copy(src, _shard_of(full, axes[a], k, n), send_sems.at[s],
                                                      recv_sems.at[s], device_id=peer, device_id_type=MESH)
                    cp.start()
                    remote.append(cp)
        for a in range(na):
            n = ins[a].shape[1 + axes[a]]
            for l in range(L):
                for r, (peer, kp) in enumerate(peers):
                    s = (a * L + l) * 3 + r
                    pltpu.make_async_remote_copy(ins[a].at[l], _shard_of(outs[a * L + l], axes[a], kp, n), send_sems.at[s],
                                                 recv_sems.at[s], device_id=peer, device_id_type=MESH).wait_recv()
        for cp in remote:
            cp.wait_send()
        for lc in local:
            lc.wait()

    out_shape = []
    for a in range(na):
        _, r, c = shards[a].shape
        shp = (r * N_CHIPS, c) if axes[a] == 0 else (r, c * N_CHIPS)
        out_shape += [jax.ShapeDtypeStruct(shp, BF16)] * L
    n_rem = na * L * 3
    return pl.pallas_call(
        body, name="gather_weights", in_specs=[ANY] * na, out_specs=[ANY] * (na * L), out_shape=out_shape,
        scratch_shapes=[pltpu.SemaphoreType.DMA((n_rem,)), pltpu.SemaphoreType.DMA((n_rem,)),
                        pltpu.SemaphoreType.DMA((na * L,))],
    )(*shards)


def scatter_grads(g16, g32, axes):
    na = len(axes)
    L = len(g16) // na

    def shard_shape(a):
        r, c = g32[a * L].shape
        return (r // N_CHIPS, c) if axes[a] == 0 else (r, c // N_CHIPS)

    def body(*refs):
        b16, b32 = refs[:na * L], refs[na * L:2 * na * L]
        recv, own = refs[2 * na * L:2 * na * L + na], refs[2 * na * L + na:2 * na * L + 2 * na]
        send_sems, recv_sems, loc_sems = refs[2 * na * L + 2 * na:]
        x, y, c = _place()
        k = 2 * x + y
        peers = _chip_peers(x, y, c)
        local, remote = [], []
        for a in range(na):
            n = shard_shape(a)[axes[a]]
            for l in range(L):
                i = a * L + l
                lc = pltpu.make_async_copy(_shard_of(b32[i], axes[a], k, n), own[a].at[l], loc_sems.at[i])
                lc.start()
                local.append(lc)
                for r, (peer, kp) in enumerate(peers):
                    cp = pltpu.make_async_remote_copy(_shard_of(b16[i], axes[a], kp, n), recv[a].at[r, l],
                                                      send_sems.at[i * 3 + r], recv_sems.at[i * 3 + r],
                                                      device_id=peer, device_id_type=MESH)
                    cp.start()
                    remote.append(cp)
        for cp in remote:
            cp.wait_recv()
        for cp in remote:
            cp.wait_send()
        for lc in local:
            lc.wait()

    n_rem = na * L * 3
    out_shape = [jax.ShapeDtypeStruct((3, L) + shard_shape(a), BF16) for a in range(na)]
    out_shape += [jax.ShapeDtypeStruct((L,) + shard_shape(a), F32) for a in range(na)]
    res = pl.pallas_call(
        body, name="scatter_grads", in_specs=[ANY] * (2 * na * L), out_specs=[ANY] * (2 * na), out_shape=out_shape,
        scratch_shapes=[pltpu.SemaphoreType.DMA((n_rem,)), pltpu.SemaphoreType.DMA((n_rem,)),
                        pltpu.SemaphoreType.DMA((na * L,))],
    )(*g16, *g32)
    return res[:na], res[na:]


def swap_with_sibling(parts):
    na = len(parts)

    def body(*refs):
        ins, outs = refs[:na], refs[na:2 * na]
        send_sems, recv_sems = refs[2 * na:]
        x, y, c = _place()
        cps = [pltpu.make_async_remote_copy(ins[a], outs[a], send_sems.at[a], recv_sems.at[a],
                                            device_id=(x, y, 1 - c), device_id_type=MESH) for a in range(na)]
        for cp in cps:
            cp.start()
        for cp in cps:
            cp.wait()

    return pl.pallas_call(
        body, name="swap_with_sibling", in_specs=[ANY] * na, out_specs=[ANY] * na,
        out_shape=[jax.ShapeDtypeStruct(p.shape, p.dtype) for p in parts],
        scratch_shapes=[pltpu.SemaphoreType.DMA((na,)), pltpu.SemaphoreType.DMA((na,))],
    )(*parts)


def allreduce_small(p):
    R = p.shape[0]

    def body(p_ref, o_ref, buf, send_sems, recv_sems):
        x, y, c = _place()
        me = 4 * x + 2 * y + c
        cps = []
        for rel in range(1, 8):
            dx, dy, dc = rel >> 2, (rel >> 1) & 1, rel & 1
            peer = (1 - x if dx else x, 1 - y if dy else y, 1 - c if dc else c)
            cp = pltpu.make_async_remote_copy(p_ref, buf.at[me], send_sems.at[rel - 1], recv_sems.at[rel - 1],
                                              device_id=peer, device_id_type=MESH)
            cp.start()
            cps.append((cp, 4 * peer[0] + 2 * peer[1] + peer[2]))
        buf[me] = p_ref[...]
        for rel, (cp, who) in enumerate(cps):
            pltpu.make_async_remote_copy(p_ref, buf.at[who], send_sems.at[rel], recv_sems.at[rel],
                                         device_id=(x, y, c), device_id_type=MESH).wait_recv()
        acc = buf[0]
        for d in range(1, 8):
            acc = acc + buf[d]
        o_ref[...] = acc
        for cp, _ in cps:
            cp.wait_send()

    return pl.pallas_call(
        body, name="allreduce_small",
        in_specs=[pl.BlockSpec(memory_space=pltpu.VMEM)], out_specs=pl.BlockSpec(memory_space=pltpu.VMEM),
        out_shape=jax.ShapeDtypeStruct((R, 128), F32),
        scratch_shapes=[pltpu.VMEM((8, R, 128), F32), pltpu.SemaphoreType.DMA((7,)), pltpu.SemaphoreType.DMA((7,))],
        compiler_params=pltpu.CompilerParams(vmem_limit_bytes=VMEM_LIMIT),
    )(p)


BIG = ("w_in", "p_ret", "p_sb", "p_sgu", "w_out", "w_up", "w_down")
BIG_AXIS = {"w_in": 1, "p_ret": 1, "p_sb": 1, "p_sgu": 1, "w_out": 0, "w_up": 1, "w_down": 0}
SMALL = ("ret_gn_g", "ret_gn_b", "sgu_ln_g", "sgu_ln_b", "sgu_w", "sgu_b", "ln1_g", "ln1_b", "ln2_g", "ln2_b")


def layer_forward(l, x0, W, sm, rope, rconsts):
    T = x0.shape[0]
    n = f"l{l}_"
    proj = matmul(x0, W["w_in"], mode="nn", tm=1024, tn=640, tk=1024, name=n + "proj")
    retg, raw, states = ret_fwd(proj, *rope, rconsts, sm["ret_gn_g"], sm["ret_gn_b"], name=n + "ret_fwd")
    sb = sb_fwd(proj, name=n + "sb_fwd")
    sg = sgu_fwd(proj, sm["sgu_ln_g"], sm["sgu_ln_b"], sm["sgu_w"], sm["sgu_bias"], name=n + "sgu_fwd")
    merged, r1, r2, r3 = merge_fwd(retg, sb, sg, W["p_ret"], W["p_sb"], W["p_sgu"], proj, name=n + "merge_fwd")
    x1, xh1, rs1 = matmul_ln(merged, W["w_out"], x0, sm["ln1_g"], sm["ln1_b"], tk=1024, name=n + "out_ln1")
    h1 = matmul(x1, W["w_up"], mode="nn", tm=1024, tn=1024, tk=1024, name=n + "up")
    x2, xh2, rs2 = matmul_ln(h1, W["w_down"], x1, sm["ln2_g"], sm["ln2_b"], pro=_relu2, tk=1024, name=n + "down_ln2")
    saved = dict(x0=x0, proj=proj, retg=retg, raw=raw, states=states, sb=sb, sg=sg, merged=merged, r=(r1, r2, r3),
                 x1=x1, xh1=xh1, rs1=rs1, h1=h1, xh2=xh2, rs2=rs2)
    return x2, saved


def layer_backward(l, dx2, s, W, sm, rope, rconsts):
    n = f"l{l}_"
    two = ((F32, None), (BF16, None))
    gw, gs = {}, {}
    du2, gs["ln2_g"], gs["ln2_b"] = ln_bwd(dx2, s["xh2"], s["rs2"], sm["ln2_g"], name=n + "ln2_bwd")
    gw["w_down"] = matmul(s["h1"], du2, mode="tn", tm=1024, tn=1024, tk=512, pro=_relu2, outs=two, name=n + "g_down")
    dh1 = matmul(du2, W["w_down"], mode="nt", tm=1024, tn=1024, tk=1024, outs=((BF16, None),),
                 epi=lambda acc, h: (acc * (2.0 * jnp.maximum(h, 0.0)),), tiles=(s["h1"],), name=n + "d_h1")
    gw["w_up"] = matmul(s["x1"], dh1, mode="tn", tm=1024, tn=1024, tk=512, outs=two, name=n + "g_up")
    dx1 = matmul(dh1, W["w_up"], mode="nt", tm=1024, tn=1024, tk=1024,
                 epi=lambda acc, d: (acc + ALPHA * d,), tiles=(du2,), name=n + "d_x1")
    du1, gs["ln1_g"], gs["ln1_b"] = ln_bwd(dx1, s["xh1"], s["rs1"], sm["ln1_g"], name=n + "ln1_bwd")
    gw["w_out"] = matmul(s["merged"], du1, mode="tn", tm=1024, tn=1024, tk=512, outs=two, name=n + "g_out")
    dmerged = matmul(du1, W["w_out"], mode="nt", tm=1024, tn=1024, tk=1024, name=n + "d_merged")
    dr1, dr2, dr3, dg1, dg2, dg3 = merge_bwd(dmerged, *s["r"], s["proj"], name=n + "merge_bwd")
    d_branch = {}
    for nm, a, dr in (("p_ret", s["retg"], dr1), ("p_sb", s["sb"], dr2), ("p_sgu", s["sg"], dr3)):
        gw[nm] = matmul(a, dr, mode="tn", tm=512, tn=1024, tk=512, outs=two, name=n + "g_" + nm)
        d_branch[nm] = matmul(dr, W[nm], mode="nt", tm=1024, tn=512, tk=1024, name=n + "d_" + nm)
    dret, gs["ret_gn_g"], gs["ret_gn_b"] = ret_bwd(s["proj"], *rope, rconsts, sm["ret_gn_g"], sm["ret_gn_b"], s["raw"],
                                                    s["states"], d_branch["p_ret"], name=n + "ret_bwd")
    dsq, dsk, dsv = sb_bwd(s["proj"], s["sb"], d_branch["p_sb"], name=n + "sb_bwd")
    dsgu, gs["sgu_w"], dbias, gs["sgu_ln_g"], gs["sgu_ln_b"] = sgu_bwd(
        s["proj"], sm["sgu_ln_g"], sm["sgu_ln_b"], sm["sgu_w"], sm["sgu_bias"], d_branch["p_sgu"], name=n + "sgu_bwd")
    gs["sgu_b"] = dbias[:, :, 0]
    dproj = jnp.concatenate([dret, dsq, dsk, dsv, dsgu, dg1, dg2, dg3], axis=1)
    gw["w_in"] = matmul(s["x0"], dproj, mode="tn", tm=1024, tn=1536, tk=512, outs=two, name=n + "g_in")
    dx0 = matmul(dproj, W["w_in"], mode="nt", tm=1024, tn=1024, tk=1536,
                 epi=lambda acc, d: (acc + ALPHA * d,), tiles=(du1,), name=n + "d_x0")
    return dx0, gw, gs


def local_step(x, target, Wfull, small):
    T = x.shape[0]
    rope = _rope_tables(T)
    rconsts = _ret_consts()
    sms = []
    for l in range(DEPTH):
        sm = {k: small[k][l][None, :] for k in SMALL if k not in ("sgu_w", "sgu_b")}
        sm["sgu_w"] = small["sgu_w"][l]
        sm["sgu_bias"] = jnp.broadcast_to(small["sgu_b"][l][:, :, None], (4, CHUNK, CHUNK))
        sms.append(sm)
    h, saved = x, []
    for l in range(DEPTH):
        h, s = layer_forward(l, h, {k: Wfull[k][l] for k in BIG}, sms[l], rope, rconsts)
        saved.append(s)
    dy, sq = loss_head(h, target)
    gw = {k: [None] * DEPTH for k in BIG}
    gs = {k: [None] * DEPTH for k in SMALL}
    for l in reversed(range(DEPTH)):
        dy, gwl, gsl = layer_backward(l, dy, saved[l], {k: Wfull[k][l] for k in BIG}, sms[l], rope, rconsts)
        for k in BIG:
            gw[k][l] = gwl[k]
        for k in SMALL:
            gs[k][l] = gsl[k].reshape(small[k].shape[1:])
    return sq[0, 0], dy, gw, {k: jnp.stack(v) for k, v in gs.items()}


def _flat2(a):
    return a.reshape(-1, a.shape[-1])


def kernel(x, w_in, ret_gn_g, ret_gn_b, sgu_ln_g, sgu_ln_b, sgu_w, sgu_b, p_ret, p_sb, p_sgu, w_out, ln1_g, ln1_b, w_up, w_down, ln2_g, ln2_b, loss_target, m_w_in, m_ret_gn_g, m_ret_gn_b, m_sgu_ln_g, m_sgu_ln_b, m_sgu_w, m_sgu_b, m_p_ret, m_p_sb, m_p_sgu, m_w_out, m_ln1_g, m_ln1_b, m_w_up, m_w_down, m_ln2_g, m_ln2_b, v_w_in, v_ret_gn_g, v_ret_gn_b, v_sgu_ln_g, v_sgu_ln_b, v_sgu_w, v_sgu_b, v_p_ret, v_p_sb, v_p_sgu, v_w_out, v_ln1_g, v_ln1_b, v_w_up, v_w_down, v_ln2_g, v_ln2_b):
    given = dict(locals())
    order = BIG[:1] + SMALL[:6] + BIG[1:5] + SMALL[6:8] + BIG[5:7] + SMALL[8:10]
    L = DEPTH

    shards16 = []
    for k in BIG:
        w = given[k]
        (c16,) = _rows_call(lambda a: (a,), [_flat2(w)], [BF16], name="cast_" + k)
        shards16.append(c16.reshape(w.shape))
    full = gather_weights(shards16, [BIG_AXIS[k] for k in BIG])
    Wfull = {k: [full[a * L + l] for l in range(L)] for a, k in enumerate(BIG)}

    sq, dx, gw, gs = local_step(x[0], loss_target[0], Wfull, {k: given[k] for k in SMALL})
    loss = 0.5 * lax.psum(sq, ("x", "y", "c"))

    axes = [BIG_AXIS[k] for k in BIG]
    g16 = [gw[k][l][1] for k in BIG for l in range(L)]
    g32 = [gw[k][l][0] for k in BIG for l in range(L)]
    recv, own = scatter_grads(g16, g32, axes)
    parts = []
    for a, k in enumerate(BIG):
        o = _flat2(own[a])
        rv = recv[a].reshape(3, *o.shape)
        (part,) = _rows_call(lambda o_, a_, b_, c_: (((o_ + a_.astype(F32)) + b_.astype(F32)) + c_.astype(F32),),
                             [o, (rv, 0), (rv, 1), (rv, 2)], [F32], name="chip_sum_" + k)
        parts.append(part)
    others = swap_with_sibling(parts)
    out = {}
    for a, k in enumerate(BIG):
        shp = given[k].shape
        res = _rows_call(lambda p_, q_, w_, m_, v_: (p_ + q_,) + _adamw(w_, p_ + q_, m_, v_),
                         [parts[a], others[a], _flat2(given[k]), _flat2(given["m_" + k]), _flat2(given["v_" + k])],
                         [F32] * 4, name="adamw_" + k)
        out[k] = [r.reshape(shp) for r in res]

    def pack(d, pre=""):
        return jnp.concatenate([d[pre + k].reshape(-1) for k in SMALL]).reshape(-1, 128)

    g_small = allreduce_small(pack(gs))
    res = _rows_call(lambda g_, w_, m_, v_: (g_,) + _adamw(w_, g_, m_, v_),
                     [g_small, pack(given), pack(given, "m_"), pack(given, "v_")], [F32] * 4, name="adamw_small", tr=8 * 47)
    off = 0
    for k in SMALL:
        sz = given[k].size
        out[k] = [r.reshape(-1)[off:off + sz].reshape(given[k].shape) for r in res]
        off += sz

    grads = [out[k][0] for k in order]
    deltas = [out[k][1] for k in order]
    new_m = [out[k][2] for k in order]
    new_v = [out[k][3] for k in order]
    return (loss, dx[None], *grads, *deltas, *new_m, *new_v)
```

```python
import functools
import math

import jax
import jax.numpy as jnp
from jax import lax
from jax.experimental import pallas as pl
from jax.experimental.pallas import tpu as pltpu

F32 = jnp.float32
BF16 = jnp.bfloat16

D_MODEL = 1024
SEQ = 4096
DEPTH = 2
CHUNK = 128
RET_HEADS = 4
BRANCH_W = 512
N_IN = 7680
D_FF = 4096
LN_EPS = 1e-5
ROPE_BASE = 10000.0
ALPHA = (2 * DEPTH) ** 0.25
RET_SCALE = 128 ** -0.5
SB_SCALE = 64 ** -0.5
C_RET, C_SB, C_SGU, C_GATE = 0, 2048, 3584, 4608

ADAM_LR, ADAM_B1, ADAM_B2, ADAM_EPS, ADAM_WD, ADAM_STEP = 0.001, 0.9, 0.999, 1e-08, 0.01, 10

N_CHIPS = 4
VMEM_LIMIT = 56 * 1024 * 1024
MESH = pl.DeviceIdType.MESH

NN = ((1,), (0,))
NT = ((1,), (1,))
TN = ((0,), (0,))


def _dot(a, b, dims):
    return lax.dot_general(a, b, (dims, ((), ())), preferred_element_type=F32)


def _params(sem):
    return pltpu.CompilerParams(dimension_semantics=sem, vmem_limit_bytes=VMEM_LIMIT)


def _relu2(h):
    r = jnp.maximum(h, 0.0)
    return r * r


def matmul(a, b, *, mode, tm, tn, tk, outs=((F32, None),), pro=None, epi=None, tiles=(), rows=(), name):
    if mode == "nn":
        (M, K), N = a.shape, b.shape[1]
    elif mode == "nt":
        (M, K), N = a.shape, b.shape[0]
    else:
        (K, M), N = a.shape, b.shape[1]
    tm, tn, tk = min(tm, M), min(tn, N), min(tk, K)
    assert M % tm == 0 and N % tn == 0 and K % tk == 0, (name, M, N, K, tm, tn, tk)
    if mode == "nn":
        a_spec = pl.BlockSpec((tm, tk), lambda i, j, k: (i, k))
        b_spec = pl.BlockSpec((tk, tn), lambda i, j, k: (k, j))
        dims = NN
    elif mode == "nt":
        a_spec = pl.BlockSpec((tm, tk), lambda i, j, k: (i, k))
        b_spec = pl.BlockSpec((tn, tk), lambda i, j, k: (j, k))
        dims = NT
    else:
        a_spec = pl.BlockSpec((tk, tm), lambda i, j, k: (k, i))
        b_spec = pl.BlockSpec((tk, tn), lambda i, j, k: (k, j))
        dims = TN
    nk = K // tk
    nt_, nr, no = len(tiles), len(rows), len(outs)

    def body(a_ref, b_ref, *rest):
        tile_refs = rest[:nt_]
        row_refs = rest[nt_:nt_ + nr]
        out_refs = rest[nt_ + nr:nt_ + nr + no]
        av = a_ref[...]
        if pro is not None:
            av = pro(av)
        p = _dot(av.astype(BF16), b_ref[...].astype(BF16), dims)

        def finish(acc):
            vals = (acc,) * no if epi is None else epi(acc, *[r[...] for r in tile_refs], *[r[...] for r in row_refs])
            for o_ref, v in zip(out_refs, vals):
                o_ref[...] = v.astype(o_ref.dtype)

        if nk == 1:
            finish(p)
        else:
            acc_ref = rest[-1]
            k = pl.program_id(2)

            @pl.when(k == 0)
            def _():
                acc_ref[...] = p

            @pl.when(k > 0)
            def _():
                acc_ref[...] += p

            @pl.when(k == nk - 1)
            def _():
                finish(acc_ref[...])

    out_shape, out_specs = [], []
    for dt, width in outs:
        if width is None:
            out_shape.append(jax.ShapeDtypeStruct((M, N), dt))
            out_specs.append(pl.BlockSpec((tm, tn), lambda i, j, k: (i, j)))
        else:
            assert N == tn
            out_shape.append(jax.ShapeDtypeStruct((M, width), dt))
            out_specs.append(pl.BlockSpec((tm, width), lambda i, j, k: (i, 0)))
    in_specs = [a_spec, b_spec]
    in_specs += [pl.BlockSpec((tm, tn), lambda i, j, k: (i, j)) for _ in tiles]
    in_specs += [pl.BlockSpec((1, tn), lambda i, j, k: (0, j)) for _ in rows]
    res = pl.pallas_call(
        body, name=name, grid=(M // tm, N // tn, nk),
        in_specs=in_specs, out_specs=out_specs, out_shape=out_shape,
        scratch_shapes=[pltpu.VMEM((tm, tn), F32)] if nk > 1 else [],
        compiler_params=_params(("parallel", "parallel", "arbitrary")),
    )(a, b, *tiles, *rows)
    return res[0] if no == 1 else res


def _ln_epi(acc, res, g, b):
    u = ALPHA * res + acc
    mu = jnp.mean(u, axis=-1, keepdims=True)
    xc = u - mu
    var = jnp.mean(xc * xc, axis=-1, keepdims=True)
    rstd = lax.rsqrt(var + LN_EPS)
    xhat = xc * rstd
    return xhat * g + b, xhat, jnp.broadcast_to(rstd, (u.shape[0], 128))


def matmul_ln(a, w, res, g, b, *, pro=None, tk, name):
    n = w.shape[1]
    return matmul(a, w, mode="nn", tm=512, tn=n, tk=tk, pro=pro, epi=_ln_epi, tiles=(res,), rows=(g, b),
                  outs=((F32, None), (F32, None), (F32, 128)), name=name)


def ln_bwd(dy, xhat, rstd, g, *, name):
    T, D = dy.shape
    tm = min(512, T)

    def body(dy_ref, xh_ref, rs_ref, g_ref, du_ref, dg_ref, db_ref):
        dyv, xh = dy_ref[...], xh_ref[...]
        r = rs_ref[:, 0:1]
        dxh = dyv * g_ref[...]
        m1 = jnp.mean(dxh, axis=-1, keepdims=True)
        m2 = jnp.mean(dxh * xh, axis=-1, keepdims=True)
        du_ref[...] = r * (dxh - m1 - xh * m2)

        @pl.when(pl.program_id(0) == 0)
        def _():
            dg_ref[...] = jnp.zeros_like(dg_ref)
            db_ref[...] = jnp.zeros_like(db_ref)

        dg_ref[...] += jnp.sum(dyv * xh, axis=0, keepdims=True)
        db_ref[...] += jnp.sum(dyv, axis=0, keepdims=True)

    row = pl.BlockSpec((tm, D), lambda i: (i, 0))
    vec = pl.BlockSpec((1, D), lambda i: (0, 0))
    return pl.pallas_call(
        body, name=name, grid=(T // tm,),
        in_specs=[row, row, pl.BlockSpec((tm, 128), lambda i: (i, 0)), vec],
        out_specs=[row, vec, vec],
        out_shape=[jax.ShapeDtypeStruct((T, D), F32), jax.ShapeDtypeStruct((1, D), F32), jax.ShapeDtypeStruct((1, D), F32)],
        compiler_params=_params(("arbitrary",)),
    )(dy, xhat, rstd, g)


def loss_head(y, target):
    T, D = y.shape
    tm = min(512, T)

    def body(y_ref, t_ref, dy_ref, s_ref):
        e = y_ref[...] - t_ref[...]
        dy_ref[...] = e * (1.0 / D)

        @pl.when(pl.program_id(0) == 0)
        def _():
            s_ref[...] = jnp.zeros_like(s_ref)

        s_ref[...] += jnp.sum(jnp.mean(e * e, axis=-1, keepdims=True))

    row = pl.BlockSpec((tm, D), lambda i: (i, 0))
    return pl.pallas_call(
        body, name="loss_head", grid=(T // tm,),
        in_specs=[row, row], out_specs=[row, pl.BlockSpec((8, 128), lambda i: (0, 0))],
        out_shape=[jax.ShapeDtypeStruct((T, D), F32), jax.ShapeDtypeStruct((8, 128), F32)],
        compiler_params=_params(("arbitrary",)),
    )(y, target)


def _rope_tables(T):
    half = 64
    inv_freq = ROPE_BASE ** (-jnp.arange(half, dtype=F32) / half)
    ang = jnp.arange(T, dtype=jnp.int32).astype(F32)[:, None] * inv_freq[None, :]
    cos, sin = jnp.cos(ang), jnp.sin(ang)
    return jnp.concatenate([cos, cos], axis=1), jnp.concatenate([-sin, sin], axis=1)


def _ret_consts():
    H = RET_HEADS
    log_g = jnp.log(1.0 - 2.0 ** (-5.0 - jnp.arange(H, dtype=F32)))
    idx = jnp.arange(CHUNK, dtype=F32)
    diff = idx[:, None] - idx[None, :]
    dmat = jnp.where(diff[None] >= 0, jnp.exp(log_g[:, None, None] * diff[None]), 0.0)
    kd = jnp.exp(log_g[:, None] * (CHUNK - 1 - idx)[None, :])
    qd = jnp.exp(log_g[:, None] * (idx + 1.0)[None, :])
    cd = jnp.exp(log_g * CHUNK)
    full = (H, CHUNK, CHUNK)
    return (dmat.astype(F32), jnp.broadcast_to(kd[:, :, None], full), jnp.broadcast_to(qd[:, :, None], full),
            jnp.broadcast_to(cd[:, None, None], full))


def _swap_halves(v):
    return pltpu.roll(v, 64, 1)


def _group_norm(o):
    mu = jnp.mean(o, axis=-1, keepdims=True)
    xc = o - mu
    var = jnp.mean(xc * xc, axis=-1, keepdims=True)
    rstd = lax.rsqrt(var + LN_EPS)
    return xc * rstd, rstd


def ret_fwd(proj, cosf, sinf, consts, gn_g, gn_b, *, name):
    T = proj.shape[0]
    tb = min(512, T)
    nch = tb // CHUNK
    H = RET_HEADS

    def body(p_ref, cos_ref, sin_ref, dm_ref, kd_ref, qd_ref, cd_ref, g_ref, b_ref, out_ref, raw_ref, st_ref, s_ref):
        @pl.when(pl.program_id(0) == 0)
        def _():
            s_ref[...] = jnp.zeros_like(s_ref)

        for c in range(nch):
            r = slice(c * CHUNK, (c + 1) * CHUNK)
            cs, sn = cos_ref[r, :], sin_ref[r, :]
            for h in range(H):
                hc = slice(h * 128, (h + 1) * 128)
                q = p_ref[r, h * 128:(h + 1) * 128]
                k = p_ref[r, 512 + h * 128:512 + (h + 1) * 128]
                v = p_ref[r, 1024 + h * 128:1024 + (h + 1) * 128]
                gt = p_ref[r, 1536 + h * 128:1536 + (h + 1) * 128]
                qr = q * cs + _swap_halves(q) * sn
                kr = (k * cs + _swap_halves(k) * sn) * RET_SCALE
                sprev = s_ref[h]
                st_ref[c, h] = sprev
                qb, kb, vb = qr.astype(BF16), kr.astype(BF16), v.astype(BF16)
                s = _dot(qb, kb, NT) * dm_ref[h]
                o = _dot(s.astype(BF16), vb, NN) + _dot((qr * qd_ref[h]).astype(BF16), sprev.astype(BF16), NN)
                s_ref[h] = sprev * cd_ref[h] + _dot((kr * kd_ref[h]).astype(BF16), vb, TN)
                raw_ref[r, hc] = o
                y, _ = _group_norm(o)
                out_ref[r, hc] = (gt * jax.nn.sigmoid(gt)) * (y * g_ref[:, hc] + b_ref[:, hc])

    cmat = pl.BlockSpec((H, CHUNK, CHUNK), lambda i: (0, 0, 0))
    vec = pl.BlockSpec((1, BRANCH_W), lambda i: (0, 0))
    rope = pl.BlockSpec((tb, 128), lambda i: (i, 0))
    blk = pl.BlockSpec((tb, BRANCH_W), lambda i: (i, 0))
    return pl.pallas_call(
        body, name=name, grid=(T // tb,),
        in_specs=[pl.BlockSpec((tb, 2048), lambda i: (i, 0)), rope, rope, cmat, cmat, cmat, cmat, vec, vec],
        out_specs=[blk, blk, pl.BlockSpec((nch, H, CHUNK, CHUNK), lambda i: (i, 0, 0, 0))],
        out_shape=[jax.ShapeDtypeStruct((T, BRANCH_W), F32), jax.ShapeDtypeStruct((T, BRANCH_W), F32),
                   jax.ShapeDtypeStruct((T // CHUNK, H, CHUNK, CHUNK), F32)],
        scratch_shapes=[pltpu.VMEM((H, CHUNK, CHUNK), F32)],
        compiler_params=_params(("arbitrary",)),
    )(proj, cosf, sinf, *consts, gn_g, gn_b)


def ret_bwd(proj, cosf, sinf, consts, gn_g, gn_b, raw, states, dout, *, name):
    T = proj.shape[0]
    tb = min(512, T)
    nch = tb // CHUNK
    nb = T // tb
    H = RET_HEADS

    def body(p_ref, cos_ref, sin_ref, dm_ref, kd_ref, qd_ref, cd_ref, g_ref, b_ref, raw_ref, st_ref, do_ref,
             dp_ref, dg_ref, db_ref, ds_ref):
        @pl.when(pl.program_id(0) == 0)
        def _():
            ds_ref[...] = jnp.zeros_like(ds_ref)
            dg_ref[...] = jnp.zeros_like(dg_ref)
            db_ref[...] = jnp.zeros_like(db_ref)

        for c in reversed(range(nch)):
            r = slice(c * CHUNK, (c + 1) * CHUNK)
            cs, sn = cos_ref[r, :], sin_ref[r, :]
            for h in range(H):
                hc = slice(h * 128, (h + 1) * 128)
                q = p_ref[r, h * 128:(h + 1) * 128]
                k = p_ref[r, 512 + h * 128:512 + (h + 1) * 128]
                v = p_ref[r, 1024 + h * 128:1024 + (h + 1) * 128]
                gt = p_ref[r, 1536 + h * 128:1536 + (h + 1) * 128]
                qr = q * cs + _swap_halves(q) * sn
                kr = (k * cs + _swap_halves(k) * sn) * RET_SCALE
                sprev = st_ref[c, h]
                gv = g_ref[:, hc]
                y, rstd = _group_norm(raw_ref[r, hc])
                d_out = do_ref[r, hc]
                sg = jax.nn.sigmoid(gt)
                d_gate = d_out * (y * gv + b_ref[:, hc]) * (sg * (1.0 + gt * (1.0 - sg)))
                d_aff = d_out * (gt * sg)
                dg_ref[:, hc] += jnp.sum(d_aff * y, axis=0, keepdims=True)
                db_ref[:, hc] += jnp.sum(d_aff, axis=0, keepdims=True)
                dxh = d_aff * gv
                m1 = jnp.mean(dxh, axis=-1, keepdims=True)
                m2 = jnp.mean(dxh * y, axis=-1, keepdims=True)
                d_o = (rstd * (dxh - m1 - y * m2)).astype(BF16)
                qb, kb, vb = qr.astype(BF16), kr.astype(BF16), v.astype(BF16)
                dm, kd, qd = dm_ref[h], kd_ref[h], qd_ref[h]
                p = (_dot(qb, kb, NT) * dm).astype(BF16)
                dp = (_dot(d_o, vb, NT) * dm).astype(BF16)
                dsn = ds_ref[h]
                dsb = dsn.astype(BF16)
                dq_r = _dot(dp, kb, NN) + _dot(d_o, sprev.astype(BF16), NT) * qd
                dk_r = (_dot(dp, qb, TN) + _dot(vb, dsb, NT) * kd) * RET_SCALE
                d_v = _dot(p, d_o, TN) + _dot((kr * kd).astype(BF16), dsb, NN)
                ds_ref[h] = dsn * cd_ref[h] + _dot((qr * qd).astype(BF16), d_o, TN)
                dp_ref[r, h * 128:(h + 1) * 128] = (dq_r * cs - _swap_halves(dq_r) * sn).astype(BF16)
                dp_ref[r, 512 + h * 128:512 + (h + 1) * 128] = (dk_r * cs - _swap_halves(dk_r) * sn).astype(BF16)
                dp_ref[r, 1024 + h * 128:1024 + (h + 1) * 128] = d_v.astype(BF16)
                dp_ref[r, 1536 + h * 128:1536 + (h + 1) * 128] = d_gate.astype(BF16)

    cmat = pl.BlockSpec((H, CHUNK, CHUNK), lambda i: (0, 0, 0))
    vec = pl.BlockSpec((1, BRANCH_W), lambda i: (0, 0))
    rope = pl.BlockSpec((tb, 128), lambda i: (nb - 1 - i, 0))
    blk = pl.BlockSpec((tb, BRANCH_W), lambda i: (nb - 1 - i, 0))
    wide = pl.BlockSpec((tb, 2048), lambda i: (nb - 1 - i, 0))
    return pl.pallas_call(
        body, name=name, grid=(nb,),
        in_specs=[wide, rope, rope, cmat, cmat, cmat, cmat, vec, vec, blk,
                  pl.BlockSpec((nch, H, CHUNK, CHUNK), lambda i: (nb - 1 - i, 0, 0, 0)), blk],
        out_specs=[wide, vec, vec],
        out_shape=[jax.ShapeDtypeStruct((T, 2048), BF16), jax.ShapeDtypeStruct((1, BRANCH_W), F32),
                   jax.ShapeDtypeStruct((1, BRANCH_W), F32)],
        scratch_shapes=[pltpu.VMEM((H, CHUNK, CHUNK), F32)],
        compiler_params=_params(("arbitrary",)),
    )(proj, cosf, sinf, *consts, gn_g, gn_b, raw, states, dout)


def _sb_masks():
    row = lax.broadcasted_iota(jnp.int32, (CHUNK, CHUNK), 0)
    lane = lax.broadcasted_iota(jnp.int32, (CHUNK, CHUNK), 1)
    return row, lane


SB_QT = 512


def _pair(v):
    hi = v.astype(BF16)
    return jnp.concatenate([hi, (v - hi.astype(F32)).astype(BF16)], axis=1)


def _sb_consts():
    r = lax.broadcasted_iota(jnp.int32, (256, 256), 0) & 127
    c = lax.broadcasted_iota(jnp.int32, (256, 256), 1)
    ones = c >= 128
    lane = lax.broadcasted_iota(jnp.int32, (CHUNK, CHUNK), 1)
    return (ones | (r > c)).astype(BF16), (ones | (r >= c)).astype(BF16), (lane < 64, lane >= 64)


def _per_head(x, hms):
    return jnp.concatenate([jnp.where(hm, x, 0.0) for hm in hms], axis=0).astype(BF16)


def _sb_logits(qb, kb2, mask2):
    z = _dot(qb, kb2, NT)
    l1p = jnp.log(1.0 + jnp.exp(-jnp.abs(z)))
    lsp = jnp.minimum(z, 0.0) - l1p
    lsn = lsp - z
    if mask2 is not None:
        lsn = jnp.where(mask2, lsn, 0.0)
    return lsp, lsn


def _sb_tile_mask(qt):
    trow = lax.broadcasted_iota(jnp.int32, (qt, 256), 0)
    tlane = lax.broadcasted_iota(jnp.int32, (qt, 256), 1) & 127
    return lambda m: (tlane + m * CHUNK) < trow


def sb_fwd(proj, *, name):
    T = proj.shape[0]
    qt = min(SB_QT, T)
    nsub = qt // CHUNK
    cb = C_SB // 128

    def body(q_ref, k_ref, v_ref, o_ref):
        u_gt, _, hms = _sb_consts()
        tile_mask = _sb_tile_mask(qt)

        def qtile(i, _):
            rq = pl.ds(pl.multiple_of(i * qt, qt), qt)
            qb = (q_ref[rq, :] * SB_SCALE).astype(BF16)

            def step(j, state, mask2):
                carry, acc = list(state[:2]), state[2]
                rk = pl.ds(pl.multiple_of(j * CHUNK, CHUNK), CHUNK)
                lsp, lsn = _sb_logits(qb, _per_head(k_ref[rk, :], hms), mask2)
                vf = v_ref[rk, :]
                for h in range(2):
                    hc = slice(h * 128, (h + 1) * 128)
                    r = _dot(_pair(lsn[:, hc]), u_gt, NN)
                    a = jnp.exp(lsp[:, hc] + r[:, :128] + carry[h])
                    if mask2 is not None:
                        a = jnp.where(mask2[:, hc], a, 0.0)
                    carry[h] = carry[h] + r[:, 128:]
                    vh = jnp.where(hms[h], vf, 0.0).astype(BF16)
                    acc = acc + _dot(_pair(a), jnp.concatenate([vh, vh], axis=0), NN)
                return carry[0], carry[1], acc

            zero = jnp.zeros((qt, 128), F32)
            state = lax.fori_loop(
                0, nsub, lambda mm, st: step(i * nsub + nsub - 1 - mm, st, tile_mask(nsub - 1 - mm)), (zero, zero, zero))
            state = lax.fori_loop(0, i * nsub, lambda jj, st: step(i * nsub - 1 - jj, st, None), state)
            o_ref[rq, :] = state[2]
            return 0

        lax.fori_loop(0, T // qt, qtile, 0)

    def col(off):
        return pl.BlockSpec((T, 128), lambda hp: (0, off + hp))

    return pl.pallas_call(
        body, name=name, grid=(BRANCH_W // 128,),
        in_specs=[col(cb), col(cb + 4), col(cb + 8)], out_specs=col(0),
        out_shape=jax.ShapeDtypeStruct((T, BRANCH_W), F32),
        compiler_params=_params(("parallel",)),
    )(proj, proj, proj)


def sb_bwd(proj, out, dout, *, name):
    T = proj.shape[0]
    qt = min(SB_QT, T)
    nsub = qt // CHUNK
    cb = C_SB // 128

    def body(q_ref, k_ref, v_ref, o_ref, do_ref, dq_ref, dk_ref, dv_ref, dkt_ref, dvt_ref):
        u_gt, u_ge, hms = _sb_consts()
        tile_mask = _sb_tile_mask(qt)
        tall_lane = lax.broadcasted_iota(jnp.int32, (qt, 128), 1)
        top = lax.broadcasted_iota(jnp.int32, (CHUNK, CHUNK), 0) < 64
        dkt_ref[...] = jnp.zeros_like(dkt_ref)
        dvt_ref[...] = jnp.zeros_like(dvt_ref)

        def qtile(i, _):
            rq = pl.ds(pl.multiple_of(i * qt, qt), qt)
            qs = q_ref[rq, :] * SB_SCALE
            qb, q_t = qs.astype(BF16), qs.T.astype(BF16)
            dov = do_ref[rq, :]
            dob, do_t = dov.astype(BF16), dov.T.astype(BF16)
            prod = dob.astype(F32) * o_ref[rq, :]
            total = [jnp.broadcast_to(jnp.sum(jnp.where(hm, prod, 0.0), axis=1, keepdims=True), (qt, 128))
                     for hm in (tall_lane < 64, tall_lane >= 64)]

            def step(j, state, mask2):
                c_l, c_w, dq = list(state[:2]), list(state[2:4]), state[4]
                rk = pl.ds(pl.multiple_of(j * CHUNK, CHUNK), CHUNK)
                kb2, vb2 = _per_head(k_ref[rk, :], hms), _per_head(v_ref[rk, :], hms)
                lsp, lsn = _sb_logits(qb, kb2, mask2)
                da = _dot(dob, vb2, NT)
                sp = jnp.exp(lsp)
                a_b, dz_b = [], []
                for h in range(2):
                    hc = slice(h * 128, (h + 1) * 128)
                    r = _dot(_pair(lsn[:, hc]), u_gt, NN)
                    a = jnp.exp(lsp[:, hc] + r[:, :128] + c_l[h])
                    if mask2 is not None:
                        a = jnp.where(mask2[:, hc], a, 0.0)
                    c_l[h] = c_l[h] + r[:, 128:]
                    w = a * da[:, hc]
                    r = _dot(_pair(w), u_ge, NN)
                    later_w = r[:, :128] + c_w[h]
                    c_w[h] = c_w[h] + r[:, 128:]
                    dz = w * (1.0 - sp[:, hc]) - sp[:, hc] * (total[h] - later_w)
                    if mask2 is not None:
                        dz = jnp.where(mask2[:, hc], dz, 0.0)
                    a_b.append(a.astype(BF16))
                    dz_b.append(dz.astype(BF16))
                a_b, dz_b = jnp.concatenate(a_b, axis=1), jnp.concatenate(dz_b, axis=1)
                dkt = _dot(q_t, dz_b, NN)
                dvt = _dot(do_t, a_b, NN)
                dkt_ref[j] += jnp.where(top, dkt[:, :128], dkt[:, 128:])
                dvt_ref[j] += jnp.where(top, dvt[:, :128], dvt[:, 128:])
                return c_l[0], c_l[1], c_w[0], c_w[1], dq + _dot(dz_b, kb2, NN)

            zero = jnp.zeros((qt, 128), F32)
            state = lax.fori_loop(
                0, nsub, lambda mm, st: step(i * nsub + nsub - 1 - mm, st, tile_mask(nsub - 1 - mm)), (zero,) * 5)
            state = lax.fori_loop(0, i * nsub, lambda jj, st: step(i * nsub - 1 - jj, st, None), state)
            dq_ref[rq, :] = (state[4] * SB_SCALE).astype(BF16)
            return 0

        lax.fori_loop(0, T // qt, qtile, 0)

        def untranspose(jb, _):
            rk = pl.ds(pl.multiple_of(jb * CHUNK, CHUNK), CHUNK)
            dk_ref[rk, :] = dkt_ref[jb].T.astype(BF16)
            dv_ref[rk, :] = dvt_ref[jb].T.astype(BF16)
            return 0

        lax.fori_loop(0, T // CHUNK, untranspose, 0)

    def col(off):
        return pl.BlockSpec((T, 128), lambda hp: (0, off + hp))

    o16 = jax.ShapeDtypeStruct((T, BRANCH_W), BF16)
    return pl.pallas_call(
        body, name=name, grid=(BRANCH_W // 128,),
        in_specs=[col(cb), col(cb + 4), col(cb + 8), col(0), col(0)], out_specs=[col(0), col(0), col(0)],
        out_shape=[o16, o16, o16],
        scratch_shapes=[pltpu.VMEM((T // CHUNK, CHUNK, CHUNK), F32), pltpu.VMEM((T // CHUNK, CHUNK, CHUNK), F32)],
        compiler_params=_params(("parallel",)),
    )(proj, proj, proj, out, dout)


_G0 = math.sqrt(2.0 / math.pi)
_G1 = 0.044715


def _gelu(x):
    return 0.5 * x * (1.0 + jnp.tanh(_G0 * (x + _G1 * x * x * x)))


def _gelu_grad(x):
    t = jnp.tanh(_G0 * (x + _G1 * x * x * x))
    return 0.5 * (1.0 + t) + 0.5 * x * (1.0 - t * t) * (_G0 * (1.0 + 3.0 * _G1 * x * x))


def _tril():
    row, lane = _sb_masks()
    return row >= lane


def sgu_fwd(proj, ln_g, ln_b, w, bias, *, name):
    T = proj.shape[0]
    tb = min(512, T)
    G = BRANCH_W // 128

    def body(u_ref, v_ref, g_ref, b_ref, w_ref, bias_ref, o_ref):
        vv = _gelu(v_ref[...])
        xh, _ = _group_norm(vv)
        vn = (xh * g_ref[...] + b_ref[...]).astype(BF16)
        tril = _tril()
        for g in range(G):
            wg = jnp.where(tril, w_ref[g], 0.0).astype(BF16)
            gc = slice(g * 128, (g + 1) * 128)
            for c in range(tb // CHUNK):
                r = slice(c * CHUNK, (c + 1) * CHUNK)
                sv = _dot(wg, vn[r, gc], NN) + bias_ref[g]
                o_ref[r, gc] = _gelu(u_ref[r, gc]) * sv

    cu, cv = C_SGU // BRANCH_W, C_SGU // BRANCH_W + 1
    vec = pl.BlockSpec((1, BRANCH_W), lambda i: (0, 0))
    mat = pl.BlockSpec((G, CHUNK, CHUNK), lambda i: (0, 0, 0))
    return pl.pallas_call(
        body, name=name, grid=(T // tb,),
        in_specs=[pl.BlockSpec((tb, BRANCH_W), lambda i: (i, cu)), pl.BlockSpec((tb, BRANCH_W), lambda i: (i, cv)),
                  vec, vec, mat, mat],
        out_specs=pl.BlockSpec((tb, BRANCH_W), lambda i: (i, 0)),
        out_shape=jax.ShapeDtypeStruct((T, BRANCH_W), F32),
        compiler_params=_params(("parallel",)),
    )(proj, proj, ln_g, ln_b, w, bias)


def sgu_bwd(proj, ln_g, ln_b, w, bias, dout, *, name):
    T = proj.shape[0]
    tb = min(512, T)
    G = BRANCH_W // 128

    def body(u_ref, v_ref, g_ref, b_ref, w_ref, bias_ref, do_ref, dp_ref, dw_ref, dbias_ref, dg_ref, db_ref, dvn_ref):
        @pl.when(pl.program_id(0) == 0)
        def _():
            dw_ref[...] = jnp.zeros_like(dw_ref)
            dbias_ref[...] = jnp.zeros_like(dbias_ref)
            dg_ref[...] = jnp.zeros_like(dg_ref)
            db_ref[...] = jnp.zeros_like(db_ref)

        gv = v_ref[...]
        vv = _gelu(gv)
        xh, rstd = _group_norm(vv)
        vn = (xh * g_ref[...] + b_ref[...]).astype(BF16)
        tril = _tril()
        for g in range(G):
            wg = jnp.where(tril, w_ref[g], 0.0).astype(BF16)
            gc = slice(g * 128, (g + 1) * 128)
            for c in range(tb // CHUNK):
                r = slice(c * CHUNK, (c + 1) * CHUNK)
                vn_c = vn[r, gc]
                sv = _dot(wg, vn_c, NN) + bias_ref[g]
                gu = u_ref[r, gc]
                d_o = do_ref[r, gc]
                dp_ref[r, gc] = (d_o * sv * _gelu_grad(gu)).astype(BF16)
                dsv = d_o * _gelu(gu)
                dsv_b = dsv.astype(BF16)
                dvn_ref[r, gc] = _dot(wg, dsv_b, TN)
                dw_ref[g] += jnp.where(tril, _dot(dsv_b, vn_c, NT), 0.0)
                dbias_ref[g] += jnp.broadcast_to(jnp.sum(dsv, axis=1, keepdims=True), (CHUNK, CHUNK))
        dvn = dvn_ref[...]
        dg_ref[...] += jnp.sum(dvn * xh, axis=0, keepdims=True)
        db_ref[...] += jnp.sum(dvn, axis=0, keepdims=True)
        dxh = dvn * g_ref[...]
        m1 = jnp.mean(dxh, axis=-1, keepdims=True)
        m2 = jnp.mean(dxh * xh, axis=-1, keepdims=True)
        dp_ref[:, BRANCH_W:2 * BRANCH_W] = (rstd * (dxh - m1 - xh * m2) * _gelu_grad(gv)).astype(BF16)

    cu, cv = C_SGU // BRANCH_W, C_SGU // BRANCH_W + 1
    vec = pl.BlockSpec((1, BRANCH_W), lambda i: (0, 0))
    mat = pl.BlockSpec((G, CHUNK, CHUNK), lambda i: (0, 0, 0))
    blk = pl.BlockSpec((tb, BRANCH_W), lambda i: (i, 0))
    msh = jax.ShapeDtypeStruct((G, CHUNK, CHUNK), F32)
    vsh = jax.ShapeDtypeStruct((1, BRANCH_W), F32)
    return pl.pallas_call(
        body, name=name, grid=(T // tb,),
        in_specs=[pl.BlockSpec((tb, BRANCH_W), lambda i: (i, cu)), pl.BlockSpec((tb, BRANCH_W), lambda i: (i, cv)),
                  vec, vec, mat, mat, blk],
        out_specs=[pl.BlockSpec((tb, 2 * BRANCH_W), lambda i: (i, 0)), mat, mat, vec, vec],
        out_shape=[jax.ShapeDtypeStruct((T, 2 * BRANCH_W), BF16), msh, msh, vsh, vsh],
        scratch_shapes=[pltpu.VMEM((tb, BRANCH_W), F32)],
        compiler_params=_params(("arbitrary",)),
    )(proj, proj, ln_g, ln_b, w, bias, dout)


def merge_fwd(a1, a2, a3, p1, p2, p3, proj, *, name):
    T = a1.shape[0]
    tm, tn = min(1024, T), 512
    gb = C_GATE // tn

    def body(a1_ref, a2_ref, a3_ref, p1_ref, p2_ref, p3_ref, g1_ref, g2_ref, g3_ref, m_ref, r1_ref, r2_ref, r3_ref):
        m = None
        for a_ref, p_ref, g_ref, r_ref in ((a1_ref, p1_ref, g1_ref, r1_ref), (a2_ref, p2_ref, g2_ref, r2_ref),
                                           (a3_ref, p3_ref, g3_ref, r3_ref)):
            r = _dot(a_ref[...].astype(BF16), p_ref[...], NN)
            r_ref[...] = r
            t = jax.nn.sigmoid(g_ref[...]) * r
            m = t if m is None else m + t
        m_ref[...] = m

    a_spec = pl.BlockSpec((tm, BRANCH_W), lambda i, j: (i, 0))
    p_spec = pl.BlockSpec((BRANCH_W, tn), lambda i, j: (0, j))
    o_spec = pl.BlockSpec((tm, tn), lambda i, j: (i, j))
    osh = jax.ShapeDtypeStruct((T, D_MODEL), F32)
    gates = [pl.BlockSpec((tm, tn), functools.partial(lambda i, j, o: (i, o + j), o=gb + 2 * n)) for n in range(3)]
    return pl.pallas_call(
        body, name=name, grid=(T // tm, D_MODEL // tn),
        in_specs=[a_spec, a_spec, a_spec, p_spec, p_spec, p_spec, *gates],
        out_specs=[o_spec] * 4, out_shape=[osh] * 4,
        compiler_params=_params(("parallel", "parallel")),
    )(a1, a2, a3, p1, p2, p3, proj, proj, proj)


def merge_bwd(dm, r1, r2, r3, proj, *, name):
    T = dm.shape[0]
    tm, tn = min(512, T), 512
    gb = C_GATE // tn

    def body(dm_ref, r1_ref, r2_ref, r3_ref, g1_ref, g2_ref, g3_ref, dr1_ref, dr2_ref, dr3_ref, dg1_ref, dg2_ref, dg3_ref):
        d = dm_ref[...]
        for r_ref, g_ref, dr_ref, dg_ref in ((r1_ref, g1_ref, dr1_ref, dg1_ref), (r2_ref, g2_ref, dr2_ref, dg2_ref),
                                             (r3_ref, g3_ref, dr3_ref, dg3_ref)):
            s = jax.nn.sigmoid(g_ref[...])
            dr_ref[...] = (d * s).astype(BF16)
            dg_ref[...] = (d * r_ref[...] * (s * (1.0 - s))).astype(BF16)

    o_spec = pl.BlockSpec((tm, tn), lambda i, j: (i, j))
    osh = jax.ShapeDtypeStruct((T, D_MODEL), BF16)
    gates = [pl.BlockSpec((tm, tn), functools.partial(lambda i, j, o: (i, o + j), o=gb + 2 * n)) for n in range(3)]
    return pl.pallas_call(
        body, name=name, grid=(T // tm, D_MODEL // tn),
        in_specs=[o_spec] * 4 + gates, out_specs=[o_spec] * 6, out_shape=[osh] * 6,
        compiler_params=_params(("parallel", "parallel")),
    )(dm, r1, r2, r3, proj, proj, proj)


def _rows_call(fn, ins, out_dtypes, *, name, tr=256):
    first = ins[0][0] if isinstance(ins[0], tuple) else ins[0]
    R, C = first.shape[-2:]
    tr = min(tr, R)
    assert R % tr == 0, (name, R, tr)
    arrs, specs = [], []
    for x in ins:
        if isinstance(x, tuple):
            arrs.append(x[0])
            specs.append(pl.BlockSpec((None, tr, C), functools.partial(lambda i, n: (n, i, 0), n=x[1])))
        else:
            arrs.append(x)
            specs.append(pl.BlockSpec((tr, C), lambda i: (i, 0)))
    ni = len(arrs)

    def body(*refs):
        vals = fn(*[r[...] for r in refs[:ni]])
        for o_ref, v in zip(refs[ni:], vals):
            o_ref[...] = v.astype(o_ref.dtype)

    res = pl.pallas_call(
        body, name=name, grid=(R // tr,), in_specs=specs,
        out_specs=[pl.BlockSpec((tr, C), lambda i: (i, 0)) for _ in out_dtypes],
        out_shape=[jax.ShapeDtypeStruct((R, C), dt) for dt in out_dtypes],
        compiler_params=_params(("parallel",)),
    )(*arrs)
    return res


def _adamw(w, g, m, v):
    m2 = ADAM_B1 * m + (1.0 - ADAM_B1) * g
    v2 = ADAM_B2 * v + (1.0 - ADAM_B2) * (g * g)
    m_hat = m2 / (1.0 - ADAM_B1 ** ADAM_STEP)
    v_hat = v2 / (1.0 - ADAM_B2 ** ADAM_STEP)
    delta = -ADAM_LR * (m_hat / (jnp.sqrt(v_hat) + ADAM_EPS) + ADAM_WD * w)
    return delta, m2, v2


def _place():
    return lax.axis_index("x"), lax.axis_index("y"), lax.axis_index("c")


def _chip_peers(x, y, c):
    return [((1 - x, y, c), 2 * (1 - x) + y), ((x, 1 - y, c), 2 * x + 1 - y), ((1 - x, 1 - y, c), 2 * (1 - x) + 1 - y)]


def _shard_of(ref, axis, k, n):
    start = pl.multiple_of(k * n, 128)
    return ref.at[pl.ds(start, n), :] if axis == 0 else ref.at[:, pl.ds(start, n)]


ANY = pl.BlockSpec(memory_space=pl.ANY)


def gather_weights(shards, axes):
    na = len(shards)
    L = shards[0].shape[0]

    def body(*refs):
        ins, outs = refs[:na], refs[na:na + na * L]
        send_sems, recv_sems, loc_sems = refs[na + na * L:]
        x, y, c = _place()
        k = 2 * x + y
        peers = _chip_peers(x, y, c)
        local, remote = [], []
        for a in range(na):
            n = ins[a].shape[1 + axes[a]]
            for l in range(L):
                src = ins[a].at[l]
                full = outs[a * L + l]
                lc = pltpu.make_async_copy(src, _shard_of(full, axes[a], k, n), loc_sems.at[a * L + l])
                lc.start()
                local.append(lc)
                for r, (peer, _) in enumerate(peers):
                    s = (a * L + l) * 3 + r
                    cp = pltpu.make_async_remote_copy(src, _shard_of(full, axes[a], k, n), send_sems.at[s],
                                                      recv_sems.at[s], device_id=peer, device_id_type=MESH)
                    cp.start()
                    remote.append(cp)
        for a in range(na):
            n = ins[a].shape[1 + axes[a]]
            for l in range(L):
                for r, (peer, kp) in enumerate(peers):
                    s = (a * L + l) * 3 + r
                    pltpu.make_async_remote_copy(ins[a].at[l], _shard_of(outs[a * L + l], axes[a], kp, n), send_sems.at[s],
                                                 recv_sems.at[s], device_id=peer, device_id_type=MESH).wait_recv()
        for cp in remote:
            cp.wait_send()
        for lc in local:
            lc.wait()

    out_shape = []
    for a in range(na):
        _, r, c = shards[a].shape
        shp = (r * N_CHIPS, c) if axes[a] == 0 else (r, c * N_CHIPS)
        out_shape += [jax.ShapeDtypeStruct(shp, BF16)] * L
    n_rem = na * L * 3
    return pl.pallas_call(
        body, name="gather_weights", in_specs=[ANY] * na, out_specs=[ANY] * (na * L), out_shape=out_shape,
        scratch_shapes=[pltpu.SemaphoreType.DMA((n_rem,)), pltpu.SemaphoreType.DMA((n_rem,)),
                        pltpu.SemaphoreType.DMA((na * L,))],
    )(*shards)


def scatter_grads(g16, g32, axes):
    na = len(axes)
    L = len(g16) // na

    def shard_shape(a):
        r, c = g32[a * L].shape
        return (r // N_CHIPS, c) if axes[a] == 0 else (r, c // N_CHIPS)

    def body(*refs):
        b16, b32 = refs[:na * L], refs[na * L:2 * na * L]
        recv, own = refs[2 * na * L:2 * na * L + na], refs[2 * na * L + na:2 * na * L + 2 * na]
        send_sems, recv_sems, loc_sems = refs[2 * na * L + 2 * na:]
        x, y, c = _place()
        k = 2 * x + y
        peers = _chip_peers(x, y, c)
        local, remote = [], []
        for a in range(na):
            n = shard_shape(a)[axes[a]]
            for l in range(L):
                i = a * L + l
                lc = pltpu.make_async_copy(_shard_of(b32[i], axes[a], k, n), own[a].at[l], loc_sems.at[i])
                lc.start()
                local.append(lc)
                for r, (peer, kp) in enumerate(peers):
                    cp = pltpu.make_async_remote_copy(_shard_of(b16[i], axes[a], kp, n), recv[a].at[r, l],
                                                      send_sems.at[i * 3 + r], recv_sems.at[i * 3 + r],
                                                      device_id=peer, device_id_type=MESH)
                    cp.start()
                    remote.append(cp)
        for cp in remote:
            cp.wait_recv()
        for cp in remote:
            cp.wait_send()
        for lc in local:
            lc.wait()

    n_rem = na * L * 3
    out_shape = [jax.ShapeDtypeStruct((3, L) + shard_shape(a), BF16) for a in range(na)]
    out_shape += [jax.ShapeDtypeStruct((L,) + shard_shape(a), F32) for a in range(na)]
    res = pl.pallas_call(
        body, name="scatter_grads", in_specs=[ANY] * (2 * na * L), out_specs=[ANY] * (2 * na), out_shape=out_shape,
        scratch_shapes=[pltpu.SemaphoreType.DMA((n_rem,)), pltpu.SemaphoreType.DMA((n_rem,)),
                        pltpu.SemaphoreType.DMA((na * L,))],
    )(*g16, *g32)
    return res[:na], res[na:]


def swap_with_sibling(parts):
    na = len(parts)

    def body(*refs):
        ins, outs = refs[:na], refs[na:2 * na]
        send_sems, recv_sems = refs[2 * na:]
        x, y, c = _place()
        cps = [pltpu.make_async_remote_copy(ins[a], outs[a], send_sems.at[a], recv_sems.at[a],
                                            device_id=(x, y, 1 - c), device_id_type=MESH) for a in range(na)]
        for cp in cps:
            cp.start()
        for cp in cps:
            cp.wait()

    return pl.pallas_call(
        body, name="swap_with_sibling", in_specs=[ANY] * na, out_specs=[ANY] * na,
        out_shape=[jax.ShapeDtypeStruct(p.shape, p.dtype) for p in parts],
        scratch_shapes=[pltpu.SemaphoreType.DMA((na,)), pltpu.SemaphoreType.DMA((na,))],
    )(*parts)


def allreduce_small(p):
    R = p.shape[0]

    def body(p_ref, o_ref, buf, send_sems, recv_sems):
        x, y, c = _place()
        me = 4 * x + 2 * y + c
        cps = []
        for rel in range(1, 8):
            dx, dy, dc = rel >> 2, (rel >> 1) & 1, rel & 1
            peer = (1 - x if dx else x, 1 - y if dy else y, 1 - c if dc else c)
            cp = pltpu.make_async_remote_copy(p_ref, buf.at[me], send_sems.at[rel - 1], recv_sems.at[rel - 1],
                                              device_id=peer, device_id_type=MESH)
            cp.start()
            cps.append((cp, 4 * peer[0] + 2 * peer[1] + peer[2]))
        buf[me] = p_ref[...]
        for rel, (cp, who) in enumerate(cps):
            pltpu.make_async_remote_copy(p_ref, buf.at[who], send_sems.at[rel], recv_sems.at[rel],
                                         device_id=(x, y, c), device_id_type=MESH).wait_recv()
        acc = buf[0]
        for d in range(1, 8):
            acc = acc + buf[d]
        o_ref[...] = acc
        for cp, _ in cps:
            cp.wait_send()

    return pl.pallas_call(
        body, name="allreduce_small",
        in_specs=[pl.BlockSpec(memory_space=pltpu.VMEM)], out_specs=pl.BlockSpec(memory_space=pltpu.VMEM),
        out_shape=jax.ShapeDtypeStruct((R, 128), F32),
        scratch_shapes=[pltpu.VMEM((8, R, 128), F32), pltpu.SemaphoreType.DMA((7,)), pltpu.SemaphoreType.DMA((7,))],
        compiler_params=pltpu.CompilerParams(vmem_limit_bytes=VMEM_LIMIT),
    )(p)


BIG = ("w_in", "p_ret", "p_sb", "p_sgu", "w_out", "w_up", "w_down")
BIG_AXIS = {"w_in": 1, "p_ret": 1, "p_sb": 1, "p_sgu": 1, "w_out": 0, "w_up": 1, "w_down": 0}
SMALL = ("ret_gn_g", "ret_gn_b", "sgu_ln_g", "sgu_ln_b", "sgu_w", "sgu_b", "ln1_g", "ln1_b", "ln2_g", "ln2_b")


def layer_forward(l, x0, W, sm, rope, rconsts):
    T = x0.shape[0]
    n = f"l{l}_"
    proj = matmul(x0, W["w_in"], mode="nn", tm=1024, tn=640, tk=1024, name=n + "proj")
    retg, raw, states = ret_fwd(proj, *rope, rconsts, sm["ret_gn_g"], sm["ret_gn_b"], name=n + "ret_fwd")
    sb = sb_fwd(proj, name=n + "sb_fwd")
    sg = sgu_fwd(proj, sm["sgu_ln_g"], sm["sgu_ln_b"], sm["sgu_w"], sm["sgu_bias"], name=n + "sgu_fwd")
    merged, r1, r2, r3 = merge_fwd(retg, sb, sg, W["p_ret"], W["p_sb"], W["p_sgu"], proj, name=n + "merge_fwd")
    x1, xh1, rs1 = matmul_ln(merged, W["w_out"], x0, sm["ln1_g"], sm["ln1_b"], tk=1024, name=n + "out_ln1")
    h1 = matmul(x1, W["w_up"], mode="nn", tm=1024, tn=1024, tk=1024, name=n + "up")
    x2, xh2, rs2 = matmul_ln(h1, W["w_down"], x1, sm["ln2_g"], sm["ln2_b"], pro=_relu2, tk=1024, name=n + "down_ln2")
    saved = dict(x0=x0, proj=proj, retg=retg, raw=raw, states=states, sb=sb, sg=sg, merged=merged, r=(r1, r2, r3),
                 x1=x1, xh1=xh1, rs1=rs1, h1=h1, xh2=xh2, rs2=rs2)
    return x2, saved


def layer_backward(l, dx2, s, W, sm, rope, rconsts):
    n = f"l{l}_"
    two = ((F32, None), (BF16, None))
    gw, gs = {}, {}
    du2, gs["ln2_g"], gs["ln2_b"] = ln_bwd(dx2, s["xh2"], s["rs2"], sm["ln2_g"], name=n + "ln2_bwd")
    gw["w_down"] = matmul(s["h1"], du2, mode="tn", tm=1024, tn=1024, tk=512, pro=_relu2, outs=two, name=n + "g_down")
    dh1 = matmul(du2, W["w_down"], mode="nt", tm=1024, tn=1024, tk=1024, outs=((BF16, None),),
                 epi=lambda acc, h: (acc * (2.0 * jnp.maximum(h, 0.0)),), tiles=(s["h1"],), name=n + "d_h1")
    gw["w_up"] = matmul(s["x1"], dh1, mode="tn", tm=1024, tn=1024, tk=512, outs=two, name=n + "g_up")
    dx1 = matmul(dh1, W["w_up"], mode="nt", tm=1024, tn=1024, tk=1024,
                 epi=lambda acc, d: (acc + ALPHA * d,), tiles=(du2,), name=n + "d_x1")
    du1, gs["ln1_g"], gs["ln1_b"] = ln_bwd(dx1, s["xh1"], s["rs1"], sm["ln1_g"], name=n + "ln1_bwd")
    gw["w_out"] = matmul(s["merged"], du1, mode="tn", tm=1024, tn=1024, tk=512, outs=two, name=n + "g_out")
    dmerged = matmul(du1, W["w_out"], mode="nt", tm=1024, tn=1024, tk=1024, name=n + "d_merged")
    dr1, dr2, dr3, dg1, dg2, dg3 = merge_bwd(dmerged, *s["r"], s["proj"], name=n + "merge_bwd")
    d_branch = {}
    for nm, a, dr in (("p_ret", s["retg"], dr1), ("p_sb", s["sb"], dr2), ("p_sgu", s["sg"], dr3)):
        gw[nm] = matmul(a, dr, mode="tn", tm=512, tn=1024, tk=512, outs=two, name=n + "g_" + nm)
        d_branch[nm] = matmul(dr, W[nm], mode="nt", tm=1024, tn=512, tk=1024, name=n + "d_" + nm)
    dret, gs["ret_gn_g"], gs["ret_gn_b"] = ret_bwd(s["proj"], *rope, rconsts, sm["ret_gn_g"], sm["ret_gn_b"], s["raw"],
                                                    s["states"], d_branch["p_ret"], name=n + "ret_bwd")
    dsq, dsk, dsv = sb_bwd(s["proj"], s["sb"], d_branch["p_sb"], name=n + "sb_bwd")
    dsgu, gs["sgu_w"], dbias, gs["sgu_ln_g"], gs["sgu_ln_b"] = sgu_bwd(
        s["proj"], sm["sgu_ln_g"], sm["sgu_ln_b"], sm["sgu_w"], sm["sgu_bias"], d_branch["p_sgu"], name=n + "sgu_bwd")
    gs["sgu_b"] = dbias[:, :, 0]
    dproj = jnp.concatenate([dret, dsq, dsk, dsv, dsgu, dg1, dg2, dg3], axis=1)
    gw["w_in"] = matmul(s["x0"], dproj, mode="tn", tm=1024, tn=1536, tk=512, outs=two, name=n + "g_in")
    dx0 = matmul(dproj, W["w_in"], mode="nt", tm=1024, tn=1024, tk=1536,
                 epi=lambda acc, d: (acc + ALPHA * d,), tiles=(du1,), name=n + "d_x0")
    return dx0, gw, gs


def local_step(x, target, Wfull, small):
    T = x.shape[0]
    rope = _rope_tables(T)
    rconsts = _ret_consts()
    sms = []
    for l in range(DEPTH):
        sm = {k: small[k][l][None, :] for k in SMALL if k not in ("sgu_w", "sgu_b")}
        sm["sgu_w"] = small["sgu_w"][l]
        sm["sgu_bias"] = jnp.broadcast_to(small["sgu_b"][l][:, :, None], (4, CHUNK, CHUNK))
        sms.append(sm)
    h, saved = x, []
    for l in range(DEPTH):
        h, s = layer_forward(l, h, {k: Wfull[k][l] for k in BIG}, sms[l], rope, rconsts)
        saved.append(s)
    dy, sq = loss_head(h, target)
    gw = {k: [None] * DEPTH for k in BIG}
    gs = {k: [None] * DEPTH for k in SMALL}
    for l in reversed(range(DEPTH)):
        dy, gwl, gsl = layer_backward(l, dy, saved[l], {k: Wfull[k][l] for k in BIG}, sms[l], rope, rconsts)
        for k in BIG:
            gw[k][l] = gwl[k]
        for k in SMALL:
            gs[k][l] = gsl[k].reshape(small[k].shape[1:])
    return sq[0, 0], dy, gw, {k: jnp.stack(v) for k, v in gs.items()}


def _flat2(a):
    return a.reshape(-1, a.shape[-1])


def kernel(x, w_in, ret_gn_g, ret_gn_b, sgu_ln_g, sgu_ln_b, sgu_w, sgu_b, p_ret, p_sb, p_sgu, w_out, ln1_g, ln1_b, w_up, w_down, ln2_g, ln2_b, loss_target, m_w_in, m_ret_gn_g, m_ret_gn_b, m_sgu_ln_g, m_sgu_ln_b, m_sgu_w, m_sgu_b, m_p_ret, m_p_sb, m_p_sgu, m_w_out, m_ln1_g, m_ln1_b, m_w_up, m_w_down, m_ln2_g, m_ln2_b, v_w_in, v_ret_gn_g, v_ret_gn_b, v_sgu_ln_g, v_sgu_ln_b, v_sgu_w, v_sgu_b, v_p_ret, v_p_sb, v_p_sgu, v_w_out, v_ln1_g, v_ln1_b, v_w_up, v_w_down, v_ln2_g, v_ln2_b):
    given = dict(locals())
    order = BIG[:1] + SMALL[:6] + BIG[1:5] + SMALL[6:8] + BIG[5:7] + SMALL[8:10]
    L = DEPTH

    shards16 = []
    for k in BIG:
        w = given[k]
        (c16,) = _rows_call(lambda a: (a,), [_flat2(w)], [BF16], name="cast_" + k)
        shards16.append(c16.reshape(w.shape))
    full = gather_weights(shards16, [BIG_AXIS[k] for k in BIG])
    Wfull = {k: [full[a * L + l] for l in range(L)] for a, k in enumerate(BIG)}

    sq, dx, gw, gs = local_step(x[0], loss_target[0], Wfull, {k: given[k] for k in SMALL})
    loss = 0.5 * lax.psum(sq, ("x", "y", "c"))

    axes = [BIG_AXIS[k] for k in BIG]
    g16 = [gw[k][l][1] for k in BIG for l in range(L)]
    g32 = [gw[k][l][0] for k in BIG for l in range(L)]
    recv, own = scatter_grads(g16, g32, axes)
    parts = []
    for a, k in enumerate(BIG):
        o = _flat2(own[a])
        rv = recv[a].reshape(3, *o.shape)
        (part,) = _rows_call(lambda o_, a_, b_, c_: (((o_ + a_.astype(F32)) + b_.astype(F32)) + c_.astype(F32),),
                             [o, (rv, 0), (rv, 1), (rv, 2)], [F32], name="chip_sum_" + k)
        parts.append(part)
    others = swap_with_sibling(parts)
    out = {}
    for a, k in enumerate(BIG):
        shp = given[k].shape
        res = _rows_call(lambda p_, q_, w_, m_, v_: (p_ + q_,) + _adamw(w_, p_ + q_, m_, v_),
                         [parts[a], others[a], _flat2(given[k]), _flat2(given["m_" + k]), _flat2(given["v_" + k])],
                         [F32] * 4, name="adamw_" + k)
        out[k] = [r.reshape(shp) for r in res]

    def pack(d, pre=""):
        return jnp.concatenate([d[pre + k].reshape(-1) for k in SMALL]).reshape(-1, 128)

    g_small = allreduce_small(pack(gs))
    res = _rows_call(lambda g_, w_, m_, v_: (g_,) + _adamw(w_, g_, m_, v_),
                     [g_small, pack(given), pack(given, "m_"), pack(given, "v_")], [F32] * 4, name="adamw_small", tr=8 * 47)
    off = 0
    for k in SMALL:
        sz = given[k].size
        out[k] = [r.reshape(-1)[off:off + sz].reshape(given[k].shape) for r in res]
        off += sz

    grads = [out[k][0] for k in order]
    deltas = [out[k][1] for k in order]
    new_m = [out[k][2] for k in order]
    new_v = [out[k][3] for k in order]
    return (loss, dx[None], *grads, *deltas, *new_m, *new_v)
```

```python
import functools
import math

import jax
import jax.numpy as jnp
from jax import lax
from jax.experimental import pallas as pl
from jax.experimental.pallas import tpu as pltpu

F32 = jnp.float32
BF16 = jnp.bfloat16

D_MODEL = 1024
SEQ = 4096
DEPTH = 2
CHUNK = 128
RET_HEADS = 4
BRANCH_W = 512
N_IN = 7680
D_FF = 4096
LN_EPS = 1e-5
ROPE_BASE = 10000.0
ALPHA = (2 * DEPTH) ** 0.25
RET_SCALE = 128 ** -0.5
SB_SCALE = 64 ** -0.5
C_RET, C_SB, C_SGU, C_GATE = 0, 2048, 3584, 4608

ADAM_LR, ADAM_B1, ADAM_B2, ADAM_EPS, ADAM_WD, ADAM_STEP = 0.001, 0.9, 0.999, 1e-08, 0.01, 10

N_CHIPS = 4
VMEM_LIMIT = 56 * 1024 * 1024
MESH = pl.DeviceIdType.MESH

NN = ((1,), (0,))
NT = ((1,), (1,))
TN = ((0,), (0,))


def _dot(a, b, dims):
    return lax.dot_general(a, b, (dims, ((), ())), preferred_element_type=F32)


def _params(sem):
    return pltpu.CompilerParams(dimension_semantics=sem, vmem_limit_bytes=VMEM_LIMIT)


def _relu2(h):
    r = jnp.maximum(h, 0.0)
    return r * r


def matmul(a, b, *, mode, tm, tn, tk, outs=((F32, None),), pro=None, epi=None, tiles=(), rows=(), name):
    if mode == "nn":
        (M, K), N = a.shape, b.shape[1]
    elif mode == "nt":
        (M, K), N = a.shape, b.shape[0]
    else:
        (K, M), N = a.shape, b.shape[1]
    tm, tn, tk = min(tm, M), min(tn, N), min(tk, K)
    assert M % tm == 0 and N % tn == 0 and K % tk == 0, (name, M, N, K, tm, tn, tk)
    if mode == "nn":
        a_spec = pl.BlockSpec((tm, tk), lambda i, j, k: (i, k))
        b_spec = pl.BlockSpec((tk, tn), lambda i, j, k: (k, j))
        dims = NN
    elif mode == "nt":
        a_spec = pl.BlockSpec((tm, tk), lambda i, j, k: (i, k))
        b_spec = pl.BlockSpec((tn, tk), lambda i, j, k: (j, k))
        dims = NT
    else:
        a_spec = pl.BlockSpec((tk, tm), lambda i, j, k: (k, i))
        b_spec = pl.BlockSpec((tk, tn), lambda i, j, k: (k, j))
        dims = TN
    nk = K // tk
    nt_, nr, no = len(tiles), len(rows), len(outs)

    def body(a_ref, b_ref, *rest):
        tile_refs = rest[:nt_]
        row_refs = rest[nt_:nt_ + nr]
        out_refs = rest[nt_ + nr:nt_ + nr + no]
        av = a_ref[...]
        if pro is not None:
            av = pro(av)
        p = _dot(av.astype(BF16), b_ref[...].astype(BF16), dims)

        def finish(acc):
            vals = (acc,) * no if epi is None else epi(acc, *[r[...] for r in tile_refs], *[r[...] for r in row_refs])
            for o_ref, v in zip(out_refs, vals):
                o_ref[...] = v.astype(o_ref.dtype)

        if nk == 1:
            finish(p)
        else:
            acc_ref = rest[-1]
            k = pl.program_id(2)

            @pl.when(k == 0)
            def _():
                acc_ref[...] = p

            @pl.when(k > 0)
            def _():
                acc_ref[...] += p

            @pl.when(k == nk - 1)
            def _():
                finish(acc_ref[...])

    out_shape, out_specs = [], []
    for dt, width in outs:
        if width is None:
            out_shape.append(jax.ShapeDtypeStruct((M, N), dt))
            out_specs.append(pl.BlockSpec((tm, tn), lambda i, j, k: (i, j)))
        else:
            assert N == tn
            out_shape.append(jax.ShapeDtypeStruct((M, width), dt))
            out_specs.append(pl.BlockSpec((tm, width), lambda i, j, k: (i, 0)))
    in_specs = [a_spec, b_spec]
    in_specs += [pl.BlockSpec((tm, tn), lambda i, j, k: (i, j)) for _ in tiles]
    in_specs += [pl.BlockSpec((1, tn), lambda i, j, k: (0, j)) for _ in rows]
    res = pl.pallas_call(
        body, name=name, grid=(M // tm, N // tn, nk),
        in_specs=in_specs, out_specs=out_specs, out_shape=out_shape,
        scratch_shapes=[pltpu.VMEM((tm, tn), F32)] if nk > 1 else [],
        compiler_params=_params(("parallel", "parallel", "arbitrary")),
    )(a, b, *tiles, *rows)
    return res[0] if no == 1 else res


def _ln_epi(acc, res, g, b):
    u = ALPHA * res + acc
    mu = jnp.mean(u, axis=-1, keepdims=True)
    xc = u - mu
    var = jnp.mean(xc * xc, axis=-1, keepdims=True)
    rstd = lax.rsqrt(var + LN_EPS)
    xhat = xc * rstd
    return xhat * g + b, xhat, jnp.broadcast_to(rstd, (u.shape[0], 128))


def matmul_ln(a, w, res, g, b, *, pro=None, tk, name):
    n = w.shape[1]
    return matmul(a, w, mode="nn", tm=512, tn=n, tk=tk, pro=pro, epi=_ln_epi, tiles=(res,), rows=(g, b),
                  outs=((F32, None), (F32, None), (F32, 128)), name=name)


def ln_bwd(dy, xhat, rstd, g, *, name):
    T, D = dy.shape
    tm = min(512, T)

    def body(dy_ref, xh_ref, rs_ref, g_ref, du_ref, dg_ref, db_ref):
        dyv, xh = dy_ref[...], xh_ref[...]
        r = rs_ref[:, 0:1]
        dxh = dyv * g_ref[...]
        m1 = jnp.mean(dxh, axis=-1, keepdims=True)
        m2 = jnp.mean(dxh * xh, axis=-1, keepdims=True)
        du_ref[...] = r * (dxh - m1 - xh * m2)

        @pl.when(pl.program_id(0) == 0)
        def _():
            dg_ref[...] = jnp.zeros_like(dg_ref)
            db_ref[...] = jnp.zeros_like(db_ref)

        dg_ref[...] += jnp.sum(dyv * xh, axis=0, keepdims=True)
        db_ref[...] += jnp.sum(dyv, axis=0, keepdims=True)

    row = pl.BlockSpec((tm, D), lambda i: (i, 0))
    vec = pl.BlockSpec((1, D), lambda i: (0, 0))
    return pl.pallas_call(
        body, name=name, grid=(T // tm,),
        in_specs=[row, row, pl.BlockSpec((tm, 128), lambda i: (i, 0)), vec],
        out_specs=[row, vec, vec],
        out_shape=[jax.ShapeDtypeStruct((T, D), F32), jax.ShapeDtypeStruct((1, D), F32), jax.ShapeDtypeStruct((1, D), F32)],
        compiler_params=_params(("arbitrary",)),
    )(dy, xhat, rstd, g)


def loss_head(y, target):
    T, D = y.shape
    tm = min(512, T)

    def body(y_ref, t_ref, dy_ref, s_ref):
        e = y_ref[...] - t_ref[...]
        dy_ref[...] = e * (1.0 / D)

        @pl.when(pl.program_id(0) == 0)
        def _():
            s_ref[...] = jnp.zeros_like(s_ref)

        s_ref[...] += jnp.sum(jnp.mean(e * e, axis=-1, keepdims=True))

    row = pl.BlockSpec((tm, D), lambda i: (i, 0))
    return pl.pallas_call(
        body, name="loss_head", grid=(T // tm,),
        in_specs=[row, row], out_specs=[row, pl.BlockSpec((8, 128), lambda i: (0, 0))],
        out_shape=[jax.ShapeDtypeStruct((T, D), F32), jax.ShapeDtypeStruct((8, 128), F32)],
        compiler_params=_params(("arbitrary",)),
    )(y, target)


def _rope_tables(T):
    half = 64
    inv_freq = ROPE_BASE ** (-jnp.arange(half, dtype=F32) / half)
    ang = jnp.arange(T, dtype=jnp.int32).astype(F32)[:, None] * inv_freq[None, :]
    cos, sin = jnp.cos(ang), jnp.sin(ang)
    return jnp.concatenate([cos, cos], axis=1), jnp.concatenate([-sin, sin], axis=1)


def _ret_consts():
    H = RET_HEADS
    log_g = jnp.log(1.0 - 2.0 ** (-5.0 - jnp.arange(H, dtype=F32)))
    idx = jnp.arange(CHUNK, dtype=F32)
    diff = idx[:, None] - idx[None, :]
    dmat = jnp.where(diff[None] >= 0, jnp.exp(log_g[:, None, None] * diff[None]), 0.0)
    kd = jnp.exp(log_g[:, None] * (CHUNK - 1 - idx)[None, :])
    qd = jnp.exp(log_g[:, None] * (idx + 1.0)[None, :])
    cd = jnp.exp(log_g * CHUNK)
    full = (H, CHUNK, CHUNK)
    return (dmat.astype(F32), jnp.broadcast_to(kd[:, :, None], full), jnp.broadcast_to(qd[:, :, None], full),
            jnp.broadcast_to(cd[:, None, None], full))


def _swap_halves(v):
    return pltpu.roll(v, 64, 1)


def _group_norm(o):
    mu = jnp.mean(o, axis=-1, keepdims=True)
    xc = o - mu
    var = jnp.mean(xc * xc, axis=-1, keepdims=True)
    rstd = lax.rsqrt(var + LN_EPS)
    return xc * rstd, rstd


def ret_fwd(proj, cosf, sinf, consts, gn_g, gn_b, *, name):
    T = proj.shape[0]
    tb = min(512, T)
    nch = tb // CHUNK
    H = RET_HEADS

    def body(p_ref, cos_ref, sin_ref, dm_ref, kd_ref, qd_ref, cd_ref, g_ref, b_ref, out_ref, raw_ref, st_ref, s_ref):
        @pl.when(pl.program_id(0) == 0)
        def _():
            s_ref[...] = jnp.zeros_like(s_ref)

        for c in range(nch):
            r = slice(c * CHUNK, (c + 1) * CHUNK)
            cs, sn = cos_ref[r, :], sin_ref[r, :]
            for h in range(H):
                hc = slice(h * 128, (h + 1) * 128)
                q = p_ref[r, h * 128:(h + 1) * 128]
                k = p_ref[r, 512 + h * 128:512 + (h + 1) * 128]
                v = p_ref[r, 1024 + h * 128:1024 + (h + 1) * 128]
                gt = p_ref[r, 1536 + h * 128:1536 + (h + 1) * 128]
                qr = q * cs + _swap_halves(q) * sn
                kr = (k * cs + _swap_halves(k) * sn) * RET_SCALE
                sprev = s_ref[h]
                st_ref[c, h] = sprev
                qb, kb, vb = qr.astype(BF16), kr.astype(BF16), v.astype(BF16)
                s = _dot(qb, kb, NT) * dm_ref[h]
                o = _dot(s.astype(BF16), vb, NN) + _dot((qr * qd_ref[h]).astype(BF16), sprev.astype(BF16), NN)
                s_ref[h] = sprev * cd_ref[h] + _dot((kr * kd_ref[h]).astype(BF16), vb, TN)
                raw_ref[r, hc] = o
                y, _ = _group_norm(o)
                out_ref[r, hc] = (gt * jax.nn.sigmoid(gt)) * (y * g_ref[:, hc] + b_ref[:, hc])

    cmat = pl.BlockSpec((H, CHUNK, CHUNK), lambda i: (0, 0, 0))
    vec = pl.BlockSpec((1, BRANCH_W), lambda i: (0, 0))
    rope = pl.BlockSpec((tb, 128), lambda i: (i, 0))
    blk = pl.BlockSpec((tb, BRANCH_W), lambda i: (i, 0))
    return pl.pallas_call(
        body, name=name, grid=(T // tb,),
        in_specs=[pl.BlockSpec((tb, 2048), lambda i: (i, 0)), rope, rope, cmat, cmat, cmat, cmat, vec, vec],
        out_specs=[blk, blk, pl.BlockSpec((nch, H, CHUNK, CHUNK), lambda i: (i, 0, 0, 0))],
        out_shape=[jax.ShapeDtypeStruct((T, BRANCH_W), F32), jax.ShapeDtypeStruct((T, BRANCH_W), F32),
                   jax.ShapeDtypeStruct((T // CHUNK, H, CHUNK, CHUNK), F32)],
        scratch_shapes=[pltpu.VMEM((H, CHUNK, CHUNK), F32)],
        compiler_params=_params(("arbitrary",)),
    )(proj, cosf, sinf, *consts, gn_g, gn_b)


def ret_bwd(proj, cosf, sinf, consts, gn_g, gn_b, raw, states, dout, *, name):
    T = proj.shape[0]
    tb = min(512, T)
    nch = tb // CHUNK
    nb = T // tb
    H = RET_HEADS

    def body(p_ref, cos_ref, sin_ref, dm_ref, kd_ref, qd_ref, cd_ref, g_ref, b_ref, raw_ref, st_ref, do_ref,
             dp_ref, dg_ref, db_ref, ds_ref):
        @pl.when(pl.program_id(0) == 0)
        def _():
            ds_ref[...] = jnp.zeros_like(ds_ref)
            dg_ref[...] = jnp.zeros_like(dg_ref)
            db_ref[...] = jnp.zeros_like(db_ref)

        for c in reversed(range(nch)):
            r = slice(c * CHUNK, (c + 1) * CHUNK)
            cs, sn = cos_ref[r, :], sin_ref[r, :]
            for h in range(H):
                hc = slice(h * 128, (h + 1) * 128)
                q = p_ref[r, h * 128:(h + 1) * 128]
                k = p_ref[r, 512 + h * 128:512 + (h + 1) * 128]
                v = p_ref[r, 1024 + h * 128:1024 + (h + 1) * 128]
                gt = p_ref[r, 1536 + h * 128:1536 + (h + 1) * 128]
                qr = q * cs + _swap_halves(q) * sn
                kr = (k * cs + _swap_halves(k) * sn) * RET_SCALE
                sprev = st_ref[c, h]
                gv = g_ref[:, hc]
                y, rstd = _group_norm(raw_ref[r, hc])
                d_out = do_ref[r, hc]
                sg = jax.nn.sigmoid(gt)
                d_gate = d_out * (y * gv + b_ref[:, hc]) * (sg * (1.0 + gt * (1.0 - sg)))
                d_aff = d_out * (gt * sg)
                dg_ref[:, hc] += jnp.sum(d_aff * y, axis=0, keepdims=True)
                db_ref[:, hc] += jnp.sum(d_aff, axis=0, keepdims=True)
                dxh = d_aff * gv
                m1 = jnp.mean(dxh, axis=-1, keepdims=True)
                m2 = jnp.mean(dxh * y, axis=-1, keepdims=True)
                d_o = (rstd * (dxh - m1 - y * m2)).astype(BF16)
                qb, kb, vb = qr.astype(BF16), kr.astype(BF16), v.astype(BF16)
                dm, kd, qd = dm_ref[h], kd_ref[h], qd_ref[h]
                p = (_dot(qb, kb, NT) * dm).astype(BF16)
                dp = (_dot(d_o, vb, NT) * dm).astype(BF16)
                dsn = ds_ref[h]
                dsb = dsn.astype(BF16)
                dq_r = _dot(dp, kb, NN) + _dot(d_o, sprev.astype(BF16), NT) * qd
                dk_r = (_dot(dp, qb, TN) + _dot(vb, dsb, NT) * kd) * RET_SCALE
                d_v = _dot(p, d_o, TN) + _dot((kr * kd).astype(BF16), dsb, NN)
                ds_ref[h] = dsn * cd_ref[h] + _dot((qr * qd).astype(BF16), d_o, TN)
                dp_ref[r, h * 128:(h + 1) * 128] = (dq_r * cs - _swap_halves(dq_r) * sn).astype(BF16)
                dp_ref[r, 512 + h * 128:512 + (h + 1) * 128] = (dk_r * cs - _swap_halves(dk_r) * sn).astype(BF16)
                dp_ref[r, 1024 + h * 128:1024 + (h + 1) * 128] = d_v.astype(BF16)
                dp_ref[r, 1536 + h * 128:1536 + (h + 1) * 128] = d_gate.astype(BF16)

    cmat = pl.BlockSpec((H, CHUNK, CHUNK), lambda i: (0, 0, 0))
    vec = pl.BlockSpec((1, BRANCH_W), lambda i: (0, 0))
    rope = pl.BlockSpec((tb, 128), lambda i: (nb - 1 - i, 0))
    blk = pl.BlockSpec((tb, BRANCH_W), lambda i: (nb - 1 - i, 0))
    wide = pl.BlockSpec((tb, 2048), lambda i: (nb - 1 - i, 0))
    return pl.pallas_call(
        body, name=name, grid=(nb,),
        in_specs=[wide, rope, rope, cmat, cmat, cmat, cmat, vec, vec, blk,
                  pl.BlockSpec((nch, H, CHUNK, CHUNK), lambda i: (nb - 1 - i, 0, 0, 0)), blk],
        out_specs=[wide, vec, vec],
        out_shape=[jax.ShapeDtypeStruct((T, 2048), BF16), jax.ShapeDtypeStruct((1, BRANCH_W), F32),
                   jax.ShapeDtypeStruct((1, BRANCH_W), F32)],
        scratch_shapes=[pltpu.VMEM((H, CHUNK, CHUNK), F32)],
        compiler_params=_params(("arbitrary",)),
    )(proj, cosf, sinf, *consts, gn_g, gn_b, raw, states, dout)


def _sb_masks():
    row = lax.broadcasted_iota(jnp.int32, (CHUNK, CHUNK), 0)
    lane = lax.broadcasted_iota(jnp.int32, (CHUNK, CHUNK), 1)
    return row, lane


SB_QT = 512


def _pair(v):
    hi = v.astype(BF16)
    return jnp.concatenate([hi, (v - hi.astype(F32)).astype(BF16)], axis=1)


def _sb_consts():
    r = lax.broadcasted_iota(jnp.int32, (256, 256), 0) & 127
    c = lax.broadcasted_iota(jnp.int32, (256, 256), 1)
    ones = c >= 128
    lane = lax.broadcasted_iota(jnp.int32, (CHUNK, CHUNK), 1)
    return (ones | (r > c)).astype(BF16), (ones | (r >= c)).astype(BF16), (lane < 64, lane >= 64)


def _per_head(x, hms):
    return jnp.concatenate([jnp.where(hm, x, 0.0) for hm in hms], axis=0).astype(BF16)


def _sb_logits(qb, kb2, mask2):
    z = _dot(qb, kb2, NT)
    l1p = jnp.log(1.0 + jnp.exp(-jnp.abs(z)))
    lsp = jnp.minimum(z, 0.0) - l1p
    lsn = lsp - z
    if mask2 is not None:
        lsn = jnp.where(mask2, lsn, 0.0)
    return lsp, lsn


def _sb_tile_mask(qt):
    trow = lax.broadcasted_iota(jnp.int32, (qt, 256), 0)
    tlane = lax.broadcasted_iota(jnp.int32, (qt, 256), 1) & 127
    return lambda m: (tlane + m * CHUNK) < trow


def sb_fwd(proj, *, name, job=None):
    T = proj.shape[0]
    qt = min(SB_QT, T)
    nsub = qt // CHUNK
    cb = C_SB // 128

    def body(q_ref, k_ref, v_ref, o_ref):
        u_gt, _, hms = _sb_consts()
        tile_mask = _sb_tile_mask(qt)

        def qtile(i, _):
            rq = pl.ds(pl.multiple_of(i * qt, qt), qt)
            qb = (q_ref[rq, :] * SB_SCALE).astype(BF16)

            def step(j, state, mask2):
                carry, acc = list(state[:2]), state[2]
                rk = pl.ds(pl.multiple_of(j * CHUNK, CHUNK), CHUNK)
                lsp, lsn = _sb_logits(qb, _per_head(k_ref[rk, :], hms), mask2)
                vf = v_ref[rk, :]
                for h in range(2):
                    hc = slice(h * 128, (h + 1) * 128)
                    r = _dot(_pair(lsn[:, hc]), u_gt, NN)
                    a = jnp.exp(lsp[:, hc] + r[:, :128] + carry[h])
                    if mask2 is not None:
                        a = jnp.where(mask2[:, hc], a, 0.0)
                    carry[h] = carry[h] + r[:, 128:]
                    vh = jnp.where(hms[h], vf, 0.0).astype(BF16)
                    acc = acc + _dot(_pair(a), jnp.concatenate([vh, vh], axis=0), NN)
                return carry[0], carry[1], acc

            zero = jnp.zeros((qt, 128), F32)
            state = lax.fori_loop(
                0, nsub, lambda mm, st: step(i * nsub + nsub - 1 - mm, st, tile_mask(nsub - 1 - mm)), (zero, zero, zero))
            state = lax.fori_loop(0, i * nsub, lambda jj, st: step(i * nsub - 1 - jj, st, None), state)
            o_ref[rq, :] = state[2]
            return 0

        lax.fori_loop(0, T // qt, qtile, 0)

    def col(off):
        return pl.BlockSpec((T, 128), lambda hp: (0, off + hp))

    steps = BRANCH_W // 128
    j = _job_args(job, 3, 1)
    res = pl.pallas_call(
        _hosting(body, job, 3, 1, 0, steps), name=name, grid=(steps,),
        in_specs=[col(cb), col(cb + 4), col(cb + 8)] + j["in_specs"], out_specs=[col(0)] + j["out_specs"],
        out_shape=[jax.ShapeDtypeStruct((T, BRANCH_W), F32)] + j["out_shape"],
        scratch_shapes=j["scratch"], input_output_aliases=j["aliases"],
        compiler_params=_params(("parallel",) if job is None else ("arbitrary",)),
    )(proj, proj, proj, *j["ins"])
    return res[0], list(res[1:])


def sb_bwd(proj, out, dout, *, name, job=None):
    T = proj.shape[0]
    qt = min(SB_QT, T)
    nsub = qt // CHUNK
    cb = C_SB // 128

    def body(q_ref, k_ref, v_ref, o_ref, do_ref, dq_ref, dk_ref, dv_ref, dkt_ref, dvt_ref):
        u_gt, u_ge, hms = _sb_consts()
        tile_mask = _sb_tile_mask(qt)
        tall_lane = lax.broadcasted_iota(jnp.int32, (qt, 128), 1)
        top = lax.broadcasted_iota(jnp.int32, (CHUNK, CHUNK), 0) < 64
        dkt_ref[...] = jnp.zeros_like(dkt_ref)
        dvt_ref[...] = jnp.zeros_like(dvt_ref)

        def qtile(i, _):
            rq = pl.ds(pl.multiple_of(i * qt, qt), qt)
            qs = q_ref[rq, :] * SB_SCALE
            qb, q_t = qs.astype(BF16), qs.T.astype(BF16)
            dov = do_ref[rq, :]
            dob, do_t = dov.astype(BF16), dov.T.astype(BF16)
            prod = dob.astype(F32) * o_ref[rq, :]
            total = [jnp.broadcast_to(jnp.sum(jnp.where(hm, prod, 0.0), axis=1, keepdims=True), (qt, 128))
                     for hm in (tall_lane < 64, tall_lane >= 64)]

            def step(j, state, mask2):
                c_l, c_w, dq = list(state[:2]), list(state[2:4]), state[4]
                rk = pl.ds(pl.multiple_of(j * CHUNK, CHUNK), CHUNK)
                kb2, vb2 = _per_head(k_ref[rk, :], hms), _per_head(v_ref[rk, :], hms)
                lsp, lsn = _sb_logits(qb, kb2, mask2)
                da = _dot(dob, vb2, NT)
                sp = jnp.exp(lsp)
                a_b, dz_b = [], []
                for h in range(2):
                    hc = slice(h * 128, (h + 1) * 128)
                    r = _dot(_pair(lsn[:, hc]), u_gt, NN)
                    a = jnp.exp(lsp[:, hc] + r[:, :128] + c_l[h])
                    if mask2 is not None:
                        a = jnp.where(mask2[:, hc], a, 0.0)
                    c_l[h] = c_l[h] + r[:, 128:]
                    w = a * da[:, hc]
                    r = _dot(_pair(w), u_ge, NN)
                    later_w = r[:, :128] + c_w[h]
                    c_w[h] = c_w[h] + r[:, 128:]
                    dz = w * (1.0 - sp[:, hc]) - sp[:, hc] * (total[h] - later_w)
                    if mask2 is not None:
                        dz = jnp.where(mask2[:, hc], dz, 0.0)
                    a_b.append(a.astype(BF16))
                    dz_b.append(dz.astype(BF16))
                a_b, dz_b = jnp.concatenate(a_b, axis=1), jnp.concatenate(dz_b, axis=1)
                dkt = _dot(q_t, dz_b, NN)
                dvt = _dot(do_t, a_b, NN)
                dkt_ref[j] += jnp.where(top, dkt[:, :128], dkt[:, 128:])
                dvt_ref[j] += jnp.where(top, dvt[:, :128], dvt[:, 128:])
                return c_l[0], c_l[1], c_w[0], c_w[1], dq + _dot(dz_b, kb2, NN)

            zero = jnp.zeros((qt, 128), F32)
            state = lax.fori_loop(
                0, nsub, lambda mm, st: step(i * nsub + nsub - 1 - mm, st, tile_mask(nsub - 1 - mm)), (zero,) * 5)
            state = lax.fori_loop(0, i * nsub, lambda jj, st: step(i * nsub - 1 - jj, st, None), state)
            dq_ref[rq, :] = (state[4] * SB_SCALE).astype(BF16)
            return 0

        lax.fori_loop(0, T // qt, qtile, 0)

        def untranspose(jb, _):
            rk = pl.ds(pl.multiple_of(jb * CHUNK, CHUNK), CHUNK)
            dk_ref[rk, :] = dkt_ref[jb].T.astype(BF16)
            dv_ref[rk, :] = dvt_ref[jb].T.astype(BF16)
            return 0

        lax.fori_loop(0, T // CHUNK, untranspose, 0)

    def col(off):
        return pl.BlockSpec((T, 128), lambda hp: (0, off + hp))

    o16 = jax.ShapeDtypeStruct((T, BRANCH_W), BF16)
    steps = BRANCH_W // 128
    j = _job_args(job, 5, 3)
    acc = pltpu.VMEM((T // CHUNK, CHUNK, CHUNK), F32)
    res = pl.pallas_call(
        _hosting(body, job, 5, 3, 2, steps), name=name, grid=(steps,),
        in_specs=[col(cb), col(cb + 4), col(cb + 8), col(0), col(0)] + j["in_specs"],
        out_specs=[col(0), col(0), col(0)] + j["out_specs"], out_shape=[o16, o16, o16] + j["out_shape"],
        scratch_shapes=[acc, acc] + j["scratch"], input_output_aliases=j["aliases"],
        compiler_params=_params(("parallel",) if job is None else ("arbitrary",)),
    )(proj, proj, proj, out, dout, *j["ins"])
    return res[0], res[1], res[2], list(res[3:])


_G0 = math.sqrt(2.0 / math.pi)
_G1 = 0.044715


def _gelu(x):
    return 0.5 * x * (1.0 + jnp.tanh(_G0 * (x + _G1 * x * x * x)))


def _gelu_grad(x):
    t = jnp.tanh(_G0 * (x + _G1 * x * x * x))
    return 0.5 * (1.0 + t) + 0.5 * x * (1.0 - t * t) * (_G0 * (1.0 + 3.0 * _G1 * x * x))


def _tril():
    row, lane = _sb_masks()
    return row >= lane


def sgu_fwd(proj, ln_g, ln_b, w, bias, *, name):
    T = proj.shape[0]
    tb = min(512, T)
    G = BRANCH_W // 128

    def body(u_ref, v_ref, g_ref, b_ref, w_ref, bias_ref, o_ref):
        vv = _gelu(v_ref[...])
        xh, _ = _group_norm(vv)
        vn = (xh * g_ref[...] + b_ref[...]).astype(BF16)
        tril = _tril()
        for g in range(G):
            wg = jnp.where(tril, w_ref[g], 0.0).astype(BF16)
            gc = slice(g * 128, (g + 1) * 128)
            for c in range(tb // CHUNK):
                r = slice(c * CHUNK, (c + 1) * CHUNK)
                sv = _dot(wg, vn[r, gc], NN) + bias_ref[g]
                o_ref[r, gc] = _gelu(u_ref[r, gc]) * sv

    cu, cv = C_SGU // BRANCH_W, C_SGU // BRANCH_W + 1
    vec = pl.BlockSpec((1, BRANCH_W), lambda i: (0, 0))
    mat = pl.BlockSpec((G, CHUNK, CHUNK), lambda i: (0, 0, 0))
    return pl.pallas_call(
        body, name=name, grid=(T // tb,),
        in_specs=[pl.BlockSpec((tb, BRANCH_W), lambda i: (i, cu)), pl.BlockSpec((tb, BRANCH_W), lambda i: (i, cv)),
                  vec, vec, mat, mat],
        out_specs=pl.BlockSpec((tb, BRANCH_W), lambda i: (i, 0)),
        out_shape=jax.ShapeDtypeStruct((T, BRANCH_W), F32),
        compiler_params=_params(("parallel",)),
    )(proj, proj, ln_g, ln_b, w, bias)


def sgu_bwd(proj, ln_g, ln_b, w, bias, dout, *, name):
    T = proj.shape[0]
    tb = min(512, T)
    G = BRANCH_W // 128

    def body(u_ref, v_ref, g_ref, b_ref, w_ref, bias_ref, do_ref, dp_ref, dw_ref, dbias_ref, dg_ref, db_ref, dvn_ref):
        @pl.when(pl.program_id(0) == 0)
        def _():
            dw_ref[...] = jnp.zeros_like(dw_ref)
            dbias_ref[...] = jnp.zeros_like(dbias_ref)
            dg_ref[...] = jnp.zeros_like(dg_ref)
            db_ref[...] = jnp.zeros_like(db_ref)

        gv = v_ref[...]
        vv = _gelu(gv)
        xh, rstd = _group_norm(vv)
        vn = (xh * g_ref[...] + b_ref[...]).astype(BF16)
        tril = _tril()
        for g in range(G):
            wg = jnp.where(tril, w_ref[g], 0.0).astype(BF16)
            gc = slice(g * 128, (g + 1) * 128)
            for c in range(tb // CHUNK):
                r = slice(c * CHUNK, (c + 1) * CHUNK)
                vn_c = vn[r, gc]
                sv = _dot(wg, vn_c, NN) + bias_ref[g]
                gu = u_ref[r, gc]
                d_o = do_ref[r, gc]
                dp_ref[r, gc] = (d_o * sv * _gelu_grad(gu)).astype(BF16)
                dsv = d_o * _gelu(gu)
                dsv_b = dsv.astype(BF16)
                dvn_ref[r, gc] = _dot(wg, dsv_b, TN)
                dw_ref[g] += jnp.where(tril, _dot(dsv_b, vn_c, NT), 0.0)
                dbias_ref[g] += jnp.broadcast_to(jnp.sum(dsv, axis=1, keepdims=True), (CHUNK, CHUNK))
        dvn = dvn_ref[...]
        dg_ref[...] += jnp.sum(dvn * xh, axis=0, keepdims=True)
        db_ref[...] += jnp.sum(dvn, axis=0, keepdims=True)
        dxh = dvn * g_ref[...]
        m1 = jnp.mean(dxh, axis=-1, keepdims=True)
        m2 = jnp.mean(dxh * xh, axis=-1, keepdims=True)
        dp_ref[:, BRANCH_W:2 * BRANCH_W] = (rstd * (dxh - m1 - xh * m2) * _gelu_grad(gv)).astype(BF16)

    cu, cv = C_SGU // BRANCH_W, C_SGU // BRANCH_W + 1
    vec = pl.BlockSpec((1, BRANCH_W), lambda i: (0, 0))
    mat = pl.BlockSpec((G, CHUNK, CHUNK), lambda i: (0, 0, 0))
    blk = pl.BlockSpec((tb, BRANCH_W), lambda i: (i, 0))
    msh = jax.ShapeDtypeStruct((G, CHUNK, CHUNK), F32)
    vsh = jax.ShapeDtypeStruct((1, BRANCH_W), F32)
    return pl.pallas_call(
        body, name=name, grid=(T // tb,),
        in_specs=[pl.BlockSpec((tb, BRANCH_W), lambda i: (i, cu)), pl.BlockSpec((tb, BRANCH_W), lambda i: (i, cv)),
                  vec, vec, mat, mat, blk],
        out_specs=[pl.BlockSpec((tb, 2 * BRANCH_W), lambda i: (i, 0)), mat, mat, vec, vec],
        out_shape=[jax.ShapeDtypeStruct((T, 2 * BRANCH_W), BF16), msh, msh, vsh, vsh],
        scratch_shapes=[pltpu.VMEM((tb, BRANCH_W), F32)],
        compiler_params=_params(("arbitrary",)),
    )(proj, proj, ln_g, ln_b, w, bias, dout)


def merge_fwd(a1, a2, a3, p1, p2, p3, proj, *, name):
    T = a1.shape[0]
    tm, tn = min(1024, T), 512
    gb = C_GATE // tn

    def body(a1_ref, a2_ref, a3_ref, p1_ref, p2_ref, p3_ref, g1_ref, g2_ref, g3_ref, m_ref, r1_ref, r2_ref, r3_ref):
        m = None
        for a_ref, p_ref, g_ref, r_ref in ((a1_ref, p1_ref, g1_ref, r1_ref), (a2_ref, p2_ref, g2_ref, r2_ref),
                                           (a3_ref, p3_ref, g3_ref, r3_ref)):
            r = _dot(a_ref[...].astype(BF16), p_ref[...], NN)
            r_ref[...] = r
            t = jax.nn.sigmoid(g_ref[...]) * r
            m = t if m is None else m + t
        m_ref[...] = m

    a_spec = pl.BlockSpec((tm, BRANCH_W), lambda i, j: (i, 0))
    p_spec = pl.BlockSpec((BRANCH_W, tn), lambda i, j: (0, j))
    o_spec = pl.BlockSpec((tm, tn), lambda i, j: (i, j))
    osh = jax.ShapeDtypeStruct((T, D_MODEL), F32)
    gates = [pl.BlockSpec((tm, tn), functools.partial(lambda i, j, o: (i, o + j), o=gb + 2 * n)) for n in range(3)]
    return pl.pallas_call(
        body, name=name, grid=(T // tm, D_MODEL // tn),
        in_specs=[a_spec, a_spec, a_spec, p_spec, p_spec, p_spec, *gates],
        out_specs=[o_spec] * 4, out_shape=[osh] * 4,
        compiler_params=_params(("parallel", "parallel")),
    )(a1, a2, a3, p1, p2, p3, proj, proj, proj)


def merge_bwd(dm, r1, r2, r3, proj, *, name):
    T = dm.shape[0]
    tm, tn = min(512, T), 512
    gb = C_GATE // tn

    def body(dm_ref, r1_ref, r2_ref, r3_ref, g1_ref, g2_ref, g3_ref, dr1_ref, dr2_ref, dr3_ref, dg1_ref, dg2_ref, dg3_ref):
        d = dm_ref[...]
        for r_ref, g_ref, dr_ref, dg_ref in ((r1_ref, g1_ref, dr1_ref, dg1_ref), (r2_ref, g2_ref, dr2_ref, dg2_ref),
                                             (r3_ref, g3_ref, dr3_ref, dg3_ref)):
            s = jax.nn.sigmoid(g_ref[...])
            dr_ref[...] = (d * s).astype(BF16)
            dg_ref[...] = (d * r_ref[...] * (s * (1.0 - s))).astype(BF16)

    o_spec = pl.BlockSpec((tm, tn), lambda i, j: (i, j))
    osh = jax.ShapeDtypeStruct((T, D_MODEL), BF16)
    gates = [pl.BlockSpec((tm, tn), functools.partial(lambda i, j, o: (i, o + j), o=gb + 2 * n)) for n in range(3)]
    return pl.pallas_call(
        body, name=name, grid=(T // tm, D_MODEL // tn),
        in_specs=[o_spec] * 4 + gates, out_specs=[o_spec] * 6, out_shape=[osh] * 6,
        compiler_params=_params(("parallel", "parallel")),
    )(dm, r1, r2, r3, proj, proj, proj)


def _rows_call(fn, ins, out_dtypes, *, name, tr=256):
    first = ins[0][0] if isinstance(ins[0], tuple) else ins[0]
    R, C = first.shape[-2:]
    tr = min(tr, R)
    assert R % tr == 0, (name, R, tr)
    arrs, specs = [], []
    for x in ins:
        if isinstance(x, tuple):
            arrs.append(x[0])
            specs.append(pl.BlockSpec((None, tr, C), functools.partial(lambda i, n: (n, i, 0), n=x[1])))
        else:
            arrs.append(x)
            specs.append(pl.BlockSpec((tr, C), lambda i: (i, 0)))
    ni = len(arrs)

    def body(*refs):
        vals = fn(*[r[...] for r in refs[:ni]])
        for o_ref, v in zip(refs[ni:], vals):
            o_ref[...] = v.astype(o_ref.dtype)

    res = pl.pallas_call(
        body, name=name, grid=(R // tr,), in_specs=specs,
        out_specs=[pl.BlockSpec((tr, C), lambda i: (i, 0)) for _ in out_dtypes],
        out_shape=[jax.ShapeDtypeStruct((R, C), dt) for dt in out_dtypes],
        compiler_params=_params(("parallel",)),
    )(*arrs)
    return res


def _adamw(w, g, m, v):
    m2 = ADAM_B1 * m + (1.0 - ADAM_B1) * g
    v2 = ADAM_B2 * v + (1.0 - ADAM_B2) * (g * g)
    m_hat = m2 / (1.0 - ADAM_B1 ** ADAM_STEP)
    v_hat = v2 / (1.0 - ADAM_B2 ** ADAM_STEP)
    delta = -ADAM_LR * (m_hat / (jnp.sqrt(v_hat) + ADAM_EPS) + ADAM_WD * w)
    return delta, m2, v2


def _place():
    return lax.axis_index("x"), lax.axis_index("y"), lax.axis_index("c")


def _chip_peers(x, y, c):
    return [((1 - x, y, c), 2 * (1 - x) + y), ((x, 1 - y, c), 2 * x + 1 - y), ((1 - x, 1 - y, c), 2 * (1 - x) + 1 - y)]


def _shard_of(ref, axis, k, n):
    start = pl.multiple_of(k * n, 128)
    return ref.at[pl.ds(start, n), :] if axis == 0 else ref.at[:, pl.ds(start, n)]


ANY = pl.BlockSpec(memory_space=pl.ANY)


class CopyJob:
    def __init__(self, ins, out_shape, scratch, copies, aliases=None):
        self.ins, self.out_shape, self.scratch, self.copies = list(ins), list(out_shape), list(scratch), copies
        self.aliases = dict(aliases or {})

    def start(self, ins, outs, sems):
        local, remote, _ = self.copies(ins, outs, sems)
        for d in local + remote:
            d.start()

    def finish(self, ins, outs, sems):
        local, remote, arrivals = self.copies(ins, outs, sems)
        for d in arrivals:
            d.wait_recv()
        for d in remote:
            d.wait_send()
        for d in local:
            d.wait()


def run_job(job, *, name):
    ni, no = len(job.ins), len(job.out_shape)

    def body(*refs):
        parts = refs[:ni], refs[ni:ni + no], refs[ni + no:]
        job.start(*parts)
        job.finish(*parts)

    return pl.pallas_call(
        body, name=name, in_specs=[ANY] * ni, out_specs=[ANY] * no, out_shape=job.out_shape,
        scratch_shapes=job.scratch, input_output_aliases=job.aliases,
    )(*job.ins)


def _job_args(job, n_in, n_out):
    if job is None:
        return dict(ins=[], in_specs=[], out_specs=[], out_shape=[], scratch=[], aliases={})
    return dict(ins=job.ins, in_specs=[ANY] * len(job.ins), out_specs=[ANY] * len(job.out_shape),
                out_shape=job.out_shape, scratch=job.scratch,
                aliases={n_in + i: n_out + o for i, o in job.aliases.items()})


def _hosting(body, job, n_in, n_out, n_scratch, steps):
    if job is None:
        return body
    ji, jo = len(job.ins), len(job.out_shape)

    def hosted(*refs):
        o = n_in + ji
        s = o + n_out + jo
        parts = refs[n_in:o], refs[o + n_out:s], refs[s + n_scratch:]

        @pl.when(pl.program_id(0) == 0)
        def _():
            job.start(*parts)

        body(*refs[:n_in], *refs[o:o + n_out], *refs[s:s + n_scratch])

        @pl.when(pl.program_id(0) == steps - 1)
        def _():
            job.finish(*parts)

    return hosted


def _job_sems(n_remote, n_local):
    return [pltpu.SemaphoreType.DMA((n_remote,)), pltpu.SemaphoreType.DMA((n_remote,)), pltpu.SemaphoreType.DMA((n_local,))]


def gather_job(shards, axes):
    na = len(shards)

    def copies(ins, outs, sems):
        send, recv, loc = sems
        x, y, c = _place()
        k = 2 * x + y
        local, remote, arrivals = [], [], []
        for a in range(na):
            n = ins[a].shape[axes[a]]
            mine = _shard_of(outs[a], axes[a], k, n)
            local.append(pltpu.make_async_copy(ins[a], mine, loc.at[a]))
            for r, (peer, kp) in enumerate(_chip_peers(x, y, c)):
                s = 3 * a + r
                remote.append(pltpu.make_async_remote_copy(ins[a], mine, send.at[s], recv.at[s],
                                                           device_id=peer, device_id_type=MESH))
                arrivals.append(pltpu.make_async_remote_copy(ins[a], _shard_of(outs[a], axes[a], kp, n), send.at[s],
                                                             recv.at[s], device_id=peer, device_id_type=MESH))
        return local, remote, arrivals

    out_shape = []
    for a in range(na):
        r, c = shards[a].shape
        out_shape.append(jax.ShapeDtypeStruct((r * N_CHIPS, c) if axes[a] == 0 else (r, c * N_CHIPS), BF16))
    return CopyJob(shards, out_shape, _job_sems(3 * na, na), copies)


def scatter_job(l, g16, g32, axes, filled=None):
    na = len(axes)

    def shard_shape(a):
        r, c = g32[a].shape
        return (r // N_CHIPS, c) if axes[a] == 0 else (r, c // N_CHIPS)

    def copies(ins, outs, sems):
        send, recv_sems, loc = sems
        b16, b32 = ins[:na], ins[na:2 * na]
        recv, own = outs[:na], outs[na:]
        x, y, c = _place()
        k = 2 * x + y
        local, remote = [], []
        for a in range(na):
            n = shard_shape(a)[axes[a]]
            local.append(pltpu.make_async_copy(_shard_of(b32[a], axes[a], k, n), own[a].at[l], loc.at[a]))
            for r, (peer, kp) in enumerate(_chip_peers(x, y, c)):
                remote.append(pltpu.make_async_remote_copy(_shard_of(b16[a], axes[a], kp, n), recv[a].at[r, l],
                                                           send.at[3 * a + r], recv_sems.at[3 * a + r],
                                                           device_id=peer, device_id_type=MESH))
        return local, remote, remote

    out_shape = [jax.ShapeDtypeStruct((3, DEPTH) + shard_shape(a), BF16) for a in range(na)]
    out_shape += [jax.ShapeDtypeStruct((DEPTH,) + shard_shape(a), F32) for a in range(na)]
    ins = list(g16) + list(g32)
    aliases = {}
    if filled is not None:
        ins += list(filled[0]) + list(filled[1])
        aliases = {2 * na + i: i for i in range(2 * na)}
    return CopyJob(ins, out_shape, _job_sems(3 * na, na), copies, aliases)


def swap_with_sibling(parts):
    na = len(parts)

    def body(*refs):
        ins, outs = refs[:na], refs[na:2 * na]
        send_sems, recv_sems = refs[2 * na:]
        x, y, c = _place()
        cps = [pltpu.make_async_remote_copy(ins[a], outs[a], send_sems.at[a], recv_sems.at[a],
                                            device_id=(x, y, 1 - c), device_id_type=MESH) for a in range(na)]
        for cp in cps:
            cp.start()
        for cp in cps:
            cp.wait()

    return pl.pallas_call(
        body, name="swap_with_sibling", in_specs=[ANY] * na, out_specs=[ANY] * na,
        out_shape=[jax.ShapeDtypeStruct(p.shape, p.dtype) for p in parts],
        scratch_shapes=[pltpu.SemaphoreType.DMA((na,)), pltpu.SemaphoreType.DMA((na,))],
    )(*parts)


def allreduce_small(p):
    R = p.shape[0]

    def body(p_ref, o_ref, buf, send_sems, recv_sems):
        x, y, c = _place()
        me = 4 * x + 2 * y + c
        cps = []
        for rel in range(1, 8):
            dx, dy, dc = rel >> 2, (rel >> 1) & 1, rel & 1
            peer = (1 - x if dx else x, 1 - y if dy else y, 1 - c if dc else c)
            cp = pltpu.make_async_remote_copy(p_ref, buf.at[me], send_sems.at[rel - 1], recv_sems.at[rel - 1],
                                              device_id=peer, device_id_type=MESH)
            cp.start()
            cps.append((cp, 4 * peer[0] + 2 * peer[1] + peer[2]))
        buf[me] = p_ref[...]
        for rel, (cp, who) in enumerate(cps):
            pltpu.make_async_remote_copy(p_ref, buf.at[who], send_sems.at[rel], recv_sems.at[rel],
                                         device_id=(x, y, c), device_id_type=MESH).wait_recv()
        acc = buf[0]
        for d in range(1, 8):
            acc = acc + buf[d]
        o_ref[...] = acc
        for cp, _ in cps:
            cp.wait_send()

    return pl.pallas_call(
        body, name="allreduce_small",
        in_specs=[pl.BlockSpec(memory_space=pltpu.VMEM)], out_specs=pl.BlockSpec(memory_space=pltpu.VMEM),
        out_shape=jax.ShapeDtypeStruct((R, 128), F32),
        scratch_shapes=[pltpu.VMEM((8, R, 128), F32), pltpu.SemaphoreType.DMA((7,)), pltpu.SemaphoreType.DMA((7,))],
        compiler_params=pltpu.CompilerParams(vmem_limit_bytes=VMEM_LIMIT),
    )(p)


BIG = ("w_in", "p_ret", "p_sb", "p_sgu", "w_out", "w_up", "w_down")
BIG_AXIS = {"w_in": 1, "p_ret": 1, "p_sb": 1, "p_sgu": 1, "w_out": 0, "w_up": 1, "w_down": 0}
SMALL = ("ret_gn_g", "ret_gn_b", "sgu_ln_g", "sgu_ln_b", "sgu_w", "sgu_b", "ln1_g", "ln1_b", "ln2_g", "ln2_b")


def layer_forward(l, x0, W, sm, rope, rconsts, job=None):
    T = x0.shape[0]
    n = f"l{l}_"
    proj = matmul(x0, W["w_in"], mode="nn", tm=1024, tn=640, tk=1024, name=n + "proj")
    retg, raw, states = ret_fwd(proj, *rope, rconsts, sm["ret_gn_g"], sm["ret_gn_b"], name=n + "ret_fwd")
    sb, job_out = sb_fwd(proj, name=n + "sb_fwd", job=job)
    sg = sgu_fwd(proj, sm["sgu_ln_g"], sm["sgu_ln_b"], sm["sgu_w"], sm["sgu_bias"], name=n + "sgu_fwd")
    merged, r1, r2, r3 = merge_fwd(retg, sb, sg, W["p_ret"], W["p_sb"], W["p_sgu"], proj, name=n + "merge_fwd")
    x1, xh1, rs1 = matmul_ln(merged, W["w_out"], x0, sm["ln1_g"], sm["ln1_b"], tk=1024, name=n + "out_ln1")
    h1 = matmul(x1, W["w_up"], mode="nn", tm=1024, tn=1024, tk=1024, name=n + "up")
    x2, xh2, rs2 = matmul_ln(h1, W["w_down"], x1, sm["ln2_g"], sm["ln2_b"], pro=_relu2, tk=1024, name=n + "down_ln2")
    saved = dict(x0=x0, proj=proj, retg=retg, raw=raw, states=states, sb=sb, sg=sg, merged=merged, r=(r1, r2, r3),
                 x1=x1, xh1=xh1, rs1=rs1, h1=h1, xh2=xh2, rs2=rs2)
    return x2, saved, job_out


def layer_backward(l, dx2, s, W, sm, rope, rconsts, job=None):
    n = f"l{l}_"
    two = ((F32, None), (BF16, None))
    gw, gs = {}, {}
    du2, gs["ln2_g"], gs["ln2_b"] = ln_bwd(dx2, s["xh2"], s["rs2"], sm["ln2_g"], name=n + "ln2_bwd")
    gw["w_down"] = matmul(s["h1"], du2, mode="tn", tm=1024, tn=1024, tk=512, pro=_relu2, outs=two, name=n + "g_down")
    dh1 = matmul(du2, W["w_down"], mode="nt", tm=1024, tn=1024, tk=1024, outs=((BF16, None),),
                 epi=lambda acc, h: (acc * (2.0 * jnp.maximum(h, 0.0)),), tiles=(s["h1"],), name=n + "d_h1")
    gw["w_up"] = matmul(s["x1"], dh1, mode="tn", tm=1024, tn=1024, tk=512, outs=two, name=n + "g_up")
    dx1 = matmul(dh1, W["w_up"], mode="nt", tm=1024, tn=1024, tk=1024,
                 epi=lambda acc, d: (acc + ALPHA * d,), tiles=(du2,), name=n + "d_x1")
    du1, gs["ln1_g"], gs["ln1_b"] = ln_bwd(dx1, s["xh1"], s["rs1"], sm["ln1_g"], name=n + "ln1_bwd")
    gw["w_out"] = matmul(s["merged"], du1, mode="tn", tm=1024, tn=1024, tk=512, outs=two, name=n + "g_out")
    dmerged = matmul(du1, W["w_out"], mode="nt", tm=1024, tn=1024, tk=1024, name=n + "d_merged")
    dr1, dr2, dr3, dg1, dg2, dg3 = merge_bwd(dmerged, *s["r"], s["proj"], name=n + "merge_bwd")
    d_branch = {}
    for nm, a, dr in (("p_ret", s["retg"], dr1), ("p_sb", s["sb"], dr2), ("p_sgu", s["sg"], dr3)):
        gw[nm] = matmul(a, dr, mode="tn", tm=512, tn=1024, tk=512, outs=two, name=n + "g_" + nm)
        d_branch[nm] = matmul(dr, W[nm], mode="nt", tm=1024, tn=512, tk=1024, name=n + "d_" + nm)
    dret, gs["ret_gn_g"], gs["ret_gn_b"] = ret_bwd(s["proj"], *rope, rconsts, sm["ret_gn_g"], sm["ret_gn_b"], s["raw"],
                                                    s["states"], d_branch["p_ret"], name=n + "ret_bwd")
    dsq, dsk, dsv, job_out = sb_bwd(s["proj"], s["sb"], d_branch["p_sb"], name=n + "sb_bwd", job=job)
    dsgu, gs["sgu_w"], dbias, gs["sgu_ln_g"], gs["sgu_ln_b"] = sgu_bwd(
        s["proj"], sm["sgu_ln_g"], sm["sgu_ln_b"], sm["sgu_w"], sm["sgu_bias"], d_branch["p_sgu"], name=n + "sgu_bwd")
    gs["sgu_b"] = dbias[:, :, 0]
    dproj = jnp.concatenate([dret, dsq, dsk, dsv, dsgu, dg1, dg2, dg3], axis=1)
    gw["w_in"] = matmul(s["x0"], dproj, mode="tn", tm=1024, tn=1536, tk=512, outs=two, name=n + "g_in")
    dx0 = matmul(dproj, W["w_in"], mode="nt", tm=1024, tn=1024, tk=1536,
                 epi=lambda acc, d: (acc + ALPHA * d,), tiles=(du1,), name=n + "d_x0")
    return dx0, gw, gs, job_out


def local_step(x, target, small, plan):
    T = x.shape[0]
    rope = _rope_tables(T)
    rconsts = _ret_consts()
    sms = []
    for l in range(DEPTH):
        sm = {k: small[k][l][None, :] for k in SMALL if k not in ("sgu_w", "sgu_b")}
        sm["sgu_w"] = small["sgu_w"][l]
        sm["sgu_bias"] = jnp.broadcast_to(small["sgu_b"][l][:, :, None], (4, CHUNK, CHUNK))
        sms.append(sm)
    h, saved = x, []
    for l in range(DEPTH):
        h, s, job_out = layer_forward(l, h, plan.weights(l), sms[l], rope, rconsts, plan.fwd_job(l))
        plan.fwd_done(l, job_out)
        saved.append(s)
    dy, sq = loss_head(h, target)
    gs = {k: [None] * DEPTH for k in SMALL}
    for l in reversed(range(DEPTH)):
        dy, gwl, gsl, job_out = layer_backward(l, dy, saved[l], plan.weights(l), sms[l], rope, rconsts, plan.bwd_job(l))
        plan.bwd_done(l, job_out)
        plan.grads(l, gwl)
        for k in SMALL:
            gs[k][l] = gsl[k].reshape(small[k].shape[1:])
    return sq[0, 0], dy, {k: jnp.stack(v) for k, v in gs.items()}


class _StepPlan:
    def __init__(self, shards16):
        self.shards16 = shards16
        self.axes = [BIG_AXIS[k] for k in BIG]
        self.full = {0: run_job(gather_job(shards16[0], self.axes), name="gather_l0")}
        self.gw, self.filled = {}, None

    def weights(self, l):
        return dict(zip(BIG, self.full[l]))

    def grads(self, l, gw):
        self.gw[l] = gw

    def fwd_job(self, l):
        return gather_job(self.shards16[l + 1], self.axes) if l + 1 < DEPTH else None

    def fwd_done(self, l, outs):
        if outs:
            self.full[l + 1] = outs

    def _scatter(self, l):
        gw = self.gw[l]
        return scatter_job(l, [gw[k][1] for k in BIG], [gw[k][0] for k in BIG], self.axes, self.filled)

    def bwd_job(self, l):
        return self._scatter(l + 1) if l + 1 < DEPTH else None

    def bwd_done(self, l, outs):
        if outs:
            self.filled = (outs[:len(BIG)], outs[len(BIG):])

    def finish(self):
        outs = run_job(self._scatter(0), name="scatter_l0")
        return outs[:len(BIG)], outs[len(BIG):]


def _flat2(a):
    return a.reshape(-1, a.shape[-1])


def kernel(x, w_in, ret_gn_g, ret_gn_b, sgu_ln_g, sgu_ln_b, sgu_w, sgu_b, p_ret, p_sb, p_sgu, w_out, ln1_g, ln1_b, w_up, w_down, ln2_g, ln2_b, loss_target, m_w_in, m_ret_gn_g, m_ret_gn_b, m_sgu_ln_g, m_sgu_ln_b, m_sgu_w, m_sgu_b, m_p_ret, m_p_sb, m_p_sgu, m_w_out, m_ln1_g, m_ln1_b, m_w_up, m_w_down, m_ln2_g, m_ln2_b, v_w_in, v_ret_gn_g, v_ret_gn_b, v_sgu_ln_g, v_sgu_ln_b, v_sgu_w, v_sgu_b, v_p_ret, v_p_sb, v_p_sgu, v_w_out, v_ln1_g, v_ln1_b, v_w_up, v_w_down, v_ln2_g, v_ln2_b):
    given = dict(locals())
    order = BIG[:1] + SMALL[:6] + BIG[1:5] + SMALL[6:8] + BIG[5:7] + SMALL[8:10]
    L = DEPTH

    shards16 = [[_rows_call(lambda a: (a,), [(given[k], l)], [BF16], name=f"cast_{k}_{l}")[0] for k in BIG]
                for l in range(L)]
    plan = _StepPlan(shards16)
    sq, dx, gs = local_step(x[0], loss_target[0], {k: given[k] for k in SMALL}, plan)
    loss = 0.5 * lax.psum(sq, ("x", "y", "c"))

    recv, own = plan.finish()
    parts = []
    for a, k in enumerate(BIG):
        o = _flat2(own[a])
        rv = recv[a].reshape(3, *o.shape)
        (part,) = _rows_call(lambda o_, a_, b_, c_: (((o_ + a_.astype(F32)) + b_.astype(F32)) + c_.astype(F32),),
                             [o, (rv, 0), (rv, 1), (rv, 2)], [F32], name="chip_sum_" + k)
        parts.append(part)
    others = swap_with_sibling(parts)
    out = {}
    for a, k in enumerate(BIG):
        shp = given[k].shape
        res = _rows_call(lambda p_, q_, w_, m_, v_: (p_ + q_,) + _adamw(w_, p_ + q_, m_, v_),
                         [parts[a], others[a], _flat2(given[k]), _flat2(given["m_" + k]), _flat2(given["v_" + k])],
                         [F32] * 4, name="adamw_" + k)
        out[k] = [r.reshape(shp) for r in res]

    def pack(d, pre=""):
        return jnp.concatenate([d[pre + k].reshape(-1) for k in SMALL]).reshape(-1, 128)

    g_small = allreduce_small(pack(gs))
    res = _rows_call(lambda g_, w_, m_, v_: (g_,) + _adamw(w_, g_, m_, v_),
                     [g_small, pack(given), pack(given, "m_"), pack(given, "v_")], [F32] * 4, name="adamw_small", tr=8 * 47)
    off = 0
    for k in SMALL:
        sz = given[k].size
        out[k] = [r.reshape(-1)[off:off + sz].reshape(given[k].shape) for r in res]
        off += sz

    grads = [out[k][0] for k in order]
    deltas = [out[k][1] for k in order]
    new_m = [out[k][2] for k in order]
    new_v = [out[k][3] for k in order]
    return (loss, dx[None], *grads, *deltas, *new_m, *new_v)
```

```python
import functools
import math

import jax
import jax.numpy as jnp
from jax import lax
from jax.experimental import pallas as pl
from jax.experimental.pallas import tpu as pltpu

F32 = jnp.float32
BF16 = jnp.bfloat16

D_MODEL = 1024
SEQ = 4096
DEPTH = 2
CHUNK = 128
RET_HEADS = 4
BRANCH_W = 512
N_IN = 7680
D_FF = 4096
LN_EPS = 1e-5
ROPE_BASE = 10000.0
ALPHA = (2 * DEPTH) ** 0.25
RET_SCALE = 128 ** -0.5
SB_SCALE = 64 ** -0.5
C_RET, C_SB, C_SGU, C_GATE = 0, 2048, 3584, 4608

ADAM_LR, ADAM_B1, ADAM_B2, ADAM_EPS, ADAM_WD, ADAM_STEP = 0.001, 0.9, 0.999, 1e-08, 0.01, 10

N_CHIPS = 4
VMEM_LIMIT = 56 * 1024 * 1024
MESH = pl.DeviceIdType.MESH

NN = ((1,), (0,))
NT = ((1,), (1,))
TN = ((0,), (0,))


def _dot(a, b, dims):
    return lax.dot_general(a, b, (dims, ((), ())), preferred_element_type=F32)


def _params(sem):
    return pltpu.CompilerParams(dimension_semantics=sem, vmem_limit_bytes=VMEM_LIMIT)


def _relu2(h):
    r = jnp.maximum(h, 0.0)
    return r * r


def matmul(a, b, *, mode, tm, tn, tk, outs=((F32, None),), pro=None, epi=None, tiles=(), rows=(), name):
    if mode == "nn":
        (M, K), N = a.shape, b.shape[1]
    elif mode == "nt":
        (M, K), N = a.shape, b.shape[0]
    else:
        (K, M), N = a.shape, b.shape[1]
    tm, tn, tk = min(tm, M), min(tn, N), min(tk, K)
    assert M % tm == 0 and N % tn == 0 and K % tk == 0, (name, M, N, K, tm, tn, tk)
    if mode == "nn":
        a_spec = pl.BlockSpec((tm, tk), lambda i, j, k: (i, k))
        b_spec = pl.BlockSpec((tk, tn), lambda i, j, k: (k, j))
        dims = NN
    elif mode == "nt":
        a_spec = pl.BlockSpec((tm, tk), lambda i, j, k: (i, k))
        b_spec = pl.BlockSpec((tn, tk), lambda i, j, k: (j, k))
        dims = NT
    else:
        a_spec = pl.BlockSpec((tk, tm), lambda i, j, k: (k, i))
        b_spec = pl.BlockSpec((tk, tn), lambda i, j, k: (k, j))
        dims = TN
    nk = K // tk
    nt_, nr, no = len(tiles), len(rows), len(outs)

    def body(a_ref, b_ref, *rest):
        tile_refs = rest[:nt_]
        row_refs = rest[nt_:nt_ + nr]
        out_refs = rest[nt_ + nr:nt_ + nr + no]
        av = a_ref[...]
        if pro is not None:
            av = pro(av)
        p = _dot(av.astype(BF16), b_ref[...].astype(BF16), dims)

        def finish(acc):
            vals = (acc,) * no if epi is None else epi(acc, *[r[...] for r in tile_refs], *[r[...] for r in row_refs])
            for o_ref, v in zip(out_refs, vals):
                o_ref[...] = v.astype(o_ref.dtype)

        if nk == 1:
            finish(p)
        else:
            acc_ref = rest[-1]
            k = pl.program_id(2)

            @pl.when(k == 0)
            def _():
                acc_ref[...] = p

            @pl.when(k > 0)
            def _():
                acc_ref[...] += p

            @pl.when(k == nk - 1)
            def _():
                finish(acc_ref[...])

    out_shape, out_specs = [], []
    for dt, width in outs:
        if width is None:
            out_shape.append(jax.ShapeDtypeStruct((M, N), dt))
            out_specs.append(pl.BlockSpec((tm, tn), lambda i, j, k: (i, j)))
        else:
            assert N == tn
            out_shape.append(jax.ShapeDtypeStruct((M, width), dt))
            out_specs.append(pl.BlockSpec((tm, width), lambda i, j, k: (i, 0)))
    in_specs = [a_spec, b_spec]
    in_specs += [pl.BlockSpec((tm, tn), lambda i, j, k: (i, j)) for _ in tiles]
    in_specs += [pl.BlockSpec((1, tn), lambda i, j, k: (0, j)) for _ in rows]
    res = pl.pallas_call(
        body, name=name, grid=(M // tm, N // tn, nk),
        in_specs=in_specs, out_specs=out_specs, out_shape=out_shape,
        scratch_shapes=[pltpu.VMEM((tm, tn), F32)] if nk > 1 else [],
        compiler_params=_params(("parallel", "parallel", "arbitrary")),
    )(a, b, *tiles, *rows)
    return res[0] if no == 1 else res


def _ln_epi(acc, res, g, b):
    u = ALPHA * res + acc
    mu = jnp.mean(u, axis=-1, keepdims=True)
    xc = u - mu
    var = jnp.mean(xc * xc, axis=-1, keepdims=True)
    rstd = lax.rsqrt(var + LN_EPS)
    xhat = xc * rstd
    return xhat * g + b, xhat, jnp.broadcast_to(rstd, (u.shape[0], 128))


def matmul_ln(a, w, res, g, b, *, pro=None, tk, name):
    n = w.shape[1]
    return matmul(a, w, mode="nn", tm=512, tn=n, tk=tk, pro=pro, epi=_ln_epi, tiles=(res,), rows=(g, b),
                  outs=((F32, None), (F32, None), (F32, 128)), name=name)


def ln_bwd(dy, xhat, rstd, g, *, name):
    T, D = dy.shape
    tm = min(512, T)

    def body(dy_ref, xh_ref, rs_ref, g_ref, du_ref, dg_ref, db_ref):
        dyv, xh = dy_ref[...], xh_ref[...]
        r = rs_ref[:, 0:1]
        dxh = dyv * g_ref[...]
        m1 = jnp.mean(dxh, axis=-1, keepdims=True)
        m2 = jnp.mean(dxh * xh, axis=-1, keepdims=True)
        du_ref[...] = r * (dxh - m1 - xh * m2)

        @pl.when(pl.program_id(0) == 0)
        def _():
            dg_ref[...] = jnp.zeros_like(dg_ref)
            db_ref[...] = jnp.zeros_like(db_ref)

        dg_ref[...] += jnp.sum(dyv * xh, axis=0, keepdims=True)
        db_ref[...] += jnp.sum(dyv, axis=0, keepdims=True)

    row = pl.BlockSpec((tm, D), lambda i: (i, 0))
    vec = pl.BlockSpec((1, D), lambda i: (0, 0))
    return pl.pallas_call(
        body, name=name, grid=(T // tm,),
        in_specs=[row, row, pl.BlockSpec((tm, 128), lambda i: (i, 0)), vec],
        out_specs=[row, vec, vec],
        out_shape=[jax.ShapeDtypeStruct((T, D), F32), jax.ShapeDtypeStruct((1, D), F32), jax.ShapeDtypeStruct((1, D), F32)],
        compiler_params=_params(("arbitrary",)),
    )(dy, xhat, rstd, g)


def loss_head(y, target):
    T, D = y.shape
    tm = min(512, T)

    def body(y_ref, t_ref, dy_ref, s_ref):
        e = y_ref[...] - t_ref[...]
        dy_ref[...] = e * (1.0 / D)

        @pl.when(pl.program_id(0) == 0)
        def _():
            s_ref[...] = jnp.zeros_like(s_ref)

        s_ref[...] += jnp.sum(jnp.mean(e * e, axis=-1, keepdims=True))

    row = pl.BlockSpec((tm, D), lambda i: (i, 0))
    return pl.pallas_call(
        body, name="loss_head", grid=(T // tm,),
        in_specs=[row, row], out_specs=[row, pl.BlockSpec((8, 128), lambda i: (0, 0))],
        out_shape=[jax.ShapeDtypeStruct((T, D), F32), jax.ShapeDtypeStruct((8, 128), F32)],
        compiler_params=_params(("arbitrary",)),
    )(y, target)


def _rope_tables(T):
    half = 64
    inv_freq = ROPE_BASE ** (-jnp.arange(half, dtype=F32) / half)
    ang = jnp.arange(T, dtype=jnp.int32).astype(F32)[:, None] * inv_freq[None, :]
    cos, sin = jnp.cos(ang), jnp.sin(ang)
    return jnp.concatenate([cos, cos], axis=1), jnp.concatenate([-sin, sin], axis=1)


def _ret_consts():
    H = RET_HEADS
    log_g = jnp.log(1.0 - 2.0 ** (-5.0 - jnp.arange(H, dtype=F32)))
    idx = jnp.arange(CHUNK, dtype=F32)
    diff = idx[:, None] - idx[None, :]
    dmat = jnp.where(diff[None] >= 0, jnp.exp(log_g[:, None, None] * diff[None]), 0.0)
    kd = jnp.exp(log_g[:, None] * (CHUNK - 1 - idx)[None, :])
    qd = jnp.exp(log_g[:, None] * (idx + 1.0)[None, :])
    cd = jnp.exp(log_g * CHUNK)
    full = (H, CHUNK, CHUNK)
    return (dmat.astype(F32), jnp.broadcast_to(kd[:, :, None], full), jnp.broadcast_to(qd[:, :, None], full),
            jnp.broadcast_to(cd[:, None, None], full))


def _swap_halves(v):
    return pltpu.roll(v, 64, 1)


def _group_norm(o):
    mu = jnp.mean(o, axis=-1, keepdims=True)
    xc = o - mu
    var = jnp.mean(xc * xc, axis=-1, keepdims=True)
    rstd = lax.rsqrt(var + LN_EPS)
    return xc * rstd, rstd


def ret_fwd(proj, cosf, sinf, consts, gn_g, gn_b, *, name):
    T = proj.shape[0]
    tb = min(512, T)
    nch = tb // CHUNK
    H = RET_HEADS

    def body(p_ref, cos_ref, sin_ref, dm_ref, kd_ref, qd_ref, cd_ref, g_ref, b_ref, out_ref, raw_ref, st_ref, s_ref):
        @pl.when(pl.program_id(0) == 0)
        def _():
            s_ref[...] = jnp.zeros_like(s_ref)

        for c in range(nch):
            r = slice(c * CHUNK, (c + 1) * CHUNK)
            cs, sn = cos_ref[r, :], sin_ref[r, :]
            for h in range(H):
                hc = slice(h * 128, (h + 1) * 128)
                q = p_ref[r, h * 128:(h + 1) * 128]
                k = p_ref[r, 512 + h * 128:512 + (h + 1) * 128]
                v = p_ref[r, 1024 + h * 128:1024 + (h + 1) * 128]
                gt = p_ref[r, 1536 + h * 128:1536 + (h + 1) * 128]
                qr = q * cs + _swap_halves(q) * sn
                kr = (k * cs + _swap_halves(k) * sn) * RET_SCALE
                sprev = s_ref[h]
                st_ref[c, h] = sprev
                qb, kb, vb = qr.astype(BF16), kr.astype(BF16), v.astype(BF16)
                s = _dot(qb, kb, NT) * dm_ref[h]
                o = _dot(s.astype(BF16), vb, NN) + _dot((qr * qd_ref[h]).astype(BF16), sprev.astype(BF16), NN)
                s_ref[h] = sprev * cd_ref[h] + _dot((kr * kd_ref[h]).astype(BF16), vb, TN)
                raw_ref[r, hc] = o
                y, _ = _group_norm(o)
                out_ref[r, hc] = (gt * jax.nn.sigmoid(gt)) * (y * g_ref[:, hc] + b_ref[:, hc])

    cmat = pl.BlockSpec((H, CHUNK, CHUNK), lambda i: (0, 0, 0))
    vec = pl.BlockSpec((1, BRANCH_W), lambda i: (0, 0))
    rope = pl.BlockSpec((tb, 128), lambda i: (i, 0))
    blk = pl.BlockSpec((tb, BRANCH_W), lambda i: (i, 0))
    return pl.pallas_call(
        body, name=name, grid=(T // tb,),
        in_specs=[pl.BlockSpec((tb, 2048), lambda i: (i, 0)), rope, rope, cmat, cmat, cmat, cmat, vec, vec],
        out_specs=[blk, blk, pl.BlockSpec((nch, H, CHUNK, CHUNK), lambda i: (i, 0, 0, 0))],
        out_shape=[jax.ShapeDtypeStruct((T, BRANCH_W), F32), jax.ShapeDtypeStruct((T, BRANCH_W), F32),
                   jax.ShapeDtypeStruct((T // CHUNK, H, CHUNK, CHUNK), F32)],
        scratch_shapes=[pltpu.VMEM((H, CHUNK, CHUNK), F32)],
        compiler_params=_params(("arbitrary",)),
    )(proj, cosf, sinf, *consts, gn_g, gn_b)


def ret_bwd(proj, cosf, sinf, consts, gn_g, gn_b, raw, states, dout, *, name):
    T = proj.shape[0]
    tb = min(512, T)
    nch = tb // CHUNK
    nb = T // tb
    H = RET_HEADS

    def body(p_ref, cos_ref, sin_ref, dm_ref, kd_ref, qd_ref, cd_ref, g_ref, b_ref, raw_ref, st_ref, do_ref,
             dp_ref, dg_ref, db_ref, ds_ref):
        @pl.when(pl.program_id(0) == 0)
        def _():
            ds_ref[...] = jnp.zeros_like(ds_ref)
            dg_ref[...] = jnp.zeros_like(dg_ref)
            db_ref[...] = jnp.zeros_like(db_ref)

        for c in reversed(range(nch)):
            r = slice(c * CHUNK, (c + 1) * CHUNK)
            cs, sn = cos_ref[r, :], sin_ref[r, :]
            for h in range(H):
                hc = slice(h * 128, (h + 1) * 128)
                q = p_ref[r, h * 128:(h + 1) * 128]
                k = p_ref[r, 512 + h * 128:512 + (h + 1) * 128]
                v = p_ref[r, 1024 + h * 128:1024 + (h + 1) * 128]
                gt = p_ref[r, 1536 + h * 128:1536 + (h + 1) * 128]
                qr = q * cs + _swap_halves(q) * sn
                kr = (k * cs + _swap_halves(k) * sn) * RET_SCALE
                sprev = st_ref[c, h]
                gv = g_ref[:, hc]
                y, rstd = _group_norm(raw_ref[r, hc])
                d_out = do_ref[r, hc]
                sg = jax.nn.sigmoid(gt)
                d_gate = d_out * (y * gv + b_ref[:, hc]) * (sg * (1.0 + gt * (1.0 - sg)))
                d_aff = d_out * (gt * sg)
                dg_ref[:, hc] += jnp.sum(d_aff * y, axis=0, keepdims=True)
                db_ref[:, hc] += jnp.sum(d_aff, axis=0, keepdims=True)
                dxh = d_aff * gv
                m1 = jnp.mean(dxh, axis=-1, keepdims=True)
                m2 = jnp.mean(dxh * y, axis=-1, keepdims=True)
                d_o = (rstd * (dxh - m1 - y * m2)).astype(BF16)
                qb, kb, vb = qr.astype(BF16), kr.astype(BF16), v.astype(BF16)
                dm, kd, qd = dm_ref[h], kd_ref[h], qd_ref[h]
                p = (_dot(qb, kb, NT) * dm).astype(BF16)
                dp = (_dot(d_o, vb, NT) * dm).astype(BF16)
                dsn = ds_ref[h]
                dsb = dsn.astype(BF16)
                dq_r = _dot(dp, kb, NN) + _dot(d_o, sprev.astype(BF16), NT) * qd
                dk_r = (_dot(dp, qb, TN) + _dot(vb, dsb, NT) * kd) * RET_SCALE
                d_v = _dot(p, d_o, TN) + _dot((kr * kd).astype(BF16), dsb, NN)
                ds_ref[h] = dsn * cd_ref[h] + _dot((qr * qd).astype(BF16), d_o, TN)
                dp_ref[r, h * 128:(h + 1) * 128] = (dq_r * cs - _swap_halves(dq_r) * sn).astype(BF16)
                dp_ref[r, 512 + h * 128:512 + (h + 1) * 128] = (dk_r * cs - _swap_halves(dk_r) * sn).astype(BF16)
                dp_ref[r, 1024 + h * 128:1024 + (h + 1) * 128] = d_v.astype(BF16)
                dp_ref[r, 1536 + h * 128:1536 + (h + 1) * 128] = d_gate.astype(BF16)

    cmat = pl.BlockSpec((H, CHUNK, CHUNK), lambda i: (0, 0, 0))
    vec = pl.BlockSpec((1, BRANCH_W), lambda i: (0, 0))
    rope = pl.BlockSpec((tb, 128), lambda i: (nb - 1 - i, 0))
    blk = pl.BlockSpec((tb, BRANCH_W), lambda i: (nb - 1 - i, 0))
    wide = pl.BlockSpec((tb, 2048), lambda i: (nb - 1 - i, 0))
    return pl.pallas_call(
        body, name=name, grid=(nb,),
        in_specs=[wide, rope, rope, cmat, cmat, cmat, cmat, vec, vec, blk,
                  pl.BlockSpec((nch, H, CHUNK, CHUNK), lambda i: (nb - 1 - i, 0, 0, 0)), blk],
        out_specs=[wide, vec, vec],
        out_shape=[jax.ShapeDtypeStruct((T, 2048), BF16), jax.ShapeDtypeStruct((1, BRANCH_W), F32),
                   jax.ShapeDtypeStruct((1, BRANCH_W), F32)],
        scratch_shapes=[pltpu.VMEM((H, CHUNK, CHUNK), F32)],
        compiler_params=_params(("arbitrary",)),
    )(proj, cosf, sinf, *consts, gn_g, gn_b, raw, states, dout)


def _sb_masks():
    row = lax.broadcasted_iota(jnp.int32, (CHUNK, CHUNK), 0)
    lane = lax.broadcasted_iota(jnp.int32, (CHUNK, CHUNK), 1)
    return row, lane


SB_QT = 512


def _pair(v):
    hi = v.astype(BF16)
    return jnp.concatenate([hi, (v - hi.astype(F32)).astype(BF16)], axis=1)


def _sb_consts():
    r = lax.broadcasted_iota(jnp.int32, (256, 256), 0) & 127
    c = lax.broadcasted_iota(jnp.int32, (256, 256), 1)
    ones = c >= 128
    lane = lax.broadcasted_iota(jnp.int32, (CHUNK, CHUNK), 1)
    return (ones | (r > c)).astype(BF16), (ones | (r >= c)).astype(BF16), (lane < 64, lane >= 64)


def _per_head(x, hms):
    return jnp.concatenate([jnp.where(hm, x, 0.0) for hm in hms], axis=0).astype(BF16)


def _sb_logits(qb, kb2, mask2):
    z = _dot(qb, kb2, NT)
    l1p = jnp.log(1.0 + jnp.exp(-jnp.abs(z)))
    lsp = jnp.minimum(z, 0.0) - l1p
    lsn = lsp - z
    if mask2 is not None:
        lsn = jnp.where(mask2, lsn, 0.0)
    return lsp, lsn


def _sb_tile_mask(qt):
    trow = lax.broadcasted_iota(jnp.int32, (qt, 256), 0)
    tlane = lax.broadcasted_iota(jnp.int32, (qt, 256), 1) & 127
    return lambda m: (tlane + m * CHUNK) < trow


def sb_fwd(proj, *, name, job=None):
    T = proj.shape[0]
    qt = min(SB_QT, T)
    nsub = qt // CHUNK
    cb = C_SB // 128

    def body(q_ref, k_ref, v_ref, o_ref):
        u_gt, _, hms = _sb_consts()
        tile_mask = _sb_tile_mask(qt)

        def qtile(i, _):
            rq = pl.ds(pl.multiple_of(i * qt, qt), qt)
            qb = (q_ref[rq, :] * SB_SCALE).astype(BF16)

            def step(j, state, mask2):
                carry, acc = list(state[:2]), state[2]
                rk = pl.ds(pl.multiple_of(j * CHUNK, CHUNK), CHUNK)
                lsp, lsn = _sb_logits(qb, _per_head(k_ref[rk, :], hms), mask2)
                vf = v_ref[rk, :]
                for h in range(2):
                    hc = slice(h * 128, (h + 1) * 128)
                    r = _dot(_pair(lsn[:, hc]), u_gt, NN)
                    a = jnp.exp(lsp[:, hc] + r[:, :128] + carry[h])
                    if mask2 is not None:
                        a = jnp.where(mask2[:, hc], a, 0.0)
                    carry[h] = carry[h] + r[:, 128:]
                    vh = jnp.where(hms[h], vf, 0.0).astype(BF16)
                    acc = acc + _dot(_pair(a), jnp.concatenate([vh, vh], axis=0), NN)
                return carry[0], carry[1], acc

            zero = jnp.zeros((qt, 128), F32)
            state = lax.fori_loop(
                0, nsub, lambda mm, st: step(i * nsub + nsub - 1 - mm, st, tile_mask(nsub - 1 - mm)), (zero, zero, zero))
            state = lax.fori_loop(0, i * nsub, lambda jj, st: step(i * nsub - 1 - jj, st, None), state)
            o_ref[rq, :] = state[2]
            return 0

        lax.fori_loop(0, T // qt, qtile, 0)

    def col(off):
        return pl.BlockSpec((T, 128), lambda hp: (0, off + hp))

    steps = BRANCH_W // 128
    j = _job_args(job, 3, 1)
    res = pl.pallas_call(
        _hosting(body, job, 3, 1, 0, steps), name=name, grid=(steps,),
        in_specs=[col(cb), col(cb + 4), col(cb + 8)] + j["in_specs"], out_specs=[col(0)] + j["out_specs"],
        out_shape=[jax.ShapeDtypeStruct((T, BRANCH_W), F32)] + j["out_shape"],
        scratch_shapes=j["scratch"], input_output_aliases=j["aliases"],
        compiler_params=_params(("parallel",) if job is None else ("arbitrary",)),
    )(proj, proj, proj, *j["ins"])
    return res[0], list(res[1:])


def sb_bwd(proj, out, dout, *, name, job=None):
    T = proj.shape[0]
    qt = min(SB_QT, T)
    nsub = qt // CHUNK
    cb = C_SB // 128

    def body(q_ref, k_ref, v_ref, o_ref, do_ref, dq_ref, dk_ref, dv_ref, dkt_ref, dvt_ref):
        u_gt, u_ge, hms = _sb_consts()
        tile_mask = _sb_tile_mask(qt)
        tall_lane = lax.broadcasted_iota(jnp.int32, (qt, 128), 1)
        top = lax.broadcasted_iota(jnp.int32, (CHUNK, CHUNK), 0) < 64
        dkt_ref[...] = jnp.zeros_like(dkt_ref)
        dvt_ref[...] = jnp.zeros_like(dvt_ref)

        def qtile(i, _):
            rq = pl.ds(pl.multiple_of(i * qt, qt), qt)
            qs = q_ref[rq, :] * SB_SCALE
            qb, q_t = qs.astype(BF16), qs.T.astype(BF16)
            dov = do_ref[rq, :]
            dob, do_t = dov.astype(BF16), dov.T.astype(BF16)
            prod = dob.astype(F32) * o_ref[rq, :]
            total = [jnp.broadcast_to(jnp.sum(jnp.where(hm, prod, 0.0), axis=1, keepdims=True), (qt, 128))
                     for hm in (tall_lane < 64, tall_lane >= 64)]

            def step(j, state, mask2):
                c_l, c_w, dq = list(state[:2]), list(state[2:4]), state[4]
                rk = pl.ds(pl.multiple_of(j * CHUNK, CHUNK), CHUNK)
                kb2, vb2 = _per_head(k_ref[rk, :], hms), _per_head(v_ref[rk, :], hms)
                lsp, lsn = _sb_logits(qb, kb2, mask2)
                da = _dot(dob, vb2, NT)
                sp = jnp.exp(lsp)
                a_b, dz_b = [], []
                for h in range(2):
                    hc = slice(h * 128, (h + 1) * 128)
                    r = _dot(_pair(lsn[:, hc]), u_gt, NN)
                    a = jnp.exp(lsp[:, hc] + r[:, :128] + c_l[h])
                    if mask2 is not None:
                        a = jnp.where(mask2[:, hc], a, 0.0)
                    c_l[h] = c_l[h] + r[:, 128:]
                    w = a * da[:, hc]
                    r = _dot(_pair(w), u_ge, NN)
                    later_w = r[:, :128] + c_w[h]
                    c_w[h] = c_w[h] + r[:, 128:]
                    dz = w * (1.0 - sp[:, hc]) - sp[:, hc] * (total[h] - later_w)
                    if mask2 is not None:
                        dz = jnp.where(mask2[:, hc], dz, 0.0)
                    a_b.append(a.astype(BF16))
                    dz_b.append(dz.astype(BF16))
                a_b, dz_b = jnp.concatenate(a_b, axis=1), jnp.concatenate(dz_b, axis=1)
                dkt = _dot(q_t, dz_b, NN)
                dvt = _dot(do_t, a_b, NN)
                dkt_ref[j] += jnp.where(top, dkt[:, :128], dkt[:, 128:])
                dvt_ref[j] += jnp.where(top, dvt[:, :128], dvt[:, 128:])
                return c_l[0], c_l[1], c_w[0], c_w[1], dq + _dot(dz_b, kb2, NN)

            zero = jnp.zeros((qt, 128), F32)
            state = lax.fori_loop(
                0, nsub, lambda mm, st: step(i * nsub + nsub - 1 - mm, st, tile_mask(nsub - 1 - mm)), (zero,) * 5)
            state = lax.fori_loop(0, i * nsub, lambda jj, st: step(i * nsub - 1 - jj, st, None), state)
            dq_ref[rq, :] = (state[4] * SB_SCALE).astype(BF16)
            return 0

        lax.fori_loop(0, T // qt, qtile, 0)

        def untranspose(jb, _):
            rk = pl.ds(pl.multiple_of(jb * CHUNK, CHUNK), CHUNK)
            dk_ref[rk, :] = dkt_ref[jb].T.astype(BF16)
            dv_ref[rk, :] = dvt_ref[jb].T.astype(BF16)
            return 0

        lax.fori_loop(0, T // CHUNK, untranspose, 0)

    def col(off):
        return pl.BlockSpec((T, 128), lambda hp: (0, off + hp))

    o16 = jax.ShapeDtypeStruct((T, BRANCH_W), BF16)
    steps = BRANCH_W // 128
    j = _job_args(job, 5, 3)
    acc = pltpu.VMEM((T // CHUNK, CHUNK, CHUNK), F32)
    res = pl.pallas_call(
        _hosting(body, job, 5, 3, 2, steps), name=name, grid=(steps,),
        in_specs=[col(cb), col(cb + 4), col(cb + 8), col(0), col(0)] + j["in_specs"],
        out_specs=[col(0), col(0), col(0)] + j["out_specs"], out_shape=[o16, o16, o16] + j["out_shape"],
        scratch_shapes=[acc, acc] + j["scratch"], input_output_aliases=j["aliases"],
        compiler_params=_params(("parallel",) if job is None else ("arbitrary",)),
    )(proj, proj, proj, out, dout, *j["ins"])
    return res[0], res[1], res[2], list(res[3:])


_G0 = math.sqrt(2.0 / math.pi)
_G1 = 0.044715


def _gelu(x):
    return 0.5 * x * (1.0 + jnp.tanh(_G0 * (x + _G1 * x * x * x)))


def _gelu_grad(x):
    t = jnp.tanh(_G0 * (x + _G1 * x * x * x))
    return 0.5 * (1.0 + t) + 0.5 * x * (1.0 - t * t) * (_G0 * (1.0 + 3.0 * _G1 * x * x))


def _tril():
    row, lane = _sb_masks()
    return row >= lane


def sgu_fwd(proj, ln_g, ln_b, w, bias, *, name):
    T = proj.shape[0]
    tb = min(512, T)
    G = BRANCH_W // 128

    def body(u_ref, v_ref, g_ref, b_ref, w_ref, bias_ref, o_ref):
        vv = _gelu(v_ref[...])
        xh, _ = _group_norm(vv)
        vn = (xh * g_ref[...] + b_ref[...]).astype(BF16)
        tril = _tril()
        for g in range(G):
            wg = jnp.where(tril, w_ref[g], 0.0).astype(BF16)
            gc = slice(g * 128, (g + 1) * 128)
            for c in range(tb // CHUNK):
                r = slice(c * CHUNK, (c + 1) * CHUNK)
                sv = _dot(wg, vn[r, gc], NN) + bias_ref[g]
                o_ref[r, gc] = _gelu(u_ref[r, gc]) * sv

    cu, cv = C_SGU // BRANCH_W, C_SGU // BRANCH_W + 1
    vec = pl.BlockSpec((1, BRANCH_W), lambda i: (0, 0))
    mat = pl.BlockSpec((G, CHUNK, CHUNK), lambda i: (0, 0, 0))
    return pl.pallas_call(
        body, name=name, grid=(T // tb,),
        in_specs=[pl.BlockSpec((tb, BRANCH_W), lambda i: (i, cu)), pl.BlockSpec((tb, BRANCH_W), lambda i: (i, cv)),
                  vec, vec, mat, mat],
        out_specs=pl.BlockSpec((tb, BRANCH_W), lambda i: (i, 0)),
        out_shape=jax.ShapeDtypeStruct((T, BRANCH_W), F32),
        compiler_params=_params(("parallel",)),
    )(proj, proj, ln_g, ln_b, w, bias)


def sgu_bwd(proj, ln_g, ln_b, w, bias, dout, *, name):
    T = proj.shape[0]
    tb = min(512, T)
    G = BRANCH_W // 128

    def body(u_ref, v_ref, g_ref, b_ref, w_ref, bias_ref, do_ref, dp_ref, dw_ref, dbias_ref, dg_ref, db_ref, dvn_ref):
        @pl.when(pl.program_id(0) == 0)
        def _():
            dw_ref[...] = jnp.zeros_like(dw_ref)
            dbias_ref[...] = jnp.zeros_like(dbias_ref)
            dg_ref[...] = jnp.zeros_like(dg_ref)
            db_ref[...] = jnp.zeros_like(db_ref)

        gv = v_ref[...]
        vv = _gelu(gv)
        xh, rstd = _group_norm(vv)
        vn = (xh * g_ref[...] + b_ref[...]).astype(BF16)
        tril = _tril()
        for g in range(G):
            wg = jnp.where(tril, w_ref[g], 0.0).astype(BF16)
            gc = slice(g * 128, (g + 1) * 128)
            for c in range(tb // CHUNK):
                r = slice(c * CHUNK, (c + 1) * CHUNK)
                vn_c = vn[r, gc]
                sv = _dot(wg, vn_c, NN) + bias_ref[g]
                gu = u_ref[r, gc]
                d_o = do_ref[r, gc]
                dp_ref[r, gc] = (d_o * sv * _gelu_grad(gu)).astype(BF16)
                dsv = d_o * _gelu(gu)
                dsv_b = dsv.astype(BF16)
                dvn_ref[r, gc] = _dot(wg, dsv_b, TN)
                dw_ref[g] += jnp.where(tril, _dot(dsv_b, vn_c, NT), 0.0)
                dbias_ref[g] += jnp.broadcast_to(jnp.sum(dsv, axis=1, keepdims=True), (CHUNK, CHUNK))
        dvn = dvn_ref[...]
        dg_ref[...] += jnp.sum(dvn * xh, axis=0, keepdims=True)
        db_ref[...] += jnp.sum(dvn, axis=0, keepdims=True)
        dxh = dvn * g_ref[...]
        m1 = jnp.mean(dxh, axis=-1, keepdims=True)
        m2 = jnp.mean(dxh * xh, axis=-1, keepdims=True)
        dp_ref[:, BRANCH_W:2 * BRANCH_W] = (rstd * (dxh - m1 - xh * m2) * _gelu_grad(gv)).astype(BF16)

    cu, cv = C_SGU // BRANCH_W, C_SGU // BRANCH_W + 1
    vec = pl.BlockSpec((1, BRANCH_W), lambda i: (0, 0))
    mat = pl.BlockSpec((G, CHUNK, CHUNK), lambda i: (0, 0, 0))
    blk = pl.BlockSpec((tb, BRANCH_W), lambda i: (i, 0))
    msh = jax.ShapeDtypeStruct((G, CHUNK, CHUNK), F32)
    vsh = jax.ShapeDtypeStruct((1, BRANCH_W), F32)
    return pl.pallas_call(
        body, name=name, grid=(T // tb,),
        in_specs=[pl.BlockSpec((tb, BRANCH_W), lambda i: (i, cu)), pl.BlockSpec((tb, BRANCH_W), lambda i: (i, cv)),
                  vec, vec, mat, mat, blk],
        out_specs=[pl.BlockSpec((tb, 2 * BRANCH_W), lambda i: (i, 0)), mat, mat, vec, vec],
        out_shape=[jax.ShapeDtypeStruct((T, 2 * BRANCH_W), BF16), msh, msh, vsh, vsh],
        scratch_shapes=[pltpu.VMEM((tb, BRANCH_W), F32)],
        compiler_params=_params(("arbitrary",)),
    )(proj, proj, ln_g, ln_b, w, bias, dout)


def merge_fwd(a1, a2, a3, p1, p2, p3, proj, *, name):
    T = a1.shape[0]
    tm, tn = min(1024, T), 512
    gb = C_GATE // tn

    def body(a1_ref, a2_ref, a3_ref, p1_ref, p2_ref, p3_ref, g1_ref, g2_ref, g3_ref, m_ref, r1_ref, r2_ref, r3_ref):
        m = None
        for a_ref, p_ref, g_ref, r_ref in ((a1_ref, p1_ref, g1_ref, r1_ref), (a2_ref, p2_ref, g2_ref, r2_ref),
                                           (a3_ref, p3_ref, g3_ref, r3_ref)):
            r = _dot(a_ref[...].astype(BF16), p_ref[...], NN)
            r_ref[...] = r
            t = jax.nn.sigmoid(g_ref[...]) * r
            m = t if m is None else m + t
        m_ref[...] = m

    a_spec = pl.BlockSpec((tm, BRANCH_W), lambda i, j: (i, 0))
    p_spec = pl.BlockSpec((BRANCH_W, tn), lambda i, j: (0, j))
    o_spec = pl.BlockSpec((tm, tn), lambda i, j: (i, j))
    osh = jax.ShapeDtypeStruct((T, D_MODEL), F32)
    gates = [pl.BlockSpec((tm, tn), functools.partial(lambda i, j, o: (i, o + j), o=gb + 2 * n)) for n in range(3)]
    return pl.pallas_call(
        body, name=name, grid=(T // tm, D_MODEL // tn),
        in_specs=[a_spec, a_spec, a_spec, p_spec, p_spec, p_spec, *gates],
        out_specs=[o_spec] * 4, out_shape=[osh] * 4,
        compiler_params=_params(("parallel", "parallel")),
    )(a1, a2, a3, p1, p2, p3, proj, proj, proj)


def merge_bwd(dm, r1, r2, r3, proj, *, name):
    T = dm.shape[0]
    tm, tn = min(512, T), 512
    gb = C_GATE // tn

    def body(dm_ref, r1_ref, r2_ref, r3_ref, g1_ref, g2_ref, g3_ref, dr1_ref, dr2_ref, dr3_ref, dg1_ref, dg2_ref, dg3_ref):
        d = dm_ref[...]
        for r_ref, g_ref, dr_ref, dg_ref in ((r1_ref, g1_ref, dr1_ref, dg1_ref), (r2_ref, g2_ref, dr2_ref, dg2_ref),
                                             (r3_ref, g3_ref, dr3_ref, dg3_ref)):
            s = jax.nn.sigmoid(g_ref[...])
            dr_ref[...] = (d * s).astype(BF16)
            dg_ref[...] = (d * r_ref[...] * (s * (1.0 - s))).astype(BF16)

    o_spec = pl.BlockSpec((tm, tn), lambda i, j: (i, j))
    osh = jax.ShapeDtypeStruct((T, D_MODEL), BF16)
    gates = [pl.BlockSpec((tm, tn), functools.partial(lambda i, j, o: (i, o + j), o=gb + 2 * n)) for n in range(3)]
    return pl.pallas_call(
        body, name=name, grid=(T // tm, D_MODEL // tn),
        in_specs=[o_spec] * 4 + gates, out_specs=[o_spec] * 6, out_shape=[osh] * 6,
        compiler_params=_params(("parallel", "parallel")),
    )(dm, r1, r2, r3, proj, proj, proj)


def _rows_call(fn, ins, out_dtypes, *, name, tr=256):
    first = ins[0][0] if isinstance(ins[0], tuple) else ins[0]
    R, C = first.shape[-2:]
    tr = min(tr, R)
    assert R % tr == 0, (name, R, tr)
    arrs, specs = [], []
    for x in ins:
        if isinstance(x, tuple):
            arrs.append(x[0])
            specs.append(pl.BlockSpec((None, tr, C), functools.partial(lambda i, n: (n, i, 0), n=x[1])))
        else:
            arrs.append(x)
            specs.append(pl.BlockSpec((tr, C), lambda i: (i, 0)))
    ni = len(arrs)

    def body(*refs):
        vals = fn(*[r[...] for r in refs[:ni]])
        for o_ref, v in zip(refs[ni:], vals):
            o_ref[...] = v.astype(o_ref.dtype)

    res = pl.pallas_call(
        body, name=name, grid=(R // tr,), in_specs=specs,
        out_specs=[pl.BlockSpec((tr, C), lambda i: (i, 0)) for _ in out_dtypes],
        out_shape=[jax.ShapeDtypeStruct((R, C), dt) for dt in out_dtypes],
        compiler_params=_params(("parallel",)),
    )(*arrs)
    return res


def _adamw(w, g, m, v):
    m2 = ADAM_B1 * m + (1.0 - ADAM_B1) * g
    v2 = ADAM_B2 * v + (1.0 - ADAM_B2) * (g * g)
    m_hat = m2 / (1.0 - ADAM_B1 ** ADAM_STEP)
    v_hat = v2 / (1.0 - ADAM_B2 ** ADAM_STEP)
    delta = -ADAM_LR * (m_hat / (jnp.sqrt(v_hat) + ADAM_EPS) + ADAM_WD * w)
    return delta, m2, v2


def _place():
    return lax.axis_index("x"), lax.axis_index("y"), lax.axis_index("c")


def _chip_peers(x, y, c):
    return [((1 - x, y, c), 2 * (1 - x) + y), ((x, 1 - y, c), 2 * x + 1 - y), ((1 - x, 1 - y, c), 2 * (1 - x) + 1 - y)]


def _shard_of(ref, axis, k, n):
    start = pl.multiple_of(k * n, 128)
    return ref.at[pl.ds(start, n), :] if axis == 0 else ref.at[:, pl.ds(start, n)]


ANY = pl.BlockSpec(memory_space=pl.ANY)


class CopyJob:
    def __init__(self, ins, out_shape, scratch, copies, aliases=None):
        self.ins, self.out_shape, self.scratch, self.copies = list(ins), list(out_shape), list(scratch), copies
        self.aliases = dict(aliases or {})

    def start(self, ins, outs, sems):
        local, remote, _ = self.copies(ins, outs, sems)
        for d in local + remote:
            d.start()

    def finish(self, ins, outs, sems):
        local, remote, arrivals = self.copies(ins, outs, sems)
        for d in arrivals:
            d.wait_recv()
        for d in remote:
            d.wait_send()
        for d in local:
            d.wait()


def run_job(job, *, name):
    ni, no = len(job.ins), len(job.out_shape)

    def body(*refs):
        parts = refs[:ni], refs[ni:ni + no], refs[ni + no:]
        job.start(*parts)
        job.finish(*parts)

    return pl.pallas_call(
        body, name=name, in_specs=[ANY] * ni, out_specs=[ANY] * no, out_shape=job.out_shape,
        scratch_shapes=job.scratch, input_output_aliases=job.aliases,
    )(*job.ins)


def _job_args(job, n_in, n_out):
    if job is None:
        return dict(ins=[], in_specs=[], out_specs=[], out_shape=[], scratch=[], aliases={})
    return dict(ins=job.ins, in_specs=[ANY] * len(job.ins), out_specs=[ANY] * len(job.out_shape),
                out_shape=job.out_shape, scratch=job.scratch,
                aliases={n_in + i: n_out + o for i, o in job.aliases.items()})


def _hosting(body, job, n_in, n_out, n_scratch, steps):
    if job is None:
        return body
    ji, jo = len(job.ins), len(job.out_shape)

    def hosted(*refs):
        o = n_in + ji
        s = o + n_out + jo
        parts = refs[n_in:o], refs[o + n_out:s], refs[s + n_scratch:]

        @pl.when(pl.program_id(0) == 0)
        def _():
            job.start(*parts)

        body(*refs[:n_in], *refs[o:o + n_out], *refs[s:s + n_scratch])

        @pl.when(pl.program_id(0) == steps - 1)
        def _():
            job.finish(*parts)

    return hosted


def _job_sems(n_remote, n_local):
    return [pltpu.SemaphoreType.DMA((n_remote,)), pltpu.SemaphoreType.DMA((n_remote,)), pltpu.SemaphoreType.DMA((n_local,))]


def gather_job(shards, axes):
    na = len(shards)

    def copies(ins, outs, sems):
        send, recv, loc = sems
        x, y, c = _place()
        k = 2 * x + y
        local, remote, arrivals = [], [], []
        for a in range(na):
            n = ins[a].shape[axes[a]]
            mine = _shard_of(outs[a], axes[a], k, n)
            local.append(pltpu.make_async_copy(ins[a], mine, loc.at[a]))
            for r, (peer, kp) in enumerate(_chip_peers(x, y, c)):
                s = 3 * a + r
                remote.append(pltpu.make_async_remote_copy(ins[a], mine, send.at[s], recv.at[s],
                                                           device_id=peer, device_id_type=MESH))
                arrivals.append(pltpu.make_async_remote_copy(ins[a], _shard_of(outs[a], axes[a], kp, n), send.at[s],
                                                             recv.at[s], device_id=peer, device_id_type=MESH))
        return local, remote, arrivals

    out_shape = []
    for a in range(na):
        r, c = shards[a].shape
        out_shape.append(jax.ShapeDtypeStruct((r * N_CHIPS, c) if axes[a] == 0 else (r, c * N_CHIPS), BF16))
    return CopyJob(shards, out_shape, _job_sems(3 * na, na), copies)


def scatter_job(layers, g16, g32, axes, filled):
    na = len(axes)

    def shard_shape(a):
        r, c = g32[a].shape
        return (r // N_CHIPS, c) if axes[a] == 0 else (r, c // N_CHIPS)

    def copies(ins, outs, sems):
        send, recv_sems, loc = sems
        b16, b32 = ins[:na], ins[na:2 * na]
        recv, own = outs[:na], outs[na:]
        x, y, c = _place()
        k = 2 * x + y
        local, remote = [], []
        for a in range(na):
            n = shard_shape(a)[axes[a]]
            local.append(pltpu.make_async_copy(_shard_of(b32[a], axes[a], k, n), own[a].at[layers[a]], loc.at[a]))
            for r, (peer, kp) in enumerate(_chip_peers(x, y, c)):
                remote.append(pltpu.make_async_remote_copy(_shard_of(b16[a], axes[a], kp, n), recv[a].at[r, layers[a]],
                                                           send.at[3 * a + r], recv_sems.at[3 * a + r],
                                                           device_id=peer, device_id_type=MESH))
        return local, remote, remote

    out_shape = [jax.ShapeDtypeStruct((3, DEPTH) + shard_shape(a), BF16) for a in range(na)]
    out_shape += [jax.ShapeDtypeStruct((DEPTH,) + shard_shape(a), F32) for a in range(na)]
    ins = list(g16) + list(g32)
    aliases = {}
    for a in range(na):
        if filled[a] is not None:
            aliases[len(ins)] = a
            aliases[len(ins) + 1] = na + a
            ins += list(filled[a])
    return CopyJob(ins, out_shape, _job_sems(3 * na, na), copies, aliases)


def swap_with_sibling(parts):
    na = len(parts)

    def body(*refs):
        ins, outs = refs[:na], refs[na:2 * na]
        send_sems, recv_sems = refs[2 * na:]
        x, y, c = _place()
        cps = [pltpu.make_async_remote_copy(ins[a], outs[a], send_sems.at[a], recv_sems.at[a],
                                            device_id=(x, y, 1 - c), device_id_type=MESH) for a in range(na)]
        for cp in cps:
            cp.start()
        for cp in cps:
            cp.wait()

    return pl.pallas_call(
        body, name="swap_with_sibling", in_specs=[ANY] * na, out_specs=[ANY] * na,
        out_shape=[jax.ShapeDtypeStruct(p.shape, p.dtype) for p in parts],
        scratch_shapes=[pltpu.SemaphoreType.DMA((na,)), pltpu.SemaphoreType.DMA((na,))],
    )(*parts)


def allreduce_small(p):
    R = p.shape[0]

    def body(p_ref, o_ref, buf, send_sems, recv_sems):
        x, y, c = _place()
        me = 4 * x + 2 * y + c
        cps = []
        for rel in range(1, 8):
            dx, dy, dc = rel >> 2, (rel >> 1) & 1, rel & 1
            peer = (1 - x if dx else x, 1 - y if dy else y, 1 - c if dc else c)
            cp = pltpu.make_async_remote_copy(p_ref, buf.at[me], send_sems.at[rel - 1], recv_sems.at[rel - 1],
                                              device_id=peer, device_id_type=MESH)
            cp.start()
            cps.append((cp, 4 * peer[0] + 2 * peer[1] + peer[2]))
        buf[me] = p_ref[...]
        for rel, (cp, who) in enumerate(cps):
            pltpu.make_async_remote_copy(p_ref, buf.at[who], send_sems.at[rel], recv_sems.at[rel],
                                         device_id=(x, y, c), device_id_type=MESH).wait_recv()
        acc = buf[0]
        for d in range(1, 8):
            acc = acc + buf[d]
        o_ref[...] = acc
        for cp, _ in cps:
            cp.wait_send()

    return pl.pallas_call(
        body, name="allreduce_small",
        in_specs=[pl.BlockSpec(memory_space=pltpu.VMEM)], out_specs=pl.BlockSpec(memory_space=pltpu.VMEM),
        out_shape=jax.ShapeDtypeStruct((R, 128), F32),
        scratch_shapes=[pltpu.VMEM((8, R, 128), F32), pltpu.SemaphoreType.DMA((7,)), pltpu.SemaphoreType.DMA((7,))],
        compiler_params=pltpu.CompilerParams(vmem_limit_bytes=VMEM_LIMIT),
    )(p)


BIG = ("w_in", "p_ret", "p_sb", "p_sgu", "w_out", "w_up", "w_down")
BIG_AXIS = {"w_in": 1, "p_ret": 1, "p_sb": 1, "p_sgu": 1, "w_out": 0, "w_up": 1, "w_down": 0}
SMALL = ("ret_gn_g", "ret_gn_b", "sgu_ln_g", "sgu_ln_b", "sgu_w", "sgu_b", "ln1_g", "ln1_b", "ln2_g", "ln2_b")


def layer_forward(l, x0, W, sm, rope, rconsts, job=None, job_done=None):
    n = f"l{l}_"
    proj = matmul(x0, W["w_in"], mode="nn", tm=1024, tn=640, tk=1024, name=n + "proj")
    retg, raw, states = ret_fwd(proj, *rope, rconsts, sm["ret_gn_g"], sm["ret_gn_b"], name=n + "ret_fwd")
    sb, job_out = sb_fwd(proj, name=n + "sb_fwd", job=job)
    if job is not None:
        job_done(job_out)
    sg = sgu_fwd(proj, sm["sgu_ln_g"], sm["sgu_ln_b"], sm["sgu_w"], sm["sgu_bias"], name=n + "sgu_fwd")
    merged, r1, r2, r3 = merge_fwd(retg, sb, sg, W["p_ret"], W["p_sb"], W["p_sgu"], proj, name=n + "merge_fwd")
    x1, xh1, rs1 = matmul_ln(merged, W["w_out"], x0, sm["ln1_g"], sm["ln1_b"], tk=1024, name=n + "out_ln1")
    h1 = matmul(x1, W["w_up"], mode="nn", tm=1024, tn=1024, tk=1024, name=n + "up")
    x2, xh2, rs2 = matmul_ln(h1, W["w_down"], x1, sm["ln2_g"], sm["ln2_b"], pro=_relu2, tk=1024, name=n + "down_ln2")
    saved = dict(x0=x0, proj=proj, retg=retg, raw=raw, states=states, sb=sb, sg=sg, merged=merged, r=(r1, r2, r3),
                 x1=x1, xh1=xh1, rs1=rs1, h1=h1, xh2=xh2, rs2=rs2)
    return x2, saved


def layer_backward(l, dx2, s, W, sm, rope, rconsts, make_job=None, job_done=None):
    n = f"l{l}_"
    two = ((F32, None), (BF16, None))
    gw, gs = {}, {}
    du2, gs["ln2_g"], gs["ln2_b"] = ln_bwd(dx2, s["xh2"], s["rs2"], sm["ln2_g"], name=n + "ln2_bwd")
    gw["w_down"] = matmul(s["h1"], du2, mode="tn", tm=1024, tn=1024, tk=512, pro=_relu2, outs=two, name=n + "g_down")
    dh1 = matmul(du2, W["w_down"], mode="nt", tm=1024, tn=1024, tk=1024, outs=((BF16, None),),
                 epi=lambda acc, h: (acc * (2.0 * jnp.maximum(h, 0.0)),), tiles=(s["h1"],), name=n + "d_h1")
    gw["w_up"] = matmul(s["x1"], dh1, mode="tn", tm=1024, tn=1024, tk=512, outs=two, name=n + "g_up")
    dx1 = matmul(dh1, W["w_up"], mode="nt", tm=1024, tn=1024, tk=1024,
                 epi=lambda acc, d: (acc + ALPHA * d,), tiles=(du2,), name=n + "d_x1")
    du1, gs["ln1_g"], gs["ln1_b"] = ln_bwd(dx1, s["xh1"], s["rs1"], sm["ln1_g"], name=n + "ln1_bwd")
    gw["w_out"] = matmul(s["merged"], du1, mode="tn", tm=1024, tn=1024, tk=512, outs=two, name=n + "g_out")
    dmerged = matmul(du1, W["w_out"], mode="nt", tm=1024, tn=1024, tk=1024, name=n + "d_merged")
    dr1, dr2, dr3, dg1, dg2, dg3 = merge_bwd(dmerged, *s["r"], s["proj"], name=n + "merge_bwd")
    d_branch = {}
    for nm, a, dr in (("p_ret", s["retg"], dr1), ("p_sb", s["sb"], dr2), ("p_sgu", s["sg"], dr3)):
        gw[nm] = matmul(a, dr, mode="tn", tm=512, tn=1024, tk=512, outs=two, name=n + "g_" + nm)
        d_branch[nm] = matmul(dr, W[nm], mode="nt", tm=1024, tn=512, tk=1024, name=n + "d_" + nm)
    dret, gs["ret_gn_g"], gs["ret_gn_b"] = ret_bwd(s["proj"], *rope, rconsts, sm["ret_gn_g"], sm["ret_gn_b"], s["raw"],
                                                    s["states"], d_branch["p_ret"], name=n + "ret_bwd")
    job = make_job(gw) if make_job is not None else None
    dsq, dsk, dsv, job_out = sb_bwd(s["proj"], s["sb"], d_branch["p_sb"], name=n + "sb_bwd", job=job)
    if job is not None:
        job_done(job_out)
    dsgu, gs["sgu_w"], dbias, gs["sgu_ln_g"], gs["sgu_ln_b"] = sgu_bwd(
        s["proj"], sm["sgu_ln_g"], sm["sgu_ln_b"], sm["sgu_w"], sm["sgu_bias"], d_branch["p_sgu"], name=n + "sgu_bwd")
    gs["sgu_b"] = dbias[:, :, 0]
    dproj = jnp.concatenate([dret, dsq, dsk, dsv, dsgu, dg1, dg2, dg3], axis=1)
    gw["w_in"] = matmul(s["x0"], dproj, mode="tn", tm=1024, tn=1536, tk=512, outs=two, name=n + "g_in")
    dx0 = matmul(dproj, W["w_in"], mode="nt", tm=1024, tn=1024, tk=1536,
                 epi=lambda acc, d: (acc + ALPHA * d,), tiles=(du1,), name=n + "d_x0")
    return dx0, gw, gs


def local_step(x, target, small, plan):
    T = x.shape[0]
    rope = _rope_tables(T)
    rconsts = _ret_consts()
    sms = []
    for l in range(DEPTH):
        sm = {k: small[k][l][None, :] for k in SMALL if k not in ("sgu_w", "sgu_b")}
        sm["sgu_w"] = small["sgu_w"][l]
        sm["sgu_bias"] = jnp.broadcast_to(small["sgu_b"][l][:, :, None], (4, CHUNK, CHUNK))
        sms.append(sm)
    h, saved = x, []
    for l in range(DEPTH):
        h, s = layer_forward(l, h, plan.weights(l), sms[l], rope, rconsts, plan.fwd_job(l), plan.job_done)
        saved.append(s)
    dy, sq = loss_head(h, target)
    gs = {k: [None] * DEPTH for k in SMALL}
    for l in reversed(range(DEPTH)):
        dy, gwl, gsl = layer_backward(l, dy, saved[l], plan.weights(l), sms[l], rope, rconsts,
                                      functools.partial(plan.bwd_job, l), plan.job_done)
        plan.grads(l, gwl)
        for k in SMALL:
            gs[k][l] = gsl[k].reshape(small[k].shape[1:])
    return sq[0, 0], dy, {k: jnp.stack(v) for k, v in gs.items()}


LATE_USE = ("w_up", "w_down")
EARLY_GRADS = ("p_ret", "p_sb", "p_sgu", "w_out", "w_up", "w_down")


class _StepPlan:
    def __init__(self, shards16):
        self.shards16 = shards16
        self.full = [dict() for _ in range(DEPTH)]
        self.gw = [None] * DEPTH
        self.bufs = {}
        self.job_done(run_job(self._gather([(0, "w_in")]), name="gather_first"))

    def weights(self, l):
        return self.full[l]

    def grads(self, l, gw):
        self.gw[l] = gw

    def _gather(self, items):
        self.pending = ("gather", items)
        return gather_job([self.shards16[l][k] for l, k in items], [BIG_AXIS[k] for _, k in items])

    def _scatter(self, items):
        self.pending = ("scatter", [(l, k) for l, k, _ in items])
        return scatter_job([l for l, _, _ in items], [g[1] for _, _, g in items], [g[0] for _, _, g in items],
                           [BIG_AXIS[k] for _, k, _ in items], [self.bufs.get(k) for _, k, _ in items])

    def fwd_job(self, l):
        items = [(l, k) for k in (BIG[1:] if l == 0 else LATE_USE)]
        if l + 1 < DEPTH:
            items += [(l + 1, k) for k in BIG if k not in LATE_USE]
        return self._gather(items)

    def bwd_job(self, l, ready):
        items = [(l, k, ready[k]) for k in EARLY_GRADS]
        if l + 1 < DEPTH:
            items.append((l + 1, "w_in", self.gw[l + 1]["w_in"]))
        return self._scatter(items)

    def job_done(self, outs):
        kind, items = self.pending
        for a, (l, k) in enumerate(items):
            if kind == "gather":
                self.full[l][k] = outs[a]
            else:
                self.bufs[k] = (outs[a], outs[len(items) + a])

    def finish(self):
        self.job_done(run_job(self._scatter([(0, "w_in", self.gw[0]["w_in"])]), name="scatter_last"))
        return self.bufs


def _flat2(a):
    return a.reshape(-1, a.shape[-1])


def kernel(x, w_in, ret_gn_g, ret_gn_b, sgu_ln_g, sgu_ln_b, sgu_w, sgu_b, p_ret, p_sb, p_sgu, w_out, ln1_g, ln1_b, w_up, w_down, ln2_g, ln2_b, loss_target, m_w_in, m_ret_gn_g, m_ret_gn_b, m_sgu_ln_g, m_sgu_ln_b, m_sgu_w, m_sgu_b, m_p_ret, m_p_sb, m_p_sgu, m_w_out, m_ln1_g, m_ln1_b, m_w_up, m_w_down, m_ln2_g, m_ln2_b, v_w_in, v_ret_gn_g, v_ret_gn_b, v_sgu_ln_g, v_sgu_ln_b, v_sgu_w, v_sgu_b, v_p_ret, v_p_sb, v_p_sgu, v_w_out, v_ln1_g, v_ln1_b, v_w_up, v_w_down, v_ln2_g, v_ln2_b):
    given = dict(locals())
    order = BIG[:1] + SMALL[:6] + BIG[1:5] + SMALL[6:8] + BIG[5:7] + SMALL[8:10]
    L = DEPTH

    shards16 = [{k: _rows_call(lambda a: (a,), [(given[k], l)], [BF16], name=f"cast_{k}_{l}")[0] for k in BIG}
                for l in range(L)]
    plan = _StepPlan(shards16)
    sq, dx, gs = local_step(x[0], loss_target[0], {k: given[k] for k in SMALL}, plan)
    loss = 0.5 * lax.psum(sq, ("x", "y", "c"))

    bufs = plan.finish()
    parts = []
    for a, k in enumerate(BIG):
        o = _flat2(bufs[k][1])
        rv = bufs[k][0].reshape(3, *o.shape)
        (part,) = _rows_call(lambda o_, a_, b_, c_: (((o_ + a_.astype(F32)) + b_.astype(F32)) + c_.astype(F32),),
                             [o, (rv, 0), (rv, 1), (rv, 2)], [F32], name="chip_sum_" + k)
        parts.append(part)
    others = swap_with_sibling(parts)
    out = {}
    for a, k in enumerate(BIG):
        shp = given[k].shape
        res = _rows_call(lambda p_, q_, w_, m_, v_: (p_ + q_,) + _adamw(w_, p_ + q_, m_, v_),
                         [parts[a], others[a], _flat2(given[k]), _flat2(given["m_" + k]), _flat2(given["v_" + k])],
                         [F32] * 4, name="adamw_" + k)
        out[k] = [r.reshape(shp) for r in res]

    def pack(d, pre=""):
        return jnp.concatenate([d[pre + k].reshape(-1) for k in SMALL]).reshape(-1, 128)

    g_small = allreduce_small(pack(gs))
    res = _rows_call(lambda g_, w_, m_, v_: (g_,) + _adamw(w_, g_, m_, v_),
                     [g_small, pack(given), pack(given, "m_"), pack(given, "v_")], [F32] * 4, name="adamw_small", tr=8 * 47)
    off = 0
    for k in SMALL:
        sz = given[k].size
        out[k] = [r.reshape(-1)[off:off + sz].reshape(given[k].shape) for r in res]
        off += sz

    grads = [out[k][0] for k in order]
    deltas = [out[k][1] for k in order]
    new_m = [out[k][2] for k in order]
    new_v = [out[k][3] for k in order]
    return (loss, dx[None], *grads, *deltas, *new_m, *new_v)
```

```python
import functools
import math

import jax
import jax.numpy as jnp
from jax import lax
from jax.experimental import pallas as pl
from jax.experimental.pallas import tpu as pltpu

F32 = jnp.float32
BF16 = jnp.bfloat16

D_MODEL = 1024
SEQ = 4096
DEPTH = 2
CHUNK = 128
RET_HEADS = 4
BRANCH_W = 512
N_IN = 7680
D_FF = 4096
LN_EPS = 1e-5
ROPE_BASE = 10000.0
ALPHA = (2 * DEPTH) ** 0.25
RET_SCALE = 128 ** -0.5
SB_SCALE = 64 ** -0.5
C_RET, C_SB, C_SGU, C_GATE = 0, 2048, 3584, 4608

ADAM_LR, ADAM_B1, ADAM_B2, ADAM_EPS, ADAM_WD, ADAM_STEP = 0.001, 0.9, 0.999, 1e-08, 0.01, 10

N_CHIPS = 4
VMEM_LIMIT = 56 * 1024 * 1024
MESH = pl.DeviceIdType.MESH

NN = ((1,), (0,))
NT = ((1,), (1,))
TN = ((0,), (0,))


def _dot(a, b, dims):
    return lax.dot_general(a, b, (dims, ((), ())), preferred_element_type=F32)


def _params(sem):
    return pltpu.CompilerParams(dimension_semantics=sem, vmem_limit_bytes=VMEM_LIMIT)


def _relu2(h):
    r = jnp.maximum(h, 0.0)
    return r * r


def matmul(a, b, *, mode, tm, tn, tk, outs=((F32, None),), pro=None, epi=None, tiles=(), rows=(), name):
    if mode == "nn":
        (M, K), N = a.shape, b.shape[1]
    elif mode == "nt":
        (M, K), N = a.shape, b.shape[0]
    else:
        (K, M), N = a.shape, b.shape[1]
    tm, tn, tk = min(tm, M), min(tn, N), min(tk, K)
    assert M % tm == 0 and N % tn == 0 and K % tk == 0, (name, M, N, K, tm, tn, tk)
    if mode == "nn":
        a_spec = pl.BlockSpec((tm, tk), lambda i, j, k: (i, k))
        b_spec = pl.BlockSpec((tk, tn), lambda i, j, k: (k, j))
        dims = NN
    elif mode == "nt":
        a_spec = pl.BlockSpec((tm, tk), lambda i, j, k: (i, k))
        b_spec = pl.BlockSpec((tn, tk), lambda i, j, k: (j, k))
        dims = NT
    else:
        a_spec = pl.BlockSpec((tk, tm), lambda i, j, k: (k, i))
        b_spec = pl.BlockSpec((tk, tn), lambda i, j, k: (k, j))
        dims = TN
    nk = K // tk
    nt_, nr, no = len(tiles), len(rows), len(outs)

    def body(a_ref, b_ref, *rest):
        tile_refs = rest[:nt_]
        row_refs = rest[nt_:nt_ + nr]
        out_refs = rest[nt_ + nr:nt_ + nr + no]
        av = a_ref[...]
        if pro is not None:
            av = pro(av)
        p = _dot(av.astype(BF16), b_ref[...].astype(BF16), dims)

        def finish(acc):
            vals = (acc,) * no if epi is None else epi(acc, *[r[...] for r in tile_refs], *[r[...] for r in row_refs])
            for o_ref, v in zip(out_refs, vals):
                o_ref[...] = v.astype(o_ref.dtype)

        if nk == 1:
            finish(p)
        else:
            acc_ref = rest[-1]
            k = pl.program_id(2)

            @pl.when(k == 0)
            def _():
                acc_ref[...] = p

            @pl.when(k > 0)
            def _():
                acc_ref[...] += p

            @pl.when(k == nk - 1)
            def _():
                finish(acc_ref[...])

    out_shape, out_specs = [], []
    for dt, width in outs:
        if width is None:
            out_shape.append(jax.ShapeDtypeStruct((M, N), dt))
            out_specs.append(pl.BlockSpec((tm, tn), lambda i, j, k: (i, j)))
        else:
            assert N == tn
            out_shape.append(jax.ShapeDtypeStruct((M, width), dt))
            out_specs.append(pl.BlockSpec((tm, width), lambda i, j, k: (i, 0)))
    in_specs = [a_spec, b_spec]
    in_specs += [pl.BlockSpec((tm, tn), lambda i, j, k: (i, j)) for _ in tiles]
    in_specs += [pl.BlockSpec((1, tn), lambda i, j, k: (0, j)) for _ in rows]
    res = pl.pallas_call(
        body, name=name, grid=(M // tm, N // tn, nk),
        in_specs=in_specs, out_specs=out_specs, out_shape=out_shape,
        scratch_shapes=[pltpu.VMEM((tm, tn), F32)] if nk > 1 else [],
        compiler_params=_params(("parallel", "parallel", "arbitrary")),
    )(a, b, *tiles, *rows)
    return res[0] if no == 1 else res


def _ln_epi(acc, res, g, b):
    u = ALPHA * res + acc
    mu = jnp.mean(u, axis=-1, keepdims=True)
    xc = u - mu
    var = jnp.mean(xc * xc, axis=-1, keepdims=True)
    rstd = lax.rsqrt(var + LN_EPS)
    xhat = xc * rstd
    return xhat * g + b, xhat, jnp.broadcast_to(rstd, (u.shape[0], 128))


def matmul_ln(a, w, res, g, b, *, pro=None, tk, name):
    n = w.shape[1]
    return matmul(a, w, mode="nn", tm=512, tn=n, tk=tk, pro=pro, epi=_ln_epi, tiles=(res,), rows=(g, b),
                  outs=((F32, None), (F32, None), (F32, 128)), name=name)


def ln_bwd(dy, xhat, rstd, g, *, name):
    T, D = dy.shape
    tm = min(512, T)

    def body(dy_ref, xh_ref, rs_ref, g_ref, du_ref, dg_ref, db_ref):
        dyv, xh = dy_ref[...], xh_ref[...]
        r = rs_ref[:, 0:1]
        dxh = dyv * g_ref[...]
        m1 = jnp.mean(dxh, axis=-1, keepdims=True)
        m2 = jnp.mean(dxh * xh, axis=-1, keepdims=True)
        du_ref[...] = r * (dxh - m1 - xh * m2)

        @pl.when(pl.program_id(0) == 0)
        def _():
            dg_ref[...] = jnp.zeros_like(dg_ref)
            db_ref[...] = jnp.zeros_like(db_ref)

        dg_ref[...] += jnp.sum(dyv * xh, axis=0, keepdims=True)
        db_ref[...] += jnp.sum(dyv, axis=0, keepdims=True)

    row = pl.BlockSpec((tm, D), lambda i: (i, 0))
    vec = pl.BlockSpec((1, D), lambda i: (0, 0))
    return pl.pallas_call(
        body, name=name, grid=(T // tm,),
        in_specs=[row, row, pl.BlockSpec((tm, 128), lambda i: (i, 0)), vec],
        out_specs=[row, vec, vec],
        out_shape=[jax.ShapeDtypeStruct((T, D), F32), jax.ShapeDtypeStruct((1, D), F32), jax.ShapeDtypeStruct((1, D), F32)],
        compiler_params=_params(("arbitrary",)),
    )(dy, xhat, rstd, g)


def loss_head(y, target):
    T, D = y.shape
    tm = min(512, T)

    def body(y_ref, t_ref, dy_ref, s_ref):
        e = y_ref[...] - t_ref[...]
        dy_ref[...] = e * (1.0 / D)

        @pl.when(pl.program_id(0) == 0)
        def _():
            s_ref[...] = jnp.zeros_like(s_ref)

        s_ref[...] += jnp.sum(jnp.mean(e * e, axis=-1, keepdims=True))

    row = pl.BlockSpec((tm, D), lambda i: (i, 0))
    return pl.pallas_call(
        body, name="loss_head", grid=(T // tm,),
        in_specs=[row, row], out_specs=[row, pl.BlockSpec((8, 128), lambda i: (0, 0))],
        out_shape=[jax.ShapeDtypeStruct((T, D), F32), jax.ShapeDtypeStruct((8, 128), F32)],
        compiler_params=_params(("arbitrary",)),
    )(y, target)


def _rope_tables(T):
    half = 64
    inv_freq = ROPE_BASE ** (-jnp.arange(half, dtype=F32) / half)
    ang = jnp.arange(T, dtype=jnp.int32).astype(F32)[:, None] * inv_freq[None, :]
    cos, sin = jnp.cos(ang), jnp.sin(ang)
    return jnp.concatenate([cos, cos], axis=1), jnp.concatenate([-sin, sin], axis=1)


def _ret_consts():
    H = RET_HEADS
    log_g = jnp.log(1.0 - 2.0 ** (-5.0 - jnp.arange(H, dtype=F32)))
    idx = jnp.arange(CHUNK, dtype=F32)
    diff = idx[:, None] - idx[None, :]
    dmat = jnp.where(diff[None] >= 0, jnp.exp(log_g[:, None, None] * diff[None]), 0.0)
    kd = jnp.exp(log_g[:, None] * (CHUNK - 1 - idx)[None, :])
    qd = jnp.exp(log_g[:, None] * (idx + 1.0)[None, :])
    cd = jnp.exp(log_g * CHUNK)
    full = (H, CHUNK, CHUNK)
    return (dmat.astype(F32), jnp.broadcast_to(kd[:, :, None], full), jnp.broadcast_to(qd[:, :, None], full),
            jnp.broadcast_to(cd[:, None, None], full))


def _swap_halves(v):
    return pltpu.roll(v, 64, 1)


def _group_norm(o):
    mu = jnp.mean(o, axis=-1, keepdims=True)
    xc = o - mu
    var = jnp.mean(xc * xc, axis=-1, keepdims=True)
    rstd = lax.rsqrt(var + LN_EPS)
    return xc * rstd, rstd


def ret_fwd(proj, cosf, sinf, consts, gn_g, gn_b, *, name):
    T = proj.shape[0]
    tb = min(512, T)
    nch = tb // CHUNK
    H = RET_HEADS

    def body(p_ref, cos_ref, sin_ref, dm_ref, kd_ref, qd_ref, cd_ref, g_ref, b_ref, out_ref, raw_ref, st_ref, s_ref):
        @pl.when(pl.program_id(0) == 0)
        def _():
            s_ref[...] = jnp.zeros_like(s_ref)

        for c in range(nch):
            r = slice(c * CHUNK, (c + 1) * CHUNK)
            cs, sn = cos_ref[r, :], sin_ref[r, :]
            for h in range(H):
                hc = slice(h * 128, (h + 1) * 128)
                q = p_ref[r, h * 128:(h + 1) * 128]
                k = p_ref[r, 512 + h * 128:512 + (h + 1) * 128]
                v = p_ref[r, 1024 + h * 128:1024 + (h + 1) * 128]
                gt = p_ref[r, 1536 + h * 128:1536 + (h + 1) * 128]
                qr = q * cs + _swap_halves(q) * sn
                kr = (k * cs + _swap_halves(k) * sn) * RET_SCALE
                sprev = s_ref[h]
                st_ref[c, h] = sprev
                qb, kb, vb = qr.astype(BF16), kr.astype(BF16), v.astype(BF16)
                s = _dot(qb, kb, NT) * dm_ref[h]
                o = _dot(s.astype(BF16), vb, NN) + _dot((qr * qd_ref[h]).astype(BF16), sprev.astype(BF16), NN)
                s_ref[h] = sprev * cd_ref[h] + _dot((kr * kd_ref[h]).astype(BF16), vb, TN)
                raw_ref[r, hc] = o
                y, _ = _group_norm(o)
                out_ref[r, hc] = (gt * jax.nn.sigmoid(gt)) * (y * g_ref[:, hc] + b_ref[:, hc])

    cmat = pl.BlockSpec((H, CHUNK, CHUNK), lambda i: (0, 0, 0))
    vec = pl.BlockSpec((1, BRANCH_W), lambda i: (0, 0))
    rope = pl.BlockSpec((tb, 128), lambda i: (i, 0))
    blk = pl.BlockSpec((tb, BRANCH_W), lambda i: (i, 0))
    return pl.pallas_call(
        body, name=name, grid=(T // tb,),
        in_specs=[pl.BlockSpec((tb, 2048), lambda i: (i, 0)), rope, rope, cmat, cmat, cmat, cmat, vec, vec],
        out_specs=[blk, blk, pl.BlockSpec((nch, H, CHUNK, CHUNK), lambda i: (i, 0, 0, 0))],
        out_shape=[jax.ShapeDtypeStruct((T, BRANCH_W), F32), jax.ShapeDtypeStruct((T, BRANCH_W), F32),
                   jax.ShapeDtypeStruct((T // CHUNK, H, CHUNK, CHUNK), F32)],
        scratch_shapes=[pltpu.VMEM((H, CHUNK, CHUNK), F32)],
        compiler_params=_params(("arbitrary",)),
    )(proj, cosf, sinf, *consts, gn_g, gn_b)


def ret_bwd(proj, cosf, sinf, consts, gn_g, gn_b, raw, states, dout, *, name):
    T = proj.shape[0]
    tb = min(512, T)
    nch = tb // CHUNK
    nb = T // tb
    H = RET_HEADS

    def body(p_ref, cos_ref, sin_ref, dm_ref, kd_ref, qd_ref, cd_ref, g_ref, b_ref, raw_ref, st_ref, do_ref,
             dp_ref, dg_ref, db_ref, ds_ref):
        @pl.when(pl.program_id(0) == 0)
        def _():
            ds_ref[...] = jnp.zeros_like(ds_ref)
            dg_ref[...] = jnp.zeros_like(dg_ref)
            db_ref[...] = jnp.zeros_like(db_ref)

        for c in reversed(range(nch)):
            r = slice(c * CHUNK, (c + 1) * CHUNK)
            cs, sn = cos_ref[r, :], sin_ref[r, :]
            for h in range(H):
                hc = slice(h * 128, (h + 1) * 128)
                q = p_ref[r, h * 128:(h + 1) * 128]
                k = p_ref[r, 512 + h * 128:512 + (h + 1) * 128]
                v = p_ref[r, 1024 + h * 128:1024 + (h + 1) * 128]
                gt = p_ref[r, 1536 + h * 128:1536 + (h + 1) * 128]
                qr = q * cs + _swap_halves(q) * sn
                kr = (k * cs + _swap_halves(k) * sn) * RET_SCALE
                sprev = st_ref[c, h]
                gv = g_ref[:, hc]
                y, rstd = _group_norm(raw_ref[r, hc])
                d_out = do_ref[r, hc]
                sg = jax.nn.sigmoid(gt)
                d_gate = d_out * (y * gv + b_ref[:, hc]) * (sg * (1.0 + gt * (1.0 - sg)))
                d_aff = d_out * (gt * sg)
                dg_ref[:, hc] += jnp.sum(d_aff * y, axis=0, keepdims=True)
                db_ref[:, hc] += jnp.sum(d_aff, axis=0, keepdims=True)
                dxh = d_aff * gv
                m1 = jnp.mean(dxh, axis=-1, keepdims=True)
                m2 = jnp.mean(dxh * y, axis=-1, keepdims=True)
                d_o = (rstd * (dxh - m1 - y * m2)).astype(BF16)
                qb, kb, vb = qr.astype(BF16), kr.astype(BF16), v.astype(BF16)
                dm, kd, qd = dm_ref[h], kd_ref[h], qd_ref[h]
                p = (_dot(qb, kb, NT) * dm).astype(BF16)
                dp = (_dot(d_o, vb, NT) * dm).astype(BF16)
                dsn = ds_ref[h]
                dsb = dsn.astype(BF16)
                dq_r = _dot(dp, kb, NN) + _dot(d_o, sprev.astype(BF16), NT) * qd
                dk_r = (_dot(dp, qb, TN) + _dot(vb, dsb, NT) * kd) * RET_SCALE
                d_v = _dot(p, d_o, TN) + _dot((kr * kd).astype(BF16), dsb, NN)
                ds_ref[h] = dsn * cd_ref[h] + _dot((qr * qd).astype(BF16), d_o, TN)
                dp_ref[r, h * 128:(h + 1) * 128] = (dq_r * cs - _swap_halves(dq_r) * sn).astype(BF16)
                dp_ref[r, 512 + h * 128:512 + (h + 1) * 128] = (dk_r * cs - _swap_halves(dk_r) * sn).astype(BF16)
                dp_ref[r, 1024 + h * 128:1024 + (h + 1) * 128] = d_v.astype(BF16)
                dp_ref[r, 1536 + h * 128:1536 + (h + 1) * 128] = d_gate.astype(BF16)

    cmat = pl.BlockSpec((H, CHUNK, CHUNK), lambda i: (0, 0, 0))
    vec = pl.BlockSpec((1, BRANCH_W), lambda i: (0, 0))
    rope = pl.BlockSpec((tb, 128), lambda i: (nb - 1 - i, 0))
    blk = pl.BlockSpec((tb, BRANCH_W), lambda i: (nb - 1 - i, 0))
    wide = pl.BlockSpec((tb, 2048), lambda i: (nb - 1 - i, 0))
    return pl.pallas_call(
        body, name=name, grid=(nb,),
        in_specs=[wide, rope, rope, cmat, cmat, cmat, cmat, vec, vec, blk,
                  pl.BlockSpec((nch, H, CHUNK, CHUNK), lambda i: (nb - 1 - i, 0, 0, 0)), blk],
        out_specs=[wide, vec, vec],
        out_shape=[jax.ShapeDtypeStruct((T, 2048), BF16), jax.ShapeDtypeStruct((1, BRANCH_W), F32),
                   jax.ShapeDtypeStruct((1, BRANCH_W), F32)],
        scratch_shapes=[pltpu.VMEM((H, CHUNK, CHUNK), F32)],
        compiler_params=_params(("arbitrary",)),
    )(proj, cosf, sinf, *consts, gn_g, gn_b, raw, states, dout)


def _sb_masks():
    row = lax.broadcasted_iota(jnp.int32, (CHUNK, CHUNK), 0)
    lane = lax.broadcasted_iota(jnp.int32, (CHUNK, CHUNK), 1)
    return row, lane


SB_QT = 512
SB_DEAD = -105.0


def _pair(v):
    hi = v.astype(BF16)
    return jnp.concatenate([hi, (v - hi.astype(F32)).astype(BF16)], axis=1)


def _sb_consts():
    r = lax.broadcasted_iota(jnp.int32, (256, 256), 0) & 127
    c = lax.broadcasted_iota(jnp.int32, (256, 256), 1)
    ones = c >= 128
    lane = lax.broadcasted_iota(jnp.int32, (CHUNK, CHUNK), 1)
    return (ones | (r > c)).astype(BF16), (ones | (r >= c)).astype(BF16), (lane < 64, lane >= 64)


def _per_head(x, hms):
    return jnp.concatenate([jnp.where(hm, x, 0.0) for hm in hms], axis=0).astype(BF16)


def _sb_logits(qb, kb2, mask2):
    z = _dot(qb, kb2, NT)
    l1p = jnp.log(1.0 + jnp.exp(-jnp.abs(z)))
    lsp = jnp.minimum(z, 0.0) - l1p
    lsn = lsp - z
    if mask2 is not None:
        lsn = jnp.where(mask2, lsn, 0.0)
    return lsp, lsn


def _sb_tile_mask(qt):
    trow = lax.broadcasted_iota(jnp.int32, (qt, 256), 0)
    tlane = lax.broadcasted_iota(jnp.int32, (qt, 256), 1) & 127
    return lambda m: (tlane + m * CHUNK) < trow


def sb_fwd(proj, *, name, job=None):
    T = proj.shape[0]
    qt = min(SB_QT, T)
    nsub = qt // CHUNK
    cb = C_SB // 128

    def body(q_ref, k_ref, v_ref, o_ref):
        u_gt, _, hms = _sb_consts()
        tile_mask = _sb_tile_mask(qt)

        def qtile(i, _):
            rq = pl.ds(pl.multiple_of(i * qt, qt), qt)
            qb = (q_ref[rq, :] * SB_SCALE).astype(BF16)

            def step(j, state, mask2):
                carry, acc = list(state[:2]), state[2]
                rk = pl.ds(pl.multiple_of(j * CHUNK, CHUNK), CHUNK)
                lsp, lsn = _sb_logits(qb, _per_head(k_ref[rk, :], hms), mask2)
                a_b = []
                for h in range(2):
                    hc = slice(h * 128, (h + 1) * 128)
                    r = _dot(_pair(lsn[:, hc]), u_gt, NN)
                    a = jnp.exp(lsp[:, hc] + r[:, :128] + carry[h])
                    if mask2 is not None:
                        a = jnp.where(mask2[:, hc], a, 0.0)
                    carry[h] = carry[h] + r[:, 128:]
                    a_b.append(a.astype(BF16))
                acc = acc + _dot(jnp.concatenate(a_b, axis=1), _per_head(v_ref[rk, :], hms), NN)
                return carry[0], carry[1], acc

            zero = jnp.zeros((qt, 128), F32)
            state = (zero, zero, zero)
            for m in reversed(range(nsub)):
                state = step(i * nsub + m, state, tile_mask(m))

            def live(c):
                return jnp.logical_and(c[0] < i, jnp.maximum(jnp.max(c[1][0]), jnp.max(c[1][1])) > SB_DEAD)

            def blocks(c):
                jj, st = c
                for u in range(nsub):
                    st = step((i - jj) * nsub - 1 - u, st, None)
                return jj + 1, st

            _, state = lax.while_loop(live, blocks, (jnp.int32(0), state))
            o_ref[rq, :] = state[2]
            return 0

        lax.fori_loop(0, T // qt, qtile, 0)

    def col(off):
        return pl.BlockSpec((T, 128), lambda hp: (0, off + hp))

    steps = BRANCH_W // 128
    j = _job_args(job, 3, 1)
    res = pl.pallas_call(
        _hosting(body, job, 3, 1, 0, steps), name=name, grid=(steps,),
        in_specs=[col(cb), col(cb + 4), col(cb + 8)] + j["in_specs"], out_specs=[col(0)] + j["out_specs"],
        out_shape=[jax.ShapeDtypeStruct((T, BRANCH_W), F32)] + j["out_shape"],
        scratch_shapes=j["scratch"], input_output_aliases=j["aliases"],
        compiler_params=_params(("parallel",) if job is None else ("arbitrary",)),
    )(proj, proj, proj, *j["ins"])
    return res[0], list(res[1:])


def sb_bwd(proj, out, dout, *, name, job=None):
    T = proj.shape[0]
    qt = min(SB_QT, T)
    nsub = qt // CHUNK
    cb = C_SB // 128

    def body(q_ref, k_ref, v_ref, o_ref, do_ref, dq_ref, dk_ref, dv_ref, dkt_ref, dvt_ref):
        u_gt, u_ge, hms = _sb_consts()
        tile_mask = _sb_tile_mask(qt)
        tall_lane = lax.broadcasted_iota(jnp.int32, (qt, 128), 1)
        top = lax.broadcasted_iota(jnp.int32, (CHUNK, CHUNK), 0) < 64
        dkt_ref[...] = jnp.zeros_like(dkt_ref)
        dvt_ref[...] = jnp.zeros_like(dvt_ref)

        def qtile(i, _):
            rq = pl.ds(pl.multiple_of(i * qt, qt), qt)
            qs = q_ref[rq, :] * SB_SCALE
            qb, q_t = qs.astype(BF16), qs.T.astype(BF16)
            dov = do_ref[rq, :]
            dob, do_t = dov.astype(BF16), dov.T.astype(BF16)
            prod = dob.astype(F32) * o_ref[rq, :]
            total = [jnp.broadcast_to(jnp.sum(jnp.where(hm, prod, 0.0), axis=1, keepdims=True), (qt, 128))
                     for hm in (tall_lane < 64, tall_lane >= 64)]

            def step(j, state, mask2):
                c_l, c_w, dq = list(state[:2]), list(state[2:4]), state[4]
                rk = pl.ds(pl.multiple_of(j * CHUNK, CHUNK), CHUNK)
                kb2, vb2 = _per_head(k_ref[rk, :], hms), _per_head(v_ref[rk, :], hms)
                lsp, lsn = _sb_logits(qb, kb2, mask2)
                da = _dot(dob, vb2, NT)
                sp = jnp.exp(lsp)
                a_b, dz_b = [], []
                for h in range(2):
                    hc = slice(h * 128, (h + 1) * 128)
                    r = _dot(_pair(lsn[:, hc]), u_gt, NN)
                    a = jnp.exp(lsp[:, hc] + r[:, :128] + c_l[h])
                    if mask2 is not None:
                        a = jnp.where(mask2[:, hc], a, 0.0)
                    c_l[h] = c_l[h] + r[:, 128:]
                    a = a.astype(BF16)
                    w = a.astype(F32) * da[:, hc]
                    r = _dot(_pair(w), u_ge, NN)
                    later_w = r[:, :128] + c_w[h]
                    c_w[h] = c_w[h] + r[:, 128:]
                    dz = w * (1.0 - sp[:, hc]) - sp[:, hc] * (total[h] - later_w)
                    if mask2 is not None:
                        dz = jnp.where(mask2[:, hc], dz, 0.0)
                    a_b.append(a)
                    dz_b.append(dz.astype(BF16))
                a_b, dz_b = jnp.concatenate(a_b, axis=1), jnp.concatenate(dz_b, axis=1)
                dkt = _dot(q_t, dz_b, NN)
                dvt = _dot(do_t, a_b, NN)
                dkt_ref[j] += jnp.where(top, dkt[:, :128], dkt[:, 128:])
                dvt_ref[j] += jnp.where(top, dvt[:, :128], dvt[:, 128:])
                return c_l[0], c_l[1], c_w[0], c_w[1], dq + _dot(dz_b, kb2, NN)

            zero = jnp.zeros((qt, 128), F32)
            state = (zero,) * 5
            for m in reversed(range(nsub)):
                state = step(i * nsub + m, state, tile_mask(m))

            def live(c):
                return jnp.logical_and(c[0] < i, jnp.maximum(jnp.max(c[1][0]), jnp.max(c[1][1])) > SB_DEAD)

            def blocks(c):
                jj, st = c
                for u in range(nsub):
                    st = step((i - jj) * nsub - 1 - u, st, None)
                return jj + 1, st

            _, state = lax.while_loop(live, blocks, (jnp.int32(0), state))
            dq_ref[rq, :] = (state[4] * SB_SCALE).astype(BF16)
            return 0

        lax.fori_loop(0, T // qt, qtile, 0)

        def untranspose(jb, _):
            rk = pl.ds(pl.multiple_of(jb * CHUNK, CHUNK), CHUNK)
            dk_ref[rk, :] = dkt_ref[jb].T.astype(BF16)
            dv_ref[rk, :] = dvt_ref[jb].T.astype(BF16)
            return 0

        lax.fori_loop(0, T // CHUNK, untranspose, 0)

    def col(off):
        return pl.BlockSpec((T, 128), lambda hp: (0, off + hp))

    o16 = jax.ShapeDtypeStruct((T, BRANCH_W), BF16)
    steps = BRANCH_W // 128
    j = _job_args(job, 5, 3)
    acc = pltpu.VMEM((T // CHUNK, CHUNK, CHUNK), F32)
    res = pl.pallas_call(
        _hosting(body, job, 5, 3, 2, steps), name=name, grid=(steps,),
        in_specs=[col(cb), col(cb + 4), col(cb + 8), col(0), col(0)] + j["in_specs"],
        out_specs=[col(0), col(0), col(0)] + j["out_specs"], out_shape=[o16, o16, o16] + j["out_shape"],
        scratch_shapes=[acc, acc] + j["scratch"], input_output_aliases=j["aliases"],
        compiler_params=_params(("parallel",) if job is None else ("arbitrary",)),
    )(proj, proj, proj, out, dout, *j["ins"])
    return res[0], res[1], res[2], list(res[3:])


_G0 = math.sqrt(2.0 / math.pi)
_G1 = 0.044715


def _gelu(x):
    return 0.5 * x * (1.0 + jnp.tanh(_G0 * (x + _G1 * x * x * x)))


def _gelu_grad(x):
    t = jnp.tanh(_G0 * (x + _G1 * x * x * x))
    return 0.5 * (1.0 + t) + 0.5 * x * (1.0 - t * t) * (_G0 * (1.0 + 3.0 * _G1 * x * x))


def _tril():
    row, lane = _sb_masks()
    return row >= lane


def sgu_fwd(proj, ln_g, ln_b, w, bias, *, name):
    T = proj.shape[0]
    tb = min(512, T)
    G = BRANCH_W // 128

    def body(u_ref, v_ref, g_ref, b_ref, w_ref, bias_ref, o_ref):
        vv = _gelu(v_ref[...])
        xh, _ = _group_norm(vv)
        vn = (xh * g_ref[...] + b_ref[...]).astype(BF16)
        tril = _tril()
        for g in range(G):
            wg = jnp.where(tril, w_ref[g], 0.0).astype(BF16)
            gc = slice(g * 128, (g + 1) * 128)
            for c in range(tb // CHUNK):
                r = slice(c * CHUNK, (c + 1) * CHUNK)
                sv = _dot(wg, vn[r, gc], NN) + bias_ref[g]
                o_ref[r, gc] = _gelu(u_ref[r, gc]) * sv

    cu, cv = C_SGU // BRANCH_W, C_SGU // BRANCH_W + 1
    vec = pl.BlockSpec((1, BRANCH_W), lambda i: (0, 0))
    mat = pl.BlockSpec((G, CHUNK, CHUNK), lambda i: (0, 0, 0))
    return pl.pallas_call(
        body, name=name, grid=(T // tb,),
        in_specs=[pl.BlockSpec((tb, BRANCH_W), lambda i: (i, cu)), pl.BlockSpec((tb, BRANCH_W), lambda i: (i, cv)),
                  vec, vec, mat, mat],
        out_specs=pl.BlockSpec((tb, BRANCH_W), lambda i: (i, 0)),
        out_shape=jax.ShapeDtypeStruct((T, BRANCH_W), F32),
        compiler_params=_params(("parallel",)),
    )(proj, proj, ln_g, ln_b, w, bias)


def sgu_bwd(proj, ln_g, ln_b, w, bias, dout, *, name):
    T = proj.shape[0]
    tb = min(512, T)
    G = BRANCH_W // 128

    def body(u_ref, v_ref, g_ref, b_ref, w_ref, bias_ref, do_ref, dp_ref, dw_ref, dbias_ref, dg_ref, db_ref, dvn_ref):
        @pl.when(pl.program_id(0) == 0)
        def _():
            dw_ref[...] = jnp.zeros_like(dw_ref)
            dbias_ref[...] = jnp.zeros_like(dbias_ref)
            dg_ref[...] = jnp.zeros_like(dg_ref)
            db_ref[...] = jnp.zeros_like(db_ref)

        gv = v_ref[...]
        vv = _gelu(gv)
        xh, rstd = _group_norm(vv)
        vn = (xh * g_ref[...] + b_ref[...]).astype(BF16)
        tril = _tril()
        for g in range(G):
            wg = jnp.where(tril, w_ref[g], 0.0).astype(BF16)
            gc = slice(g * 128, (g + 1) * 128)
            for c in range(tb // CHUNK):
                r = slice(c * CHUNK, (c + 1) * CHUNK)
                vn_c = vn[r, gc]
                sv = _dot(wg, vn_c, NN) + bias_ref[g]
                gu = u_ref[r, gc]
                d_o = do_ref[r, gc]
                dp_ref[r, gc] = (d_o * sv * _gelu_grad(gu)).astype(BF16)
                dsv = d_o * _gelu(gu)
                dsv_b = dsv.astype(BF16)
                dvn_ref[r, gc] = _dot(wg, dsv_b, TN)
                dw_ref[g] += jnp.where(tril, _dot(dsv_b, vn_c, NT), 0.0)
                dbias_ref[g] += jnp.broadcast_to(jnp.sum(dsv, axis=1, keepdims=True), (CHUNK, CHUNK))
        dvn = dvn_ref[...]
        dg_ref[...] += jnp.sum(dvn * xh, axis=0, keepdims=True)
        db_ref[...] += jnp.sum(dvn, axis=0, keepdims=True)
        dxh = dvn * g_ref[...]
        m1 = jnp.mean(dxh, axis=-1, keepdims=True)
        m2 = jnp.mean(dxh * xh, axis=-1, keepdims=True)
        dp_ref[:, BRANCH_W:2 * BRANCH_W] = (rstd * (dxh - m1 - xh * m2) * _gelu_grad(gv)).astype(BF16)

    cu, cv = C_SGU // BRANCH_W, C_SGU // BRANCH_W + 1
    vec = pl.BlockSpec((1, BRANCH_W), lambda i: (0, 0))
    mat = pl.BlockSpec((G, CHUNK, CHUNK), lambda i: (0, 0, 0))
    blk = pl.BlockSpec((tb, BRANCH_W), lambda i: (i, 0))
    msh = jax.ShapeDtypeStruct((G, CHUNK, CHUNK), F32)
    vsh = jax.ShapeDtypeStruct((1, BRANCH_W), F32)
    return pl.pallas_call(
        body, name=name, grid=(T // tb,),
        in_specs=[pl.BlockSpec((tb, BRANCH_W), lambda i: (i, cu)), pl.BlockSpec((tb, BRANCH_W), lambda i: (i, cv)),
                  vec, vec, mat, mat, blk],
        out_specs=[pl.BlockSpec((tb, 2 * BRANCH_W), lambda i: (i, 0)), mat, mat, vec, vec],
        out_shape=[jax.ShapeDtypeStruct((T, 2 * BRANCH_W), BF16), msh, msh, vsh, vsh],
        scratch_shapes=[pltpu.VMEM((tb, BRANCH_W), F32)],
        compiler_params=_params(("arbitrary",)),
    )(proj, proj, ln_g, ln_b, w, bias, dout)


def merge_fwd(a1, a2, a3, p1, p2, p3, proj, *, name):
    T = a1.shape[0]
    tm, tn = min(1024, T), 512
    gb = C_GATE // tn

    def body(a1_ref, a2_ref, a3_ref, p1_ref, p2_ref, p3_ref, g1_ref, g2_ref, g3_ref, m_ref, r1_ref, r2_ref, r3_ref):
        m = None
        for a_ref, p_ref, g_ref, r_ref in ((a1_ref, p1_ref, g1_ref, r1_ref), (a2_ref, p2_ref, g2_ref, r2_ref),
                                           (a3_ref, p3_ref, g3_ref, r3_ref)):
            r = _dot(a_ref[...].astype(BF16), p_ref[...], NN)
            r_ref[...] = r
            t = jax.nn.sigmoid(g_ref[...]) * r
            m = t if m is None else m + t
        m_ref[...] = m

    a_spec = pl.BlockSpec((tm, BRANCH_W), lambda i, j: (i, 0))
    p_spec = pl.BlockSpec((BRANCH_W, tn), lambda i, j: (0, j))
    o_spec = pl.BlockSpec((tm, tn), lambda i, j: (i, j))
    osh = jax.ShapeDtypeStruct((T, D_MODEL), F32)
    gates = [pl.BlockSpec((tm, tn), functools.partial(lambda i, j, o: (i, o + j), o=gb + 2 * n)) for n in range(3)]
    return pl.pallas_call(
        body, name=name, grid=(T // tm, D_MODEL // tn),
        in_specs=[a_spec, a_spec, a_spec, p_spec, p_spec, p_spec, *gates],
        out_specs=[o_spec] * 4, out_shape=[osh] * 4,
        compiler_params=_params(("parallel", "parallel")),
    )(a1, a2, a3, p1, p2, p3, proj, proj, proj)


def merge_bwd(dm, r1, r2, r3, proj, *, name):
    T = dm.shape[0]
    tm, tn = min(512, T), 512
    gb = C_GATE // tn

    def body(dm_ref, r1_ref, r2_ref, r3_ref, g1_ref, g2_ref, g3_ref, dr1_ref, dr2_ref, dr3_ref, dg1_ref, dg2_ref, dg3_ref):
        d = dm_ref[...]
        for r_ref, g_ref, dr_ref, dg_ref in ((r1_ref, g1_ref, dr1_ref, dg1_ref), (r2_ref, g2_ref, dr2_ref, dg2_ref),
                                             (r3_ref, g3_ref, dr3_ref, dg3_ref)):
            s = jax.nn.sigmoid(g_ref[...])
            dr_ref[...] = (d * s).astype(BF16)
            dg_ref[...] = (d * r_ref[...] * (s * (1.0 - s))).astype(BF16)

    o_spec = pl.BlockSpec((tm, tn), lambda i, j: (i, j))
    osh = jax.ShapeDtypeStruct((T, D_MODEL), BF16)
    gates = [pl.BlockSpec((tm, tn), functools.partial(lambda i, j, o: (i, o + j), o=gb + 2 * n)) for n in range(3)]
    return pl.pallas_call(
        body, name=name, grid=(T // tm, D_MODEL // tn),
        in_specs=[o_spec] * 4 + gates, out_specs=[o_spec] * 6, out_shape=[osh] * 6,
        compiler_params=_params(("parallel", "parallel")),
    )(dm, r1, r2, r3, proj, proj, proj)


def _rows_call(fn, ins, out_dtypes, *, name, tr=256):
    first = ins[0][0] if isinstance(ins[0], tuple) else ins[0]
    R, C = first.shape[-2:]
    tr = min(tr, R)
    assert R % tr == 0, (name, R, tr)
    arrs, specs = [], []
    for x in ins:
        if isinstance(x, tuple):
            arrs.append(x[0])
            specs.append(pl.BlockSpec((None, tr, C), functools.partial(lambda i, n: (n, i, 0), n=x[1])))
        else:
            arrs.append(x)
            specs.append(pl.BlockSpec((tr, C), lambda i: (i, 0)))
    ni = len(arrs)

    def body(*refs):
        vals = fn(*[r[...] for r in refs[:ni]])
        for o_ref, v in zip(refs[ni:], vals):
            o_ref[...] = v.astype(o_ref.dtype)

    res = pl.pallas_call(
        body, name=name, grid=(R // tr,), in_specs=specs,
        out_specs=[pl.BlockSpec((tr, C), lambda i: (i, 0)) for _ in out_dtypes],
        out_shape=[jax.ShapeDtypeStruct((R, C), dt) for dt in out_dtypes],
        compiler_params=_params(("parallel",)),
    )(*arrs)
    return res


def _adamw(w, g, m, v):
    m2 = ADAM_B1 * m + (1.0 - ADAM_B1) * g
    v2 = ADAM_B2 * v + (1.0 - ADAM_B2) * (g * g)
    m_hat = m2 / (1.0 - ADAM_B1 ** ADAM_STEP)
    v_hat = v2 / (1.0 - ADAM_B2 ** ADAM_STEP)
    delta = -ADAM_LR * (m_hat / (jnp.sqrt(v_hat) + ADAM_EPS) + ADAM_WD * w)
    return delta, m2, v2


def _place():
    return lax.axis_index("x"), lax.axis_index("y"), lax.axis_index("c")


def _chip_peers(x, y, c):
    return [((1 - x, y, c), 2 * (1 - x) + y), ((x, 1 - y, c), 2 * x + 1 - y), ((1 - x, 1 - y, c), 2 * (1 - x) + 1 - y)]


def _shard_of(ref, axis, k, n):
    start = pl.multiple_of(k * n, 128)
    return ref.at[pl.ds(start, n), :] if axis == 0 else ref.at[:, pl.ds(start, n)]


ANY = pl.BlockSpec(memory_space=pl.ANY)


class CopyJob:
    def __init__(self, ins, out_shape, scratch, copies, aliases=None):
        self.ins, self.out_shape, self.scratch, self.copies = list(ins), list(out_shape), list(scratch), copies
        self.aliases = dict(aliases or {})

    def start(self, ins, outs, sems):
        local, remote, _ = self.copies(ins, outs, sems)
        for d in local + remote:
            d.start()

    def finish(self, ins, outs, sems):
        local, remote, arrivals = self.copies(ins, outs, sems)
        for d in arrivals:
            d.wait_recv()
        for d in remote:
            d.wait_send()
        for d in local:
            d.wait()


def run_job(job, *, name):
    ni, no = len(job.ins), len(job.out_shape)

    def body(*refs):
        parts = refs[:ni], refs[ni:ni + no], refs[ni + no:]
        job.start(*parts)
        job.finish(*parts)

    return pl.pallas_call(
        body, name=name, in_specs=[ANY] * ni, out_specs=[ANY] * no, out_shape=job.out_shape,
        scratch_shapes=job.scratch, input_output_aliases=job.aliases,
    )(*job.ins)


def _job_args(job, n_in, n_out):
    if job is None:
        return dict(ins=[], in_specs=[], out_specs=[], out_shape=[], scratch=[], aliases={})
    return dict(ins=job.ins, in_specs=[ANY] * len(job.ins), out_specs=[ANY] * len(job.out_shape),
                out_shape=job.out_shape, scratch=job.scratch,
                aliases={n_in + i: n_out + o for i, o in job.aliases.items()})


def _hosting(body, job, n_in, n_out, n_scratch, steps):
    if job is None:
        return body
    ji, jo = len(job.ins), len(job.out_shape)

    def hosted(*refs):
        o = n_in + ji
        s = o + n_out + jo
        parts = refs[n_in:o], refs[o + n_out:s], refs[s + n_scratch:]

        @pl.when(pl.program_id(0) == 0)
        def _():
            job.start(*parts)

        body(*refs[:n_in], *refs[o:o + n_out], *refs[s:s + n_scratch])

        @pl.when(pl.program_id(0) == steps - 1)
        def _():
            job.finish(*parts)

    return hosted


def _job_sems(n_remote, n_local):
    return [pltpu.SemaphoreType.DMA((n_remote,)), pltpu.SemaphoreType.DMA((n_remote,)), pltpu.SemaphoreType.DMA((n_local,))]


def gather_job(shards, axes):
    na = len(shards)

    def copies(ins, outs, sems):
        send, recv, loc = sems
        x, y, c = _place()
        k = 2 * x + y
        local, remote, arrivals = [], [], []
        for a in range(na):
            n = ins[a].shape[axes[a]]
            mine = _shard_of(outs[a], axes[a], k, n)
            local.append(pltpu.make_async_copy(ins[a], mine, loc.at[a]))
            for r, (peer, kp) in enumerate(_chip_peers(x, y, c)):
                s = 3 * a + r
                remote.append(pltpu.make_async_remote_copy(ins[a], mine, send.at[s], recv.at[s],
                                                           device_id=peer, device_id_type=MESH))
                arrivals.append(pltpu.make_async_remote_copy(ins[a], _shard_of(outs[a], axes[a], kp, n), send.at[s],
                                                             recv.at[s], device_id=peer, device_id_type=MESH))
        return local, remote, arrivals

    out_shape = []
    for a in range(na):
        r, c = shards[a].shape
        out_shape.append(jax.ShapeDtypeStruct((r * N_CHIPS, c) if axes[a] == 0 else (r, c * N_CHIPS), BF16))
    return CopyJob(shards, out_shape, _job_sems(3 * na, na), copies)


def scatter_job(layers, g16, g32, axes, filled):
    na = len(axes)

    def shard_shape(a):
        r, c = g32[a].shape
        return (r // N_CHIPS, c) if axes[a] == 0 else (r, c // N_CHIPS)

    def copies(ins, outs, sems):
        send, recv_sems, loc = sems
        b16, b32 = ins[:na], ins[na:2 * na]
        recv, own = outs[:na], outs[na:]
        x, y, c = _place()
        k = 2 * x + y
        local, remote = [], []
        for a in range(na):
            n = shard_shape(a)[axes[a]]
            local.append(pltpu.make_async_copy(_shard_of(b32[a], axes[a], k, n), own[a].at[layers[a]], loc.at[a]))
            for r, (peer, kp) in enumerate(_chip_peers(x, y, c)):
                remote.append(pltpu.make_async_remote_copy(_shard_of(b16[a], axes[a], kp, n), recv[a].at[r, layers[a]],
                                                           send.at[3 * a + r], recv_sems.at[3 * a + r],
                                                           device_id=peer, device_id_type=MESH))
        return local, remote, remote

    out_shape = [jax.ShapeDtypeStruct((3, DEPTH) + shard_shape(a), BF16) for a in range(na)]
    out_shape += [jax.ShapeDtypeStruct((DEPTH,) + shard_shape(a), F32) for a in range(na)]
    ins = list(g16) + list(g32)
    aliases = {}
    for a in range(na):
        if filled[a] is not None:
            aliases[len(ins)] = a
            aliases[len(ins) + 1] = na + a
            ins += list(filled[a])
    return CopyJob(ins, out_shape, _job_sems(3 * na, na), copies, aliases)


def swap_with_sibling(parts):
    na = len(parts)

    def body(*refs):
        ins, outs = refs[:na], refs[na:2 * na]
        send_sems, recv_sems = refs[2 * na:]
        x, y, c = _place()
        cps = [pltpu.make_async_remote_copy(ins[a], outs[a], send_sems.at[a], recv_sems.at[a],
                                            device_id=(x, y, 1 - c), device_id_type=MESH) for a in range(na)]
        for cp in cps:
            cp.start()
        for cp in cps:
            cp.wait()

    return pl.pallas_call(
        body, name="swap_with_sibling", in_specs=[ANY] * na, out_specs=[ANY] * na,
        out_shape=[jax.ShapeDtypeStruct(p.shape, p.dtype) for p in parts],
        scratch_shapes=[pltpu.SemaphoreType.DMA((na,)), pltpu.SemaphoreType.DMA((na,))],
    )(*parts)


def allreduce_small(p):
    R = p.shape[0]

    def body(p_ref, o_ref, buf, send_sems, recv_sems):
        x, y, c = _place()
        me = 4 * x + 2 * y + c
        cps = []
        for rel in range(1, 8):
            dx, dy, dc = rel >> 2, (rel >> 1) & 1, rel & 1
            peer = (1 - x if dx else x, 1 - y if dy else y, 1 - c if dc else c)
            cp = pltpu.make_async_remote_copy(p_ref, buf.at[me], send_sems.at[rel - 1], recv_sems.at[rel - 1],
                                              device_id=peer, device_id_type=MESH)
            cp.start()
            cps.append((cp, 4 * peer[0] + 2 * peer[1] + peer[2]))
        buf[me] = p_ref[...]
        for rel, (cp, who) in enumerate(cps):
            pltpu.make_async_remote_copy(p_ref, buf.at[who], send_sems.at[rel], recv_sems.at[rel],
                                         device_id=(x, y, c), device_id_type=MESH).wait_recv()
        acc = buf[0]
        for d in range(1, 8):
            acc = acc + buf[d]
        o_ref[...] = acc
        for cp, _ in cps:
            cp.wait_send()

    return pl.pallas_call(
        body, name="allreduce_small",
        in_specs=[pl.BlockSpec(memory_space=pltpu.VMEM)], out_specs=pl.BlockSpec(memory_space=pltpu.VMEM),
        out_shape=jax.ShapeDtypeStruct((R, 128), F32),
        scratch_shapes=[pltpu.VMEM((8, R, 128), F32), pltpu.SemaphoreType.DMA((7,)), pltpu.SemaphoreType.DMA((7,))],
        compiler_params=pltpu.CompilerParams(vmem_limit_bytes=VMEM_LIMIT),
    )(p)


BIG = ("w_in", "p_ret", "p_sb", "p_sgu", "w_out", "w_up", "w_down")
BIG_AXIS = {"w_in": 1, "p_ret": 1, "p_sb": 1, "p_sgu": 1, "w_out": 0, "w_up": 1, "w_down": 0}
SMALL = ("ret_gn_g", "ret_gn_b", "sgu_ln_g", "sgu_ln_b", "sgu_w", "sgu_b", "ln1_g", "ln1_b", "ln2_g", "ln2_b")


def layer_forward(l, x0, W, sm, rope, rconsts, job=None, job_done=None):
    n = f"l{l}_"
    proj = matmul(x0, W["w_in"], mode="nn", tm=1024, tn=640, tk=1024, name=n + "proj")
    retg, raw, states = ret_fwd(proj, *rope, rconsts, sm["ret_gn_g"], sm["ret_gn_b"], name=n + "ret_fwd")
    sb, job_out = sb_fwd(proj, name=n + "sb_fwd", job=job)
    if job is not None:
        job_done(job_out)
    sg = sgu_fwd(proj, sm["sgu_ln_g"], sm["sgu_ln_b"], sm["sgu_w"], sm["sgu_bias"], name=n + "sgu_fwd")
    merged, r1, r2, r3 = merge_fwd(retg, sb, sg, W["p_ret"], W["p_sb"], W["p_sgu"], proj, name=n + "merge_fwd")
    x1, xh1, rs1 = matmul_ln(merged, W["w_out"], x0, sm["ln1_g"], sm["ln1_b"], tk=1024, name=n + "out_ln1")
    h1 = matmul(x1, W["w_up"], mode="nn", tm=1024, tn=1024, tk=1024, name=n + "up")
    x2, xh2, rs2 = matmul_ln(h1, W["w_down"], x1, sm["ln2_g"], sm["ln2_b"], pro=_relu2, tk=1024, name=n + "down_ln2")
    saved = dict(x0=x0, proj=proj, retg=retg, raw=raw, states=states, sb=sb, sg=sg, merged=merged, r=(r1, r2, r3),
                 x1=x1, xh1=xh1, rs1=rs1, h1=h1, xh2=xh2, rs2=rs2)
    return x2, saved


def layer_backward(l, dx2, s, W, sm, rope, rconsts, make_job=None, job_done=None):
    n = f"l{l}_"
    two = ((F32, None), (BF16, None))
    gw, gs = {}, {}
    du2, gs["ln2_g"], gs["ln2_b"] = ln_bwd(dx2, s["xh2"], s["rs2"], sm["ln2_g"], name=n + "ln2_bwd")
    gw["w_down"] = matmul(s["h1"], du2, mode="tn", tm=1024, tn=1024, tk=512, pro=_relu2, outs=two, name=n + "g_down")
    dh1 = matmul(du2, W["w_down"], mode="nt", tm=1024, tn=1024, tk=1024, outs=((BF16, None),),
                 epi=lambda acc, h: (acc * (2.0 * jnp.maximum(h, 0.0)),), tiles=(s["h1"],), name=n + "d_h1")
    gw["w_up"] = matmul(s["x1"], dh1, mode="tn", tm=1024, tn=1024, tk=512, outs=two, name=n + "g_up")
    dx1 = matmul(dh1, W["w_up"], mode="nt", tm=1024, tn=1024, tk=1024,
                 epi=lambda acc, d: (acc + ALPHA * d,), tiles=(du2,), name=n + "d_x1")
    du1, gs["ln1_g"], gs["ln1_b"] = ln_bwd(dx1, s["xh1"], s["rs1"], sm["ln1_g"], name=n + "ln1_bwd")
    gw["w_out"] = matmul(s["merged"], du1, mode="tn", tm=1024, tn=1024, tk=512, outs=two, name=n + "g_out")
    dmerged = matmul(du1, W["w_out"], mode="nt", tm=1024, tn=1024, tk=1024, name=n + "d_merged")
    dr1, dr2, dr3, dg1, dg2, dg3 = merge_bwd(dmerged, *s["r"], s["proj"], name=n + "merge_bwd")
    d_branch = {}
    for nm, a, dr in (("p_ret", s["retg"], dr1), ("p_sb", s["sb"], dr2), ("p_sgu", s["sg"], dr3)):
        gw[nm] = matmul(a, dr, mode="tn", tm=512, tn=1024, tk=512, outs=two, name=n + "g_" + nm)
        d_branch[nm] = matmul(dr, W[nm], mode="nt", tm=1024, tn=512, tk=1024, name=n + "d_" + nm)
    dret, gs["ret_gn_g"], gs["ret_gn_b"] = ret_bwd(s["proj"], *rope, rconsts, sm["ret_gn_g"], sm["ret_gn_b"], s["raw"],
                                                    s["states"], d_branch["p_ret"], name=n + "ret_bwd")
    job = make_job(gw) if make_job is not None else None
    dsq, dsk, dsv, job_out = sb_bwd(s["proj"], s["sb"], d_branch["p_sb"], name=n + "sb_bwd", job=job)
    if job is not None:
        job_done(job_out)
    dsgu, gs["sgu_w"], dbias, gs["sgu_ln_g"], gs["sgu_ln_b"] = sgu_bwd(
        s["proj"], sm["sgu_ln_g"], sm["sgu_ln_b"], sm["sgu_w"], sm["sgu_bias"], d_branch["p_sgu"], name=n + "sgu_bwd")
    gs["sgu_b"] = dbias[:, :, 0]
    dproj = jnp.concatenate([dret, dsq, dsk, dsv, dsgu, dg1, dg2, dg3], axis=1)
    gw["w_in"] = matmul(s["x0"], dproj, mode="tn", tm=1024, tn=1536, tk=512, outs=two, name=n + "g_in")
    dx0 = matmul(dproj, W["w_in"], mode="nt", tm=1024, tn=1024, tk=1536,
                 epi=lambda acc, d: (acc + ALPHA * d,), tiles=(du1,), name=n + "d_x0")
    return dx0, gw, gs


def local_step(x, target, small, plan):
    T = x.shape[0]
    rope = _rope_tables(T)
    rconsts = _ret_consts()
    sms = []
    for l in range(DEPTH):
        sm = {k: small[k][l][None, :] for k in SMALL if k not in ("sgu_w", "sgu_b")}
        sm["sgu_w"] = small["sgu_w"][l]
        sm["sgu_bias"] = jnp.broadcast_to(small["sgu_b"][l][:, :, None], (4, CHUNK, CHUNK))
        sms.append(sm)
    h, saved = x, []
    for l in range(DEPTH):
        h, s = layer_forward(l, h, plan.weights(l), sms[l], rope, rconsts, plan.fwd_job(l), plan.job_done)
        saved.append(s)
    dy, sq = loss_head(h, target)
    gs = {k: [None] * DEPTH for k in SMALL}
    for l in reversed(range(DEPTH)):
        dy, gwl, gsl = layer_backward(l, dy, saved[l], plan.weights(l), sms[l], rope, rconsts,
                                      functools.partial(plan.bwd_job, l), plan.job_done)
        plan.grads(l, gwl)
        for k in SMALL:
            gs[k][l] = gsl[k].reshape(small[k].shape[1:])
    return sq[0, 0], dy, {k: jnp.stack(v) for k, v in gs.items()}


LATE_USE = ("w_up", "w_down")
EARLY_GRADS = ("p_ret", "p_sb", "p_sgu", "w_out", "w_up", "w_down")


class _StepPlan:
    def __init__(self, shards16):
        self.shards16 = shards16
        self.full = [dict() for _ in range(DEPTH)]
        self.gw = [None] * DEPTH
        self.bufs = {}
        self.job_done(run_job(self._gather([(0, "w_in")]), name="gather_first"))

    def weights(self, l):
        return self.full[l]

    def grads(self, l, gw):
        self.gw[l] = gw

    def _gather(self, items):
        self.pending = ("gather", items)
        return gather_job([self.shards16[l][k] for l, k in items], [BIG_AXIS[k] for _, k in items])

    def _scatter(self, items):
        self.pending = ("scatter", [(l, k) for l, k, _ in items])
        return scatter_job([l for l, _, _ in items], [g[1] for _, _, g in items], [g[0] for _, _, g in items],
                           [BIG_AXIS[k] for _, k, _ in items], [self.bufs.get(k) for _, k, _ in items])

    def fwd_job(self, l):
        items = [(l, k) for k in (BIG[1:] if l == 0 else LATE_USE)]
        if l + 1 < DEPTH:
            items += [(l + 1, k) for k in BIG if k not in LATE_USE]
        return self._gather(items)

    def bwd_job(self, l, ready):
        items = [(l, k, ready[k]) for k in EARLY_GRADS]
        if l + 1 < DEPTH:
            items.append((l + 1, "w_in", self.gw[l + 1]["w_in"]))
        return self._scatter(items)

    def job_done(self, outs):
        kind, items = self.pending
        for a, (l, k) in enumerate(items):
            if kind == "gather":
                self.full[l][k] = outs[a]
            else:
                self.bufs[k] = (outs[a], outs[len(items) + a])

    def finish(self):
        self.job_done(run_job(self._scatter([(0, "w_in", self.gw[0]["w_in"])]), name="scatter_last"))
        return self.bufs


def _flat2(a):
    return a.reshape(-1, a.shape[-1])


def kernel(x, w_in, ret_gn_g, ret_gn_b, sgu_ln_g, sgu_ln_b, sgu_w, sgu_b, p_ret, p_sb, p_sgu, w_out, ln1_g, ln1_b, w_up, w_down, ln2_g, ln2_b, loss_target, m_w_in, m_ret_gn_g, m_ret_gn_b, m_sgu_ln_g, m_sgu_ln_b, m_sgu_w, m_sgu_b, m_p_ret, m_p_sb, m_p_sgu, m_w_out, m_ln1_g, m_ln1_b, m_w_up, m_w_down, m_ln2_g, m_ln2_b, v_w_in, v_ret_gn_g, v_ret_gn_b, v_sgu_ln_g, v_sgu_ln_b, v_sgu_w, v_sgu_b, v_p_ret, v_p_sb, v_p_sgu, v_w_out, v_ln1_g, v_ln1_b, v_w_up, v_w_down, v_ln2_g, v_ln2_b):
    given = dict(locals())
    order = BIG[:1] + SMALL[:6] + BIG[1:5] + SMALL[6:8] + BIG[5:7] + SMALL[8:10]
    L = DEPTH

    shards16 = [{k: _rows_call(lambda a: (a,), [(given[k], l)], [BF16], name=f"cast_{k}_{l}")[0] for k in BIG}
                for l in range(L)]
    plan = _StepPlan(shards16)
    sq, dx, gs = local_step(x[0], loss_target[0], {k: given[k] for k in SMALL}, plan)
    loss = 0.5 * lax.psum(sq, ("x", "y", "c"))

    bufs = plan.finish()
    parts = []
    for a, k in enumerate(BIG):
        o = _flat2(bufs[k][1])
        rv = bufs[k][0].reshape(3, *o.shape)
        (part,) = _rows_call(lambda o_, a_, b_, c_: (((o_ + a_.astype(F32)) + b_.astype(F32)) + c_.astype(F32),),
                             [o, (rv, 0), (rv, 1), (rv, 2)], [F32], name="chip_sum_" + k)
        parts.append(part)
    others = swap_with_sibling(parts)
    out = {}
    for a, k in enumerate(BIG):
        shp = given[k].shape
        res = _rows_call(lambda p_, q_, w_, m_, v_: (p_ + q_,) + _adamw(w_, p_ + q_, m_, v_),
                         [parts[a], others[a], _flat2(given[k]), _flat2(given["m_" + k]), _flat2(given["v_" + k])],
                         [F32] * 4, name="adamw_" + k)
        out[k] = [r.reshape(shp) for r in res]

    def pack(d, pre=""):
        return jnp.concatenate([d[pre + k].reshape(-1) for k in SMALL]).reshape(-1, 128)

    g_small = allreduce_small(pack(gs))
    res = _rows_call(lambda g_, w_, m_, v_: (g_,) + _adamw(w_, g_, m_, v_),
                     [g_small, pack(given), pack(given, "m_"), pack(given, "v_")], [F32] * 4, name="adamw_small", tr=8 * 47)
    off = 0
    for k in SMALL:
        sz = given[k].size
        out[k] = [r.reshape(-1)[off:off + sz].reshape(given[k].shape) for r in res]
        off += sz

    grads = [out[k][0] for k in order]
    deltas = [out[k][1] for k in order]
    new_m = [out[k][2] for k in order]
    new_v = [out[k][3] for k in order]
    return (loss, dx[None], *grads, *deltas, *new_m, *new_v)
```

```python
import functools
import math

import jax
import jax.numpy as jnp
from jax import lax
from jax.experimental import pallas as pl
from jax.experimental.pallas import tpu as pltpu

F32 = jnp.float32
BF16 = jnp.bfloat16

D_MODEL = 1024
SEQ = 4096
DEPTH = 2
CHUNK = 128
RET_HEADS = 4
BRANCH_W = 512
N_IN = 7680
D_FF = 4096
LN_EPS = 1e-5
ROPE_BASE = 10000.0
ALPHA = (2 * DEPTH) ** 0.25
RET_SCALE = 128 ** -0.5
SB_SCALE = 64 ** -0.5
C_RET, C_SB, C_SGU, C_GATE = 0, 2048, 3584, 4608

ADAM_LR, ADAM_B1, ADAM_B2, ADAM_EPS, ADAM_WD, ADAM_STEP = 0.001, 0.9, 0.999, 1e-08, 0.01, 10

N_CHIPS = 4
VMEM_LIMIT = 56 * 1024 * 1024
MESH = pl.DeviceIdType.MESH

NN = ((1,), (0,))
NT = ((1,), (1,))
TN = ((0,), (0,))


def _dot(a, b, dims):
    return lax.dot_general(a, b, (dims, ((), ())), preferred_element_type=F32)


def _params(sem):
    return pltpu.CompilerParams(dimension_semantics=sem, vmem_limit_bytes=VMEM_LIMIT)


def _relu2(h):
    r = jnp.maximum(h, 0.0)
    return r * r


def matmul(a, b, *, mode, tm, tn, tk, outs=((F32, None),), pro=None, epi=None, tiles=(), rows=(), name):
    if mode == "nn":
        (M, K), N = a.shape, b.shape[1]
    elif mode == "nt":
        (M, K), N = a.shape, b.shape[0]
    else:
        (K, M), N = a.shape, b.shape[1]
    tm, tn, tk = min(tm, M), min(tn, N), min(tk, K)
    assert M % tm == 0 and N % tn == 0 and K % tk == 0, (name, M, N, K, tm, tn, tk)
    if mode == "nn":
        a_spec = pl.BlockSpec((tm, tk), lambda i, j, k: (i, k))
        b_spec = pl.BlockSpec((tk, tn), lambda i, j, k: (k, j))
        dims = NN
    elif mode == "nt":
        a_spec = pl.BlockSpec((tm, tk), lambda i, j, k: (i, k))
        b_spec = pl.BlockSpec((tn, tk), lambda i, j, k: (j, k))
        dims = NT
    else:
        a_spec = pl.BlockSpec((tk, tm), lambda i, j, k: (k, i))
        b_spec = pl.BlockSpec((tk, tn), lambda i, j, k: (k, j))
        dims = TN
    nk = K // tk
    nt_, nr, no = len(tiles), len(rows), len(outs)

    def body(a_ref, b_ref, *rest):
        tile_refs = rest[:nt_]
        row_refs = rest[nt_:nt_ + nr]
        out_refs = rest[nt_ + nr:nt_ + nr + no]
        av = a_ref[...]
        if pro is not None:
            av = pro(av)
        p = _dot(av.astype(BF16), b_ref[...].astype(BF16), dims)

        def finish(acc):
            vals = (acc,) * no if epi is None else epi(acc, *[r[...] for r in tile_refs], *[r[...] for r in row_refs])
            for o_ref, v in zip(out_refs, vals):
                o_ref[...] = v.astype(o_ref.dtype)

        if nk == 1:
            finish(p)
        else:
            acc_ref = rest[-1]
            k = pl.program_id(2)

            @pl.when(k == 0)
            def _():
                acc_ref[...] = p

            @pl.when(k > 0)
            def _():
                acc_ref[...] += p

            @pl.when(k == nk - 1)
            def _():
                finish(acc_ref[...])

    out_shape, out_specs = [], []
    for dt, width in outs:
        if width is None:
            out_shape.append(jax.ShapeDtypeStruct((M, N), dt))
            out_specs.append(pl.BlockSpec((tm, tn), lambda i, j, k: (i, j)))
        else:
            assert N == tn
            out_shape.append(jax.ShapeDtypeStruct((M, width), dt))
            out_specs.append(pl.BlockSpec((tm, width), lambda i, j, k: (i, 0)))
    in_specs = [a_spec, b_spec]
    in_specs += [pl.BlockSpec((tm, tn), lambda i, j, k: (i, j)) for _ in tiles]
    in_specs += [pl.BlockSpec((1, tn), lambda i, j, k: (0, j)) for _ in rows]
    res = pl.pallas_call(
        body, name=name, grid=(M // tm, N // tn, nk),
        in_specs=in_specs, out_specs=out_specs, out_shape=out_shape,
        scratch_shapes=[pltpu.VMEM((tm, tn), F32)] if nk > 1 else [],
        compiler_params=_params(("parallel", "parallel", "arbitrary")),
    )(a, b, *tiles, *rows)
    return res[0] if no == 1 else res


def _ln_epi(acc, res, g, b):
    u = ALPHA * res + acc
    mu = jnp.mean(u, axis=-1, keepdims=True)
    xc = u - mu
    var = jnp.mean(xc * xc, axis=-1, keepdims=True)
    rstd = lax.rsqrt(var + LN_EPS)
    xhat = xc * rstd
    return xhat * g + b, xhat, jnp.broadcast_to(rstd, (u.shape[0], 128))


def matmul_ln(a, w, res, g, b, *, pro=None, tk, name):
    n = w.shape[1]
    return matmul(a, w, mode="nn", tm=512, tn=n, tk=tk, pro=pro, epi=_ln_epi, tiles=(res,), rows=(g, b),
                  outs=((F32, None), (F32, None), (F32, 128)), name=name)


def ln_bwd(dy, xhat, rstd, g, *, name):
    T, D = dy.shape
    tm = min(512, T)

    def body(dy_ref, xh_ref, rs_ref, g_ref, du_ref, dg_ref, db_ref):
        dyv, xh = dy_ref[...], xh_ref[...]
        r = rs_ref[:, 0:1]
        dxh = dyv * g_ref[...]
        m1 = jnp.mean(dxh, axis=-1, keepdims=True)
        m2 = jnp.mean(dxh * xh, axis=-1, keepdims=True)
        du_ref[...] = r * (dxh - m1 - xh * m2)

        @pl.when(pl.program_id(0) == 0)
        def _():
            dg_ref[...] = jnp.zeros_like(dg_ref)
            db_ref[...] = jnp.zeros_like(db_ref)

        dg_ref[...] += jnp.sum(dyv * xh, axis=0, keepdims=True)
        db_ref[...] += jnp.sum(dyv, axis=0, keepdims=True)

    row = pl.BlockSpec((tm, D), lambda i: (i, 0))
    vec = pl.BlockSpec((1, D), lambda i: (0, 0))
    return pl.pallas_call(
        body, name=name, grid=(T // tm,),
        in_specs=[row, row, pl.BlockSpec((tm, 128), lambda i: (i, 0)), vec],
        out_specs=[row, vec, vec],
        out_shape=[jax.ShapeDtypeStruct((T, D), F32), jax.ShapeDtypeStruct((1, D), F32), jax.ShapeDtypeStruct((1, D), F32)],
        compiler_params=_params(("arbitrary",)),
    )(dy, xhat, rstd, g)


def loss_head(y, target):
    T, D = y.shape
    tm = min(512, T)

    def body(y_ref, t_ref, dy_ref, s_ref):
        e = y_ref[...] - t_ref[...]
        dy_ref[...] = e * (1.0 / D)

        @pl.when(pl.program_id(0) == 0)
        def _():
            s_ref[...] = jnp.zeros_like(s_ref)

        s_ref[...] += jnp.sum(jnp.mean(e * e, axis=-1, keepdims=True))

    row = pl.BlockSpec((tm, D), lambda i: (i, 0))
    return pl.pallas_call(
        body, name="loss_head", grid=(T // tm,),
        in_specs=[row, row], out_specs=[row, pl.BlockSpec((8, 128), lambda i: (0, 0))],
        out_shape=[jax.ShapeDtypeStruct((T, D), F32), jax.ShapeDtypeStruct((8, 128), F32)],
        compiler_params=_params(("arbitrary",)),
    )(y, target)


def _rope_tables(T):
    half = 64
    inv_freq = ROPE_BASE ** (-jnp.arange(half, dtype=F32) / half)
    ang = jnp.arange(T, dtype=jnp.int32).astype(F32)[:, None] * inv_freq[None, :]
    cos, sin = jnp.cos(ang), jnp.sin(ang)
    return jnp.concatenate([cos, cos], axis=1), jnp.concatenate([-sin, sin], axis=1)


def _ret_consts():
    H = RET_HEADS
    log_g = jnp.log(1.0 - 2.0 ** (-5.0 - jnp.arange(H, dtype=F32)))
    idx = jnp.arange(CHUNK, dtype=F32)
    diff = idx[:, None] - idx[None, :]
    dmat = jnp.where(diff[None] >= 0, jnp.exp(log_g[:, None, None] * diff[None]), 0.0)
    kd = jnp.exp(log_g[:, None] * (CHUNK - 1 - idx)[None, :])
    qd = jnp.exp(log_g[:, None] * (idx + 1.0)[None, :])
    cd = jnp.exp(log_g * CHUNK)
    full = (H, CHUNK, CHUNK)
    return (dmat.astype(F32), jnp.broadcast_to(kd[:, :, None], full), jnp.broadcast_to(qd[:, :, None], full),
            jnp.broadcast_to(cd[:, None, None], full))


def _swap_halves(v):
    return pltpu.roll(v, 64, 1)


def _group_norm(o):
    mu = jnp.mean(o, axis=-1, keepdims=True)
    xc = o - mu
    var = jnp.mean(xc * xc, axis=-1, keepdims=True)
    rstd = lax.rsqrt(var + LN_EPS)
    return xc * rstd, rstd


def ret_fwd(proj, cosf, sinf, consts, gn_g, gn_b, *, name):
    T = proj.shape[0]
    tb = min(512, T)
    nch = tb // CHUNK
    H = RET_HEADS

    def body(p_ref, cos_ref, sin_ref, dm_ref, kd_ref, qd_ref, cd_ref, g_ref, b_ref, out_ref, raw_ref, st_ref, s_ref):
        @pl.when(pl.program_id(0) == 0)
        def _():
            s_ref[...] = jnp.zeros_like(s_ref)

        for c in range(nch):
            r = slice(c * CHUNK, (c + 1) * CHUNK)
            cs, sn = cos_ref[r, :], sin_ref[r, :]
            for h in range(H):
                hc = slice(h * 128, (h + 1) * 128)
                q = p_ref[r, h * 128:(h + 1) * 128]
                k = p_ref[r, 512 + h * 128:512 + (h + 1) * 128]
                v = p_ref[r, 1024 + h * 128:1024 + (h + 1) * 128]
                gt = p_ref[r, 1536 + h * 128:1536 + (h + 1) * 128]
                qr = q * cs + _swap_halves(q) * sn
                kr = (k * cs + _swap_halves(k) * sn) * RET_SCALE
                sprev = s_ref[h]
                st_ref[c, h] = sprev
                qb, kb, vb = qr.astype(BF16), kr.astype(BF16), v.astype(BF16)
                s = _dot(qb, kb, NT) * dm_ref[h]
                o = _dot(s.astype(BF16), vb, NN) + _dot((qr * qd_ref[h]).astype(BF16), sprev.astype(BF16), NN)
                s_ref[h] = sprev * cd_ref[h] + _dot((kr * kd_ref[h]).astype(BF16), vb, TN)
                raw_ref[r, hc] = o
                y, _ = _group_norm(o)
                out_ref[r, hc] = (gt * jax.nn.sigmoid(gt)) * (y * g_ref[:, hc] + b_ref[:, hc])

    cmat = pl.BlockSpec((H, CHUNK, CHUNK), lambda i: (0, 0, 0))
    vec = pl.BlockSpec((1, BRANCH_W), lambda i: (0, 0))
    rope = pl.BlockSpec((tb, 128), lambda i: (i, 0))
    blk = pl.BlockSpec((tb, BRANCH_W), lambda i: (i, 0))
    return pl.pallas_call(
        body, name=name, grid=(T // tb,),
        in_specs=[pl.BlockSpec((tb, 2048), lambda i: (i, 0)), rope, rope, cmat, cmat, cmat, cmat, vec, vec],
        out_specs=[blk, blk, pl.BlockSpec((nch, H, CHUNK, CHUNK), lambda i: (i, 0, 0, 0))],
        out_shape=[jax.ShapeDtypeStruct((T, BRANCH_W), F32), jax.ShapeDtypeStruct((T, BRANCH_W), F32),
                   jax.ShapeDtypeStruct((T // CHUNK, H, CHUNK, CHUNK), F32)],
        scratch_shapes=[pltpu.VMEM((H, CHUNK, CHUNK), F32)],
        compiler_params=_params(("arbitrary",)),
    )(proj, cosf, sinf, *consts, gn_g, gn_b)


def ret_bwd(proj, cosf, sinf, consts, gn_g, gn_b, raw, states, dout, *, name):
    T = proj.shape[0]
    tb = min(512, T)
    nch = tb // CHUNK
    nb = T // tb
    H = RET_HEADS

    def body(p_ref, cos_ref, sin_ref, dm_ref, kd_ref, qd_ref, cd_ref, g_ref, b_ref, raw_ref, st_ref, do_ref,
             dp_ref, dg_ref, db_ref, ds_ref):
        @pl.when(pl.program_id(0) == 0)
        def _():
            ds_ref[...] = jnp.zeros_like(ds_ref)
            dg_ref[...] = jnp.zeros_like(dg_ref)
            db_ref[...] = jnp.zeros_like(db_ref)

        for c in reversed(range(nch)):
            r = slice(c * CHUNK, (c + 1) * CHUNK)
            cs, sn = cos_ref[r, :], sin_ref[r, :]
            for h in range(H):
                hc = slice(h * 128, (h + 1) * 128)
                q = p_ref[r, h * 128:(h + 1) * 128]
                k = p_ref[r, 512 + h * 128:512 + (h + 1) * 128]
                v = p_ref[r, 1024 + h * 128:1024 + (h + 1) * 128]
                gt = p_ref[r, 1536 + h * 128:1536 + (h + 1) * 128]
                qr = q * cs + _swap_halves(q) * sn
                kr = (k * cs + _swap_halves(k) * sn) * RET_SCALE
                sprev = st_ref[c, h]
                gv = g_ref[:, hc]
                y, rstd = _group_norm(raw_ref[r, hc])
                d_out = do_ref[r, hc]
                sg = jax.nn.sigmoid(gt)
                d_gate = d_out * (y * gv + b_ref[:, hc]) * (sg * (1.0 + gt * (1.0 - sg)))
                d_aff = d_out * (gt * sg)
                dg_ref[:, hc] += jnp.sum(d_aff * y, axis=0, keepdims=True)
                db_ref[:, hc] += jnp.sum(d_aff, axis=0, keepdims=True)
                dxh = d_aff * gv
                m1 = jnp.mean(dxh, axis=-1, keepdims=True)
                m2 = jnp.mean(dxh * y, axis=-1, keepdims=True)
                d_o = (rstd * (dxh - m1 - y * m2)).astype(BF16)
                qb, kb, vb = qr.astype(BF16), kr.astype(BF16), v.astype(BF16)
                dm, kd, qd = dm_ref[h], kd_ref[h], qd_ref[h]
                p = (_dot(qb, kb, NT) * dm).astype(BF16)
                dp = (_dot(d_o, vb, NT) * dm).astype(BF16)
                dsn = ds_ref[h]
                dsb = dsn.astype(BF16)
                dq_r = _dot(dp, kb, NN) + _dot(d_o, sprev.astype(BF16), NT) * qd
                dk_r = (_dot(dp, qb, TN) + _dot(vb, dsb, NT) * kd) * RET_SCALE
                d_v = _dot(p, d_o, TN) + _dot((kr * kd).astype(BF16), dsb, NN)
                ds_ref[h] = dsn * cd_ref[h] + _dot((qr * qd).astype(BF16), d_o, TN)
                dp_ref[r, h * 128:(h + 1) * 128] = (dq_r * cs - _swap_halves(dq_r) * sn).astype(BF16)
                dp_ref[r, 512 + h * 128:512 + (h + 1) * 128] = (dk_r * cs - _swap_halves(dk_r) * sn).astype(BF16)
                dp_ref[r, 1024 + h * 128:1024 + (h + 1) * 128] = d_v.astype(BF16)
                dp_ref[r, 1536 + h * 128:1536 + (h + 1) * 128] = d_gate.astype(BF16)

    cmat = pl.BlockSpec((H, CHUNK, CHUNK), lambda i: (0, 0, 0))
    vec = pl.BlockSpec((1, BRANCH_W), lambda i: (0, 0))
    rope = pl.BlockSpec((tb, 128), lambda i: (nb - 1 - i, 0))
    blk = pl.BlockSpec((tb, BRANCH_W), lambda i: (nb - 1 - i, 0))
    wide = pl.BlockSpec((tb, 2048), lambda i: (nb - 1 - i, 0))
    return pl.pallas_call(
        body, name=name, grid=(nb,),
        in_specs=[wide, rope, rope, cmat, cmat, cmat, cmat, vec, vec, blk,
                  pl.BlockSpec((nch, H, CHUNK, CHUNK), lambda i: (nb - 1 - i, 0, 0, 0)), blk],
        out_specs=[wide, vec, vec],
        out_shape=[jax.ShapeDtypeStruct((T, 2048), BF16), jax.ShapeDtypeStruct((1, BRANCH_W), F32),
                   jax.ShapeDtypeStruct((1, BRANCH_W), F32)],
        scratch_shapes=[pltpu.VMEM((H, CHUNK, CHUNK), F32)],
        compiler_params=_params(("arbitrary",)),
    )(proj, cosf, sinf, *consts, gn_g, gn_b, raw, states, dout)


def _sb_masks():
    row = lax.broadcasted_iota(jnp.int32, (CHUNK, CHUNK), 0)
    lane = lax.broadcasted_iota(jnp.int32, (CHUNK, CHUNK), 1)
    return row, lane


SB_QT = 512
SB_DEAD = -105.0


def _pair(v):
    hi = v.astype(BF16)
    return jnp.concatenate([hi, (v - hi.astype(F32)).astype(BF16)], axis=1)


def _sb_consts():
    r = lax.broadcasted_iota(jnp.int32, (256, 256), 0) & 127
    c = lax.broadcasted_iota(jnp.int32, (256, 256), 1)
    ones = c >= 128
    lane = lax.broadcasted_iota(jnp.int32, (CHUNK, CHUNK), 1)
    return (ones | (r > c)).astype(BF16), (ones | (r >= c)).astype(BF16), (lane < 64, lane >= 64)


def _per_head(x, hms):
    return jnp.concatenate([jnp.where(hm, x, 0.0) for hm in hms], axis=0).astype(BF16)


def _sb_logits(qb, kb2, mask2):
    z = _dot(qb, kb2, NT)
    l1p = jnp.log(1.0 + jnp.exp(-jnp.abs(z)))
    lsp = jnp.minimum(z, 0.0) - l1p
    lsn = lsp - z
    if mask2 is not None:
        lsn = jnp.where(mask2, lsn, 0.0)
    return lsp, lsn


def _sb_tile_mask(qt):
    trow = lax.broadcasted_iota(jnp.int32, (qt, 256), 0)
    tlane = lax.broadcasted_iota(jnp.int32, (qt, 256), 1) & 127
    return lambda m: (tlane + m * CHUNK) < trow


def sb_fwd(proj, *, name, job=None):
    T = proj.shape[0]
    qt = min(SB_QT, T)
    nsub = qt // CHUNK
    cb = C_SB // 128

    def body(q_ref, k_ref, v_ref, o_ref):
        u_gt, _, hms = _sb_consts()
        tile_mask = _sb_tile_mask(qt)

        def qtile(i, _):
            rq = pl.ds(pl.multiple_of(i * qt, qt), qt)
            qb = (q_ref[rq, :] * SB_SCALE).astype(BF16)

            def step(j, state, mask2):
                carry, acc = list(state[:2]), state[2]
                rk = pl.ds(pl.multiple_of(j * CHUNK, CHUNK), CHUNK)
                lsp, lsn = _sb_logits(qb, _per_head(k_ref[rk, :], hms), mask2)
                a_b = []
                for h in range(2):
                    hc = slice(h * 128, (h + 1) * 128)
                    r = _dot(_pair(lsn[:, hc]), u_gt, NN)
                    a = jnp.exp(lsp[:, hc] + r[:, :128] + carry[h])
                    if mask2 is not None:
                        a = jnp.where(mask2[:, hc], a, 0.0)
                    carry[h] = carry[h] + r[:, 128:]
                    a_b.append(a.astype(BF16))
                acc = acc + _dot(jnp.concatenate(a_b, axis=1), _per_head(v_ref[rk, :], hms), NN)
                return carry[0], carry[1], acc

            zero = jnp.zeros((qt, 128), F32)
            state = (zero, zero, zero)
            for m in reversed(range(nsub)):
                state = step(i * nsub + m, state, tile_mask(m))

            def live(c):
                return jnp.logical_and(c[0] < i, jnp.maximum(jnp.max(c[1][0]), jnp.max(c[1][1])) > SB_DEAD)

            def blocks(c):
                jj, st = c
                for u in range(nsub):
                    st = step((i - jj) * nsub - 1 - u, st, None)
                return jj + 1, st

            _, state = lax.while_loop(live, blocks, (jnp.int32(0), state))
            o_ref[rq, :] = state[2]
            return 0

        lax.fori_loop(0, T // qt, qtile, 0)

    def col(off):
        return pl.BlockSpec((T, 128), lambda hp: (0, off + hp))

    steps = BRANCH_W // 128
    j = _job_args(job, 3, 1)
    res = pl.pallas_call(
        _hosting(body, job, 3, 1, 0, steps), name=name, grid=(steps,),
        in_specs=[col(cb), col(cb + 4), col(cb + 8)] + j["in_specs"], out_specs=[col(0)] + j["out_specs"],
        out_shape=[jax.ShapeDtypeStruct((T, BRANCH_W), F32)] + j["out_shape"],
        scratch_shapes=j["scratch"], input_output_aliases=j["aliases"],
        compiler_params=_params(("parallel",) if job is None else ("arbitrary",)),
    )(proj, proj, proj, *j["ins"])
    return res[0], list(res[1:])


def sb_bwd(proj, out, dout, *, name, job=None):
    T = proj.shape[0]
    qt = min(SB_QT, T)
    nsub = qt // CHUNK
    cb = C_SB // 128

    def body(q_ref, k_ref, v_ref, o_ref, do_ref, dq_ref, dk_ref, dv_ref, dkt_ref, dvt_ref):
        u_gt, u_ge, hms = _sb_consts()
        tile_mask = _sb_tile_mask(qt)
        tall_lane = lax.broadcasted_iota(jnp.int32, (qt, 128), 1)
        top = lax.broadcasted_iota(jnp.int32, (CHUNK, CHUNK), 0) < 64
        dkt_ref[...] = jnp.zeros_like(dkt_ref)
        dvt_ref[...] = jnp.zeros_like(dvt_ref)

        def qtile(i, _):
            rq = pl.ds(pl.multiple_of(i * qt, qt), qt)
            qs = q_ref[rq, :] * SB_SCALE
            qb, q_t = qs.astype(BF16), qs.T.astype(BF16)
            dov = do_ref[rq, :]
            dob, do_t = dov.astype(BF16), dov.T.astype(BF16)
            prod = dob.astype(F32) * o_ref[rq, :]
            total = [jnp.broadcast_to(jnp.sum(jnp.where(hm, prod, 0.0), axis=1, keepdims=True), (qt, 128))
                     for hm in (tall_lane < 64, tall_lane >= 64)]

            def step(j, state, mask2):
                c_l, c_w, dq = list(state[:2]), list(state[2:4]), state[4]
                rk = pl.ds(pl.multiple_of(j * CHUNK, CHUNK), CHUNK)
                kb2, vb2 = _per_head(k_ref[rk, :], hms), _per_head(v_ref[rk, :], hms)
                lsp, lsn = _sb_logits(qb, kb2, mask2)
                da = _dot(dob, vb2, NT)
                sp = jnp.exp(lsp)
                a_b, dz_b = [], []
                for h in range(2):
                    hc = slice(h * 128, (h + 1) * 128)
                    r = _dot(_pair(lsn[:, hc]), u_gt, NN)
                    a = jnp.exp(lsp[:, hc] + r[:, :128] + c_l[h])
                    if mask2 is not None:
                        a = jnp.where(mask2[:, hc], a, 0.0)
                    c_l[h] = c_l[h] + r[:, 128:]
                    a = a.astype(BF16)
                    w = a.astype(F32) * da[:, hc]
                    r = _dot(_pair(w), u_ge, NN)
                    later_w = r[:, :128] + c_w[h]
                    c_w[h] = c_w[h] + r[:, 128:]
                    dz = w * (1.0 - sp[:, hc]) - sp[:, hc] * (total[h] - later_w)
                    if mask2 is not None:
                        dz = jnp.where(mask2[:, hc], dz, 0.0)
                    a_b.append(a)
                    dz_b.append(dz.astype(BF16))
                a_b, dz_b = jnp.concatenate(a_b, axis=1), jnp.concatenate(dz_b, axis=1)
                dkt = _dot(q_t, dz_b, NN)
                dvt = _dot(do_t, a_b, NN)
                dkt_ref[j] += jnp.where(top, dkt[:, :128], dkt[:, 128:])
                dvt_ref[j] += jnp.where(top, dvt[:, :128], dvt[:, 128:])
                return c_l[0], c_l[1], c_w[0], c_w[1], dq + _dot(dz_b, kb2, NN)

            zero = jnp.zeros((qt, 128), F32)
            state = (zero,) * 5
            for m in reversed(range(nsub)):
                state = step(i * nsub + m, state, tile_mask(m))

            def live(c):
                return jnp.logical_and(c[0] < i, jnp.maximum(jnp.max(c[1][0]), jnp.max(c[1][1])) > SB_DEAD)

            def blocks(c):
                jj, st = c
                for u in range(nsub):
                    st = step((i - jj) * nsub - 1 - u, st, None)
                return jj + 1, st

            _, state = lax.while_loop(live, blocks, (jnp.int32(0), state))
            dq_ref[rq, :] = (state[4] * SB_SCALE).astype(BF16)
            return 0

        lax.fori_loop(0, T // qt, qtile, 0)

        def untranspose(jb, _):
            rk = pl.ds(pl.multiple_of(jb * CHUNK, CHUNK), CHUNK)
            dk_ref[rk, :] = dkt_ref[jb].T.astype(BF16)
            dv_ref[rk, :] = dvt_ref[jb].T.astype(BF16)
            return 0

        lax.fori_loop(0, T // CHUNK, untranspose, 0)

    def col(off):
        return pl.BlockSpec((T, 128), lambda hp: (0, off + hp))

    o16 = jax.ShapeDtypeStruct((T, BRANCH_W), BF16)
    steps = BRANCH_W // 128
    j = _job_args(job, 5, 3)
    acc = pltpu.VMEM((T // CHUNK, CHUNK, CHUNK), F32)
    res = pl.pallas_call(
        _hosting(body, job, 5, 3, 2, steps), name=name, grid=(steps,),
        in_specs=[col(cb), col(cb + 4), col(cb + 8), col(0), col(0)] + j["in_specs"],
        out_specs=[col(0), col(0), col(0)] + j["out_specs"], out_shape=[o16, o16, o16] + j["out_shape"],
        scratch_shapes=[acc, acc] + j["scratch"], input_output_aliases=j["aliases"],
        compiler_params=_params(("parallel",) if job is None else ("arbitrary",)),
    )(proj, proj, proj, out, dout, *j["ins"])
    return res[0], res[1], res[2], list(res[3:])


_G0 = math.sqrt(2.0 / math.pi)
_G1 = 0.044715


def _gelu(x):
    return 0.5 * x * (1.0 + jnp.tanh(_G0 * (x + _G1 * x * x * x)))


def _gelu_grad(x):
    t = jnp.tanh(_G0 * (x + _G1 * x * x * x))
    return 0.5 * (1.0 + t) + 0.5 * x * (1.0 - t * t) * (_G0 * (1.0 + 3.0 * _G1 * x * x))


def _tril():
    row, lane = _sb_masks()
    return row >= lane


def sgu_fwd(proj, ln_g, ln_b, w, bias, *, name):
    T = proj.shape[0]
    tb = min(512, T)
    G = BRANCH_W // 128

    def body(u_ref, v_ref, g_ref, b_ref, w_ref, bias_ref, o_ref):
        vv = _gelu(v_ref[...])
        xh, _ = _group_norm(vv)
        vn = (xh * g_ref[...] + b_ref[...]).astype(BF16)
        tril = _tril()
        for g in range(G):
            wg = jnp.where(tril, w_ref[g], 0.0).astype(BF16)
            gc = slice(g * 128, (g + 1) * 128)
            for c in range(tb // CHUNK):
                r = slice(c * CHUNK, (c + 1) * CHUNK)
                sv = _dot(wg, vn[r, gc], NN) + bias_ref[g]
                o_ref[r, gc] = _gelu(u_ref[r, gc]) * sv

    cu, cv = C_SGU // BRANCH_W, C_SGU // BRANCH_W + 1
    vec = pl.BlockSpec((1, BRANCH_W), lambda i: (0, 0))
    mat = pl.BlockSpec((G, CHUNK, CHUNK), lambda i: (0, 0, 0))
    return pl.pallas_call(
        body, name=name, grid=(T // tb,),
        in_specs=[pl.BlockSpec((tb, BRANCH_W), lambda i: (i, cu)), pl.BlockSpec((tb, BRANCH_W), lambda i: (i, cv)),
                  vec, vec, mat, mat],
        out_specs=pl.BlockSpec((tb, BRANCH_W), lambda i: (i, 0)),
        out_shape=jax.ShapeDtypeStruct((T, BRANCH_W), F32),
        compiler_params=_params(("parallel",)),
    )(proj, proj, ln_g, ln_b, w, bias)


def sgu_bwd(proj, ln_g, ln_b, w, bias, dout, *, name):
    T = proj.shape[0]
    tb = min(512, T)
    G = BRANCH_W // 128

    def body(u_ref, v_ref, g_ref, b_ref, w_ref, bias_ref, do_ref, dp_ref, dw_ref, dbias_ref, dg_ref, db_ref, dvn_ref):
        @pl.when(pl.program_id(0) == 0)
        def _():
            dw_ref[...] = jnp.zeros_like(dw_ref)
            dbias_ref[...] = jnp.zeros_like(dbias_ref)
            dg_ref[...] = jnp.zeros_like(dg_ref)
            db_ref[...] = jnp.zeros_like(db_ref)

        gv = v_ref[...]
        vv = _gelu(gv)
        xh, rstd = _group_norm(vv)
        vn = (xh * g_ref[...] + b_ref[...]).astype(BF16)
        tril = _tril()
        for g in range(G):
            wg = jnp.where(tril, w_ref[g], 0.0).astype(BF16)
            gc = slice(g * 128, (g + 1) * 128)
            for c in range(tb // CHUNK):
                r = slice(c * CHUNK, (c + 1) * CHUNK)
                vn_c = vn[r, gc]
                sv = _dot(wg, vn_c, NN) + bias_ref[g]
                gu = u_ref[r, gc]
                d_o = do_ref[r, gc]
                dp_ref[r, gc] = (d_o * sv * _gelu_grad(gu)).astype(BF16)
                dsv = d_o * _gelu(gu)
                dsv_b = dsv.astype(BF16)
                dvn_ref[r, gc] = _dot(wg, dsv_b, TN)
                dw_ref[g] += jnp.where(tril, _dot(dsv_b, vn_c, NT), 0.0)
                dbias_ref[g] += jnp.broadcast_to(jnp.sum(dsv, axis=1, keepdims=True), (CHUNK, CHUNK))
        dvn = dvn_ref[...]
        dg_ref[...] += jnp.sum(dvn * xh, axis=0, keepdims=True)
        db_ref[...] += jnp.sum(dvn, axis=0, keepdims=True)
        dxh = dvn * g_ref[...]
        m1 = jnp.mean(dxh, axis=-1, keepdims=True)
        m2 = jnp.mean(dxh * xh, axis=-1, keepdims=True)
        dp_ref[:, BRANCH_W:2 * BRANCH_W] = (rstd * (dxh - m1 - xh * m2) * _gelu_grad(gv)).astype(BF16)

    cu, cv = C_SGU // BRANCH_W, C_SGU // BRANCH_W + 1
    vec = pl.BlockSpec((1, BRANCH_W), lambda i: (0, 0))
    mat = pl.BlockSpec((G, CHUNK, CHUNK), lambda i: (0, 0, 0))
    blk = pl.BlockSpec((tb, BRANCH_W), lambda i: (i, 0))
    msh = jax.ShapeDtypeStruct((G, CHUNK, CHUNK), F32)
    vsh = jax.ShapeDtypeStruct((1, BRANCH_W), F32)
    return pl.pallas_call(
        body, name=name, grid=(T // tb,),
        in_specs=[pl.BlockSpec((tb, BRANCH_W), lambda i: (i, cu)), pl.BlockSpec((tb, BRANCH_W), lambda i: (i, cv)),
                  vec, vec, mat, mat, blk],
        out_specs=[pl.BlockSpec((tb, 2 * BRANCH_W), lambda i: (i, 0)), mat, mat, vec, vec],
        out_shape=[jax.ShapeDtypeStruct((T, 2 * BRANCH_W), BF16), msh, msh, vsh, vsh],
        scratch_shapes=[pltpu.VMEM((tb, BRANCH_W), F32)],
        compiler_params=_params(("arbitrary",)),
    )(proj, proj, ln_g, ln_b, w, bias, dout)


def merge_fwd(a1, a2, a3, p1, p2, p3, proj, *, name):
    T = a1.shape[0]
    tm, tn = min(1024, T), 512
    gb = C_GATE // tn

    def body(a1_ref, a2_ref, a3_ref, p1_ref, p2_ref, p3_ref, g1_ref, g2_ref, g3_ref, m_ref, r1_ref, r2_ref, r3_ref):
        m = None
        for a_ref, p_ref, g_ref, r_ref in ((a1_ref, p1_ref, g1_ref, r1_ref), (a2_ref, p2_ref, g2_ref, r2_ref),
                                           (a3_ref, p3_ref, g3_ref, r3_ref)):
            r = _dot(a_ref[...].astype(BF16), p_ref[...], NN)
            r_ref[...] = r
            t = jax.nn.sigmoid(g_ref[...]) * r
            m = t if m is None else m + t
        m_ref[...] = m

    a_spec = pl.BlockSpec((tm, BRANCH_W), lambda i, j: (i, 0))
    p_spec = pl.BlockSpec((BRANCH_W, tn), lambda i, j: (0, j))
    o_spec = pl.BlockSpec((tm, tn), lambda i, j: (i, j))
    osh = jax.ShapeDtypeStruct((T, D_MODEL), F32)
    gates = [pl.BlockSpec((tm, tn), functools.partial(lambda i, j, o: (i, o + j), o=gb + 2 * n)) for n in range(3)]
    return pl.pallas_call(
        body, name=name, grid=(T // tm, D_MODEL // tn),
        in_specs=[a_spec, a_spec, a_spec, p_spec, p_spec, p_spec, *gates],
        out_specs=[o_spec] * 4, out_shape=[osh] * 4,
        compiler_params=_params(("parallel", "parallel")),
    )(a1, a2, a3, p1, p2, p3, proj, proj, proj)


def merge_bwd(dm, r1, r2, r3, proj, *, name):
    T = dm.shape[0]
    tm, tn = min(512, T), 512
    gb = C_GATE // tn

    def body(dm_ref, r1_ref, r2_ref, r3_ref, g1_ref, g2_ref, g3_ref, dr1_ref, dr2_ref, dr3_ref, dg1_ref, dg2_ref, dg3_ref):
        d = dm_ref[...]
        for r_ref, g_ref, dr_ref, dg_ref in ((r1_ref, g1_ref, dr1_ref, dg1_ref), (r2_ref, g2_ref, dr2_ref, dg2_ref),
                                             (r3_ref, g3_ref, dr3_ref, dg3_ref)):
            s = jax.nn.sigmoid(g_ref[...])
            dr_ref[...] = (d * s).astype(BF16)
            dg_ref[...] = (d * r_ref[...] * (s * (1.0 - s))).astype(BF16)

    o_spec = pl.BlockSpec((tm, tn), lambda i, j: (i, j))
    osh = jax.ShapeDtypeStruct((T, D_MODEL), BF16)
    gates = [pl.BlockSpec((tm, tn), functools.partial(lambda i, j, o: (i, o + j), o=gb + 2 * n)) for n in range(3)]
    return pl.pallas_call(
        body, name=name, grid=(T // tm, D_MODEL // tn),
        in_specs=[o_spec] * 4 + gates, out_specs=[o_spec] * 6, out_shape=[osh] * 6,
        compiler_params=_params(("parallel", "parallel")),
    )(dm, r1, r2, r3, proj, proj, proj)


def _rows_call(fn, ins, out_dtypes, *, name, tr=256):
    first = ins[0][0] if isinstance(ins[0], tuple) else ins[0]
    R, C = first.shape[-2:]
    tr = min(tr, R)
    assert R % tr == 0, (name, R, tr)
    arrs, specs = [], []
    for x in ins:
        if isinstance(x, tuple):
            arrs.append(x[0])
            specs.append(pl.BlockSpec((None, tr, C), functools.partial(lambda i, n: (n, i, 0), n=x[1])))
        else:
            arrs.append(x)
            specs.append(pl.BlockSpec((tr, C), lambda i: (i, 0)))
    ni = len(arrs)

    def body(*refs):
        vals = fn(*[r[...] for r in refs[:ni]])
        for o_ref, v in zip(refs[ni:], vals):
            o_ref[...] = v.astype(o_ref.dtype)

    res = pl.pallas_call(
        body, name=name, grid=(R // tr,), in_specs=specs,
        out_specs=[pl.BlockSpec((tr, C), lambda i: (i, 0)) for _ in out_dtypes],
        out_shape=[jax.ShapeDtypeStruct((R, C), dt) for dt in out_dtypes],
        compiler_params=_params(("parallel",)),
    )(*arrs)
    return res


def _adamw(w, g, m, v):
    m2 = ADAM_B1 * m + (1.0 - ADAM_B1) * g
    v2 = ADAM_B2 * v + (1.0 - ADAM_B2) * (g * g)
    m_hat = m2 / (1.0 - ADAM_B1 ** ADAM_STEP)
    v_hat = v2 / (1.0 - ADAM_B2 ** ADAM_STEP)
    delta = -ADAM_LR * (m_hat / (jnp.sqrt(v_hat) + ADAM_EPS) + ADAM_WD * w)
    return delta, m2, v2


def _place():
    return lax.axis_index("x"), lax.axis_index("y"), lax.axis_index("c")


def _chip_peers(x, y, c):
    return [((1 - x, y, c), 2 * (1 - x) + y), ((x, 1 - y, c), 2 * x + 1 - y), ((1 - x, 1 - y, c), 2 * (1 - x) + 1 - y)]


def _shard_of(ref, axis, k, n):
    start = pl.multiple_of(k * n, 128)
    return ref.at[pl.ds(start, n), :] if axis == 0 else ref.at[:, pl.ds(start, n)]


ANY = pl.BlockSpec(memory_space=pl.ANY)


class CopyJob:
    def __init__(self, ins, out_shape, scratch, copies, aliases=None):
        self.ins, self.out_shape, self.scratch, self.copies = list(ins), list(out_shape), list(scratch), copies
        self.aliases = dict(aliases or {})

    def start(self, ins, outs, sems):
        local, remote, _, _ = self.copies(ins, outs, sems)
        for d in local + remote:
            d.start()

    def finish(self, ins, outs, sems):
        local, remote, arrivals, relays = self.copies(ins, outs, sems)
        for needs, sends, _ in relays:
            for d in needs:
                d.wait_recv()
            for d in sends:
                d.start()
        for d in arrivals + [d for _, _, arrives in relays for d in arrives]:
            d.wait_recv()
        for d in remote + [d for _, sends, _ in relays for d in sends]:
            d.wait_send()
        for d in local:
            d.wait()


def run_job(job, *, name):
    ni, no = len(job.ins), len(job.out_shape)

    def body(*refs):
        parts = refs[:ni], refs[ni:ni + no], refs[ni + no:]
        job.start(*parts)
        job.finish(*parts)

    return pl.pallas_call(
        body, name=name, in_specs=[ANY] * ni, out_specs=[ANY] * no, out_shape=job.out_shape,
        scratch_shapes=job.scratch, input_output_aliases=job.aliases,
    )(*job.ins)


def _job_args(job, n_in, n_out):
    if job is None:
        return dict(ins=[], in_specs=[], out_specs=[], out_shape=[], scratch=[], aliases={})
    return dict(ins=job.ins, in_specs=[ANY] * len(job.ins), out_specs=[ANY] * len(job.out_shape),
                out_shape=job.out_shape, scratch=job.scratch,
                aliases={n_in + i: n_out + o for i, o in job.aliases.items()})


def _hosting(body, job, n_in, n_out, n_scratch, steps):
    if job is None:
        return body
    ji, jo = len(job.ins), len(job.out_shape)

    def hosted(*refs):
        o = n_in + ji
        s = o + n_out + jo
        parts = refs[n_in:o], refs[o + n_out:s], refs[s + n_scratch:]

        @pl.when(pl.program_id(0) == 0)
        def _():
            job.start(*parts)

        body(*refs[:n_in], *refs[o:o + n_out], *refs[s:s + n_scratch])

        @pl.when(pl.program_id(0) == steps - 1)
        def _():
            job.finish(*parts)

    return hosted


def _job_sems(n_remote, n_local):
    return [pltpu.SemaphoreType.DMA((n_remote,)), pltpu.SemaphoreType.DMA((n_remote,)), pltpu.SemaphoreType.DMA((n_local,))]


def gather_job(shards, axes):
    na = len(shards)

    def copies(ins, outs, sems):
        send, recv, loc = sems
        x, y, c = _place()
        k = 2 * x + y
        local, remote, relays = [], [], []
        for a in range(na):
            r, n = ins[a].shape[0], ins[a].shape[axes[a]]
            half = r // 2

            def part(kk, cc, a=a, n=n, half=half):
                rows = pl.ds(pl.multiple_of(cc * half + (kk * n if axes[a] == 0 else 0), 8), half)
                return outs[a].at[rows, :] if axes[a] == 0 else outs[a].at[rows, pl.ds(pl.multiple_of(kk * n, 128), n)]

            local.append(pltpu.make_async_copy(ins[a], _shard_of(outs[a], axes[a], k, n), loc.at[a]))
            needs, passes, lands = [], [], []
            for j, (peer, kp) in enumerate(_chip_peers(x, y, c)):
                s = 6 * a + j
                remote.append(pltpu.make_async_remote_copy(
                    ins[a].at[pl.ds(pl.multiple_of(c * half, 8), half), :], part(k, c), send.at[s], recv.at[s],
                    device_id=peer, device_id_type=MESH))
                needs.append(pltpu.make_async_remote_copy(part(kp, c), part(kp, c), send.at[s], recv.at[s],
                                                          device_id=peer, device_id_type=MESH))
                passes.append(pltpu.make_async_remote_copy(part(kp, c), part(kp, c), send.at[s + 3], recv.at[s + 3],
                                                           device_id=(x, y, 1 - c), device_id_type=MESH))
                lands.append(pltpu.make_async_remote_copy(part(kp, 1 - c), part(kp, 1 - c), send.at[s + 3], recv.at[s + 3],
                                                          device_id=(x, y, 1 - c), device_id_type=MESH))
            relays.append((needs, passes, lands))
        return local, remote, [], relays

    out_shape = []
    for a in range(na):
        r, c = shards[a].shape
        out_shape.append(jax.ShapeDtypeStruct((r * N_CHIPS, c) if axes[a] == 0 else (r, c * N_CHIPS), BF16))
    return CopyJob(shards, out_shape, _job_sems(6 * na, na), copies)


def scatter_job(layers, g16, g32, axes, filled):
    na = len(axes)

    def shard_shape(a):
        r, c = g32[a].shape
        return (r // N_CHIPS, c) if axes[a] == 0 else (r, c // N_CHIPS)

    def copies(ins, outs, sems):
        send, recv_sems, loc = sems
        b16, b32 = ins[:na], ins[na:2 * na]
        recv, own = outs[:na], outs[na:]
        x, y, c = _place()
        k = 2 * x + y
        local, remote = [], []
        for a in range(na):
            n = shard_shape(a)[axes[a]]
            local.append(pltpu.make_async_copy(_shard_of(b32[a], axes[a], k, n), own[a].at[layers[a]], loc.at[a]))
            for r, (peer, kp) in enumerate(_chip_peers(x, y, c)):
                remote.append(pltpu.make_async_remote_copy(_shard_of(b16[a], axes[a], kp, n), recv[a].at[r, layers[a]],
                                                           send.at[3 * a + r], recv_sems.at[3 * a + r],
                                                           device_id=peer, device_id_type=MESH))
        return local, remote, remote, []

    out_shape = [jax.ShapeDtypeStruct((3, DEPTH) + shard_shape(a), BF16) for a in range(na)]
    out_shape += [jax.ShapeDtypeStruct((DEPTH,) + shard_shape(a), F32) for a in range(na)]
    ins = list(g16) + list(g32)
    aliases = {}
    for a in range(na):
        if filled[a] is not None:
            aliases[len(ins)] = a
            aliases[len(ins) + 1] = na + a
            ins += list(filled[a])
    return CopyJob(ins, out_shape, _job_sems(3 * na, na), copies, aliases)


def swap_with_sibling(parts):
    na = len(parts)

    def body(*refs):
        ins, outs = refs[:na], refs[na:2 * na]
        send_sems, recv_sems = refs[2 * na:]
        x, y, c = _place()
        cps = [pltpu.make_async_remote_copy(ins[a], outs[a], send_sems.at[a], recv_sems.at[a],
                                            device_id=(x, y, 1 - c), device_id_type=MESH) for a in range(na)]
        for cp in cps:
            cp.start()
        for cp in cps:
            cp.wait()

    return pl.pallas_call(
        body, name="swap_with_sibling", in_specs=[ANY] * na, out_specs=[ANY] * na,
        out_shape=[jax.ShapeDtypeStruct(p.shape, p.dtype) for p in parts],
        scratch_shapes=[pltpu.SemaphoreType.DMA((na,)), pltpu.SemaphoreType.DMA((na,))],
    )(*parts)


def allreduce_small(p):
    R = p.shape[0]

    def body(p_ref, o_ref, buf, send_sems, recv_sems):
        x, y, c = _place()
        me = 4 * x + 2 * y + c
        cps = []
        for rel in range(1, 8):
            dx, dy, dc = rel >> 2, (rel >> 1) & 1, rel & 1
            peer = (1 - x if dx else x, 1 - y if dy else y, 1 - c if dc else c)
            cp = pltpu.make_async_remote_copy(p_ref, buf.at[me], send_sems.at[rel - 1], recv_sems.at[rel - 1],
                                              device_id=peer, device_id_type=MESH)
            cp.start()
            cps.append((cp, 4 * peer[0] + 2 * peer[1] + peer[2]))
        buf[me] = p_ref[...]
        for rel, (cp, who) in enumerate(cps):
            pltpu.make_async_remote_copy(p_ref, buf.at[who], send_sems.at[rel], recv_sems.at[rel],
                                         device_id=(x, y, c), device_id_type=MESH).wait_recv()
        acc = buf[0]
        for d in range(1, 8):
            acc = acc + buf[d]
        o_ref[...] = acc
        for cp, _ in cps:
            cp.wait_send()

    return pl.pallas_call(
        body, name="allreduce_small",
        in_specs=[pl.BlockSpec(memory_space=pltpu.VMEM)], out_specs=pl.BlockSpec(memory_space=pltpu.VMEM),
        out_shape=jax.ShapeDtypeStruct((R, 128), F32),
        scratch_shapes=[pltpu.VMEM((8, R, 128), F32), pltpu.SemaphoreType.DMA((7,)), pltpu.SemaphoreType.DMA((7,))],
        compiler_params=pltpu.CompilerParams(vmem_limit_bytes=VMEM_LIMIT),
    )(p)


BIG = ("w_in", "p_ret", "p_sb", "p_sgu", "w_out", "w_up", "w_down")
BIG_AXIS = {"w_in": 1, "p_ret": 1, "p_sb": 1, "p_sgu": 1, "w_out": 0, "w_up": 1, "w_down": 0}
SMALL = ("ret_gn_g", "ret_gn_b", "sgu_ln_g", "sgu_ln_b", "sgu_w", "sgu_b", "ln1_g", "ln1_b", "ln2_g", "ln2_b")


def layer_forward(l, x0, W, sm, rope, rconsts, job=None, job_done=None):
    n = f"l{l}_"
    proj = matmul(x0, W["w_in"], mode="nn", tm=1024, tn=640, tk=1024, name=n + "proj")
    retg, raw, states = ret_fwd(proj, *rope, rconsts, sm["ret_gn_g"], sm["ret_gn_b"], name=n + "ret_fwd")
    sb, job_out = sb_fwd(proj, name=n + "sb_fwd", job=job)
    if job is not None:
        job_done(job_out)
    sg = sgu_fwd(proj, sm["sgu_ln_g"], sm["sgu_ln_b"], sm["sgu_w"], sm["sgu_bias"], name=n + "sgu_fwd")
    merged, r1, r2, r3 = merge_fwd(retg, sb, sg, W["p_ret"], W["p_sb"], W["p_sgu"], proj, name=n + "merge_fwd")
    x1, xh1, rs1 = matmul_ln(merged, W["w_out"], x0, sm["ln1_g"], sm["ln1_b"], tk=1024, name=n + "out_ln1")
    h1 = matmul(x1, W["w_up"], mode="nn", tm=1024, tn=1024, tk=1024, name=n + "up")
    x2, xh2, rs2 = matmul_ln(h1, W["w_down"], x1, sm["ln2_g"], sm["ln2_b"], pro=_relu2, tk=1024, name=n + "down_ln2")
    saved = dict(x0=x0, proj=proj, retg=retg, raw=raw, states=states, sb=sb, sg=sg, merged=merged, r=(r1, r2, r3),
                 x1=x1, xh1=xh1, rs1=rs1, h1=h1, xh2=xh2, rs2=rs2)
    return x2, saved


def layer_backward(l, dx2, s, W, sm, rope, rconsts, make_job=None, job_done=None):
    n = f"l{l}_"
    two = ((F32, None), (BF16, None))
    gw, gs = {}, {}
    du2, gs["ln2_g"], gs["ln2_b"] = ln_bwd(dx2, s["xh2"], s["rs2"], sm["ln2_g"], name=n + "ln2_bwd")
    gw["w_down"] = matmul(s["h1"], du2, mode="tn", tm=1024, tn=1024, tk=512, pro=_relu2, outs=two, name=n + "g_down")
    dh1 = matmul(du2, W["w_down"], mode="nt", tm=1024, tn=1024, tk=1024, outs=((BF16, None),),
                 epi=lambda acc, h: (acc * (2.0 * jnp.maximum(h, 0.0)),), tiles=(s["h1"],), name=n + "d_h1")
    gw["w_up"] = matmul(s["x1"], dh1, mode="tn", tm=1024, tn=1024, tk=512, outs=two, name=n + "g_up")
    dx1 = matmul(dh1, W["w_up"], mode="nt", tm=1024, tn=1024, tk=1024,
                 epi=lambda acc, d: (acc + ALPHA * d,), tiles=(du2,), name=n + "d_x1")
    du1, gs["ln1_g"], gs["ln1_b"] = ln_bwd(dx1, s["xh1"], s["rs1"], sm["ln1_g"], name=n + "ln1_bwd")
    gw["w_out"] = matmul(s["merged"], du1, mode="tn", tm=1024, tn=1024, tk=512, outs=two, name=n + "g_out")
    dmerged = matmul(du1, W["w_out"], mode="nt", tm=1024, tn=1024, tk=1024, name=n + "d_merged")
    dr1, dr2, dr3, dg1, dg2, dg3 = merge_bwd(dmerged, *s["r"], s["proj"], name=n + "merge_bwd")
    d_branch = {}
    for nm, a, dr in (("p_ret", s["retg"], dr1), ("p_sb", s["sb"], dr2), ("p_sgu", s["sg"], dr3)):
        gw[nm] = matmul(a, dr, mode="tn", tm=512, tn=1024, tk=512, outs=two, name=n + "g_" + nm)
        d_branch[nm] = matmul(dr, W[nm], mode="nt", tm=1024, tn=512, tk=1024, name=n + "d_" + nm)
    dret, gs["ret_gn_g"], gs["ret_gn_b"] = ret_bwd(s["proj"], *rope, rconsts, sm["ret_gn_g"], sm["ret_gn_b"], s["raw"],
                                                    s["states"], d_branch["p_ret"], name=n + "ret_bwd")
    job = make_job(gw) if make_job is not None else None
    dsq, dsk, dsv, job_out = sb_bwd(s["proj"], s["sb"], d_branch["p_sb"], name=n + "sb_bwd", job=job)
    if job is not None:
        job_done(job_out)
    dsgu, gs["sgu_w"], dbias, gs["sgu_ln_g"], gs["sgu_ln_b"] = sgu_bwd(
        s["proj"], sm["sgu_ln_g"], sm["sgu_ln_b"], sm["sgu_w"], sm["sgu_bias"], d_branch["p_sgu"], name=n + "sgu_bwd")
    gs["sgu_b"] = dbias[:, :, 0]
    dproj = jnp.concatenate([dret, dsq, dsk, dsv, dsgu, dg1, dg2, dg3], axis=1)
    gw["w_in"] = matmul(s["x0"], dproj, mode="tn", tm=1024, tn=1536, tk=512, outs=two, name=n + "g_in")
    dx0 = matmul(dproj, W["w_in"], mode="nt", tm=1024, tn=1024, tk=1536,
                 epi=lambda acc, d: (acc + ALPHA * d,), tiles=(du1,), name=n + "d_x0")
    return dx0, gw, gs


def local_step(x, target, small, plan):
    T = x.shape[0]
    rope = _rope_tables(T)
    rconsts = _ret_consts()
    sms = []
    for l in range(DEPTH):
        sm = {k: small[k][l][None, :] for k in SMALL if k not in ("sgu_w", "sgu_b")}
        sm["sgu_w"] = small["sgu_w"][l]
        sm["sgu_bias"] = jnp.broadcast_to(small["sgu_b"][l][:, :, None], (4, CHUNK, CHUNK))
        sms.append(sm)
    h, saved = x, []
    for l in range(DEPTH):
        h, s = layer_forward(l, h, plan.weights(l), sms[l], rope, rconsts, plan.fwd_job(l), plan.job_done)
        saved.append(s)
    dy, sq = loss_head(h, target)
    gs = {k: [None] * DEPTH for k in SMALL}
    for l in reversed(range(DEPTH)):
        dy, gwl, gsl = layer_backward(l, dy, saved[l], plan.weights(l), sms[l], rope, rconsts,
                                      functools.partial(plan.bwd_job, l), plan.job_done)
        plan.grads(l, gwl)
        for k in SMALL:
            gs[k][l] = gsl[k].reshape(small[k].shape[1:])
    return sq[0, 0], dy, {k: jnp.stack(v) for k, v in gs.items()}


LATE_USE = ("w_up", "w_down")
EARLY_GRADS = ("p_ret", "p_sb", "p_sgu", "w_out", "w_up", "w_down")


class _StepPlan:
    def __init__(self, shards16):
        self.shards16 = shards16
        self.full = [dict() for _ in range(DEPTH)]
        self.gw = [None] * DEPTH
        self.bufs = {}
        self.job_done(run_job(self._gather([(0, "w_in")]), name="gather_first"))

    def weights(self, l):
        return self.full[l]

    def grads(self, l, gw):
        self.gw[l] = gw

    def _gather(self, items):
        self.pending = ("gather", items)
        return gather_job([self.shards16[l][k] for l, k in items], [BIG_AXIS[k] for _, k in items])

    def _scatter(self, items):
        self.pending = ("scatter", [(l, k) for l, k, _ in items])
        return scatter_job([l for l, _, _ in items], [g[1] for _, _, g in items], [g[0] for _, _, g in items],
                           [BIG_AXIS[k] for _, k, _ in items], [self.bufs.get(k) for _, k, _ in items])

    def fwd_job(self, l):
        items = [(l, k) for k in (BIG[1:] if l == 0 else LATE_USE)]
        if l + 1 < DEPTH:
            items += [(l + 1, k) for k in BIG if k not in LATE_USE]
        return self._gather(items)

    def bwd_job(self, l, ready):
        items = [(l, k, ready[k]) for k in EARLY_GRADS]
        if l + 1 < DEPTH:
            items.append((l + 1, "w_in", self.gw[l + 1]["w_in"]))
        return self._scatter(items)

    def job_done(self, outs):
        kind, items = self.pending
        for a, (l, k) in enumerate(items):
            if kind == "gather":
                self.full[l][k] = outs[a]
            else:
                self.bufs[k] = (outs[a], outs[len(items) + a])

    def finish(self):
        self.job_done(run_job(self._scatter([(0, "w_in", self.gw[0]["w_in"])]), name="scatter_last"))
        return self.bufs


def _flat2(a):
    return a.reshape(-1, a.shape[-1])


def kernel(x, w_in, ret_gn_g, ret_gn_b, sgu_ln_g, sgu_ln_b, sgu_w, sgu_b, p_ret, p_sb, p_sgu, w_out, ln1_g, ln1_b, w_up, w_down, ln2_g, ln2_b, loss_target, m_w_in, m_ret_gn_g, m_ret_gn_b, m_sgu_ln_g, m_sgu_ln_b, m_sgu_w, m_sgu_b, m_p_ret, m_p_sb, m_p_sgu, m_w_out, m_ln1_g, m_ln1_b, m_w_up, m_w_down, m_ln2_g, m_ln2_b, v_w_in, v_ret_gn_g, v_ret_gn_b, v_sgu_ln_g, v_sgu_ln_b, v_sgu_w, v_sgu_b, v_p_ret, v_p_sb, v_p_sgu, v_w_out, v_ln1_g, v_ln1_b, v_w_up, v_w_down, v_ln2_g, v_ln2_b):
    given = dict(locals())
    order = BIG[:1] + SMALL[:6] + BIG[1:5] + SMALL[6:8] + BIG[5:7] + SMALL[8:10]
    L = DEPTH

    shards16 = [{k: _rows_call(lambda a: (a,), [(given[k], l)], [BF16], name=f"cast_{k}_{l}")[0] for k in BIG}
                for l in range(L)]
    plan = _StepPlan(shards16)
    sq, dx, gs = local_step(x[0], loss_target[0], {k: given[k] for k in SMALL}, plan)
    loss = 0.5 * lax.psum(sq, ("x", "y", "c"))

    bufs = plan.finish()
    parts = []
    for a, k in enumerate(BIG):
        o = _flat2(bufs[k][1])
        rv = bufs[k][0].reshape(3, *o.shape)
        (part,) = _rows_call(lambda o_, a_, b_, c_: (((o_ + a_.astype(F32)) + b_.astype(F32)) + c_.astype(F32),),
                             [o, (rv, 0), (rv, 1), (rv, 2)], [F32], name="chip_sum_" + k)
        parts.append(part)
    others = swap_with_sibling(parts)
    out = {}
    for a, k in enumerate(BIG):
        shp = given[k].shape
        res = _rows_call(lambda p_, q_, w_, m_, v_: (p_ + q_,) + _adamw(w_, p_ + q_, m_, v_),
                         [parts[a], others[a], _flat2(given[k]), _flat2(given["m_" + k]), _flat2(given["v_" + k])],
                         [F32] * 4, name="adamw_" + k)
        out[k] = [r.reshape(shp) for r in res]

    def pack(d, pre=""):
        return jnp.concatenate([d[pre + k].reshape(-1) for k in SMALL]).reshape(-1, 128)

    g_small = allreduce_small(pack(gs))
    res = _rows_call(lambda g_, w_, m_, v_: (g_,) + _adamw(w_, g_, m_, v_),
                     [g_small, pack(given), pack(given, "m_"), pack(given, "v_")], [F32] * 4, name="adamw_small", tr=8 * 47)
    off = 0
    for k in SMALL:
        sz = given[k].size
        out[k] = [r.reshape(-1)[off:off + sz].reshape(given[k].shape) for r in res]
        off += sz

    grads = [out[k][0] for k in order]
    deltas = [out[k][1] for k in order]
    new_m = [out[k][2] for k in order]
    new_v = [out[k][3] for k in order]
    return (loss, dx[None], *grads, *deltas, *new_m, *new_v)
```

```python
import functools
import math

import jax
import jax.numpy as jnp
from jax import lax
from jax.experimental import pallas as pl
from jax.experimental.pallas import tpu as pltpu

F32 = jnp.float32
BF16 = jnp.bfloat16

D_MODEL = 1024
SEQ = 4096
DEPTH = 2
CHUNK = 128
RET_HEADS = 4
BRANCH_W = 512
N_IN = 7680
D_FF = 4096
LN_EPS = 1e-5
ROPE_BASE = 10000.0
ALPHA = (2 * DEPTH) ** 0.25
RET_SCALE = 128 ** -0.5
SB_SCALE = 64 ** -0.5
C_RET, C_SB, C_SGU, C_GATE = 0, 2048, 3584, 4608

ADAM_LR, ADAM_B1, ADAM_B2, ADAM_EPS, ADAM_WD, ADAM_STEP = 0.001, 0.9, 0.999, 1e-08, 0.01, 10

N_CHIPS = 4
VMEM_LIMIT = 56 * 1024 * 1024
MESH = pl.DeviceIdType.MESH

NN = ((1,), (0,))
NT = ((1,), (1,))
TN = ((0,), (0,))


def _dot(a, b, dims):
    return lax.dot_general(a, b, (dims, ((), ())), preferred_element_type=F32)


def _params(sem):
    return pltpu.CompilerParams(dimension_semantics=sem, vmem_limit_bytes=VMEM_LIMIT)


def _relu2(h):
    r = jnp.maximum(h, 0.0)
    return r * r


def matmul(a, b, *, mode, tm, tn, tk, outs=((F32, None),), pro=None, epi=None, tiles=(), rows=(), name):
    if mode == "nn":
        (M, K), N = a.shape, b.shape[1]
    elif mode == "nt":
        (M, K), N = a.shape, b.shape[0]
    else:
        (K, M), N = a.shape, b.shape[1]
    tm, tn, tk = min(tm, M), min(tn, N), min(tk, K)
    assert M % tm == 0 and N % tn == 0 and K % tk == 0, (name, M, N, K, tm, tn, tk)
    if mode == "nn":
        a_spec = pl.BlockSpec((tm, tk), lambda i, j, k: (i, k))
        b_spec = pl.BlockSpec((tk, tn), lambda i, j, k: (k, j))
        dims = NN
    elif mode == "nt":
        a_spec = pl.BlockSpec((tm, tk), lambda i, j, k: (i, k))
        b_spec = pl.BlockSpec((tn, tk), lambda i, j, k: (j, k))
        dims = NT
    else:
        a_spec = pl.BlockSpec((tk, tm), lambda i, j, k: (k, i))
        b_spec = pl.BlockSpec((tk, tn), lambda i, j, k: (k, j))
        dims = TN
    nk = K // tk
    nt_, nr, no = len(tiles), len(rows), len(outs)

    def body(a_ref, b_ref, *rest):
        tile_refs = rest[:nt_]
        row_refs = rest[nt_:nt_ + nr]
        out_refs = rest[nt_ + nr:nt_ + nr + no]
        av = a_ref[...]
        if pro is not None:
            av = pro(av)
        p = _dot(av.astype(BF16), b_ref[...].astype(BF16), dims)

        def finish(acc):
            vals = (acc,) * no if epi is None else epi(acc, *[r[...] for r in tile_refs], *[r[...] for r in row_refs])
            for o_ref, v in zip(out_refs, vals):
                o_ref[...] = v.astype(o_ref.dtype)

        if nk == 1:
            finish(p)
        else:
            acc_ref = rest[-1]
            k = pl.program_id(2)

            @pl.when(k == 0)
            def _():
                acc_ref[...] = p

            @pl.when(k > 0)
            def _():
                acc_ref[...] += p

            @pl.when(k == nk - 1)
            def _():
                finish(acc_ref[...])

    out_shape, out_specs = [], []
    for dt, width in outs:
        if width is None:
            out_shape.append(jax.ShapeDtypeStruct((M, N), dt))
            out_specs.append(pl.BlockSpec((tm, tn), lambda i, j, k: (i, j)))
        else:
            assert N == tn
            out_shape.append(jax.ShapeDtypeStruct((M, width), dt))
            out_specs.append(pl.BlockSpec((tm, width), lambda i, j, k: (i, 0)))
    in_specs = [a_spec, b_spec]
    in_specs += [pl.BlockSpec((tm, tn), lambda i, j, k: (i, j)) for _ in tiles]
    in_specs += [pl.BlockSpec((1, tn), lambda i, j, k: (0, j)) for _ in rows]
    res = pl.pallas_call(
        body, name=name, grid=(M // tm, N // tn, nk),
        in_specs=in_specs, out_specs=out_specs, out_shape=out_shape,
        scratch_shapes=[pltpu.VMEM((tm, tn), F32)] if nk > 1 else [],
        compiler_params=_params(("parallel", "parallel", "arbitrary")),
    )(a, b, *tiles, *rows)
    return res[0] if no == 1 else res


def _ln_epi(acc, res, g, b):
    u = ALPHA * res + acc
    mu = jnp.mean(u, axis=-1, keepdims=True)
    xc = u - mu
    var = jnp.mean(xc * xc, axis=-1, keepdims=True)
    rstd = lax.rsqrt(var + LN_EPS)
    xhat = xc * rstd
    return xhat * g + b, xhat, jnp.broadcast_to(rstd, (u.shape[0], 128))


def matmul_ln(a, w, res, g, b, *, pro=None, tk, name):
    n = w.shape[1]
    return matmul(a, w, mode="nn", tm=512, tn=n, tk=tk, pro=pro, epi=_ln_epi, tiles=(res,), rows=(g, b),
                  outs=((F32, None), (F32, None), (F32, 128)), name=name)


def ln_bwd(dy, xhat, rstd, g, *, name):
    T, D = dy.shape
    tm = min(512, T)

    def body(dy_ref, xh_ref, rs_ref, g_ref, du_ref, dg_ref, db_ref):
        dyv, xh = dy_ref[...], xh_ref[...]
        r = rs_ref[:, 0:1]
        dxh = dyv * g_ref[...]
        m1 = jnp.mean(dxh, axis=-1, keepdims=True)
        m2 = jnp.mean(dxh * xh, axis=-1, keepdims=True)
        du_ref[...] = r * (dxh - m1 - xh * m2)

        @pl.when(pl.program_id(0) == 0)
        def _():
            dg_ref[...] = jnp.zeros_like(dg_ref)
            db_ref[...] = jnp.zeros_like(db_ref)

        dg_ref[...] += jnp.sum(dyv * xh, axis=0, keepdims=True)
        db_ref[...] += jnp.sum(dyv, axis=0, keepdims=True)

    row = pl.BlockSpec((tm, D), lambda i: (i, 0))
    vec = pl.BlockSpec((1, D), lambda i: (0, 0))
    return pl.pallas_call(
        body, name=name, grid=(T // tm,),
        in_specs=[row, row, pl.BlockSpec((tm, 128), lambda i: (i, 0)), vec],
        out_specs=[row, vec, vec],
        out_shape=[jax.ShapeDtypeStruct((T, D), F32), jax.ShapeDtypeStruct((1, D), F32), jax.ShapeDtypeStruct((1, D), F32)],
        compiler_params=_params(("arbitrary",)),
    )(dy, xhat, rstd, g)


def loss_head(y, target):
    T, D = y.shape
    tm = min(512, T)

    def body(y_ref, t_ref, dy_ref, s_ref):
        e = y_ref[...] - t_ref[...]
        dy_ref[...] = e * (1.0 / D)

        @pl.when(pl.program_id(0) == 0)
        def _():
            s_ref[...] = jnp.zeros_like(s_ref)

        s_ref[...] += jnp.sum(jnp.mean(e * e, axis=-1, keepdims=True))

    row = pl.BlockSpec((tm, D), lambda i: (i, 0))
    return pl.pallas_call(
        body, name="loss_head", grid=(T // tm,),
        in_specs=[row, row], out_specs=[row, pl.BlockSpec((8, 128), lambda i: (0, 0))],
        out_shape=[jax.ShapeDtypeStruct((T, D), F32), jax.ShapeDtypeStruct((8, 128), F32)],
        compiler_params=_params(("arbitrary",)),
    )(y, target)


def _rope_tables(T):
    half = 64
    inv_freq = ROPE_BASE ** (-jnp.arange(half, dtype=F32) / half)
    ang = jnp.arange(T, dtype=jnp.int32).astype(F32)[:, None] * inv_freq[None, :]
    cos, sin = jnp.cos(ang), jnp.sin(ang)
    return jnp.concatenate([cos, cos], axis=1), jnp.concatenate([-sin, sin], axis=1)


def _ret_consts():
    H = RET_HEADS
    log_g = jnp.log(1.0 - 2.0 ** (-5.0 - jnp.arange(H, dtype=F32)))
    idx = jnp.arange(CHUNK, dtype=F32)
    diff = idx[:, None] - idx[None, :]
    dmat = jnp.where(diff[None] >= 0, jnp.exp(log_g[:, None, None] * diff[None]), 0.0)
    kd = jnp.exp(log_g[:, None] * (CHUNK - 1 - idx)[None, :])
    qd = jnp.exp(log_g[:, None] * (idx + 1.0)[None, :])
    cd = jnp.exp(log_g * CHUNK)
    full = (H, CHUNK, CHUNK)
    return (dmat.astype(F32), jnp.broadcast_to(kd[:, :, None], full), jnp.broadcast_to(qd[:, :, None], full),
            jnp.broadcast_to(cd[:, None, None], full))


def _swap_halves(v):
    return pltpu.roll(v, 64, 1)


def _group_norm(o):
    mu = jnp.mean(o, axis=-1, keepdims=True)
    xc = o - mu
    var = jnp.mean(xc * xc, axis=-1, keepdims=True)
    rstd = lax.rsqrt(var + LN_EPS)
    return xc * rstd, rstd


def ret_fwd(proj, cosf, sinf, consts, gn_g, gn_b, *, name):
    T = proj.shape[0]
    tb = min(512, T)
    nch = tb // CHUNK
    H = RET_HEADS

    def body(p_ref, cos_ref, sin_ref, dm_ref, kd_ref, qd_ref, cd_ref, g_ref, b_ref, out_ref, raw_ref, st_ref, s_ref):
        @pl.when(pl.program_id(0) == 0)
        def _():
            s_ref[...] = jnp.zeros_like(s_ref)

        for c in range(nch):
            r = slice(c * CHUNK, (c + 1) * CHUNK)
            cs, sn = cos_ref[r, :], sin_ref[r, :]
            for h in range(H):
                hc = slice(h * 128, (h + 1) * 128)
                q = p_ref[r, h * 128:(h + 1) * 128]
                k = p_ref[r, 512 + h * 128:512 + (h + 1) * 128]
                v = p_ref[r, 1024 + h * 128:1024 + (h + 1) * 128]
                gt = p_ref[r, 1536 + h * 128:1536 + (h + 1) * 128]
                qr = q * cs + _swap_halves(q) * sn
                kr = (k * cs + _swap_halves(k) * sn) * RET_SCALE
                sprev = s_ref[h]
                st_ref[c, h] = sprev
                qb, kb, vb = qr.astype(BF16), kr.astype(BF16), v.astype(BF16)
                s = _dot(qb, kb, NT) * dm_ref[h]
                o = _dot(s.astype(BF16), vb, NN) + _dot((qr * qd_ref[h]).astype(BF16), sprev.astype(BF16), NN)
                s_ref[h] = sprev * cd_ref[h] + _dot((kr * kd_ref[h]).astype(BF16), vb, TN)
                raw_ref[r, hc] = o
                y, _ = _group_norm(o)
                out_ref[r, hc] = (gt * jax.nn.sigmoid(gt)) * (y * g_ref[:, hc] + b_ref[:, hc])

    cmat = pl.BlockSpec((H, CHUNK, CHUNK), lambda i: (0, 0, 0))
    vec = pl.BlockSpec((1, BRANCH_W), lambda i: (0, 0))
    rope = pl.BlockSpec((tb, 128), lambda i: (i, 0))
    blk = pl.BlockSpec((tb, BRANCH_W), lambda i: (i, 0))
    return pl.pallas_call(
        body, name=name, grid=(T // tb,),
        in_specs=[pl.BlockSpec((tb, 2048), lambda i: (i, 0)), rope, rope, cmat, cmat, cmat, cmat, vec, vec],
        out_specs=[blk, blk, pl.BlockSpec((nch, H, CHUNK, CHUNK), lambda i: (i, 0, 0, 0))],
        out_shape=[jax.ShapeDtypeStruct((T, BRANCH_W), F32), jax.ShapeDtypeStruct((T, BRANCH_W), F32),
                   jax.ShapeDtypeStruct((T // CHUNK, H, CHUNK, CHUNK), F32)],
        scratch_shapes=[pltpu.VMEM((H, CHUNK, CHUNK), F32)],
        compiler_params=_params(("arbitrary",)),
    )(proj, cosf, sinf, *consts, gn_g, gn_b)


def ret_bwd(proj, cosf, sinf, consts, gn_g, gn_b, raw, states, dout, *, name):
    T = proj.shape[0]
    tb = min(512, T)
    nch = tb // CHUNK
    nb = T // tb
    H = RET_HEADS

    def body(p_ref, cos_ref, sin_ref, dm_ref, kd_ref, qd_ref, cd_ref, g_ref, b_ref, raw_ref, st_ref, do_ref,
             dp_ref, dg_ref, db_ref, ds_ref):
        @pl.when(pl.program_id(0) == 0)
        def _():
            ds_ref[...] = jnp.zeros_like(ds_ref)
            dg_ref[...] = jnp.zeros_like(dg_ref)
            db_ref[...] = jnp.zeros_like(db_ref)

        for c in reversed(range(nch)):
            r = slice(c * CHUNK, (c + 1) * CHUNK)
            cs, sn = cos_ref[r, :], sin_ref[r, :]
            for h in range(H):
                hc = slice(h * 128, (h + 1) * 128)
                q = p_ref[r, h * 128:(h + 1) * 128]
                k = p_ref[r, 512 + h * 128:512 + (h + 1) * 128]
                v = p_ref[r, 1024 + h * 128:1024 + (h + 1) * 128]
                gt = p_ref[r, 1536 + h * 128:1536 + (h + 1) * 128]
                qr = q * cs + _swap_halves(q) * sn
                kr = (k * cs + _swap_halves(k) * sn) * RET_SCALE
                sprev = st_ref[c, h]
                gv = g_ref[:, hc]
                y, rstd = _group_norm(raw_ref[r, hc])
                d_out = do_ref[r, hc]
                sg = jax.nn.sigmoid(gt)
                d_gate = d_out * (y * gv + b_ref[:, hc]) * (sg * (1.0 + gt * (1.0 - sg)))
                d_aff = d_out * (gt * sg)
                dg_ref[:, hc] += jnp.sum(d_aff * y, axis=0, keepdims=True)
                db_ref[:, hc] += jnp.sum(d_aff, axis=0, keepdims=True)
                dxh = d_aff * gv
                m1 = jnp.mean(dxh, axis=-1, keepdims=True)
                m2 = jnp.mean(dxh * y, axis=-1, keepdims=True)
                d_o = (rstd * (dxh - m1 - y * m2)).astype(BF16)
                qb, kb, vb = qr.astype(BF16), kr.astype(BF16), v.astype(BF16)
                dm, kd, qd = dm_ref[h], kd_ref[h], qd_ref[h]
                p = (_dot(qb, kb, NT) * dm).astype(BF16)
                dp = (_dot(d_o, vb, NT) * dm).astype(BF16)
                dsn = ds_ref[h]
                dsb = dsn.astype(BF16)
                dq_r = _dot(dp, kb, NN) + _dot(d_o, sprev.astype(BF16), NT) * qd
                dk_r = (_dot(dp, qb, TN) + _dot(vb, dsb, NT) * kd) * RET_SCALE
                d_v = _dot(p, d_o, TN) + _dot((kr * kd).astype(BF16), dsb, NN)
                ds_ref[h] = dsn * cd_ref[h] + _dot((qr * qd).astype(BF16), d_o, TN)
                dp_ref[r, h * 128:(h + 1) * 128] = (dq_r * cs - _swap_halves(dq_r) * sn).astype(BF16)
                dp_ref[r, 512 + h * 128:512 + (h + 1) * 128] = (dk_r * cs - _swap_halves(dk_r) * sn).astype(BF16)
                dp_ref[r, 1024 + h * 128:1024 + (h + 1) * 128] = d_v.astype(BF16)
                dp_ref[r, 1536 + h * 128:1536 + (h + 1) * 128] = d_gate.astype(BF16)

    cmat = pl.BlockSpec((H, CHUNK, CHUNK), lambda i: (0, 0, 0))
    vec = pl.BlockSpec((1, BRANCH_W), lambda i: (0, 0))
    rope = pl.BlockSpec((tb, 128), lambda i: (nb - 1 - i, 0))
    blk = pl.BlockSpec((tb, BRANCH_W), lambda i: (nb - 1 - i, 0))
    wide = pl.BlockSpec((tb, 2048), lambda i: (nb - 1 - i, 0))
    return pl.pallas_call(
        body, name=name, grid=(nb,),
        in_specs=[wide, rope, rope, cmat, cmat, cmat, cmat, vec, vec, blk,
                  pl.BlockSpec((nch, H, CHUNK, CHUNK), lambda i: (nb - 1 - i, 0, 0, 0)), blk],
        out_specs=[wide, vec, vec],
        out_shape=[jax.ShapeDtypeStruct((T, 2048), BF16), jax.ShapeDtypeStruct((1, BRANCH_W), F32),
                   jax.ShapeDtypeStruct((1, BRANCH_W), F32)],
        scratch_shapes=[pltpu.VMEM((H, CHUNK, CHUNK), F32)],
        compiler_params=_params(("arbitrary",)),
    )(proj, cosf, sinf, *consts, gn_g, gn_b, raw, states, dout)


def _sb_masks():
    row = lax.broadcasted_iota(jnp.int32, (CHUNK, CHUNK), 0)
    lane = lax.broadcasted_iota(jnp.int32, (CHUNK, CHUNK), 1)
    return row, lane


SB_QT = 512
SB_DEAD = -105.0


def _pair(v):
    hi = v.astype(BF16)
    return jnp.concatenate([hi, (v - hi.astype(F32)).astype(BF16)], axis=1)


def _sb_consts():
    r = lax.broadcasted_iota(jnp.int32, (256, 256), 0) & 127
    c = lax.broadcasted_iota(jnp.int32, (256, 256), 1)
    ones = c >= 128
    lane = lax.broadcasted_iota(jnp.int32, (CHUNK, CHUNK), 1)
    return (ones | (r > c)).astype(BF16), (ones | (r >= c)).astype(BF16), (lane < 64, lane >= 64)


def _per_head(x, hms):
    return jnp.concatenate([jnp.where(hm, x, 0.0) for hm in hms], axis=0).astype(BF16)


def _sb_logits(qb, kb2, mask2):
    z = _dot(qb, kb2, NT)
    l1p = jnp.log(1.0 + jnp.exp(-jnp.abs(z)))
    lsp = jnp.minimum(z, 0.0) - l1p
    lsn = lsp - z
    if mask2 is not None:
        lsn = jnp.where(mask2, lsn, 0.0)
    return lsp, lsn


def _sb_tile_mask(qt):
    trow = lax.broadcasted_iota(jnp.int32, (qt, 256), 0)
    tlane = lax.broadcasted_iota(jnp.int32, (qt, 256), 1) & 127
    return lambda m: (tlane + m * CHUNK) < trow


def sb_fwd(proj, *, name, job=None):
    T = proj.shape[0]
    qt = min(SB_QT, T)
    nsub = qt // CHUNK
    cb = C_SB // 128

    def body(q_ref, k_ref, v_ref, o_ref):
        u_gt, _, hms = _sb_consts()
        tile_mask = _sb_tile_mask(qt)

        def qtile(i, _):
            rq = pl.ds(pl.multiple_of(i * qt, qt), qt)
            qb = (q_ref[rq, :] * SB_SCALE).astype(BF16)

            def step(j, state, mask2):
                carry, acc = list(state[:2]), state[2]
                rk = pl.ds(pl.multiple_of(j * CHUNK, CHUNK), CHUNK)
                lsp, lsn = _sb_logits(qb, _per_head(k_ref[rk, :], hms), mask2)
                a_b = []
                for h in range(2):
                    hc = slice(h * 128, (h + 1) * 128)
                    r = _dot(_pair(lsn[:, hc]), u_gt, NN)
                    a = jnp.exp(lsp[:, hc] + r[:, :128] + carry[h])
                    if mask2 is not None:
                        a = jnp.where(mask2[:, hc], a, 0.0)
                    carry[h] = carry[h] + r[:, 128:]
                    a_b.append(a.astype(BF16))
                acc = acc + _dot(jnp.concatenate(a_b, axis=1), _per_head(v_ref[rk, :], hms), NN)
                return carry[0], carry[1], acc

            zero = jnp.zeros((qt, 128), F32)
            state = (zero, zero, zero)
            for m in reversed(range(nsub)):
                state = step(i * nsub + m, state, tile_mask(m))

            def live(c):
                return jnp.logical_and(c[0] < i, jnp.maximum(jnp.max(c[1][0]), jnp.max(c[1][1])) > SB_DEAD)

            def blocks(c):
                jj, st = c
                for u in range(nsub):
                    st = step((i - jj) * nsub - 1 - u, st, None)
                return jj + 1, st

            _, state = lax.while_loop(live, blocks, (jnp.int32(0), state))
            o_ref[rq, :] = state[2]
            return 0

        lax.fori_loop(0, T // qt, qtile, 0)

    def col(off):
        return pl.BlockSpec((T, 128), lambda hp: (0, off + hp))

    steps = BRANCH_W // 128
    j = _job_args(job, 3, 1)
    res = pl.pallas_call(
        _hosting(body, job, 3, 1, 0, steps), name=name, grid=(steps,),
        in_specs=[col(cb), col(cb + 4), col(cb + 8)] + j["in_specs"], out_specs=[col(0)] + j["out_specs"],
        out_shape=[jax.ShapeDtypeStruct((T, BRANCH_W), F32)] + j["out_shape"],
        scratch_shapes=j["scratch"], input_output_aliases=j["aliases"],
        compiler_params=_params(("parallel",) if job is None else ("arbitrary",)),
    )(proj, proj, proj, *j["ins"])
    return res[0], list(res[1:])


def sb_bwd(proj, out, dout, *, name, job=None):
    T = proj.shape[0]
    qt = min(SB_QT, T)
    nsub = qt // CHUNK
    cb = C_SB // 128

    def body(q_ref, k_ref, v_ref, o_ref, do_ref, dq_ref, dk_ref, dv_ref, dkt_ref, dvt_ref):
        u_gt, u_ge, hms = _sb_consts()
        tile_mask = _sb_tile_mask(qt)
        tall_lane = lax.broadcasted_iota(jnp.int32, (qt, 128), 1)
        top = lax.broadcasted_iota(jnp.int32, (CHUNK, CHUNK), 0) < 64
        dkt_ref[...] = jnp.zeros_like(dkt_ref)
        dvt_ref[...] = jnp.zeros_like(dvt_ref)

        def qtile(i, _):
            rq = pl.ds(pl.multiple_of(i * qt, qt), qt)
            qs = q_ref[rq, :] * SB_SCALE
            qb, q_t = qs.astype(BF16), qs.T.astype(BF16)
            dov = do_ref[rq, :]
            dob, do_t = dov.astype(BF16), dov.T.astype(BF16)
            prod = dob.astype(F32) * o_ref[rq, :]
            total = [jnp.broadcast_to(jnp.sum(jnp.where(hm, prod, 0.0), axis=1, keepdims=True), (qt, 128))
                     for hm in (tall_lane < 64, tall_lane >= 64)]

            def step(j, state, mask2):
                c_l, c_w, dq = list(state[:2]), list(state[2:4]), state[4]
                rk = pl.ds(pl.multiple_of(j * CHUNK, CHUNK), CHUNK)
                kb2, vb2 = _per_head(k_ref[rk, :], hms), _per_head(v_ref[rk, :], hms)
                lsp, lsn = _sb_logits(qb, kb2, mask2)
                da = _dot(dob, vb2, NT)
                sp = jnp.exp(lsp)
                a_b, dz_b = [], []
                for h in range(2):
                    hc = slice(h * 128, (h + 1) * 128)
                    r = _dot(_pair(lsn[:, hc]), u_gt, NN)
                    a = jnp.exp(lsp[:, hc] + r[:, :128] + c_l[h])
                    if mask2 is not None:
                        a = jnp.where(mask2[:, hc], a, 0.0)
                    c_l[h] = c_l[h] + r[:, 128:]
                    a = a.astype(BF16)
                    w = a.astype(F32) * da[:, hc]
                    r = _dot(_pair(w), u_ge, NN)
                    later_w = r[:, :128] + c_w[h]
                    c_w[h] = c_w[h] + r[:, 128:]
                    dz = w * (1.0 - sp[:, hc]) - sp[:, hc] * (total[h] - later_w)
                    if mask2 is not None:
                        dz = jnp.where(mask2[:, hc], dz, 0.0)
                    a_b.append(a)
                    dz_b.append(dz.astype(BF16))
                a_b, dz_b = jnp.concatenate(a_b, axis=1), jnp.concatenate(dz_b, axis=1)
                dkt = _dot(q_t, dz_b, NN)
                dvt = _dot(do_t, a_b, NN)
                dkt_ref[j] += jnp.where(top, dkt[:, :128], dkt[:, 128:])
                dvt_ref[j] += jnp.where(top, dvt[:, :128], dvt[:, 128:])
                return c_l[0], c_l[1], c_w[0], c_w[1], dq + _dot(dz_b, kb2, NN)

            zero = jnp.zeros((qt, 128), F32)
            state = (zero,) * 5
            for m in reversed(range(nsub)):
                state = step(i * nsub + m, state, tile_mask(m))

            def live(c):
                return jnp.logical_and(c[0] < i, jnp.maximum(jnp.max(c[1][0]), jnp.max(c[1][1])) > SB_DEAD)

            def blocks(c):
                jj, st = c
                for u in range(nsub):
                    st = step((i - jj) * nsub - 1 - u, st, None)
                return jj + 1, st

            _, state = lax.while_loop(live, blocks, (jnp.int32(0), state))
            dq_ref[rq, :] = (state[4] * SB_SCALE).astype(BF16)
            return 0

        lax.fori_loop(0, T // qt, qtile, 0)

        def untranspose(jb, _):
            rk = pl.ds(pl.multiple_of(jb * CHUNK, CHUNK), CHUNK)
            dk_ref[rk, :] = dkt_ref[jb].T.astype(BF16)
            dv_ref[rk, :] = dvt_ref[jb].T.astype(BF16)
            return 0

        lax.fori_loop(0, T // CHUNK, untranspose, 0)

    def col(off):
        return pl.BlockSpec((T, 128), lambda hp: (0, off + hp))

    o16 = jax.ShapeDtypeStruct((T, BRANCH_W), BF16)
    steps = BRANCH_W // 128
    j = _job_args(job, 5, 3)
    acc = pltpu.VMEM((T // CHUNK, CHUNK, CHUNK), F32)
    res = pl.pallas_call(
        _hosting(body, job, 5, 3, 2, steps), name=name, grid=(steps,),
        in_specs=[col(cb), col(cb + 4), col(cb + 8), col(0), col(0)] + j["in_specs"],
        out_specs=[col(0), col(0), col(0)] + j["out_specs"], out_shape=[o16, o16, o16] + j["out_shape"],
        scratch_shapes=[acc, acc] + j["scratch"], input_output_aliases=j["aliases"],
        compiler_params=_params(("parallel",) if job is None else ("arbitrary",)),
    )(proj, proj, proj, out, dout, *j["ins"])
    return res[0], res[1], res[2], list(res[3:])


_G0 = math.sqrt(2.0 / math.pi)
_G1 = 0.044715


def _gelu(x):
    return 0.5 * x * (1.0 + jnp.tanh(_G0 * (x + _G1 * x * x * x)))


def _gelu_grad(x):
    t = jnp.tanh(_G0 * (x + _G1 * x * x * x))
    return 0.5 * (1.0 + t) + 0.5 * x * (1.0 - t * t) * (_G0 * (1.0 + 3.0 * _G1 * x * x))


def _tril():
    row, lane = _sb_masks()
    return row >= lane


def sgu_fwd(proj, ln_g, ln_b, w, bias, *, name):
    T = proj.shape[0]
    tb = min(512, T)
    G = BRANCH_W // 128

    def body(u_ref, v_ref, g_ref, b_ref, w_ref, bias_ref, o_ref):
        vv = _gelu(v_ref[...])
        xh, _ = _group_norm(vv)
        vn = (xh * g_ref[...] + b_ref[...]).astype(BF16)
        tril = _tril()
        for g in range(G):
            wg = jnp.where(tril, w_ref[g], 0.0).astype(BF16)
            gc = slice(g * 128, (g + 1) * 128)
            for c in range(tb // CHUNK):
                r = slice(c * CHUNK, (c + 1) * CHUNK)
                sv = _dot(wg, vn[r, gc], NN) + bias_ref[g]
                o_ref[r, gc] = _gelu(u_ref[r, gc]) * sv

    cu, cv = C_SGU // BRANCH_W, C_SGU // BRANCH_W + 1
    vec = pl.BlockSpec((1, BRANCH_W), lambda i: (0, 0))
    mat = pl.BlockSpec((G, CHUNK, CHUNK), lambda i: (0, 0, 0))
    return pl.pallas_call(
        body, name=name, grid=(T // tb,),
        in_specs=[pl.BlockSpec((tb, BRANCH_W), lambda i: (i, cu)), pl.BlockSpec((tb, BRANCH_W), lambda i: (i, cv)),
                  vec, vec, mat, mat],
        out_specs=pl.BlockSpec((tb, BRANCH_W), lambda i: (i, 0)),
        out_shape=jax.ShapeDtypeStruct((T, BRANCH_W), F32),
        compiler_params=_params(("parallel",)),
    )(proj, proj, ln_g, ln_b, w, bias)


def sgu_bwd(proj, ln_g, ln_b, w, bias, dout, *, name):
    T = proj.shape[0]
    tb = min(512, T)
    G = BRANCH_W // 128

    def body(u_ref, v_ref, g_ref, b_ref, w_ref, bias_ref, do_ref, dp_ref, dw_ref, dbias_ref, dg_ref, db_ref, dvn_ref):
        @pl.when(pl.program_id(0) == 0)
        def _():
            dw_ref[...] = jnp.zeros_like(dw_ref)
            dbias_ref[...] = jnp.zeros_like(dbias_ref)
            dg_ref[...] = jnp.zeros_like(dg_ref)
            db_ref[...] = jnp.zeros_like(db_ref)

        gv = v_ref[...]
        vv = _gelu(gv)
        xh, rstd = _group_norm(vv)
        vn = (xh * g_ref[...] + b_ref[...]).astype(BF16)
        tril = _tril()
        for g in range(G):
            wg = jnp.where(tril, w_ref[g], 0.0).astype(BF16)
            gc = slice(g * 128, (g + 1) * 128)
            for c in range(tb // CHUNK):
                r = slice(c * CHUNK, (c + 1) * CHUNK)
                vn_c = vn[r, gc]
                sv = _dot(wg, vn_c, NN) + bias_ref[g]
                gu = u_ref[r, gc]
                d_o = do_ref[r, gc]
                dp_ref[r, gc] = (d_o * sv * _gelu_grad(gu)).astype(BF16)
                dsv = d_o * _gelu(gu)
                dsv_b = dsv.astype(BF16)
                dvn_ref[r, gc] = _dot(wg, dsv_b, TN)
                dw_ref[g] += jnp.where(tril, _dot(dsv_b, vn_c, NT), 0.0)
                dbias_ref[g] += jnp.broadcast_to(jnp.sum(dsv, axis=1, keepdims=True), (CHUNK, CHUNK))
        dvn = dvn_ref[...]
        dg_ref[...] += jnp.sum(dvn * xh, axis=0, keepdims=True)
        db_ref[...] += jnp.sum(dvn, axis=0, keepdims=True)
        dxh = dvn * g_ref[...]
        m1 = jnp.mean(dxh, axis=-1, keepdims=True)
        m2 = jnp.mean(dxh * xh, axis=-1, keepdims=True)
        dp_ref[:, BRANCH_W:2 * BRANCH_W] = (rstd * (dxh - m1 - xh * m2) * _gelu_grad(gv)).astype(BF16)

    cu, cv = C_SGU // BRANCH_W, C_SGU // BRANCH_W + 1
    vec = pl.BlockSpec((1, BRANCH_W), lambda i: (0, 0))
    mat = pl.BlockSpec((G, CHUNK, CHUNK), lambda i: (0, 0, 0))
    blk = pl.BlockSpec((tb, BRANCH_W), lambda i: (i, 0))
    msh = jax.ShapeDtypeStruct((G, CHUNK, CHUNK), F32)
    vsh = jax.ShapeDtypeStruct((1, BRANCH_W), F32)
    return pl.pallas_call(
        body, name=name, grid=(T // tb,),
        in_specs=[pl.BlockSpec((tb, BRANCH_W), lambda i: (i, cu)), pl.BlockSpec((tb, BRANCH_W), lambda i: (i, cv)),
                  vec, vec, mat, mat, blk],
        out_specs=[pl.BlockSpec((tb, 2 * BRANCH_W), lambda i: (i, 0)), mat, mat, vec, vec],
        out_shape=[jax.ShapeDtypeStruct((T, 2 * BRANCH_W), BF16), msh, msh, vsh, vsh],
        scratch_shapes=[pltpu.VMEM((tb, BRANCH_W), F32)],
        compiler_params=_params(("arbitrary",)),
    )(proj, proj, ln_g, ln_b, w, bias, dout)


def merge_fwd(a1, a2, a3, p1, p2, p3, proj, *, name):
    T = a1.shape[0]
    tm, tn = min(1024, T), 512
    gb = C_GATE // tn

    def body(a1_ref, a2_ref, a3_ref, p1_ref, p2_ref, p3_ref, g1_ref, g2_ref, g3_ref, m_ref, r1_ref, r2_ref, r3_ref):
        m = None
        for a_ref, p_ref, g_ref, r_ref in ((a1_ref, p1_ref, g1_ref, r1_ref), (a2_ref, p2_ref, g2_ref, r2_ref),
                                           (a3_ref, p3_ref, g3_ref, r3_ref)):
            r = _dot(a_ref[...].astype(BF16), p_ref[...], NN)
            r_ref[...] = r
            t = jax.nn.sigmoid(g_ref[...]) * r
            m = t if m is None else m + t
        m_ref[...] = m

    a_spec = pl.BlockSpec((tm, BRANCH_W), lambda i, j: (i, 0))
    p_spec = pl.BlockSpec((BRANCH_W, tn), lambda i, j: (0, j))
    o_spec = pl.BlockSpec((tm, tn), lambda i, j: (i, j))
    osh = jax.ShapeDtypeStruct((T, D_MODEL), F32)
    gates = [pl.BlockSpec((tm, tn), functools.partial(lambda i, j, o: (i, o + j), o=gb + 2 * n)) for n in range(3)]
    return pl.pallas_call(
        body, name=name, grid=(T // tm, D_MODEL // tn),
        in_specs=[a_spec, a_spec, a_spec, p_spec, p_spec, p_spec, *gates],
        out_specs=[o_spec] * 4, out_shape=[osh] * 4,
        compiler_params=_params(("parallel", "parallel")),
    )(a1, a2, a3, p1, p2, p3, proj, proj, proj)


def merge_bwd(dm, r1, r2, r3, proj, *, name):
    T = dm.shape[0]
    tm, tn = min(512, T), 512
    gb = C_GATE // tn

    def body(dm_ref, r1_ref, r2_ref, r3_ref, g1_ref, g2_ref, g3_ref, dr1_ref, dr2_ref, dr3_ref, dg1_ref, dg2_ref, dg3_ref):
        d = dm_ref[...]
        for r_ref, g_ref, dr_ref, dg_ref in ((r1_ref, g1_ref, dr1_ref, dg1_ref), (r2_ref, g2_ref, dr2_ref, dg2_ref),
                                             (r3_ref, g3_ref, dr3_ref, dg3_ref)):
            s = jax.nn.sigmoid(g_ref[...])
            dr_ref[...] = (d * s).astype(BF16)
            dg_ref[...] = (d * r_ref[...] * (s * (1.0 - s))).astype(BF16)

    o_spec = pl.BlockSpec((tm, tn), lambda i, j: (i, j))
    osh = jax.ShapeDtypeStruct((T, D_MODEL), BF16)
    gates = [pl.BlockSpec((tm, tn), functools.partial(lambda i, j, o: (i, o + j), o=gb + 2 * n)) for n in range(3)]
    return pl.pallas_call(
        body, name=name, grid=(T // tm, D_MODEL // tn),
        in_specs=[o_spec] * 4 + gates, out_specs=[o_spec] * 6, out_shape=[osh] * 6,
        compiler_params=_params(("parallel", "parallel")),
    )(dm, r1, r2, r3, proj, proj, proj)


def _rows_call(fn, ins, out_dtypes, *, name, tr=256):
    first = ins[0][0] if isinstance(ins[0], tuple) else ins[0]
    R, C = first.shape[-2:]
    tr = min(tr, R)
    assert R % tr == 0, (name, R, tr)
    arrs, specs = [], []
    for x in ins:
        if isinstance(x, tuple):
            arrs.append(x[0])
            specs.append(pl.BlockSpec((None, tr, C), functools.partial(lambda i, n: (n, i, 0), n=x[1])))
        else:
            arrs.append(x)
            specs.append(pl.BlockSpec((tr, C), lambda i: (i, 0)))
    ni = len(arrs)

    def body(*refs):
        vals = fn(*[r[...] for r in refs[:ni]])
        for o_ref, v in zip(refs[ni:], vals):
            o_ref[...] = v.astype(o_ref.dtype)

    res = pl.pallas_call(
        body, name=name, grid=(R // tr,), in_specs=specs,
        out_specs=[pl.BlockSpec((tr, C), lambda i: (i, 0)) for _ in out_dtypes],
        out_shape=[jax.ShapeDtypeStruct((R, C), dt) for dt in out_dtypes],
        compiler_params=_params(("parallel",)),
    )(*arrs)
    return res


def _adamw(w, g, m, v):
    m2 = ADAM_B1 * m + (1.0 - ADAM_B1) * g
    v2 = ADAM_B2 * v + (1.0 - ADAM_B2) * (g * g)
    m_hat = m2 / (1.0 - ADAM_B1 ** ADAM_STEP)
    v_hat = v2 / (1.0 - ADAM_B2 ** ADAM_STEP)
    delta = -ADAM_LR * (m_hat / (jnp.sqrt(v_hat) + ADAM_EPS) + ADAM_WD * w)
    return delta, m2, v2


def _place():
    return lax.axis_index("x"), lax.axis_index("y"), lax.axis_index("c")


def _chip_peers(x, y, c):
    return [((1 - x, y, c), 2 * (1 - x) + y), ((x, 1 - y, c), 2 * x + 1 - y), ((1 - x, 1 - y, c), 2 * (1 - x) + 1 - y)]


def _shard_of(ref, axis, k, n):
    start = pl.multiple_of(k * n, 128)
    return ref.at[pl.ds(start, n), :] if axis == 0 else ref.at[:, pl.ds(start, n)]


ANY = pl.BlockSpec(memory_space=pl.ANY)


class CopyJob:
    def __init__(self, ins, out_shape, scratch, copies, aliases=None):
        self.ins, self.out_shape, self.scratch, self.copies = list(ins), list(out_shape), list(scratch), copies
        self.aliases = dict(aliases or {})

    def start(self, ins, outs, sems):
        local, remote, _, _ = self.copies(ins, outs, sems)
        for d in local + remote:
            d.start()

    def finish(self, ins, outs, sems):
        local, remote, arrivals, relays = self.copies(ins, outs, sems)
        for needs, sends, _ in relays:
            for d in needs:
                d.wait_recv()
            for d in sends:
                d.start()
        for d in arrivals + [d for _, _, arrives in relays for d in arrives]:
            d.wait_recv()
        for d in remote + [d for _, sends, _ in relays for d in sends]:
            d.wait_send()
        for d in local:
            d.wait()


def run_job(job, *, name):
    ni, no = len(job.ins), len(job.out_shape)

    def body(*refs):
        parts = refs[:ni], refs[ni:ni + no], refs[ni + no:]
        job.start(*parts)
        job.finish(*parts)

    return pl.pallas_call(
        body, name=name, in_specs=[ANY] * ni, out_specs=[ANY] * no, out_shape=job.out_shape,
        scratch_shapes=job.scratch, input_output_aliases=job.aliases,
    )(*job.ins)


def _job_args(job, n_in, n_out):
    if job is None:
        return dict(ins=[], in_specs=[], out_specs=[], out_shape=[], scratch=[], aliases={})
    return dict(ins=job.ins, in_specs=[ANY] * len(job.ins), out_specs=[ANY] * len(job.out_shape),
                out_shape=job.out_shape, scratch=job.scratch,
                aliases={n_in + i: n_out + o for i, o in job.aliases.items()})


def _hosting(body, job, n_in, n_out, n_scratch, steps):
    if job is None:
        return body
    ji, jo = len(job.ins), len(job.out_shape)

    def hosted(*refs):
        o = n_in + ji
        s = o + n_out + jo
        parts = refs[n_in:o], refs[o + n_out:s], refs[s + n_scratch:]

        @pl.when(pl.program_id(0) == 0)
        def _():
            job.start(*parts)

        body(*refs[:n_in], *refs[o:o + n_out], *refs[s:s + n_scratch])

        @pl.when(pl.program_id(0) == steps - 1)
        def _():
            job.finish(*parts)

    return hosted


def _job_sems(n_remote, n_local):
    return [pltpu.SemaphoreType.DMA((n_remote,)), pltpu.SemaphoreType.DMA((n_remote,)), pltpu.SemaphoreType.DMA((n_local,))]


def gather_job(shards, axes):
    na = len(shards)

    def copies(ins, outs, sems):
        send, recv, loc = sems
        x, y, c = _place()
        k = 2 * x + y
        local, remote, relays = [], [], []
        for a in range(na):
            r, n = ins[a].shape[0], ins[a].shape[axes[a]]
            half = r // 2

            def part(kk, cc, a=a, n=n, half=half):
                rows = pl.ds(pl.multiple_of(cc * half + (kk * n if axes[a] == 0 else 0), 8), half)
                return outs[a].at[rows, :] if axes[a] == 0 else outs[a].at[rows, pl.ds(pl.multiple_of(kk * n, 128), n)]

            local.append(pltpu.make_async_copy(ins[a], _shard_of(outs[a], axes[a], k, n), loc.at[a]))
            needs, passes, lands = [], [], []
            for j, (peer, kp) in enumerate(_chip_peers(x, y, c)):
                s = 6 * a + j
                remote.append(pltpu.make_async_remote_copy(
                    ins[a].at[pl.ds(pl.multiple_of(c * half, 8), half), :], part(k, c), send.at[s], recv.at[s],
                    device_id=peer, device_id_type=MESH))
                needs.append(pltpu.make_async_remote_copy(part(kp, c), part(kp, c), send.at[s], recv.at[s],
                                                          device_id=peer, device_id_type=MESH))
                passes.append(pltpu.make_async_remote_copy(part(kp, c), part(kp, c), send.at[s + 3], recv.at[s + 3],
                                                           device_id=(x, y, 1 - c), device_id_type=MESH))
                lands.append(pltpu.make_async_remote_copy(part(kp, 1 - c), part(kp, 1 - c), send.at[s + 3], recv.at[s + 3],
                                                          device_id=(x, y, 1 - c), device_id_type=MESH))
            relays.append((needs, passes, lands))
        return local, remote, [], relays

    out_shape = []
    for a in range(na):
        r, c = shards[a].shape
        out_shape.append(jax.ShapeDtypeStruct((r * N_CHIPS, c) if axes[a] == 0 else (r, c * N_CHIPS), BF16))
    return CopyJob(shards, out_shape, _job_sems(6 * na, na), copies)


def scatter_job(layers, g16, g32, axes, filled):
    na = len(axes)

    def shard_shape(a):
        r, c = g32[a].shape
        return (r // N_CHIPS, c) if axes[a] == 0 else (r, c // N_CHIPS)

    def copies(ins, outs, sems):
        send, recv_sems, loc = sems
        b16, b32 = ins[:na], ins[na:2 * na]
        recv, own = outs[:na], outs[na:]
        x, y, c = _place()
        k = 2 * x + y
        local, remote = [], []
        for a in range(na):
            n = shard_shape(a)[axes[a]]
            local.append(pltpu.make_async_copy(_shard_of(b32[a], axes[a], k, n), own[a].at[layers[a]], loc.at[a]))
            for r, (peer, kp) in enumerate(_chip_peers(x, y, c)):
                remote.append(pltpu.make_async_remote_copy(_shard_of(b16[a], axes[a], kp, n), recv[a].at[r, layers[a]],
                                                           send.at[3 * a + r], recv_sems.at[3 * a + r],
                                                           device_id=peer, device_id_type=MESH))
        return local, remote, remote, []

    out_shape = [jax.ShapeDtypeStruct((3, DEPTH) + shard_shape(a), BF16) for a in range(na)]
    out_shape += [jax.ShapeDtypeStruct((DEPTH,) + shard_shape(a), F32) for a in range(na)]
    ins = list(g16) + list(g32)
    aliases = {}
    for a in range(na):
        if filled[a] is not None:
            aliases[len(ins)] = a
            aliases[len(ins) + 1] = na + a
            ins += list(filled[a])
    return CopyJob(ins, out_shape, _job_sems(3 * na, na), copies, aliases)


def pair_job(g16, g32, axes):
    na = len(axes)
    pieces = [1 if ax == 1 else N_CHIPS for ax in axes]

    def copies(ins, outs, sems):
        send, recv, loc = sems
        b16, b32 = ins[:na], ins[na:]
        theirs, mine = outs[:na], outs[na:]
        x, y, c = _place()
        local, remote = [], []
        s = 0
        for a in range(na):
            rows = g32[a].shape[0] // (2 * pieces[a])
            for kk in range(pieces[a]):
                def src(ref, cc, kk=kk, rows=rows):
                    return ref.at[pl.ds(pl.multiple_of((2 * kk + cc) * rows, 8), rows), :]

                dst = pl.ds(kk * rows, rows)
                local.append(pltpu.make_async_copy(src(b32[a], c), mine[a].at[dst, :], loc.at[s]))
                remote.append(pltpu.make_async_remote_copy(src(b16[a], 1 - c), theirs[a].at[dst, :], send.at[s], recv.at[s],
                                                           device_id=(x, y, 1 - c), device_id_type=MESH))
                s += 1
        return local, remote, remote, []

    out_shape = [jax.ShapeDtypeStruct((g.shape[0] // 2, g.shape[1]), BF16) for g in g32]
    out_shape += [jax.ShapeDtypeStruct((g.shape[0] // 2, g.shape[1]), F32) for g in g32]
    return CopyJob(list(g16) + list(g32), out_shape, _job_sems(sum(pieces), sum(pieces)), copies)


def join_job(halves):
    na = len(halves)

    def copies(ins, outs, sems):
        send, recv, loc = sems
        x, y, c = _place()
        local = [pltpu.make_async_copy(ins[a], outs[a].at[:, c], loc.at[a]) for a in range(na)]
        remote = [pltpu.make_async_remote_copy(ins[a], outs[a].at[:, c], send.at[a], recv.at[a],
                                               device_id=(x, y, 1 - c), device_id_type=MESH) for a in range(na)]
        lands = [pltpu.make_async_remote_copy(ins[a], outs[a].at[:, 1 - c], send.at[a], recv.at[a],
                                              device_id=(x, y, 1 - c), device_id_type=MESH) for a in range(na)]
        return local, remote, lands, []

    out_shape = [jax.ShapeDtypeStruct((h.shape[0], 2) + h.shape[1:], F32) for h in halves]
    return CopyJob(halves, out_shape, _job_sems(na, na), copies)


def allreduce_small(p):
    R = p.shape[0]

    def body(p_ref, o_ref, buf, send_sems, recv_sems):
        x, y, c = _place()
        me = 4 * x + 2 * y + c
        cps = []
        for rel in range(1, 8):
            dx, dy, dc = rel >> 2, (rel >> 1) & 1, rel & 1
            peer = (1 - x if dx else x, 1 - y if dy else y, 1 - c if dc else c)
            cp = pltpu.make_async_remote_copy(p_ref, buf.at[me], send_sems.at[rel - 1], recv_sems.at[rel - 1],
                                              device_id=peer, device_id_type=MESH)
            cp.start()
            cps.append((cp, 4 * peer[0] + 2 * peer[1] + peer[2]))
        buf[me] = p_ref[...]
        for rel, (cp, who) in enumerate(cps):
            pltpu.make_async_remote_copy(p_ref, buf.at[who], send_sems.at[rel], recv_sems.at[rel],
                                         device_id=(x, y, c), device_id_type=MESH).wait_recv()
        acc = buf[0]
        for d in range(1, 8):
            acc = acc + buf[d]
        o_ref[...] = acc
        for cp, _ in cps:
            cp.wait_send()

    return pl.pallas_call(
        body, name="allreduce_small",
        in_specs=[pl.BlockSpec(memory_space=pltpu.VMEM)], out_specs=pl.BlockSpec(memory_space=pltpu.VMEM),
        out_shape=jax.ShapeDtypeStruct((R, 128), F32),
        scratch_shapes=[pltpu.VMEM((8, R, 128), F32), pltpu.SemaphoreType.DMA((7,)), pltpu.SemaphoreType.DMA((7,))],
        compiler_params=pltpu.CompilerParams(vmem_limit_bytes=VMEM_LIMIT),
    )(p)


BIG = ("w_in", "p_ret", "p_sb", "p_sgu", "w_out", "w_up", "w_down")
BIG_AXIS = {"w_in": 1, "p_ret": 1, "p_sb": 1, "p_sgu": 1, "w_out": 0, "w_up": 1, "w_down": 0}
SMALL = ("ret_gn_g", "ret_gn_b", "sgu_ln_g", "sgu_ln_b", "sgu_w", "sgu_b", "ln1_g", "ln1_b", "ln2_g", "ln2_b")


def layer_forward(l, x0, W, sm, rope, rconsts, job=None, job_done=None):
    n = f"l{l}_"
    proj = matmul(x0, W["w_in"], mode="nn", tm=1024, tn=640, tk=1024, name=n + "proj")
    retg, raw, states = ret_fwd(proj, *rope, rconsts, sm["ret_gn_g"], sm["ret_gn_b"], name=n + "ret_fwd")
    sb, job_out = sb_fwd(proj, name=n + "sb_fwd", job=job)
    if job is not None:
        job_done(job_out)
    sg = sgu_fwd(proj, sm["sgu_ln_g"], sm["sgu_ln_b"], sm["sgu_w"], sm["sgu_bias"], name=n + "sgu_fwd")
    merged, r1, r2, r3 = merge_fwd(retg, sb, sg, W["p_ret"], W["p_sb"], W["p_sgu"], proj, name=n + "merge_fwd")
    x1, xh1, rs1 = matmul_ln(merged, W["w_out"], x0, sm["ln1_g"], sm["ln1_b"], tk=1024, name=n + "out_ln1")
    h1 = matmul(x1, W["w_up"], mode="nn", tm=1024, tn=1024, tk=1024, name=n + "up")
    x2, xh2, rs2 = matmul_ln(h1, W["w_down"], x1, sm["ln2_g"], sm["ln2_b"], pro=_relu2, tk=1024, name=n + "down_ln2")
    saved = dict(x0=x0, proj=proj, retg=retg, raw=raw, states=states, sb=sb, sg=sg, merged=merged, r=(r1, r2, r3),
                 x1=x1, xh1=xh1, rs1=rs1, h1=h1, xh2=xh2, rs2=rs2)
    return x2, saved


def layer_backward(l, dx2, s, W, sm, rope, rconsts, make_job=None, job_done=None):
    n = f"l{l}_"
    two = ((F32, None), (BF16, None))
    gw, gs = {}, {}
    du2, gs["ln2_g"], gs["ln2_b"] = ln_bwd(dx2, s["xh2"], s["rs2"], sm["ln2_g"], name=n + "ln2_bwd")
    gw["w_down"] = matmul(s["h1"], du2, mode="tn", tm=1024, tn=1024, tk=512, pro=_relu2, outs=two, name=n + "g_down")
    dh1 = matmul(du2, W["w_down"], mode="nt", tm=1024, tn=1024, tk=1024, outs=((BF16, None),),
                 epi=lambda acc, h: (acc * (2.0 * jnp.maximum(h, 0.0)),), tiles=(s["h1"],), name=n + "d_h1")
    gw["w_up"] = matmul(s["x1"], dh1, mode="tn", tm=1024, tn=1024, tk=512, outs=two, name=n + "g_up")
    dx1 = matmul(dh1, W["w_up"], mode="nt", tm=1024, tn=1024, tk=1024,
                 epi=lambda acc, d: (acc + ALPHA * d,), tiles=(du2,), name=n + "d_x1")
    du1, gs["ln1_g"], gs["ln1_b"] = ln_bwd(dx1, s["xh1"], s["rs1"], sm["ln1_g"], name=n + "ln1_bwd")
    gw["w_out"] = matmul(s["merged"], du1, mode="tn", tm=1024, tn=1024, tk=512, outs=two, name=n + "g_out")
    dmerged = matmul(du1, W["w_out"], mode="nt", tm=1024, tn=1024, tk=1024, name=n + "d_merged")
    dr1, dr2, dr3, dg1, dg2, dg3 = merge_bwd(dmerged, *s["r"], s["proj"], name=n + "merge_bwd")
    d_branch = {}
    for nm, a, dr in (("p_ret", s["retg"], dr1), ("p_sb", s["sb"], dr2), ("p_sgu", s["sg"], dr3)):
        gw[nm] = matmul(a, dr, mode="tn", tm=512, tn=1024, tk=512, outs=two, name=n + "g_" + nm)
        d_branch[nm] = matmul(dr, W[nm], mode="nt", tm=1024, tn=512, tk=1024, name=n + "d_" + nm)
    dret, gs["ret_gn_g"], gs["ret_gn_b"] = ret_bwd(s["proj"], *rope, rconsts, sm["ret_gn_g"], sm["ret_gn_b"], s["raw"],
                                                    s["states"], d_branch["p_ret"], name=n + "ret_bwd")
    job = make_job(gw) if make_job is not None else None
    dsq, dsk, dsv, job_out = sb_bwd(s["proj"], s["sb"], d_branch["p_sb"], name=n + "sb_bwd", job=job)
    if job is not None:
        job_done(job_out)
    dsgu, gs["sgu_w"], dbias, gs["sgu_ln_g"], gs["sgu_ln_b"] = sgu_bwd(
        s["proj"], sm["sgu_ln_g"], sm["sgu_ln_b"], sm["sgu_w"], sm["sgu_bias"], d_branch["p_sgu"], name=n + "sgu_bwd")
    gs["sgu_b"] = dbias[:, :, 0]
    dproj = jnp.concatenate([dret, dsq, dsk, dsv, dsgu, dg1, dg2, dg3], axis=1)
    gw["w_in"] = matmul(s["x0"], dproj, mode="tn", tm=1024, tn=1536, tk=512, outs=two, name=n + "g_in")
    dx0 = matmul(dproj, W["w_in"], mode="nt", tm=1024, tn=1024, tk=1536,
                 epi=lambda acc, d: (acc + ALPHA * d,), tiles=(du1,), name=n + "d_x0")
    return dx0, gw, gs


def local_step(x, target, small, plan):
    T = x.shape[0]
    rope = _rope_tables(T)
    rconsts = _ret_consts()
    sms = []
    for l in range(DEPTH):
        sm = {k: small[k][l][None, :] for k in SMALL if k not in ("sgu_w", "sgu_b")}
        sm["sgu_w"] = small["sgu_w"][l]
        sm["sgu_bias"] = jnp.broadcast_to(small["sgu_b"][l][:, :, None], (4, CHUNK, CHUNK))
        sms.append(sm)
    h, saved = x, []
    for l in range(DEPTH):
        h, s = layer_forward(l, h, plan.weights(l), sms[l], rope, rconsts, plan.fwd_job(l), plan.job_done)
        saved.append(s)
    dy, sq = loss_head(h, target)
    gs = {k: [None] * DEPTH for k in SMALL}
    for l in reversed(range(DEPTH)):
        dy, gwl, gsl = layer_backward(l, dy, saved[l], plan.weights(l), sms[l], rope, rconsts,
                                      functools.partial(plan.bwd_job, l), plan.job_done)
        plan.grads(l, gwl)
        for k in SMALL:
            gs[k][l] = gsl[k].reshape(small[k].shape[1:])
    return sq[0, 0], dy, {k: jnp.stack(v) for k, v in gs.items()}


LATE_USE = ("w_up", "w_down")
EARLY_GRADS = ("p_ret", "p_sb", "p_sgu", "w_out", "w_up", "w_down")


class _StepPlan:
    def __init__(self, shards16):
        self.shards16 = shards16
        self.full = [dict() for _ in range(DEPTH)]
        self.gw = [None] * DEPTH
        self.bufs = {}
        self.job_done(run_job(self._gather([(0, "w_in")]), name="gather_first"))

    def weights(self, l):
        return self.full[l]

    def grads(self, l, gw):
        self.gw[l] = gw

    def _gather(self, items):
        self.pending = ("gather", items)
        return gather_job([self.shards16[l][k] for l, k in items], [BIG_AXIS[k] for _, k in items])

    def _scatter(self, items, tag):
        n = len(items)
        axes = [BIG_AXIS[k] for _, k, _ in items]
        got = run_job(pair_job([g[1] for _, _, g in items], [g[0] for _, _, g in items], axes), name="pair_" + tag)
        sums = [_rows_call(lambda theirs, mine: (mine + theirs.astype(F32),) * 2, [got[a], got[n + a]], [F32, BF16],
                           name=f"pair_sum_{k}_{l}") for a, (l, k, _) in enumerate(items)]
        self.pending = ("scatter", [(l, k) for l, k, _ in items])
        return scatter_job([l for l, _, _ in items], [s[1] for s in sums], [s[0] for s in sums], axes,
                           [self.bufs.get(k) for _, k, _ in items])

    def fwd_job(self, l):
        items = [(l, k) for k in (BIG[1:] if l == 0 else LATE_USE)]
        if l + 1 < DEPTH:
            items += [(l + 1, k) for k in BIG if k not in LATE_USE]
        return self._gather(items)

    def bwd_job(self, l, ready):
        items = [(l, k, ready[k]) for k in EARLY_GRADS]
        if l + 1 < DEPTH:
            items.append((l + 1, "w_in", self.gw[l + 1]["w_in"]))
        return self._scatter(items, f"l{l}")

    def job_done(self, outs):
        kind, items = self.pending
        for a, (l, k) in enumerate(items):
            if kind == "gather":
                self.full[l][k] = outs[a]
            else:
                self.bufs[k] = (outs[a], outs[len(items) + a])

    def finish(self):
        self.job_done(run_job(self._scatter([(0, "w_in", self.gw[0]["w_in"])], "last"), name="scatter_last"))
        return self.bufs


def _flat2(a):
    return a.reshape(-1, a.shape[-1])


def kernel(x, w_in, ret_gn_g, ret_gn_b, sgu_ln_g, sgu_ln_b, sgu_w, sgu_b, p_ret, p_sb, p_sgu, w_out, ln1_g, ln1_b, w_up, w_down, ln2_g, ln2_b, loss_target, m_w_in, m_ret_gn_g, m_ret_gn_b, m_sgu_ln_g, m_sgu_ln_b, m_sgu_w, m_sgu_b, m_p_ret, m_p_sb, m_p_sgu, m_w_out, m_ln1_g, m_ln1_b, m_w_up, m_w_down, m_ln2_g, m_ln2_b, v_w_in, v_ret_gn_g, v_ret_gn_b, v_sgu_ln_g, v_sgu_ln_b, v_sgu_w, v_sgu_b, v_p_ret, v_p_sb, v_p_sgu, v_w_out, v_ln1_g, v_ln1_b, v_w_up, v_w_down, v_ln2_g, v_ln2_b):
    given = dict(locals())
    order = BIG[:1] + SMALL[:6] + BIG[1:5] + SMALL[6:8] + BIG[5:7] + SMALL[8:10]
    L = DEPTH

    shards16 = [{k: _rows_call(lambda a: (a,), [(given[k], l)], [BF16], name=f"cast_{k}_{l}")[0] for k in BIG}
                for l in range(L)]
    plan = _StepPlan(shards16)
    sq, dx, gs = local_step(x[0], loss_target[0], {k: given[k] for k in SMALL}, plan)
    loss = 0.5 * lax.psum(sq, ("x", "y", "c"))

    bufs = plan.finish()
    halves = []
    for k in BIG:
        o = _flat2(bufs[k][1])
        rv = bufs[k][0].reshape(3, *o.shape)
        (half,) = _rows_call(lambda o_, a_, b_, c_: (((o_ + a_.astype(F32)) + b_.astype(F32)) + c_.astype(F32),),
                             [o, (rv, 0), (rv, 1), (rv, 2)], [F32], name="chip_sum_" + k)
        halves.append(half.reshape(bufs[k][1].shape))
    joined = run_job(join_job(halves), name="join_halves")
    out = {}
    for a, k in enumerate(BIG):
        shp = given[k].shape
        res = _rows_call(lambda g_, w_, m_, v_: (g_,) + _adamw(w_, g_, m_, v_),
                         [joined[a].reshape(-1, shp[-1]), _flat2(given[k]), _flat2(given["m_" + k]), _flat2(given["v_" + k])],
                         [F32] * 4, name="adamw_" + k)
        out[k] = [r.reshape(shp) for r in res]

    def pack(d, pre=""):
        return jnp.concatenate([d[pre + k].reshape(-1) for k in SMALL]).reshape(-1, 128)

    g_small = allreduce_small(pack(gs))
    res = _rows_call(lambda g_, w_, m_, v_: (g_,) + _adamw(w_, g_, m_, v_),
                     [g_small, pack(given), pack(given, "m_"), pack(given, "v_")], [F32] * 4, name="adamw_small", tr=8 * 47)
    off = 0
    for k in SMALL:
        sz = given[k].size
        out[k] = [r.reshape(-1)[off:off + sz].reshape(given[k].shape) for r in res]
        off += sz

    grads = [out[k][0] for k in order]
    deltas = [out[k][1] for k in order]
    new_m = [out[k][2] for k in order]
    new_v = [out[k][3] for k in order]
    return (loss, dx[None], *grads, *deltas, *new_m, *new_v)
```

```python
import functools
import math

import jax
import jax.numpy as jnp
from jax import lax
from jax.experimental import pallas as pl
from jax.experimental.pallas import tpu as pltpu

F32 = jnp.float32
BF16 = jnp.bfloat16

D_MODEL = 1024
SEQ = 4096
DEPTH = 2
CHUNK = 128
RET_HEADS = 4
BRANCH_W = 512
N_IN = 7680
D_FF = 4096
LN_EPS = 1e-5
ROPE_BASE = 10000.0
ALPHA = (2 * DEPTH) ** 0.25
RET_SCALE = 128 ** -0.5
SB_SCALE = 64 ** -0.5
C_RET, C_SB, C_SGU, C_GATE = 0, 2048, 3584, 4608

ADAM_LR, ADAM_B1, ADAM_B2, ADAM_EPS, ADAM_WD, ADAM_STEP = 0.001, 0.9, 0.999, 1e-08, 0.01, 10

N_CHIPS = 4
VMEM_LIMIT = 56 * 1024 * 1024
MESH = pl.DeviceIdType.MESH

NN = ((1,), (0,))
NT = ((1,), (1,))
TN = ((0,), (0,))


def _dot(a, b, dims):
    return lax.dot_general(a, b, (dims, ((), ())), preferred_element_type=F32)


def _params(sem):
    return pltpu.CompilerParams(dimension_semantics=sem, vmem_limit_bytes=VMEM_LIMIT)


def _relu2(h):
    r = jnp.maximum(h, 0.0)
    return r * r


def matmul(a, b, *, mode, tm, tn, tk, outs=((F32, None),), pro=None, epi=None, tiles=(), rows=(), name):
    if mode == "nn":
        (M, K), N = a.shape, b.shape[1]
    elif mode == "nt":
        (M, K), N = a.shape, b.shape[0]
    else:
        (K, M), N = a.shape, b.shape[1]
    tm, tn, tk = min(tm, M), min(tn, N), min(tk, K)
    assert M % tm == 0 and N % tn == 0 and K % tk == 0, (name, M, N, K, tm, tn, tk)
    if mode == "nn":
        a_spec = pl.BlockSpec((tm, tk), lambda i, j, k: (i, k))
        b_spec = pl.BlockSpec((tk, tn), lambda i, j, k: (k, j))
        dims = NN
    elif mode == "nt":
        a_spec = pl.BlockSpec((tm, tk), lambda i, j, k: (i, k))
        b_spec = pl.BlockSpec((tn, tk), lambda i, j, k: (j, k))
        dims = NT
    else:
        a_spec = pl.BlockSpec((tk, tm), lambda i, j, k: (k, i))
        b_spec = pl.BlockSpec((tk, tn), lambda i, j, k: (k, j))
        dims = TN
    nk = K // tk
    nt_, nr, no = len(tiles), len(rows), len(outs)

    def body(a_ref, b_ref, *rest):
        tile_refs = rest[:nt_]
        row_refs = rest[nt_:nt_ + nr]
        out_refs = rest[nt_ + nr:nt_ + nr + no]
        av = a_ref[...]
        if pro is not None:
            av = pro(av)
        p = _dot(av.astype(BF16), b_ref[...].astype(BF16), dims)

        def finish(acc):
            vals = (acc,) * no if epi is None else epi(acc, *[r[...] for r in tile_refs], *[r[...] for r in row_refs])
            for o_ref, v in zip(out_refs, vals):
                o_ref[...] = v.astype(o_ref.dtype)

        if nk == 1:
            finish(p)
        else:
            acc_ref = rest[-1]
            k = pl.program_id(2)

            @pl.when(k == 0)
            def _():
                acc_ref[...] = p

            @pl.when(k > 0)
            def _():
                acc_ref[...] += p

            @pl.when(k == nk - 1)
            def _():
                finish(acc_ref[...])

    out_shape, out_specs = [], []
    for dt, width in outs:
        if width is None:
            out_shape.append(jax.ShapeDtypeStruct((M, N), dt))
            out_specs.append(pl.BlockSpec((tm, tn), lambda i, j, k: (i, j)))
        else:
            assert N == tn
            out_shape.append(jax.ShapeDtypeStruct((M, width), dt))
            out_specs.append(pl.BlockSpec((tm, width), lambda i, j, k: (i, 0)))
    in_specs = [a_spec, b_spec]
    in_specs += [pl.BlockSpec((tm, tn), lambda i, j, k: (i, j)) for _ in tiles]
    in_specs += [pl.BlockSpec((1, tn), lambda i, j, k: (0, j)) for _ in rows]
    res = pl.pallas_call(
        body, name=name, grid=(M // tm, N // tn, nk),
        in_specs=in_specs, out_specs=out_specs, out_shape=out_shape,
        scratch_shapes=[pltpu.VMEM((tm, tn), F32)] if nk > 1 else [],
        compiler_params=_params(("parallel", "parallel", "arbitrary")),
    )(a, b, *tiles, *rows)
    return res[0] if no == 1 else res


def _ln_epi(acc, res, g, b):
    u = ALPHA * res + acc
    mu = jnp.mean(u, axis=-1, keepdims=True)
    xc = u - mu
    var = jnp.mean(xc * xc, axis=-1, keepdims=True)
    rstd = lax.rsqrt(var + LN_EPS)
    xhat = xc * rstd
    return xhat * g + b, xhat, jnp.broadcast_to(rstd, (u.shape[0], 128))


def matmul_ln(a, w, res, g, b, *, pro=None, tk, name):
    n = w.shape[1]
    return matmul(a, w, mode="nn", tm=512, tn=n, tk=tk, pro=pro, epi=_ln_epi, tiles=(res,), rows=(g, b),
                  outs=((F32, None), (F32, None), (F32, 128)), name=name)


def ln_bwd(dy, xhat, rstd, g, *, name):
    T, D = dy.shape
    tm = min(512, T)

    def body(dy_ref, xh_ref, rs_ref, g_ref, du_ref, dg_ref, db_ref):
        dyv, xh = dy_ref[...], xh_ref[...]
        r = rs_ref[:, 0:1]
        dxh = dyv * g_ref[...]
        m1 = jnp.mean(dxh, axis=-1, keepdims=True)
        m2 = jnp.mean(dxh * xh, axis=-1, keepdims=True)
        du_ref[...] = r * (dxh - m1 - xh * m2)

        @pl.when(pl.program_id(0) == 0)
        def _():
            dg_ref[...] = jnp.zeros_like(dg_ref)
            db_ref[...] = jnp.zeros_like(db_ref)

        dg_ref[...] += jnp.sum(dyv * xh, axis=0, keepdims=True)
        db_ref[...] += jnp.sum(dyv, axis=0, keepdims=True)

    row = pl.BlockSpec((tm, D), lambda i: (i, 0))
    vec = pl.BlockSpec((1, D), lambda i: (0, 0))
    return pl.pallas_call(
        body, name=name, grid=(T // tm,),
        in_specs=[row, row, pl.BlockSpec((tm, 128), lambda i: (i, 0)), vec],
        out_specs=[row, vec, vec],
        out_shape=[jax.ShapeDtypeStruct((T, D), F32), jax.ShapeDtypeStruct((1, D), F32), jax.ShapeDtypeStruct((1, D), F32)],
        compiler_params=_params(("arbitrary",)),
    )(dy, xhat, rstd, g)


def loss_head(y, target):
    T, D = y.shape
    tm = min(512, T)

    def body(y_ref, t_ref, dy_ref, s_ref):
        e = y_ref[...] - t_ref[...]
        dy_ref[...] = e * (1.0 / D)

        @pl.when(pl.program_id(0) == 0)
        def _():
            s_ref[...] = jnp.zeros_like(s_ref)

        s_ref[...] += jnp.sum(jnp.mean(e * e, axis=-1, keepdims=True))

    row = pl.BlockSpec((tm, D), lambda i: (i, 0))
    return pl.pallas_call(
        body, name="loss_head", grid=(T // tm,),
        in_specs=[row, row], out_specs=[row, pl.BlockSpec((8, 128), lambda i: (0, 0))],
        out_shape=[jax.ShapeDtypeStruct((T, D), F32), jax.ShapeDtypeStruct((8, 128), F32)],
        compiler_params=_params(("arbitrary",)),
    )(y, target)


def _rope_tables(T):
    half = 64
    inv_freq = ROPE_BASE ** (-jnp.arange(half, dtype=F32) / half)
    ang = jnp.arange(T, dtype=jnp.int32).astype(F32)[:, None] * inv_freq[None, :]
    cos, sin = jnp.cos(ang), jnp.sin(ang)
    return jnp.concatenate([cos, cos], axis=1), jnp.concatenate([-sin, sin], axis=1)


def _ret_consts():
    H = RET_HEADS
    log_g = jnp.log(1.0 - 2.0 ** (-5.0 - jnp.arange(H, dtype=F32)))
    idx = jnp.arange(CHUNK, dtype=F32)
    diff = idx[:, None] - idx[None, :]
    dmat = jnp.where(diff[None] >= 0, jnp.exp(log_g[:, None, None] * diff[None]), 0.0)
    kd = jnp.exp(log_g[:, None] * (CHUNK - 1 - idx)[None, :])
    qd = jnp.exp(log_g[:, None] * (idx + 1.0)[None, :])
    cd = jnp.exp(log_g * CHUNK)
    full = (H, CHUNK, CHUNK)
    return (dmat.astype(F32), jnp.broadcast_to(kd[:, :, None], full), jnp.broadcast_to(qd[:, :, None], full),
            jnp.broadcast_to(cd[:, None, None], full))


def _swap_halves(v):
    return pltpu.roll(v, 64, 1)


def _group_norm(o):
    mu = jnp.mean(o, axis=-1, keepdims=True)
    xc = o - mu
    var = jnp.mean(xc * xc, axis=-1, keepdims=True)
    rstd = lax.rsqrt(var + LN_EPS)
    return xc * rstd, rstd


def ret_fwd(proj, cosf, sinf, consts, gn_g, gn_b, *, name):
    T = proj.shape[0]
    tb = min(512, T)
    nch = tb // CHUNK
    H = RET_HEADS

    def body(p_ref, cos_ref, sin_ref, dm_ref, kd_ref, qd_ref, cd_ref, g_ref, b_ref, out_ref, raw_ref, st_ref, s_ref):
        @pl.when(pl.program_id(0) == 0)
        def _():
            s_ref[...] = jnp.zeros_like(s_ref)

        for c in range(nch):
            r = slice(c * CHUNK, (c + 1) * CHUNK)
            cs, sn = cos_ref[r, :], sin_ref[r, :]
            for h in range(H):
                hc = slice(h * 128, (h + 1) * 128)
                q = p_ref[r, h * 128:(h + 1) * 128]
                k = p_ref[r, 512 + h * 128:512 + (h + 1) * 128]
                v = p_ref[r, 1024 + h * 128:1024 + (h + 1) * 128]
                gt = p_ref[r, 1536 + h * 128:1536 + (h + 1) * 128]
                qr = q * cs + _swap_halves(q) * sn
                kr = (k * cs + _swap_halves(k) * sn) * RET_SCALE
                sprev = s_ref[h]
                st_ref[c, h] = sprev
                qb, kb, vb = qr.astype(BF16), kr.astype(BF16), v.astype(BF16)
                s = _dot(qb, kb, NT) * dm_ref[h]
                o = _dot(s.astype(BF16), vb, NN) + _dot((qr * qd_ref[h]).astype(BF16), sprev.astype(BF16), NN)
                s_ref[h] = sprev * cd_ref[h] + _dot((kr * kd_ref[h]).astype(BF16), vb, TN)
                raw_ref[r, hc] = o
                y, _ = _group_norm(o)
                out_ref[r, hc] = (gt * jax.nn.sigmoid(gt)) * (y * g_ref[:, hc] + b_ref[:, hc])

    cmat = pl.BlockSpec((H, CHUNK, CHUNK), lambda i: (0, 0, 0))
    vec = pl.BlockSpec((1, BRANCH_W), lambda i: (0, 0))
    rope = pl.BlockSpec((tb, 128), lambda i: (i, 0))
    blk = pl.BlockSpec((tb, BRANCH_W), lambda i: (i, 0))
    return pl.pallas_call(
        body, name=name, grid=(T // tb,),
        in_specs=[pl.BlockSpec((tb, 2048), lambda i: (i, 0)), rope, rope, cmat, cmat, cmat, cmat, vec, vec],
        out_specs=[blk, blk, pl.BlockSpec((nch, H, CHUNK, CHUNK), lambda i: (i, 0, 0, 0))],
        out_shape=[jax.ShapeDtypeStruct((T, BRANCH_W), F32), jax.ShapeDtypeStruct((T, BRANCH_W), F32),
                   jax.ShapeDtypeStruct((T // CHUNK, H, CHUNK, CHUNK), F32)],
        scratch_shapes=[pltpu.VMEM((H, CHUNK, CHUNK), F32)],
        compiler_params=_params(("arbitrary",)),
    )(proj, cosf, sinf, *consts, gn_g, gn_b)


def ret_bwd(proj, cosf, sinf, consts, gn_g, gn_b, raw, states, dout, *, name):
    T = proj.shape[0]
    tb = min(512, T)
    nch = tb // CHUNK
    nb = T // tb
    H = RET_HEADS

    def body(p_ref, cos_ref, sin_ref, dm_ref, kd_ref, qd_ref, cd_ref, g_ref, b_ref, raw_ref, st_ref, do_ref,
             dp_ref, dg_ref, db_ref, ds_ref):
        @pl.when(pl.program_id(0) == 0)
        def _():
            ds_ref[...] = jnp.zeros_like(ds_ref)
            dg_ref[...] = jnp.zeros_like(dg_ref)
            db_ref[...] = jnp.zeros_like(db_ref)

        for c in reversed(range(nch)):
            r = slice(c * CHUNK, (c + 1) * CHUNK)
            cs, sn = cos_ref[r, :], sin_ref[r, :]
            for h in range(H):
                hc = slice(h * 128, (h + 1) * 128)
                q = p_ref[r, h * 128:(h + 1) * 128]
                k = p_ref[r, 512 + h * 128:512 + (h + 1) * 128]
                v = p_ref[r, 1024 + h * 128:1024 + (h + 1) * 128]
                gt = p_ref[r, 1536 + h * 128:1536 + (h + 1) * 128]
                qr = q * cs + _swap_halves(q) * sn
                kr = (k * cs + _swap_halves(k) * sn) * RET_SCALE
                sprev = st_ref[c, h]
                gv = g_ref[:, hc]
                y, rstd = _group_norm(raw_ref[r, hc])
                d_out = do_ref[r, hc]
                sg = jax.nn.sigmoid(gt)
                d_gate = d_out * (y * gv + b_ref[:, hc]) * (sg * (1.0 + gt * (1.0 - sg)))
                d_aff = d_out * (gt * sg)
                dg_ref[:, hc] += jnp.sum(d_aff * y, axis=0, keepdims=True)
                db_ref[:, hc] += jnp.sum(d_aff, axis=0, keepdims=True)
                dxh = d_aff * gv
                m1 = jnp.mean(dxh, axis=-1, keepdims=True)
                m2 = jnp.mean(dxh * y, axis=-1, keepdims=True)
                d_o = (rstd * (dxh - m1 - y * m2)).astype(BF16)
                qb, kb, vb = qr.astype(BF16), kr.astype(BF16), v.astype(BF16)
                dm, kd, qd = dm_ref[h], kd_ref[h], qd_ref[h]
                p = (_dot(qb, kb, NT) * dm).astype(BF16)
                dp = (_dot(d_o, vb, NT) * dm).astype(BF16)
                dsn = ds_ref[h]
                dsb = dsn.astype(BF16)
                dq_r = _dot(dp, kb, NN) + _dot(d_o, sprev.astype(BF16), NT) * qd
                dk_r = (_dot(dp, qb, TN) + _dot(vb, dsb, NT) * kd) * RET_SCALE
                d_v = _dot(p, d_o, TN) + _dot((kr * kd).astype(BF16), dsb, NN)
                ds_ref[h] = dsn * cd_ref[h] + _dot((qr * qd).astype(BF16), d_o, TN)
                dp_ref[r, h * 128:(h + 1) * 128] = (dq_r * cs - _swap_halves(dq_r) * sn).astype(BF16)
                dp_ref[r, 512 + h * 128:512 + (h + 1) * 128] = (dk_r * cs - _swap_halves(dk_r) * sn).astype(BF16)
                dp_ref[r, 1024 + h * 128:1024 + (h + 1) * 128] = d_v.astype(BF16)
                dp_ref[r, 1536 + h * 128:1536 + (h + 1) * 128] = d_gate.astype(BF16)

    cmat = pl.BlockSpec((H, CHUNK, CHUNK), lambda i: (0, 0, 0))
    vec = pl.BlockSpec((1, BRANCH_W), lambda i: (0, 0))
    rope = pl.BlockSpec((tb, 128), lambda i: (nb - 1 - i, 0))
    blk = pl.BlockSpec((tb, BRANCH_W), lambda i: (nb - 1 - i, 0))
    wide = pl.BlockSpec((tb, 2048), lambda i: (nb - 1 - i, 0))
    return pl.pallas_call(
        body, name=name, grid=(nb,),
        in_specs=[wide, rope, rope, cmat, cmat, cmat, cmat, vec, vec, blk,
                  pl.BlockSpec((nch, H, CHUNK, CHUNK), lambda i: (nb - 1 - i, 0, 0, 0)), blk],
        out_specs=[wide, vec, vec],
        out_shape=[jax.ShapeDtypeStruct((T, 2048), BF16), jax.ShapeDtypeStruct((1, BRANCH_W), F32),
                   jax.ShapeDtypeStruct((1, BRANCH_W), F32)],
        scratch_shapes=[pltpu.VMEM((H, CHUNK, CHUNK), F32)],
        compiler_params=_params(("arbitrary",)),
    )(proj, cosf, sinf, *consts, gn_g, gn_b, raw, states, dout)


def _sb_masks():
    row = lax.broadcasted_iota(jnp.int32, (CHUNK, CHUNK), 0)
    lane = lax.broadcasted_iota(jnp.int32, (CHUNK, CHUNK), 1)
    return row, lane


SB_QT = 512
SB_DEAD = -105.0


def _pair(v):
    hi = v.astype(BF16)
    return jnp.concatenate([hi, (v - hi.astype(F32)).astype(BF16)], axis=1)


def _sb_consts():
    r = lax.broadcasted_iota(jnp.int32, (256, 256), 0) & 127
    c = lax.broadcasted_iota(jnp.int32, (256, 256), 1)
    ones = c >= 128
    lane = lax.broadcasted_iota(jnp.int32, (CHUNK, CHUNK), 1)
    return (ones | (r > c)).astype(BF16), (ones | (r >= c)).astype(BF16), (lane < 64, lane >= 64)


def _per_head(x, hms):
    return jnp.concatenate([jnp.where(hm, x, 0.0) for hm in hms], axis=0).astype(BF16)


def _sb_logits(qb, kb2, mask2):
    z = _dot(qb, kb2, NT)
    l1p = jnp.log(1.0 + jnp.exp(-jnp.abs(z)))
    lsp = jnp.minimum(z, 0.0) - l1p
    lsn = lsp - z
    if mask2 is not None:
        lsn = jnp.where(mask2, lsn, 0.0)
    return lsp, lsn


def _sb_tile_mask(qt):
    trow = lax.broadcasted_iota(jnp.int32, (qt, 256), 0)
    tlane = lax.broadcasted_iota(jnp.int32, (qt, 256), 1) & 127
    return lambda m: (tlane + m * CHUNK) < trow


def sb_fwd(proj, *, name, job=None):
    T = proj.shape[0]
    qt = min(SB_QT, T)
    nsub = qt // CHUNK
    cb = C_SB // 128

    def body(q_ref, k_ref, v_ref, o_ref):
        u_gt, _, hms = _sb_consts()
        tile_mask = _sb_tile_mask(qt)

        def qtile(i, _):
            rq = pl.ds(pl.multiple_of(i * qt, qt), qt)
            qb = (q_ref[rq, :] * SB_SCALE).astype(BF16)

            def step(j, state, mask2):
                carry, acc = list(state[:2]), state[2]
                rk = pl.ds(pl.multiple_of(j * CHUNK, CHUNK), CHUNK)
                lsp, lsn = _sb_logits(qb, _per_head(k_ref[rk, :], hms), mask2)
                a_b = []
                for h in range(2):
                    hc = slice(h * 128, (h + 1) * 128)
                    r = _dot(_pair(lsn[:, hc]), u_gt, NN)
                    a = jnp.exp(lsp[:, hc] + r[:, :128] + carry[h])
                    if mask2 is not None:
                        a = jnp.where(mask2[:, hc], a, 0.0)
                    carry[h] = carry[h] + r[:, 128:]
                    a_b.append(a.astype(BF16))
                acc = acc + _dot(jnp.concatenate(a_b, axis=1), _per_head(v_ref[rk, :], hms), NN)
                return carry[0], carry[1], acc

            zero = jnp.zeros((qt, 128), F32)
            state = (zero, zero, zero)
            for m in reversed(range(nsub)):
                state = step(i * nsub + m, state, tile_mask(m))

            def live(c):
                return jnp.logical_and(c[0] < i, jnp.maximum(jnp.max(c[1][0]), jnp.max(c[1][1])) > SB_DEAD)

            def blocks(c):
                jj, st = c
                for u in range(nsub):
                    st = step((i - jj) * nsub - 1 - u, st, None)
                return jj + 1, st

            _, state = lax.while_loop(live, blocks, (jnp.int32(0), state))
            o_ref[rq, :] = state[2]
            return 0

        lax.fori_loop(0, T // qt, qtile, 0)

    def col(off):
        return pl.BlockSpec((T, 128), lambda hp: (0, off + hp))

    steps = BRANCH_W // 128
    j = _job_args(job, 3, 1)
    res = pl.pallas_call(
        _hosting(body, job, 3, 1, 0, steps), name=name, grid=(steps,),
        in_specs=[col(cb), col(cb + 4), col(cb + 8)] + j["in_specs"], out_specs=[col(0)] + j["out_specs"],
        out_shape=[jax.ShapeDtypeStruct((T, BRANCH_W), F32)] + j["out_shape"],
        scratch_shapes=j["scratch"], input_output_aliases=j["aliases"],
        compiler_params=_params(("parallel",) if job is None else ("arbitrary",)),
    )(proj, proj, proj, *j["ins"])
    return res[0], list(res[1:])


def sb_bwd(proj, out, dout, *, name, job=None):
    T = proj.shape[0]
    qt = min(SB_QT, T)
    nsub = qt // CHUNK
    cb = C_SB // 128

    def body(q_ref, k_ref, v_ref, o_ref, do_ref, dq_ref, dk_ref, dv_ref, dkt_ref, dvt_ref):
        u_gt, u_ge, hms = _sb_consts()
        tile_mask = _sb_tile_mask(qt)
        tall_lane = lax.broadcasted_iota(jnp.int32, (qt, 128), 1)
        top = lax.broadcasted_iota(jnp.int32, (CHUNK, CHUNK), 0) < 64
        dkt_ref[...] = jnp.zeros_like(dkt_ref)
        dvt_ref[...] = jnp.zeros_like(dvt_ref)

        def qtile(i, _):
            rq = pl.ds(pl.multiple_of(i * qt, qt), qt)
            qs = q_ref[rq, :] * SB_SCALE
            qb, q_t = qs.astype(BF16), qs.T.astype(BF16)
            dov = do_ref[rq, :]
            dob, do_t = dov.astype(BF16), dov.T.astype(BF16)
            prod = dob.astype(F32) * o_ref[rq, :]
            total = [jnp.broadcast_to(jnp.sum(jnp.where(hm, prod, 0.0), axis=1, keepdims=True), (qt, 128))
                     for hm in (tall_lane < 64, tall_lane >= 64)]

            def step(j, state, mask2):
                c_l, c_w, dq = list(state[:2]), list(state[2:4]), state[4]
                rk = pl.ds(pl.multiple_of(j * CHUNK, CHUNK), CHUNK)
                kb2, vb2 = _per_head(k_ref[rk, :], hms), _per_head(v_ref[rk, :], hms)
                lsp, lsn = _sb_logits(qb, kb2, mask2)
                da = _dot(dob, vb2, NT)
                sp = jnp.exp(lsp)
                a_b, dz_b = [], []
                for h in range(2):
                    hc = slice(h * 128, (h + 1) * 128)
                    r = _dot(_pair(lsn[:, hc]), u_gt, NN)
                    a = jnp.exp(lsp[:, hc] + r[:, :128] + c_l[h])
                    if mask2 is not None:
                        a = jnp.where(mask2[:, hc], a, 0.0)
                    c_l[h] = c_l[h] + r[:, 128:]
                    a = a.astype(BF16)
                    w = a.astype(F32) * da[:, hc]
                    r = _dot(_pair(w), u_ge, NN)
                    later_w = r[:, :128] + c_w[h]
                    c_w[h] = c_w[h] + r[:, 128:]
                    dz = w * (1.0 - sp[:, hc]) - sp[:, hc] * (total[h] - later_w)
                    if mask2 is not None:
                        dz = jnp.where(mask2[:, hc], dz, 0.0)
                    a_b.append(a)
                    dz_b.append(dz.astype(BF16))
                a_b, dz_b = jnp.concatenate(a_b, axis=1), jnp.concatenate(dz_b, axis=1)
                dkt = _dot(q_t, dz_b, NN)
                dvt = _dot(do_t, a_b, NN)
                dkt_ref[j] += jnp.where(top, dkt[:, :128], dkt[:, 128:])
                dvt_ref[j] += jnp.where(top, dvt[:, :128], dvt[:, 128:])
                return c_l[0], c_l[1], c_w[0], c_w[1], dq + _dot(dz_b, kb2, NN)

            zero = jnp.zeros((qt, 128), F32)
            state = (zero,) * 5
            for m in reversed(range(nsub)):
                state = step(i * nsub + m, state, tile_mask(m))

            def live(c):
                return jnp.logical_and(c[0] < i, jnp.maximum(jnp.max(c[1][0]), jnp.max(c[1][1])) > SB_DEAD)

            def blocks(c):
                jj, st = c
                for u in range(nsub):
                    st = step((i - jj) * nsub - 1 - u, st, None)
                return jj + 1, st

            _, state = lax.while_loop(live, blocks, (jnp.int32(0), state))
            dq_ref[rq, :] = (state[4] * SB_SCALE).astype(BF16)
            return 0

        lax.fori_loop(0, T // qt, qtile, 0)

        def untranspose(jb, _):
            rk = pl.ds(pl.multiple_of(jb * CHUNK, CHUNK), CHUNK)
            dk_ref[rk, :] = dkt_ref[jb].T.astype(BF16)
            dv_ref[rk, :] = dvt_ref[jb].T.astype(BF16)
            return 0

        lax.fori_loop(0, T // CHUNK, untranspose, 0)

    def col(off):
        return pl.BlockSpec((T, 128), lambda hp: (0, off + hp))

    o16 = jax.ShapeDtypeStruct((T, BRANCH_W), BF16)
    steps = BRANCH_W // 128
    j = _job_args(job, 5, 3)
    acc = pltpu.VMEM((T // CHUNK, CHUNK, CHUNK), F32)
    res = pl.pallas_call(
        _hosting(body, job, 5, 3, 2, steps), name=name, grid=(steps,),
        in_specs=[col(cb), col(cb + 4), col(cb + 8), col(0), col(0)] + j["in_specs"],
        out_specs=[col(0), col(0), col(0)] + j["out_specs"], out_shape=[o16, o16, o16] + j["out_shape"],
        scratch_shapes=[acc, acc] + j["scratch"], input_output_aliases=j["aliases"],
        compiler_params=_params(("parallel",) if job is None else ("arbitrary",)),
    )(proj, proj, proj, out, dout, *j["ins"])
    return res[0], res[1], res[2], list(res[3:])


_G0 = math.sqrt(2.0 / math.pi)
_G1 = 0.044715


def _gelu(x):
    return 0.5 * x * (1.0 + jnp.tanh(_G0 * (x + _G1 * x * x * x)))


def _gelu_grad(x):
    t = jnp.tanh(_G0 * (x + _G1 * x * x * x))
    return 0.5 * (1.0 + t) + 0.5 * x * (1.0 - t * t) * (_G0 * (1.0 + 3.0 * _G1 * x * x))


def _tril():
    row, lane = _sb_masks()
    return row >= lane


def sgu_fwd(proj, ln_g, ln_b, w, bias, *, name):
    T = proj.shape[0]
    tb = min(512, T)
    G = BRANCH_W // 128

    def body(u_ref, v_ref, g_ref, b_ref, w_ref, bias_ref, o_ref):
        vv = _gelu(v_ref[...])
        xh, _ = _group_norm(vv)
        vn = (xh * g_ref[...] + b_ref[...]).astype(BF16)
        tril = _tril()
        for g in range(G):
            wg = jnp.where(tril, w_ref[g], 0.0).astype(BF16)
            gc = slice(g * 128, (g + 1) * 128)
            for c in range(tb // CHUNK):
                r = slice(c * CHUNK, (c + 1) * CHUNK)
                sv = _dot(wg, vn[r, gc], NN) + bias_ref[g]
                o_ref[r, gc] = _gelu(u_ref[r, gc]) * sv

    cu, cv = C_SGU // BRANCH_W, C_SGU // BRANCH_W + 1
    vec = pl.BlockSpec((1, BRANCH_W), lambda i: (0, 0))
    mat = pl.BlockSpec((G, CHUNK, CHUNK), lambda i: (0, 0, 0))
    return pl.pallas_call(
        body, name=name, grid=(T // tb,),
        in_specs=[pl.BlockSpec((tb, BRANCH_W), lambda i: (i, cu)), pl.BlockSpec((tb, BRANCH_W), lambda i: (i, cv)),
                  vec, vec, mat, mat],
        out_specs=pl.BlockSpec((tb, BRANCH_W), lambda i: (i, 0)),
        out_shape=jax.ShapeDtypeStruct((T, BRANCH_W), F32),
        compiler_params=_params(("parallel",)),
    )(proj, proj, ln_g, ln_b, w, bias)


def sgu_bwd(proj, ln_g, ln_b, w, bias, dout, *, name):
    T = proj.shape[0]
    tb = min(512, T)
    G = BRANCH_W // 128

    def body(u_ref, v_ref, g_ref, b_ref, w_ref, bias_ref, do_ref, dp_ref, dw_ref, dbias_ref, dg_ref, db_ref, dvn_ref):
        @pl.when(pl.program_id(0) == 0)
        def _():
            dw_ref[...] = jnp.zeros_like(dw_ref)
            dbias_ref[...] = jnp.zeros_like(dbias_ref)
            dg_ref[...] = jnp.zeros_like(dg_ref)
            db_ref[...] = jnp.zeros_like(db_ref)

        gv = v_ref[...]
        vv = _gelu(gv)
        xh, rstd = _group_norm(vv)
        vn = (xh * g_ref[...] + b_ref[...]).astype(BF16)
        tril = _tril()
        for g in range(G):
            wg = jnp.where(tril, w_ref[g], 0.0).astype(BF16)
            gc = slice(g * 128, (g + 1) * 128)
            for c in range(tb // CHUNK):
                r = slice(c * CHUNK, (c + 1) * CHUNK)
                vn_c = vn[r, gc]
                sv = _dot(wg, vn_c, NN) + bias_ref[g]
                gu = u_ref[r, gc]
                d_o = do_ref[r, gc]
                dp_ref[r, gc] = (d_o * sv * _gelu_grad(gu)).astype(BF16)
                dsv = d_o * _gelu(gu)
                dsv_b = dsv.astype(BF16)
                dvn_ref[r, gc] = _dot(wg, dsv_b, TN)
                dw_ref[g] += jnp.where(tril, _dot(dsv_b, vn_c, NT), 0.0)
                dbias_ref[g] += jnp.broadcast_to(jnp.sum(dsv, axis=1, keepdims=True), (CHUNK, CHUNK))
        dvn = dvn_ref[...]
        dg_ref[...] += jnp.sum(dvn * xh, axis=0, keepdims=True)
        db_ref[...] += jnp.sum(dvn, axis=0, keepdims=True)
        dxh = dvn * g_ref[...]
        m1 = jnp.mean(dxh, axis=-1, keepdims=True)
        m2 = jnp.mean(dxh * xh, axis=-1, keepdims=True)
        dp_ref[:, BRANCH_W:2 * BRANCH_W] = (rstd * (dxh - m1 - xh * m2) * _gelu_grad(gv)).astype(BF16)

    cu, cv = C_SGU // BRANCH_W, C_SGU // BRANCH_W + 1
    vec = pl.BlockSpec((1, BRANCH_W), lambda i: (0, 0))
    mat = pl.BlockSpec((G, CHUNK, CHUNK), lambda i: (0, 0, 0))
    blk = pl.BlockSpec((tb, BRANCH_W), lambda i: (i, 0))
    msh = jax.ShapeDtypeStruct((G, CHUNK, CHUNK), F32)
    vsh = jax.ShapeDtypeStruct((1, BRANCH_W), F32)
    return pl.pallas_call(
        body, name=name, grid=(T // tb,),
        in_specs=[pl.BlockSpec((tb, BRANCH_W), lambda i: (i, cu)), pl.BlockSpec((tb, BRANCH_W), lambda i: (i, cv)),
                  vec, vec, mat, mat, blk],
        out_specs=[pl.BlockSpec((tb, 2 * BRANCH_W), lambda i: (i, 0)), mat, mat, vec, vec],
        out_shape=[jax.ShapeDtypeStruct((T, 2 * BRANCH_W), BF16), msh, msh, vsh, vsh],
        scratch_shapes=[pltpu.VMEM((tb, BRANCH_W), F32)],
        compiler_params=_params(("arbitrary",)),
    )(proj, proj, ln_g, ln_b, w, bias, dout)


def merge_fwd(a1, a2, a3, p1, p2, p3, proj, *, name):
    T = a1.shape[0]
    tm, tn = min(1024, T), 512
    gb = C_GATE // tn

    def body(a1_ref, a2_ref, a3_ref, p1_ref, p2_ref, p3_ref, g1_ref, g2_ref, g3_ref, m_ref, r1_ref, r2_ref, r3_ref):
        m = None
        for a_ref, p_ref, g_ref, r_ref in ((a1_ref, p1_ref, g1_ref, r1_ref), (a2_ref, p2_ref, g2_ref, r2_ref),
                                           (a3_ref, p3_ref, g3_ref, r3_ref)):
            r = _dot(a_ref[...].astype(BF16), p_ref[...], NN)
            r_ref[...] = r
            t = jax.nn.sigmoid(g_ref[...]) * r
            m = t if m is None else m + t
        m_ref[...] = m

    a_spec = pl.BlockSpec((tm, BRANCH_W), lambda i, j: (i, 0))
    p_spec = pl.BlockSpec((BRANCH_W, tn), lambda i, j: (0, j))
    o_spec = pl.BlockSpec((tm, tn), lambda i, j: (i, j))
    osh = jax.ShapeDtypeStruct((T, D_MODEL), F32)
    gates = [pl.BlockSpec((tm, tn), functools.partial(lambda i, j, o: (i, o + j), o=gb + 2 * n)) for n in range(3)]
    return pl.pallas_call(
        body, name=name, grid=(T // tm, D_MODEL // tn),
        in_specs=[a_spec, a_spec, a_spec, p_spec, p_spec, p_spec, *gates],
        out_specs=[o_spec] * 4, out_shape=[osh] * 4,
        compiler_params=_params(("parallel", "parallel")),
    )(a1, a2, a3, p1, p2, p3, proj, proj, proj)


def merge_bwd(dm, r1, r2, r3, proj, *, name):
    T = dm.shape[0]
    tm, tn = min(512, T), 512
    gb = C_GATE // tn

    def body(dm_ref, r1_ref, r2_ref, r3_ref, g1_ref, g2_ref, g3_ref, dr1_ref, dr2_ref, dr3_ref, dg1_ref, dg2_ref, dg3_ref):
        d = dm_ref[...]
        for r_ref, g_ref, dr_ref, dg_ref in ((r1_ref, g1_ref, dr1_ref, dg1_ref), (r2_ref, g2_ref, dr2_ref, dg2_ref),
                                             (r3_ref, g3_ref, dr3_ref, dg3_ref)):
            s = jax.nn.sigmoid(g_ref[...])
            dr_ref[...] = (d * s).astype(BF16)
            dg_ref[...] = (d * r_ref[...] * (s * (1.0 - s))).astype(BF16)

    o_spec = pl.BlockSpec((tm, tn), lambda i, j: (i, j))
    osh = jax.ShapeDtypeStruct((T, D_MODEL), BF16)
    gates = [pl.BlockSpec((tm, tn), functools.partial(lambda i, j, o: (i, o + j), o=gb + 2 * n)) for n in range(3)]
    return pl.pallas_call(
        body, name=name, grid=(T // tm, D_MODEL // tn),
        in_specs=[o_spec] * 4 + gates, out_specs=[o_spec] * 6, out_shape=[osh] * 6,
        compiler_params=_params(("parallel", "parallel")),
    )(dm, r1, r2, r3, proj, proj, proj)


def _rows_call(fn, ins, out_dtypes, *, name, tr=256):
    first = ins[0][0] if isinstance(ins[0], tuple) else ins[0]
    R, C = first.shape[-2:]
    tr = min(tr, R)
    assert R % tr == 0, (name, R, tr)
    arrs, specs = [], []
    for x in ins:
        if isinstance(x, tuple):
            arrs.append(x[0])
            specs.append(pl.BlockSpec((None, tr, C), functools.partial(lambda i, n: (n, i, 0), n=x[1])))
        else:
            arrs.append(x)
            specs.append(pl.BlockSpec((tr, C), lambda i: (i, 0)))
    ni = len(arrs)

    def body(*refs):
        vals = fn(*[r[...] for r in refs[:ni]])
        for o_ref, v in zip(refs[ni:], vals):
            o_ref[...] = v.astype(o_ref.dtype)

    res = pl.pallas_call(
        body, name=name, grid=(R // tr,), in_specs=specs,
        out_specs=[pl.BlockSpec((tr, C), lambda i: (i, 0)) for _ in out_dtypes],
        out_shape=[jax.ShapeDtypeStruct((R, C), dt) for dt in out_dtypes],
        compiler_params=_params(("parallel",)),
    )(*arrs)
    return res


def _tile_rows(rows, cols):
    t = 256
    while t > 8 and (t * cols > 512 * 1024 or rows % t):
        t //= 2
    return t


def _rows_at(fn, pos, ins, outs, steps, *, name, aliases=None):
    read = [n for n, (_, s) in enumerate(ins) if s is not ANY]
    ni = len(ins)

    def body(pos_ref, *refs):
        vals = fn(*[refs[n][...] for n in read])
        for o_ref, v in zip(refs[ni:], vals):
            o_ref[...] = v.astype(o_ref.dtype)

    return pl.pallas_call(
        body, name=name,
        grid_spec=pltpu.PrefetchScalarGridSpec(num_scalar_prefetch=1, grid=(steps,), in_specs=[s for _, s in ins],
                                               out_specs=[s for _, s in outs]),
        out_shape=[sh for sh, _ in outs],
        input_output_aliases={1 + i: o for i, o in (aliases or {}).items()},
        compiler_params=_params(("parallel",)),
    )(pos, *[a for a, _ in ins])


def cast_into_whole(pos, w, l, axis, *, name):
    _, r, n = w.shape
    tr = _tile_rows(r, n)
    if axis == 1:
        shape, spec = (r, n * N_CHIPS), pl.BlockSpec((tr, n), lambda i, p: (i, p[3]))
    else:
        shape, spec = (r * N_CHIPS, n), pl.BlockSpec((tr, n), lambda i, p: (p[3] * (r // tr) + i, 0))
    return _rows_at(lambda a: (a,), pos, [(w, pl.BlockSpec((None, tr, n), lambda i, p: (l, i, 0)))],
                    [(jax.ShapeDtypeStruct(shape, BF16), spec)], r // tr, name=name)[0]


def pair_sum(pos, theirs, g32, axis, *, name):
    rows2, cols = theirs.shape
    h = rows2 // (N_CHIPS if axis == 0 else 1)
    tr = _tile_rows(h, cols)
    hb = h // tr
    if axis == 1:
        own = pl.BlockSpec((tr, cols), lambda i, p: (p[2] * hb + i, 0))
    else:
        own = pl.BlockSpec((tr, cols), lambda i, p: ((2 * (i // hb) + p[2]) * hb + i % hb, 0))
    row = pl.BlockSpec((tr, cols), lambda i, p: (i, 0))
    return _rows_at(lambda t, m: (m + t.astype(F32),) * 2, pos, [(theirs, row), (g32, own)],
                    [(jax.ShapeDtypeStruct((rows2, cols), F32), row), (jax.ShapeDtypeStruct((rows2, cols), BF16), row)],
                    rows2 // tr, name=name)


def chip_sum(pos, h32, recv, l, axis, whole, *, name):
    _, depth, h, n = recv.shape
    tr = _tile_rows(h, n)
    hb = h // tr
    if axis == 1:
        mine = pl.BlockSpec((tr, n), lambda i, p: (i, p[3]))
    else:
        mine = pl.BlockSpec((tr, n), lambda i, p: (p[3] * hb + i, 0))
    ins = [(h32, mine)] + [(recv, pl.BlockSpec((None, None, tr, n), functools.partial(lambda i, p, j: (j, l, i, 0), j=j)))
                           for j in range(3)]
    if whole is not None:
        ins.append((whole, ANY))
    return _rows_at(lambda o, a, b, c: (((o + a.astype(F32)) + b.astype(F32)) + c.astype(F32),), pos, ins,
                    [(jax.ShapeDtypeStruct((depth, 2, h, n), F32), pl.BlockSpec((None, None, tr, n), lambda i, p: (l, p[2], i, 0)))],
                    hb, name=name, aliases=None if whole is None else {4: 0})[0]


def _adamw(w, g, m, v):
    m2 = ADAM_B1 * m + (1.0 - ADAM_B1) * g
    v2 = ADAM_B2 * v + (1.0 - ADAM_B2) * (g * g)
    m_hat = m2 / (1.0 - ADAM_B1 ** ADAM_STEP)
    v_hat = v2 / (1.0 - ADAM_B2 ** ADAM_STEP)
    delta = -ADAM_LR * (m_hat / (jnp.sqrt(v_hat) + ADAM_EPS) + ADAM_WD * w)
    return delta, m2, v2


def _place():
    return lax.axis_index("x"), lax.axis_index("y"), lax.axis_index("c")


def _chip_peers(x, y, c):
    return [((1 - x, y, c), 2 * (1 - x) + y), ((x, 1 - y, c), 2 * x + 1 - y), ((1 - x, 1 - y, c), 2 * (1 - x) + 1 - y)]


def _shard_of(ref, axis, k, n):
    start = pl.multiple_of(k * n, 128)
    return ref.at[pl.ds(start, n), :] if axis == 0 else ref.at[:, pl.ds(start, n)]


ANY = pl.BlockSpec(memory_space=pl.ANY)


class CopyJob:
    def __init__(self, ins, out_shape, scratch, copies, aliases=None):
        self.ins, self.out_shape, self.scratch, self.copies = list(ins), list(out_shape), list(scratch), copies
        self.aliases = dict(aliases or {})

    def start(self, ins, outs, sems):
        local, remote, _, _ = self.copies(ins, outs, sems)
        for d in local + remote:
            d.start()

    def finish(self, ins, outs, sems):
        local, remote, arrivals, relays = self.copies(ins, outs, sems)
        for needs, sends, _ in relays:
            for d in needs:
                d.wait_recv()
            for d in sends:
                d.start()
        for d in arrivals + [d for _, _, arrives in relays for d in arrives]:
            d.wait_recv()
        for d in remote + [d for _, sends, _ in relays for d in sends]:
            d.wait_send()
        for d in local:
            d.wait()


def run_job(job, *, name):
    ni, no = len(job.ins), len(job.out_shape)

    def body(*refs):
        parts = refs[:ni], refs[ni:ni + no], refs[ni + no:]
        job.start(*parts)
        job.finish(*parts)

    return pl.pallas_call(
        body, name=name, in_specs=[ANY] * ni, out_specs=[ANY] * no, out_shape=job.out_shape,
        scratch_shapes=job.scratch, input_output_aliases=job.aliases,
    )(*job.ins)


def _job_args(job, n_in, n_out):
    if job is None:
        return dict(ins=[], in_specs=[], out_specs=[], out_shape=[], scratch=[], aliases={})
    return dict(ins=job.ins, in_specs=[ANY] * len(job.ins), out_specs=[ANY] * len(job.out_shape),
                out_shape=job.out_shape, scratch=job.scratch,
                aliases={n_in + i: n_out + o for i, o in job.aliases.items()})


def _hosting(body, job, n_in, n_out, n_scratch, steps):
    if job is None:
        return body
    ji, jo = len(job.ins), len(job.out_shape)

    def hosted(*refs):
        o = n_in + ji
        s = o + n_out + jo
        parts = refs[n_in:o], refs[o + n_out:s], refs[s + n_scratch:]

        @pl.when(pl.program_id(0) == 0)
        def _():
            job.start(*parts)

        body(*refs[:n_in], *refs[o:o + n_out], *refs[s:s + n_scratch])

        @pl.when(pl.program_id(0) == steps - 1)
        def _():
            job.finish(*parts)

    return hosted


def _job_sems(n_remote, n_local):
    return [pltpu.SemaphoreType.DMA((n_remote,)), pltpu.SemaphoreType.DMA((n_remote,)), pltpu.SemaphoreType.DMA((n_local,))]


def gather_job(shards, axes):
    na = len(shards)

    def copies(ins, outs, sems):
        send, recv, _ = sems
        x, y, c = _place()
        k = 2 * x + y
        remote, relays = [], []
        for a in range(na):
            r = outs[a].shape[0] // (N_CHIPS if axes[a] == 0 else 1)
            n = outs[a].shape[axes[a]] // N_CHIPS
            half = r // 2

            def part(kk, cc, a=a, n=n, half=half):
                rows = pl.ds(pl.multiple_of(cc * half + (kk * n if axes[a] == 0 else 0), 8), half)
                return outs[a].at[rows, :] if axes[a] == 0 else outs[a].at[rows, pl.ds(pl.multiple_of(kk * n, 128), n)]

            needs, passes, lands = [], [], []
            for j, (peer, kp) in enumerate(_chip_peers(x, y, c)):
                s = 6 * a + j
                remote.append(pltpu.make_async_remote_copy(part(k, c), part(k, c), send.at[s], recv.at[s],
                                                           device_id=peer, device_id_type=MESH))
                needs.append(pltpu.make_async_remote_copy(part(kp, c), part(kp, c), send.at[s], recv.at[s],
                                                          device_id=peer, device_id_type=MESH))
                passes.append(pltpu.make_async_remote_copy(part(kp, c), part(kp, c), send.at[s + 3], recv.at[s + 3],
                                                           device_id=(x, y, 1 - c), device_id_type=MESH))
                lands.append(pltpu.make_async_remote_copy(part(kp, 1 - c), part(kp, 1 - c), send.at[s + 3], recv.at[s + 3],
                                                          device_id=(x, y, 1 - c), device_id_type=MESH))
            relays.append((needs, passes, lands))
        return [], remote, [], relays

    out_shape = [jax.ShapeDtypeStruct(w.shape, BF16) for w in shards]
    return CopyJob(shards, out_shape, _job_sems(6 * na, 1), copies, {a: a for a in range(na)})


def scatter_job(layers, g16, axes, filled):
    na = len(axes)

    def shard_shape(a):
        r, c = g16[a].shape
        return (r // N_CHIPS, c) if axes[a] == 0 else (r, c // N_CHIPS)

    def copies(ins, outs, sems):
        send, recv_sems, _ = sems
        x, y, c = _place()
        remote = []
        for a in range(na):
            n = shard_shape(a)[axes[a]]
            for r, (peer, kp) in enumerate(_chip_peers(x, y, c)):
                remote.append(pltpu.make_async_remote_copy(_shard_of(ins[a], axes[a], kp, n), outs[a].at[r, layers[a]],
                                                           send.at[3 * a + r], recv_sems.at[3 * a + r],
                                                           device_id=peer, device_id_type=MESH))
        return [], remote, remote, []

    out_shape = [jax.ShapeDtypeStruct((3, DEPTH) + shard_shape(a), BF16) for a in range(na)]
    ins = list(g16)
    aliases = {}
    for a in range(na):
        if filled[a] is not None:
            aliases[len(ins)] = a
            ins.append(filled[a])
    return CopyJob(ins, out_shape, _job_sems(3 * na, 1), copies, aliases)


def pair_job(g16, axes):
    na = len(axes)
    pieces = [1 if ax == 1 else N_CHIPS for ax in axes]

    def copies(ins, outs, sems):
        send, recv, _ = sems
        x, y, c = _place()
        remote = []
        s = 0
        for a in range(na):
            rows = g16[a].shape[0] // (2 * pieces[a])
            for kk in range(pieces[a]):
                src = ins[a].at[pl.ds(pl.multiple_of((2 * kk + 1 - c) * rows, 8), rows), :]
                remote.append(pltpu.make_async_remote_copy(src, outs[a].at[pl.ds(kk * rows, rows), :], send.at[s], recv.at[s],
                                                           device_id=(x, y, 1 - c), device_id_type=MESH))
                s += 1
        return [], remote, remote, []

    out_shape = [jax.ShapeDtypeStruct((g.shape[0] // 2, g.shape[1]), BF16) for g in g16]
    return CopyJob(g16, out_shape, _job_sems(sum(pieces), 1), copies)


def join_job(shards):
    na = len(shards)

    def copies(ins, outs, sems):
        send, recv, _ = sems
        x, y, c = _place()
        remote = [pltpu.make_async_remote_copy(outs[a].at[:, c], outs[a].at[:, c], send.at[a], recv.at[a],
                                               device_id=(x, y, 1 - c), device_id_type=MESH) for a in range(na)]
        lands = [pltpu.make_async_remote_copy(outs[a].at[:, 1 - c], outs[a].at[:, 1 - c], send.at[a], recv.at[a],
                                              device_id=(x, y, 1 - c), device_id_type=MESH) for a in range(na)]
        return [], remote, lands, []

    out_shape = [jax.ShapeDtypeStruct(s.shape, F32) for s in shards]
    return CopyJob(shards, out_shape, _job_sems(na, 1), copies, {a: a for a in range(na)})


def allreduce_small(p):
    R = p.shape[0]

    def body(p_ref, o_ref, buf, send_sems, recv_sems):
        x, y, c = _place()
        me = 4 * x + 2 * y + c
        cps = []
        for rel in range(1, 8):
            dx, dy, dc = rel >> 2, (rel >> 1) & 1, rel & 1
            peer = (1 - x if dx else x, 1 - y if dy else y, 1 - c if dc else c)
            cp = pltpu.make_async_remote_copy(p_ref, buf.at[me], send_sems.at[rel - 1], recv_sems.at[rel - 1],
                                              device_id=peer, device_id_type=MESH)
            cp.start()
            cps.append((cp, 4 * peer[0] + 2 * peer[1] + peer[2]))
        buf[me] = p_ref[...]
        for rel, (cp, who) in enumerate(cps):
            pltpu.make_async_remote_copy(p_ref, buf.at[who], send_sems.at[rel], recv_sems.at[rel],
                                         device_id=(x, y, c), device_id_type=MESH).wait_recv()
        acc = buf[0]
        for d in range(1, 8):
            acc = acc + buf[d]
        o_ref[...] = acc
        for cp, _ in cps:
            cp.wait_send()

    return pl.pallas_call(
        body, name="allreduce_small",
        in_specs=[pl.BlockSpec(memory_space=pltpu.VMEM)], out_specs=pl.BlockSpec(memory_space=pltpu.VMEM),
        out_shape=jax.ShapeDtypeStruct((R, 128), F32),
        scratch_shapes=[pltpu.VMEM((8, R, 128), F32), pltpu.SemaphoreType.DMA((7,)), pltpu.SemaphoreType.DMA((7,))],
        compiler_params=pltpu.CompilerParams(vmem_limit_bytes=VMEM_LIMIT),
    )(p)


BIG = ("w_in", "p_ret", "p_sb", "p_sgu", "w_out", "w_up", "w_down")
BIG_AXIS = {"w_in": 1, "p_ret": 1, "p_sb": 1, "p_sgu": 1, "w_out": 0, "w_up": 1, "w_down": 0}
SMALL = ("ret_gn_g", "ret_gn_b", "sgu_ln_g", "sgu_ln_b", "sgu_w", "sgu_b", "ln1_g", "ln1_b", "ln2_g", "ln2_b")


def layer_forward(l, x0, W, sm, rope, rconsts, job=None, job_done=None):
    n = f"l{l}_"
    proj = matmul(x0, W["w_in"], mode="nn", tm=1024, tn=640, tk=1024, name=n + "proj")
    retg, raw, states = ret_fwd(proj, *rope, rconsts, sm["ret_gn_g"], sm["ret_gn_b"], name=n + "ret_fwd")
    sb, job_out = sb_fwd(proj, name=n + "sb_fwd", job=job)
    if job is not None:
        job_done(job_out)
    sg = sgu_fwd(proj, sm["sgu_ln_g"], sm["sgu_ln_b"], sm["sgu_w"], sm["sgu_bias"], name=n + "sgu_fwd")
    merged, r1, r2, r3 = merge_fwd(retg, sb, sg, W["p_ret"], W["p_sb"], W["p_sgu"], proj, name=n + "merge_fwd")
    x1, xh1, rs1 = matmul_ln(merged, W["w_out"], x0, sm["ln1_g"], sm["ln1_b"], tk=1024, name=n + "out_ln1")
    h1 = matmul(x1, W["w_up"], mode="nn", tm=1024, tn=1024, tk=1024, name=n + "up")
    x2, xh2, rs2 = matmul_ln(h1, W["w_down"], x1, sm["ln2_g"], sm["ln2_b"], pro=_relu2, tk=1024, name=n + "down_ln2")
    saved = dict(x0=x0, proj=proj, retg=retg, raw=raw, states=states, sb=sb, sg=sg, merged=merged, r=(r1, r2, r3),
                 x1=x1, xh1=xh1, rs1=rs1, h1=h1, xh2=xh2, rs2=rs2)
    return x2, saved


def layer_backward(l, dx2, s, W, sm, rope, rconsts, make_job=None, job_done=None):
    n = f"l{l}_"
    two = ((F32, None), (BF16, None))
    gw, gs = {}, {}
    du2, gs["ln2_g"], gs["ln2_b"] = ln_bwd(dx2, s["xh2"], s["rs2"], sm["ln2_g"], name=n + "ln2_bwd")
    gw["w_down"] = matmul(s["h1"], du2, mode="tn", tm=1024, tn=1024, tk=512, pro=_relu2, outs=two, name=n + "g_down")
    dh1 = matmul(du2, W["w_down"], mode="nt", tm=1024, tn=1024, tk=1024, outs=((BF16, None),),
                 epi=lambda acc, h: (acc * (2.0 * jnp.maximum(h, 0.0)),), tiles=(s["h1"],), name=n + "d_h1")
    gw["w_up"] = matmul(s["x1"], dh1, mode="tn", tm=1024, tn=1024, tk=512, outs=two, name=n + "g_up")
    dx1 = matmul(dh1, W["w_up"], mode="nt", tm=1024, tn=1024, tk=1024,
                 epi=lambda acc, d: (acc + ALPHA * d,), tiles=(du2,), name=n + "d_x1")
    du1, gs["ln1_g"], gs["ln1_b"] = ln_bwd(dx1, s["xh1"], s["rs1"], sm["ln1_g"], name=n + "ln1_bwd")
    gw["w_out"] = matmul(s["merged"], du1, mode="tn", tm=1024, tn=1024, tk=512, outs=two, name=n + "g_out")
    dmerged = matmul(du1, W["w_out"], mode="nt", tm=1024, tn=1024, tk=1024, name=n + "d_merged")
    dr1, dr2, dr3, dg1, dg2, dg3 = merge_bwd(dmerged, *s["r"], s["proj"], name=n + "merge_bwd")
    d_branch = {}
    for nm, a, dr in (("p_ret", s["retg"], dr1), ("p_sb", s["sb"], dr2), ("p_sgu", s["sg"], dr3)):
        gw[nm] = matmul(a, dr, mode="tn", tm=512, tn=1024, tk=512, outs=two, name=n + "g_" + nm)
        d_branch[nm] = matmul(dr, W[nm], mode="nt", tm=1024, tn=512, tk=1024, name=n + "d_" + nm)
    dret, gs["ret_gn_g"], gs["ret_gn_b"] = ret_bwd(s["proj"], *rope, rconsts, sm["ret_gn_g"], sm["ret_gn_b"], s["raw"],
                                                    s["states"], d_branch["p_ret"], name=n + "ret_bwd")
    job = make_job(gw) if make_job is not None else None
    dsq, dsk, dsv, job_out = sb_bwd(s["proj"], s["sb"], d_branch["p_sb"], name=n + "sb_bwd", job=job)
    if job is not None:
        job_done(job_out)
    dsgu, gs["sgu_w"], dbias, gs["sgu_ln_g"], gs["sgu_ln_b"] = sgu_bwd(
        s["proj"], sm["sgu_ln_g"], sm["sgu_ln_b"], sm["sgu_w"], sm["sgu_bias"], d_branch["p_sgu"], name=n + "sgu_bwd")
    gs["sgu_b"] = dbias[:, :, 0]
    dproj = jnp.concatenate([dret, dsq, dsk, dsv, dsgu, dg1, dg2, dg3], axis=1)
    gw["w_in"] = matmul(s["x0"], dproj, mode="tn", tm=1024, tn=1536, tk=512, outs=two, name=n + "g_in")
    dx0 = matmul(dproj, W["w_in"], mode="nt", tm=1024, tn=1024, tk=1536,
                 epi=lambda acc, d: (acc + ALPHA * d,), tiles=(du1,), name=n + "d_x0")
    return dx0, gw, gs


def local_step(x, target, small, plan):
    T = x.shape[0]
    rope = _rope_tables(T)
    rconsts = _ret_consts()
    sms = []
    for l in range(DEPTH):
        sm = {k: small[k][l][None, :] for k in SMALL if k not in ("sgu_w", "sgu_b")}
        sm["sgu_w"] = small["sgu_w"][l]
        sm["sgu_bias"] = jnp.broadcast_to(small["sgu_b"][l][:, :, None], (4, CHUNK, CHUNK))
        sms.append(sm)
    h, saved = x, []
    for l in range(DEPTH):
        h, s = layer_forward(l, h, plan.weights(l), sms[l], rope, rconsts, plan.fwd_job(l), plan.job_done)
        saved.append(s)
    dy, sq = loss_head(h, target)
    gs = {k: [None] * DEPTH for k in SMALL}
    for l in reversed(range(DEPTH)):
        dy, gwl, gsl = layer_backward(l, dy, saved[l], plan.weights(l), sms[l], rope, rconsts,
                                      functools.partial(plan.bwd_job, l), plan.job_done)
        plan.grads(l, gwl)
        for k in SMALL:
            gs[k][l] = gsl[k].reshape(small[k].shape[1:])
    return sq[0, 0], dy, {k: jnp.stack(v) for k, v in gs.items()}


LATE_USE = ("w_up", "w_down")
EARLY_GRADS = ("p_ret", "p_sb", "p_sgu", "w_out", "w_up", "w_down")


class _StepPlan:
    def __init__(self, pos, shards16):
        self.pos = pos
        self.shards16 = shards16
        self.full = [dict() for _ in range(DEPTH)]
        self.gw = [None] * DEPTH
        self.bufs = {}
        self.sums = {}
        self.job_done(run_job(self._gather([(0, "w_in")]), name="gather_first"))

    def weights(self, l):
        return self.full[l]

    def grads(self, l, gw):
        self.gw[l] = gw

    def _gather(self, items):
        self.pending = ("gather", items)
        return gather_job([self.shards16[l][k] for l, k in items], [BIG_AXIS[k] for _, k in items])

    def _scatter(self, items, tag):
        axes = [BIG_AXIS[k] for _, k, _ in items]
        got = run_job(pair_job([g[1] for _, _, g in items], axes), name="pair_" + tag)
        sums16 = []
        for a, (l, k, g) in enumerate(items):
            self.sums[(l, k)], s16 = pair_sum(self.pos, got[a], g[0], axes[a], name=f"pair_sum_{k}_{l}")
            sums16.append(s16)
        self.pending = ("scatter", [(l, k) for l, k, _ in items])
        return scatter_job([l for l, _, _ in items], sums16, axes, [self.bufs.get(k) for _, k, _ in items])

    def fwd_job(self, l):
        items = [(l, k) for k in (BIG[1:] if l == 0 else LATE_USE)]
        if l + 1 < DEPTH:
            items += [(l + 1, k) for k in BIG if k not in LATE_USE]
        return self._gather(items)

    def bwd_job(self, l, ready):
        items = [(l, k, ready[k]) for k in EARLY_GRADS]
        if l + 1 < DEPTH:
            items.append((l + 1, "w_in", self.gw[l + 1]["w_in"]))
        return self._scatter(items, f"l{l}")

    def job_done(self, outs):
        kind, items = self.pending
        for a, (l, k) in enumerate(items):
            if kind == "gather":
                self.full[l][k] = outs[a]
            else:
                self.bufs[k] = outs[a]

    def finish(self):
        self.job_done(run_job(self._scatter([(0, "w_in", self.gw[0]["w_in"])], "last"), name="scatter_last"))
        return self.bufs, self.sums


def _flat2(a):
    return a.reshape(-1, a.shape[-1])


def kernel(x, w_in, ret_gn_g, ret_gn_b, sgu_ln_g, sgu_ln_b, sgu_w, sgu_b, p_ret, p_sb, p_sgu, w_out, ln1_g, ln1_b, w_up, w_down, ln2_g, ln2_b, loss_target, m_w_in, m_ret_gn_g, m_ret_gn_b, m_sgu_ln_g, m_sgu_ln_b, m_sgu_w, m_sgu_b, m_p_ret, m_p_sb, m_p_sgu, m_w_out, m_ln1_g, m_ln1_b, m_w_up, m_w_down, m_ln2_g, m_ln2_b, v_w_in, v_ret_gn_g, v_ret_gn_b, v_sgu_ln_g, v_sgu_ln_b, v_sgu_w, v_sgu_b, v_p_ret, v_p_sb, v_p_sgu, v_w_out, v_ln1_g, v_ln1_b, v_w_up, v_w_down, v_ln2_g, v_ln2_b):
    given = dict(locals())
    order = BIG[:1] + SMALL[:6] + BIG[1:5] + SMALL[6:8] + BIG[5:7] + SMALL[8:10]
    L = DEPTH

    px, py, pc = _place()
    pos = jnp.stack([px, py, pc, 2 * px + py]).astype(jnp.int32)

    shards16 = [{k: cast_into_whole(pos, given[k], l, BIG_AXIS[k], name=f"cast_{k}_{l}") for k in BIG} for l in range(L)]
    plan = _StepPlan(pos, shards16)
    sq, dx, gs = local_step(x[0], loss_target[0], {k: given[k] for k in SMALL}, plan)
    loss = 0.5 * lax.psum(sq, ("x", "y", "c"))

    bufs, sums = plan.finish()
    shards = []
    for k in BIG:
        whole = None
        for l in range(L):
            whole = chip_sum(pos, sums[(l, k)], bufs[k], l, BIG_AXIS[k], whole, name=f"chip_sum_{k}_{l}")
        shards.append(whole)
    joined = run_job(join_job(shards), name="join_halves")
    out = {}
    for a, k in enumerate(BIG):
        shp = given[k].shape
        res = _rows_call(lambda g_, w_, m_, v_: (g_,) + _adamw(w_, g_, m_, v_),
                         [joined[a].reshape(-1, shp[-1]), _flat2(given[k]), _flat2(given["m_" + k]), _flat2(given["v_" + k])],
                         [F32] * 4, name="adamw_" + k)
        out[k] = [r.reshape(shp) for r in res]

    def pack(d, pre=""):
        return jnp.concatenate([d[pre + k].reshape(-1) for k in SMALL]).reshape(-1, 128)

    g_small = allreduce_small(pack(gs))
    res = _rows_call(lambda g_, w_, m_, v_: (g_,) + _adamw(w_, g_, m_, v_),
                     [g_small, pack(given), pack(given, "m_"), pack(given, "v_")], [F32] * 4, name="adamw_small", tr=8 * 47)
    off = 0
    for k in SMALL:
        sz = given[k].size
        out[k] = [r.reshape(-1)[off:off + sz].reshape(given[k].shape) for r in res]
        off += sz

    grads = [out[k][0] for k in order]
    deltas = [out[k][1] for k in order]
    new_m = [out[k][2] for k in order]
    new_v = [out[k][3] for k in order]
    return (loss, dx[None], *grads, *deltas, *new_m, *new_v)
```

```python
import functools
import math

import jax
import jax.numpy as jnp
from jax import lax
from jax.experimental import pallas as pl
from jax.experimental.pallas import tpu as pltpu

F32 = jnp.float32
BF16 = jnp.bfloat16

D_MODEL = 1024
SEQ = 4096
DEPTH = 2
CHUNK = 128
RET_HEADS = 4
BRANCH_W = 512
N_IN = 7680
D_FF = 4096
LN_EPS = 1e-5
ROPE_BASE = 10000.0
ALPHA = (2 * DEPTH) ** 0.25
RET_SCALE = 128 ** -0.5
SB_SCALE = 64 ** -0.5
C_RET, C_SB, C_SGU, C_GATE = 0, 2048, 3584, 4608

ADAM_LR, ADAM_B1, ADAM_B2, ADAM_EPS, ADAM_WD, ADAM_STEP = 0.001, 0.9, 0.999, 1e-08, 0.01, 10

N_CHIPS = 4
VMEM_LIMIT = 56 * 1024 * 1024
MESH = pl.DeviceIdType.MESH

NN = ((1,), (0,))
NT = ((1,), (1,))
TN = ((0,), (0,))


def _dot(a, b, dims):
    return lax.dot_general(a, b, (dims, ((), ())), preferred_element_type=F32)


def _params(sem):
    return pltpu.CompilerParams(dimension_semantics=sem, vmem_limit_bytes=VMEM_LIMIT)


def _relu2(h):
    r = jnp.maximum(h, 0.0)
    return r * r


def matmul(a, b, *, mode, tm, tn, tk, outs=((F32, None),), pro=None, epi=None, tiles=(), rows=(), name, job=None):
    if mode == "nn":
        (M, K), N = a.shape, b.shape[1]
    elif mode == "nt":
        (M, K), N = a.shape, b.shape[0]
    else:
        (K, M), N = a.shape, b.shape[1]
    tm, tn, tk = min(tm, M), min(tn, N), min(tk, K)
    assert M % tm == 0 and N % tn == 0 and K % tk == 0, (name, M, N, K, tm, tn, tk)
    if mode == "nn":
        a_spec = pl.BlockSpec((tm, tk), lambda i, j, k: (i, k))
        b_spec = pl.BlockSpec((tk, tn), lambda i, j, k: (k, j))
        dims = NN
    elif mode == "nt":
        a_spec = pl.BlockSpec((tm, tk), lambda i, j, k: (i, k))
        b_spec = pl.BlockSpec((tn, tk), lambda i, j, k: (j, k))
        dims = NT
    else:
        a_spec = pl.BlockSpec((tk, tm), lambda i, j, k: (k, i))
        b_spec = pl.BlockSpec((tk, tn), lambda i, j, k: (k, j))
        dims = TN
    nk = K // tk
    nt_, nr, no = len(tiles), len(rows), len(outs)

    def body(a_ref, b_ref, *rest):
        tile_refs = rest[:nt_]
        row_refs = rest[nt_:nt_ + nr]
        out_refs = rest[nt_ + nr:nt_ + nr + no]
        av = a_ref[...]
        if pro is not None:
            av = pro(av)
        p = _dot(av.astype(BF16), b_ref[...].astype(BF16), dims)

        def finish(acc):
            vals = (acc,) * no if epi is None else epi(acc, *[r[...] for r in tile_refs], *[r[...] for r in row_refs])
            for o_ref, v in zip(out_refs, vals):
                o_ref[...] = v.astype(o_ref.dtype)

        if nk == 1:
            finish(p)
        else:
            acc_ref = rest[-1]
            k = pl.program_id(2)

            @pl.when(k == 0)
            def _():
                acc_ref[...] = p

            @pl.when(k > 0)
            def _():
                acc_ref[...] += p

            @pl.when(k == nk - 1)
            def _():
                finish(acc_ref[...])

    out_shape, out_specs = [], []
    for dt, width in outs:
        if width is None:
            out_shape.append(jax.ShapeDtypeStruct((M, N), dt))
            out_specs.append(pl.BlockSpec((tm, tn), lambda i, j, k: (i, j)))
        else:
            assert N == tn
            out_shape.append(jax.ShapeDtypeStruct((M, width), dt))
            out_specs.append(pl.BlockSpec((tm, width), lambda i, j, k: (i, 0)))
    in_specs = [a_spec, b_spec]
    in_specs += [pl.BlockSpec((tm, tn), lambda i, j, k: (i, j)) for _ in tiles]
    in_specs += [pl.BlockSpec((1, tn), lambda i, j, k: (0, j)) for _ in rows]
    grid = (M // tm, N // tn, nk)
    scratch = [pltpu.VMEM((tm, tn), F32)] if nk > 1 else []
    j = _job_args(job, len(in_specs), no)
    res = pl.pallas_call(
        _hosting(body, job, len(in_specs), no, len(scratch), grid), name=name, grid=grid,
        in_specs=in_specs + j["in_specs"], out_specs=out_specs + j["out_specs"], out_shape=out_shape + j["out_shape"],
        scratch_shapes=scratch + j["scratch"], input_output_aliases=j["aliases"],
        compiler_params=_params(("parallel", "parallel", "arbitrary") if job is None else ("arbitrary",) * 3),
    )(a, b, *tiles, *rows, *j["ins"])
    mine = res[0] if no == 1 else list(res[:no])
    return mine if job is None else (mine, list(res[no:]))


def _ln_epi(acc, res, g, b):
    u = ALPHA * res + acc
    mu = jnp.mean(u, axis=-1, keepdims=True)
    xc = u - mu
    var = jnp.mean(xc * xc, axis=-1, keepdims=True)
    rstd = lax.rsqrt(var + LN_EPS)
    xhat = xc * rstd
    return xhat * g + b, xhat, jnp.broadcast_to(rstd, (u.shape[0], 128))


def matmul_ln(a, w, res, g, b, *, pro=None, tk, name):
    n = w.shape[1]
    return matmul(a, w, mode="nn", tm=512, tn=n, tk=tk, pro=pro, epi=_ln_epi, tiles=(res,), rows=(g, b),
                  outs=((F32, None), (F32, None), (F32, 128)), name=name)


def ln_bwd(dy, xhat, rstd, g, *, name):
    T, D = dy.shape
    tm = min(512, T)

    def body(dy_ref, xh_ref, rs_ref, g_ref, du_ref, dg_ref, db_ref):
        dyv, xh = dy_ref[...], xh_ref[...]
        r = rs_ref[:, 0:1]
        dxh = dyv * g_ref[...]
        m1 = jnp.mean(dxh, axis=-1, keepdims=True)
        m2 = jnp.mean(dxh * xh, axis=-1, keepdims=True)
        du_ref[...] = r * (dxh - m1 - xh * m2)

        @pl.when(pl.program_id(0) == 0)
        def _():
            dg_ref[...] = jnp.zeros_like(dg_ref)
            db_ref[...] = jnp.zeros_like(db_ref)

        dg_ref[...] += jnp.sum(dyv * xh, axis=0, keepdims=True)
        db_ref[...] += jnp.sum(dyv, axis=0, keepdims=True)

    row = pl.BlockSpec((tm, D), lambda i: (i, 0))
    vec = pl.BlockSpec((1, D), lambda i: (0, 0))
    return pl.pallas_call(
        body, name=name, grid=(T // tm,),
        in_specs=[row, row, pl.BlockSpec((tm, 128), lambda i: (i, 0)), vec],
        out_specs=[row, vec, vec],
        out_shape=[jax.ShapeDtypeStruct((T, D), F32), jax.ShapeDtypeStruct((1, D), F32), jax.ShapeDtypeStruct((1, D), F32)],
        compiler_params=_params(("arbitrary",)),
    )(dy, xhat, rstd, g)


def loss_head(y, target):
    T, D = y.shape
    tm = min(512, T)

    def body(y_ref, t_ref, dy_ref, s_ref):
        e = y_ref[...] - t_ref[...]
        dy_ref[...] = e * (1.0 / D)

        @pl.when(pl.program_id(0) == 0)
        def _():
            s_ref[...] = jnp.zeros_like(s_ref)

        s_ref[...] += jnp.sum(jnp.mean(e * e, axis=-1, keepdims=True))

    row = pl.BlockSpec((tm, D), lambda i: (i, 0))
    return pl.pallas_call(
        body, name="loss_head", grid=(T // tm,),
        in_specs=[row, row], out_specs=[row, pl.BlockSpec((8, 128), lambda i: (0, 0))],
        out_shape=[jax.ShapeDtypeStruct((T, D), F32), jax.ShapeDtypeStruct((8, 128), F32)],
        compiler_params=_params(("arbitrary",)),
    )(y, target)


def _rope_tables(T):
    half = 64
    inv_freq = ROPE_BASE ** (-jnp.arange(half, dtype=F32) / half)
    ang = jnp.arange(T, dtype=jnp.int32).astype(F32)[:, None] * inv_freq[None, :]
    cos, sin = jnp.cos(ang), jnp.sin(ang)
    return jnp.concatenate([cos, cos], axis=1), jnp.concatenate([-sin, sin], axis=1)


def _ret_consts():
    H = RET_HEADS
    log_g = jnp.log(1.0 - 2.0 ** (-5.0 - jnp.arange(H, dtype=F32)))
    idx = jnp.arange(CHUNK, dtype=F32)
    diff = idx[:, None] - idx[None, :]
    dmat = jnp.where(diff[None] >= 0, jnp.exp(log_g[:, None, None] * diff[None]), 0.0)
    kd = jnp.exp(log_g[:, None] * (CHUNK - 1 - idx)[None, :])
    qd = jnp.exp(log_g[:, None] * (idx + 1.0)[None, :])
    cd = jnp.exp(log_g * CHUNK)
    full = (H, CHUNK, CHUNK)
    return (dmat.astype(F32), jnp.broadcast_to(kd[:, :, None], full), jnp.broadcast_to(qd[:, :, None], full),
            jnp.broadcast_to(cd[:, None, None], full))


def _swap_halves(v):
    return pltpu.roll(v, 64, 1)


def _group_norm(o):
    mu = jnp.mean(o, axis=-1, keepdims=True)
    xc = o - mu
    var = jnp.mean(xc * xc, axis=-1, keepdims=True)
    rstd = lax.rsqrt(var + LN_EPS)
    return xc * rstd, rstd


def ret_fwd(proj, cosf, sinf, consts, gn_g, gn_b, *, name):
    T = proj.shape[0]
    tb = min(512, T)
    nch = tb // CHUNK
    H = RET_HEADS

    def body(p_ref, cos_ref, sin_ref, dm_ref, kd_ref, qd_ref, cd_ref, g_ref, b_ref, out_ref, raw_ref, st_ref, s_ref):
        @pl.when(pl.program_id(0) == 0)
        def _():
            s_ref[...] = jnp.zeros_like(s_ref)

        for c in range(nch):
            r = slice(c * CHUNK, (c + 1) * CHUNK)
            cs, sn = cos_ref[r, :], sin_ref[r, :]
            for h in range(H):
                hc = slice(h * 128, (h + 1) * 128)
                q = p_ref[r, h * 128:(h + 1) * 128]
                k = p_ref[r, 512 + h * 128:512 + (h + 1) * 128]
                v = p_ref[r, 1024 + h * 128:1024 + (h + 1) * 128]
                gt = p_ref[r, 1536 + h * 128:1536 + (h + 1) * 128]
                qr = q * cs + _swap_halves(q) * sn
                kr = (k * cs + _swap_halves(k) * sn) * RET_SCALE
                sprev = s_ref[h]
                st_ref[c, h] = sprev
                qb, kb, vb = qr.astype(BF16), kr.astype(BF16), v.astype(BF16)
                s = _dot(qb, kb, NT) * dm_ref[h]
                o = _dot(s.astype(BF16), vb, NN) + _dot((qr * qd_ref[h]).astype(BF16), sprev.astype(BF16), NN)
                s_ref[h] = sprev * cd_ref[h] + _dot((kr * kd_ref[h]).astype(BF16), vb, TN)
                raw_ref[r, hc] = o
                y, _ = _group_norm(o)
                out_ref[r, hc] = (gt * jax.nn.sigmoid(gt)) * (y * g_ref[:, hc] + b_ref[:, hc])

    cmat = pl.BlockSpec((H, CHUNK, CHUNK), lambda i: (0, 0, 0))
    vec = pl.BlockSpec((1, BRANCH_W), lambda i: (0, 0))
    rope = pl.BlockSpec((tb, 128), lambda i: (i, 0))
    blk = pl.BlockSpec((tb, BRANCH_W), lambda i: (i, 0))
    return pl.pallas_call(
        body, name=name, grid=(T // tb,),
        in_specs=[pl.BlockSpec((tb, 2048), lambda i: (i, 0)), rope, rope, cmat, cmat, cmat, cmat, vec, vec],
        out_specs=[blk, blk, pl.BlockSpec((nch, H, CHUNK, CHUNK), lambda i: (i, 0, 0, 0))],
        out_shape=[jax.ShapeDtypeStruct((T, BRANCH_W), F32), jax.ShapeDtypeStruct((T, BRANCH_W), F32),
                   jax.ShapeDtypeStruct((T // CHUNK, H, CHUNK, CHUNK), F32)],
        scratch_shapes=[pltpu.VMEM((H, CHUNK, CHUNK), F32)],
        compiler_params=_params(("arbitrary",)),
    )(proj, cosf, sinf, *consts, gn_g, gn_b)


def ret_bwd(proj, cosf, sinf, consts, gn_g, gn_b, raw, states, dout, *, name):
    T = proj.shape[0]
    tb = min(512, T)
    nch = tb // CHUNK
    nb = T // tb
    H = RET_HEADS

    def body(p_ref, cos_ref, sin_ref, dm_ref, kd_ref, qd_ref, cd_ref, g_ref, b_ref, raw_ref, st_ref, do_ref,
             dp_ref, dg_ref, db_ref, ds_ref):
        @pl.when(pl.program_id(0) == 0)
        def _():
            ds_ref[...] = jnp.zeros_like(ds_ref)
            dg_ref[...] = jnp.zeros_like(dg_ref)
            db_ref[...] = jnp.zeros_like(db_ref)

        for c in reversed(range(nch)):
            r = slice(c * CHUNK, (c + 1) * CHUNK)
            cs, sn = cos_ref[r, :], sin_ref[r, :]
            for h in range(H):
                hc = slice(h * 128, (h + 1) * 128)
                q = p_ref[r, h * 128:(h + 1) * 128]
                k = p_ref[r, 512 + h * 128:512 + (h + 1) * 128]
                v = p_ref[r, 1024 + h * 128:1024 + (h + 1) * 128]
                gt = p_ref[r, 1536 + h * 128:1536 + (h + 1) * 128]
                qr = q * cs + _swap_halves(q) * sn
                kr = (k * cs + _swap_halves(k) * sn) * RET_SCALE
                sprev = st_ref[c, h]
                gv = g_ref[:, hc]
                y, rstd = _group_norm(raw_ref[r, hc])
                d_out = do_ref[r, hc]
                sg = jax.nn.sigmoid(gt)
                d_gate = d_out * (y * gv + b_ref[:, hc]) * (sg * (1.0 + gt * (1.0 - sg)))
                d_aff = d_out * (gt * sg)
                dg_ref[:, hc] += jnp.sum(d_aff * y, axis=0, keepdims=True)
                db_ref[:, hc] += jnp.sum(d_aff, axis=0, keepdims=True)
                dxh = d_aff * gv
                m1 = jnp.mean(dxh, axis=-1, keepdims=True)
                m2 = jnp.mean(dxh * y, axis=-1, keepdims=True)
                d_o = (rstd * (dxh - m1 - y * m2)).astype(BF16)
                qb, kb, vb = qr.astype(BF16), kr.astype(BF16), v.astype(BF16)
                dm, kd, qd = dm_ref[h], kd_ref[h], qd_ref[h]
                p = (_dot(qb, kb, NT) * dm).astype(BF16)
                dp = (_dot(d_o, vb, NT) * dm).astype(BF16)
                dsn = ds_ref[h]
                dsb = dsn.astype(BF16)
                dq_r = _dot(dp, kb, NN) + _dot(d_o, sprev.astype(BF16), NT) * qd
                dk_r = (_dot(dp, qb, TN) + _dot(vb, dsb, NT) * kd) * RET_SCALE
                d_v = _dot(p, d_o, TN) + _dot((kr * kd).astype(BF16), dsb, NN)
                ds_ref[h] = dsn * cd_ref[h] + _dot((qr * qd).astype(BF16), d_o, TN)
                dp_ref[r, h * 128:(h + 1) * 128] = (dq_r * cs - _swap_halves(dq_r) * sn).astype(BF16)
                dp_ref[r, 512 + h * 128:512 + (h + 1) * 128] = (dk_r * cs - _swap_halves(dk_r) * sn).astype(BF16)
                dp_ref[r, 1024 + h * 128:1024 + (h + 1) * 128] = d_v.astype(BF16)
                dp_ref[r, 1536 + h * 128:1536 + (h + 1) * 128] = d_gate.astype(BF16)

    cmat = pl.BlockSpec((H, CHUNK, CHUNK), lambda i: (0, 0, 0))
    vec = pl.BlockSpec((1, BRANCH_W), lambda i: (0, 0))
    rope = pl.BlockSpec((tb, 128), lambda i: (nb - 1 - i, 0))
    blk = pl.BlockSpec((tb, BRANCH_W), lambda i: (nb - 1 - i, 0))
    wide = pl.BlockSpec((tb, 2048), lambda i: (nb - 1 - i, 0))
    return pl.pallas_call(
        body, name=name, grid=(nb,),
        in_specs=[wide, rope, rope, cmat, cmat, cmat, cmat, vec, vec, blk,
                  pl.BlockSpec((nch, H, CHUNK, CHUNK), lambda i: (nb - 1 - i, 0, 0, 0)), blk],
        out_specs=[wide, vec, vec],
        out_shape=[jax.ShapeDtypeStruct((T, 2048), BF16), jax.ShapeDtypeStruct((1, BRANCH_W), F32),
                   jax.ShapeDtypeStruct((1, BRANCH_W), F32)],
        scratch_shapes=[pltpu.VMEM((H, CHUNK, CHUNK), F32)],
        compiler_params=_params(("arbitrary",)),
    )(proj, cosf, sinf, *consts, gn_g, gn_b, raw, states, dout)


def _sb_masks():
    row = lax.broadcasted_iota(jnp.int32, (CHUNK, CHUNK), 0)
    lane = lax.broadcasted_iota(jnp.int32, (CHUNK, CHUNK), 1)
    return row, lane


SB_QT = 256
SB_DEAD = -105.0


def _pair(v):
    hi = v.astype(BF16)
    return jnp.concatenate([hi, (v - hi.astype(F32)).astype(BF16)], axis=1)


def _sb_consts():
    r = lax.broadcasted_iota(jnp.int32, (256, 256), 0) & 127
    c = lax.broadcasted_iota(jnp.int32, (256, 256), 1)
    ones = c >= 128
    lane = lax.broadcasted_iota(jnp.int32, (CHUNK, CHUNK), 1)
    return (ones | (r > c)).astype(BF16), (ones | (r >= c)).astype(BF16), (lane < 64, lane >= 64)


def _per_head(x, hms):
    return jnp.concatenate([jnp.where(hm, x, 0.0) for hm in hms], axis=0).astype(BF16)


def _sb_logits(qb, kb2, mask2):
    z = _dot(qb, kb2, NT)
    l1p = jnp.log(1.0 + jnp.exp(-jnp.abs(z)))
    lsp = jnp.minimum(z, 0.0) - l1p
    lsn = lsp - z
    if mask2 is not None:
        lsn = jnp.where(mask2, lsn, 0.0)
    return lsp, lsn


def _sb_tile_mask(qt):
    trow = lax.broadcasted_iota(jnp.int32, (qt, 256), 0)
    tlane = lax.broadcasted_iota(jnp.int32, (qt, 256), 1) & 127
    return lambda m: (tlane + m * CHUNK) < trow


def sb_fwd(proj, *, name, job=None):
    T = proj.shape[0]
    qt = min(SB_QT, T)
    nsub = qt // CHUNK
    cb = C_SB // 128

    def body(q_ref, k_ref, v_ref, o_ref):
        u_gt, _, hms = _sb_consts()
        tile_mask = _sb_tile_mask(qt)

        def qtile(i, _):
            rq = pl.ds(pl.multiple_of(i * qt, qt), qt)
            qb = (q_ref[rq, :] * SB_SCALE).astype(BF16)

            def step(j, state, mask2):
                carry, acc = list(state[:2]), state[2]
                rk = pl.ds(pl.multiple_of(j * CHUNK, CHUNK), CHUNK)
                lsp, lsn = _sb_logits(qb, _per_head(k_ref[rk, :], hms), mask2)
                a_b = []
                for h in range(2):
                    hc = slice(h * 128, (h + 1) * 128)
                    r = _dot(_pair(lsn[:, hc]), u_gt, NN)
                    a = jnp.exp(lsp[:, hc] + r[:, :128] + carry[h])
                    if mask2 is not None:
                        a = jnp.where(mask2[:, hc], a, 0.0)
                    carry[h] = carry[h] + r[:, 128:]
                    a_b.append(a.astype(BF16))
                acc = acc + _dot(jnp.concatenate(a_b, axis=1), _per_head(v_ref[rk, :], hms), NN)
                return carry[0], carry[1], acc

            zero = jnp.zeros((qt, 128), F32)
            state = (zero, zero, zero)
            for m in reversed(range(nsub)):
                state = step(i * nsub + m, state, tile_mask(m))

            def live(c):
                return jnp.logical_and(c[0] < i, jnp.maximum(jnp.max(c[1][0]), jnp.max(c[1][1])) > SB_DEAD)

            def blocks(c):
                jj, st = c
                for u in range(nsub):
                    st = step((i - jj) * nsub - 1 - u, st, None)
                return jj + 1, st

            _, state = lax.while_loop(live, blocks, (jnp.int32(0), state))
            o_ref[rq, :] = state[2]
            return 0

        lax.fori_loop(0, T // qt, qtile, 0)

    def col(off):
        return pl.BlockSpec((T, 128), lambda hp: (0, off + hp))

    steps = BRANCH_W // 128
    j = _job_args(job, 3, 1)
    res = pl.pallas_call(
        _hosting(body, job, 3, 1, 0, steps), name=name, grid=(steps,),
        in_specs=[col(cb), col(cb + 4), col(cb + 8)] + j["in_specs"], out_specs=[col(0)] + j["out_specs"],
        out_shape=[jax.ShapeDtypeStruct((T, BRANCH_W), F32)] + j["out_shape"],
        scratch_shapes=j["scratch"], input_output_aliases=j["aliases"],
        compiler_params=_params(("parallel",) if job is None else ("arbitrary",)),
    )(proj, proj, proj, *j["ins"])
    return res[0], list(res[1:])


def sb_bwd(proj, out, dout, *, name, job=None):
    T = proj.shape[0]
    qt = min(SB_QT, T)
    nsub = qt // CHUNK
    cb = C_SB // 128

    def body(q_ref, k_ref, v_ref, o_ref, do_ref, dq_ref, dk_ref, dv_ref, dkt_ref, dvt_ref):
        u_gt, u_ge, hms = _sb_consts()
        tile_mask = _sb_tile_mask(qt)
        tall_lane = lax.broadcasted_iota(jnp.int32, (qt, 128), 1)
        top = lax.broadcasted_iota(jnp.int32, (CHUNK, CHUNK), 0) < 64
        dkt_ref[...] = jnp.zeros_like(dkt_ref)
        dvt_ref[...] = jnp.zeros_like(dvt_ref)

        def qtile(i, _):
            rq = pl.ds(pl.multiple_of(i * qt, qt), qt)
            qs = q_ref[rq, :] * SB_SCALE
            qb, q_t = qs.astype(BF16), qs.T.astype(BF16)
            dov = do_ref[rq, :]
            dob, do_t = dov.astype(BF16), dov.T.astype(BF16)
            prod = dob.astype(F32) * o_ref[rq, :]
            total = [jnp.broadcast_to(jnp.sum(jnp.where(hm, prod, 0.0), axis=1, keepdims=True), (qt, 128))
                     for hm in (tall_lane < 64, tall_lane >= 64)]

            def step(j, state, mask2):
                c_l, c_w, dq = list(state[:2]), list(state[2:4]), state[4]
                rk = pl.ds(pl.multiple_of(j * CHUNK, CHUNK), CHUNK)
                kb2, vb2 = _per_head(k_ref[rk, :], hms), _per_head(v_ref[rk, :], hms)
                lsp, lsn = _sb_logits(qb, kb2, mask2)
                da = _dot(dob, vb2, NT)
                sp = jnp.exp(lsp)
                a_b, dz_b = [], []
                for h in range(2):
                    hc = slice(h * 128, (h + 1) * 128)
                    r = _dot(_pair(lsn[:, hc]), u_gt, NN)
                    a = jnp.exp(lsp[:, hc] + r[:, :128] + c_l[h])
                    if mask2 is not None:
                        a = jnp.where(mask2[:, hc], a, 0.0)
                    c_l[h] = c_l[h] + r[:, 128:]
                    a = a.astype(BF16)
                    w = a.astype(F32) * da[:, hc]
                    r = _dot(_pair(w), u_ge, NN)
                    later_w = r[:, :128] + c_w[h]
                    c_w[h] = c_w[h] + r[:, 128:]
                    dz = w * (1.0 - sp[:, hc]) - sp[:, hc] * (total[h] - later_w)
                    if mask2 is not None:
                        dz = jnp.where(mask2[:, hc], dz, 0.0)
                    a_b.append(a)
                    dz_b.append(dz.astype(BF16))
                a_b, dz_b = jnp.concatenate(a_b, axis=1), jnp.concatenate(dz_b, axis=1)
                dkt = _dot(q_t, dz_b, NN)
                dvt = _dot(do_t, a_b, NN)
                dkt_ref[j] += jnp.where(top, dkt[:, :128], dkt[:, 128:])
                dvt_ref[j] += jnp.where(top, dvt[:, :128], dvt[:, 128:])
                return c_l[0], c_l[1], c_w[0], c_w[1], dq + _dot(dz_b, kb2, NN)

            zero = jnp.zeros((qt, 128), F32)
            state = (zero,) * 5
            for m in reversed(range(nsub)):
                state = step(i * nsub + m, state, tile_mask(m))

            def live(c):
                return jnp.logical_and(c[0] < i, jnp.maximum(jnp.max(c[1][0]), jnp.max(c[1][1])) > SB_DEAD)

            def blocks(c):
                jj, st = c
                for u in range(nsub):
                    st = step((i - jj) * nsub - 1 - u, st, None)
                return jj + 1, st

            _, state = lax.while_loop(live, blocks, (jnp.int32(0), state))
            dq_ref[rq, :] = (state[4] * SB_SCALE).astype(BF16)
            return 0

        lax.fori_loop(0, T // qt, qtile, 0)

        def untranspose(jb, _):
            rk = pl.ds(pl.multiple_of(jb * CHUNK, CHUNK), CHUNK)
            dk_ref[rk, :] = dkt_ref[jb].T.astype(BF16)
            dv_ref[rk, :] = dvt_ref[jb].T.astype(BF16)
            return 0

        lax.fori_loop(0, T // CHUNK, untranspose, 0)

    def col(off):
        return pl.BlockSpec((T, 128), lambda hp: (0, off + hp))

    o16 = jax.ShapeDtypeStruct((T, BRANCH_W), BF16)
    steps = BRANCH_W // 128
    j = _job_args(job, 5, 3)
    acc = pltpu.VMEM((T // CHUNK, CHUNK, CHUNK), F32)
    res = pl.pallas_call(
        _hosting(body, job, 5, 3, 2, steps), name=name, grid=(steps,),
        in_specs=[col(cb), col(cb + 4), col(cb + 8), col(0), col(0)] + j["in_specs"],
        out_specs=[col(0), col(0), col(0)] + j["out_specs"], out_shape=[o16, o16, o16] + j["out_shape"],
        scratch_shapes=[acc, acc] + j["scratch"], input_output_aliases=j["aliases"],
        compiler_params=_params(("parallel",) if job is None else ("arbitrary",)),
    )(proj, proj, proj, out, dout, *j["ins"])
    return res[0], res[1], res[2], list(res[3:])


_G0 = math.sqrt(2.0 / math.pi)
_G1 = 0.044715


def _gelu(x):
    return 0.5 * x * (1.0 + jnp.tanh(_G0 * (x + _G1 * x * x * x)))


def _gelu_grad(x):
    t = jnp.tanh(_G0 * (x + _G1 * x * x * x))
    return 0.5 * (1.0 + t) + 0.5 * x * (1.0 - t * t) * (_G0 * (1.0 + 3.0 * _G1 * x * x))


def _tril():
    row, lane = _sb_masks()
    return row >= lane


def sgu_fwd(proj, ln_g, ln_b, w, bias, *, name):
    T = proj.shape[0]
    tb = min(512, T)
    G = BRANCH_W // 128

    def body(u_ref, v_ref, g_ref, b_ref, w_ref, bias_ref, o_ref):
        vv = _gelu(v_ref[...])
        xh, _ = _group_norm(vv)
        vn = (xh * g_ref[...] + b_ref[...]).astype(BF16)
        tril = _tril()
        for g in range(G):
            wg = jnp.where(tril, w_ref[g], 0.0).astype(BF16)
            gc = slice(g * 128, (g + 1) * 128)
            for c in range(tb // CHUNK):
                r = slice(c * CHUNK, (c + 1) * CHUNK)
                sv = _dot(wg, vn[r, gc], NN) + bias_ref[g]
                o_ref[r, gc] = _gelu(u_ref[r, gc]) * sv

    cu, cv = C_SGU // BRANCH_W, C_SGU // BRANCH_W + 1
    vec = pl.BlockSpec((1, BRANCH_W), lambda i: (0, 0))
    mat = pl.BlockSpec((G, CHUNK, CHUNK), lambda i: (0, 0, 0))
    return pl.pallas_call(
        body, name=name, grid=(T // tb,),
        in_specs=[pl.BlockSpec((tb, BRANCH_W), lambda i: (i, cu)), pl.BlockSpec((tb, BRANCH_W), lambda i: (i, cv)),
                  vec, vec, mat, mat],
        out_specs=pl.BlockSpec((tb, BRANCH_W), lambda i: (i, 0)),
        out_shape=jax.ShapeDtypeStruct((T, BRANCH_W), F32),
        compiler_params=_params(("parallel",)),
    )(proj, proj, ln_g, ln_b, w, bias)


def sgu_bwd(proj, ln_g, ln_b, w, bias, dout, *, name):
    T = proj.shape[0]
    tb = min(512, T)
    G = BRANCH_W // 128

    def body(u_ref, v_ref, g_ref, b_ref, w_ref, bias_ref, do_ref, dp_ref, dw_ref, dbias_ref, dg_ref, db_ref, dvn_ref):
        @pl.when(pl.program_id(0) == 0)
        def _():
            dw_ref[...] = jnp.zeros_like(dw_ref)
            dbias_ref[...] = jnp.zeros_like(dbias_ref)
            dg_ref[...] = jnp.zeros_like(dg_ref)
            db_ref[...] = jnp.zeros_like(db_ref)

        gv = v_ref[...]
        vv = _gelu(gv)
        xh, rstd = _group_norm(vv)
        vn = (xh * g_ref[...] + b_ref[...]).astype(BF16)
        tril = _tril()
        for g in range(G):
            wg = jnp.where(tril, w_ref[g], 0.0).astype(BF16)
            gc = slice(g * 128, (g + 1) * 128)
            for c in range(tb // CHUNK):
                r = slice(c * CHUNK, (c + 1) * CHUNK)
                vn_c = vn[r, gc]
                sv = _dot(wg, vn_c, NN) + bias_ref[g]
                gu = u_ref[r, gc]
                d_o = do_ref[r, gc]
                dp_ref[r, gc] = (d_o * sv * _gelu_grad(gu)).astype(BF16)
                dsv = d_o * _gelu(gu)
                dsv_b = dsv.astype(BF16)
                dvn_ref[r, gc] = _dot(wg, dsv_b, TN)
                dw_ref[g] += jnp.where(tril, _dot(dsv_b, vn_c, NT), 0.0)
                dbias_ref[g] += jnp.broadcast_to(jnp.sum(dsv, axis=1, keepdims=True), (CHUNK, CHUNK))
        dvn = dvn_ref[...]
        dg_ref[...] += jnp.sum(dvn * xh, axis=0, keepdims=True)
        db_ref[...] += jnp.sum(dvn, axis=0, keepdims=True)
        dxh = dvn * g_ref[...]
        m1 = jnp.mean(dxh, axis=-1, keepdims=True)
        m2 = jnp.mean(dxh * xh, axis=-1, keepdims=True)
        dp_ref[:, BRANCH_W:2 * BRANCH_W] = (rstd * (dxh - m1 - xh * m2) * _gelu_grad(gv)).astype(BF16)

    cu, cv = C_SGU // BRANCH_W, C_SGU // BRANCH_W + 1
    vec = pl.BlockSpec((1, BRANCH_W), lambda i: (0, 0))
    mat = pl.BlockSpec((G, CHUNK, CHUNK), lambda i: (0, 0, 0))
    blk = pl.BlockSpec((tb, BRANCH_W), lambda i: (i, 0))
    msh = jax.ShapeDtypeStruct((G, CHUNK, CHUNK), F32)
    vsh = jax.ShapeDtypeStruct((1, BRANCH_W), F32)
    return pl.pallas_call(
        body, name=name, grid=(T // tb,),
        in_specs=[pl.BlockSpec((tb, BRANCH_W), lambda i: (i, cu)), pl.BlockSpec((tb, BRANCH_W), lambda i: (i, cv)),
                  vec, vec, mat, mat, blk],
        out_specs=[pl.BlockSpec((tb, 2 * BRANCH_W), lambda i: (i, 0)), mat, mat, vec, vec],
        out_shape=[jax.ShapeDtypeStruct((T, 2 * BRANCH_W), BF16), msh, msh, vsh, vsh],
        scratch_shapes=[pltpu.VMEM((tb, BRANCH_W), F32)],
        compiler_params=_params(("arbitrary",)),
    )(proj, proj, ln_g, ln_b, w, bias, dout)


def merge_fwd(a1, a2, a3, p1, p2, p3, proj, *, name):
    T = a1.shape[0]
    tm, tn = min(1024, T), 512
    gb = C_GATE // tn

    def body(a1_ref, a2_ref, a3_ref, p1_ref, p2_ref, p3_ref, g1_ref, g2_ref, g3_ref, m_ref, r1_ref, r2_ref, r3_ref):
        m = None
        for a_ref, p_ref, g_ref, r_ref in ((a1_ref, p1_ref, g1_ref, r1_ref), (a2_ref, p2_ref, g2_ref, r2_ref),
                                           (a3_ref, p3_ref, g3_ref, r3_ref)):
            r = _dot(a_ref[...].astype(BF16), p_ref[...], NN)
            r_ref[...] = r
            t = jax.nn.sigmoid(g_ref[...]) * r
            m = t if m is None else m + t
        m_ref[...] = m

    a_spec = pl.BlockSpec((tm, BRANCH_W), lambda i, j: (i, 0))
    p_spec = pl.BlockSpec((BRANCH_W, tn), lambda i, j: (0, j))
    o_spec = pl.BlockSpec((tm, tn), lambda i, j: (i, j))
    osh = jax.ShapeDtypeStruct((T, D_MODEL), F32)
    gates = [pl.BlockSpec((tm, tn), functools.partial(lambda i, j, o: (i, o + j), o=gb + 2 * n)) for n in range(3)]
    return pl.pallas_call(
        body, name=name, grid=(T // tm, D_MODEL // tn),
        in_specs=[a_spec, a_spec, a_spec, p_spec, p_spec, p_spec, *gates],
        out_specs=[o_spec] * 4, out_shape=[osh] * 4,
        compiler_params=_params(("parallel", "parallel")),
    )(a1, a2, a3, p1, p2, p3, proj, proj, proj)


def merge_bwd(dm, r1, r2, r3, proj, *, name):
    T = dm.shape[0]
    tm, tn = min(512, T), 512
    gb = C_GATE // tn

    def body(dm_ref, r1_ref, r2_ref, r3_ref, g1_ref, g2_ref, g3_ref, dr1_ref, dr2_ref, dr3_ref, dg1_ref, dg2_ref, dg3_ref):
        d = dm_ref[...]
        for r_ref, g_ref, dr_ref, dg_ref in ((r1_ref, g1_ref, dr1_ref, dg1_ref), (r2_ref, g2_ref, dr2_ref, dg2_ref),
                                             (r3_ref, g3_ref, dr3_ref, dg3_ref)):
            s = jax.nn.sigmoid(g_ref[...])
            dr_ref[...] = (d * s).astype(BF16)
            dg_ref[...] = (d * r_ref[...] * (s * (1.0 - s))).astype(BF16)

    o_spec = pl.BlockSpec((tm, tn), lambda i, j: (i, j))
    osh = jax.ShapeDtypeStruct((T, D_MODEL), BF16)
    gates = [pl.BlockSpec((tm, tn), functools.partial(lambda i, j, o: (i, o + j), o=gb + 2 * n)) for n in range(3)]
    return pl.pallas_call(
        body, name=name, grid=(T // tm, D_MODEL // tn),
        in_specs=[o_spec] * 4 + gates, out_specs=[o_spec] * 6, out_shape=[osh] * 6,
        compiler_params=_params(("parallel", "parallel")),
    )(dm, r1, r2, r3, proj, proj, proj)


def _rows_call(fn, ins, out_dtypes, *, name, tr=256):
    first = ins[0][0] if isinstance(ins[0], tuple) else ins[0]
    R, C = first.shape[-2:]
    tr = min(tr, R)
    assert R % tr == 0, (name, R, tr)
    arrs, specs = [], []
    for x in ins:
        if isinstance(x, tuple):
            arrs.append(x[0])
            specs.append(pl.BlockSpec((None, tr, C), functools.partial(lambda i, n: (n, i, 0), n=x[1])))
        else:
            arrs.append(x)
            specs.append(pl.BlockSpec((tr, C), lambda i: (i, 0)))
    ni = len(arrs)

    def body(*refs):
        vals = fn(*[r[...] for r in refs[:ni]])
        for o_ref, v in zip(refs[ni:], vals):
            o_ref[...] = v.astype(o_ref.dtype)

    res = pl.pallas_call(
        body, name=name, grid=(R // tr,), in_specs=specs,
        out_specs=[pl.BlockSpec((tr, C), lambda i: (i, 0)) for _ in out_dtypes],
        out_shape=[jax.ShapeDtypeStruct((R, C), dt) for dt in out_dtypes],
        compiler_params=_params(("parallel",)),
    )(*arrs)
    return res


def _tile_rows(rows, cols):
    t = 256
    while t > 8 and (t * cols > 512 * 1024 or rows % t):
        t //= 2
    return t


def _rows_at(fn, pos, ins, outs, steps, *, name, aliases=None):
    read = [n for n, (_, s) in enumerate(ins) if s is not ANY]
    ni = len(ins)

    def body(pos_ref, *refs):
        vals = fn(*[refs[n][...] for n in read])
        for o_ref, v in zip(refs[ni:], vals):
            o_ref[...] = v.astype(o_ref.dtype)

    return pl.pallas_call(
        body, name=name,
        grid_spec=pltpu.PrefetchScalarGridSpec(num_scalar_prefetch=1, grid=(steps,), in_specs=[s for _, s in ins],
                                               out_specs=[s for _, s in outs]),
        out_shape=[sh for sh, _ in outs],
        input_output_aliases={1 + i: o for i, o in (aliases or {}).items()},
        compiler_params=_params(("parallel",)),
    )(pos, *[a for a, _ in ins])


def cast_into_whole(pos, w, l, axis, *, name):
    _, r, n = w.shape
    tr = _tile_rows(r, n)
    if axis == 1:
        shape, spec = (r, n * N_CHIPS), pl.BlockSpec((tr, n), lambda i, p: (i, p[3]))
    else:
        shape, spec = (r * N_CHIPS, n), pl.BlockSpec((tr, n), lambda i, p: (p[3] * (r // tr) + i, 0))
    return _rows_at(lambda a: (a,), pos, [(w, pl.BlockSpec((None, tr, n), lambda i, p: (l, i, 0)))],
                    [(jax.ShapeDtypeStruct(shape, BF16), spec)], r // tr, name=name)[0]


def pair_sum(pos, theirs, g32, axis, *, name):
    rows2, cols = theirs.shape
    h = rows2 // (N_CHIPS if axis == 0 else 1)
    tr = _tile_rows(h, cols)
    hb = h // tr
    if axis == 1:
        own = pl.BlockSpec((tr, cols), lambda i, p: (p[2] * hb + i, 0))
    else:
        own = pl.BlockSpec((tr, cols), lambda i, p: ((2 * (i // hb) + p[2]) * hb + i % hb, 0))
    row = pl.BlockSpec((tr, cols), lambda i, p: (i, 0))
    return _rows_at(lambda t, m: (m + t.astype(F32),) * 2, pos, [(theirs, row), (g32, own)],
                    [(jax.ShapeDtypeStruct((rows2, cols), F32), row), (jax.ShapeDtypeStruct((rows2, cols), BF16), row)],
                    rows2 // tr, name=name)


def chip_sum(pos, h32, recv, l, axis, whole, *, name):
    _, depth, h, n = recv.shape
    tr = _tile_rows(h, n)
    hb = h // tr
    if axis == 1:
        mine = pl.BlockSpec((tr, n), lambda i, p: (i, p[3]))
    else:
        mine = pl.BlockSpec((tr, n), lambda i, p: (p[3] * hb + i, 0))
    ins = [(h32, mine)] + [(recv, pl.BlockSpec((None, None, tr, n), functools.partial(lambda i, p, j: (j, l, i, 0), j=j)))
                           for j in range(3)]
    if whole is not None:
        ins.append((whole, ANY))
    return _rows_at(lambda o, a, b, c: (((o + a.astype(F32)) + b.astype(F32)) + c.astype(F32),), pos, ins,
                    [(jax.ShapeDtypeStruct((depth, 2, h, n), F32), pl.BlockSpec((None, None, tr, n), lambda i, p: (l, p[2], i, 0)))],
                    hb, name=name, aliases=None if whole is None else {4: 0})[0]


def _adamw(w, g, m, v):
    m2 = ADAM_B1 * m + (1.0 - ADAM_B1) * g
    v2 = ADAM_B2 * v + (1.0 - ADAM_B2) * (g * g)
    m_hat = m2 / (1.0 - ADAM_B1 ** ADAM_STEP)
    v_hat = v2 / (1.0 - ADAM_B2 ** ADAM_STEP)
    delta = -ADAM_LR * (m_hat / (jnp.sqrt(v_hat) + ADAM_EPS) + ADAM_WD * w)
    return delta, m2, v2


def _place():
    return lax.axis_index("x"), lax.axis_index("y"), lax.axis_index("c")


def _chip_peers(x, y, c):
    return [((1 - x, y, c), 2 * (1 - x) + y), ((x, 1 - y, c), 2 * x + 1 - y), ((1 - x, 1 - y, c), 2 * (1 - x) + 1 - y)]


def _shard_of(ref, axis, k, n):
    start = pl.multiple_of(k * n, 128)
    return ref.at[pl.ds(start, n), :] if axis == 0 else ref.at[:, pl.ds(start, n)]


ANY = pl.BlockSpec(memory_space=pl.ANY)


class CopyJob:
    def __init__(self, ins, out_shape, scratch, copies, aliases=None):
        self.ins, self.out_shape, self.scratch, self.copies = list(ins), list(out_shape), list(scratch), copies
        self.aliases = dict(aliases or {})

    def start(self, ins, outs, sems):
        local, remote, _, _ = self.copies(ins, outs, sems)
        for d in local + remote:
            d.start()

    def finish(self, ins, outs, sems):
        local, remote, arrivals, relays = self.copies(ins, outs, sems)
        for needs, sends, _ in relays:
            for d in needs:
                d.wait_recv()
            for d in sends:
                d.start()
        for d in arrivals + [d for _, _, arrives in relays for d in arrives]:
            d.wait_recv()
        for d in remote + [d for _, sends, _ in relays for d in sends]:
            d.wait_send()
        for d in local:
            d.wait()


def run_job(job, *, name):
    ni, no = len(job.ins), len(job.out_shape)

    def body(*refs):
        parts = refs[:ni], refs[ni:ni + no], refs[ni + no:]
        job.start(*parts)
        job.finish(*parts)

    return pl.pallas_call(
        body, name=name, in_specs=[ANY] * ni, out_specs=[ANY] * no, out_shape=job.out_shape,
        scratch_shapes=job.scratch, input_output_aliases=job.aliases,
    )(*job.ins)


def _job_args(job, n_in, n_out):
    if job is None:
        return dict(ins=[], in_specs=[], out_specs=[], out_shape=[], scratch=[], aliases={})
    return dict(ins=job.ins, in_specs=[ANY] * len(job.ins), out_specs=[ANY] * len(job.out_shape),
                out_shape=job.out_shape, scratch=job.scratch,
                aliases={n_in + i: n_out + o for i, o in job.aliases.items()})


def _hosting(body, job, n_in, n_out, n_scratch, grid):
    if job is None:
        return body
    ji, jo = len(job.ins), len(job.out_shape)
    grid = (grid,) if isinstance(grid, int) else tuple(grid)

    def at(ends):
        hit = None
        for ax, e in enumerate(ends):
            here = pl.program_id(ax) == e
            hit = here if hit is None else jnp.logical_and(hit, here)
        return hit

    def hosted(*refs):
        o = n_in + ji
        s = o + n_out + jo
        parts = refs[n_in:o], refs[o + n_out:s], refs[s + n_scratch:]

        @pl.when(at([0] * len(grid)))
        def _():
            job.start(*parts)

        body(*refs[:n_in], *refs[o:o + n_out], *refs[s:s + n_scratch])

        @pl.when(at([g - 1 for g in grid]))
        def _():
            job.finish(*parts)

    return hosted


def _job_sems(n_remote, n_local):
    return [pltpu.SemaphoreType.DMA((n_remote,)), pltpu.SemaphoreType.DMA((n_remote,)), pltpu.SemaphoreType.DMA((n_local,))]


def gather_job(shards, axes):
    na = len(shards)

    def copies(ins, outs, sems):
        send, recv, _ = sems
        x, y, c = _place()
        k = 2 * x + y
        remote, relays = [], []
        for a in range(na):
            r = outs[a].shape[0] // (N_CHIPS if axes[a] == 0 else 1)
            n = outs[a].shape[axes[a]] // N_CHIPS
            half = r // 2

            def part(kk, cc, a=a, n=n, half=half):
                rows = pl.ds(pl.multiple_of(cc * half + (kk * n if axes[a] == 0 else 0), 8), half)
                return outs[a].at[rows, :] if axes[a] == 0 else outs[a].at[rows, pl.ds(pl.multiple_of(kk * n, 128), n)]

            needs, passes, lands = [], [], []
            for j, (peer, kp) in enumerate(_chip_peers(x, y, c)):
                s = 6 * a + j
                remote.append(pltpu.make_async_remote_copy(part(k, c), part(k, c), send.at[s], recv.at[s],
                                                           device_id=peer, device_id_type=MESH))
                needs.append(pltpu.make_async_remote_copy(part(kp, c), part(kp, c), send.at[s], recv.at[s],
                                                          device_id=peer, device_id_type=MESH))
                passes.append(pltpu.make_async_remote_copy(part(kp, c), part(kp, c), send.at[s + 3], recv.at[s + 3],
                                                           device_id=(x, y, 1 - c), device_id_type=MESH))
                lands.append(pltpu.make_async_remote_copy(part(kp, 1 - c), part(kp, 1 - c), send.at[s + 3], recv.at[s + 3],
                                                          device_id=(x, y, 1 - c), device_id_type=MESH))
            relays.append((needs, passes, lands))
        return [], remote, [], relays

    out_shape = [jax.ShapeDtypeStruct(w.shape, BF16) for w in shards]
    return CopyJob(shards, out_shape, _job_sems(6 * na, 1), copies, {a: a for a in range(na)})


def scatter_job(layers, g16, axes, filled):
    na = len(axes)

    def shard_shape(a):
        r, c = g16[a].shape
        return (r // N_CHIPS, c) if axes[a] == 0 else (r, c // N_CHIPS)

    def copies(ins, outs, sems):
        send, recv_sems, _ = sems
        x, y, c = _place()
        remote = []
        for a in range(na):
            n = shard_shape(a)[axes[a]]
            for r, (peer, kp) in enumerate(_chip_peers(x, y, c)):
                remote.append(pltpu.make_async_remote_copy(_shard_of(ins[a], axes[a], kp, n), outs[a].at[r, layers[a]],
                                                           send.at[3 * a + r], recv_sems.at[3 * a + r],
                                                           device_id=peer, device_id_type=MESH))
        return [], remote, remote, []

    out_shape = [jax.ShapeDtypeStruct((3, DEPTH) + shard_shape(a), BF16) for a in range(na)]
    ins = list(g16)
    aliases = {}
    for a in range(na):
        if filled[a] is not None:
            aliases[len(ins)] = a
            ins.append(filled[a])
    return CopyJob(ins, out_shape, _job_sems(3 * na, 1), copies, aliases)


def pair_job(g16, axes):
    na = len(axes)
    pieces = [1 if ax == 1 else N_CHIPS for ax in axes]

    def copies(ins, outs, sems):
        send, recv, _ = sems
        x, y, c = _place()
        remote = []
        s = 0
        for a in range(na):
            rows = g16[a].shape[0] // (2 * pieces[a])
            for kk in range(pieces[a]):
                src = ins[a].at[pl.ds(pl.multiple_of((2 * kk + 1 - c) * rows, 8), rows), :]
                remote.append(pltpu.make_async_remote_copy(src, outs[a].at[pl.ds(kk * rows, rows), :], send.at[s], recv.at[s],
                                                           device_id=(x, y, 1 - c), device_id_type=MESH))
                s += 1
        return [], remote, remote, []

    out_shape = [jax.ShapeDtypeStruct((g.shape[0] // 2, g.shape[1]), BF16) for g in g16]
    return CopyJob(g16, out_shape, _job_sems(sum(pieces), 1), copies)


def join_job(shards):
    na = len(shards)

    def copies(ins, outs, sems):
        send, recv, _ = sems
        x, y, c = _place()
        remote = [pltpu.make_async_remote_copy(outs[a].at[:, c], outs[a].at[:, c], send.at[a], recv.at[a],
                                               device_id=(x, y, 1 - c), device_id_type=MESH) for a in range(na)]
        lands = [pltpu.make_async_remote_copy(outs[a].at[:, 1 - c], outs[a].at[:, 1 - c], send.at[a], recv.at[a],
                                              device_id=(x, y, 1 - c), device_id_type=MESH) for a in range(na)]
        return [], remote, lands, []

    out_shape = [jax.ShapeDtypeStruct(s.shape, F32) for s in shards]
    return CopyJob(shards, out_shape, _job_sems(na, 1), copies, {a: a for a in range(na)})


def allreduce_small(p):
    R = p.shape[0]

    def body(p_ref, o_ref, buf, send_sems, recv_sems):
        x, y, c = _place()
        me = 4 * x + 2 * y + c
        cps = []
        for rel in range(1, 8):
            dx, dy, dc = rel >> 2, (rel >> 1) & 1, rel & 1
            peer = (1 - x if dx else x, 1 - y if dy else y, 1 - c if dc else c)
            cp = pltpu.make_async_remote_copy(p_ref, buf.at[me], send_sems.at[rel - 1], recv_sems.at[rel - 1],
                                              device_id=peer, device_id_type=MESH)
            cp.start()
            cps.append((cp, 4 * peer[0] + 2 * peer[1] + peer[2]))
        buf[me] = p_ref[...]
        for rel, (cp, who) in enumerate(cps):
            pltpu.make_async_remote_copy(p_ref, buf.at[who], send_sems.at[rel], recv_sems.at[rel],
                                         device_id=(x, y, c), device_id_type=MESH).wait_recv()
        acc = buf[0]
        for d in range(1, 8):
            acc = acc + buf[d]
        o_ref[...] = acc
        for cp, _ in cps:
            cp.wait_send()

    return pl.pallas_call(
        body, name="allreduce_small",
        in_specs=[pl.BlockSpec(memory_space=pltpu.VMEM)], out_specs=pl.BlockSpec(memory_space=pltpu.VMEM),
        out_shape=jax.ShapeDtypeStruct((R, 128), F32),
        scratch_shapes=[pltpu.VMEM((8, R, 128), F32), pltpu.SemaphoreType.DMA((7,)), pltpu.SemaphoreType.DMA((7,))],
        compiler_params=pltpu.CompilerParams(vmem_limit_bytes=VMEM_LIMIT),
    )(p)


BIG = ("w_in", "p_ret", "p_sb", "p_sgu", "w_out", "w_up", "w_down")
BIG_AXIS = {"w_in": 1, "p_ret": 1, "p_sb": 1, "p_sgu": 1, "w_out": 0, "w_up": 1, "w_down": 0}
SMALL = ("ret_gn_g", "ret_gn_b", "sgu_ln_g", "sgu_ln_b", "sgu_w", "sgu_b", "ln1_g", "ln1_b", "ln2_g", "ln2_b")


def layer_forward(l, x0, W, sm, rope, rconsts, job=None, job_done=None):
    n = f"l{l}_"
    proj = matmul(x0, W["w_in"], mode="nn", tm=1024, tn=640, tk=1024, name=n + "proj")
    retg, raw, states = ret_fwd(proj, *rope, rconsts, sm["ret_gn_g"], sm["ret_gn_b"], name=n + "ret_fwd")
    sb, job_out = sb_fwd(proj, name=n + "sb_fwd", job=job)
    if job is not None:
        job_done(job_out)
    sg = sgu_fwd(proj, sm["sgu_ln_g"], sm["sgu_ln_b"], sm["sgu_w"], sm["sgu_bias"], name=n + "sgu_fwd")
    merged, r1, r2, r3 = merge_fwd(retg, sb, sg, W["p_ret"], W["p_sb"], W["p_sgu"], proj, name=n + "merge_fwd")
    x1, xh1, rs1 = matmul_ln(merged, W["w_out"], x0, sm["ln1_g"], sm["ln1_b"], tk=1024, name=n + "out_ln1")
    h1 = matmul(x1, W["w_up"], mode="nn", tm=1024, tn=1024, tk=1024, name=n + "up")
    x2, xh2, rs2 = matmul_ln(h1, W["w_down"], x1, sm["ln2_g"], sm["ln2_b"], pro=_relu2, tk=1024, name=n + "down_ln2")
    saved = dict(x0=x0, proj=proj, retg=retg, raw=raw, states=states, sb=sb, sg=sg, merged=merged, r=(r1, r2, r3),
                 x1=x1, xh1=xh1, rs1=rs1, h1=h1, xh2=xh2, rs2=rs2)
    return x2, saved


def layer_backward(l, dx2, s, W, sm, rope, rconsts, make_job=None, job_done=None, make_tail_job=None):
    n = f"l{l}_"
    two = ((F32, None), (BF16, None))
    gw, gs = {}, {}
    du2, gs["ln2_g"], gs["ln2_b"] = ln_bwd(dx2, s["xh2"], s["rs2"], sm["ln2_g"], name=n + "ln2_bwd")
    gw["w_down"] = matmul(s["h1"], du2, mode="tn", tm=1024, tn=1024, tk=512, pro=_relu2, outs=two, name=n + "g_down")
    dh1 = matmul(du2, W["w_down"], mode="nt", tm=1024, tn=1024, tk=1024, outs=((BF16, None),),
                 epi=lambda acc, h: (acc * (2.0 * jnp.maximum(h, 0.0)),), tiles=(s["h1"],), name=n + "d_h1")
    gw["w_up"] = matmul(s["x1"], dh1, mode="tn", tm=1024, tn=1024, tk=512, outs=two, name=n + "g_up")
    dx1 = matmul(dh1, W["w_up"], mode="nt", tm=1024, tn=1024, tk=1024,
                 epi=lambda acc, d: (acc + ALPHA * d,), tiles=(du2,), name=n + "d_x1")
    du1, gs["ln1_g"], gs["ln1_b"] = ln_bwd(dx1, s["xh1"], s["rs1"], sm["ln1_g"], name=n + "ln1_bwd")
    gw["w_out"] = matmul(s["merged"], du1, mode="tn", tm=1024, tn=1024, tk=512, outs=two, name=n + "g_out")
    dmerged = matmul(du1, W["w_out"], mode="nt", tm=1024, tn=1024, tk=1024, name=n + "d_merged")
    dr1, dr2, dr3, dg1, dg2, dg3 = merge_bwd(dmerged, *s["r"], s["proj"], name=n + "merge_bwd")
    d_branch = {}
    for nm, a, dr in (("p_ret", s["retg"], dr1), ("p_sb", s["sb"], dr2), ("p_sgu", s["sg"], dr3)):
        gw[nm] = matmul(a, dr, mode="tn", tm=512, tn=1024, tk=512, outs=two, name=n + "g_" + nm)
        d_branch[nm] = matmul(dr, W[nm], mode="nt", tm=1024, tn=512, tk=1024, name=n + "d_" + nm)
    dret, gs["ret_gn_g"], gs["ret_gn_b"] = ret_bwd(s["proj"], *rope, rconsts, sm["ret_gn_g"], sm["ret_gn_b"], s["raw"],
                                                    s["states"], d_branch["p_ret"], name=n + "ret_bwd")
    job = make_job(gw) if make_job is not None else None
    dsq, dsk, dsv, job_out = sb_bwd(s["proj"], s["sb"], d_branch["p_sb"], name=n + "sb_bwd", job=job)
    if job is not None:
        job_done(job_out)
    dsgu, gs["sgu_w"], dbias, gs["sgu_ln_g"], gs["sgu_ln_b"] = sgu_bwd(
        s["proj"], sm["sgu_ln_g"], sm["sgu_ln_b"], sm["sgu_w"], sm["sgu_bias"], d_branch["p_sgu"], name=n + "sgu_bwd")
    gs["sgu_b"] = dbias[:, :, 0]
    dproj = jnp.concatenate([dret, dsq, dsk, dsv, dsgu, dg1, dg2, dg3], axis=1)
    gw["w_in"] = matmul(s["x0"], dproj, mode="tn", tm=1024, tn=1536, tk=512, outs=two, name=n + "g_in")
    job = make_tail_job(gw["w_in"]) if make_tail_job is not None else None
    dx0 = matmul(dproj, W["w_in"], mode="nt", tm=1024, tn=1024, tk=1536,
                 epi=lambda acc, d: (acc + ALPHA * d,), tiles=(du1,), name=n + "d_x0", job=job)
    if job is not None:
        dx0, job_out = dx0
        job_done(job_out)
    return dx0, gw, gs


def local_step(x, target, small, plan):
    T = x.shape[0]
    rope = _rope_tables(T)
    rconsts = _ret_consts()
    sms = []
    for l in range(DEPTH):
        sm = {k: small[k][l][None, :] for k in SMALL if k not in ("sgu_w", "sgu_b")}
        sm["sgu_w"] = small["sgu_w"][l]
        sm["sgu_bias"] = jnp.broadcast_to(small["sgu_b"][l][:, :, None], (4, CHUNK, CHUNK))
        sms.append(sm)
    h, saved = x, []
    for l in range(DEPTH):
        h, s = layer_forward(l, h, plan.weights(l), sms[l], rope, rconsts, plan.fwd_job(l), plan.job_done)
        saved.append(s)
    dy, sq = loss_head(h, target)
    gs = {k: [None] * DEPTH for k in SMALL}
    for l in reversed(range(DEPTH)):
        dy, gwl, gsl = layer_backward(l, dy, saved[l], plan.weights(l), sms[l], rope, rconsts,
                                      functools.partial(plan.bwd_job, l), plan.job_done, plan.tail_job(l))
        plan.grads(l, gwl)
        for k in SMALL:
            gs[k][l] = gsl[k].reshape(small[k].shape[1:])
    return sq[0, 0], dy, {k: jnp.stack(v) for k, v in gs.items()}


LATE_USE = ("w_up", "w_down")
EARLY_GRADS = ("p_ret", "p_sb", "p_sgu", "w_out", "w_up", "w_down")


class _StepPlan:
    def __init__(self, pos, shards16):
        self.pos = pos
        self.shards16 = shards16
        self.full = [dict() for _ in range(DEPTH)]
        self.gw = [None] * DEPTH
        self.bufs = {}
        self.sums = {}
        self.job_done(run_job(self._gather([(0, "w_in")]), name="gather_first"))

    def weights(self, l):
        return self.full[l]

    def grads(self, l, gw):
        self.gw[l] = gw

    def _gather(self, items):
        self.pending = ("gather", items)
        return gather_job([self.shards16[l][k] for l, k in items], [BIG_AXIS[k] for _, k in items])

    def _scatter(self, items, tag):
        axes = [BIG_AXIS[k] for _, k, _ in items]
        got = run_job(pair_job([g[1] for _, _, g in items], axes), name="pair_" + tag)
        sums16 = []
        for a, (l, k, g) in enumerate(items):
            self.sums[(l, k)], s16 = pair_sum(self.pos, got[a], g[0], axes[a], name=f"pair_sum_{k}_{l}")
            sums16.append(s16)
        self.pending = ("scatter", [(l, k) for l, k, _ in items])
        return scatter_job([l for l, _, _ in items], sums16, axes, [self.bufs.get(k) for _, k, _ in items])

    def fwd_job(self, l):
        items = [(l, k) for k in (BIG[1:] if l == 0 else LATE_USE)]
        if l + 1 < DEPTH:
            items += [(l + 1, k) for k in BIG if k not in LATE_USE]
        return self._gather(items)

    def bwd_job(self, l, ready):
        items = [(l, k, ready[k]) for k in EARLY_GRADS]
        if l + 1 < DEPTH:
            items.append((l + 1, "w_in", self.gw[l + 1]["w_in"]))
        return self._scatter(items, f"l{l}")

    def job_done(self, outs):
        kind, items = self.pending
        for a, (l, k) in enumerate(items):
            if kind == "gather":
                self.full[l][k] = outs[a]
            else:
                self.bufs[k] = outs[a]

    def tail_job(self, l):
        return (lambda g: self._scatter([(0, "w_in", g)], "last")) if l == 0 else None

    def finish(self):
        return self.bufs, self.sums


def _flat2(a):
    return a.reshape(-1, a.shape[-1])


def kernel(x, w_in, ret_gn_g, ret_gn_b, sgu_ln_g, sgu_ln_b, sgu_w, sgu_b, p_ret, p_sb, p_sgu, w_out, ln1_g, ln1_b, w_up, w_down, ln2_g, ln2_b, loss_target, m_w_in, m_ret_gn_g, m_ret_gn_b, m_sgu_ln_g, m_sgu_ln_b, m_sgu_w, m_sgu_b, m_p_ret, m_p_sb, m_p_sgu, m_w_out, m_ln1_g, m_ln1_b, m_w_up, m_w_down, m_ln2_g, m_ln2_b, v_w_in, v_ret_gn_g, v_ret_gn_b, v_sgu_ln_g, v_sgu_ln_b, v_sgu_w, v_sgu_b, v_p_ret, v_p_sb, v_p_sgu, v_w_out, v_ln1_g, v_ln1_b, v_w_up, v_w_down, v_ln2_g, v_ln2_b):
    given = dict(locals())
    order = BIG[:1] + SMALL[:6] + BIG[1:5] + SMALL[6:8] + BIG[5:7] + SMALL[8:10]
    L = DEPTH

    px, py, pc = _place()
    pos = jnp.stack([px, py, pc, 2 * px + py]).astype(jnp.int32)

    shards16 = [{k: cast_into_whole(pos, given[k], l, BIG_AXIS[k], name=f"cast_{k}_{l}") for k in BIG} for l in range(L)]
    plan = _StepPlan(pos, shards16)
    sq, dx, gs = local_step(x[0], loss_target[0], {k: given[k] for k in SMALL}, plan)
    loss = 0.5 * lax.psum(sq, ("x", "y", "c"))

    bufs, sums = plan.finish()
    shards = []
    for k in BIG:
        whole = None
        for l in range(L):
            whole = chip_sum(pos, sums[(l, k)], bufs[k], l, BIG_AXIS[k], whole, name=f"chip_sum_{k}_{l}")
        shards.append(whole)
    joined = run_job(join_job(shards), name="join_halves")
    out = {}
    for a, k in enumerate(BIG):
        shp = given[k].shape
        res = _rows_call(lambda g_, w_, m_, v_: (g_,) + _adamw(w_, g_, m_, v_),
                         [joined[a].reshape(-1, shp[-1]), _flat2(given[k]), _flat2(given["m_" + k]), _flat2(given["v_" + k])],
                         [F32] * 4, name="adamw_" + k)
        out[k] = [r.reshape(shp) for r in res]

    def pack(d, pre=""):
        return jnp.concatenate([d[pre + k].reshape(-1) for k in SMALL]).reshape(-1, 128)

    g_small = allreduce_small(pack(gs))
    res = _rows_call(lambda g_, w_, m_, v_: (g_,) + _adamw(w_, g_, m_, v_),
                     [g_small, pack(given), pack(given, "m_"), pack(given, "v_")], [F32] * 4, name="adamw_small", tr=8 * 47)
    off = 0
    for k in SMALL:
        sz = given[k].size
        out[k] = [r.reshape(-1)[off:off + sz].reshape(given[k].shape) for r in res]
        off += sz

    grads = [out[k][0] for k in order]
    deltas = [out[k][1] for k in order]
    new_m = [out[k][2] for k in order]
    new_v = [out[k][3] for k in order]
    return (loss, dx[None], *grads, *deltas, *new_m, *new_v)
```

```python
import functools
import math

import jax
import jax.numpy as jnp
from jax import lax
from jax.experimental import pallas as pl
from jax.experimental.pallas import tpu as pltpu

F32 = jnp.float32
BF16 = jnp.bfloat16

D_MODEL = 1024
SEQ = 4096
DEPTH = 2
CHUNK = 128
RET_HEADS = 4
BRANCH_W = 512
N_IN = 7680
D_FF = 4096
LN_EPS = 1e-5
ROPE_BASE = 10000.0
ALPHA = (2 * DEPTH) ** 0.25
RET_SCALE = 128 ** -0.5
SB_SCALE = 64 ** -0.5
C_RET, C_SB, C_SGU, C_GATE = 0, 2048, 3584, 4608

ADAM_LR, ADAM_B1, ADAM_B2, ADAM_EPS, ADAM_WD, ADAM_STEP = 0.001, 0.9, 0.999, 1e-08, 0.01, 10

N_CHIPS = 4
VMEM_LIMIT = 56 * 1024 * 1024
MESH = pl.DeviceIdType.MESH

NN = ((1,), (0,))
NT = ((1,), (1,))
TN = ((0,), (0,))


def _dot(a, b, dims):
    return lax.dot_general(a, b, (dims, ((), ())), preferred_element_type=F32)


def _params(sem):
    return pltpu.CompilerParams(dimension_semantics=sem, vmem_limit_bytes=VMEM_LIMIT)


def _relu2(h):
    r = jnp.maximum(h, 0.0)
    return r * r


def matmul(a, b, *, mode, tm, tn, tk, outs=((F32, None),), pro=None, epi=None, tiles=(), rows=(), name, job=None):
    if mode == "nn":
        (M, K), N = a.shape, b.shape[1]
    elif mode == "nt":
        (M, K), N = a.shape, b.shape[0]
    else:
        (K, M), N = a.shape, b.shape[1]
    tm, tn, tk = min(tm, M), min(tn, N), min(tk, K)
    assert M % tm == 0 and N % tn == 0 and K % tk == 0, (name, M, N, K, tm, tn, tk)
    if mode == "nn":
        a_spec = pl.BlockSpec((tm, tk), lambda i, j, k: (i, k))
        b_spec = pl.BlockSpec((tk, tn), lambda i, j, k: (k, j))
        dims = NN
    elif mode == "nt":
        a_spec = pl.BlockSpec((tm, tk), lambda i, j, k: (i, k))
        b_spec = pl.BlockSpec((tn, tk), lambda i, j, k: (j, k))
        dims = NT
    else:
        a_spec = pl.BlockSpec((tk, tm), lambda i, j, k: (k, i))
        b_spec = pl.BlockSpec((tk, tn), lambda i, j, k: (k, j))
        dims = TN
    nk = K // tk
    nt_, nr, no = len(tiles), len(rows), len(outs)

    def body(a_ref, b_ref, *rest):
        tile_refs = rest[:nt_]
        row_refs = rest[nt_:nt_ + nr]
        out_refs = rest[nt_ + nr:nt_ + nr + no]
        av = a_ref[...]
        if pro is not None:
            av = pro(av)
        p = _dot(av.astype(BF16), b_ref[...].astype(BF16), dims)

        def finish(acc):
            vals = (acc,) * no if epi is None else epi(acc, *[r[...] for r in tile_refs], *[r[...] for r in row_refs])
            for o_ref, v in zip(out_refs, vals):
                o_ref[...] = v.astype(o_ref.dtype)

        if nk == 1:
            finish(p)
        else:
            acc_ref = rest[-1]
            k = pl.program_id(2)

            @pl.when(k == 0)
            def _():
                acc_ref[...] = p

            @pl.when(k > 0)
            def _():
                acc_ref[...] += p

            @pl.when(k == nk - 1)
            def _():
                finish(acc_ref[...])

    out_shape, out_specs = [], []
    for dt, width in outs:
        if width is None:
            out_shape.append(jax.ShapeDtypeStruct((M, N), dt))
            out_specs.append(pl.BlockSpec((tm, tn), lambda i, j, k: (i, j)))
        else:
            assert N == tn
            out_shape.append(jax.ShapeDtypeStruct((M, width), dt))
            out_specs.append(pl.BlockSpec((tm, width), lambda i, j, k: (i, 0)))
    in_specs = [a_spec, b_spec]
    in_specs += [pl.BlockSpec((tm, tn), lambda i, j, k: (i, j)) for _ in tiles]
    in_specs += [pl.BlockSpec((1, tn), lambda i, j, k: (0, j)) for _ in rows]
    grid = (M // tm, N // tn, nk)
    scratch = [pltpu.VMEM((tm, tn), F32)] if nk > 1 else []
    j = _job_args(job, len(in_specs), no)
    res = pl.pallas_call(
        _hosting(body, job, len(in_specs), no, len(scratch), grid), name=name, grid=grid,
        in_specs=in_specs + j["in_specs"], out_specs=out_specs + j["out_specs"], out_shape=out_shape + j["out_shape"],
        scratch_shapes=scratch + j["scratch"], input_output_aliases=j["aliases"],
        compiler_params=_params(("parallel", "parallel", "arbitrary") if job is None else ("arbitrary",) * 3),
    )(a, b, *tiles, *rows, *j["ins"])
    mine = res[0] if no == 1 else list(res[:no])
    return mine if job is None else (mine, list(res[no:]))


def _ln_epi(acc, res, g, b):
    u = ALPHA * res + acc
    mu = jnp.mean(u, axis=-1, keepdims=True)
    xc = u - mu
    var = jnp.mean(xc * xc, axis=-1, keepdims=True)
    rstd = lax.rsqrt(var + LN_EPS)
    xhat = xc * rstd
    return xhat * g + b, xhat, jnp.broadcast_to(rstd, (u.shape[0], 128))


def matmul_ln(a, w, res, g, b, *, pro=None, tk, name, job=None):
    n = w.shape[1]
    return matmul(a, w, mode="nn", tm=512, tn=n, tk=tk, pro=pro, epi=_ln_epi, tiles=(res,), rows=(g, b),
                  outs=((F32, None), (F32, None), (F32, 128)), name=name, job=job)


def ln_bwd(dy, xhat, rstd, g, *, name):
    T, D = dy.shape
    tm = min(512, T)

    def body(dy_ref, xh_ref, rs_ref, g_ref, du_ref, dg_ref, db_ref):
        dyv, xh = dy_ref[...], xh_ref[...]
        r = rs_ref[:, 0:1]
        dxh = dyv * g_ref[...]
        m1 = jnp.mean(dxh, axis=-1, keepdims=True)
        m2 = jnp.mean(dxh * xh, axis=-1, keepdims=True)
        du_ref[...] = r * (dxh - m1 - xh * m2)

        @pl.when(pl.program_id(0) == 0)
        def _():
            dg_ref[...] = jnp.zeros_like(dg_ref)
            db_ref[...] = jnp.zeros_like(db_ref)

        dg_ref[...] += jnp.sum(dyv * xh, axis=0, keepdims=True)
        db_ref[...] += jnp.sum(dyv, axis=0, keepdims=True)

    row = pl.BlockSpec((tm, D), lambda i: (i, 0))
    vec = pl.BlockSpec((1, D), lambda i: (0, 0))
    return pl.pallas_call(
        body, name=name, grid=(T // tm,),
        in_specs=[row, row, pl.BlockSpec((tm, 128), lambda i: (i, 0)), vec],
        out_specs=[row, vec, vec],
        out_shape=[jax.ShapeDtypeStruct((T, D), F32), jax.ShapeDtypeStruct((1, D), F32), jax.ShapeDtypeStruct((1, D), F32)],
        compiler_params=_params(("arbitrary",)),
    )(dy, xhat, rstd, g)


def loss_head(y, target):
    T, D = y.shape
    tm = min(512, T)

    def body(y_ref, t_ref, dy_ref, s_ref):
        e = y_ref[...] - t_ref[...]
        dy_ref[...] = e * (1.0 / D)

        @pl.when(pl.program_id(0) == 0)
        def _():
            s_ref[...] = jnp.zeros_like(s_ref)

        s_ref[...] += jnp.sum(jnp.mean(e * e, axis=-1, keepdims=True))

    row = pl.BlockSpec((tm, D), lambda i: (i, 0))
    return pl.pallas_call(
        body, name="loss_head", grid=(T // tm,),
        in_specs=[row, row], out_specs=[row, pl.BlockSpec((8, 128), lambda i: (0, 0))],
        out_shape=[jax.ShapeDtypeStruct((T, D), F32), jax.ShapeDtypeStruct((8, 128), F32)],
        compiler_params=_params(("arbitrary",)),
    )(y, target)


def _rope_tables(T):
    half = 64
    inv_freq = ROPE_BASE ** (-jnp.arange(half, dtype=F32) / half)
    ang = jnp.arange(T, dtype=jnp.int32).astype(F32)[:, None] * inv_freq[None, :]
    cos, sin = jnp.cos(ang), jnp.sin(ang)
    return jnp.concatenate([cos, cos], axis=1), jnp.concatenate([-sin, sin], axis=1)


def _ret_consts():
    H = RET_HEADS
    log_g = jnp.log(1.0 - 2.0 ** (-5.0 - jnp.arange(H, dtype=F32)))
    idx = jnp.arange(CHUNK, dtype=F32)
    diff = idx[:, None] - idx[None, :]
    dmat = jnp.where(diff[None] >= 0, jnp.exp(log_g[:, None, None] * diff[None]), 0.0)
    kd = jnp.exp(log_g[:, None] * (CHUNK - 1 - idx)[None, :])
    qd = jnp.exp(log_g[:, None] * (idx + 1.0)[None, :])
    cd = jnp.exp(log_g * CHUNK)
    full = (H, CHUNK, CHUNK)
    return (dmat.astype(F32), jnp.broadcast_to(kd[:, :, None], full), jnp.broadcast_to(qd[:, :, None], full),
            jnp.broadcast_to(cd[:, None, None], full))


def _swap_halves(v):
    return pltpu.roll(v, 64, 1)


def _group_norm(o):
    mu = jnp.mean(o, axis=-1, keepdims=True)
    xc = o - mu
    var = jnp.mean(xc * xc, axis=-1, keepdims=True)
    rstd = lax.rsqrt(var + LN_EPS)
    return xc * rstd, rstd


def ret_fwd(proj, cosf, sinf, consts, gn_g, gn_b, *, name):
    T = proj.shape[0]
    tb = min(512, T)
    nch = tb // CHUNK
    H = RET_HEADS

    def body(p_ref, cos_ref, sin_ref, dm_ref, kd_ref, qd_ref, cd_ref, g_ref, b_ref, out_ref, raw_ref, st_ref, s_ref):
        @pl.when(pl.program_id(0) == 0)
        def _():
            s_ref[...] = jnp.zeros_like(s_ref)

        for c in range(nch):
            r = slice(c * CHUNK, (c + 1) * CHUNK)
            cs, sn = cos_ref[r, :], sin_ref[r, :]
            for h in range(H):
                hc = slice(h * 128, (h + 1) * 128)
                q = p_ref[r, h * 128:(h + 1) * 128]
                k = p_ref[r, 512 + h * 128:512 + (h + 1) * 128]
                v = p_ref[r, 1024 + h * 128:1024 + (h + 1) * 128]
                gt = p_ref[r, 1536 + h * 128:1536 + (h + 1) * 128]
                qr = q * cs + _swap_halves(q) * sn
                kr = (k * cs + _swap_halves(k) * sn) * RET_SCALE
                sprev = s_ref[h]
                st_ref[c, h] = sprev
                qb, kb, vb = qr.astype(BF16), kr.astype(BF16), v.astype(BF16)
                s = _dot(qb, kb, NT) * dm_ref[h]
                o = _dot(s.astype(BF16), vb, NN) + _dot((qr * qd_ref[h]).astype(BF16), sprev.astype(BF16), NN)
                s_ref[h] = sprev * cd_ref[h] + _dot((kr * kd_ref[h]).astype(BF16), vb, TN)
                raw_ref[r, hc] = o
                y, _ = _group_norm(o)
                out_ref[r, hc] = (gt * jax.nn.sigmoid(gt)) * (y * g_ref[:, hc] + b_ref[:, hc])

    cmat = pl.BlockSpec((H, CHUNK, CHUNK), lambda i: (0, 0, 0))
    vec = pl.BlockSpec((1, BRANCH_W), lambda i: (0, 0))
    rope = pl.BlockSpec((tb, 128), lambda i: (i, 0))
    blk = pl.BlockSpec((tb, BRANCH_W), lambda i: (i, 0))
    return pl.pallas_call(
        body, name=name, grid=(T // tb,),
        in_specs=[pl.BlockSpec((tb, 2048), lambda i: (i, 0)), rope, rope, cmat, cmat, cmat, cmat, vec, vec],
        out_specs=[blk, blk, pl.BlockSpec((nch, H, CHUNK, CHUNK), lambda i: (i, 0, 0, 0))],
        out_shape=[jax.ShapeDtypeStruct((T, BRANCH_W), F32), jax.ShapeDtypeStruct((T, BRANCH_W), F32),
                   jax.ShapeDtypeStruct((T // CHUNK, H, CHUNK, CHUNK), F32)],
        scratch_shapes=[pltpu.VMEM((H, CHUNK, CHUNK), F32)],
        compiler_params=_params(("arbitrary",)),
    )(proj, cosf, sinf, *consts, gn_g, gn_b)


def ret_bwd(proj, cosf, sinf, consts, gn_g, gn_b, raw, states, dout, *, name, job=None):
    T = proj.shape[0]
    tb = min(512, T)
    nch = tb // CHUNK
    nb = T // tb
    H = RET_HEADS

    def body(p_ref, cos_ref, sin_ref, dm_ref, kd_ref, qd_ref, cd_ref, g_ref, b_ref, raw_ref, st_ref, do_ref,
             dp_ref, dg_ref, db_ref, ds_ref):
        @pl.when(pl.program_id(0) == 0)
        def _():
            ds_ref[...] = jnp.zeros_like(ds_ref)
            dg_ref[...] = jnp.zeros_like(dg_ref)
            db_ref[...] = jnp.zeros_like(db_ref)

        for c in reversed(range(nch)):
            r = slice(c * CHUNK, (c + 1) * CHUNK)
            cs, sn = cos_ref[r, :], sin_ref[r, :]
            for h in range(H):
                hc = slice(h * 128, (h + 1) * 128)
                q = p_ref[r, h * 128:(h + 1) * 128]
                k = p_ref[r, 512 + h * 128:512 + (h + 1) * 128]
                v = p_ref[r, 1024 + h * 128:1024 + (h + 1) * 128]
                gt = p_ref[r, 1536 + h * 128:1536 + (h + 1) * 128]
                qr = q * cs + _swap_halves(q) * sn
                kr = (k * cs + _swap_halves(k) * sn) * RET_SCALE
                sprev = st_ref[c, h]
                gv = g_ref[:, hc]
                y, rstd = _group_norm(raw_ref[r, hc])
                d_out = do_ref[r, hc]
                sg = jax.nn.sigmoid(gt)
                d_gate = d_out * (y * gv + b_ref[:, hc]) * (sg * (1.0 + gt * (1.0 - sg)))
                d_aff = d_out * (gt * sg)
                dg_ref[:, hc] += jnp.sum(d_aff * y, axis=0, keepdims=True)
                db_ref[:, hc] += jnp.sum(d_aff, axis=0, keepdims=True)
                dxh = d_aff * gv
                m1 = jnp.mean(dxh, axis=-1, keepdims=True)
                m2 = jnp.mean(dxh * y, axis=-1, keepdims=True)
                d_o = (rstd * (dxh - m1 - y * m2)).astype(BF16)
                qb, kb, vb = qr.astype(BF16), kr.astype(BF16), v.astype(BF16)
                dm, kd, qd = dm_ref[h], kd_ref[h], qd_ref[h]
                p = (_dot(qb, kb, NT) * dm).astype(BF16)
                dp = (_dot(d_o, vb, NT) * dm).astype(BF16)
                dsn = ds_ref[h]
                dsb = dsn.astype(BF16)
                dq_r = _dot(dp, kb, NN) + _dot(d_o, sprev.astype(BF16), NT) * qd
                dk_r = (_dot(dp, qb, TN) + _dot(vb, dsb, NT) * kd) * RET_SCALE
                d_v = _dot(p, d_o, TN) + _dot((kr * kd).astype(BF16), dsb, NN)
                ds_ref[h] = dsn * cd_ref[h] + _dot((qr * qd).astype(BF16), d_o, TN)
                dp_ref[r, h * 128:(h + 1) * 128] = (dq_r * cs - _swap_halves(dq_r) * sn).astype(BF16)
                dp_ref[r, 512 + h * 128:512 + (h + 1) * 128] = (dk_r * cs - _swap_halves(dk_r) * sn).astype(BF16)
                dp_ref[r, 1024 + h * 128:1024 + (h + 1) * 128] = d_v.astype(BF16)
                dp_ref[r, 1536 + h * 128:1536 + (h + 1) * 128] = d_gate.astype(BF16)

    cmat = pl.BlockSpec((H, CHUNK, CHUNK), lambda i: (0, 0, 0))
    vec = pl.BlockSpec((1, BRANCH_W), lambda i: (0, 0))
    rope = pl.BlockSpec((tb, 128), lambda i: (nb - 1 - i, 0))
    blk = pl.BlockSpec((tb, BRANCH_W), lambda i: (nb - 1 - i, 0))
    wide = pl.BlockSpec((tb, 2048), lambda i: (nb - 1 - i, 0))
    j = _job_args(job, 12, 3)
    res = pl.pallas_call(
        _hosting(body, job, 12, 3, 1, nb), name=name, grid=(nb,),
        in_specs=[wide, rope, rope, cmat, cmat, cmat, cmat, vec, vec, blk,
                  pl.BlockSpec((nch, H, CHUNK, CHUNK), lambda i: (nb - 1 - i, 0, 0, 0)), blk] + j["in_specs"],
        out_specs=[wide, vec, vec] + j["out_specs"],
        out_shape=[jax.ShapeDtypeStruct((T, 2048), BF16), jax.ShapeDtypeStruct((1, BRANCH_W), F32),
                   jax.ShapeDtypeStruct((1, BRANCH_W), F32)] + j["out_shape"],
        scratch_shapes=[pltpu.VMEM((H, CHUNK, CHUNK), F32)] + j["scratch"], input_output_aliases=j["aliases"],
        compiler_params=_params(("arbitrary",)),
    )(proj, cosf, sinf, *consts, gn_g, gn_b, raw, states, dout, *j["ins"])
    return res[0], res[1], res[2], list(res[3:])


def _sb_masks():
    row = lax.broadcasted_iota(jnp.int32, (CHUNK, CHUNK), 0)
    lane = lax.broadcasted_iota(jnp.int32, (CHUNK, CHUNK), 1)
    return row, lane


SB_QT = 256
SB_DEAD = -105.0


def _pair(v):
    hi = v.astype(BF16)
    return jnp.concatenate([hi, (v - hi.astype(F32)).astype(BF16)], axis=1)


def _sb_consts():
    r = lax.broadcasted_iota(jnp.int32, (256, 256), 0) & 127
    c = lax.broadcasted_iota(jnp.int32, (256, 256), 1)
    ones = c >= 128
    lane = lax.broadcasted_iota(jnp.int32, (CHUNK, CHUNK), 1)
    return (ones | (r > c)).astype(BF16), (ones | (r >= c)).astype(BF16), (lane < 64, lane >= 64)


def _per_head(x, hms):
    return jnp.concatenate([jnp.where(hm, x, 0.0) for hm in hms], axis=0).astype(BF16)


def _sb_logits(qb, kb2, mask2):
    z = _dot(qb, kb2, NT)
    l1p = jnp.log(1.0 + jnp.exp(-jnp.abs(z)))
    lsp = jnp.minimum(z, 0.0) - l1p
    lsn = lsp - z
    if mask2 is not None:
        lsn = jnp.where(mask2, lsn, 0.0)
    return lsp, lsn


def _sb_tile_mask(qt):
    trow = lax.broadcasted_iota(jnp.int32, (qt, 256), 0)
    tlane = lax.broadcasted_iota(jnp.int32, (qt, 256), 1) & 127
    return lambda m: (tlane + m * CHUNK) < trow


def sb_fwd(proj, *, name, job=None):
    T = proj.shape[0]
    qt = min(SB_QT, T)
    nsub = qt // CHUNK
    cb = C_SB // 128

    def body(q_ref, k_ref, v_ref, o_ref):
        u_gt, _, hms = _sb_consts()
        tile_mask = _sb_tile_mask(qt)

        def qtile(i, _):
            rq = pl.ds(pl.multiple_of(i * qt, qt), qt)
            qb = (q_ref[rq, :] * SB_SCALE).astype(BF16)

            def step(j, state, mask2):
                carry, acc = list(state[:2]), state[2]
                rk = pl.ds(pl.multiple_of(j * CHUNK, CHUNK), CHUNK)
                lsp, lsn = _sb_logits(qb, _per_head(k_ref[rk, :], hms), mask2)
                a_b = []
                for h in range(2):
                    hc = slice(h * 128, (h + 1) * 128)
                    r = _dot(_pair(lsn[:, hc]), u_gt, NN)
                    a = jnp.exp(lsp[:, hc] + r[:, :128] + carry[h])
                    if mask2 is not None:
                        a = jnp.where(mask2[:, hc], a, 0.0)
                    carry[h] = carry[h] + r[:, 128:]
                    a_b.append(a.astype(BF16))
                acc = acc + _dot(jnp.concatenate(a_b, axis=1), _per_head(v_ref[rk, :], hms), NN)
                return carry[0], carry[1], acc

            zero = jnp.zeros((qt, 128), F32)
            state = (zero, zero, zero)
            for m in reversed(range(nsub)):
                state = step(i * nsub + m, state, tile_mask(m))

            def live(c):
                return jnp.logical_and(c[0] < i, jnp.maximum(jnp.max(c[1][0]), jnp.max(c[1][1])) > SB_DEAD)

            def blocks(c):
                jj, st = c
                for u in range(nsub):
                    st = step((i - jj) * nsub - 1 - u, st, None)
                return jj + 1, st

            _, state = lax.while_loop(live, blocks, (jnp.int32(0), state))
            o_ref[rq, :] = state[2]
            return 0

        lax.fori_loop(0, T // qt, qtile, 0)

    def col(off):
        return pl.BlockSpec((T, 128), lambda hp: (0, off + hp))

    steps = BRANCH_W // 128
    j = _job_args(job, 3, 1)
    res = pl.pallas_call(
        _hosting(body, job, 3, 1, 0, steps), name=name, grid=(steps,),
        in_specs=[col(cb), col(cb + 4), col(cb + 8)] + j["in_specs"], out_specs=[col(0)] + j["out_specs"],
        out_shape=[jax.ShapeDtypeStruct((T, BRANCH_W), F32)] + j["out_shape"],
        scratch_shapes=j["scratch"], input_output_aliases=j["aliases"],
        compiler_params=_params(("parallel",) if job is None else ("arbitrary",)),
    )(proj, proj, proj, *j["ins"])
    return res[0], list(res[1:])


def sb_bwd(proj, out, dout, *, name, job=None):
    T = proj.shape[0]
    qt = min(SB_QT, T)
    nsub = qt // CHUNK
    cb = C_SB // 128

    def body(q_ref, k_ref, v_ref, o_ref, do_ref, dq_ref, dk_ref, dv_ref, dkt_ref, dvt_ref):
        u_gt, u_ge, hms = _sb_consts()
        tile_mask = _sb_tile_mask(qt)
        tall_lane = lax.broadcasted_iota(jnp.int32, (qt, 128), 1)
        top = lax.broadcasted_iota(jnp.int32, (CHUNK, CHUNK), 0) < 64
        dkt_ref[...] = jnp.zeros_like(dkt_ref)
        dvt_ref[...] = jnp.zeros_like(dvt_ref)

        def qtile(i, _):
            rq = pl.ds(pl.multiple_of(i * qt, qt), qt)
            qs = q_ref[rq, :] * SB_SCALE
            qb, q_t = qs.astype(BF16), qs.T.astype(BF16)
            dov = do_ref[rq, :]
            dob, do_t = dov.astype(BF16), dov.T.astype(BF16)
            prod = dob.astype(F32) * o_ref[rq, :]
            total = [jnp.broadcast_to(jnp.sum(jnp.where(hm, prod, 0.0), axis=1, keepdims=True), (qt, 128))
                     for hm in (tall_lane < 64, tall_lane >= 64)]

            def step(j, state, mask2):
                c_l, c_w, dq = list(state[:2]), list(state[2:4]), state[4]
                rk = pl.ds(pl.multiple_of(j * CHUNK, CHUNK), CHUNK)
                kb2, vb2 = _per_head(k_ref[rk, :], hms), _per_head(v_ref[rk, :], hms)
                lsp, lsn = _sb_logits(qb, kb2, mask2)
                da = _dot(dob, vb2, NT)
                sp = jnp.exp(lsp)
                a_b, dz_b = [], []
                for h in range(2):
                    hc = slice(h * 128, (h + 1) * 128)
                    r = _dot(_pair(lsn[:, hc]), u_gt, NN)
                    a = jnp.exp(lsp[:, hc] + r[:, :128] + c_l[h])
                    if mask2 is not None:
                        a = jnp.where(mask2[:, hc], a, 0.0)
                    c_l[h] = c_l[h] + r[:, 128:]
                    a = a.astype(BF16)
                    w = a.astype(F32) * da[:, hc]
                    r = _dot(_pair(w), u_ge, NN)
                    later_w = r[:, :128] + c_w[h]
                    c_w[h] = c_w[h] + r[:, 128:]
                    dz = w * (1.0 - sp[:, hc]) - sp[:, hc] * (total[h] - later_w)
                    if mask2 is not None:
                        dz = jnp.where(mask2[:, hc], dz, 0.0)
                    a_b.append(a)
                    dz_b.append(dz.astype(BF16))
                a_b, dz_b = jnp.concatenate(a_b, axis=1), jnp.concatenate(dz_b, axis=1)
                dkt = _dot(q_t, dz_b, NN)
                dvt = _dot(do_t, a_b, NN)
                dkt_ref[j] += jnp.where(top, dkt[:, :128], dkt[:, 128:])
                dvt_ref[j] += jnp.where(top, dvt[:, :128], dvt[:, 128:])
                return c_l[0], c_l[1], c_w[0], c_w[1], dq + _dot(dz_b, kb2, NN)

            zero = jnp.zeros((qt, 128), F32)
            state = (zero,) * 5
            for m in reversed(range(nsub)):
                state = step(i * nsub + m, state, tile_mask(m))

            def live(c):
                return jnp.logical_and(c[0] < i, jnp.maximum(jnp.max(c[1][0]), jnp.max(c[1][1])) > SB_DEAD)

            def blocks(c):
                jj, st = c
                for u in range(nsub):
                    st = step((i - jj) * nsub - 1 - u, st, None)
                return jj + 1, st

            _, state = lax.while_loop(live, blocks, (jnp.int32(0), state))
            dq_ref[rq, :] = (state[4] * SB_SCALE).astype(BF16)
            return 0

        lax.fori_loop(0, T // qt, qtile, 0)

        def untranspose(jb, _):
            rk = pl.ds(pl.multiple_of(jb * CHUNK, CHUNK), CHUNK)
            dk_ref[rk, :] = dkt_ref[jb].T.astype(BF16)
            dv_ref[rk, :] = dvt_ref[jb].T.astype(BF16)
            return 0

        lax.fori_loop(0, T // CHUNK, untranspose, 0)

    def col(off):
        return pl.BlockSpec((T, 128), lambda hp: (0, off + hp))

    o16 = jax.ShapeDtypeStruct((T, BRANCH_W), BF16)
    steps = BRANCH_W // 128
    j = _job_args(job, 5, 3)
    acc = pltpu.VMEM((T // CHUNK, CHUNK, CHUNK), F32)
    res = pl.pallas_call(
        _hosting(body, job, 5, 3, 2, steps), name=name, grid=(steps,),
        in_specs=[col(cb), col(cb + 4), col(cb + 8), col(0), col(0)] + j["in_specs"],
        out_specs=[col(0), col(0), col(0)] + j["out_specs"], out_shape=[o16, o16, o16] + j["out_shape"],
        scratch_shapes=[acc, acc] + j["scratch"], input_output_aliases=j["aliases"],
        compiler_params=_params(("parallel",) if job is None else ("arbitrary",)),
    )(proj, proj, proj, out, dout, *j["ins"])
    return res[0], res[1], res[2], list(res[3:])


_G0 = math.sqrt(2.0 / math.pi)
_G1 = 0.044715


def _gelu(x):
    return 0.5 * x * (1.0 + jnp.tanh(_G0 * (x + _G1 * x * x * x)))


def _gelu_grad(x):
    t = jnp.tanh(_G0 * (x + _G1 * x * x * x))
    return 0.5 * (1.0 + t) + 0.5 * x * (1.0 - t * t) * (_G0 * (1.0 + 3.0 * _G1 * x * x))


def _tril():
    row, lane = _sb_masks()
    return row >= lane


def sgu_fwd(proj, ln_g, ln_b, w, bias, *, name):
    T = proj.shape[0]
    tb = min(512, T)
    G = BRANCH_W // 128

    def body(u_ref, v_ref, g_ref, b_ref, w_ref, bias_ref, o_ref):
        vv = _gelu(v_ref[...])
        xh, _ = _group_norm(vv)
        vn = (xh * g_ref[...] + b_ref[...]).astype(BF16)
        tril = _tril()
        for g in range(G):
            wg = jnp.where(tril, w_ref[g], 0.0).astype(BF16)
            gc = slice(g * 128, (g + 1) * 128)
            for c in range(tb // CHUNK):
                r = slice(c * CHUNK, (c + 1) * CHUNK)
                sv = _dot(wg, vn[r, gc], NN) + bias_ref[g]
                o_ref[r, gc] = _gelu(u_ref[r, gc]) * sv

    cu, cv = C_SGU // BRANCH_W, C_SGU // BRANCH_W + 1
    vec = pl.BlockSpec((1, BRANCH_W), lambda i: (0, 0))
    mat = pl.BlockSpec((G, CHUNK, CHUNK), lambda i: (0, 0, 0))
    return pl.pallas_call(
        body, name=name, grid=(T // tb,),
        in_specs=[pl.BlockSpec((tb, BRANCH_W), lambda i: (i, cu)), pl.BlockSpec((tb, BRANCH_W), lambda i: (i, cv)),
                  vec, vec, mat, mat],
        out_specs=pl.BlockSpec((tb, BRANCH_W), lambda i: (i, 0)),
        out_shape=jax.ShapeDtypeStruct((T, BRANCH_W), F32),
        compiler_params=_params(("parallel",)),
    )(proj, proj, ln_g, ln_b, w, bias)


def sgu_bwd(proj, ln_g, ln_b, w, bias, dout, *, name):
    T = proj.shape[0]
    tb = min(512, T)
    G = BRANCH_W // 128

    def body(u_ref, v_ref, g_ref, b_ref, w_ref, bias_ref, do_ref, dp_ref, dw_ref, dbias_ref, dg_ref, db_ref, dvn_ref):
        @pl.when(pl.program_id(0) == 0)
        def _():
            dw_ref[...] = jnp.zeros_like(dw_ref)
            dbias_ref[...] = jnp.zeros_like(dbias_ref)
            dg_ref[...] = jnp.zeros_like(dg_ref)
            db_ref[...] = jnp.zeros_like(db_ref)

        gv = v_ref[...]
        vv = _gelu(gv)
        xh, rstd = _group_norm(vv)
        vn = (xh * g_ref[...] + b_ref[...]).astype(BF16)
        tril = _tril()
        for g in range(G):
            wg = jnp.where(tril, w_ref[g], 0.0).astype(BF16)
            gc = slice(g * 128, (g + 1) * 128)
            for c in range(tb // CHUNK):
                r = slice(c * CHUNK, (c + 1) * CHUNK)
                vn_c = vn[r, gc]
                sv = _dot(wg, vn_c, NN) + bias_ref[g]
                gu = u_ref[r, gc]
                d_o = do_ref[r, gc]
                dp_ref[r, gc] = (d_o * sv * _gelu_grad(gu)).astype(BF16)
                dsv = d_o * _gelu(gu)
                dsv_b = dsv.astype(BF16)
                dvn_ref[r, gc] = _dot(wg, dsv_b, TN)
                dw_ref[g] += jnp.where(tril, _dot(dsv_b, vn_c, NT), 0.0)
                dbias_ref[g] += jnp.broadcast_to(jnp.sum(dsv, axis=1, keepdims=True), (CHUNK, CHUNK))
        dvn = dvn_ref[...]
        dg_ref[...] += jnp.sum(dvn * xh, axis=0, keepdims=True)
        db_ref[...] += jnp.sum(dvn, axis=0, keepdims=True)
        dxh = dvn * g_ref[...]
        m1 = jnp.mean(dxh, axis=-1, keepdims=True)
        m2 = jnp.mean(dxh * xh, axis=-1, keepdims=True)
        dp_ref[:, BRANCH_W:2 * BRANCH_W] = (rstd * (dxh - m1 - xh * m2) * _gelu_grad(gv)).astype(BF16)

    cu, cv = C_SGU // BRANCH_W, C_SGU // BRANCH_W + 1
    vec = pl.BlockSpec((1, BRANCH_W), lambda i: (0, 0))
    mat = pl.BlockSpec((G, CHUNK, CHUNK), lambda i: (0, 0, 0))
    blk = pl.BlockSpec((tb, BRANCH_W), lambda i: (i, 0))
    msh = jax.ShapeDtypeStruct((G, CHUNK, CHUNK), F32)
    vsh = jax.ShapeDtypeStruct((1, BRANCH_W), F32)
    return pl.pallas_call(
        body, name=name, grid=(T // tb,),
        in_specs=[pl.BlockSpec((tb, BRANCH_W), lambda i: (i, cu)), pl.BlockSpec((tb, BRANCH_W), lambda i: (i, cv)),
                  vec, vec, mat, mat, blk],
        out_specs=[pl.BlockSpec((tb, 2 * BRANCH_W), lambda i: (i, 0)), mat, mat, vec, vec],
        out_shape=[jax.ShapeDtypeStruct((T, 2 * BRANCH_W), BF16), msh, msh, vsh, vsh],
        scratch_shapes=[pltpu.VMEM((tb, BRANCH_W), F32)],
        compiler_params=_params(("arbitrary",)),
    )(proj, proj, ln_g, ln_b, w, bias, dout)


def merge_fwd(a1, a2, a3, p1, p2, p3, proj, *, name):
    T = a1.shape[0]
    tm, tn = min(1024, T), 512
    gb = C_GATE // tn

    def body(a1_ref, a2_ref, a3_ref, p1_ref, p2_ref, p3_ref, g1_ref, g2_ref, g3_ref, m_ref, r1_ref, r2_ref, r3_ref):
        m = None
        for a_ref, p_ref, g_ref, r_ref in ((a1_ref, p1_ref, g1_ref, r1_ref), (a2_ref, p2_ref, g2_ref, r2_ref),
                                           (a3_ref, p3_ref, g3_ref, r3_ref)):
            r = _dot(a_ref[...].astype(BF16), p_ref[...], NN)
            r_ref[...] = r
            t = jax.nn.sigmoid(g_ref[...]) * r
            m = t if m is None else m + t
        m_ref[...] = m

    a_spec = pl.BlockSpec((tm, BRANCH_W), lambda i, j: (i, 0))
    p_spec = pl.BlockSpec((BRANCH_W, tn), lambda i, j: (0, j))
    o_spec = pl.BlockSpec((tm, tn), lambda i, j: (i, j))
    osh = jax.ShapeDtypeStruct((T, D_MODEL), F32)
    gates = [pl.BlockSpec((tm, tn), functools.partial(lambda i, j, o: (i, o + j), o=gb + 2 * n)) for n in range(3)]
    return pl.pallas_call(
        body, name=name, grid=(T // tm, D_MODEL // tn),
        in_specs=[a_spec, a_spec, a_spec, p_spec, p_spec, p_spec, *gates],
        out_specs=[o_spec] * 4, out_shape=[osh] * 4,
        compiler_params=_params(("parallel", "parallel")),
    )(a1, a2, a3, p1, p2, p3, proj, proj, proj)


def merge_bwd(dm, r1, r2, r3, proj, *, name):
    T = dm.shape[0]
    tm, tn = min(512, T), 512
    gb = C_GATE // tn

    def body(dm_ref, r1_ref, r2_ref, r3_ref, g1_ref, g2_ref, g3_ref, dr1_ref, dr2_ref, dr3_ref, dg1_ref, dg2_ref, dg3_ref):
        d = dm_ref[...]
        for r_ref, g_ref, dr_ref, dg_ref in ((r1_ref, g1_ref, dr1_ref, dg1_ref), (r2_ref, g2_ref, dr2_ref, dg2_ref),
                                             (r3_ref, g3_ref, dr3_ref, dg3_ref)):
            s = jax.nn.sigmoid(g_ref[...])
            dr_ref[...] = (d * s).astype(BF16)
            dg_ref[...] = (d * r_ref[...] * (s * (1.0 - s))).astype(BF16)

    o_spec = pl.BlockSpec((tm, tn), lambda i, j: (i, j))
    osh = jax.ShapeDtypeStruct((T, D_MODEL), BF16)
    gates = [pl.BlockSpec((tm, tn), functools.partial(lambda i, j, o: (i, o + j), o=gb + 2 * n)) for n in range(3)]
    return pl.pallas_call(
        body, name=name, grid=(T // tm, D_MODEL // tn),
        in_specs=[o_spec] * 4 + gates, out_specs=[o_spec] * 6, out_shape=[osh] * 6,
        compiler_params=_params(("parallel", "parallel")),
    )(dm, r1, r2, r3, proj, proj, proj)


def _rows_call(fn, ins, out_dtypes, *, name, tr=256):
    first = ins[0][0] if isinstance(ins[0], tuple) else ins[0]
    R, C = first.shape[-2:]
    tr = min(tr, R)
    assert R % tr == 0, (name, R, tr)
    arrs, specs = [], []
    for x in ins:
        if isinstance(x, tuple):
            arrs.append(x[0])
            specs.append(pl.BlockSpec((None, tr, C), functools.partial(lambda i, n: (n, i, 0), n=x[1])))
        else:
            arrs.append(x)
            specs.append(pl.BlockSpec((tr, C), lambda i: (i, 0)))
    ni = len(arrs)

    def body(*refs):
        vals = fn(*[r[...] for r in refs[:ni]])
        for o_ref, v in zip(refs[ni:], vals):
            o_ref[...] = v.astype(o_ref.dtype)

    res = pl.pallas_call(
        body, name=name, grid=(R // tr,), in_specs=specs,
        out_specs=[pl.BlockSpec((tr, C), lambda i: (i, 0)) for _ in out_dtypes],
        out_shape=[jax.ShapeDtypeStruct((R, C), dt) for dt in out_dtypes],
        compiler_params=_params(("parallel",)),
    )(*arrs)
    return res


def _tile_rows(rows, cols):
    t = 256
    while t > 8 and (t * cols > 512 * 1024 or rows % t):
        t //= 2
    return t


def _rows_at(fn, pos, ins, outs, steps, *, name, aliases=None):
    read = [n for n, (_, s) in enumerate(ins) if s is not ANY]
    ni = len(ins)

    def body(pos_ref, *refs):
        vals = fn(*[refs[n][...] for n in read])
        for o_ref, v in zip(refs[ni:], vals):
            o_ref[...] = v.astype(o_ref.dtype)

    return pl.pallas_call(
        body, name=name,
        grid_spec=pltpu.PrefetchScalarGridSpec(num_scalar_prefetch=1, grid=(steps,), in_specs=[s for _, s in ins],
                                               out_specs=[s for _, s in outs]),
        out_shape=[sh for sh, _ in outs],
        input_output_aliases={1 + i: o for i, o in (aliases or {}).items()},
        compiler_params=_params(("parallel",)),
    )(pos, *[a for a, _ in ins])


def cast_into_whole(pos, w, l, axis, *, name):
    _, r, n = w.shape
    tr = _tile_rows(r, n)
    if axis == 1:
        shape, spec = (r, n * N_CHIPS), pl.BlockSpec((tr, n), lambda i, p: (i, p[3]))
    else:
        shape, spec = (r * N_CHIPS, n), pl.BlockSpec((tr, n), lambda i, p: (p[3] * (r // tr) + i, 0))
    return _rows_at(lambda a: (a,), pos, [(w, pl.BlockSpec((None, tr, n), lambda i, p: (l, i, 0)))],
                    [(jax.ShapeDtypeStruct(shape, BF16), spec)], r // tr, name=name)[0]


def pair_sum(pos, theirs, g32, axis, *, name):
    rows2, cols = theirs.shape
    h = rows2 // (N_CHIPS if axis == 0 else 1)
    tr = _tile_rows(h, cols)
    hb = h // tr
    if axis == 1:
        own = pl.BlockSpec((tr, cols), lambda i, p: (p[2] * hb + i, 0))
    else:
        own = pl.BlockSpec((tr, cols), lambda i, p: ((2 * (i // hb) + p[2]) * hb + i % hb, 0))
    row = pl.BlockSpec((tr, cols), lambda i, p: (i, 0))
    return _rows_at(lambda t, m: (m + t.astype(F32),) * 2, pos, [(theirs, row), (g32, own)],
                    [(jax.ShapeDtypeStruct((rows2, cols), F32), row), (jax.ShapeDtypeStruct((rows2, cols), BF16), row)],
                    rows2 // tr, name=name)


def chip_sum(pos, h32, recv, l, axis, whole, *, name):
    _, depth, h, n = recv.shape
    tr = _tile_rows(h, n)
    hb = h // tr
    if axis == 1:
        mine = pl.BlockSpec((tr, n), lambda i, p: (i, p[3]))
    else:
        mine = pl.BlockSpec((tr, n), lambda i, p: (p[3] * hb + i, 0))
    ins = [(h32, mine)] + [(recv, pl.BlockSpec((None, None, tr, n), functools.partial(lambda i, p, j: (j, l, i, 0), j=j)))
                           for j in range(3)]
    if whole is not None:
        ins.append((whole, ANY))
    return _rows_at(lambda o, a, b, c: (((o + a.astype(F32)) + b.astype(F32)) + c.astype(F32),), pos, ins,
                    [(jax.ShapeDtypeStruct((depth, 2, h, n), F32), pl.BlockSpec((None, None, tr, n), lambda i, p: (l, p[2], i, 0)))],
                    hb, name=name, aliases=None if whole is None else {4: 0})[0]


def _adamw(w, g, m, v):
    m2 = ADAM_B1 * m + (1.0 - ADAM_B1) * g
    v2 = ADAM_B2 * v + (1.0 - ADAM_B2) * (g * g)
    m_hat = m2 / (1.0 - ADAM_B1 ** ADAM_STEP)
    v_hat = v2 / (1.0 - ADAM_B2 ** ADAM_STEP)
    delta = -ADAM_LR * (m_hat / (jnp.sqrt(v_hat) + ADAM_EPS) + ADAM_WD * w)
    return delta, m2, v2


def _place():
    return lax.axis_index("x"), lax.axis_index("y"), lax.axis_index("c")


def _chip_peers(x, y, c):
    return [((1 - x, y, c), 2 * (1 - x) + y), ((x, 1 - y, c), 2 * x + 1 - y), ((1 - x, 1 - y, c), 2 * (1 - x) + 1 - y)]


def _shard_of(ref, axis, k, n):
    start = pl.multiple_of(k * n, 128)
    return ref.at[pl.ds(start, n), :] if axis == 0 else ref.at[:, pl.ds(start, n)]


ANY = pl.BlockSpec(memory_space=pl.ANY)


class CopyJob:
    def __init__(self, ins, out_shape, scratch, copies, aliases=None):
        self.ins, self.out_shape, self.scratch, self.copies = list(ins), list(out_shape), list(scratch), copies
        self.aliases = dict(aliases or {})

    def start(self, ins, outs, sems):
        local, remote, _, _ = self.copies(ins, outs, sems)
        for d in local + remote:
            d.start()

    def finish(self, ins, outs, sems):
        local, remote, arrivals, relays = self.copies(ins, outs, sems)
        for needs, sends, _ in relays:
            for d in needs:
                d.wait_recv()
            for d in sends:
                d.start()
        for d in arrivals + [d for _, _, arrives in relays for d in arrives]:
            d.wait_recv()
        for d in remote + [d for _, sends, _ in relays for d in sends]:
            d.wait_send()
        for d in local:
            d.wait()


def run_job(job, *, name):
    ni, no = len(job.ins), len(job.out_shape)

    def body(*refs):
        parts = refs[:ni], refs[ni:ni + no], refs[ni + no:]
        job.start(*parts)
        job.finish(*parts)

    return pl.pallas_call(
        body, name=name, in_specs=[ANY] * ni, out_specs=[ANY] * no, out_shape=job.out_shape,
        scratch_shapes=job.scratch, input_output_aliases=job.aliases,
    )(*job.ins)


def _job_args(job, n_in, n_out):
    if job is None:
        return dict(ins=[], in_specs=[], out_specs=[], out_shape=[], scratch=[], aliases={})
    return dict(ins=job.ins, in_specs=[ANY] * len(job.ins), out_specs=[ANY] * len(job.out_shape),
                out_shape=job.out_shape, scratch=job.scratch,
                aliases={n_in + i: n_out + o for i, o in job.aliases.items()})


def _hosting(body, job, n_in, n_out, n_scratch, grid):
    if job is None:
        return body
    ji, jo = len(job.ins), len(job.out_shape)
    grid = (grid,) if isinstance(grid, int) else tuple(grid)

    def at(ends):
        hit = None
        for ax, e in enumerate(ends):
            here = pl.program_id(ax) == e
            hit = here if hit is None else jnp.logical_and(hit, here)
        return hit

    def hosted(*refs):
        o = n_in + ji
        s = o + n_out + jo
        parts = refs[n_in:o], refs[o + n_out:s], refs[s + n_scratch:]

        @pl.when(at([0] * len(grid)))
        def _():
            job.start(*parts)

        body(*refs[:n_in], *refs[o:o + n_out], *refs[s:s + n_scratch])

        @pl.when(at([g - 1 for g in grid]))
        def _():
            job.finish(*parts)

    return hosted


def _job_sems(n_remote, n_local):
    return [pltpu.SemaphoreType.DMA((n_remote,)), pltpu.SemaphoreType.DMA((n_remote,)), pltpu.SemaphoreType.DMA((n_local,))]


def gather_job(shards, axes):
    na = len(shards)

    def copies(ins, outs, sems):
        send, recv, _ = sems
        x, y, c = _place()
        k = 2 * x + y
        remote, relays = [], []
        for a in range(na):
            r = outs[a].shape[0] // (N_CHIPS if axes[a] == 0 else 1)
            n = outs[a].shape[axes[a]] // N_CHIPS
            half = r // 2

            def part(kk, cc, a=a, n=n, half=half):
                rows = pl.ds(pl.multiple_of(cc * half + (kk * n if axes[a] == 0 else 0), 8), half)
                return outs[a].at[rows, :] if axes[a] == 0 else outs[a].at[rows, pl.ds(pl.multiple_of(kk * n, 128), n)]

            needs, passes, lands = [], [], []
            for j, (peer, kp) in enumerate(_chip_peers(x, y, c)):
                s = 6 * a + j
                remote.append(pltpu.make_async_remote_copy(part(k, c), part(k, c), send.at[s], recv.at[s],
                                                           device_id=peer, device_id_type=MESH))
                needs.append(pltpu.make_async_remote_copy(part(kp, c), part(kp, c), send.at[s], recv.at[s],
                                                          device_id=peer, device_id_type=MESH))
                passes.append(pltpu.make_async_remote_copy(part(kp, c), part(kp, c), send.at[s + 3], recv.at[s + 3],
                                                           device_id=(x, y, 1 - c), device_id_type=MESH))
                lands.append(pltpu.make_async_remote_copy(part(kp, 1 - c), part(kp, 1 - c), send.at[s + 3], recv.at[s + 3],
                                                          device_id=(x, y, 1 - c), device_id_type=MESH))
            relays.append((needs, passes, lands))
        return [], remote, [], relays

    out_shape = [jax.ShapeDtypeStruct(w.shape, BF16) for w in shards]
    return CopyJob(shards, out_shape, _job_sems(6 * na, 1), copies, {a: a for a in range(na)})


def scatter_job(layers, g16, axes, filled):
    na = len(axes)

    def shard_shape(a):
        r, c = g16[a].shape
        return (r // N_CHIPS, c) if axes[a] == 0 else (r, c // N_CHIPS)

    def copies(ins, outs, sems):
        send, recv_sems, _ = sems
        x, y, c = _place()
        remote = []
        for a in range(na):
            n = shard_shape(a)[axes[a]]
            for r, (peer, kp) in enumerate(_chip_peers(x, y, c)):
                remote.append(pltpu.make_async_remote_copy(_shard_of(ins[a], axes[a], kp, n), outs[a].at[r, layers[a]],
                                                           send.at[3 * a + r], recv_sems.at[3 * a + r],
                                                           device_id=peer, device_id_type=MESH))
        return [], remote, remote, []

    out_shape = [jax.ShapeDtypeStruct((3, DEPTH) + shard_shape(a), BF16) for a in range(na)]
    ins = list(g16)
    aliases = {}
    for a in range(na):
        if filled[a] is not None:
            aliases[len(ins)] = a
            ins.append(filled[a])
    return CopyJob(ins, out_shape, _job_sems(3 * na, 1), copies, aliases)


def pair_job(g16, axes):
    na = len(axes)
    pieces = [1 if ax == 1 else N_CHIPS for ax in axes]

    def copies(ins, outs, sems):
        send, recv, _ = sems
        x, y, c = _place()
        remote = []
        s = 0
        for a in range(na):
            rows = g16[a].shape[0] // (2 * pieces[a])
            for kk in range(pieces[a]):
                src = ins[a].at[pl.ds(pl.multiple_of((2 * kk + 1 - c) * rows, 8), rows), :]
                remote.append(pltpu.make_async_remote_copy(src, outs[a].at[pl.ds(kk * rows, rows), :], send.at[s], recv.at[s],
                                                           device_id=(x, y, 1 - c), device_id_type=MESH))
                s += 1
        return [], remote, remote, []

    out_shape = [jax.ShapeDtypeStruct((g.shape[0] // 2, g.shape[1]), BF16) for g in g16]
    return CopyJob(g16, out_shape, _job_sems(sum(pieces), 1), copies)


def join_job(shards):
    na = len(shards)

    def copies(ins, outs, sems):
        send, recv, _ = sems
        x, y, c = _place()
        remote = [pltpu.make_async_remote_copy(outs[a].at[:, c], outs[a].at[:, c], send.at[a], recv.at[a],
                                               device_id=(x, y, 1 - c), device_id_type=MESH) for a in range(na)]
        lands = [pltpu.make_async_remote_copy(outs[a].at[:, 1 - c], outs[a].at[:, 1 - c], send.at[a], recv.at[a],
                                              device_id=(x, y, 1 - c), device_id_type=MESH) for a in range(na)]
        return [], remote, lands, []

    out_shape = [jax.ShapeDtypeStruct(s.shape, F32) for s in shards]
    return CopyJob(shards, out_shape, _job_sems(na, 1), copies, {a: a for a in range(na)})


def allreduce_small(p):
    R = p.shape[0]

    def body(p_ref, o_ref, buf, send_sems, recv_sems):
        x, y, c = _place()
        me = 4 * x + 2 * y + c
        cps = []
        for rel in range(1, 8):
            dx, dy, dc = rel >> 2, (rel >> 1) & 1, rel & 1
            peer = (1 - x if dx else x, 1 - y if dy else y, 1 - c if dc else c)
            cp = pltpu.make_async_remote_copy(p_ref, buf.at[me], send_sems.at[rel - 1], recv_sems.at[rel - 1],
                                              device_id=peer, device_id_type=MESH)
            cp.start()
            cps.append((cp, 4 * peer[0] + 2 * peer[1] + peer[2]))
        buf[me] = p_ref[...]
        for rel, (cp, who) in enumerate(cps):
            pltpu.make_async_remote_copy(p_ref, buf.at[who], send_sems.at[rel], recv_sems.at[rel],
                                         device_id=(x, y, c), device_id_type=MESH).wait_recv()
        acc = buf[0]
        for d in range(1, 8):
            acc = acc + buf[d]
        o_ref[...] = acc
        for cp, _ in cps:
            cp.wait_send()

    return pl.pallas_call(
        body, name="allreduce_small",
        in_specs=[pl.BlockSpec(memory_space=pltpu.VMEM)], out_specs=pl.BlockSpec(memory_space=pltpu.VMEM),
        out_shape=jax.ShapeDtypeStruct((R, 128), F32),
        scratch_shapes=[pltpu.VMEM((8, R, 128), F32), pltpu.SemaphoreType.DMA((7,)), pltpu.SemaphoreType.DMA((7,))],
        compiler_params=pltpu.CompilerParams(vmem_limit_bytes=VMEM_LIMIT),
    )(p)


BIG = ("w_in", "p_ret", "p_sb", "p_sgu", "w_out", "w_up", "w_down")
BIG_AXIS = {"w_in": 1, "p_ret": 1, "p_sb": 1, "p_sgu": 1, "w_out": 0, "w_up": 1, "w_down": 0}
SMALL = ("ret_gn_g", "ret_gn_b", "sgu_ln_g", "sgu_ln_b", "sgu_w", "sgu_b", "ln1_g", "ln1_b", "ln2_g", "ln2_b")


def layer_forward(l, x0, W, sm, rope, rconsts, hooks):
    n = f"l{l}_"
    jobs = hooks.fwd_jobs(l)
    proj = matmul(x0, W["w_in"], mode="nn", tm=1024, tn=640, tk=1024, name=n + "proj")
    retg, raw, states = ret_fwd(proj, *rope, rconsts, sm["ret_gn_g"], sm["ret_gn_b"], name=n + "ret_fwd")
    sb, job_out = sb_fwd(proj, name=n + "sb_fwd", job=jobs.get("sb"))
    if jobs.get("sb") is not None:
        hooks.done(jobs["sb"], job_out)
    sg = sgu_fwd(proj, sm["sgu_ln_g"], sm["sgu_ln_b"], sm["sgu_w"], sm["sgu_bias"], name=n + "sgu_fwd")
    merged, r1, r2, r3 = merge_fwd(retg, sb, sg, W["p_ret"], W["p_sb"], W["p_sgu"], proj, name=n + "merge_fwd")
    x1, xh1, rs1 = matmul_ln(merged, W["w_out"], x0, sm["ln1_g"], sm["ln1_b"], tk=1024, name=n + "out_ln1")
    h1 = matmul(x1, W["w_up"], mode="nn", tm=1024, tn=1024, tk=1024, name=n + "up")
    res = matmul_ln(h1, W["w_down"], x1, sm["ln2_g"], sm["ln2_b"], pro=_relu2, tk=1024, name=n + "down_ln2",
                    job=jobs.get("down"))
    if jobs.get("down") is not None:
        res, job_out = res
        hooks.done(jobs["down"], job_out)
    x2, xh2, rs2 = res
    saved = dict(x0=x0, proj=proj, retg=retg, raw=raw, states=states, sb=sb, sg=sg, merged=merged, r=(r1, r2, r3),
                 x1=x1, xh1=xh1, rs1=rs1, h1=h1, xh2=xh2, rs2=rs2)
    return x2, saved


def layer_backward(l, dx2, s, W, sm, rope, rconsts, hooks):
    n = f"l{l}_"
    two = ((F32, None), (BF16, None))
    gw, gs = {}, {}
    du2, gs["ln2_g"], gs["ln2_b"] = ln_bwd(dx2, s["xh2"], s["rs2"], sm["ln2_g"], name=n + "ln2_bwd")
    gw["w_down"] = matmul(s["h1"], du2, mode="tn", tm=1024, tn=1024, tk=512, pro=_relu2, outs=two, name=n + "g_down")
    dh1 = matmul(du2, W["w_down"], mode="nt", tm=1024, tn=1024, tk=1024, outs=((BF16, None),),
                 epi=lambda acc, h: (acc * (2.0 * jnp.maximum(h, 0.0)),), tiles=(s["h1"],), name=n + "d_h1")
    gw["w_up"] = matmul(s["x1"], dh1, mode="tn", tm=1024, tn=1024, tk=512, outs=two, name=n + "g_up")
    dx1 = matmul(dh1, W["w_up"], mode="nt", tm=1024, tn=1024, tk=1024,
                 epi=lambda acc, d: (acc + ALPHA * d,), tiles=(du2,), name=n + "d_x1")
    du1, gs["ln1_g"], gs["ln1_b"] = ln_bwd(dx1, s["xh1"], s["rs1"], sm["ln1_g"], name=n + "ln1_bwd")
    gw["w_out"] = matmul(s["merged"], du1, mode="tn", tm=1024, tn=1024, tk=512, outs=two, name=n + "g_out")
    dmerged = matmul(du1, W["w_out"], mode="nt", tm=1024, tn=1024, tk=1024, name=n + "d_merged")
    dr1, dr2, dr3, dg1, dg2, dg3 = merge_bwd(dmerged, *s["r"], s["proj"], name=n + "merge_bwd")
    d_branch = {}
    for nm, a, dr in (("p_ret", s["retg"], dr1), ("p_sb", s["sb"], dr2), ("p_sgu", s["sg"], dr3)):
        gw[nm] = matmul(a, dr, mode="tn", tm=512, tn=1024, tk=512, outs=two, name=n + "g_" + nm)
        d_branch[nm] = matmul(dr, W[nm], mode="nt", tm=1024, tn=512, tk=1024, name=n + "d_" + nm)
    job = hooks.pair(l, gw)
    dret, gs["ret_gn_g"], gs["ret_gn_b"], job_out = ret_bwd(s["proj"], *rope, rconsts, sm["ret_gn_g"], sm["ret_gn_b"],
                                                             s["raw"], s["states"], d_branch["p_ret"], name=n + "ret_bwd", job=job)
    if job is not None:
        hooks.done(job, job_out)
    job = hooks.scatter(l) if job is not None else None
    dsq, dsk, dsv, job_out = sb_bwd(s["proj"], s["sb"], d_branch["p_sb"], name=n + "sb_bwd", job=job)
    if job is not None:
        hooks.done(job, job_out)
    dsgu, gs["sgu_w"], dbias, gs["sgu_ln_g"], gs["sgu_ln_b"] = sgu_bwd(
        s["proj"], sm["sgu_ln_g"], sm["sgu_ln_b"], sm["sgu_w"], sm["sgu_bias"], d_branch["p_sgu"], name=n + "sgu_bwd")
    gs["sgu_b"] = dbias[:, :, 0]
    dproj = jnp.concatenate([dret, dsq, dsk, dsv, dsgu, dg1, dg2, dg3], axis=1)
    gw["w_in"] = matmul(s["x0"], dproj, mode="tn", tm=1024, tn=1536, tk=512, outs=two, name=n + "g_in")
    job = hooks.tail(l, gw["w_in"])
    dx0 = matmul(dproj, W["w_in"], mode="nt", tm=1024, tn=1024, tk=1536,
                 epi=lambda acc, d: (acc + ALPHA * d,), tiles=(du1,), name=n + "d_x0", job=job)
    if job is not None:
        dx0, job_out = dx0
        hooks.done(job, job_out)
    return dx0, gw, gs


def local_step(x, target, small, plan):
    T = x.shape[0]
    rope = _rope_tables(T)
    rconsts = _ret_consts()
    sms = []
    for l in range(DEPTH):
        sm = {k: small[k][l][None, :] for k in SMALL if k not in ("sgu_w", "sgu_b")}
        sm["sgu_w"] = small["sgu_w"][l]
        sm["sgu_bias"] = jnp.broadcast_to(small["sgu_b"][l][:, :, None], (4, CHUNK, CHUNK))
        sms.append(sm)
    h, saved = x, []
    for l in range(DEPTH):
        h, s = layer_forward(l, h, plan.weights(l), sms[l], rope, rconsts, plan)
        saved.append(s)
    dy, sq = loss_head(h, target)
    gs = {k: [None] * DEPTH for k in SMALL}
    for l in reversed(range(DEPTH)):
        dy, gwl, gsl = layer_backward(l, dy, saved[l], plan.weights(l), sms[l], rope, rconsts, plan)
        plan.grads(l, gwl)
        for k in SMALL:
            gs[k][l] = gsl[k].reshape(small[k].shape[1:])
    return sq[0, 0], dy, {k: jnp.stack(v) for k, v in gs.items()}


EARLY_GRADS = ("p_ret", "p_sb", "p_sgu", "w_out", "w_up", "w_down")


class _StepPlan:
    def __init__(self, pos, shards16):
        self.pos = pos
        self.shards16 = shards16
        self.full = [dict() for _ in range(DEPTH)]
        self.gw = [None] * DEPTH
        self.bufs = {}
        self.sums = {}
        first = self._gather([(0, "w_in")])
        self.done(first, run_job(first, name="gather_first"))

    def weights(self, l):
        return self.full[l]

    def grads(self, l, gw):
        self.gw[l] = gw

    def _gather(self, items):
        job = gather_job([self.shards16[l][k] for l, k in items], [BIG_AXIS[k] for _, k in items])
        job.note = ("gather", items)
        return job

    def _pair(self, items):
        job = pair_job([g[1] for _, _, g in items], [BIG_AXIS[k] for _, k, _ in items])
        job.note = ("pair", items)
        return job

    def fwd_jobs(self, l):
        jobs = {"sb": self._gather([(l, k) for k in BIG[1:]])}
        if l + 1 < DEPTH:
            jobs["down"] = self._gather([(l + 1, "w_in")])
        return jobs

    def pair(self, l, ready):
        items = [(l, k, ready[k]) for k in EARLY_GRADS]
        if l + 1 < DEPTH:
            items.append((l + 1, "w_in", self.gw[l + 1]["w_in"]))
        return self._pair(items)

    def scatter(self, l):
        items, sums16 = self.summed
        job = scatter_job([l_ for l_, _, _ in items], sums16, [BIG_AXIS[k] for _, k, _ in items],
                          [self.bufs.get(k) for _, k, _ in items])
        job.note = ("scatter", items)
        return job

    def tail(self, l, g):
        if l != 0:
            return None
        last = self._pair([(0, "w_in", g)])
        self.done(last, run_job(last, name="pair_last"))
        return self.scatter(0)

    def done(self, job, outs):
        kind, items = job.note
        if kind == "pair":
            sums16 = []
            for a, (l, k, g) in enumerate(items):
                self.sums[(l, k)], s16 = pair_sum(self.pos, outs[a], g[0], BIG_AXIS[k], name=f"pair_sum_{k}_{l}")
                sums16.append(s16)
            self.summed = (items, sums16)
        for a, item in enumerate(items):
            if kind == "gather":
                self.full[item[0]][item[1]] = outs[a]
            elif kind == "scatter":
                self.bufs[item[1]] = outs[a]

    def finish(self):
        return self.bufs, self.sums


def _flat2(a):
    return a.reshape(-1, a.shape[-1])


def kernel(x, w_in, ret_gn_g, ret_gn_b, sgu_ln_g, sgu_ln_b, sgu_w, sgu_b, p_ret, p_sb, p_sgu, w_out, ln1_g, ln1_b, w_up, w_down, ln2_g, ln2_b, loss_target, m_w_in, m_ret_gn_g, m_ret_gn_b, m_sgu_ln_g, m_sgu_ln_b, m_sgu_w, m_sgu_b, m_p_ret, m_p_sb, m_p_sgu, m_w_out, m_ln1_g, m_ln1_b, m_w_up, m_w_down, m_ln2_g, m_ln2_b, v_w_in, v_ret_gn_g, v_ret_gn_b, v_sgu_ln_g, v_sgu_ln_b, v_sgu_w, v_sgu_b, v_p_ret, v_p_sb, v_p_sgu, v_w_out, v_ln1_g, v_ln1_b, v_w_up, v_w_down, v_ln2_g, v_ln2_b):
    given = dict(locals())
    order = BIG[:1] + SMALL[:6] + BIG[1:5] + SMALL[6:8] + BIG[5:7] + SMALL[8:10]
    L = DEPTH

    px, py, pc = _place()
    pos = jnp.stack([px, py, pc, 2 * px + py]).astype(jnp.int32)

    shards16 = [{k: cast_into_whole(pos, given[k], l, BIG_AXIS[k], name=f"cast_{k}_{l}") for k in BIG} for l in range(L)]
    plan = _StepPlan(pos, shards16)
    sq, dx, gs = local_step(x[0], loss_target[0], {k: given[k] for k in SMALL}, plan)
    loss = 0.5 * lax.psum(sq, ("x", "y", "c"))

    bufs, sums = plan.finish()
    shards = []
    for k in BIG:
        whole = None
        for l in range(L):
            whole = chip_sum(pos, sums[(l, k)], bufs[k], l, BIG_AXIS[k], whole, name=f"chip_sum_{k}_{l}")
        shards.append(whole)
    joined = run_job(join_job(shards), name="join_halves")
    out = {}
    for a, k in enumerate(BIG):
        shp = given[k].shape
        res = _rows_call(lambda g_, w_, m_, v_: (g_,) + _adamw(w_, g_, m_, v_),
                         [joined[a].reshape(-1, shp[-1]), _flat2(given[k]), _flat2(given["m_" + k]), _flat2(given["v_" + k])],
                         [F32] * 4, name="adamw_" + k)
        out[k] = [r.reshape(shp) for r in res]

    def pack(d, pre=""):
        return jnp.concatenate([d[pre + k].reshape(-1) for k in SMALL]).reshape(-1, 128)

    g_small = allreduce_small(pack(gs))
    res = _rows_call(lambda g_, w_, m_, v_: (g_,) + _adamw(w_, g_, m_, v_),
                     [g_small, pack(given), pack(given, "m_"), pack(given, "v_")], [F32] * 4, name="adamw_small", tr=8 * 47)
    off = 0
    for k in SMALL:
        sz = given[k].size
        out[k] = [r.reshape(-1)[off:off + sz].reshape(given[k].shape) for r in res]
        off += sz

    grads = [out[k][0] for k in order]
    deltas = [out[k][1] for k in order]
    new_m = [out[k][2] for k in order]
    new_v = [out[k][3] for k in order]
    return (loss, dx[None], *grads, *deltas, *new_m, *new_v)
```

```python
import functools
import math

import jax
import jax.numpy as jnp
from jax import lax
from jax.experimental import pallas as pl
from jax.experimental.pallas import tpu as pltpu

F32 = jnp.float32
BF16 = jnp.bfloat16

D_MODEL = 1024
SEQ = 4096
DEPTH = 2
CHUNK = 128
RET_HEADS = 4
BRANCH_W = 512
N_IN = 7680
D_FF = 4096
LN_EPS = 1e-5
ROPE_BASE = 10000.0
ALPHA = (2 * DEPTH) ** 0.25
RET_SCALE = 128 ** -0.5
SB_SCALE = 64 ** -0.5
C_RET, C_SB, C_SGU, C_GATE = 0, 2048, 3584, 4608

ADAM_LR, ADAM_B1, ADAM_B2, ADAM_EPS, ADAM_WD, ADAM_STEP = 0.001, 0.9, 0.999, 1e-08, 0.01, 10

N_CHIPS = 4
VMEM_LIMIT = 56 * 1024 * 1024
MESH = pl.DeviceIdType.MESH

NN = ((1,), (0,))
NT = ((1,), (1,))
TN = ((0,), (0,))


def _dot(a, b, dims):
    return lax.dot_general(a, b, (dims, ((), ())), preferred_element_type=F32)


def _params(sem):
    return pltpu.CompilerParams(dimension_semantics=sem, vmem_limit_bytes=VMEM_LIMIT)


def _relu2(h):
    r = jnp.maximum(h, 0.0)
    return r * r


def matmul(a, b, *, mode, tm, tn, tk, outs=((F32, None),), pro=None, epi=None, tiles=(), rows=(), name, job=None):
    if mode == "nn":
        (M, K), N = a.shape, b.shape[1]
    elif mode == "nt":
        (M, K), N = a.shape, b.shape[0]
    else:
        (K, M), N = a.shape, b.shape[1]
    tm, tn, tk = min(tm, M), min(tn, N), min(tk, K)
    assert M % tm == 0 and N % tn == 0 and K % tk == 0, (name, M, N, K, tm, tn, tk)
    if mode == "nn":
        a_spec = pl.BlockSpec((tm, tk), lambda i, j, k: (i, k))
        b_spec = pl.BlockSpec((tk, tn), lambda i, j, k: (k, j))
        dims = NN
    elif mode == "nt":
        a_spec = pl.BlockSpec((tm, tk), lambda i, j, k: (i, k))
        b_spec = pl.BlockSpec((tn, tk), lambda i, j, k: (j, k))
        dims = NT
    else:
        a_spec = pl.BlockSpec((tk, tm), lambda i, j, k: (k, i))
        b_spec = pl.BlockSpec((tk, tn), lambda i, j, k: (k, j))
        dims = TN
    nk = K // tk
    nt_, nr, no = len(tiles), len(rows), len(outs)

    def body(a_ref, b_ref, *rest):
        tile_refs = rest[:nt_]
        row_refs = rest[nt_:nt_ + nr]
        out_refs = rest[nt_ + nr:nt_ + nr + no]
        av = a_ref[...]
        if pro is not None:
            av = pro(av)
        p = _dot(av.astype(BF16), b_ref[...].astype(BF16), dims)

        def finish(acc):
            vals = (acc,) * no if epi is None else epi(acc, *[r[...] for r in tile_refs], *[r[...] for r in row_refs])
            for o_ref, v in zip(out_refs, vals):
                o_ref[...] = v.astype(o_ref.dtype)

        if nk == 1:
            finish(p)
        else:
            acc_ref = rest[-1]
            k = pl.program_id(2)

            @pl.when(k == 0)
            def _():
                acc_ref[...] = p

            @pl.when(k > 0)
            def _():
                acc_ref[...] += p

            @pl.when(k == nk - 1)
            def _():
                finish(acc_ref[...])

    out_shape, out_specs = [], []
    for dt, width in outs:
        if width is None:
            out_shape.append(jax.ShapeDtypeStruct((M, N), dt))
            out_specs.append(pl.BlockSpec((tm, tn), lambda i, j, k: (i, j)))
        else:
            assert N == tn
            out_shape.append(jax.ShapeDtypeStruct((M, width), dt))
            out_specs.append(pl.BlockSpec((tm, width), lambda i, j, k: (i, 0)))
    in_specs = [a_spec, b_spec]
    in_specs += [pl.BlockSpec((tm, tn), lambda i, j, k: (i, j)) for _ in tiles]
    in_specs += [pl.BlockSpec((1, tn), lambda i, j, k: (0, j)) for _ in rows]
    grid = (M // tm, N // tn, nk)
    scratch = [pltpu.VMEM((tm, tn), F32)] if nk > 1 else []
    j = _job_args(job, len(in_specs), no)
    res = pl.pallas_call(
        _hosting(body, job, len(in_specs), no, len(scratch), grid), name=name, grid=grid,
        in_specs=in_specs + j["in_specs"], out_specs=out_specs + j["out_specs"], out_shape=out_shape + j["out_shape"],
        scratch_shapes=scratch + j["scratch"], input_output_aliases=j["aliases"],
        compiler_params=_params(("parallel", "parallel", "arbitrary") if job is None else ("arbitrary",) * 3),
    )(a, b, *tiles, *rows, *j["ins"])
    mine = res[0] if no == 1 else list(res[:no])
    return mine if job is None else (mine, list(res[no:]))


def _ln_epi(acc, res, g, b):
    u = ALPHA * res + acc
    mu = jnp.mean(u, axis=-1, keepdims=True)
    xc = u - mu
    var = jnp.mean(xc * xc, axis=-1, keepdims=True)
    rstd = lax.rsqrt(var + LN_EPS)
    xhat = xc * rstd
    return xhat * g + b, xhat, jnp.broadcast_to(rstd, (u.shape[0], 128))


def matmul_ln(a, w, res, g, b, *, pro=None, tk, name, job=None):
    n = w.shape[1]
    return matmul(a, w, mode="nn", tm=512, tn=n, tk=tk, pro=pro, epi=_ln_epi, tiles=(res,), rows=(g, b),
                  outs=((F32, None), (F32, None), (F32, 128)), name=name, job=job)


def ln_bwd(dy, xhat, rstd, g, *, name):
    T, D = dy.shape
    tm = min(512, T)

    def body(dy_ref, xh_ref, rs_ref, g_ref, du_ref, dg_ref, db_ref):
        dyv, xh = dy_ref[...], xh_ref[...]
        r = rs_ref[:, 0:1]
        dxh = dyv * g_ref[...]
        m1 = jnp.mean(dxh, axis=-1, keepdims=True)
        m2 = jnp.mean(dxh * xh, axis=-1, keepdims=True)
        du_ref[...] = r * (dxh - m1 - xh * m2)

        @pl.when(pl.program_id(0) == 0)
        def _():
            dg_ref[...] = jnp.zeros_like(dg_ref)
            db_ref[...] = jnp.zeros_like(db_ref)

        dg_ref[...] += jnp.sum(dyv * xh, axis=0, keepdims=True)
        db_ref[...] += jnp.sum(dyv, axis=0, keepdims=True)

    row = pl.BlockSpec((tm, D), lambda i: (i, 0))
    vec = pl.BlockSpec((1, D), lambda i: (0, 0))
    return pl.pallas_call(
        body, name=name, grid=(T // tm,),
        in_specs=[row, row, pl.BlockSpec((tm, 128), lambda i: (i, 0)), vec],
        out_specs=[row, vec, vec],
        out_shape=[jax.ShapeDtypeStruct((T, D), F32), jax.ShapeDtypeStruct((1, D), F32), jax.ShapeDtypeStruct((1, D), F32)],
        compiler_params=_params(("arbitrary",)),
    )(dy, xhat, rstd, g)


def loss_head(y, target):
    T, D = y.shape
    tm = min(512, T)

    def body(y_ref, t_ref, dy_ref, s_ref):
        e = y_ref[...] - t_ref[...]
        dy_ref[...] = e * (1.0 / D)

        @pl.when(pl.program_id(0) == 0)
        def _():
            s_ref[...] = jnp.zeros_like(s_ref)

        s_ref[...] += jnp.sum(jnp.mean(e * e, axis=-1, keepdims=True))

    row = pl.BlockSpec((tm, D), lambda i: (i, 0))
    return pl.pallas_call(
        body, name="loss_head", grid=(T // tm,),
        in_specs=[row, row], out_specs=[row, pl.BlockSpec((8, 128), lambda i: (0, 0))],
        out_shape=[jax.ShapeDtypeStruct((T, D), F32), jax.ShapeDtypeStruct((8, 128), F32)],
        compiler_params=_params(("arbitrary",)),
    )(y, target)


def _rope_tables(T):
    half = 64
    inv_freq = ROPE_BASE ** (-jnp.arange(half, dtype=F32) / half)
    ang = jnp.arange(T, dtype=jnp.int32).astype(F32)[:, None] * inv_freq[None, :]
    cos, sin = jnp.cos(ang), jnp.sin(ang)
    return jnp.concatenate([cos, cos], axis=1), jnp.concatenate([-sin, sin], axis=1)


def _ret_consts():
    H = RET_HEADS
    log_g = jnp.log(1.0 - 2.0 ** (-5.0 - jnp.arange(H, dtype=F32)))
    idx = jnp.arange(CHUNK, dtype=F32)
    diff = idx[:, None] - idx[None, :]
    dmat = jnp.where(diff[None] >= 0, jnp.exp(log_g[:, None, None] * diff[None]), 0.0)
    kd = jnp.exp(log_g[:, None] * (CHUNK - 1 - idx)[None, :])
    qd = jnp.exp(log_g[:, None] * (idx + 1.0)[None, :])
    cd = jnp.exp(log_g * CHUNK)
    full = (H, CHUNK, CHUNK)
    return (dmat.astype(F32), jnp.broadcast_to(kd[:, :, None], full), jnp.broadcast_to(qd[:, :, None], full),
            jnp.broadcast_to(cd[:, None, None], full))


def _swap_halves(v):
    return pltpu.roll(v, 64, 1)


def _group_norm(o):
    mu = jnp.mean(o, axis=-1, keepdims=True)
    xc = o - mu
    var = jnp.mean(xc * xc, axis=-1, keepdims=True)
    rstd = lax.rsqrt(var + LN_EPS)
    return xc * rstd, rstd


def ret_fwd(proj, cosf, sinf, consts, gn_g, gn_b, *, name):
    T = proj.shape[0]
    tb = min(512, T)
    nch = tb // CHUNK
    H = RET_HEADS

    def body(p_ref, cos_ref, sin_ref, dm_ref, kd_ref, qd_ref, cd_ref, g_ref, b_ref, out_ref, raw_ref, st_ref, s_ref):
        @pl.when(pl.program_id(0) == 0)
        def _():
            s_ref[...] = jnp.zeros_like(s_ref)

        for c in range(nch):
            r = slice(c * CHUNK, (c + 1) * CHUNK)
            cs, sn = cos_ref[r, :], sin_ref[r, :]
            for h in range(H):
                hc = slice(h * 128, (h + 1) * 128)
                q = p_ref[r, h * 128:(h + 1) * 128]
                k = p_ref[r, 512 + h * 128:512 + (h + 1) * 128]
                v = p_ref[r, 1024 + h * 128:1024 + (h + 1) * 128]
                gt = p_ref[r, 1536 + h * 128:1536 + (h + 1) * 128]
                qr = q * cs + _swap_halves(q) * sn
                kr = (k * cs + _swap_halves(k) * sn) * RET_SCALE
                sprev = s_ref[h]
                st_ref[c, h] = sprev
                qb, kb, vb = qr.astype(BF16), kr.astype(BF16), v.astype(BF16)
                s = _dot(qb, kb, NT) * dm_ref[h]
                o = _dot(s.astype(BF16), vb, NN) + _dot((qr * qd_ref[h]).astype(BF16), sprev.astype(BF16), NN)
                s_ref[h] = sprev * cd_ref[h] + _dot((kr * kd_ref[h]).astype(BF16), vb, TN)
                raw_ref[r, hc] = o
                y, _ = _group_norm(o)
                out_ref[r, hc] = (gt * jax.nn.sigmoid(gt)) * (y * g_ref[:, hc] + b_ref[:, hc])

    cmat = pl.BlockSpec((H, CHUNK, CHUNK), lambda i: (0, 0, 0))
    vec = pl.BlockSpec((1, BRANCH_W), lambda i: (0, 0))
    rope = pl.BlockSpec((tb, 128), lambda i: (i, 0))
    blk = pl.BlockSpec((tb, BRANCH_W), lambda i: (i, 0))
    return pl.pallas_call(
        body, name=name, grid=(T // tb,),
        in_specs=[pl.BlockSpec((tb, 2048), lambda i: (i, 0)), rope, rope, cmat, cmat, cmat, cmat, vec, vec],
        out_specs=[blk, blk, pl.BlockSpec((nch, H, CHUNK, CHUNK), lambda i: (i, 0, 0, 0))],
        out_shape=[jax.ShapeDtypeStruct((T, BRANCH_W), F32), jax.ShapeDtypeStruct((T, BRANCH_W), F32),
                   jax.ShapeDtypeStruct((T // CHUNK, H, CHUNK, CHUNK), F32)],
        scratch_shapes=[pltpu.VMEM((H, CHUNK, CHUNK), F32)],
        compiler_params=_params(("arbitrary",)),
    )(proj, cosf, sinf, *consts, gn_g, gn_b)


def ret_bwd(proj, cosf, sinf, consts, gn_g, gn_b, raw, states, dout, *, name, job=None):
    T = proj.shape[0]
    tb = min(512, T)
    nch = tb // CHUNK
    nb = T // tb
    H = RET_HEADS

    def body(p_ref, cos_ref, sin_ref, dm_ref, kd_ref, qd_ref, cd_ref, g_ref, b_ref, raw_ref, st_ref, do_ref,
             dp_ref, dg_ref, db_ref, ds_ref):
        @pl.when(pl.program_id(0) == 0)
        def _():
            ds_ref[...] = jnp.zeros_like(ds_ref)
            dg_ref[...] = jnp.zeros_like(dg_ref)
            db_ref[...] = jnp.zeros_like(db_ref)

        for c in reversed(range(nch)):
            r = slice(c * CHUNK, (c + 1) * CHUNK)
            cs, sn = cos_ref[r, :], sin_ref[r, :]
            for h in range(H):
                hc = slice(h * 128, (h + 1) * 128)
                q = p_ref[r, h * 128:(h + 1) * 128]
                k = p_ref[r, 512 + h * 128:512 + (h + 1) * 128]
                v = p_ref[r, 1024 + h * 128:1024 + (h + 1) * 128]
                gt = p_ref[r, 1536 + h * 128:1536 + (h + 1) * 128]
                qr = q * cs + _swap_halves(q) * sn
                kr = (k * cs + _swap_halves(k) * sn) * RET_SCALE
                sprev = st_ref[c, h]
                gv = g_ref[:, hc]
                y, rstd = _group_norm(raw_ref[r, hc])
                d_out = do_ref[r, hc]
                sg = jax.nn.sigmoid(gt)
                d_gate = d_out * (y * gv + b_ref[:, hc]) * (sg * (1.0 + gt * (1.0 - sg)))
                d_aff = d_out * (gt * sg)
                dg_ref[:, hc] += jnp.sum(d_aff * y, axis=0, keepdims=True)
                db_ref[:, hc] += jnp.sum(d_aff, axis=0, keepdims=True)
                dxh = d_aff * gv
                m1 = jnp.mean(dxh, axis=-1, keepdims=True)
                m2 = jnp.mean(dxh * y, axis=-1, keepdims=True)
                d_o = (rstd * (dxh - m1 - y * m2)).astype(BF16)
                qb, kb, vb = qr.astype(BF16), kr.astype(BF16), v.astype(BF16)
                dm, kd, qd = dm_ref[h], kd_ref[h], qd_ref[h]
                p = (_dot(qb, kb, NT) * dm).astype(BF16)
                dp = (_dot(d_o, vb, NT) * dm).astype(BF16)
                dsn = ds_ref[h]
                dsb = dsn.astype(BF16)
                dq_r = _dot(dp, kb, NN) + _dot(d_o, sprev.astype(BF16), NT) * qd
                dk_r = (_dot(dp, qb, TN) + _dot(vb, dsb, NT) * kd) * RET_SCALE
                d_v = _dot(p, d_o, TN) + _dot((kr * kd).astype(BF16), dsb, NN)
                ds_ref[h] = dsn * cd_ref[h] + _dot((qr * qd).astype(BF16), d_o, TN)
                dp_ref[r, h * 128:(h + 1) * 128] = (dq_r * cs - _swap_halves(dq_r) * sn).astype(BF16)
                dp_ref[r, 512 + h * 128:512 + (h + 1) * 128] = (dk_r * cs - _swap_halves(dk_r) * sn).astype(BF16)
                dp_ref[r, 1024 + h * 128:1024 + (h + 1) * 128] = d_v.astype(BF16)
                dp_ref[r, 1536 + h * 128:1536 + (h + 1) * 128] = d_gate.astype(BF16)

    cmat = pl.BlockSpec((H, CHUNK, CHUNK), lambda i: (0, 0, 0))
    vec = pl.BlockSpec((1, BRANCH_W), lambda i: (0, 0))
    rope = pl.BlockSpec((tb, 128), lambda i: (nb - 1 - i, 0))
    blk = pl.BlockSpec((tb, BRANCH_W), lambda i: (nb - 1 - i, 0))
    wide = pl.BlockSpec((tb, 2048), lambda i: (nb - 1 - i, 0))
    j = _job_args(job, 12, 3)
    res = pl.pallas_call(
        _hosting(body, job, 12, 3, 1, nb), name=name, grid=(nb,),
        in_specs=[wide, rope, rope, cmat, cmat, cmat, cmat, vec, vec, blk,
                  pl.BlockSpec((nch, H, CHUNK, CHUNK), lambda i: (nb - 1 - i, 0, 0, 0)), blk] + j["in_specs"],
        out_specs=[wide, vec, vec] + j["out_specs"],
        out_shape=[jax.ShapeDtypeStruct((T, 2048), BF16), jax.ShapeDtypeStruct((1, BRANCH_W), F32),
                   jax.ShapeDtypeStruct((1, BRANCH_W), F32)] + j["out_shape"],
        scratch_shapes=[pltpu.VMEM((H, CHUNK, CHUNK), F32)] + j["scratch"], input_output_aliases=j["aliases"],
        compiler_params=_params(("arbitrary",)),
    )(proj, cosf, sinf, *consts, gn_g, gn_b, raw, states, dout, *j["ins"])
    return res[0], res[1], res[2], list(res[3:])


def _sb_masks():
    row = lax.broadcasted_iota(jnp.int32, (CHUNK, CHUNK), 0)
    lane = lax.broadcasted_iota(jnp.int32, (CHUNK, CHUNK), 1)
    return row, lane


SB_QT = 256
SB_DEAD = -105.0


def _pair(v):
    hi = v.astype(BF16)
    return jnp.concatenate([hi, (v - hi.astype(F32)).astype(BF16)], axis=1)


def _sb_consts():
    r = lax.broadcasted_iota(jnp.int32, (256, 256), 0) & 127
    c = lax.broadcasted_iota(jnp.int32, (256, 256), 1)
    ones = c >= 128
    lane = lax.broadcasted_iota(jnp.int32, (CHUNK, CHUNK), 1)
    return (ones | (r > c)).astype(BF16), (ones | (r >= c)).astype(BF16), (lane < 64, lane >= 64)


def _per_head(x, hms):
    return jnp.concatenate([jnp.where(hm, x, 0.0) for hm in hms], axis=0).astype(BF16)


def _sb_logits(qb, kb2, mask2):
    z = _dot(qb, kb2, NT)
    l1p = jnp.log(1.0 + jnp.exp(-jnp.abs(z)))
    lsp = jnp.minimum(z, 0.0) - l1p
    lsn = lsp - z
    if mask2 is not None:
        lsn = jnp.where(mask2, lsn, 0.0)
    return lsp, lsn


def _sb_tile_mask(qt):
    trow = lax.broadcasted_iota(jnp.int32, (qt, 256), 0)
    tlane = lax.broadcasted_iota(jnp.int32, (qt, 256), 1) & 127
    return lambda m: (tlane + m * CHUNK) < trow


def sb_fwd(proj, *, name, job=None):
    T = proj.shape[0]
    qt = min(SB_QT, T)
    nsub = qt // CHUNK
    cb = C_SB // 128

    def body(q_ref, k_ref, v_ref, o_ref):
        u_gt, _, hms = _sb_consts()
        tile_mask = _sb_tile_mask(qt)

        def qtile(i, _):
            rq = pl.ds(pl.multiple_of(i * qt, qt), qt)
            qb = (q_ref[rq, :] * SB_SCALE).astype(BF16)

            def step(j, state, mask2):
                carry, acc = list(state[:2]), state[2]
                rk = pl.ds(pl.multiple_of(j * CHUNK, CHUNK), CHUNK)
                lsp, lsn = _sb_logits(qb, _per_head(k_ref[rk, :], hms), mask2)
                a_b = []
                for h in range(2):
                    hc = slice(h * 128, (h + 1) * 128)
                    r = _dot(_pair(lsn[:, hc]), u_gt, NN)
                    a = jnp.exp(lsp[:, hc] + r[:, :128] + carry[h])
                    if mask2 is not None:
                        a = jnp.where(mask2[:, hc], a, 0.0)
                    carry[h] = carry[h] + r[:, 128:]
                    a_b.append(a.astype(BF16))
                acc = acc + _dot(jnp.concatenate(a_b, axis=1), _per_head(v_ref[rk, :], hms), NN)
                return carry[0], carry[1], acc

            zero = jnp.zeros((qt, 128), F32)
            state = (zero, zero, zero)
            for m in reversed(range(nsub)):
                state = step(i * nsub + m, state, tile_mask(m))

            def live(c):
                return jnp.logical_and(c[0] < i, jnp.maximum(jnp.max(c[1][0]), jnp.max(c[1][1])) > SB_DEAD)

            def blocks(c):
                jj, st = c
                for u in range(nsub):
                    st = step((i - jj) * nsub - 1 - u, st, None)
                return jj + 1, st

            _, state = lax.while_loop(live, blocks, (jnp.int32(0), state))
            o_ref[rq, :] = state[2]
            return 0

        lax.fori_loop(0, T // qt, qtile, 0)

    def col(off):
        return pl.BlockSpec((T, 128), lambda hp: (0, off + hp))

    steps = BRANCH_W // 128
    j = _job_args(job, 3, 1)
    res = pl.pallas_call(
        _hosting(body, job, 3, 1, 0, steps), name=name, grid=(steps,),
        in_specs=[col(cb), col(cb + 4), col(cb + 8)] + j["in_specs"], out_specs=[col(0)] + j["out_specs"],
        out_shape=[jax.ShapeDtypeStruct((T, BRANCH_W), F32)] + j["out_shape"],
        scratch_shapes=j["scratch"], input_output_aliases=j["aliases"],
        compiler_params=_params(("parallel",) if job is None else ("arbitrary",)),
    )(proj, proj, proj, *j["ins"])
    return res[0], list(res[1:])


def sb_bwd(proj, out, dout, *, name, job=None):
    T = proj.shape[0]
    qt = min(SB_QT, T)
    nsub = qt // CHUNK
    cb = C_SB // 128

    def body(q_ref, k_ref, v_ref, o_ref, do_ref, dq_ref, dk_ref, dv_ref, dkt_ref, dvt_ref):
        u_gt, u_ge, hms = _sb_consts()
        tile_mask = _sb_tile_mask(qt)
        tall_lane = lax.broadcasted_iota(jnp.int32, (qt, 128), 1)
        top = lax.broadcasted_iota(jnp.int32, (CHUNK, CHUNK), 0) < 64
        dkt_ref[...] = jnp.zeros_like(dkt_ref)
        dvt_ref[...] = jnp.zeros_like(dvt_ref)

        def qtile(i, _):
            rq = pl.ds(pl.multiple_of(i * qt, qt), qt)
            qs = q_ref[rq, :] * SB_SCALE
            qb, q_t = qs.astype(BF16), qs.T.astype(BF16)
            dov = do_ref[rq, :]
            dob, do_t = dov.astype(BF16), dov.T.astype(BF16)
            prod = dob.astype(F32) * o_ref[rq, :]
            total = [jnp.broadcast_to(jnp.sum(jnp.where(hm, prod, 0.0), axis=1, keepdims=True), (qt, 128))
                     for hm in (tall_lane < 64, tall_lane >= 64)]

            def step(j, state, mask2):
                c_l, c_w, dq = list(state[:2]), list(state[2:4]), state[4]
                rk = pl.ds(pl.multiple_of(j * CHUNK, CHUNK), CHUNK)
                kb2, vb2 = _per_head(k_ref[rk, :], hms), _per_head(v_ref[rk, :], hms)
                lsp, lsn = _sb_logits(qb, kb2, mask2)
                da = _dot(dob, vb2, NT)
                sp = jnp.exp(lsp)
                a_b, dz_b = [], []
                for h in range(2):
                    hc = slice(h * 128, (h + 1) * 128)
                    r = _dot(_pair(lsn[:, hc]), u_gt, NN)
                    a = jnp.exp(lsp[:, hc] + r[:, :128] + c_l[h])
                    if mask2 is not None:
                        a = jnp.where(mask2[:, hc], a, 0.0)
                    c_l[h] = c_l[h] + r[:, 128:]
                    a = a.astype(BF16)
                    w = a.astype(F32) * da[:, hc]
                    r = _dot(_pair(w), u_ge, NN)
                    later_w = r[:, :128] + c_w[h]
                    c_w[h] = c_w[h] + r[:, 128:]
                    dz = w * (1.0 - sp[:, hc]) - sp[:, hc] * (total[h] - later_w)
                    if mask2 is not None:
                        dz = jnp.where(mask2[:, hc], dz, 0.0)
                    a_b.append(a)
                    dz_b.append(dz.astype(BF16))
                a_b, dz_b = jnp.concatenate(a_b, axis=1), jnp.concatenate(dz_b, axis=1)
                dkt = _dot(q_t, dz_b, NN)
                dvt = _dot(do_t, a_b, NN)
                dkt_ref[j] += jnp.where(top, dkt[:, :128], dkt[:, 128:])
                dvt_ref[j] += jnp.where(top, dvt[:, :128], dvt[:, 128:])
                return c_l[0], c_l[1], c_w[0], c_w[1], dq + _dot(dz_b, kb2, NN)

            zero = jnp.zeros((qt, 128), F32)
            state = (zero,) * 5
            for m in reversed(range(nsub)):
                state = step(i * nsub + m, state, tile_mask(m))

            def live(c):
                return jnp.logical_and(c[0] < i, jnp.maximum(jnp.max(c[1][0]), jnp.max(c[1][1])) > SB_DEAD)

            def blocks(c):
                jj, st = c
                for u in range(nsub):
                    st = step((i - jj) * nsub - 1 - u, st, None)
                return jj + 1, st

            _, state = lax.while_loop(live, blocks, (jnp.int32(0), state))
            dq_ref[rq, :] = (state[4] * SB_SCALE).astype(BF16)
            return 0

        lax.fori_loop(0, T // qt, qtile, 0)

        def untranspose(jb, _):
            rk = pl.ds(pl.multiple_of(jb * CHUNK, CHUNK), CHUNK)
            dk_ref[rk, :] = dkt_ref[jb].T.astype(BF16)
            dv_ref[rk, :] = dvt_ref[jb].T.astype(BF16)
            return 0

        lax.fori_loop(0, T // CHUNK, untranspose, 0)

    def col(off):
        return pl.BlockSpec((T, 128), lambda hp: (0, off + hp))

    o16 = jax.ShapeDtypeStruct((T, BRANCH_W), BF16)
    steps = BRANCH_W // 128
    j = _job_args(job, 5, 3)
    acc = pltpu.VMEM((T // CHUNK, CHUNK, CHUNK), F32)
    res = pl.pallas_call(
        _hosting(body, job, 5, 3, 2, steps), name=name, grid=(steps,),
        in_specs=[col(cb), col(cb + 4), col(cb + 8), col(0), col(0)] + j["in_specs"],
        out_specs=[col(0), col(0), col(0)] + j["out_specs"], out_shape=[o16, o16, o16] + j["out_shape"],
        scratch_shapes=[acc, acc] + j["scratch"], input_output_aliases=j["aliases"],
        compiler_params=_params(("parallel",) if job is None else ("arbitrary",)),
    )(proj, proj, proj, out, dout, *j["ins"])
    return res[0], res[1], res[2], list(res[3:])


_G0 = math.sqrt(2.0 / math.pi)
_G1 = 0.044715


def _gelu(x):
    return 0.5 * x * (1.0 + jnp.tanh(_G0 * (x + _G1 * x * x * x)))


def _gelu_grad(x):
    t = jnp.tanh(_G0 * (x + _G1 * x * x * x))
    return 0.5 * (1.0 + t) + 0.5 * x * (1.0 - t * t) * (_G0 * (1.0 + 3.0 * _G1 * x * x))


def _tril():
    row, lane = _sb_masks()
    return row >= lane


def sgu_fwd(proj, ln_g, ln_b, w, bias, *, name):
    T = proj.shape[0]
    tb = min(512, T)
    G = BRANCH_W // 128

    def body(u_ref, v_ref, g_ref, b_ref, w_ref, bias_ref, o_ref):
        vv = _gelu(v_ref[...])
        xh, _ = _group_norm(vv)
        vn = (xh * g_ref[...] + b_ref[...]).astype(BF16)
        tril = _tril()
        for g in range(G):
            wg = jnp.where(tril, w_ref[g], 0.0).astype(BF16)
            gc = slice(g * 128, (g + 1) * 128)
            for c in range(tb // CHUNK):
                r = slice(c * CHUNK, (c + 1) * CHUNK)
                sv = _dot(wg, vn[r, gc], NN) + bias_ref[g]
                o_ref[r, gc] = _gelu(u_ref[r, gc]) * sv

    cu, cv = C_SGU // BRANCH_W, C_SGU // BRANCH_W + 1
    vec = pl.BlockSpec((1, BRANCH_W), lambda i: (0, 0))
    mat = pl.BlockSpec((G, CHUNK, CHUNK), lambda i: (0, 0, 0))
    return pl.pallas_call(
        body, name=name, grid=(T // tb,),
        in_specs=[pl.BlockSpec((tb, BRANCH_W), lambda i: (i, cu)), pl.BlockSpec((tb, BRANCH_W), lambda i: (i, cv)),
                  vec, vec, mat, mat],
        out_specs=pl.BlockSpec((tb, BRANCH_W), lambda i: (i, 0)),
        out_shape=jax.ShapeDtypeStruct((T, BRANCH_W), F32),
        compiler_params=_params(("parallel",)),
    )(proj, proj, ln_g, ln_b, w, bias)


def sgu_bwd(proj, ln_g, ln_b, w, bias, dout, *, name):
    T = proj.shape[0]
    tb = min(512, T)
    G = BRANCH_W // 128

    def body(u_ref, v_ref, g_ref, b_ref, w_ref, bias_ref, do_ref, dp_ref, dw_ref, dbias_ref, dg_ref, db_ref, dvn_ref):
        @pl.when(pl.program_id(0) == 0)
        def _():
            dw_ref[...] = jnp.zeros_like(dw_ref)
            dbias_ref[...] = jnp.zeros_like(dbias_ref)
            dg_ref[...] = jnp.zeros_like(dg_ref)
            db_ref[...] = jnp.zeros_like(db_ref)

        gv = v_ref[...]
        vv = _gelu(gv)
        xh, rstd = _group_norm(vv)
        vn = (xh * g_ref[...] + b_ref[...]).astype(BF16)
        tril = _tril()
        for g in range(G):
            wg = jnp.where(tril, w_ref[g], 0.0).astype(BF16)
            gc = slice(g * 128, (g + 1) * 128)
            for c in range(tb // CHUNK):
                r = slice(c * CHUNK, (c + 1) * CHUNK)
                vn_c = vn[r, gc]
                sv = _dot(wg, vn_c, NN) + bias_ref[g]
                gu = u_ref[r, gc]
                d_o = do_ref[r, gc]
                dp_ref[r, gc] = (d_o * sv * _gelu_grad(gu)).astype(BF16)
                dsv = d_o * _gelu(gu)
                dsv_b = dsv.astype(BF16)
                dvn_ref[r, gc] = _dot(wg, dsv_b, TN)
                dw_ref[g] += jnp.where(tril, _dot(dsv_b, vn_c, NT), 0.0)
                dbias_ref[g] += jnp.broadcast_to(jnp.sum(dsv, axis=1, keepdims=True), (CHUNK, CHUNK))
        dvn = dvn_ref[...]
        dg_ref[...] += jnp.sum(dvn * xh, axis=0, keepdims=True)
        db_ref[...] += jnp.sum(dvn, axis=0, keepdims=True)
        dxh = dvn * g_ref[...]
        m1 = jnp.mean(dxh, axis=-1, keepdims=True)
        m2 = jnp.mean(dxh * xh, axis=-1, keepdims=True)
        dp_ref[:, BRANCH_W:2 * BRANCH_W] = (rstd * (dxh - m1 - xh * m2) * _gelu_grad(gv)).astype(BF16)

    cu, cv = C_SGU // BRANCH_W, C_SGU // BRANCH_W + 1
    vec = pl.BlockSpec((1, BRANCH_W), lambda i: (0, 0))
    mat = pl.BlockSpec((G, CHUNK, CHUNK), lambda i: (0, 0, 0))
    blk = pl.BlockSpec((tb, BRANCH_W), lambda i: (i, 0))
    msh = jax.ShapeDtypeStruct((G, CHUNK, CHUNK), F32)
    vsh = jax.ShapeDtypeStruct((1, BRANCH_W), F32)
    return pl.pallas_call(
        body, name=name, grid=(T // tb,),
        in_specs=[pl.BlockSpec((tb, BRANCH_W), lambda i: (i, cu)), pl.BlockSpec((tb, BRANCH_W), lambda i: (i, cv)),
                  vec, vec, mat, mat, blk],
        out_specs=[pl.BlockSpec((tb, 2 * BRANCH_W), lambda i: (i, 0)), mat, mat, vec, vec],
        out_shape=[jax.ShapeDtypeStruct((T, 2 * BRANCH_W), BF16), msh, msh, vsh, vsh],
        scratch_shapes=[pltpu.VMEM((tb, BRANCH_W), F32)],
        compiler_params=_params(("arbitrary",)),
    )(proj, proj, ln_g, ln_b, w, bias, dout)


def merge_fwd(a1, a2, a3, p1, p2, p3, proj, *, name):
    T = a1.shape[0]
    tm, tn = min(1024, T), 512
    gb = C_GATE // tn

    def body(a1_ref, a2_ref, a3_ref, p1_ref, p2_ref, p3_ref, g1_ref, g2_ref, g3_ref, m_ref, r1_ref, r2_ref, r3_ref):
        m = None
        for a_ref, p_ref, g_ref, r_ref in ((a1_ref, p1_ref, g1_ref, r1_ref), (a2_ref, p2_ref, g2_ref, r2_ref),
                                           (a3_ref, p3_ref, g3_ref, r3_ref)):
            r = _dot(a_ref[...].astype(BF16), p_ref[...], NN)
            r_ref[...] = r
            t = jax.nn.sigmoid(g_ref[...]) * r
            m = t if m is None else m + t
        m_ref[...] = m

    a_spec = pl.BlockSpec((tm, BRANCH_W), lambda i, j: (i, 0))
    p_spec = pl.BlockSpec((BRANCH_W, tn), lambda i, j: (0, j))
    o_spec = pl.BlockSpec((tm, tn), lambda i, j: (i, j))
    osh = jax.ShapeDtypeStruct((T, D_MODEL), F32)
    gates = [pl.BlockSpec((tm, tn), functools.partial(lambda i, j, o: (i, o + j), o=gb + 2 * n)) for n in range(3)]
    return pl.pallas_call(
        body, name=name, grid=(T // tm, D_MODEL // tn),
        in_specs=[a_spec, a_spec, a_spec, p_spec, p_spec, p_spec, *gates],
        out_specs=[o_spec] * 4, out_shape=[osh] * 4,
        compiler_params=_params(("parallel", "parallel")),
    )(a1, a2, a3, p1, p2, p3, proj, proj, proj)


def merge_bwd(dm, r1, r2, r3, proj, *, name):
    T = dm.shape[0]
    tm, tn = min(512, T), 512
    gb = C_GATE // tn

    def body(dm_ref, r1_ref, r2_ref, r3_ref, g1_ref, g2_ref, g3_ref, dr1_ref, dr2_ref, dr3_ref, dg1_ref, dg2_ref, dg3_ref):
        d = dm_ref[...]
        for r_ref, g_ref, dr_ref, dg_ref in ((r1_ref, g1_ref, dr1_ref, dg1_ref), (r2_ref, g2_ref, dr2_ref, dg2_ref),
                                             (r3_ref, g3_ref, dr3_ref, dg3_ref)):
            s = jax.nn.sigmoid(g_ref[...])
            dr_ref[...] = (d * s).astype(BF16)
            dg_ref[...] = (d * r_ref[...] * (s * (1.0 - s))).astype(BF16)

    o_spec = pl.BlockSpec((tm, tn), lambda i, j: (i, j))
    osh = jax.ShapeDtypeStruct((T, D_MODEL), BF16)
    gates = [pl.BlockSpec((tm, tn), functools.partial(lambda i, j, o: (i, o + j), o=gb + 2 * n)) for n in range(3)]
    return pl.pallas_call(
        body, name=name, grid=(T // tm, D_MODEL // tn),
        in_specs=[o_spec] * 4 + gates, out_specs=[o_spec] * 6, out_shape=[osh] * 6,
        compiler_params=_params(("parallel", "parallel")),
    )(dm, r1, r2, r3, proj, proj, proj)


def _rows_call(fn, ins, out_dtypes, *, name, tr=256):
    first = ins[0][0] if isinstance(ins[0], tuple) else ins[0]
    R, C = first.shape[-2:]
    tr = min(tr, R)
    assert R % tr == 0, (name, R, tr)
    arrs, specs = [], []
    for x in ins:
        if isinstance(x, tuple):
            arrs.append(x[0])
            specs.append(pl.BlockSpec((None, tr, C), functools.partial(lambda i, n: (n, i, 0), n=x[1])))
        else:
            arrs.append(x)
            specs.append(pl.BlockSpec((tr, C), lambda i: (i, 0)))
    ni = len(arrs)

    def body(*refs):
        vals = fn(*[r[...] for r in refs[:ni]])
        for o_ref, v in zip(refs[ni:], vals):
            o_ref[...] = v.astype(o_ref.dtype)

    res = pl.pallas_call(
        body, name=name, grid=(R // tr,), in_specs=specs,
        out_specs=[pl.BlockSpec((tr, C), lambda i: (i, 0)) for _ in out_dtypes],
        out_shape=[jax.ShapeDtypeStruct((R, C), dt) for dt in out_dtypes],
        compiler_params=_params(("parallel",)),
    )(*arrs)
    return res


def _tile_rows(rows, cols):
    t = 256
    while t > 8 and (t * cols > 512 * 1024 or rows % t):
        t //= 2
    return t


def _rows_at(fn, pos, ins, outs, steps, *, name, aliases=None):
    read = [n for n, (_, s) in enumerate(ins) if s is not ANY]
    ni = len(ins)

    def body(pos_ref, *refs):
        vals = fn(*[refs[n][...] for n in read])
        for o_ref, v in zip(refs[ni:], vals):
            o_ref[...] = v.astype(o_ref.dtype)

    return pl.pallas_call(
        body, name=name,
        grid_spec=pltpu.PrefetchScalarGridSpec(num_scalar_prefetch=1, grid=(steps,), in_specs=[s for _, s in ins],
                                               out_specs=[s for _, s in outs]),
        out_shape=[sh for sh, _ in outs],
        input_output_aliases={1 + i: o for i, o in (aliases or {}).items()},
        compiler_params=_params(("parallel",)),
    )(pos, *[a for a, _ in ins])


def cast_into_whole(pos, w, l, axis, *, name):
    _, r, n = w.shape
    tr = _tile_rows(r, n)
    if axis == 1:
        shape, spec = (r, n * N_CHIPS), pl.BlockSpec((tr, n), lambda i, p: (i, p[3]))
    else:
        shape, spec = (r * N_CHIPS, n), pl.BlockSpec((tr, n), lambda i, p: (p[3] * (r // tr) + i, 0))
    return _rows_at(lambda a: (a,), pos, [(w, pl.BlockSpec((None, tr, n), lambda i, p: (l, i, 0)))],
                    [(jax.ShapeDtypeStruct(shape, BF16), spec)], r // tr, name=name)[0]


def pair_sum(pos, theirs, g32, axis, *, name):
    rows2, cols = theirs.shape
    h = rows2 // (N_CHIPS if axis == 0 else 1)
    tr = _tile_rows(h, cols)
    hb = h // tr
    if axis == 1:
        own = pl.BlockSpec((tr, cols), lambda i, p: (p[2] * hb + i, 0))
    else:
        own = pl.BlockSpec((tr, cols), lambda i, p: ((2 * (i // hb) + p[2]) * hb + i % hb, 0))
    row = pl.BlockSpec((tr, cols), lambda i, p: (i, 0))
    return _rows_at(lambda t, m: (m + t.astype(F32),) * 2, pos, [(theirs, row), (g32, own)],
                    [(jax.ShapeDtypeStruct((rows2, cols), F32), row), (jax.ShapeDtypeStruct((rows2, cols), BF16), row)],
                    rows2 // tr, name=name)


def chip_sum(pos, h32, recv, l, axis, whole, *, name):
    _, depth, h, n = recv.shape
    tr = _tile_rows(h, n)
    hb = h // tr
    if axis == 1:
        mine = pl.BlockSpec((tr, n), lambda i, p: (i, p[3]))
    else:
        mine = pl.BlockSpec((tr, n), lambda i, p: (p[3] * hb + i, 0))
    ins = [(h32, mine)] + [(recv, pl.BlockSpec((None, None, tr, n), functools.partial(lambda i, p, j: (j, l, i, 0), j=j)))
                           for j in range(3)]
    if whole is not None:
        ins.append((whole, ANY))
    return _rows_at(lambda o, a, b, c: (((o + a.astype(F32)) + b.astype(F32)) + c.astype(F32),), pos, ins,
                    [(jax.ShapeDtypeStruct((depth, 2, h, n), F32), pl.BlockSpec((None, None, tr, n), lambda i, p: (l, p[2], i, 0)))],
                    hb, name=name, aliases=None if whole is None else {4: 0})[0]


def _adamw(w, g, m, v):
    m2 = ADAM_B1 * m + (1.0 - ADAM_B1) * g
    v2 = ADAM_B2 * v + (1.0 - ADAM_B2) * (g * g)
    m_hat = m2 / (1.0 - ADAM_B1 ** ADAM_STEP)
    v_hat = v2 / (1.0 - ADAM_B2 ** ADAM_STEP)
    delta = -ADAM_LR * (m_hat / (jnp.sqrt(v_hat) + ADAM_EPS) + ADAM_WD * w)
    return delta, m2, v2


def _place():
    return lax.axis_index("x"), lax.axis_index("y"), lax.axis_index("c")


def _chip_peers(x, y, c):
    return [((1 - x, y, c), 2 * (1 - x) + y), ((x, 1 - y, c), 2 * x + 1 - y), ((1 - x, 1 - y, c), 2 * (1 - x) + 1 - y)]


def _shard_of(ref, axis, k, n):
    start = pl.multiple_of(k * n, 128)
    return ref.at[pl.ds(start, n), :] if axis == 0 else ref.at[:, pl.ds(start, n)]


ANY = pl.BlockSpec(memory_space=pl.ANY)


class CopyJob:
    def __init__(self, ins, out_shape, scratch, copies, aliases=None):
        self.ins, self.out_shape, self.scratch, self.copies = list(ins), list(out_shape), list(scratch), copies
        self.aliases = dict(aliases or {})

    def start(self, ins, outs, sems):
        local, remote, _, _ = self.copies(ins, outs, sems)
        for d in local + remote:
            d.start()

    def finish(self, ins, outs, sems):
        local, remote, arrivals, relays = self.copies(ins, outs, sems)
        for needs, sends, _ in relays:
            for d in needs:
                d.wait_recv()
            for d in sends:
                d.start()
        for d in arrivals + [d for _, _, arrives in relays for d in arrives]:
            d.wait_recv()
        for d in remote + [d for _, sends, _ in relays for d in sends]:
            d.wait_send()
        for d in local:
            d.wait()


def run_job(job, *, name):
    ni, no = len(job.ins), len(job.out_shape)

    def body(*refs):
        parts = refs[:ni], refs[ni:ni + no], refs[ni + no:]
        job.start(*parts)
        job.finish(*parts)

    return pl.pallas_call(
        body, name=name, in_specs=[ANY] * ni, out_specs=[ANY] * no, out_shape=job.out_shape,
        scratch_shapes=job.scratch, input_output_aliases=job.aliases,
    )(*job.ins)


def _job_args(job, n_in, n_out):
    if job is None:
        return dict(ins=[], in_specs=[], out_specs=[], out_shape=[], scratch=[], aliases={})
    return dict(ins=job.ins, in_specs=[ANY] * len(job.ins), out_specs=[ANY] * len(job.out_shape),
                out_shape=job.out_shape, scratch=job.scratch,
                aliases={n_in + i: n_out + o for i, o in job.aliases.items()})


def _hosting(body, job, n_in, n_out, n_scratch, grid):
    if job is None:
        return body
    ji, jo = len(job.ins), len(job.out_shape)
    grid = (grid,) if isinstance(grid, int) else tuple(grid)

    def at(ends):
        hit = None
        for ax, e in enumerate(ends):
            here = pl.program_id(ax) == e
            hit = here if hit is None else jnp.logical_and(hit, here)
        return hit

    def hosted(*refs):
        o = n_in + ji
        s = o + n_out + jo
        parts = refs[n_in:o], refs[o + n_out:s], refs[s + n_scratch:]

        @pl.when(at([0] * len(grid)))
        def _():
            job.start(*parts)

        body(*refs[:n_in], *refs[o:o + n_out], *refs[s:s + n_scratch])

        @pl.when(at([g - 1 for g in grid]))
        def _():
            job.finish(*parts)

    return hosted


def _job_sems(n_remote, n_local):
    return [pltpu.SemaphoreType.DMA((n_remote,)), pltpu.SemaphoreType.DMA((n_remote,)), pltpu.SemaphoreType.DMA((n_local,))]


def gather_job(shards, axes):
    na = len(shards)

    def copies(ins, outs, sems):
        send, recv, _ = sems
        x, y, c = _place()
        k = 2 * x + y
        remote, relays = [], []
        for a in range(na):
            r = outs[a].shape[0] // (N_CHIPS if axes[a] == 0 else 1)
            n = outs[a].shape[axes[a]] // N_CHIPS
            half = r // 2

            def part(kk, cc, a=a, n=n, half=half):
                rows = pl.ds(pl.multiple_of(cc * half + (kk * n if axes[a] == 0 else 0), 8), half)
                return outs[a].at[rows, :] if axes[a] == 0 else outs[a].at[rows, pl.ds(pl.multiple_of(kk * n, 128), n)]

            needs, passes, lands = [], [], []
            for j, (peer, kp) in enumerate(_chip_peers(x, y, c)):
                s = 6 * a + j
                remote.append(pltpu.make_async_remote_copy(part(k, c), part(k, c), send.at[s], recv.at[s],
                                                           device_id=peer, device_id_type=MESH))
                needs.append(pltpu.make_async_remote_copy(part(kp, c), part(kp, c), send.at[s], recv.at[s],
                                                          device_id=peer, device_id_type=MESH))
                passes.append(pltpu.make_async_remote_copy(part(kp, c), part(kp, c), send.at[s + 3], recv.at[s + 3],
                                                           device_id=(x, y, 1 - c), device_id_type=MESH))
                lands.append(pltpu.make_async_remote_copy(part(kp, 1 - c), part(kp, 1 - c), send.at[s + 3], recv.at[s + 3],
                                                          device_id=(x, y, 1 - c), device_id_type=MESH))
            relays.append((needs, passes, lands))
        return [], remote, [], relays

    out_shape = [jax.ShapeDtypeStruct(w.shape, BF16) for w in shards]
    return CopyJob(shards, out_shape, _job_sems(6 * na, 1), copies, {a: a for a in range(na)})


def scatter_job(layers, g16, axes, filled):
    na = len(axes)

    def shard_shape(a):
        r, c = g16[a].shape
        return (r // N_CHIPS, c) if axes[a] == 0 else (r, c // N_CHIPS)

    def copies(ins, outs, sems):
        send, recv_sems, _ = sems
        x, y, c = _place()
        remote = []
        for a in range(na):
            n = shard_shape(a)[axes[a]]
            for r, (peer, kp) in enumerate(_chip_peers(x, y, c)):
                remote.append(pltpu.make_async_remote_copy(_shard_of(ins[a], axes[a], kp, n), outs[a].at[r, layers[a]],
                                                           send.at[3 * a + r], recv_sems.at[3 * a + r],
                                                           device_id=peer, device_id_type=MESH))
        return [], remote, remote, []

    out_shape = [jax.ShapeDtypeStruct((3, DEPTH) + shard_shape(a), BF16) for a in range(na)]
    ins = list(g16)
    aliases = {}
    for a in range(na):
        if filled[a] is not None:
            aliases[len(ins)] = a
            ins.append(filled[a])
    return CopyJob(ins, out_shape, _job_sems(3 * na, 1), copies, aliases)


def pair_job(g16, axes):
    na = len(axes)
    pieces = [1 if ax == 1 else N_CHIPS for ax in axes]

    def copies(ins, outs, sems):
        send, recv, _ = sems
        x, y, c = _place()
        remote = []
        s = 0
        for a in range(na):
            rows = g16[a].shape[0] // (2 * pieces[a])
            for kk in range(pieces[a]):
                src = ins[a].at[pl.ds(pl.multiple_of((2 * kk + 1 - c) * rows, 8), rows), :]
                remote.append(pltpu.make_async_remote_copy(src, outs[a].at[pl.ds(kk * rows, rows), :], send.at[s], recv.at[s],
                                                           device_id=(x, y, 1 - c), device_id_type=MESH))
                s += 1
        return [], remote, remote, []

    out_shape = [jax.ShapeDtypeStruct((g.shape[0] // 2, g.shape[1]), BF16) for g in g16]
    return CopyJob(g16, out_shape, _job_sems(sum(pieces), 1), copies)


def join_job(shards):
    na = len(shards)

    def copies(ins, outs, sems):
        send, recv, _ = sems
        x, y, c = _place()
        remote = [pltpu.make_async_remote_copy(outs[a].at[:, c], outs[a].at[:, c], send.at[a], recv.at[a],
                                               device_id=(x, y, 1 - c), device_id_type=MESH) for a in range(na)]
        lands = [pltpu.make_async_remote_copy(outs[a].at[:, 1 - c], outs[a].at[:, 1 - c], send.at[a], recv.at[a],
                                              device_id=(x, y, 1 - c), device_id_type=MESH) for a in range(na)]
        return [], remote, lands, []

    out_shape = [jax.ShapeDtypeStruct(s.shape, F32) for s in shards]
    return CopyJob(shards, out_shape, _job_sems(na, 1), copies, {a: a for a in range(na)})


def small_job(p):
    def copies(ins, outs, sems):
        send, recv, loc = sems
        x, y, c = _place()
        me = 4 * x + 2 * y + c
        remote, lands = [], []
        for rel in range(1, 8):
            dx, dy, dc = rel >> 2, (rel >> 1) & 1, rel & 1
            peer = (1 - x if dx else x, 1 - y if dy else y, 1 - c if dc else c)
            who = 4 * peer[0] + 2 * peer[1] + peer[2]
            remote.append(pltpu.make_async_remote_copy(ins[0], outs[0].at[me], send.at[rel - 1], recv.at[rel - 1],
                                                       device_id=peer, device_id_type=MESH))
            lands.append(pltpu.make_async_remote_copy(ins[0], outs[0].at[who], send.at[rel - 1], recv.at[rel - 1],
                                                      device_id=peer, device_id_type=MESH))
        return [pltpu.make_async_copy(ins[0], outs[0].at[me], loc.at[0])], remote, lands, []

    return CopyJob([p], [jax.ShapeDtypeStruct((8,) + p.shape, F32)], _job_sems(7, 1), copies)


def small_sum(slots):
    def add(*terms):
        acc = terms[0]
        for t in terms[1:]:
            acc = acc + t
        return (acc,)

    return _rows_call(add, [(slots, d) for d in range(8)], [F32], name="small_sum", tr=8 * 47)[0]


BIG = ("w_in", "p_ret", "p_sb", "p_sgu", "w_out", "w_up", "w_down")
BIG_AXIS = {"w_in": 1, "p_ret": 1, "p_sb": 1, "p_sgu": 1, "w_out": 0, "w_up": 1, "w_down": 0}
SMALL = ("ret_gn_g", "ret_gn_b", "sgu_ln_g", "sgu_ln_b", "sgu_w", "sgu_b", "ln1_g", "ln1_b", "ln2_g", "ln2_b")


def layer_forward(l, x0, W, sm, rope, rconsts, hooks):
    n = f"l{l}_"
    jobs = hooks.fwd_jobs(l)
    proj = matmul(x0, W["w_in"], mode="nn", tm=1024, tn=640, tk=1024, name=n + "proj")
    retg, raw, states = ret_fwd(proj, *rope, rconsts, sm["ret_gn_g"], sm["ret_gn_b"], name=n + "ret_fwd")
    sb, job_out = sb_fwd(proj, name=n + "sb_fwd", job=jobs.get("sb"))
    if jobs.get("sb") is not None:
        hooks.done(jobs["sb"], job_out)
    sg = sgu_fwd(proj, sm["sgu_ln_g"], sm["sgu_ln_b"], sm["sgu_w"], sm["sgu_bias"], name=n + "sgu_fwd")
    merged, r1, r2, r3 = merge_fwd(retg, sb, sg, W["p_ret"], W["p_sb"], W["p_sgu"], proj, name=n + "merge_fwd")
    x1, xh1, rs1 = matmul_ln(merged, W["w_out"], x0, sm["ln1_g"], sm["ln1_b"], tk=1024, name=n + "out_ln1")
    h1 = matmul(x1, W["w_up"], mode="nn", tm=1024, tn=1024, tk=1024, name=n + "up")
    res = matmul_ln(h1, W["w_down"], x1, sm["ln2_g"], sm["ln2_b"], pro=_relu2, tk=1024, name=n + "down_ln2",
                    job=jobs.get("down"))
    if jobs.get("down") is not None:
        res, job_out = res
        hooks.done(jobs["down"], job_out)
    x2, xh2, rs2 = res
    saved = dict(x0=x0, proj=proj, retg=retg, raw=raw, states=states, sb=sb, sg=sg, merged=merged, r=(r1, r2, r3),
                 x1=x1, xh1=xh1, rs1=rs1, h1=h1, xh2=xh2, rs2=rs2)
    return x2, saved


def layer_backward(l, dx2, s, W, sm, rope, rconsts, hooks):
    n = f"l{l}_"
    two = ((F32, None), (BF16, None))
    gw, gs = {}, {}
    du2, gs["ln2_g"], gs["ln2_b"] = ln_bwd(dx2, s["xh2"], s["rs2"], sm["ln2_g"], name=n + "ln2_bwd")
    gw["w_down"] = matmul(s["h1"], du2, mode="tn", tm=1024, tn=1024, tk=512, pro=_relu2, outs=two, name=n + "g_down")
    dh1 = matmul(du2, W["w_down"], mode="nt", tm=1024, tn=1024, tk=1024, outs=((BF16, None),),
                 epi=lambda acc, h: (acc * (2.0 * jnp.maximum(h, 0.0)),), tiles=(s["h1"],), name=n + "d_h1")
    gw["w_up"] = matmul(s["x1"], dh1, mode="tn", tm=1024, tn=1024, tk=512, outs=two, name=n + "g_up")
    dx1 = matmul(dh1, W["w_up"], mode="nt", tm=1024, tn=1024, tk=1024,
                 epi=lambda acc, d: (acc + ALPHA * d,), tiles=(du2,), name=n + "d_x1")
    du1, gs["ln1_g"], gs["ln1_b"] = ln_bwd(dx1, s["xh1"], s["rs1"], sm["ln1_g"], name=n + "ln1_bwd")
    gw["w_out"] = matmul(s["merged"], du1, mode="tn", tm=1024, tn=1024, tk=512, outs=two, name=n + "g_out")
    dmerged = matmul(du1, W["w_out"], mode="nt", tm=1024, tn=1024, tk=1024, name=n + "d_merged")
    dr1, dr2, dr3, dg1, dg2, dg3 = merge_bwd(dmerged, *s["r"], s["proj"], name=n + "merge_bwd")
    d_branch = {}
    for nm, a, dr in (("p_ret", s["retg"], dr1), ("p_sb", s["sb"], dr2), ("p_sgu", s["sg"], dr3)):
        gw[nm] = matmul(a, dr, mode="tn", tm=512, tn=1024, tk=512, outs=two, name=n + "g_" + nm)
        d_branch[nm] = matmul(dr, W[nm], mode="nt", tm=1024, tn=512, tk=1024, name=n + "d_" + nm)
    job = hooks.pair(l, gw)
    dret, gs["ret_gn_g"], gs["ret_gn_b"], job_out = ret_bwd(s["proj"], *rope, rconsts, sm["ret_gn_g"], sm["ret_gn_b"],
                                                             s["raw"], s["states"], d_branch["p_ret"], name=n + "ret_bwd", job=job)
    if job is not None:
        hooks.done(job, job_out)
    job = hooks.scatter(l) if job is not None else None
    dsq, dsk, dsv, job_out = sb_bwd(s["proj"], s["sb"], d_branch["p_sb"], name=n + "sb_bwd", job=job)
    if job is not None:
        hooks.done(job, job_out)
    dsgu, gs["sgu_w"], dbias, gs["sgu_ln_g"], gs["sgu_ln_b"] = sgu_bwd(
        s["proj"], sm["sgu_ln_g"], sm["sgu_ln_b"], sm["sgu_w"], sm["sgu_bias"], d_branch["p_sgu"], name=n + "sgu_bwd")
    gs["sgu_b"] = dbias[:, :, 0]
    dproj = jnp.concatenate([dret, dsq, dsk, dsv, dsgu, dg1, dg2, dg3], axis=1)
    job = hooks.small(l, gs)
    gw["w_in"] = matmul(s["x0"], dproj, mode="tn", tm=1024, tn=1536, tk=512, outs=two, name=n + "g_in", job=job)
    if job is not None:
        gw["w_in"], job_out = gw["w_in"]
        hooks.done(job, job_out)
    job = hooks.tail(l, gw["w_in"])
    dx0 = matmul(dproj, W["w_in"], mode="nt", tm=1024, tn=1024, tk=1536,
                 epi=lambda acc, d: (acc + ALPHA * d,), tiles=(du1,), name=n + "d_x0", job=job)
    if job is not None:
        dx0, job_out = dx0
        hooks.done(job, job_out)
    return dx0, gw, gs


def local_step(x, target, small, plan):
    T = x.shape[0]
    rope = _rope_tables(T)
    rconsts = _ret_consts()
    sms = []
    for l in range(DEPTH):
        sm = {k: small[k][l][None, :] for k in SMALL if k not in ("sgu_w", "sgu_b")}
        sm["sgu_w"] = small["sgu_w"][l]
        sm["sgu_bias"] = jnp.broadcast_to(small["sgu_b"][l][:, :, None], (4, CHUNK, CHUNK))
        sms.append(sm)
    h, saved = x, []
    for l in range(DEPTH):
        h, s = layer_forward(l, h, plan.weights(l), sms[l], rope, rconsts, plan)
        saved.append(s)
    dy, sq = loss_head(h, target)
    gs = {k: [None] * DEPTH for k in SMALL}
    for l in reversed(range(DEPTH)):
        dy, gwl, gsl = layer_backward(l, dy, saved[l], plan.weights(l), sms[l], rope, rconsts, plan)
        plan.grads(l, gwl)
        for k in SMALL:
            gs[k][l] = gsl[k].reshape(small[k].shape[1:])
    return sq[0, 0], dy, {k: jnp.stack(v) for k, v in gs.items()}


EARLY_GRADS = ("p_ret", "p_sb", "p_sgu", "w_out", "w_up", "w_down")


class _StepPlan:
    def __init__(self, pos, shards16):
        self.pos = pos
        self.shards16 = shards16
        self.full = [dict() for _ in range(DEPTH)]
        self.gw = [None] * DEPTH
        self.bufs = {}
        self.sums = {}
        self.gs = [None] * DEPTH
        first = self._gather([(0, "w_in")])
        self.done(first, run_job(first, name="gather_first"))

    def weights(self, l):
        return self.full[l]

    def grads(self, l, gw):
        self.gw[l] = gw

    def _gather(self, items):
        job = gather_job([self.shards16[l][k] for l, k in items], [BIG_AXIS[k] for _, k in items])
        job.note = ("gather", items)
        return job

    def _pair(self, items):
        job = pair_job([g[1] for _, _, g in items], [BIG_AXIS[k] for _, k, _ in items])
        job.note = ("pair", items)
        return job

    def fwd_jobs(self, l):
        jobs = {"sb": self._gather([(l, k) for k in BIG[1:]])}
        if l + 1 < DEPTH:
            jobs["down"] = self._gather([(l + 1, "w_in")])
        return jobs

    def pair(self, l, ready):
        items = [(l, k, ready[k]) for k in EARLY_GRADS]
        if l + 1 < DEPTH:
            items.append((l + 1, "w_in", self.gw[l + 1]["w_in"]))
        return self._pair(items)

    def scatter(self, l):
        items, sums16 = self.summed
        job = scatter_job([l_ for l_, _, _ in items], sums16, [BIG_AXIS[k] for _, k, _ in items],
                          [self.bufs.get(k) for _, k, _ in items])
        job.note = ("scatter", items)
        return job

    def small(self, l, gs):
        self.gs[l] = {k: gs[k].reshape(-1) for k in SMALL}
        if l != 0:
            return None
        job = small_job(_pack_small({k: jnp.stack([self.gs[l_][k] for l_ in range(DEPTH)]) for k in SMALL}))
        job.note = ("small", [])
        return job

    def tail(self, l, g):
        if l != 0:
            return None
        last = self._pair([(0, "w_in", g)])
        self.done(last, run_job(last, name="pair_last"))
        return self.scatter(0)

    def done(self, job, outs):
        kind, items = job.note
        if kind == "small":
            self.small_slots = outs[0]
        if kind == "pair":
            sums16 = []
            for a, (l, k, g) in enumerate(items):
                self.sums[(l, k)], s16 = pair_sum(self.pos, outs[a], g[0], BIG_AXIS[k], name=f"pair_sum_{k}_{l}")
                sums16.append(s16)
            self.summed = (items, sums16)
        for a, item in enumerate(items):
            if kind == "gather":
                self.full[item[0]][item[1]] = outs[a]
            elif kind == "scatter":
                self.bufs[item[1]] = outs[a]

    def finish(self):
        return self.bufs, self.sums


def _flat2(a):
    return a.reshape(-1, a.shape[-1])


def _pack_small(d, pre=""):
    return jnp.concatenate([d[pre + k].reshape(-1) for k in SMALL]).reshape(-1, 128)


def kernel(x, w_in, ret_gn_g, ret_gn_b, sgu_ln_g, sgu_ln_b, sgu_w, sgu_b, p_ret, p_sb, p_sgu, w_out, ln1_g, ln1_b, w_up, w_down, ln2_g, ln2_b, loss_target, m_w_in, m_ret_gn_g, m_ret_gn_b, m_sgu_ln_g, m_sgu_ln_b, m_sgu_w, m_sgu_b, m_p_ret, m_p_sb, m_p_sgu, m_w_out, m_ln1_g, m_ln1_b, m_w_up, m_w_down, m_ln2_g, m_ln2_b, v_w_in, v_ret_gn_g, v_ret_gn_b, v_sgu_ln_g, v_sgu_ln_b, v_sgu_w, v_sgu_b, v_p_ret, v_p_sb, v_p_sgu, v_w_out, v_ln1_g, v_ln1_b, v_w_up, v_w_down, v_ln2_g, v_ln2_b):
    given = dict(locals())
    order = BIG[:1] + SMALL[:6] + BIG[1:5] + SMALL[6:8] + BIG[5:7] + SMALL[8:10]
    L = DEPTH

    px, py, pc = _place()
    pos = jnp.stack([px, py, pc, 2 * px + py]).astype(jnp.int32)

    shards16 = [{k: cast_into_whole(pos, given[k], l, BIG_AXIS[k], name=f"cast_{k}_{l}") for k in BIG} for l in range(L)]
    plan = _StepPlan(pos, shards16)
    sq, dx, gs = local_step(x[0], loss_target[0], {k: given[k] for k in SMALL}, plan)
    loss = 0.5 * lax.psum(sq, ("x", "y", "c"))

    bufs, sums = plan.finish()
    shards = []
    for k in BIG:
        whole = None
        for l in range(L):
            whole = chip_sum(pos, sums[(l, k)], bufs[k], l, BIG_AXIS[k], whole, name=f"chip_sum_{k}_{l}")
        shards.append(whole)
    joined = run_job(join_job(shards), name="join_halves")
    out = {}
    for a, k in enumerate(BIG):
        shp = given[k].shape
        res = _rows_call(lambda g_, w_, m_, v_: (g_,) + _adamw(w_, g_, m_, v_),
                         [joined[a].reshape(-1, shp[-1]), _flat2(given[k]), _flat2(given["m_" + k]), _flat2(given["v_" + k])],
                         [F32] * 4, name="adamw_" + k)
        out[k] = [r.reshape(shp) for r in res]

    pack = _pack_small
    res = _rows_call(lambda g_, w_, m_, v_: (g_,) + _adamw(w_, g_, m_, v_),
                     [small_sum(plan.small_slots), pack(given), pack(given, "m_"), pack(given, "v_")], [F32] * 4,
                     name="adamw_small", tr=8 * 47)
    off = 0
    for k in SMALL:
        sz = given[k].size
        out[k] = [r.reshape(-1)[off:off + sz].reshape(given[k].shape) for r in res]
        off += sz

    grads = [out[k][0] for k in order]
    deltas = [out[k][1] for k in order]
    new_m = [out[k][2] for k in order]
    new_v = [out[k][3] for k in order]
    return (loss, dx[None], *grads, *deltas, *new_m, *new_v)
```

```python
import functools
import math

import jax
import jax.numpy as jnp
from jax import lax
from jax.experimental import pallas as pl
from jax.experimental.pallas import tpu as pltpu

F32 = jnp.float32
BF16 = jnp.bfloat16

D_MODEL = 1024
SEQ = 4096
DEPTH = 2
CHUNK = 128
RET_HEADS = 4
BRANCH_W = 512
N_IN = 7680
D_FF = 4096
LN_EPS = 1e-5
ROPE_BASE = 10000.0
ALPHA = (2 * DEPTH) ** 0.25
RET_SCALE = 128 ** -0.5
SB_SCALE = 64 ** -0.5
C_RET, C_SB, C_SGU, C_GATE = 0, 2048, 3584, 4608

ADAM_LR, ADAM_B1, ADAM_B2, ADAM_EPS, ADAM_WD, ADAM_STEP = 0.001, 0.9, 0.999, 1e-08, 0.01, 10

N_CHIPS = 4
VMEM_LIMIT = 56 * 1024 * 1024
MESH = pl.DeviceIdType.MESH

NN = ((1,), (0,))
NT = ((1,), (1,))
TN = ((0,), (0,))


def _dot(a, b, dims):
    return lax.dot_general(a, b, (dims, ((), ())), preferred_element_type=F32)


def _params(sem):
    return pltpu.CompilerParams(dimension_semantics=sem, vmem_limit_bytes=VMEM_LIMIT)


def _relu2(h):
    r = jnp.maximum(h, 0.0)
    return r * r


def matmul(a, b, *, mode, tm, tn, tk, outs=((F32, None),), pro=None, epi=None, tiles=(), rows=(), name, job=None):
    if mode == "nn":
        (M, K), N = a.shape, b.shape[1]
    elif mode == "nt":
        (M, K), N = a.shape, b.shape[0]
    else:
        (K, M), N = a.shape, b.shape[1]
    tm, tn, tk = min(tm, M), min(tn, N), min(tk, K)
    assert M % tm == 0 and N % tn == 0 and K % tk == 0, (name, M, N, K, tm, tn, tk)
    if mode == "nn":
        a_spec = pl.BlockSpec((tm, tk), lambda i, j, k: (i, k))
        b_spec = pl.BlockSpec((tk, tn), lambda i, j, k: (k, j))
        dims = NN
    elif mode == "nt":
        a_spec = pl.BlockSpec((tm, tk), lambda i, j, k: (i, k))
        b_spec = pl.BlockSpec((tn, tk), lambda i, j, k: (j, k))
        dims = NT
    else:
        a_spec = pl.BlockSpec((tk, tm), lambda i, j, k: (k, i))
        b_spec = pl.BlockSpec((tk, tn), lambda i, j, k: (k, j))
        dims = TN
    nk = K // tk
    nt_, nr, no = len(tiles), len(rows), len(outs)

    def body(a_ref, b_ref, *rest):
        tile_refs = rest[:nt_]
        row_refs = rest[nt_:nt_ + nr]
        out_refs = rest[nt_ + nr:nt_ + nr + no]
        av = a_ref[...]
        if pro is not None:
            av = pro(av)
        p = _dot(av.astype(BF16), b_ref[...].astype(BF16), dims)

        def finish(acc):
            vals = (acc,) * no if epi is None else epi(acc, *[r[...] for r in tile_refs], *[r[...] for r in row_refs])
            for o_ref, v in zip(out_refs, vals):
                o_ref[...] = v.astype(o_ref.dtype)

        if nk == 1:
            finish(p)
        else:
            acc_ref = rest[-1]
            k = pl.program_id(2)

            @pl.when(k == 0)
            def _():
                acc_ref[...] = p

            @pl.when(k > 0)
            def _():
                acc_ref[...] += p

            @pl.when(k == nk - 1)
            def _():
                finish(acc_ref[...])

    out_shape, out_specs = [], []
    for dt, width in outs:
        if width is None:
            out_shape.append(jax.ShapeDtypeStruct((M, N), dt))
            out_specs.append(pl.BlockSpec((tm, tn), lambda i, j, k: (i, j)))
        else:
            assert N == tn
            out_shape.append(jax.ShapeDtypeStruct((M, width), dt))
            out_specs.append(pl.BlockSpec((tm, width), lambda i, j, k: (i, 0)))
    in_specs = [a_spec, b_spec]
    in_specs += [pl.BlockSpec((tm, tn), lambda i, j, k: (i, j)) for _ in tiles]
    in_specs += [pl.BlockSpec((1, tn), lambda i, j, k: (0, j)) for _ in rows]
    grid = (M // tm, N // tn, nk)
    scratch = [pltpu.VMEM((tm, tn), F32)] if nk > 1 else []
    j = _job_args(job, len(in_specs), no)
    res = pl.pallas_call(
        _hosting(body, job, len(in_specs), no, len(scratch), grid), name=name, grid=grid,
        in_specs=in_specs + j["in_specs"], out_specs=out_specs + j["out_specs"], out_shape=out_shape + j["out_shape"],
        scratch_shapes=scratch + j["scratch"], input_output_aliases=j["aliases"],
        compiler_params=_params(("parallel", "parallel", "arbitrary") if job is None else ("arbitrary",) * 3),
    )(a, b, *tiles, *rows, *j["ins"])
    mine = res[0] if no == 1 else list(res[:no])
    return mine if job is None else (mine, list(res[no:]))


def _ln_epi(acc, res, g, b):
    u = ALPHA * res + acc
    mu = jnp.mean(u, axis=-1, keepdims=True)
    xc = u - mu
    var = jnp.mean(xc * xc, axis=-1, keepdims=True)
    rstd = lax.rsqrt(var + LN_EPS)
    xhat = xc * rstd
    return xhat * g + b, xhat, jnp.broadcast_to(rstd, (u.shape[0], 128))


def matmul_ln(a, w, res, g, b, *, pro=None, tk, name, job=None):
    n = w.shape[1]
    return matmul(a, w, mode="nn", tm=1024, tn=n, tk=tk, pro=pro, epi=_ln_epi, tiles=(res,), rows=(g, b),
                  outs=((F32, None), (F32, None), (F32, 128)), name=name, job=job)


def ln_bwd(dy, xhat, rstd, g, *, name):
    T, D = dy.shape
    tm = min(512, T)

    def body(dy_ref, xh_ref, rs_ref, g_ref, du_ref, du16_ref, dg_ref, db_ref):
        dyv, xh = dy_ref[...], xh_ref[...]
        r = rs_ref[:, 0:1]
        dxh = dyv * g_ref[...]
        m1 = jnp.mean(dxh, axis=-1, keepdims=True)
        m2 = jnp.mean(dxh * xh, axis=-1, keepdims=True)
        du = r * (dxh - m1 - xh * m2)
        du_ref[...] = du
        du16_ref[...] = du.astype(BF16)

        @pl.when(pl.program_id(0) == 0)
        def _():
            dg_ref[...] = jnp.zeros_like(dg_ref)
            db_ref[...] = jnp.zeros_like(db_ref)

        dg_ref[...] += jnp.sum(dyv * xh, axis=0, keepdims=True)
        db_ref[...] += jnp.sum(dyv, axis=0, keepdims=True)

    row = pl.BlockSpec((tm, D), lambda i: (i, 0))
    vec = pl.BlockSpec((1, D), lambda i: (0, 0))
    return pl.pallas_call(
        body, name=name, grid=(T // tm,),
        in_specs=[row, row, pl.BlockSpec((tm, 128), lambda i: (i, 0)), vec],
        out_specs=[row, row, vec, vec],
        out_shape=[jax.ShapeDtypeStruct((T, D), F32), jax.ShapeDtypeStruct((T, D), BF16),
                   jax.ShapeDtypeStruct((1, D), F32), jax.ShapeDtypeStruct((1, D), F32)],
        compiler_params=_params(("arbitrary",)),
    )(dy, xhat, rstd, g)


def loss_head(y, target):
    T, D = y.shape
    tm = min(512, T)

    def body(y_ref, t_ref, dy_ref, s_ref):
        e = y_ref[...] - t_ref[...]
        dy_ref[...] = e * (1.0 / D)

        @pl.when(pl.program_id(0) == 0)
        def _():
            s_ref[...] = jnp.zeros_like(s_ref)

        s_ref[...] += jnp.sum(jnp.mean(e * e, axis=-1, keepdims=True))

    row = pl.BlockSpec((tm, D), lambda i: (i, 0))
    return pl.pallas_call(
        body, name="loss_head", grid=(T // tm,),
        in_specs=[row, row], out_specs=[row, pl.BlockSpec((8, 128), lambda i: (0, 0))],
        out_shape=[jax.ShapeDtypeStruct((T, D), F32), jax.ShapeDtypeStruct((8, 128), F32)],
        compiler_params=_params(("arbitrary",)),
    )(y, target)


def _rope_tables(T):
    half = 64
    inv_freq = ROPE_BASE ** (-jnp.arange(half, dtype=F32) / half)
    ang = jnp.arange(T, dtype=jnp.int32).astype(F32)[:, None] * inv_freq[None, :]
    cos, sin = jnp.cos(ang), jnp.sin(ang)
    return jnp.concatenate([cos, cos], axis=1), jnp.concatenate([-sin, sin], axis=1)


def _ret_consts():
    H = RET_HEADS
    log_g = jnp.log(1.0 - 2.0 ** (-5.0 - jnp.arange(H, dtype=F32)))
    idx = jnp.arange(CHUNK, dtype=F32)
    diff = idx[:, None] - idx[None, :]
    dmat = jnp.where(diff[None] >= 0, jnp.exp(log_g[:, None, None] * diff[None]), 0.0)
    kd = jnp.exp(log_g[:, None] * (CHUNK - 1 - idx)[None, :])
    qd = jnp.exp(log_g[:, None] * (idx + 1.0)[None, :])
    cd = jnp.exp(log_g * CHUNK)
    full = (H, CHUNK, CHUNK)
    return (dmat.astype(F32), jnp.broadcast_to(kd[:, :, None], full), jnp.broadcast_to(qd[:, :, None], full),
            jnp.broadcast_to(cd[:, None, None], full))


def _swap_halves(v):
    return pltpu.roll(v, 64, 1)


def _group_norm(o):
    mu = jnp.mean(o, axis=-1, keepdims=True)
    xc = o - mu
    var = jnp.mean(xc * xc, axis=-1, keepdims=True)
    rstd = lax.rsqrt(var + LN_EPS)
    return xc * rstd, rstd


def ret_fwd(proj, cosf, sinf, consts, gn_g, gn_b, *, name):
    T = proj.shape[0]
    tb = min(512, T)
    nch = tb // CHUNK
    H = RET_HEADS

    def body(p_ref, cos_ref, sin_ref, dm_ref, kd_ref, qd_ref, cd_ref, g_ref, b_ref, out_ref, raw_ref, st_ref, s_ref):
        @pl.when(pl.program_id(0) == 0)
        def _():
            s_ref[...] = jnp.zeros_like(s_ref)

        for c in range(nch):
            r = slice(c * CHUNK, (c + 1) * CHUNK)
            cs, sn = cos_ref[r, :], sin_ref[r, :]
            for h in range(H):
                hc = slice(h * 128, (h + 1) * 128)
                q = p_ref[r, h * 128:(h + 1) * 128]
                k = p_ref[r, 512 + h * 128:512 + (h + 1) * 128]
                v = p_ref[r, 1024 + h * 128:1024 + (h + 1) * 128]
                gt = p_ref[r, 1536 + h * 128:1536 + (h + 1) * 128]
                qr = q * cs + _swap_halves(q) * sn
                kr = (k * cs + _swap_halves(k) * sn) * RET_SCALE
                sprev = s_ref[h]
                st_ref[c, h] = sprev
                qb, kb, vb = qr.astype(BF16), kr.astype(BF16), v.astype(BF16)
                s = _dot(qb, kb, NT) * dm_ref[h]
                o = _dot(s.astype(BF16), vb, NN) + _dot((qr * qd_ref[h]).astype(BF16), sprev.astype(BF16), NN)
                s_ref[h] = sprev * cd_ref[h] + _dot((kr * kd_ref[h]).astype(BF16), vb, TN)
                raw_ref[r, hc] = o
                y, _ = _group_norm(o)
                out_ref[r, hc] = (gt * jax.nn.sigmoid(gt)) * (y * g_ref[:, hc] + b_ref[:, hc])

    cmat = pl.BlockSpec((H, CHUNK, CHUNK), lambda i: (0, 0, 0))
    vec = pl.BlockSpec((1, BRANCH_W), lambda i: (0, 0))
    rope = pl.BlockSpec((tb, 128), lambda i: (i, 0))
    blk = pl.BlockSpec((tb, BRANCH_W), lambda i: (i, 0))
    return pl.pallas_call(
        body, name=name, grid=(T // tb,),
        in_specs=[pl.BlockSpec((tb, 2048), lambda i: (i, 0)), rope, rope, cmat, cmat, cmat, cmat, vec, vec],
        out_specs=[blk, blk, pl.BlockSpec((nch, H, CHUNK, CHUNK), lambda i: (i, 0, 0, 0))],
        out_shape=[jax.ShapeDtypeStruct((T, BRANCH_W), F32), jax.ShapeDtypeStruct((T, BRANCH_W), F32),
                   jax.ShapeDtypeStruct((T // CHUNK, H, CHUNK, CHUNK), F32)],
        scratch_shapes=[pltpu.VMEM((H, CHUNK, CHUNK), F32)],
        compiler_params=_params(("arbitrary",)),
    )(proj, cosf, sinf, *consts, gn_g, gn_b)


def ret_bwd(proj, cosf, sinf, consts, gn_g, gn_b, raw, states, dout, *, name, job=None):
    T = proj.shape[0]
    tb = min(512, T)
    nch = tb // CHUNK
    nb = T // tb
    H = RET_HEADS

    def body(p_ref, cos_ref, sin_ref, dm_ref, kd_ref, qd_ref, cd_ref, g_ref, b_ref, raw_ref, st_ref, do_ref,
             dp_ref, dg_ref, db_ref, ds_ref):
        @pl.when(pl.program_id(0) == 0)
        def _():
            ds_ref[...] = jnp.zeros_like(ds_ref)
            dg_ref[...] = jnp.zeros_like(dg_ref)
            db_ref[...] = jnp.zeros_like(db_ref)

        for c in reversed(range(nch)):
            r = slice(c * CHUNK, (c + 1) * CHUNK)
            cs, sn = cos_ref[r, :], sin_ref[r, :]
            for h in range(H):
                hc = slice(h * 128, (h + 1) * 128)
                q = p_ref[r, h * 128:(h + 1) * 128]
                k = p_ref[r, 512 + h * 128:512 + (h + 1) * 128]
                v = p_ref[r, 1024 + h * 128:1024 + (h + 1) * 128]
                gt = p_ref[r, 1536 + h * 128:1536 + (h + 1) * 128]
                qr = q * cs + _swap_halves(q) * sn
                kr = (k * cs + _swap_halves(k) * sn) * RET_SCALE
                sprev = st_ref[c, h]
                gv = g_ref[:, hc]
                y, rstd = _group_norm(raw_ref[r, hc])
                d_out = do_ref[r, hc]
                sg = jax.nn.sigmoid(gt)
                d_gate = d_out * (y * gv + b_ref[:, hc]) * (sg * (1.0 + gt * (1.0 - sg)))
                d_aff = d_out * (gt * sg)
                dg_ref[:, hc] += jnp.sum(d_aff * y, axis=0, keepdims=True)
                db_ref[:, hc] += jnp.sum(d_aff, axis=0, keepdims=True)
                dxh = d_aff * gv
                m1 = jnp.mean(dxh, axis=-1, keepdims=True)
                m2 = jnp.mean(dxh * y, axis=-1, keepdims=True)
                d_o = (rstd * (dxh - m1 - y * m2)).astype(BF16)
                qb, kb, vb = qr.astype(BF16), kr.astype(BF16), v.astype(BF16)
                dm, kd, qd = dm_ref[h], kd_ref[h], qd_ref[h]
                p = (_dot(qb, kb, NT) * dm).astype(BF16)
                dp = (_dot(d_o, vb, NT) * dm).astype(BF16)
                dsn = ds_ref[h]
                dsb = dsn.astype(BF16)
                dq_r = _dot(dp, kb, NN) + _dot(d_o, sprev.astype(BF16), NT) * qd
                dk_r = (_dot(dp, qb, TN) + _dot(vb, dsb, NT) * kd) * RET_SCALE
                d_v = _dot(p, d_o, TN) + _dot((kr * kd).astype(BF16), dsb, NN)
                ds_ref[h] = dsn * cd_ref[h] + _dot((qr * qd).astype(BF16), d_o, TN)
                dp_ref[r, h * 128:(h + 1) * 128] = (dq_r * cs - _swap_halves(dq_r) * sn).astype(BF16)
                dp_ref[r, 512 + h * 128:512 + (h + 1) * 128] = (dk_r * cs - _swap_halves(dk_r) * sn).astype(BF16)
                dp_ref[r, 1024 + h * 128:1024 + (h + 1) * 128] = d_v.astype(BF16)
                dp_ref[r, 1536 + h * 128:1536 + (h + 1) * 128] = d_gate.astype(BF16)

    cmat = pl.BlockSpec((H, CHUNK, CHUNK), lambda i: (0, 0, 0))
    vec = pl.BlockSpec((1, BRANCH_W), lambda i: (0, 0))
    rope = pl.BlockSpec((tb, 128), lambda i: (nb - 1 - i, 0))
    blk = pl.BlockSpec((tb, BRANCH_W), lambda i: (nb - 1 - i, 0))
    wide = pl.BlockSpec((tb, 2048), lambda i: (nb - 1 - i, 0))
    j = _job_args(job, 12, 3)
    res = pl.pallas_call(
        _hosting(body, job, 12, 3, 1, nb), name=name, grid=(nb,),
        in_specs=[wide, rope, rope, cmat, cmat, cmat, cmat, vec, vec, blk,
                  pl.BlockSpec((nch, H, CHUNK, CHUNK), lambda i: (nb - 1 - i, 0, 0, 0)), blk] + j["in_specs"],
        out_specs=[wide, vec, vec] + j["out_specs"],
        out_shape=[jax.ShapeDtypeStruct((T, 2048), BF16), jax.ShapeDtypeStruct((1, BRANCH_W), F32),
                   jax.ShapeDtypeStruct((1, BRANCH_W), F32)] + j["out_shape"],
        scratch_shapes=[pltpu.VMEM((H, CHUNK, CHUNK), F32)] + j["scratch"], input_output_aliases=j["aliases"],
        compiler_params=_params(("arbitrary",)),
    )(proj, cosf, sinf, *consts, gn_g, gn_b, raw, states, dout, *j["ins"])
    return res[0], res[1], res[2], list(res[3:])


def _sb_masks():
    row = lax.broadcasted_iota(jnp.int32, (CHUNK, CHUNK), 0)
    lane = lax.broadcasted_iota(jnp.int32, (CHUNK, CHUNK), 1)
    return row, lane


SB_QT = 256
SB_DEAD = -105.0


def _pair(v):
    hi = v.astype(BF16)
    return jnp.concatenate([hi, (v - hi.astype(F32)).astype(BF16)], axis=1)


def _sb_consts():
    r = lax.broadcasted_iota(jnp.int32, (256, 256), 0) & 127
    c = lax.broadcasted_iota(jnp.int32, (256, 256), 1)
    ones = c >= 128
    lane = lax.broadcasted_iota(jnp.int32, (CHUNK, CHUNK), 1)
    return (ones | (r > c)).astype(BF16), (ones | (r >= c)).astype(BF16), (lane < 64, lane >= 64)


def _per_head(x, hms):
    return jnp.concatenate([jnp.where(hm, x, 0.0) for hm in hms], axis=0).astype(BF16)


def _sb_logits(qb, kb2, mask2):
    z = _dot(qb, kb2, NT)
    l1p = jnp.log(1.0 + jnp.exp(-jnp.abs(z)))
    lsp = jnp.minimum(z, 0.0) - l1p
    lsn = lsp - z
    if mask2 is not None:
        lsn = jnp.where(mask2, lsn, 0.0)
    return lsp, lsn


def _sb_tile_mask(qt):
    trow = lax.broadcasted_iota(jnp.int32, (qt, 256), 0)
    tlane = lax.broadcasted_iota(jnp.int32, (qt, 256), 1) & 127
    return lambda m: (tlane + m * CHUNK) < trow


def sb_fwd(proj, *, name, job=None):
    T = proj.shape[0]
    qt = min(SB_QT, T)
    nsub = qt // CHUNK
    cb = C_SB // 128

    def body(q_ref, k_ref, v_ref, o_ref):
        u_gt, _, hms = _sb_consts()
        tile_mask = _sb_tile_mask(qt)

        def qtile(i, _):
            rq = pl.ds(pl.multiple_of(i * qt, qt), qt)
            qb = (q_ref[rq, :] * SB_SCALE).astype(BF16)

            def step(j, state, mask2):
                carry, acc = list(state[:2]), state[2]
                rk = pl.ds(pl.multiple_of(j * CHUNK, CHUNK), CHUNK)
                lsp, lsn = _sb_logits(qb, _per_head(k_ref[rk, :], hms), mask2)
                a_b = []
                for h in range(2):
                    hc = slice(h * 128, (h + 1) * 128)
                    r = _dot(_pair(lsn[:, hc]), u_gt, NN)
                    a = jnp.exp(lsp[:, hc] + r[:, :128] + carry[h])
                    if mask2 is not None:
                        a = jnp.where(mask2[:, hc], a, 0.0)
                    carry[h] = carry[h] + r[:, 128:]
                    a_b.append(a.astype(BF16))
                acc = acc + _dot(jnp.concatenate(a_b, axis=1), _per_head(v_ref[rk, :], hms), NN)
                return carry[0], carry[1], acc

            zero = jnp.zeros((qt, 128), F32)
            state = (zero, zero, zero)
            for m in reversed(range(nsub)):
                state = step(i * nsub + m, state, tile_mask(m))

            def live(c):
                return jnp.logical_and(c[0] < i, jnp.maximum(jnp.max(c[1][0]), jnp.max(c[1][1])) > SB_DEAD)

            def blocks(c):
                jj, st = c
                for u in range(nsub):
                    st = step((i - jj) * nsub - 1 - u, st, None)
                return jj + 1, st

            _, state = lax.while_loop(live, blocks, (jnp.int32(0), state))
            o_ref[rq, :] = state[2]
            return 0

        lax.fori_loop(0, T // qt, qtile, 0)

    def col(off):
        return pl.BlockSpec((T, 128), lambda hp: (0, off + hp))

    steps = BRANCH_W // 128
    j = _job_args(job, 3, 1)
    res = pl.pallas_call(
        _hosting(body, job, 3, 1, 0, steps), name=name, grid=(steps,),
        in_specs=[col(cb), col(cb + 4), col(cb + 8)] + j["in_specs"], out_specs=[col(0)] + j["out_specs"],
        out_shape=[jax.ShapeDtypeStruct((T, BRANCH_W), F32)] + j["out_shape"],
        scratch_shapes=j["scratch"], input_output_aliases=j["aliases"],
        compiler_params=_params(("parallel",) if job is None else ("arbitrary",)),
    )(proj, proj, proj, *j["ins"])
    return res[0], list(res[1:])


def sb_bwd(proj, out, dout, *, name, job=None):
    T = proj.shape[0]
    qt = min(SB_QT, T)
    nsub = qt // CHUNK
    cb = C_SB // 128

    def body(q_ref, k_ref, v_ref, o_ref, do_ref, dq_ref, dk_ref, dv_ref, dkt_ref, dvt_ref):
        u_gt, u_ge, hms = _sb_consts()
        tile_mask = _sb_tile_mask(qt)
        tall_lane = lax.broadcasted_iota(jnp.int32, (qt, 128), 1)
        top = lax.broadcasted_iota(jnp.int32, (CHUNK, CHUNK), 0) < 64
        dkt_ref[...] = jnp.zeros_like(dkt_ref)
        dvt_ref[...] = jnp.zeros_like(dvt_ref)

        def qtile(i, _):
            rq = pl.ds(pl.multiple_of(i * qt, qt), qt)
            qs = q_ref[rq, :] * SB_SCALE
            qb, q_t = qs.astype(BF16), qs.T.astype(BF16)
            dov = do_ref[rq, :]
            dob, do_t = dov.astype(BF16), dov.T.astype(BF16)
            prod = dob.astype(F32) * o_ref[rq, :]
            total = [jnp.broadcast_to(jnp.sum(jnp.where(hm, prod, 0.0), axis=1, keepdims=True), (qt, 128))
                     for hm in (tall_lane < 64, tall_lane >= 64)]

            def step(j, state, mask2):
                c_l, c_w, dq = list(state[:2]), list(state[2:4]), state[4]
                rk = pl.ds(pl.multiple_of(j * CHUNK, CHUNK), CHUNK)
                kb2, vb2 = _per_head(k_ref[rk, :], hms), _per_head(v_ref[rk, :], hms)
                lsp, lsn = _sb_logits(qb, kb2, mask2)
                da = _dot(dob, vb2, NT)
                sp = jnp.exp(lsp)
                a_b, dz_b = [], []
                for h in range(2):
                    hc = slice(h * 128, (h + 1) * 128)
                    r = _dot(_pair(lsn[:, hc]), u_gt, NN)
                    a = jnp.exp(lsp[:, hc] + r[:, :128] + c_l[h])
                    if mask2 is not None:
                        a = jnp.where(mask2[:, hc], a, 0.0)
                    c_l[h] = c_l[h] + r[:, 128:]
                    a = a.astype(BF16)
                    w = a.astype(F32) * da[:, hc]
                    r = _dot(_pair(w), u_ge, NN)
                    later_w = r[:, :128] + c_w[h]
                    c_w[h] = c_w[h] + r[:, 128:]
                    dz = w * (1.0 - sp[:, hc]) - sp[:, hc] * (total[h] - later_w)
                    if mask2 is not None:
                        dz = jnp.where(mask2[:, hc], dz, 0.0)
                    a_b.append(a)
                    dz_b.append(dz.astype(BF16))
                a_b, dz_b = jnp.concatenate(a_b, axis=1), jnp.concatenate(dz_b, axis=1)
                dkt = _dot(q_t, dz_b, NN)
                dvt = _dot(do_t, a_b, NN)
                dkt_ref[j] += jnp.where(top, dkt[:, :128], dkt[:, 128:])
                dvt_ref[j] += jnp.where(top, dvt[:, :128], dvt[:, 128:])
                return c_l[0], c_l[1], c_w[0], c_w[1], dq + _dot(dz_b, kb2, NN)

            zero = jnp.zeros((qt, 128), F32)
            state = (zero,) * 5
            for m in reversed(range(nsub)):
                state = step(i * nsub + m, state, tile_mask(m))

            def live(c):
                return jnp.logical_and(c[0] < i, jnp.maximum(jnp.max(c[1][0]), jnp.max(c[1][1])) > SB_DEAD)

            def blocks(c):
                jj, st = c
                for u in range(nsub):
                    st = step((i - jj) * nsub - 1 - u, st, None)
                return jj + 1, st

            _, state = lax.while_loop(live, blocks, (jnp.int32(0), state))
            dq_ref[rq, :] = (state[4] * SB_SCALE).astype(BF16)
            return 0

        lax.fori_loop(0, T // qt, qtile, 0)

        def untranspose(jb, _):
            rk = pl.ds(pl.multiple_of(jb * CHUNK, CHUNK), CHUNK)
            dk_ref[rk, :] = dkt_ref[jb].T.astype(BF16)
            dv_ref[rk, :] = dvt_ref[jb].T.astype(BF16)
            return 0

        lax.fori_loop(0, T // CHUNK, untranspose, 0)

    def col(off):
        return pl.BlockSpec((T, 128), lambda hp: (0, off + hp))

    o16 = jax.ShapeDtypeStruct((T, BRANCH_W), BF16)
    steps = BRANCH_W // 128
    j = _job_args(job, 5, 3)
    acc = pltpu.VMEM((T // CHUNK, CHUNK, CHUNK), F32)
    res = pl.pallas_call(
        _hosting(body, job, 5, 3, 2, steps), name=name, grid=(steps,),
        in_specs=[col(cb), col(cb + 4), col(cb + 8), col(0), col(0)] + j["in_specs"],
        out_specs=[col(0), col(0), col(0)] + j["out_specs"], out_shape=[o16, o16, o16] + j["out_shape"],
        scratch_shapes=[acc, acc] + j["scratch"], input_output_aliases=j["aliases"],
        compiler_params=_params(("parallel",) if job is None else ("arbitrary",)),
    )(proj, proj, proj, out, dout, *j["ins"])
    return res[0], res[1], res[2], list(res[3:])


_G0 = math.sqrt(2.0 / math.pi)
_G1 = 0.044715


def _gelu(x):
    return 0.5 * x * (1.0 + jnp.tanh(_G0 * (x + _G1 * x * x * x)))


def _gelu_grad(x):
    t = jnp.tanh(_G0 * (x + _G1 * x * x * x))
    return 0.5 * (1.0 + t) + 0.5 * x * (1.0 - t * t) * (_G0 * (1.0 + 3.0 * _G1 * x * x))


def _tril():
    row, lane = _sb_masks()
    return row >= lane


def sgu_fwd(proj, ln_g, ln_b, w, bias, *, name):
    T = proj.shape[0]
    tb = min(512, T)
    G = BRANCH_W // 128

    def body(u_ref, v_ref, g_ref, b_ref, w_ref, bias_ref, o_ref):
        vv = _gelu(v_ref[...])
        xh, _ = _group_norm(vv)
        vn = (xh * g_ref[...] + b_ref[...]).astype(BF16)
        tril = _tril()
        for g in range(G):
            wg = jnp.where(tril, w_ref[g], 0.0).astype(BF16)
            gc = slice(g * 128, (g + 1) * 128)
            for c in range(tb // CHUNK):
                r = slice(c * CHUNK, (c + 1) * CHUNK)
                sv = _dot(wg, vn[r, gc], NN) + bias_ref[g]
                o_ref[r, gc] = _gelu(u_ref[r, gc]) * sv

    cu, cv = C_SGU // BRANCH_W, C_SGU // BRANCH_W + 1
    vec = pl.BlockSpec((1, BRANCH_W), lambda i: (0, 0))
    mat = pl.BlockSpec((G, CHUNK, CHUNK), lambda i: (0, 0, 0))
    return pl.pallas_call(
        body, name=name, grid=(T // tb,),
        in_specs=[pl.BlockSpec((tb, BRANCH_W), lambda i: (i, cu)), pl.BlockSpec((tb, BRANCH_W), lambda i: (i, cv)),
                  vec, vec, mat, mat],
        out_specs=pl.BlockSpec((tb, BRANCH_W), lambda i: (i, 0)),
        out_shape=jax.ShapeDtypeStruct((T, BRANCH_W), F32),
        compiler_params=_params(("parallel",)),
    )(proj, proj, ln_g, ln_b, w, bias)


def sgu_bwd(proj, ln_g, ln_b, w, bias, dout, *, name):
    T = proj.shape[0]
    tb = min(512, T)
    G = BRANCH_W // 128

    def body(u_ref, v_ref, g_ref, b_ref, w_ref, bias_ref, do_ref, dp_ref, dw_ref, dbias_ref, dg_ref, db_ref, dvn_ref):
        @pl.when(pl.program_id(0) == 0)
        def _():
            dw_ref[...] = jnp.zeros_like(dw_ref)
            dbias_ref[...] = jnp.zeros_like(dbias_ref)
            dg_ref[...] = jnp.zeros_like(dg_ref)
            db_ref[...] = jnp.zeros_like(db_ref)

        gv = v_ref[...]
        vv = _gelu(gv)
        xh, rstd = _group_norm(vv)
        vn = (xh * g_ref[...] + b_ref[...]).astype(BF16)
        tril = _tril()
        for g in range(G):
            wg = jnp.where(tril, w_ref[g], 0.0).astype(BF16)
            gc = slice(g * 128, (g + 1) * 128)
            for c in range(tb // CHUNK):
                r = slice(c * CHUNK, (c + 1) * CHUNK)
                vn_c = vn[r, gc]
                sv = _dot(wg, vn_c, NN) + bias_ref[g]
                gu = u_ref[r, gc]
                d_o = do_ref[r, gc]
                dp_ref[r, gc] = (d_o * sv * _gelu_grad(gu)).astype(BF16)
                dsv = d_o * _gelu(gu)
                dsv_b = dsv.astype(BF16)
                dvn_ref[r, gc] = _dot(wg, dsv_b, TN)
                dw_ref[g] += jnp.where(tril, _dot(dsv_b, vn_c, NT), 0.0)
                dbias_ref[g] += jnp.broadcast_to(jnp.sum(dsv, axis=1, keepdims=True), (CHUNK, CHUNK))
        dvn = dvn_ref[...]
        dg_ref[...] += jnp.sum(dvn * xh, axis=0, keepdims=True)
        db_ref[...] += jnp.sum(dvn, axis=0, keepdims=True)
        dxh = dvn * g_ref[...]
        m1 = jnp.mean(dxh, axis=-1, keepdims=True)
        m2 = jnp.mean(dxh * xh, axis=-1, keepdims=True)
        dp_ref[:, BRANCH_W:2 * BRANCH_W] = (rstd * (dxh - m1 - xh * m2) * _gelu_grad(gv)).astype(BF16)

    cu, cv = C_SGU // BRANCH_W, C_SGU // BRANCH_W + 1
    vec = pl.BlockSpec((1, BRANCH_W), lambda i: (0, 0))
    mat = pl.BlockSpec((G, CHUNK, CHUNK), lambda i: (0, 0, 0))
    blk = pl.BlockSpec((tb, BRANCH_W), lambda i: (i, 0))
    msh = jax.ShapeDtypeStruct((G, CHUNK, CHUNK), F32)
    vsh = jax.ShapeDtypeStruct((1, BRANCH_W), F32)
    return pl.pallas_call(
        body, name=name, grid=(T // tb,),
        in_specs=[pl.BlockSpec((tb, BRANCH_W), lambda i: (i, cu)), pl.BlockSpec((tb, BRANCH_W), lambda i: (i, cv)),
                  vec, vec, mat, mat, blk],
        out_specs=[pl.BlockSpec((tb, 2 * BRANCH_W), lambda i: (i, 0)), mat, mat, vec, vec],
        out_shape=[jax.ShapeDtypeStruct((T, 2 * BRANCH_W), BF16), msh, msh, vsh, vsh],
        scratch_shapes=[pltpu.VMEM((tb, BRANCH_W), F32)],
        compiler_params=_params(("arbitrary",)),
    )(proj, proj, ln_g, ln_b, w, bias, dout)


def merge_fwd(a1, a2, a3, p1, p2, p3, proj, *, name):
    T = a1.shape[0]
    tm, tn = min(1024, T), 512
    gb = C_GATE // tn

    def body(a1_ref, a2_ref, a3_ref, p1_ref, p2_ref, p3_ref, g1_ref, g2_ref, g3_ref, m_ref, r1_ref, r2_ref, r3_ref):
        m = None
        for a_ref, p_ref, g_ref, r_ref in ((a1_ref, p1_ref, g1_ref, r1_ref), (a2_ref, p2_ref, g2_ref, r2_ref),
                                           (a3_ref, p3_ref, g3_ref, r3_ref)):
            r = _dot(a_ref[...].astype(BF16), p_ref[...], NN)
            r_ref[...] = r
            t = jax.nn.sigmoid(g_ref[...]) * r
            m = t if m is None else m + t
        m_ref[...] = m.astype(m_ref.dtype)

    a_spec = pl.BlockSpec((tm, BRANCH_W), lambda i, j: (i, 0))
    p_spec = pl.BlockSpec((BRANCH_W, tn), lambda i, j: (0, j))
    o_spec = pl.BlockSpec((tm, tn), lambda i, j: (i, j))
    osh = jax.ShapeDtypeStruct((T, D_MODEL), F32)
    gates = [pl.BlockSpec((tm, tn), functools.partial(lambda i, j, o: (i, o + j), o=gb + 2 * n)) for n in range(3)]
    return pl.pallas_call(
        body, name=name, grid=(T // tm, D_MODEL // tn),
        in_specs=[a_spec, a_spec, a_spec, p_spec, p_spec, p_spec, *gates],
        out_specs=[o_spec] * 4, out_shape=[jax.ShapeDtypeStruct((T, D_MODEL), BF16)] + [osh] * 3,
        compiler_params=_params(("parallel", "parallel")),
    )(a1, a2, a3, p1, p2, p3, proj, proj, proj)


def merge_bwd(dm, r1, r2, r3, proj, *, name):
    T = dm.shape[0]
    tm, tn = min(512, T), 512
    gb = C_GATE // tn

    def body(dm_ref, r1_ref, r2_ref, r3_ref, g1_ref, g2_ref, g3_ref, dr1_ref, dr2_ref, dr3_ref, dg1_ref, dg2_ref, dg3_ref):
        d = dm_ref[...]
        for r_ref, g_ref, dr_ref, dg_ref in ((r1_ref, g1_ref, dr1_ref, dg1_ref), (r2_ref, g2_ref, dr2_ref, dg2_ref),
                                             (r3_ref, g3_ref, dr3_ref, dg3_ref)):
            s = jax.nn.sigmoid(g_ref[...])
            dr_ref[...] = (d * s).astype(BF16)
            dg_ref[...] = (d * r_ref[...] * (s * (1.0 - s))).astype(BF16)

    o_spec = pl.BlockSpec((tm, tn), lambda i, j: (i, j))
    osh = jax.ShapeDtypeStruct((T, D_MODEL), BF16)
    gates = [pl.BlockSpec((tm, tn), functools.partial(lambda i, j, o: (i, o + j), o=gb + 2 * n)) for n in range(3)]
    return pl.pallas_call(
        body, name=name, grid=(T // tm, D_MODEL // tn),
        in_specs=[o_spec] * 4 + gates, out_specs=[o_spec] * 6, out_shape=[osh] * 6,
        compiler_params=_params(("parallel", "parallel")),
    )(dm, r1, r2, r3, proj, proj, proj)


def _rows_call(fn, ins, out_dtypes, *, name, tr=256):
    first = ins[0][0] if isinstance(ins[0], tuple) else ins[0]
    R, C = first.shape[-2:]
    tr = min(tr, R)
    assert R % tr == 0, (name, R, tr)
    arrs, specs = [], []
    for x in ins:
        if isinstance(x, tuple):
            arrs.append(x[0])
            specs.append(pl.BlockSpec((None, tr, C), functools.partial(lambda i, n: (n, i, 0), n=x[1])))
        else:
            arrs.append(x)
            specs.append(pl.BlockSpec((tr, C), lambda i: (i, 0)))
    ni = len(arrs)

    def body(*refs):
        vals = fn(*[r[...] for r in refs[:ni]])
        for o_ref, v in zip(refs[ni:], vals):
            o_ref[...] = v.astype(o_ref.dtype)

    res = pl.pallas_call(
        body, name=name, grid=(R // tr,), in_specs=specs,
        out_specs=[pl.BlockSpec((tr, C), lambda i: (i, 0)) for _ in out_dtypes],
        out_shape=[jax.ShapeDtypeStruct((R, C), dt) for dt in out_dtypes],
        compiler_params=_params(("parallel",)),
    )(*arrs)
    return res


def _tile_rows(rows, cols):
    t = 256
    while t > 8 and (t * cols > 512 * 1024 or rows % t):
        t //= 2
    return t


def _rows_at(fn, pos, ins, outs, steps, *, name, aliases=None):
    read = [n for n, (_, s) in enumerate(ins) if s is not ANY]
    ni = len(ins)

    def body(pos_ref, *refs):
        vals = fn(*[refs[n][...] for n in read])
        for o_ref, v in zip(refs[ni:], vals):
            o_ref[...] = v.astype(o_ref.dtype)

    return pl.pallas_call(
        body, name=name,
        grid_spec=pltpu.PrefetchScalarGridSpec(num_scalar_prefetch=1, grid=(steps,), in_specs=[s for _, s in ins],
                                               out_specs=[s for _, s in outs]),
        out_shape=[sh for sh, _ in outs],
        input_output_aliases={1 + i: o for i, o in (aliases or {}).items()},
        compiler_params=_params(("parallel",)),
    )(pos, *[a for a, _ in ins])


def cast_into_whole(pos, w, l, axis, *, name):
    _, r, n = w.shape
    tr = _tile_rows(r, n)
    if axis == 1:
        shape, spec = (r, n * N_CHIPS), pl.BlockSpec((tr, n), lambda i, p: (i, p[3]))
    else:
        shape, spec = (r * N_CHIPS, n), pl.BlockSpec((tr, n), lambda i, p: (p[3] * (r // tr) + i, 0))
    return _rows_at(lambda a: (a,), pos, [(w, pl.BlockSpec((None, tr, n), lambda i, p: (l, i, 0)))],
                    [(jax.ShapeDtypeStruct(shape, BF16), spec)], r // tr, name=name)[0]


def pair_sum(pos, theirs, g32, axis, *, name):
    rows2, cols = theirs.shape
    h = rows2 // (N_CHIPS if axis == 0 else 1)
    tr = _tile_rows(h, cols)
    hb = h // tr
    if axis == 1:
        own = pl.BlockSpec((tr, cols), lambda i, p: (p[2] * hb + i, 0))
    else:
        own = pl.BlockSpec((tr, cols), lambda i, p: ((2 * (i // hb) + p[2]) * hb + i % hb, 0))
    row = pl.BlockSpec((tr, cols), lambda i, p: (i, 0))
    return _rows_at(lambda t, m: (m + t.astype(F32),) * 2, pos, [(theirs, row), (g32, own)],
                    [(jax.ShapeDtypeStruct((rows2, cols), F32), row), (jax.ShapeDtypeStruct((rows2, cols), BF16), row)],
                    rows2 // tr, name=name)


def chip_sum(pos, h32, recv, l, axis, whole, *, name):
    _, depth, h, n = recv.shape
    tr = _tile_rows(h, n)
    hb = h // tr
    if axis == 1:
        mine = pl.BlockSpec((tr, n), lambda i, p: (i, p[3]))
    else:
        mine = pl.BlockSpec((tr, n), lambda i, p: (p[3] * hb + i, 0))
    ins = [(h32, mine)] + [(recv, pl.BlockSpec((None, None, tr, n), functools.partial(lambda i, p, j: (j, l, i, 0), j=j)))
                           for j in range(3)]
    if whole is not None:
        ins.append((whole, ANY))
    return _rows_at(lambda o, a, b, c: (((o + a.astype(F32)) + b.astype(F32)) + c.astype(F32),), pos, ins,
                    [(jax.ShapeDtypeStruct((depth, 2, h, n), F32), pl.BlockSpec((None, None, tr, n), lambda i, p: (l, p[2], i, 0)))],
                    hb, name=name, aliases=None if whole is None else {4: 0})[0]


def _adamw(w, g, m, v):
    m2 = ADAM_B1 * m + (1.0 - ADAM_B1) * g
    v2 = ADAM_B2 * v + (1.0 - ADAM_B2) * (g * g)
    m_hat = m2 / (1.0 - ADAM_B1 ** ADAM_STEP)
    v_hat = v2 / (1.0 - ADAM_B2 ** ADAM_STEP)
    delta = -ADAM_LR * (m_hat / (jnp.sqrt(v_hat) + ADAM_EPS) + ADAM_WD * w)
    return delta, m2, v2


def _place():
    return lax.axis_index("x"), lax.axis_index("y"), lax.axis_index("c")


def _chip_peers(x, y, c):
    return [((1 - x, y, c), 2 * (1 - x) + y), ((x, 1 - y, c), 2 * x + 1 - y), ((1 - x, 1 - y, c), 2 * (1 - x) + 1 - y)]


def _shard_of(ref, axis, k, n):
    start = pl.multiple_of(k * n, 128)
    return ref.at[pl.ds(start, n), :] if axis == 0 else ref.at[:, pl.ds(start, n)]


ANY = pl.BlockSpec(memory_space=pl.ANY)


class CopyJob:
    def __init__(self, ins, out_shape, scratch, copies, aliases=None):
        self.ins, self.out_shape, self.scratch, self.copies = list(ins), list(out_shape), list(scratch), copies
        self.aliases = dict(aliases or {})

    def start(self, ins, outs, sems):
        local, remote, _, _ = self.copies(ins, outs, sems)
        for d in local + remote:
            d.start()

    def finish(self, ins, outs, sems):
        local, remote, arrivals, relays = self.copies(ins, outs, sems)
        for needs, sends, _ in relays:
            for d in needs:
                d.wait_recv()
            for d in sends:
                d.start()
        for d in arrivals + [d for _, _, arrives in relays for d in arrives]:
            d.wait_recv()
        for d in remote + [d for _, sends, _ in relays for d in sends]:
            d.wait_send()
        for d in local:
            d.wait()


def run_job(job, *, name):
    ni, no = len(job.ins), len(job.out_shape)

    def body(*refs):
        parts = refs[:ni], refs[ni:ni + no], refs[ni + no:]
        job.start(*parts)
        job.finish(*parts)

    return pl.pallas_call(
        body, name=name, in_specs=[ANY] * ni, out_specs=[ANY] * no, out_shape=job.out_shape,
        scratch_shapes=job.scratch, input_output_aliases=job.aliases,
    )(*job.ins)


def _job_args(job, n_in, n_out):
    if job is None:
        return dict(ins=[], in_specs=[], out_specs=[], out_shape=[], scratch=[], aliases={})
    return dict(ins=job.ins, in_specs=[ANY] * len(job.ins), out_specs=[ANY] * len(job.out_shape),
                out_shape=job.out_shape, scratch=job.scratch,
                aliases={n_in + i: n_out + o for i, o in job.aliases.items()})


def _hosting(body, job, n_in, n_out, n_scratch, grid):
    if job is None:
        return body
    ji, jo = len(job.ins), len(job.out_shape)
    grid = (grid,) if isinstance(grid, int) else tuple(grid)

    def at(ends):
        hit = None
        for ax, e in enumerate(ends):
            here = pl.program_id(ax) == e
            hit = here if hit is None else jnp.logical_and(hit, here)
        return hit

    def hosted(*refs):
        o = n_in + ji
        s = o + n_out + jo
        parts = refs[n_in:o], refs[o + n_out:s], refs[s + n_scratch:]

        @pl.when(at([0] * len(grid)))
        def _():
            job.start(*parts)

        body(*refs[:n_in], *refs[o:o + n_out], *refs[s:s + n_scratch])

        @pl.when(at([g - 1 for g in grid]))
        def _():
            job.finish(*parts)

    return hosted


def _job_sems(n_remote, n_local):
    return [pltpu.SemaphoreType.DMA((n_remote,)), pltpu.SemaphoreType.DMA((n_remote,)), pltpu.SemaphoreType.DMA((n_local,))]


def gather_job(shards, axes):
    na = len(shards)

    def copies(ins, outs, sems):
        send, recv, _ = sems
        x, y, c = _place()
        k = 2 * x + y
        remote, relays = [], []
        for a in range(na):
            r = outs[a].shape[0] // (N_CHIPS if axes[a] == 0 else 1)
            n = outs[a].shape[axes[a]] // N_CHIPS
            half = r // 2

            def part(kk, cc, a=a, n=n, half=half):
                rows = pl.ds(pl.multiple_of(cc * half + (kk * n if axes[a] == 0 else 0), 8), half)
                return outs[a].at[rows, :] if axes[a] == 0 else outs[a].at[rows, pl.ds(pl.multiple_of(kk * n, 128), n)]

            needs, passes, lands = [], [], []
            for j, (peer, kp) in enumerate(_chip_peers(x, y, c)):
                s = 6 * a + j
                remote.append(pltpu.make_async_remote_copy(part(k, c), part(k, c), send.at[s], recv.at[s],
                                                           device_id=peer, device_id_type=MESH))
                needs.append(pltpu.make_async_remote_copy(part(kp, c), part(kp, c), send.at[s], recv.at[s],
                                                          device_id=peer, device_id_type=MESH))
                passes.append(pltpu.make_async_remote_copy(part(kp, c), part(kp, c), send.at[s + 3], recv.at[s + 3],
                                                           device_id=(x, y, 1 - c), device_id_type=MESH))
                lands.append(pltpu.make_async_remote_copy(part(kp, 1 - c), part(kp, 1 - c), send.at[s + 3], recv.at[s + 3],
                                                          device_id=(x, y, 1 - c), device_id_type=MESH))
            relays.append((needs, passes, lands))
        return [], remote, [], relays

    out_shape = [jax.ShapeDtypeStruct(w.shape, BF16) for w in shards]
    return CopyJob(shards, out_shape, _job_sems(6 * na, 1), copies, {a: a for a in range(na)})


def scatter_job(layers, g16, axes, filled):
    na = len(axes)

    def shard_shape(a):
        r, c = g16[a].shape
        return (r // N_CHIPS, c) if axes[a] == 0 else (r, c // N_CHIPS)

    def copies(ins, outs, sems):
        send, recv_sems, _ = sems
        x, y, c = _place()
        remote = []
        for a in range(na):
            n = shard_shape(a)[axes[a]]
            for r, (peer, kp) in enumerate(_chip_peers(x, y, c)):
                remote.append(pltpu.make_async_remote_copy(_shard_of(ins[a], axes[a], kp, n), outs[a].at[r, layers[a]],
                                                           send.at[3 * a + r], recv_sems.at[3 * a + r],
                                                           device_id=peer, device_id_type=MESH))
        return [], remote, remote, []

    out_shape = [jax.ShapeDtypeStruct((3, DEPTH) + shard_shape(a), BF16) for a in range(na)]
    ins = list(g16)
    aliases = {}
    for a in range(na):
        if filled[a] is not None:
            aliases[len(ins)] = a
            ins.append(filled[a])
    return CopyJob(ins, out_shape, _job_sems(3 * na, 1), copies, aliases)


def pair_job(g16, axes):
    na = len(axes)
    pieces = [1 if ax == 1 else N_CHIPS for ax in axes]

    def copies(ins, outs, sems):
        send, recv, _ = sems
        x, y, c = _place()
        remote = []
        s = 0
        for a in range(na):
            rows = g16[a].shape[0] // (2 * pieces[a])
            for kk in range(pieces[a]):
                src = ins[a].at[pl.ds(pl.multiple_of((2 * kk + 1 - c) * rows, 8), rows), :]
                remote.append(pltpu.make_async_remote_copy(src, outs[a].at[pl.ds(kk * rows, rows), :], send.at[s], recv.at[s],
                                                           device_id=(x, y, 1 - c), device_id_type=MESH))
                s += 1
        return [], remote, remote, []

    out_shape = [jax.ShapeDtypeStruct((g.shape[0] // 2, g.shape[1]), BF16) for g in g16]
    return CopyJob(g16, out_shape, _job_sems(sum(pieces), 1), copies)


def join_job(shards):
    na = len(shards)

    def copies(ins, outs, sems):
        send, recv, _ = sems
        x, y, c = _place()
        remote = [pltpu.make_async_remote_copy(outs[a].at[:, c], outs[a].at[:, c], send.at[a], recv.at[a],
                                               device_id=(x, y, 1 - c), device_id_type=MESH) for a in range(na)]
        lands = [pltpu.make_async_remote_copy(outs[a].at[:, 1 - c], outs[a].at[:, 1 - c], send.at[a], recv.at[a],
                                              device_id=(x, y, 1 - c), device_id_type=MESH) for a in range(na)]
        return [], remote, lands, []

    out_shape = [jax.ShapeDtypeStruct(s.shape, F32) for s in shards]
    return CopyJob(shards, out_shape, _job_sems(na, 1), copies, {a: a for a in range(na)})


def small_job(p):
    def copies(ins, outs, sems):
        send, recv, loc = sems
        x, y, c = _place()
        me = 4 * x + 2 * y + c
        remote, lands = [], []
        for rel in range(1, 8):
            dx, dy, dc = rel >> 2, (rel >> 1) & 1, rel & 1
            peer = (1 - x if dx else x, 1 - y if dy else y, 1 - c if dc else c)
            who = 4 * peer[0] + 2 * peer[1] + peer[2]
            remote.append(pltpu.make_async_remote_copy(ins[0], outs[0].at[me], send.at[rel - 1], recv.at[rel - 1],
                                                       device_id=peer, device_id_type=MESH))
            lands.append(pltpu.make_async_remote_copy(ins[0], outs[0].at[who], send.at[rel - 1], recv.at[rel - 1],
                                                      device_id=peer, device_id_type=MESH))
        return [pltpu.make_async_copy(ins[0], outs[0].at[me], loc.at[0])], remote, lands, []

    return CopyJob([p], [jax.ShapeDtypeStruct((8,) + p.shape, F32)], _job_sems(7, 1), copies)


def small_sum(slots):
    def add(*terms):
        acc = terms[0]
        for t in terms[1:]:
            acc = acc + t
        return (acc,)

    return _rows_call(add, [(slots, d) for d in range(8)], [F32], name="small_sum", tr=8 * 47)[0]


BIG = ("w_in", "p_ret", "p_sb", "p_sgu", "w_out", "w_up", "w_down")
BIG_AXIS = {"w_in": 1, "p_ret": 1, "p_sb": 1, "p_sgu": 1, "w_out": 0, "w_up": 1, "w_down": 0}
SMALL = ("ret_gn_g", "ret_gn_b", "sgu_ln_g", "sgu_ln_b", "sgu_w", "sgu_b", "ln1_g", "ln1_b", "ln2_g", "ln2_b")


def layer_forward(l, x0, W, sm, rope, rconsts, hooks):
    n = f"l{l}_"
    jobs = hooks.fwd_jobs(l)
    proj = matmul(x0, W["w_in"], mode="nn", tm=2048, tn=640, tk=1024, name=n + "proj")
    retg, raw, states = ret_fwd(proj, *rope, rconsts, sm["ret_gn_g"], sm["ret_gn_b"], name=n + "ret_fwd")
    sb, job_out = sb_fwd(proj, name=n + "sb_fwd", job=jobs.get("sb"))
    if jobs.get("sb") is not None:
        hooks.done(jobs["sb"], job_out)
    sg = sgu_fwd(proj, sm["sgu_ln_g"], sm["sgu_ln_b"], sm["sgu_w"], sm["sgu_bias"], name=n + "sgu_fwd")
    merged, r1, r2, r3 = merge_fwd(retg, sb, sg, W["p_ret"], W["p_sb"], W["p_sgu"], proj, name=n + "merge_fwd")
    x1, xh1, rs1 = matmul_ln(merged, W["w_out"], x0, sm["ln1_g"], sm["ln1_b"], tk=1024, name=n + "out_ln1")
    h1 = matmul(x1, W["w_up"], mode="nn", tm=1024, tn=1024, tk=1024, name=n + "up")
    res = matmul_ln(h1, W["w_down"], x1, sm["ln2_g"], sm["ln2_b"], pro=_relu2, tk=1024, name=n + "down_ln2",
                    job=jobs.get("down"))
    if jobs.get("down") is not None:
        res, job_out = res
        hooks.done(jobs["down"], job_out)
    x2, xh2, rs2 = res
    saved = dict(x0=x0, proj=proj, retg=retg, raw=raw, states=states, sb=sb, sg=sg, merged=merged, r=(r1, r2, r3),
                 x1=x1, xh1=xh1, rs1=rs1, h1=h1, xh2=xh2, rs2=rs2)
    return x2, saved


def layer_backward(l, dx2, s, W, sm, rope, rconsts, hooks):
    n = f"l{l}_"
    two = ((F32, None), (BF16, None))
    gw, gs = {}, {}
    du2, du2h, gs["ln2_g"], gs["ln2_b"] = ln_bwd(dx2, s["xh2"], s["rs2"], sm["ln2_g"], name=n + "ln2_bwd")
    gw["w_down"] = matmul(s["h1"], du2h, mode="tn", tm=1024, tn=1024, tk=512, pro=_relu2, outs=two, name=n + "g_down")
    dh1 = matmul(du2h, W["w_down"], mode="nt", tm=1024, tn=1024, tk=1024, outs=((BF16, None),),
                 epi=lambda acc, h: (acc * (2.0 * jnp.maximum(h, 0.0)),), tiles=(s["h1"],), name=n + "d_h1")
    gw["w_up"] = matmul(s["x1"], dh1, mode="tn", tm=1024, tn=1024, tk=512, outs=two, name=n + "g_up")
    dx1 = matmul(dh1, W["w_up"], mode="nt", tm=1024, tn=1024, tk=1024,
                 epi=lambda acc, d: (acc + ALPHA * d,), tiles=(du2,), name=n + "d_x1")
    du1, du1h, gs["ln1_g"], gs["ln1_b"] = ln_bwd(dx1, s["xh1"], s["rs1"], sm["ln1_g"], name=n + "ln1_bwd")
    gw["w_out"] = matmul(s["merged"], du1h, mode="tn", tm=1024, tn=1024, tk=512, outs=two, name=n + "g_out")
    dmerged = matmul(du1h, W["w_out"], mode="nt", tm=1024, tn=1024, tk=1024, name=n + "d_merged")
    dr1, dr2, dr3, dg1, dg2, dg3 = merge_bwd(dmerged, *s["r"], s["proj"], name=n + "merge_bwd")
    d_branch = {}
    for nm, a, dr in (("p_ret", s["retg"], dr1), ("p_sb", s["sb"], dr2), ("p_sgu", s["sg"], dr3)):
        gw[nm] = matmul(a, dr, mode="tn", tm=512, tn=1024, tk=512, outs=two, name=n + "g_" + nm)
        d_branch[nm] = matmul(dr, W[nm], mode="nt", tm=1024, tn=512, tk=1024, name=n + "d_" + nm)
    job = hooks.pair(l, gw)
    dret, gs["ret_gn_g"], gs["ret_gn_b"], job_out = ret_bwd(s["proj"], *rope, rconsts, sm["ret_gn_g"], sm["ret_gn_b"],
                                                             s["raw"], s["states"], d_branch["p_ret"], name=n + "ret_bwd", job=job)
    if job is not None:
        hooks.done(job, job_out)
    job = hooks.scatter(l) if job is not None else None
    dsq, dsk, dsv, job_out = sb_bwd(s["proj"], s["sb"], d_branch["p_sb"], name=n + "sb_bwd", job=job)
    if job is not None:
        hooks.done(job, job_out)
    dsgu, gs["sgu_w"], dbias, gs["sgu_ln_g"], gs["sgu_ln_b"] = sgu_bwd(
        s["proj"], sm["sgu_ln_g"], sm["sgu_ln_b"], sm["sgu_w"], sm["sgu_bias"], d_branch["p_sgu"], name=n + "sgu_bwd")
    gs["sgu_b"] = dbias[:, :, 0]
    dproj = jnp.concatenate([dret, dsq, dsk, dsv, dsgu, dg1, dg2, dg3], axis=1)
    job = hooks.small(l, gs)
    gw["w_in"] = matmul(s["x0"], dproj, mode="tn", tm=1024, tn=1920, tk=512, outs=two, name=n + "g_in", job=job)
    if job is not None:
        gw["w_in"], job_out = gw["w_in"]
        hooks.done(job, job_out)
    job = hooks.tail(l, gw["w_in"])
    dx0 = matmul(dproj, W["w_in"], mode="nt", tm=1024, tn=1024, tk=1536,
                 epi=lambda acc, d: (acc + ALPHA * d,), tiles=(du1,), name=n + "d_x0", job=job)
    if job is not None:
        dx0, job_out = dx0
        hooks.done(job, job_out)
    return dx0, gw, gs


def local_step(x, target, small, plan):
    T = x.shape[0]
    rope = _rope_tables(T)
    rconsts = _ret_consts()
    sms = []
    for l in range(DEPTH):
        sm = {k: small[k][l][None, :] for k in SMALL if k not in ("sgu_w", "sgu_b")}
        sm["sgu_w"] = small["sgu_w"][l]
        sm["sgu_bias"] = jnp.broadcast_to(small["sgu_b"][l][:, :, None], (4, CHUNK, CHUNK))
        sms.append(sm)
    h, saved = x, []
    for l in range(DEPTH):
        h, s = layer_forward(l, h, plan.weights(l), sms[l], rope, rconsts, plan)
        saved.append(s)
    dy, sq = loss_head(h, target)
    gs = {k: [None] * DEPTH for k in SMALL}
    for l in reversed(range(DEPTH)):
        dy, gwl, gsl = layer_backward(l, dy, saved[l], plan.weights(l), sms[l], rope, rconsts, plan)
        plan.grads(l, gwl)
        for k in SMALL:
            gs[k][l] = gsl[k].reshape(small[k].shape[1:])
    return sq[0, 0], dy, {k: jnp.stack(v) for k, v in gs.items()}


EARLY_GRADS = ("p_ret", "p_sb", "p_sgu", "w_out", "w_up", "w_down")


class _StepPlan:
    def __init__(self, pos, shards16):
        self.pos = pos
        self.shards16 = shards16
        self.full = [dict() for _ in range(DEPTH)]
        self.gw = [None] * DEPTH
        self.bufs = {}
        self.sums = {}
        self.gs = [None] * DEPTH
        first = self._gather([(0, "w_in")])
        self.done(first, run_job(first, name="gather_first"))

    def weights(self, l):
        return self.full[l]

    def grads(self, l, gw):
        self.gw[l] = gw

    def _gather(self, items):
        job = gather_job([self.shards16[l][k] for l, k in items], [BIG_AXIS[k] for _, k in items])
        job.note = ("gather", items)
        return job

    def _pair(self, items):
        job = pair_job([g[1] for _, _, g in items], [BIG_AXIS[k] for _, k, _ in items])
        job.note = ("pair", items)
        return job

    def fwd_jobs(self, l):
        jobs = {"sb": self._gather([(l, k) for k in BIG[1:]])}
        if l + 1 < DEPTH:
            jobs["down"] = self._gather([(l + 1, "w_in")])
        return jobs

    def pair(self, l, ready):
        items = [(l, k, ready[k]) for k in EARLY_GRADS]
        if l + 1 < DEPTH:
            items.append((l + 1, "w_in", self.gw[l + 1]["w_in"]))
        return self._pair(items)

    def scatter(self, l):
        items, sums16 = self.summed
        job = scatter_job([l_ for l_, _, _ in items], sums16, [BIG_AXIS[k] for _, k, _ in items],
                          [self.bufs.get(k) for _, k, _ in items])
        job.note = ("scatter", items)
        return job

    def small(self, l, gs):
        self.gs[l] = {k: gs[k].reshape(-1) for k in SMALL}
        if l != 0:
            return None
        job = small_job(_pack_small({k: jnp.stack([self.gs[l_][k] for l_ in range(DEPTH)]) for k in SMALL}))
        job.note = ("small", [])
        return job

    def tail(self, l, g):
        if l != 0:
            return None
        last = self._pair([(0, "w_in", g)])
        self.done(last, run_job(last, name="pair_last"))
        return self.scatter(0)

    def done(self, job, outs):
        kind, items = job.note
        if kind == "small":
            self.small_slots = outs[0]
        if kind == "pair":
            sums16 = []
            for a, (l, k, g) in enumerate(items):
                self.sums[(l, k)], s16 = pair_sum(self.pos, outs[a], g[0], BIG_AXIS[k], name=f"pair_sum_{k}_{l}")
                sums16.append(s16)
            self.summed = (items, sums16)
        for a, item in enumerate(items):
            if kind == "gather":
                self.full[item[0]][item[1]] = outs[a]
            elif kind == "scatter":
                self.bufs[item[1]] = outs[a]

    def finish(self):
        return self.bufs, self.sums


def _flat2(a):
    return a.reshape(-1, a.shape[-1])


def _pack_small(d, pre=""):
    return jnp.concatenate([d[pre + k].reshape(-1) for k in SMALL]).reshape(-1, 128)


def kernel(x, w_in, ret_gn_g, ret_gn_b, sgu_ln_g, sgu_ln_b, sgu_w, sgu_b, p_ret, p_sb, p_sgu, w_out, ln1_g, ln1_b, w_up, w_down, ln2_g, ln2_b, loss_target, m_w_in, m_ret_gn_g, m_ret_gn_b, m_sgu_ln_g, m_sgu_ln_b, m_sgu_w, m_sgu_b, m_p_ret, m_p_sb, m_p_sgu, m_w_out, m_ln1_g, m_ln1_b, m_w_up, m_w_down, m_ln2_g, m_ln2_b, v_w_in, v_ret_gn_g, v_ret_gn_b, v_sgu_ln_g, v_sgu_ln_b, v_sgu_w, v_sgu_b, v_p_ret, v_p_sb, v_p_sgu, v_w_out, v_ln1_g, v_ln1_b, v_w_up, v_w_down, v_ln2_g, v_ln2_b):
    given = dict(locals())
    order = BIG[:1] + SMALL[:6] + BIG[1:5] + SMALL[6:8] + BIG[5:7] + SMALL[8:10]
    L = DEPTH

    px, py, pc = _place()
    pos = jnp.stack([px, py, pc, 2 * px + py]).astype(jnp.int32)

    shards16 = [{k: cast_into_whole(pos, given[k], l, BIG_AXIS[k], name=f"cast_{k}_{l}") for k in BIG} for l in range(L)]
    plan = _StepPlan(pos, shards16)
    sq, dx, gs = local_step(x[0], loss_target[0], {k: given[k] for k in SMALL}, plan)
    loss = 0.5 * lax.psum(sq, ("x", "y", "c"))

    bufs, sums = plan.finish()
    shards = []
    for k in BIG:
        whole = None
        for l in range(L):
            whole = chip_sum(pos, sums[(l, k)], bufs[k], l, BIG_AXIS[k], whole, name=f"chip_sum_{k}_{l}")
        shards.append(whole)
    joined = run_job(join_job(shards), name="join_halves")
    out = {}
    for a, k in enumerate(BIG):
        shp = given[k].shape
        res = _rows_call(lambda g_, w_, m_, v_: (g_,) + _adamw(w_, g_, m_, v_),
                         [joined[a].reshape(-1, shp[-1]), _flat2(given[k]), _flat2(given["m_" + k]), _flat2(given["v_" + k])],
                         [F32] * 4, name="adamw_" + k)
        out[k] = [r.reshape(shp) for r in res]

    pack = _pack_small
    res = _rows_call(lambda g_, w_, m_, v_: (g_,) + _adamw(w_, g_, m_, v_),
                     [small_sum(plan.small_slots), pack(given), pack(given, "m_"), pack(given, "v_")], [F32] * 4,
                     name="adamw_small", tr=8 * 47)
    off = 0
    for k in SMALL:
        sz = given[k].size
        out[k] = [r.reshape(-1)[off:off + sz].reshape(given[k].shape) for r in res]
        off += sz

    grads = [out[k][0] for k in order]
    deltas = [out[k][1] for k in order]
    new_m = [out[k][2] for k in order]
    new_v = [out[k][3] for k in order]
    return (loss, dx[None], *grads, *deltas, *new_m, *new_v)
```

```python
import functools
import math

import jax
import jax.numpy as jnp
from jax import lax
from jax.experimental import pallas as pl
from jax.experimental.pallas import tpu as pltpu

F32 = jnp.float32
BF16 = jnp.bfloat16

D_MODEL = 1024
SEQ = 4096
DEPTH = 2
CHUNK = 128
RET_HEADS = 4
BRANCH_W = 512
N_IN = 7680
D_FF = 4096
LN_EPS = 1e-5
ROPE_BASE = 10000.0
ALPHA = (2 * DEPTH) ** 0.25
RET_SCALE = 128 ** -0.5
SB_SCALE = 64 ** -0.5
C_RET, C_SB, C_SGU, C_GATE = 0, 2048, 3584, 4608

ADAM_LR, ADAM_B1, ADAM_B2, ADAM_EPS, ADAM_WD, ADAM_STEP = 0.001, 0.9, 0.999, 1e-08, 0.01, 10

N_CHIPS = 4
VMEM_LIMIT = 56 * 1024 * 1024
MESH = pl.DeviceIdType.MESH

NN = ((1,), (0,))
NT = ((1,), (1,))
TN = ((0,), (0,))


def _dot(a, b, dims):
    return lax.dot_general(a, b, (dims, ((), ())), preferred_element_type=F32)


def _params(sem):
    return pltpu.CompilerParams(dimension_semantics=sem, vmem_limit_bytes=VMEM_LIMIT)


def _relu2(h):
    r = jnp.maximum(h, 0.0)
    return r * r


def matmul(a, b, *, mode, tm, tn, tk, outs=((F32, None),), pro=None, epi=None, tiles=(), rows=(), name, job=None):
    if mode == "nn":
        (M, K), N = a.shape, b.shape[1]
    elif mode == "nt":
        (M, K), N = a.shape, b.shape[0]
    else:
        (K, M), N = a.shape, b.shape[1]
    tm, tn, tk = min(tm, M), min(tn, N), min(tk, K)
    assert M % tm == 0 and N % tn == 0 and K % tk == 0, (name, M, N, K, tm, tn, tk)
    if mode == "nn":
        a_spec = pl.BlockSpec((tm, tk), lambda i, j, k: (i, k))
        b_spec = pl.BlockSpec((tk, tn), lambda i, j, k: (k, j))
        dims = NN
    elif mode == "nt":
        a_spec = pl.BlockSpec((tm, tk), lambda i, j, k: (i, k))
        b_spec = pl.BlockSpec((tn, tk), lambda i, j, k: (j, k))
        dims = NT
    else:
        a_spec = pl.BlockSpec((tk, tm), lambda i, j, k: (k, i))
        b_spec = pl.BlockSpec((tk, tn), lambda i, j, k: (k, j))
        dims = TN
    nk = K // tk
    nt_, nr, no = len(tiles), len(rows), len(outs)

    def body(a_ref, b_ref, *rest):
        tile_refs = rest[:nt_]
        row_refs = rest[nt_:nt_ + nr]
        out_refs = rest[nt_ + nr:nt_ + nr + no]
        av = a_ref[...]
        if pro is not None:
            av = pro(av)
        p = _dot(av.astype(BF16), b_ref[...].astype(BF16), dims)

        def finish(acc):
            vals = (acc,) * no if epi is None else epi(acc, *[r[...] for r in tile_refs], *[r[...] for r in row_refs])
            for o_ref, v in zip(out_refs, vals):
                o_ref[...] = v.astype(o_ref.dtype)

        if nk == 1:
            finish(p)
        else:
            acc_ref = rest[-1]
            k = pl.program_id(2)

            @pl.when(k == 0)
            def _():
                acc_ref[...] = p

            @pl.when(k > 0)
            def _():
                acc_ref[...] += p

            @pl.when(k == nk - 1)
            def _():
                finish(acc_ref[...])

    out_shape, out_specs = [], []
    for dt, width in outs:
        if width is None:
            out_shape.append(jax.ShapeDtypeStruct((M, N), dt))
            out_specs.append(pl.BlockSpec((tm, tn), lambda i, j, k: (i, j)))
        else:
            assert N == tn
            out_shape.append(jax.ShapeDtypeStruct((M, width), dt))
            out_specs.append(pl.BlockSpec((tm, width), lambda i, j, k: (i, 0)))
    in_specs = [a_spec, b_spec]
    in_specs += [pl.BlockSpec((tm, tn), lambda i, j, k: (i, j)) for _ in tiles]
    in_specs += [pl.BlockSpec((1, tn), lambda i, j, k: (0, j)) for _ in rows]
    grid = (M // tm, N // tn, nk)
    scratch = [pltpu.VMEM((tm, tn), F32)] if nk > 1 else []
    j = _job_args(job, len(in_specs), no)
    res = pl.pallas_call(
        _hosting(body, job, len(in_specs), no, len(scratch), grid), name=name, grid=grid,
        in_specs=in_specs + j["in_specs"], out_specs=out_specs + j["out_specs"], out_shape=out_shape + j["out_shape"],
        scratch_shapes=scratch + j["scratch"], input_output_aliases=j["aliases"],
        compiler_params=_params(("parallel", "parallel", "arbitrary") if job is None else ("arbitrary",) * 3),
    )(a, b, *tiles, *rows, *j["ins"])
    mine = res[0] if no == 1 else list(res[:no])
    return mine if job is None else (mine, list(res[no:]))


def _ln_epi(acc, res, g, b):
    u = ALPHA * res + acc
    mu = jnp.mean(u, axis=-1, keepdims=True)
    xc = u - mu
    var = jnp.mean(xc * xc, axis=-1, keepdims=True)
    rstd = lax.rsqrt(var + LN_EPS)
    xhat = xc * rstd
    return xhat * g + b, xhat, jnp.broadcast_to(rstd, (u.shape[0], 128))


def matmul_ln(a, w, res, g, b, *, pro=None, tk, name, job=None):
    n = w.shape[1]
    return matmul(a, w, mode="nn", tm=1024, tn=n, tk=tk, pro=pro, epi=_ln_epi, tiles=(res,), rows=(g, b),
                  outs=((F32, None), (F32, None), (F32, 128)), name=name, job=job)


def ln_bwd(dy, xhat, rstd, g, *, name):
    T, D = dy.shape
    tm = min(512, T)

    def body(dy_ref, xh_ref, rs_ref, g_ref, du_ref, du16_ref, dg_ref, db_ref):
        dyv, xh = dy_ref[...], xh_ref[...]
        r = rs_ref[:, 0:1]
        dxh = dyv * g_ref[...]
        m1 = jnp.mean(dxh, axis=-1, keepdims=True)
        m2 = jnp.mean(dxh * xh, axis=-1, keepdims=True)
        du = r * (dxh - m1 - xh * m2)
        du_ref[...] = du
        du16_ref[...] = du.astype(BF16)

        @pl.when(pl.program_id(0) == 0)
        def _():
            dg_ref[...] = jnp.zeros_like(dg_ref)
            db_ref[...] = jnp.zeros_like(db_ref)

        dg_ref[...] += jnp.sum(dyv * xh, axis=0, keepdims=True)
        db_ref[...] += jnp.sum(dyv, axis=0, keepdims=True)

    row = pl.BlockSpec((tm, D), lambda i: (i, 0))
    vec = pl.BlockSpec((1, D), lambda i: (0, 0))
    return pl.pallas_call(
        body, name=name, grid=(T // tm,),
        in_specs=[row, row, pl.BlockSpec((tm, 128), lambda i: (i, 0)), vec],
        out_specs=[row, row, vec, vec],
        out_shape=[jax.ShapeDtypeStruct((T, D), F32), jax.ShapeDtypeStruct((T, D), BF16),
                   jax.ShapeDtypeStruct((1, D), F32), jax.ShapeDtypeStruct((1, D), F32)],
        compiler_params=_params(("arbitrary",)),
    )(dy, xhat, rstd, g)


def loss_head(y, target):
    T, D = y.shape
    tm = min(512, T)

    def body(y_ref, t_ref, dy_ref, s_ref):
        e = y_ref[...] - t_ref[...]
        dy_ref[...] = e * (1.0 / D)

        @pl.when(pl.program_id(0) == 0)
        def _():
            s_ref[...] = jnp.zeros_like(s_ref)

        s_ref[...] += jnp.sum(jnp.mean(e * e, axis=-1, keepdims=True))

    row = pl.BlockSpec((tm, D), lambda i: (i, 0))
    return pl.pallas_call(
        body, name="loss_head", grid=(T // tm,),
        in_specs=[row, row], out_specs=[row, pl.BlockSpec((8, 128), lambda i: (0, 0))],
        out_shape=[jax.ShapeDtypeStruct((T, D), F32), jax.ShapeDtypeStruct((8, 128), F32)],
        compiler_params=_params(("arbitrary",)),
    )(y, target)


def _rope_tables(T):
    half = 64
    inv_freq = ROPE_BASE ** (-jnp.arange(half, dtype=F32) / half)
    ang = jnp.arange(T, dtype=jnp.int32).astype(F32)[:, None] * inv_freq[None, :]
    cos, sin = jnp.cos(ang), jnp.sin(ang)
    return jnp.concatenate([cos, cos], axis=1), jnp.concatenate([-sin, sin], axis=1)


def _ret_consts():
    H = RET_HEADS
    log_g = jnp.log(1.0 - 2.0 ** (-5.0 - jnp.arange(H, dtype=F32)))
    idx = jnp.arange(CHUNK, dtype=F32)
    diff = idx[:, None] - idx[None, :]
    dmat = jnp.where(diff[None] >= 0, jnp.exp(log_g[:, None, None] * diff[None]), 0.0)
    kd = jnp.exp(log_g[:, None] * (CHUNK - 1 - idx)[None, :])
    qd = jnp.exp(log_g[:, None] * (idx + 1.0)[None, :])
    cd = jnp.exp(log_g * CHUNK)
    full = (H, CHUNK, CHUNK)
    return (dmat.astype(F32), jnp.broadcast_to(kd[:, :, None], full), jnp.broadcast_to(qd[:, :, None], full),
            jnp.broadcast_to(cd[:, None, None], full))


def _swap_halves(v):
    return pltpu.roll(v, 64, 1)


def _group_norm(o):
    mu = jnp.mean(o, axis=-1, keepdims=True)
    xc = o - mu
    var = jnp.mean(xc * xc, axis=-1, keepdims=True)
    rstd = lax.rsqrt(var + LN_EPS)
    return xc * rstd, rstd


def ret_fwd(proj, cosf, sinf, consts, gn_g, gn_b, *, name):
    T = proj.shape[0]
    tb = min(512, T)
    nch = tb // CHUNK
    H = RET_HEADS

    def body(p_ref, cos_ref, sin_ref, dm_ref, kd_ref, qd_ref, cd_ref, g_ref, b_ref, out_ref, raw_ref, st_ref, s_ref):
        @pl.when(pl.program_id(0) == 0)
        def _():
            s_ref[...] = jnp.zeros_like(s_ref)

        for c in range(nch):
            r = slice(c * CHUNK, (c + 1) * CHUNK)
            cs, sn = cos_ref[r, :], sin_ref[r, :]
            for h in range(H):
                hc = slice(h * 128, (h + 1) * 128)
                q = p_ref[r, h * 128:(h + 1) * 128]
                k = p_ref[r, 512 + h * 128:512 + (h + 1) * 128]
                v = p_ref[r, 1024 + h * 128:1024 + (h + 1) * 128]
                gt = p_ref[r, 1536 + h * 128:1536 + (h + 1) * 128]
                qr = q * cs + _swap_halves(q) * sn
                kr = (k * cs + _swap_halves(k) * sn) * RET_SCALE
                sprev = s_ref[h]
                st_ref[c, h] = sprev
                qb, kb, vb = qr.astype(BF16), kr.astype(BF16), v.astype(BF16)
                s = _dot(qb, kb, NT) * dm_ref[h]
                o = _dot(s.astype(BF16), vb, NN) + _dot((qr * qd_ref[h]).astype(BF16), sprev.astype(BF16), NN)
                s_ref[h] = sprev * cd_ref[h] + _dot((kr * kd_ref[h]).astype(BF16), vb, TN)
                raw_ref[r, hc] = o
                y, _ = _group_norm(o)
                out_ref[r, hc] = (gt * jax.nn.sigmoid(gt)) * (y * g_ref[:, hc] + b_ref[:, hc])

    cmat = pl.BlockSpec((H, CHUNK, CHUNK), lambda i: (0, 0, 0))
    vec = pl.BlockSpec((1, BRANCH_W), lambda i: (0, 0))
    rope = pl.BlockSpec((tb, 128), lambda i: (i, 0))
    blk = pl.BlockSpec((tb, BRANCH_W), lambda i: (i, 0))
    return pl.pallas_call(
        body, name=name, grid=(T // tb,),
        in_specs=[pl.BlockSpec((tb, 2048), lambda i: (i, 0)), rope, rope, cmat, cmat, cmat, cmat, vec, vec],
        out_specs=[blk, blk, pl.BlockSpec((nch, H, CHUNK, CHUNK), lambda i: (i, 0, 0, 0))],
        out_shape=[jax.ShapeDtypeStruct((T, BRANCH_W), F32), jax.ShapeDtypeStruct((T, BRANCH_W), F32),
                   jax.ShapeDtypeStruct((T // CHUNK, H, CHUNK, CHUNK), F32)],
        scratch_shapes=[pltpu.VMEM((H, CHUNK, CHUNK), F32)],
        compiler_params=_params(("arbitrary",)),
    )(proj, cosf, sinf, *consts, gn_g, gn_b)


def ret_bwd(proj, cosf, sinf, consts, gn_g, gn_b, raw, states, dout, *, name, job=None):
    T = proj.shape[0]
    tb = min(512, T)
    nch = tb // CHUNK
    nb = T // tb
    H = RET_HEADS

    def body(p_ref, cos_ref, sin_ref, dm_ref, kd_ref, qd_ref, cd_ref, g_ref, b_ref, raw_ref, st_ref, do_ref,
             dp_ref, dg_ref, db_ref, ds_ref):
        @pl.when(pl.program_id(0) == 0)
        def _():
            ds_ref[...] = jnp.zeros_like(ds_ref)
            dg_ref[...] = jnp.zeros_like(dg_ref)
            db_ref[...] = jnp.zeros_like(db_ref)

        for c in reversed(range(nch)):
            r = slice(c * CHUNK, (c + 1) * CHUNK)
            cs, sn = cos_ref[r, :], sin_ref[r, :]
            for h in range(H):
                hc = slice(h * 128, (h + 1) * 128)
                q = p_ref[r, h * 128:(h + 1) * 128]
                k = p_ref[r, 512 + h * 128:512 + (h + 1) * 128]
                v = p_ref[r, 1024 + h * 128:1024 + (h + 1) * 128]
                gt = p_ref[r, 1536 + h * 128:1536 + (h + 1) * 128]
                qr = q * cs + _swap_halves(q) * sn
                kr = (k * cs + _swap_halves(k) * sn) * RET_SCALE
                sprev = st_ref[c, h]
                gv = g_ref[:, hc]
                y, rstd = _group_norm(raw_ref[r, hc])
                d_out = do_ref[r, hc]
                sg = jax.nn.sigmoid(gt)
                d_gate = d_out * (y * gv + b_ref[:, hc]) * (sg * (1.0 + gt * (1.0 - sg)))
                d_aff = d_out * (gt * sg)
                dg_ref[:, hc] += jnp.sum(d_aff * y, axis=0, keepdims=True)
                db_ref[:, hc] += jnp.sum(d_aff, axis=0, keepdims=True)
                dxh = d_aff * gv
                m1 = jnp.mean(dxh, axis=-1, keepdims=True)
                m2 = jnp.mean(dxh * y, axis=-1, keepdims=True)
                d_o = (rstd * (dxh - m1 - y * m2)).astype(BF16)
                qb, kb, vb = qr.astype(BF16), kr.astype(BF16), v.astype(BF16)
                dm, kd, qd = dm_ref[h], kd_ref[h], qd_ref[h]
                p = (_dot(qb, kb, NT) * dm).astype(BF16)
                dp = (_dot(d_o, vb, NT) * dm).astype(BF16)
                dsn = ds_ref[h]
                dsb = dsn.astype(BF16)
                dq_r = _dot(dp, kb, NN) + _dot(d_o, sprev.astype(BF16), NT) * qd
                dk_r = (_dot(dp, qb, TN) + _dot(vb, dsb, NT) * kd) * RET_SCALE
                d_v = _dot(p, d_o, TN) + _dot((kr * kd).astype(BF16), dsb, NN)
                ds_ref[h] = dsn * cd_ref[h] + _dot((qr * qd).astype(BF16), d_o, TN)
                dp_ref[r, h * 128:(h + 1) * 128] = (dq_r * cs - _swap_halves(dq_r) * sn).astype(BF16)
                dp_ref[r, 512 + h * 128:512 + (h + 1) * 128] = (dk_r * cs - _swap_halves(dk_r) * sn).astype(BF16)
                dp_ref[r, 1024 + h * 128:1024 + (h + 1) * 128] = d_v.astype(BF16)
                dp_ref[r, 1536 + h * 128:1536 + (h + 1) * 128] = d_gate.astype(BF16)

    cmat = pl.BlockSpec((H, CHUNK, CHUNK), lambda i: (0, 0, 0))
    vec = pl.BlockSpec((1, BRANCH_W), lambda i: (0, 0))
    rope = pl.BlockSpec((tb, 128), lambda i: (nb - 1 - i, 0))
    blk = pl.BlockSpec((tb, BRANCH_W), lambda i: (nb - 1 - i, 0))
    wide = pl.BlockSpec((tb, 2048), lambda i: (nb - 1 - i, 0))
    j = _job_args(job, 12, 3)
    res = pl.pallas_call(
        _hosting(body, job, 12, 3, 1, nb), name=name, grid=(nb,),
        in_specs=[wide, rope, rope, cmat, cmat, cmat, cmat, vec, vec, blk,
                  pl.BlockSpec((nch, H, CHUNK, CHUNK), lambda i: (nb - 1 - i, 0, 0, 0)), blk] + j["in_specs"],
        out_specs=[wide, vec, vec] + j["out_specs"],
        out_shape=[jax.ShapeDtypeStruct((T, 2048), BF16), jax.ShapeDtypeStruct((1, BRANCH_W), F32),
                   jax.ShapeDtypeStruct((1, BRANCH_W), F32)] + j["out_shape"],
        scratch_shapes=[pltpu.VMEM((H, CHUNK, CHUNK), F32)] + j["scratch"], input_output_aliases=j["aliases"],
        compiler_params=_params(("arbitrary",)),
    )(proj, cosf, sinf, *consts, gn_g, gn_b, raw, states, dout, *j["ins"])
    return res[0], res[1], res[2], list(res[3:])


def _sb_masks():
    row = lax.broadcasted_iota(jnp.int32, (CHUNK, CHUNK), 0)
    lane = lax.broadcasted_iota(jnp.int32, (CHUNK, CHUNK), 1)
    return row, lane


SB_QT = 256
SB_DEAD = -105.0


def _pair(v):
    hi = v.astype(BF16)
    return jnp.concatenate([hi, (v - hi.astype(F32)).astype(BF16)], axis=1)


def _sb_consts():
    r = lax.broadcasted_iota(jnp.int32, (256, 256), 0) & 127
    c = lax.broadcasted_iota(jnp.int32, (256, 256), 1)
    ones = c >= 128
    lane = lax.broadcasted_iota(jnp.int32, (CHUNK, CHUNK), 1)
    return (ones | (r > c)).astype(BF16), (ones | (r >= c)).astype(BF16), (lane < 64, lane >= 64)


def _per_head(x, hms):
    return jnp.concatenate([jnp.where(hm, x, 0.0) for hm in hms], axis=0).astype(BF16)


def _sb_logits(qb, kb2, mask2):
    z = _dot(qb, kb2, NT)
    l1p = jnp.log(1.0 + jnp.exp(-jnp.abs(z)))
    lsp = jnp.minimum(z, 0.0) - l1p
    lsn = lsp - z
    if mask2 is not None:
        lsn = jnp.where(mask2, lsn, 0.0)
    return lsp, lsn


def _sb_tile_mask(qt):
    trow = lax.broadcasted_iota(jnp.int32, (qt, 256), 0)
    tlane = lax.broadcasted_iota(jnp.int32, (qt, 256), 1) & 127
    return lambda m: (tlane + m * CHUNK) < trow


def sb_fwd(proj, *, name, job=None):
    T = proj.shape[0]
    qt = min(SB_QT, T)
    nsub = qt // CHUNK
    cb = C_SB // 128

    def body(q_ref, k_ref, v_ref, o_ref):
        u_gt, _, hms = _sb_consts()
        tile_mask = _sb_tile_mask(qt)

        def qtile(i, _):
            rq = pl.ds(pl.multiple_of(i * qt, qt), qt)
            qb = (q_ref[rq, :] * SB_SCALE).astype(BF16)

            def group(js, masks, state):
                carry, acc = list(state[:2]), state[2]
                rows = [pl.ds(pl.multiple_of(j * CHUNK, CHUNK), CHUNK) for j in js]
                logits = [_sb_logits(qb, _per_head(k_ref[rk, :], hms), m) for rk, m in zip(rows, masks)]
                sums = [[_dot(_pair(lsn[:, h * 128:(h + 1) * 128]), u_gt, NN) for h in range(2)] for _, lsn in logits]
                weights = []
                for (lsp, _), r, m in zip(logits, sums, masks):
                    a_b = []
                    for h in range(2):
                        hc = slice(h * 128, (h + 1) * 128)
                        a = jnp.exp(lsp[:, hc] + r[h][:, :128] + carry[h])
                        if m is not None:
                            a = jnp.where(m[:, hc], a, 0.0)
                        carry[h] = carry[h] + r[h][:, 128:]
                        a_b.append(a.astype(BF16))
                    weights.append(jnp.concatenate(a_b, axis=1))
                for rk, a in zip(rows, weights):
                    acc = acc + _dot(a, _per_head(v_ref[rk, :], hms), NN)
                return carry[0], carry[1], acc

            zero = jnp.zeros((qt, 128), F32)
            diag = list(reversed(range(nsub)))
            state = group([i * nsub + m for m in diag], [tile_mask(m) for m in diag], (zero, zero, zero))

            def live(c):
                return jnp.logical_and(c[0] < i, jnp.maximum(jnp.max(c[1][0]), jnp.max(c[1][1])) > SB_DEAD)

            def blocks(c):
                jj, st = c
                return jj + 1, group([(i - jj) * nsub - 1 - u for u in range(nsub)], [None] * nsub, st)

            _, state = lax.while_loop(live, blocks, (jnp.int32(0), state))
            o_ref[rq, :] = state[2]
            return 0

        lax.fori_loop(0, T // qt, qtile, 0)

    def col(off):
        return pl.BlockSpec((T, 128), lambda hp: (0, off + hp))

    steps = BRANCH_W // 128
    j = _job_args(job, 3, 1)
    res = pl.pallas_call(
        _hosting(body, job, 3, 1, 0, steps), name=name, grid=(steps,),
        in_specs=[col(cb), col(cb + 4), col(cb + 8)] + j["in_specs"], out_specs=[col(0)] + j["out_specs"],
        out_shape=[jax.ShapeDtypeStruct((T, BRANCH_W), F32)] + j["out_shape"],
        scratch_shapes=j["scratch"], input_output_aliases=j["aliases"],
        compiler_params=_params(("parallel",) if job is None else ("arbitrary",)),
    )(proj, proj, proj, *j["ins"])
    return res[0], list(res[1:])


def sb_bwd(proj, out, dout, *, name, job=None):
    T = proj.shape[0]
    qt = min(SB_QT, T)
    nsub = qt // CHUNK
    cb = C_SB // 128

    def body(q_ref, k_ref, v_ref, o_ref, do_ref, dq_ref, dk_ref, dv_ref, dkt_ref, dvt_ref):
        u_gt, u_ge, hms = _sb_consts()
        tile_mask = _sb_tile_mask(qt)
        tall_lane = lax.broadcasted_iota(jnp.int32, (qt, 128), 1)
        top = lax.broadcasted_iota(jnp.int32, (CHUNK, CHUNK), 0) < 64
        dkt_ref[...] = jnp.zeros_like(dkt_ref)
        dvt_ref[...] = jnp.zeros_like(dvt_ref)

        def qtile(i, _):
            rq = pl.ds(pl.multiple_of(i * qt, qt), qt)
            qs = q_ref[rq, :] * SB_SCALE
            qb, q_t = qs.astype(BF16), qs.T.astype(BF16)
            dov = do_ref[rq, :]
            dob, do_t = dov.astype(BF16), dov.T.astype(BF16)
            prod = dob.astype(F32) * o_ref[rq, :]
            total = [jnp.broadcast_to(jnp.sum(jnp.where(hm, prod, 0.0), axis=1, keepdims=True), (qt, 128))
                     for hm in (tall_lane < 64, tall_lane >= 64)]

            def group(js, masks, state):
                c_l, c_w, dq = list(state[:2]), list(state[2:4]), state[4]
                heads = [slice(h * 128, (h + 1) * 128) for h in range(2)]
                rows = [pl.ds(pl.multiple_of(j * CHUNK, CHUNK), CHUNK) for j in js]
                kb2 = [_per_head(k_ref[rk, :], hms) for rk in rows]
                logits = [_sb_logits(qb, kb, m) for kb, m in zip(kb2, masks)]
                da = [_dot(dob, _per_head(v_ref[rk, :], hms), NT) for rk in rows]
                sums = [[_dot(_pair(lsn[:, hc]), u_gt, NN) for hc in heads] for _, lsn in logits]
                a_b, w_all = [], []
                for (lsp, _), r, d, m in zip(logits, sums, da, masks):
                    a_h, w_h = [], []
                    for h, hc in enumerate(heads):
                        a = jnp.exp(lsp[:, hc] + r[h][:, :128] + c_l[h])
                        if m is not None:
                            a = jnp.where(m[:, hc], a, 0.0)
                        c_l[h] = c_l[h] + r[h][:, 128:]
                        a = a.astype(BF16)
                        a_h.append(a)
                        w_h.append(a.astype(F32) * d[:, hc])
                    a_b.append(jnp.concatenate(a_h, axis=1))
                    w_all.append(w_h)
                sums_w = [[_dot(_pair(w), u_ge, NN) for w in w_h] for w_h in w_all]
                dz_b = []
                for (lsp, _), w_h, r, m in zip(logits, w_all, sums_w, masks):
                    sp = jnp.exp(lsp)
                    dz_h = []
                    for h, hc in enumerate(heads):
                        later_w = r[h][:, :128] + c_w[h]
                        c_w[h] = c_w[h] + r[h][:, 128:]
                        dz = w_h[h] * (1.0 - sp[:, hc]) - sp[:, hc] * (total[h] - later_w)
                        if m is not None:
                            dz = jnp.where(m[:, hc], dz, 0.0)
                        dz_h.append(dz.astype(BF16))
                    dz_b.append(jnp.concatenate(dz_h, axis=1))
                for j, kb, a, dz in zip(js, kb2, a_b, dz_b):
                    dkt = _dot(q_t, dz, NN)
                    dvt = _dot(do_t, a, NN)
                    dkt_ref[j] += jnp.where(top, dkt[:, :128], dkt[:, 128:])
                    dvt_ref[j] += jnp.where(top, dvt[:, :128], dvt[:, 128:])
                    dq = dq + _dot(dz, kb, NN)
                return c_l[0], c_l[1], c_w[0], c_w[1], dq

            zero = jnp.zeros((qt, 128), F32)
            diag = list(reversed(range(nsub)))
            state = group([i * nsub + m for m in diag], [tile_mask(m) for m in diag], (zero,) * 5)

            def live(c):
                return jnp.logical_and(c[0] < i, jnp.maximum(jnp.max(c[1][0]), jnp.max(c[1][1])) > SB_DEAD)

            def blocks(c):
                jj, st = c
                return jj + 1, group([(i - jj) * nsub - 1 - u for u in range(nsub)], [None] * nsub, st)

            _, state = lax.while_loop(live, blocks, (jnp.int32(0), state))
            dq_ref[rq, :] = (state[4] * SB_SCALE).astype(BF16)
            return 0

        lax.fori_loop(0, T // qt, qtile, 0)

        def untranspose(jb, _):
            rk = pl.ds(pl.multiple_of(jb * CHUNK, CHUNK), CHUNK)
            dk_ref[rk, :] = dkt_ref[jb].T.astype(BF16)
            dv_ref[rk, :] = dvt_ref[jb].T.astype(BF16)
            return 0

        lax.fori_loop(0, T // CHUNK, untranspose, 0)

    def col(off):
        return pl.BlockSpec((T, 128), lambda hp: (0, off + hp))

    o16 = jax.ShapeDtypeStruct((T, BRANCH_W), BF16)
    steps = BRANCH_W // 128
    j = _job_args(job, 5, 3)
    acc = pltpu.VMEM((T // CHUNK, CHUNK, CHUNK), F32)
    res = pl.pallas_call(
        _hosting(body, job, 5, 3, 2, steps), name=name, grid=(steps,),
        in_specs=[col(cb), col(cb + 4), col(cb + 8), col(0), col(0)] + j["in_specs"],
        out_specs=[col(0), col(0), col(0)] + j["out_specs"], out_shape=[o16, o16, o16] + j["out_shape"],
        scratch_shapes=[acc, acc] + j["scratch"], input_output_aliases=j["aliases"],
        compiler_params=_params(("parallel",) if job is None else ("arbitrary",)),
    )(proj, proj, proj, out, dout, *j["ins"])
    return res[0], res[1], res[2], list(res[3:])


_G0 = math.sqrt(2.0 / math.pi)
_G1 = 0.044715


def _gelu(x):
    return 0.5 * x * (1.0 + jnp.tanh(_G0 * (x + _G1 * x * x * x)))


def _gelu_grad(x):
    t = jnp.tanh(_G0 * (x + _G1 * x * x * x))
    return 0.5 * (1.0 + t) + 0.5 * x * (1.0 - t * t) * (_G0 * (1.0 + 3.0 * _G1 * x * x))


def _tril():
    row, lane = _sb_masks()
    return row >= lane


def sgu_fwd(proj, ln_g, ln_b, w, bias, *, name):
    T = proj.shape[0]
    tb = min(512, T)
    G = BRANCH_W // 128

    def body(u_ref, v_ref, g_ref, b_ref, w_ref, bias_ref, o_ref):
        vv = _gelu(v_ref[...])
        xh, _ = _group_norm(vv)
        vn = (xh * g_ref[...] + b_ref[...]).astype(BF16)
        tril = _tril()
        for g in range(G):
            wg = jnp.where(tril, w_ref[g], 0.0).astype(BF16)
            gc = slice(g * 128, (g + 1) * 128)
            for c in range(tb // CHUNK):
                r = slice(c * CHUNK, (c + 1) * CHUNK)
                sv = _dot(wg, vn[r, gc], NN) + bias_ref[g]
                o_ref[r, gc] = _gelu(u_ref[r, gc]) * sv

    cu, cv = C_SGU // BRANCH_W, C_SGU // BRANCH_W + 1
    vec = pl.BlockSpec((1, BRANCH_W), lambda i: (0, 0))
    mat = pl.BlockSpec((G, CHUNK, CHUNK), lambda i: (0, 0, 0))
    return pl.pallas_call(
        body, name=name, grid=(T // tb,),
        in_specs=[pl.BlockSpec((tb, BRANCH_W), lambda i: (i, cu)), pl.BlockSpec((tb, BRANCH_W), lambda i: (i, cv)),
                  vec, vec, mat, mat],
        out_specs=pl.BlockSpec((tb, BRANCH_W), lambda i: (i, 0)),
        out_shape=jax.ShapeDtypeStruct((T, BRANCH_W), F32),
        compiler_params=_params(("parallel",)),
    )(proj, proj, ln_g, ln_b, w, bias)


def sgu_bwd(proj, ln_g, ln_b, w, bias, dout, *, name):
    T = proj.shape[0]
    tb = min(512, T)
    G = BRANCH_W // 128

    def body(u_ref, v_ref, g_ref, b_ref, w_ref, bias_ref, do_ref, dp_ref, dw_ref, dbias_ref, dg_ref, db_ref, dvn_ref):
        @pl.when(pl.program_id(0) == 0)
        def _():
            dw_ref[...] = jnp.zeros_like(dw_ref)
            dbias_ref[...] = jnp.zeros_like(dbias_ref)
            dg_ref[...] = jnp.zeros_like(dg_ref)
            db_ref[...] = jnp.zeros_like(db_ref)

        gv = v_ref[...]
        vv = _gelu(gv)
        xh, rstd = _group_norm(vv)
        vn = (xh * g_ref[...] + b_ref[...]).astype(BF16)
        tril = _tril()
        for g in range(G):
            wg = jnp.where(tril, w_ref[g], 0.0).astype(BF16)
            gc = slice(g * 128, (g + 1) * 128)
            for c in range(tb // CHUNK):
                r = slice(c * CHUNK, (c + 1) * CHUNK)
                vn_c = vn[r, gc]
                sv = _dot(wg, vn_c, NN) + bias_ref[g]
                gu = u_ref[r, gc]
                d_o = do_ref[r, gc]
                dp_ref[r, gc] = (d_o * sv * _gelu_grad(gu)).astype(BF16)
                dsv = d_o * _gelu(gu)
                dsv_b = dsv.astype(BF16)
                dvn_ref[r, gc] = _dot(wg, dsv_b, TN)
                dw_ref[g] += jnp.where(tril, _dot(dsv_b, vn_c, NT), 0.0)
                dbias_ref[g] += jnp.broadcast_to(jnp.sum(dsv, axis=1, keepdims=True), (CHUNK, CHUNK))
        dvn = dvn_ref[...]
        dg_ref[...] += jnp.sum(dvn * xh, axis=0, keepdims=True)
        db_ref[...] += jnp.sum(dvn, axis=0, keepdims=True)
        dxh = dvn * g_ref[...]
        m1 = jnp.mean(dxh, axis=-1, keepdims=True)
        m2 = jnp.mean(dxh * xh, axis=-1, keepdims=True)
        dp_ref[:, BRANCH_W:2 * BRANCH_W] = (rstd * (dxh - m1 - xh * m2) * _gelu_grad(gv)).astype(BF16)

    cu, cv = C_SGU // BRANCH_W, C_SGU // BRANCH_W + 1
    vec = pl.BlockSpec((1, BRANCH_W), lambda i: (0, 0))
    mat = pl.BlockSpec((G, CHUNK, CHUNK), lambda i: (0, 0, 0))
    blk = pl.BlockSpec((tb, BRANCH_W), lambda i: (i, 0))
    msh = jax.ShapeDtypeStruct((G, CHUNK, CHUNK), F32)
    vsh = jax.ShapeDtypeStruct((1, BRANCH_W), F32)
    return pl.pallas_call(
        body, name=name, grid=(T // tb,),
        in_specs=[pl.BlockSpec((tb, BRANCH_W), lambda i: (i, cu)), pl.BlockSpec((tb, BRANCH_W), lambda i: (i, cv)),
                  vec, vec, mat, mat, blk],
        out_specs=[pl.BlockSpec((tb, 2 * BRANCH_W), lambda i: (i, 0)), mat, mat, vec, vec],
        out_shape=[jax.ShapeDtypeStruct((T, 2 * BRANCH_W), BF16), msh, msh, vsh, vsh],
        scratch_shapes=[pltpu.VMEM((tb, BRANCH_W), F32)],
        compiler_params=_params(("arbitrary",)),
    )(proj, proj, ln_g, ln_b, w, bias, dout)


def merge_fwd(a1, a2, a3, p1, p2, p3, proj, *, name):
    T = a1.shape[0]
    tm, tn = min(1024, T), 512
    gb = C_GATE // tn

    def body(a1_ref, a2_ref, a3_ref, p1_ref, p2_ref, p3_ref, g1_ref, g2_ref, g3_ref, m_ref, r1_ref, r2_ref, r3_ref):
        m = None
        for a_ref, p_ref, g_ref, r_ref in ((a1_ref, p1_ref, g1_ref, r1_ref), (a2_ref, p2_ref, g2_ref, r2_ref),
                                           (a3_ref, p3_ref, g3_ref, r3_ref)):
            r = _dot(a_ref[...].astype(BF16), p_ref[...], NN)
            r_ref[...] = r
            t = jax.nn.sigmoid(g_ref[...]) * r
            m = t if m is None else m + t
        m_ref[...] = m.astype(m_ref.dtype)

    a_spec = pl.BlockSpec((tm, BRANCH_W), lambda i, j: (i, 0))
    p_spec = pl.BlockSpec((BRANCH_W, tn), lambda i, j: (0, j))
    o_spec = pl.BlockSpec((tm, tn), lambda i, j: (i, j))
    osh = jax.ShapeDtypeStruct((T, D_MODEL), F32)
    gates = [pl.BlockSpec((tm, tn), functools.partial(lambda i, j, o: (i, o + j), o=gb + 2 * n)) for n in range(3)]
    return pl.pallas_call(
        body, name=name, grid=(T // tm, D_MODEL // tn),
        in_specs=[a_spec, a_spec, a_spec, p_spec, p_spec, p_spec, *gates],
        out_specs=[o_spec] * 4, out_shape=[jax.ShapeDtypeStruct((T, D_MODEL), BF16)] + [osh] * 3,
        compiler_params=_params(("parallel", "parallel")),
    )(a1, a2, a3, p1, p2, p3, proj, proj, proj)


def merge_bwd(dm, r1, r2, r3, proj, *, name):
    T = dm.shape[0]
    tm, tn = min(512, T), 512
    gb = C_GATE // tn

    def body(dm_ref, r1_ref, r2_ref, r3_ref, g1_ref, g2_ref, g3_ref, dr1_ref, dr2_ref, dr3_ref, dg1_ref, dg2_ref, dg3_ref):
        d = dm_ref[...]
        for r_ref, g_ref, dr_ref, dg_ref in ((r1_ref, g1_ref, dr1_ref, dg1_ref), (r2_ref, g2_ref, dr2_ref, dg2_ref),
                                             (r3_ref, g3_ref, dr3_ref, dg3_ref)):
            s = jax.nn.sigmoid(g_ref[...])
            dr_ref[...] = (d * s).astype(BF16)
            dg_ref[...] = (d * r_ref[...] * (s * (1.0 - s))).astype(BF16)

    o_spec = pl.BlockSpec((tm, tn), lambda i, j: (i, j))
    osh = jax.ShapeDtypeStruct((T, D_MODEL), BF16)
    gates = [pl.BlockSpec((tm, tn), functools.partial(lambda i, j, o: (i, o + j), o=gb + 2 * n)) for n in range(3)]
    return pl.pallas_call(
        body, name=name, grid=(T // tm, D_MODEL // tn),
        in_specs=[o_spec] * 4 + gates, out_specs=[o_spec] * 6, out_shape=[osh] * 6,
        compiler_params=_params(("parallel", "parallel")),
    )(dm, r1, r2, r3, proj, proj, proj)


def _rows_call(fn, ins, out_dtypes, *, name, tr=256):
    first = ins[0][0] if isinstance(ins[0], tuple) else ins[0]
    R, C = first.shape[-2:]
    tr = min(tr, R)
    assert R % tr == 0, (name, R, tr)
    arrs, specs = [], []
    for x in ins:
        if isinstance(x, tuple):
            arrs.append(x[0])
            specs.append(pl.BlockSpec((None, tr, C), functools.partial(lambda i, n: (n, i, 0), n=x[1])))
        else:
            arrs.append(x)
            specs.append(pl.BlockSpec((tr, C), lambda i: (i, 0)))
    ni = len(arrs)

    def body(*refs):
        vals = fn(*[r[...] for r in refs[:ni]])
        for o_ref, v in zip(refs[ni:], vals):
            o_ref[...] = v.astype(o_ref.dtype)

    res = pl.pallas_call(
        body, name=name, grid=(R // tr,), in_specs=specs,
        out_specs=[pl.BlockSpec((tr, C), lambda i: (i, 0)) for _ in out_dtypes],
        out_shape=[jax.ShapeDtypeStruct((R, C), dt) for dt in out_dtypes],
        compiler_params=_params(("parallel",)),
    )(*arrs)
    return res


def _tile_rows(rows, cols):
    t = 256
    while t > 8 and (t * cols > 512 * 1024 or rows % t):
        t //= 2
    return t


def _rows_at(fn, pos, ins, outs, steps, *, name, aliases=None):
    read = [n for n, (_, s) in enumerate(ins) if s is not ANY]
    ni = len(ins)

    def body(pos_ref, *refs):
        vals = fn(*[refs[n][...] for n in read])
        for o_ref, v in zip(refs[ni:], vals):
            o_ref[...] = v.astype(o_ref.dtype)

    return pl.pallas_call(
        body, name=name,
        grid_spec=pltpu.PrefetchScalarGridSpec(num_scalar_prefetch=1, grid=(steps,), in_specs=[s for _, s in ins],
                                               out_specs=[s for _, s in outs]),
        out_shape=[sh for sh, _ in outs],
        input_output_aliases={1 + i: o for i, o in (aliases or {}).items()},
        compiler_params=_params(("parallel",)),
    )(pos, *[a for a, _ in ins])


def cast_into_whole(pos, w, l, axis, *, name):
    _, r, n = w.shape
    tr = _tile_rows(r, n)
    if axis == 1:
        shape, spec = (r, n * N_CHIPS), pl.BlockSpec((tr, n), lambda i, p: (i, p[3]))
    else:
        shape, spec = (r * N_CHIPS, n), pl.BlockSpec((tr, n), lambda i, p: (p[3] * (r // tr) + i, 0))
    return _rows_at(lambda a: (a,), pos, [(w, pl.BlockSpec((None, tr, n), lambda i, p: (l, i, 0)))],
                    [(jax.ShapeDtypeStruct(shape, BF16), spec)], r // tr, name=name)[0]


def pair_sum(pos, theirs, g32, axis, *, name):
    rows2, cols = theirs.shape
    h = rows2 // (N_CHIPS if axis == 0 else 1)
    tr = _tile_rows(h, cols)
    hb = h // tr
    if axis == 1:
        own = pl.BlockSpec((tr, cols), lambda i, p: (p[2] * hb + i, 0))
    else:
        own = pl.BlockSpec((tr, cols), lambda i, p: ((2 * (i // hb) + p[2]) * hb + i % hb, 0))
    row = pl.BlockSpec((tr, cols), lambda i, p: (i, 0))
    return _rows_at(lambda t, m: (m + t.astype(F32),) * 2, pos, [(theirs, row), (g32, own)],
                    [(jax.ShapeDtypeStruct((rows2, cols), F32), row), (jax.ShapeDtypeStruct((rows2, cols), BF16), row)],
                    rows2 // tr, name=name)


def chip_sum(pos, h32, recv, l, axis, whole, *, name):
    _, depth, h, n = recv.shape
    tr = _tile_rows(h, n)
    hb = h // tr
    if axis == 1:
        mine = pl.BlockSpec((tr, n), lambda i, p: (i, p[3]))
    else:
        mine = pl.BlockSpec((tr, n), lambda i, p: (p[3] * hb + i, 0))
    ins = [(h32, mine)] + [(recv, pl.BlockSpec((None, None, tr, n), functools.partial(lambda i, p, j: (j, l, i, 0), j=j)))
                           for j in range(3)]
    if whole is not None:
        ins.append((whole, ANY))
    return _rows_at(lambda o, a, b, c: (((o + a.astype(F32)) + b.astype(F32)) + c.astype(F32),), pos, ins,
                    [(jax.ShapeDtypeStruct((depth, 2, h, n), F32), pl.BlockSpec((None, None, tr, n), lambda i, p: (l, p[2], i, 0)))],
                    hb, name=name, aliases=None if whole is None else {4: 0})[0]


def _adamw(w, g, m, v):
    m2 = ADAM_B1 * m + (1.0 - ADAM_B1) * g
    v2 = ADAM_B2 * v + (1.0 - ADAM_B2) * (g * g)
    m_hat = m2 / (1.0 - ADAM_B1 ** ADAM_STEP)
    v_hat = v2 / (1.0 - ADAM_B2 ** ADAM_STEP)
    delta = -ADAM_LR * (m_hat / (jnp.sqrt(v_hat) + ADAM_EPS) + ADAM_WD * w)
    return delta, m2, v2


def _place():
    return lax.axis_index("x"), lax.axis_index("y"), lax.axis_index("c")


def _chip_peers(x, y, c):
    return [((1 - x, y, c), 2 * (1 - x) + y), ((x, 1 - y, c), 2 * x + 1 - y), ((1 - x, 1 - y, c), 2 * (1 - x) + 1 - y)]


def _shard_of(ref, axis, k, n):
    start = pl.multiple_of(k * n, 128)
    return ref.at[pl.ds(start, n), :] if axis == 0 else ref.at[:, pl.ds(start, n)]


ANY = pl.BlockSpec(memory_space=pl.ANY)


class CopyJob:
    def __init__(self, ins, out_shape, scratch, copies, aliases=None):
        self.ins, self.out_shape, self.scratch, self.copies = list(ins), list(out_shape), list(scratch), copies
        self.aliases = dict(aliases or {})

    def start(self, ins, outs, sems):
        local, remote, _, _ = self.copies(ins, outs, sems)
        for d in local + remote:
            d.start()

    def finish(self, ins, outs, sems):
        local, remote, arrivals, relays = self.copies(ins, outs, sems)
        for needs, sends, _ in relays:
            for d in needs:
                d.wait_recv()
            for d in sends:
                d.start()
        for d in arrivals + [d for _, _, arrives in relays for d in arrives]:
            d.wait_recv()
        for d in remote + [d for _, sends, _ in relays for d in sends]:
            d.wait_send()
        for d in local:
            d.wait()


def run_job(job, *, name):
    ni, no = len(job.ins), len(job.out_shape)

    def body(*refs):
        parts = refs[:ni], refs[ni:ni + no], refs[ni + no:]
        job.start(*parts)
        job.finish(*parts)

    return pl.pallas_call(
        body, name=name, in_specs=[ANY] * ni, out_specs=[ANY] * no, out_shape=job.out_shape,
        scratch_shapes=job.scratch, input_output_aliases=job.aliases,
    )(*job.ins)


def _job_args(job, n_in, n_out):
    if job is None:
        return dict(ins=[], in_specs=[], out_specs=[], out_shape=[], scratch=[], aliases={})
    return dict(ins=job.ins, in_specs=[ANY] * len(job.ins), out_specs=[ANY] * len(job.out_shape),
                out_shape=job.out_shape, scratch=job.scratch,
                aliases={n_in + i: n_out + o for i, o in job.aliases.items()})


def _hosting(body, job, n_in, n_out, n_scratch, grid):
    if job is None:
        return body
    ji, jo = len(job.ins), len(job.out_shape)
    grid = (grid,) if isinstance(grid, int) else tuple(grid)

    def at(ends):
        hit = None
        for ax, e in enumerate(ends):
            here = pl.program_id(ax) == e
            hit = here if hit is None else jnp.logical_and(hit, here)
        return hit

    def hosted(*refs):
        o = n_in + ji
        s = o + n_out + jo
        parts = refs[n_in:o], refs[o + n_out:s], refs[s + n_scratch:]

        @pl.when(at([0] * len(grid)))
        def _():
            job.start(*parts)

        body(*refs[:n_in], *refs[o:o + n_out], *refs[s:s + n_scratch])

        @pl.when(at([g - 1 for g in grid]))
        def _():
            job.finish(*parts)

    return hosted


def _job_sems(n_remote, n_local):
    return [pltpu.SemaphoreType.DMA((n_remote,)), pltpu.SemaphoreType.DMA((n_remote,)), pltpu.SemaphoreType.DMA((n_local,))]


def gather_job(shards, axes, chips=(0, 1, 2)):
    na = len(shards)

    def copies(ins, outs, sems):
        send, recv, _ = sems
        x, y, c = _place()
        k = 2 * x + y
        remote, relays = [], []
        for a in range(na):
            r = outs[a].shape[0] // (N_CHIPS if axes[a] == 0 else 1)
            n = outs[a].shape[axes[a]] // N_CHIPS
            half = r // 2

            def part(kk, cc, a=a, n=n, half=half):
                rows = pl.ds(pl.multiple_of(cc * half + (kk * n if axes[a] == 0 else 0), 8), half)
                return outs[a].at[rows, :] if axes[a] == 0 else outs[a].at[rows, pl.ds(pl.multiple_of(kk * n, 128), n)]

            needs, passes, lands = [], [], []
            for j, (peer, kp) in enumerate(_chip_peers(x, y, c)):
                if j not in chips:
                    continue
                s = 6 * a + j
                remote.append(pltpu.make_async_remote_copy(part(k, c), part(k, c), send.at[s], recv.at[s],
                                                           device_id=peer, device_id_type=MESH))
                needs.append(pltpu.make_async_remote_copy(part(kp, c), part(kp, c), send.at[s], recv.at[s],
                                                          device_id=peer, device_id_type=MESH))
                passes.append(pltpu.make_async_remote_copy(part(kp, c), part(kp, c), send.at[s + 3], recv.at[s + 3],
                                                           device_id=(x, y, 1 - c), device_id_type=MESH))
                lands.append(pltpu.make_async_remote_copy(part(kp, 1 - c), part(kp, 1 - c), send.at[s + 3], recv.at[s + 3],
                                                          device_id=(x, y, 1 - c), device_id_type=MESH))
            relays.append((needs, passes, lands))
        return [], remote, [], relays

    out_shape = [jax.ShapeDtypeStruct(w.shape, BF16) for w in shards]
    return CopyJob(shards, out_shape, _job_sems(6 * na, 1), copies, {a: a for a in range(na)})


def scatter_job(layers, g16, axes, filled):
    na = len(axes)

    def shard_shape(a):
        r, c = g16[a].shape
        return (r // N_CHIPS, c) if axes[a] == 0 else (r, c // N_CHIPS)

    def copies(ins, outs, sems):
        send, recv_sems, _ = sems
        x, y, c = _place()
        remote = []
        for a in range(na):
            n = shard_shape(a)[axes[a]]
            for r, (peer, kp) in enumerate(_chip_peers(x, y, c)):
                remote.append(pltpu.make_async_remote_copy(_shard_of(ins[a], axes[a], kp, n), outs[a].at[r, layers[a]],
                                                           send.at[3 * a + r], recv_sems.at[3 * a + r],
                                                           device_id=peer, device_id_type=MESH))
        return [], remote, remote, []

    out_shape = [jax.ShapeDtypeStruct((3, DEPTH) + shard_shape(a), BF16) for a in range(na)]
    ins = list(g16)
    aliases = {}
    for a in range(na):
        if filled[a] is not None:
            aliases[len(ins)] = a
            ins.append(filled[a])
    return CopyJob(ins, out_shape, _job_sems(3 * na, 1), copies, aliases)


def pair_job(g16, axes):
    na = len(axes)
    pieces = [1 if ax == 1 else N_CHIPS for ax in axes]

    def copies(ins, outs, sems):
        send, recv, _ = sems
        x, y, c = _place()
        remote = []
        s = 0
        for a in range(na):
            rows = g16[a].shape[0] // (2 * pieces[a])
            for kk in range(pieces[a]):
                src = ins[a].at[pl.ds(pl.multiple_of((2 * kk + 1 - c) * rows, 8), rows), :]
                remote.append(pltpu.make_async_remote_copy(src, outs[a].at[pl.ds(kk * rows, rows), :], send.at[s], recv.at[s],
                                                           device_id=(x, y, 1 - c), device_id_type=MESH))
                s += 1
        return [], remote, remote, []

    out_shape = [jax.ShapeDtypeStruct((g.shape[0] // 2, g.shape[1]), BF16) for g in g16]
    return CopyJob(g16, out_shape, _job_sems(sum(pieces), 1), copies)


def join_job(shards):
    na = len(shards)

    def copies(ins, outs, sems):
        send, recv, _ = sems
        x, y, c = _place()
        remote = [pltpu.make_async_remote_copy(outs[a].at[:, c], outs[a].at[:, c], send.at[a], recv.at[a],
                                               device_id=(x, y, 1 - c), device_id_type=MESH) for a in range(na)]
        lands = [pltpu.make_async_remote_copy(outs[a].at[:, 1 - c], outs[a].at[:, 1 - c], send.at[a], recv.at[a],
                                              device_id=(x, y, 1 - c), device_id_type=MESH) for a in range(na)]
        return [], remote, lands, []

    out_shape = [jax.ShapeDtypeStruct(s.shape, F32) for s in shards]
    return CopyJob(shards, out_shape, _job_sems(na, 1), copies, {a: a for a in range(na)})


def small_job(p):
    def copies(ins, outs, sems):
        send, recv, loc = sems
        x, y, c = _place()
        me = 4 * x + 2 * y + c
        remote, lands = [], []
        for rel in range(1, 8):
            dx, dy, dc = rel >> 2, (rel >> 1) & 1, rel & 1
            peer = (1 - x if dx else x, 1 - y if dy else y, 1 - c if dc else c)
            who = 4 * peer[0] + 2 * peer[1] + peer[2]
            remote.append(pltpu.make_async_remote_copy(ins[0], outs[0].at[me], send.at[rel - 1], recv.at[rel - 1],
                                                       device_id=peer, device_id_type=MESH))
            lands.append(pltpu.make_async_remote_copy(ins[0], outs[0].at[who], send.at[rel - 1], recv.at[rel - 1],
                                                      device_id=peer, device_id_type=MESH))
        return [pltpu.make_async_copy(ins[0], outs[0].at[me], loc.at[0])], remote, lands, []

    return CopyJob([p], [jax.ShapeDtypeStruct((8,) + p.shape, F32)], _job_sems(7, 1), copies)


def small_sum(slots):
    def add(*terms):
        acc = terms[0]
        for t in terms[1:]:
            acc = acc + t
        return (acc,)

    return _rows_call(add, [(slots, d) for d in range(8)], [F32], name="small_sum", tr=8 * 47)[0]


BIG = ("w_in", "p_ret", "p_sb", "p_sgu", "w_out", "w_up", "w_down")
BIG_AXIS = {"w_in": 1, "p_ret": 1, "p_sb": 1, "p_sgu": 1, "w_out": 0, "w_up": 1, "w_down": 0}
SMALL = ("ret_gn_g", "ret_gn_b", "sgu_ln_g", "sgu_ln_b", "sgu_w", "sgu_b", "ln1_g", "ln1_b", "ln2_g", "ln2_b")


def layer_forward(l, x0, W, sm, rope, rconsts, hooks):
    n = f"l{l}_"
    proj = matmul(x0, W["w_in"], mode="nn", tm=2048, tn=640, tk=1024, name=n + "proj")
    retg, raw, states = ret_fwd(proj, *rope, rconsts, sm["ret_gn_g"], sm["ret_gn_b"], name=n + "ret_fwd")
    job = hooks.fwd_job(l, "sb")
    sb, job_out = sb_fwd(proj, name=n + "sb_fwd", job=job)
    if job is not None:
        hooks.done(job, job_out)
    sg = sgu_fwd(proj, sm["sgu_ln_g"], sm["sgu_ln_b"], sm["sgu_w"], sm["sgu_bias"], name=n + "sgu_fwd")
    merged, r1, r2, r3 = merge_fwd(retg, sb, sg, W["p_ret"], W["p_sb"], W["p_sgu"], proj, name=n + "merge_fwd")
    x1, xh1, rs1 = matmul_ln(merged, W["w_out"], x0, sm["ln1_g"], sm["ln1_b"], tk=1024, name=n + "out_ln1")
    job = hooks.fwd_job(l, "up")
    h1 = matmul(x1, W["w_up"], mode="nn", tm=1024, tn=1024, tk=1024, name=n + "up", job=job)
    if job is not None:
        h1, job_out = h1
        hooks.done(job, job_out)
    job = hooks.fwd_job(l, "down")
    res = matmul_ln(h1, W["w_down"], x1, sm["ln2_g"], sm["ln2_b"], pro=_relu2, tk=1024, name=n + "down_ln2", job=job)
    if job is not None:
        res, job_out = res
        hooks.done(job, job_out)
    x2, xh2, rs2 = res
    saved = dict(x0=x0, proj=proj, retg=retg, raw=raw, states=states, sb=sb, sg=sg, merged=merged, r=(r1, r2, r3),
                 x1=x1, xh1=xh1, rs1=rs1, h1=h1, xh2=xh2, rs2=rs2)
    return x2, saved


def layer_backward(l, dx2, s, W, sm, rope, rconsts, hooks):
    n = f"l{l}_"
    two = ((F32, None), (BF16, None))
    gw, gs = {}, {}
    du2, du2h, gs["ln2_g"], gs["ln2_b"] = ln_bwd(dx2, s["xh2"], s["rs2"], sm["ln2_g"], name=n + "ln2_bwd")
    gw["w_down"] = matmul(s["h1"], du2h, mode="tn", tm=1024, tn=1024, tk=512, pro=_relu2, outs=two, name=n + "g_down")
    dh1 = matmul(du2h, W["w_down"], mode="nt", tm=1024, tn=1024, tk=1024, outs=((BF16, None),),
                 epi=lambda acc, h: (acc * (2.0 * jnp.maximum(h, 0.0)),), tiles=(s["h1"],), name=n + "d_h1")
    gw["w_up"] = matmul(s["x1"], dh1, mode="tn", tm=1024, tn=1024, tk=512, outs=two, name=n + "g_up")
    dx1 = matmul(dh1, W["w_up"], mode="nt", tm=1024, tn=1024, tk=1024,
                 epi=lambda acc, d: (acc + ALPHA * d,), tiles=(du2,), name=n + "d_x1")
    du1, du1h, gs["ln1_g"], gs["ln1_b"] = ln_bwd(dx1, s["xh1"], s["rs1"], sm["ln1_g"], name=n + "ln1_bwd")
    gw["w_out"] = matmul(s["merged"], du1h, mode="tn", tm=1024, tn=1024, tk=512, outs=two, name=n + "g_out")
    dmerged = matmul(du1h, W["w_out"], mode="nt", tm=1024, tn=1024, tk=1024, name=n + "d_merged")
    dr1, dr2, dr3, dg1, dg2, dg3 = merge_bwd(dmerged, *s["r"], s["proj"], name=n + "merge_bwd")
    d_branch = {}
    for nm, a, dr in (("p_ret", s["retg"], dr1), ("p_sb", s["sb"], dr2), ("p_sgu", s["sg"], dr3)):
        gw[nm] = matmul(a, dr, mode="tn", tm=512, tn=1024, tk=512, outs=two, name=n + "g_" + nm)
        d_branch[nm] = matmul(dr, W[nm], mode="nt", tm=1024, tn=512, tk=1024, name=n + "d_" + nm)
    job = hooks.pair(l, gw)
    dret, gs["ret_gn_g"], gs["ret_gn_b"], job_out = ret_bwd(s["proj"], *rope, rconsts, sm["ret_gn_g"], sm["ret_gn_b"],
                                                             s["raw"], s["states"], d_branch["p_ret"], name=n + "ret_bwd", job=job)
    if job is not None:
        hooks.done(job, job_out)
    job = hooks.scatter(l) if job is not None else None
    dsq, dsk, dsv, job_out = sb_bwd(s["proj"], s["sb"], d_branch["p_sb"], name=n + "sb_bwd", job=job)
    if job is not None:
        hooks.done(job, job_out)
    dsgu, gs["sgu_w"], dbias, gs["sgu_ln_g"], gs["sgu_ln_b"] = sgu_bwd(
        s["proj"], sm["sgu_ln_g"], sm["sgu_ln_b"], sm["sgu_w"], sm["sgu_bias"], d_branch["p_sgu"], name=n + "sgu_bwd")
    gs["sgu_b"] = dbias[:, :, 0]
    dproj = jnp.concatenate([dret, dsq, dsk, dsv, dsgu, dg1, dg2, dg3], axis=1)
    job = hooks.small(l, gs)
    gw["w_in"] = matmul(s["x0"], dproj, mode="tn", tm=1024, tn=1920, tk=512, outs=two, name=n + "g_in", job=job)
    if job is not None:
        gw["w_in"], job_out = gw["w_in"]
        hooks.done(job, job_out)
    job = hooks.tail(l, gw["w_in"])
    dx0 = matmul(dproj, W["w_in"], mode="nt", tm=1024, tn=1024, tk=1536,
                 epi=lambda acc, d: (acc + ALPHA * d,), tiles=(du1,), name=n + "d_x0", job=job)
    if job is not None:
        dx0, job_out = dx0
        hooks.done(job, job_out)
    return dx0, gw, gs


def local_step(x, target, small, plan):
    T = x.shape[0]
    rope = _rope_tables(T)
    rconsts = _ret_consts()
    sms = []
    for l in range(DEPTH):
        sm = {k: small[k][l][None, :] for k in SMALL if k not in ("sgu_w", "sgu_b")}
        sm["sgu_w"] = small["sgu_w"][l]
        sm["sgu_bias"] = jnp.broadcast_to(small["sgu_b"][l][:, :, None], (4, CHUNK, CHUNK))
        sms.append(sm)
    h, saved = x, []
    for l in range(DEPTH):
        h, s = layer_forward(l, h, plan.weights(l), sms[l], rope, rconsts, plan)
        saved.append(s)
    dy, sq = loss_head(h, target)
    gs = {k: [None] * DEPTH for k in SMALL}
    for l in reversed(range(DEPTH)):
        dy, gwl, gsl = layer_backward(l, dy, saved[l], plan.weights(l), sms[l], rope, rconsts, plan)
        plan.grads(l, gwl)
        for k in SMALL:
            gs[k][l] = gsl[k].reshape(small[k].shape[1:])
    return sq[0, 0], dy, {k: jnp.stack(v) for k, v in gs.items()}


EARLY_GRADS = ("p_ret", "p_sb", "p_sgu", "w_out", "w_up", "w_down")


class _StepPlan:
    def __init__(self, pos, shards16):
        self.pos = pos
        self.shards16 = shards16
        self.full = [dict() for _ in range(DEPTH)]
        self.gw = [None] * DEPTH
        self.bufs = {}
        self.sums = {}
        self.gs = [None] * DEPTH
        first = self._gather([(0, "w_in")])
        self.done(first, run_job(first, name="gather_first"))

    def weights(self, l):
        return self.full[l]

    def grads(self, l, gw):
        self.gw[l] = gw

    def _gather(self, items, chips=(0, 1, 2)):
        job = gather_job([self.shards16[l][k] for l, k in items], [BIG_AXIS[k] for _, k in items], chips)
        job.note = ("gather" if 2 in chips else "gather_part", items)
        return job

    def _pair(self, items):
        job = pair_job([g[1] for _, _, g in items], [BIG_AXIS[k] for _, k, _ in items])
        job.note = ("pair", items)
        return job

    def fwd_job(self, l, host):
        if host == "sb":
            return self._gather([(l, k) for k in BIG[1:]])
        if l + 1 == DEPTH:
            return None
        return self._gather([(l + 1, "w_in")], (0, 1) if host == "up" else (2,))

    def pair(self, l, ready):
        items = [(l, k, ready[k]) for k in EARLY_GRADS]
        if l + 1 < DEPTH:
            items.append((l + 1, "w_in", self.gw[l + 1]["w_in"]))
        return self._pair(items)

    def scatter(self, l):
        items, sums16 = self.summed
        job = scatter_job([l_ for l_, _, _ in items], sums16, [BIG_AXIS[k] for _, k, _ in items],
                          [self.bufs.get(k) for _, k, _ in items])
        job.note = ("scatter", items)
        return job

    def small(self, l, gs):
        self.gs[l] = {k: gs[k].reshape(-1) for k in SMALL}
        if l != 0:
            return None
        job = small_job(_pack_small({k: jnp.stack([self.gs[l_][k] for l_ in range(DEPTH)]) for k in SMALL}))
        job.note = ("small", [])
        return job

    def tail(self, l, g):
        if l != 0:
            return None
        last = self._pair([(0, "w_in", g)])
        self.done(last, run_job(last, name="pair_last"))
        return self.scatter(0)

    def done(self, job, outs):
        kind, items = job.note
        if kind == "small":
            self.small_slots = outs[0]
        if kind == "pair":
            sums16 = []
            for a, (l, k, g) in enumerate(items):
                self.sums[(l, k)], s16 = pair_sum(self.pos, outs[a], g[0], BIG_AXIS[k], name=f"pair_sum_{k}_{l}")
                sums16.append(s16)
            self.summed = (items, sums16)
        for a, item in enumerate(items):
            if kind == "gather_part":
                self.shards16[item[0]][item[1]] = outs[a]
            elif kind == "gather":
                self.full[item[0]][item[1]] = outs[a]
            elif kind == "scatter":
                self.bufs[item[1]] = outs[a]

    def finish(self):
        return self.bufs, self.sums


def _flat2(a):
    return a.reshape(-1, a.shape[-1])


def _pack_small(d, pre=""):
    return jnp.concatenate([d[pre + k].reshape(-1) for k in SMALL]).reshape(-1, 128)


def kernel(x, w_in, ret_gn_g, ret_gn_b, sgu_ln_g, sgu_ln_b, sgu_w, sgu_b, p_ret, p_sb, p_sgu, w_out, ln1_g, ln1_b, w_up, w_down, ln2_g, ln2_b, loss_target, m_w_in, m_ret_gn_g, m_ret_gn_b, m_sgu_ln_g, m_sgu_ln_b, m_sgu_w, m_sgu_b, m_p_ret, m_p_sb, m_p_sgu, m_w_out, m_ln1_g, m_ln1_b, m_w_up, m_w_down, m_ln2_g, m_ln2_b, v_w_in, v_ret_gn_g, v_ret_gn_b, v_sgu_ln_g, v_sgu_ln_b, v_sgu_w, v_sgu_b, v_p_ret, v_p_sb, v_p_sgu, v_w_out, v_ln1_g, v_ln1_b, v_w_up, v_w_down, v_ln2_g, v_ln2_b):
    given = dict(locals())
    order = BIG[:1] + SMALL[:6] + BIG[1:5] + SMALL[6:8] + BIG[5:7] + SMALL[8:10]
    L = DEPTH

    px, py, pc = _place()
    pos = jnp.stack([px, py, pc, 2 * px + py]).astype(jnp.int32)

    shards16 = [{k: cast_into_whole(pos, given[k], l, BIG_AXIS[k], name=f"cast_{k}_{l}") for k in BIG} for l in range(L)]
    plan = _StepPlan(pos, shards16)
    sq, dx, gs = local_step(x[0], loss_target[0], {k: given[k] for k in SMALL}, plan)
    loss = 0.5 * lax.psum(sq, ("x", "y", "c"))

    bufs, sums = plan.finish()
    shards = []
    for k in BIG:
        whole = None
        for l in range(L):
            whole = chip_sum(pos, sums[(l, k)], bufs[k], l, BIG_AXIS[k], whole, name=f"chip_sum_{k}_{l}")
        shards.append(whole)
    joined = run_job(join_job(shards), name="join_halves")
    out = {}
    for a, k in enumerate(BIG):
        shp = given[k].shape
        res = _rows_call(lambda g_, w_, m_, v_: (g_,) + _adamw(w_, g_, m_, v_),
                         [joined[a].reshape(-1, shp[-1]), _flat2(given[k]), _flat2(given["m_" + k]), _flat2(given["v_" + k])],
                         [F32] * 4, name="adamw_" + k)
        out[k] = [r.reshape(shp) for r in res]

    pack = _pack_small
    res = _rows_call(lambda g_, w_, m_, v_: (g_,) + _adamw(w_, g_, m_, v_),
                     [small_sum(plan.small_slots), pack(given), pack(given, "m_"), pack(given, "v_")], [F32] * 4,
                     name="adamw_small", tr=8 * 47)
    off = 0
    for k in SMALL:
        sz = given[k].size
        out[k] = [r.reshape(-1)[off:off + sz].reshape(given[k].shape) for r in res]
        off += sz

    grads = [out[k][0] for k in order]
    deltas = [out[k][1] for k in order]
    new_m = [out[k][2] for k in order]
    new_v = [out[k][3] for k in order]
    return (loss, dx[None], *grads, *deltas, *new_m, *new_v)
```

```python
import functools
import math

import jax
import jax.numpy as jnp
from jax import lax
from jax.experimental import pallas as pl
from jax.experimental.pallas import tpu as pltpu

F32 = jnp.float32
BF16 = jnp.bfloat16

D_MODEL = 1024
SEQ = 4096
DEPTH = 2
CHUNK = 128
RET_HEADS = 4
BRANCH_W = 512
N_IN = 7680
D_FF = 4096
LN_EPS = 1e-5
ROPE_BASE = 10000.0
ALPHA = (2 * DEPTH) ** 0.25
RET_SCALE = 128 ** -0.5
SB_SCALE = 64 ** -0.5
C_RET, C_SB, C_SGU, C_GATE = 0, 2048, 3584, 4608

ADAM_LR, ADAM_B1, ADAM_B2, ADAM_EPS, ADAM_WD, ADAM_STEP = 0.001, 0.9, 0.999, 1e-08, 0.01, 10

N_CHIPS = 4
VMEM_LIMIT = 56 * 1024 * 1024
MESH = pl.DeviceIdType.MESH

NN = ((1,), (0,))
NT = ((1,), (1,))
TN = ((0,), (0,))


def _dot(a, b, dims):
    return lax.dot_general(a, b, (dims, ((), ())), preferred_element_type=F32)


def _params(sem):
    return pltpu.CompilerParams(dimension_semantics=sem, vmem_limit_bytes=VMEM_LIMIT)


def _relu2(h):
    r = jnp.maximum(h, 0.0)
    return r * r


def matmul(a, b, *, mode, tm, tn, tk, outs=((F32, None),), pro=None, epi=None, tiles=(), rows=(), name, job=None):
    if mode == "nn":
        (M, K), N = a.shape, b.shape[1]
    elif mode == "nt":
        (M, K), N = a.shape, b.shape[0]
    else:
        (K, M), N = a.shape, b.shape[1]
    tm, tn, tk = min(tm, M), min(tn, N), min(tk, K)
    assert M % tm == 0 and N % tn == 0 and K % tk == 0, (name, M, N, K, tm, tn, tk)
    if mode == "nn":
        a_spec = pl.BlockSpec((tm, tk), lambda i, j, k: (i, k))
        b_spec = pl.BlockSpec((tk, tn), lambda i, j, k: (k, j))
        dims = NN
    elif mode == "nt":
        a_spec = pl.BlockSpec((tm, tk), lambda i, j, k: (i, k))
        b_spec = pl.BlockSpec((tn, tk), lambda i, j, k: (j, k))
        dims = NT
    else:
        a_spec = pl.BlockSpec((tk, tm), lambda i, j, k: (k, i))
        b_spec = pl.BlockSpec((tk, tn), lambda i, j, k: (k, j))
        dims = TN
    nk = K // tk
    nt_, nr, no = len(tiles), len(rows), len(outs)

    def body(a_ref, b_ref, *rest):
        tile_refs = rest[:nt_]
        row_refs = rest[nt_:nt_ + nr]
        out_refs = rest[nt_ + nr:nt_ + nr + no]
        av = a_ref[...]
        if pro is not None:
            av = pro(av)
        p = _dot(av.astype(BF16), b_ref[...].astype(BF16), dims)

        def finish(acc):
            vals = (acc,) * no if epi is None else epi(acc, *[r[...] for r in tile_refs], *[r[...] for r in row_refs])
            for o_ref, v in zip(out_refs, vals):
                o_ref[...] = v.astype(o_ref.dtype)

        if nk == 1:
            finish(p)
        else:
            acc_ref = rest[-1]
            k = pl.program_id(2)

            @pl.when(k == 0)
            def _():
                acc_ref[...] = p

            @pl.when(k > 0)
            def _():
                acc_ref[...] += p

            @pl.when(k == nk - 1)
            def _():
                finish(acc_ref[...])

    out_shape, out_specs = [], []
    for dt, width in outs:
        if width is None:
            out_shape.append(jax.ShapeDtypeStruct((M, N), dt))
            out_specs.append(pl.BlockSpec((tm, tn), lambda i, j, k: (i, j)))
        else:
            assert N == tn
            out_shape.append(jax.ShapeDtypeStruct((M, width), dt))
            out_specs.append(pl.BlockSpec((tm, width), lambda i, j, k: (i, 0)))
    in_specs = [a_spec, b_spec]
    in_specs += [pl.BlockSpec((tm, tn), lambda i, j, k: (i, j)) for _ in tiles]
    in_specs += [pl.BlockSpec((1, tn), lambda i, j, k: (0, j)) for _ in rows]
    grid = (M // tm, N // tn, nk)
    scratch = [pltpu.VMEM((tm, tn), F32)] if nk > 1 else []
    j = _job_args(job, len(in_specs), no)
    res = pl.pallas_call(
        _hosting(body, job, len(in_specs), no, len(scratch), grid), name=name, grid=grid,
        in_specs=in_specs + j["in_specs"], out_specs=out_specs + j["out_specs"], out_shape=out_shape + j["out_shape"],
        scratch_shapes=scratch + j["scratch"], input_output_aliases=j["aliases"],
        compiler_params=_params(("parallel", "parallel", "arbitrary") if job is None else ("arbitrary",) * 3),
    )(a, b, *tiles, *rows, *j["ins"])
    mine = res[0] if no == 1 else list(res[:no])
    return mine if job is None else (mine, list(res[no:]))


def _ln_epi(acc, res, g, b):
    u = ALPHA * res + acc
    mu = jnp.mean(u, axis=-1, keepdims=True)
    xc = u - mu
    var = jnp.mean(xc * xc, axis=-1, keepdims=True)
    rstd = lax.rsqrt(var + LN_EPS)
    xhat = xc * rstd
    return xhat * g + b, xhat, jnp.broadcast_to(rstd, (u.shape[0], 128))


def matmul_ln(a, w, res, g, b, *, pro=None, tk, name, job=None):
    n = w.shape[1]
    return matmul(a, w, mode="nn", tm=1024, tn=n, tk=tk, pro=pro, epi=_ln_epi, tiles=(res,), rows=(g, b),
                  outs=((F32, None), (F32, None), (F32, 128)), name=name, job=job)


def ln_bwd(dy, xhat, rstd, g, *, name):
    T, D = dy.shape
    tm = min(512, T)

    def body(dy_ref, xh_ref, rs_ref, g_ref, du_ref, du16_ref, dg_ref, db_ref):
        dyv, xh = dy_ref[...], xh_ref[...]
        r = rs_ref[:, 0:1]
        dxh = dyv * g_ref[...]
        m1 = jnp.mean(dxh, axis=-1, keepdims=True)
        m2 = jnp.mean(dxh * xh, axis=-1, keepdims=True)
        du = r * (dxh - m1 - xh * m2)
        du_ref[...] = du
        du16_ref[...] = du.astype(BF16)

        @pl.when(pl.program_id(0) == 0)
        def _():
            dg_ref[...] = jnp.zeros_like(dg_ref)
            db_ref[...] = jnp.zeros_like(db_ref)

        dg_ref[...] += jnp.sum(dyv * xh, axis=0, keepdims=True)
        db_ref[...] += jnp.sum(dyv, axis=0, keepdims=True)

    row = pl.BlockSpec((tm, D), lambda i: (i, 0))
    vec = pl.BlockSpec((1, D), lambda i: (0, 0))
    return pl.pallas_call(
        body, name=name, grid=(T // tm,),
        in_specs=[row, row, pl.BlockSpec((tm, 128), lambda i: (i, 0)), vec],
        out_specs=[row, row, vec, vec],
        out_shape=[jax.ShapeDtypeStruct((T, D), F32), jax.ShapeDtypeStruct((T, D), BF16),
                   jax.ShapeDtypeStruct((1, D), F32), jax.ShapeDtypeStruct((1, D), F32)],
        compiler_params=_params(("arbitrary",)),
    )(dy, xhat, rstd, g)


def loss_head(y, target):
    T, D = y.shape
    tm = min(512, T)

    def body(y_ref, t_ref, dy_ref, s_ref):
        e = y_ref[...] - t_ref[...]
        dy_ref[...] = e * (1.0 / D)

        @pl.when(pl.program_id(0) == 0)
        def _():
            s_ref[...] = jnp.zeros_like(s_ref)

        s_ref[...] += jnp.sum(jnp.mean(e * e, axis=-1, keepdims=True))

    row = pl.BlockSpec((tm, D), lambda i: (i, 0))
    return pl.pallas_call(
        body, name="loss_head", grid=(T // tm,),
        in_specs=[row, row], out_specs=[row, pl.BlockSpec((8, 128), lambda i: (0, 0))],
        out_shape=[jax.ShapeDtypeStruct((T, D), F32), jax.ShapeDtypeStruct((8, 128), F32)],
        compiler_params=_params(("arbitrary",)),
    )(y, target)


def _rope_tables(T):
    half = 64
    inv_freq = ROPE_BASE ** (-jnp.arange(half, dtype=F32) / half)
    ang = jnp.arange(T, dtype=jnp.int32).astype(F32)[:, None] * inv_freq[None, :]
    cos, sin = jnp.cos(ang), jnp.sin(ang)
    return jnp.concatenate([cos, cos], axis=1), jnp.concatenate([-sin, sin], axis=1)


def _ret_consts():
    H = RET_HEADS
    log_g = jnp.log(1.0 - 2.0 ** (-5.0 - jnp.arange(H, dtype=F32)))
    idx = jnp.arange(CHUNK, dtype=F32)
    diff = idx[:, None] - idx[None, :]
    dmat = jnp.where(diff[None] >= 0, jnp.exp(log_g[:, None, None] * diff[None]), 0.0)
    kd = jnp.exp(log_g[:, None] * (CHUNK - 1 - idx)[None, :])
    qd = jnp.exp(log_g[:, None] * (idx + 1.0)[None, :])
    cd = jnp.exp(log_g * CHUNK)
    full = (H, CHUNK, CHUNK)
    return (dmat.astype(F32), jnp.broadcast_to(kd[:, :, None], full), jnp.broadcast_to(qd[:, :, None], full),
            jnp.broadcast_to(cd[:, None, None], full))


def _swap_halves(v):
    return pltpu.roll(v, 64, 1)


def _group_norm(o):
    mu = jnp.mean(o, axis=-1, keepdims=True)
    xc = o - mu
    var = jnp.mean(xc * xc, axis=-1, keepdims=True)
    rstd = lax.rsqrt(var + LN_EPS)
    return xc * rstd, rstd


def ret_fwd(proj, cosf, sinf, consts, gn_g, gn_b, *, name):
    T = proj.shape[0]
    tb = min(512, T)
    nch = tb // CHUNK
    H = RET_HEADS

    def body(p_ref, cos_ref, sin_ref, dm_ref, kd_ref, qd_ref, cd_ref, g_ref, b_ref, out_ref, raw_ref, st_ref, s_ref):
        @pl.when(pl.program_id(0) == 0)
        def _():
            s_ref[...] = jnp.zeros_like(s_ref)

        for c in range(nch):
            r = slice(c * CHUNK, (c + 1) * CHUNK)
            cs, sn = cos_ref[r, :], sin_ref[r, :]
            for h in range(H):
                hc = slice(h * 128, (h + 1) * 128)
                q = p_ref[r, h * 128:(h + 1) * 128]
                k = p_ref[r, 512 + h * 128:512 + (h + 1) * 128]
                v = p_ref[r, 1024 + h * 128:1024 + (h + 1) * 128]
                gt = p_ref[r, 1536 + h * 128:1536 + (h + 1) * 128]
                qr = q * cs + _swap_halves(q) * sn
                kr = (k * cs + _swap_halves(k) * sn) * RET_SCALE
                sprev = s_ref[h]
                st_ref[c, h] = sprev
                qb, kb, vb = qr.astype(BF16), kr.astype(BF16), v.astype(BF16)
                s = _dot(qb, kb, NT) * dm_ref[h]
                o = _dot(s.astype(BF16), vb, NN) + _dot((qr * qd_ref[h]).astype(BF16), sprev.astype(BF16), NN)
                s_ref[h] = sprev * cd_ref[h] + _dot((kr * kd_ref[h]).astype(BF16), vb, TN)
                raw_ref[r, hc] = o
                y, _ = _group_norm(o)
                out_ref[r, hc] = (gt * jax.nn.sigmoid(gt)) * (y * g_ref[:, hc] + b_ref[:, hc])

    cmat = pl.BlockSpec((H, CHUNK, CHUNK), lambda i: (0, 0, 0))
    vec = pl.BlockSpec((1, BRANCH_W), lambda i: (0, 0))
    rope = pl.BlockSpec((tb, 128), lambda i: (i, 0))
    blk = pl.BlockSpec((tb, BRANCH_W), lambda i: (i, 0))
    return pl.pallas_call(
        body, name=name, grid=(T // tb,),
        in_specs=[pl.BlockSpec((tb, 2048), lambda i: (i, 0)), rope, rope, cmat, cmat, cmat, cmat, vec, vec],
        out_specs=[blk, blk, pl.BlockSpec((nch, H, CHUNK, CHUNK), lambda i: (i, 0, 0, 0))],
        out_shape=[jax.ShapeDtypeStruct((T, BRANCH_W), F32), jax.ShapeDtypeStruct((T, BRANCH_W), F32),
                   jax.ShapeDtypeStruct((T // CHUNK, H, CHUNK, CHUNK), F32)],
        scratch_shapes=[pltpu.VMEM((H, CHUNK, CHUNK), F32)],
        compiler_params=_params(("arbitrary",)),
    )(proj, cosf, sinf, *consts, gn_g, gn_b)


def ret_bwd(proj, cosf, sinf, consts, gn_g, gn_b, raw, states, dout, *, name, job=None):
    T = proj.shape[0]
    tb = min(512, T)
    nch = tb // CHUNK
    nb = T // tb
    H = RET_HEADS

    def body(p_ref, cos_ref, sin_ref, dm_ref, kd_ref, qd_ref, cd_ref, g_ref, b_ref, raw_ref, st_ref, do_ref,
             dp_ref, dg_ref, db_ref, ds_ref):
        @pl.when(pl.program_id(0) == 0)
        def _():
            ds_ref[...] = jnp.zeros_like(ds_ref)
            dg_ref[...] = jnp.zeros_like(dg_ref)
            db_ref[...] = jnp.zeros_like(db_ref)

        for c in reversed(range(nch)):
            r = slice(c * CHUNK, (c + 1) * CHUNK)
            cs, sn = cos_ref[r, :], sin_ref[r, :]
            for h in range(H):
                hc = slice(h * 128, (h + 1) * 128)
                q = p_ref[r, h * 128:(h + 1) * 128]
                k = p_ref[r, 512 + h * 128:512 + (h + 1) * 128]
                v = p_ref[r, 1024 + h * 128:1024 + (h + 1) * 128]
                gt = p_ref[r, 1536 + h * 128:1536 + (h + 1) * 128]
                qr = q * cs + _swap_halves(q) * sn
                kr = (k * cs + _swap_halves(k) * sn) * RET_SCALE
                sprev = st_ref[c, h]
                gv = g_ref[:, hc]
                y, rstd = _group_norm(raw_ref[r, hc])
                d_out = do_ref[r, hc]
                sg = jax.nn.sigmoid(gt)
                d_gate = d_out * (y * gv + b_ref[:, hc]) * (sg * (1.0 + gt * (1.0 - sg)))
                d_aff = d_out * (gt * sg)
                dg_ref[:, hc] += jnp.sum(d_aff * y, axis=0, keepdims=True)
                db_ref[:, hc] += jnp.sum(d_aff, axis=0, keepdims=True)
                dxh = d_aff * gv
                m1 = jnp.mean(dxh, axis=-1, keepdims=True)
                m2 = jnp.mean(dxh * y, axis=-1, keepdims=True)
                d_o = (rstd * (dxh - m1 - y * m2)).astype(BF16)
                qb, kb, vb = qr.astype(BF16), kr.astype(BF16), v.astype(BF16)
                dm, kd, qd = dm_ref[h], kd_ref[h], qd_ref[h]
                p = (_dot(qb, kb, NT) * dm).astype(BF16)
                dp = (_dot(d_o, vb, NT) * dm).astype(BF16)
                dsn = ds_ref[h]
                dsb = dsn.astype(BF16)
                dq_r = _dot(dp, kb, NN) + _dot(d_o, sprev.astype(BF16), NT) * qd
                dk_r = (_dot(dp, qb, TN) + _dot(vb, dsb, NT) * kd) * RET_SCALE
                d_v = _dot(p, d_o, TN) + _dot((kr * kd).astype(BF16), dsb, NN)
                ds_ref[h] = dsn * cd_ref[h] + _dot((qr * qd).astype(BF16), d_o, TN)
                dp_ref[r, h * 128:(h + 1) * 128] = (dq_r * cs - _swap_halves(dq_r) * sn).astype(BF16)
                dp_ref[r, 512 + h * 128:512 + (h + 1) * 128] = (dk_r * cs - _swap_halves(dk_r) * sn).astype(BF16)
                dp_ref[r, 1024 + h * 128:1024 + (h + 1) * 128] = d_v.astype(BF16)
                dp_ref[r, 1536 + h * 128:1536 + (h + 1) * 128] = d_gate.astype(BF16)

    cmat = pl.BlockSpec((H, CHUNK, CHUNK), lambda i: (0, 0, 0))
    vec = pl.BlockSpec((1, BRANCH_W), lambda i: (0, 0))
    rope = pl.BlockSpec((tb, 128), lambda i: (nb - 1 - i, 0))
    blk = pl.BlockSpec((tb, BRANCH_W), lambda i: (nb - 1 - i, 0))
    wide = pl.BlockSpec((tb, 2048), lambda i: (nb - 1 - i, 0))
    j = _job_args(job, 12, 3)
    res = pl.pallas_call(
        _hosting(body, job, 12, 3, 1, nb), name=name, grid=(nb,),
        in_specs=[wide, rope, rope, cmat, cmat, cmat, cmat, vec, vec, blk,
                  pl.BlockSpec((nch, H, CHUNK, CHUNK), lambda i: (nb - 1 - i, 0, 0, 0)), blk] + j["in_specs"],
        out_specs=[wide, vec, vec] + j["out_specs"],
        out_shape=[jax.ShapeDtypeStruct((T, 2048), BF16), jax.ShapeDtypeStruct((1, BRANCH_W), F32),
                   jax.ShapeDtypeStruct((1, BRANCH_W), F32)] + j["out_shape"],
        scratch_shapes=[pltpu.VMEM((H, CHUNK, CHUNK), F32)] + j["scratch"], input_output_aliases=j["aliases"],
        compiler_params=_params(("arbitrary",)),
    )(proj, cosf, sinf, *consts, gn_g, gn_b, raw, states, dout, *j["ins"])
    return res[0], res[1], res[2], list(res[3:])


def _sb_masks():
    row = lax.broadcasted_iota(jnp.int32, (CHUNK, CHUNK), 0)
    lane = lax.broadcasted_iota(jnp.int32, (CHUNK, CHUNK), 1)
    return row, lane


SB_QT = 256
SB_DEAD = -105.0


def _pair(v):
    hi = v.astype(BF16)
    return jnp.concatenate([hi, (v - hi.astype(F32)).astype(BF16)], axis=1)


def _sb_consts():
    r = lax.broadcasted_iota(jnp.int32, (256, 256), 0) & 127
    c = lax.broadcasted_iota(jnp.int32, (256, 256), 1)
    ones = c >= 128
    lane = lax.broadcasted_iota(jnp.int32, (CHUNK, CHUNK), 1)
    return (ones | (r > c)).astype(BF16), (ones | (r >= c)).astype(BF16), (lane < 64, lane >= 64)


def _per_head(x, hms):
    return jnp.concatenate([jnp.where(hm, x, 0.0) for hm in hms], axis=0).astype(BF16)


def _sb_logits(qb, kb2, mask2):
    z = _dot(qb, kb2, NT)
    l1p = jnp.log(1.0 + jnp.exp(-jnp.abs(z)))
    lsp = jnp.minimum(z, 0.0) - l1p
    lsn = lsp - z
    if mask2 is not None:
        lsn = jnp.where(mask2, lsn, 0.0)
    return lsp, lsn


def _sb_tile_mask(qt):
    trow = lax.broadcasted_iota(jnp.int32, (qt, 256), 0)
    tlane = lax.broadcasted_iota(jnp.int32, (qt, 256), 1) & 127
    return lambda m: (tlane + m * CHUNK) < trow


def sb_fwd(proj, *, name, job=None):
    T = proj.shape[0]
    qt = min(SB_QT, T)
    nsub = qt // CHUNK
    cb = C_SB // 128

    def body(q_ref, k_ref, v_ref, o_ref):
        u_gt, _, hms = _sb_consts()
        tile_mask = _sb_tile_mask(qt)

        def qtile(i, _):
            rq = pl.ds(pl.multiple_of(i * qt, qt), qt)
            qb = (q_ref[rq, :] * SB_SCALE).astype(BF16)

            def group(js, masks, state):
                carry, acc = list(state[:2]), state[2]
                rows = [pl.ds(pl.multiple_of(j * CHUNK, CHUNK), CHUNK) for j in js]
                logits = [_sb_logits(qb, _per_head(k_ref[rk, :], hms), m) for rk, m in zip(rows, masks)]
                sums = [[_dot(_pair(lsn[:, h * 128:(h + 1) * 128]), u_gt, NN) for h in range(2)] for _, lsn in logits]
                weights = []
                for (lsp, _), r, m in zip(logits, sums, masks):
                    a_b = []
                    for h in range(2):
                        hc = slice(h * 128, (h + 1) * 128)
                        a = jnp.exp(lsp[:, hc] + r[h][:, :128] + carry[h])
                        if m is not None:
                            a = jnp.where(m[:, hc], a, 0.0)
                        carry[h] = carry[h] + r[h][:, 128:]
                        a_b.append(a.astype(BF16))
                    weights.append(jnp.concatenate(a_b, axis=1))
                for rk, a in zip(rows, weights):
                    acc = acc + _dot(a, _per_head(v_ref[rk, :], hms), NN)
                return carry[0], carry[1], acc

            zero = jnp.zeros((qt, 128), F32)
            diag = list(reversed(range(nsub)))
            state = group([i * nsub + m for m in diag], [tile_mask(m) for m in diag], (zero, zero, zero))

            def live(c):
                return jnp.logical_and(c[0] < i, jnp.maximum(jnp.max(c[1][0]), jnp.max(c[1][1])) > SB_DEAD)

            def blocks(c):
                jj, st = c
                return jj + 1, group([(i - jj) * nsub - 1 - u for u in range(nsub)], [None] * nsub, st)

            _, state = lax.while_loop(live, blocks, (jnp.int32(0), state))
            o_ref[rq, :] = state[2]
            return 0

        lax.fori_loop(0, T // qt, qtile, 0)

    def col(off):
        return pl.BlockSpec((T, 128), lambda hp: (0, off + hp))

    steps = BRANCH_W // 128
    j = _job_args(job, 3, 1)
    res = pl.pallas_call(
        _hosting(body, job, 3, 1, 0, steps), name=name, grid=(steps,),
        in_specs=[col(cb), col(cb + 4), col(cb + 8)] + j["in_specs"], out_specs=[col(0)] + j["out_specs"],
        out_shape=[jax.ShapeDtypeStruct((T, BRANCH_W), F32)] + j["out_shape"],
        scratch_shapes=j["scratch"], input_output_aliases=j["aliases"],
        compiler_params=_params(("parallel",) if job is None else ("arbitrary",)),
    )(proj, proj, proj, *j["ins"])
    return res[0], list(res[1:])


def sb_bwd(proj, out, dout, *, name, job=None):
    T = proj.shape[0]
    qt = min(SB_QT, T)
    nsub = qt // CHUNK
    cb = C_SB // 128

    def body(q_ref, k_ref, v_ref, o_ref, do_ref, dq_ref, dk_ref, dv_ref, dkt_ref, dvt_ref):
        u_gt, u_ge, hms = _sb_consts()
        tile_mask = _sb_tile_mask(qt)
        tall_lane = lax.broadcasted_iota(jnp.int32, (qt, 128), 1)
        top = lax.broadcasted_iota(jnp.int32, (CHUNK, CHUNK), 0) < 64
        dkt_ref[...] = jnp.zeros_like(dkt_ref)
        dvt_ref[...] = jnp.zeros_like(dvt_ref)

        def qtile(i, _):
            rq = pl.ds(pl.multiple_of(i * qt, qt), qt)
            qs = q_ref[rq, :] * SB_SCALE
            qb, q_t = qs.astype(BF16), qs.T.astype(BF16)
            dov = do_ref[rq, :]
            dob, do_t = dov.astype(BF16), dov.T.astype(BF16)
            prod = dob.astype(F32) * o_ref[rq, :]
            total = [jnp.broadcast_to(jnp.sum(jnp.where(hm, prod, 0.0), axis=1, keepdims=True), (qt, 128))
                     for hm in (tall_lane < 64, tall_lane >= 64)]

            def group(js, masks, state):
                c_l, c_w, dq = list(state[:2]), list(state[2:4]), state[4]
                heads = [slice(h * 128, (h + 1) * 128) for h in range(2)]
                rows = [pl.ds(pl.multiple_of(j * CHUNK, CHUNK), CHUNK) for j in js]
                kb2 = [_per_head(k_ref[rk, :], hms) for rk in rows]
                logits = [_sb_logits(qb, kb, m) for kb, m in zip(kb2, masks)]
                da = [_dot(dob, _per_head(v_ref[rk, :], hms), NT) for rk in rows]
                sums = [[_dot(_pair(lsn[:, hc]), u_gt, NN) for hc in heads] for _, lsn in logits]
                a_b, w_all = [], []
                for (lsp, _), r, d, m in zip(logits, sums, da, masks):
                    a_h, w_h = [], []
                    for h, hc in enumerate(heads):
                        a = jnp.exp(lsp[:, hc] + r[h][:, :128] + c_l[h])
                        if m is not None:
                            a = jnp.where(m[:, hc], a, 0.0)
                        c_l[h] = c_l[h] + r[h][:, 128:]
                        a = a.astype(BF16)
                        a_h.append(a)
                        w_h.append(a.astype(F32) * d[:, hc])
                    a_b.append(jnp.concatenate(a_h, axis=1))
                    w_all.append(w_h)
                sums_w = [[_dot(_pair(w), u_ge, NN) for w in w_h] for w_h in w_all]
                dz_b = []
                for (lsp, _), w_h, r, m in zip(logits, w_all, sums_w, masks):
                    sp = jnp.exp(lsp)
                    dz_h = []
                    for h, hc in enumerate(heads):
                        later_w = r[h][:, :128] + c_w[h]
                        c_w[h] = c_w[h] + r[h][:, 128:]
                        dz = w_h[h] * (1.0 - sp[:, hc]) - sp[:, hc] * (total[h] - later_w)
                        if m is not None:
                            dz = jnp.where(m[:, hc], dz, 0.0)
                        dz_h.append(dz.astype(BF16))
                    dz_b.append(jnp.concatenate(dz_h, axis=1))
                for j, kb, a, dz in zip(js, kb2, a_b, dz_b):
                    dkt = _dot(q_t, dz, NN)
                    dvt = _dot(do_t, a, NN)
                    dkt_ref[j] += jnp.where(top, dkt[:, :128], dkt[:, 128:])
                    dvt_ref[j] += jnp.where(top, dvt[:, :128], dvt[:, 128:])
                    dq = dq + _dot(dz, kb, NN)
                return c_l[0], c_l[1], c_w[0], c_w[1], dq

            zero = jnp.zeros((qt, 128), F32)
            diag = list(reversed(range(nsub)))
            state = group([i * nsub + m for m in diag], [tile_mask(m) for m in diag], (zero,) * 5)

            def live(c):
                return jnp.logical_and(c[0] < i, jnp.maximum(jnp.max(c[1][0]), jnp.max(c[1][1])) > SB_DEAD)

            def blocks(c):
                jj, st = c
                return jj + 1, group([(i - jj) * nsub - 1 - u for u in range(nsub)], [None] * nsub, st)

            _, state = lax.while_loop(live, blocks, (jnp.int32(0), state))
            dq_ref[rq, :] = (state[4] * SB_SCALE).astype(BF16)
            return 0

        lax.fori_loop(0, T // qt, qtile, 0)

        def untranspose(jb, _):
            rk = pl.ds(pl.multiple_of(jb * CHUNK, CHUNK), CHUNK)
            dk_ref[rk, :] = dkt_ref[jb].T.astype(BF16)
            dv_ref[rk, :] = dvt_ref[jb].T.astype(BF16)
            return 0

        lax.fori_loop(0, T // CHUNK, untranspose, 0)

    def col(off):
        return pl.BlockSpec((T, 128), lambda hp: (0, off + hp))

    o16 = jax.ShapeDtypeStruct((T, BRANCH_W), BF16)
    steps = BRANCH_W // 128
    j = _job_args(job, 5, 3)
    acc = pltpu.VMEM((T // CHUNK, CHUNK, CHUNK), F32)
    res = pl.pallas_call(
        _hosting(body, job, 5, 3, 2, steps), name=name, grid=(steps,),
        in_specs=[col(cb), col(cb + 4), col(cb + 8), col(0), col(0)] + j["in_specs"],
        out_specs=[col(0), col(0), col(0)] + j["out_specs"], out_shape=[o16, o16, o16] + j["out_shape"],
        scratch_shapes=[acc, acc] + j["scratch"], input_output_aliases=j["aliases"],
        compiler_params=_params(("parallel",) if job is None else ("arbitrary",)),
    )(proj, proj, proj, out, dout, *j["ins"])
    return res[0], res[1], res[2], list(res[3:])


_G0 = math.sqrt(2.0 / math.pi)
_G1 = 0.044715


def _gelu(x):
    return 0.5 * x * (1.0 + jnp.tanh(_G0 * (x + _G1 * x * x * x)))


def _gelu_grad(x):
    t = jnp.tanh(_G0 * (x + _G1 * x * x * x))
    return 0.5 * (1.0 + t) + 0.5 * x * (1.0 - t * t) * (_G0 * (1.0 + 3.0 * _G1 * x * x))


def _tril():
    row, lane = _sb_masks()
    return row >= lane


def sgu_fwd(proj, ln_g, ln_b, w, bias, *, name):
    T = proj.shape[0]
    tb = min(512, T)
    G = BRANCH_W // 128

    def body(u_ref, v_ref, g_ref, b_ref, w_ref, bias_ref, o_ref):
        vv = _gelu(v_ref[...])
        xh, _ = _group_norm(vv)
        vn = (xh * g_ref[...] + b_ref[...]).astype(BF16)
        tril = _tril()
        for g in range(G):
            wg = jnp.where(tril, w_ref[g], 0.0).astype(BF16)
            gc = slice(g * 128, (g + 1) * 128)
            for c in range(tb // CHUNK):
                r = slice(c * CHUNK, (c + 1) * CHUNK)
                sv = _dot(wg, vn[r, gc], NN) + bias_ref[g]
                o_ref[r, gc] = _gelu(u_ref[r, gc]) * sv

    cu, cv = C_SGU // BRANCH_W, C_SGU // BRANCH_W + 1
    vec = pl.BlockSpec((1, BRANCH_W), lambda i: (0, 0))
    mat = pl.BlockSpec((G, CHUNK, CHUNK), lambda i: (0, 0, 0))
    return pl.pallas_call(
        body, name=name, grid=(T // tb,),
        in_specs=[pl.BlockSpec((tb, BRANCH_W), lambda i: (i, cu)), pl.BlockSpec((tb, BRANCH_W), lambda i: (i, cv)),
                  vec, vec, mat, mat],
        out_specs=pl.BlockSpec((tb, BRANCH_W), lambda i: (i, 0)),
        out_shape=jax.ShapeDtypeStruct((T, BRANCH_W), F32),
        compiler_params=_params(("parallel",)),
    )(proj, proj, ln_g, ln_b, w, bias)


def sgu_bwd(proj, ln_g, ln_b, w, bias, dout, *, name):
    T = proj.shape[0]
    tb = min(512, T)
    G = BRANCH_W // 128

    def body(u_ref, v_ref, g_ref, b_ref, w_ref, bias_ref, do_ref, dp_ref, dw_ref, dbias_ref, dg_ref, db_ref, dvn_ref):
        @pl.when(pl.program_id(0) == 0)
        def _():
            dw_ref[...] = jnp.zeros_like(dw_ref)
            dbias_ref[...] = jnp.zeros_like(dbias_ref)
            dg_ref[...] = jnp.zeros_like(dg_ref)
            db_ref[...] = jnp.zeros_like(db_ref)

        gv = v_ref[...]
        vv = _gelu(gv)
        xh, rstd = _group_norm(vv)
        vn = (xh * g_ref[...] + b_ref[...]).astype(BF16)
        tril = _tril()
        for g in range(G):
            wg = jnp.where(tril, w_ref[g], 0.0).astype(BF16)
            gc = slice(g * 128, (g + 1) * 128)
            for c in range(tb // CHUNK):
                r = slice(c * CHUNK, (c + 1) * CHUNK)
                vn_c = vn[r, gc]
                sv = _dot(wg, vn_c, NN) + bias_ref[g]
                gu = u_ref[r, gc]
                d_o = do_ref[r, gc]
                dp_ref[r, gc] = (d_o * sv * _gelu_grad(gu)).astype(BF16)
                dsv = d_o * _gelu(gu)
                dsv_b = dsv.astype(BF16)
                dvn_ref[r, gc] = _dot(wg, dsv_b, TN)
                dw_ref[g] += jnp.where(tril, _dot(dsv_b, vn_c, NT), 0.0)
                dbias_ref[g] += jnp.broadcast_to(jnp.sum(dsv, axis=1, keepdims=True), (CHUNK, CHUNK))
        dvn = dvn_ref[...]
        dg_ref[...] += jnp.sum(dvn * xh, axis=0, keepdims=True)
        db_ref[...] += jnp.sum(dvn, axis=0, keepdims=True)
        dxh = dvn * g_ref[...]
        m1 = jnp.mean(dxh, axis=-1, keepdims=True)
        m2 = jnp.mean(dxh * xh, axis=-1, keepdims=True)
        dp_ref[:, BRANCH_W:2 * BRANCH_W] = (rstd * (dxh - m1 - xh * m2) * _gelu_grad(gv)).astype(BF16)

    cu, cv = C_SGU // BRANCH_W, C_SGU // BRANCH_W + 1
    vec = pl.BlockSpec((1, BRANCH_W), lambda i: (0, 0))
    mat = pl.BlockSpec((G, CHUNK, CHUNK), lambda i: (0, 0, 0))
    blk = pl.BlockSpec((tb, BRANCH_W), lambda i: (i, 0))
    msh = jax.ShapeDtypeStruct((G, CHUNK, CHUNK), F32)
    vsh = jax.ShapeDtypeStruct((1, BRANCH_W), F32)
    return pl.pallas_call(
        body, name=name, grid=(T // tb,),
        in_specs=[pl.BlockSpec((tb, BRANCH_W), lambda i: (i, cu)), pl.BlockSpec((tb, BRANCH_W), lambda i: (i, cv)),
                  vec, vec, mat, mat, blk],
        out_specs=[pl.BlockSpec((tb, 2 * BRANCH_W), lambda i: (i, 0)), mat, mat, vec, vec],
        out_shape=[jax.ShapeDtypeStruct((T, 2 * BRANCH_W), BF16), msh, msh, vsh, vsh],
        scratch_shapes=[pltpu.VMEM((tb, BRANCH_W), F32)],
        compiler_params=_params(("arbitrary",)),
    )(proj, proj, ln_g, ln_b, w, bias, dout)


def merge_fwd(a1, a2, a3, p1, p2, p3, proj, *, name):
    T = a1.shape[0]
    tm, tn = min(1024, T), 512
    gb = C_GATE // tn

    def body(a1_ref, a2_ref, a3_ref, p1_ref, p2_ref, p3_ref, g1_ref, g2_ref, g3_ref, m_ref, r1_ref, r2_ref, r3_ref):
        m = None
        for a_ref, p_ref, g_ref, r_ref in ((a1_ref, p1_ref, g1_ref, r1_ref), (a2_ref, p2_ref, g2_ref, r2_ref),
                                           (a3_ref, p3_ref, g3_ref, r3_ref)):
            r = _dot(a_ref[...].astype(BF16), p_ref[...], NN)
            r_ref[...] = r
            t = jax.nn.sigmoid(g_ref[...]) * r
            m = t if m is None else m + t
        m_ref[...] = m.astype(m_ref.dtype)

    a_spec = pl.BlockSpec((tm, BRANCH_W), lambda i, j: (i, 0))
    p_spec = pl.BlockSpec((BRANCH_W, tn), lambda i, j: (0, j))
    o_spec = pl.BlockSpec((tm, tn), lambda i, j: (i, j))
    osh = jax.ShapeDtypeStruct((T, D_MODEL), F32)
    gates = [pl.BlockSpec((tm, tn), functools.partial(lambda i, j, o: (i, o + j), o=gb + 2 * n)) for n in range(3)]
    return pl.pallas_call(
        body, name=name, grid=(T // tm, D_MODEL // tn),
        in_specs=[a_spec, a_spec, a_spec, p_spec, p_spec, p_spec, *gates],
        out_specs=[o_spec] * 4, out_shape=[jax.ShapeDtypeStruct((T, D_MODEL), BF16)] + [osh] * 3,
        compiler_params=_params(("parallel", "parallel")),
    )(a1, a2, a3, p1, p2, p3, proj, proj, proj)


def merge_bwd(dm, r1, r2, r3, proj, *, name):
    T = dm.shape[0]
    tm, tn = min(512, T), 512
    gb = C_GATE // tn

    def body(dm_ref, r1_ref, r2_ref, r3_ref, g1_ref, g2_ref, g3_ref, dr1_ref, dr2_ref, dr3_ref, dg1_ref, dg2_ref, dg3_ref):
        d = dm_ref[...]
        for r_ref, g_ref, dr_ref, dg_ref in ((r1_ref, g1_ref, dr1_ref, dg1_ref), (r2_ref, g2_ref, dr2_ref, dg2_ref),
                                             (r3_ref, g3_ref, dr3_ref, dg3_ref)):
            s = jax.nn.sigmoid(g_ref[...])
            dr_ref[...] = (d * s).astype(BF16)
            dg_ref[...] = (d * r_ref[...] * (s * (1.0 - s))).astype(BF16)

    o_spec = pl.BlockSpec((tm, tn), lambda i, j: (i, j))
    osh = jax.ShapeDtypeStruct((T, D_MODEL), BF16)
    gates = [pl.BlockSpec((tm, tn), functools.partial(lambda i, j, o: (i, o + j), o=gb + 2 * n)) for n in range(3)]
    return pl.pallas_call(
        body, name=name, grid=(T // tm, D_MODEL // tn),
        in_specs=[o_spec] * 4 + gates, out_specs=[o_spec] * 6, out_shape=[osh] * 6,
        compiler_params=_params(("parallel", "parallel")),
    )(dm, r1, r2, r3, proj, proj, proj)


def _rows_call(fn, ins, out_dtypes, *, name, tr=256):
    first = ins[0][0] if isinstance(ins[0], tuple) else ins[0]
    R, C = first.shape[-2:]
    tr = min(tr, R)
    assert R % tr == 0, (name, R, tr)
    arrs, specs = [], []
    for x in ins:
        if isinstance(x, tuple):
            arrs.append(x[0])
            specs.append(pl.BlockSpec((None, tr, C), functools.partial(lambda i, n: (n, i, 0), n=x[1])))
        else:
            arrs.append(x)
            specs.append(pl.BlockSpec((tr, C), lambda i: (i, 0)))
    ni = len(arrs)

    def body(*refs):
        vals = fn(*[r[...] for r in refs[:ni]])
        for o_ref, v in zip(refs[ni:], vals):
            o_ref[...] = v.astype(o_ref.dtype)

    res = pl.pallas_call(
        body, name=name, grid=(R // tr,), in_specs=specs,
        out_specs=[pl.BlockSpec((tr, C), lambda i: (i, 0)) for _ in out_dtypes],
        out_shape=[jax.ShapeDtypeStruct((R, C), dt) for dt in out_dtypes],
        compiler_params=_params(("parallel",)),
    )(*arrs)
    return res


def _tile_rows(rows, cols):
    t = 256
    while t > 8 and (t * cols > 512 * 1024 or rows % t):
        t //= 2
    return t


def _rows_at(fn, pos, ins, outs, steps, *, name, aliases=None):
    read = [n for n, (_, s) in enumerate(ins) if s is not ANY]
    ni = len(ins)

    def body(pos_ref, *refs):
        vals = fn(*[refs[n][...] for n in read])
        for o_ref, v in zip(refs[ni:], vals):
            o_ref[...] = v.astype(o_ref.dtype)

    return pl.pallas_call(
        body, name=name,
        grid_spec=pltpu.PrefetchScalarGridSpec(num_scalar_prefetch=1, grid=(steps,), in_specs=[s for _, s in ins],
                                               out_specs=[s for _, s in outs]),
        out_shape=[sh for sh, _ in outs],
        input_output_aliases={1 + i: o for i, o in (aliases or {}).items()},
        compiler_params=_params(("parallel",)),
    )(pos, *[a for a, _ in ins])


def cast_into_whole(pos, w, l, axis, *, name):
    _, r, n = w.shape
    tr = _tile_rows(r, n)
    if axis == 1:
        shape, spec = (r, n * N_CHIPS), pl.BlockSpec((tr, n), lambda i, p: (i, p[3]))
    else:
        shape, spec = (r * N_CHIPS, n), pl.BlockSpec((tr, n), lambda i, p: (p[3] * (r // tr) + i, 0))
    return _rows_at(lambda a: (a,), pos, [(w, pl.BlockSpec((None, tr, n), lambda i, p: (l, i, 0)))],
                    [(jax.ShapeDtypeStruct(shape, BF16), spec)], r // tr, name=name)[0]


def pair_sum(pos, theirs, g32, axis, *, name):
    rows2, cols = theirs.shape
    h = rows2 // (N_CHIPS if axis == 0 else 1)
    tr = _tile_rows(h, cols)
    hb = h // tr
    if axis == 1:
        own = pl.BlockSpec((tr, cols), lambda i, p: (p[2] * hb + i, 0))
    else:
        own = pl.BlockSpec((tr, cols), lambda i, p: ((2 * (i // hb) + p[2]) * hb + i % hb, 0))
    row = pl.BlockSpec((tr, cols), lambda i, p: (i, 0))
    return _rows_at(lambda t, m: (m + t.astype(F32),) * 2, pos, [(theirs, row), (g32, own)],
                    [(jax.ShapeDtypeStruct((rows2, cols), F32), row), (jax.ShapeDtypeStruct((rows2, cols), BF16), row)],
                    rows2 // tr, name=name)


def chip_sum(pos, h32, recv, l, axis, whole, *, name):
    _, depth, h, n = recv.shape
    tr = _tile_rows(h, n)
    hb = h // tr
    if axis == 1:
        mine = pl.BlockSpec((tr, n), lambda i, p: (i, p[3]))
    else:
        mine = pl.BlockSpec((tr, n), lambda i, p: (p[3] * hb + i, 0))
    ins = [(h32, mine)] + [(recv, pl.BlockSpec((None, None, tr, n), functools.partial(lambda i, p, j: (j, l, i, 0), j=j)))
                           for j in range(3)]
    if whole is not None:
        ins.append((whole, ANY))
    return _rows_at(lambda o, a, b, c: (((o + a.astype(F32)) + b.astype(F32)) + c.astype(F32),), pos, ins,
                    [(jax.ShapeDtypeStruct((depth, 2, h, n), F32), pl.BlockSpec((None, None, tr, n), lambda i, p: (l, p[2], i, 0)))],
                    hb, name=name, aliases=None if whole is None else {4: 0})[0]


def _adamw(w, g, m, v):
    m2 = ADAM_B1 * m + (1.0 - ADAM_B1) * g
    v2 = ADAM_B2 * v + (1.0 - ADAM_B2) * (g * g)
    m_hat = m2 / (1.0 - ADAM_B1 ** ADAM_STEP)
    v_hat = v2 / (1.0 - ADAM_B2 ** ADAM_STEP)
    delta = -ADAM_LR * (m_hat / (jnp.sqrt(v_hat) + ADAM_EPS) + ADAM_WD * w)
    return delta, m2, v2


def _place():
    return lax.axis_index("x"), lax.axis_index("y"), lax.axis_index("c")


def _chip_peers(x, y, c):
    return [((1 - x, y, c), 2 * (1 - x) + y), ((x, 1 - y, c), 2 * x + 1 - y), ((1 - x, 1 - y, c), 2 * (1 - x) + 1 - y)]


def _shard_of(ref, axis, k, n):
    start = pl.multiple_of(k * n, 128)
    return ref.at[pl.ds(start, n), :] if axis == 0 else ref.at[:, pl.ds(start, n)]


ANY = pl.BlockSpec(memory_space=pl.ANY)


class CopyJob:
    def __init__(self, ins, out_shape, scratch, copies, aliases=None):
        self.ins, self.out_shape, self.scratch, self.copies = list(ins), list(out_shape), list(scratch), copies
        self.aliases = dict(aliases or {})

    def start(self, ins, outs, sems):
        local, remote, _, _ = self.copies(ins, outs, sems)
        for d in local + remote:
            d.start()

    def finish(self, ins, outs, sems):
        local, remote, arrivals, relays = self.copies(ins, outs, sems)
        for needs, sends, _ in relays:
            for d in needs:
                d.wait_recv()
            for d in sends:
                d.start()
        for d in arrivals + [d for _, _, arrives in relays for d in arrives]:
            d.wait_recv()
        for d in remote + [d for _, sends, _ in relays for d in sends]:
            d.wait_send()
        for d in local:
            d.wait()


def run_job(job, *, name):
    ni, no = len(job.ins), len(job.out_shape)

    def body(*refs):
        parts = refs[:ni], refs[ni:ni + no], refs[ni + no:]
        job.start(*parts)
        job.finish(*parts)

    return pl.pallas_call(
        body, name=name, in_specs=[ANY] * ni, out_specs=[ANY] * no, out_shape=job.out_shape,
        scratch_shapes=job.scratch, input_output_aliases=job.aliases,
    )(*job.ins)


def _job_args(job, n_in, n_out):
    if job is None:
        return dict(ins=[], in_specs=[], out_specs=[], out_shape=[], scratch=[], aliases={})
    return dict(ins=job.ins, in_specs=[ANY] * len(job.ins), out_specs=[ANY] * len(job.out_shape),
                out_shape=job.out_shape, scratch=job.scratch,
                aliases={n_in + i: n_out + o for i, o in job.aliases.items()})


def _hosting(body, job, n_in, n_out, n_scratch, grid):
    if job is None:
        return body
    ji, jo = len(job.ins), len(job.out_shape)
    grid = (grid,) if isinstance(grid, int) else tuple(grid)

    def at(ends):
        hit = None
        for ax, e in enumerate(ends):
            here = pl.program_id(ax) == e
            hit = here if hit is None else jnp.logical_and(hit, here)
        return hit

    def hosted(*refs):
        o = n_in + ji
        s = o + n_out + jo
        parts = refs[n_in:o], refs[o + n_out:s], refs[s + n_scratch:]

        @pl.when(at([0] * len(grid)))
        def _():
            job.start(*parts)

        body(*refs[:n_in], *refs[o:o + n_out], *refs[s:s + n_scratch])

        @pl.when(at([g - 1 for g in grid]))
        def _():
            job.finish(*parts)

    return hosted


def _job_sems(n_remote, n_local):
    return [pltpu.SemaphoreType.DMA((n_remote,)), pltpu.SemaphoreType.DMA((n_remote,)), pltpu.SemaphoreType.DMA((n_local,))]


def gather_job(shards, axes, chips=(0, 1, 2)):
    na = len(shards)

    def copies(ins, outs, sems):
        send, recv, _ = sems
        x, y, c = _place()
        k = 2 * x + y
        remote, relays = [], []
        for a in range(na):
            r = outs[a].shape[0] // (N_CHIPS if axes[a] == 0 else 1)
            n = outs[a].shape[axes[a]] // N_CHIPS
            half = r // 2

            def part(kk, cc, a=a, n=n, half=half):
                rows = pl.ds(pl.multiple_of(cc * half + (kk * n if axes[a] == 0 else 0), 8), half)
                return outs[a].at[rows, :] if axes[a] == 0 else outs[a].at[rows, pl.ds(pl.multiple_of(kk * n, 128), n)]

            needs, passes, lands = [], [], []
            for j, (peer, kp) in enumerate(_chip_peers(x, y, c)):
                if j not in chips:
                    continue
                s = 6 * a + j
                remote.append(pltpu.make_async_remote_copy(part(k, c), part(k, c), send.at[s], recv.at[s],
                                                           device_id=peer, device_id_type=MESH))
                needs.append(pltpu.make_async_remote_copy(part(kp, c), part(kp, c), send.at[s], recv.at[s],
                                                          device_id=peer, device_id_type=MESH))
                passes.append(pltpu.make_async_remote_copy(part(kp, c), part(kp, c), send.at[s + 3], recv.at[s + 3],
                                                           device_id=(x, y, 1 - c), device_id_type=MESH))
                lands.append(pltpu.make_async_remote_copy(part(kp, 1 - c), part(kp, 1 - c), send.at[s + 3], recv.at[s + 3],
                                                          device_id=(x, y, 1 - c), device_id_type=MESH))
            relays.append((needs, passes, lands))
        return [], remote, [], relays

    out_shape = [jax.ShapeDtypeStruct(w.shape, BF16) for w in shards]
    return CopyJob(shards, out_shape, _job_sems(6 * na, 1), copies, {a: a for a in range(na)})


def scatter_job(layers, g16, axes, filled, chips=(0, 1, 2)):
    na = len(axes)

    def shard_shape(a):
        r, c = g16[a].shape
        return (r // N_CHIPS, c) if axes[a] == 0 else (r, c // N_CHIPS)

    def copies(ins, outs, sems):
        send, recv_sems, _ = sems
        x, y, c = _place()
        remote = []
        for a in range(na):
            n = shard_shape(a)[axes[a]]
            for r, (peer, kp) in enumerate(_chip_peers(x, y, c)):
                if r not in chips:
                    continue
                remote.append(pltpu.make_async_remote_copy(_shard_of(ins[a], axes[a], kp, n), outs[a].at[r, layers[a]],
                                                           send.at[3 * a + r], recv_sems.at[3 * a + r],
                                                           device_id=peer, device_id_type=MESH))
        return [], remote, remote, []

    out_shape = [jax.ShapeDtypeStruct((3, DEPTH) + shard_shape(a), BF16) for a in range(na)]
    ins = list(g16)
    aliases = {}
    for a in range(na):
        if filled[a] is not None:
            aliases[len(ins)] = a
            ins.append(filled[a])
    return CopyJob(ins, out_shape, _job_sems(3 * na, 1), copies, aliases)


def pair_job(g16, axes):
    na = len(axes)
    pieces = [1 if ax == 1 else N_CHIPS for ax in axes]

    def copies(ins, outs, sems):
        send, recv, _ = sems
        x, y, c = _place()
        remote = []
        s = 0
        for a in range(na):
            rows = g16[a].shape[0] // (2 * pieces[a])
            for kk in range(pieces[a]):
                src = ins[a].at[pl.ds(pl.multiple_of((2 * kk + 1 - c) * rows, 8), rows), :]
                remote.append(pltpu.make_async_remote_copy(src, outs[a].at[pl.ds(kk * rows, rows), :], send.at[s], recv.at[s],
                                                           device_id=(x, y, 1 - c), device_id_type=MESH))
                s += 1
        return [], remote, remote, []

    out_shape = [jax.ShapeDtypeStruct((g.shape[0] // 2, g.shape[1]), BF16) for g in g16]
    return CopyJob(g16, out_shape, _job_sems(sum(pieces), 1), copies)


def join_job(shards):
    na = len(shards)

    def copies(ins, outs, sems):
        send, recv, _ = sems
        x, y, c = _place()
        remote = [pltpu.make_async_remote_copy(outs[a].at[:, c], outs[a].at[:, c], send.at[a], recv.at[a],
                                               device_id=(x, y, 1 - c), device_id_type=MESH) for a in range(na)]
        lands = [pltpu.make_async_remote_copy(outs[a].at[:, 1 - c], outs[a].at[:, 1 - c], send.at[a], recv.at[a],
                                              device_id=(x, y, 1 - c), device_id_type=MESH) for a in range(na)]
        return [], remote, lands, []

    out_shape = [jax.ShapeDtypeStruct(s.shape, F32) for s in shards]
    return CopyJob(shards, out_shape, _job_sems(na, 1), copies, {a: a for a in range(na)})


def small_job(p):
    def copies(ins, outs, sems):
        send, recv, loc = sems
        x, y, c = _place()
        me = 4 * x + 2 * y + c
        remote, lands = [], []
        for rel in range(1, 8):
            dx, dy, dc = rel >> 2, (rel >> 1) & 1, rel & 1
            peer = (1 - x if dx else x, 1 - y if dy else y, 1 - c if dc else c)
            who = 4 * peer[0] + 2 * peer[1] + peer[2]
            remote.append(pltpu.make_async_remote_copy(ins[0], outs[0].at[me], send.at[rel - 1], recv.at[rel - 1],
                                                       device_id=peer, device_id_type=MESH))
            lands.append(pltpu.make_async_remote_copy(ins[0], outs[0].at[who], send.at[rel - 1], recv.at[rel - 1],
                                                      device_id=peer, device_id_type=MESH))
        return [pltpu.make_async_copy(ins[0], outs[0].at[me], loc.at[0])], remote, lands, []

    return CopyJob([p], [jax.ShapeDtypeStruct((8,) + p.shape, F32)], _job_sems(7, 1), copies)


def small_sum(slots):
    def add(*terms):
        acc = terms[0]
        for t in terms[1:]:
            acc = acc + t
        return (acc,)

    return _rows_call(add, [(slots, d) for d in range(8)], [F32], name="small_sum", tr=8 * 47)[0]


BIG = ("w_in", "p_ret", "p_sb", "p_sgu", "w_out", "w_up", "w_down")
BIG_AXIS = {"w_in": 1, "p_ret": 1, "p_sb": 1, "p_sgu": 1, "w_out": 0, "w_up": 1, "w_down": 0}
SMALL = ("ret_gn_g", "ret_gn_b", "sgu_ln_g", "sgu_ln_b", "sgu_w", "sgu_b", "ln1_g", "ln1_b", "ln2_g", "ln2_b")


def layer_forward(l, x0, W, sm, rope, rconsts, hooks):
    n = f"l{l}_"
    job = hooks.fwd_job(l, "proj")
    proj = matmul(x0, W["w_in"], mode="nn", tm=2048, tn=640, tk=1024, name=n + "proj", job=job)
    if job is not None:
        proj, job_out = proj
        hooks.done(job, job_out)
    retg, raw, states = ret_fwd(proj, *rope, rconsts, sm["ret_gn_g"], sm["ret_gn_b"], name=n + "ret_fwd")
    job = hooks.fwd_job(l, "sb")
    sb, job_out = sb_fwd(proj, name=n + "sb_fwd", job=job)
    if job is not None:
        hooks.done(job, job_out)
    sg = sgu_fwd(proj, sm["sgu_ln_g"], sm["sgu_ln_b"], sm["sgu_w"], sm["sgu_bias"], name=n + "sgu_fwd")
    merged, r1, r2, r3 = merge_fwd(retg, sb, sg, W["p_ret"], W["p_sb"], W["p_sgu"], proj, name=n + "merge_fwd")
    x1, xh1, rs1 = matmul_ln(merged, W["w_out"], x0, sm["ln1_g"], sm["ln1_b"], tk=1024, name=n + "out_ln1")
    job = hooks.fwd_job(l, "up")
    h1 = matmul(x1, W["w_up"], mode="nn", tm=1024, tn=1024, tk=1024, name=n + "up", job=job)
    if job is not None:
        h1, job_out = h1
        hooks.done(job, job_out)
    job = hooks.fwd_job(l, "down")
    res = matmul_ln(h1, W["w_down"], x1, sm["ln2_g"], sm["ln2_b"], pro=_relu2, tk=1024, name=n + "down_ln2", job=job)
    if job is not None:
        res, job_out = res
        hooks.done(job, job_out)
    x2, xh2, rs2 = res
    saved = dict(x0=x0, proj=proj, retg=retg, raw=raw, states=states, sb=sb, sg=sg, merged=merged, r=(r1, r2, r3),
                 x1=x1, xh1=xh1, rs1=rs1, h1=h1, xh2=xh2, rs2=rs2)
    return x2, saved


def layer_backward(l, dx2, s, W, sm, rope, rconsts, hooks):
    n = f"l{l}_"
    two = ((F32, None), (BF16, None))
    gw, gs = {}, {}
    du2, du2h, gs["ln2_g"], gs["ln2_b"] = ln_bwd(dx2, s["xh2"], s["rs2"], sm["ln2_g"], name=n + "ln2_bwd")
    job = hooks.bwd_job(l, "g_down")
    gw["w_down"] = matmul(s["h1"], du2h, mode="tn", tm=1024, tn=1024, tk=512, pro=_relu2, outs=two, name=n + "g_down", job=job)
    if job is not None:
        gw["w_down"], job_out = gw["w_down"]
        hooks.done(job, job_out)
    dh1 = matmul(du2h, W["w_down"], mode="nt", tm=1024, tn=1024, tk=1024, outs=((BF16, None),),
                 epi=lambda acc, h: (acc * (2.0 * jnp.maximum(h, 0.0)),), tiles=(s["h1"],), name=n + "d_h1")
    job = hooks.bwd_job(l, "g_up")
    gw["w_up"] = matmul(s["x1"], dh1, mode="tn", tm=1024, tn=1024, tk=512, outs=two, name=n + "g_up", job=job)
    if job is not None:
        gw["w_up"], job_out = gw["w_up"]
        hooks.done(job, job_out)
    dx1 = matmul(dh1, W["w_up"], mode="nt", tm=1024, tn=1024, tk=1024,
                 epi=lambda acc, d: (acc + ALPHA * d,), tiles=(du2,), name=n + "d_x1")
    du1, du1h, gs["ln1_g"], gs["ln1_b"] = ln_bwd(dx1, s["xh1"], s["rs1"], sm["ln1_g"], name=n + "ln1_bwd")
    gw["w_out"] = matmul(s["merged"], du1h, mode="tn", tm=1024, tn=1024, tk=512, outs=two, name=n + "g_out")
    dmerged = matmul(du1h, W["w_out"], mode="nt", tm=1024, tn=1024, tk=1024, name=n + "d_merged")
    dr1, dr2, dr3, dg1, dg2, dg3 = merge_bwd(dmerged, *s["r"], s["proj"], name=n + "merge_bwd")
    d_branch = {}
    for nm, a, dr in (("p_ret", s["retg"], dr1), ("p_sb", s["sb"], dr2), ("p_sgu", s["sg"], dr3)):
        gw[nm] = matmul(a, dr, mode="tn", tm=512, tn=1024, tk=512, outs=two, name=n + "g_" + nm)
        d_branch[nm] = matmul(dr, W[nm], mode="nt", tm=1024, tn=512, tk=1024, name=n + "d_" + nm)
    job = hooks.pair(l, gw)
    dret, gs["ret_gn_g"], gs["ret_gn_b"], job_out = ret_bwd(s["proj"], *rope, rconsts, sm["ret_gn_g"], sm["ret_gn_b"],
                                                             s["raw"], s["states"], d_branch["p_ret"], name=n + "ret_bwd", job=job)
    if job is not None:
        hooks.done(job, job_out)
    job = hooks.scatter(l) if job is not None else None
    dsq, dsk, dsv, job_out = sb_bwd(s["proj"], s["sb"], d_branch["p_sb"], name=n + "sb_bwd", job=job)
    if job is not None:
        hooks.done(job, job_out)
    dsgu, gs["sgu_w"], dbias, gs["sgu_ln_g"], gs["sgu_ln_b"] = sgu_bwd(
        s["proj"], sm["sgu_ln_g"], sm["sgu_ln_b"], sm["sgu_w"], sm["sgu_bias"], d_branch["p_sgu"], name=n + "sgu_bwd")
    gs["sgu_b"] = dbias[:, :, 0]
    dproj = jnp.concatenate([dret, dsq, dsk, dsv, dsgu, dg1, dg2, dg3], axis=1)
    job = hooks.small(l, gs)
    gw["w_in"] = matmul(s["x0"], dproj, mode="tn", tm=1024, tn=1920, tk=512, outs=two, name=n + "g_in", job=job)
    if job is not None:
        gw["w_in"], job_out = gw["w_in"]
        hooks.done(job, job_out)
    job = hooks.tail(l, gw["w_in"])
    dx0 = matmul(dproj, W["w_in"], mode="nt", tm=1024, tn=1024, tk=1536,
                 epi=lambda acc, d: (acc + ALPHA * d,), tiles=(du1,), name=n + "d_x0", job=job)
    if job is not None:
        dx0, job_out = dx0
        hooks.done(job, job_out)
    return dx0, gw, gs


def local_step(x, target, small, plan):
    T = x.shape[0]
    rope = _rope_tables(T)
    rconsts = _ret_consts()
    sms = []
    for l in range(DEPTH):
        sm = {k: small[k][l][None, :] for k in SMALL if k not in ("sgu_w", "sgu_b")}
        sm["sgu_w"] = small["sgu_w"][l]
        sm["sgu_bias"] = jnp.broadcast_to(small["sgu_b"][l][:, :, None], (4, CHUNK, CHUNK))
        sms.append(sm)
    h, saved = x, []
    for l in range(DEPTH):
        h, s = layer_forward(l, h, plan.weights(l), sms[l], rope, rconsts, plan)
        saved.append(s)
    dy, sq = loss_head(h, target)
    gs = {k: [None] * DEPTH for k in SMALL}
    for l in reversed(range(DEPTH)):
        dy, gwl, gsl = layer_backward(l, dy, saved[l], plan.weights(l), sms[l], rope, rconsts, plan)
        plan.grads(l, gwl)
        for k in SMALL:
            gs[k][l] = gsl[k].reshape(small[k].shape[1:])
    return sq[0, 0], dy, {k: jnp.stack(v) for k, v in gs.items()}


EARLY_GRADS = ("p_ret", "p_sb", "p_sgu", "w_out", "w_up", "w_down")


class _StepPlan:
    def __init__(self, pos, shards16):
        self.pos = pos
        self.shards16 = shards16
        self.full = [dict() for _ in range(DEPTH)]
        self.gw = [None] * DEPTH
        self.bufs = {}
        self.sums = {}
        self.gs = [None] * DEPTH
        first = self._gather([(0, "w_in")])
        self.done(first, run_job(first, name="gather_first"))

    def weights(self, l):
        return self.full[l]

    def grads(self, l, gw):
        self.gw[l] = gw

    def _gather(self, items, chips=(0, 1, 2)):
        job = gather_job([self.shards16[l][k] for l, k in items], [BIG_AXIS[k] for _, k in items], chips)
        job.note = ("gather" if 2 in chips else "gather_part", items)
        return job

    def _pair(self, items):
        job = pair_job([g[1] for _, _, g in items], [BIG_AXIS[k] for _, k, _ in items])
        job.note = ("pair", items)
        return job

    def fwd_job(self, l, host):
        if host == "proj":
            return self._gather([(l, "w_up"), (l, "w_down")])
        if host == "sb":
            return self._gather([(l, k) for k in ("p_ret", "p_sb", "p_sgu", "w_out")])
        if l + 1 == DEPTH:
            return None
        return self._gather([(l + 1, "w_in")], (0, 1) if host == "up" else (2,))

    def bwd_job(self, l, host):
        if l + 1 == DEPTH:
            return None
        items, sums16 = self.summed_w_in
        job = scatter_job([l_ for l_, _, _ in items], sums16, [BIG_AXIS[k] for _, k, _ in items],
                          [self.bufs.get(k) for _, k, _ in items], (0, 1) if host == "g_down" else (2,))
        job.note = ("scatter", items)
        return job

    def pair(self, l, ready):
        return self._pair([(l, k, ready[k]) for k in EARLY_GRADS])

    def scatter(self, l):
        items, sums16 = self.summed
        job = scatter_job([l_ for l_, _, _ in items], sums16, [BIG_AXIS[k] for _, k, _ in items],
                          [self.bufs.get(k) for _, k, _ in items])
        job.note = ("scatter", items)
        return job

    def small(self, l, gs):
        self.gs[l] = {k: gs[k].reshape(-1) for k in SMALL}
        if l != 0:
            return None
        job = small_job(_pack_small({k: jnp.stack([self.gs[l_][k] for l_ in range(DEPTH)]) for k in SMALL}))
        job.note = ("small", [])
        return job

    def tail(self, l, g):
        if l != 0:
            job = self._pair([(l, "w_in", g)])
            job.note = ("pair_w_in", job.note[1])
            return job
        last = self._pair([(0, "w_in", g)])
        self.done(last, run_job(last, name="pair_last"))
        return self.scatter(0)

    def done(self, job, outs):
        kind, items = job.note
        if kind == "small":
            self.small_slots = outs[0]
        if kind in ("pair", "pair_w_in"):
            sums16 = []
            for a, (l, k, g) in enumerate(items):
                self.sums[(l, k)], s16 = pair_sum(self.pos, outs[a], g[0], BIG_AXIS[k], name=f"pair_sum_{k}_{l}")
                sums16.append(s16)
            if kind == "pair":
                self.summed = (items, sums16)
            else:
                self.summed_w_in = (items, sums16)
        for a, item in enumerate(items):
            if kind == "gather_part":
                self.shards16[item[0]][item[1]] = outs[a]
            elif kind == "gather":
                self.full[item[0]][item[1]] = outs[a]
            elif kind == "scatter":
                self.bufs[item[1]] = outs[a]

    def finish(self):
        return self.bufs, self.sums


def _flat2(a):
    return a.reshape(-1, a.shape[-1])


def _pack_small(d, pre=""):
    return jnp.concatenate([d[pre + k].reshape(-1) for k in SMALL]).reshape(-1, 128)


def kernel(x, w_in, ret_gn_g, ret_gn_b, sgu_ln_g, sgu_ln_b, sgu_w, sgu_b, p_ret, p_sb, p_sgu, w_out, ln1_g, ln1_b, w_up, w_down, ln2_g, ln2_b, loss_target, m_w_in, m_ret_gn_g, m_ret_gn_b, m_sgu_ln_g, m_sgu_ln_b, m_sgu_w, m_sgu_b, m_p_ret, m_p_sb, m_p_sgu, m_w_out, m_ln1_g, m_ln1_b, m_w_up, m_w_down, m_ln2_g, m_ln2_b, v_w_in, v_ret_gn_g, v_ret_gn_b, v_sgu_ln_g, v_sgu_ln_b, v_sgu_w, v_sgu_b, v_p_ret, v_p_sb, v_p_sgu, v_w_out, v_ln1_g, v_ln1_b, v_w_up, v_w_down, v_ln2_g, v_ln2_b):
    given = dict(locals())
    order = BIG[:1] + SMALL[:6] + BIG[1:5] + SMALL[6:8] + BIG[5:7] + SMALL[8:10]
    L = DEPTH

    px, py, pc = _place()
    pos = jnp.stack([px, py, pc, 2 * px + py]).astype(jnp.int32)

    shards16 = [{k: cast_into_whole(pos, given[k], l, BIG_AXIS[k], name=f"cast_{k}_{l}") for k in BIG} for l in range(L)]
    plan = _StepPlan(pos, shards16)
    sq, dx, gs = local_step(x[0], loss_target[0], {k: given[k] for k in SMALL}, plan)
    loss = 0.5 * lax.psum(sq, ("x", "y", "c"))

    bufs, sums = plan.finish()
    shards = []
    for k in BIG:
        whole = None
        for l in range(L):
            whole = chip_sum(pos, sums[(l, k)], bufs[k], l, BIG_AXIS[k], whole, name=f"chip_sum_{k}_{l}")
        shards.append(whole)
    joined = run_job(join_job(shards), name="join_halves")
    out = {}
    for a, k in enumerate(BIG):
        shp = given[k].shape
        res = _rows_call(lambda g_, w_, m_, v_: (g_,) + _adamw(w_, g_, m_, v_),
                         [joined[a].reshape(-1, shp[-1]), _flat2(given[k]), _flat2(given["m_" + k]), _flat2(given["v_" + k])],
                         [F32] * 4, name="adamw_" + k)
        out[k] = [r.reshape(shp) for r in res]

    pack = _pack_small
    res = _rows_call(lambda g_, w_, m_, v_: (g_,) + _adamw(w_, g_, m_, v_),
                     [small_sum(plan.small_slots), pack(given), pack(given, "m_"), pack(given, "v_")], [F32] * 4,
                     name="adamw_small", tr=8 * 47)
    off = 0
    for k in SMALL:
        sz = given[k].size
        out[k] = [r.reshape(-1)[off:off + sz].reshape(given[k].shape) for r in res]
        off += sz

    grads = [out[k][0] for k in order]
    deltas = [out[k][1] for k in order]
    new_m = [out[k][2] for k in order]
    new_v = [out[k][3] for k in order]
    return (loss, dx[None], *grads, *deltas, *new_m, *new_v)
```

```python
import functools
import math

import jax
import jax.numpy as jnp
from jax import lax
from jax.experimental import pallas as pl
from jax.experimental.pallas import tpu as pltpu

F32 = jnp.float32
BF16 = jnp.bfloat16

D_MODEL = 1024
SEQ = 4096
DEPTH = 2
CHUNK = 128
RET_HEADS = 4
BRANCH_W = 512
N_IN = 7680
D_FF = 4096
LN_EPS = 1e-5
ROPE_BASE = 10000.0
ALPHA = (2 * DEPTH) ** 0.25
RET_SCALE = 128 ** -0.5
SB_SCALE = 64 ** -0.5
C_RET, C_SB, C_SGU, C_GATE = 0, 2048, 3584, 4608

ADAM_LR, ADAM_B1, ADAM_B2, ADAM_EPS, ADAM_WD, ADAM_STEP = 0.001, 0.9, 0.999, 1e-08, 0.01, 10

N_CHIPS = 4
VMEM_LIMIT = 56 * 1024 * 1024
MESH = pl.DeviceIdType.MESH

NN = ((1,), (0,))
NT = ((1,), (1,))
TN = ((0,), (0,))


def _dot(a, b, dims):
    return lax.dot_general(a, b, (dims, ((), ())), preferred_element_type=F32)


def _params(sem):
    return pltpu.CompilerParams(dimension_semantics=sem, vmem_limit_bytes=VMEM_LIMIT)


def _relu2(h):
    r = jnp.maximum(h, 0.0)
    return r * r


def matmul(a, b, *, mode, tm, tn, tk, outs=((F32, None),), pro=None, epi=None, tiles=(), rows=(), name, job=None):
    if mode == "nn":
        (M, K), N = a.shape, b.shape[1]
    elif mode == "nt":
        (M, K), N = a.shape, b.shape[0]
    else:
        (K, M), N = a.shape, b.shape[1]
    tm, tn, tk = min(tm, M), min(tn, N), min(tk, K)
    assert M % tm == 0 and N % tn == 0 and K % tk == 0, (name, M, N, K, tm, tn, tk)
    if mode == "nn":
        a_spec = pl.BlockSpec((tm, tk), lambda i, j, k: (i, k))
        b_spec = pl.BlockSpec((tk, tn), lambda i, j, k: (k, j))
        dims = NN
    elif mode == "nt":
        a_spec = pl.BlockSpec((tm, tk), lambda i, j, k: (i, k))
        b_spec = pl.BlockSpec((tn, tk), lambda i, j, k: (j, k))
        dims = NT
    else:
        a_spec = pl.BlockSpec((tk, tm), lambda i, j, k: (k, i))
        b_spec = pl.BlockSpec((tk, tn), lambda i, j, k: (k, j))
        dims = TN
    nk = K // tk
    nt_, nr, no = len(tiles), len(rows), len(outs)

    def body(a_ref, b_ref, *rest):
        tile_refs = rest[:nt_]
        row_refs = rest[nt_:nt_ + nr]
        out_refs = rest[nt_ + nr:nt_ + nr + no]
        av = a_ref[...]
        if pro is not None:
            av = pro(av)
        p = _dot(av.astype(BF16), b_ref[...].astype(BF16), dims)

        def finish(acc):
            vals = (acc,) * no if epi is None else epi(acc, *[r[...] for r in tile_refs], *[r[...] for r in row_refs])
            for o_ref, v in zip(out_refs, vals):
                o_ref[...] = v.astype(o_ref.dtype)

        if nk == 1:
            finish(p)
        else:
            acc_ref = rest[-1]
            k = pl.program_id(2)

            @pl.when(k == 0)
            def _():
                acc_ref[...] = p

            @pl.when(k > 0)
            def _():
                acc_ref[...] += p

            @pl.when(k == nk - 1)
            def _():
                finish(acc_ref[...])

    out_shape, out_specs = [], []
    for dt, width in outs:
        if width is None:
            out_shape.append(jax.ShapeDtypeStruct((M, N), dt))
            out_specs.append(pl.BlockSpec((tm, tn), lambda i, j, k: (i, j)))
        else:
            assert N == tn
            out_shape.append(jax.ShapeDtypeStruct((M, width), dt))
            out_specs.append(pl.BlockSpec((tm, width), lambda i, j, k: (i, 0)))
    in_specs = [a_spec, b_spec]
    in_specs += [pl.BlockSpec((tm, tn), lambda i, j, k: (i, j)) for _ in tiles]
    in_specs += [pl.BlockSpec((1, tn), lambda i, j, k: (0, j)) for _ in rows]
    grid = (M // tm, N // tn, nk)
    scratch = [pltpu.VMEM((tm, tn), F32)] if nk > 1 else []
    j = _job_args(job, len(in_specs), no)
    res = pl.pallas_call(
        _hosting(body, job, len(in_specs), no, len(scratch), grid), name=name, grid=grid,
        in_specs=in_specs + j["in_specs"], out_specs=out_specs + j["out_specs"], out_shape=out_shape + j["out_shape"],
        scratch_shapes=scratch + j["scratch"], input_output_aliases=j["aliases"],
        compiler_params=_params(("parallel", "parallel", "arbitrary") if job is None else ("arbitrary",) * 3),
    )(a, b, *tiles, *rows, *j["ins"])
    mine = res[0] if no == 1 else list(res[:no])
    return mine if job is None else (mine, list(res[no:]))


def _ln_epi(acc, res, g, b):
    u = ALPHA * res + acc
    mu = jnp.mean(u, axis=-1, keepdims=True)
    xc = u - mu
    var = jnp.mean(xc * xc, axis=-1, keepdims=True)
    rstd = lax.rsqrt(var + LN_EPS)
    xhat = xc * rstd
    return xhat * g + b, xhat, jnp.broadcast_to(rstd, (u.shape[0], 128))


def matmul_ln(a, w, res, g, b, *, pro=None, tk, name, job=None):
    n = w.shape[1]
    return matmul(a, w, mode="nn", tm=1024, tn=n, tk=tk, pro=pro, epi=_ln_epi, tiles=(res,), rows=(g, b),
                  outs=((F32, None), (F32, None), (F32, 128)), name=name, job=job)


def ln_bwd(dy, xhat, rstd, g, *, name, job=None):
    T, D = dy.shape
    tm = min(512, T)

    def body(dy_ref, xh_ref, rs_ref, g_ref, du_ref, du16_ref, dg_ref, db_ref):
        dyv, xh = dy_ref[...], xh_ref[...]
        r = rs_ref[:, 0:1]
        dxh = dyv * g_ref[...]
        m1 = jnp.mean(dxh, axis=-1, keepdims=True)
        m2 = jnp.mean(dxh * xh, axis=-1, keepdims=True)
        du = r * (dxh - m1 - xh * m2)
        du_ref[...] = du
        du16_ref[...] = du.astype(BF16)

        @pl.when(pl.program_id(0) == 0)
        def _():
            dg_ref[...] = jnp.zeros_like(dg_ref)
            db_ref[...] = jnp.zeros_like(db_ref)

        dg_ref[...] += jnp.sum(dyv * xh, axis=0, keepdims=True)
        db_ref[...] += jnp.sum(dyv, axis=0, keepdims=True)

    row = pl.BlockSpec((tm, D), lambda i: (i, 0))
    vec = pl.BlockSpec((1, D), lambda i: (0, 0))
    j = _job_args(job, 4, 4)
    res = pl.pallas_call(
        _hosting(body, job, 4, 4, 0, T // tm), name=name, grid=(T // tm,),
        in_specs=[row, row, pl.BlockSpec((tm, 128), lambda i: (i, 0)), vec] + j["in_specs"],
        out_specs=[row, row, vec, vec] + j["out_specs"],
        out_shape=[jax.ShapeDtypeStruct((T, D), F32), jax.ShapeDtypeStruct((T, D), BF16),
                   jax.ShapeDtypeStruct((1, D), F32), jax.ShapeDtypeStruct((1, D), F32)] + j["out_shape"],
        scratch_shapes=j["scratch"], input_output_aliases=j["aliases"],
        compiler_params=_params(("arbitrary",)),
    )(dy, xhat, rstd, g, *j["ins"])
    return list(res[:4]) if job is None else (list(res[:4]), list(res[4:]))


def loss_head(y, target):
    T, D = y.shape
    tm = min(512, T)

    def body(y_ref, t_ref, dy_ref, s_ref):
        e = y_ref[...] - t_ref[...]
        dy_ref[...] = e * (1.0 / D)

        @pl.when(pl.program_id(0) == 0)
        def _():
            s_ref[...] = jnp.zeros_like(s_ref)

        s_ref[...] += jnp.sum(jnp.mean(e * e, axis=-1, keepdims=True))

    row = pl.BlockSpec((tm, D), lambda i: (i, 0))
    return pl.pallas_call(
        body, name="loss_head", grid=(T // tm,),
        in_specs=[row, row], out_specs=[row, pl.BlockSpec((8, 128), lambda i: (0, 0))],
        out_shape=[jax.ShapeDtypeStruct((T, D), F32), jax.ShapeDtypeStruct((8, 128), F32)],
        compiler_params=_params(("arbitrary",)),
    )(y, target)


def _rope_tables(T):
    half = 64
    inv_freq = ROPE_BASE ** (-jnp.arange(half, dtype=F32) / half)
    ang = jnp.arange(T, dtype=jnp.int32).astype(F32)[:, None] * inv_freq[None, :]
    cos, sin = jnp.cos(ang), jnp.sin(ang)
    return jnp.concatenate([cos, cos], axis=1), jnp.concatenate([-sin, sin], axis=1)


def _ret_consts():
    H = RET_HEADS
    log_g = jnp.log(1.0 - 2.0 ** (-5.0 - jnp.arange(H, dtype=F32)))
    idx = jnp.arange(CHUNK, dtype=F32)
    diff = idx[:, None] - idx[None, :]
    dmat = jnp.where(diff[None] >= 0, jnp.exp(log_g[:, None, None] * diff[None]), 0.0)
    kd = jnp.exp(log_g[:, None] * (CHUNK - 1 - idx)[None, :])
    qd = jnp.exp(log_g[:, None] * (idx + 1.0)[None, :])
    cd = jnp.exp(log_g * CHUNK)
    full = (H, CHUNK, CHUNK)
    return (dmat.astype(F32), jnp.broadcast_to(kd[:, :, None], full), jnp.broadcast_to(qd[:, :, None], full),
            jnp.broadcast_to(cd[:, None, None], full))


def _swap_halves(v):
    return pltpu.roll(v, 64, 1)


def _group_norm(o):
    mu = jnp.mean(o, axis=-1, keepdims=True)
    xc = o - mu
    var = jnp.mean(xc * xc, axis=-1, keepdims=True)
    rstd = lax.rsqrt(var + LN_EPS)
    return xc * rstd, rstd


def ret_fwd(proj, cosf, sinf, consts, gn_g, gn_b, *, name):
    T = proj.shape[0]
    tb = min(512, T)
    nch = tb // CHUNK
    H = RET_HEADS

    def body(p_ref, cos_ref, sin_ref, dm_ref, kd_ref, qd_ref, cd_ref, g_ref, b_ref, out_ref, raw_ref, st_ref, s_ref):
        @pl.when(pl.program_id(0) == 0)
        def _():
            s_ref[...] = jnp.zeros_like(s_ref)

        for c in range(nch):
            r = slice(c * CHUNK, (c + 1) * CHUNK)
            cs, sn = cos_ref[r, :], sin_ref[r, :]
            for h in range(H):
                hc = slice(h * 128, (h + 1) * 128)
                q = p_ref[r, h * 128:(h + 1) * 128]
                k = p_ref[r, 512 + h * 128:512 + (h + 1) * 128]
                v = p_ref[r, 1024 + h * 128:1024 + (h + 1) * 128]
                gt = p_ref[r, 1536 + h * 128:1536 + (h + 1) * 128]
                qr = q * cs + _swap_halves(q) * sn
                kr = (k * cs + _swap_halves(k) * sn) * RET_SCALE
                sprev = s_ref[h]
                st_ref[c, h] = sprev
                qb, kb, vb = qr.astype(BF16), kr.astype(BF16), v.astype(BF16)
                s = _dot(qb, kb, NT) * dm_ref[h]
                o = _dot(s.astype(BF16), vb, NN) + _dot((qr * qd_ref[h]).astype(BF16), sprev.astype(BF16), NN)
                s_ref[h] = sprev * cd_ref[h] + _dot((kr * kd_ref[h]).astype(BF16), vb, TN)
                raw_ref[r, hc] = o
                y, _ = _group_norm(o)
                out_ref[r, hc] = (gt * jax.nn.sigmoid(gt)) * (y * g_ref[:, hc] + b_ref[:, hc])

    cmat = pl.BlockSpec((H, CHUNK, CHUNK), lambda i: (0, 0, 0))
    vec = pl.BlockSpec((1, BRANCH_W), lambda i: (0, 0))
    rope = pl.BlockSpec((tb, 128), lambda i: (i, 0))
    blk = pl.BlockSpec((tb, BRANCH_W), lambda i: (i, 0))
    return pl.pallas_call(
        body, name=name, grid=(T // tb,),
        in_specs=[pl.BlockSpec((tb, 2048), lambda i: (i, 0)), rope, rope, cmat, cmat, cmat, cmat, vec, vec],
        out_specs=[blk, blk, pl.BlockSpec((nch, H, CHUNK, CHUNK), lambda i: (i, 0, 0, 0))],
        out_shape=[jax.ShapeDtypeStruct((T, BRANCH_W), F32), jax.ShapeDtypeStruct((T, BRANCH_W), F32),
                   jax.ShapeDtypeStruct((T // CHUNK, H, CHUNK, CHUNK), F32)],
        scratch_shapes=[pltpu.VMEM((H, CHUNK, CHUNK), F32)],
        compiler_params=_params(("arbitrary",)),
    )(proj, cosf, sinf, *consts, gn_g, gn_b)


def ret_bwd(proj, cosf, sinf, consts, gn_g, gn_b, raw, states, dout, *, name, job=None):
    T = proj.shape[0]
    tb = min(512, T)
    nch = tb // CHUNK
    nb = T // tb
    H = RET_HEADS

    def body(p_ref, cos_ref, sin_ref, dm_ref, kd_ref, qd_ref, cd_ref, g_ref, b_ref, raw_ref, st_ref, do_ref,
             dp_ref, dg_ref, db_ref, ds_ref):
        @pl.when(pl.program_id(0) == 0)
        def _():
            ds_ref[...] = jnp.zeros_like(ds_ref)
            dg_ref[...] = jnp.zeros_like(dg_ref)
            db_ref[...] = jnp.zeros_like(db_ref)

        for c in reversed(range(nch)):
            r = slice(c * CHUNK, (c + 1) * CHUNK)
            cs, sn = cos_ref[r, :], sin_ref[r, :]
            for h in range(H):
                hc = slice(h * 128, (h + 1) * 128)
                q = p_ref[r, h * 128:(h + 1) * 128]
                k = p_ref[r, 512 + h * 128:512 + (h + 1) * 128]
                v = p_ref[r, 1024 + h * 128:1024 + (h + 1) * 128]
                gt = p_ref[r, 1536 + h * 128:1536 + (h + 1) * 128]
                qr = q * cs + _swap_halves(q) * sn
                kr = (k * cs + _swap_halves(k) * sn) * RET_SCALE
                sprev = st_ref[c, h]
                gv = g_ref[:, hc]
                y, rstd = _group_norm(raw_ref[r, hc])
                d_out = do_ref[r, hc]
                sg = jax.nn.sigmoid(gt)
                d_gate = d_out * (y * gv + b_ref[:, hc]) * (sg * (1.0 + gt * (1.0 - sg)))
                d_aff = d_out * (gt * sg)
                dg_ref[:, hc] += jnp.sum(d_aff * y, axis=0, keepdims=True)
                db_ref[:, hc] += jnp.sum(d_aff, axis=0, keepdims=True)
                dxh = d_aff * gv
                m1 = jnp.mean(dxh, axis=-1, keepdims=True)
                m2 = jnp.mean(dxh * y, axis=-1, keepdims=True)
                d_o = (rstd * (dxh - m1 - y * m2)).astype(BF16)
                qb, kb, vb = qr.astype(BF16), kr.astype(BF16), v.astype(BF16)
                dm, kd, qd = dm_ref[h], kd_ref[h], qd_ref[h]
                p = (_dot(qb, kb, NT) * dm).astype(BF16)
                dp = (_dot(d_o, vb, NT) * dm).astype(BF16)
                dsn = ds_ref[h]
                dsb = dsn.astype(BF16)
                dq_r = _dot(dp, kb, NN) + _dot(d_o, sprev.astype(BF16), NT) * qd
                dk_r = (_dot(dp, qb, TN) + _dot(vb, dsb, NT) * kd) * RET_SCALE
                d_v = _dot(p, d_o, TN) + _dot((kr * kd).astype(BF16), dsb, NN)
                ds_ref[h] = dsn * cd_ref[h] + _dot((qr * qd).astype(BF16), d_o, TN)
                dp_ref[r, h * 128:(h + 1) * 128] = (dq_r * cs - _swap_halves(dq_r) * sn).astype(BF16)
                dp_ref[r, 512 + h * 128:512 + (h + 1) * 128] = (dk_r * cs - _swap_halves(dk_r) * sn).astype(BF16)
                dp_ref[r, 1024 + h * 128:1024 + (h + 1) * 128] = d_v.astype(BF16)
                dp_ref[r, 1536 + h * 128:1536 + (h + 1) * 128] = d_gate.astype(BF16)

    cmat = pl.BlockSpec((H, CHUNK, CHUNK), lambda i: (0, 0, 0))
    vec = pl.BlockSpec((1, BRANCH_W), lambda i: (0, 0))
    rope = pl.BlockSpec((tb, 128), lambda i: (nb - 1 - i, 0))
    blk = pl.BlockSpec((tb, BRANCH_W), lambda i: (nb - 1 - i, 0))
    wide = pl.BlockSpec((tb, 2048), lambda i: (nb - 1 - i, 0))
    j = _job_args(job, 12, 3)
    res = pl.pallas_call(
        _hosting(body, job, 12, 3, 1, nb), name=name, grid=(nb,),
        in_specs=[wide, rope, rope, cmat, cmat, cmat, cmat, vec, vec, blk,
                  pl.BlockSpec((nch, H, CHUNK, CHUNK), lambda i: (nb - 1 - i, 0, 0, 0)), blk] + j["in_specs"],
        out_specs=[wide, vec, vec] + j["out_specs"],
        out_shape=[jax.ShapeDtypeStruct((T, 2048), BF16), jax.ShapeDtypeStruct((1, BRANCH_W), F32),
                   jax.ShapeDtypeStruct((1, BRANCH_W), F32)] + j["out_shape"],
        scratch_shapes=[pltpu.VMEM((H, CHUNK, CHUNK), F32)] + j["scratch"], input_output_aliases=j["aliases"],
        compiler_params=_params(("arbitrary",)),
    )(proj, cosf, sinf, *consts, gn_g, gn_b, raw, states, dout, *j["ins"])
    return res[0], res[1], res[2], list(res[3:])


def _sb_masks():
    row = lax.broadcasted_iota(jnp.int32, (CHUNK, CHUNK), 0)
    lane = lax.broadcasted_iota(jnp.int32, (CHUNK, CHUNK), 1)
    return row, lane


SB_QT = 256
SB_DEAD = -105.0


def _pair(v):
    hi = v.astype(BF16)
    return jnp.concatenate([hi, (v - hi.astype(F32)).astype(BF16)], axis=1)


def _sb_consts():
    r = lax.broadcasted_iota(jnp.int32, (256, 256), 0) & 127
    c = lax.broadcasted_iota(jnp.int32, (256, 256), 1)
    ones = c >= 128
    lane = lax.broadcasted_iota(jnp.int32, (CHUNK, CHUNK), 1)
    return (ones | (r > c)).astype(BF16), (ones | (r >= c)).astype(BF16), (lane < 64, lane >= 64)


def _per_head(x, hms):
    return jnp.concatenate([jnp.where(hm, x, 0.0) for hm in hms], axis=0).astype(BF16)


def _sb_logits(qb, kb2, mask2):
    z = _dot(qb, kb2, NT)
    l1p = jnp.log(1.0 + jnp.exp(-jnp.abs(z)))
    lsp = jnp.minimum(z, 0.0) - l1p
    lsn = lsp - z
    if mask2 is not None:
        lsn = jnp.where(mask2, lsn, 0.0)
    return lsp, lsn


def _sb_tile_mask(qt):
    trow = lax.broadcasted_iota(jnp.int32, (qt, 256), 0)
    tlane = lax.broadcasted_iota(jnp.int32, (qt, 256), 1) & 127
    return lambda m: (tlane + m * CHUNK) < trow


def sb_fwd(proj, *, name, job=None):
    T = proj.shape[0]
    qt = min(SB_QT, T)
    nsub = qt // CHUNK
    cb = C_SB // 128

    def body(q_ref, k_ref, v_ref, o_ref):
        u_gt, _, hms = _sb_consts()
        tile_mask = _sb_tile_mask(qt)

        def qtile(i, _):
            rq = pl.ds(pl.multiple_of(i * qt, qt), qt)
            qb = (q_ref[rq, :] * SB_SCALE).astype(BF16)

            def group(js, masks, state):
                carry, acc = list(state[:2]), state[2]
                rows = [pl.ds(pl.multiple_of(j * CHUNK, CHUNK), CHUNK) for j in js]
                logits = [_sb_logits(qb, _per_head(k_ref[rk, :], hms), m) for rk, m in zip(rows, masks)]
                sums = [[_dot(_pair(lsn[:, h * 128:(h + 1) * 128]), u_gt, NN) for h in range(2)] for _, lsn in logits]
                weights = []
                for (lsp, _), r, m in zip(logits, sums, masks):
                    a_b = []
                    for h in range(2):
                        hc = slice(h * 128, (h + 1) * 128)
                        a = jnp.exp(lsp[:, hc] + r[h][:, :128] + carry[h])
                        if m is not None:
                            a = jnp.where(m[:, hc], a, 0.0)
                        carry[h] = carry[h] + r[h][:, 128:]
                        a_b.append(a.astype(BF16))
                    weights.append(jnp.concatenate(a_b, axis=1))
                for rk, a in zip(rows, weights):
                    acc = acc + _dot(a, _per_head(v_ref[rk, :], hms), NN)
                return carry[0], carry[1], acc

            zero = jnp.zeros((qt, 128), F32)
            diag = list(reversed(range(nsub)))
            state = group([i * nsub + m for m in diag], [tile_mask(m) for m in diag], (zero, zero, zero))

            def live(c):
                return jnp.logical_and(c[0] < i, jnp.maximum(jnp.max(c[1][0]), jnp.max(c[1][1])) > SB_DEAD)

            def blocks(c):
                jj, st = c
                return jj + 1, group([(i - jj) * nsub - 1 - u for u in range(nsub)], [None] * nsub, st)

            _, state = lax.while_loop(live, blocks, (jnp.int32(0), state))
            o_ref[rq, :] = state[2]
            return 0

        lax.fori_loop(0, T // qt, qtile, 0)

    def col(off):
        return pl.BlockSpec((T, 128), lambda hp: (0, off + hp))

    steps = BRANCH_W // 128
    j = _job_args(job, 3, 1)
    res = pl.pallas_call(
        _hosting(body, job, 3, 1, 0, steps), name=name, grid=(steps,),
        in_specs=[col(cb), col(cb + 4), col(cb + 8)] + j["in_specs"], out_specs=[col(0)] + j["out_specs"],
        out_shape=[jax.ShapeDtypeStruct((T, BRANCH_W), F32)] + j["out_shape"],
        scratch_shapes=j["scratch"], input_output_aliases=j["aliases"],
        compiler_params=_params(("parallel",) if job is None else ("arbitrary",)),
    )(proj, proj, proj, *j["ins"])
    return res[0], list(res[1:])


def sb_bwd(proj, out, dout, *, name, job=None):
    T = proj.shape[0]
    qt = min(SB_QT, T)
    nsub = qt // CHUNK
    cb = C_SB // 128

    def body(q_ref, k_ref, v_ref, o_ref, do_ref, dq_ref, dk_ref, dv_ref, dkt_ref, dvt_ref):
        u_gt, u_ge, hms = _sb_consts()
        tile_mask = _sb_tile_mask(qt)
        tall_lane = lax.broadcasted_iota(jnp.int32, (qt, 128), 1)
        top = lax.broadcasted_iota(jnp.int32, (CHUNK, CHUNK), 0) < 64
        dkt_ref[...] = jnp.zeros_like(dkt_ref)
        dvt_ref[...] = jnp.zeros_like(dvt_ref)

        def qtile(i, _):
            rq = pl.ds(pl.multiple_of(i * qt, qt), qt)
            qs = q_ref[rq, :] * SB_SCALE
            qb, q_t = qs.astype(BF16), qs.T.astype(BF16)
            dov = do_ref[rq, :]
            dob, do_t = dov.astype(BF16), dov.T.astype(BF16)
            prod = dob.astype(F32) * o_ref[rq, :]
            total = [jnp.broadcast_to(jnp.sum(jnp.where(hm, prod, 0.0), axis=1, keepdims=True), (qt, 128))
                     for hm in (tall_lane < 64, tall_lane >= 64)]

            def group(js, masks, state):
                c_l, c_w, dq = list(state[:2]), list(state[2:4]), state[4]
                heads = [slice(h * 128, (h + 1) * 128) for h in range(2)]
                rows = [pl.ds(pl.multiple_of(j * CHUNK, CHUNK), CHUNK) for j in js]
                kb2 = [_per_head(k_ref[rk, :], hms) for rk in rows]
                logits = [_sb_logits(qb, kb, m) for kb, m in zip(kb2, masks)]
                da = [_dot(dob, _per_head(v_ref[rk, :], hms), NT) for rk in rows]
                sums = [[_dot(_pair(lsn[:, hc]), u_gt, NN) for hc in heads] for _, lsn in logits]
                a_b, w_all = [], []
                for (lsp, _), r, d, m in zip(logits, sums, da, masks):
                    a_h, w_h = [], []
                    for h, hc in enumerate(heads):
                        a = jnp.exp(lsp[:, hc] + r[h][:, :128] + c_l[h])
                        if m is not None:
                            a = jnp.where(m[:, hc], a, 0.0)
                        c_l[h] = c_l[h] + r[h][:, 128:]
                        a = a.astype(BF16)
                        a_h.append(a)
                        w_h.append(a.astype(F32) * d[:, hc])
                    a_b.append(jnp.concatenate(a_h, axis=1))
                    w_all.append(w_h)
                sums_w = [[_dot(_pair(w), u_ge, NN) for w in w_h] for w_h in w_all]
                dz_b = []
                for (lsp, _), w_h, r, m in zip(logits, w_all, sums_w, masks):
                    sp = jnp.exp(lsp)
                    dz_h = []
                    for h, hc in enumerate(heads):
                        later_w = r[h][:, :128] + c_w[h]
                        c_w[h] = c_w[h] + r[h][:, 128:]
                        dz = w_h[h] * (1.0 - sp[:, hc]) - sp[:, hc] * (total[h] - later_w)
                        if m is not None:
                            dz = jnp.where(m[:, hc], dz, 0.0)
                        dz_h.append(dz.astype(BF16))
                    dz_b.append(jnp.concatenate(dz_h, axis=1))
                for j, kb, a, dz in zip(js, kb2, a_b, dz_b):
                    dkt = _dot(q_t, dz, NN)
                    dvt = _dot(do_t, a, NN)
                    dkt_ref[j] += jnp.where(top, dkt[:, :128], dkt[:, 128:])
                    dvt_ref[j] += jnp.where(top, dvt[:, :128], dvt[:, 128:])
                    dq = dq + _dot(dz, kb, NN)
                return c_l[0], c_l[1], c_w[0], c_w[1], dq

            zero = jnp.zeros((qt, 128), F32)
            diag = list(reversed(range(nsub)))
            state = group([i * nsub + m for m in diag], [tile_mask(m) for m in diag], (zero,) * 5)

            def live(c):
                return jnp.logical_and(c[0] < i, jnp.maximum(jnp.max(c[1][0]), jnp.max(c[1][1])) > SB_DEAD)

            def blocks(c):
                jj, st = c
                return jj + 1, group([(i - jj) * nsub - 1 - u for u in range(nsub)], [None] * nsub, st)

            _, state = lax.while_loop(live, blocks, (jnp.int32(0), state))
            dq_ref[rq, :] = (state[4] * SB_SCALE).astype(BF16)
            return 0

        lax.fori_loop(0, T // qt, qtile, 0)

        def untranspose(jb, _):
            rk = pl.ds(pl.multiple_of(jb * CHUNK, CHUNK), CHUNK)
            dk_ref[rk, :] = dkt_ref[jb].T.astype(BF16)
            dv_ref[rk, :] = dvt_ref[jb].T.astype(BF16)
            return 0

        lax.fori_loop(0, T // CHUNK, untranspose, 0)

    def col(off):
        return pl.BlockSpec((T, 128), lambda hp: (0, off + hp))

    o16 = jax.ShapeDtypeStruct((T, BRANCH_W), BF16)
    steps = BRANCH_W // 128
    j = _job_args(job, 5, 3)
    acc = pltpu.VMEM((T // CHUNK, CHUNK, CHUNK), F32)
    res = pl.pallas_call(
        _hosting(body, job, 5, 3, 2, steps), name=name, grid=(steps,),
        in_specs=[col(cb), col(cb + 4), col(cb + 8), col(0), col(0)] + j["in_specs"],
        out_specs=[col(0), col(0), col(0)] + j["out_specs"], out_shape=[o16, o16, o16] + j["out_shape"],
        scratch_shapes=[acc, acc] + j["scratch"], input_output_aliases=j["aliases"],
        compiler_params=_params(("parallel",) if job is None else ("arbitrary",)),
    )(proj, proj, proj, out, dout, *j["ins"])
    return res[0], res[1], res[2], list(res[3:])


_G0 = math.sqrt(2.0 / math.pi)
_G1 = 0.044715


def _gelu(x):
    return 0.5 * x * (1.0 + jnp.tanh(_G0 * (x + _G1 * x * x * x)))


def _gelu_grad(x):
    t = jnp.tanh(_G0 * (x + _G1 * x * x * x))
    return 0.5 * (1.0 + t) + 0.5 * x * (1.0 - t * t) * (_G0 * (1.0 + 3.0 * _G1 * x * x))


def _tril():
    row, lane = _sb_masks()
    return row >= lane


def sgu_fwd(proj, ln_g, ln_b, w, bias, *, name):
    T = proj.shape[0]
    tb = min(512, T)
    G = BRANCH_W // 128

    def body(u_ref, v_ref, g_ref, b_ref, w_ref, bias_ref, o_ref):
        vv = _gelu(v_ref[...])
        xh, _ = _group_norm(vv)
        vn = (xh * g_ref[...] + b_ref[...]).astype(BF16)
        tril = _tril()
        for g in range(G):
            wg = jnp.where(tril, w_ref[g], 0.0).astype(BF16)
            gc = slice(g * 128, (g + 1) * 128)
            for c in range(tb // CHUNK):
                r = slice(c * CHUNK, (c + 1) * CHUNK)
                sv = _dot(wg, vn[r, gc], NN) + bias_ref[g]
                o_ref[r, gc] = _gelu(u_ref[r, gc]) * sv

    cu, cv = C_SGU // BRANCH_W, C_SGU // BRANCH_W + 1
    vec = pl.BlockSpec((1, BRANCH_W), lambda i: (0, 0))
    mat = pl.BlockSpec((G, CHUNK, CHUNK), lambda i: (0, 0, 0))
    return pl.pallas_call(
        body, name=name, grid=(T // tb,),
        in_specs=[pl.BlockSpec((tb, BRANCH_W), lambda i: (i, cu)), pl.BlockSpec((tb, BRANCH_W), lambda i: (i, cv)),
                  vec, vec, mat, mat],
        out_specs=pl.BlockSpec((tb, BRANCH_W), lambda i: (i, 0)),
        out_shape=jax.ShapeDtypeStruct((T, BRANCH_W), F32),
        compiler_params=_params(("parallel",)),
    )(proj, proj, ln_g, ln_b, w, bias)


def sgu_bwd(proj, ln_g, ln_b, w, bias, dout, *, name):
    T = proj.shape[0]
    tb = min(512, T)
    G = BRANCH_W // 128

    def body(u_ref, v_ref, g_ref, b_ref, w_ref, bias_ref, do_ref, dp_ref, dw_ref, dbias_ref, dg_ref, db_ref, dvn_ref):
        @pl.when(pl.program_id(0) == 0)
        def _():
            dw_ref[...] = jnp.zeros_like(dw_ref)
            dbias_ref[...] = jnp.zeros_like(dbias_ref)
            dg_ref[...] = jnp.zeros_like(dg_ref)
            db_ref[...] = jnp.zeros_like(db_ref)

        gv = v_ref[...]
        vv = _gelu(gv)
        xh, rstd = _group_norm(vv)
        vn = (xh * g_ref[...] + b_ref[...]).astype(BF16)
        tril = _tril()
        for g in range(G):
            wg = jnp.where(tril, w_ref[g], 0.0).astype(BF16)
            gc = slice(g * 128, (g + 1) * 128)
            for c in range(tb // CHUNK):
                r = slice(c * CHUNK, (c + 1) * CHUNK)
                vn_c = vn[r, gc]
                sv = _dot(wg, vn_c, NN) + bias_ref[g]
                gu = u_ref[r, gc]
                d_o = do_ref[r, gc]
                dp_ref[r, gc] = (d_o * sv * _gelu_grad(gu)).astype(BF16)
                dsv = d_o * _gelu(gu)
                dsv_b = dsv.astype(BF16)
                dvn_ref[r, gc] = _dot(wg, dsv_b, TN)
                dw_ref[g] += jnp.where(tril, _dot(dsv_b, vn_c, NT), 0.0)
                dbias_ref[g] += jnp.broadcast_to(jnp.sum(dsv, axis=1, keepdims=True), (CHUNK, CHUNK))
        dvn = dvn_ref[...]
        dg_ref[...] += jnp.sum(dvn * xh, axis=0, keepdims=True)
        db_ref[...] += jnp.sum(dvn, axis=0, keepdims=True)
        dxh = dvn * g_ref[...]
        m1 = jnp.mean(dxh, axis=-1, keepdims=True)
        m2 = jnp.mean(dxh * xh, axis=-1, keepdims=True)
        dp_ref[:, BRANCH_W:2 * BRANCH_W] = (rstd * (dxh - m1 - xh * m2) * _gelu_grad(gv)).astype(BF16)

    cu, cv = C_SGU // BRANCH_W, C_SGU // BRANCH_W + 1
    vec = pl.BlockSpec((1, BRANCH_W), lambda i: (0, 0))
    mat = pl.BlockSpec((G, CHUNK, CHUNK), lambda i: (0, 0, 0))
    blk = pl.BlockSpec((tb, BRANCH_W), lambda i: (i, 0))
    msh = jax.ShapeDtypeStruct((G, CHUNK, CHUNK), F32)
    vsh = jax.ShapeDtypeStruct((1, BRANCH_W), F32)
    return pl.pallas_call(
        body, name=name, grid=(T // tb,),
        in_specs=[pl.BlockSpec((tb, BRANCH_W), lambda i: (i, cu)), pl.BlockSpec((tb, BRANCH_W), lambda i: (i, cv)),
                  vec, vec, mat, mat, blk],
        out_specs=[pl.BlockSpec((tb, 2 * BRANCH_W), lambda i: (i, 0)), mat, mat, vec, vec],
        out_shape=[jax.ShapeDtypeStruct((T, 2 * BRANCH_W), BF16), msh, msh, vsh, vsh],
        scratch_shapes=[pltpu.VMEM((tb, BRANCH_W), F32)],
        compiler_params=_params(("arbitrary",)),
    )(proj, proj, ln_g, ln_b, w, bias, dout)


def merge_fwd(a1, a2, a3, p1, p2, p3, proj, *, name):
    T = a1.shape[0]
    tm, tn = min(1024, T), 512
    gb = C_GATE // tn

    def body(a1_ref, a2_ref, a3_ref, p1_ref, p2_ref, p3_ref, g1_ref, g2_ref, g3_ref, m_ref, r1_ref, r2_ref, r3_ref):
        m = None
        for a_ref, p_ref, g_ref, r_ref in ((a1_ref, p1_ref, g1_ref, r1_ref), (a2_ref, p2_ref, g2_ref, r2_ref),
                                           (a3_ref, p3_ref, g3_ref, r3_ref)):
            r = _dot(a_ref[...].astype(BF16), p_ref[...], NN)
            r_ref[...] = r
            t = jax.nn.sigmoid(g_ref[...]) * r
            m = t if m is None else m + t
        m_ref[...] = m.astype(m_ref.dtype)

    a_spec = pl.BlockSpec((tm, BRANCH_W), lambda i, j: (i, 0))
    p_spec = pl.BlockSpec((BRANCH_W, tn), lambda i, j: (0, j))
    o_spec = pl.BlockSpec((tm, tn), lambda i, j: (i, j))
    osh = jax.ShapeDtypeStruct((T, D_MODEL), F32)
    gates = [pl.BlockSpec((tm, tn), functools.partial(lambda i, j, o: (i, o + j), o=gb + 2 * n)) for n in range(3)]
    return pl.pallas_call(
        body, name=name, grid=(T // tm, D_MODEL // tn),
        in_specs=[a_spec, a_spec, a_spec, p_spec, p_spec, p_spec, *gates],
        out_specs=[o_spec] * 4, out_shape=[jax.ShapeDtypeStruct((T, D_MODEL), BF16)] + [osh] * 3,
        compiler_params=_params(("parallel", "parallel")),
    )(a1, a2, a3, p1, p2, p3, proj, proj, proj)


def merge_bwd(dm, r1, r2, r3, proj, *, name):
    T = dm.shape[0]
    tm, tn = min(512, T), 512
    gb = C_GATE // tn

    def body(dm_ref, r1_ref, r2_ref, r3_ref, g1_ref, g2_ref, g3_ref, dr1_ref, dr2_ref, dr3_ref, dg1_ref, dg2_ref, dg3_ref):
        d = dm_ref[...]
        for r_ref, g_ref, dr_ref, dg_ref in ((r1_ref, g1_ref, dr1_ref, dg1_ref), (r2_ref, g2_ref, dr2_ref, dg2_ref),
                                             (r3_ref, g3_ref, dr3_ref, dg3_ref)):
            s = jax.nn.sigmoid(g_ref[...])
            dr_ref[...] = (d * s).astype(BF16)
            dg_ref[...] = (d * r_ref[...] * (s * (1.0 - s))).astype(BF16)

    o_spec = pl.BlockSpec((tm, tn), lambda i, j: (i, j))
    osh = jax.ShapeDtypeStruct((T, D_MODEL), BF16)
    gates = [pl.BlockSpec((tm, tn), functools.partial(lambda i, j, o: (i, o + j), o=gb + 2 * n)) for n in range(3)]
    return pl.pallas_call(
        body, name=name, grid=(T // tm, D_MODEL // tn),
        in_specs=[o_spec] * 4 + gates, out_specs=[o_spec] * 6, out_shape=[osh] * 6,
        compiler_params=_params(("parallel", "parallel")),
    )(dm, r1, r2, r3, proj, proj, proj)


def _rows_call(fn, ins, out_dtypes, *, name, tr=256):
    first = ins[0][0] if isinstance(ins[0], tuple) else ins[0]
    R, C = first.shape[-2:]
    tr = min(tr, R)
    assert R % tr == 0, (name, R, tr)
    arrs, specs = [], []
    for x in ins:
        if isinstance(x, tuple):
            arrs.append(x[0])
            specs.append(pl.BlockSpec((None, tr, C), functools.partial(lambda i, n: (n, i, 0), n=x[1])))
        else:
            arrs.append(x)
            specs.append(pl.BlockSpec((tr, C), lambda i: (i, 0)))
    ni = len(arrs)

    def body(*refs):
        vals = fn(*[r[...] for r in refs[:ni]])
        for o_ref, v in zip(refs[ni:], vals):
            o_ref[...] = v.astype(o_ref.dtype)

    res = pl.pallas_call(
        body, name=name, grid=(R // tr,), in_specs=specs,
        out_specs=[pl.BlockSpec((tr, C), lambda i: (i, 0)) for _ in out_dtypes],
        out_shape=[jax.ShapeDtypeStruct((R, C), dt) for dt in out_dtypes],
        compiler_params=_params(("parallel",)),
    )(*arrs)
    return res


def _tile_rows(rows, cols):
    t = 256
    while t > 8 and (t * cols > 512 * 1024 or rows % t):
        t //= 2
    return t


def _rows_at(fn, pos, ins, outs, steps, *, name, aliases=None):
    read = [n for n, (_, s) in enumerate(ins) if s is not ANY]
    ni = len(ins)

    def body(pos_ref, *refs):
        vals = fn(*[refs[n][...] for n in read])
        for o_ref, v in zip(refs[ni:], vals):
            o_ref[...] = v.astype(o_ref.dtype)

    return pl.pallas_call(
        body, name=name,
        grid_spec=pltpu.PrefetchScalarGridSpec(num_scalar_prefetch=1, grid=(steps,), in_specs=[s for _, s in ins],
                                               out_specs=[s for _, s in outs]),
        out_shape=[sh for sh, _ in outs],
        input_output_aliases={1 + i: o for i, o in (aliases or {}).items()},
        compiler_params=_params(("parallel",)),
    )(pos, *[a for a, _ in ins])


def cast_into_whole(pos, w, l, axis, *, name):
    _, r, n = w.shape
    tr = _tile_rows(r, n)
    if axis == 1:
        shape, spec = (r, n * N_CHIPS), pl.BlockSpec((tr, n), lambda i, p: (i, p[3]))
    else:
        shape, spec = (r * N_CHIPS, n), pl.BlockSpec((tr, n), lambda i, p: (p[3] * (r // tr) + i, 0))
    return _rows_at(lambda a: (a,), pos, [(w, pl.BlockSpec((None, tr, n), lambda i, p: (l, i, 0)))],
                    [(jax.ShapeDtypeStruct(shape, BF16), spec)], r // tr, name=name)[0]


def pair_sum(pos, theirs, g32, axis, *, name):
    rows2, cols = theirs.shape
    h = rows2 // (N_CHIPS if axis == 0 else 1)
    tr = _tile_rows(h, cols)
    hb = h // tr
    if axis == 1:
        own = pl.BlockSpec((tr, cols), lambda i, p: (p[2] * hb + i, 0))
    else:
        own = pl.BlockSpec((tr, cols), lambda i, p: ((2 * (i // hb) + p[2]) * hb + i % hb, 0))
    row = pl.BlockSpec((tr, cols), lambda i, p: (i, 0))
    return _rows_at(lambda t, m: (m + t.astype(F32),) * 2, pos, [(theirs, row), (g32, own)],
                    [(jax.ShapeDtypeStruct((rows2, cols), F32), row), (jax.ShapeDtypeStruct((rows2, cols), BF16), row)],
                    rows2 // tr, name=name)


def chip_sum(pos, h32, recv, l, axis, whole, *, name):
    _, depth, h, n = recv.shape
    tr = _tile_rows(h, n)
    hb = h // tr
    if axis == 1:
        mine = pl.BlockSpec((tr, n), lambda i, p: (i, p[3]))
    else:
        mine = pl.BlockSpec((tr, n), lambda i, p: (p[3] * hb + i, 0))
    ins = [(h32, mine)] + [(recv, pl.BlockSpec((None, None, tr, n), functools.partial(lambda i, p, j: (j, l, i, 0), j=j)))
                           for j in range(3)]
    if whole is not None:
        ins.append((whole, ANY))
    return _rows_at(lambda o, a, b, c: (((o + a.astype(F32)) + b.astype(F32)) + c.astype(F32),), pos, ins,
                    [(jax.ShapeDtypeStruct((depth, 2, h, n), F32), pl.BlockSpec((None, None, tr, n), lambda i, p: (l, p[2], i, 0)))],
                    hb, name=name, aliases=None if whole is None else {4: 0})[0]


def _adamw(w, g, m, v):
    m2 = ADAM_B1 * m + (1.0 - ADAM_B1) * g
    v2 = ADAM_B2 * v + (1.0 - ADAM_B2) * (g * g)
    m_hat = m2 / (1.0 - ADAM_B1 ** ADAM_STEP)
    v_hat = v2 / (1.0 - ADAM_B2 ** ADAM_STEP)
    delta = -ADAM_LR * (m_hat / (jnp.sqrt(v_hat) + ADAM_EPS) + ADAM_WD * w)
    return delta, m2, v2


def _place():
    return lax.axis_index("x"), lax.axis_index("y"), lax.axis_index("c")


def _chip_peers(x, y, c):
    return [((1 - x, y, c), 2 * (1 - x) + y), ((x, 1 - y, c), 2 * x + 1 - y), ((1 - x, 1 - y, c), 2 * (1 - x) + 1 - y)]


def _shard_of(ref, axis, k, n):
    start = pl.multiple_of(k * n, 128)
    return ref.at[pl.ds(start, n), :] if axis == 0 else ref.at[:, pl.ds(start, n)]


ANY = pl.BlockSpec(memory_space=pl.ANY)


class CopyJob:
    def __init__(self, ins, out_shape, scratch, copies, aliases=None):
        self.ins, self.out_shape, self.scratch, self.copies = list(ins), list(out_shape), list(scratch), copies
        self.aliases = dict(aliases or {})

    def start(self, ins, outs, sems):
        local, remote, _, _ = self.copies(ins, outs, sems)
        for d in local + remote:
            d.start()

    def finish(self, ins, outs, sems):
        local, remote, arrivals, relays = self.copies(ins, outs, sems)
        for needs, sends, _ in relays:
            for d in needs:
                d.wait_recv()
            for d in sends:
                d.start()
        for d in arrivals + [d for _, _, arrives in relays for d in arrives]:
            d.wait_recv()
        for d in remote + [d for _, sends, _ in relays for d in sends]:
            d.wait_send()
        for d in local:
            d.wait()


def run_job(job, *, name):
    ni, no = len(job.ins), len(job.out_shape)

    def body(*refs):
        parts = refs[:ni], refs[ni:ni + no], refs[ni + no:]
        job.start(*parts)
        job.finish(*parts)

    return pl.pallas_call(
        body, name=name, in_specs=[ANY] * ni, out_specs=[ANY] * no, out_shape=job.out_shape,
        scratch_shapes=job.scratch, input_output_aliases=job.aliases,
    )(*job.ins)


def _job_args(job, n_in, n_out):
    if job is None:
        return dict(ins=[], in_specs=[], out_specs=[], out_shape=[], scratch=[], aliases={})
    return dict(ins=job.ins, in_specs=[ANY] * len(job.ins), out_specs=[ANY] * len(job.out_shape),
                out_shape=job.out_shape, scratch=job.scratch,
                aliases={n_in + i: n_out + o for i, o in job.aliases.items()})


def _hosting(body, job, n_in, n_out, n_scratch, grid):
    if job is None:
        return body
    ji, jo = len(job.ins), len(job.out_shape)
    grid = (grid,) if isinstance(grid, int) else tuple(grid)

    def at(ends):
        hit = None
        for ax, e in enumerate(ends):
            here = pl.program_id(ax) == e
            hit = here if hit is None else jnp.logical_and(hit, here)
        return hit

    def hosted(*refs):
        o = n_in + ji
        s = o + n_out + jo
        parts = refs[n_in:o], refs[o + n_out:s], refs[s + n_scratch:]

        @pl.when(at([0] * len(grid)))
        def _():
            job.start(*parts)

        body(*refs[:n_in], *refs[o:o + n_out], *refs[s:s + n_scratch])

        @pl.when(at([g - 1 for g in grid]))
        def _():
            job.finish(*parts)

    return hosted


def _job_sems(n_remote, n_local):
    return [pltpu.SemaphoreType.DMA((n_remote,)), pltpu.SemaphoreType.DMA((n_remote,)), pltpu.SemaphoreType.DMA((n_local,))]


def gather_job(shards, axes, chips=(0, 1, 2)):
    na = len(shards)

    def copies(ins, outs, sems):
        send, recv, _ = sems
        x, y, c = _place()
        k = 2 * x + y
        remote, relays = [], []
        for a in range(na):
            r = outs[a].shape[0] // (N_CHIPS if axes[a] == 0 else 1)
            n = outs[a].shape[axes[a]] // N_CHIPS
            half = r // 2

            def part(kk, cc, a=a, n=n, half=half):
                rows = pl.ds(pl.multiple_of(cc * half + (kk * n if axes[a] == 0 else 0), 8), half)
                return outs[a].at[rows, :] if axes[a] == 0 else outs[a].at[rows, pl.ds(pl.multiple_of(kk * n, 128), n)]

            needs, passes, lands = [], [], []
            for j, (peer, kp) in enumerate(_chip_peers(x, y, c)):
                if j not in chips:
                    continue
                s = 6 * a + j
                remote.append(pltpu.make_async_remote_copy(part(k, c), part(k, c), send.at[s], recv.at[s],
                                                           device_id=peer, device_id_type=MESH))
                needs.append(pltpu.make_async_remote_copy(part(kp, c), part(kp, c), send.at[s], recv.at[s],
                                                          device_id=peer, device_id_type=MESH))
                passes.append(pltpu.make_async_remote_copy(part(kp, c), part(kp, c), send.at[s + 3], recv.at[s + 3],
                                                           device_id=(x, y, 1 - c), device_id_type=MESH))
                lands.append(pltpu.make_async_remote_copy(part(kp, 1 - c), part(kp, 1 - c), send.at[s + 3], recv.at[s + 3],
                                                          device_id=(x, y, 1 - c), device_id_type=MESH))
            relays.append((needs, passes, lands))
        return [], remote, [], relays

    out_shape = [jax.ShapeDtypeStruct(w.shape, BF16) for w in shards]
    return CopyJob(shards, out_shape, _job_sems(6 * na, 1), copies, {a: a for a in range(na)})


def scatter_job(layers, g16, axes, filled, chips=(0, 1, 2)):
    na = len(axes)

    def shard_shape(a):
        r, c = g16[a].shape
        return (r // N_CHIPS, c) if axes[a] == 0 else (r, c // N_CHIPS)

    def copies(ins, outs, sems):
        send, recv_sems, _ = sems
        x, y, c = _place()
        remote = []
        for a in range(na):
            n = shard_shape(a)[axes[a]]
            for r, (peer, kp) in enumerate(_chip_peers(x, y, c)):
                if r not in chips:
                    continue
                remote.append(pltpu.make_async_remote_copy(_shard_of(ins[a], axes[a], kp, n), outs[a].at[r, layers[a]],
                                                           send.at[3 * a + r], recv_sems.at[3 * a + r],
                                                           device_id=peer, device_id_type=MESH))
        return [], remote, remote, []

    out_shape = [jax.ShapeDtypeStruct((3, DEPTH) + shard_shape(a), BF16) for a in range(na)]
    ins = list(g16)
    aliases = {}
    for a in range(na):
        if filled[a] is not None:
            aliases[len(ins)] = a
            ins.append(filled[a])
    return CopyJob(ins, out_shape, _job_sems(3 * na, 1), copies, aliases)


def pair_job(g16, axes):
    na = len(axes)
    pieces = [1 if ax == 1 else N_CHIPS for ax in axes]

    def copies(ins, outs, sems):
        send, recv, _ = sems
        x, y, c = _place()
        remote = []
        s = 0
        for a in range(na):
            rows = g16[a].shape[0] // (2 * pieces[a])
            for kk in range(pieces[a]):
                src = ins[a].at[pl.ds(pl.multiple_of((2 * kk + 1 - c) * rows, 8), rows), :]
                remote.append(pltpu.make_async_remote_copy(src, outs[a].at[pl.ds(kk * rows, rows), :], send.at[s], recv.at[s],
                                                           device_id=(x, y, 1 - c), device_id_type=MESH))
                s += 1
        return [], remote, remote, []

    out_shape = [jax.ShapeDtypeStruct((g.shape[0] // 2, g.shape[1]), BF16) for g in g16]
    return CopyJob(g16, out_shape, _job_sems(sum(pieces), 1), copies)


def join_job(shards):
    na = len(shards)

    def copies(ins, outs, sems):
        send, recv, _ = sems
        x, y, c = _place()
        remote = [pltpu.make_async_remote_copy(outs[a].at[:, c], outs[a].at[:, c], send.at[a], recv.at[a],
                                               device_id=(x, y, 1 - c), device_id_type=MESH) for a in range(na)]
        lands = [pltpu.make_async_remote_copy(outs[a].at[:, 1 - c], outs[a].at[:, 1 - c], send.at[a], recv.at[a],
                                              device_id=(x, y, 1 - c), device_id_type=MESH) for a in range(na)]
        return [], remote, lands, []

    out_shape = [jax.ShapeDtypeStruct(s.shape, F32) for s in shards]
    return CopyJob(shards, out_shape, _job_sems(na, 1), copies, {a: a for a in range(na)})


def small_job(p):
    def copies(ins, outs, sems):
        send, recv, loc = sems
        x, y, c = _place()
        me = 4 * x + 2 * y + c
        remote, lands = [], []
        for rel in range(1, 8):
            dx, dy, dc = rel >> 2, (rel >> 1) & 1, rel & 1
            peer = (1 - x if dx else x, 1 - y if dy else y, 1 - c if dc else c)
            who = 4 * peer[0] + 2 * peer[1] + peer[2]
            remote.append(pltpu.make_async_remote_copy(ins[0], outs[0].at[me], send.at[rel - 1], recv.at[rel - 1],
                                                       device_id=peer, device_id_type=MESH))
            lands.append(pltpu.make_async_remote_copy(ins[0], outs[0].at[who], send.at[rel - 1], recv.at[rel - 1],
                                                      device_id=peer, device_id_type=MESH))
        return [pltpu.make_async_copy(ins[0], outs[0].at[me], loc.at[0])], remote, lands, []

    return CopyJob([p], [jax.ShapeDtypeStruct((8,) + p.shape, F32)], _job_sems(7, 1), copies)


def small_sum(slots):
    def add(*terms):
        acc = terms[0]
        for t in terms[1:]:
            acc = acc + t
        return (acc,)

    return _rows_call(add, [(slots, d) for d in range(8)], [F32], name="small_sum", tr=8 * 47)[0]


BIG = ("w_in", "p_ret", "p_sb", "p_sgu", "w_out", "w_up", "w_down")
BIG_AXIS = {"w_in": 1, "p_ret": 1, "p_sb": 1, "p_sgu": 1, "w_out": 0, "w_up": 1, "w_down": 0}
SMALL = ("ret_gn_g", "ret_gn_b", "sgu_ln_g", "sgu_ln_b", "sgu_w", "sgu_b", "ln1_g", "ln1_b", "ln2_g", "ln2_b")


def layer_forward(l, x0, W, sm, rope, rconsts, hooks):
    n = f"l{l}_"
    job = hooks.fwd_job(l, "proj")
    proj = matmul(x0, W["w_in"], mode="nn", tm=2048, tn=640, tk=1024, name=n + "proj", job=job)
    if job is not None:
        proj, job_out = proj
        hooks.done(job, job_out)
    retg, raw, states = ret_fwd(proj, *rope, rconsts, sm["ret_gn_g"], sm["ret_gn_b"], name=n + "ret_fwd")
    job = hooks.fwd_job(l, "sb")
    sb, job_out = sb_fwd(proj, name=n + "sb_fwd", job=job)
    if job is not None:
        hooks.done(job, job_out)
    sg = sgu_fwd(proj, sm["sgu_ln_g"], sm["sgu_ln_b"], sm["sgu_w"], sm["sgu_bias"], name=n + "sgu_fwd")
    merged, r1, r2, r3 = merge_fwd(retg, sb, sg, W["p_ret"], W["p_sb"], W["p_sgu"], proj, name=n + "merge_fwd")
    x1, xh1, rs1 = matmul_ln(merged, W["w_out"], x0, sm["ln1_g"], sm["ln1_b"], tk=1024, name=n + "out_ln1")
    job = hooks.fwd_job(l, "up")
    h1 = matmul(x1, W["w_up"], mode="nn", tm=1024, tn=1024, tk=1024, name=n + "up", job=job)
    if job is not None:
        h1, job_out = h1
        hooks.done(job, job_out)
    job = hooks.fwd_job(l, "down")
    res = matmul_ln(h1, W["w_down"], x1, sm["ln2_g"], sm["ln2_b"], pro=_relu2, tk=1024, name=n + "down_ln2", job=job)
    if job is not None:
        res, job_out = res
        hooks.done(job, job_out)
    x2, xh2, rs2 = res
    saved = dict(x0=x0, proj=proj, retg=retg, raw=raw, states=states, sb=sb, sg=sg, merged=merged, r=(r1, r2, r3),
                 x1=x1, xh1=xh1, rs1=rs1, h1=h1, xh2=xh2, rs2=rs2)
    return x2, saved


def layer_backward(l, dx2, s, W, sm, rope, rconsts, hooks):
    n = f"l{l}_"
    two = ((F32, None), (BF16, None))
    gw, gs = {}, {}
    job = hooks.bwd_job(l, "ln2")
    res = ln_bwd(dx2, s["xh2"], s["rs2"], sm["ln2_g"], name=n + "ln2_bwd", job=job)
    if job is not None:
        res, job_out = res
        hooks.done(job, job_out)
    du2, du2h, gs["ln2_g"], gs["ln2_b"] = res
    job = hooks.bwd_job(l, "g_down")
    gw["w_down"] = matmul(s["h1"], du2h, mode="tn", tm=1024, tn=1024, tk=512, pro=_relu2, outs=two, name=n + "g_down", job=job)
    if job is not None:
        gw["w_down"], job_out = gw["w_down"]
        hooks.done(job, job_out)
    dh1 = matmul(du2h, W["w_down"], mode="nt", tm=1024, tn=1024, tk=1024, outs=((BF16, None),),
                 epi=lambda acc, h: (acc * (2.0 * jnp.maximum(h, 0.0)),), tiles=(s["h1"],), name=n + "d_h1")
    job = hooks.bwd_job(l, "g_up")
    gw["w_up"] = matmul(s["x1"], dh1, mode="tn", tm=1024, tn=1024, tk=512, outs=two, name=n + "g_up", job=job)
    if job is not None:
        gw["w_up"], job_out = gw["w_up"]
        hooks.done(job, job_out)
    dx1 = matmul(dh1, W["w_up"], mode="nt", tm=1024, tn=1024, tk=1024,
                 epi=lambda acc, d: (acc + ALPHA * d,), tiles=(du2,), name=n + "d_x1")
    du1, du1h, gs["ln1_g"], gs["ln1_b"] = ln_bwd(dx1, s["xh1"], s["rs1"], sm["ln1_g"], name=n + "ln1_bwd")
    gw["w_out"] = matmul(s["merged"], du1h, mode="tn", tm=1024, tn=1024, tk=512, outs=two, name=n + "g_out")
    dmerged = matmul(du1h, W["w_out"], mode="nt", tm=1024, tn=1024, tk=1024, name=n + "d_merged")
    dr1, dr2, dr3, dg1, dg2, dg3 = merge_bwd(dmerged, *s["r"], s["proj"], name=n + "merge_bwd")
    d_branch = {}
    for nm, a, dr in (("p_ret", s["retg"], dr1), ("p_sb", s["sb"], dr2), ("p_sgu", s["sg"], dr3)):
        gw[nm] = matmul(a, dr, mode="tn", tm=512, tn=1024, tk=512, outs=two, name=n + "g_" + nm)
        d_branch[nm] = matmul(dr, W[nm], mode="nt", tm=1024, tn=512, tk=1024, name=n + "d_" + nm)
    job = hooks.pair(l, gw)
    dret, gs["ret_gn_g"], gs["ret_gn_b"], job_out = ret_bwd(s["proj"], *rope, rconsts, sm["ret_gn_g"], sm["ret_gn_b"],
                                                             s["raw"], s["states"], d_branch["p_ret"], name=n + "ret_bwd", job=job)
    if job is not None:
        hooks.done(job, job_out)
    job = hooks.scatter(l) if job is not None else None
    dsq, dsk, dsv, job_out = sb_bwd(s["proj"], s["sb"], d_branch["p_sb"], name=n + "sb_bwd", job=job)
    if job is not None:
        hooks.done(job, job_out)
    dsgu, gs["sgu_w"], dbias, gs["sgu_ln_g"], gs["sgu_ln_b"] = sgu_bwd(
        s["proj"], sm["sgu_ln_g"], sm["sgu_ln_b"], sm["sgu_w"], sm["sgu_bias"], d_branch["p_sgu"], name=n + "sgu_bwd")
    gs["sgu_b"] = dbias[:, :, 0]
    dproj = jnp.concatenate([dret, dsq, dsk, dsv, dsgu, dg1, dg2, dg3], axis=1)
    job = hooks.small(l, gs)
    gw["w_in"] = matmul(s["x0"], dproj, mode="tn", tm=1024, tn=1920, tk=512, outs=two, name=n + "g_in", job=job)
    if job is not None:
        gw["w_in"], job_out = gw["w_in"]
        hooks.done(job, job_out)
    job = hooks.tail(l, gw["w_in"])
    dx0 = matmul(dproj, W["w_in"], mode="nt", tm=1024, tn=1024, tk=1536,
                 epi=lambda acc, d: (acc + ALPHA * d,), tiles=(du1,), name=n + "d_x0", job=job)
    if job is not None:
        dx0, job_out = dx0
        hooks.done(job, job_out)
    return dx0, gw, gs


def local_step(x, target, small, plan):
    T = x.shape[0]
    rope = _rope_tables(T)
    rconsts = _ret_consts()
    sms = []
    for l in range(DEPTH):
        sm = {k: small[k][l][None, :] for k in SMALL if k not in ("sgu_w", "sgu_b")}
        sm["sgu_w"] = small["sgu_w"][l]
        sm["sgu_bias"] = jnp.broadcast_to(small["sgu_b"][l][:, :, None], (4, CHUNK, CHUNK))
        sms.append(sm)
    h, saved = x, []
    for l in range(DEPTH):
        h, s = layer_forward(l, h, plan.weights(l), sms[l], rope, rconsts, plan)
        saved.append(s)
    dy, sq = loss_head(h, target)
    gs = {k: [None] * DEPTH for k in SMALL}
    for l in reversed(range(DEPTH)):
        dy, gwl, gsl = layer_backward(l, dy, saved[l], plan.weights(l), sms[l], rope, rconsts, plan)
        plan.grads(l, gwl)
        for k in SMALL:
            gs[k][l] = gsl[k].reshape(small[k].shape[1:])
    return sq[0, 0], dy, {k: jnp.stack(v) for k, v in gs.items()}


EARLY_GRADS = ("p_ret", "p_sb", "p_sgu", "w_out", "w_up", "w_down")


class _StepPlan:
    def __init__(self, pos, shards16):
        self.pos = pos
        self.shards16 = shards16
        self.full = [dict() for _ in range(DEPTH)]
        self.gw = [None] * DEPTH
        self.bufs = {}
        self.sums = {}
        self.gs = [None] * DEPTH
        first = self._gather([(0, "w_in")])
        self.done(first, run_job(first, name="gather_first"))

    def weights(self, l):
        return self.full[l]

    def grads(self, l, gw):
        self.gw[l] = gw

    def _gather(self, items, chips=(0, 1, 2)):
        job = gather_job([self.shards16[l][k] for l, k in items], [BIG_AXIS[k] for _, k in items], chips)
        job.note = ("gather" if 2 in chips else "gather_part", items)
        return job

    def _pair(self, items):
        job = pair_job([g[1] for _, _, g in items], [BIG_AXIS[k] for _, k, _ in items])
        job.note = ("pair", items)
        return job

    def fwd_job(self, l, host):
        if host == "proj":
            return self._gather([(l, "w_down")])
        if host == "sb":
            return self._gather([(l, k) for k in ("p_ret", "p_sb", "p_sgu", "w_out", "w_up")])
        if l + 1 == DEPTH:
            return None
        return self._gather([(l + 1, "w_in")], (0, 1) if host == "up" else (2,))

    def bwd_job(self, l, host):
        if l + 1 == DEPTH:
            return None
        if host == "ln2":
            job = self._pair([(l + 1, "w_in", self.gw[l + 1]["w_in"])])
            job.note = ("pair_w_in", job.note[1])
            return job
        items, sums16 = self.summed_w_in
        job = scatter_job([l_ for l_, _, _ in items], sums16, [BIG_AXIS[k] for _, k, _ in items],
                          [self.bufs.get(k) for _, k, _ in items], (0, 1) if host == "g_down" else (2,))
        job.note = ("scatter", items)
        return job

    def pair(self, l, ready):
        return self._pair([(l, k, ready[k]) for k in EARLY_GRADS])

    def scatter(self, l):
        items, sums16 = self.summed
        job = scatter_job([l_ for l_, _, _ in items], sums16, [BIG_AXIS[k] for _, k, _ in items],
                          [self.bufs.get(k) for _, k, _ in items])
        job.note = ("scatter", items)
        return job

    def small(self, l, gs):
        self.gs[l] = {k: gs[k].reshape(-1) for k in SMALL}
        if l != 0:
            return None
        job = small_job(_pack_small({k: jnp.stack([self.gs[l_][k] for l_ in range(DEPTH)]) for k in SMALL}))
        job.note = ("small", [])
        return job

    def tail(self, l, g):
        if l != 0:
            return None
        last = self._pair([(0, "w_in", g)])
        self.done(last, run_job(last, name="pair_last"))
        return self.scatter(0)

    def done(self, job, outs):
        kind, items = job.note
        if kind == "small":
            self.small_slots = outs[0]
        if kind in ("pair", "pair_w_in"):
            sums16 = []
            for a, (l, k, g) in enumerate(items):
                self.sums[(l, k)], s16 = pair_sum(self.pos, outs[a], g[0], BIG_AXIS[k], name=f"pair_sum_{k}_{l}")
                sums16.append(s16)
            if kind == "pair":
                self.summed = (items, sums16)
            else:
                self.summed_w_in = (items, sums16)
        for a, item in enumerate(items):
            if kind == "gather_part":
                self.shards16[item[0]][item[1]] = outs[a]
            elif kind == "gather":
                self.full[item[0]][item[1]] = outs[a]
            elif kind == "scatter":
                self.bufs[item[1]] = outs[a]

    def finish(self):
        return self.bufs, self.sums


def _flat2(a):
    return a.reshape(-1, a.shape[-1])


def _pack_small(d, pre=""):
    return jnp.concatenate([d[pre + k].reshape(-1) for k in SMALL]).reshape(-1, 128)


def kernel(x, w_in, ret_gn_g, ret_gn_b, sgu_ln_g, sgu_ln_b, sgu_w, sgu_b, p_ret, p_sb, p_sgu, w_out, ln1_g, ln1_b, w_up, w_down, ln2_g, ln2_b, loss_target, m_w_in, m_ret_gn_g, m_ret_gn_b, m_sgu_ln_g, m_sgu_ln_b, m_sgu_w, m_sgu_b, m_p_ret, m_p_sb, m_p_sgu, m_w_out, m_ln1_g, m_ln1_b, m_w_up, m_w_down, m_ln2_g, m_ln2_b, v_w_in, v_ret_gn_g, v_ret_gn_b, v_sgu_ln_g, v_sgu_ln_b, v_sgu_w, v_sgu_b, v_p_ret, v_p_sb, v_p_sgu, v_w_out, v_ln1_g, v_ln1_b, v_w_up, v_w_down, v_ln2_g, v_ln2_b):
    given = dict(locals())
    order = BIG[:1] + SMALL[:6] + BIG[1:5] + SMALL[6:8] + BIG[5:7] + SMALL[8:10]
    L = DEPTH

    px, py, pc = _place()
    pos = jnp.stack([px, py, pc, 2 * px + py]).astype(jnp.int32)

    shards16 = [{k: cast_into_whole(pos, given[k], l, BIG_AXIS[k], name=f"cast_{k}_{l}") for k in BIG} for l in range(L)]
    plan = _StepPlan(pos, shards16)
    sq, dx, gs = local_step(x[0], loss_target[0], {k: given[k] for k in SMALL}, plan)
    loss = 0.5 * lax.psum(sq, ("x", "y", "c"))

    bufs, sums = plan.finish()
    shards = []
    for k in BIG:
        whole = None
        for l in range(L):
            whole = chip_sum(pos, sums[(l, k)], bufs[k], l, BIG_AXIS[k], whole, name=f"chip_sum_{k}_{l}")
        shards.append(whole)
    joined = run_job(join_job(shards), name="join_halves")
    out = {}
    for a, k in enumerate(BIG):
        shp = given[k].shape
        res = _rows_call(lambda g_, w_, m_, v_: (g_,) + _adamw(w_, g_, m_, v_),
                         [joined[a].reshape(-1, shp[-1]), _flat2(given[k]), _flat2(given["m_" + k]), _flat2(given["v_" + k])],
                         [F32] * 4, name="adamw_" + k)
        out[k] = [r.reshape(shp) for r in res]

    pack = _pack_small
    res = _rows_call(lambda g_, w_, m_, v_: (g_,) + _adamw(w_, g_, m_, v_),
                     [small_sum(plan.small_slots), pack(given), pack(given, "m_"), pack(given, "v_")], [F32] * 4,
                     name="adamw_small", tr=8 * 47)
    off = 0
    for k in SMALL:
        sz = given[k].size
        out[k] = [r.reshape(-1)[off:off + sz].reshape(given[k].shape) for r in res]
        off += sz

    grads = [out[k][0] for k in order]
    deltas = [out[k][1] for k in order]
    new_m = [out[k][2] for k in order]
    new_v = [out[k][3] for k in order]
    return (loss, dx[None], *grads, *deltas, *new_m, *new_v)
```

```python
import functools
import math

import jax
import jax.numpy as jnp
from jax import lax
from jax.experimental import pallas as pl
from jax.experimental.pallas import tpu as pltpu

F32 = jnp.float32
BF16 = jnp.bfloat16

D_MODEL = 1024
SEQ = 4096
DEPTH = 2
CHUNK = 128
RET_HEADS = 4
BRANCH_W = 512
N_IN = 7680
D_FF = 4096
LN_EPS = 1e-5
ROPE_BASE = 10000.0
ALPHA = (2 * DEPTH) ** 0.25
RET_SCALE = 128 ** -0.5
SB_SCALE = 64 ** -0.5
C_RET, C_SB, C_SGU, C_GATE = 0, 2048, 3584, 4608

ADAM_LR, ADAM_B1, ADAM_B2, ADAM_EPS, ADAM_WD, ADAM_STEP = 0.001, 0.9, 0.999, 1e-08, 0.01, 10

N_CHIPS = 4
VMEM_LIMIT = 56 * 1024 * 1024
MESH = pl.DeviceIdType.MESH

NN = ((1,), (0,))
NT = ((1,), (1,))
TN = ((0,), (0,))


def _dot(a, b, dims):
    return lax.dot_general(a, b, (dims, ((), ())), preferred_element_type=F32)


def _params(sem):
    return pltpu.CompilerParams(dimension_semantics=sem, vmem_limit_bytes=VMEM_LIMIT)


def _relu2(h):
    r = jnp.maximum(h, 0.0)
    return r * r


def matmul(a, b, *, mode, tm, tn, tk, outs=((F32, None),), pro=None, epi=None, tiles=(), rows=(), name, job=None):
    if mode == "nn":
        (M, K), N = a.shape, b.shape[1]
    elif mode == "nt":
        (M, K), N = a.shape, b.shape[0]
    else:
        (K, M), N = a.shape, b.shape[1]
    tm, tn, tk = min(tm, M), min(tn, N), min(tk, K)
    assert M % tm == 0 and N % tn == 0 and K % tk == 0, (name, M, N, K, tm, tn, tk)
    if mode == "nn":
        a_spec = pl.BlockSpec((tm, tk), lambda i, j, k: (i, k))
        b_spec = pl.BlockSpec((tk, tn), lambda i, j, k: (k, j))
        dims = NN
    elif mode == "nt":
        a_spec = pl.BlockSpec((tm, tk), lambda i, j, k: (i, k))
        b_spec = pl.BlockSpec((tn, tk), lambda i, j, k: (j, k))
        dims = NT
    else:
        a_spec = pl.BlockSpec((tk, tm), lambda i, j, k: (k, i))
        b_spec = pl.BlockSpec((tk, tn), lambda i, j, k: (k, j))
        dims = TN
    nk = K // tk
    nt_, nr, no = len(tiles), len(rows), len(outs)

    def body(a_ref, b_ref, *rest):
        tile_refs = rest[:nt_]
        row_refs = rest[nt_:nt_ + nr]
        out_refs = rest[nt_ + nr:nt_ + nr + no]
        av = a_ref[...]
        if pro is not None:
            av = pro(av)
        p = _dot(av.astype(BF16), b_ref[...].astype(BF16), dims)

        def finish(acc):
            vals = (acc,) * no if epi is None else epi(acc, *[r[...] for r in tile_refs], *[r[...] for r in row_refs])
            for o_ref, v in zip(out_refs, vals):
                o_ref[...] = v.astype(o_ref.dtype)

        if nk == 1:
            finish(p)
        else:
            acc_ref = rest[-1]
            k = pl.program_id(2)

            @pl.when(k == 0)
            def _():
                acc_ref[...] = p

            @pl.when(k > 0)
            def _():
                acc_ref[...] += p

            @pl.when(k == nk - 1)
            def _():
                finish(acc_ref[...])

    out_shape, out_specs = [], []
    for dt, width in outs:
        if width is None:
            out_shape.append(jax.ShapeDtypeStruct((M, N), dt))
            out_specs.append(pl.BlockSpec((tm, tn), lambda i, j, k: (i, j)))
        else:
            assert N == tn
            out_shape.append(jax.ShapeDtypeStruct((M, width), dt))
            out_specs.append(pl.BlockSpec((tm, width), lambda i, j, k: (i, 0)))
    in_specs = [a_spec, b_spec]
    in_specs += [pl.BlockSpec((tm, tn), lambda i, j, k: (i, j)) for _ in tiles]
    in_specs += [pl.BlockSpec((1, tn), lambda i, j, k: (0, j)) for _ in rows]
    grid = (M // tm, N // tn, nk)
    scratch = [pltpu.VMEM((tm, tn), F32)] if nk > 1 else []
    j = _job_args(job, len(in_specs), no)
    res = pl.pallas_call(
        _hosting(body, job, len(in_specs), no, len(scratch), grid), name=name, grid=grid,
        in_specs=in_specs + j["in_specs"], out_specs=out_specs + j["out_specs"], out_shape=out_shape + j["out_shape"],
        scratch_shapes=scratch + j["scratch"], input_output_aliases=j["aliases"],
        compiler_params=_params(("parallel", "parallel", "arbitrary") if job is None else ("arbitrary",) * 3),
    )(a, b, *tiles, *rows, *j["ins"])
    mine = res[0] if no == 1 else list(res[:no])
    return mine if job is None else (mine, list(res[no:]))


def _ln_epi(acc, res, g, b):
    u = ALPHA * res + acc
    mu = jnp.mean(u, axis=-1, keepdims=True)
    xc = u - mu
    var = jnp.mean(xc * xc, axis=-1, keepdims=True)
    rstd = lax.rsqrt(var + LN_EPS)
    xhat = xc * rstd
    return xhat * g + b, xhat, jnp.broadcast_to(rstd, (u.shape[0], 128))


def matmul_ln(a, w, res, g, b, *, pro=None, tk, name, job=None):
    n = w.shape[1]
    return matmul(a, w, mode="nn", tm=1024, tn=n, tk=tk, pro=pro, epi=_ln_epi, tiles=(res,), rows=(g, b),
                  outs=((F32, None), (F32, None), (F32, 128)), name=name, job=job)


def ln_bwd(dy, xhat, rstd, g, *, name, job=None):
    T, D = dy.shape
    tm = min(512, T)

    def body(dy_ref, xh_ref, rs_ref, g_ref, du_ref, du16_ref, dg_ref, db_ref):
        dyv, xh = dy_ref[...], xh_ref[...]
        r = rs_ref[:, 0:1]
        dxh = dyv * g_ref[...]
        m1 = jnp.mean(dxh, axis=-1, keepdims=True)
        m2 = jnp.mean(dxh * xh, axis=-1, keepdims=True)
        du = r * (dxh - m1 - xh * m2)
        du_ref[...] = du
        du16_ref[...] = du.astype(BF16)

        @pl.when(pl.program_id(0) == 0)
        def _():
            dg_ref[...] = jnp.zeros_like(dg_ref)
            db_ref[...] = jnp.zeros_like(db_ref)

        dg_ref[...] += jnp.sum(dyv * xh, axis=0, keepdims=True)
        db_ref[...] += jnp.sum(dyv, axis=0, keepdims=True)

    row = pl.BlockSpec((tm, D), lambda i: (i, 0))
    vec = pl.BlockSpec((1, D), lambda i: (0, 0))
    j = _job_args(job, 4, 4)
    res = pl.pallas_call(
        _hosting(body, job, 4, 4, 0, T // tm), name=name, grid=(T // tm,),
        in_specs=[row, row, pl.BlockSpec((tm, 128), lambda i: (i, 0)), vec] + j["in_specs"],
        out_specs=[row, row, vec, vec] + j["out_specs"],
        out_shape=[jax.ShapeDtypeStruct((T, D), F32), jax.ShapeDtypeStruct((T, D), BF16),
                   jax.ShapeDtypeStruct((1, D), F32), jax.ShapeDtypeStruct((1, D), F32)] + j["out_shape"],
        scratch_shapes=j["scratch"], input_output_aliases=j["aliases"],
        compiler_params=_params(("arbitrary",)),
    )(dy, xhat, rstd, g, *j["ins"])
    return list(res[:4]) if job is None else (list(res[:4]), list(res[4:]))


def loss_head(y, target):
    T, D = y.shape
    tm = min(512, T)

    def body(y_ref, t_ref, dy_ref, s_ref):
        e = y_ref[...] - t_ref[...]
        dy_ref[...] = e * (1.0 / D)

        @pl.when(pl.program_id(0) == 0)
        def _():
            s_ref[...] = jnp.zeros_like(s_ref)

        s_ref[...] += jnp.sum(jnp.mean(e * e, axis=-1, keepdims=True))

    row = pl.BlockSpec((tm, D), lambda i: (i, 0))
    return pl.pallas_call(
        body, name="loss_head", grid=(T // tm,),
        in_specs=[row, row], out_specs=[row, pl.BlockSpec((8, 128), lambda i: (0, 0))],
        out_shape=[jax.ShapeDtypeStruct((T, D), F32), jax.ShapeDtypeStruct((8, 128), F32)],
        compiler_params=_params(("arbitrary",)),
    )(y, target)


def _rope_tables(T):
    half = 64
    inv_freq = ROPE_BASE ** (-jnp.arange(half, dtype=F32) / half)
    ang = jnp.arange(T, dtype=jnp.int32).astype(F32)[:, None] * inv_freq[None, :]
    cos, sin = jnp.cos(ang), jnp.sin(ang)
    return jnp.concatenate([cos, cos], axis=1), jnp.concatenate([-sin, sin], axis=1)


def _ret_consts():
    H = RET_HEADS
    log_g = jnp.log(1.0 - 2.0 ** (-5.0 - jnp.arange(H, dtype=F32)))
    idx = jnp.arange(CHUNK, dtype=F32)
    diff = idx[:, None] - idx[None, :]
    dmat = jnp.where(diff[None] >= 0, jnp.exp(log_g[:, None, None] * diff[None]), 0.0)
    kd = jnp.exp(log_g[:, None] * (CHUNK - 1 - idx)[None, :])
    qd = jnp.exp(log_g[:, None] * (idx + 1.0)[None, :])
    cd = jnp.exp(log_g * CHUNK)
    full = (H, CHUNK, CHUNK)
    return (dmat.astype(F32), jnp.broadcast_to(kd[:, :, None], full), jnp.broadcast_to(qd[:, :, None], full),
            jnp.broadcast_to(cd[:, None, None], full))


def _swap_halves(v):
    return pltpu.roll(v, 64, 1)


def _group_norm(o):
    mu = jnp.mean(o, axis=-1, keepdims=True)
    xc = o - mu
    var = jnp.mean(xc * xc, axis=-1, keepdims=True)
    rstd = lax.rsqrt(var + LN_EPS)
    return xc * rstd, rstd


def ret_fwd(proj, cosf, sinf, consts, gn_g, gn_b, *, name):
    T = proj.shape[0]
    tb = min(512, T)
    nch = tb // CHUNK
    H = RET_HEADS

    def body(p_ref, cos_ref, sin_ref, dm_ref, kd_ref, qd_ref, cd_ref, g_ref, b_ref, out_ref, raw_ref, st_ref, s_ref):
        @pl.when(pl.program_id(0) == 0)
        def _():
            s_ref[...] = jnp.zeros_like(s_ref)

        for c in range(nch):
            r = slice(c * CHUNK, (c + 1) * CHUNK)
            cs, sn = cos_ref[r, :], sin_ref[r, :]
            for h in range(H):
                hc = slice(h * 128, (h + 1) * 128)
                q = p_ref[r, h * 128:(h + 1) * 128]
                k = p_ref[r, 512 + h * 128:512 + (h + 1) * 128]
                v = p_ref[r, 1024 + h * 128:1024 + (h + 1) * 128]
                gt = p_ref[r, 1536 + h * 128:1536 + (h + 1) * 128]
                qr = q * cs + _swap_halves(q) * sn
                kr = (k * cs + _swap_halves(k) * sn) * RET_SCALE
                sprev = s_ref[h]
                st_ref[c, h] = sprev
                qb, kb, vb = qr.astype(BF16), kr.astype(BF16), v.astype(BF16)
                s = _dot(qb, kb, NT) * dm_ref[h]
                o = _dot(s.astype(BF16), vb, NN) + _dot((qr * qd_ref[h]).astype(BF16), sprev.astype(BF16), NN)
                s_ref[h] = sprev * cd_ref[h] + _dot((kr * kd_ref[h]).astype(BF16), vb, TN)
                raw_ref[r, hc] = o
                y, _ = _group_norm(o)
                out_ref[r, hc] = (gt * jax.nn.sigmoid(gt)) * (y * g_ref[:, hc] + b_ref[:, hc])

    cmat = pl.BlockSpec((H, CHUNK, CHUNK), lambda i: (0, 0, 0))
    vec = pl.BlockSpec((1, BRANCH_W), lambda i: (0, 0))
    rope = pl.BlockSpec((tb, 128), lambda i: (i, 0))
    blk = pl.BlockSpec((tb, BRANCH_W), lambda i: (i, 0))
    return pl.pallas_call(
        body, name=name, grid=(T // tb,),
        in_specs=[pl.BlockSpec((tb, 2048), lambda i: (i, 0)), rope, rope, cmat, cmat, cmat, cmat, vec, vec],
        out_specs=[blk, blk, pl.BlockSpec((nch, H, CHUNK, CHUNK), lambda i: (i, 0, 0, 0))],
        out_shape=[jax.ShapeDtypeStruct((T, BRANCH_W), F32), jax.ShapeDtypeStruct((T, BRANCH_W), F32),
                   jax.ShapeDtypeStruct((T // CHUNK, H, CHUNK, CHUNK), F32)],
        scratch_shapes=[pltpu.VMEM((H, CHUNK, CHUNK), F32)],
        compiler_params=_params(("arbitrary",)),
    )(proj, cosf, sinf, *consts, gn_g, gn_b)


def ret_bwd(proj, cosf, sinf, consts, gn_g, gn_b, raw, states, dout, *, name, job=None):
    T = proj.shape[0]
    tb = min(512, T)
    nch = tb // CHUNK
    nb = T // tb
    H = RET_HEADS

    def body(p_ref, cos_ref, sin_ref, dm_ref, kd_ref, qd_ref, cd_ref, g_ref, b_ref, raw_ref, st_ref, do_ref,
             dp_ref, dg_ref, db_ref, ds_ref):
        @pl.when(pl.program_id(0) == 0)
        def _():
            ds_ref[...] = jnp.zeros_like(ds_ref)
            dg_ref[...] = jnp.zeros_like(dg_ref)
            db_ref[...] = jnp.zeros_like(db_ref)

        for c in reversed(range(nch)):
            r = slice(c * CHUNK, (c + 1) * CHUNK)
            cs, sn = cos_ref[r, :], sin_ref[r, :]
            for h in range(H):
                hc = slice(h * 128, (h + 1) * 128)
                q = p_ref[r, h * 128:(h + 1) * 128]
                k = p_ref[r, 512 + h * 128:512 + (h + 1) * 128]
                v = p_ref[r, 1024 + h * 128:1024 + (h + 1) * 128]
                gt = p_ref[r, 1536 + h * 128:1536 + (h + 1) * 128]
                qr = q * cs + _swap_halves(q) * sn
                kr = (k * cs + _swap_halves(k) * sn) * RET_SCALE
                sprev = st_ref[c, h]
                gv = g_ref[:, hc]
                y, rstd = _group_norm(raw_ref[r, hc])
                d_out = do_ref[r, hc]
                sg = jax.nn.sigmoid(gt)
                d_gate = d_out * (y * gv + b_ref[:, hc]) * (sg * (1.0 + gt * (1.0 - sg)))
                d_aff = d_out * (gt * sg)
                dg_ref[:, hc] += jnp.sum(d_aff * y, axis=0, keepdims=True)
                db_ref[:, hc] += jnp.sum(d_aff, axis=0, keepdims=True)
                dxh = d_aff * gv
                m1 = jnp.mean(dxh, axis=-1, keepdims=True)
                m2 = jnp.mean(dxh * y, axis=-1, keepdims=True)
                d_o = (rstd * (dxh - m1 - y * m2)).astype(BF16)
                qb, kb, vb = qr.astype(BF16), kr.astype(BF16), v.astype(BF16)
                dm, kd, qd = dm_ref[h], kd_ref[h], qd_ref[h]
                p = (_dot(qb, kb, NT) * dm).astype(BF16)
                dp = (_dot(d_o, vb, NT) * dm).astype(BF16)
                dsn = ds_ref[h]
                dsb = dsn.astype(BF16)
                dq_r = _dot(dp, kb, NN) + _dot(d_o, sprev.astype(BF16), NT) * qd
                dk_r = (_dot(dp, qb, TN) + _dot(vb, dsb, NT) * kd) * RET_SCALE
                d_v = _dot(p, d_o, TN) + _dot((kr * kd).astype(BF16), dsb, NN)
                ds_ref[h] = dsn * cd_ref[h] + _dot((qr * qd).astype(BF16), d_o, TN)
                dp_ref[r, h * 128:(h + 1) * 128] = (dq_r * cs - _swap_halves(dq_r) * sn).astype(BF16)
                dp_ref[r, 512 + h * 128:512 + (h + 1) * 128] = (dk_r * cs - _swap_halves(dk_r) * sn).astype(BF16)
                dp_ref[r, 1024 + h * 128:1024 + (h + 1) * 128] = d_v.astype(BF16)
                dp_ref[r, 1536 + h * 128:1536 + (h + 1) * 128] = d_gate.astype(BF16)

    cmat = pl.BlockSpec((H, CHUNK, CHUNK), lambda i: (0, 0, 0))
    vec = pl.BlockSpec((1, BRANCH_W), lambda i: (0, 0))
    rope = pl.BlockSpec((tb, 128), lambda i: (nb - 1 - i, 0))
    blk = pl.BlockSpec((tb, BRANCH_W), lambda i: (nb - 1 - i, 0))
    wide = pl.BlockSpec((tb, 2048), lambda i: (nb - 1 - i, 0))
    j = _job_args(job, 12, 3)
    res = pl.pallas_call(
        _hosting(body, job, 12, 3, 1, nb), name=name, grid=(nb,),
        in_specs=[wide, rope, rope, cmat, cmat, cmat, cmat, vec, vec, blk,
                  pl.BlockSpec((nch, H, CHUNK, CHUNK), lambda i: (nb - 1 - i, 0, 0, 0)), blk] + j["in_specs"],
        out_specs=[wide, vec, vec] + j["out_specs"],
        out_shape=[jax.ShapeDtypeStruct((T, 2048), BF16), jax.ShapeDtypeStruct((1, BRANCH_W), F32),
                   jax.ShapeDtypeStruct((1, BRANCH_W), F32)] + j["out_shape"],
        scratch_shapes=[pltpu.VMEM((H, CHUNK, CHUNK), F32)] + j["scratch"], input_output_aliases=j["aliases"],
        compiler_params=_params(("arbitrary",)),
    )(proj, cosf, sinf, *consts, gn_g, gn_b, raw, states, dout, *j["ins"])
    return res[0], res[1], res[2], list(res[3:])


def _sb_masks():
    row = lax.broadcasted_iota(jnp.int32, (CHUNK, CHUNK), 0)
    lane = lax.broadcasted_iota(jnp.int32, (CHUNK, CHUNK), 1)
    return row, lane


SB_QT = 256
SB_DEAD = -105.0


def _pair(v):
    hi = v.astype(BF16)
    return jnp.concatenate([hi, (v - hi.astype(F32)).astype(BF16)], axis=1)


def _sb_consts():
    r = lax.broadcasted_iota(jnp.int32, (256, 256), 0) & 127
    c = lax.broadcasted_iota(jnp.int32, (256, 256), 1)
    ones = c >= 128
    lane = lax.broadcasted_iota(jnp.int32, (CHUNK, CHUNK), 1)
    return (ones | (r > c)).astype(BF16), (ones | (r >= c)).astype(BF16), (lane < 64, lane >= 64)


def _per_head(x, hms):
    return jnp.concatenate([jnp.where(hm, x, 0.0) for hm in hms], axis=0).astype(BF16)


def _sb_logits(qb, kb2, mask2):
    z = _dot(qb, kb2, NT)
    l1p = jnp.log(1.0 + jnp.exp(-jnp.abs(z)))
    lsp = jnp.minimum(z, 0.0) - l1p
    lsn = lsp - z
    if mask2 is not None:
        lsn = jnp.where(mask2, lsn, 0.0)
    return lsp, lsn


def _sb_tile_mask(qt):
    trow = lax.broadcasted_iota(jnp.int32, (qt, 256), 0)
    tlane = lax.broadcasted_iota(jnp.int32, (qt, 256), 1) & 127
    return lambda m: (tlane + m * CHUNK) < trow


def sb_fwd(proj, *, name, job=None):
    T = proj.shape[0]
    qt = min(SB_QT, T)
    nsub = qt // CHUNK
    cb = C_SB // 128

    def body(q_ref, k_ref, v_ref, o_ref):
        u_gt, _, hms = _sb_consts()
        tile_mask = _sb_tile_mask(qt)

        def qtile(i, _):
            rq = pl.ds(pl.multiple_of(i * qt, qt), qt)
            qb = (q_ref[rq, :] * SB_SCALE).astype(BF16)

            def group(js, masks, state):
                carry, acc = list(state[:2]), state[2]
                rows = [pl.ds(pl.multiple_of(j * CHUNK, CHUNK), CHUNK) for j in js]
                logits = [_sb_logits(qb, _per_head(k_ref[rk, :], hms), m) for rk, m in zip(rows, masks)]
                sums = [[_dot(_pair(lsn[:, h * 128:(h + 1) * 128]), u_gt, NN) for h in range(2)] for _, lsn in logits]
                weights = []
                for (lsp, _), r, m in zip(logits, sums, masks):
                    a_b = []
                    for h in range(2):
                        hc = slice(h * 128, (h + 1) * 128)
                        a = jnp.exp(lsp[:, hc] + r[h][:, :128] + carry[h])
                        if m is not None:
                            a = jnp.where(m[:, hc], a, 0.0)
                        carry[h] = carry[h] + r[h][:, 128:]
                        a_b.append(a.astype(BF16))
                    weights.append(jnp.concatenate(a_b, axis=1))
                for rk, a in zip(rows, weights):
                    acc = acc + _dot(a, _per_head(v_ref[rk, :], hms), NN)
                return carry[0], carry[1], acc

            zero = jnp.zeros((qt, 128), F32)
            diag = list(reversed(range(nsub)))
            state = group([i * nsub + m for m in diag], [tile_mask(m) for m in diag], (zero, zero, zero))

            def live(c):
                return jnp.logical_and(c[0] < i, jnp.maximum(jnp.max(c[1][0]), jnp.max(c[1][1])) > SB_DEAD)

            def blocks(c):
                jj, st = c
                return jj + 1, group([(i - jj) * nsub - 1 - u for u in range(nsub)], [None] * nsub, st)

            _, state = lax.while_loop(live, blocks, (jnp.int32(0), state))
            o_ref[rq, :] = state[2]
            return 0

        lax.fori_loop(0, T // qt, qtile, 0)

    def col(off):
        return pl.BlockSpec((T, 128), lambda hp: (0, off + hp))

    steps = BRANCH_W // 128
    j = _job_args(job, 3, 1)
    res = pl.pallas_call(
        _hosting(body, job, 3, 1, 0, steps), name=name, grid=(steps,),
        in_specs=[col(cb), col(cb + 4), col(cb + 8)] + j["in_specs"], out_specs=[col(0)] + j["out_specs"],
        out_shape=[jax.ShapeDtypeStruct((T, BRANCH_W), F32)] + j["out_shape"],
        scratch_shapes=j["scratch"], input_output_aliases=j["aliases"],
        compiler_params=_params(("parallel",) if job is None else ("arbitrary",)),
    )(proj, proj, proj, *j["ins"])
    return res[0], list(res[1:])


def sb_bwd(proj, out, dout, *, name, job=None):
    T = proj.shape[0]
    qt = min(SB_QT, T)
    nsub = qt // CHUNK
    cb = C_SB // 128

    def body(q_ref, k_ref, v_ref, o_ref, do_ref, dq_ref, dk_ref, dv_ref, dkt_ref, dvt_ref):
        u_gt, u_ge, hms = _sb_consts()
        tile_mask = _sb_tile_mask(qt)
        tall_lane = lax.broadcasted_iota(jnp.int32, (qt, 128), 1)
        top = lax.broadcasted_iota(jnp.int32, (CHUNK, CHUNK), 0) < 64
        dkt_ref[...] = jnp.zeros_like(dkt_ref)
        dvt_ref[...] = jnp.zeros_like(dvt_ref)

        def qtile(i, _):
            rq = pl.ds(pl.multiple_of(i * qt, qt), qt)
            qs = q_ref[rq, :] * SB_SCALE
            qb, q_t = qs.astype(BF16), qs.T.astype(BF16)
            dov = do_ref[rq, :]
            dob, do_t = dov.astype(BF16), dov.T.astype(BF16)
            prod = dob.astype(F32) * o_ref[rq, :]
            total = [jnp.broadcast_to(jnp.sum(jnp.where(hm, prod, 0.0), axis=1, keepdims=True), (qt, 128))
                     for hm in (tall_lane < 64, tall_lane >= 64)]

            def group(js, masks, state):
                c_l, c_w, dq = list(state[:2]), list(state[2:4]), state[4]
                heads = [slice(h * 128, (h + 1) * 128) for h in range(2)]
                rows = [pl.ds(pl.multiple_of(j * CHUNK, CHUNK), CHUNK) for j in js]
                kb2 = [_per_head(k_ref[rk, :], hms) for rk in rows]
                logits = [_sb_logits(qb, kb, m) for kb, m in zip(kb2, masks)]
                da = [_dot(dob, _per_head(v_ref[rk, :], hms), NT) for rk in rows]
                sums = [[_dot(_pair(lsn[:, hc]), u_gt, NN) for hc in heads] for _, lsn in logits]
                a_b, w_all = [], []
                for (lsp, _), r, d, m in zip(logits, sums, da, masks):
                    a_h, w_h = [], []
                    for h, hc in enumerate(heads):
                        a = jnp.exp(lsp[:, hc] + r[h][:, :128] + c_l[h])
                        if m is not None:
                            a = jnp.where(m[:, hc], a, 0.0)
                        c_l[h] = c_l[h] + r[h][:, 128:]
                        a = a.astype(BF16)
                        a_h.append(a)
                        w_h.append(a.astype(F32) * d[:, hc])
                    a_b.append(jnp.concatenate(a_h, axis=1))
                    w_all.append(w_h)
                sums_w = [[_dot(_pair(w), u_ge, NN) for w in w_h] for w_h in w_all]
                dz_b = []
                for (lsp, _), w_h, r, m in zip(logits, w_all, sums_w, masks):
                    sp = jnp.exp(lsp)
                    dz_h = []
                    for h, hc in enumerate(heads):
                        later_w = r[h][:, :128] + c_w[h]
                        c_w[h] = c_w[h] + r[h][:, 128:]
                        dz = w_h[h] * (1.0 - sp[:, hc]) - sp[:, hc] * (total[h] - later_w)
                        if m is not None:
                            dz = jnp.where(m[:, hc], dz, 0.0)
                        dz_h.append(dz.astype(BF16))
                    dz_b.append(jnp.concatenate(dz_h, axis=1))
                for j, kb, a, dz in zip(js, kb2, a_b, dz_b):
                    dkt = _dot(q_t, dz, NN)
                    dvt = _dot(do_t, a, NN)
                    dkt_ref[j] += jnp.where(top, dkt[:, :128], dkt[:, 128:])
                    dvt_ref[j] += jnp.where(top, dvt[:, :128], dvt[:, 128:])
                    dq = dq + _dot(dz, kb, NN)
                return c_l[0], c_l[1], c_w[0], c_w[1], dq

            zero = jnp.zeros((qt, 128), F32)
            diag = list(reversed(range(nsub)))
            state = group([i * nsub + m for m in diag], [tile_mask(m) for m in diag], (zero,) * 5)

            def live(c):
                return jnp.logical_and(c[0] < i, jnp.maximum(jnp.max(c[1][0]), jnp.max(c[1][1])) > SB_DEAD)

            def blocks(c):
                jj, st = c
                return jj + 1, group([(i - jj) * nsub - 1 - u for u in range(nsub)], [None] * nsub, st)

            _, state = lax.while_loop(live, blocks, (jnp.int32(0), state))
            dq_ref[rq, :] = (state[4] * SB_SCALE).astype(BF16)
            return 0

        lax.fori_loop(0, T // qt, qtile, 0)

        def untranspose(jb, _):
            rk = pl.ds(pl.multiple_of(jb * CHUNK, CHUNK), CHUNK)
            dk_ref[rk, :] = dkt_ref[jb].T.astype(BF16)
            dv_ref[rk, :] = dvt_ref[jb].T.astype(BF16)
            return 0

        lax.fori_loop(0, T // CHUNK, untranspose, 0)

    def col(off):
        return pl.BlockSpec((T, 128), lambda hp: (0, off + hp))

    o16 = jax.ShapeDtypeStruct((T, BRANCH_W), BF16)
    steps = BRANCH_W // 128
    j = _job_args(job, 5, 3)
    acc = pltpu.VMEM((T // CHUNK, CHUNK, CHUNK), F32)
    res = pl.pallas_call(
        _hosting(body, job, 5, 3, 2, steps), name=name, grid=(steps,),
        in_specs=[col(cb), col(cb + 4), col(cb + 8), col(0), col(0)] + j["in_specs"],
        out_specs=[col(0), col(0), col(0)] + j["out_specs"], out_shape=[o16, o16, o16] + j["out_shape"],
        scratch_shapes=[acc, acc] + j["scratch"], input_output_aliases=j["aliases"],
        compiler_params=_params(("parallel",) if job is None else ("arbitrary",)),
    )(proj, proj, proj, out, dout, *j["ins"])
    return res[0], res[1], res[2], list(res[3:])


_G0 = math.sqrt(2.0 / math.pi)
_G1 = 0.044715


def _gelu(x):
    return 0.5 * x * (1.0 + jnp.tanh(_G0 * (x + _G1 * x * x * x)))


def _gelu_grad(x):
    t = jnp.tanh(_G0 * (x + _G1 * x * x * x))
    return 0.5 * (1.0 + t) + 0.5 * x * (1.0 - t * t) * (_G0 * (1.0 + 3.0 * _G1 * x * x))


def _tril():
    row, lane = _sb_masks()
    return row >= lane


def sgu_fwd(proj, ln_g, ln_b, w, bias, *, name):
    T = proj.shape[0]
    tb = min(512, T)
    G = BRANCH_W // 128

    def body(u_ref, v_ref, g_ref, b_ref, w_ref, bias_ref, o_ref):
        vv = _gelu(v_ref[...])
        xh, _ = _group_norm(vv)
        vn = (xh * g_ref[...] + b_ref[...]).astype(BF16)
        tril = _tril()
        for g in range(G):
            wg = jnp.where(tril, w_ref[g], 0.0).astype(BF16)
            gc = slice(g * 128, (g + 1) * 128)
            for c in range(tb // CHUNK):
                r = slice(c * CHUNK, (c + 1) * CHUNK)
                sv = _dot(wg, vn[r, gc], NN) + bias_ref[g]
                o_ref[r, gc] = _gelu(u_ref[r, gc]) * sv

    cu, cv = C_SGU // BRANCH_W, C_SGU // BRANCH_W + 1
    vec = pl.BlockSpec((1, BRANCH_W), lambda i: (0, 0))
    mat = pl.BlockSpec((G, CHUNK, CHUNK), lambda i: (0, 0, 0))
    return pl.pallas_call(
        body, name=name, grid=(T // tb,),
        in_specs=[pl.BlockSpec((tb, BRANCH_W), lambda i: (i, cu)), pl.BlockSpec((tb, BRANCH_W), lambda i: (i, cv)),
                  vec, vec, mat, mat],
        out_specs=pl.BlockSpec((tb, BRANCH_W), lambda i: (i, 0)),
        out_shape=jax.ShapeDtypeStruct((T, BRANCH_W), F32),
        compiler_params=_params(("parallel",)),
    )(proj, proj, ln_g, ln_b, w, bias)


def sgu_bwd(proj, ln_g, ln_b, w, bias, dout, *, name):
    T = proj.shape[0]
    tb = min(512, T)
    G = BRANCH_W // 128

    def body(u_ref, v_ref, g_ref, b_ref, w_ref, bias_ref, do_ref, dp_ref, dw_ref, dbias_ref, dg_ref, db_ref, dvn_ref):
        @pl.when(pl.program_id(0) == 0)
        def _():
            dw_ref[...] = jnp.zeros_like(dw_ref)
            dbias_ref[...] = jnp.zeros_like(dbias_ref)
            dg_ref[...] = jnp.zeros_like(dg_ref)
            db_ref[...] = jnp.zeros_like(db_ref)

        gv = v_ref[...]
        vv = _gelu(gv)
        xh, rstd = _group_norm(vv)
        vn = (xh * g_ref[...] + b_ref[...]).astype(BF16)
        tril = _tril()
        for g in range(G):
            wg = jnp.where(tril, w_ref[g], 0.0).astype(BF16)
            gc = slice(g * 128, (g + 1) * 128)
            for c in range(tb // CHUNK):
                r = slice(c * CHUNK, (c + 1) * CHUNK)
                vn_c = vn[r, gc]
                sv = _dot(wg, vn_c, NN) + bias_ref[g]
                gu = u_ref[r, gc]
                d_o = do_ref[r, gc]
                dp_ref[r, gc] = (d_o * sv * _gelu_grad(gu)).astype(BF16)
                dsv = d_o * _gelu(gu)
                dsv_b = dsv.astype(BF16)
                dvn_ref[r, gc] = _dot(wg, dsv_b, TN)
                dw_ref[g] += jnp.where(tril, _dot(dsv_b, vn_c, NT), 0.0)
                dbias_ref[g] += jnp.broadcast_to(jnp.sum(dsv, axis=1, keepdims=True), (CHUNK, CHUNK))
        dvn = dvn_ref[...]
        dg_ref[...] += jnp.sum(dvn * xh, axis=0, keepdims=True)
        db_ref[...] += jnp.sum(dvn, axis=0, keepdims=True)
        dxh = dvn * g_ref[...]
        m1 = jnp.mean(dxh, axis=-1, keepdims=True)
        m2 = jnp.mean(dxh * xh, axis=-1, keepdims=True)
        dp_ref[:, BRANCH_W:2 * BRANCH_W] = (rstd * (dxh - m1 - xh * m2) * _gelu_grad(gv)).astype(BF16)

    cu, cv = C_SGU // BRANCH_W, C_SGU // BRANCH_W + 1
    vec = pl.BlockSpec((1, BRANCH_W), lambda i: (0, 0))
    mat = pl.BlockSpec((G, CHUNK, CHUNK), lambda i: (0, 0, 0))
    blk = pl.BlockSpec((tb, BRANCH_W), lambda i: (i, 0))
    msh = jax.ShapeDtypeStruct((G, CHUNK, CHUNK), F32)
    vsh = jax.ShapeDtypeStruct((1, BRANCH_W), F32)
    return pl.pallas_call(
        body, name=name, grid=(T // tb,),
        in_specs=[pl.BlockSpec((tb, BRANCH_W), lambda i: (i, cu)), pl.BlockSpec((tb, BRANCH_W), lambda i: (i, cv)),
                  vec, vec, mat, mat, blk],
        out_specs=[pl.BlockSpec((tb, 2 * BRANCH_W), lambda i: (i, 0)), mat, mat, vec, vec],
        out_shape=[jax.ShapeDtypeStruct((T, 2 * BRANCH_W), BF16), msh, msh, vsh, vsh],
        scratch_shapes=[pltpu.VMEM((tb, BRANCH_W), F32)],
        compiler_params=_params(("arbitrary",)),
    )(proj, proj, ln_g, ln_b, w, bias, dout)


def merge_fwd(a1, a2, a3, p1, p2, p3, proj, *, name):
    T = a1.shape[0]
    tm, tn = min(1024, T), 512
    gb = C_GATE // tn

    def body(a1_ref, a2_ref, a3_ref, p1_ref, p2_ref, p3_ref, g1_ref, g2_ref, g3_ref, m_ref, r1_ref, r2_ref, r3_ref):
        m = None
        for a_ref, p_ref, g_ref, r_ref in ((a1_ref, p1_ref, g1_ref, r1_ref), (a2_ref, p2_ref, g2_ref, r2_ref),
                                           (a3_ref, p3_ref, g3_ref, r3_ref)):
            r = _dot(a_ref[...].astype(BF16), p_ref[...], NN)
            r_ref[...] = r
            t = jax.nn.sigmoid(g_ref[...]) * r
            m = t if m is None else m + t
        m_ref[...] = m.astype(m_ref.dtype)

    a_spec = pl.BlockSpec((tm, BRANCH_W), lambda i, j: (i, 0))
    p_spec = pl.BlockSpec((BRANCH_W, tn), lambda i, j: (0, j))
    o_spec = pl.BlockSpec((tm, tn), lambda i, j: (i, j))
    osh = jax.ShapeDtypeStruct((T, D_MODEL), F32)
    gates = [pl.BlockSpec((tm, tn), functools.partial(lambda i, j, o: (i, o + j), o=gb + 2 * n)) for n in range(3)]
    return pl.pallas_call(
        body, name=name, grid=(T // tm, D_MODEL // tn),
        in_specs=[a_spec, a_spec, a_spec, p_spec, p_spec, p_spec, *gates],
        out_specs=[o_spec] * 4, out_shape=[jax.ShapeDtypeStruct((T, D_MODEL), BF16)] + [osh] * 3,
        compiler_params=_params(("parallel", "parallel")),
    )(a1, a2, a3, p1, p2, p3, proj, proj, proj)


def merge_bwd(dm, r1, r2, r3, proj, *, name):
    T = dm.shape[0]
    tm, tn = min(512, T), 512
    gb = C_GATE // tn

    def body(dm_ref, r1_ref, r2_ref, r3_ref, g1_ref, g2_ref, g3_ref, dr1_ref, dr2_ref, dr3_ref, dg1_ref, dg2_ref, dg3_ref):
        d = dm_ref[...]
        for r_ref, g_ref, dr_ref, dg_ref in ((r1_ref, g1_ref, dr1_ref, dg1_ref), (r2_ref, g2_ref, dr2_ref, dg2_ref),
                                             (r3_ref, g3_ref, dr3_ref, dg3_ref)):
            s = jax.nn.sigmoid(g_ref[...])
            dr_ref[...] = (d * s).astype(BF16)
            dg_ref[...] = (d * r_ref[...] * (s * (1.0 - s))).astype(BF16)

    o_spec = pl.BlockSpec((tm, tn), lambda i, j: (i, j))
    osh = jax.ShapeDtypeStruct((T, D_MODEL), BF16)
    gates = [pl.BlockSpec((tm, tn), functools.partial(lambda i, j, o: (i, o + j), o=gb + 2 * n)) for n in range(3)]
    return pl.pallas_call(
        body, name=name, grid=(T // tm, D_MODEL // tn),
        in_specs=[o_spec] * 4 + gates, out_specs=[o_spec] * 6, out_shape=[osh] * 6,
        compiler_params=_params(("parallel", "parallel")),
    )(dm, r1, r2, r3, proj, proj, proj)


def _rows_call(fn, ins, out_dtypes, *, name, tr=256):
    first = ins[0][0] if isinstance(ins[0], tuple) else ins[0]
    R, C = first.shape[-2:]
    tr = min(tr, R)
    assert R % tr == 0, (name, R, tr)
    arrs, specs = [], []
    for x in ins:
        if isinstance(x, tuple):
            arrs.append(x[0])
            specs.append(pl.BlockSpec((None, tr, C), functools.partial(lambda i, n: (n, i, 0), n=x[1])))
        else:
            arrs.append(x)
            specs.append(pl.BlockSpec((tr, C), lambda i: (i, 0)))
    ni = len(arrs)

    def body(*refs):
        vals = fn(*[r[...] for r in refs[:ni]])
        for o_ref, v in zip(refs[ni:], vals):
            o_ref[...] = v.astype(o_ref.dtype)

    res = pl.pallas_call(
        body, name=name, grid=(R // tr,), in_specs=specs,
        out_specs=[pl.BlockSpec((tr, C), lambda i: (i, 0)) for _ in out_dtypes],
        out_shape=[jax.ShapeDtypeStruct((R, C), dt) for dt in out_dtypes],
        compiler_params=_params(("parallel",)),
    )(*arrs)
    return res


def _tile_rows(rows, cols):
    t = 256
    while t > 8 and (t * cols > 512 * 1024 or rows % t):
        t //= 2
    return t


def _rows_at(fn, pos, ins, outs, steps, *, name, aliases=None):
    read = [n for n, (_, s) in enumerate(ins) if s is not ANY]
    ni = len(ins)

    def body(pos_ref, *refs):
        vals = fn(*[refs[n][...] for n in read])
        for o_ref, v in zip(refs[ni:], vals):
            o_ref[...] = v.astype(o_ref.dtype)

    return pl.pallas_call(
        body, name=name,
        grid_spec=pltpu.PrefetchScalarGridSpec(num_scalar_prefetch=1, grid=(steps,), in_specs=[s for _, s in ins],
                                               out_specs=[s for _, s in outs]),
        out_shape=[sh for sh, _ in outs],
        input_output_aliases={1 + i: o for i, o in (aliases or {}).items()},
        compiler_params=_params(("parallel",)),
    )(pos, *[a for a, _ in ins])


def cast_into_whole(pos, w, l, axis, *, name):
    _, r, n = w.shape
    tr = _tile_rows(r, n)
    if axis == 1:
        shape, spec = (r, n * N_CHIPS), pl.BlockSpec((tr, n), lambda i, p: (i, p[3]))
    else:
        shape, spec = (r * N_CHIPS, n), pl.BlockSpec((tr, n), lambda i, p: (p[3] * (r // tr) + i, 0))
    return _rows_at(lambda a: (a,), pos, [(w, pl.BlockSpec((None, tr, n), lambda i, p: (l, i, 0)))],
                    [(jax.ShapeDtypeStruct(shape, BF16), spec)], r // tr, name=name)[0]


def pair_sum(pos, theirs, g32, axis, *, name):
    rows2, cols = theirs.shape
    h = rows2 // (N_CHIPS if axis == 0 else 1)
    tr = _tile_rows(h, cols)
    hb = h // tr
    if axis == 1:
        own = pl.BlockSpec((tr, cols), lambda i, p: (p[2] * hb + i, 0))
    else:
        own = pl.BlockSpec((tr, cols), lambda i, p: ((2 * (i // hb) + p[2]) * hb + i % hb, 0))
    row = pl.BlockSpec((tr, cols), lambda i, p: (i, 0))
    return _rows_at(lambda t, m: (m + t.astype(F32),) * 2, pos, [(theirs, row), (g32, own)],
                    [(jax.ShapeDtypeStruct((rows2, cols), F32), row), (jax.ShapeDtypeStruct((rows2, cols), BF16), row)],
                    rows2 // tr, name=name)


def chip_sum(pos, h32, recv, l, axis, whole, *, name):
    _, depth, h, n = recv.shape
    tr = _tile_rows(h, n)
    hb = h // tr
    if axis == 1:
        mine = pl.BlockSpec((tr, n), lambda i, p: (i, p[3]))
    else:
        mine = pl.BlockSpec((tr, n), lambda i, p: (p[3] * hb + i, 0))
    ins = [(h32, mine)] + [(recv, pl.BlockSpec((None, None, tr, n), functools.partial(lambda i, p, j: (j, l, i, 0), j=j)))
                           for j in range(3)]
    if whole is not None:
        ins.append((whole, ANY))
    return _rows_at(lambda o, a, b, c: (((o + a.astype(F32)) + b.astype(F32)) + c.astype(F32),), pos, ins,
                    [(jax.ShapeDtypeStruct((depth, 2, h, n), F32), pl.BlockSpec((None, None, tr, n), lambda i, p: (l, p[2], i, 0)))],
                    hb, name=name, aliases=None if whole is None else {4: 0})[0]


def _adamw(w, g, m, v):
    m2 = ADAM_B1 * m + (1.0 - ADAM_B1) * g
    v2 = ADAM_B2 * v + (1.0 - ADAM_B2) * (g * g)
    m_hat = m2 / (1.0 - ADAM_B1 ** ADAM_STEP)
    v_hat = v2 / (1.0 - ADAM_B2 ** ADAM_STEP)
    delta = -ADAM_LR * (m_hat / (jnp.sqrt(v_hat) + ADAM_EPS) + ADAM_WD * w)
    return delta, m2, v2


def _place():
    return lax.axis_index("x"), lax.axis_index("y"), lax.axis_index("c")


def _chip_peers(x, y, c):
    return [((1 - x, y, c), 2 * (1 - x) + y), ((x, 1 - y, c), 2 * x + 1 - y), ((1 - x, 1 - y, c), 2 * (1 - x) + 1 - y)]


def _shard_of(ref, axis, k, n):
    start = pl.multiple_of(k * n, 128)
    return ref.at[pl.ds(start, n), :] if axis == 0 else ref.at[:, pl.ds(start, n)]


ANY = pl.BlockSpec(memory_space=pl.ANY)


class CopyJob:
    def __init__(self, ins, out_shape, scratch, copies, aliases=None):
        self.ins, self.out_shape, self.scratch, self.copies = list(ins), list(out_shape), list(scratch), copies
        self.aliases = dict(aliases or {})

    def start(self, ins, outs, sems):
        local, remote, _, _ = self.copies(ins, outs, sems)
        for d in local + remote:
            d.start()

    def finish(self, ins, outs, sems):
        local, remote, arrivals, relays = self.copies(ins, outs, sems)
        for needs, sends, _ in relays:
            for d in needs:
                d.wait_recv()
            for d in sends:
                d.start()
        for d in arrivals + [d for _, _, arrives in relays for d in arrives]:
            d.wait_recv()
        for d in remote + [d for _, sends, _ in relays for d in sends]:
            d.wait_send()
        for d in local:
            d.wait()


def run_job(job, *, name):
    ni, no = len(job.ins), len(job.out_shape)

    def body(*refs):
        parts = refs[:ni], refs[ni:ni + no], refs[ni + no:]
        job.start(*parts)
        job.finish(*parts)

    return pl.pallas_call(
        body, name=name, in_specs=[ANY] * ni, out_specs=[ANY] * no, out_shape=job.out_shape,
        scratch_shapes=job.scratch, input_output_aliases=job.aliases,
    )(*job.ins)


def _job_args(job, n_in, n_out):
    if job is None:
        return dict(ins=[], in_specs=[], out_specs=[], out_shape=[], scratch=[], aliases={})
    return dict(ins=job.ins, in_specs=[ANY] * len(job.ins), out_specs=[ANY] * len(job.out_shape),
                out_shape=job.out_shape, scratch=job.scratch,
                aliases={n_in + i: n_out + o for i, o in job.aliases.items()})


def _hosting(body, job, n_in, n_out, n_scratch, grid):
    if job is None:
        return body
    ji, jo = len(job.ins), len(job.out_shape)
    grid = (grid,) if isinstance(grid, int) else tuple(grid)

    def at(ends):
        hit = None
        for ax, e in enumerate(ends):
            here = pl.program_id(ax) == e
            hit = here if hit is None else jnp.logical_and(hit, here)
        return hit

    def hosted(*refs):
        o = n_in + ji
        s = o + n_out + jo
        parts = refs[n_in:o], refs[o + n_out:s], refs[s + n_scratch:]

        @pl.when(at([0] * len(grid)))
        def _():
            job.start(*parts)

        body(*refs[:n_in], *refs[o:o + n_out], *refs[s:s + n_scratch])

        @pl.when(at([g - 1 for g in grid]))
        def _():
            job.finish(*parts)

    return hosted


def _job_sems(n_remote, n_local):
    return [pltpu.SemaphoreType.DMA((n_remote,)), pltpu.SemaphoreType.DMA((n_remote,)), pltpu.SemaphoreType.DMA((n_local,))]


def gather_job(shards, axes, chips=(0, 1, 2)):
    na = len(shards)

    def copies(ins, outs, sems):
        send, recv, _ = sems
        x, y, c = _place()
        k = 2 * x + y
        remote, relays = [], []
        for a in range(na):
            r = outs[a].shape[0] // (N_CHIPS if axes[a] == 0 else 1)
            n = outs[a].shape[axes[a]] // N_CHIPS
            half = r // 2

            def part(kk, cc, a=a, n=n, half=half):
                rows = pl.ds(pl.multiple_of(cc * half + (kk * n if axes[a] == 0 else 0), 8), half)
                return outs[a].at[rows, :] if axes[a] == 0 else outs[a].at[rows, pl.ds(pl.multiple_of(kk * n, 128), n)]

            needs, passes, lands = [], [], []
            for j, (peer, kp) in enumerate(_chip_peers(x, y, c)):
                if j not in chips:
                    continue
                s = 6 * a + j
                remote.append(pltpu.make_async_remote_copy(part(k, c), part(k, c), send.at[s], recv.at[s],
                                                           device_id=peer, device_id_type=MESH))
                needs.append(pltpu.make_async_remote_copy(part(kp, c), part(kp, c), send.at[s], recv.at[s],
                                                          device_id=peer, device_id_type=MESH))
                passes.append(pltpu.make_async_remote_copy(part(kp, c), part(kp, c), send.at[s + 3], recv.at[s + 3],
                                                           device_id=(x, y, 1 - c), device_id_type=MESH))
                lands.append(pltpu.make_async_remote_copy(part(kp, 1 - c), part(kp, 1 - c), send.at[s + 3], recv.at[s + 3],
                                                          device_id=(x, y, 1 - c), device_id_type=MESH))
            relays.append((needs, passes, lands))
        return [], remote, [], relays

    out_shape = [jax.ShapeDtypeStruct(w.shape, BF16) for w in shards]
    return CopyJob(shards, out_shape, _job_sems(6 * na, 1), copies, {a: a for a in range(na)})


def scatter_job(layers, g16, axes, filled, chips=(0, 1, 2)):
    na = len(axes)

    def shard_shape(a):
        r, c = g16[a].shape
        return (r // N_CHIPS, c) if axes[a] == 0 else (r, c // N_CHIPS)

    def copies(ins, outs, sems):
        send, recv_sems, _ = sems
        x, y, c = _place()
        remote = []
        for a in range(na):
            n = shard_shape(a)[axes[a]]
            for r, (peer, kp) in enumerate(_chip_peers(x, y, c)):
                if r not in chips:
                    continue
                remote.append(pltpu.make_async_remote_copy(_shard_of(ins[a], axes[a], kp, n), outs[a].at[r, layers[a]],
                                                           send.at[3 * a + r], recv_sems.at[3 * a + r],
                                                           device_id=peer, device_id_type=MESH))
        return [], remote, remote, []

    out_shape = [jax.ShapeDtypeStruct((3, DEPTH) + shard_shape(a), BF16) for a in range(na)]
    ins = list(g16)
    aliases = {}
    for a in range(na):
        if filled[a] is not None:
            aliases[len(ins)] = a
            ins.append(filled[a])
    return CopyJob(ins, out_shape, _job_sems(3 * na, 1), copies, aliases)


def pair_job(g16, axes):
    na = len(axes)
    pieces = [1 if ax == 1 else N_CHIPS for ax in axes]

    def copies(ins, outs, sems):
        send, recv, _ = sems
        x, y, c = _place()
        remote = []
        s = 0
        for a in range(na):
            rows = g16[a].shape[0] // (2 * pieces[a])
            for kk in range(pieces[a]):
                src = ins[a].at[pl.ds(pl.multiple_of((2 * kk + 1 - c) * rows, 8), rows), :]
                remote.append(pltpu.make_async_remote_copy(src, outs[a].at[pl.ds(kk * rows, rows), :], send.at[s], recv.at[s],
                                                           device_id=(x, y, 1 - c), device_id_type=MESH))
                s += 1
        return [], remote, remote, []

    out_shape = [jax.ShapeDtypeStruct((g.shape[0] // 2, g.shape[1]), BF16) for g in g16]
    return CopyJob(g16, out_shape, _job_sems(sum(pieces), 1), copies)


def join_job(shards):
    na = len(shards)

    def copies(ins, outs, sems):
        send, recv, _ = sems
        x, y, c = _place()
        remote = [pltpu.make_async_remote_copy(outs[a].at[:, c], outs[a].at[:, c], send.at[a], recv.at[a],
                                               device_id=(x, y, 1 - c), device_id_type=MESH) for a in range(na)]
        lands = [pltpu.make_async_remote_copy(outs[a].at[:, 1 - c], outs[a].at[:, 1 - c], send.at[a], recv.at[a],
                                              device_id=(x, y, 1 - c), device_id_type=MESH) for a in range(na)]
        return [], remote, lands, []

    out_shape = [jax.ShapeDtypeStruct(s.shape, F32) for s in shards]
    return CopyJob(shards, out_shape, _job_sems(na, 1), copies, {a: a for a in range(na)})


def small_job(p):
    def copies(ins, outs, sems):
        send, recv, loc = sems
        x, y, c = _place()
        me = 4 * x + 2 * y + c
        remote, lands = [], []
        for rel in range(1, 8):
            dx, dy, dc = rel >> 2, (rel >> 1) & 1, rel & 1
            peer = (1 - x if dx else x, 1 - y if dy else y, 1 - c if dc else c)
            who = 4 * peer[0] + 2 * peer[1] + peer[2]
            remote.append(pltpu.make_async_remote_copy(ins[0], outs[0].at[me], send.at[rel - 1], recv.at[rel - 1],
                                                       device_id=peer, device_id_type=MESH))
            lands.append(pltpu.make_async_remote_copy(ins[0], outs[0].at[who], send.at[rel - 1], recv.at[rel - 1],
                                                      device_id=peer, device_id_type=MESH))
        return [pltpu.make_async_copy(ins[0], outs[0].at[me], loc.at[0])], remote, lands, []

    return CopyJob([p], [jax.ShapeDtypeStruct((8,) + p.shape, F32)], _job_sems(7, 1), copies)


def small_sum(slots):
    def add(*terms):
        acc = terms[0]
        for t in terms[1:]:
            acc = acc + t
        return (acc,)

    return _rows_call(add, [(slots, d) for d in range(8)], [F32], name="small_sum", tr=8 * 47)[0]


BIG = ("w_in", "p_ret", "p_sb", "p_sgu", "w_out", "w_up", "w_down")
BIG_AXIS = {"w_in": 1, "p_ret": 1, "p_sb": 1, "p_sgu": 1, "w_out": 0, "w_up": 1, "w_down": 0}
SMALL = ("ret_gn_g", "ret_gn_b", "sgu_ln_g", "sgu_ln_b", "sgu_w", "sgu_b", "ln1_g", "ln1_b", "ln2_g", "ln2_b")


def layer_forward(l, x0, W, sm, rope, rconsts, hooks):
    n = f"l{l}_"
    job = hooks.fwd_job(l, "proj")
    proj = matmul(x0, W["w_in"], mode="nn", tm=2048, tn=640, tk=1024, name=n + "proj", job=job)
    if job is not None:
        proj, job_out = proj
        hooks.done(job, job_out)
    retg, raw, states = ret_fwd(proj, *rope, rconsts, sm["ret_gn_g"], sm["ret_gn_b"], name=n + "ret_fwd")
    job = hooks.fwd_job(l, "sb")
    sb, job_out = sb_fwd(proj, name=n + "sb_fwd", job=job)
    if job is not None:
        hooks.done(job, job_out)
    sg = sgu_fwd(proj, sm["sgu_ln_g"], sm["sgu_ln_b"], sm["sgu_w"], sm["sgu_bias"], name=n + "sgu_fwd")
    merged, r1, r2, r3 = merge_fwd(retg, sb, sg, W["p_ret"], W["p_sb"], W["p_sgu"], proj, name=n + "merge_fwd")
    x1, xh1, rs1 = matmul_ln(merged, W["w_out"], x0, sm["ln1_g"], sm["ln1_b"], tk=1024, name=n + "out_ln1")
    job = hooks.fwd_job(l, "up")
    h1 = matmul(x1, W["w_up"], mode="nn", tm=1024, tn=1024, tk=1024, name=n + "up", job=job)
    if job is not None:
        h1, job_out = h1
        hooks.done(job, job_out)
    job = hooks.fwd_job(l, "down")
    res = matmul_ln(h1, W["w_down"], x1, sm["ln2_g"], sm["ln2_b"], pro=_relu2, tk=1024, name=n + "down_ln2", job=job)
    if job is not None:
        res, job_out = res
        hooks.done(job, job_out)
    x2, xh2, rs2 = res
    saved = dict(x0=x0, proj=proj, retg=retg, raw=raw, states=states, sb=sb, sg=sg, merged=merged, r=(r1, r2, r3),
                 x1=x1, xh1=xh1, rs1=rs1, h1=h1, xh2=xh2, rs2=rs2)
    return x2, saved


def layer_backward(l, dx2, s, W, sm, rope, rconsts, hooks):
    n = f"l{l}_"
    two = ((F32, None), (BF16, None))
    gw, gs = {}, {}
    job = hooks.bwd_job(l, "ln2")
    res = ln_bwd(dx2, s["xh2"], s["rs2"], sm["ln2_g"], name=n + "ln2_bwd", job=job)
    if job is not None:
        res, job_out = res
        hooks.done(job, job_out)
    du2, du2h, gs["ln2_g"], gs["ln2_b"] = res
    job = hooks.bwd_job(l, "g_down")
    gw["w_down"] = matmul(s["h1"], du2h, mode="tn", tm=1024, tn=1024, tk=1024, pro=_relu2, outs=two, name=n + "g_down", job=job)
    if job is not None:
        gw["w_down"], job_out = gw["w_down"]
        hooks.done(job, job_out)
    dh1 = matmul(du2h, W["w_down"], mode="nt", tm=1024, tn=1024, tk=1024, outs=((BF16, None),),
                 epi=lambda acc, h: (acc * (2.0 * jnp.maximum(h, 0.0)),), tiles=(s["h1"],), name=n + "d_h1")
    job = hooks.bwd_job(l, "g_up")
    gw["w_up"] = matmul(s["x1"], dh1, mode="tn", tm=1024, tn=1024, tk=1024, outs=two, name=n + "g_up", job=job)
    if job is not None:
        gw["w_up"], job_out = gw["w_up"]
        hooks.done(job, job_out)
    dx1 = matmul(dh1, W["w_up"], mode="nt", tm=1024, tn=1024, tk=1024,
                 epi=lambda acc, d: (acc + ALPHA * d,), tiles=(du2,), name=n + "d_x1")
    du1, du1h, gs["ln1_g"], gs["ln1_b"] = ln_bwd(dx1, s["xh1"], s["rs1"], sm["ln1_g"], name=n + "ln1_bwd")
    gw["w_out"] = matmul(s["merged"], du1h, mode="tn", tm=1024, tn=1024, tk=1024, outs=two, name=n + "g_out")
    dmerged = matmul(du1h, W["w_out"], mode="nt", tm=1024, tn=1024, tk=1024, name=n + "d_merged")
    dr1, dr2, dr3, dg1, dg2, dg3 = merge_bwd(dmerged, *s["r"], s["proj"], name=n + "merge_bwd")
    d_branch = {}
    for nm, a, dr in (("p_ret", s["retg"], dr1), ("p_sb", s["sb"], dr2), ("p_sgu", s["sg"], dr3)):
        gw[nm] = matmul(a, dr, mode="tn", tm=512, tn=1024, tk=1024, outs=two, name=n + "g_" + nm)
        d_branch[nm] = matmul(dr, W[nm], mode="nt", tm=1024, tn=512, tk=1024, name=n + "d_" + nm)
    job = hooks.pair(l, gw)
    dret, gs["ret_gn_g"], gs["ret_gn_b"], job_out = ret_bwd(s["proj"], *rope, rconsts, sm["ret_gn_g"], sm["ret_gn_b"],
                                                             s["raw"], s["states"], d_branch["p_ret"], name=n + "ret_bwd", job=job)
    if job is not None:
        hooks.done(job, job_out)
    job = hooks.scatter(l) if job is not None else None
    dsq, dsk, dsv, job_out = sb_bwd(s["proj"], s["sb"], d_branch["p_sb"], name=n + "sb_bwd", job=job)
    if job is not None:
        hooks.done(job, job_out)
    dsgu, gs["sgu_w"], dbias, gs["sgu_ln_g"], gs["sgu_ln_b"] = sgu_bwd(
        s["proj"], sm["sgu_ln_g"], sm["sgu_ln_b"], sm["sgu_w"], sm["sgu_bias"], d_branch["p_sgu"], name=n + "sgu_bwd")
    gs["sgu_b"] = dbias[:, :, 0]
    dproj = jnp.concatenate([dret, dsq, dsk, dsv, dsgu, dg1, dg2, dg3], axis=1)
    job = hooks.small(l, gs)
    gw["w_in"] = matmul(s["x0"], dproj, mode="tn", tm=1024, tn=1920, tk=1024, outs=two, name=n + "g_in", job=job)
    if job is not None:
        gw["w_in"], job_out = gw["w_in"]
        hooks.done(job, job_out)
    job = hooks.tail(l, gw["w_in"])
    dx0 = matmul(dproj, W["w_in"], mode="nt", tm=1024, tn=1024, tk=1536,
                 epi=lambda acc, d: (acc + ALPHA * d,), tiles=(du1,), name=n + "d_x0", job=job)
    if job is not None:
        dx0, job_out = dx0
        hooks.done(job, job_out)
    return dx0, gw, gs


def local_step(x, target, small, plan):
    T = x.shape[0]
    rope = _rope_tables(T)
    rconsts = _ret_consts()
    sms = []
    for l in range(DEPTH):
        sm = {k: small[k][l][None, :] for k in SMALL if k not in ("sgu_w", "sgu_b")}
        sm["sgu_w"] = small["sgu_w"][l]
        sm["sgu_bias"] = jnp.broadcast_to(small["sgu_b"][l][:, :, None], (4, CHUNK, CHUNK))
        sms.append(sm)
    h, saved = x, []
    for l in range(DEPTH):
        h, s = layer_forward(l, h, plan.weights(l), sms[l], rope, rconsts, plan)
        saved.append(s)
    dy, sq = loss_head(h, target)
    gs = {k: [None] * DEPTH for k in SMALL}
    for l in reversed(range(DEPTH)):
        dy, gwl, gsl = layer_backward(l, dy, saved[l], plan.weights(l), sms[l], rope, rconsts, plan)
        plan.grads(l, gwl)
        for k in SMALL:
            gs[k][l] = gsl[k].reshape(small[k].shape[1:])
    return sq[0, 0], dy, {k: jnp.stack(v) for k, v in gs.items()}


EARLY_GRADS = ("p_ret", "p_sb", "p_sgu", "w_out", "w_up", "w_down")


class _StepPlan:
    def __init__(self, pos, shards16):
        self.pos = pos
        self.shards16 = shards16
        self.full = [dict() for _ in range(DEPTH)]
        self.gw = [None] * DEPTH
        self.bufs = {}
        self.sums = {}
        self.gs = [None] * DEPTH
        first = self._gather([(0, "w_in")])
        self.done(first, run_job(first, name="gather_first"))

    def weights(self, l):
        return self.full[l]

    def grads(self, l, gw):
        self.gw[l] = gw

    def _gather(self, items, chips=(0, 1, 2)):
        job = gather_job([self.shards16[l][k] for l, k in items], [BIG_AXIS[k] for _, k in items], chips)
        job.note = ("gather" if 2 in chips else "gather_part", items)
        return job

    def _pair(self, items):
        job = pair_job([g[1] for _, _, g in items], [BIG_AXIS[k] for _, k, _ in items])
        job.note = ("pair", items)
        return job

    def fwd_job(self, l, host):
        if host == "proj":
            return self._gather([(l, "w_down")])
        if host == "sb":
            return self._gather([(l, k) for k in ("p_ret", "p_sb", "p_sgu", "w_out", "w_up")])
        if l + 1 == DEPTH:
            return None
        return self._gather([(l + 1, "w_in")], (0, 1) if host == "up" else (2,))

    def bwd_job(self, l, host):
        if l + 1 == DEPTH:
            return None
        if host == "ln2":
            job = self._pair([(l + 1, "w_in", self.gw[l + 1]["w_in"])])
            job.note = ("pair_w_in", job.note[1])
            return job
        items, sums16 = self.summed_w_in
        job = scatter_job([l_ for l_, _, _ in items], sums16, [BIG_AXIS[k] for _, k, _ in items],
                          [self.bufs.get(k) for _, k, _ in items], (0, 1) if host == "g_down" else (2,))
        job.note = ("scatter", items)
        return job

    def pair(self, l, ready):
        return self._pair([(l, k, ready[k]) for k in EARLY_GRADS])

    def scatter(self, l):
        items, sums16 = self.summed
        job = scatter_job([l_ for l_, _, _ in items], sums16, [BIG_AXIS[k] for _, k, _ in items],
                          [self.bufs.get(k) for _, k, _ in items])
        job.note = ("scatter", items)
        return job

    def small(self, l, gs):
        self.gs[l] = {k: gs[k].reshape(-1) for k in SMALL}
        if l != 0:
            return None
        job = small_job(_pack_small({k: jnp.stack([self.gs[l_][k] for l_ in range(DEPTH)]) for k in SMALL}))
        job.note = ("small", [])
        return job

    def tail(self, l, g):
        if l != 0:
            return None
        last = self._pair([(0, "w_in", g)])
        self.done(last, run_job(last, name="pair_last"))
        return self.scatter(0)

    def done(self, job, outs):
        kind, items = job.note
        if kind == "small":
            self.small_slots = outs[0]
        if kind in ("pair", "pair_w_in"):
            sums16 = []
            for a, (l, k, g) in enumerate(items):
                self.sums[(l, k)], s16 = pair_sum(self.pos, outs[a], g[0], BIG_AXIS[k], name=f"pair_sum_{k}_{l}")
                sums16.append(s16)
            if kind == "pair":
                self.summed = (items, sums16)
            else:
                self.summed_w_in = (items, sums16)
        for a, item in enumerate(items):
            if kind == "gather_part":
                self.shards16[item[0]][item[1]] = outs[a]
            elif kind == "gather":
                self.full[item[0]][item[1]] = outs[a]
            elif kind == "scatter":
                self.bufs[item[1]] = outs[a]

    def finish(self):
        return self.bufs, self.sums


def _flat2(a):
    return a.reshape(-1, a.shape[-1])


def _pack_small(d, pre=""):
    return jnp.concatenate([d[pre + k].reshape(-1) for k in SMALL]).reshape(-1, 128)


def kernel(x, w_in, ret_gn_g, ret_gn_b, sgu_ln_g, sgu_ln_b, sgu_w, sgu_b, p_ret, p_sb, p_sgu, w_out, ln1_g, ln1_b, w_up, w_down, ln2_g, ln2_b, loss_target, m_w_in, m_ret_gn_g, m_ret_gn_b, m_sgu_ln_g, m_sgu_ln_b, m_sgu_w, m_sgu_b, m_p_ret, m_p_sb, m_p_sgu, m_w_out, m_ln1_g, m_ln1_b, m_w_up, m_w_down, m_ln2_g, m_ln2_b, v_w_in, v_ret_gn_g, v_ret_gn_b, v_sgu_ln_g, v_sgu_ln_b, v_sgu_w, v_sgu_b, v_p_ret, v_p_sb, v_p_sgu, v_w_out, v_ln1_g, v_ln1_b, v_w_up, v_w_down, v_ln2_g, v_ln2_b):
    given = dict(locals())
    order = BIG[:1] + SMALL[:6] + BIG[1:5] + SMALL[6:8] + BIG[5:7] + SMALL[8:10]
    L = DEPTH

    px, py, pc = _place()
    pos = jnp.stack([px, py, pc, 2 * px + py]).astype(jnp.int32)

    shards16 = [{k: cast_into_whole(pos, given[k], l, BIG_AXIS[k], name=f"cast_{k}_{l}") for k in BIG} for l in range(L)]
    plan = _StepPlan(pos, shards16)
    sq, dx, gs = local_step(x[0], loss_target[0], {k: given[k] for k in SMALL}, plan)
    loss = 0.5 * lax.psum(sq, ("x", "y", "c"))

    bufs, sums = plan.finish()
    shards = []
    for k in BIG:
        whole = None
        for l in range(L):
            whole = chip_sum(pos, sums[(l, k)], bufs[k], l, BIG_AXIS[k], whole, name=f"chip_sum_{k}_{l}")
        shards.append(whole)
    joined = run_job(join_job(shards), name="join_halves")
    out = {}
    for a, k in enumerate(BIG):
        shp = given[k].shape
        res = _rows_call(lambda g_, w_, m_, v_: (g_,) + _adamw(w_, g_, m_, v_),
                         [joined[a].reshape(-1, shp[-1]), _flat2(given[k]), _flat2(given["m_" + k]), _flat2(given["v_" + k])],
                         [F32] * 4, name="adamw_" + k)
        out[k] = [r.reshape(shp) for r in res]

    pack = _pack_small
    res = _rows_call(lambda g_, w_, m_, v_: (g_,) + _adamw(w_, g_, m_, v_),
                     [small_sum(plan.small_slots), pack(given), pack(given, "m_"), pack(given, "v_")], [F32] * 4,
                     name="adamw_small", tr=8 * 47)
    off = 0
    for k in SMALL:
        sz = given[k].size
        out[k] = [r.reshape(-1)[off:off + sz].reshape(given[k].shape) for r in res]
        off += sz

    grads = [out[k][0] for k in order]
    deltas = [out[k][1] for k in order]
    new_m = [out[k][2] for k in order]
    new_v = [out[k][3] for k in order]
    return (loss, dx[None], *grads, *deltas, *new_m, *new_v)
```

```python
import functools
import math

import jax
import jax.numpy as jnp
from jax import lax
from jax.experimental import pallas as pl
from jax.experimental.pallas import tpu as pltpu

F32 = jnp.float32
BF16 = jnp.bfloat16

D_MODEL = 1024
SEQ = 4096
DEPTH = 2
CHUNK = 128
RET_HEADS = 4
BRANCH_W = 512
N_IN = 7680
D_FF = 4096
LN_EPS = 1e-5
ROPE_BASE = 10000.0
ALPHA = (2 * DEPTH) ** 0.25
RET_SCALE = 128 ** -0.5
SB_SCALE = 64 ** -0.5
C_RET, C_SB, C_SGU, C_GATE = 0, 2048, 3584, 4608

ADAM_LR, ADAM_B1, ADAM_B2, ADAM_EPS, ADAM_WD, ADAM_STEP = 0.001, 0.9, 0.999, 1e-08, 0.01, 10

N_CHIPS = 4
VMEM_LIMIT = 56 * 1024 * 1024
MESH = pl.DeviceIdType.MESH

NN = ((1,), (0,))
NT = ((1,), (1,))
TN = ((0,), (0,))


def _dot(a, b, dims):
    return lax.dot_general(a, b, (dims, ((), ())), preferred_element_type=F32)


def _params(sem):
    return pltpu.CompilerParams(dimension_semantics=sem, vmem_limit_bytes=VMEM_LIMIT)


def _relu2(h):
    r = jnp.maximum(h, 0.0)
    return r * r


def matmul(a, b, *, mode, tm, tn, tk, outs=((F32, None),), pro=None, epi=None, tiles=(), rows=(), name, job=None):
    if mode == "nn":
        (M, K), N = a.shape, b.shape[1]
    elif mode == "nt":
        (M, K), N = a.shape, b.shape[0]
    else:
        (K, M), N = a.shape, b.shape[1]
    tm, tn, tk = min(tm, M), min(tn, N), min(tk, K)
    assert M % tm == 0 and N % tn == 0 and K % tk == 0, (name, M, N, K, tm, tn, tk)
    if mode == "nn":
        a_spec = pl.BlockSpec((tm, tk), lambda i, j, k: (i, k))
        b_spec = pl.BlockSpec((tk, tn), lambda i, j, k: (k, j))
        dims = NN
    elif mode == "nt":
        a_spec = pl.BlockSpec((tm, tk), lambda i, j, k: (i, k))
        b_spec = pl.BlockSpec((tn, tk), lambda i, j, k: (j, k))
        dims = NT
    else:
        a_spec = pl.BlockSpec((tk, tm), lambda i, j, k: (k, i))
        b_spec = pl.BlockSpec((tk, tn), lambda i, j, k: (k, j))
        dims = TN
    nk = K // tk
    nt_, nr, no = len(tiles), len(rows), len(outs)

    def body(a_ref, b_ref, *rest):
        tile_refs = rest[:nt_]
        row_refs = rest[nt_:nt_ + nr]
        out_refs = rest[nt_ + nr:nt_ + nr + no]
        av = a_ref[...]
        if pro is not None:
            av = pro(av)
        p = _dot(av.astype(BF16), b_ref[...].astype(BF16), dims)

        def finish(acc):
            vals = (acc,) * no if epi is None else epi(acc, *[r[...] for r in tile_refs], *[r[...] for r in row_refs])
            for o_ref, v in zip(out_refs, vals):
                o_ref[...] = v.astype(o_ref.dtype)

        if nk == 1:
            finish(p)
        else:
            acc_ref = rest[-1]
            k = pl.program_id(2)

            @pl.when(k == 0)
            def _():
                acc_ref[...] = p

            @pl.when(k > 0)
            def _():
                acc_ref[...] += p

            @pl.when(k == nk - 1)
            def _():
                finish(acc_ref[...])

    out_shape, out_specs = [], []
    for dt, width in outs:
        if width is None:
            out_shape.append(jax.ShapeDtypeStruct((M, N), dt))
            out_specs.append(pl.BlockSpec((tm, tn), lambda i, j, k: (i, j)))
        else:
            assert N == tn
            out_shape.append(jax.ShapeDtypeStruct((M, width), dt))
            out_specs.append(pl.BlockSpec((tm, width), lambda i, j, k: (i, 0)))
    in_specs = [a_spec, b_spec]
    in_specs += [pl.BlockSpec((tm, tn), lambda i, j, k: (i, j)) for _ in tiles]
    in_specs += [pl.BlockSpec((1, tn), lambda i, j, k: (0, j)) for _ in rows]
    grid = (M // tm, N // tn, nk)
    scratch = [pltpu.VMEM((tm, tn), F32)] if nk > 1 else []
    j = _job_args(job, len(in_specs), no)
    res = pl.pallas_call(
        _hosting(body, job, len(in_specs), no, len(scratch), grid), name=name, grid=grid,
        in_specs=in_specs + j["in_specs"], out_specs=out_specs + j["out_specs"], out_shape=out_shape + j["out_shape"],
        scratch_shapes=scratch + j["scratch"], input_output_aliases=j["aliases"],
        compiler_params=_params(("parallel", "parallel", "arbitrary") if job is None else ("arbitrary",) * 3),
    )(a, b, *tiles, *rows, *j["ins"])
    mine = res[0] if no == 1 else list(res[:no])
    return mine if job is None else (mine, list(res[no:]))


def _ln_epi(acc, res, g, b):
    u = ALPHA * res + acc
    mu = jnp.mean(u, axis=-1, keepdims=True)
    xc = u - mu
    var = jnp.mean(xc * xc, axis=-1, keepdims=True)
    rstd = lax.rsqrt(var + LN_EPS)
    xhat = xc * rstd
    return xhat * g + b, xhat, jnp.broadcast_to(rstd, (u.shape[0], 128))


def matmul_ln(a, w, res, g, b, *, pro=None, tk, name, job=None):
    n = w.shape[1]
    return matmul(a, w, mode="nn", tm=1024, tn=n, tk=tk, pro=pro, epi=_ln_epi, tiles=(res,), rows=(g, b),
                  outs=((F32, None), (F32, None), (F32, 128)), name=name, job=job)


def ln_bwd(dy, xhat, rstd, g, *, name, job=None):
    T, D = dy.shape
    tm = min(512, T)

    def body(dy_ref, xh_ref, rs_ref, g_ref, du_ref, du16_ref, dg_ref, db_ref):
        dyv, xh = dy_ref[...], xh_ref[...]
        r = rs_ref[:, 0:1]
        dxh = dyv * g_ref[...]
        m1 = jnp.mean(dxh, axis=-1, keepdims=True)
        m2 = jnp.mean(dxh * xh, axis=-1, keepdims=True)
        du = r * (dxh - m1 - xh * m2)
        du_ref[...] = du
        du16_ref[...] = du.astype(BF16)

        @pl.when(pl.program_id(0) == 0)
        def _():
            dg_ref[...] = jnp.zeros_like(dg_ref)
            db_ref[...] = jnp.zeros_like(db_ref)

        dg_ref[...] += jnp.sum(dyv * xh, axis=0, keepdims=True)
        db_ref[...] += jnp.sum(dyv, axis=0, keepdims=True)

    row = pl.BlockSpec((tm, D), lambda i: (i, 0))
    vec = pl.BlockSpec((1, D), lambda i: (0, 0))
    j = _job_args(job, 4, 4)
    res = pl.pallas_call(
        _hosting(body, job, 4, 4, 0, T // tm), name=name, grid=(T // tm,),
        in_specs=[row, row, pl.BlockSpec((tm, 128), lambda i: (i, 0)), vec] + j["in_specs"],
        out_specs=[row, row, vec, vec] + j["out_specs"],
        out_shape=[jax.ShapeDtypeStruct((T, D), F32), jax.ShapeDtypeStruct((T, D), BF16),
                   jax.ShapeDtypeStruct((1, D), F32), jax.ShapeDtypeStruct((1, D), F32)] + j["out_shape"],
        scratch_shapes=j["scratch"], input_output_aliases=j["aliases"],
        compiler_params=_params(("arbitrary",)),
    )(dy, xhat, rstd, g, *j["ins"])
    return list(res[:4]) if job is None else (list(res[:4]), list(res[4:]))


def loss_head(y, target):
    T, D = y.shape
    tm = min(512, T)

    def body(y_ref, t_ref, dy_ref, s_ref):
        e = y_ref[...] - t_ref[...]
        dy_ref[...] = e * (1.0 / D)

        @pl.when(pl.program_id(0) == 0)
        def _():
            s_ref[...] = jnp.zeros_like(s_ref)

        s_ref[...] += jnp.sum(jnp.mean(e * e, axis=-1, keepdims=True))

    row = pl.BlockSpec((tm, D), lambda i: (i, 0))
    return pl.pallas_call(
        body, name="loss_head", grid=(T // tm,),
        in_specs=[row, row], out_specs=[row, pl.BlockSpec((8, 128), lambda i: (0, 0))],
        out_shape=[jax.ShapeDtypeStruct((T, D), F32), jax.ShapeDtypeStruct((8, 128), F32)],
        compiler_params=_params(("arbitrary",)),
    )(y, target)


def _rope_tables(T):
    half = 64
    inv_freq = ROPE_BASE ** (-jnp.arange(half, dtype=F32) / half)
    ang = jnp.arange(T, dtype=jnp.int32).astype(F32)[:, None] * inv_freq[None, :]
    cos, sin = jnp.cos(ang), jnp.sin(ang)
    return jnp.concatenate([cos, cos], axis=1), jnp.concatenate([-sin, sin], axis=1)


def _ret_consts():
    H = RET_HEADS
    log_g = jnp.log(1.0 - 2.0 ** (-5.0 - jnp.arange(H, dtype=F32)))
    idx = jnp.arange(CHUNK, dtype=F32)
    diff = idx[:, None] - idx[None, :]
    dmat = jnp.where(diff[None] >= 0, jnp.exp(log_g[:, None, None] * diff[None]), 0.0)
    kd = jnp.exp(log_g[:, None] * (CHUNK - 1 - idx)[None, :])
    qd = jnp.exp(log_g[:, None] * (idx + 1.0)[None, :])
    cd = jnp.exp(log_g * CHUNK)
    full = (H, CHUNK, CHUNK)
    return (dmat.astype(F32), jnp.broadcast_to(kd[:, :, None], full), jnp.broadcast_to(qd[:, :, None], full),
            jnp.broadcast_to(cd[:, None, None], full))


def _swap_halves(v):
    return pltpu.roll(v, 64, 1)


def _group_norm(o):
    mu = jnp.mean(o, axis=-1, keepdims=True)
    xc = o - mu
    var = jnp.mean(xc * xc, axis=-1, keepdims=True)
    rstd = lax.rsqrt(var + LN_EPS)
    return xc * rstd, rstd


def ret_fwd(proj, cosf, sinf, consts, gn_g, gn_b, *, name):
    T = proj.shape[0]
    tb = min(512, T)
    nch = tb // CHUNK
    H = RET_HEADS

    def body(p_ref, cos_ref, sin_ref, dm_ref, kd_ref, qd_ref, cd_ref, g_ref, b_ref, out_ref, raw_ref, st_ref, s_ref):
        @pl.when(pl.program_id(0) == 0)
        def _():
            s_ref[...] = jnp.zeros_like(s_ref)

        for c in range(nch):
            r = slice(c * CHUNK, (c + 1) * CHUNK)
            cs, sn = cos_ref[r, :], sin_ref[r, :]
            for h in range(H):
                hc = slice(h * 128, (h + 1) * 128)
                q = p_ref[r, h * 128:(h + 1) * 128]
                k = p_ref[r, 512 + h * 128:512 + (h + 1) * 128]
                v = p_ref[r, 1024 + h * 128:1024 + (h + 1) * 128]
                gt = p_ref[r, 1536 + h * 128:1536 + (h + 1) * 128]
                qr = q * cs + _swap_halves(q) * sn
                kr = (k * cs + _swap_halves(k) * sn) * RET_SCALE
                sprev = s_ref[h]
                st_ref[c, h] = sprev
                qb, kb, vb = qr.astype(BF16), kr.astype(BF16), v.astype(BF16)
                s = _dot(qb, kb, NT) * dm_ref[h]
                o = _dot(s.astype(BF16), vb, NN) + _dot((qr * qd_ref[h]).astype(BF16), sprev.astype(BF16), NN)
                s_ref[h] = sprev * cd_ref[h] + _dot((kr * kd_ref[h]).astype(BF16), vb, TN)
                raw_ref[r, hc] = o
                y, _ = _group_norm(o)
                out_ref[r, hc] = (gt * jax.nn.sigmoid(gt)) * (y * g_ref[:, hc] + b_ref[:, hc])

    cmat = pl.BlockSpec((H, CHUNK, CHUNK), lambda i: (0, 0, 0))
    vec = pl.BlockSpec((1, BRANCH_W), lambda i: (0, 0))
    rope = pl.BlockSpec((tb, 128), lambda i: (i, 0))
    blk = pl.BlockSpec((tb, BRANCH_W), lambda i: (i, 0))
    return pl.pallas_call(
        body, name=name, grid=(T // tb,),
        in_specs=[pl.BlockSpec((tb, 2048), lambda i: (i, 0)), rope, rope, cmat, cmat, cmat, cmat, vec, vec],
        out_specs=[blk, blk, pl.BlockSpec((nch, H, CHUNK, CHUNK), lambda i: (i, 0, 0, 0))],
        out_shape=[jax.ShapeDtypeStruct((T, BRANCH_W), F32), jax.ShapeDtypeStruct((T, BRANCH_W), F32),
                   jax.ShapeDtypeStruct((T // CHUNK, H, CHUNK, CHUNK), F32)],
        scratch_shapes=[pltpu.VMEM((H, CHUNK, CHUNK), F32)],
        compiler_params=_params(("arbitrary",)),
    )(proj, cosf, sinf, *consts, gn_g, gn_b)


def ret_bwd(proj, cosf, sinf, consts, gn_g, gn_b, raw, states, dout, *, name, job=None):
    T = proj.shape[0]
    tb = min(512, T)
    nch = tb // CHUNK
    nb = T // tb
    H = RET_HEADS

    def body(p_ref, cos_ref, sin_ref, dm_ref, kd_ref, qd_ref, cd_ref, g_ref, b_ref, raw_ref, st_ref, do_ref,
             dp_ref, dg_ref, db_ref, ds_ref):
        @pl.when(pl.program_id(0) == 0)
        def _():
            ds_ref[...] = jnp.zeros_like(ds_ref)
            dg_ref[...] = jnp.zeros_like(dg_ref)
            db_ref[...] = jnp.zeros_like(db_ref)

        for c in reversed(range(nch)):
            r = slice(c * CHUNK, (c + 1) * CHUNK)
            cs, sn = cos_ref[r, :], sin_ref[r, :]
            for h in range(H):
                hc = slice(h * 128, (h + 1) * 128)
                q = p_ref[r, h * 128:(h + 1) * 128]
                k = p_ref[r, 512 + h * 128:512 + (h + 1) * 128]
                v = p_ref[r, 1024 + h * 128:1024 + (h + 1) * 128]
                gt = p_ref[r, 1536 + h * 128:1536 + (h + 1) * 128]
                qr = q * cs + _swap_halves(q) * sn
                kr = (k * cs + _swap_halves(k) * sn) * RET_SCALE
                sprev = st_ref[c, h]
                gv = g_ref[:, hc]
                y, rstd = _group_norm(raw_ref[r, hc])
                d_out = do_ref[r, hc]
                sg = jax.nn.sigmoid(gt)
                d_gate = d_out * (y * gv + b_ref[:, hc]) * (sg * (1.0 + gt * (1.0 - sg)))
                d_aff = d_out * (gt * sg)
                dg_ref[:, hc] += jnp.sum(d_aff * y, axis=0, keepdims=True)
                db_ref[:, hc] += jnp.sum(d_aff, axis=0, keepdims=True)
                dxh = d_aff * gv
                m1 = jnp.mean(dxh, axis=-1, keepdims=True)
                m2 = jnp.mean(dxh * y, axis=-1, keepdims=True)
                d_o = (rstd * (dxh - m1 - y * m2)).astype(BF16)
                qb, kb, vb = qr.astype(BF16), kr.astype(BF16), v.astype(BF16)
                dm, kd, qd = dm_ref[h], kd_ref[h], qd_ref[h]
                p = (_dot(qb, kb, NT) * dm).astype(BF16)
                dp = (_dot(d_o, vb, NT) * dm).astype(BF16)
                dsn = ds_ref[h]
                dsb = dsn.astype(BF16)
                dq_r = _dot(dp, kb, NN) + _dot(d_o, sprev.astype(BF16), NT) * qd
                dk_r = (_dot(dp, qb, TN) + _dot(vb, dsb, NT) * kd) * RET_SCALE
                d_v = _dot(p, d_o, TN) + _dot((kr * kd).astype(BF16), dsb, NN)
                ds_ref[h] = dsn * cd_ref[h] + _dot((qr * qd).astype(BF16), d_o, TN)
                dp_ref[r, h * 128:(h + 1) * 128] = (dq_r * cs - _swap_halves(dq_r) * sn).astype(BF16)
                dp_ref[r, 512 + h * 128:512 + (h + 1) * 128] = (dk_r * cs - _swap_halves(dk_r) * sn).astype(BF16)
                dp_ref[r, 1024 + h * 128:1024 + (h + 1) * 128] = d_v.astype(BF16)
                dp_ref[r, 1536 + h * 128:1536 + (h + 1) * 128] = d_gate.astype(BF16)

    cmat = pl.BlockSpec((H, CHUNK, CHUNK), lambda i: (0, 0, 0))
    vec = pl.BlockSpec((1, BRANCH_W), lambda i: (0, 0))
    rope = pl.BlockSpec((tb, 128), lambda i: (nb - 1 - i, 0))
    blk = pl.BlockSpec((tb, BRANCH_W), lambda i: (nb - 1 - i, 0))
    wide = pl.BlockSpec((tb, 2048), lambda i: (nb - 1 - i, 0))
    j = _job_args(job, 12, 3)
    res = pl.pallas_call(
        _hosting(body, job, 12, 3, 1, nb), name=name, grid=(nb,),
        in_specs=[wide, rope, rope, cmat, cmat, cmat, cmat, vec, vec, blk,
                  pl.BlockSpec((nch, H, CHUNK, CHUNK), lambda i: (nb - 1 - i, 0, 0, 0)), blk] + j["in_specs"],
        out_specs=[wide, vec, vec] + j["out_specs"],
        out_shape=[jax.ShapeDtypeStruct((T, 2048), BF16), jax.ShapeDtypeStruct((1, BRANCH_W), F32),
                   jax.ShapeDtypeStruct((1, BRANCH_W), F32)] + j["out_shape"],
        scratch_shapes=[pltpu.VMEM((H, CHUNK, CHUNK), F32)] + j["scratch"], input_output_aliases=j["aliases"],
        compiler_params=_params(("arbitrary",)),
    )(proj, cosf, sinf, *consts, gn_g, gn_b, raw, states, dout, *j["ins"])
    return res[0], res[1], res[2], list(res[3:])


def _sb_masks():
    row = lax.broadcasted_iota(jnp.int32, (CHUNK, CHUNK), 0)
    lane = lax.broadcasted_iota(jnp.int32, (CHUNK, CHUNK), 1)
    return row, lane


SB_QT = 256
SB_DEAD = -105.0


def _pair(v):
    hi = v.astype(BF16)
    return jnp.concatenate([hi, (v - hi.astype(F32)).astype(BF16)], axis=1)


def _sb_consts():
    r = lax.broadcasted_iota(jnp.int32, (256, 256), 0) & 127
    c = lax.broadcasted_iota(jnp.int32, (256, 256), 1)
    ones = c >= 128
    lane = lax.broadcasted_iota(jnp.int32, (CHUNK, CHUNK), 1)
    return (ones | (r > c)).astype(BF16), (ones | (r >= c)).astype(BF16), (lane < 64, lane >= 64)


def _per_head(x, hms):
    return jnp.concatenate([jnp.where(hm, x, 0.0) for hm in hms], axis=0).astype(BF16)


def _sb_logits(qb, kb2, mask2):
    z = _dot(qb, kb2, NT)
    l1p = jnp.log(1.0 + jnp.exp(-jnp.abs(z)))
    lsp = jnp.minimum(z, 0.0) - l1p
    lsn = lsp - z
    if mask2 is not None:
        lsn = jnp.where(mask2, lsn, 0.0)
    return lsp, lsn


def _sb_tile_mask(qt):
    trow = lax.broadcasted_iota(jnp.int32, (qt, 256), 0)
    tlane = lax.broadcasted_iota(jnp.int32, (qt, 256), 1) & 127
    return lambda m: (tlane + m * CHUNK) < trow


def sb_fwd(proj, *, name, job=None):
    T = proj.shape[0]
    qt = min(SB_QT, T)
    nsub = qt // CHUNK
    cb = C_SB // 128

    def body(q_ref, k_ref, v_ref, o_ref):
        u_gt, _, hms = _sb_consts()
        tile_mask = _sb_tile_mask(qt)

        def qtile(i, _):
            rq = pl.ds(pl.multiple_of(i * qt, qt), qt)
            qb = (q_ref[rq, :] * SB_SCALE).astype(BF16)

            def group(js, masks, state):
                carry, acc = list(state[:2]), state[2]
                rows = [pl.ds(pl.multiple_of(j * CHUNK, CHUNK), CHUNK) for j in js]
                logits = [_sb_logits(qb, _per_head(k_ref[rk, :], hms), m) for rk, m in zip(rows, masks)]
                sums = [[_dot(_pair(lsn[:, h * 128:(h + 1) * 128]), u_gt, NN) for h in range(2)] for _, lsn in logits]
                weights = []
                for (lsp, _), r, m in zip(logits, sums, masks):
                    a_b = []
                    for h in range(2):
                        hc = slice(h * 128, (h + 1) * 128)
                        a = jnp.exp(lsp[:, hc] + r[h][:, :128] + carry[h])
                        if m is not None:
                            a = jnp.where(m[:, hc], a, 0.0)
                        carry[h] = carry[h] + r[h][:, 128:]
                        a_b.append(a.astype(BF16))
                    weights.append(jnp.concatenate(a_b, axis=1))
                for rk, a in zip(rows, weights):
                    acc = acc + _dot(a, _per_head(v_ref[rk, :], hms), NN)
                return carry[0], carry[1], acc

            zero = jnp.zeros((qt, 128), F32)
            diag = list(reversed(range(nsub)))
            state = group([i * nsub + m for m in diag], [tile_mask(m) for m in diag], (zero, zero, zero))

            def live(c):
                return jnp.logical_and(c[0] < i, jnp.maximum(jnp.max(c[1][0]), jnp.max(c[1][1])) > SB_DEAD)

            def blocks(c):
                jj, st = c
                return jj + 1, group([(i - jj) * nsub - 1 - u for u in range(nsub)], [None] * nsub, st)

            _, state = lax.while_loop(live, blocks, (jnp.int32(0), state))
            o_ref[rq, :] = state[2]
            return 0

        lax.fori_loop(0, T // qt, qtile, 0)

    def col(off):
        return pl.BlockSpec((T, 128), lambda hp: (0, off + hp))

    steps = BRANCH_W // 128
    j = _job_args(job, 3, 1)
    res = pl.pallas_call(
        _hosting(body, job, 3, 1, 0, steps), name=name, grid=(steps,),
        in_specs=[col(cb), col(cb + 4), col(cb + 8)] + j["in_specs"], out_specs=[col(0)] + j["out_specs"],
        out_shape=[jax.ShapeDtypeStruct((T, BRANCH_W), F32)] + j["out_shape"],
        scratch_shapes=j["scratch"], input_output_aliases=j["aliases"],
        compiler_params=_params(("parallel",) if job is None else ("arbitrary",)),
    )(proj, proj, proj, *j["ins"])
    return res[0], list(res[1:])


def sb_bwd(proj, out, dout, *, name, job=None):
    T = proj.shape[0]
    qt = min(SB_QT, T)
    nsub = qt // CHUNK
    cb = C_SB // 128

    def body(q_ref, k_ref, v_ref, o_ref, do_ref, dq_ref, dk_ref, dv_ref, dkt_ref, dvt_ref):
        u_gt, u_ge, hms = _sb_consts()
        tile_mask = _sb_tile_mask(qt)
        tall_lane = lax.broadcasted_iota(jnp.int32, (qt, 128), 1)
        top = lax.broadcasted_iota(jnp.int32, (CHUNK, CHUNK), 0) < 64
        dkt_ref[...] = jnp.zeros_like(dkt_ref)
        dvt_ref[...] = jnp.zeros_like(dvt_ref)

        def qtile(i, _):
            rq = pl.ds(pl.multiple_of(i * qt, qt), qt)
            qs = q_ref[rq, :] * SB_SCALE
            qb, q_t = qs.astype(BF16), qs.T.astype(BF16)
            dov = do_ref[rq, :]
            dob, do_t = dov.astype(BF16), dov.T.astype(BF16)
            prod = dob.astype(F32) * o_ref[rq, :]
            total = [jnp.broadcast_to(jnp.sum(jnp.where(hm, prod, 0.0), axis=1, keepdims=True), (qt, 128))
                     for hm in (tall_lane < 64, tall_lane >= 64)]

            def group(js, masks, state):
                c_l, c_w, dq = list(state[:2]), list(state[2:4]), state[4]
                heads = [slice(h * 128, (h + 1) * 128) for h in range(2)]
                rows = [pl.ds(pl.multiple_of(j * CHUNK, CHUNK), CHUNK) for j in js]
                kb2 = [_per_head(k_ref[rk, :], hms) for rk in rows]
                logits = [_sb_logits(qb, kb, m) for kb, m in zip(kb2, masks)]
                da = [_dot(dob, _per_head(v_ref[rk, :], hms), NT) for rk in rows]
                sums = [[_dot(_pair(lsn[:, hc]), u_gt, NN) for hc in heads] for _, lsn in logits]
                a_b, w_all = [], []
                for (lsp, _), r, d, m in zip(logits, sums, da, masks):
                    a_h, w_h = [], []
                    for h, hc in enumerate(heads):
                        a = jnp.exp(lsp[:, hc] + r[h][:, :128] + c_l[h])
                        if m is not None:
                            a = jnp.where(m[:, hc], a, 0.0)
                        c_l[h] = c_l[h] + r[h][:, 128:]
                        a = a.astype(BF16)
                        a_h.append(a)
                        w_h.append(a.astype(F32) * d[:, hc])
                    a_b.append(jnp.concatenate(a_h, axis=1))
                    w_all.append(w_h)
                sums_w = [[_dot(_pair(w), u_ge, NN) for w in w_h] for w_h in w_all]
                dz_b = []
                for (lsp, _), w_h, r, m in zip(logits, w_all, sums_w, masks):
                    sp = jnp.exp(lsp)
                    dz_h = []
                    for h, hc in enumerate(heads):
                        later_w = r[h][:, :128] + c_w[h]
                        c_w[h] = c_w[h] + r[h][:, 128:]
                        dz = w_h[h] * (1.0 - sp[:, hc]) - sp[:, hc] * (total[h] - later_w)
                        if m is not None:
                            dz = jnp.where(m[:, hc], dz, 0.0)
                        dz_h.append(dz.astype(BF16))
                    dz_b.append(jnp.concatenate(dz_h, axis=1))
                for j, kb, a, dz in zip(js, kb2, a_b, dz_b):
                    dkt = _dot(q_t, dz, NN)
                    dvt = _dot(do_t, a, NN)
                    dkt_ref[j] += jnp.where(top, dkt[:, :128], dkt[:, 128:])
                    dvt_ref[j] += jnp.where(top, dvt[:, :128], dvt[:, 128:])
                    dq = dq + _dot(dz, kb, NN)
                return c_l[0], c_l[1], c_w[0], c_w[1], dq

            zero = jnp.zeros((qt, 128), F32)
            diag = list(reversed(range(nsub)))
            state = group([i * nsub + m for m in diag], [tile_mask(m) for m in diag], (zero,) * 5)

            def live(c):
                return jnp.logical_and(c[0] < i, jnp.maximum(jnp.max(c[1][0]), jnp.max(c[1][1])) > SB_DEAD)

            def blocks(c):
                jj, st = c
                return jj + 1, group([(i - jj) * nsub - 1 - u for u in range(nsub)], [None] * nsub, st)

            _, state = lax.while_loop(live, blocks, (jnp.int32(0), state))
            dq_ref[rq, :] = (state[4] * SB_SCALE).astype(BF16)
            return 0

        lax.fori_loop(0, T // qt, qtile, 0)

        def untranspose(jb, _):
            rk = pl.ds(pl.multiple_of(jb * CHUNK, CHUNK), CHUNK)
            dk_ref[rk, :] = dkt_ref[jb].T.astype(BF16)
            dv_ref[rk, :] = dvt_ref[jb].T.astype(BF16)
            return 0

        lax.fori_loop(0, T // CHUNK, untranspose, 0)

    def col(off):
        return pl.BlockSpec((T, 128), lambda hp: (0, off + hp))

    o16 = jax.ShapeDtypeStruct((T, BRANCH_W), BF16)
    steps = BRANCH_W // 128
    j = _job_args(job, 5, 3)
    acc = pltpu.VMEM((T // CHUNK, CHUNK, CHUNK), F32)
    res = pl.pallas_call(
        _hosting(body, job, 5, 3, 2, steps), name=name, grid=(steps,),
        in_specs=[col(cb), col(cb + 4), col(cb + 8), col(0), col(0)] + j["in_specs"],
        out_specs=[col(0), col(0), col(0)] + j["out_specs"], out_shape=[o16, o16, o16] + j["out_shape"],
        scratch_shapes=[acc, acc] + j["scratch"], input_output_aliases=j["aliases"],
        compiler_params=_params(("parallel",) if job is None else ("arbitrary",)),
    )(proj, proj, proj, out, dout, *j["ins"])
    return res[0], res[1], res[2], list(res[3:])


_G0 = math.sqrt(2.0 / math.pi)
_G1 = 0.044715


def _gelu(x):
    return 0.5 * x * (1.0 + jnp.tanh(_G0 * (x + _G1 * x * x * x)))


def _gelu_grad(x):
    t = jnp.tanh(_G0 * (x + _G1 * x * x * x))
    return 0.5 * (1.0 + t) + 0.5 * x * (1.0 - t * t) * (_G0 * (1.0 + 3.0 * _G1 * x * x))


def _tril():
    row, lane = _sb_masks()
    return row >= lane


def sgu_fwd(proj, ln_g, ln_b, w, bias, *, name):
    T = proj.shape[0]
    tb = min(512, T)
    G = BRANCH_W // 128

    def body(u_ref, v_ref, g_ref, b_ref, w_ref, bias_ref, o_ref):
        vv = _gelu(v_ref[...])
        xh, _ = _group_norm(vv)
        vn = (xh * g_ref[...] + b_ref[...]).astype(BF16)
        tril = _tril()
        for g in range(G):
            wg = jnp.where(tril, w_ref[g], 0.0).astype(BF16)
            gc = slice(g * 128, (g + 1) * 128)
            for c in range(tb // CHUNK):
                r = slice(c * CHUNK, (c + 1) * CHUNK)
                sv = _dot(wg, vn[r, gc], NN) + bias_ref[g]
                o_ref[r, gc] = _gelu(u_ref[r, gc]) * sv

    cu, cv = C_SGU // BRANCH_W, C_SGU // BRANCH_W + 1
    vec = pl.BlockSpec((1, BRANCH_W), lambda i: (0, 0))
    mat = pl.BlockSpec((G, CHUNK, CHUNK), lambda i: (0, 0, 0))
    return pl.pallas_call(
        body, name=name, grid=(T // tb,),
        in_specs=[pl.BlockSpec((tb, BRANCH_W), lambda i: (i, cu)), pl.BlockSpec((tb, BRANCH_W), lambda i: (i, cv)),
                  vec, vec, mat, mat],
        out_specs=pl.BlockSpec((tb, BRANCH_W), lambda i: (i, 0)),
        out_shape=jax.ShapeDtypeStruct((T, BRANCH_W), F32),
        compiler_params=_params(("parallel",)),
    )(proj, proj, ln_g, ln_b, w, bias)


def sgu_bwd(proj, ln_g, ln_b, w, bias, dout, *, name):
    T = proj.shape[0]
    tb = min(512, T)
    G = BRANCH_W // 128

    def body(u_ref, v_ref, g_ref, b_ref, w_ref, bias_ref, do_ref, dp_ref, dw_ref, dbias_ref, dg_ref, db_ref, dvn_ref):
        @pl.when(pl.program_id(0) == 0)
        def _():
            dw_ref[...] = jnp.zeros_like(dw_ref)
            dbias_ref[...] = jnp.zeros_like(dbias_ref)
            dg_ref[...] = jnp.zeros_like(dg_ref)
            db_ref[...] = jnp.zeros_like(db_ref)

        gv = v_ref[...]
        vv = _gelu(gv)
        xh, rstd = _group_norm(vv)
        vn = (xh * g_ref[...] + b_ref[...]).astype(BF16)
        tril = _tril()
        for g in range(G):
            wg = jnp.where(tril, w_ref[g], 0.0).astype(BF16)
            gc = slice(g * 128, (g + 1) * 128)
            for c in range(tb // CHUNK):
                r = slice(c * CHUNK, (c + 1) * CHUNK)
                vn_c = vn[r, gc]
                sv = _dot(wg, vn_c, NN) + bias_ref[g]
                gu = u_ref[r, gc]
                d_o = do_ref[r, gc]
                dp_ref[r, gc] = (d_o * sv * _gelu_grad(gu)).astype(BF16)
                dsv = d_o * _gelu(gu)
                dsv_b = dsv.astype(BF16)
                dvn_ref[r, gc] = _dot(wg, dsv_b, TN)
                dw_ref[g] += jnp.where(tril, _dot(dsv_b, vn_c, NT), 0.0)
                dbias_ref[g] += jnp.broadcast_to(jnp.sum(dsv, axis=1, keepdims=True), (CHUNK, CHUNK))
        dvn = dvn_ref[...]
        dg_ref[...] += jnp.sum(dvn * xh, axis=0, keepdims=True)
        db_ref[...] += jnp.sum(dvn, axis=0, keepdims=True)
        dxh = dvn * g_ref[...]
        m1 = jnp.mean(dxh, axis=-1, keepdims=True)
        m2 = jnp.mean(dxh * xh, axis=-1, keepdims=True)
        dp_ref[:, BRANCH_W:2 * BRANCH_W] = (rstd * (dxh - m1 - xh * m2) * _gelu_grad(gv)).astype(BF16)

    cu, cv = C_SGU // BRANCH_W, C_SGU // BRANCH_W + 1
    vec = pl.BlockSpec((1, BRANCH_W), lambda i: (0, 0))
    mat = pl.BlockSpec((G, CHUNK, CHUNK), lambda i: (0, 0, 0))
    blk = pl.BlockSpec((tb, BRANCH_W), lambda i: (i, 0))
    msh = jax.ShapeDtypeStruct((G, CHUNK, CHUNK), F32)
    vsh = jax.ShapeDtypeStruct((1, BRANCH_W), F32)
    return pl.pallas_call(
        body, name=name, grid=(T // tb,),
        in_specs=[pl.BlockSpec((tb, BRANCH_W), lambda i: (i, cu)), pl.BlockSpec((tb, BRANCH_W), lambda i: (i, cv)),
                  vec, vec, mat, mat, blk],
        out_specs=[pl.BlockSpec((tb, 2 * BRANCH_W), lambda i: (i, 0)), mat, mat, vec, vec],
        out_shape=[jax.ShapeDtypeStruct((T, 2 * BRANCH_W), BF16), msh, msh, vsh, vsh],
        scratch_shapes=[pltpu.VMEM((tb, BRANCH_W), F32)],
        compiler_params=_params(("arbitrary",)),
    )(proj, proj, ln_g, ln_b, w, bias, dout)


def merge_fwd(a1, a2, a3, p1, p2, p3, proj, *, name):
    T = a1.shape[0]
    tm, tn = min(1024, T), 512
    gb = C_GATE // tn

    def body(a1_ref, a2_ref, a3_ref, p1_ref, p2_ref, p3_ref, g1_ref, g2_ref, g3_ref, m_ref, r1_ref, r2_ref, r3_ref):
        m = None
        for a_ref, p_ref, g_ref, r_ref in ((a1_ref, p1_ref, g1_ref, r1_ref), (a2_ref, p2_ref, g2_ref, r2_ref),
                                           (a3_ref, p3_ref, g3_ref, r3_ref)):
            r = _dot(a_ref[...].astype(BF16), p_ref[...], NN)
            r_ref[...] = r
            t = jax.nn.sigmoid(g_ref[...]) * r
            m = t if m is None else m + t
        m_ref[...] = m.astype(m_ref.dtype)

    a_spec = pl.BlockSpec((tm, BRANCH_W), lambda i, j: (i, 0))
    p_spec = pl.BlockSpec((BRANCH_W, tn), lambda i, j: (0, j))
    o_spec = pl.BlockSpec((tm, tn), lambda i, j: (i, j))
    osh = jax.ShapeDtypeStruct((T, D_MODEL), F32)
    gates = [pl.BlockSpec((tm, tn), functools.partial(lambda i, j, o: (i, o + j), o=gb + 2 * n)) for n in range(3)]
    return pl.pallas_call(
        body, name=name, grid=(T // tm, D_MODEL // tn),
        in_specs=[a_spec, a_spec, a_spec, p_spec, p_spec, p_spec, *gates],
        out_specs=[o_spec] * 4, out_shape=[jax.ShapeDtypeStruct((T, D_MODEL), BF16)] + [osh] * 3,
        compiler_params=_params(("parallel", "parallel")),
    )(a1, a2, a3, p1, p2, p3, proj, proj, proj)


def merge_bwd(dm, r1, r2, r3, proj, *, name):
    T = dm.shape[0]
    tm, tn = min(512, T), 512
    gb = C_GATE // tn

    def body(dm_ref, r1_ref, r2_ref, r3_ref, g1_ref, g2_ref, g3_ref, dr1_ref, dr2_ref, dr3_ref, dg1_ref, dg2_ref, dg3_ref):
        d = dm_ref[...]
        for r_ref, g_ref, dr_ref, dg_ref in ((r1_ref, g1_ref, dr1_ref, dg1_ref), (r2_ref, g2_ref, dr2_ref, dg2_ref),
                                             (r3_ref, g3_ref, dr3_ref, dg3_ref)):
            s = jax.nn.sigmoid(g_ref[...])
            dr_ref[...] = (d * s).astype(BF16)
            dg_ref[...] = (d * r_ref[...] * (s * (1.0 - s))).astype(BF16)

    o_spec = pl.BlockSpec((tm, tn), lambda i, j: (i, j))
    osh = jax.ShapeDtypeStruct((T, D_MODEL), BF16)
    gates = [pl.BlockSpec((tm, tn), functools.partial(lambda i, j, o: (i, o + j), o=gb + 2 * n)) for n in range(3)]
    return pl.pallas_call(
        body, name=name, grid=(T // tm, D_MODEL // tn),
        in_specs=[o_spec] * 4 + gates, out_specs=[o_spec] * 6, out_shape=[osh] * 6,
        compiler_params=_params(("parallel", "parallel")),
    )(dm, r1, r2, r3, proj, proj, proj)


def _rows_call(fn, ins, out_dtypes, *, name, tr=256):
    first = ins[0][0] if isinstance(ins[0], tuple) else ins[0]
    R, C = first.shape[-2:]
    tr = min(tr, R)
    assert R % tr == 0, (name, R, tr)
    arrs, specs = [], []
    for x in ins:
        if isinstance(x, tuple):
            arrs.append(x[0])
            specs.append(pl.BlockSpec((None, tr, C), functools.partial(lambda i, n: (n, i, 0), n=x[1])))
        else:
            arrs.append(x)
            specs.append(pl.BlockSpec((tr, C), lambda i: (i, 0)))
    ni = len(arrs)

    def body(*refs):
        vals = fn(*[r[...] for r in refs[:ni]])
        for o_ref, v in zip(refs[ni:], vals):
            o_ref[...] = v.astype(o_ref.dtype)

    res = pl.pallas_call(
        body, name=name, grid=(R // tr,), in_specs=specs,
        out_specs=[pl.BlockSpec((tr, C), lambda i: (i, 0)) for _ in out_dtypes],
        out_shape=[jax.ShapeDtypeStruct((R, C), dt) for dt in out_dtypes],
        compiler_params=_params(("parallel",)),
    )(*arrs)
    return res


def _tile_rows(rows, cols):
    t = 256
    while t > 8 and (t * cols > 512 * 1024 or rows % t):
        t //= 2
    return t


def _rows_at(fn, pos, ins, outs, steps, *, name, aliases=None):
    read = [n for n, (_, s) in enumerate(ins) if s is not ANY]
    ni = len(ins)

    def body(pos_ref, *refs):
        vals = fn(*[refs[n][...] for n in read])
        for o_ref, v in zip(refs[ni:], vals):
            o_ref[...] = v.astype(o_ref.dtype)

    return pl.pallas_call(
        body, name=name,
        grid_spec=pltpu.PrefetchScalarGridSpec(num_scalar_prefetch=1, grid=(steps,), in_specs=[s for _, s in ins],
                                               out_specs=[s for _, s in outs]),
        out_shape=[sh for sh, _ in outs],
        input_output_aliases={1 + i: o for i, o in (aliases or {}).items()},
        compiler_params=_params(("parallel",)),
    )(pos, *[a for a, _ in ins])


def cast_into_whole(pos, w, l, axis, *, name):
    _, r, n = w.shape
    tr = _tile_rows(r, n)
    if axis == 1:
        shape, spec = (r, n * N_CHIPS), pl.BlockSpec((tr, n), lambda i, p: (i, p[3]))
    else:
        shape, spec = (r * N_CHIPS, n), pl.BlockSpec((tr, n), lambda i, p: (p[3] * (r // tr) + i, 0))
    return _rows_at(lambda a: (a,), pos, [(w, pl.BlockSpec((None, tr, n), lambda i, p: (l, i, 0)))],
                    [(jax.ShapeDtypeStruct(shape, BF16), spec)], r // tr, name=name)[0]


def pair_sum(pos, theirs, g32, axis, *, name):
    rows2, cols = theirs.shape
    h = rows2 // (N_CHIPS if axis == 0 else 1)
    tr = _tile_rows(h, cols)
    hb = h // tr
    if axis == 1:
        own = pl.BlockSpec((tr, cols), lambda i, p: (p[2] * hb + i, 0))
    else:
        own = pl.BlockSpec((tr, cols), lambda i, p: ((2 * (i // hb) + p[2]) * hb + i % hb, 0))
    row = pl.BlockSpec((tr, cols), lambda i, p: (i, 0))
    return _rows_at(lambda t, m: (m + t.astype(F32),) * 2, pos, [(theirs, row), (g32, own)],
                    [(jax.ShapeDtypeStruct((rows2, cols), F32), row), (jax.ShapeDtypeStruct((rows2, cols), BF16), row)],
                    rows2 // tr, name=name)


def chip_sum(pos, h32, recv, l, axis, whole, *, name):
    _, depth, h, n = recv.shape
    tr = _tile_rows(h, n)
    hb = h // tr
    if axis == 1:
        mine = pl.BlockSpec((tr, n), lambda i, p: (i, p[3]))
    else:
        mine = pl.BlockSpec((tr, n), lambda i, p: (p[3] * hb + i, 0))
    ins = [(h32, mine)] + [(recv, pl.BlockSpec((None, None, tr, n), functools.partial(lambda i, p, j: (j, l, i, 0), j=j)))
                           for j in range(3)]
    if whole is not None:
        ins.append((whole, ANY))
    return _rows_at(lambda o, a, b, c: (((o + a.astype(F32)) + b.astype(F32)) + c.astype(F32),), pos, ins,
                    [(jax.ShapeDtypeStruct((depth, 2, h, n), F32), pl.BlockSpec((None, None, tr, n), lambda i, p: (l, p[2], i, 0)))],
                    hb, name=name, aliases=None if whole is None else {4: 0})[0]


def _adamw(w, g, m, v):
    m2 = ADAM_B1 * m + (1.0 - ADAM_B1) * g
    v2 = ADAM_B2 * v + (1.0 - ADAM_B2) * (g * g)
    m_hat = m2 / (1.0 - ADAM_B1 ** ADAM_STEP)
    v_hat = v2 / (1.0 - ADAM_B2 ** ADAM_STEP)
    delta = -ADAM_LR * (m_hat / (jnp.sqrt(v_hat) + ADAM_EPS) + ADAM_WD * w)
    return delta, m2, v2


def _place():
    return lax.axis_index("x"), lax.axis_index("y"), lax.axis_index("c")


def _chip_peers(x, y, c):
    return [((1 - x, y, c), 2 * (1 - x) + y), ((x, 1 - y, c), 2 * x + 1 - y), ((1 - x, 1 - y, c), 2 * (1 - x) + 1 - y)]


def _shard_of(ref, axis, k, n):
    start = pl.multiple_of(k * n, 128)
    return ref.at[pl.ds(start, n), :] if axis == 0 else ref.at[:, pl.ds(start, n)]


ANY = pl.BlockSpec(memory_space=pl.ANY)


class CopyJob:
    def __init__(self, ins, out_shape, scratch, copies, aliases=None):
        self.ins, self.out_shape, self.scratch, self.copies = list(ins), list(out_shape), list(scratch), copies
        self.aliases = dict(aliases or {})

    def start(self, ins, outs, sems):
        local, remote, _, _ = self.copies(ins, outs, sems)
        for d in local + remote:
            d.start()

    def finish(self, ins, outs, sems):
        local, remote, arrivals, relays = self.copies(ins, outs, sems)
        for needs, sends, _ in relays:
            for d in needs:
                d.wait_recv()
            for d in sends:
                d.start()
        for d in arrivals + [d for _, _, arrives in relays for d in arrives]:
            d.wait_recv()
        for d in remote + [d for _, sends, _ in relays for d in sends]:
            d.wait_send()
        for d in local:
            d.wait()


def run_job(job, *, name):
    ni, no = len(job.ins), len(job.out_shape)

    def body(*refs):
        parts = refs[:ni], refs[ni:ni + no], refs[ni + no:]
        job.start(*parts)
        job.finish(*parts)

    return pl.pallas_call(
        body, name=name, in_specs=[ANY] * ni, out_specs=[ANY] * no, out_shape=job.out_shape,
        scratch_shapes=job.scratch, input_output_aliases=job.aliases,
    )(*job.ins)


def _job_args(job, n_in, n_out):
    if job is None:
        return dict(ins=[], in_specs=[], out_specs=[], out_shape=[], scratch=[], aliases={})
    return dict(ins=job.ins, in_specs=[ANY] * len(job.ins), out_specs=[ANY] * len(job.out_shape),
                out_shape=job.out_shape, scratch=job.scratch,
                aliases={n_in + i: n_out + o for i, o in job.aliases.items()})


def _hosting(body, job, n_in, n_out, n_scratch, grid):
    if job is None:
        return body
    ji, jo = len(job.ins), len(job.out_shape)
    grid = (grid,) if isinstance(grid, int) else tuple(grid)

    def at(ends):
        hit = None
        for ax, e in enumerate(ends):
            here = pl.program_id(ax) == e
            hit = here if hit is None else jnp.logical_and(hit, here)
        return hit

    def hosted(*refs):
        o = n_in + ji
        s = o + n_out + jo
        parts = refs[n_in:o], refs[o + n_out:s], refs[s + n_scratch:]

        @pl.when(at([0] * len(grid)))
        def _():
            job.start(*parts)

        body(*refs[:n_in], *refs[o:o + n_out], *refs[s:s + n_scratch])

        @pl.when(at([g - 1 for g in grid]))
        def _():
            job.finish(*parts)

    return hosted


def _job_sems(n_remote, n_local):
    return [pltpu.SemaphoreType.DMA((n_remote,)), pltpu.SemaphoreType.DMA((n_remote,)), pltpu.SemaphoreType.DMA((n_local,))]


def gather_job(shards, axes, chips=(0, 1, 2)):
    na = len(shards)

    def copies(ins, outs, sems):
        send, recv, _ = sems
        x, y, c = _place()
        k = 2 * x + y
        remote, relays = [], []
        for a in range(na):
            r = outs[a].shape[0] // (N_CHIPS if axes[a] == 0 else 1)
            n = outs[a].shape[axes[a]] // N_CHIPS
            half = r // 2

            def part(kk, cc, a=a, n=n, half=half):
                rows = pl.ds(pl.multiple_of(cc * half + (kk * n if axes[a] == 0 else 0), 8), half)
                return outs[a].at[rows, :] if axes[a] == 0 else outs[a].at[rows, pl.ds(pl.multiple_of(kk * n, 128), n)]

            needs, passes, lands = [], [], []
            for j, (peer, kp) in enumerate(_chip_peers(x, y, c)):
                if j not in chips:
                    continue
                s = 6 * a + j
                remote.append(pltpu.make_async_remote_copy(part(k, c), part(k, c), send.at[s], recv.at[s],
                                                           device_id=peer, device_id_type=MESH))
                needs.append(pltpu.make_async_remote_copy(part(kp, c), part(kp, c), send.at[s], recv.at[s],
                                                          device_id=peer, device_id_type=MESH))
                passes.append(pltpu.make_async_remote_copy(part(kp, c), part(kp, c), send.at[s + 3], recv.at[s + 3],
                                                           device_id=(x, y, 1 - c), device_id_type=MESH))
                lands.append(pltpu.make_async_remote_copy(part(kp, 1 - c), part(kp, 1 - c), send.at[s + 3], recv.at[s + 3],
                                                          device_id=(x, y, 1 - c), device_id_type=MESH))
            relays.append((needs, passes, lands))
        return [], remote, [], relays

    out_shape = [jax.ShapeDtypeStruct(w.shape, BF16) for w in shards]
    return CopyJob(shards, out_shape, _job_sems(6 * na, 1), copies, {a: a for a in range(na)})


def scatter_job(layers, g16, axes, filled, chips=(0, 1, 2)):
    na = len(axes)

    def shard_shape(a):
        r, c = g16[a].shape
        return (r // N_CHIPS, c) if axes[a] == 0 else (r, c // N_CHIPS)

    def copies(ins, outs, sems):
        send, recv_sems, _ = sems
        x, y, c = _place()
        remote = []
        for a in range(na):
            n = shard_shape(a)[axes[a]]
            for r, (peer, kp) in enumerate(_chip_peers(x, y, c)):
                if r not in chips:
                    continue
                remote.append(pltpu.make_async_remote_copy(_shard_of(ins[a], axes[a], kp, n), outs[a].at[r, layers[a]],
                                                           send.at[3 * a + r], recv_sems.at[3 * a + r],
                                                           device_id=peer, device_id_type=MESH))
        return [], remote, remote, []

    out_shape = [jax.ShapeDtypeStruct((3, DEPTH) + shard_shape(a), BF16) for a in range(na)]
    ins = list(g16)
    aliases = {}
    for a in range(na):
        if filled[a] is not None:
            aliases[len(ins)] = a
            ins.append(filled[a])
    return CopyJob(ins, out_shape, _job_sems(3 * na, 1), copies, aliases)


def pair_job(g16, axes):
    na = len(axes)
    pieces = [1 if ax == 1 else N_CHIPS for ax in axes]

    def copies(ins, outs, sems):
        send, recv, _ = sems
        x, y, c = _place()
        remote = []
        s = 0
        for a in range(na):
            rows = g16[a].shape[0] // (2 * pieces[a])
            for kk in range(pieces[a]):
                src = ins[a].at[pl.ds(pl.multiple_of((2 * kk + 1 - c) * rows, 8), rows), :]
                remote.append(pltpu.make_async_remote_copy(src, outs[a].at[pl.ds(kk * rows, rows), :], send.at[s], recv.at[s],
                                                           device_id=(x, y, 1 - c), device_id_type=MESH))
                s += 1
        return [], remote, remote, []

    out_shape = [jax.ShapeDtypeStruct((g.shape[0] // 2, g.shape[1]), BF16) for g in g16]
    return CopyJob(g16, out_shape, _job_sems(sum(pieces), 1), copies)


def join_job(shards):
    na = len(shards)

    def copies(ins, outs, sems):
        send, recv, _ = sems
        x, y, c = _place()
        remote = [pltpu.make_async_remote_copy(outs[a].at[:, c], outs[a].at[:, c], send.at[a], recv.at[a],
                                               device_id=(x, y, 1 - c), device_id_type=MESH) for a in range(na)]
        lands = [pltpu.make_async_remote_copy(outs[a].at[:, 1 - c], outs[a].at[:, 1 - c], send.at[a], recv.at[a],
                                              device_id=(x, y, 1 - c), device_id_type=MESH) for a in range(na)]
        return [], remote, lands, []

    out_shape = [jax.ShapeDtypeStruct(s.shape, F32) for s in shards]
    return CopyJob(shards, out_shape, _job_sems(na, 1), copies, {a: a for a in range(na)})


def small_job(p):
    def copies(ins, outs, sems):
        send, recv, loc = sems
        x, y, c = _place()
        me = 4 * x + 2 * y + c
        remote, lands = [], []
        for rel in range(1, 8):
            dx, dy, dc = rel >> 2, (rel >> 1) & 1, rel & 1
            peer = (1 - x if dx else x, 1 - y if dy else y, 1 - c if dc else c)
            who = 4 * peer[0] + 2 * peer[1] + peer[2]
            remote.append(pltpu.make_async_remote_copy(ins[0], outs[0].at[me], send.at[rel - 1], recv.at[rel - 1],
                                                       device_id=peer, device_id_type=MESH))
            lands.append(pltpu.make_async_remote_copy(ins[0], outs[0].at[who], send.at[rel - 1], recv.at[rel - 1],
                                                      device_id=peer, device_id_type=MESH))
        return [pltpu.make_async_copy(ins[0], outs[0].at[me], loc.at[0])], remote, lands, []

    return CopyJob([p], [jax.ShapeDtypeStruct((8,) + p.shape, F32)], _job_sems(7, 1), copies)


def small_sum(slots):
    def add(*terms):
        acc = terms[0]
        for t in terms[1:]:
            acc = acc + t
        return (acc,)

    return _rows_call(add, [(slots, d) for d in range(8)], [F32], name="small_sum", tr=8 * 47)[0]


BIG = ("w_in", "p_ret", "p_sb", "p_sgu", "w_out", "w_up", "w_down")
BIG_AXIS = {"w_in": 1, "p_ret": 1, "p_sb": 1, "p_sgu": 1, "w_out": 0, "w_up": 1, "w_down": 0}
SMALL = ("ret_gn_g", "ret_gn_b", "sgu_ln_g", "sgu_ln_b", "sgu_w", "sgu_b", "ln1_g", "ln1_b", "ln2_g", "ln2_b")


def layer_forward(l, x0, W, sm, rope, rconsts, hooks):
    n = f"l{l}_"
    job = hooks.fwd_job(l, "proj")
    proj = matmul(x0, W["w_in"], mode="nn", tm=2048, tn=640, tk=1024, name=n + "proj", job=job)
    if job is not None:
        proj, job_out = proj
        hooks.done(job, job_out)
    retg, raw, states = ret_fwd(proj, *rope, rconsts, sm["ret_gn_g"], sm["ret_gn_b"], name=n + "ret_fwd")
    job = hooks.fwd_job(l, "sb")
    sb, job_out = sb_fwd(proj, name=n + "sb_fwd", job=job)
    if job is not None:
        hooks.done(job, job_out)
    sg = sgu_fwd(proj, sm["sgu_ln_g"], sm["sgu_ln_b"], sm["sgu_w"], sm["sgu_bias"], name=n + "sgu_fwd")
    merged, r1, r2, r3 = merge_fwd(retg, sb, sg, W["p_ret"], W["p_sb"], W["p_sgu"], proj, name=n + "merge_fwd")
    x1, xh1, rs1 = matmul_ln(merged, W["w_out"], x0, sm["ln1_g"], sm["ln1_b"], tk=1024, name=n + "out_ln1")
    job = hooks.fwd_job(l, "up")
    h1 = matmul(x1, W["w_up"], mode="nn", tm=1024, tn=1024, tk=1024, name=n + "up", job=job)
    if job is not None:
        h1, job_out = h1
        hooks.done(job, job_out)
    job = hooks.fwd_job(l, "down")
    res = matmul_ln(h1, W["w_down"], x1, sm["ln2_g"], sm["ln2_b"], pro=_relu2, tk=1024, name=n + "down_ln2", job=job)
    if job is not None:
        res, job_out = res
        hooks.done(job, job_out)
    x2, xh2, rs2 = res
    saved = dict(x0=x0, proj=proj, retg=retg, raw=raw, states=states, sb=sb, sg=sg, merged=merged, r=(r1, r2, r3),
                 x1=x1, xh1=xh1, rs1=rs1, h1=h1, xh2=xh2, rs2=rs2)
    return x2, saved


def layer_backward(l, dx2, s, W, sm, rope, rconsts, hooks):
    n = f"l{l}_"
    two = ((F32, None), (BF16, None))
    gw, gs = {}, {}
    job = hooks.bwd_job(l, "ln2")
    res = ln_bwd(dx2, s["xh2"], s["rs2"], sm["ln2_g"], name=n + "ln2_bwd", job=job)
    if job is not None:
        res, job_out = res
        hooks.done(job, job_out)
    du2, du2h, gs["ln2_g"], gs["ln2_b"] = res
    job = hooks.bwd_job(l, "g_down")
    gw["w_down"] = matmul(s["h1"], du2h, mode="tn", tm=1024, tn=1024, tk=2048, pro=_relu2, outs=two, name=n + "g_down", job=job)
    if job is not None:
        gw["w_down"], job_out = gw["w_down"]
        hooks.done(job, job_out)
    dh1 = matmul(du2h, W["w_down"], mode="nt", tm=1024, tn=1024, tk=1024, outs=((BF16, None),),
                 epi=lambda acc, h: (acc * (2.0 * jnp.maximum(h, 0.0)),), tiles=(s["h1"],), name=n + "d_h1")
    job = hooks.bwd_job(l, "g_up")
    gw["w_up"] = matmul(s["x1"], dh1, mode="tn", tm=1024, tn=1024, tk=2048, outs=two, name=n + "g_up", job=job)
    if job is not None:
        gw["w_up"], job_out = gw["w_up"]
        hooks.done(job, job_out)
    dx1 = matmul(dh1, W["w_up"], mode="nt", tm=1024, tn=1024, tk=2048,
                 epi=lambda acc, d: (acc + ALPHA * d,), tiles=(du2,), name=n + "d_x1")
    du1, du1h, gs["ln1_g"], gs["ln1_b"] = ln_bwd(dx1, s["xh1"], s["rs1"], sm["ln1_g"], name=n + "ln1_bwd")
    gw["w_out"] = matmul(s["merged"], du1h, mode="tn", tm=1024, tn=1024, tk=2048, outs=two, name=n + "g_out")
    dmerged = matmul(du1h, W["w_out"], mode="nt", tm=1024, tn=1024, tk=1024, name=n + "d_merged")
    dr1, dr2, dr3, dg1, dg2, dg3 = merge_bwd(dmerged, *s["r"], s["proj"], name=n + "merge_bwd")
    d_branch = {}
    for nm, a, dr in (("p_ret", s["retg"], dr1), ("p_sb", s["sb"], dr2), ("p_sgu", s["sg"], dr3)):
        gw[nm] = matmul(a, dr, mode="tn", tm=512, tn=1024, tk=2048, outs=two, name=n + "g_" + nm)
        d_branch[nm] = matmul(dr, W[nm], mode="nt", tm=1024, tn=512, tk=1024, name=n + "d_" + nm)
    job = hooks.pair(l, gw)
    dret, gs["ret_gn_g"], gs["ret_gn_b"], job_out = ret_bwd(s["proj"], *rope, rconsts, sm["ret_gn_g"], sm["ret_gn_b"],
                                                             s["raw"], s["states"], d_branch["p_ret"], name=n + "ret_bwd", job=job)
    if job is not None:
        hooks.done(job, job_out)
    job = hooks.scatter(l) if job is not None else None
    dsq, dsk, dsv, job_out = sb_bwd(s["proj"], s["sb"], d_branch["p_sb"], name=n + "sb_bwd", job=job)
    if job is not None:
        hooks.done(job, job_out)
    dsgu, gs["sgu_w"], dbias, gs["sgu_ln_g"], gs["sgu_ln_b"] = sgu_bwd(
        s["proj"], sm["sgu_ln_g"], sm["sgu_ln_b"], sm["sgu_w"], sm["sgu_bias"], d_branch["p_sgu"], name=n + "sgu_bwd")
    gs["sgu_b"] = dbias[:, :, 0]
    dproj = jnp.concatenate([dret, dsq, dsk, dsv, dsgu, dg1, dg2, dg3], axis=1)
    job = hooks.small(l, gs)
    gw["w_in"] = matmul(s["x0"], dproj, mode="tn", tm=1024, tn=1920, tk=1024, outs=two, name=n + "g_in", job=job)
    if job is not None:
        gw["w_in"], job_out = gw["w_in"]
        hooks.done(job, job_out)
    job = hooks.tail(l, gw["w_in"])
    dx0 = matmul(dproj, W["w_in"], mode="nt", tm=1024, tn=1024, tk=2560,
                 epi=lambda acc, d: (acc + ALPHA * d,), tiles=(du1,), name=n + "d_x0", job=job)
    if job is not None:
        dx0, job_out = dx0
        hooks.done(job, job_out)
    return dx0, gw, gs


def local_step(x, target, small, plan):
    T = x.shape[0]
    rope = _rope_tables(T)
    rconsts = _ret_consts()
    sms = []
    for l in range(DEPTH):
        sm = {k: small[k][l][None, :] for k in SMALL if k not in ("sgu_w", "sgu_b")}
        sm["sgu_w"] = small["sgu_w"][l]
        sm["sgu_bias"] = jnp.broadcast_to(small["sgu_b"][l][:, :, None], (4, CHUNK, CHUNK))
        sms.append(sm)
    h, saved = x, []
    for l in range(DEPTH):
        h, s = layer_forward(l, h, plan.weights(l), sms[l], rope, rconsts, plan)
        saved.append(s)
    dy, sq = loss_head(h, target)
    gs = {k: [None] * DEPTH for k in SMALL}
    for l in reversed(range(DEPTH)):
        dy, gwl, gsl = layer_backward(l, dy, saved[l], plan.weights(l), sms[l], rope, rconsts, plan)
        plan.grads(l, gwl)
        for k in SMALL:
            gs[k][l] = gsl[k].reshape(small[k].shape[1:])
    return sq[0, 0], dy, {k: jnp.stack(v) for k, v in gs.items()}


EARLY_GRADS = ("p_ret", "p_sb", "p_sgu", "w_out", "w_up", "w_down")


class _StepPlan:
    def __init__(self, pos, shards16):
        self.pos = pos
        self.shards16 = shards16
        self.full = [dict() for _ in range(DEPTH)]
        self.gw = [None] * DEPTH
        self.bufs = {}
        self.sums = {}
        self.gs = [None] * DEPTH
        first = self._gather([(0, "w_in")])
        self.done(first, run_job(first, name="gather_first"))

    def weights(self, l):
        return self.full[l]

    def grads(self, l, gw):
        self.gw[l] = gw

    def _gather(self, items, chips=(0, 1, 2)):
        job = gather_job([self.shards16[l][k] for l, k in items], [BIG_AXIS[k] for _, k in items], chips)
        job.note = ("gather" if 2 in chips else "gather_part", items)
        return job

    def _pair(self, items):
        job = pair_job([g[1] for _, _, g in items], [BIG_AXIS[k] for _, k, _ in items])
        job.note = ("pair", items)
        return job

    def fwd_job(self, l, host):
        if host == "proj":
            return self._gather([(l, "w_down")])
        if host == "sb":
            return self._gather([(l, k) for k in ("p_ret", "p_sb", "p_sgu", "w_out", "w_up")])
        if l + 1 == DEPTH:
            return None
        return self._gather([(l + 1, "w_in")], (0, 1) if host == "up" else (2,))

    def bwd_job(self, l, host):
        if l + 1 == DEPTH:
            return None
        if host == "ln2":
            job = self._pair([(l + 1, "w_in", self.gw[l + 1]["w_in"])])
            job.note = ("pair_w_in", job.note[1])
            return job
        items, sums16 = self.summed_w_in
        job = scatter_job([l_ for l_, _, _ in items], sums16, [BIG_AXIS[k] for _, k, _ in items],
                          [self.bufs.get(k) for _, k, _ in items], (0, 1) if host == "g_down" else (2,))
        job.note = ("scatter", items)
        return job

    def pair(self, l, ready):
        return self._pair([(l, k, ready[k]) for k in EARLY_GRADS])

    def scatter(self, l):
        items, sums16 = self.summed
        job = scatter_job([l_ for l_, _, _ in items], sums16, [BIG_AXIS[k] for _, k, _ in items],
                          [self.bufs.get(k) for _, k, _ in items])
        job.note = ("scatter", items)
        return job

    def small(self, l, gs):
        self.gs[l] = {k: gs[k].reshape(-1) for k in SMALL}
        if l != 0:
            return None
        job = small_job(_pack_small({k: jnp.stack([self.gs[l_][k] for l_ in range(DEPTH)]) for k in SMALL}))
        job.note = ("small", [])
        return job

    def tail(self, l, g):
        if l != 0:
            return None
        last = self._pair([(0, "w_in", g)])
        self.done(last, run_job(last, name="pair_last"))
        return self.scatter(0)

    def done(self, job, outs):
        kind, items = job.note
        if kind == "small":
            self.small_slots = outs[0]
        if kind in ("pair", "pair_w_in"):
            sums16 = []
            for a, (l, k, g) in enumerate(items):
                self.sums[(l, k)], s16 = pair_sum(self.pos, outs[a], g[0], BIG_AXIS[k], name=f"pair_sum_{k}_{l}")
                sums16.append(s16)
            if kind == "pair":
                self.summed = (items, sums16)
            else:
                self.summed_w_in = (items, sums16)
        for a, item in enumerate(items):
            if kind == "gather_part":
                self.shards16[item[0]][item[1]] = outs[a]
            elif kind == "gather":
                self.full[item[0]][item[1]] = outs[a]
            elif kind == "scatter":
                self.bufs[item[1]] = outs[a]

    def finish(self):
        return self.bufs, self.sums


def _flat2(a):
    return a.reshape(-1, a.shape[-1])


def _pack_small(d, pre=""):
    return jnp.concatenate([d[pre + k].reshape(-1) for k in SMALL]).reshape(-1, 128)


def kernel(x, w_in, ret_gn_g, ret_gn_b, sgu_ln_g, sgu_ln_b, sgu_w, sgu_b, p_ret, p_sb, p_sgu, w_out, ln1_g, ln1_b, w_up, w_down, ln2_g, ln2_b, loss_target, m_w_in, m_ret_gn_g, m_ret_gn_b, m_sgu_ln_g, m_sgu_ln_b, m_sgu_w, m_sgu_b, m_p_ret, m_p_sb, m_p_sgu, m_w_out, m_ln1_g, m_ln1_b, m_w_up, m_w_down, m_ln2_g, m_ln2_b, v_w_in, v_ret_gn_g, v_ret_gn_b, v_sgu_ln_g, v_sgu_ln_b, v_sgu_w, v_sgu_b, v_p_ret, v_p_sb, v_p_sgu, v_w_out, v_ln1_g, v_ln1_b, v_w_up, v_w_down, v_ln2_g, v_ln2_b):
    given = dict(locals())
    order = BIG[:1] + SMALL[:6] + BIG[1:5] + SMALL[6:8] + BIG[5:7] + SMALL[8:10]
    L = DEPTH

    px, py, pc = _place()
    pos = jnp.stack([px, py, pc, 2 * px + py]).astype(jnp.int32)

    shards16 = [{k: cast_into_whole(pos, given[k], l, BIG_AXIS[k], name=f"cast_{k}_{l}") for k in BIG} for l in range(L)]
    plan = _StepPlan(pos, shards16)
    sq, dx, gs = local_step(x[0], loss_target[0], {k: given[k] for k in SMALL}, plan)
    loss = 0.5 * lax.psum(sq, ("x", "y", "c"))

    bufs, sums = plan.finish()
    shards = []
    for k in BIG:
        whole = None
        for l in range(L):
            whole = chip_sum(pos, sums[(l, k)], bufs[k], l, BIG_AXIS[k], whole, name=f"chip_sum_{k}_{l}")
        shards.append(whole)
    joined = run_job(join_job(shards), name="join_halves")
    out = {}
    for a, k in enumerate(BIG):
        shp = given[k].shape
        res = _rows_call(lambda g_, w_, m_, v_: (g_,) + _adamw(w_, g_, m_, v_),
                         [joined[a].reshape(-1, shp[-1]), _flat2(given[k]), _flat2(given["m_" + k]), _flat2(given["v_" + k])],
                         [F32] * 4, name="adamw_" + k)
        out[k] = [r.reshape(shp) for r in res]

    pack = _pack_small
    res = _rows_call(lambda g_, w_, m_, v_: (g_,) + _adamw(w_, g_, m_, v_),
                     [small_sum(plan.small_slots), pack(given), pack(given, "m_"), pack(given, "v_")], [F32] * 4,
                     name="adamw_small", tr=8 * 47)
    off = 0
    for k in SMALL:
        sz = given[k].size
        out[k] = [r.reshape(-1)[off:off + sz].reshape(given[k].shape) for r in res]
        off += sz

    grads = [out[k][0] for k in order]
    deltas = [out[k][1] for k in order]
    new_m = [out[k][2] for k in order]
    new_v = [out[k][3] for k in order]
    return (loss, dx[None], *grads, *deltas, *new_m, *new_v)
```

```python
import functools
import math

import jax
import jax.numpy as jnp
from jax import lax
from jax.experimental import pallas as pl
from jax.experimental.pallas import tpu as pltpu

F32 = jnp.float32
BF16 = jnp.bfloat16

D_MODEL = 1024
SEQ = 4096
DEPTH = 2
CHUNK = 128
RET_HEADS = 4
BRANCH_W = 512
N_IN = 7680
D_FF = 4096
LN_EPS = 1e-5
ROPE_BASE = 10000.0
ALPHA = (2 * DEPTH) ** 0.25
RET_SCALE = 128 ** -0.5
SB_SCALE = 64 ** -0.5
C_RET, C_SB, C_SGU, C_GATE = 0, 2048, 3584, 4608

ADAM_LR, ADAM_B1, ADAM_B2, ADAM_EPS, ADAM_WD, ADAM_STEP = 0.001, 0.9, 0.999, 1e-08, 0.01, 10

N_CHIPS = 4
VMEM_LIMIT = 56 * 1024 * 1024
MESH = pl.DeviceIdType.MESH

NN = ((1,), (0,))
NT = ((1,), (1,))
TN = ((0,), (0,))


def _dot(a, b, dims):
    return lax.dot_general(a, b, (dims, ((), ())), preferred_element_type=F32)


def _params(sem):
    return pltpu.CompilerParams(dimension_semantics=sem, vmem_limit_bytes=VMEM_LIMIT)


def _relu2(h):
    r = jnp.maximum(h.astype(F32), 0.0)
    return r * r


def matmul(a, b, *, mode, tm, tn, tk, outs=((F32, None),), pro=None, epi=None, tiles=(), rows=(), name, job=None):
    if mode == "nn":
        (M, K), N = a.shape, b.shape[1]
    elif mode == "nt":
        (M, K), N = a.shape, b.shape[0]
    else:
        (K, M), N = a.shape, b.shape[1]
    tm, tn, tk = min(tm, M), min(tn, N), min(tk, K)
    assert M % tm == 0 and N % tn == 0 and K % tk == 0, (name, M, N, K, tm, tn, tk)
    if mode == "nn":
        a_spec = pl.BlockSpec((tm, tk), lambda i, j, k: (i, k))
        b_spec = pl.BlockSpec((tk, tn), lambda i, j, k: (k, j))
        dims = NN
    elif mode == "nt":
        a_spec = pl.BlockSpec((tm, tk), lambda i, j, k: (i, k))
        b_spec = pl.BlockSpec((tn, tk), lambda i, j, k: (j, k))
        dims = NT
    else:
        a_spec = pl.BlockSpec((tk, tm), lambda i, j, k: (k, i))
        b_spec = pl.BlockSpec((tk, tn), lambda i, j, k: (k, j))
        dims = TN
    nk = K // tk
    nt_, nr, no = len(tiles), len(rows), len(outs)

    def body(a_ref, b_ref, *rest):
        tile_refs = rest[:nt_]
        row_refs = rest[nt_:nt_ + nr]
        out_refs = rest[nt_ + nr:nt_ + nr + no]
        av = a_ref[...]
        if pro is not None:
            av = pro(av)
        p = _dot(av.astype(BF16), b_ref[...].astype(BF16), dims)

        def finish(acc):
            vals = (acc,) * no if epi is None else epi(acc, *[r[...] for r in tile_refs], *[r[...] for r in row_refs])
            for o_ref, v in zip(out_refs, vals):
                o_ref[...] = v.astype(o_ref.dtype)

        if nk == 1:
            finish(p)
        else:
            acc_ref = rest[-1]
            k = pl.program_id(2)

            @pl.when(k == 0)
            def _():
                acc_ref[...] = p

            @pl.when(k > 0)
            def _():
                acc_ref[...] += p

            @pl.when(k == nk - 1)
            def _():
                finish(acc_ref[...])

    out_shape, out_specs = [], []
    for dt, width in outs:
        if width is None:
            out_shape.append(jax.ShapeDtypeStruct((M, N), dt))
            out_specs.append(pl.BlockSpec((tm, tn), lambda i, j, k: (i, j)))
        else:
            assert N == tn
            out_shape.append(jax.ShapeDtypeStruct((M, width), dt))
            out_specs.append(pl.BlockSpec((tm, width), lambda i, j, k: (i, 0)))
    in_specs = [a_spec, b_spec]
    in_specs += [pl.BlockSpec((tm, tn), lambda i, j, k: (i, j)) for _ in tiles]
    in_specs += [pl.BlockSpec((1, tn), lambda i, j, k: (0, j)) for _ in rows]
    grid = (M // tm, N // tn, nk)
    scratch = [pltpu.VMEM((tm, tn), F32)] if nk > 1 else []
    j = _job_args(job, len(in_specs), no)
    res = pl.pallas_call(
        _hosting(body, job, len(in_specs), no, len(scratch), grid), name=name, grid=grid,
        in_specs=in_specs + j["in_specs"], out_specs=out_specs + j["out_specs"], out_shape=out_shape + j["out_shape"],
        scratch_shapes=scratch + j["scratch"], input_output_aliases=j["aliases"],
        compiler_params=_params(("parallel", "parallel", "arbitrary") if job is None else ("arbitrary",) * 3),
    )(a, b, *tiles, *rows, *j["ins"])
    mine = res[0] if no == 1 else list(res[:no])
    return mine if job is None else (mine, list(res[no:]))


def _ln_epi(acc, res, g, b):
    u = ALPHA * res + acc
    mu = jnp.mean(u, axis=-1, keepdims=True)
    xc = u - mu
    var = jnp.mean(xc * xc, axis=-1, keepdims=True)
    rstd = lax.rsqrt(var + LN_EPS)
    xhat = xc * rstd
    return xhat * g + b, xhat, jnp.broadcast_to(rstd, (u.shape[0], 128))


def matmul_ln(a, w, res, g, b, *, pro=None, tk, name, job=None):
    n = w.shape[1]
    return matmul(a, w, mode="nn", tm=1024, tn=n, tk=tk, pro=pro, epi=_ln_epi, tiles=(res,), rows=(g, b),
                  outs=((F32, None), (F32, None), (F32, 128)), name=name, job=job)


def ln_bwd(dy, xhat, rstd, g, *, name, job=None):
    T, D = dy.shape
    tm = min(512, T)

    def body(dy_ref, xh_ref, rs_ref, g_ref, du_ref, du16_ref, dg_ref, db_ref):
        dyv, xh = dy_ref[...], xh_ref[...]
        r = rs_ref[:, 0:1]
        dxh = dyv * g_ref[...]
        m1 = jnp.mean(dxh, axis=-1, keepdims=True)
        m2 = jnp.mean(dxh * xh, axis=-1, keepdims=True)
        du = r * (dxh - m1 - xh * m2)
        du_ref[...] = du
        du16_ref[...] = du.astype(BF16)

        @pl.when(pl.program_id(0) == 0)
        def _():
            dg_ref[...] = jnp.zeros_like(dg_ref)
            db_ref[...] = jnp.zeros_like(db_ref)

        dg_ref[...] += jnp.sum(dyv * xh, axis=0, keepdims=True)
        db_ref[...] += jnp.sum(dyv, axis=0, keepdims=True)

    row = pl.BlockSpec((tm, D), lambda i: (i, 0))
    vec = pl.BlockSpec((1, D), lambda i: (0, 0))
    j = _job_args(job, 4, 4)
    res = pl.pallas_call(
        _hosting(body, job, 4, 4, 0, T // tm), name=name, grid=(T // tm,),
        in_specs=[row, row, pl.BlockSpec((tm, 128), lambda i: (i, 0)), vec] + j["in_specs"],
        out_specs=[row, row, vec, vec] + j["out_specs"],
        out_shape=[jax.ShapeDtypeStruct((T, D), F32), jax.ShapeDtypeStruct((T, D), BF16),
                   jax.ShapeDtypeStruct((1, D), F32), jax.ShapeDtypeStruct((1, D), F32)] + j["out_shape"],
        scratch_shapes=j["scratch"], input_output_aliases=j["aliases"],
        compiler_params=_params(("arbitrary",)),
    )(dy, xhat, rstd, g, *j["ins"])
    return list(res[:4]) if job is None else (list(res[:4]), list(res[4:]))


def loss_head(y, target):
    T, D = y.shape
    tm = min(512, T)

    def body(y_ref, t_ref, dy_ref, s_ref):
        e = y_ref[...] - t_ref[...]
        dy_ref[...] = e * (1.0 / D)

        @pl.when(pl.program_id(0) == 0)
        def _():
            s_ref[...] = jnp.zeros_like(s_ref)

        s_ref[...] += jnp.sum(jnp.mean(e * e, axis=-1, keepdims=True))

    row = pl.BlockSpec((tm, D), lambda i: (i, 0))
    return pl.pallas_call(
        body, name="loss_head", grid=(T // tm,),
        in_specs=[row, row], out_specs=[row, pl.BlockSpec((8, 128), lambda i: (0, 0))],
        out_shape=[jax.ShapeDtypeStruct((T, D), F32), jax.ShapeDtypeStruct((8, 128), F32)],
        compiler_params=_params(("arbitrary",)),
    )(y, target)


def _rope_tables(T):
    half = 64
    inv_freq = ROPE_BASE ** (-jnp.arange(half, dtype=F32) / half)
    ang = jnp.arange(T, dtype=jnp.int32).astype(F32)[:, None] * inv_freq[None, :]
    cos, sin = jnp.cos(ang), jnp.sin(ang)
    return jnp.concatenate([cos, cos], axis=1), jnp.concatenate([-sin, sin], axis=1)


def _ret_consts():
    H = RET_HEADS
    log_g = jnp.log(1.0 - 2.0 ** (-5.0 - jnp.arange(H, dtype=F32)))
    idx = jnp.arange(CHUNK, dtype=F32)
    diff = idx[:, None] - idx[None, :]
    dmat = jnp.where(diff[None] >= 0, jnp.exp(log_g[:, None, None] * diff[None]), 0.0)
    kd = jnp.exp(log_g[:, None] * (CHUNK - 1 - idx)[None, :])
    qd = jnp.exp(log_g[:, None] * (idx + 1.0)[None, :])
    cd = jnp.exp(log_g * CHUNK)
    full = (H, CHUNK, CHUNK)
    return (dmat.astype(F32), jnp.broadcast_to(kd[:, :, None], full), jnp.broadcast_to(qd[:, :, None], full),
            jnp.broadcast_to(cd[:, None, None], full))


def _swap_halves(v):
    return pltpu.roll(v, 64, 1)


def _group_norm(o):
    mu = jnp.mean(o, axis=-1, keepdims=True)
    xc = o - mu
    var = jnp.mean(xc * xc, axis=-1, keepdims=True)
    rstd = lax.rsqrt(var + LN_EPS)
    return xc * rstd, rstd


def ret_fwd(proj, cosf, sinf, consts, gn_g, gn_b, *, name):
    T = proj.shape[0]
    tb = min(512, T)
    nch = tb // CHUNK
    H = RET_HEADS

    def body(p_ref, cos_ref, sin_ref, dm_ref, kd_ref, qd_ref, cd_ref, g_ref, b_ref, out_ref, raw_ref, st_ref, s_ref):
        @pl.when(pl.program_id(0) == 0)
        def _():
            s_ref[...] = jnp.zeros_like(s_ref)

        for c in range(nch):
            r = slice(c * CHUNK, (c + 1) * CHUNK)
            cs, sn = cos_ref[r, :], sin_ref[r, :]
            for h in range(H):
                hc = slice(h * 128, (h + 1) * 128)
                q = p_ref[r, h * 128:(h + 1) * 128]
                k = p_ref[r, 512 + h * 128:512 + (h + 1) * 128]
                v = p_ref[r, 1024 + h * 128:1024 + (h + 1) * 128]
                gt = p_ref[r, 1536 + h * 128:1536 + (h + 1) * 128]
                qr = q * cs + _swap_halves(q) * sn
                kr = (k * cs + _swap_halves(k) * sn) * RET_SCALE
                sprev = s_ref[h]
                st_ref[c, h] = sprev
                qb, kb, vb = qr.astype(BF16), kr.astype(BF16), v.astype(BF16)
                s = _dot(qb, kb, NT) * dm_ref[h]
                o = _dot(s.astype(BF16), vb, NN) + _dot((qr * qd_ref[h]).astype(BF16), sprev.astype(BF16), NN)
                s_ref[h] = sprev * cd_ref[h] + _dot((kr * kd_ref[h]).astype(BF16), vb, TN)
                raw_ref[r, hc] = o
                y, _ = _group_norm(o)
                out_ref[r, hc] = (gt * jax.nn.sigmoid(gt)) * (y * g_ref[:, hc] + b_ref[:, hc])

    cmat = pl.BlockSpec((H, CHUNK, CHUNK), lambda i: (0, 0, 0))
    vec = pl.BlockSpec((1, BRANCH_W), lambda i: (0, 0))
    rope = pl.BlockSpec((tb, 128), lambda i: (i, 0))
    blk = pl.BlockSpec((tb, BRANCH_W), lambda i: (i, 0))
    return pl.pallas_call(
        body, name=name, grid=(T // tb,),
        in_specs=[pl.BlockSpec((tb, 2048), lambda i: (i, 0)), rope, rope, cmat, cmat, cmat, cmat, vec, vec],
        out_specs=[blk, blk, pl.BlockSpec((nch, H, CHUNK, CHUNK), lambda i: (i, 0, 0, 0))],
        out_shape=[jax.ShapeDtypeStruct((T, BRANCH_W), F32), jax.ShapeDtypeStruct((T, BRANCH_W), F32),
                   jax.ShapeDtypeStruct((T // CHUNK, H, CHUNK, CHUNK), F32)],
        scratch_shapes=[pltpu.VMEM((H, CHUNK, CHUNK), F32)],
        compiler_params=_params(("arbitrary",)),
    )(proj, cosf, sinf, *consts, gn_g, gn_b)


def ret_bwd(proj, cosf, sinf, consts, gn_g, gn_b, raw, states, dout, *, name, job=None):
    T = proj.shape[0]
    tb = min(512, T)
    nch = tb // CHUNK
    nb = T // tb
    H = RET_HEADS

    def body(p_ref, cos_ref, sin_ref, dm_ref, kd_ref, qd_ref, cd_ref, g_ref, b_ref, raw_ref, st_ref, do_ref,
             dp_ref, dg_ref, db_ref, ds_ref):
        @pl.when(pl.program_id(0) == 0)
        def _():
            ds_ref[...] = jnp.zeros_like(ds_ref)
            dg_ref[...] = jnp.zeros_like(dg_ref)
            db_ref[...] = jnp.zeros_like(db_ref)

        for c in reversed(range(nch)):
            r = slice(c * CHUNK, (c + 1) * CHUNK)
            cs, sn = cos_ref[r, :], sin_ref[r, :]
            for h in range(H):
                hc = slice(h * 128, (h + 1) * 128)
                q = p_ref[r, h * 128:(h + 1) * 128]
                k = p_ref[r, 512 + h * 128:512 + (h + 1) * 128]
                v = p_ref[r, 1024 + h * 128:1024 + (h + 1) * 128]
                gt = p_ref[r, 1536 + h * 128:1536 + (h + 1) * 128]
                qr = q * cs + _swap_halves(q) * sn
                kr = (k * cs + _swap_halves(k) * sn) * RET_SCALE
                sprev = st_ref[c, h]
                gv = g_ref[:, hc]
                y, rstd = _group_norm(raw_ref[r, hc])
                d_out = do_ref[r, hc]
                sg = jax.nn.sigmoid(gt)
                d_gate = d_out * (y * gv + b_ref[:, hc]) * (sg * (1.0 + gt * (1.0 - sg)))
                d_aff = d_out * (gt * sg)
                dg_ref[:, hc] += jnp.sum(d_aff * y, axis=0, keepdims=True)
                db_ref[:, hc] += jnp.sum(d_aff, axis=0, keepdims=True)
                dxh = d_aff * gv
                m1 = jnp.mean(dxh, axis=-1, keepdims=True)
                m2 = jnp.mean(dxh * y, axis=-1, keepdims=True)
                d_o = (rstd * (dxh - m1 - y * m2)).astype(BF16)
                qb, kb, vb = qr.astype(BF16), kr.astype(BF16), v.astype(BF16)
                dm, kd, qd = dm_ref[h], kd_ref[h], qd_ref[h]
                p = (_dot(qb, kb, NT) * dm).astype(BF16)
                dp = (_dot(d_o, vb, NT) * dm).astype(BF16)
                dsn = ds_ref[h]
                dsb = dsn.astype(BF16)
                dq_r = _dot(dp, kb, NN) + _dot(d_o, sprev.astype(BF16), NT) * qd
                dk_r = (_dot(dp, qb, TN) + _dot(vb, dsb, NT) * kd) * RET_SCALE
                d_v = _dot(p, d_o, TN) + _dot((kr * kd).astype(BF16), dsb, NN)
                ds_ref[h] = dsn * cd_ref[h] + _dot((qr * qd).astype(BF16), d_o, TN)
                dp_ref[r, h * 128:(h + 1) * 128] = (dq_r * cs - _swap_halves(dq_r) * sn).astype(BF16)
                dp_ref[r, 512 + h * 128:512 + (h + 1) * 128] = (dk_r * cs - _swap_halves(dk_r) * sn).astype(BF16)
                dp_ref[r, 1024 + h * 128:1024 + (h + 1) * 128] = d_v.astype(BF16)
                dp_ref[r, 1536 + h * 128:1536 + (h + 1) * 128] = d_gate.astype(BF16)

    cmat = pl.BlockSpec((H, CHUNK, CHUNK), lambda i: (0, 0, 0))
    vec = pl.BlockSpec((1, BRANCH_W), lambda i: (0, 0))
    rope = pl.BlockSpec((tb, 128), lambda i: (nb - 1 - i, 0))
    blk = pl.BlockSpec((tb, BRANCH_W), lambda i: (nb - 1 - i, 0))
    wide = pl.BlockSpec((tb, 2048), lambda i: (nb - 1 - i, 0))
    j = _job_args(job, 12, 3)
    res = pl.pallas_call(
        _hosting(body, job, 12, 3, 1, nb), name=name, grid=(nb,),
        in_specs=[wide, rope, rope, cmat, cmat, cmat, cmat, vec, vec, blk,
                  pl.BlockSpec((nch, H, CHUNK, CHUNK), lambda i: (nb - 1 - i, 0, 0, 0)), blk] + j["in_specs"],
        out_specs=[wide, vec, vec] + j["out_specs"],
        out_shape=[jax.ShapeDtypeStruct((T, 2048), BF16), jax.ShapeDtypeStruct((1, BRANCH_W), F32),
                   jax.ShapeDtypeStruct((1, BRANCH_W), F32)] + j["out_shape"],
        scratch_shapes=[pltpu.VMEM((H, CHUNK, CHUNK), F32)] + j["scratch"], input_output_aliases=j["aliases"],
        compiler_params=_params(("arbitrary",)),
    )(proj, cosf, sinf, *consts, gn_g, gn_b, raw, states, dout, *j["ins"])
    return res[0], res[1], res[2], list(res[3:])


def _sb_masks():
    row = lax.broadcasted_iota(jnp.int32, (CHUNK, CHUNK), 0)
    lane = lax.broadcasted_iota(jnp.int32, (CHUNK, CHUNK), 1)
    return row, lane


SB_QT = 256
SB_DEAD = -105.0


def _pair(v):
    hi = v.astype(BF16)
    return jnp.concatenate([hi, (v - hi.astype(F32)).astype(BF16)], axis=1)


def _sb_consts():
    r = lax.broadcasted_iota(jnp.int32, (256, 256), 0) & 127
    c = lax.broadcasted_iota(jnp.int32, (256, 256), 1)
    ones = c >= 128
    lane = lax.broadcasted_iota(jnp.int32, (CHUNK, CHUNK), 1)
    return (ones | (r > c)).astype(BF16), (ones | (r >= c)).astype(BF16), (lane < 64, lane >= 64)


def _per_head(x, hms):
    return jnp.concatenate([jnp.where(hm, x, 0.0) for hm in hms], axis=0).astype(BF16)


def _sb_logits(qb, kb2, mask2):
    z = _dot(qb, kb2, NT)
    l1p = jnp.log(1.0 + jnp.exp(-jnp.abs(z)))
    lsp = jnp.minimum(z, 0.0) - l1p
    lsn = lsp - z
    if mask2 is not None:
        lsn = jnp.where(mask2, lsn, 0.0)
    return lsp, lsn


def _sb_tile_mask(qt):
    trow = lax.broadcasted_iota(jnp.int32, (qt, 256), 0)
    tlane = lax.broadcasted_iota(jnp.int32, (qt, 256), 1) & 127
    return lambda m: (tlane + m * CHUNK) < trow


def sb_fwd(proj, *, name, job=None):
    T = proj.shape[0]
    qt = min(SB_QT, T)
    nsub = qt // CHUNK
    cb = C_SB // 128

    def body(q_ref, k_ref, v_ref, o_ref):
        u_gt, _, hms = _sb_consts()
        tile_mask = _sb_tile_mask(qt)

        def qtile(i, _):
            rq = pl.ds(pl.multiple_of(i * qt, qt), qt)
            qb = (q_ref[rq, :] * SB_SCALE).astype(BF16)

            def group(js, masks, state):
                carry, acc = list(state[:2]), state[2]
                rows = [pl.ds(pl.multiple_of(j * CHUNK, CHUNK), CHUNK) for j in js]
                logits = [_sb_logits(qb, _per_head(k_ref[rk, :], hms), m) for rk, m in zip(rows, masks)]
                sums = [[_dot(_pair(lsn[:, h * 128:(h + 1) * 128]), u_gt, NN) for h in range(2)] for _, lsn in logits]
                weights = []
                for (lsp, _), r, m in zip(logits, sums, masks):
                    a_b = []
                    for h in range(2):
                        hc = slice(h * 128, (h + 1) * 128)
                        a = jnp.exp(lsp[:, hc] + r[h][:, :128] + carry[h])
                        if m is not None:
                            a = jnp.where(m[:, hc], a, 0.0)
                        carry[h] = carry[h] + r[h][:, 128:]
                        a_b.append(a.astype(BF16))
                    weights.append(jnp.concatenate(a_b, axis=1))
                for rk, a in zip(rows, weights):
                    acc = acc + _dot(a, _per_head(v_ref[rk, :], hms), NN)
                return carry[0], carry[1], acc

            zero = jnp.zeros((qt, 128), F32)
            diag = list(reversed(range(nsub)))
            state = group([i * nsub + m for m in diag], [tile_mask(m) for m in diag], (zero, zero, zero))

            def live(c):
                return jnp.logical_and(c[0] < i, jnp.maximum(jnp.max(c[1][0]), jnp.max(c[1][1])) > SB_DEAD)

            def blocks(c):
                jj, st = c
                return jj + 1, group([(i - jj) * nsub - 1 - u for u in range(nsub)], [None] * nsub, st)

            _, state = lax.while_loop(live, blocks, (jnp.int32(0), state))
            o_ref[rq, :] = state[2]
            return 0

        lax.fori_loop(0, T // qt, qtile, 0)

    def col(off):
        return pl.BlockSpec((T, 128), lambda hp: (0, off + hp))

    steps = BRANCH_W // 128
    j = _job_args(job, 3, 1)
    res = pl.pallas_call(
        _hosting(body, job, 3, 1, 0, steps), name=name, grid=(steps,),
        in_specs=[col(cb), col(cb + 4), col(cb + 8)] + j["in_specs"], out_specs=[col(0)] + j["out_specs"],
        out_shape=[jax.ShapeDtypeStruct((T, BRANCH_W), F32)] + j["out_shape"],
        scratch_shapes=j["scratch"], input_output_aliases=j["aliases"],
        compiler_params=_params(("parallel",) if job is None else ("arbitrary",)),
    )(proj, proj, proj, *j["ins"])
    return res[0], list(res[1:])


def sb_bwd(proj, out, dout, *, name, job=None):
    T = proj.shape[0]
    qt = min(SB_QT, T)
    nsub = qt // CHUNK
    cb = C_SB // 128

    def body(q_ref, k_ref, v_ref, o_ref, do_ref, dq_ref, dk_ref, dv_ref, dkt_ref, dvt_ref):
        u_gt, u_ge, hms = _sb_consts()
        tile_mask = _sb_tile_mask(qt)
        tall_lane = lax.broadcasted_iota(jnp.int32, (qt, 128), 1)
        top = lax.broadcasted_iota(jnp.int32, (CHUNK, CHUNK), 0) < 64
        dkt_ref[...] = jnp.zeros_like(dkt_ref)
        dvt_ref[...] = jnp.zeros_like(dvt_ref)

        def qtile(i, _):
            rq = pl.ds(pl.multiple_of(i * qt, qt), qt)
            qs = q_ref[rq, :] * SB_SCALE
            qb, q_t = qs.astype(BF16), qs.T.astype(BF16)
            dov = do_ref[rq, :]
            dob, do_t = dov.astype(BF16), dov.T.astype(BF16)
            prod = dob.astype(F32) * o_ref[rq, :]
            total = [jnp.broadcast_to(jnp.sum(jnp.where(hm, prod, 0.0), axis=1, keepdims=True), (qt, 128))
                     for hm in (tall_lane < 64, tall_lane >= 64)]

            def group(js, masks, state):
                c_l, c_w, dq = list(state[:2]), list(state[2:4]), state[4]
                heads = [slice(h * 128, (h + 1) * 128) for h in range(2)]
                rows = [pl.ds(pl.multiple_of(j * CHUNK, CHUNK), CHUNK) for j in js]
                kb2 = [_per_head(k_ref[rk, :], hms) for rk in rows]
                logits = [_sb_logits(qb, kb, m) for kb, m in zip(kb2, masks)]
                da = [_dot(dob, _per_head(v_ref[rk, :], hms), NT) for rk in rows]
                sums = [[_dot(_pair(lsn[:, hc]), u_gt, NN) for hc in heads] for _, lsn in logits]
                a_b, w_all = [], []
                for (lsp, _), r, d, m in zip(logits, sums, da, masks):
                    a_h, w_h = [], []
                    for h, hc in enumerate(heads):
                        a = jnp.exp(lsp[:, hc] + r[h][:, :128] + c_l[h])
                        if m is not None:
                            a = jnp.where(m[:, hc], a, 0.0)
                        c_l[h] = c_l[h] + r[h][:, 128:]
                        a = a.astype(BF16)
                        a_h.append(a)
                        w_h.append(a.astype(F32) * d[:, hc])
                    a_b.append(jnp.concatenate(a_h, axis=1))
                    w_all.append(w_h)
                sums_w = [[_dot(_pair(w), u_ge, NN) for w in w_h] for w_h in w_all]
                dz_b = []
                for (lsp, _), w_h, r, m in zip(logits, w_all, sums_w, masks):
                    sp = jnp.exp(lsp)
                    dz_h = []
                    for h, hc in enumerate(heads):
                        later_w = r[h][:, :128] + c_w[h]
                        c_w[h] = c_w[h] + r[h][:, 128:]
                        dz = w_h[h] * (1.0 - sp[:, hc]) - sp[:, hc] * (total[h] - later_w)
                        if m is not None:
                            dz = jnp.where(m[:, hc], dz, 0.0)
                        dz_h.append(dz.astype(BF16))
                    dz_b.append(jnp.concatenate(dz_h, axis=1))
                for j, kb, a, dz in zip(js, kb2, a_b, dz_b):
                    dkt = _dot(q_t, dz, NN)
                    dvt = _dot(do_t, a, NN)
                    dkt_ref[j] += jnp.where(top, dkt[:, :128], dkt[:, 128:])
                    dvt_ref[j] += jnp.where(top, dvt[:, :128], dvt[:, 128:])
                    dq = dq + _dot(dz, kb, NN)
                return c_l[0], c_l[1], c_w[0], c_w[1], dq

            zero = jnp.zeros((qt, 128), F32)
            diag = list(reversed(range(nsub)))
            state = group([i * nsub + m for m in diag], [tile_mask(m) for m in diag], (zero,) * 5)

            def live(c):
                return jnp.logical_and(c[0] < i, jnp.maximum(jnp.max(c[1][0]), jnp.max(c[1][1])) > SB_DEAD)

            def blocks(c):
                jj, st = c
                return jj + 1, group([(i - jj) * nsub - 1 - u for u in range(nsub)], [None] * nsub, st)

            _, state = lax.while_loop(live, blocks, (jnp.int32(0), state))
            dq_ref[rq, :] = (state[4] * SB_SCALE).astype(BF16)
            return 0

        lax.fori_loop(0, T // qt, qtile, 0)

        def untranspose(jb, _):
            rk = pl.ds(pl.multiple_of(jb * CHUNK, CHUNK), CHUNK)
            dk_ref[rk, :] = dkt_ref[jb].T.astype(BF16)
            dv_ref[rk, :] = dvt_ref[jb].T.astype(BF16)
            return 0

        lax.fori_loop(0, T // CHUNK, untranspose, 0)

    def col(off):
        return pl.BlockSpec((T, 128), lambda hp: (0, off + hp))

    o16 = jax.ShapeDtypeStruct((T, BRANCH_W), BF16)
    steps = BRANCH_W // 128
    j = _job_args(job, 5, 3)
    acc = pltpu.VMEM((T // CHUNK, CHUNK, CHUNK), F32)
    res = pl.pallas_call(
        _hosting(body, job, 5, 3, 2, steps), name=name, grid=(steps,),
        in_specs=[col(cb), col(cb + 4), col(cb + 8), col(0), col(0)] + j["in_specs"],
        out_specs=[col(0), col(0), col(0)] + j["out_specs"], out_shape=[o16, o16, o16] + j["out_shape"],
        scratch_shapes=[acc, acc] + j["scratch"], input_output_aliases=j["aliases"],
        compiler_params=_params(("parallel",) if job is None else ("arbitrary",)),
    )(proj, proj, proj, out, dout, *j["ins"])
    return res[0], res[1], res[2], list(res[3:])


_G0 = math.sqrt(2.0 / math.pi)
_G1 = 0.044715


def _gelu(x):
    return 0.5 * x * (1.0 + jnp.tanh(_G0 * (x + _G1 * x * x * x)))


def _gelu_grad(x):
    t = jnp.tanh(_G0 * (x + _G1 * x * x * x))
    return 0.5 * (1.0 + t) + 0.5 * x * (1.0 - t * t) * (_G0 * (1.0 + 3.0 * _G1 * x * x))


def _tril():
    row, lane = _sb_masks()
    return row >= lane


def sgu_fwd(proj, ln_g, ln_b, w, bias, *, name):
    T = proj.shape[0]
    tb = min(512, T)
    G = BRANCH_W // 128

    def body(u_ref, v_ref, g_ref, b_ref, w_ref, bias_ref, o_ref):
        vv = _gelu(v_ref[...])
        xh, _ = _group_norm(vv)
        vn = (xh * g_ref[...] + b_ref[...]).astype(BF16)
        tril = _tril()
        for g in range(G):
            wg = jnp.where(tril, w_ref[g], 0.0).astype(BF16)
            gc = slice(g * 128, (g + 1) * 128)
            for c in range(tb // CHUNK):
                r = slice(c * CHUNK, (c + 1) * CHUNK)
                sv = _dot(wg, vn[r, gc], NN) + bias_ref[g]
                o_ref[r, gc] = _gelu(u_ref[r, gc]) * sv

    cu, cv = C_SGU // BRANCH_W, C_SGU // BRANCH_W + 1
    vec = pl.BlockSpec((1, BRANCH_W), lambda i: (0, 0))
    mat = pl.BlockSpec((G, CHUNK, CHUNK), lambda i: (0, 0, 0))
    return pl.pallas_call(
        body, name=name, grid=(T // tb,),
        in_specs=[pl.BlockSpec((tb, BRANCH_W), lambda i: (i, cu)), pl.BlockSpec((tb, BRANCH_W), lambda i: (i, cv)),
                  vec, vec, mat, mat],
        out_specs=pl.BlockSpec((tb, BRANCH_W), lambda i: (i, 0)),
        out_shape=jax.ShapeDtypeStruct((T, BRANCH_W), F32),
        compiler_params=_params(("parallel",)),
    )(proj, proj, ln_g, ln_b, w, bias)


def sgu_bwd(proj, ln_g, ln_b, w, bias, dout, *, name):
    T = proj.shape[0]
    tb = min(512, T)
    G = BRANCH_W // 128

    def body(u_ref, v_ref, g_ref, b_ref, w_ref, bias_ref, do_ref, dp_ref, dw_ref, dbias_ref, dg_ref, db_ref, dvn_ref):
        @pl.when(pl.program_id(0) == 0)
        def _():
            dw_ref[...] = jnp.zeros_like(dw_ref)
            dbias_ref[...] = jnp.zeros_like(dbias_ref)
            dg_ref[...] = jnp.zeros_like(dg_ref)
            db_ref[...] = jnp.zeros_like(db_ref)

        gv = v_ref[...]
        vv = _gelu(gv)
        xh, rstd = _group_norm(vv)
        vn = (xh * g_ref[...] + b_ref[...]).astype(BF16)
        tril = _tril()
        for g in range(G):
            wg = jnp.where(tril, w_ref[g], 0.0).astype(BF16)
            gc = slice(g * 128, (g + 1) * 128)
            for c in range(tb // CHUNK):
                r = slice(c * CHUNK, (c + 1) * CHUNK)
                vn_c = vn[r, gc]
                sv = _dot(wg, vn_c, NN) + bias_ref[g]
                gu = u_ref[r, gc]
                d_o = do_ref[r, gc]
                dp_ref[r, gc] = (d_o * sv * _gelu_grad(gu)).astype(BF16)
                dsv = d_o * _gelu(gu)
                dsv_b = dsv.astype(BF16)
                dvn_ref[r, gc] = _dot(wg, dsv_b, TN)
                dw_ref[g] += jnp.where(tril, _dot(dsv_b, vn_c, NT), 0.0)
                dbias_ref[g] += jnp.broadcast_to(jnp.sum(dsv, axis=1, keepdims=True), (CHUNK, CHUNK))
        dvn = dvn_ref[...]
        dg_ref[...] += jnp.sum(dvn * xh, axis=0, keepdims=True)
        db_ref[...] += jnp.sum(dvn, axis=0, keepdims=True)
        dxh = dvn * g_ref[...]
        m1 = jnp.mean(dxh, axis=-1, keepdims=True)
        m2 = jnp.mean(dxh * xh, axis=-1, keepdims=True)
        dp_ref[:, BRANCH_W:2 * BRANCH_W] = (rstd * (dxh - m1 - xh * m2) * _gelu_grad(gv)).astype(BF16)

    cu, cv = C_SGU // BRANCH_W, C_SGU // BRANCH_W + 1
    vec = pl.BlockSpec((1, BRANCH_W), lambda i: (0, 0))
    mat = pl.BlockSpec((G, CHUNK, CHUNK), lambda i: (0, 0, 0))
    blk = pl.BlockSpec((tb, BRANCH_W), lambda i: (i, 0))
    msh = jax.ShapeDtypeStruct((G, CHUNK, CHUNK), F32)
    vsh = jax.ShapeDtypeStruct((1, BRANCH_W), F32)
    return pl.pallas_call(
        body, name=name, grid=(T // tb,),
        in_specs=[pl.BlockSpec((tb, BRANCH_W), lambda i: (i, cu)), pl.BlockSpec((tb, BRANCH_W), lambda i: (i, cv)),
                  vec, vec, mat, mat, blk],
        out_specs=[pl.BlockSpec((tb, 2 * BRANCH_W), lambda i: (i, 0)), mat, mat, vec, vec],
        out_shape=[jax.ShapeDtypeStruct((T, 2 * BRANCH_W), BF16), msh, msh, vsh, vsh],
        scratch_shapes=[pltpu.VMEM((tb, BRANCH_W), F32)],
        compiler_params=_params(("arbitrary",)),
    )(proj, proj, ln_g, ln_b, w, bias, dout)


def merge_fwd(a1, a2, a3, p1, p2, p3, proj, *, name):
    T = a1.shape[0]
    tm, tn = min(1024, T), 512
    gb = C_GATE // tn

    def body(a1_ref, a2_ref, a3_ref, p1_ref, p2_ref, p3_ref, g1_ref, g2_ref, g3_ref, m_ref, r1_ref, r2_ref, r3_ref):
        m = None
        for a_ref, p_ref, g_ref, r_ref in ((a1_ref, p1_ref, g1_ref, r1_ref), (a2_ref, p2_ref, g2_ref, r2_ref),
                                           (a3_ref, p3_ref, g3_ref, r3_ref)):
            r = _dot(a_ref[...].astype(BF16), p_ref[...], NN)
            r_ref[...] = r.astype(r_ref.dtype)
            t = jax.nn.sigmoid(g_ref[...]) * r
            m = t if m is None else m + t
        m_ref[...] = m.astype(m_ref.dtype)

    a_spec = pl.BlockSpec((tm, BRANCH_W), lambda i, j: (i, 0))
    p_spec = pl.BlockSpec((BRANCH_W, tn), lambda i, j: (0, j))
    o_spec = pl.BlockSpec((tm, tn), lambda i, j: (i, j))
    osh = jax.ShapeDtypeStruct((T, D_MODEL), F32)
    gates = [pl.BlockSpec((tm, tn), functools.partial(lambda i, j, o: (i, o + j), o=gb + 2 * n)) for n in range(3)]
    return pl.pallas_call(
        body, name=name, grid=(T // tm, D_MODEL // tn),
        in_specs=[a_spec, a_spec, a_spec, p_spec, p_spec, p_spec, *gates],
        out_specs=[o_spec] * 4, out_shape=[jax.ShapeDtypeStruct((T, D_MODEL), BF16)] * 4,
        compiler_params=_params(("parallel", "parallel")),
    )(a1, a2, a3, p1, p2, p3, proj, proj, proj)


def merge_bwd(dm, r1, r2, r3, proj, *, name):
    T = dm.shape[0]
    tm, tn = min(512, T), 512
    gb = C_GATE // tn

    def body(dm_ref, r1_ref, r2_ref, r3_ref, g1_ref, g2_ref, g3_ref, dr1_ref, dr2_ref, dr3_ref, dg1_ref, dg2_ref, dg3_ref):
        d = dm_ref[...]
        for r_ref, g_ref, dr_ref, dg_ref in ((r1_ref, g1_ref, dr1_ref, dg1_ref), (r2_ref, g2_ref, dr2_ref, dg2_ref),
                                             (r3_ref, g3_ref, dr3_ref, dg3_ref)):
            s = jax.nn.sigmoid(g_ref[...])
            dr_ref[...] = (d * s).astype(BF16)
            dg_ref[...] = (d * r_ref[...].astype(F32) * (s * (1.0 - s))).astype(BF16)

    o_spec = pl.BlockSpec((tm, tn), lambda i, j: (i, j))
    osh = jax.ShapeDtypeStruct((T, D_MODEL), BF16)
    gates = [pl.BlockSpec((tm, tn), functools.partial(lambda i, j, o: (i, o + j), o=gb + 2 * n)) for n in range(3)]
    return pl.pallas_call(
        body, name=name, grid=(T // tm, D_MODEL // tn),
        in_specs=[o_spec] * 4 + gates, out_specs=[o_spec] * 6, out_shape=[osh] * 6,
        compiler_params=_params(("parallel", "parallel")),
    )(dm, r1, r2, r3, proj, proj, proj)


def _rows_call(fn, ins, out_dtypes, *, name, tr=256):
    first = ins[0][0] if isinstance(ins[0], tuple) else ins[0]
    R, C = first.shape[-2:]
    tr = min(tr, R)
    assert R % tr == 0, (name, R, tr)
    arrs, specs = [], []
    for x in ins:
        if isinstance(x, tuple):
            arrs.append(x[0])
            specs.append(pl.BlockSpec((None, tr, C), functools.partial(lambda i, n: (n, i, 0), n=x[1])))
        else:
            arrs.append(x)
            specs.append(pl.BlockSpec((tr, C), lambda i: (i, 0)))
    ni = len(arrs)

    def body(*refs):
        vals = fn(*[r[...] for r in refs[:ni]])
        for o_ref, v in zip(refs[ni:], vals):
            o_ref[...] = v.astype(o_ref.dtype)

    res = pl.pallas_call(
        body, name=name, grid=(R // tr,), in_specs=specs,
        out_specs=[pl.BlockSpec((tr, C), lambda i: (i, 0)) for _ in out_dtypes],
        out_shape=[jax.ShapeDtypeStruct((R, C), dt) for dt in out_dtypes],
        compiler_params=_params(("parallel",)),
    )(*arrs)
    return res


def _tile_rows(rows, cols):
    t = 256
    while t > 8 and (t * cols > 512 * 1024 or rows % t):
        t //= 2
    return t


def _rows_at(fn, pos, ins, outs, steps, *, name, aliases=None):
    read = [n for n, (_, s) in enumerate(ins) if s is not ANY]
    ni = len(ins)

    def body(pos_ref, *refs):
        vals = fn(*[refs[n][...] for n in read])
        for o_ref, v in zip(refs[ni:], vals):
            o_ref[...] = v.astype(o_ref.dtype)

    return pl.pallas_call(
        body, name=name,
        grid_spec=pltpu.PrefetchScalarGridSpec(num_scalar_prefetch=1, grid=(steps,), in_specs=[s for _, s in ins],
                                               out_specs=[s for _, s in outs]),
        out_shape=[sh for sh, _ in outs],
        input_output_aliases={1 + i: o for i, o in (aliases or {}).items()},
        compiler_params=_params(("parallel",)),
    )(pos, *[a for a, _ in ins])


def cast_into_whole(pos, w, l, axis, *, name):
    _, r, n = w.shape
    tr = _tile_rows(r, n)
    if axis == 1:
        shape, spec = (r, n * N_CHIPS), pl.BlockSpec((tr, n), lambda i, p: (i, p[3]))
    else:
        shape, spec = (r * N_CHIPS, n), pl.BlockSpec((tr, n), lambda i, p: (p[3] * (r // tr) + i, 0))
    return _rows_at(lambda a: (a,), pos, [(w, pl.BlockSpec((None, tr, n), lambda i, p: (l, i, 0)))],
                    [(jax.ShapeDtypeStruct(shape, BF16), spec)], r // tr, name=name)[0]


def pair_sum(pos, theirs, g32, axis, *, name):
    rows2, cols = theirs.shape
    h = rows2 // (N_CHIPS if axis == 0 else 1)
    tr = _tile_rows(h, cols)
    hb = h // tr
    if axis == 1:
        own = pl.BlockSpec((tr, cols), lambda i, p: (p[2] * hb + i, 0))
    else:
        own = pl.BlockSpec((tr, cols), lambda i, p: ((2 * (i // hb) + p[2]) * hb + i % hb, 0))
    row = pl.BlockSpec((tr, cols), lambda i, p: (i, 0))
    return _rows_at(lambda t, m: (m + t.astype(F32),) * 2, pos, [(theirs, row), (g32, own)],
                    [(jax.ShapeDtypeStruct((rows2, cols), F32), row), (jax.ShapeDtypeStruct((rows2, cols), BF16), row)],
                    rows2 // tr, name=name)


def chip_sum(pos, h32, recv, l, axis, whole, *, name):
    _, depth, h, n = recv.shape
    tr = _tile_rows(h, n)
    hb = h // tr
    if axis == 1:
        mine = pl.BlockSpec((tr, n), lambda i, p: (i, p[3]))
    else:
        mine = pl.BlockSpec((tr, n), lambda i, p: (p[3] * hb + i, 0))
    ins = [(h32, mine)] + [(recv, pl.BlockSpec((None, None, tr, n), functools.partial(lambda i, p, j: (j, l, i, 0), j=j)))
                           for j in range(3)]
    if whole is not None:
        ins.append((whole, ANY))
    return _rows_at(lambda o, a, b, c: (((o + a.astype(F32)) + b.astype(F32)) + c.astype(F32),), pos, ins,
                    [(jax.ShapeDtypeStruct((depth, 2, h, n), F32), pl.BlockSpec((None, None, tr, n), lambda i, p: (l, p[2], i, 0)))],
                    hb, name=name, aliases=None if whole is None else {4: 0})[0]


def _adamw(w, g, m, v):
    m2 = ADAM_B1 * m + (1.0 - ADAM_B1) * g
    v2 = ADAM_B2 * v + (1.0 - ADAM_B2) * (g * g)
    m_hat = m2 / (1.0 - ADAM_B1 ** ADAM_STEP)
    v_hat = v2 / (1.0 - ADAM_B2 ** ADAM_STEP)
    delta = -ADAM_LR * (m_hat / (jnp.sqrt(v_hat) + ADAM_EPS) + ADAM_WD * w)
    return delta, m2, v2


def _place():
    return lax.axis_index("x"), lax.axis_index("y"), lax.axis_index("c")


def _chip_peers(x, y, c):
    return [((1 - x, y, c), 2 * (1 - x) + y), ((x, 1 - y, c), 2 * x + 1 - y), ((1 - x, 1 - y, c), 2 * (1 - x) + 1 - y)]


def _shard_of(ref, axis, k, n):
    start = pl.multiple_of(k * n, 128)
    return ref.at[pl.ds(start, n), :] if axis == 0 else ref.at[:, pl.ds(start, n)]


ANY = pl.BlockSpec(memory_space=pl.ANY)


class CopyJob:
    def __init__(self, ins, out_shape, scratch, copies, aliases=None):
        self.ins, self.out_shape, self.scratch, self.copies = list(ins), list(out_shape), list(scratch), copies
        self.aliases = dict(aliases or {})

    def start(self, ins, outs, sems):
        local, remote, _, _ = self.copies(ins, outs, sems)
        for d in local + remote:
            d.start()

    def finish(self, ins, outs, sems):
        local, remote, arrivals, relays = self.copies(ins, outs, sems)
        for needs, sends, _ in relays:
            for d in needs:
                d.wait_recv()
            for d in sends:
                d.start()
        for d in arrivals + [d for _, _, arrives in relays for d in arrives]:
            d.wait_recv()
        for d in remote + [d for _, sends, _ in relays for d in sends]:
            d.wait_send()
        for d in local:
            d.wait()


def run_job(job, *, name):
    ni, no = len(job.ins), len(job.out_shape)

    def body(*refs):
        parts = refs[:ni], refs[ni:ni + no], refs[ni + no:]
        job.start(*parts)
        job.finish(*parts)

    return pl.pallas_call(
        body, name=name, in_specs=[ANY] * ni, out_specs=[ANY] * no, out_shape=job.out_shape,
        scratch_shapes=job.scratch, input_output_aliases=job.aliases,
    )(*job.ins)


def _job_args(job, n_in, n_out):
    if job is None:
        return dict(ins=[], in_specs=[], out_specs=[], out_shape=[], scratch=[], aliases={})
    return dict(ins=job.ins, in_specs=[ANY] * len(job.ins), out_specs=[ANY] * len(job.out_shape),
                out_shape=job.out_shape, scratch=job.scratch,
                aliases={n_in + i: n_out + o for i, o in job.aliases.items()})


def _hosting(body, job, n_in, n_out, n_scratch, grid):
    if job is None:
        return body
    ji, jo = len(job.ins), len(job.out_shape)
    grid = (grid,) if isinstance(grid, int) else tuple(grid)

    def at(ends):
        hit = None
        for ax, e in enumerate(ends):
            here = pl.program_id(ax) == e
            hit = here if hit is None else jnp.logical_and(hit, here)
        return hit

    def hosted(*refs):
        o = n_in + ji
        s = o + n_out + jo
        parts = refs[n_in:o], refs[o + n_out:s], refs[s + n_scratch:]

        @pl.when(at([0] * len(grid)))
        def _():
            job.start(*parts)

        body(*refs[:n_in], *refs[o:o + n_out], *refs[s:s + n_scratch])

        @pl.when(at([g - 1 for g in grid]))
        def _():
            job.finish(*parts)

    return hosted


def _job_sems(n_remote, n_local):
    return [pltpu.SemaphoreType.DMA((n_remote,)), pltpu.SemaphoreType.DMA((n_remote,)), pltpu.SemaphoreType.DMA((n_local,))]


def gather_job(shards, axes, chips=(0, 1, 2)):
    na = len(shards)

    def copies(ins, outs, sems):
        send, recv, _ = sems
        x, y, c = _place()
        k = 2 * x + y
        remote, relays = [], []
        for a in range(na):
            r = outs[a].shape[0] // (N_CHIPS if axes[a] == 0 else 1)
            n = outs[a].shape[axes[a]] // N_CHIPS
            half = r // 2

            def part(kk, cc, a=a, n=n, half=half):
                rows = pl.ds(pl.multiple_of(cc * half + (kk * n if axes[a] == 0 else 0), 8), half)
                return outs[a].at[rows, :] if axes[a] == 0 else outs[a].at[rows, pl.ds(pl.multiple_of(kk * n, 128), n)]

            needs, passes, lands = [], [], []
            for j, (peer, kp) in enumerate(_chip_peers(x, y, c)):
                if j not in chips:
                    continue
                s = 6 * a + j
                remote.append(pltpu.make_async_remote_copy(part(k, c), part(k, c), send.at[s], recv.at[s],
                                                           device_id=peer, device_id_type=MESH))
                needs.append(pltpu.make_async_remote_copy(part(kp, c), part(kp, c), send.at[s], recv.at[s],
                                                          device_id=peer, device_id_type=MESH))
                passes.append(pltpu.make_async_remote_copy(part(kp, c), part(kp, c), send.at[s + 3], recv.at[s + 3],
                                                           device_id=(x, y, 1 - c), device_id_type=MESH))
                lands.append(pltpu.make_async_remote_copy(part(kp, 1 - c), part(kp, 1 - c), send.at[s + 3], recv.at[s + 3],
                                                          device_id=(x, y, 1 - c), device_id_type=MESH))
            relays.append((needs, passes, lands))
        return [], remote, [], relays

    out_shape = [jax.ShapeDtypeStruct(w.shape, BF16) for w in shards]
    return CopyJob(shards, out_shape, _job_sems(6 * na, 1), copies, {a: a for a in range(na)})


def scatter_job(layers, g16, axes, filled, chips=(0, 1, 2)):
    na = len(axes)

    def shard_shape(a):
        r, c = g16[a].shape
        return (r // N_CHIPS, c) if axes[a] == 0 else (r, c // N_CHIPS)

    def copies(ins, outs, sems):
        send, recv_sems, _ = sems
        x, y, c = _place()
        remote = []
        for a in range(na):
            n = shard_shape(a)[axes[a]]
            for r, (peer, kp) in enumerate(_chip_peers(x, y, c)):
                if r not in chips:
                    continue
                remote.append(pltpu.make_async_remote_copy(_shard_of(ins[a], axes[a], kp, n), outs[a].at[r, layers[a]],
                                                           send.at[3 * a + r], recv_sems.at[3 * a + r],
                                                           device_id=peer, device_id_type=MESH))
        return [], remote, remote, []

    out_shape = [jax.ShapeDtypeStruct((3, DEPTH) + shard_shape(a), BF16) for a in range(na)]
    ins = list(g16)
    aliases = {}
    for a in range(na):
        if filled[a] is not None:
            aliases[len(ins)] = a
            ins.append(filled[a])
    return CopyJob(ins, out_shape, _job_sems(3 * na, 1), copies, aliases)


def pair_job(g16, axes):
    na = len(axes)
    pieces = [1 if ax == 1 else N_CHIPS for ax in axes]

    def copies(ins, outs, sems):
        send, recv, _ = sems
        x, y, c = _place()
        remote = []
        s = 0
        for a in range(na):
            rows = g16[a].shape[0] // (2 * pieces[a])
            for kk in range(pieces[a]):
                src = ins[a].at[pl.ds(pl.multiple_of((2 * kk + 1 - c) * rows, 8), rows), :]
                remote.append(pltpu.make_async_remote_copy(src, outs[a].at[pl.ds(kk * rows, rows), :], send.at[s], recv.at[s],
                                                           device_id=(x, y, 1 - c), device_id_type=MESH))
                s += 1
        return [], remote, remote, []

    out_shape = [jax.ShapeDtypeStruct((g.shape[0] // 2, g.shape[1]), BF16) for g in g16]
    return CopyJob(g16, out_shape, _job_sems(sum(pieces), 1), copies)


def join_job(shards):
    na = len(shards)

    def copies(ins, outs, sems):
        send, recv, _ = sems
        x, y, c = _place()
        remote = [pltpu.make_async_remote_copy(outs[a].at[:, c], outs[a].at[:, c], send.at[a], recv.at[a],
                                               device_id=(x, y, 1 - c), device_id_type=MESH) for a in range(na)]
        lands = [pltpu.make_async_remote_copy(outs[a].at[:, 1 - c], outs[a].at[:, 1 - c], send.at[a], recv.at[a],
                                              device_id=(x, y, 1 - c), device_id_type=MESH) for a in range(na)]
        return [], remote, lands, []

    out_shape = [jax.ShapeDtypeStruct(s.shape, F32) for s in shards]
    return CopyJob(shards, out_shape, _job_sems(na, 1), copies, {a: a for a in range(na)})


def small_job(p):
    def copies(ins, outs, sems):
        send, recv, loc = sems
        x, y, c = _place()
        me = 4 * x + 2 * y + c
        remote, lands = [], []
        for rel in range(1, 8):
            dx, dy, dc = rel >> 2, (rel >> 1) & 1, rel & 1
            peer = (1 - x if dx else x, 1 - y if dy else y, 1 - c if dc else c)
            who = 4 * peer[0] + 2 * peer[1] + peer[2]
            remote.append(pltpu.make_async_remote_copy(ins[0], outs[0].at[me], send.at[rel - 1], recv.at[rel - 1],
                                                       device_id=peer, device_id_type=MESH))
            lands.append(pltpu.make_async_remote_copy(ins[0], outs[0].at[who], send.at[rel - 1], recv.at[rel - 1],
                                                      device_id=peer, device_id_type=MESH))
        return [pltpu.make_async_copy(ins[0], outs[0].at[me], loc.at[0])], remote, lands, []

    return CopyJob([p], [jax.ShapeDtypeStruct((8,) + p.shape, F32)], _job_sems(7, 1), copies)


def small_sum(slots):
    def add(*terms):
        acc = terms[0]
        for t in terms[1:]:
            acc = acc + t
        return (acc,)

    return _rows_call(add, [(slots, d) for d in range(8)], [F32], name="small_sum", tr=8 * 47)[0]


BIG = ("w_in", "p_ret", "p_sb", "p_sgu", "w_out", "w_up", "w_down")
BIG_AXIS = {"w_in": 1, "p_ret": 1, "p_sb": 1, "p_sgu": 1, "w_out": 0, "w_up": 1, "w_down": 0}
SMALL = ("ret_gn_g", "ret_gn_b", "sgu_ln_g", "sgu_ln_b", "sgu_w", "sgu_b", "ln1_g", "ln1_b", "ln2_g", "ln2_b")


def layer_forward(l, x0, W, sm, rope, rconsts, hooks):
    n = f"l{l}_"
    job = hooks.fwd_job(l, "proj")
    proj = matmul(x0, W["w_in"], mode="nn", tm=2048, tn=640, tk=1024, name=n + "proj", job=job)
    if job is not None:
        proj, job_out = proj
        hooks.done(job, job_out)
    retg, raw, states = ret_fwd(proj, *rope, rconsts, sm["ret_gn_g"], sm["ret_gn_b"], name=n + "ret_fwd")
    job = hooks.fwd_job(l, "sb")
    sb, job_out = sb_fwd(proj, name=n + "sb_fwd", job=job)
    if job is not None:
        hooks.done(job, job_out)
    sg = sgu_fwd(proj, sm["sgu_ln_g"], sm["sgu_ln_b"], sm["sgu_w"], sm["sgu_bias"], name=n + "sgu_fwd")
    merged, r1, r2, r3 = merge_fwd(retg, sb, sg, W["p_ret"], W["p_sb"], W["p_sgu"], proj, name=n + "merge_fwd")
    x1, xh1, rs1 = matmul_ln(merged, W["w_out"], x0, sm["ln1_g"], sm["ln1_b"], tk=1024, name=n + "out_ln1")
    job = hooks.fwd_job(l, "up")
    h1 = matmul(x1, W["w_up"], mode="nn", tm=1024, tn=1024, tk=1024, outs=((BF16, None),), name=n + "up", job=job)
    if job is not None:
        h1, job_out = h1
        hooks.done(job, job_out)
    job = hooks.fwd_job(l, "down")
    res = matmul_ln(h1, W["w_down"], x1, sm["ln2_g"], sm["ln2_b"], pro=_relu2, tk=1024, name=n + "down_ln2", job=job)
    if job is not None:
        res, job_out = res
        hooks.done(job, job_out)
    x2, xh2, rs2 = res
    saved = dict(x0=x0, proj=proj, retg=retg, raw=raw, states=states, sb=sb, sg=sg, merged=merged, r=(r1, r2, r3),
                 x1=x1, xh1=xh1, rs1=rs1, h1=h1, xh2=xh2, rs2=rs2)
    return x2, saved


def layer_backward(l, dx2, s, W, sm, rope, rconsts, hooks):
    n = f"l{l}_"
    two = ((F32, None), (BF16, None))
    gw, gs = {}, {}
    job = hooks.bwd_job(l, "ln2")
    res = ln_bwd(dx2, s["xh2"], s["rs2"], sm["ln2_g"], name=n + "ln2_bwd", job=job)
    if job is not None:
        res, job_out = res
        hooks.done(job, job_out)
    du2, du2h, gs["ln2_g"], gs["ln2_b"] = res
    job = hooks.bwd_job(l, "g_down")
    gw["w_down"] = matmul(s["h1"], du2h, mode="tn", tm=1024, tn=1024, tk=2048, pro=_relu2, outs=two, name=n + "g_down", job=job)
    if job is not None:
        gw["w_down"], job_out = gw["w_down"]
        hooks.done(job, job_out)
    dh1 = matmul(du2h, W["w_down"], mode="nt", tm=1024, tn=1024, tk=1024, outs=((BF16, None),),
                 epi=lambda acc, h: (acc * (2.0 * jnp.maximum(h.astype(F32), 0.0)),), tiles=(s["h1"],), name=n + "d_h1")
    job = hooks.bwd_job(l, "g_up")
    gw["w_up"] = matmul(s["x1"], dh1, mode="tn", tm=1024, tn=1024, tk=2048, outs=two, name=n + "g_up", job=job)
    if job is not None:
        gw["w_up"], job_out = gw["w_up"]
        hooks.done(job, job_out)
    dx1 = matmul(dh1, W["w_up"], mode="nt", tm=1024, tn=1024, tk=2048,
                 epi=lambda acc, d: (acc + ALPHA * d,), tiles=(du2,), name=n + "d_x1")
    du1, du1h, gs["ln1_g"], gs["ln1_b"] = ln_bwd(dx1, s["xh1"], s["rs1"], sm["ln1_g"], name=n + "ln1_bwd")
    gw["w_out"] = matmul(s["merged"], du1h, mode="tn", tm=1024, tn=1024, tk=2048, outs=two, name=n + "g_out")
    dmerged = matmul(du1h, W["w_out"], mode="nt", tm=1024, tn=1024, tk=1024, name=n + "d_merged")
    dr1, dr2, dr3, dg1, dg2, dg3 = merge_bwd(dmerged, *s["r"], s["proj"], name=n + "merge_bwd")
    d_branch = {}
    for nm, a, dr in (("p_ret", s["retg"], dr1), ("p_sb", s["sb"], dr2), ("p_sgu", s["sg"], dr3)):
        gw[nm] = matmul(a, dr, mode="tn", tm=512, tn=1024, tk=2048, outs=two, name=n + "g_" + nm)
        d_branch[nm] = matmul(dr, W[nm], mode="nt", tm=1024, tn=512, tk=1024, name=n + "d_" + nm)
    job = hooks.pair(l, gw)
    dret, gs["ret_gn_g"], gs["ret_gn_b"], job_out = ret_bwd(s["proj"], *rope, rconsts, sm["ret_gn_g"], sm["ret_gn_b"],
                                                             s["raw"], s["states"], d_branch["p_ret"], name=n + "ret_bwd", job=job)
    if job is not None:
        hooks.done(job, job_out)
    job = hooks.scatter(l) if job is not None else None
    dsq, dsk, dsv, job_out = sb_bwd(s["proj"], s["sb"], d_branch["p_sb"], name=n + "sb_bwd", job=job)
    if job is not None:
        hooks.done(job, job_out)
    dsgu, gs["sgu_w"], dbias, gs["sgu_ln_g"], gs["sgu_ln_b"] = sgu_bwd(
        s["proj"], sm["sgu_ln_g"], sm["sgu_ln_b"], sm["sgu_w"], sm["sgu_bias"], d_branch["p_sgu"], name=n + "sgu_bwd")
    gs["sgu_b"] = dbias[:, :, 0]
    dproj = jnp.concatenate([dret, dsq, dsk, dsv, dsgu, dg1, dg2, dg3], axis=1)
    job = hooks.small(l, gs)
    gw["w_in"] = matmul(s["x0"], dproj, mode="tn", tm=1024, tn=1920, tk=1024, outs=two, name=n + "g_in", job=job)
    if job is not None:
        gw["w_in"], job_out = gw["w_in"]
        hooks.done(job, job_out)
    job = hooks.tail(l, gw["w_in"])
    dx0 = matmul(dproj, W["w_in"], mode="nt", tm=1024, tn=1024, tk=2560,
                 epi=lambda acc, d: (acc + ALPHA * d,), tiles=(du1,), name=n + "d_x0", job=job)
    if job is not None:
        dx0, job_out = dx0
        hooks.done(job, job_out)
    return dx0, gw, gs


def local_step(x, target, small, plan):
    T = x.shape[0]
    rope = _rope_tables(T)
    rconsts = _ret_consts()
    sms = []
    for l in range(DEPTH):
        sm = {k: small[k][l][None, :] for k in SMALL if k not in ("sgu_w", "sgu_b")}
        sm["sgu_w"] = small["sgu_w"][l]
        sm["sgu_bias"] = jnp.broadcast_to(small["sgu_b"][l][:, :, None], (4, CHUNK, CHUNK))
        sms.append(sm)
    h, saved = x, []
    for l in range(DEPTH):
        h, s = layer_forward(l, h, plan.weights(l), sms[l], rope, rconsts, plan)
        saved.append(s)
    dy, sq = loss_head(h, target)
    gs = {k: [None] * DEPTH for k in SMALL}
    for l in reversed(range(DEPTH)):
        dy, gwl, gsl = layer_backward(l, dy, saved[l], plan.weights(l), sms[l], rope, rconsts, plan)
        plan.grads(l, gwl)
        for k in SMALL:
            gs[k][l] = gsl[k].reshape(small[k].shape[1:])
    return sq[0, 0], dy, {k: jnp.stack(v) for k, v in gs.items()}


EARLY_GRADS = ("p_ret", "p_sb", "p_sgu", "w_out", "w_up", "w_down")


class _StepPlan:
    def __init__(self, pos, shards16):
        self.pos = pos
        self.shards16 = shards16
        self.full = [dict() for _ in range(DEPTH)]
        self.gw = [None] * DEPTH
        self.bufs = {}
        self.sums = {}
        self.gs = [None] * DEPTH
        first = self._gather([(0, "w_in")])
        self.done(first, run_job(first, name="gather_first"))

    def weights(self, l):
        return self.full[l]

    def grads(self, l, gw):
        self.gw[l] = gw

    def _gather(self, items, chips=(0, 1, 2)):
        job = gather_job([self.shards16[l][k] for l, k in items], [BIG_AXIS[k] for _, k in items], chips)
        job.note = ("gather" if 2 in chips else "gather_part", items)
        return job

    def _pair(self, items):
        job = pair_job([g[1] for _, _, g in items], [BIG_AXIS[k] for _, k, _ in items])
        job.note = ("pair", items)
        return job

    def fwd_job(self, l, host):
        if host == "proj":
            return None
        if host == "sb":
            return self._gather([(l, k) for k in BIG[1:]])
        if l + 1 == DEPTH:
            return None
        return self._gather([(l + 1, "w_in")], (0, 1) if host == "up" else (2,))

    def bwd_job(self, l, host):
        if l + 1 == DEPTH:
            return None
        if host == "ln2":
            job = self._pair([(l + 1, "w_in", self.gw[l + 1]["w_in"])])
            job.note = ("pair_w_in", job.note[1])
            return job
        items, sums16 = self.summed_w_in
        job = scatter_job([l_ for l_, _, _ in items], sums16, [BIG_AXIS[k] for _, k, _ in items],
                          [self.bufs.get(k) for _, k, _ in items], (0, 1) if host == "g_down" else (2,))
        job.note = ("scatter", items)
        return job

    def pair(self, l, ready):
        return self._pair([(l, k, ready[k]) for k in EARLY_GRADS])

    def scatter(self, l):
        items, sums16 = self.summed
        job = scatter_job([l_ for l_, _, _ in items], sums16, [BIG_AXIS[k] for _, k, _ in items],
                          [self.bufs.get(k) for _, k, _ in items])
        job.note = ("scatter", items)
        return job

    def small(self, l, gs):
        self.gs[l] = {k: gs[k].reshape(-1) for k in SMALL}
        if l != 0:
            return None
        job = small_job(_pack_small({k: jnp.stack([self.gs[l_][k] for l_ in range(DEPTH)]) for k in SMALL}))
        job.note = ("small", [])
        return job

    def tail(self, l, g):
        if l != 0:
            return None
        last = self._pair([(0, "w_in", g)])
        self.done(last, run_job(last, name="pair_last"))
        return self.scatter(0)

    def done(self, job, outs):
        kind, items = job.note
        if kind == "small":
            self.small_slots = outs[0]
        if kind in ("pair", "pair_w_in"):
            sums16 = []
            for a, (l, k, g) in enumerate(items):
                self.sums[(l, k)], s16 = pair_sum(self.pos, outs[a], g[0], BIG_AXIS[k], name=f"pair_sum_{k}_{l}")
                sums16.append(s16)
            if kind == "pair":
                self.summed = (items, sums16)
            else:
                self.summed_w_in = (items, sums16)
        for a, item in enumerate(items):
            if kind == "gather_part":
                self.shards16[item[0]][item[1]] = outs[a]
            elif kind == "gather":
                self.full[item[0]][item[1]] = outs[a]
            elif kind == "scatter":
                self.bufs[item[1]] = outs[a]

    def finish(self):
        return self.bufs, self.sums


def _flat2(a):
    return a.reshape(-1, a.shape[-1])


def _pack_small(d, pre=""):
    return jnp.concatenate([d[pre + k].reshape(-1) for k in SMALL]).reshape(-1, 128)


def kernel(x, w_in, ret_gn_g, ret_gn_b, sgu_ln_g, sgu_ln_b, sgu_w, sgu_b, p_ret, p_sb, p_sgu, w_out, ln1_g, ln1_b, w_up, w_down, ln2_g, ln2_b, loss_target, m_w_in, m_ret_gn_g, m_ret_gn_b, m_sgu_ln_g, m_sgu_ln_b, m_sgu_w, m_sgu_b, m_p_ret, m_p_sb, m_p_sgu, m_w_out, m_ln1_g, m_ln1_b, m_w_up, m_w_down, m_ln2_g, m_ln2_b, v_w_in, v_ret_gn_g, v_ret_gn_b, v_sgu_ln_g, v_sgu_ln_b, v_sgu_w, v_sgu_b, v_p_ret, v_p_sb, v_p_sgu, v_w_out, v_ln1_g, v_ln1_b, v_w_up, v_w_down, v_ln2_g, v_ln2_b):
    given = dict(locals())
    order = BIG[:1] + SMALL[:6] + BIG[1:5] + SMALL[6:8] + BIG[5:7] + SMALL[8:10]
    L = DEPTH

    px, py, pc = _place()
    pos = jnp.stack([px, py, pc, 2 * px + py]).astype(jnp.int32)

    shards16 = [{k: cast_into_whole(pos, given[k], l, BIG_AXIS[k], name=f"cast_{k}_{l}") for k in BIG} for l in range(L)]
    plan = _StepPlan(pos, shards16)
    sq, dx, gs = local_step(x[0], loss_target[0], {k: given[k] for k in SMALL}, plan)
    loss = 0.5 * lax.psum(sq, ("x", "y", "c"))

    bufs, sums = plan.finish()
    shards = []
    for k in BIG:
        whole = None
        for l in range(L):
            whole = chip_sum(pos, sums[(l, k)], bufs[k], l, BIG_AXIS[k], whole, name=f"chip_sum_{k}_{l}")
        shards.append(whole)
    joined = run_job(join_job(shards), name="join_halves")
    out = {}
    for a, k in enumerate(BIG):
        shp = given[k].shape
        res = _rows_call(lambda g_, w_, m_, v_: (g_,) + _adamw(w_, g_, m_, v_),
                         [joined[a].reshape(-1, shp[-1]), _flat2(given[k]), _flat2(given["m_" + k]), _flat2(given["v_" + k])],
                         [F32] * 4, name="adamw_" + k)
        out[k] = [r.reshape(shp) for r in res]

    pack = _pack_small
    res = _rows_call(lambda g_, w_, m_, v_: (g_,) + _adamw(w_, g_, m_, v_),
                     [small_sum(plan.small_slots), pack(given), pack(given, "m_"), pack(given, "v_")], [F32] * 4,
                     name="adamw_small", tr=8 * 47)
    off = 0
    for k in SMALL:
        sz = given[k].size
        out[k] = [r.reshape(-1)[off:off + sz].reshape(given[k].shape) for r in res]
        off += sz

    grads = [out[k][0] for k in order]
    deltas = [out[k][1] for k in order]
    new_m = [out[k][2] for k in order]
    new_v = [out[k][3] for k in order]
    return (loss, dx[None], *grads, *deltas, *new_m, *new_v)
```

```python
import functools
import math

import jax
import jax.numpy as jnp
from jax import lax
from jax.experimental import pallas as pl
from jax.experimental.pallas import tpu as pltpu

F32 = jnp.float32
BF16 = jnp.bfloat16

D_MODEL = 1024
SEQ = 4096
DEPTH = 2
CHUNK = 128
RET_HEADS = 4
BRANCH_W = 512
N_IN = 7680
D_FF = 4096
LN_EPS = 1e-5
ROPE_BASE = 10000.0
ALPHA = (2 * DEPTH) ** 0.25
RET_SCALE = 128 ** -0.5
SB_SCALE = 64 ** -0.5
C_RET, C_SB, C_SGU, C_GATE = 0, 2048, 3584, 4608

ADAM_LR, ADAM_B1, ADAM_B2, ADAM_EPS, ADAM_WD, ADAM_STEP = 0.001, 0.9, 0.999, 1e-08, 0.01, 10

N_CHIPS = 4
VMEM_LIMIT = 56 * 1024 * 1024
MESH = pl.DeviceIdType.MESH

NN = ((1,), (0,))
NT = ((1,), (1,))
TN = ((0,), (0,))


def _dot(a, b, dims):
    return lax.dot_general(a, b, (dims, ((), ())), preferred_element_type=F32)


def _params(sem):
    return pltpu.CompilerParams(dimension_semantics=sem, vmem_limit_bytes=VMEM_LIMIT)


def _relu2(h):
    r = jnp.maximum(h.astype(F32), 0.0)
    return r * r


def matmul(a, b, *, mode, tm, tn, tk, outs=((F32, None),), pro=None, epi=None, tiles=(), rows=(), name, job=None):
    if mode == "nn":
        (M, K), N = a.shape, b.shape[1]
    elif mode == "nt":
        (M, K), N = a.shape, b.shape[0]
    else:
        (K, M), N = a.shape, b.shape[1]
    tm, tn, tk = min(tm, M), min(tn, N), min(tk, K)
    assert M % tm == 0 and N % tn == 0 and K % tk == 0, (name, M, N, K, tm, tn, tk)
    if mode == "nn":
        a_spec = pl.BlockSpec((tm, tk), lambda i, j, k: (i, k))
        b_spec = pl.BlockSpec((tk, tn), lambda i, j, k: (k, j))
        dims = NN
    elif mode == "nt":
        a_spec = pl.BlockSpec((tm, tk), lambda i, j, k: (i, k))
        b_spec = pl.BlockSpec((tn, tk), lambda i, j, k: (j, k))
        dims = NT
    else:
        a_spec = pl.BlockSpec((tk, tm), lambda i, j, k: (k, i))
        b_spec = pl.BlockSpec((tk, tn), lambda i, j, k: (k, j))
        dims = TN
    nk = K // tk
    nt_, nr, no = len(tiles), len(rows), len(outs)

    def body(a_ref, b_ref, *rest):
        tile_refs = rest[:nt_]
        row_refs = rest[nt_:nt_ + nr]
        out_refs = rest[nt_ + nr:nt_ + nr + no]
        av = a_ref[...]
        if pro is not None:
            av = pro(av)
        p = _dot(av.astype(BF16), b_ref[...].astype(BF16), dims)

        def finish(acc):
            vals = (acc,) * no if epi is None else epi(acc, *[r[...] for r in tile_refs], *[r[...] for r in row_refs])
            for o_ref, v in zip(out_refs, vals):
                o_ref[...] = v.astype(o_ref.dtype)

        if nk == 1:
            finish(p)
        else:
            acc_ref = rest[-1]
            k = pl.program_id(2)

            @pl.when(k == 0)
            def _():
                acc_ref[...] = p

            @pl.when(k > 0)
            def _():
                acc_ref[...] += p

            @pl.when(k == nk - 1)
            def _():
                finish(acc_ref[...])

    out_shape, out_specs = [], []
    for dt, width in outs:
        if width is None:
            out_shape.append(jax.ShapeDtypeStruct((M, N), dt))
            out_specs.append(pl.BlockSpec((tm, tn), lambda i, j, k: (i, j)))
        else:
            assert N == tn
            out_shape.append(jax.ShapeDtypeStruct((M, width), dt))
            out_specs.append(pl.BlockSpec((tm, width), lambda i, j, k: (i, 0)))
    in_specs = [a_spec, b_spec]
    in_specs += [pl.BlockSpec((tm, tn), lambda i, j, k: (i, j)) for _ in tiles]
    in_specs += [pl.BlockSpec((1, tn), lambda i, j, k: (0, j)) for _ in rows]
    grid = (M // tm, N // tn, nk)
    scratch = [pltpu.VMEM((tm, tn), F32)] if nk > 1 else []
    j = _job_args(job, len(in_specs), no)
    res = pl.pallas_call(
        _hosting(body, job, len(in_specs), no, len(scratch), grid), name=name, grid=grid,
        in_specs=in_specs + j["in_specs"], out_specs=out_specs + j["out_specs"], out_shape=out_shape + j["out_shape"],
        scratch_shapes=scratch + j["scratch"], input_output_aliases=j["aliases"],
        compiler_params=_params(("parallel", "parallel", "arbitrary") if job is None else ("arbitrary",) * 3),
    )(a, b, *tiles, *rows, *j["ins"])
    mine = res[0] if no == 1 else list(res[:no])
    return mine if job is None else (mine, list(res[no:]))


def _ln_epi(acc, res, g, b):
    u = ALPHA * res + acc
    mu = jnp.mean(u, axis=-1, keepdims=True)
    xc = u - mu
    var = jnp.mean(xc * xc, axis=-1, keepdims=True)
    rstd = lax.rsqrt(var + LN_EPS)
    xhat = xc * rstd
    return xhat * g + b, xhat, jnp.broadcast_to(rstd, (u.shape[0], 128))


def matmul_ln(a, w, res, g, b, *, pro=None, tk, name, job=None):
    n = w.shape[1]
    return matmul(a, w, mode="nn", tm=1024, tn=n, tk=tk, pro=pro, epi=_ln_epi, tiles=(res,), rows=(g, b),
                  outs=((F32, None), (F32, None), (F32, 128)), name=name, job=job)


def ln_bwd(dy, xhat, rstd, g, *, name, job=None):
    T, D = dy.shape
    tm = min(512, T)

    def body(dy_ref, xh_ref, rs_ref, g_ref, du_ref, du16_ref, dg_ref, db_ref):
        dyv, xh = dy_ref[...], xh_ref[...]
        r = rs_ref[:, 0:1]
        dxh = dyv * g_ref[...]
        m1 = jnp.mean(dxh, axis=-1, keepdims=True)
        m2 = jnp.mean(dxh * xh, axis=-1, keepdims=True)
        du = r * (dxh - m1 - xh * m2)
        du_ref[...] = du
        du16_ref[...] = du.astype(BF16)

        @pl.when(pl.program_id(0) == 0)
        def _():
            dg_ref[...] = jnp.zeros_like(dg_ref)
            db_ref[...] = jnp.zeros_like(db_ref)

        dg_ref[...] += jnp.sum(dyv * xh, axis=0, keepdims=True)
        db_ref[...] += jnp.sum(dyv, axis=0, keepdims=True)

    row = pl.BlockSpec((tm, D), lambda i: (i, 0))
    vec = pl.BlockSpec((1, D), lambda i: (0, 0))
    j = _job_args(job, 4, 4)
    res = pl.pallas_call(
        _hosting(body, job, 4, 4, 0, T // tm), name=name, grid=(T // tm,),
        in_specs=[row, row, pl.BlockSpec((tm, 128), lambda i: (i, 0)), vec] + j["in_specs"],
        out_specs=[row, row, vec, vec] + j["out_specs"],
        out_shape=[jax.ShapeDtypeStruct((T, D), F32), jax.ShapeDtypeStruct((T, D), BF16),
                   jax.ShapeDtypeStruct((1, D), F32), jax.ShapeDtypeStruct((1, D), F32)] + j["out_shape"],
        scratch_shapes=j["scratch"], input_output_aliases=j["aliases"],
        compiler_params=_params(("arbitrary",)),
    )(dy, xhat, rstd, g, *j["ins"])
    return list(res[:4]) if job is None else (list(res[:4]), list(res[4:]))


def loss_head(y, target):
    T, D = y.shape
    tm = min(512, T)

    def body(y_ref, t_ref, dy_ref, s_ref):
        e = y_ref[...] - t_ref[...]
        dy_ref[...] = e * (1.0 / D)

        @pl.when(pl.program_id(0) == 0)
        def _():
            s_ref[...] = jnp.zeros_like(s_ref)

        s_ref[...] += jnp.sum(jnp.mean(e * e, axis=-1, keepdims=True))

    row = pl.BlockSpec((tm, D), lambda i: (i, 0))
    return pl.pallas_call(
        body, name="loss_head", grid=(T // tm,),
        in_specs=[row, row], out_specs=[row, pl.BlockSpec((8, 128), lambda i: (0, 0))],
        out_shape=[jax.ShapeDtypeStruct((T, D), F32), jax.ShapeDtypeStruct((8, 128), F32)],
        compiler_params=_params(("arbitrary",)),
    )(y, target)


def _rope_tables(T):
    half = 64
    inv_freq = ROPE_BASE ** (-jnp.arange(half, dtype=F32) / half)
    ang = jnp.arange(T, dtype=jnp.int32).astype(F32)[:, None] * inv_freq[None, :]
    cos, sin = jnp.cos(ang), jnp.sin(ang)
    return jnp.concatenate([cos, cos], axis=1), jnp.concatenate([-sin, sin], axis=1)


def _ret_consts():
    H = RET_HEADS
    log_g = jnp.log(1.0 - 2.0 ** (-5.0 - jnp.arange(H, dtype=F32)))
    idx = jnp.arange(CHUNK, dtype=F32)
    diff = idx[:, None] - idx[None, :]
    dmat = jnp.where(diff[None] >= 0, jnp.exp(log_g[:, None, None] * diff[None]), 0.0)
    kd = jnp.exp(log_g[:, None] * (CHUNK - 1 - idx)[None, :])
    qd = jnp.exp(log_g[:, None] * (idx + 1.0)[None, :])
    cd = jnp.exp(log_g * CHUNK)
    full = (H, CHUNK, CHUNK)
    return (dmat.astype(F32), jnp.broadcast_to(kd[:, :, None], full), jnp.broadcast_to(qd[:, :, None], full),
            jnp.broadcast_to(cd[:, None, None], full))


def _swap_halves(v):
    return pltpu.roll(v, 64, 1)


def _group_norm(o):
    mu = jnp.mean(o, axis=-1, keepdims=True)
    xc = o - mu
    var = jnp.mean(xc * xc, axis=-1, keepdims=True)
    rstd = lax.rsqrt(var + LN_EPS)
    return xc * rstd, rstd


def ret_fwd(proj, cosf, sinf, consts, gn_g, gn_b, *, name):
    T = proj.shape[0]
    tb = min(512, T)
    nch = tb // CHUNK
    H = RET_HEADS

    def body(p_ref, cos_ref, sin_ref, dm_ref, kd_ref, qd_ref, cd_ref, g_ref, b_ref, out_ref, raw_ref, st_ref, s_ref):
        @pl.when(pl.program_id(0) == 0)
        def _():
            s_ref[...] = jnp.zeros_like(s_ref)

        for c in range(nch):
            r = slice(c * CHUNK, (c + 1) * CHUNK)
            cs, sn = cos_ref[r, :], sin_ref[r, :]
            for h in range(H):
                hc = slice(h * 128, (h + 1) * 128)
                q = p_ref[r, h * 128:(h + 1) * 128]
                k = p_ref[r, 512 + h * 128:512 + (h + 1) * 128]
                v = p_ref[r, 1024 + h * 128:1024 + (h + 1) * 128]
                gt = p_ref[r, 1536 + h * 128:1536 + (h + 1) * 128]
                qr = q * cs + _swap_halves(q) * sn
                kr = (k * cs + _swap_halves(k) * sn) * RET_SCALE
                sprev = s_ref[h]
                st_ref[c, h] = sprev
                qb, kb, vb = qr.astype(BF16), kr.astype(BF16), v.astype(BF16)
                s = _dot(qb, kb, NT) * dm_ref[h]
                o = _dot(s.astype(BF16), vb, NN) + _dot((qr * qd_ref[h]).astype(BF16), sprev.astype(BF16), NN)
                s_ref[h] = sprev * cd_ref[h] + _dot((kr * kd_ref[h]).astype(BF16), vb, TN)
                raw_ref[r, hc] = o
                y, _ = _group_norm(o)
                out_ref[r, hc] = (gt * jax.nn.sigmoid(gt)) * (y * g_ref[:, hc] + b_ref[:, hc])

    cmat = pl.BlockSpec((H, CHUNK, CHUNK), lambda i: (0, 0, 0))
    vec = pl.BlockSpec((1, BRANCH_W), lambda i: (0, 0))
    rope = pl.BlockSpec((tb, 128), lambda i: (i, 0))
    blk = pl.BlockSpec((tb, BRANCH_W), lambda i: (i, 0))
    return pl.pallas_call(
        body, name=name, grid=(T // tb,),
        in_specs=[pl.BlockSpec((tb, 2048), lambda i: (i, 0)), rope, rope, cmat, cmat, cmat, cmat, vec, vec],
        out_specs=[blk, blk, pl.BlockSpec((nch, H, CHUNK, CHUNK), lambda i: (i, 0, 0, 0))],
        out_shape=[jax.ShapeDtypeStruct((T, BRANCH_W), F32), jax.ShapeDtypeStruct((T, BRANCH_W), F32),
                   jax.ShapeDtypeStruct((T // CHUNK, H, CHUNK, CHUNK), F32)],
        scratch_shapes=[pltpu.VMEM((H, CHUNK, CHUNK), F32)],
        compiler_params=_params(("arbitrary",)),
    )(proj, cosf, sinf, *consts, gn_g, gn_b)


def ret_bwd(proj, cosf, sinf, consts, gn_g, gn_b, raw, states, dout, *, name, job=None):
    T = proj.shape[0]
    tb = min(512, T)
    nch = tb // CHUNK
    nb = T // tb
    H = RET_HEADS

    def body(p_ref, cos_ref, sin_ref, dm_ref, kd_ref, qd_ref, cd_ref, g_ref, b_ref, raw_ref, st_ref, do_ref,
             dp_ref, dg_ref, db_ref, ds_ref):
        @pl.when(pl.program_id(0) == 0)
        def _():
            ds_ref[...] = jnp.zeros_like(ds_ref)
            dg_ref[...] = jnp.zeros_like(dg_ref)
            db_ref[...] = jnp.zeros_like(db_ref)

        for c in reversed(range(nch)):
            r = slice(c * CHUNK, (c + 1) * CHUNK)
            cs, sn = cos_ref[r, :], sin_ref[r, :]
            for h in range(H):
                hc = slice(h * 128, (h + 1) * 128)
                q = p_ref[r, h * 128:(h + 1) * 128]
                k = p_ref[r, 512 + h * 128:512 + (h + 1) * 128]
                v = p_ref[r, 1024 + h * 128:1024 + (h + 1) * 128]
                gt = p_ref[r, 1536 + h * 128:1536 + (h + 1) * 128]
                qr = q * cs + _swap_halves(q) * sn
                kr = (k * cs + _swap_halves(k) * sn) * RET_SCALE
                sprev = st_ref[c, h]
                gv = g_ref[:, hc]
                y, rstd = _group_norm(raw_ref[r, hc])
                d_out = do_ref[r, hc]
                sg = jax.nn.sigmoid(gt)
                d_gate = d_out * (y * gv + b_ref[:, hc]) * (sg * (1.0 + gt * (1.0 - sg)))
                d_aff = d_out * (gt * sg)
                dg_ref[:, hc] += jnp.sum(d_aff * y, axis=0, keepdims=True)
                db_ref[:, hc] += jnp.sum(d_aff, axis=0, keepdims=True)
                dxh = d_aff * gv
                m1 = jnp.mean(dxh, axis=-1, keepdims=True)
                m2 = jnp.mean(dxh * y, axis=-1, keepdims=True)
                d_o = (rstd * (dxh - m1 - y * m2)).astype(BF16)
                qb, kb, vb = qr.astype(BF16), kr.astype(BF16), v.astype(BF16)
                dm, kd, qd = dm_ref[h], kd_ref[h], qd_ref[h]
                p = (_dot(qb, kb, NT) * dm).astype(BF16)
                dp = (_dot(d_o, vb, NT) * dm).astype(BF16)
                dsn = ds_ref[h]
                dsb = dsn.astype(BF16)
                dq_r = _dot(dp, kb, NN) + _dot(d_o, sprev.astype(BF16), NT) * qd
                dk_r = (_dot(dp, qb, TN) + _dot(vb, dsb, NT) * kd) * RET_SCALE
                d_v = _dot(p, d_o, TN) + _dot((kr * kd).astype(BF16), dsb, NN)
                ds_ref[h] = dsn * cd_ref[h] + _dot((qr * qd).astype(BF16), d_o, TN)
                dp_ref[r, h * 128:(h + 1) * 128] = (dq_r * cs - _swap_halves(dq_r) * sn).astype(BF16)
                dp_ref[r, 512 + h * 128:512 + (h + 1) * 128] = (dk_r * cs - _swap_halves(dk_r) * sn).astype(BF16)
                dp_ref[r, 1024 + h * 128:1024 + (h + 1) * 128] = d_v.astype(BF16)
                dp_ref[r, 1536 + h * 128:1536 + (h + 1) * 128] = d_gate.astype(BF16)

    cmat = pl.BlockSpec((H, CHUNK, CHUNK), lambda i: (0, 0, 0))
    vec = pl.BlockSpec((1, BRANCH_W), lambda i: (0, 0))
    rope = pl.BlockSpec((tb, 128), lambda i: (nb - 1 - i, 0))
    blk = pl.BlockSpec((tb, BRANCH_W), lambda i: (nb - 1 - i, 0))
    wide = pl.BlockSpec((tb, 2048), lambda i: (nb - 1 - i, 0))
    j = _job_args(job, 12, 3)
    res = pl.pallas_call(
        _hosting(body, job, 12, 3, 1, nb), name=name, grid=(nb,),
        in_specs=[wide, rope, rope, cmat, cmat, cmat, cmat, vec, vec, blk,
                  pl.BlockSpec((nch, H, CHUNK, CHUNK), lambda i: (nb - 1 - i, 0, 0, 0)), blk] + j["in_specs"],
        out_specs=[wide, vec, vec] + j["out_specs"],
        out_shape=[jax.ShapeDtypeStruct((T, 2048), BF16), jax.ShapeDtypeStruct((1, BRANCH_W), F32),
                   jax.ShapeDtypeStruct((1, BRANCH_W), F32)] + j["out_shape"],
        scratch_shapes=[pltpu.VMEM((H, CHUNK, CHUNK), F32)] + j["scratch"], input_output_aliases=j["aliases"],
        compiler_params=_params(("arbitrary",)),
    )(proj, cosf, sinf, *consts, gn_g, gn_b, raw, states, dout, *j["ins"])
    return res[0], res[1], res[2], list(res[3:])


def _sb_masks():
    row = lax.broadcasted_iota(jnp.int32, (CHUNK, CHUNK), 0)
    lane = lax.broadcasted_iota(jnp.int32, (CHUNK, CHUNK), 1)
    return row, lane


SB_QT = 256
SB_DEAD = -105.0


def _pair(v):
    hi = v.astype(BF16)
    return jnp.concatenate([hi, (v - hi.astype(F32)).astype(BF16)], axis=1)


def _sb_consts():
    r = lax.broadcasted_iota(jnp.int32, (256, 256), 0) & 127
    c = lax.broadcasted_iota(jnp.int32, (256, 256), 1)
    ones = c >= 128
    lane = lax.broadcasted_iota(jnp.int32, (CHUNK, CHUNK), 1)
    return (ones | (r > c)).astype(BF16), (ones | (r >= c)).astype(BF16), (lane < 64, lane >= 64)


def _per_head(x, hms):
    return jnp.concatenate([jnp.where(hm, x, 0.0) for hm in hms], axis=0).astype(BF16)


def _sb_logits(qb, kb2, mask2):
    z = _dot(qb, kb2, NT)
    l1p = jnp.log(1.0 + jnp.exp(-jnp.abs(z)))
    lsp = jnp.minimum(z, 0.0) - l1p
    lsn = lsp - z
    if mask2 is not None:
        lsn = jnp.where(mask2, lsn, 0.0)
    return lsp, lsn


def _sb_tile_mask(qt):
    trow = lax.broadcasted_iota(jnp.int32, (qt, 256), 0)
    tlane = lax.broadcasted_iota(jnp.int32, (qt, 256), 1) & 127
    return lambda m: (tlane + m * CHUNK) < trow


def sb_fwd(proj, *, name, job=None):
    T = proj.shape[0]
    qt = min(SB_QT, T)
    nsub = qt // CHUNK
    cb = C_SB // 128

    def body(q_ref, k_ref, v_ref, o_ref):
        u_gt, _, hms = _sb_consts()
        tile_mask = _sb_tile_mask(qt)

        def qtile(i, _):
            rq = pl.ds(pl.multiple_of(i * qt, qt), qt)
            qb = (q_ref[rq, :] * SB_SCALE).astype(BF16)

            def group(js, masks, state):
                carry, acc = list(state[:2]), state[2]
                rows = [pl.ds(pl.multiple_of(j * CHUNK, CHUNK), CHUNK) for j in js]
                logits = [_sb_logits(qb, _per_head(k_ref[rk, :], hms), m) for rk, m in zip(rows, masks)]
                sums = [[_dot(_pair(lsn[:, h * 128:(h + 1) * 128]), u_gt, NN) for h in range(2)] for _, lsn in logits]
                weights = []
                for (lsp, _), r, m in zip(logits, sums, masks):
                    a_b = []
                    for h in range(2):
                        hc = slice(h * 128, (h + 1) * 128)
                        a = jnp.exp(lsp[:, hc] + r[h][:, :128] + carry[h])
                        if m is not None:
                            a = jnp.where(m[:, hc], a, 0.0)
                        carry[h] = carry[h] + r[h][:, 128:]
                        a_b.append(a.astype(BF16))
                    weights.append(jnp.concatenate(a_b, axis=1))
                for rk, a in zip(rows, weights):
                    acc = acc + _dot(a, _per_head(v_ref[rk, :], hms), NN)
                return carry[0], carry[1], acc

            zero = jnp.zeros((qt, 128), F32)
            diag = list(reversed(range(nsub)))
            state = group([i * nsub + m for m in diag], [tile_mask(m) for m in diag], (zero, zero, zero))

            def live(c):
                return jnp.logical_and(c[0] < i, jnp.maximum(jnp.max(c[1][0]), jnp.max(c[1][1])) > SB_DEAD)

            def blocks(c):
                jj, st = c
                return jj + 1, group([(i - jj) * nsub - 1 - u for u in range(nsub)], [None] * nsub, st)

            _, state = lax.while_loop(live, blocks, (jnp.int32(0), state))
            o_ref[rq, :] = state[2]
            return 0

        lax.fori_loop(0, T // qt, qtile, 0)

    def col(off):
        return pl.BlockSpec((T, 128), lambda hp: (0, off + hp))

    steps = BRANCH_W // 128
    j = _job_args(job, 3, 1)
    res = pl.pallas_call(
        _hosting(body, job, 3, 1, 0, steps), name=name, grid=(steps,),
        in_specs=[col(cb), col(cb + 4), col(cb + 8)] + j["in_specs"], out_specs=[col(0)] + j["out_specs"],
        out_shape=[jax.ShapeDtypeStruct((T, BRANCH_W), F32)] + j["out_shape"],
        scratch_shapes=j["scratch"], input_output_aliases=j["aliases"],
        compiler_params=_params(("parallel",) if job is None else ("arbitrary",)),
    )(proj, proj, proj, *j["ins"])
    return res[0], list(res[1:])


def sb_bwd(proj, out, dout, *, name, job=None):
    T = proj.shape[0]
    qt = min(SB_QT, T)
    nsub = qt // CHUNK
    cb = C_SB // 128

    def body(q_ref, k_ref, v_ref, o_ref, do_ref, dq_ref, dk_ref, dv_ref, dkt_ref, dvt_ref):
        u_gt, u_ge, hms = _sb_consts()
        tile_mask = _sb_tile_mask(qt)
        tall_lane = lax.broadcasted_iota(jnp.int32, (qt, 128), 1)
        top = lax.broadcasted_iota(jnp.int32, (CHUNK, CHUNK), 0) < 64
        dkt_ref[...] = jnp.zeros_like(dkt_ref)
        dvt_ref[...] = jnp.zeros_like(dvt_ref)

        def qtile(i, _):
            rq = pl.ds(pl.multiple_of(i * qt, qt), qt)
            qs = q_ref[rq, :] * SB_SCALE
            qb, q_t = qs.astype(BF16), qs.T.astype(BF16)
            dov = do_ref[rq, :]
            dob, do_t = dov.astype(BF16), dov.T.astype(BF16)
            prod = dob.astype(F32) * o_ref[rq, :]
            total = [jnp.broadcast_to(jnp.sum(jnp.where(hm, prod, 0.0), axis=1, keepdims=True), (qt, 128))
                     for hm in (tall_lane < 64, tall_lane >= 64)]

            def group(js, masks, state):
                c_l, c_w, dq = list(state[:2]), list(state[2:4]), state[4]
                heads = [slice(h * 128, (h + 1) * 128) for h in range(2)]
                rows = [pl.ds(pl.multiple_of(j * CHUNK, CHUNK), CHUNK) for j in js]
                kb2 = [_per_head(k_ref[rk, :], hms) for rk in rows]
                logits = [_sb_logits(qb, kb, m) for kb, m in zip(kb2, masks)]
                da = [_dot(dob, _per_head(v_ref[rk, :], hms), NT) for rk in rows]
                sums = [[_dot(_pair(lsn[:, hc]), u_gt, NN) for hc in heads] for _, lsn in logits]
                a_b, w_all = [], []
                for (lsp, _), r, d, m in zip(logits, sums, da, masks):
                    a_h, w_h = [], []
                    for h, hc in enumerate(heads):
                        a = jnp.exp(lsp[:, hc] + r[h][:, :128] + c_l[h])
                        if m is not None:
                            a = jnp.where(m[:, hc], a, 0.0)
                        c_l[h] = c_l[h] + r[h][:, 128:]
                        a = a.astype(BF16)
                        a_h.append(a)
                        w_h.append(a.astype(F32) * d[:, hc])
                    a_b.append(jnp.concatenate(a_h, axis=1))
                    w_all.append(w_h)
                sums_w = [[_dot(_pair(w), u_ge, NN) for w in w_h] for w_h in w_all]
                dz_b = []
                for (lsp, _), w_h, r, m in zip(logits, w_all, sums_w, masks):
                    sp = jnp.exp(lsp)
                    dz_h = []
                    for h, hc in enumerate(heads):
                        later_w = r[h][:, :128] + c_w[h]
                        c_w[h] = c_w[h] + r[h][:, 128:]
                        dz = w_h[h] * (1.0 - sp[:, hc]) - sp[:, hc] * (total[h] - later_w)
                        if m is not None:
                            dz = jnp.where(m[:, hc], dz, 0.0)
                        dz_h.append(dz.astype(BF16))
                    dz_b.append(jnp.concatenate(dz_h, axis=1))
                for j, kb, a, dz in zip(js, kb2, a_b, dz_b):
                    dkt = _dot(q_t, dz, NN)
                    dvt = _dot(do_t, a, NN)
                    dkt_ref[j] += jnp.where(top, dkt[:, :128], dkt[:, 128:])
                    dvt_ref[j] += jnp.where(top, dvt[:, :128], dvt[:, 128:])
                    dq = dq + _dot(dz, kb, NN)
                return c_l[0], c_l[1], c_w[0], c_w[1], dq

            zero = jnp.zeros((qt, 128), F32)
            diag = list(reversed(range(nsub)))
            state = group([i * nsub + m for m in diag], [tile_mask(m) for m in diag], (zero,) * 5)

            def live(c):
                return jnp.logical_and(c[0] < i, jnp.maximum(jnp.max(c[1][0]), jnp.max(c[1][1])) > SB_DEAD)

            def blocks(c):
                jj, st = c
                return jj + 1, group([(i - jj) * nsub - 1 - u for u in range(nsub)], [None] * nsub, st)

            _, state = lax.while_loop(live, blocks, (jnp.int32(0), state))
            dq_ref[rq, :] = (state[4] * SB_SCALE).astype(BF16)
            return 0

        lax.fori_loop(0, T // qt, qtile, 0)

        def untranspose(jb, _):
            rk = pl.ds(pl.multiple_of(jb * CHUNK, CHUNK), CHUNK)
            dk_ref[rk, :] = dkt_ref[jb].T.astype(BF16)
            dv_ref[rk, :] = dvt_ref[jb].T.astype(BF16)
            return 0

        lax.fori_loop(0, T // CHUNK, untranspose, 0)

    def col(off):
        return pl.BlockSpec((T, 128), lambda hp: (0, off + hp))

    o16 = jax.ShapeDtypeStruct((T, BRANCH_W), BF16)
    steps = BRANCH_W // 128
    j = _job_args(job, 5, 3)
    acc = pltpu.VMEM((T // CHUNK, CHUNK, CHUNK), F32)
    res = pl.pallas_call(
        _hosting(body, job, 5, 3, 2, steps), name=name, grid=(steps,),
        in_specs=[col(cb), col(cb + 4), col(cb + 8), col(0), col(0)] + j["in_specs"],
        out_specs=[col(0), col(0), col(0)] + j["out_specs"], out_shape=[o16, o16, o16] + j["out_shape"],
        scratch_shapes=[acc, acc] + j["scratch"], input_output_aliases=j["aliases"],
        compiler_params=_params(("parallel",) if job is None else ("arbitrary",)),
    )(proj, proj, proj, out, dout, *j["ins"])
    return res[0], res[1], res[2], list(res[3:])


_G0 = math.sqrt(2.0 / math.pi)
_G1 = 0.044715


def _gelu(x):
    return 0.5 * x * (1.0 + jnp.tanh(_G0 * (x + _G1 * x * x * x)))


def _gelu_grad(x):
    t = jnp.tanh(_G0 * (x + _G1 * x * x * x))
    return 0.5 * (1.0 + t) + 0.5 * x * (1.0 - t * t) * (_G0 * (1.0 + 3.0 * _G1 * x * x))


def _tril():
    row, lane = _sb_masks()
    return row >= lane


def sgu_fwd(proj, ln_g, ln_b, w, bias, *, name):
    T = proj.shape[0]
    tb = min(512, T)
    G = BRANCH_W // 128

    def body(u_ref, v_ref, g_ref, b_ref, w_ref, bias_ref, o_ref):
        vv = _gelu(v_ref[...])
        xh, _ = _group_norm(vv)
        vn = (xh * g_ref[...] + b_ref[...]).astype(BF16)
        tril = _tril()
        for g in range(G):
            wg = jnp.where(tril, w_ref[g], 0.0).astype(BF16)
            gc = slice(g * 128, (g + 1) * 128)
            for c in range(tb // CHUNK):
                r = slice(c * CHUNK, (c + 1) * CHUNK)
                sv = _dot(wg, vn[r, gc], NN) + bias_ref[g]
                o_ref[r, gc] = _gelu(u_ref[r, gc]) * sv

    cu, cv = C_SGU // BRANCH_W, C_SGU // BRANCH_W + 1
    vec = pl.BlockSpec((1, BRANCH_W), lambda i: (0, 0))
    mat = pl.BlockSpec((G, CHUNK, CHUNK), lambda i: (0, 0, 0))
    return pl.pallas_call(
        body, name=name, grid=(T // tb,),
        in_specs=[pl.BlockSpec((tb, BRANCH_W), lambda i: (i, cu)), pl.BlockSpec((tb, BRANCH_W), lambda i: (i, cv)),
                  vec, vec, mat, mat],
        out_specs=pl.BlockSpec((tb, BRANCH_W), lambda i: (i, 0)),
        out_shape=jax.ShapeDtypeStruct((T, BRANCH_W), F32),
        compiler_params=_params(("parallel",)),
    )(proj, proj, ln_g, ln_b, w, bias)


def sgu_bwd(proj, ln_g, ln_b, w, bias, dout, *, name):
    T = proj.shape[0]
    tb = min(512, T)
    G = BRANCH_W // 128

    def body(u_ref, v_ref, g_ref, b_ref, w_ref, bias_ref, do_ref, dp_ref, dw_ref, dbias_ref, dg_ref, db_ref, dvn_ref):
        @pl.when(pl.program_id(0) == 0)
        def _():
            dw_ref[...] = jnp.zeros_like(dw_ref)
            dbias_ref[...] = jnp.zeros_like(dbias_ref)
            dg_ref[...] = jnp.zeros_like(dg_ref)
            db_ref[...] = jnp.zeros_like(db_ref)

        gv = v_ref[...]
        vv = _gelu(gv)
        xh, rstd = _group_norm(vv)
        vn = (xh * g_ref[...] + b_ref[...]).astype(BF16)
        tril = _tril()
        for g in range(G):
            wg = jnp.where(tril, w_ref[g], 0.0).astype(BF16)
            gc = slice(g * 128, (g + 1) * 128)
            for c in range(tb // CHUNK):
                r = slice(c * CHUNK, (c + 1) * CHUNK)
                vn_c = vn[r, gc]
                sv = _dot(wg, vn_c, NN) + bias_ref[g]
                gu = u_ref[r, gc]
                d_o = do_ref[r, gc]
                dp_ref[r, gc] = (d_o * sv * _gelu_grad(gu)).astype(BF16)
                dsv = d_o * _gelu(gu)
                dsv_b = dsv.astype(BF16)
                dvn_ref[r, gc] = _dot(wg, dsv_b, TN)
                dw_ref[g] += jnp.where(tril, _dot(dsv_b, vn_c, NT), 0.0)
                dbias_ref[g] += jnp.broadcast_to(jnp.sum(dsv, axis=1, keepdims=True), (CHUNK, CHUNK))
        dvn = dvn_ref[...]
        dg_ref[...] += jnp.sum(dvn * xh, axis=0, keepdims=True)
        db_ref[...] += jnp.sum(dvn, axis=0, keepdims=True)
        dxh = dvn * g_ref[...]
        m1 = jnp.mean(dxh, axis=-1, keepdims=True)
        m2 = jnp.mean(dxh * xh, axis=-1, keepdims=True)
        dp_ref[:, BRANCH_W:2 * BRANCH_W] = (rstd * (dxh - m1 - xh * m2) * _gelu_grad(gv)).astype(BF16)

    cu, cv = C_SGU // BRANCH_W, C_SGU // BRANCH_W + 1
    vec = pl.BlockSpec((1, BRANCH_W), lambda i: (0, 0))
    mat = pl.BlockSpec((G, CHUNK, CHUNK), lambda i: (0, 0, 0))
    blk = pl.BlockSpec((tb, BRANCH_W), lambda i: (i, 0))
    msh = jax.ShapeDtypeStruct((G, CHUNK, CHUNK), F32)
    vsh = jax.ShapeDtypeStruct((1, BRANCH_W), F32)
    return pl.pallas_call(
        body, name=name, grid=(T // tb,),
        in_specs=[pl.BlockSpec((tb, BRANCH_W), lambda i: (i, cu)), pl.BlockSpec((tb, BRANCH_W), lambda i: (i, cv)),
                  vec, vec, mat, mat, blk],
        out_specs=[pl.BlockSpec((tb, 2 * BRANCH_W), lambda i: (i, 0)), mat, mat, vec, vec],
        out_shape=[jax.ShapeDtypeStruct((T, 2 * BRANCH_W), BF16), msh, msh, vsh, vsh],
        scratch_shapes=[pltpu.VMEM((tb, BRANCH_W), F32)],
        compiler_params=_params(("arbitrary",)),
    )(proj, proj, ln_g, ln_b, w, bias, dout)


def merge_fwd(a1, a2, a3, p1, p2, p3, proj, *, name):
    T = a1.shape[0]
    tm, tn = min(1024, T), 512
    gb = C_GATE // tn

    def body(a1_ref, a2_ref, a3_ref, p1_ref, p2_ref, p3_ref, g1_ref, g2_ref, g3_ref, m_ref, r1_ref, r2_ref, r3_ref):
        m = None
        for a_ref, p_ref, g_ref, r_ref in ((a1_ref, p1_ref, g1_ref, r1_ref), (a2_ref, p2_ref, g2_ref, r2_ref),
                                           (a3_ref, p3_ref, g3_ref, r3_ref)):
            r = _dot(a_ref[...].astype(BF16), p_ref[...], NN)
            r_ref[...] = r.astype(r_ref.dtype)
            t = jax.nn.sigmoid(g_ref[...]) * r
            m = t if m is None else m + t
        m_ref[...] = m.astype(m_ref.dtype)

    a_spec = pl.BlockSpec((tm, BRANCH_W), lambda i, j: (i, 0))
    p_spec = pl.BlockSpec((BRANCH_W, tn), lambda i, j: (0, j))
    o_spec = pl.BlockSpec((tm, tn), lambda i, j: (i, j))
    osh = jax.ShapeDtypeStruct((T, D_MODEL), F32)
    gates = [pl.BlockSpec((tm, tn), functools.partial(lambda i, j, o: (i, o + j), o=gb + 2 * n)) for n in range(3)]
    return pl.pallas_call(
        body, name=name, grid=(T // tm, D_MODEL // tn),
        in_specs=[a_spec, a_spec, a_spec, p_spec, p_spec, p_spec, *gates],
        out_specs=[o_spec] * 4, out_shape=[jax.ShapeDtypeStruct((T, D_MODEL), BF16)] * 4,
        compiler_params=_params(("parallel", "parallel")),
    )(a1, a2, a3, p1, p2, p3, proj, proj, proj)


def merge_bwd(dm, r1, r2, r3, proj, *, name):
    T = dm.shape[0]
    tm, tn = min(512, T), 512
    gb = C_GATE // tn

    def body(dm_ref, r1_ref, r2_ref, r3_ref, g1_ref, g2_ref, g3_ref, dr1_ref, dr2_ref, dr3_ref, dg1_ref, dg2_ref, dg3_ref):
        d = dm_ref[...]
        for r_ref, g_ref, dr_ref, dg_ref in ((r1_ref, g1_ref, dr1_ref, dg1_ref), (r2_ref, g2_ref, dr2_ref, dg2_ref),
                                             (r3_ref, g3_ref, dr3_ref, dg3_ref)):
            s = jax.nn.sigmoid(g_ref[...])
            dr_ref[...] = (d * s).astype(BF16)
            dg_ref[...] = (d * r_ref[...].astype(F32) * (s * (1.0 - s))).astype(BF16)

    o_spec = pl.BlockSpec((tm, tn), lambda i, j: (i, j))
    osh = jax.ShapeDtypeStruct((T, D_MODEL), BF16)
    gates = [pl.BlockSpec((tm, tn), functools.partial(lambda i, j, o: (i, o + j), o=gb + 2 * n)) for n in range(3)]
    return pl.pallas_call(
        body, name=name, grid=(T // tm, D_MODEL // tn),
        in_specs=[o_spec] * 4 + gates, out_specs=[o_spec] * 6, out_shape=[osh] * 6,
        compiler_params=_params(("parallel", "parallel")),
    )(dm, r1, r2, r3, proj, proj, proj)


def _rows_call(fn, ins, out_dtypes, *, name, tr=256):
    first = ins[0][0] if isinstance(ins[0], tuple) else ins[0]
    R, C = first.shape[-2:]
    tr = min(tr, R)
    assert R % tr == 0, (name, R, tr)
    arrs, specs = [], []
    for x in ins:
        if isinstance(x, tuple):
            arrs.append(x[0])
            specs.append(pl.BlockSpec((None, tr, C), functools.partial(lambda i, n: (n, i, 0), n=x[1])))
        else:
            arrs.append(x)
            specs.append(pl.BlockSpec((tr, C), lambda i: (i, 0)))
    ni = len(arrs)

    def body(*refs):
        vals = fn(*[r[...] for r in refs[:ni]])
        for o_ref, v in zip(refs[ni:], vals):
            o_ref[...] = v.astype(o_ref.dtype)

    res = pl.pallas_call(
        body, name=name, grid=(R // tr,), in_specs=specs,
        out_specs=[pl.BlockSpec((tr, C), lambda i: (i, 0)) for _ in out_dtypes],
        out_shape=[jax.ShapeDtypeStruct((R, C), dt) for dt in out_dtypes],
        compiler_params=_params(("parallel",)),
    )(*arrs)
    return res


def _tile_rows(rows, cols):
    t = 256
    while t > 8 and (t * cols > 512 * 1024 or rows % t):
        t //= 2
    return t


def _rows_at(fn, pos, ins, outs, steps, *, name, aliases=None):
    read = [n for n, (_, s) in enumerate(ins) if s is not ANY]
    ni = len(ins)

    def body(pos_ref, *refs):
        vals = fn(*[refs[n][...] for n in read])
        for o_ref, v in zip(refs[ni:], vals):
            o_ref[...] = v.astype(o_ref.dtype)

    return pl.pallas_call(
        body, name=name,
        grid_spec=pltpu.PrefetchScalarGridSpec(num_scalar_prefetch=1, grid=(steps,), in_specs=[s for _, s in ins],
                                               out_specs=[s for _, s in outs]),
        out_shape=[sh for sh, _ in outs],
        input_output_aliases={1 + i: o for i, o in (aliases or {}).items()},
        compiler_params=_params(("parallel",)),
    )(pos, *[a for a, _ in ins])


def cast_into_whole(pos, w, l, axis, *, name):
    _, r, n = w.shape
    tr = _tile_rows(r, n)
    if axis == 1:
        shape, spec = (r, n * N_CHIPS), pl.BlockSpec((tr, n), lambda i, p: (i, p[3]))
    else:
        shape, spec = (r * N_CHIPS, n), pl.BlockSpec((tr, n), lambda i, p: (p[3] * (r // tr) + i, 0))
    return _rows_at(lambda a: (a,), pos, [(w, pl.BlockSpec((None, tr, n), lambda i, p: (l, i, 0)))],
                    [(jax.ShapeDtypeStruct(shape, BF16), spec)], r // tr, name=name)[0]


def pair_sum(pos, theirs, g32, axis, *, name):
    rows2, cols = theirs.shape
    h = rows2 // (N_CHIPS if axis == 0 else 1)
    tr = _tile_rows(h, cols)
    hb = h // tr
    if axis == 1:
        own = pl.BlockSpec((tr, cols), lambda i, p: (p[2] * hb + i, 0))
    else:
        own = pl.BlockSpec((tr, cols), lambda i, p: ((2 * (i // hb) + p[2]) * hb + i % hb, 0))
    row = pl.BlockSpec((tr, cols), lambda i, p: (i, 0))
    return _rows_at(lambda t, m: (m + t.astype(F32),) * 2, pos, [(theirs, row), (g32, own)],
                    [(jax.ShapeDtypeStruct((rows2, cols), F32), row), (jax.ShapeDtypeStruct((rows2, cols), BF16), row)],
                    rows2 // tr, name=name)


def chip_sum(pos, h32, recv, l, axis, whole, *, name):
    _, depth, h, n = recv.shape
    tr = _tile_rows(h, n)
    hb = h // tr
    if axis == 1:
        mine = pl.BlockSpec((tr, n), lambda i, p: (i, p[3]))
    else:
        mine = pl.BlockSpec((tr, n), lambda i, p: (p[3] * hb + i, 0))
    ins = [(h32, mine)] + [(recv, pl.BlockSpec((None, None, tr, n), functools.partial(lambda i, p, j: (j, l, i, 0), j=j)))
                           for j in range(3)]
    if whole is not None:
        ins.append((whole, ANY))
    return _rows_at(lambda o, a, b, c: (((o + a.astype(F32)) + b.astype(F32)) + c.astype(F32),), pos, ins,
                    [(jax.ShapeDtypeStruct((depth, 2, h, n), F32), pl.BlockSpec((None, None, tr, n), lambda i, p: (l, p[2], i, 0)))],
                    hb, name=name, aliases=None if whole is None else {4: 0})[0]


def _adamw(w, g, m, v):
    m2 = ADAM_B1 * m + (1.0 - ADAM_B1) * g
    v2 = ADAM_B2 * v + (1.0 - ADAM_B2) * (g * g)
    m_hat = m2 / (1.0 - ADAM_B1 ** ADAM_STEP)
    v_hat = v2 / (1.0 - ADAM_B2 ** ADAM_STEP)
    delta = -ADAM_LR * (m_hat / (jnp.sqrt(v_hat) + ADAM_EPS) + ADAM_WD * w)
    return delta, m2, v2


def _place():
    return lax.axis_index("x"), lax.axis_index("y"), lax.axis_index("c")


def _chip_peers(x, y, c):
    return [((1 - x, y, c), 2 * (1 - x) + y), ((x, 1 - y, c), 2 * x + 1 - y), ((1 - x, 1 - y, c), 2 * (1 - x) + 1 - y)]


def _shard_of(ref, axis, k, n):
    start = pl.multiple_of(k * n, 128)
    return ref.at[pl.ds(start, n), :] if axis == 0 else ref.at[:, pl.ds(start, n)]


ANY = pl.BlockSpec(memory_space=pl.ANY)


class CopyJob:
    def __init__(self, ins, out_shape, scratch, copies, aliases=None):
        self.ins, self.out_shape, self.scratch, self.copies = list(ins), list(out_shape), list(scratch), copies
        self.aliases = dict(aliases or {})

    def start(self, ins, outs, sems):
        local, remote, _, _ = self.copies(ins, outs, sems)
        for d in local + remote:
            d.start()

    def finish(self, ins, outs, sems):
        local, remote, arrivals, relays = self.copies(ins, outs, sems)
        for needs, sends, _ in relays:
            for d in needs:
                d.wait_recv()
            for d in sends:
                d.start()
        for d in arrivals + [d for _, _, arrives in relays for d in arrives]:
            d.wait_recv()
        for d in remote + [d for _, sends, _ in relays for d in sends]:
            d.wait_send()
        for d in local:
            d.wait()


def run_job(job, *, name):
    ni, no = len(job.ins), len(job.out_shape)

    def body(*refs):
        parts = refs[:ni], refs[ni:ni + no], refs[ni + no:]
        job.start(*parts)
        job.finish(*parts)

    return pl.pallas_call(
        body, name=name, in_specs=[ANY] * ni, out_specs=[ANY] * no, out_shape=job.out_shape,
        scratch_shapes=job.scratch, input_output_aliases=job.aliases,
    )(*job.ins)


def _job_args(job, n_in, n_out):
    if job is None:
        return dict(ins=[], in_specs=[], out_specs=[], out_shape=[], scratch=[], aliases={})
    return dict(ins=job.ins, in_specs=[ANY] * len(job.ins), out_specs=[ANY] * len(job.out_shape),
                out_shape=job.out_shape, scratch=job.scratch,
                aliases={n_in + i: n_out + o for i, o in job.aliases.items()})


def _hosting(body, job, n_in, n_out, n_scratch, grid):
    if job is None:
        return body
    ji, jo = len(job.ins), len(job.out_shape)
    grid = (grid,) if isinstance(grid, int) else tuple(grid)

    def at(ends):
        hit = None
        for ax, e in enumerate(ends):
            here = pl.program_id(ax) == e
            hit = here if hit is None else jnp.logical_and(hit, here)
        return hit

    def hosted(*refs):
        o = n_in + ji
        s = o + n_out + jo
        parts = refs[n_in:o], refs[o + n_out:s], refs[s + n_scratch:]

        @pl.when(at([0] * len(grid)))
        def _():
            job.start(*parts)

        body(*refs[:n_in], *refs[o:o + n_out], *refs[s:s + n_scratch])

        @pl.when(at([g - 1 for g in grid]))
        def _():
            job.finish(*parts)

    return hosted


def _job_sems(n_remote, n_local):
    return [pltpu.SemaphoreType.DMA((n_remote,)), pltpu.SemaphoreType.DMA((n_remote,)), pltpu.SemaphoreType.DMA((n_local,))]


def gather_job(shards, axes, chips=(0, 1, 2)):
    na = len(shards)

    def copies(ins, outs, sems):
        send, recv, _ = sems
        x, y, c = _place()
        k = 2 * x + y
        remote, relays = [], []
        for a in range(na):
            r = outs[a].shape[0] // (N_CHIPS if axes[a] == 0 else 1)
            n = outs[a].shape[axes[a]] // N_CHIPS
            half = r // 2

            def part(kk, cc, a=a, n=n, half=half):
                rows = pl.ds(pl.multiple_of(cc * half + (kk * n if axes[a] == 0 else 0), 8), half)
                return outs[a].at[rows, :] if axes[a] == 0 else outs[a].at[rows, pl.ds(pl.multiple_of(kk * n, 128), n)]

            needs, passes, lands = [], [], []
            for j, (peer, kp) in enumerate(_chip_peers(x, y, c)):
                if j not in chips:
                    continue
                s = 6 * a + j
                remote.append(pltpu.make_async_remote_copy(part(k, c), part(k, c), send.at[s], recv.at[s],
                                                           device_id=peer, device_id_type=MESH))
                needs.append(pltpu.make_async_remote_copy(part(kp, c), part(kp, c), send.at[s], recv.at[s],
                                                          device_id=peer, device_id_type=MESH))
                passes.append(pltpu.make_async_remote_copy(part(kp, c), part(kp, c), send.at[s + 3], recv.at[s + 3],
                                                           device_id=(x, y, 1 - c), device_id_type=MESH))
                lands.append(pltpu.make_async_remote_copy(part(kp, 1 - c), part(kp, 1 - c), send.at[s + 3], recv.at[s + 3],
                                                          device_id=(x, y, 1 - c), device_id_type=MESH))
            relays.append((needs, passes, lands))
        return [], remote, [], relays

    out_shape = [jax.ShapeDtypeStruct(w.shape, BF16) for w in shards]
    return CopyJob(shards, out_shape, _job_sems(6 * na, 1), copies, {a: a for a in range(na)})


def scatter_job(layers, g16, axes, filled, chips=(0, 1, 2)):
    na = len(axes)

    def shard_shape(a):
        r, c = g16[a].shape
        return (r // N_CHIPS, c) if axes[a] == 0 else (r, c // N_CHIPS)

    def copies(ins, outs, sems):
        send, recv_sems, _ = sems
        x, y, c = _place()
        remote = []
        for a in range(na):
            n = shard_shape(a)[axes[a]]
            for r, (peer, kp) in enumerate(_chip_peers(x, y, c)):
                if r not in chips:
                    continue
                remote.append(pltpu.make_async_remote_copy(_shard_of(ins[a], axes[a], kp, n), outs[a].at[r, layers[a]],
                                                           send.at[3 * a + r], recv_sems.at[3 * a + r],
                                                           device_id=peer, device_id_type=MESH))
        return [], remote, remote, []

    out_shape = [jax.ShapeDtypeStruct((3, DEPTH) + shard_shape(a), BF16) for a in range(na)]
    ins = list(g16)
    aliases = {}
    for a in range(na):
        if filled[a] is not None:
            aliases[len(ins)] = a
            ins.append(filled[a])
    return CopyJob(ins, out_shape, _job_sems(3 * na, 1), copies, aliases)


def pair_job(g16, axes):
    na = len(axes)
    pieces = [1 if ax == 1 else N_CHIPS for ax in axes]

    def copies(ins, outs, sems):
        send, recv, _ = sems
        x, y, c = _place()
        remote = []
        s = 0
        for a in range(na):
            rows = g16[a].shape[0] // (2 * pieces[a])
            for kk in range(pieces[a]):
                src = ins[a].at[pl.ds(pl.multiple_of((2 * kk + 1 - c) * rows, 8), rows), :]
                remote.append(pltpu.make_async_remote_copy(src, outs[a].at[pl.ds(kk * rows, rows), :], send.at[s], recv.at[s],
                                                           device_id=(x, y, 1 - c), device_id_type=MESH))
                s += 1
        return [], remote, remote, []

    out_shape = [jax.ShapeDtypeStruct((g.shape[0] // 2, g.shape[1]), BF16) for g in g16]
    return CopyJob(g16, out_shape, _job_sems(sum(pieces), 1), copies)


def join_job(shards):
    na = len(shards)

    def copies(ins, outs, sems):
        send, recv, _ = sems
        x, y, c = _place()
        remote = [pltpu.make_async_remote_copy(outs[a].at[:, c], outs[a].at[:, c], send.at[a], recv.at[a],
                                               device_id=(x, y, 1 - c), device_id_type=MESH) for a in range(na)]
        lands = [pltpu.make_async_remote_copy(outs[a].at[:, 1 - c], outs[a].at[:, 1 - c], send.at[a], recv.at[a],
                                              device_id=(x, y, 1 - c), device_id_type=MESH) for a in range(na)]
        return [], remote, lands, []

    out_shape = [jax.ShapeDtypeStruct(s.shape, F32) for s in shards]
    return CopyJob(shards, out_shape, _job_sems(na, 1), copies, {a: a for a in range(na)})


def small_job(p):
    def copies(ins, outs, sems):
        send, recv, loc = sems
        x, y, c = _place()
        me = 4 * x + 2 * y + c
        remote, lands = [], []
        for rel in range(1, 8):
            dx, dy, dc = rel >> 2, (rel >> 1) & 1, rel & 1
            peer = (1 - x if dx else x, 1 - y if dy else y, 1 - c if dc else c)
            who = 4 * peer[0] + 2 * peer[1] + peer[2]
            remote.append(pltpu.make_async_remote_copy(ins[0], outs[0].at[me], send.at[rel - 1], recv.at[rel - 1],
                                                       device_id=peer, device_id_type=MESH))
            lands.append(pltpu.make_async_remote_copy(ins[0], outs[0].at[who], send.at[rel - 1], recv.at[rel - 1],
                                                      device_id=peer, device_id_type=MESH))
        return [pltpu.make_async_copy(ins[0], outs[0].at[me], loc.at[0])], remote, lands, []

    return CopyJob([p], [jax.ShapeDtypeStruct((8,) + p.shape, F32)], _job_sems(7, 1), copies)


def small_sum(slots):
    def add(*terms):
        acc = terms[0]
        for t in terms[1:]:
            acc = acc + t
        return (acc,)

    return _rows_call(add, [(slots, d) for d in range(8)], [F32], name="small_sum", tr=8 * 47)[0]


BIG = ("w_in", "p_ret", "p_sb", "p_sgu", "w_out", "w_up", "w_down")
BIG_AXIS = {"w_in": 1, "p_ret": 1, "p_sb": 1, "p_sgu": 1, "w_out": 0, "w_up": 1, "w_down": 0}
SMALL = ("ret_gn_g", "ret_gn_b", "sgu_ln_g", "sgu_ln_b", "sgu_w", "sgu_b", "ln1_g", "ln1_b", "ln2_g", "ln2_b")


def layer_forward(l, x0, W, sm, rope, rconsts, hooks):
    n = f"l{l}_"
    job = hooks.fwd_job(l, "proj")
    proj = matmul(x0, W["w_in"], mode="nn", tm=2048, tn=768, tk=1024, name=n + "proj", job=job)
    if job is not None:
        proj, job_out = proj
        hooks.done(job, job_out)
    retg, raw, states = ret_fwd(proj, *rope, rconsts, sm["ret_gn_g"], sm["ret_gn_b"], name=n + "ret_fwd")
    job = hooks.fwd_job(l, "sb")
    sb, job_out = sb_fwd(proj, name=n + "sb_fwd", job=job)
    if job is not None:
        hooks.done(job, job_out)
    sg = sgu_fwd(proj, sm["sgu_ln_g"], sm["sgu_ln_b"], sm["sgu_w"], sm["sgu_bias"], name=n + "sgu_fwd")
    merged, r1, r2, r3 = merge_fwd(retg, sb, sg, W["p_ret"], W["p_sb"], W["p_sgu"], proj, name=n + "merge_fwd")
    x1, xh1, rs1 = matmul_ln(merged, W["w_out"], x0, sm["ln1_g"], sm["ln1_b"], tk=1024, name=n + "out_ln1")
    job = hooks.fwd_job(l, "up")
    h1 = matmul(x1, W["w_up"], mode="nn", tm=1024, tn=1024, tk=1024, outs=((BF16, None),), name=n + "up", job=job)
    if job is not None:
        h1, job_out = h1
        hooks.done(job, job_out)
    job = hooks.fwd_job(l, "down")
    res = matmul_ln(h1, W["w_down"], x1, sm["ln2_g"], sm["ln2_b"], pro=_relu2, tk=1024, name=n + "down_ln2", job=job)
    if job is not None:
        res, job_out = res
        hooks.done(job, job_out)
    x2, xh2, rs2 = res
    saved = dict(x0=x0, proj=proj, retg=retg, raw=raw, states=states, sb=sb, sg=sg, merged=merged, r=(r1, r2, r3),
                 x1=x1, xh1=xh1, rs1=rs1, h1=h1, xh2=xh2, rs2=rs2)
    return x2, saved


def layer_backward(l, dx2, s, W, sm, rope, rconsts, hooks):
    n = f"l{l}_"
    two = ((F32, None), (BF16, None))
    gw, gs = {}, {}
    job = hooks.bwd_job(l, "ln2")
    res = ln_bwd(dx2, s["xh2"], s["rs2"], sm["ln2_g"], name=n + "ln2_bwd", job=job)
    if job is not None:
        res, job_out = res
        hooks.done(job, job_out)
    du2, du2h, gs["ln2_g"], gs["ln2_b"] = res
    job = hooks.bwd_job(l, "g_down")
    gw["w_down"] = matmul(s["h1"], du2h, mode="tn", tm=1024, tn=1024, tk=2048, pro=_relu2, outs=two, name=n + "g_down", job=job)
    if job is not None:
        gw["w_down"], job_out = gw["w_down"]
        hooks.done(job, job_out)
    dh1 = matmul(du2h, W["w_down"], mode="nt", tm=1024, tn=1024, tk=1024, outs=((BF16, None),),
                 epi=lambda acc, h: (acc * (2.0 * jnp.maximum(h.astype(F32), 0.0)),), tiles=(s["h1"],), name=n + "d_h1")
    job = hooks.bwd_job(l, "g_up")
    gw["w_up"] = matmul(s["x1"], dh1, mode="tn", tm=1024, tn=1024, tk=2048, outs=two, name=n + "g_up", job=job)
    if job is not None:
        gw["w_up"], job_out = gw["w_up"]
        hooks.done(job, job_out)
    dx1 = matmul(dh1, W["w_up"], mode="nt", tm=1024, tn=1024, tk=2048,
                 epi=lambda acc, d: (acc + ALPHA * d,), tiles=(du2,), name=n + "d_x1")
    du1, du1h, gs["ln1_g"], gs["ln1_b"] = ln_bwd(dx1, s["xh1"], s["rs1"], sm["ln1_g"], name=n + "ln1_bwd")
    gw["w_out"] = matmul(s["merged"], du1h, mode="tn", tm=1024, tn=1024, tk=2048, outs=two, name=n + "g_out")
    dmerged = matmul(du1h, W["w_out"], mode="nt", tm=1024, tn=1024, tk=1024, name=n + "d_merged")
    dr1, dr2, dr3, dg1, dg2, dg3 = merge_bwd(dmerged, *s["r"], s["proj"], name=n + "merge_bwd")
    d_branch = {}
    for nm, a, dr in (("p_ret", s["retg"], dr1), ("p_sb", s["sb"], dr2), ("p_sgu", s["sg"], dr3)):
        gw[nm] = matmul(a, dr, mode="tn", tm=512, tn=1024, tk=2048, outs=two, name=n + "g_" + nm)
        d_branch[nm] = matmul(dr, W[nm], mode="nt", tm=1024, tn=512, tk=1024, name=n + "d_" + nm)
    job = hooks.pair(l, gw)
    dret, gs["ret_gn_g"], gs["ret_gn_b"], job_out = ret_bwd(s["proj"], *rope, rconsts, sm["ret_gn_g"], sm["ret_gn_b"],
                                                             s["raw"], s["states"], d_branch["p_ret"], name=n + "ret_bwd", job=job)
    if job is not None:
        hooks.done(job, job_out)
    job = hooks.scatter(l) if job is not None else None
    dsq, dsk, dsv, job_out = sb_bwd(s["proj"], s["sb"], d_branch["p_sb"], name=n + "sb_bwd", job=job)
    if job is not None:
        hooks.done(job, job_out)
    dsgu, gs["sgu_w"], dbias, gs["sgu_ln_g"], gs["sgu_ln_b"] = sgu_bwd(
        s["proj"], sm["sgu_ln_g"], sm["sgu_ln_b"], sm["sgu_w"], sm["sgu_bias"], d_branch["p_sgu"], name=n + "sgu_bwd")
    gs["sgu_b"] = dbias[:, :, 0]
    dproj = jnp.concatenate([dret, dsq, dsk, dsv, dsgu, dg1, dg2, dg3], axis=1)
    job = hooks.small(l, gs)
    gw["w_in"] = matmul(s["x0"], dproj, mode="tn", tm=1024, tn=1920, tk=1024, outs=two, name=n + "g_in", job=job)
    if job is not None:
        gw["w_in"], job_out = gw["w_in"]
        hooks.done(job, job_out)
    job = hooks.tail(l, gw["w_in"])
    dx0 = matmul(dproj, W["w_in"], mode="nt", tm=1024, tn=1024, tk=2560,
                 epi=lambda acc, d: (acc + ALPHA * d,), tiles=(du1,), name=n + "d_x0", job=job)
    if job is not None:
        dx0, job_out = dx0
        hooks.done(job, job_out)
    return dx0, gw, gs


def local_step(x, target, small, plan):
    T = x.shape[0]
    rope = _rope_tables(T)
    rconsts = _ret_consts()
    sms = []
    for l in range(DEPTH):
        sm = {k: small[k][l][None, :] for k in SMALL if k not in ("sgu_w", "sgu_b")}
        sm["sgu_w"] = small["sgu_w"][l]
        sm["sgu_bias"] = jnp.broadcast_to(small["sgu_b"][l][:, :, None], (4, CHUNK, CHUNK))
        sms.append(sm)
    h, saved = x, []
    for l in range(DEPTH):
        h, s = layer_forward(l, h, plan.weights(l), sms[l], rope, rconsts, plan)
        saved.append(s)
    dy, sq = loss_head(h, target)
    gs = {k: [None] * DEPTH for k in SMALL}
    for l in reversed(range(DEPTH)):
        dy, gwl, gsl = layer_backward(l, dy, saved[l], plan.weights(l), sms[l], rope, rconsts, plan)
        plan.grads(l, gwl)
        for k in SMALL:
            gs[k][l] = gsl[k].reshape(small[k].shape[1:])
    return sq[0, 0], dy, {k: jnp.stack(v) for k, v in gs.items()}


EARLY_GRADS = ("p_ret", "p_sb", "p_sgu", "w_out", "w_up", "w_down")


class _StepPlan:
    def __init__(self, pos, shards16):
        self.pos = pos
        self.shards16 = shards16
        self.full = [dict() for _ in range(DEPTH)]
        self.gw = [None] * DEPTH
        self.bufs = {}
        self.sums = {}
        self.gs = [None] * DEPTH
        first = self._gather([(0, "w_in")])
        self.done(first, run_job(first, name="gather_first"))

    def weights(self, l):
        return self.full[l]

    def grads(self, l, gw):
        self.gw[l] = gw

    def _gather(self, items, chips=(0, 1, 2)):
        job = gather_job([self.shards16[l][k] for l, k in items], [BIG_AXIS[k] for _, k in items], chips)
        job.note = ("gather" if 2 in chips else "gather_part", items)
        return job

    def _pair(self, items):
        job = pair_job([g[1] for _, _, g in items], [BIG_AXIS[k] for _, k, _ in items])
        job.note = ("pair", items)
        return job

    def fwd_job(self, l, host):
        if host == "proj":
            return None
        if host == "sb":
            return self._gather([(l, k) for k in BIG[1:]])
        if l + 1 == DEPTH:
            return None
        return self._gather([(l + 1, "w_in")], (0, 1) if host == "up" else (2,))

    def bwd_job(self, l, host):
        if l + 1 == DEPTH:
            return None
        if host == "ln2":
            job = self._pair([(l + 1, "w_in", self.gw[l + 1]["w_in"])])
            job.note = ("pair_w_in", job.note[1])
            return job
        items, sums16 = self.summed_w_in
        job = scatter_job([l_ for l_, _, _ in items], sums16, [BIG_AXIS[k] for _, k, _ in items],
                          [self.bufs.get(k) for _, k, _ in items], (0, 1) if host == "g_down" else (2,))
        job.note = ("scatter", items)
        return job

    def pair(self, l, ready):
        return self._pair([(l, k, ready[k]) for k in EARLY_GRADS])

    def scatter(self, l):
        items, sums16 = self.summed
        job = scatter_job([l_ for l_, _, _ in items], sums16, [BIG_AXIS[k] for _, k, _ in items],
                          [self.bufs.get(k) for _, k, _ in items])
        job.note = ("scatter", items)
        return job

    def small(self, l, gs):
        self.gs[l] = {k: gs[k].reshape(-1) for k in SMALL}
        if l != 0:
            return None
        job = small_job(_pack_small({k: jnp.stack([self.gs[l_][k] for l_ in range(DEPTH)]) for k in SMALL}))
        job.note = ("small", [])
        return job

    def tail(self, l, g):
        if l != 0:
            return None
        last = self._pair([(0, "w_in", g)])
        self.done(last, run_job(last, name="pair_last"))
        return self.scatter(0)

    def done(self, job, outs):
        kind, items = job.note
        if kind == "small":
            self.small_slots = outs[0]
        if kind in ("pair", "pair_w_in"):
            sums16 = []
            for a, (l, k, g) in enumerate(items):
                self.sums[(l, k)], s16 = pair_sum(self.pos, outs[a], g[0], BIG_AXIS[k], name=f"pair_sum_{k}_{l}")
                sums16.append(s16)
            if kind == "pair":
                self.summed = (items, sums16)
            else:
                self.summed_w_in = (items, sums16)
        for a, item in enumerate(items):
            if kind == "gather_part":
                self.shards16[item[0]][item[1]] = outs[a]
            elif kind == "gather":
                self.full[item[0]][item[1]] = outs[a]
            elif kind == "scatter":
                self.bufs[item[1]] = outs[a]

    def finish(self):
        return self.bufs, self.sums


def _flat2(a):
    return a.reshape(-1, a.shape[-1])


def _pack_small(d, pre=""):
    return jnp.concatenate([d[pre + k].reshape(-1) for k in SMALL]).reshape(-1, 128)


def kernel(x, w_in, ret_gn_g, ret_gn_b, sgu_ln_g, sgu_ln_b, sgu_w, sgu_b, p_ret, p_sb, p_sgu, w_out, ln1_g, ln1_b, w_up, w_down, ln2_g, ln2_b, loss_target, m_w_in, m_ret_gn_g, m_ret_gn_b, m_sgu_ln_g, m_sgu_ln_b, m_sgu_w, m_sgu_b, m_p_ret, m_p_sb, m_p_sgu, m_w_out, m_ln1_g, m_ln1_b, m_w_up, m_w_down, m_ln2_g, m_ln2_b, v_w_in, v_ret_gn_g, v_ret_gn_b, v_sgu_ln_g, v_sgu_ln_b, v_sgu_w, v_sgu_b, v_p_ret, v_p_sb, v_p_sgu, v_w_out, v_ln1_g, v_ln1_b, v_w_up, v_w_down, v_ln2_g, v_ln2_b):
    given = dict(locals())
    order = BIG[:1] + SMALL[:6] + BIG[1:5] + SMALL[6:8] + BIG[5:7] + SMALL[8:10]
    L = DEPTH

    px, py, pc = _place()
    pos = jnp.stack([px, py, pc, 2 * px + py]).astype(jnp.int32)

    shards16 = [{k: cast_into_whole(pos, given[k], l, BIG_AXIS[k], name=f"cast_{k}_{l}") for k in BIG} for l in range(L)]
    plan = _StepPlan(pos, shards16)
    sq, dx, gs = local_step(x[0], loss_target[0], {k: given[k] for k in SMALL}, plan)
    loss = 0.5 * lax.psum(sq, ("x", "y", "c"))

    bufs, sums = plan.finish()
    shards = []
    for k in BIG:
        whole = None
        for l in range(L):
            whole = chip_sum(pos, sums[(l, k)], bufs[k], l, BIG_AXIS[k], whole, name=f"chip_sum_{k}_{l}")
        shards.append(whole)
    joined = run_job(join_job(shards), name="join_halves")
    out = {}
    for a, k in enumerate(BIG):
        shp = given[k].shape
        res = _rows_call(lambda g_, w_, m_, v_: (g_,) + _adamw(w_, g_, m_, v_),
                         [joined[a].reshape(-1, shp[-1]), _flat2(given[k]), _flat2(given["m_" + k]), _flat2(given["v_" + k])],
                         [F32] * 4, name="adamw_" + k)
        out[k] = [r.reshape(shp) for r in res]

    pack = _pack_small
    res = _rows_call(lambda g_, w_, m_, v_: (g_,) + _adamw(w_, g_, m_, v_),
                     [small_sum(plan.small_slots), pack(given), pack(given, "m_"), pack(given, "v_")], [F32] * 4,
                     name="adamw_small", tr=8 * 47)
    off = 0
    for k in SMALL:
        sz = given[k].size
        out[k] = [r.reshape(-1)[off:off + sz].reshape(given[k].shape) for r in res]
        off += sz

    grads = [out[k][0] for k in order]
    deltas = [out[k][1] for k in order]
    new_m = [out[k][2] for k in order]
    new_v = [out[k][3] for k in order]
    return (loss, dx[None], *grads, *deltas, *new_m, *new_v)
```

```python
import functools
import math

import jax
import jax.numpy as jnp
from jax import lax
from jax.experimental import pallas as pl
from jax.experimental.pallas import tpu as pltpu

F32 = jnp.float32
BF16 = jnp.bfloat16

D_MODEL = 1024
SEQ = 4096
DEPTH = 2
CHUNK = 128
RET_HEADS = 4
BRANCH_W = 512
N_IN = 7680
D_FF = 4096
LN_EPS = 1e-5
ROPE_BASE = 10000.0
ALPHA = (2 * DEPTH) ** 0.25
RET_SCALE = 128 ** -0.5
SB_SCALE = 64 ** -0.5
C_RET, C_SB, C_SGU, C_GATE = 0, 2048, 3584, 4608

ADAM_LR, ADAM_B1, ADAM_B2, ADAM_EPS, ADAM_WD, ADAM_STEP = 0.001, 0.9, 0.999, 1e-08, 0.01, 10

N_CHIPS = 4
VMEM_LIMIT = 56 * 1024 * 1024
MESH = pl.DeviceIdType.MESH

NN = ((1,), (0,))
NT = ((1,), (1,))
TN = ((0,), (0,))


def _dot(a, b, dims):
    return lax.dot_general(a, b, (dims, ((), ())), preferred_element_type=F32)


def _params(sem):
    return pltpu.CompilerParams(dimension_semantics=sem, vmem_limit_bytes=VMEM_LIMIT)


def _relu2(h):
    r = jnp.maximum(h.astype(F32), 0.0)
    return r * r


def matmul(a, b, *, mode, tm, tn, tk, outs=((F32, None),), pro=None, epi=None, tiles=(), rows=(), name, job=None):
    if mode == "nn":
        (M, K), N = a.shape, b.shape[1]
    elif mode == "nt":
        (M, K), N = a.shape, b.shape[0]
    else:
        (K, M), N = a.shape, b.shape[1]
    tm, tn, tk = min(tm, M), min(tn, N), min(tk, K)
    assert M % tm == 0 and N % tn == 0 and K % tk == 0, (name, M, N, K, tm, tn, tk)
    if mode == "nn":
        a_spec = pl.BlockSpec((tm, tk), lambda i, j, k: (i, k))
        b_spec = pl.BlockSpec((tk, tn), lambda i, j, k: (k, j))
        dims = NN
    elif mode == "nt":
        a_spec = pl.BlockSpec((tm, tk), lambda i, j, k: (i, k))
        b_spec = pl.BlockSpec((tn, tk), lambda i, j, k: (j, k))
        dims = NT
    else:
        a_spec = pl.BlockSpec((tk, tm), lambda i, j, k: (k, i))
        b_spec = pl.BlockSpec((tk, tn), lambda i, j, k: (k, j))
        dims = TN
    nk = K // tk
    nt_, nr, no = len(tiles), len(rows), len(outs)

    def body(a_ref, b_ref, *rest):
        tile_refs = rest[:nt_]
        row_refs = rest[nt_:nt_ + nr]
        out_refs = rest[nt_ + nr:nt_ + nr + no]
        av = a_ref[...]
        if pro is not None:
            av = pro(av)
        p = _dot(av.astype(BF16), b_ref[...].astype(BF16), dims)

        def finish(acc):
            vals = (acc,) * no if epi is None else epi(acc, *[r[...] for r in tile_refs], *[r[...] for r in row_refs])
            for o_ref, v in zip(out_refs, vals):
                o_ref[...] = v.astype(o_ref.dtype)

        if nk == 1:
            finish(p)
        else:
            acc_ref = rest[-1]
            k = pl.program_id(2)

            @pl.when(k == 0)
            def _():
                acc_ref[...] = p

            @pl.when(k > 0)
            def _():
                acc_ref[...] += p

            @pl.when(k == nk - 1)
            def _():
                finish(acc_ref[...])

    out_shape, out_specs = [], []
    for dt, width in outs:
        if width is None:
            out_shape.append(jax.ShapeDtypeStruct((M, N), dt))
            out_specs.append(pl.BlockSpec((tm, tn), lambda i, j, k: (i, j)))
        else:
            assert N == tn
            out_shape.append(jax.ShapeDtypeStruct((M, width), dt))
            out_specs.append(pl.BlockSpec((tm, width), lambda i, j, k: (i, 0)))
    in_specs = [a_spec, b_spec]
    offs = [t[1] if isinstance(t, tuple) else 0 for t in tiles]
    tiles = [t[0] if isinstance(t, tuple) else t for t in tiles]
    in_specs += [pl.BlockSpec((tm, tn), functools.partial(lambda i, j, k, o: (i, j + o), o=o)) for o in offs]
    in_specs += [pl.BlockSpec((1, tn), lambda i, j, k: (0, j)) for _ in rows]
    grid = (M // tm, N // tn, nk)
    scratch = [pltpu.VMEM((tm, tn), F32)] if nk > 1 else []
    j = _job_args(job, len(in_specs), no)
    res = pl.pallas_call(
        _hosting(body, job, len(in_specs), no, len(scratch), grid), name=name, grid=grid,
        in_specs=in_specs + j["in_specs"], out_specs=out_specs + j["out_specs"], out_shape=out_shape + j["out_shape"],
        scratch_shapes=scratch + j["scratch"], input_output_aliases=j["aliases"],
        compiler_params=_params(("parallel", "parallel", "arbitrary") if job is None else ("arbitrary",) * 3),
    )(a, b, *tiles, *rows, *j["ins"])
    mine = res[0] if no == 1 else list(res[:no])
    return mine if job is None else (mine, list(res[no:]))


def _ln_epi(acc, res, g, b):
    u = ALPHA * res + acc
    mu = jnp.mean(u, axis=-1, keepdims=True)
    xc = u - mu
    var = jnp.mean(xc * xc, axis=-1, keepdims=True)
    rstd = lax.rsqrt(var + LN_EPS)
    xhat = xc * rstd
    return xhat * g + b, xhat, jnp.broadcast_to(rstd, (u.shape[0], 128))


def matmul_ln(a, w, res, g, b, *, pro=None, tk, name, job=None):
    n = w.shape[1]
    return matmul(a, w, mode="nn", tm=1024, tn=n, tk=tk, pro=pro, epi=_ln_epi, tiles=(res,), rows=(g, b),
                  outs=((F32, None), (F32, None), (F32, 128)), name=name, job=job)


def ln_bwd(dy, xhat, rstd, g, *, name, job=None):
    T, D = dy.shape
    tm = min(512, T)

    def body(dy_ref, xh_ref, rs_ref, g_ref, du_ref, du16_ref, dg_ref, db_ref):
        dyv, xh = dy_ref[...], xh_ref[...]
        r = rs_ref[:, 0:1]
        dxh = dyv * g_ref[...]
        m1 = jnp.mean(dxh, axis=-1, keepdims=True)
        m2 = jnp.mean(dxh * xh, axis=-1, keepdims=True)
        du = r * (dxh - m1 - xh * m2)
        du_ref[...] = du
        du16_ref[...] = du.astype(BF16)

        @pl.when(pl.program_id(0) == 0)
        def _():
            dg_ref[...] = jnp.zeros_like(dg_ref)
            db_ref[...] = jnp.zeros_like(db_ref)

        dg_ref[...] += jnp.sum(dyv * xh, axis=0, keepdims=True)
        db_ref[...] += jnp.sum(dyv, axis=0, keepdims=True)

    row = pl.BlockSpec((tm, D), lambda i: (i, 0))
    vec = pl.BlockSpec((1, D), lambda i: (0, 0))
    j = _job_args(job, 4, 4)
    res = pl.pallas_call(
        _hosting(body, job, 4, 4, 0, T // tm), name=name, grid=(T // tm,),
        in_specs=[row, row, pl.BlockSpec((tm, 128), lambda i: (i, 0)), vec] + j["in_specs"],
        out_specs=[row, row, vec, vec] + j["out_specs"],
        out_shape=[jax.ShapeDtypeStruct((T, D), F32), jax.ShapeDtypeStruct((T, D), BF16),
                   jax.ShapeDtypeStruct((1, D), F32), jax.ShapeDtypeStruct((1, D), F32)] + j["out_shape"],
        scratch_shapes=j["scratch"], input_output_aliases=j["aliases"],
        compiler_params=_params(("arbitrary",)),
    )(dy, xhat, rstd, g, *j["ins"])
    return list(res[:4]) if job is None else (list(res[:4]), list(res[4:]))


def loss_head(y, target):
    T, D = y.shape
    tm = min(512, T)

    def body(y_ref, t_ref, dy_ref, s_ref):
        e = y_ref[...] - t_ref[...]
        dy_ref[...] = e * (1.0 / D)

        @pl.when(pl.program_id(0) == 0)
        def _():
            s_ref[...] = jnp.zeros_like(s_ref)

        s_ref[...] += jnp.sum(jnp.mean(e * e, axis=-1, keepdims=True))

    row = pl.BlockSpec((tm, D), lambda i: (i, 0))
    return pl.pallas_call(
        body, name="loss_head", grid=(T // tm,),
        in_specs=[row, row], out_specs=[row, pl.BlockSpec((8, 128), lambda i: (0, 0))],
        out_shape=[jax.ShapeDtypeStruct((T, D), F32), jax.ShapeDtypeStruct((8, 128), F32)],
        compiler_params=_params(("arbitrary",)),
    )(y, target)


def _rope_tables(T):
    half = 64
    inv_freq = ROPE_BASE ** (-jnp.arange(half, dtype=F32) / half)
    ang = jnp.arange(T, dtype=jnp.int32).astype(F32)[:, None] * inv_freq[None, :]
    cos, sin = jnp.cos(ang), jnp.sin(ang)
    return jnp.concatenate([cos, cos], axis=1), jnp.concatenate([-sin, sin], axis=1)


def _ret_consts():
    H = RET_HEADS
    log_g = jnp.log(1.0 - 2.0 ** (-5.0 - jnp.arange(H, dtype=F32)))
    idx = jnp.arange(CHUNK, dtype=F32)
    diff = idx[:, None] - idx[None, :]
    dmat = jnp.where(diff[None] >= 0, jnp.exp(log_g[:, None, None] * diff[None]), 0.0)
    kd = jnp.exp(log_g[:, None] * (CHUNK - 1 - idx)[None, :])
    qd = jnp.exp(log_g[:, None] * (idx + 1.0)[None, :])
    cd = jnp.exp(log_g * CHUNK)
    full = (H, CHUNK, CHUNK)
    return (dmat.astype(F32), jnp.broadcast_to(kd[:, :, None], full), jnp.broadcast_to(qd[:, :, None], full),
            jnp.broadcast_to(cd[:, None, None], full))


def _swap_halves(v):
    return pltpu.roll(v, 64, 1)


def _group_norm(o):
    mu = jnp.mean(o, axis=-1, keepdims=True)
    xc = o - mu
    var = jnp.mean(xc * xc, axis=-1, keepdims=True)
    rstd = lax.rsqrt(var + LN_EPS)
    return xc * rstd, rstd


def ret_fwd(proj, cosf, sinf, consts, gn_g, gn_b, *, name):
    T = proj.shape[0]
    tb = min(512, T)
    nch = tb // CHUNK
    H = RET_HEADS

    def body(p_ref, cos_ref, sin_ref, dm_ref, kd_ref, qd_ref, cd_ref, g_ref, b_ref, out_ref, raw_ref, st_ref, s_ref):
        @pl.when(pl.program_id(0) == 0)
        def _():
            s_ref[...] = jnp.zeros_like(s_ref)

        for c in range(nch):
            r = slice(c * CHUNK, (c + 1) * CHUNK)
            cs, sn = cos_ref[r, :], sin_ref[r, :]
            for h in range(H):
                hc = slice(h * 128, (h + 1) * 128)
                q = p_ref[r, h * 128:(h + 1) * 128]
                k = p_ref[r, 512 + h * 128:512 + (h + 1) * 128]
                v = p_ref[r, 1024 + h * 128:1024 + (h + 1) * 128]
                gt = p_ref[r, 1536 + h * 128:1536 + (h + 1) * 128]
                qr = q * cs + _swap_halves(q) * sn
                kr = (k * cs + _swap_halves(k) * sn) * RET_SCALE
                sprev = s_ref[h]
                st_ref[c, h] = sprev
                qb, kb, vb = qr.astype(BF16), kr.astype(BF16), v.astype(BF16)
                s = _dot(qb, kb, NT) * dm_ref[h]
                o = _dot(s.astype(BF16), vb, NN) + _dot((qr * qd_ref[h]).astype(BF16), sprev.astype(BF16), NN)
                s_ref[h] = sprev * cd_ref[h] + _dot((kr * kd_ref[h]).astype(BF16), vb, TN)
                raw_ref[r, hc] = o
                y, _ = _group_norm(o)
                out_ref[r, hc] = (gt * jax.nn.sigmoid(gt)) * (y * g_ref[:, hc] + b_ref[:, hc])

    cmat = pl.BlockSpec((H, CHUNK, CHUNK), lambda i: (0, 0, 0))
    vec = pl.BlockSpec((1, BRANCH_W), lambda i: (0, 0))
    rope = pl.BlockSpec((tb, 128), lambda i: (i, 0))
    blk = pl.BlockSpec((tb, BRANCH_W), lambda i: (i, 0))
    return pl.pallas_call(
        body, name=name, grid=(T // tb,),
        in_specs=[pl.BlockSpec((tb, 2048), lambda i: (i, 0)), rope, rope, cmat, cmat, cmat, cmat, vec, vec],
        out_specs=[blk, blk, pl.BlockSpec((nch, H, CHUNK, CHUNK), lambda i: (i, 0, 0, 0))],
        out_shape=[jax.ShapeDtypeStruct((T, BRANCH_W), F32), jax.ShapeDtypeStruct((T, BRANCH_W), F32),
                   jax.ShapeDtypeStruct((T // CHUNK, H, CHUNK, CHUNK), F32)],
        scratch_shapes=[pltpu.VMEM((H, CHUNK, CHUNK), F32)],
        compiler_params=_params(("arbitrary",)),
    )(proj, cosf, sinf, *consts, gn_g, gn_b)


def ret_bwd(proj, cosf, sinf, consts, gn_g, gn_b, raw, states, dout, *, name, job=None):
    T = proj.shape[0]
    tb = min(512, T)
    nch = tb // CHUNK
    nb = T // tb
    H = RET_HEADS

    def body(p_ref, cos_ref, sin_ref, dm_ref, kd_ref, qd_ref, cd_ref, g_ref, b_ref, raw_ref, st_ref, do_ref,
             dp_ref, dg_ref, db_ref, ds_ref):
        @pl.when(pl.program_id(0) == 0)
        def _():
            ds_ref[...] = jnp.zeros_like(ds_ref)
            dg_ref[...] = jnp.zeros_like(dg_ref)
            db_ref[...] = jnp.zeros_like(db_ref)

        for c in reversed(range(nch)):
            r = slice(c * CHUNK, (c + 1) * CHUNK)
            cs, sn = cos_ref[r, :], sin_ref[r, :]
            for h in range(H):
                hc = slice(h * 128, (h + 1) * 128)
                q = p_ref[r, h * 128:(h + 1) * 128]
                k = p_ref[r, 512 + h * 128:512 + (h + 1) * 128]
                v = p_ref[r, 1024 + h * 128:1024 + (h + 1) * 128]
                gt = p_ref[r, 1536 + h * 128:1536 + (h + 1) * 128]
                qr = q * cs + _swap_halves(q) * sn
                kr = (k * cs + _swap_halves(k) * sn) * RET_SCALE
                sprev = st_ref[c, h]
                gv = g_ref[:, hc]
                y, rstd = _group_norm(raw_ref[r, hc])
                d_out = do_ref[r, hc]
                sg = jax.nn.sigmoid(gt)
                d_gate = d_out * (y * gv + b_ref[:, hc]) * (sg * (1.0 + gt * (1.0 - sg)))
                d_aff = d_out * (gt * sg)
                dg_ref[:, hc] += jnp.sum(d_aff * y, axis=0, keepdims=True)
                db_ref[:, hc] += jnp.sum(d_aff, axis=0, keepdims=True)
                dxh = d_aff * gv
                m1 = jnp.mean(dxh, axis=-1, keepdims=True)
                m2 = jnp.mean(dxh * y, axis=-1, keepdims=True)
                d_o = (rstd * (dxh - m1 - y * m2)).astype(BF16)
                qb, kb, vb = qr.astype(BF16), kr.astype(BF16), v.astype(BF16)
                dm, kd, qd = dm_ref[h], kd_ref[h], qd_ref[h]
                p = (_dot(qb, kb, NT) * dm).astype(BF16)
                dp = (_dot(d_o, vb, NT) * dm).astype(BF16)
                dsn = ds_ref[h]
                dsb = dsn.astype(BF16)
                dq_r = _dot(dp, kb, NN) + _dot(d_o, sprev.astype(BF16), NT) * qd
                dk_r = (_dot(dp, qb, TN) + _dot(vb, dsb, NT) * kd) * RET_SCALE
                d_v = _dot(p, d_o, TN) + _dot((kr * kd).astype(BF16), dsb, NN)
                ds_ref[h] = dsn * cd_ref[h] + _dot((qr * qd).astype(BF16), d_o, TN)
                dp_ref[r, h * 128:(h + 1) * 128] = (dq_r * cs - _swap_halves(dq_r) * sn).astype(BF16)
                dp_ref[r, 512 + h * 128:512 + (h + 1) * 128] = (dk_r * cs - _swap_halves(dk_r) * sn).astype(BF16)
                dp_ref[r, 1024 + h * 128:1024 + (h + 1) * 128] = d_v.astype(BF16)
                dp_ref[r, 1536 + h * 128:1536 + (h + 1) * 128] = d_gate.astype(BF16)

    cmat = pl.BlockSpec((H, CHUNK, CHUNK), lambda i: (0, 0, 0))
    vec = pl.BlockSpec((1, BRANCH_W), lambda i: (0, 0))
    rope = pl.BlockSpec((tb, 128), lambda i: (nb - 1 - i, 0))
    blk = pl.BlockSpec((tb, BRANCH_W), lambda i: (nb - 1 - i, 0))
    wide = pl.BlockSpec((tb, 2048), lambda i: (nb - 1 - i, 0))
    j = _job_args(job, 12, 3)
    res = pl.pallas_call(
        _hosting(body, job, 12, 3, 1, nb), name=name, grid=(nb,),
        in_specs=[wide, rope, rope, cmat, cmat, cmat, cmat, vec, vec, blk,
                  pl.BlockSpec((nch, H, CHUNK, CHUNK), lambda i: (nb - 1 - i, 0, 0, 0)), blk] + j["in_specs"],
        out_specs=[wide, vec, vec] + j["out_specs"],
        out_shape=[jax.ShapeDtypeStruct((T, 2048), BF16), jax.ShapeDtypeStruct((1, BRANCH_W), F32),
                   jax.ShapeDtypeStruct((1, BRANCH_W), F32)] + j["out_shape"],
        scratch_shapes=[pltpu.VMEM((H, CHUNK, CHUNK), F32)] + j["scratch"], input_output_aliases=j["aliases"],
        compiler_params=_params(("arbitrary",)),
    )(proj, cosf, sinf, *consts, gn_g, gn_b, raw, states, dout, *j["ins"])
    return res[0], res[1], res[2], list(res[3:])


def _sb_masks():
    row = lax.broadcasted_iota(jnp.int32, (CHUNK, CHUNK), 0)
    lane = lax.broadcasted_iota(jnp.int32, (CHUNK, CHUNK), 1)
    return row, lane


SB_QT = 256
SB_DEAD = -105.0


def _pair(v):
    hi = v.astype(BF16)
    return jnp.concatenate([hi, (v - hi.astype(F32)).astype(BF16)], axis=1)


def _sb_consts():
    r = lax.broadcasted_iota(jnp.int32, (256, 256), 0) & 127
    c = lax.broadcasted_iota(jnp.int32, (256, 256), 1)
    ones = c >= 128
    lane = lax.broadcasted_iota(jnp.int32, (CHUNK, CHUNK), 1)
    return (ones | (r > c)).astype(BF16), (ones | (r >= c)).astype(BF16), (lane < 64, lane >= 64)


def _per_head(x, hms):
    return jnp.concatenate([jnp.where(hm, x, 0.0) for hm in hms], axis=0).astype(BF16)


def _sb_logits(qb, kb2, mask2):
    z = _dot(qb, kb2, NT)
    l1p = jnp.log(1.0 + jnp.exp(-jnp.abs(z)))
    lsp = jnp.minimum(z, 0.0) - l1p
    lsn = lsp - z
    if mask2 is not None:
        lsn = jnp.where(mask2, lsn, 0.0)
    return lsp, lsn


def _sb_tile_mask(qt):
    trow = lax.broadcasted_iota(jnp.int32, (qt, 256), 0)
    tlane = lax.broadcasted_iota(jnp.int32, (qt, 256), 1) & 127
    return lambda m: (tlane + m * CHUNK) < trow


def sb_fwd(proj, *, name, job=None):
    T = proj.shape[0]
    qt = min(SB_QT, T)
    nsub = qt // CHUNK
    cb = C_SB // 128

    def body(q_ref, k_ref, v_ref, o_ref):
        u_gt, _, hms = _sb_consts()
        tile_mask = _sb_tile_mask(qt)

        def qtile(i, _):
            rq = pl.ds(pl.multiple_of(i * qt, qt), qt)
            qb = (q_ref[rq, :] * SB_SCALE).astype(BF16)

            def group(js, masks, state):
                carry, acc = list(state[:2]), state[2]
                rows = [pl.ds(pl.multiple_of(j * CHUNK, CHUNK), CHUNK) for j in js]
                logits = [_sb_logits(qb, _per_head(k_ref[rk, :], hms), m) for rk, m in zip(rows, masks)]
                sums = [[_dot(_pair(lsn[:, h * 128:(h + 1) * 128]), u_gt, NN) for h in range(2)] for _, lsn in logits]
                weights = []
                for (lsp, _), r, m in zip(logits, sums, masks):
                    a_b = []
                    for h in range(2):
                        hc = slice(h * 128, (h + 1) * 128)
                        a = jnp.exp(lsp[:, hc] + r[h][:, :128] + carry[h])
                        if m is not None:
                            a = jnp.where(m[:, hc], a, 0.0)
                        carry[h] = carry[h] + r[h][:, 128:]
                        a_b.append(a.astype(BF16))
                    weights.append(jnp.concatenate(a_b, axis=1))
                for rk, a in zip(rows, weights):
                    acc = acc + _dot(a, _per_head(v_ref[rk, :], hms), NN)
                return carry[0], carry[1], acc

            zero = jnp.zeros((qt, 128), F32)
            diag = list(reversed(range(nsub)))
            state = group([i * nsub + m for m in diag], [tile_mask(m) for m in diag], (zero, zero, zero))

            def live(c):
                return jnp.logical_and(c[0] < i, jnp.maximum(jnp.max(c[1][0]), jnp.max(c[1][1])) > SB_DEAD)

            def blocks(c):
                jj, st = c
                return jj + 1, group([(i - jj) * nsub - 1 - u for u in range(nsub)], [None] * nsub, st)

            _, state = lax.while_loop(live, blocks, (jnp.int32(0), state))
            o_ref[rq, :] = state[2]
            return 0

        lax.fori_loop(0, T // qt, qtile, 0)

    def col(off):
        return pl.BlockSpec((T, 128), lambda hp: (0, off + hp))

    steps = BRANCH_W // 128
    j = _job_args(job, 3, 1)
    res = pl.pallas_call(
        _hosting(body, job, 3, 1, 0, steps), name=name, grid=(steps,),
        in_specs=[col(cb), col(cb + 4), col(cb + 8)] + j["in_specs"], out_specs=[col(0)] + j["out_specs"],
        out_shape=[jax.ShapeDtypeStruct((T, BRANCH_W), F32)] + j["out_shape"],
        scratch_shapes=j["scratch"], input_output_aliases=j["aliases"],
        compiler_params=_params(("parallel",) if job is None else ("arbitrary",)),
    )(proj, proj, proj, *j["ins"])
    return res[0], list(res[1:])


def sb_bwd(proj, out, dout, *, name, job=None):
    T = proj.shape[0]
    qt = min(SB_QT, T)
    nsub = qt // CHUNK
    cb = C_SB // 128

    def body(q_ref, k_ref, v_ref, o_ref, do_ref, dq_ref, dk_ref, dv_ref, dkt_ref, dvt_ref):
        u_gt, u_ge, hms = _sb_consts()
        tile_mask = _sb_tile_mask(qt)
        tall_lane = lax.broadcasted_iota(jnp.int32, (qt, 128), 1)
        top = lax.broadcasted_iota(jnp.int32, (CHUNK, CHUNK), 0) < 64
        dkt_ref[...] = jnp.zeros_like(dkt_ref)
        dvt_ref[...] = jnp.zeros_like(dvt_ref)

        def qtile(i, _):
            rq = pl.ds(pl.multiple_of(i * qt, qt), qt)
            qs = q_ref[rq, :] * SB_SCALE
            qb, q_t = qs.astype(BF16), qs.T.astype(BF16)
            dov = do_ref[rq, :]
            dob, do_t = dov.astype(BF16), dov.T.astype(BF16)
            prod = dob.astype(F32) * o_ref[rq, :]
            total = [jnp.broadcast_to(jnp.sum(jnp.where(hm, prod, 0.0), axis=1, keepdims=True), (qt, 128))
                     for hm in (tall_lane < 64, tall_lane >= 64)]

            def group(js, masks, state):
                c_l, c_w, dq = list(state[:2]), list(state[2:4]), state[4]
                heads = [slice(h * 128, (h + 1) * 128) for h in range(2)]
                rows = [pl.ds(pl.multiple_of(j * CHUNK, CHUNK), CHUNK) for j in js]
                kb2 = [_per_head(k_ref[rk, :], hms) for rk in rows]
                logits = [_sb_logits(qb, kb, m) for kb, m in zip(kb2, masks)]
                da = [_dot(dob, _per_head(v_ref[rk, :], hms), NT) for rk in rows]
                sums = [[_dot(_pair(lsn[:, hc]), u_gt, NN) for hc in heads] for _, lsn in logits]
                a_b, w_all = [], []
                for (lsp, _), r, d, m in zip(logits, sums, da, masks):
                    a_h, w_h = [], []
                    for h, hc in enumerate(heads):
                        a = jnp.exp(lsp[:, hc] + r[h][:, :128] + c_l[h])
                        if m is not None:
                            a = jnp.where(m[:, hc], a, 0.0)
                        c_l[h] = c_l[h] + r[h][:, 128:]
                        a = a.astype(BF16)
                        a_h.append(a)
                        w_h.append(a.astype(F32) * d[:, hc])
                    a_b.append(jnp.concatenate(a_h, axis=1))
                    w_all.append(w_h)
                sums_w = [[_dot(_pair(w), u_ge, NN) for w in w_h] for w_h in w_all]
                dz_b = []
                for (lsp, _), w_h, r, m in zip(logits, w_all, sums_w, masks):
                    sp = jnp.exp(lsp)
                    dz_h = []
                    for h, hc in enumerate(heads):
                        later_w = r[h][:, :128] + c_w[h]
                        c_w[h] = c_w[h] + r[h][:, 128:]
                        dz = w_h[h] * (1.0 - sp[:, hc]) - sp[:, hc] * (total[h] - later_w)
                        if m is not None:
                            dz = jnp.where(m[:, hc], dz, 0.0)
                        dz_h.append(dz.astype(BF16))
                    dz_b.append(jnp.concatenate(dz_h, axis=1))
                for j, kb, a, dz in zip(js, kb2, a_b, dz_b):
                    dkt = _dot(q_t, dz, NN)
                    dvt = _dot(do_t, a, NN)
                    dkt_ref[j] += jnp.where(top, dkt[:, :128], dkt[:, 128:])
                    dvt_ref[j] += jnp.where(top, dvt[:, :128], dvt[:, 128:])
                    dq = dq + _dot(dz, kb, NN)
                return c_l[0], c_l[1], c_w[0], c_w[1], dq

            zero = jnp.zeros((qt, 128), F32)
            diag = list(reversed(range(nsub)))
            state = group([i * nsub + m for m in diag], [tile_mask(m) for m in diag], (zero,) * 5)

            def live(c):
                return jnp.logical_and(c[0] < i, jnp.maximum(jnp.max(c[1][0]), jnp.max(c[1][1])) > SB_DEAD)

            def blocks(c):
                jj, st = c
                return jj + 1, group([(i - jj) * nsub - 1 - u for u in range(nsub)], [None] * nsub, st)

            _, state = lax.while_loop(live, blocks, (jnp.int32(0), state))
            dq_ref[rq, :] = (state[4] * SB_SCALE).astype(BF16)
            return 0

        lax.fori_loop(0, T // qt, qtile, 0)

        def untranspose(jb, _):
            rk = pl.ds(pl.multiple_of(jb * CHUNK, CHUNK), CHUNK)
            dk_ref[rk, :] = dkt_ref[jb].T.astype(BF16)
            dv_ref[rk, :] = dvt_ref[jb].T.astype(BF16)
            return 0

        lax.fori_loop(0, T // CHUNK, untranspose, 0)

    def col(off):
        return pl.BlockSpec((T, 128), lambda hp: (0, off + hp))

    o16 = jax.ShapeDtypeStruct((T, BRANCH_W), BF16)
    steps = BRANCH_W // 128
    j = _job_args(job, 5, 3)
    acc = pltpu.VMEM((T // CHUNK, CHUNK, CHUNK), F32)
    res = pl.pallas_call(
        _hosting(body, job, 5, 3, 2, steps), name=name, grid=(steps,),
        in_specs=[col(cb), col(cb + 4), col(cb + 8), col(0), col(0)] + j["in_specs"],
        out_specs=[col(0), col(0), col(0)] + j["out_specs"], out_shape=[o16, o16, o16] + j["out_shape"],
        scratch_shapes=[acc, acc] + j["scratch"], input_output_aliases=j["aliases"],
        compiler_params=_params(("parallel",) if job is None else ("arbitrary",)),
    )(proj, proj, proj, out, dout, *j["ins"])
    return res[0], res[1], res[2], list(res[3:])


_G0 = math.sqrt(2.0 / math.pi)
_G1 = 0.044715


def _gelu(x):
    return 0.5 * x * (1.0 + jnp.tanh(_G0 * (x + _G1 * x * x * x)))


def _gelu_grad(x):
    t = jnp.tanh(_G0 * (x + _G1 * x * x * x))
    return 0.5 * (1.0 + t) + 0.5 * x * (1.0 - t * t) * (_G0 * (1.0 + 3.0 * _G1 * x * x))


def _tril():
    row, lane = _sb_masks()
    return row >= lane


def sgu_fwd(proj, ln_g, ln_b, w, bias, *, name):
    T = proj.shape[0]
    tb = min(512, T)
    G = BRANCH_W // 128

    def body(u_ref, v_ref, g_ref, b_ref, w_ref, bias_ref, o_ref):
        vv = _gelu(v_ref[...])
        xh, _ = _group_norm(vv)
        vn = (xh * g_ref[...] + b_ref[...]).astype(BF16)
        tril = _tril()
        for g in range(G):
            wg = jnp.where(tril, w_ref[g], 0.0).astype(BF16)
            gc = slice(g * 128, (g + 1) * 128)
            for c in range(tb // CHUNK):
                r = slice(c * CHUNK, (c + 1) * CHUNK)
                sv = _dot(wg, vn[r, gc], NN) + bias_ref[g]
                o_ref[r, gc] = _gelu(u_ref[r, gc]) * sv

    cu, cv = C_SGU // BRANCH_W, C_SGU // BRANCH_W + 1
    vec = pl.BlockSpec((1, BRANCH_W), lambda i: (0, 0))
    mat = pl.BlockSpec((G, CHUNK, CHUNK), lambda i: (0, 0, 0))
    return pl.pallas_call(
        body, name=name, grid=(T // tb,),
        in_specs=[pl.BlockSpec((tb, BRANCH_W), lambda i: (i, cu)), pl.BlockSpec((tb, BRANCH_W), lambda i: (i, cv)),
                  vec, vec, mat, mat],
        out_specs=pl.BlockSpec((tb, BRANCH_W), lambda i: (i, 0)),
        out_shape=jax.ShapeDtypeStruct((T, BRANCH_W), F32),
        compiler_params=_params(("parallel",)),
    )(proj, proj, ln_g, ln_b, w, bias)


def sgu_bwd(proj, ln_g, ln_b, w, bias, dout, *, name):
    T = proj.shape[0]
    tb = min(512, T)
    G = BRANCH_W // 128

    def body(u_ref, v_ref, g_ref, b_ref, w_ref, bias_ref, do_ref, dp_ref, dw_ref, dbias_ref, dg_ref, db_ref, dvn_ref):
        @pl.when(pl.program_id(0) == 0)
        def _():
            dw_ref[...] = jnp.zeros_like(dw_ref)
            dbias_ref[...] = jnp.zeros_like(dbias_ref)
            dg_ref[...] = jnp.zeros_like(dg_ref)
            db_ref[...] = jnp.zeros_like(db_ref)

        gv = v_ref[...]
        vv = _gelu(gv)
        xh, rstd = _group_norm(vv)
        vn = (xh * g_ref[...] + b_ref[...]).astype(BF16)
        tril = _tril()
        for g in range(G):
            wg = jnp.where(tril, w_ref[g], 0.0).astype(BF16)
            gc = slice(g * 128, (g + 1) * 128)
            for c in range(tb // CHUNK):
                r = slice(c * CHUNK, (c + 1) * CHUNK)
                vn_c = vn[r, gc]
                sv = _dot(wg, vn_c, NN) + bias_ref[g]
                gu = u_ref[r, gc]
                d_o = do_ref[r, gc]
                dp_ref[r, gc] = (d_o * sv * _gelu_grad(gu)).astype(BF16)
                dsv = d_o * _gelu(gu)
                dsv_b = dsv.astype(BF16)
                dvn_ref[r, gc] = _dot(wg, dsv_b, TN)
                dw_ref[g] += jnp.where(tril, _dot(dsv_b, vn_c, NT), 0.0)
                dbias_ref[g] += jnp.broadcast_to(jnp.sum(dsv, axis=1, keepdims=True), (CHUNK, CHUNK))
        dvn = dvn_ref[...]
        dg_ref[...] += jnp.sum(dvn * xh, axis=0, keepdims=True)
        db_ref[...] += jnp.sum(dvn, axis=0, keepdims=True)
        dxh = dvn * g_ref[...]
        m1 = jnp.mean(dxh, axis=-1, keepdims=True)
        m2 = jnp.mean(dxh * xh, axis=-1, keepdims=True)
        dp_ref[:, BRANCH_W:2 * BRANCH_W] = (rstd * (dxh - m1 - xh * m2) * _gelu_grad(gv)).astype(BF16)

    cu, cv = C_SGU // BRANCH_W, C_SGU // BRANCH_W + 1
    vec = pl.BlockSpec((1, BRANCH_W), lambda i: (0, 0))
    mat = pl.BlockSpec((G, CHUNK, CHUNK), lambda i: (0, 0, 0))
    blk = pl.BlockSpec((tb, BRANCH_W), lambda i: (i, 0))
    msh = jax.ShapeDtypeStruct((G, CHUNK, CHUNK), F32)
    vsh = jax.ShapeDtypeStruct((1, BRANCH_W), F32)
    return pl.pallas_call(
        body, name=name, grid=(T // tb,),
        in_specs=[pl.BlockSpec((tb, BRANCH_W), lambda i: (i, cu)), pl.BlockSpec((tb, BRANCH_W), lambda i: (i, cv)),
                  vec, vec, mat, mat, blk],
        out_specs=[pl.BlockSpec((tb, 2 * BRANCH_W), lambda i: (i, 0)), mat, mat, vec, vec],
        out_shape=[jax.ShapeDtypeStruct((T, 2 * BRANCH_W), BF16), msh, msh, vsh, vsh],
        scratch_shapes=[pltpu.VMEM((tb, BRANCH_W), F32)],
        compiler_params=_params(("arbitrary",)),
    )(proj, proj, ln_g, ln_b, w, bias, dout)


def merge_fwd(a1, a2, a3, p1, p2, p3, proj, *, name):
    T = a1.shape[0]
    tm, tn = min(1024, T), 512
    gb = C_GATE // tn

    def body(a1_ref, a2_ref, a3_ref, p1_ref, p2_ref, p3_ref, g1_ref, g2_ref, g3_ref, m_ref, r1_ref, r2_ref, r3_ref):
        m = None
        for a_ref, p_ref, g_ref, r_ref in ((a1_ref, p1_ref, g1_ref, r1_ref), (a2_ref, p2_ref, g2_ref, r2_ref),
                                           (a3_ref, p3_ref, g3_ref, r3_ref)):
            r = _dot(a_ref[...].astype(BF16), p_ref[...], NN)
            r_ref[...] = r.astype(r_ref.dtype)
            t = jax.nn.sigmoid(g_ref[...]) * r
            m = t if m is None else m + t
        m_ref[...] = m.astype(m_ref.dtype)

    a_spec = pl.BlockSpec((tm, BRANCH_W), lambda i, j: (i, 0))
    p_spec = pl.BlockSpec((BRANCH_W, tn), lambda i, j: (0, j))
    o_spec = pl.BlockSpec((tm, tn), lambda i, j: (i, j))
    osh = jax.ShapeDtypeStruct((T, D_MODEL), F32)
    gates = [pl.BlockSpec((tm, tn), functools.partial(lambda i, j, o: (i, o + j), o=gb + 2 * n)) for n in range(3)]
    return pl.pallas_call(
        body, name=name, grid=(T // tm, D_MODEL // tn),
        in_specs=[a_spec, a_spec, a_spec, p_spec, p_spec, p_spec, *gates],
        out_specs=[o_spec] * 4, out_shape=[jax.ShapeDtypeStruct((T, D_MODEL), BF16)] * 4,
        compiler_params=_params(("parallel", "parallel")),
    )(a1, a2, a3, p1, p2, p3, proj, proj, proj)


def _merge_bwd_epi(dm, r1, r2, r3, g1, g2, g3):
    d_r, d_g = [], []
    for r, g in ((r1, g1), (r2, g2), (r3, g3)):
        s = jax.nn.sigmoid(g)
        d_r.append(dm * s)
        d_g.append(dm * r.astype(F32) * (s * (1.0 - s)))
    return (*d_r, *d_g)


def _rows_call(fn, ins, out_dtypes, *, name, tr=256):
    first = ins[0][0] if isinstance(ins[0], tuple) else ins[0]
    R, C = first.shape[-2:]
    tr = min(tr, R)
    assert R % tr == 0, (name, R, tr)
    arrs, specs = [], []
    for x in ins:
        if isinstance(x, tuple):
            arrs.append(x[0])
            specs.append(pl.BlockSpec((None, tr, C), functools.partial(lambda i, n: (n, i, 0), n=x[1])))
        else:
            arrs.append(x)
            specs.append(pl.BlockSpec((tr, C), lambda i: (i, 0)))
    ni = len(arrs)

    def body(*refs):
        vals = fn(*[r[...] for r in refs[:ni]])
        for o_ref, v in zip(refs[ni:], vals):
            o_ref[...] = v.astype(o_ref.dtype)

    res = pl.pallas_call(
        body, name=name, grid=(R // tr,), in_specs=specs,
        out_specs=[pl.BlockSpec((tr, C), lambda i: (i, 0)) for _ in out_dtypes],
        out_shape=[jax.ShapeDtypeStruct((R, C), dt) for dt in out_dtypes],
        compiler_params=_params(("parallel",)),
    )(*arrs)
    return res


def _tile_rows(rows, cols):
    t = 256
    while t > 8 and (t * cols > 512 * 1024 or rows % t):
        t //= 2
    return t


def _rows_at(fn, pos, ins, outs, steps, *, name, aliases=None):
    read = [n for n, (_, s) in enumerate(ins) if s is not ANY]
    ni = len(ins)

    def body(pos_ref, *refs):
        vals = fn(*[refs[n][...] for n in read])
        for o_ref, v in zip(refs[ni:], vals):
            o_ref[...] = v.astype(o_ref.dtype)

    return pl.pallas_call(
        body, name=name,
        grid_spec=pltpu.PrefetchScalarGridSpec(num_scalar_prefetch=1, grid=(steps,), in_specs=[s for _, s in ins],
                                               out_specs=[s for _, s in outs]),
        out_shape=[sh for sh, _ in outs],
        input_output_aliases={1 + i: o for i, o in (aliases or {}).items()},
        compiler_params=_params(("parallel",)),
    )(pos, *[a for a, _ in ins])


def cast_into_whole(pos, w, l, axis, *, name):
    _, r, n = w.shape
    tr = _tile_rows(r, n)
    if axis == 1:
        shape, spec = (r, n * N_CHIPS), pl.BlockSpec((tr, n), lambda i, p: (i, p[3]))
    else:
        shape, spec = (r * N_CHIPS, n), pl.BlockSpec((tr, n), lambda i, p: (p[3] * (r // tr) + i, 0))
    return _rows_at(lambda a: (a,), pos, [(w, pl.BlockSpec((None, tr, n), lambda i, p: (l, i, 0)))],
                    [(jax.ShapeDtypeStruct(shape, BF16), spec)], r // tr, name=name)[0]


def pair_sum(pos, theirs, g32, axis, *, name):
    rows2, cols = theirs.shape
    h = rows2 // (N_CHIPS if axis == 0 else 1)
    tr = _tile_rows(h, cols)
    hb = h // tr
    if axis == 1:
        own = pl.BlockSpec((tr, cols), lambda i, p: (p[2] * hb + i, 0))
    else:
        own = pl.BlockSpec((tr, cols), lambda i, p: ((2 * (i // hb) + p[2]) * hb + i % hb, 0))
    row = pl.BlockSpec((tr, cols), lambda i, p: (i, 0))
    return _rows_at(lambda t, m: (m + t.astype(F32),) * 2, pos, [(theirs, row), (g32, own)],
                    [(jax.ShapeDtypeStruct((rows2, cols), F32), row), (jax.ShapeDtypeStruct((rows2, cols), BF16), row)],
                    rows2 // tr, name=name)


def chip_sum(pos, h32, recv, l, axis, whole, *, name):
    _, depth, h, n = recv.shape
    tr = _tile_rows(h, n)
    hb = h // tr
    if axis == 1:
        mine = pl.BlockSpec((tr, n), lambda i, p: (i, p[3]))
    else:
        mine = pl.BlockSpec((tr, n), lambda i, p: (p[3] * hb + i, 0))
    ins = [(h32, mine)] + [(recv, pl.BlockSpec((None, None, tr, n), functools.partial(lambda i, p, j: (j, l, i, 0), j=j)))
                           for j in range(3)]
    if whole is not None:
        ins.append((whole, ANY))
    return _rows_at(lambda o, a, b, c: (((o + a.astype(F32)) + b.astype(F32)) + c.astype(F32),), pos, ins,
                    [(jax.ShapeDtypeStruct((depth, 2, h, n), F32), pl.BlockSpec((None, None, tr, n), lambda i, p: (l, p[2], i, 0)))],
                    hb, name=name, aliases=None if whole is None else {4: 0})[0]


def _adamw(w, g, m, v):
    m2 = ADAM_B1 * m + (1.0 - ADAM_B1) * g
    v2 = ADAM_B2 * v + (1.0 - ADAM_B2) * (g * g)
    m_hat = m2 / (1.0 - ADAM_B1 ** ADAM_STEP)
    v_hat = v2 / (1.0 - ADAM_B2 ** ADAM_STEP)
    delta = -ADAM_LR * (m_hat / (jnp.sqrt(v_hat) + ADAM_EPS) + ADAM_WD * w)
    return delta, m2, v2


def _place():
    return lax.axis_index("x"), lax.axis_index("y"), lax.axis_index("c")


def _chip_peers(x, y, c):
    return [((1 - x, y, c), 2 * (1 - x) + y), ((x, 1 - y, c), 2 * x + 1 - y), ((1 - x, 1 - y, c), 2 * (1 - x) + 1 - y)]


def _shard_of(ref, axis, k, n):
    start = pl.multiple_of(k * n, 128)
    return ref.at[pl.ds(start, n), :] if axis == 0 else ref.at[:, pl.ds(start, n)]


ANY = pl.BlockSpec(memory_space=pl.ANY)


class CopyJob:
    def __init__(self, ins, out_shape, scratch, copies, aliases=None):
        self.ins, self.out_shape, self.scratch, self.copies = list(ins), list(out_shape), list(scratch), copies
        self.aliases = dict(aliases or {})

    def start(self, ins, outs, sems):
        local, remote, _, _ = self.copies(ins, outs, sems)
        for d in local + remote:
            d.start()

    def finish(self, ins, outs, sems):
        local, remote, arrivals, relays = self.copies(ins, outs, sems)
        for needs, sends, _ in relays:
            for d in needs:
                d.wait_recv()
            for d in sends:
                d.start()
        for d in arrivals + [d for _, _, arrives in relays for d in arrives]:
            d.wait_recv()
        for d in remote + [d for _, sends, _ in relays for d in sends]:
            d.wait_send()
        for d in local:
            d.wait()


def run_job(job, *, name):
    ni, no = len(job.ins), len(job.out_shape)

    def body(*refs):
        parts = refs[:ni], refs[ni:ni + no], refs[ni + no:]
        job.start(*parts)
        job.finish(*parts)

    return pl.pallas_call(
        body, name=name, in_specs=[ANY] * ni, out_specs=[ANY] * no, out_shape=job.out_shape,
        scratch_shapes=job.scratch, input_output_aliases=job.aliases,
    )(*job.ins)


def _job_args(job, n_in, n_out):
    if job is None:
        return dict(ins=[], in_specs=[], out_specs=[], out_shape=[], scratch=[], aliases={})
    return dict(ins=job.ins, in_specs=[ANY] * len(job.ins), out_specs=[ANY] * len(job.out_shape),
                out_shape=job.out_shape, scratch=job.scratch,
                aliases={n_in + i: n_out + o for i, o in job.aliases.items()})


def _hosting(body, job, n_in, n_out, n_scratch, grid):
    if job is None:
        return body
    ji, jo = len(job.ins), len(job.out_shape)
    grid = (grid,) if isinstance(grid, int) else tuple(grid)

    def at(ends):
        hit = None
        for ax, e in enumerate(ends):
            here = pl.program_id(ax) == e
            hit = here if hit is None else jnp.logical_and(hit, here)
        return hit

    def hosted(*refs):
        o = n_in + ji
        s = o + n_out + jo
        parts = refs[n_in:o], refs[o + n_out:s], refs[s + n_scratch:]

        @pl.when(at([0] * len(grid)))
        def _():
            job.start(*parts)

        body(*refs[:n_in], *refs[o:o + n_out], *refs[s:s + n_scratch])

        @pl.when(at([g - 1 for g in grid]))
        def _():
            job.finish(*parts)

    return hosted


def _job_sems(n_remote, n_local):
    return [pltpu.SemaphoreType.DMA((n_remote,)), pltpu.SemaphoreType.DMA((n_remote,)), pltpu.SemaphoreType.DMA((n_local,))]


def gather_job(shards, axes, chips=(0, 1, 2)):
    na = len(shards)

    def copies(ins, outs, sems):
        send, recv, _ = sems
        x, y, c = _place()
        k = 2 * x + y
        remote, relays = [], []
        for a in range(na):
            r = outs[a].shape[0] // (N_CHIPS if axes[a] == 0 else 1)
            n = outs[a].shape[axes[a]] // N_CHIPS
            half = r // 2

            def part(kk, cc, a=a, n=n, half=half):
                rows = pl.ds(pl.multiple_of(cc * half + (kk * n if axes[a] == 0 else 0), 8), half)
                return outs[a].at[rows, :] if axes[a] == 0 else outs[a].at[rows, pl.ds(pl.multiple_of(kk * n, 128), n)]

            needs, passes, lands = [], [], []
            for j, (peer, kp) in enumerate(_chip_peers(x, y, c)):
                if j not in chips:
                    continue
                s = 6 * a + j
                remote.append(pltpu.make_async_remote_copy(part(k, c), part(k, c), send.at[s], recv.at[s],
                                                           device_id=peer, device_id_type=MESH))
                needs.append(pltpu.make_async_remote_copy(part(kp, c), part(kp, c), send.at[s], recv.at[s],
                                                          device_id=peer, device_id_type=MESH))
                passes.append(pltpu.make_async_remote_copy(part(kp, c), part(kp, c), send.at[s + 3], recv.at[s + 3],
                                                           device_id=(x, y, 1 - c), device_id_type=MESH))
                lands.append(pltpu.make_async_remote_copy(part(kp, 1 - c), part(kp, 1 - c), send.at[s + 3], recv.at[s + 3],
                                                          device_id=(x, y, 1 - c), device_id_type=MESH))
            relays.append((needs, passes, lands))
        return [], remote, [], relays

    out_shape = [jax.ShapeDtypeStruct(w.shape, BF16) for w in shards]
    return CopyJob(shards, out_shape, _job_sems(6 * na, 1), copies, {a: a for a in range(na)})


def scatter_job(layers, g16, axes, filled, chips=(0, 1, 2)):
    na = len(axes)

    def shard_shape(a):
        r, c = g16[a].shape
        return (r // N_CHIPS, c) if axes[a] == 0 else (r, c // N_CHIPS)

    def copies(ins, outs, sems):
        send, recv_sems, _ = sems
        x, y, c = _place()
        remote = []
        for a in range(na):
            n = shard_shape(a)[axes[a]]
            for r, (peer, kp) in enumerate(_chip_peers(x, y, c)):
                if r not in chips:
                    continue
                remote.append(pltpu.make_async_remote_copy(_shard_of(ins[a], axes[a], kp, n), outs[a].at[r, layers[a]],
                                                           send.at[3 * a + r], recv_sems.at[3 * a + r],
                                                           device_id=peer, device_id_type=MESH))
        return [], remote, remote, []

    out_shape = [jax.ShapeDtypeStruct((3, DEPTH) + shard_shape(a), BF16) for a in range(na)]
    ins = list(g16)
    aliases = {}
    for a in range(na):
        if filled[a] is not None:
            aliases[len(ins)] = a
            ins.append(filled[a])
    return CopyJob(ins, out_shape, _job_sems(3 * na, 1), copies, aliases)


def pair_job(g16, axes):
    na = len(axes)
    pieces = [1 if ax == 1 else N_CHIPS for ax in axes]

    def copies(ins, outs, sems):
        send, recv, _ = sems
        x, y, c = _place()
        remote = []
        s = 0
        for a in range(na):
            rows = g16[a].shape[0] // (2 * pieces[a])
            for kk in range(pieces[a]):
                src = ins[a].at[pl.ds(pl.multiple_of((2 * kk + 1 - c) * rows, 8), rows), :]
                remote.append(pltpu.make_async_remote_copy(src, outs[a].at[pl.ds(kk * rows, rows), :], send.at[s], recv.at[s],
                                                           device_id=(x, y, 1 - c), device_id_type=MESH))
                s += 1
        return [], remote, remote, []

    out_shape = [jax.ShapeDtypeStruct((g.shape[0] // 2, g.shape[1]), BF16) for g in g16]
    return CopyJob(g16, out_shape, _job_sems(sum(pieces), 1), copies)


def join_job(shards):
    na = len(shards)

    def copies(ins, outs, sems):
        send, recv, _ = sems
        x, y, c = _place()
        remote = [pltpu.make_async_remote_copy(outs[a].at[:, c], outs[a].at[:, c], send.at[a], recv.at[a],
                                               device_id=(x, y, 1 - c), device_id_type=MESH) for a in range(na)]
        lands = [pltpu.make_async_remote_copy(outs[a].at[:, 1 - c], outs[a].at[:, 1 - c], send.at[a], recv.at[a],
                                              device_id=(x, y, 1 - c), device_id_type=MESH) for a in range(na)]
        return [], remote, lands, []

    out_shape = [jax.ShapeDtypeStruct(s.shape, F32) for s in shards]
    return CopyJob(shards, out_shape, _job_sems(na, 1), copies, {a: a for a in range(na)})


def small_job(p):
    def copies(ins, outs, sems):
        send, recv, loc = sems
        x, y, c = _place()
        me = 4 * x + 2 * y + c
        remote, lands = [], []
        for rel in range(1, 8):
            dx, dy, dc = rel >> 2, (rel >> 1) & 1, rel & 1
            peer = (1 - x if dx else x, 1 - y if dy else y, 1 - c if dc else c)
            who = 4 * peer[0] + 2 * peer[1] + peer[2]
            remote.append(pltpu.make_async_remote_copy(ins[0], outs[0].at[me], send.at[rel - 1], recv.at[rel - 1],
                                                       device_id=peer, device_id_type=MESH))
            lands.append(pltpu.make_async_remote_copy(ins[0], outs[0].at[who], send.at[rel - 1], recv.at[rel - 1],
                                                      device_id=peer, device_id_type=MESH))
        return [pltpu.make_async_copy(ins[0], outs[0].at[me], loc.at[0])], remote, lands, []

    return CopyJob([p], [jax.ShapeDtypeStruct((8,) + p.shape, F32)], _job_sems(7, 1), copies)


def small_sum(slots):
    def add(*terms):
        acc = terms[0]
        for t in terms[1:]:
            acc = acc + t
        return (acc,)

    return _rows_call(add, [(slots, d) for d in range(8)], [F32], name="small_sum", tr=8 * 47)[0]


BIG = ("w_in", "p_ret", "p_sb", "p_sgu", "w_out", "w_up", "w_down")
BIG_AXIS = {"w_in": 1, "p_ret": 1, "p_sb": 1, "p_sgu": 1, "w_out": 0, "w_up": 1, "w_down": 0}
SMALL = ("ret_gn_g", "ret_gn_b", "sgu_ln_g", "sgu_ln_b", "sgu_w", "sgu_b", "ln1_g", "ln1_b", "ln2_g", "ln2_b")


def layer_forward(l, x0, W, sm, rope, rconsts, hooks):
    n = f"l{l}_"
    job = hooks.fwd_job(l, "proj")
    proj = matmul(x0, W["w_in"], mode="nn", tm=2048, tn=768, tk=1024, name=n + "proj", job=job)
    if job is not None:
        proj, job_out = proj
        hooks.done(job, job_out)
    retg, raw, states = ret_fwd(proj, *rope, rconsts, sm["ret_gn_g"], sm["ret_gn_b"], name=n + "ret_fwd")
    job = hooks.fwd_job(l, "sb")
    sb, job_out = sb_fwd(proj, name=n + "sb_fwd", job=job)
    if job is not None:
        hooks.done(job, job_out)
    sg = sgu_fwd(proj, sm["sgu_ln_g"], sm["sgu_ln_b"], sm["sgu_w"], sm["sgu_bias"], name=n + "sgu_fwd")
    merged, r1, r2, r3 = merge_fwd(retg, sb, sg, W["p_ret"], W["p_sb"], W["p_sgu"], proj, name=n + "merge_fwd")
    x1, xh1, rs1 = matmul_ln(merged, W["w_out"], x0, sm["ln1_g"], sm["ln1_b"], tk=1024, name=n + "out_ln1")
    job = hooks.fwd_job(l, "up")
    h1 = matmul(x1, W["w_up"], mode="nn", tm=1024, tn=1024, tk=1024, outs=((BF16, None),), name=n + "up", job=job)
    if job is not None:
        h1, job_out = h1
        hooks.done(job, job_out)
    job = hooks.fwd_job(l, "down")
    res = matmul_ln(h1, W["w_down"], x1, sm["ln2_g"], sm["ln2_b"], pro=_relu2, tk=1024, name=n + "down_ln2", job=job)
    if job is not None:
        res, job_out = res
        hooks.done(job, job_out)
    x2, xh2, rs2 = res
    saved = dict(x0=x0, proj=proj, retg=retg, raw=raw, states=states, sb=sb, sg=sg, merged=merged, r=(r1, r2, r3),
                 x1=x1, xh1=xh1, rs1=rs1, h1=h1, xh2=xh2, rs2=rs2)
    return x2, saved


def layer_backward(l, dx2, s, W, sm, rope, rconsts, hooks):
    n = f"l{l}_"
    two = ((F32, None), (BF16, None))
    gw, gs = {}, {}
    job = hooks.bwd_job(l, "ln2")
    res = ln_bwd(dx2, s["xh2"], s["rs2"], sm["ln2_g"], name=n + "ln2_bwd", job=job)
    if job is not None:
        res, job_out = res
        hooks.done(job, job_out)
    du2, du2h, gs["ln2_g"], gs["ln2_b"] = res
    job = hooks.bwd_job(l, "g_down")
    gw["w_down"] = matmul(s["h1"], du2h, mode="tn", tm=1024, tn=1024, tk=2048, pro=_relu2, outs=two, name=n + "g_down", job=job)
    if job is not None:
        gw["w_down"], job_out = gw["w_down"]
        hooks.done(job, job_out)
    dh1 = matmul(du2h, W["w_down"], mode="nt", tm=1024, tn=1024, tk=1024, outs=((BF16, None),),
                 epi=lambda acc, h: (acc * (2.0 * jnp.maximum(h.astype(F32), 0.0)),), tiles=(s["h1"],), name=n + "d_h1")
    job = hooks.bwd_job(l, "g_up")
    gw["w_up"] = matmul(s["x1"], dh1, mode="tn", tm=1024, tn=1024, tk=2048, outs=two, name=n + "g_up", job=job)
    if job is not None:
        gw["w_up"], job_out = gw["w_up"]
        hooks.done(job, job_out)
    dx1 = matmul(dh1, W["w_up"], mode="nt", tm=1024, tn=1024, tk=2048,
                 epi=lambda acc, d: (acc + ALPHA * d,), tiles=(du2,), name=n + "d_x1")
    du1, du1h, gs["ln1_g"], gs["ln1_b"] = ln_bwd(dx1, s["xh1"], s["rs1"], sm["ln1_g"], name=n + "ln1_bwd")
    gw["w_out"] = matmul(s["merged"], du1h, mode="tn", tm=1024, tn=1024, tk=2048, outs=two, name=n + "g_out")
    gate0 = C_GATE // 512
    dr1, dr2, dr3, dg1, dg2, dg3 = matmul(
        du1h, W["w_out"], mode="nt", tm=1024, tn=512, tk=1024, outs=((BF16, None),) * 6, epi=_merge_bwd_epi,
        tiles=(*s["r"], (s["proj"], gate0), (s["proj"], gate0 + 2), (s["proj"], gate0 + 4)), name=n + "d_merged")
    d_branch = {}
    for nm, a, dr in (("p_ret", s["retg"], dr1), ("p_sb", s["sb"], dr2), ("p_sgu", s["sg"], dr3)):
        gw[nm] = matmul(a, dr, mode="tn", tm=512, tn=1024, tk=2048, outs=two, name=n + "g_" + nm)
        d_branch[nm] = matmul(dr, W[nm], mode="nt", tm=1024, tn=512, tk=1024, name=n + "d_" + nm)
    job = hooks.pair(l, gw)
    dret, gs["ret_gn_g"], gs["ret_gn_b"], job_out = ret_bwd(s["proj"], *rope, rconsts, sm["ret_gn_g"], sm["ret_gn_b"],
                                                             s["raw"], s["states"], d_branch["p_ret"], name=n + "ret_bwd", job=job)
    if job is not None:
        hooks.done(job, job_out)
    job = hooks.scatter(l) if job is not None else None
    dsq, dsk, dsv, job_out = sb_bwd(s["proj"], s["sb"], d_branch["p_sb"], name=n + "sb_bwd", job=job)
    if job is not None:
        hooks.done(job, job_out)
    dsgu, gs["sgu_w"], dbias, gs["sgu_ln_g"], gs["sgu_ln_b"] = sgu_bwd(
        s["proj"], sm["sgu_ln_g"], sm["sgu_ln_b"], sm["sgu_w"], sm["sgu_bias"], d_branch["p_sgu"], name=n + "sgu_bwd")
    gs["sgu_b"] = dbias[:, :, 0]
    dproj = jnp.concatenate([dret, dsq, dsk, dsv, dsgu, dg1, dg2, dg3], axis=1)
    job = hooks.small(l, gs)
    gw["w_in"] = matmul(s["x0"], dproj, mode="tn", tm=1024, tn=1536, tk=1024, outs=two, name=n + "g_in", job=job)
    if job is not None:
        gw["w_in"], job_out = gw["w_in"]
        hooks.done(job, job_out)
    job = hooks.tail(l, gw["w_in"])
    dx0 = matmul(dproj, W["w_in"], mode="nt", tm=1024, tn=1024, tk=2560,
                 epi=lambda acc, d: (acc + ALPHA * d,), tiles=(du1,), name=n + "d_x0", job=job)
    if job is not None:
        dx0, job_out = dx0
        hooks.done(job, job_out)
    return dx0, gw, gs


def local_step(x, target, small, plan):
    T = x.shape[0]
    rope = _rope_tables(T)
    rconsts = _ret_consts()
    sms = []
    for l in range(DEPTH):
        sm = {k: small[k][l][None, :] for k in SMALL if k not in ("sgu_w", "sgu_b")}
        sm["sgu_w"] = small["sgu_w"][l]
        sm["sgu_bias"] = jnp.broadcast_to(small["sgu_b"][l][:, :, None], (4, CHUNK, CHUNK))
        sms.append(sm)
    h, saved = x, []
    for l in range(DEPTH):
        h, s = layer_forward(l, h, plan.weights(l), sms[l], rope, rconsts, plan)
        saved.append(s)
    dy, sq = loss_head(h, target)
    gs = {k: [None] * DEPTH for k in SMALL}
    for l in reversed(range(DEPTH)):
        dy, gwl, gsl = layer_backward(l, dy, saved[l], plan.weights(l), sms[l], rope, rconsts, plan)
        plan.grads(l, gwl)
        for k in SMALL:
            gs[k][l] = gsl[k].reshape(small[k].shape[1:])
    return sq[0, 0], dy, {k: jnp.stack(v) for k, v in gs.items()}


EARLY_GRADS = ("p_ret", "p_sb", "p_sgu", "w_out", "w_up", "w_down")


class _StepPlan:
    def __init__(self, pos, shards16):
        self.pos = pos
        self.shards16 = shards16
        self.full = [dict() for _ in range(DEPTH)]
        self.gw = [None] * DEPTH
        self.bufs = {}
        self.sums = {}
        self.gs = [None] * DEPTH
        first = self._gather([(0, "w_in")])
        self.done(first, run_job(first, name="gather_first"))

    def weights(self, l):
        return self.full[l]

    def grads(self, l, gw):
        self.gw[l] = gw

    def _gather(self, items, chips=(0, 1, 2)):
        job = gather_job([self.shards16[l][k] for l, k in items], [BIG_AXIS[k] for _, k in items], chips)
        job.note = ("gather" if 2 in chips else "gather_part", items)
        return job

    def _pair(self, items):
        job = pair_job([g[1] for _, _, g in items], [BIG_AXIS[k] for _, k, _ in items])
        job.note = ("pair", items)
        return job

    def fwd_job(self, l, host):
        if host == "proj":
            return None
        if host == "sb":
            return self._gather([(l, k) for k in BIG[1:]])
        if l + 1 == DEPTH:
            return None
        return self._gather([(l + 1, "w_in")], (0, 1) if host == "up" else (2,))

    def bwd_job(self, l, host):
        if l + 1 == DEPTH:
            return None
        if host == "ln2":
            job = self._pair([(l + 1, "w_in", self.gw[l + 1]["w_in"])])
            job.note = ("pair_w_in", job.note[1])
            return job
        items, sums16 = self.summed_w_in
        job = scatter_job([l_ for l_, _, _ in items], sums16, [BIG_AXIS[k] for _, k, _ in items],
                          [self.bufs.get(k) for _, k, _ in items], (0, 1) if host == "g_down" else (2,))
        job.note = ("scatter", items)
        return job

    def pair(self, l, ready):
        return self._pair([(l, k, ready[k]) for k in EARLY_GRADS])

    def scatter(self, l):
        items, sums16 = self.summed
        job = scatter_job([l_ for l_, _, _ in items], sums16, [BIG_AXIS[k] for _, k, _ in items],
                          [self.bufs.get(k) for _, k, _ in items])
        job.note = ("scatter", items)
        return job

    def small(self, l, gs):
        self.gs[l] = {k: gs[k].reshape(-1) for k in SMALL}
        if l != 0:
            return None
        job = small_job(_pack_small({k: jnp.stack([self.gs[l_][k] for l_ in range(DEPTH)]) for k in SMALL}))
        job.note = ("small", [])
        return job

    def tail(self, l, g):
        if l != 0:
            return None
        last = self._pair([(0, "w_in", g)])
        self.done(last, run_job(last, name="pair_last"))
        return self.scatter(0)

    def done(self, job, outs):
        kind, items = job.note
        if kind == "small":
            self.small_slots = outs[0]
        if kind in ("pair", "pair_w_in"):
            sums16 = []
            for a, (l, k, g) in enumerate(items):
                self.sums[(l, k)], s16 = pair_sum(self.pos, outs[a], g[0], BIG_AXIS[k], name=f"pair_sum_{k}_{l}")
                sums16.append(s16)
            if kind == "pair":
                self.summed = (items, sums16)
            else:
                self.summed_w_in = (items, sums16)
        for a, item in enumerate(items):
            if kind == "gather_part":
                self.shards16[item[0]][item[1]] = outs[a]
            elif kind == "gather":
                self.full[item[0]][item[1]] = outs[a]
            elif kind == "scatter":
                self.bufs[item[1]] = outs[a]

    def finish(self):
        return self.bufs, self.sums


def _flat2(a):
    return a.reshape(-1, a.shape[-1])


def _pack_small(d, pre=""):
    return jnp.concatenate([d[pre + k].reshape(-1) for k in SMALL]).reshape(-1, 128)


def kernel(x, w_in, ret_gn_g, ret_gn_b, sgu_ln_g, sgu_ln_b, sgu_w, sgu_b, p_ret, p_sb, p_sgu, w_out, ln1_g, ln1_b, w_up, w_down, ln2_g, ln2_b, loss_target, m_w_in, m_ret_gn_g, m_ret_gn_b, m_sgu_ln_g, m_sgu_ln_b, m_sgu_w, m_sgu_b, m_p_ret, m_p_sb, m_p_sgu, m_w_out, m_ln1_g, m_ln1_b, m_w_up, m_w_down, m_ln2_g, m_ln2_b, v_w_in, v_ret_gn_g, v_ret_gn_b, v_sgu_ln_g, v_sgu_ln_b, v_sgu_w, v_sgu_b, v_p_ret, v_p_sb, v_p_sgu, v_w_out, v_ln1_g, v_ln1_b, v_w_up, v_w_down, v_ln2_g, v_ln2_b):
    given = dict(locals())
    order = BIG[:1] + SMALL[:6] + BIG[1:5] + SMALL[6:8] + BIG[5:7] + SMALL[8:10]
    L = DEPTH

    px, py, pc = _place()
    pos = jnp.stack([px, py, pc, 2 * px + py]).astype(jnp.int32)

    shards16 = [{k: cast_into_whole(pos, given[k], l, BIG_AXIS[k], name=f"cast_{k}_{l}") for k in BIG} for l in range(L)]
    plan = _StepPlan(pos, shards16)
    sq, dx, gs = local_step(x[0], loss_target[0], {k: given[k] for k in SMALL}, plan)
    loss = 0.5 * lax.psum(sq, ("x", "y", "c"))

    bufs, sums = plan.finish()
    shards = []
    for k in BIG:
        whole = None
        for l in range(L):
            whole = chip_sum(pos, sums[(l, k)], bufs[k], l, BIG_AXIS[k], whole, name=f"chip_sum_{k}_{l}")
        shards.append(whole)
    joined = run_job(join_job(shards), name="join_halves")
    out = {}
    for a, k in enumerate(BIG):
        shp = given[k].shape
        res = _rows_call(lambda g_, w_, m_, v_: (g_,) + _adamw(w_, g_, m_, v_),
                         [joined[a].reshape(-1, shp[-1]), _flat2(given[k]), _flat2(given["m_" + k]), _flat2(given["v_" + k])],
                         [F32] * 4, name="adamw_" + k)
        out[k] = [r.reshape(shp) for r in res]

    pack = _pack_small
    res = _rows_call(lambda g_, w_, m_, v_: (g_,) + _adamw(w_, g_, m_, v_),
                     [small_sum(plan.small_slots), pack(given), pack(given, "m_"), pack(given, "v_")], [F32] * 4,
                     name="adamw_small", tr=8 * 47)
    off = 0
    for k in SMALL:
        sz = given[k].size
        out[k] = [r.reshape(-1)[off:off + sz].reshape(given[k].shape) for r in res]
        off += sz

    grads = [out[k][0] for k in order]
    deltas = [out[k][1] for k in order]
    new_m = [out[k][2] for k in order]
    new_v = [out[k][3] for k in order]
    return (loss, dx[None], *grads, *deltas, *new_m, *new_v)
```

```python
import functools
import math

import jax
import jax.numpy as jnp
from jax import lax
from jax.experimental import pallas as pl
from jax.experimental.pallas import tpu as pltpu

F32 = jnp.float32
BF16 = jnp.bfloat16

D_MODEL = 1024
SEQ = 4096
DEPTH = 2
CHUNK = 128
RET_HEADS = 4
BRANCH_W = 512
N_IN = 7680
D_FF = 4096
LN_EPS = 1e-5
ROPE_BASE = 10000.0
ALPHA = (2 * DEPTH) ** 0.25
RET_SCALE = 128 ** -0.5
SB_SCALE = 64 ** -0.5
C_RET, C_SB, C_SGU, C_GATE = 0, 2048, 3584, 4608

ADAM_LR, ADAM_B1, ADAM_B2, ADAM_EPS, ADAM_WD, ADAM_STEP = 0.001, 0.9, 0.999, 1e-08, 0.01, 10

N_CHIPS = 4
VMEM_LIMIT = 56 * 1024 * 1024
MESH = pl.DeviceIdType.MESH

NN = ((1,), (0,))
NT = ((1,), (1,))
TN = ((0,), (0,))


def _dot(a, b, dims):
    return lax.dot_general(a, b, (dims, ((), ())), preferred_element_type=F32)


def _params(sem):
    return pltpu.CompilerParams(dimension_semantics=sem, vmem_limit_bytes=VMEM_LIMIT)


def _relu2(h):
    r = jnp.maximum(h.astype(F32), 0.0)
    return r * r


def matmul(a, b, *, mode, tm, tn, tk, outs=((F32, None),), pro=None, epi=None, tiles=(), rows=(), name, job=None):
    if mode == "nn":
        (M, K), N = a.shape, b.shape[1]
    elif mode == "nt":
        (M, K), N = a.shape, b.shape[0]
    else:
        (K, M), N = a.shape, b.shape[1]
    tm, tn, tk = min(tm, M), min(tn, N), min(tk, K)
    assert M % tm == 0 and N % tn == 0 and K % tk == 0, (name, M, N, K, tm, tn, tk)
    if mode == "nn":
        a_spec = pl.BlockSpec((tm, tk), lambda i, j, k: (i, k))
        b_spec = pl.BlockSpec((tk, tn), lambda i, j, k: (k, j))
        dims = NN
    elif mode == "nt":
        a_spec = pl.BlockSpec((tm, tk), lambda i, j, k: (i, k))
        b_spec = pl.BlockSpec((tn, tk), lambda i, j, k: (j, k))
        dims = NT
    else:
        a_spec = pl.BlockSpec((tk, tm), lambda i, j, k: (k, i))
        b_spec = pl.BlockSpec((tk, tn), lambda i, j, k: (k, j))
        dims = TN
    nk = K // tk
    nt_, nr, no = len(tiles), len(rows), len(outs)

    def body(a_ref, b_ref, *rest):
        tile_refs = rest[:nt_]
        row_refs = rest[nt_:nt_ + nr]
        out_refs = rest[nt_ + nr:nt_ + nr + no]
        av = a_ref[...]
        if pro is not None:
            av = pro(av)
        p = _dot(av.astype(BF16), b_ref[...].astype(BF16), dims)

        def finish(acc):
            vals = (acc,) * no if epi is None else epi(acc, *[r[...] for r in tile_refs], *[r[...] for r in row_refs])
            for o_ref, v in zip(out_refs, vals):
                o_ref[...] = v.astype(o_ref.dtype)

        if nk == 1:
            finish(p)
        else:
            acc_ref = rest[-1]
            k = pl.program_id(2)

            @pl.when(k == 0)
            def _():
                acc_ref[...] = p

            @pl.when(k > 0)
            def _():
                acc_ref[...] += p

            @pl.when(k == nk - 1)
            def _():
                finish(acc_ref[...])

    out_shape, out_specs = [], []
    for dt, width in outs:
        if width is None:
            out_shape.append(jax.ShapeDtypeStruct((M, N), dt))
            out_specs.append(pl.BlockSpec((tm, tn), lambda i, j, k: (i, j)))
        else:
            assert N == tn
            out_shape.append(jax.ShapeDtypeStruct((M, width), dt))
            out_specs.append(pl.BlockSpec((tm, width), lambda i, j, k: (i, 0)))
    in_specs = [a_spec, b_spec]
    offs = [t[1] if isinstance(t, tuple) else 0 for t in tiles]
    tiles = [t[0] if isinstance(t, tuple) else t for t in tiles]
    in_specs += [pl.BlockSpec((tm, tn), functools.partial(lambda i, j, k, o: (i, j + o), o=o)) for o in offs]
    in_specs += [pl.BlockSpec((1, tn), lambda i, j, k: (0, j)) for _ in rows]
    grid = (M // tm, N // tn, nk)
    scratch = [pltpu.VMEM((tm, tn), F32)] if nk > 1 else []
    j = _job_args(job, len(in_specs), no)
    res = pl.pallas_call(
        _hosting(body, job, len(in_specs), no, len(scratch), grid), name=name, grid=grid,
        in_specs=in_specs + j["in_specs"], out_specs=out_specs + j["out_specs"], out_shape=out_shape + j["out_shape"],
        scratch_shapes=scratch + j["scratch"], input_output_aliases=j["aliases"],
        compiler_params=_params(("parallel", "parallel", "arbitrary") if job is None else ("arbitrary",) * 3),
    )(a, b, *tiles, *rows, *j["ins"])
    mine = res[0] if no == 1 else list(res[:no])
    return mine if job is None else (mine, list(res[no:]))


def _ln_epi(acc, res, g, b):
    u = ALPHA * res + acc
    mu = jnp.mean(u, axis=-1, keepdims=True)
    xc = u - mu
    var = jnp.mean(xc * xc, axis=-1, keepdims=True)
    rstd = lax.rsqrt(var + LN_EPS)
    xhat = xc * rstd
    return xhat * g + b, xhat, jnp.broadcast_to(rstd, (u.shape[0], 128))


def matmul_ln(a, w, res, g, b, *, pro=None, tk, name, job=None):
    n = w.shape[1]
    return matmul(a, w, mode="nn", tm=1024, tn=n, tk=tk, pro=pro, epi=_ln_epi, tiles=(res,), rows=(g, b),
                  outs=((F32, None), (F32, None), (F32, 128)), name=name, job=job)


def ln_bwd(dy, xhat, rstd, g, *, name, job=None):
    T, D = dy.shape
    tm = min(512, T)

    def body(dy_ref, xh_ref, rs_ref, g_ref, du_ref, du16_ref, dg_ref, db_ref):
        dyv, xh = dy_ref[...], xh_ref[...]
        r = rs_ref[:, 0:1]
        dxh = dyv * g_ref[...]
        m1 = jnp.mean(dxh, axis=-1, keepdims=True)
        m2 = jnp.mean(dxh * xh, axis=-1, keepdims=True)
        du = r * (dxh - m1 - xh * m2)
        du_ref[...] = du
        du16_ref[...] = du.astype(BF16)

        @pl.when(pl.program_id(0) == 0)
        def _():
            dg_ref[...] = jnp.zeros_like(dg_ref)
            db_ref[...] = jnp.zeros_like(db_ref)

        dg_ref[...] += jnp.sum(dyv * xh, axis=0, keepdims=True)
        db_ref[...] += jnp.sum(dyv, axis=0, keepdims=True)

    row = pl.BlockSpec((tm, D), lambda i: (i, 0))
    vec = pl.BlockSpec((1, D), lambda i: (0, 0))
    j = _job_args(job, 4, 4)
    res = pl.pallas_call(
        _hosting(body, job, 4, 4, 0, T // tm), name=name, grid=(T // tm,),
        in_specs=[row, row, pl.BlockSpec((tm, 128), lambda i: (i, 0)), vec] + j["in_specs"],
        out_specs=[row, row, vec, vec] + j["out_specs"],
        out_shape=[jax.ShapeDtypeStruct((T, D), F32), jax.ShapeDtypeStruct((T, D), BF16),
                   jax.ShapeDtypeStruct((1, D), F32), jax.ShapeDtypeStruct((1, D), F32)] + j["out_shape"],
        scratch_shapes=j["scratch"], input_output_aliases=j["aliases"],
        compiler_params=_params(("arbitrary",)),
    )(dy, xhat, rstd, g, *j["ins"])
    return list(res[:4]) if job is None else (list(res[:4]), list(res[4:]))


def loss_head(y, target):
    T, D = y.shape
    tm = min(512, T)

    def body(y_ref, t_ref, dy_ref, s_ref):
        e = y_ref[...] - t_ref[...]
        dy_ref[...] = e * (1.0 / D)

        @pl.when(pl.program_id(0) == 0)
        def _():
            s_ref[...] = jnp.zeros_like(s_ref)

        s_ref[...] += jnp.sum(jnp.mean(e * e, axis=-1, keepdims=True))

    row = pl.BlockSpec((tm, D), lambda i: (i, 0))
    return pl.pallas_call(
        body, name="loss_head", grid=(T // tm,),
        in_specs=[row, row], out_specs=[row, pl.BlockSpec((8, 128), lambda i: (0, 0))],
        out_shape=[jax.ShapeDtypeStruct((T, D), F32), jax.ShapeDtypeStruct((8, 128), F32)],
        compiler_params=_params(("arbitrary",)),
    )(y, target)


def _rope_tables(T):
    half = 64
    inv_freq = ROPE_BASE ** (-jnp.arange(half, dtype=F32) / half)
    ang = jnp.arange(T, dtype=jnp.int32).astype(F32)[:, None] * inv_freq[None, :]
    cos, sin = jnp.cos(ang), jnp.sin(ang)
    return jnp.concatenate([cos, cos], axis=1), jnp.concatenate([-sin, sin], axis=1)


def _ret_consts():
    H = RET_HEADS
    log_g = jnp.log(1.0 - 2.0 ** (-5.0 - jnp.arange(H, dtype=F32)))
    idx = jnp.arange(CHUNK, dtype=F32)
    diff = idx[:, None] - idx[None, :]
    dmat = jnp.where(diff[None] >= 0, jnp.exp(log_g[:, None, None] * diff[None]), 0.0)
    kd = jnp.exp(log_g[:, None] * (CHUNK - 1 - idx)[None, :])
    qd = jnp.exp(log_g[:, None] * (idx + 1.0)[None, :])
    cd = jnp.exp(log_g * CHUNK)
    full = (H, CHUNK, CHUNK)
    return (dmat.astype(F32), jnp.broadcast_to(kd[:, :, None], full), jnp.broadcast_to(qd[:, :, None], full),
            jnp.broadcast_to(cd[:, None, None], full))


def _swap_halves(v):
    return pltpu.roll(v, 64, 1)


def _group_norm(o):
    mu = jnp.mean(o, axis=-1, keepdims=True)
    xc = o - mu
    var = jnp.mean(xc * xc, axis=-1, keepdims=True)
    rstd = lax.rsqrt(var + LN_EPS)
    return xc * rstd, rstd


def ret_fwd(proj, cosf, sinf, consts, gn_g, gn_b, *, name):
    T = proj.shape[0]
    tb = min(512, T)
    nch = tb // CHUNK
    H = RET_HEADS

    def body(p_ref, cos_ref, sin_ref, dm_ref, kd_ref, qd_ref, cd_ref, g_ref, b_ref, out_ref, raw_ref, st_ref, s_ref):
        @pl.when(pl.program_id(0) == 0)
        def _():
            s_ref[...] = jnp.zeros_like(s_ref)

        for c in range(nch):
            r = slice(c * CHUNK, (c + 1) * CHUNK)
            cs, sn = cos_ref[r, :], sin_ref[r, :]
            for h in range(H):
                hc = slice(h * 128, (h + 1) * 128)
                q = p_ref[r, h * 128:(h + 1) * 128]
                k = p_ref[r, 512 + h * 128:512 + (h + 1) * 128]
                v = p_ref[r, 1024 + h * 128:1024 + (h + 1) * 128]
                gt = p_ref[r, 1536 + h * 128:1536 + (h + 1) * 128]
                qr = q * cs + _swap_halves(q) * sn
                kr = (k * cs + _swap_halves(k) * sn) * RET_SCALE
                sprev = s_ref[h]
                st_ref[c, h] = sprev
                qb, kb, vb = qr.astype(BF16), kr.astype(BF16), v.astype(BF16)
                s = _dot(qb, kb, NT) * dm_ref[h]
                o = _dot(s.astype(BF16), vb, NN) + _dot((qr * qd_ref[h]).astype(BF16), sprev.astype(BF16), NN)
                s_ref[h] = sprev * cd_ref[h] + _dot((kr * kd_ref[h]).astype(BF16), vb, TN)
                raw_ref[r, hc] = o
                y, _ = _group_norm(o)
                out_ref[r, hc] = (gt * jax.nn.sigmoid(gt)) * (y * g_ref[:, hc] + b_ref[:, hc])

    cmat = pl.BlockSpec((H, CHUNK, CHUNK), lambda i: (0, 0, 0))
    vec = pl.BlockSpec((1, BRANCH_W), lambda i: (0, 0))
    rope = pl.BlockSpec((tb, 128), lambda i: (i, 0))
    blk = pl.BlockSpec((tb, BRANCH_W), lambda i: (i, 0))
    return pl.pallas_call(
        body, name=name, grid=(T // tb,),
        in_specs=[pl.BlockSpec((tb, 2048), lambda i: (i, 0)), rope, rope, cmat, cmat, cmat, cmat, vec, vec],
        out_specs=[blk, blk, pl.BlockSpec((nch, H, CHUNK, CHUNK), lambda i: (i, 0, 0, 0))],
        out_shape=[jax.ShapeDtypeStruct((T, BRANCH_W), F32), jax.ShapeDtypeStruct((T, BRANCH_W), F32),
                   jax.ShapeDtypeStruct((T // CHUNK, H, CHUNK, CHUNK), F32)],
        scratch_shapes=[pltpu.VMEM((H, CHUNK, CHUNK), F32)],
        compiler_params=_params(("arbitrary",)),
    )(proj, cosf, sinf, *consts, gn_g, gn_b)


def ret_bwd(proj, cosf, sinf, consts, gn_g, gn_b, raw, states, dout, *, name, job=None):
    T = proj.shape[0]
    tb = min(512, T)
    nch = tb // CHUNK
    nb = T // tb
    H = RET_HEADS

    def body(p_ref, cos_ref, sin_ref, dm_ref, kd_ref, qd_ref, cd_ref, g_ref, b_ref, raw_ref, st_ref, do_ref,
             dp_ref, dg_ref, db_ref, ds_ref):
        @pl.when(pl.program_id(0) == 0)
        def _():
            ds_ref[...] = jnp.zeros_like(ds_ref)
            dg_ref[...] = jnp.zeros_like(dg_ref)
            db_ref[...] = jnp.zeros_like(db_ref)

        for c in reversed(range(nch)):
            r = slice(c * CHUNK, (c + 1) * CHUNK)
            cs, sn = cos_ref[r, :], sin_ref[r, :]
            for h in range(H):
                hc = slice(h * 128, (h + 1) * 128)
                q = p_ref[r, h * 128:(h + 1) * 128]
                k = p_ref[r, 512 + h * 128:512 + (h + 1) * 128]
                v = p_ref[r, 1024 + h * 128:1024 + (h + 1) * 128]
                gt = p_ref[r, 1536 + h * 128:1536 + (h + 1) * 128]
                qr = q * cs + _swap_halves(q) * sn
                kr = (k * cs + _swap_halves(k) * sn) * RET_SCALE
                sprev = st_ref[c, h]
                gv = g_ref[:, hc]
                y, rstd = _group_norm(raw_ref[r, hc])
                d_out = do_ref[r, hc]
                sg = jax.nn.sigmoid(gt)
                d_gate = d_out * (y * gv + b_ref[:, hc]) * (sg * (1.0 + gt * (1.0 - sg)))
                d_aff = d_out * (gt * sg)
                dg_ref[:, hc] += jnp.sum(d_aff * y, axis=0, keepdims=True)
                db_ref[:, hc] += jnp.sum(d_aff, axis=0, keepdims=True)
                dxh = d_aff * gv
                m1 = jnp.mean(dxh, axis=-1, keepdims=True)
                m2 = jnp.mean(dxh * y, axis=-1, keepdims=True)
                d_o = (rstd * (dxh - m1 - y * m2)).astype(BF16)
                qb, kb, vb = qr.astype(BF16), kr.astype(BF16), v.astype(BF16)
                dm, kd, qd = dm_ref[h], kd_ref[h], qd_ref[h]
                p = (_dot(qb, kb, NT) * dm).astype(BF16)
                dp = (_dot(d_o, vb, NT) * dm).astype(BF16)
                dsn = ds_ref[h]
                dsb = dsn.astype(BF16)
                dq_r = _dot(dp, kb, NN) + _dot(d_o, sprev.astype(BF16), NT) * qd
                dk_r = (_dot(dp, qb, TN) + _dot(vb, dsb, NT) * kd) * RET_SCALE
                d_v = _dot(p, d_o, TN) + _dot((kr * kd).astype(BF16), dsb, NN)
                ds_ref[h] = dsn * cd_ref[h] + _dot((qr * qd).astype(BF16), d_o, TN)
                dp_ref[r, h * 128:(h + 1) * 128] = (dq_r * cs - _swap_halves(dq_r) * sn).astype(BF16)
                dp_ref[r, 512 + h * 128:512 + (h + 1) * 128] = (dk_r * cs - _swap_halves(dk_r) * sn).astype(BF16)
                dp_ref[r, 1024 + h * 128:1024 + (h + 1) * 128] = d_v.astype(BF16)
                dp_ref[r, 1536 + h * 128:1536 + (h + 1) * 128] = d_gate.astype(BF16)

    cmat = pl.BlockSpec((H, CHUNK, CHUNK), lambda i: (0, 0, 0))
    vec = pl.BlockSpec((1, BRANCH_W), lambda i: (0, 0))
    rope = pl.BlockSpec((tb, 128), lambda i: (nb - 1 - i, 0))
    blk = pl.BlockSpec((tb, BRANCH_W), lambda i: (nb - 1 - i, 0))
    wide = pl.BlockSpec((tb, 2048), lambda i: (nb - 1 - i, 0))
    j = _job_args(job, 12, 3)
    res = pl.pallas_call(
        _hosting(body, job, 12, 3, 1, nb), name=name, grid=(nb,),
        in_specs=[wide, rope, rope, cmat, cmat, cmat, cmat, vec, vec, blk,
                  pl.BlockSpec((nch, H, CHUNK, CHUNK), lambda i: (nb - 1 - i, 0, 0, 0)), blk] + j["in_specs"],
        out_specs=[wide, vec, vec] + j["out_specs"],
        out_shape=[jax.ShapeDtypeStruct((T, 2048), BF16), jax.ShapeDtypeStruct((1, BRANCH_W), F32),
                   jax.ShapeDtypeStruct((1, BRANCH_W), F32)] + j["out_shape"],
        scratch_shapes=[pltpu.VMEM((H, CHUNK, CHUNK), F32)] + j["scratch"], input_output_aliases=j["aliases"],
        compiler_params=_params(("arbitrary",)),
    )(proj, cosf, sinf, *consts, gn_g, gn_b, raw, states, dout, *j["ins"])
    return res[0], res[1], res[2], list(res[3:])


def _sb_masks():
    row = lax.broadcasted_iota(jnp.int32, (CHUNK, CHUNK), 0)
    lane = lax.broadcasted_iota(jnp.int32, (CHUNK, CHUNK), 1)
    return row, lane


SB_QT = 256
SB_DEAD = -105.0


def _pair(v):
    hi = v.astype(BF16)
    return jnp.concatenate([hi, (v - hi.astype(F32)).astype(BF16)], axis=1)


def _sb_consts():
    r = lax.broadcasted_iota(jnp.int32, (256, 256), 0) & 127
    c = lax.broadcasted_iota(jnp.int32, (256, 256), 1)
    ones = c >= 128
    lane = lax.broadcasted_iota(jnp.int32, (CHUNK, CHUNK), 1)
    return (ones | (r > c)).astype(BF16), (ones | (r >= c)).astype(BF16), (lane < 64, lane >= 64)


def _per_head(x, hms):
    return jnp.concatenate([jnp.where(hm, x, 0.0) for hm in hms], axis=0).astype(BF16)


def _sb_logits(qb, kb2, mask2):
    z = _dot(qb, kb2, NT)
    l1p = jnp.log(1.0 + jnp.exp(-jnp.abs(z)))
    lsp = jnp.minimum(z, 0.0) - l1p
    lsn = lsp - z
    if mask2 is not None:
        lsn = jnp.where(mask2, lsn, 0.0)
    return lsp, lsn


def _sb_tile_mask(qt):
    trow = lax.broadcasted_iota(jnp.int32, (qt, 256), 0)
    tlane = lax.broadcasted_iota(jnp.int32, (qt, 256), 1) & 127
    return lambda m: (tlane + m * CHUNK) < trow


def sb_fwd(proj, *, name, job=None):
    T = proj.shape[0]
    qt = min(SB_QT, T)
    nsub = qt // CHUNK
    cb = C_SB // 128

    def body(q_ref, k_ref, v_ref, o_ref):
        u_gt, _, hms = _sb_consts()
        tile_mask = _sb_tile_mask(qt)

        def qtile(i, _):
            rq = pl.ds(pl.multiple_of(i * qt, qt), qt)
            qb = (q_ref[rq, :] * SB_SCALE).astype(BF16)

            def group(js, masks, state):
                carry, acc = list(state[:2]), state[2]
                rows = [pl.ds(pl.multiple_of(j * CHUNK, CHUNK), CHUNK) for j in js]
                logits = [_sb_logits(qb, _per_head(k_ref[rk, :], hms), m) for rk, m in zip(rows, masks)]
                sums = [[_dot(_pair(lsn[:, h * 128:(h + 1) * 128]), u_gt, NN) for h in range(2)] for _, lsn in logits]
                weights = []
                for (lsp, _), r, m in zip(logits, sums, masks):
                    a_b = []
                    for h in range(2):
                        hc = slice(h * 128, (h + 1) * 128)
                        a = jnp.exp(lsp[:, hc] + r[h][:, :128] + carry[h])
                        if m is not None:
                            a = jnp.where(m[:, hc], a, 0.0)
                        carry[h] = carry[h] + r[h][:, 128:]
                        a_b.append(a.astype(BF16))
                    weights.append(jnp.concatenate(a_b, axis=1))
                for rk, a in zip(rows, weights):
                    acc = acc + _dot(a, _per_head(v_ref[rk, :], hms), NN)
                return carry[0], carry[1], acc

            zero = jnp.zeros((qt, 128), F32)
            diag = list(reversed(range(nsub)))
            state = group([i * nsub + m for m in diag], [tile_mask(m) for m in diag], (zero, zero, zero))

            def live(c):
                return jnp.logical_and(c[0] < i, jnp.maximum(jnp.max(c[1][0]), jnp.max(c[1][1])) > SB_DEAD)

            def blocks(c):
                jj, st = c
                return jj + 1, group([(i - jj) * nsub - 1 - u for u in range(nsub)], [None] * nsub, st)

            _, state = lax.while_loop(live, blocks, (jnp.int32(0), state))
            o_ref[rq, :] = state[2]
            return 0

        lax.fori_loop(0, T // qt, qtile, 0)

    def col(off):
        return pl.BlockSpec((T, 128), lambda hp: (0, off + hp))

    steps = BRANCH_W // 128
    j = _job_args(job, 3, 1)
    res = pl.pallas_call(
        _hosting(body, job, 3, 1, 0, steps), name=name, grid=(steps,),
        in_specs=[col(cb), col(cb + 4), col(cb + 8)] + j["in_specs"], out_specs=[col(0)] + j["out_specs"],
        out_shape=[jax.ShapeDtypeStruct((T, BRANCH_W), F32)] + j["out_shape"],
        scratch_shapes=j["scratch"], input_output_aliases=j["aliases"],
        compiler_params=_params(("parallel",) if job is None else ("arbitrary",)),
    )(proj, proj, proj, *j["ins"])
    return res[0], list(res[1:])


def sb_bwd(proj, out, dout, *, name, job=None):
    T = proj.shape[0]
    qt = min(SB_QT, T)
    nsub = qt // CHUNK
    cb = C_SB // 128

    def body(q_ref, k_ref, v_ref, o_ref, do_ref, dq_ref, dk_ref, dv_ref, dkt_ref, dvt_ref):
        u_gt, u_ge, hms = _sb_consts()
        tile_mask = _sb_tile_mask(qt)
        tall_lane = lax.broadcasted_iota(jnp.int32, (qt, 128), 1)
        top = lax.broadcasted_iota(jnp.int32, (CHUNK, CHUNK), 0) < 64
        dkt_ref[...] = jnp.zeros_like(dkt_ref)
        dvt_ref[...] = jnp.zeros_like(dvt_ref)

        def qtile(i, _):
            rq = pl.ds(pl.multiple_of(i * qt, qt), qt)
            qs = q_ref[rq, :] * SB_SCALE
            qb, q_t = qs.astype(BF16), qs.T.astype(BF16)
            dov = do_ref[rq, :]
            dob, do_t = dov.astype(BF16), dov.T.astype(BF16)
            prod = dob.astype(F32) * o_ref[rq, :]
            total = [jnp.broadcast_to(jnp.sum(jnp.where(hm, prod, 0.0), axis=1, keepdims=True), (qt, 128))
                     for hm in (tall_lane < 64, tall_lane >= 64)]

            def group(js, masks, state):
                c_l, c_w, dq = list(state[:2]), list(state[2:4]), state[4]
                heads = [slice(h * 128, (h + 1) * 128) for h in range(2)]
                rows = [pl.ds(pl.multiple_of(j * CHUNK, CHUNK), CHUNK) for j in js]
                kb2 = [_per_head(k_ref[rk, :], hms) for rk in rows]
                logits = [_sb_logits(qb, kb, m) for kb, m in zip(kb2, masks)]
                da = [_dot(dob, _per_head(v_ref[rk, :], hms), NT) for rk in rows]
                sums = [[_dot(_pair(lsn[:, hc]), u_gt, NN) for hc in heads] for _, lsn in logits]
                a_b, w_all = [], []
                for (lsp, _), r, d, m in zip(logits, sums, da, masks):
                    a_h, w_h = [], []
                    for h, hc in enumerate(heads):
                        a = jnp.exp(lsp[:, hc] + r[h][:, :128] + c_l[h])
                        if m is not None:
                            a = jnp.where(m[:, hc], a, 0.0)
                        c_l[h] = c_l[h] + r[h][:, 128:]
                        a = a.astype(BF16)
                        a_h.append(a)
                        w_h.append(a.astype(F32) * d[:, hc])
                    a_b.append(jnp.concatenate(a_h, axis=1))
                    w_all.append(w_h)
                sums_w = [[_dot(_pair(w), u_ge, NN) for w in w_h] for w_h in w_all]
                dz_b = []
                for (lsp, _), w_h, r, m in zip(logits, w_all, sums_w, masks):
                    sp = jnp.exp(lsp)
                    dz_h = []
                    for h, hc in enumerate(heads):
                        later_w = r[h][:, :128] + c_w[h]
                        c_w[h] = c_w[h] + r[h][:, 128:]
                        dz = w_h[h] * (1.0 - sp[:, hc]) - sp[:, hc] * (total[h] - later_w)
                        if m is not None:
                            dz = jnp.where(m[:, hc], dz, 0.0)
                        dz_h.append(dz.astype(BF16))
                    dz_b.append(jnp.concatenate(dz_h, axis=1))
                for j, kb, a, dz in zip(js, kb2, a_b, dz_b):
                    dkt = _dot(q_t, dz, NN)
                    dvt = _dot(do_t, a, NN)
                    dkt_ref[j] += jnp.where(top, dkt[:, :128], dkt[:, 128:])
                    dvt_ref[j] += jnp.where(top, dvt[:, :128], dvt[:, 128:])
                    dq = dq + _dot(dz, kb, NN)
                return c_l[0], c_l[1], c_w[0], c_w[1], dq

            zero = jnp.zeros((qt, 128), F32)
            diag = list(reversed(range(nsub)))
            state = group([i * nsub + m for m in diag], [tile_mask(m) for m in diag], (zero,) * 5)

            def live(c):
                return jnp.logical_and(c[0] < i, jnp.maximum(jnp.max(c[1][0]), jnp.max(c[1][1])) > SB_DEAD)

            def blocks(c):
                jj, st = c
                return jj + 1, group([(i - jj) * nsub - 1 - u for u in range(nsub)], [None] * nsub, st)

            _, state = lax.while_loop(live, blocks, (jnp.int32(0), state))
            dq_ref[rq, :] = (state[4] * SB_SCALE).astype(BF16)
            return 0

        lax.fori_loop(0, T // qt, qtile, 0)

        def untranspose(jb, _):
            rk = pl.ds(pl.multiple_of(jb * CHUNK, CHUNK), CHUNK)
            dk_ref[rk, :] = dkt_ref[jb].T.astype(BF16)
            dv_ref[rk, :] = dvt_ref[jb].T.astype(BF16)
            return 0

        lax.fori_loop(0, T // CHUNK, untranspose, 0)

    def col(off):
        return pl.BlockSpec((T, 128), lambda hp: (0, off + hp))

    o16 = jax.ShapeDtypeStruct((T, BRANCH_W), BF16)
    steps = BRANCH_W // 128
    j = _job_args(job, 5, 3)
    acc = pltpu.VMEM((T // CHUNK, CHUNK, CHUNK), F32)
    res = pl.pallas_call(
        _hosting(body, job, 5, 3, 2, steps), name=name, grid=(steps,),
        in_specs=[col(cb), col(cb + 4), col(cb + 8), col(0), col(0)] + j["in_specs"],
        out_specs=[col(0), col(0), col(0)] + j["out_specs"], out_shape=[o16, o16, o16] + j["out_shape"],
        scratch_shapes=[acc, acc] + j["scratch"], input_output_aliases=j["aliases"],
        compiler_params=_params(("parallel",) if job is None else ("arbitrary",)),
    )(proj, proj, proj, out, dout, *j["ins"])
    return res[0], res[1], res[2], list(res[3:])


_G0 = math.sqrt(2.0 / math.pi)
_G1 = 0.044715


def _gelu(x):
    return 0.5 * x * (1.0 + jnp.tanh(_G0 * (x + _G1 * x * x * x)))


def _gelu_grad(x):
    t = jnp.tanh(_G0 * (x + _G1 * x * x * x))
    return 0.5 * (1.0 + t) + 0.5 * x * (1.0 - t * t) * (_G0 * (1.0 + 3.0 * _G1 * x * x))


def _tril():
    row, lane = _sb_masks()
    return row >= lane


def sgu_fwd(proj, ln_g, ln_b, w, bias, *, name):
    T = proj.shape[0]
    tb = min(512, T)
    G = BRANCH_W // 128

    def body(u_ref, v_ref, g_ref, b_ref, w_ref, bias_ref, o_ref):
        vv = _gelu(v_ref[...])
        xh, _ = _group_norm(vv)
        vn = (xh * g_ref[...] + b_ref[...]).astype(BF16)
        tril = _tril()
        for g in range(G):
            wg = jnp.where(tril, w_ref[g], 0.0).astype(BF16)
            gc = slice(g * 128, (g + 1) * 128)
            for c in range(tb // CHUNK):
                r = slice(c * CHUNK, (c + 1) * CHUNK)
                sv = _dot(wg, vn[r, gc], NN) + bias_ref[g]
                o_ref[r, gc] = _gelu(u_ref[r, gc]) * sv

    cu, cv = C_SGU // BRANCH_W, C_SGU // BRANCH_W + 1
    vec = pl.BlockSpec((1, BRANCH_W), lambda i: (0, 0))
    mat = pl.BlockSpec((G, CHUNK, CHUNK), lambda i: (0, 0, 0))
    return pl.pallas_call(
        body, name=name, grid=(T // tb,),
        in_specs=[pl.BlockSpec((tb, BRANCH_W), lambda i: (i, cu)), pl.BlockSpec((tb, BRANCH_W), lambda i: (i, cv)),
                  vec, vec, mat, mat],
        out_specs=pl.BlockSpec((tb, BRANCH_W), lambda i: (i, 0)),
        out_shape=jax.ShapeDtypeStruct((T, BRANCH_W), F32),
        compiler_params=_params(("parallel",)),
    )(proj, proj, ln_g, ln_b, w, bias)


def sgu_bwd(proj, ln_g, ln_b, w, bias, dout, *, name):
    T = proj.shape[0]
    tb = min(512, T)
    G = BRANCH_W // 128

    def body(u_ref, v_ref, g_ref, b_ref, w_ref, bias_ref, do_ref, dp_ref, dw_ref, dbias_ref, dg_ref, db_ref, dvn_ref):
        @pl.when(pl.program_id(0) == 0)
        def _():
            dw_ref[...] = jnp.zeros_like(dw_ref)
            dbias_ref[...] = jnp.zeros_like(dbias_ref)
            dg_ref[...] = jnp.zeros_like(dg_ref)
            db_ref[...] = jnp.zeros_like(db_ref)

        gv = v_ref[...]
        vv = _gelu(gv)
        xh, rstd = _group_norm(vv)
        vn = (xh * g_ref[...] + b_ref[...]).astype(BF16)
        tril = _tril()
        for g in range(G):
            wg = jnp.where(tril, w_ref[g], 0.0).astype(BF16)
            gc = slice(g * 128, (g + 1) * 128)
            for c in range(tb // CHUNK):
                r = slice(c * CHUNK, (c + 1) * CHUNK)
                vn_c = vn[r, gc]
                sv = _dot(wg, vn_c, NN) + bias_ref[g]
                gu = u_ref[r, gc]
                d_o = do_ref[r, gc]
                dp_ref[r, gc] = (d_o * sv * _gelu_grad(gu)).astype(BF16)
                dsv = d_o * _gelu(gu)
                dsv_b = dsv.astype(BF16)
                dvn_ref[r, gc] = _dot(wg, dsv_b, TN)
                dw_ref[g] += jnp.where(tril, _dot(dsv_b, vn_c, NT), 0.0)
                dbias_ref[g] += jnp.broadcast_to(jnp.sum(dsv, axis=1, keepdims=True), (CHUNK, CHUNK))
        dvn = dvn_ref[...]
        dg_ref[...] += jnp.sum(dvn * xh, axis=0, keepdims=True)
        db_ref[...] += jnp.sum(dvn, axis=0, keepdims=True)
        dxh = dvn * g_ref[...]
        m1 = jnp.mean(dxh, axis=-1, keepdims=True)
        m2 = jnp.mean(dxh * xh, axis=-1, keepdims=True)
        dp_ref[:, BRANCH_W:2 * BRANCH_W] = (rstd * (dxh - m1 - xh * m2) * _gelu_grad(gv)).astype(BF16)

    cu, cv = C_SGU // BRANCH_W, C_SGU // BRANCH_W + 1
    vec = pl.BlockSpec((1, BRANCH_W), lambda i: (0, 0))
    mat = pl.BlockSpec((G, CHUNK, CHUNK), lambda i: (0, 0, 0))
    blk = pl.BlockSpec((tb, BRANCH_W), lambda i: (i, 0))
    msh = jax.ShapeDtypeStruct((G, CHUNK, CHUNK), F32)
    vsh = jax.ShapeDtypeStruct((1, BRANCH_W), F32)
    return pl.pallas_call(
        body, name=name, grid=(T // tb,),
        in_specs=[pl.BlockSpec((tb, BRANCH_W), lambda i: (i, cu)), pl.BlockSpec((tb, BRANCH_W), lambda i: (i, cv)),
                  vec, vec, mat, mat, blk],
        out_specs=[pl.BlockSpec((tb, 2 * BRANCH_W), lambda i: (i, 0)), mat, mat, vec, vec],
        out_shape=[jax.ShapeDtypeStruct((T, 2 * BRANCH_W), BF16), msh, msh, vsh, vsh],
        scratch_shapes=[pltpu.VMEM((tb, BRANCH_W), F32)],
        compiler_params=_params(("arbitrary",)),
    )(proj, proj, ln_g, ln_b, w, bias, dout)


def merge_fwd(a1, a2, a3, p1, p2, p3, proj, *, name, job=None):
    T = a1.shape[0]
    tm, tn = min(1024, T), 512
    gb = C_GATE // tn

    def body(a1_ref, a2_ref, a3_ref, p1_ref, p2_ref, p3_ref, g1_ref, g2_ref, g3_ref, m_ref, r1_ref, r2_ref, r3_ref):
        m = None
        for a_ref, p_ref, g_ref, r_ref in ((a1_ref, p1_ref, g1_ref, r1_ref), (a2_ref, p2_ref, g2_ref, r2_ref),
                                           (a3_ref, p3_ref, g3_ref, r3_ref)):
            r = _dot(a_ref[...].astype(BF16), p_ref[...], NN)
            r_ref[...] = r.astype(r_ref.dtype)
            t = jax.nn.sigmoid(g_ref[...]) * r
            m = t if m is None else m + t
        m_ref[...] = m.astype(m_ref.dtype)

    a_spec = pl.BlockSpec((tm, BRANCH_W), lambda i, j: (i, 0))
    p_spec = pl.BlockSpec((BRANCH_W, tn), lambda i, j: (0, j))
    o_spec = pl.BlockSpec((tm, tn), lambda i, j: (i, j))
    gates = [pl.BlockSpec((tm, tn), functools.partial(lambda i, j, o: (i, o + j), o=gb + 2 * n)) for n in range(3)]
    grid = (T // tm, D_MODEL // tn)
    j = _job_args(job, 9, 4)
    res = pl.pallas_call(
        _hosting(body, job, 9, 4, 0, grid), name=name, grid=grid,
        in_specs=[a_spec, a_spec, a_spec, p_spec, p_spec, p_spec, *gates] + j["in_specs"],
        out_specs=[o_spec] * 4 + j["out_specs"],
        out_shape=[jax.ShapeDtypeStruct((T, D_MODEL), BF16)] * 4 + j["out_shape"],
        scratch_shapes=j["scratch"], input_output_aliases=j["aliases"],
        compiler_params=_params(("parallel", "parallel") if job is None else ("arbitrary", "arbitrary")),
    )(a1, a2, a3, p1, p2, p3, proj, proj, proj, *j["ins"])
    return list(res[:4]) if job is None else (list(res[:4]), list(res[4:]))


def _merge_bwd_epi(dm, r1, r2, r3, g1, g2, g3):
    d_r, d_g = [], []
    for r, g in ((r1, g1), (r2, g2), (r3, g3)):
        s = jax.nn.sigmoid(g)
        d_r.append(dm * s)
        d_g.append(dm * r.astype(F32) * (s * (1.0 - s)))
    return (*d_r, *d_g)


def _rows_call(fn, ins, out_dtypes, *, name, tr=256):
    first = ins[0][0] if isinstance(ins[0], tuple) else ins[0]
    R, C = first.shape[-2:]
    tr = min(tr, R)
    assert R % tr == 0, (name, R, tr)
    arrs, specs = [], []
    for x in ins:
        if isinstance(x, tuple):
            arrs.append(x[0])
            specs.append(pl.BlockSpec((None, tr, C), functools.partial(lambda i, n: (n, i, 0), n=x[1])))
        else:
            arrs.append(x)
            specs.append(pl.BlockSpec((tr, C), lambda i: (i, 0)))
    ni = len(arrs)

    def body(*refs):
        vals = fn(*[r[...] for r in refs[:ni]])
        for o_ref, v in zip(refs[ni:], vals):
            o_ref[...] = v.astype(o_ref.dtype)

    res = pl.pallas_call(
        body, name=name, grid=(R // tr,), in_specs=specs,
        out_specs=[pl.BlockSpec((tr, C), lambda i: (i, 0)) for _ in out_dtypes],
        out_shape=[jax.ShapeDtypeStruct((R, C), dt) for dt in out_dtypes],
        compiler_params=_params(("parallel",)),
    )(*arrs)
    return res


def _tile_rows(rows, cols):
    t = 256
    while t > 8 and (t * cols > 512 * 1024 or rows % t):
        t //= 2
    return t


def _rows_at(fn, pos, ins, outs, steps, *, name, aliases=None):
    read = [n for n, (_, s) in enumerate(ins) if s is not ANY]
    ni = len(ins)

    def body(pos_ref, *refs):
        vals = fn(*[refs[n][...] for n in read])
        for o_ref, v in zip(refs[ni:], vals):
            o_ref[...] = v.astype(o_ref.dtype)

    return pl.pallas_call(
        body, name=name,
        grid_spec=pltpu.PrefetchScalarGridSpec(num_scalar_prefetch=1, grid=(steps,), in_specs=[s for _, s in ins],
                                               out_specs=[s for _, s in outs]),
        out_shape=[sh for sh, _ in outs],
        input_output_aliases={1 + i: o for i, o in (aliases or {}).items()},
        compiler_params=_params(("parallel",)),
    )(pos, *[a for a, _ in ins])


def cast_into_whole(pos, w, l, axis, *, name):
    _, r, n = w.shape
    tr = _tile_rows(r, n)
    if axis == 1:
        shape, spec = (r, n * N_CHIPS), pl.BlockSpec((tr, n), lambda i, p: (i, p[3]))
    else:
        shape, spec = (r * N_CHIPS, n), pl.BlockSpec((tr, n), lambda i, p: (p[3] * (r // tr) + i, 0))
    return _rows_at(lambda a: (a,), pos, [(w, pl.BlockSpec((None, tr, n), lambda i, p: (l, i, 0)))],
                    [(jax.ShapeDtypeStruct(shape, BF16), spec)], r // tr, name=name)[0]


def pair_sum(pos, theirs, g32, axis, *, name):
    rows2, cols = theirs.shape
    h = rows2 // (N_CHIPS if axis == 0 else 1)
    tr = _tile_rows(h, cols)
    hb = h // tr
    if axis == 1:
        own = pl.BlockSpec((tr, cols), lambda i, p: (p[2] * hb + i, 0))
    else:
        own = pl.BlockSpec((tr, cols), lambda i, p: ((2 * (i // hb) + p[2]) * hb + i % hb, 0))
    row = pl.BlockSpec((tr, cols), lambda i, p: (i, 0))
    return _rows_at(lambda t, m: (m + t.astype(F32),) * 2, pos, [(theirs, row), (g32, own)],
                    [(jax.ShapeDtypeStruct((rows2, cols), F32), row), (jax.ShapeDtypeStruct((rows2, cols), BF16), row)],
                    rows2 // tr, name=name)


def chip_sum(pos, h32, recv, l, axis, whole, *, name):
    _, depth, h, n = recv.shape
    tr = _tile_rows(h, n)
    hb = h // tr
    if axis == 1:
        mine = pl.BlockSpec((tr, n), lambda i, p: (i, p[3]))
    else:
        mine = pl.BlockSpec((tr, n), lambda i, p: (p[3] * hb + i, 0))
    ins = [(h32, mine)] + [(recv, pl.BlockSpec((None, None, tr, n), functools.partial(lambda i, p, j: (j, l, i, 0), j=j)))
                           for j in range(3)]
    if whole is not None:
        ins.append((whole, ANY))
    return _rows_at(lambda o, a, b, c: (((o + a.astype(F32)) + b.astype(F32)) + c.astype(F32),), pos, ins,
                    [(jax.ShapeDtypeStruct((depth, 2, h, n), F32), pl.BlockSpec((None, None, tr, n), lambda i, p: (l, p[2], i, 0)))],
                    hb, name=name, aliases=None if whole is None else {4: 0})[0]


def _adamw(w, g, m, v):
    m2 = ADAM_B1 * m + (1.0 - ADAM_B1) * g
    v2 = ADAM_B2 * v + (1.0 - ADAM_B2) * (g * g)
    m_hat = m2 / (1.0 - ADAM_B1 ** ADAM_STEP)
    v_hat = v2 / (1.0 - ADAM_B2 ** ADAM_STEP)
    delta = -ADAM_LR * (m_hat / (jnp.sqrt(v_hat) + ADAM_EPS) + ADAM_WD * w)
    return delta, m2, v2


def _place():
    return lax.axis_index("x"), lax.axis_index("y"), lax.axis_index("c")


def _chip_peers(x, y, c):
    return [((1 - x, y, c), 2 * (1 - x) + y), ((x, 1 - y, c), 2 * x + 1 - y), ((1 - x, 1 - y, c), 2 * (1 - x) + 1 - y)]


def _shard_of(ref, axis, k, n):
    start = pl.multiple_of(k * n, 128)
    return ref.at[pl.ds(start, n), :] if axis == 0 else ref.at[:, pl.ds(start, n)]


ANY = pl.BlockSpec(memory_space=pl.ANY)


class CopyJob:
    def __init__(self, ins, out_shape, scratch, copies, aliases=None):
        self.ins, self.out_shape, self.scratch, self.copies = list(ins), list(out_shape), list(scratch), copies
        self.aliases = dict(aliases or {})

    def start(self, ins, outs, sems):
        local, remote, _, _ = self.copies(ins, outs, sems)
        for d in local + remote:
            d.start()

    def finish(self, ins, outs, sems):
        local, remote, arrivals, relays = self.copies(ins, outs, sems)
        for needs, sends, _ in relays:
            for d in needs:
                d.wait_recv()
            for d in sends:
                d.start()
        for d in arrivals + [d for _, _, arrives in relays for d in arrives]:
            d.wait_recv()
        for d in remote + [d for _, sends, _ in relays for d in sends]:
            d.wait_send()
        for d in local:
            d.wait()


def run_job(job, *, name):
    ni, no = len(job.ins), len(job.out_shape)

    def body(*refs):
        parts = refs[:ni], refs[ni:ni + no], refs[ni + no:]
        job.start(*parts)
        job.finish(*parts)

    return pl.pallas_call(
        body, name=name, in_specs=[ANY] * ni, out_specs=[ANY] * no, out_shape=job.out_shape,
        scratch_shapes=job.scratch, input_output_aliases=job.aliases,
    )(*job.ins)


def _job_args(job, n_in, n_out):
    if job is None:
        return dict(ins=[], in_specs=[], out_specs=[], out_shape=[], scratch=[], aliases={})
    return dict(ins=job.ins, in_specs=[ANY] * len(job.ins), out_specs=[ANY] * len(job.out_shape),
                out_shape=job.out_shape, scratch=job.scratch,
                aliases={n_in + i: n_out + o for i, o in job.aliases.items()})


def _hosting(body, job, n_in, n_out, n_scratch, grid):
    if job is None:
        return body
    ji, jo = len(job.ins), len(job.out_shape)
    grid = (grid,) if isinstance(grid, int) else tuple(grid)

    def at(ends):
        hit = None
        for ax, e in enumerate(ends):
            here = pl.program_id(ax) == e
            hit = here if hit is None else jnp.logical_and(hit, here)
        return hit

    def hosted(*refs):
        o = n_in + ji
        s = o + n_out + jo
        parts = refs[n_in:o], refs[o + n_out:s], refs[s + n_scratch:]

        @pl.when(at([0] * len(grid)))
        def _():
            job.start(*parts)

        body(*refs[:n_in], *refs[o:o + n_out], *refs[s:s + n_scratch])

        @pl.when(at([g - 1 for g in grid]))
        def _():
            job.finish(*parts)

    return hosted


def _job_sems(n_remote, n_local):
    return [pltpu.SemaphoreType.DMA((n_remote,)), pltpu.SemaphoreType.DMA((n_remote,)), pltpu.SemaphoreType.DMA((n_local,))]


def gather_job(shards, axes, chips=(0, 1, 2)):
    na = len(shards)

    def copies(ins, outs, sems):
        send, recv, _ = sems
        x, y, c = _place()
        k = 2 * x + y
        remote, relays = [], []
        for a in range(na):
            r = outs[a].shape[0] // (N_CHIPS if axes[a] == 0 else 1)
            n = outs[a].shape[axes[a]] // N_CHIPS
            half = r // 2

            def part(kk, cc, a=a, n=n, half=half):
                rows = pl.ds(pl.multiple_of(cc * half + (kk * n if axes[a] == 0 else 0), 8), half)
                return outs[a].at[rows, :] if axes[a] == 0 else outs[a].at[rows, pl.ds(pl.multiple_of(kk * n, 128), n)]

            needs, passes, lands = [], [], []
            for j, (peer, kp) in enumerate(_chip_peers(x, y, c)):
                if j not in chips:
                    continue
                s = 6 * a + j
                remote.append(pltpu.make_async_remote_copy(part(k, c), part(k, c), send.at[s], recv.at[s],
                                                           device_id=peer, device_id_type=MESH))
                needs.append(pltpu.make_async_remote_copy(part(kp, c), part(kp, c), send.at[s], recv.at[s],
                                                          device_id=peer, device_id_type=MESH))
                passes.append(pltpu.make_async_remote_copy(part(kp, c), part(kp, c), send.at[s + 3], recv.at[s + 3],
                                                           device_id=(x, y, 1 - c), device_id_type=MESH))
                lands.append(pltpu.make_async_remote_copy(part(kp, 1 - c), part(kp, 1 - c), send.at[s + 3], recv.at[s + 3],
                                                          device_id=(x, y, 1 - c), device_id_type=MESH))
            relays.append((needs, passes, lands))
        return [], remote, [], relays

    out_shape = [jax.ShapeDtypeStruct(w.shape, BF16) for w in shards]
    return CopyJob(shards, out_shape, _job_sems(6 * na, 1), copies, {a: a for a in range(na)})


def scatter_job(layers, g16, axes, filled, chips=(0, 1, 2)):
    na = len(axes)

    def shard_shape(a):
        r, c = g16[a].shape
        return (r // N_CHIPS, c) if axes[a] == 0 else (r, c // N_CHIPS)

    def copies(ins, outs, sems):
        send, recv_sems, _ = sems
        x, y, c = _place()
        remote = []
        for a in range(na):
            n = shard_shape(a)[axes[a]]
            for r, (peer, kp) in enumerate(_chip_peers(x, y, c)):
                if r not in chips:
                    continue
                remote.append(pltpu.make_async_remote_copy(_shard_of(ins[a], axes[a], kp, n), outs[a].at[r, layers[a]],
                                                           send.at[3 * a + r], recv_sems.at[3 * a + r],
                                                           device_id=peer, device_id_type=MESH))
        return [], remote, remote, []

    out_shape = [jax.ShapeDtypeStruct((3, DEPTH) + shard_shape(a), BF16) for a in range(na)]
    ins = list(g16)
    aliases = {}
    for a in range(na):
        if filled[a] is not None:
            aliases[len(ins)] = a
            ins.append(filled[a])
    return CopyJob(ins, out_shape, _job_sems(3 * na, 1), copies, aliases)


def pair_job(g16, axes):
    na = len(axes)
    pieces = [1 if ax == 1 else N_CHIPS for ax in axes]

    def copies(ins, outs, sems):
        send, recv, _ = sems
        x, y, c = _place()
        remote = []
        s = 0
        for a in range(na):
            rows = g16[a].shape[0] // (2 * pieces[a])
            for kk in range(pieces[a]):
                src = ins[a].at[pl.ds(pl.multiple_of((2 * kk + 1 - c) * rows, 8), rows), :]
                remote.append(pltpu.make_async_remote_copy(src, outs[a].at[pl.ds(kk * rows, rows), :], send.at[s], recv.at[s],
                                                           device_id=(x, y, 1 - c), device_id_type=MESH))
                s += 1
        return [], remote, remote, []

    out_shape = [jax.ShapeDtypeStruct((g.shape[0] // 2, g.shape[1]), BF16) for g in g16]
    return CopyJob(g16, out_shape, _job_sems(sum(pieces), 1), copies)


def join_job(shards):
    na = len(shards)

    def copies(ins, outs, sems):
        send, recv, _ = sems
        x, y, c = _place()
        remote = [pltpu.make_async_remote_copy(outs[a].at[:, c], outs[a].at[:, c], send.at[a], recv.at[a],
                                               device_id=(x, y, 1 - c), device_id_type=MESH) for a in range(na)]
        lands = [pltpu.make_async_remote_copy(outs[a].at[:, 1 - c], outs[a].at[:, 1 - c], send.at[a], recv.at[a],
                                              device_id=(x, y, 1 - c), device_id_type=MESH) for a in range(na)]
        return [], remote, lands, []

    out_shape = [jax.ShapeDtypeStruct(s.shape, F32) for s in shards]
    return CopyJob(shards, out_shape, _job_sems(na, 1), copies, {a: a for a in range(na)})


def small_job(p):
    def copies(ins, outs, sems):
        send, recv, loc = sems
        x, y, c = _place()
        me = 4 * x + 2 * y + c
        remote, lands = [], []
        for rel in range(1, 8):
            dx, dy, dc = rel >> 2, (rel >> 1) & 1, rel & 1
            peer = (1 - x if dx else x, 1 - y if dy else y, 1 - c if dc else c)
            who = 4 * peer[0] + 2 * peer[1] + peer[2]
            remote.append(pltpu.make_async_remote_copy(ins[0], outs[0].at[me], send.at[rel - 1], recv.at[rel - 1],
                                                       device_id=peer, device_id_type=MESH))
            lands.append(pltpu.make_async_remote_copy(ins[0], outs[0].at[who], send.at[rel - 1], recv.at[rel - 1],
                                                      device_id=peer, device_id_type=MESH))
        return [pltpu.make_async_copy(ins[0], outs[0].at[me], loc.at[0])], remote, lands, []

    return CopyJob([p], [jax.ShapeDtypeStruct((8,) + p.shape, F32)], _job_sems(7, 1), copies)


def small_sum(slots):
    def add(*terms):
        acc = terms[0]
        for t in terms[1:]:
            acc = acc + t
        return (acc,)

    return _rows_call(add, [(slots, d) for d in range(8)], [F32], name="small_sum", tr=8 * 47)[0]


BIG = ("w_in", "p_ret", "p_sb", "p_sgu", "w_out", "w_up", "w_down")
BIG_AXIS = {"w_in": 1, "p_ret": 1, "p_sb": 1, "p_sgu": 1, "w_out": 0, "w_up": 1, "w_down": 0}
SMALL = ("ret_gn_g", "ret_gn_b", "sgu_ln_g", "sgu_ln_b", "sgu_w", "sgu_b", "ln1_g", "ln1_b", "ln2_g", "ln2_b")


def layer_forward(l, x0, W, sm, rope, rconsts, hooks):
    n = f"l{l}_"
    job = hooks.fwd_job(l, "proj")
    proj = matmul(x0, W["w_in"], mode="nn", tm=2048, tn=768, tk=1024, name=n + "proj", job=job)
    if job is not None:
        proj, job_out = proj
        hooks.done(job, job_out)
    retg, raw, states = ret_fwd(proj, *rope, rconsts, sm["ret_gn_g"], sm["ret_gn_b"], name=n + "ret_fwd")
    job = hooks.fwd_job(l, "sb")
    sb, job_out = sb_fwd(proj, name=n + "sb_fwd", job=job)
    if job is not None:
        hooks.done(job, job_out)
    sg = sgu_fwd(proj, sm["sgu_ln_g"], sm["sgu_ln_b"], sm["sgu_w"], sm["sgu_bias"], name=n + "sgu_fwd")
    job = hooks.fwd_job(l, "merge")
    res = merge_fwd(retg, sb, sg, W["p_ret"], W["p_sb"], W["p_sgu"], proj, name=n + "merge_fwd", job=job)
    if job is not None:
        res, job_out = res
        hooks.done(job, job_out)
    merged, r1, r2, r3 = res
    x1, xh1, rs1 = matmul_ln(merged, W["w_out"], x0, sm["ln1_g"], sm["ln1_b"], tk=1024, name=n + "out_ln1")
    job = hooks.fwd_job(l, "up")
    h1 = matmul(x1, W["w_up"], mode="nn", tm=1024, tn=1024, tk=1024, outs=((BF16, None),), name=n + "up", job=job)
    if job is not None:
        h1, job_out = h1
        hooks.done(job, job_out)
    job = hooks.fwd_job(l, "down")
    res = matmul_ln(h1, W["w_down"], x1, sm["ln2_g"], sm["ln2_b"], pro=_relu2, tk=1024, name=n + "down_ln2", job=job)
    if job is not None:
        res, job_out = res
        hooks.done(job, job_out)
    x2, xh2, rs2 = res
    saved = dict(x0=x0, proj=proj, retg=retg, raw=raw, states=states, sb=sb, sg=sg, merged=merged, r=(r1, r2, r3),
                 x1=x1, xh1=xh1, rs1=rs1, h1=h1, xh2=xh2, rs2=rs2)
    return x2, saved


def layer_backward(l, dx2, s, W, sm, rope, rconsts, hooks):
    n = f"l{l}_"
    two = ((F32, None), (BF16, None))
    gw, gs = {}, {}
    job = hooks.bwd_job(l, "ln2")
    res = ln_bwd(dx2, s["xh2"], s["rs2"], sm["ln2_g"], name=n + "ln2_bwd", job=job)
    if job is not None:
        res, job_out = res
        hooks.done(job, job_out)
    du2, du2h, gs["ln2_g"], gs["ln2_b"] = res
    job = hooks.bwd_job(l, "g_down")
    gw["w_down"] = matmul(s["h1"], du2h, mode="tn", tm=1024, tn=1024, tk=2048, pro=_relu2, outs=two, name=n + "g_down", job=job)
    if job is not None:
        gw["w_down"], job_out = gw["w_down"]
        hooks.done(job, job_out)
    dh1 = matmul(du2h, W["w_down"], mode="nt", tm=1024, tn=1024, tk=1024, outs=((BF16, None),),
                 epi=lambda acc, h: (acc * (2.0 * jnp.maximum(h.astype(F32), 0.0)),), tiles=(s["h1"],), name=n + "d_h1")
    job = hooks.bwd_job(l, "g_up")
    gw["w_up"] = matmul(s["x1"], dh1, mode="tn", tm=1024, tn=1024, tk=2048, outs=two, name=n + "g_up", job=job)
    if job is not None:
        gw["w_up"], job_out = gw["w_up"]
        hooks.done(job, job_out)
    dx1 = matmul(dh1, W["w_up"], mode="nt", tm=1024, tn=1024, tk=2048,
                 epi=lambda acc, d: (acc + ALPHA * d,), tiles=(du2,), name=n + "d_x1")
    du1, du1h, gs["ln1_g"], gs["ln1_b"] = ln_bwd(dx1, s["xh1"], s["rs1"], sm["ln1_g"], name=n + "ln1_bwd")
    gw["w_out"] = matmul(s["merged"], du1h, mode="tn", tm=1024, tn=1024, tk=2048, outs=two, name=n + "g_out")
    gate0 = C_GATE // 512
    dr1, dr2, dr3, dg1, dg2, dg3 = matmul(
        du1h, W["w_out"], mode="nt", tm=1024, tn=512, tk=1024, outs=((BF16, None),) * 6, epi=_merge_bwd_epi,
        tiles=(*s["r"], (s["proj"], gate0), (s["proj"], gate0 + 2), (s["proj"], gate0 + 4)), name=n + "d_merged")
    d_branch = {}
    for nm, a, dr in (("p_ret", s["retg"], dr1), ("p_sb", s["sb"], dr2), ("p_sgu", s["sg"], dr3)):
        gw[nm] = matmul(a, dr, mode="tn", tm=512, tn=1024, tk=2048, outs=two, name=n + "g_" + nm)
        d_branch[nm] = matmul(dr, W[nm], mode="nt", tm=1024, tn=512, tk=1024, name=n + "d_" + nm)
    job = hooks.pair(l, gw)
    dret, gs["ret_gn_g"], gs["ret_gn_b"], job_out = ret_bwd(s["proj"], *rope, rconsts, sm["ret_gn_g"], sm["ret_gn_b"],
                                                             s["raw"], s["states"], d_branch["p_ret"], name=n + "ret_bwd", job=job)
    if job is not None:
        hooks.done(job, job_out)
    job = hooks.scatter(l) if job is not None else None
    dsq, dsk, dsv, job_out = sb_bwd(s["proj"], s["sb"], d_branch["p_sb"], name=n + "sb_bwd", job=job)
    if job is not None:
        hooks.done(job, job_out)
    dsgu, gs["sgu_w"], dbias, gs["sgu_ln_g"], gs["sgu_ln_b"] = sgu_bwd(
        s["proj"], sm["sgu_ln_g"], sm["sgu_ln_b"], sm["sgu_w"], sm["sgu_bias"], d_branch["p_sgu"], name=n + "sgu_bwd")
    gs["sgu_b"] = dbias[:, :, 0]
    dproj = jnp.concatenate([dret, dsq, dsk, dsv, dsgu, dg1, dg2, dg3], axis=1)
    job = hooks.small(l, gs)
    gw["w_in"] = matmul(s["x0"], dproj, mode="tn", tm=1024, tn=1536, tk=1024, outs=two, name=n + "g_in", job=job)
    if job is not None:
        gw["w_in"], job_out = gw["w_in"]
        hooks.done(job, job_out)
    job = hooks.tail(l, gw["w_in"])
    dx0 = matmul(dproj, W["w_in"], mode="nt", tm=1024, tn=1024, tk=2560,
                 epi=lambda acc, d: (acc + ALPHA * d,), tiles=(du1,), name=n + "d_x0", job=job)
    if job is not None:
        dx0, job_out = dx0
        hooks.done(job, job_out)
    return dx0, gw, gs


def local_step(x, target, small, plan):
    T = x.shape[0]
    rope = _rope_tables(T)
    rconsts = _ret_consts()
    sms = []
    for l in range(DEPTH):
        sm = {k: small[k][l][None, :] for k in SMALL if k not in ("sgu_w", "sgu_b")}
        sm["sgu_w"] = small["sgu_w"][l]
        sm["sgu_bias"] = jnp.broadcast_to(small["sgu_b"][l][:, :, None], (4, CHUNK, CHUNK))
        sms.append(sm)
    h, saved = x, []
    for l in range(DEPTH):
        h, s = layer_forward(l, h, plan.weights(l), sms[l], rope, rconsts, plan)
        saved.append(s)
    dy, sq = loss_head(h, target)
    gs = {k: [None] * DEPTH for k in SMALL}
    for l in reversed(range(DEPTH)):
        dy, gwl, gsl = layer_backward(l, dy, saved[l], plan.weights(l), sms[l], rope, rconsts, plan)
        plan.grads(l, gwl)
        for k in SMALL:
            gs[k][l] = gsl[k].reshape(small[k].shape[1:])
    return sq[0, 0], dy, {k: jnp.stack(v) for k, v in gs.items()}


EARLY_GRADS = ("p_ret", "p_sb", "p_sgu", "w_out", "w_up", "w_down")


class _StepPlan:
    def __init__(self, pos, shards16):
        self.pos = pos
        self.shards16 = shards16
        self.full = [dict() for _ in range(DEPTH)]
        self.gw = [None] * DEPTH
        self.bufs = {}
        self.sums = {}
        self.gs = [None] * DEPTH
        first = self._gather([(0, "w_in")])
        self.done(first, run_job(first, name="gather_first"))

    def weights(self, l):
        return self.full[l]

    def grads(self, l, gw):
        self.gw[l] = gw

    def _gather(self, items, chips=(0, 1, 2)):
        job = gather_job([self.shards16[l][k] for l, k in items], [BIG_AXIS[k] for _, k in items], chips)
        job.note = ("gather" if 2 in chips else "gather_part", items)
        return job

    def _pair(self, items):
        job = pair_job([g[1] for _, _, g in items], [BIG_AXIS[k] for _, k, _ in items])
        job.note = ("pair", items)
        return job

    def fwd_job(self, l, host):
        if host == "proj":
            return None
        if host == "sb":
            return self._gather([(l, k) for k in BIG[1:] if k != "w_down"])
        if host == "merge":
            return self._gather([(l, "w_down")])
        if l + 1 == DEPTH:
            return None
        return self._gather([(l + 1, "w_in")], (0, 1) if host == "up" else (2,))

    def bwd_job(self, l, host):
        if l + 1 == DEPTH:
            return None
        if host == "ln2":
            job = self._pair([(l + 1, "w_in", self.gw[l + 1]["w_in"])])
            job.note = ("pair_w_in", job.note[1])
            return job
        items, sums16 = self.summed_w_in
        job = scatter_job([l_ for l_, _, _ in items], sums16, [BIG_AXIS[k] for _, k, _ in items],
                          [self.bufs.get(k) for _, k, _ in items], (0, 1) if host == "g_down" else (2,))
        job.note = ("scatter", items)
        return job

    def pair(self, l, ready):
        return self._pair([(l, k, ready[k]) for k in EARLY_GRADS])

    def scatter(self, l):
        items, sums16 = self.summed
        job = scatter_job([l_ for l_, _, _ in items], sums16, [BIG_AXIS[k] for _, k, _ in items],
                          [self.bufs.get(k) for _, k, _ in items])
        job.note = ("scatter", items)
        return job

    def small(self, l, gs):
        self.gs[l] = {k: gs[k].reshape(-1) for k in SMALL}
        if l != 0:
            return None
        job = small_job(_pack_small({k: jnp.stack([self.gs[l_][k] for l_ in range(DEPTH)]) for k in SMALL}))
        job.note = ("small", [])
        return job

    def tail(self, l, g):
        if l != 0:
            return None
        last = self._pair([(0, "w_in", g)])
        self.done(last, run_job(last, name="pair_last"))
        return self.scatter(0)

    def done(self, job, outs):
        kind, items = job.note
        if kind == "small":
            self.small_slots = outs[0]
        if kind in ("pair", "pair_w_in"):
            sums16 = []
            for a, (l, k, g) in enumerate(items):
                self.sums[(l, k)], s16 = pair_sum(self.pos, outs[a], g[0], BIG_AXIS[k], name=f"pair_sum_{k}_{l}")
                sums16.append(s16)
            if kind == "pair":
                self.summed = (items, sums16)
            else:
                self.summed_w_in = (items, sums16)
        for a, item in enumerate(items):
            if kind == "gather_part":
                self.shards16[item[0]][item[1]] = outs[a]
            elif kind == "gather":
                self.full[item[0]][item[1]] = outs[a]
            elif kind == "scatter":
                self.bufs[item[1]] = outs[a]

    def finish(self):
        return self.bufs, self.sums


def _flat2(a):
    return a.reshape(-1, a.shape[-1])


def _pack_small(d, pre=""):
    return jnp.concatenate([d[pre + k].reshape(-1) for k in SMALL]).reshape(-1, 128)


def kernel(x, w_in, ret_gn_g, ret_gn_b, sgu_ln_g, sgu_ln_b, sgu_w, sgu_b, p_ret, p_sb, p_sgu, w_out, ln1_g, ln1_b, w_up, w_down, ln2_g, ln2_b, loss_target, m_w_in, m_ret_gn_g, m_ret_gn_b, m_sgu_ln_g, m_sgu_ln_b, m_sgu_w, m_sgu_b, m_p_ret, m_p_sb, m_p_sgu, m_w_out, m_ln1_g, m_ln1_b, m_w_up, m_w_down, m_ln2_g, m_ln2_b, v_w_in, v_ret_gn_g, v_ret_gn_b, v_sgu_ln_g, v_sgu_ln_b, v_sgu_w, v_sgu_b, v_p_ret, v_p_sb, v_p_sgu, v_w_out, v_ln1_g, v_ln1_b, v_w_up, v_w_down, v_ln2_g, v_ln2_b):
    given = dict(locals())
    order = BIG[:1] + SMALL[:6] + BIG[1:5] + SMALL[6:8] + BIG[5:7] + SMALL[8:10]
    L = DEPTH

    px, py, pc = _place()
    pos = jnp.stack([px, py, pc, 2 * px + py]).astype(jnp.int32)

    shards16 = [{k: cast_into_whole(pos, given[k], l, BIG_AXIS[k], name=f"cast_{k}_{l}") for k in BIG} for l in range(L)]
    plan = _StepPlan(pos, shards16)
    sq, dx, gs = local_step(x[0], loss_target[0], {k: given[k] for k in SMALL}, plan)
    loss = 0.5 * lax.psum(sq, ("x", "y", "c"))

    bufs, sums = plan.finish()
    shards = []
    for k in BIG:
        whole = None
        for l in range(L):
            whole = chip_sum(pos, sums[(l, k)], bufs[k], l, BIG_AXIS[k], whole, name=f"chip_sum_{k}_{l}")
        shards.append(whole)
    joined = run_job(join_job(shards), name="join_halves")
    out = {}
    for a, k in enumerate(BIG):
        shp = given[k].shape
        res = _rows_call(lambda g_, w_, m_, v_: (g_,) + _adamw(w_, g_, m_, v_),
                         [joined[a].reshape(-1, shp[-1]), _flat2(given[k]), _flat2(given["m_" + k]), _flat2(given["v_" + k])],
                         [F32] * 4, name="adamw_" + k)
        out[k] = [r.reshape(shp) for r in res]

    pack = _pack_small
    res = _rows_call(lambda g_, w_, m_, v_: (g_,) + _adamw(w_, g_, m_, v_),
                     [small_sum(plan.small_slots), pack(given), pack(given, "m_"), pack(given, "v_")], [F32] * 4,
                     name="adamw_small", tr=8 * 47)
    off = 0
    for k in SMALL:
        sz = given[k].size
        out[k] = [r.reshape(-1)[off:off + sz].reshape(given[k].shape) for r in res]
        off += sz

    grads = [out[k][0] for k in order]
    deltas = [out[k][1] for k in order]
    new_m = [out[k][2] for k in order]
    new_v = [out[k][3] for k in order]
    return (loss, dx[None], *grads, *deltas, *new_m, *new_v)
```

```python
import functools
import math

import jax
import jax.numpy as jnp
from jax import lax
from jax.experimental import pallas as pl
from jax.experimental.pallas import tpu as pltpu

F32 = jnp.float32
BF16 = jnp.bfloat16

D_MODEL = 1024
SEQ = 4096
DEPTH = 2
CHUNK = 128
RET_HEADS = 4
BRANCH_W = 512
N_IN = 7680
D_FF = 4096
LN_EPS = 1e-5
ROPE_BASE = 10000.0
ALPHA = (2 * DEPTH) ** 0.25
RET_SCALE = 128 ** -0.5
SB_SCALE = 64 ** -0.5
C_RET, C_SB, C_SGU, C_GATE = 0, 2048, 3584, 4608

ADAM_LR, ADAM_B1, ADAM_B2, ADAM_EPS, ADAM_WD, ADAM_STEP = 0.001, 0.9, 0.999, 1e-08, 0.01, 10

N_CHIPS = 4
VMEM_LIMIT = 56 * 1024 * 1024
MESH = pl.DeviceIdType.MESH

NN = ((1,), (0,))
NT = ((1,), (1,))
TN = ((0,), (0,))


def _dot(a, b, dims):
    return lax.dot_general(a, b, (dims, ((), ())), preferred_element_type=F32)


def _params(sem):
    return pltpu.CompilerParams(dimension_semantics=sem, vmem_limit_bytes=VMEM_LIMIT)


def _relu2(h):
    r = jnp.maximum(h.astype(F32), 0.0)
    return r * r


def matmul(a, b, *, mode, tm, tn, tk, outs=((F32, None),), pro=None, epi=None, tiles=(), rows=(), name, job=None):
    if mode == "nn":
        (M, K), N = a.shape, b.shape[1]
    elif mode == "nt":
        (M, K), N = a.shape, b.shape[0]
    else:
        (K, M), N = a.shape, b.shape[1]
    tm, tn, tk = min(tm, M), min(tn, N), min(tk, K)
    assert M % tm == 0 and N % tn == 0 and K % tk == 0, (name, M, N, K, tm, tn, tk)
    if mode == "nn":
        a_spec = pl.BlockSpec((tm, tk), lambda i, j, k: (i, k))
        b_spec = pl.BlockSpec((tk, tn), lambda i, j, k: (k, j))
        dims = NN
    elif mode == "nt":
        a_spec = pl.BlockSpec((tm, tk), lambda i, j, k: (i, k))
        b_spec = pl.BlockSpec((tn, tk), lambda i, j, k: (j, k))
        dims = NT
    else:
        a_spec = pl.BlockSpec((tk, tm), lambda i, j, k: (k, i))
        b_spec = pl.BlockSpec((tk, tn), lambda i, j, k: (k, j))
        dims = TN
    nk = K // tk
    nt_, nr, no = len(tiles), len(rows), len(outs)

    def body(a_ref, b_ref, *rest):
        tile_refs = rest[:nt_]
        row_refs = rest[nt_:nt_ + nr]
        out_refs = rest[nt_ + nr:nt_ + nr + no]
        av = a_ref[...]
        if pro is not None:
            av = pro(av)
        p = _dot(av.astype(BF16), b_ref[...].astype(BF16), dims)

        def finish(acc):
            vals = (acc,) * no if epi is None else epi(acc, *[r[...] for r in tile_refs], *[r[...] for r in row_refs])
            for o_ref, v in zip(out_refs, vals):
                o_ref[...] = v.astype(o_ref.dtype)

        if nk == 1:
            finish(p)
        else:
            acc_ref = rest[-1]
            k = pl.program_id(2)

            @pl.when(k == 0)
            def _():
                acc_ref[...] = p

            @pl.when(k > 0)
            def _():
                acc_ref[...] += p

            @pl.when(k == nk - 1)
            def _():
                finish(acc_ref[...])

    out_shape, out_specs = [], []
    for dt, width in outs:
        if width is None:
            out_shape.append(jax.ShapeDtypeStruct((M, N), dt))
            out_specs.append(pl.BlockSpec((tm, tn), lambda i, j, k: (i, j)))
        else:
            assert N == tn
            out_shape.append(jax.ShapeDtypeStruct((M, width), dt))
            out_specs.append(pl.BlockSpec((tm, width), lambda i, j, k: (i, 0)))
    in_specs = [a_spec, b_spec]
    offs = [t[1] if isinstance(t, tuple) else 0 for t in tiles]
    tiles = [t[0] if isinstance(t, tuple) else t for t in tiles]
    in_specs += [pl.BlockSpec((tm, tn), functools.partial(lambda i, j, k, o: (i, j + o), o=o)) for o in offs]
    in_specs += [pl.BlockSpec((1, tn), lambda i, j, k: (0, j)) for _ in rows]
    grid = (M // tm, N // tn, nk)
    scratch = [pltpu.VMEM((tm, tn), F32)] if nk > 1 else []
    j = _job_args(job, len(in_specs), no)
    res = pl.pallas_call(
        _hosting(body, job, len(in_specs), no, len(scratch), grid), name=name, grid=grid,
        in_specs=in_specs + j["in_specs"], out_specs=out_specs + j["out_specs"], out_shape=out_shape + j["out_shape"],
        scratch_shapes=scratch + j["scratch"], input_output_aliases=j["aliases"],
        compiler_params=_params(("parallel", "parallel", "arbitrary") if job is None else ("arbitrary",) * 3),
    )(a, b, *tiles, *rows, *j["ins"])
    mine = res[0] if no == 1 else list(res[:no])
    return mine if job is None else (mine, list(res[no:]))


def _ln_epi(acc, res, g, b):
    u = ALPHA * res + acc
    mu = jnp.mean(u, axis=-1, keepdims=True)
    xc = u - mu
    var = jnp.mean(xc * xc, axis=-1, keepdims=True)
    rstd = lax.rsqrt(var + LN_EPS)
    xhat = xc * rstd
    y = xhat * g + b
    return y, xhat, jnp.broadcast_to(rstd, (u.shape[0], 128)), y


def matmul_ln(a, w, res, g, b, *, pro=None, tk, name, job=None):
    n = w.shape[1]
    return matmul(a, w, mode="nn", tm=1024, tn=n, tk=tk, pro=pro, epi=_ln_epi, tiles=(res,), rows=(g, b),
                  outs=((F32, None), (F32, None), (F32, 128), (BF16, None)), name=name, job=job)


def ln_bwd(dy, xhat, rstd, g, *, name, job=None):
    T, D = dy.shape
    tm = min(512, T)

    def body(dy_ref, xh_ref, rs_ref, g_ref, du_ref, du16_ref, dg_ref, db_ref):
        dyv, xh = dy_ref[...], xh_ref[...]
        r = rs_ref[:, 0:1]
        dxh = dyv * g_ref[...]
        m1 = jnp.mean(dxh, axis=-1, keepdims=True)
        m2 = jnp.mean(dxh * xh, axis=-1, keepdims=True)
        du = r * (dxh - m1 - xh * m2)
        du_ref[...] = du
        du16_ref[...] = du.astype(BF16)

        @pl.when(pl.program_id(0) == 0)
        def _():
            dg_ref[...] = jnp.zeros_like(dg_ref)
            db_ref[...] = jnp.zeros_like(db_ref)

        dg_ref[...] += jnp.sum(dyv * xh, axis=0, keepdims=True)
        db_ref[...] += jnp.sum(dyv, axis=0, keepdims=True)

    row = pl.BlockSpec((tm, D), lambda i: (i, 0))
    vec = pl.BlockSpec((1, D), lambda i: (0, 0))
    j = _job_args(job, 4, 4)
    res = pl.pallas_call(
        _hosting(body, job, 4, 4, 0, T // tm), name=name, grid=(T // tm,),
        in_specs=[row, row, pl.BlockSpec((tm, 128), lambda i: (i, 0)), vec] + j["in_specs"],
        out_specs=[row, row, vec, vec] + j["out_specs"],
        out_shape=[jax.ShapeDtypeStruct((T, D), F32), jax.ShapeDtypeStruct((T, D), BF16),
                   jax.ShapeDtypeStruct((1, D), F32), jax.ShapeDtypeStruct((1, D), F32)] + j["out_shape"],
        scratch_shapes=j["scratch"], input_output_aliases=j["aliases"],
        compiler_params=_params(("arbitrary",)),
    )(dy, xhat, rstd, g, *j["ins"])
    return list(res[:4]) if job is None else (list(res[:4]), list(res[4:]))


def loss_head(y, target):
    T, D = y.shape
    tm = min(512, T)

    def body(y_ref, t_ref, dy_ref, s_ref):
        e = y_ref[...] - t_ref[...]
        dy_ref[...] = e * (1.0 / D)

        @pl.when(pl.program_id(0) == 0)
        def _():
            s_ref[...] = jnp.zeros_like(s_ref)

        s_ref[...] += jnp.sum(jnp.mean(e * e, axis=-1, keepdims=True))

    row = pl.BlockSpec((tm, D), lambda i: (i, 0))
    return pl.pallas_call(
        body, name="loss_head", grid=(T // tm,),
        in_specs=[row, row], out_specs=[row, pl.BlockSpec((8, 128), lambda i: (0, 0))],
        out_shape=[jax.ShapeDtypeStruct((T, D), F32), jax.ShapeDtypeStruct((8, 128), F32)],
        compiler_params=_params(("arbitrary",)),
    )(y, target)


def _rope_tables(T):
    half = 64
    inv_freq = ROPE_BASE ** (-jnp.arange(half, dtype=F32) / half)
    ang = jnp.arange(T, dtype=jnp.int32).astype(F32)[:, None] * inv_freq[None, :]
    cos, sin = jnp.cos(ang), jnp.sin(ang)
    return jnp.concatenate([cos, cos], axis=1), jnp.concatenate([-sin, sin], axis=1)


def _ret_consts():
    H = RET_HEADS
    log_g = jnp.log(1.0 - 2.0 ** (-5.0 - jnp.arange(H, dtype=F32)))
    idx = jnp.arange(CHUNK, dtype=F32)
    diff = idx[:, None] - idx[None, :]
    dmat = jnp.where(diff[None] >= 0, jnp.exp(log_g[:, None, None] * diff[None]), 0.0)
    kd = jnp.exp(log_g[:, None] * (CHUNK - 1 - idx)[None, :])
    qd = jnp.exp(log_g[:, None] * (idx + 1.0)[None, :])
    cd = jnp.exp(log_g * CHUNK)
    full = (H, CHUNK, CHUNK)
    return (dmat.astype(F32), jnp.broadcast_to(kd[:, :, None], full), jnp.broadcast_to(qd[:, :, None], full),
            jnp.broadcast_to(cd[:, None, None], full))


def _swap_halves(v):
    return pltpu.roll(v, 64, 1)


def _group_norm(o):
    mu = jnp.mean(o, axis=-1, keepdims=True)
    xc = o - mu
    var = jnp.mean(xc * xc, axis=-1, keepdims=True)
    rstd = lax.rsqrt(var + LN_EPS)
    return xc * rstd, rstd


def ret_fwd(proj, cosf, sinf, consts, gn_g, gn_b, *, name):
    T = proj.shape[0]
    tb = min(512, T)
    nch = tb // CHUNK
    H = RET_HEADS

    def body(p_ref, cos_ref, sin_ref, dm_ref, kd_ref, qd_ref, cd_ref, g_ref, b_ref, out_ref, raw_ref, st_ref, s_ref):
        @pl.when(pl.program_id(0) == 0)
        def _():
            s_ref[...] = jnp.zeros_like(s_ref)

        for c in range(nch):
            r = slice(c * CHUNK, (c + 1) * CHUNK)
            cs, sn = cos_ref[r, :], sin_ref[r, :]
            for h in range(H):
                hc = slice(h * 128, (h + 1) * 128)
                q = p_ref[r, h * 128:(h + 1) * 128]
                k = p_ref[r, 512 + h * 128:512 + (h + 1) * 128]
                v = p_ref[r, 1024 + h * 128:1024 + (h + 1) * 128]
                gt = p_ref[r, 1536 + h * 128:1536 + (h + 1) * 128]
                qr = q * cs + _swap_halves(q) * sn
                kr = (k * cs + _swap_halves(k) * sn) * RET_SCALE
                sprev = s_ref[h]
                st_ref[c, h] = sprev
                qb, kb, vb = qr.astype(BF16), kr.astype(BF16), v.astype(BF16)
                s = _dot(qb, kb, NT) * dm_ref[h]
                o = _dot(s.astype(BF16), vb, NN) + _dot((qr * qd_ref[h]).astype(BF16), sprev.astype(BF16), NN)
                s_ref[h] = sprev * cd_ref[h] + _dot((kr * kd_ref[h]).astype(BF16), vb, TN)
                raw_ref[r, hc] = o
                y, _ = _group_norm(o)
                out_ref[r, hc] = (gt * jax.nn.sigmoid(gt)) * (y * g_ref[:, hc] + b_ref[:, hc])

    cmat = pl.BlockSpec((H, CHUNK, CHUNK), lambda i: (0, 0, 0))
    vec = pl.BlockSpec((1, BRANCH_W), lambda i: (0, 0))
    rope = pl.BlockSpec((tb, 128), lambda i: (i, 0))
    blk = pl.BlockSpec((tb, BRANCH_W), lambda i: (i, 0))
    return pl.pallas_call(
        body, name=name, grid=(T // tb,),
        in_specs=[pl.BlockSpec((tb, 2048), lambda i: (i, 0)), rope, rope, cmat, cmat, cmat, cmat, vec, vec],
        out_specs=[blk, blk, pl.BlockSpec((nch, H, CHUNK, CHUNK), lambda i: (i, 0, 0, 0))],
        out_shape=[jax.ShapeDtypeStruct((T, BRANCH_W), F32), jax.ShapeDtypeStruct((T, BRANCH_W), F32),
                   jax.ShapeDtypeStruct((T // CHUNK, H, CHUNK, CHUNK), F32)],
        scratch_shapes=[pltpu.VMEM((H, CHUNK, CHUNK), F32)],
        compiler_params=_params(("arbitrary",)),
    )(proj, cosf, sinf, *consts, gn_g, gn_b)


def ret_bwd(proj, cosf, sinf, consts, gn_g, gn_b, raw, states, dout, *, name, job=None):
    T = proj.shape[0]
    tb = min(512, T)
    nch = tb // CHUNK
    nb = T // tb
    H = RET_HEADS

    def body(p_ref, cos_ref, sin_ref, dm_ref, kd_ref, qd_ref, cd_ref, g_ref, b_ref, raw_ref, st_ref, do_ref,
             dp_ref, dg_ref, db_ref, ds_ref):
        @pl.when(pl.program_id(0) == 0)
        def _():
            ds_ref[...] = jnp.zeros_like(ds_ref)
            dg_ref[...] = jnp.zeros_like(dg_ref)
            db_ref[...] = jnp.zeros_like(db_ref)

        for c in reversed(range(nch)):
            r = slice(c * CHUNK, (c + 1) * CHUNK)
            cs, sn = cos_ref[r, :], sin_ref[r, :]
            for h in range(H):
                hc = slice(h * 128, (h + 1) * 128)
                q = p_ref[r, h * 128:(h + 1) * 128]
                k = p_ref[r, 512 + h * 128:512 + (h + 1) * 128]
                v = p_ref[r, 1024 + h * 128:1024 + (h + 1) * 128]
                gt = p_ref[r, 1536 + h * 128:1536 + (h + 1) * 128]
                qr = q * cs + _swap_halves(q) * sn
                kr = (k * cs + _swap_halves(k) * sn) * RET_SCALE
                sprev = st_ref[c, h]
                gv = g_ref[:, hc]
                y, rstd = _group_norm(raw_ref[r, hc])
                d_out = do_ref[r, hc]
                sg = jax.nn.sigmoid(gt)
                d_gate = d_out * (y * gv + b_ref[:, hc]) * (sg * (1.0 + gt * (1.0 - sg)))
                d_aff = d_out * (gt * sg)
                dg_ref[:, hc] += jnp.sum(d_aff * y, axis=0, keepdims=True)
                db_ref[:, hc] += jnp.sum(d_aff, axis=0, keepdims=True)
                dxh = d_aff * gv
                m1 = jnp.mean(dxh, axis=-1, keepdims=True)
                m2 = jnp.mean(dxh * y, axis=-1, keepdims=True)
                d_o = (rstd * (dxh - m1 - y * m2)).astype(BF16)
                qb, kb, vb = qr.astype(BF16), kr.astype(BF16), v.astype(BF16)
                dm, kd, qd = dm_ref[h], kd_ref[h], qd_ref[h]
                p = (_dot(qb, kb, NT) * dm).astype(BF16)
                dp = (_dot(d_o, vb, NT) * dm).astype(BF16)
                dsn = ds_ref[h]
                dsb = dsn.astype(BF16)
                dq_r = _dot(dp, kb, NN) + _dot(d_o, sprev.astype(BF16), NT) * qd
                dk_r = (_dot(dp, qb, TN) + _dot(vb, dsb, NT) * kd) * RET_SCALE
                d_v = _dot(p, d_o, TN) + _dot((kr * kd).astype(BF16), dsb, NN)
                ds_ref[h] = dsn * cd_ref[h] + _dot((qr * qd).astype(BF16), d_o, TN)
                dp_ref[r, h * 128:(h + 1) * 128] = (dq_r * cs - _swap_halves(dq_r) * sn).astype(BF16)
                dp_ref[r, 512 + h * 128:512 + (h + 1) * 128] = (dk_r * cs - _swap_halves(dk_r) * sn).astype(BF16)
                dp_ref[r, 1024 + h * 128:1024 + (h + 1) * 128] = d_v.astype(BF16)
                dp_ref[r, 1536 + h * 128:1536 + (h + 1) * 128] = d_gate.astype(BF16)

    cmat = pl.BlockSpec((H, CHUNK, CHUNK), lambda i: (0, 0, 0))
    vec = pl.BlockSpec((1, BRANCH_W), lambda i: (0, 0))
    rope = pl.BlockSpec((tb, 128), lambda i: (nb - 1 - i, 0))
    blk = pl.BlockSpec((tb, BRANCH_W), lambda i: (nb - 1 - i, 0))
    wide = pl.BlockSpec((tb, 2048), lambda i: (nb - 1 - i, 0))
    j = _job_args(job, 12, 3)
    res = pl.pallas_call(
        _hosting(body, job, 12, 3, 1, nb), name=name, grid=(nb,),
        in_specs=[wide, rope, rope, cmat, cmat, cmat, cmat, vec, vec, blk,
                  pl.BlockSpec((nch, H, CHUNK, CHUNK), lambda i: (nb - 1 - i, 0, 0, 0)), blk] + j["in_specs"],
        out_specs=[wide, vec, vec] + j["out_specs"],
        out_shape=[jax.ShapeDtypeStruct((T, 2048), BF16), jax.ShapeDtypeStruct((1, BRANCH_W), F32),
                   jax.ShapeDtypeStruct((1, BRANCH_W), F32)] + j["out_shape"],
        scratch_shapes=[pltpu.VMEM((H, CHUNK, CHUNK), F32)] + j["scratch"], input_output_aliases=j["aliases"],
        compiler_params=_params(("arbitrary",)),
    )(proj, cosf, sinf, *consts, gn_g, gn_b, raw, states, dout, *j["ins"])
    return res[0], res[1], res[2], list(res[3:])


def _sb_masks():
    row = lax.broadcasted_iota(jnp.int32, (CHUNK, CHUNK), 0)
    lane = lax.broadcasted_iota(jnp.int32, (CHUNK, CHUNK), 1)
    return row, lane


SB_QT = 256
SB_DEAD = -105.0


def _pair(v):
    hi = v.astype(BF16)
    return jnp.concatenate([hi, (v - hi.astype(F32)).astype(BF16)], axis=1)


def _sb_consts():
    r = lax.broadcasted_iota(jnp.int32, (256, 256), 0) & 127
    c = lax.broadcasted_iota(jnp.int32, (256, 256), 1)
    ones = c >= 128
    lane = lax.broadcasted_iota(jnp.int32, (CHUNK, CHUNK), 1)
    return (ones | (r > c)).astype(BF16), (ones | (r >= c)).astype(BF16), (lane < 64, lane >= 64)


def _per_head(x, hms):
    return jnp.concatenate([jnp.where(hm, x, 0.0) for hm in hms], axis=0).astype(BF16)


def _sb_logits(qb, kb2, mask2):
    z = _dot(qb, kb2, NT)
    l1p = jnp.log(1.0 + jnp.exp(-jnp.abs(z)))
    lsp = jnp.minimum(z, 0.0) - l1p
    lsn = lsp - z
    if mask2 is not None:
        lsn = jnp.where(mask2, lsn, 0.0)
    return lsp, lsn


def _sb_tile_mask(qt):
    trow = lax.broadcasted_iota(jnp.int32, (qt, 256), 0)
    tlane = lax.broadcasted_iota(jnp.int32, (qt, 256), 1) & 127
    return lambda m: (tlane + m * CHUNK) < trow


def sb_fwd(proj, *, name, job=None):
    T = proj.shape[0]
    qt = min(SB_QT, T)
    nsub = qt // CHUNK
    cb = C_SB // 128

    def body(q_ref, k_ref, v_ref, o_ref):
        u_gt, _, hms = _sb_consts()
        tile_mask = _sb_tile_mask(qt)

        def qtile(i, _):
            rq = pl.ds(pl.multiple_of(i * qt, qt), qt)
            qb = (q_ref[rq, :] * SB_SCALE).astype(BF16)

            def group(js, masks, state):
                carry, acc = list(state[:2]), state[2]
                rows = [pl.ds(pl.multiple_of(j * CHUNK, CHUNK), CHUNK) for j in js]
                logits = [_sb_logits(qb, _per_head(k_ref[rk, :], hms), m) for rk, m in zip(rows, masks)]
                sums = [[_dot(_pair(lsn[:, h * 128:(h + 1) * 128]), u_gt, NN) for h in range(2)] for _, lsn in logits]
                weights = []
                for (lsp, _), r, m in zip(logits, sums, masks):
                    a_b = []
                    for h in range(2):
                        hc = slice(h * 128, (h + 1) * 128)
                        a = jnp.exp(lsp[:, hc] + r[h][:, :128] + carry[h])
                        if m is not None:
                            a = jnp.where(m[:, hc], a, 0.0)
                        carry[h] = carry[h] + r[h][:, 128:]
                        a_b.append(a.astype(BF16))
                    weights.append(jnp.concatenate(a_b, axis=1))
                for rk, a in zip(rows, weights):
                    acc = acc + _dot(a, _per_head(v_ref[rk, :], hms), NN)
                return carry[0], carry[1], acc

            zero = jnp.zeros((qt, 128), F32)
            diag = list(reversed(range(nsub)))
            state = group([i * nsub + m for m in diag], [tile_mask(m) for m in diag], (zero, zero, zero))

            def live(c):
                return jnp.logical_and(c[0] < i, jnp.maximum(jnp.max(c[1][0]), jnp.max(c[1][1])) > SB_DEAD)

            def blocks(c):
                jj, st = c
                return jj + 1, group([(i - jj) * nsub - 1 - u for u in range(nsub)], [None] * nsub, st)

            _, state = lax.while_loop(live, blocks, (jnp.int32(0), state))
            o_ref[rq, :] = state[2]
            return 0

        lax.fori_loop(0, T // qt, qtile, 0)

    def col(off):
        return pl.BlockSpec((T, 128), lambda hp: (0, off + hp))

    steps = BRANCH_W // 128
    j = _job_args(job, 3, 1)
    res = pl.pallas_call(
        _hosting(body, job, 3, 1, 0, steps), name=name, grid=(steps,),
        in_specs=[col(cb), col(cb + 4), col(cb + 8)] + j["in_specs"], out_specs=[col(0)] + j["out_specs"],
        out_shape=[jax.ShapeDtypeStruct((T, BRANCH_W), F32)] + j["out_shape"],
        scratch_shapes=j["scratch"], input_output_aliases=j["aliases"],
        compiler_params=_params(("parallel",) if job is None else ("arbitrary",)),
    )(proj, proj, proj, *j["ins"])
    return res[0], list(res[1:])


def sb_bwd(proj, out, dout, *, name, job=None):
    T = proj.shape[0]
    qt = min(SB_QT, T)
    nsub = qt // CHUNK
    cb = C_SB // 128

    def body(q_ref, k_ref, v_ref, o_ref, do_ref, dq_ref, dk_ref, dv_ref, dkt_ref, dvt_ref):
        u_gt, u_ge, hms = _sb_consts()
        tile_mask = _sb_tile_mask(qt)
        tall_lane = lax.broadcasted_iota(jnp.int32, (qt, 128), 1)
        top = lax.broadcasted_iota(jnp.int32, (CHUNK, CHUNK), 0) < 64
        dkt_ref[...] = jnp.zeros_like(dkt_ref)
        dvt_ref[...] = jnp.zeros_like(dvt_ref)

        def qtile(i, _):
            rq = pl.ds(pl.multiple_of(i * qt, qt), qt)
            qs = q_ref[rq, :] * SB_SCALE
            qb, q_t = qs.astype(BF16), qs.T.astype(BF16)
            dov = do_ref[rq, :]
            dob, do_t = dov.astype(BF16), dov.T.astype(BF16)
            prod = dob.astype(F32) * o_ref[rq, :]
            total = [jnp.broadcast_to(jnp.sum(jnp.where(hm, prod, 0.0), axis=1, keepdims=True), (qt, 128))
                     for hm in (tall_lane < 64, tall_lane >= 64)]

            def group(js, masks, state):
                c_l, c_w, dq = list(state[:2]), list(state[2:4]), state[4]
                heads = [slice(h * 128, (h + 1) * 128) for h in range(2)]
                rows = [pl.ds(pl.multiple_of(j * CHUNK, CHUNK), CHUNK) for j in js]
                kb2 = [_per_head(k_ref[rk, :], hms) for rk in rows]
                logits = [_sb_logits(qb, kb, m) for kb, m in zip(kb2, masks)]
                da = [_dot(dob, _per_head(v_ref[rk, :], hms), NT) for rk in rows]
                sums = [[_dot(_pair(lsn[:, hc]), u_gt, NN) for hc in heads] for _, lsn in logits]
                a_b, w_all = [], []
                for (lsp, _), r, d, m in zip(logits, sums, da, masks):
                    a_h, w_h = [], []
                    for h, hc in enumerate(heads):
                        a = jnp.exp(lsp[:, hc] + r[h][:, :128] + c_l[h])
                        if m is not None:
                            a = jnp.where(m[:, hc], a, 0.0)
                        c_l[h] = c_l[h] + r[h][:, 128:]
                        a = a.astype(BF16)
                        a_h.append(a)
                        w_h.append(a.astype(F32) * d[:, hc])
                    a_b.append(jnp.concatenate(a_h, axis=1))
                    w_all.append(w_h)
                sums_w = [[_dot(_pair(w), u_ge, NN) for w in w_h] for w_h in w_all]
                dz_b = []
                for (lsp, _), w_h, r, m in zip(logits, w_all, sums_w, masks):
                    sp = jnp.exp(lsp)
                    dz_h = []
                    for h, hc in enumerate(heads):
                        later_w = r[h][:, :128] + c_w[h]
                        c_w[h] = c_w[h] + r[h][:, 128:]
                        dz = w_h[h] * (1.0 - sp[:, hc]) - sp[:, hc] * (total[h] - later_w)
                        if m is not None:
                            dz = jnp.where(m[:, hc], dz, 0.0)
                        dz_h.append(dz.astype(BF16))
                    dz_b.append(jnp.concatenate(dz_h, axis=1))
                for j, kb, a, dz in zip(js, kb2, a_b, dz_b):
                    dkt = _dot(q_t, dz, NN)
                    dvt = _dot(do_t, a, NN)
                    dkt_ref[j] += jnp.where(top, dkt[:, :128], dkt[:, 128:])
                    dvt_ref[j] += jnp.where(top, dvt[:, :128], dvt[:, 128:])
                    dq = dq + _dot(dz, kb, NN)
                return c_l[0], c_l[1], c_w[0], c_w[1], dq

            zero = jnp.zeros((qt, 128), F32)
            diag = list(reversed(range(nsub)))
            state = group([i * nsub + m for m in diag], [tile_mask(m) for m in diag], (zero,) * 5)

            def live(c):
                return jnp.logical_and(c[0] < i, jnp.maximum(jnp.max(c[1][0]), jnp.max(c[1][1])) > SB_DEAD)

            def blocks(c):
                jj, st = c
                return jj + 1, group([(i - jj) * nsub - 1 - u for u in range(nsub)], [None] * nsub, st)

            _, state = lax.while_loop(live, blocks, (jnp.int32(0), state))
            dq_ref[rq, :] = (state[4] * SB_SCALE).astype(BF16)
            return 0

        lax.fori_loop(0, T // qt, qtile, 0)

        def untranspose(jb, _):
            rk = pl.ds(pl.multiple_of(jb * CHUNK, CHUNK), CHUNK)
            dk_ref[rk, :] = dkt_ref[jb].T.astype(BF16)
            dv_ref[rk, :] = dvt_ref[jb].T.astype(BF16)
            return 0

        lax.fori_loop(0, T // CHUNK, untranspose, 0)

    def col(off):
        return pl.BlockSpec((T, 128), lambda hp: (0, off + hp))

    o16 = jax.ShapeDtypeStruct((T, BRANCH_W), BF16)
    steps = BRANCH_W // 128
    j = _job_args(job, 5, 3)
    acc = pltpu.VMEM((T // CHUNK, CHUNK, CHUNK), F32)
    res = pl.pallas_call(
        _hosting(body, job, 5, 3, 2, steps), name=name, grid=(steps,),
        in_specs=[col(cb), col(cb + 4), col(cb + 8), col(0), col(0)] + j["in_specs"],
        out_specs=[col(0), col(0), col(0)] + j["out_specs"], out_shape=[o16, o16, o16] + j["out_shape"],
        scratch_shapes=[acc, acc] + j["scratch"], input_output_aliases=j["aliases"],
        compiler_params=_params(("parallel",) if job is None else ("arbitrary",)),
    )(proj, proj, proj, out, dout, *j["ins"])
    return res[0], res[1], res[2], list(res[3:])


_G0 = math.sqrt(2.0 / math.pi)
_G1 = 0.044715


def _gelu(x):
    return 0.5 * x * (1.0 + jnp.tanh(_G0 * (x + _G1 * x * x * x)))


def _gelu_grad(x):
    t = jnp.tanh(_G0 * (x + _G1 * x * x * x))
    return 0.5 * (1.0 + t) + 0.5 * x * (1.0 - t * t) * (_G0 * (1.0 + 3.0 * _G1 * x * x))


def _tril():
    row, lane = _sb_masks()
    return row >= lane


def sgu_fwd(proj, ln_g, ln_b, w, bias, *, name):
    T = proj.shape[0]
    tb = min(512, T)
    G = BRANCH_W // 128

    def body(u_ref, v_ref, g_ref, b_ref, w_ref, bias_ref, o_ref):
        vv = _gelu(v_ref[...])
        xh, _ = _group_norm(vv)
        vn = (xh * g_ref[...] + b_ref[...]).astype(BF16)
        tril = _tril()
        for g in range(G):
            wg = jnp.where(tril, w_ref[g], 0.0).astype(BF16)
            gc = slice(g * 128, (g + 1) * 128)
            for c in range(tb // CHUNK):
                r = slice(c * CHUNK, (c + 1) * CHUNK)
                sv = _dot(wg, vn[r, gc], NN) + bias_ref[g]
                o_ref[r, gc] = _gelu(u_ref[r, gc]) * sv

    cu, cv = C_SGU // BRANCH_W, C_SGU // BRANCH_W + 1
    vec = pl.BlockSpec((1, BRANCH_W), lambda i: (0, 0))
    mat = pl.BlockSpec((G, CHUNK, CHUNK), lambda i: (0, 0, 0))
    return pl.pallas_call(
        body, name=name, grid=(T // tb,),
        in_specs=[pl.BlockSpec((tb, BRANCH_W), lambda i: (i, cu)), pl.BlockSpec((tb, BRANCH_W), lambda i: (i, cv)),
                  vec, vec, mat, mat],
        out_specs=pl.BlockSpec((tb, BRANCH_W), lambda i: (i, 0)),
        out_shape=jax.ShapeDtypeStruct((T, BRANCH_W), F32),
        compiler_params=_params(("parallel",)),
    )(proj, proj, ln_g, ln_b, w, bias)


def sgu_bwd(proj, ln_g, ln_b, w, bias, dout, *, name):
    T = proj.shape[0]
    tb = min(512, T)
    G = BRANCH_W // 128

    def body(u_ref, v_ref, g_ref, b_ref, w_ref, bias_ref, do_ref, dp_ref, dw_ref, dbias_ref, dg_ref, db_ref, dvn_ref):
        @pl.when(pl.program_id(0) == 0)
        def _():
            dw_ref[...] = jnp.zeros_like(dw_ref)
            dbias_ref[...] = jnp.zeros_like(dbias_ref)
            dg_ref[...] = jnp.zeros_like(dg_ref)
            db_ref[...] = jnp.zeros_like(db_ref)

        gv = v_ref[...]
        vv = _gelu(gv)
        xh, rstd = _group_norm(vv)
        vn = (xh * g_ref[...] + b_ref[...]).astype(BF16)
        tril = _tril()
        for g in range(G):
            wg = jnp.where(tril, w_ref[g], 0.0).astype(BF16)
            gc = slice(g * 128, (g + 1) * 128)
            for c in range(tb // CHUNK):
                r = slice(c * CHUNK, (c + 1) * CHUNK)
                vn_c = vn[r, gc]
                sv = _dot(wg, vn_c, NN) + bias_ref[g]
                gu = u_ref[r, gc]
                d_o = do_ref[r, gc]
                dp_ref[r, gc] = (d_o * sv * _gelu_grad(gu)).astype(BF16)
                dsv = d_o * _gelu(gu)
                dsv_b = dsv.astype(BF16)
                dvn_ref[r, gc] = _dot(wg, dsv_b, TN)
                dw_ref[g] += jnp.where(tril, _dot(dsv_b, vn_c, NT), 0.0)
                dbias_ref[g] += jnp.broadcast_to(jnp.sum(dsv, axis=1, keepdims=True), (CHUNK, CHUNK))
        dvn = dvn_ref[...]
        dg_ref[...] += jnp.sum(dvn * xh, axis=0, keepdims=True)
        db_ref[...] += jnp.sum(dvn, axis=0, keepdims=True)
        dxh = dvn * g_ref[...]
        m1 = jnp.mean(dxh, axis=-1, keepdims=True)
        m2 = jnp.mean(dxh * xh, axis=-1, keepdims=True)
        dp_ref[:, BRANCH_W:2 * BRANCH_W] = (rstd * (dxh - m1 - xh * m2) * _gelu_grad(gv)).astype(BF16)

    cu, cv = C_SGU // BRANCH_W, C_SGU // BRANCH_W + 1
    vec = pl.BlockSpec((1, BRANCH_W), lambda i: (0, 0))
    mat = pl.BlockSpec((G, CHUNK, CHUNK), lambda i: (0, 0, 0))
    blk = pl.BlockSpec((tb, BRANCH_W), lambda i: (i, 0))
    msh = jax.ShapeDtypeStruct((G, CHUNK, CHUNK), F32)
    vsh = jax.ShapeDtypeStruct((1, BRANCH_W), F32)
    return pl.pallas_call(
        body, name=name, grid=(T // tb,),
        in_specs=[pl.BlockSpec((tb, BRANCH_W), lambda i: (i, cu)), pl.BlockSpec((tb, BRANCH_W), lambda i: (i, cv)),
                  vec, vec, mat, mat, blk],
        out_specs=[pl.BlockSpec((tb, 2 * BRANCH_W), lambda i: (i, 0)), mat, mat, vec, vec],
        out_shape=[jax.ShapeDtypeStruct((T, 2 * BRANCH_W), BF16), msh, msh, vsh, vsh],
        scratch_shapes=[pltpu.VMEM((tb, BRANCH_W), F32)],
        compiler_params=_params(("arbitrary",)),
    )(proj, proj, ln_g, ln_b, w, bias, dout)


def merge_fwd(a1, a2, a3, p1, p2, p3, proj, *, name):
    T = a1.shape[0]
    tm, tn = min(1024, T), 512
    gb = C_GATE // tn

    def body(a1_ref, a2_ref, a3_ref, p1_ref, p2_ref, p3_ref, g1_ref, g2_ref, g3_ref, m_ref, r1_ref, r2_ref, r3_ref):
        m = None
        for a_ref, p_ref, g_ref, r_ref in ((a1_ref, p1_ref, g1_ref, r1_ref), (a2_ref, p2_ref, g2_ref, r2_ref),
                                           (a3_ref, p3_ref, g3_ref, r3_ref)):
            r = _dot(a_ref[...].astype(BF16), p_ref[...], NN)
            r_ref[...] = r.astype(r_ref.dtype)
            t = jax.nn.sigmoid(g_ref[...]) * r
            m = t if m is None else m + t
        m_ref[...] = m.astype(m_ref.dtype)

    a_spec = pl.BlockSpec((tm, BRANCH_W), lambda i, j: (i, 0))
    p_spec = pl.BlockSpec((BRANCH_W, tn), lambda i, j: (0, j))
    o_spec = pl.BlockSpec((tm, tn), lambda i, j: (i, j))
    osh = jax.ShapeDtypeStruct((T, D_MODEL), F32)
    gates = [pl.BlockSpec((tm, tn), functools.partial(lambda i, j, o: (i, o + j), o=gb + 2 * n)) for n in range(3)]
    return pl.pallas_call(
        body, name=name, grid=(T // tm, D_MODEL // tn),
        in_specs=[a_spec, a_spec, a_spec, p_spec, p_spec, p_spec, *gates],
        out_specs=[o_spec] * 4, out_shape=[jax.ShapeDtypeStruct((T, D_MODEL), BF16)] * 4,
        compiler_params=_params(("parallel", "parallel")),
    )(a1, a2, a3, p1, p2, p3, proj, proj, proj)


def _merge_bwd_epi(dm, r1, r2, r3, g1, g2, g3):
    d_r, d_g = [], []
    for r, g in ((r1, g1), (r2, g2), (r3, g3)):
        s = jax.nn.sigmoid(g)
        d_r.append(dm * s)
        d_g.append(dm * r.astype(F32) * (s * (1.0 - s)))
    return (*d_r, *d_g)


def _rows_call(fn, ins, out_dtypes, *, name, tr=256):
    first = ins[0][0] if isinstance(ins[0], tuple) else ins[0]
    R, C = first.shape[-2:]
    tr = min(tr, R)
    assert R % tr == 0, (name, R, tr)
    arrs, specs = [], []
    for x in ins:
        if isinstance(x, tuple):
            arrs.append(x[0])
            specs.append(pl.BlockSpec((None, tr, C), functools.partial(lambda i, n: (n, i, 0), n=x[1])))
        else:
            arrs.append(x)
            specs.append(pl.BlockSpec((tr, C), lambda i: (i, 0)))
    ni = len(arrs)

    def body(*refs):
        vals = fn(*[r[...] for r in refs[:ni]])
        for o_ref, v in zip(refs[ni:], vals):
            o_ref[...] = v.astype(o_ref.dtype)

    res = pl.pallas_call(
        body, name=name, grid=(R // tr,), in_specs=specs,
        out_specs=[pl.BlockSpec((tr, C), lambda i: (i, 0)) for _ in out_dtypes],
        out_shape=[jax.ShapeDtypeStruct((R, C), dt) for dt in out_dtypes],
        compiler_params=_params(("parallel",)),
    )(*arrs)
    return res


def _tile_rows(rows, cols):
    t = 256
    while t > 8 and (t * cols > 512 * 1024 or rows % t):
        t //= 2
    return t


def _rows_at(fn, pos, ins, outs, steps, *, name, aliases=None):
    read = [n for n, (_, s) in enumerate(ins) if s is not ANY]
    ni = len(ins)

    def body(pos_ref, *refs):
        vals = fn(*[refs[n][...] for n in read])
        for o_ref, v in zip(refs[ni:], vals):
            o_ref[...] = v.astype(o_ref.dtype)

    return pl.pallas_call(
        body, name=name,
        grid_spec=pltpu.PrefetchScalarGridSpec(num_scalar_prefetch=1, grid=(steps,), in_specs=[s for _, s in ins],
                                               out_specs=[s for _, s in outs]),
        out_shape=[sh for sh, _ in outs],
        input_output_aliases={1 + i: o for i, o in (aliases or {}).items()},
        compiler_params=_params(("parallel",)),
    )(pos, *[a for a, _ in ins])


def cast_into_whole(pos, w, l, axis, *, name):
    _, r, n = w.shape
    tr = _tile_rows(r, n)
    if axis == 1:
        shape, spec = (r, n * N_CHIPS), pl.BlockSpec((tr, n), lambda i, p: (i, p[3]))
    else:
        shape, spec = (r * N_CHIPS, n), pl.BlockSpec((tr, n), lambda i, p: (p[3] * (r // tr) + i, 0))
    return _rows_at(lambda a: (a,), pos, [(w, pl.BlockSpec((None, tr, n), lambda i, p: (l, i, 0)))],
                    [(jax.ShapeDtypeStruct(shape, BF16), spec)], r // tr, name=name)[0]


def pair_sum(pos, theirs, g32, axis, *, name):
    rows2, cols = theirs.shape
    h = rows2 // (N_CHIPS if axis == 0 else 1)
    tr = _tile_rows(h, cols)
    hb = h // tr
    if axis == 1:
        own = pl.BlockSpec((tr, cols), lambda i, p: (p[2] * hb + i, 0))
    else:
        own = pl.BlockSpec((tr, cols), lambda i, p: ((2 * (i // hb) + p[2]) * hb + i % hb, 0))
    row = pl.BlockSpec((tr, cols), lambda i, p: (i, 0))
    return _rows_at(lambda t, m: (m + t.astype(F32),) * 2, pos, [(theirs, row), (g32, own)],
                    [(jax.ShapeDtypeStruct((rows2, cols), F32), row), (jax.ShapeDtypeStruct((rows2, cols), BF16), row)],
                    rows2 // tr, name=name)


def chip_sum(pos, h32, recv, l, axis, whole, *, name):
    _, depth, h, n = recv.shape
    tr = _tile_rows(h, n)
    hb = h // tr
    if axis == 1:
        mine = pl.BlockSpec((tr, n), lambda i, p: (i, p[3]))
    else:
        mine = pl.BlockSpec((tr, n), lambda i, p: (p[3] * hb + i, 0))
    ins = [(h32, mine)] + [(recv, pl.BlockSpec((None, None, tr, n), functools.partial(lambda i, p, j: (j, l, i, 0), j=j)))
                           for j in range(3)]
    if whole is not None:
        ins.append((whole, ANY))
    return _rows_at(lambda o, a, b, c: (((o + a.astype(F32)) + b.astype(F32)) + c.astype(F32),), pos, ins,
                    [(jax.ShapeDtypeStruct((depth, 2, h, n), F32), pl.BlockSpec((None, None, tr, n), lambda i, p: (l, p[2], i, 0)))],
                    hb, name=name, aliases=None if whole is None else {4: 0})[0]


def _adamw(w, g, m, v):
    m2 = ADAM_B1 * m + (1.0 - ADAM_B1) * g
    v2 = ADAM_B2 * v + (1.0 - ADAM_B2) * (g * g)
    m_hat = m2 / (1.0 - ADAM_B1 ** ADAM_STEP)
    v_hat = v2 / (1.0 - ADAM_B2 ** ADAM_STEP)
    delta = -ADAM_LR * (m_hat / (jnp.sqrt(v_hat) + ADAM_EPS) + ADAM_WD * w)
    return delta, m2, v2


def _place():
    return lax.axis_index("x"), lax.axis_index("y"), lax.axis_index("c")


def _chip_peers(x, y, c):
    return [((1 - x, y, c), 2 * (1 - x) + y), ((x, 1 - y, c), 2 * x + 1 - y), ((1 - x, 1 - y, c), 2 * (1 - x) + 1 - y)]


def _shard_of(ref, axis, k, n):
    start = pl.multiple_of(k * n, 128)
    return ref.at[pl.ds(start, n), :] if axis == 0 else ref.at[:, pl.ds(start, n)]


ANY = pl.BlockSpec(memory_space=pl.ANY)


class CopyJob:
    def __init__(self, ins, out_shape, scratch, copies, aliases=None):
        self.ins, self.out_shape, self.scratch, self.copies = list(ins), list(out_shape), list(scratch), copies
        self.aliases = dict(aliases or {})

    def start(self, ins, outs, sems):
        local, remote, _, _ = self.copies(ins, outs, sems)
        for d in local + remote:
            d.start()

    def finish(self, ins, outs, sems):
        local, remote, arrivals, relays = self.copies(ins, outs, sems)
        for needs, sends, _ in relays:
            for d in needs:
                d.wait_recv()
            for d in sends:
                d.start()
        for d in arrivals + [d for _, _, arrives in relays for d in arrives]:
            d.wait_recv()
        for d in remote + [d for _, sends, _ in relays for d in sends]:
            d.wait_send()
        for d in local:
            d.wait()


def run_job(job, *, name):
    ni, no = len(job.ins), len(job.out_shape)

    def body(*refs):
        parts = refs[:ni], refs[ni:ni + no], refs[ni + no:]
        job.start(*parts)
        job.finish(*parts)

    return pl.pallas_call(
        body, name=name, in_specs=[ANY] * ni, out_specs=[ANY] * no, out_shape=job.out_shape,
        scratch_shapes=job.scratch, input_output_aliases=job.aliases,
    )(*job.ins)


def _job_args(job, n_in, n_out):
    if job is None:
        return dict(ins=[], in_specs=[], out_specs=[], out_shape=[], scratch=[], aliases={})
    return dict(ins=job.ins, in_specs=[ANY] * len(job.ins), out_specs=[ANY] * len(job.out_shape),
                out_shape=job.out_shape, scratch=job.scratch,
                aliases={n_in + i: n_out + o for i, o in job.aliases.items()})


def _hosting(body, job, n_in, n_out, n_scratch, grid):
    if job is None:
        return body
    ji, jo = len(job.ins), len(job.out_shape)
    grid = (grid,) if isinstance(grid, int) else tuple(grid)

    def at(ends):
        hit = None
        for ax, e in enumerate(ends):
            here = pl.program_id(ax) == e
            hit = here if hit is None else jnp.logical_and(hit, here)
        return hit

    def hosted(*refs):
        o = n_in + ji
        s = o + n_out + jo
        parts = refs[n_in:o], refs[o + n_out:s], refs[s + n_scratch:]

        @pl.when(at([0] * len(grid)))
        def _():
            job.start(*parts)

        body(*refs[:n_in], *refs[o:o + n_out], *refs[s:s + n_scratch])

        @pl.when(at([g - 1 for g in grid]))
        def _():
            job.finish(*parts)

    return hosted


def _job_sems(n_remote, n_local):
    return [pltpu.SemaphoreType.DMA((n_remote,)), pltpu.SemaphoreType.DMA((n_remote,)), pltpu.SemaphoreType.DMA((n_local,))]


def gather_job(shards, axes, chips=(0, 1, 2)):
    na = len(shards)

    def copies(ins, outs, sems):
        send, recv, _ = sems
        x, y, c = _place()
        k = 2 * x + y
        remote, relays = [], []
        for a in range(na):
            r = outs[a].shape[0] // (N_CHIPS if axes[a] == 0 else 1)
            n = outs[a].shape[axes[a]] // N_CHIPS
            half = r // 2

            def part(kk, cc, a=a, n=n, half=half):
                rows = pl.ds(pl.multiple_of(cc * half + (kk * n if axes[a] == 0 else 0), 8), half)
                return outs[a].at[rows, :] if axes[a] == 0 else outs[a].at[rows, pl.ds(pl.multiple_of(kk * n, 128), n)]

            needs, passes, lands = [], [], []
            for j, (peer, kp) in enumerate(_chip_peers(x, y, c)):
                if j not in chips:
                    continue
                s = 6 * a + j
                remote.append(pltpu.make_async_remote_copy(part(k, c), part(k, c), send.at[s], recv.at[s],
                                                           device_id=peer, device_id_type=MESH))
                needs.append(pltpu.make_async_remote_copy(part(kp, c), part(kp, c), send.at[s], recv.at[s],
                                                          device_id=peer, device_id_type=MESH))
                passes.append(pltpu.make_async_remote_copy(part(kp, c), part(kp, c), send.at[s + 3], recv.at[s + 3],
                                                           device_id=(x, y, 1 - c), device_id_type=MESH))
                lands.append(pltpu.make_async_remote_copy(part(kp, 1 - c), part(kp, 1 - c), send.at[s + 3], recv.at[s + 3],
                                                          device_id=(x, y, 1 - c), device_id_type=MESH))
            relays.append((needs, passes, lands))
        return [], remote, [], relays

    out_shape = [jax.ShapeDtypeStruct(w.shape, BF16) for w in shards]
    return CopyJob(shards, out_shape, _job_sems(6 * na, 1), copies, {a: a for a in range(na)})


def scatter_job(layers, g16, axes, filled, chips=(0, 1, 2)):
    na = len(axes)

    def shard_shape(a):
        r, c = g16[a].shape
        return (r // N_CHIPS, c) if axes[a] == 0 else (r, c // N_CHIPS)

    def copies(ins, outs, sems):
        send, recv_sems, _ = sems
        x, y, c = _place()
        remote = []
        for a in range(na):
            n = shard_shape(a)[axes[a]]
            for r, (peer, kp) in enumerate(_chip_peers(x, y, c)):
                if r not in chips:
                    continue
                remote.append(pltpu.make_async_remote_copy(_shard_of(ins[a], axes[a], kp, n), outs[a].at[r, layers[a]],
                                                           send.at[3 * a + r], recv_sems.at[3 * a + r],
                                                           device_id=peer, device_id_type=MESH))
        return [], remote, remote, []

    out_shape = [jax.ShapeDtypeStruct((3, DEPTH) + shard_shape(a), BF16) for a in range(na)]
    ins = list(g16)
    aliases = {}
    for a in range(na):
        if filled[a] is not None:
            aliases[len(ins)] = a
            ins.append(filled[a])
    return CopyJob(ins, out_shape, _job_sems(3 * na, 1), copies, aliases)


def pair_job(g16, axes):
    na = len(axes)
    pieces = [1 if ax == 1 else N_CHIPS for ax in axes]

    def copies(ins, outs, sems):
        send, recv, _ = sems
        x, y, c = _place()
        remote = []
        s = 0
        for a in range(na):
            rows = g16[a].shape[0] // (2 * pieces[a])
            for kk in range(pieces[a]):
                src = ins[a].at[pl.ds(pl.multiple_of((2 * kk + 1 - c) * rows, 8), rows), :]
                remote.append(pltpu.make_async_remote_copy(src, outs[a].at[pl.ds(kk * rows, rows), :], send.at[s], recv.at[s],
                                                           device_id=(x, y, 1 - c), device_id_type=MESH))
                s += 1
        return [], remote, remote, []

    out_shape = [jax.ShapeDtypeStruct((g.shape[0] // 2, g.shape[1]), BF16) for g in g16]
    return CopyJob(g16, out_shape, _job_sems(sum(pieces), 1), copies)


def join_job(shards):
    na = len(shards)

    def copies(ins, outs, sems):
        send, recv, _ = sems
        x, y, c = _place()
        remote = [pltpu.make_async_remote_copy(outs[a].at[:, c], outs[a].at[:, c], send.at[a], recv.at[a],
                                               device_id=(x, y, 1 - c), device_id_type=MESH) for a in range(na)]
        lands = [pltpu.make_async_remote_copy(outs[a].at[:, 1 - c], outs[a].at[:, 1 - c], send.at[a], recv.at[a],
                                              device_id=(x, y, 1 - c), device_id_type=MESH) for a in range(na)]
        return [], remote, lands, []

    out_shape = [jax.ShapeDtypeStruct(s.shape, F32) for s in shards]
    return CopyJob(shards, out_shape, _job_sems(na, 1), copies, {a: a for a in range(na)})


def small_job(p):
    def copies(ins, outs, sems):
        send, recv, loc = sems
        x, y, c = _place()
        me = 4 * x + 2 * y + c
        remote, lands = [], []
        for rel in range(1, 8):
            dx, dy, dc = rel >> 2, (rel >> 1) & 1, rel & 1
            peer = (1 - x if dx else x, 1 - y if dy else y, 1 - c if dc else c)
            who = 4 * peer[0] + 2 * peer[1] + peer[2]
            remote.append(pltpu.make_async_remote_copy(ins[0], outs[0].at[me], send.at[rel - 1], recv.at[rel - 1],
                                                       device_id=peer, device_id_type=MESH))
            lands.append(pltpu.make_async_remote_copy(ins[0], outs[0].at[who], send.at[rel - 1], recv.at[rel - 1],
                                                      device_id=peer, device_id_type=MESH))
        return [pltpu.make_async_copy(ins[0], outs[0].at[me], loc.at[0])], remote, lands, []

    return CopyJob([p], [jax.ShapeDtypeStruct((8,) + p.shape, F32)], _job_sems(7, 1), copies)


def small_sum(slots):
    def add(*terms):
        acc = terms[0]
        for t in terms[1:]:
            acc = acc + t
        return (acc,)

    return _rows_call(add, [(slots, d) for d in range(8)], [F32], name="small_sum", tr=8 * 47)[0]


BIG = ("w_in", "p_ret", "p_sb", "p_sgu", "w_out", "w_up", "w_down")
BIG_AXIS = {"w_in": 1, "p_ret": 1, "p_sb": 1, "p_sgu": 1, "w_out": 0, "w_up": 1, "w_down": 0}
SMALL = ("ret_gn_g", "ret_gn_b", "sgu_ln_g", "sgu_ln_b", "sgu_w", "sgu_b", "ln1_g", "ln1_b", "ln2_g", "ln2_b")


def layer_forward(l, x0, x0h, W, sm, rope, rconsts, hooks):
    n = f"l{l}_"
    job = hooks.fwd_job(l, "proj")
    proj = matmul(x0h, W["w_in"], mode="nn", tm=2048, tn=768, tk=1024, name=n + "proj", job=job)
    if job is not None:
        proj, job_out = proj
        hooks.done(job, job_out)
    retg, raw, states = ret_fwd(proj, *rope, rconsts, sm["ret_gn_g"], sm["ret_gn_b"], name=n + "ret_fwd")
    job = hooks.fwd_job(l, "sb")
    sb, job_out = sb_fwd(proj, name=n + "sb_fwd", job=job)
    if job is not None:
        hooks.done(job, job_out)
    sg = sgu_fwd(proj, sm["sgu_ln_g"], sm["sgu_ln_b"], sm["sgu_w"], sm["sgu_bias"], name=n + "sgu_fwd")
    merged, r1, r2, r3 = merge_fwd(retg, sb, sg, W["p_ret"], W["p_sb"], W["p_sgu"], proj, name=n + "merge_fwd")
    x1, xh1, rs1, x1h = matmul_ln(merged, W["w_out"], x0, sm["ln1_g"], sm["ln1_b"], tk=1024, name=n + "out_ln1")
    job = hooks.fwd_job(l, "up")
    h1 = matmul(x1h, W["w_up"], mode="nn", tm=1024, tn=1024, tk=1024, outs=((BF16, None),), name=n + "up", job=job)
    if job is not None:
        h1, job_out = h1
        hooks.done(job, job_out)
    job = hooks.fwd_job(l, "down")
    res = matmul_ln(h1, W["w_down"], x1, sm["ln2_g"], sm["ln2_b"], pro=_relu2, tk=1024, name=n + "down_ln2", job=job)
    if job is not None:
        res, job_out = res
        hooks.done(job, job_out)
    x2, xh2, rs2, x2h = res
    saved = dict(x0h=x0h, proj=proj, retg=retg, raw=raw, states=states, sb=sb, sg=sg, merged=merged, r=(r1, r2, r3),
                 x1h=x1h, xh1=xh1, rs1=rs1, h1=h1, xh2=xh2, rs2=rs2)
    return x2, x2h, saved


def layer_backward(l, dx2, s, W, sm, rope, rconsts, hooks):
    n = f"l{l}_"
    two = ((F32, None), (BF16, None))
    gw, gs = {}, {}
    job = hooks.bwd_job(l, "ln2")
    res = ln_bwd(dx2, s["xh2"], s["rs2"], sm["ln2_g"], name=n + "ln2_bwd", job=job)
    if job is not None:
        res, job_out = res
        hooks.done(job, job_out)
    du2, du2h, gs["ln2_g"], gs["ln2_b"] = res
    job = hooks.bwd_job(l, "g_down")
    gw["w_down"] = matmul(s["h1"], du2h, mode="tn", tm=1024, tn=1024, tk=2048, pro=_relu2, outs=two, name=n + "g_down", job=job)
    if job is not None:
        gw["w_down"], job_out = gw["w_down"]
        hooks.done(job, job_out)
    dh1 = matmul(du2h, W["w_down"], mode="nt", tm=1024, tn=1024, tk=1024, outs=((BF16, None),),
                 epi=lambda acc, h: (acc * (2.0 * jnp.maximum(h.astype(F32), 0.0)),), tiles=(s["h1"],), name=n + "d_h1")
    job = hooks.bwd_job(l, "g_up")
    gw["w_up"] = matmul(s["x1h"], dh1, mode="tn", tm=1024, tn=1024, tk=2048, outs=two, name=n + "g_up", job=job)
    if job is not None:
        gw["w_up"], job_out = gw["w_up"]
        hooks.done(job, job_out)
    dx1 = matmul(dh1, W["w_up"], mode="nt", tm=1024, tn=1024, tk=2048,
                 epi=lambda acc, d: (acc + ALPHA * d,), tiles=(du2,), name=n + "d_x1")
    du1, du1h, gs["ln1_g"], gs["ln1_b"] = ln_bwd(dx1, s["xh1"], s["rs1"], sm["ln1_g"], name=n + "ln1_bwd")
    gw["w_out"] = matmul(s["merged"], du1h, mode="tn", tm=1024, tn=1024, tk=2048, outs=two, name=n + "g_out")
    gate0 = C_GATE // 512
    dr1, dr2, dr3, dg1, dg2, dg3 = matmul(
        du1h, W["w_out"], mode="nt", tm=1024, tn=512, tk=1024, outs=((BF16, None),) * 6, epi=_merge_bwd_epi,
        tiles=(*s["r"], (s["proj"], gate0), (s["proj"], gate0 + 2), (s["proj"], gate0 + 4)), name=n + "d_merged")
    d_branch = {}
    for nm, a, dr in (("p_ret", s["retg"], dr1), ("p_sb", s["sb"], dr2), ("p_sgu", s["sg"], dr3)):
        gw[nm] = matmul(a, dr, mode="tn", tm=512, tn=1024, tk=2048, outs=two, name=n + "g_" + nm)
        d_branch[nm] = matmul(dr, W[nm], mode="nt", tm=1024, tn=512, tk=1024, name=n + "d_" + nm)
    job = hooks.pair(l, gw)
    dret, gs["ret_gn_g"], gs["ret_gn_b"], job_out = ret_bwd(s["proj"], *rope, rconsts, sm["ret_gn_g"], sm["ret_gn_b"],
                                                             s["raw"], s["states"], d_branch["p_ret"], name=n + "ret_bwd", job=job)
    if job is not None:
        hooks.done(job, job_out)
    job = hooks.scatter(l) if job is not None else None
    dsq, dsk, dsv, job_out = sb_bwd(s["proj"], s["sb"], d_branch["p_sb"], name=n + "sb_bwd", job=job)
    if job is not None:
        hooks.done(job, job_out)
    dsgu, gs["sgu_w"], dbias, gs["sgu_ln_g"], gs["sgu_ln_b"] = sgu_bwd(
        s["proj"], sm["sgu_ln_g"], sm["sgu_ln_b"], sm["sgu_w"], sm["sgu_bias"], d_branch["p_sgu"], name=n + "sgu_bwd")
    gs["sgu_b"] = dbias[:, :, 0]
    dproj = jnp.concatenate([dret, dsq, dsk, dsv, dsgu, dg1, dg2, dg3], axis=1)
    job = hooks.small(l, gs)
    gw["w_in"] = matmul(s["x0h"], dproj, mode="tn", tm=1024, tn=1536, tk=1024, outs=two, name=n + "g_in", job=job)
    if job is not None:
        gw["w_in"], job_out = gw["w_in"]
        hooks.done(job, job_out)
    job = hooks.tail(l, gw["w_in"])
    dx0 = matmul(dproj, W["w_in"], mode="nt", tm=1024, tn=1024, tk=2560,
                 epi=lambda acc, d: (acc + ALPHA * d,), tiles=(du1,), name=n + "d_x0", job=job)
    if job is not None:
        dx0, job_out = dx0
        hooks.done(job, job_out)
    return dx0, gw, gs


def local_step(x, target, small, plan):
    T = x.shape[0]
    rope = _rope_tables(T)
    rconsts = _ret_consts()
    sms = []
    for l in range(DEPTH):
        sm = {k: small[k][l][None, :] for k in SMALL if k not in ("sgu_w", "sgu_b")}
        sm["sgu_w"] = small["sgu_w"][l]
        sm["sgu_bias"] = jnp.broadcast_to(small["sgu_b"][l][:, :, None], (4, CHUNK, CHUNK))
        sms.append(sm)
    h, saved = x, []
    hh = _rows_call(lambda a: (a,), [x], [BF16], name="cast_x")[0]
    for l in range(DEPTH):
        h, hh, s = layer_forward(l, h, hh, plan.weights(l), sms[l], rope, rconsts, plan)
        saved.append(s)
    dy, sq = loss_head(h, target)
    gs = {k: [None] * DEPTH for k in SMALL}
    for l in reversed(range(DEPTH)):
        dy, gwl, gsl = layer_backward(l, dy, saved[l], plan.weights(l), sms[l], rope, rconsts, plan)
        plan.grads(l, gwl)
        for k in SMALL:
            gs[k][l] = gsl[k].reshape(small[k].shape[1:])
    return sq[0, 0], dy, {k: jnp.stack(v) for k, v in gs.items()}


EARLY_GRADS = ("p_ret", "p_sb", "p_sgu", "w_out", "w_up", "w_down")


class _StepPlan:
    def __init__(self, pos, shards16):
        self.pos = pos
        self.shards16 = shards16
        self.full = [dict() for _ in range(DEPTH)]
        self.gw = [None] * DEPTH
        self.bufs = {}
        self.sums = {}
        self.gs = [None] * DEPTH
        first = self._gather([(0, "w_in")])
        self.done(first, run_job(first, name="gather_first"))

    def weights(self, l):
        return self.full[l]

    def grads(self, l, gw):
        self.gw[l] = gw

    def _gather(self, items, chips=(0, 1, 2)):
        job = gather_job([self.shards16[l][k] for l, k in items], [BIG_AXIS[k] for _, k in items], chips)
        job.note = ("gather" if 2 in chips else "gather_part", items)
        return job

    def _pair(self, items):
        job = pair_job([g[1] for _, _, g in items], [BIG_AXIS[k] for _, k, _ in items])
        job.note = ("pair", items)
        return job

    def fwd_job(self, l, host):
        if host == "proj":
            return None
        if host == "sb":
            return self._gather([(l, k) for k in BIG[1:]])
        if l + 1 == DEPTH:
            return None
        return self._gather([(l + 1, "w_in")], (0, 1) if host == "up" else (2,))

    def bwd_job(self, l, host):
        if l + 1 == DEPTH:
            return None
        if host == "ln2":
            job = self._pair([(l + 1, "w_in", self.gw[l + 1]["w_in"])])
            job.note = ("pair_w_in", job.note[1])
            return job
        items, sums16 = self.summed_w_in
        job = scatter_job([l_ for l_, _, _ in items], sums16, [BIG_AXIS[k] for _, k, _ in items],
                          [self.bufs.get(k) for _, k, _ in items], (0, 1) if host == "g_down" else (2,))
        job.note = ("scatter", items)
        return job

    def pair(self, l, ready):
        return self._pair([(l, k, ready[k]) for k in EARLY_GRADS])

    def scatter(self, l):
        items, sums16 = self.summed
        job = scatter_job([l_ for l_, _, _ in items], sums16, [BIG_AXIS[k] for _, k, _ in items],
                          [self.bufs.get(k) for _, k, _ in items])
        job.note = ("scatter", items)
        return job

    def small(self, l, gs):
        self.gs[l] = {k: gs[k].reshape(-1) for k in SMALL}
        if l != 0:
            return None
        job = small_job(_pack_small({k: jnp.stack([self.gs[l_][k] for l_ in range(DEPTH)]) for k in SMALL}))
        job.note = ("small", [])
        return job

    def tail(self, l, g):
        if l != 0:
            return None
        last = self._pair([(0, "w_in", g)])
        self.done(last, run_job(last, name="pair_last"))
        return self.scatter(0)

    def done(self, job, outs):
        kind, items = job.note
        if kind == "small":
            self.small_slots = outs[0]
        if kind in ("pair", "pair_w_in"):
            sums16 = []
            for a, (l, k, g) in enumerate(items):
                self.sums[(l, k)], s16 = pair_sum(self.pos, outs[a], g[0], BIG_AXIS[k], name=f"pair_sum_{k}_{l}")
                sums16.append(s16)
            if kind == "pair":
                self.summed = (items, sums16)
            else:
                self.summed_w_in = (items, sums16)
        for a, item in enumerate(items):
            if kind == "gather_part":
                self.shards16[item[0]][item[1]] = outs[a]
            elif kind == "gather":
                self.full[item[0]][item[1]] = outs[a]
            elif kind == "scatter":
                self.bufs[item[1]] = outs[a]

    def finish(self):
        return self.bufs, self.sums


def _flat2(a):
    return a.reshape(-1, a.shape[-1])


def _pack_small(d, pre=""):
    return jnp.concatenate([d[pre + k].reshape(-1) for k in SMALL]).reshape(-1, 128)


def kernel(x, w_in, ret_gn_g, ret_gn_b, sgu_ln_g, sgu_ln_b, sgu_w, sgu_b, p_ret, p_sb, p_sgu, w_out, ln1_g, ln1_b, w_up, w_down, ln2_g, ln2_b, loss_target, m_w_in, m_ret_gn_g, m_ret_gn_b, m_sgu_ln_g, m_sgu_ln_b, m_sgu_w, m_sgu_b, m_p_ret, m_p_sb, m_p_sgu, m_w_out, m_ln1_g, m_ln1_b, m_w_up, m_w_down, m_ln2_g, m_ln2_b, v_w_in, v_ret_gn_g, v_ret_gn_b, v_sgu_ln_g, v_sgu_ln_b, v_sgu_w, v_sgu_b, v_p_ret, v_p_sb, v_p_sgu, v_w_out, v_ln1_g, v_ln1_b, v_w_up, v_w_down, v_ln2_g, v_ln2_b):
    given = dict(locals())
    order = BIG[:1] + SMALL[:6] + BIG[1:5] + SMALL[6:8] + BIG[5:7] + SMALL[8:10]
    L = DEPTH

    px, py, pc = _place()
    pos = jnp.stack([px, py, pc, 2 * px + py]).astype(jnp.int32)

    shards16 = [{k: cast_into_whole(pos, given[k], l, BIG_AXIS[k], name=f"cast_{k}_{l}") for k in BIG} for l in range(L)]
    plan = _StepPlan(pos, shards16)
    sq, dx, gs = local_step(x[0], loss_target[0], {k: given[k] for k in SMALL}, plan)
    loss = 0.5 * lax.psum(sq, ("x", "y", "c"))

    bufs, sums = plan.finish()
    shards = []
    for k in BIG:
        whole = None
        for l in range(L):
            whole = chip_sum(pos, sums[(l, k)], bufs[k], l, BIG_AXIS[k], whole, name=f"chip_sum_{k}_{l}")
        shards.append(whole)
    joined = run_job(join_job(shards), name="join_halves")
    out = {}
    for a, k in enumerate(BIG):
        shp = given[k].shape
        res = _rows_call(lambda g_, w_, m_, v_: (g_,) + _adamw(w_, g_, m_, v_),
                         [joined[a].reshape(-1, shp[-1]), _flat2(given[k]), _flat2(given["m_" + k]), _flat2(given["v_" + k])],
                         [F32] * 4, name="adamw_" + k)
        out[k] = [r.reshape(shp) for r in res]

    pack = _pack_small
    res = _rows_call(lambda g_, w_, m_, v_: (g_,) + _adamw(w_, g_, m_, v_),
                     [small_sum(plan.small_slots), pack(given), pack(given, "m_"), pack(given, "v_")], [F32] * 4,
                     name="adamw_small", tr=8 * 47)
    off = 0
    for k in SMALL:
        sz = given[k].size
        out[k] = [r.reshape(-1)[off:off + sz].reshape(given[k].shape) for r in res]
        off += sz

    grads = [out[k][0] for k in order]
    deltas = [out[k][1] for k in order]
    new_m = [out[k][2] for k in order]
    new_v = [out[k][3] for k in order]
    return (loss, dx[None], *grads, *deltas, *new_m, *new_v)
```

```python
import functools
import math

import jax
import jax.numpy as jnp
from jax import lax
from jax.experimental import pallas as pl
from jax.experimental.pallas import tpu as pltpu

F32 = jnp.float32
BF16 = jnp.bfloat16

D_MODEL = 1024
SEQ = 4096
DEPTH = 2
CHUNK = 128
RET_HEADS = 4
BRANCH_W = 512
N_IN = 7680
D_FF = 4096
LN_EPS = 1e-5
ROPE_BASE = 10000.0
ALPHA = (2 * DEPTH) ** 0.25
RET_SCALE = 128 ** -0.5
SB_SCALE = 64 ** -0.5
C_RET, C_SB, C_SGU, C_GATE = 0, 2048, 3584, 4608

ADAM_LR, ADAM_B1, ADAM_B2, ADAM_EPS, ADAM_WD, ADAM_STEP = 0.001, 0.9, 0.999, 1e-08, 0.01, 10

N_CHIPS = 4
VMEM_LIMIT = 56 * 1024 * 1024
MESH = pl.DeviceIdType.MESH

NN = ((1,), (0,))
NT = ((1,), (1,))
TN = ((0,), (0,))


def _dot(a, b, dims):
    return lax.dot_general(a, b, (dims, ((), ())), preferred_element_type=F32)


def _params(sem):
    return pltpu.CompilerParams(dimension_semantics=sem, vmem_limit_bytes=VMEM_LIMIT)


def _relu2(h):
    r = jnp.maximum(h.astype(F32), 0.0)
    return r * r


def matmul(a, b, *, mode, tm, tn, tk, outs=((F32, None),), pro=None, epi=None, tiles=(), rows=(), name, job=None):
    if mode == "nn":
        (M, K), N = a.shape, b.shape[1]
    elif mode == "nt":
        (M, K), N = a.shape, b.shape[0]
    else:
        (K, M), N = a.shape, b.shape[1]
    tm, tn, tk = min(tm, M), min(tn, N), min(tk, K)
    assert M % tm == 0 and N % tn == 0 and K % tk == 0, (name, M, N, K, tm, tn, tk)
    if mode == "nn":
        a_spec = pl.BlockSpec((tm, tk), lambda i, j, k: (i, k))
        b_spec = pl.BlockSpec((tk, tn), lambda i, j, k: (k, j))
        dims = NN
    elif mode == "nt":
        a_spec = pl.BlockSpec((tm, tk), lambda i, j, k: (i, k))
        b_spec = pl.BlockSpec((tn, tk), lambda i, j, k: (j, k))
        dims = NT
    else:
        a_spec = pl.BlockSpec((tk, tm), lambda i, j, k: (k, i))
        b_spec = pl.BlockSpec((tk, tn), lambda i, j, k: (k, j))
        dims = TN
    nk = K // tk
    nt_, nr, no = len(tiles), len(rows), len(outs)

    def body(a_ref, b_ref, *rest):
        tile_refs = rest[:nt_]
        row_refs = rest[nt_:nt_ + nr]
        out_refs = rest[nt_ + nr:nt_ + nr + no]
        av = a_ref[...]
        if pro is not None:
            av = pro(av)
        p = _dot(av.astype(BF16), b_ref[...].astype(BF16), dims)

        def finish(acc):
            vals = (acc,) * no if epi is None else epi(acc, *[r[...] for r in tile_refs], *[r[...] for r in row_refs])
            for o_ref, v in zip(out_refs, vals):
                o_ref[...] = v.astype(o_ref.dtype)

        if nk == 1:
            finish(p)
        else:
            acc_ref = rest[-1]
            k = pl.program_id(2)

            @pl.when(k == 0)
            def _():
                acc_ref[...] = p

            @pl.when(k > 0)
            def _():
                acc_ref[...] += p

            @pl.when(k == nk - 1)
            def _():
                finish(acc_ref[...])

    out_shape, out_specs = [], []
    for dt, width in outs:
        if width is None:
            out_shape.append(jax.ShapeDtypeStruct((M, N), dt))
            out_specs.append(pl.BlockSpec((tm, tn), lambda i, j, k: (i, j)))
        else:
            assert N == tn
            out_shape.append(jax.ShapeDtypeStruct((M, width), dt))
            out_specs.append(pl.BlockSpec((tm, width), lambda i, j, k: (i, 0)))
    in_specs = [a_spec, b_spec]
    offs = [t[1] if isinstance(t, tuple) else 0 for t in tiles]
    tiles = [t[0] if isinstance(t, tuple) else t for t in tiles]
    in_specs += [pl.BlockSpec((tm, tn), functools.partial(lambda i, j, k, o: (i, j + o), o=o)) for o in offs]
    in_specs += [pl.BlockSpec((1, tn), lambda i, j, k: (0, j)) for _ in rows]
    grid = (M // tm, N // tn, nk)
    scratch = [pltpu.VMEM((tm, tn), F32)] if nk > 1 else []
    j = _job_args(job, len(in_specs), no)
    res = pl.pallas_call(
        _hosting(body, job, len(in_specs), no, len(scratch), grid), name=name, grid=grid,
        in_specs=in_specs + j["in_specs"], out_specs=out_specs + j["out_specs"], out_shape=out_shape + j["out_shape"],
        scratch_shapes=scratch + j["scratch"], input_output_aliases=j["aliases"],
        compiler_params=_params(("parallel", "parallel", "arbitrary") if job is None else ("arbitrary",) * 3),
    )(a, b, *tiles, *rows, *j["ins"])
    mine = res[0] if no == 1 else list(res[:no])
    return mine if job is None else (mine, list(res[no:]))


def _ln_epi(acc, res, g, b):
    u = ALPHA * res + acc
    mu = jnp.mean(u, axis=-1, keepdims=True)
    xc = u - mu
    var = jnp.mean(xc * xc, axis=-1, keepdims=True)
    rstd = lax.rsqrt(var + LN_EPS)
    xhat = xc * rstd
    y = xhat * g + b
    return y, xhat, jnp.broadcast_to(rstd, (u.shape[0], 128)), y


def matmul_ln(a, w, res, g, b, *, pro=None, tk, name, job=None):
    n = w.shape[1]
    return matmul(a, w, mode="nn", tm=1024, tn=n, tk=tk, pro=pro, epi=_ln_epi, tiles=(res,), rows=(g, b),
                  outs=((F32, None), (F32, None), (F32, 128), (BF16, None)), name=name, job=job)


def ln_bwd(dy, xhat, rstd, g, *, name, job=None):
    T, D = dy.shape
    tm = min(512, T)

    def body(dy_ref, xh_ref, rs_ref, g_ref, du_ref, du16_ref, dg_ref, db_ref):
        dyv, xh = dy_ref[...], xh_ref[...]
        r = rs_ref[:, 0:1]
        dxh = dyv * g_ref[...]
        m1 = jnp.mean(dxh, axis=-1, keepdims=True)
        m2 = jnp.mean(dxh * xh, axis=-1, keepdims=True)
        du = r * (dxh - m1 - xh * m2)
        du_ref[...] = du
        du16_ref[...] = du.astype(BF16)

        @pl.when(pl.program_id(0) == 0)
        def _():
            dg_ref[...] = jnp.zeros_like(dg_ref)
            db_ref[...] = jnp.zeros_like(db_ref)

        dg_ref[...] += jnp.sum(dyv * xh, axis=0, keepdims=True)
        db_ref[...] += jnp.sum(dyv, axis=0, keepdims=True)

    row = pl.BlockSpec((tm, D), lambda i: (i, 0))
    vec = pl.BlockSpec((1, D), lambda i: (0, 0))
    j = _job_args(job, 4, 4)
    res = pl.pallas_call(
        _hosting(body, job, 4, 4, 0, T // tm), name=name, grid=(T // tm,),
        in_specs=[row, row, pl.BlockSpec((tm, 128), lambda i: (i, 0)), vec] + j["in_specs"],
        out_specs=[row, row, vec, vec] + j["out_specs"],
        out_shape=[jax.ShapeDtypeStruct((T, D), F32), jax.ShapeDtypeStruct((T, D), BF16),
                   jax.ShapeDtypeStruct((1, D), F32), jax.ShapeDtypeStruct((1, D), F32)] + j["out_shape"],
        scratch_shapes=j["scratch"], input_output_aliases=j["aliases"],
        compiler_params=_params(("arbitrary",)),
    )(dy, xhat, rstd, g, *j["ins"])
    return list(res[:4]) if job is None else (list(res[:4]), list(res[4:]))


def loss_head(y, target):
    T, D = y.shape
    tm = min(512, T)

    def body(y_ref, t_ref, dy_ref, s_ref):
        e = y_ref[...] - t_ref[...]
        dy_ref[...] = e * (1.0 / D)

        @pl.when(pl.program_id(0) == 0)
        def _():
            s_ref[...] = jnp.zeros_like(s_ref)

        s_ref[...] += jnp.sum(jnp.mean(e * e, axis=-1, keepdims=True))

    row = pl.BlockSpec((tm, D), lambda i: (i, 0))
    return pl.pallas_call(
        body, name="loss_head", grid=(T // tm,),
        in_specs=[row, row], out_specs=[row, pl.BlockSpec((8, 128), lambda i: (0, 0))],
        out_shape=[jax.ShapeDtypeStruct((T, D), F32), jax.ShapeDtypeStruct((8, 128), F32)],
        compiler_params=_params(("arbitrary",)),
    )(y, target)


def _rope_tables(T):
    half = 64
    inv_freq = ROPE_BASE ** (-jnp.arange(half, dtype=F32) / half)
    ang = jnp.arange(T, dtype=jnp.int32).astype(F32)[:, None] * inv_freq[None, :]
    cos, sin = jnp.cos(ang), jnp.sin(ang)
    return jnp.concatenate([cos, cos], axis=1), jnp.concatenate([-sin, sin], axis=1)


def _ret_consts():
    H = RET_HEADS
    log_g = jnp.log(1.0 - 2.0 ** (-5.0 - jnp.arange(H, dtype=F32)))
    idx = jnp.arange(CHUNK, dtype=F32)
    diff = idx[:, None] - idx[None, :]
    dmat = jnp.where(diff[None] >= 0, jnp.exp(log_g[:, None, None] * diff[None]), 0.0)
    kd = jnp.exp(log_g[:, None] * (CHUNK - 1 - idx)[None, :])
    qd = jnp.exp(log_g[:, None] * (idx + 1.0)[None, :])
    cd = jnp.exp(log_g * CHUNK)
    full = (H, CHUNK, CHUNK)
    return (dmat.astype(F32), jnp.broadcast_to(kd[:, :, None], full), jnp.broadcast_to(qd[:, :, None], full),
            jnp.broadcast_to(cd[:, None, None], full))


def _swap_halves(v):
    return pltpu.roll(v, 64, 1)


def _group_norm(o):
    mu = jnp.mean(o, axis=-1, keepdims=True)
    xc = o - mu
    var = jnp.mean(xc * xc, axis=-1, keepdims=True)
    rstd = lax.rsqrt(var + LN_EPS)
    return xc * rstd, rstd


def ret_fwd(proj, cosf, sinf, consts, gn_g, gn_b, *, name):
    T = proj.shape[0]
    tb = min(512, T)
    nch = tb // CHUNK
    H = RET_HEADS

    def body(p_ref, cos_ref, sin_ref, dm_ref, kd_ref, qd_ref, cd_ref, g_ref, b_ref, out_ref, raw_ref, st_ref, s_ref):
        @pl.when(pl.program_id(0) == 0)
        def _():
            s_ref[...] = jnp.zeros_like(s_ref)

        for c in range(nch):
            r = slice(c * CHUNK, (c + 1) * CHUNK)
            cs, sn = cos_ref[r, :], sin_ref[r, :]
            for h in range(H):
                hc = slice(h * 128, (h + 1) * 128)
                q = p_ref[r, h * 128:(h + 1) * 128]
                k = p_ref[r, 512 + h * 128:512 + (h + 1) * 128]
                v = p_ref[r, 1024 + h * 128:1024 + (h + 1) * 128]
                gt = p_ref[r, 1536 + h * 128:1536 + (h + 1) * 128]
                qr = q * cs + _swap_halves(q) * sn
                kr = (k * cs + _swap_halves(k) * sn) * RET_SCALE
                sprev = s_ref[h]
                st_ref[c, h] = sprev
                qb, kb, vb = qr.astype(BF16), kr.astype(BF16), v.astype(BF16)
                s = _dot(qb, kb, NT) * dm_ref[h]
                o = _dot(s.astype(BF16), vb, NN) + _dot((qr * qd_ref[h]).astype(BF16), sprev.astype(BF16), NN)
                s_ref[h] = sprev * cd_ref[h] + _dot((kr * kd_ref[h]).astype(BF16), vb, TN)
                raw_ref[r, hc] = o
                y, _ = _group_norm(o)
                out_ref[r, hc] = (gt * jax.nn.sigmoid(gt)) * (y * g_ref[:, hc] + b_ref[:, hc])

    cmat = pl.BlockSpec((H, CHUNK, CHUNK), lambda i: (0, 0, 0))
    vec = pl.BlockSpec((1, BRANCH_W), lambda i: (0, 0))
    rope = pl.BlockSpec((tb, 128), lambda i: (i, 0))
    blk = pl.BlockSpec((tb, BRANCH_W), lambda i: (i, 0))
    return pl.pallas_call(
        body, name=name, grid=(T // tb,),
        in_specs=[pl.BlockSpec((tb, 2048), lambda i: (i, 0)), rope, rope, cmat, cmat, cmat, cmat, vec, vec],
        out_specs=[blk, blk, pl.BlockSpec((nch, H, CHUNK, CHUNK), lambda i: (i, 0, 0, 0))],
        out_shape=[jax.ShapeDtypeStruct((T, BRANCH_W), F32), jax.ShapeDtypeStruct((T, BRANCH_W), F32),
                   jax.ShapeDtypeStruct((T // CHUNK, H, CHUNK, CHUNK), F32)],
        scratch_shapes=[pltpu.VMEM((H, CHUNK, CHUNK), F32)],
        compiler_params=_params(("arbitrary",)),
    )(proj, cosf, sinf, *consts, gn_g, gn_b)


def ret_bwd(proj, cosf, sinf, consts, gn_g, gn_b, raw, states, dout, *, name, job=None):
    T = proj.shape[0]
    tb = min(512, T)
    nch = tb // CHUNK
    nb = T // tb
    H = RET_HEADS

    def body(p_ref, cos_ref, sin_ref, dm_ref, kd_ref, qd_ref, cd_ref, g_ref, b_ref, raw_ref, st_ref, do_ref,
             dp_ref, dg_ref, db_ref, ds_ref):
        @pl.when(pl.program_id(0) == 0)
        def _():
            ds_ref[...] = jnp.zeros_like(ds_ref)
            dg_ref[...] = jnp.zeros_like(dg_ref)
            db_ref[...] = jnp.zeros_like(db_ref)

        for c in reversed(range(nch)):
            r = slice(c * CHUNK, (c + 1) * CHUNK)
            cs, sn = cos_ref[r, :], sin_ref[r, :]
            for h in range(H):
                hc = slice(h * 128, (h + 1) * 128)
                q = p_ref[r, h * 128:(h + 1) * 128]
                k = p_ref[r, 512 + h * 128:512 + (h + 1) * 128]
                v = p_ref[r, 1024 + h * 128:1024 + (h + 1) * 128]
                gt = p_ref[r, 1536 + h * 128:1536 + (h + 1) * 128]
                qr = q * cs + _swap_halves(q) * sn
                kr = (k * cs + _swap_halves(k) * sn) * RET_SCALE
                sprev = st_ref[c, h]
                gv = g_ref[:, hc]
                y, rstd = _group_norm(raw_ref[r, hc])
                d_out = do_ref[r, hc]
                sg = jax.nn.sigmoid(gt)
                d_gate = d_out * (y * gv + b_ref[:, hc]) * (sg * (1.0 + gt * (1.0 - sg)))
                d_aff = d_out * (gt * sg)
                dg_ref[:, hc] += jnp.sum(d_aff * y, axis=0, keepdims=True)
                db_ref[:, hc] += jnp.sum(d_aff, axis=0, keepdims=True)
                dxh = d_aff * gv
                m1 = jnp.mean(dxh, axis=-1, keepdims=True)
                m2 = jnp.mean(dxh * y, axis=-1, keepdims=True)
                d_o = (rstd * (dxh - m1 - y * m2)).astype(BF16)
                qb, kb, vb = qr.astype(BF16), kr.astype(BF16), v.astype(BF16)
                dm, kd, qd = dm_ref[h], kd_ref[h], qd_ref[h]
                p = (_dot(qb, kb, NT) * dm).astype(BF16)
                dp = (_dot(d_o, vb, NT) * dm).astype(BF16)
                dsn = ds_ref[h]
                dsb = dsn.astype(BF16)
                dq_r = _dot(dp, kb, NN) + _dot(d_o, sprev.astype(BF16), NT) * qd
                dk_r = (_dot(dp, qb, TN) + _dot(vb, dsb, NT) * kd) * RET_SCALE
                d_v = _dot(p, d_o, TN) + _dot((kr * kd).astype(BF16), dsb, NN)
                ds_ref[h] = dsn * cd_ref[h] + _dot((qr * qd).astype(BF16), d_o, TN)
                dp_ref[r, h * 128:(h + 1) * 128] = (dq_r * cs - _swap_halves(dq_r) * sn).astype(BF16)
                dp_ref[r, 512 + h * 128:512 + (h + 1) * 128] = (dk_r * cs - _swap_halves(dk_r) * sn).astype(BF16)
                dp_ref[r, 1024 + h * 128:1024 + (h + 1) * 128] = d_v.astype(BF16)
                dp_ref[r, 1536 + h * 128:1536 + (h + 1) * 128] = d_gate.astype(BF16)

    cmat = pl.BlockSpec((H, CHUNK, CHUNK), lambda i: (0, 0, 0))
    vec = pl.BlockSpec((1, BRANCH_W), lambda i: (0, 0))
    rope = pl.BlockSpec((tb, 128), lambda i: (nb - 1 - i, 0))
    blk = pl.BlockSpec((tb, BRANCH_W), lambda i: (nb - 1 - i, 0))
    wide = pl.BlockSpec((tb, 2048), lambda i: (nb - 1 - i, 0))
    j = _job_args(job, 12, 3)
    res = pl.pallas_call(
        _hosting(body, job, 12, 3, 1, nb), name=name, grid=(nb,),
        in_specs=[wide, rope, rope, cmat, cmat, cmat, cmat, vec, vec, blk,
                  pl.BlockSpec((nch, H, CHUNK, CHUNK), lambda i: (nb - 1 - i, 0, 0, 0)), blk] + j["in_specs"],
        out_specs=[wide, vec, vec] + j["out_specs"],
        out_shape=[jax.ShapeDtypeStruct((T, 2048), BF16), jax.ShapeDtypeStruct((1, BRANCH_W), F32),
                   jax.ShapeDtypeStruct((1, BRANCH_W), F32)] + j["out_shape"],
        scratch_shapes=[pltpu.VMEM((H, CHUNK, CHUNK), F32)] + j["scratch"], input_output_aliases=j["aliases"],
        compiler_params=_params(("arbitrary",)),
    )(proj, cosf, sinf, *consts, gn_g, gn_b, raw, states, dout, *j["ins"])
    return res[0], res[1], res[2], list(res[3:])


def _sb_masks():
    row = lax.broadcasted_iota(jnp.int32, (CHUNK, CHUNK), 0)
    lane = lax.broadcasted_iota(jnp.int32, (CHUNK, CHUNK), 1)
    return row, lane


SB_QT = 256
SB_DEAD = -105.0


def _pair(v):
    hi = v.astype(BF16)
    return jnp.concatenate([hi, (v - hi.astype(F32)).astype(BF16)], axis=1)


def _sb_consts():
    r = lax.broadcasted_iota(jnp.int32, (256, 256), 0) & 127
    c = lax.broadcasted_iota(jnp.int32, (256, 256), 1)
    ones = c >= 128
    lane = lax.broadcasted_iota(jnp.int32, (CHUNK, CHUNK), 1)
    return (ones | (r > c)).astype(BF16), (ones | (r >= c)).astype(BF16), (lane < 64, lane >= 64)


def _per_head(x, hms):
    return jnp.concatenate([jnp.where(hm, x, 0.0) for hm in hms], axis=0).astype(BF16)


def _sb_logits(qb, kb2, mask2):
    z = _dot(qb, kb2, NT)
    l1p = jnp.log(1.0 + jnp.exp(-jnp.abs(z)))
    lsp = jnp.minimum(z, 0.0) - l1p
    lsn = lsp - z
    if mask2 is not None:
        lsn = jnp.where(mask2, lsn, 0.0)
    return lsp, lsn


def _sb_tile_mask(qt):
    trow = lax.broadcasted_iota(jnp.int32, (qt, 256), 0)
    tlane = lax.broadcasted_iota(jnp.int32, (qt, 256), 1) & 127
    return lambda m: (tlane + m * CHUNK) < trow


def sb_fwd(proj, *, name, job=None):
    T = proj.shape[0]
    qt = min(SB_QT, T)
    nsub = qt // CHUNK
    cb = C_SB // 128

    def body(q_ref, k_ref, v_ref, o_ref):
        u_gt, _, hms = _sb_consts()
        tile_mask = _sb_tile_mask(qt)

        def qtile(i, _):
            rq = pl.ds(pl.multiple_of(i * qt, qt), qt)
            qb = (q_ref[rq, :] * SB_SCALE).astype(BF16)

            def group(js, masks, state):
                carry, acc = list(state[:2]), state[2]
                rows = [pl.ds(pl.multiple_of(j * CHUNK, CHUNK), CHUNK) for j in js]
                logits = [_sb_logits(qb, _per_head(k_ref[rk, :], hms), m) for rk, m in zip(rows, masks)]
                sums = [[_dot(_pair(lsn[:, h * 128:(h + 1) * 128]), u_gt, NN) for h in range(2)] for _, lsn in logits]
                weights = []
                for (lsp, _), r, m in zip(logits, sums, masks):
                    a_b = []
                    for h in range(2):
                        hc = slice(h * 128, (h + 1) * 128)
                        a = jnp.exp(lsp[:, hc] + r[h][:, :128] + carry[h])
                        if m is not None:
                            a = jnp.where(m[:, hc], a, 0.0)
                        carry[h] = carry[h] + r[h][:, 128:]
                        a_b.append(a.astype(BF16))
                    weights.append(jnp.concatenate(a_b, axis=1))
                for rk, a in zip(rows, weights):
                    acc = acc + _dot(a, _per_head(v_ref[rk, :], hms), NN)
                return carry[0], carry[1], acc

            zero = jnp.zeros((qt, 128), F32)
            diag = list(reversed(range(nsub)))
            state = group([i * nsub + m for m in diag], [tile_mask(m) for m in diag], (zero, zero, zero))

            def live(c):
                return jnp.logical_and(c[0] < i, jnp.maximum(jnp.max(c[1][0]), jnp.max(c[1][1])) > SB_DEAD)

            def blocks(c):
                jj, st = c
                return jj + 1, group([(i - jj) * nsub - 1 - u for u in range(nsub)], [None] * nsub, st)

            _, state = lax.while_loop(live, blocks, (jnp.int32(0), state))
            o_ref[rq, :] = state[2]
            return 0

        lax.fori_loop(0, T // qt, qtile, 0)

    def col(off):
        return pl.BlockSpec((T, 128), lambda hp: (0, off + hp))

    steps = BRANCH_W // 128
    j = _job_args(job, 3, 1)
    res = pl.pallas_call(
        _hosting(body, job, 3, 1, 0, steps), name=name, grid=(steps,),
        in_specs=[col(cb), col(cb + 4), col(cb + 8)] + j["in_specs"], out_specs=[col(0)] + j["out_specs"],
        out_shape=[jax.ShapeDtypeStruct((T, BRANCH_W), F32)] + j["out_shape"],
        scratch_shapes=j["scratch"], input_output_aliases=j["aliases"],
        compiler_params=_params(("parallel",) if job is None else ("arbitrary",)),
    )(proj, proj, proj, *j["ins"])
    return res[0], list(res[1:])


def sb_bwd(proj, out, dout, *, name, job=None):
    T = proj.shape[0]
    qt = min(SB_QT, T)
    nsub = qt // CHUNK
    cb = C_SB // 128

    def body(q_ref, k_ref, v_ref, o_ref, do_ref, dq_ref, dk_ref, dv_ref, dkt_ref, dvt_ref):
        u_gt, u_ge, hms = _sb_consts()
        tile_mask = _sb_tile_mask(qt)
        tall_lane = lax.broadcasted_iota(jnp.int32, (qt, 128), 1)
        top = lax.broadcasted_iota(jnp.int32, (CHUNK, CHUNK), 0) < 64
        dkt_ref[...] = jnp.zeros_like(dkt_ref)
        dvt_ref[...] = jnp.zeros_like(dvt_ref)

        def qtile(i, _):
            rq = pl.ds(pl.multiple_of(i * qt, qt), qt)
            qs = q_ref[rq, :] * SB_SCALE
            qb, q_t = qs.astype(BF16), qs.T.astype(BF16)
            dov = do_ref[rq, :]
            dob, do_t = dov.astype(BF16), dov.T.astype(BF16)
            prod = dob.astype(F32) * o_ref[rq, :]
            total = [jnp.broadcast_to(jnp.sum(jnp.where(hm, prod, 0.0), axis=1, keepdims=True), (qt, 128))
                     for hm in (tall_lane < 64, tall_lane >= 64)]

            def group(js, masks, state):
                c_l, c_w, dq = list(state[:2]), list(state[2:4]), state[4]
                heads = [slice(h * 128, (h + 1) * 128) for h in range(2)]
                rows = [pl.ds(pl.multiple_of(j * CHUNK, CHUNK), CHUNK) for j in js]
                kb2 = [_per_head(k_ref[rk, :], hms) for rk in rows]
                logits = [_sb_logits(qb, kb, m) for kb, m in zip(kb2, masks)]
                da = [_dot(dob, _per_head(v_ref[rk, :], hms), NT) for rk in rows]
                sums = [[_dot(_pair(lsn[:, hc]), u_gt, NN) for hc in heads] for _, lsn in logits]
                a_b, w_all = [], []
                for (lsp, _), r, d, m in zip(logits, sums, da, masks):
                    a_h, w_h = [], []
                    for h, hc in enumerate(heads):
                        a = jnp.exp(lsp[:, hc] + r[h][:, :128] + c_l[h])
                        if m is not None:
                            a = jnp.where(m[:, hc], a, 0.0)
                        c_l[h] = c_l[h] + r[h][:, 128:]
                        a = a.astype(BF16)
                        a_h.append(a)
                        w_h.append(a.astype(F32) * d[:, hc])
                    a_b.append(jnp.concatenate(a_h, axis=1))
                    w_all.append(w_h)
                sums_w = [[_dot(_pair(w), u_ge, NN) for w in w_h] for w_h in w_all]
                dz_b = []
                for (lsp, _), w_h, r, m in zip(logits, w_all, sums_w, masks):
                    sp = jnp.exp(lsp)
                    dz_h = []
                    for h, hc in enumerate(heads):
                        later_w = r[h][:, :128] + c_w[h]
                        c_w[h] = c_w[h] + r[h][:, 128:]
                        dz = w_h[h] * (1.0 - sp[:, hc]) - sp[:, hc] * (total[h] - later_w)
                        if m is not None:
                            dz = jnp.where(m[:, hc], dz, 0.0)
                        dz_h.append(dz.astype(BF16))
                    dz_b.append(jnp.concatenate(dz_h, axis=1))
                for j, kb, a, dz in zip(js, kb2, a_b, dz_b):
                    dkt = _dot(q_t, dz, NN)
                    dvt = _dot(do_t, a, NN)
                    dkt_ref[j] += jnp.where(top, dkt[:, :128], dkt[:, 128:])
                    dvt_ref[j] += jnp.where(top, dvt[:, :128], dvt[:, 128:])
                    dq = dq + _dot(dz, kb, NN)
                return c_l[0], c_l[1], c_w[0], c_w[1], dq

            zero = jnp.zeros((qt, 128), F32)
            diag = list(reversed(range(nsub)))
            state = group([i * nsub + m for m in diag], [tile_mask(m) for m in diag], (zero,) * 5)

            def live(c):
                return jnp.logical_and(c[0] < i, jnp.maximum(jnp.max(c[1][0]), jnp.max(c[1][1])) > SB_DEAD)

            def blocks(c):
                jj, st = c
                return jj + 1, group([(i - jj) * nsub - 1 - u for u in range(nsub)], [None] * nsub, st)

            _, state = lax.while_loop(live, blocks, (jnp.int32(0), state))
            dq_ref[rq, :] = (state[4] * SB_SCALE).astype(BF16)
            return 0

        lax.fori_loop(0, T // qt, qtile, 0)

        def untranspose(jb, _):
            rk = pl.ds(pl.multiple_of(jb * CHUNK, CHUNK), CHUNK)
            dk_ref[rk, :] = dkt_ref[jb].T.astype(BF16)
            dv_ref[rk, :] = dvt_ref[jb].T.astype(BF16)
            return 0

        lax.fori_loop(0, T // CHUNK, untranspose, 0)

    def col(off):
        return pl.BlockSpec((T, 128), lambda hp: (0, off + hp))

    o16 = jax.ShapeDtypeStruct((T, BRANCH_W), BF16)
    steps = BRANCH_W // 128
    j = _job_args(job, 5, 3)
    acc = pltpu.VMEM((T // CHUNK, CHUNK, CHUNK), F32)
    res = pl.pallas_call(
        _hosting(body, job, 5, 3, 2, steps), name=name, grid=(steps,),
        in_specs=[col(cb), col(cb + 4), col(cb + 8), col(0), col(0)] + j["in_specs"],
        out_specs=[col(0), col(0), col(0)] + j["out_specs"], out_shape=[o16, o16, o16] + j["out_shape"],
        scratch_shapes=[acc, acc] + j["scratch"], input_output_aliases=j["aliases"],
        compiler_params=_params(("parallel",) if job is None else ("arbitrary",)),
    )(proj, proj, proj, out, dout, *j["ins"])
    return res[0], res[1], res[2], list(res[3:])


_G0 = math.sqrt(2.0 / math.pi)
_G1 = 0.044715


def _gelu(x):
    return 0.5 * x * (1.0 + jnp.tanh(_G0 * (x + _G1 * x * x * x)))


def _gelu_grad(x):
    t = jnp.tanh(_G0 * (x + _G1 * x * x * x))
    return 0.5 * (1.0 + t) + 0.5 * x * (1.0 - t * t) * (_G0 * (1.0 + 3.0 * _G1 * x * x))


def _tril():
    row, lane = _sb_masks()
    return row >= lane


def sgu_fwd(proj, ln_g, ln_b, w, bias, *, name):
    T = proj.shape[0]
    tb = min(512, T)
    G = BRANCH_W // 128

    def body(u_ref, v_ref, g_ref, b_ref, w_ref, bias_ref, o_ref):
        vv = _gelu(v_ref[...])
        xh, _ = _group_norm(vv)
        vn = (xh * g_ref[...] + b_ref[...]).astype(BF16)
        tril = _tril()
        for g in range(G):
            wg = jnp.where(tril, w_ref[g], 0.0).astype(BF16)
            gc = slice(g * 128, (g + 1) * 128)
            for c in range(tb // CHUNK):
                r = slice(c * CHUNK, (c + 1) * CHUNK)
                sv = _dot(wg, vn[r, gc], NN) + bias_ref[g]
                o_ref[r, gc] = _gelu(u_ref[r, gc]) * sv

    cu, cv = C_SGU // BRANCH_W, C_SGU // BRANCH_W + 1
    vec = pl.BlockSpec((1, BRANCH_W), lambda i: (0, 0))
    mat = pl.BlockSpec((G, CHUNK, CHUNK), lambda i: (0, 0, 0))
    return pl.pallas_call(
        body, name=name, grid=(T // tb,),
        in_specs=[pl.BlockSpec((tb, BRANCH_W), lambda i: (i, cu)), pl.BlockSpec((tb, BRANCH_W), lambda i: (i, cv)),
                  vec, vec, mat, mat],
        out_specs=pl.BlockSpec((tb, BRANCH_W), lambda i: (i, 0)),
        out_shape=jax.ShapeDtypeStruct((T, BRANCH_W), F32),
        compiler_params=_params(("parallel",)),
    )(proj, proj, ln_g, ln_b, w, bias)


def sgu_bwd(proj, ln_g, ln_b, w, bias, dout, *, name):
    T = proj.shape[0]
    tb = min(512, T)
    G = BRANCH_W // 128

    def body(u_ref, v_ref, g_ref, b_ref, w_ref, bias_ref, do_ref, dp_ref, dw_ref, dbias_ref, dg_ref, db_ref, dvn_ref):
        @pl.when(pl.program_id(0) == 0)
        def _():
            dw_ref[...] = jnp.zeros_like(dw_ref)
            dbias_ref[...] = jnp.zeros_like(dbias_ref)
            dg_ref[...] = jnp.zeros_like(dg_ref)
            db_ref[...] = jnp.zeros_like(db_ref)

        gv = v_ref[...]
        vv = _gelu(gv)
        xh, rstd = _group_norm(vv)
        vn = (xh * g_ref[...] + b_ref[...]).astype(BF16)
        tril = _tril()
        for g in range(G):
            wg = jnp.where(tril, w_ref[g], 0.0).astype(BF16)
            gc = slice(g * 128, (g + 1) * 128)
            for c in range(tb // CHUNK):
                r = slice(c * CHUNK, (c + 1) * CHUNK)
                vn_c = vn[r, gc]
                sv = _dot(wg, vn_c, NN) + bias_ref[g]
                gu = u_ref[r, gc]
                d_o = do_ref[r, gc]
                dp_ref[r, gc] = (d_o * sv * _gelu_grad(gu)).astype(BF16)
                dsv = d_o * _gelu(gu)
                dsv_b = dsv.astype(BF16)
                dvn_ref[r, gc] = _dot(wg, dsv_b, TN)
                dw_ref[g] += jnp.where(tril, _dot(dsv_b, vn_c, NT), 0.0)
                dbias_ref[g] += jnp.broadcast_to(jnp.sum(dsv, axis=1, keepdims=True), (CHUNK, CHUNK))
        dvn = dvn_ref[...]
        dg_ref[...] += jnp.sum(dvn * xh, axis=0, keepdims=True)
        db_ref[...] += jnp.sum(dvn, axis=0, keepdims=True)
        dxh = dvn * g_ref[...]
        m1 = jnp.mean(dxh, axis=-1, keepdims=True)
        m2 = jnp.mean(dxh * xh, axis=-1, keepdims=True)
        dp_ref[:, BRANCH_W:2 * BRANCH_W] = (rstd * (dxh - m1 - xh * m2) * _gelu_grad(gv)).astype(BF16)

    cu, cv = C_SGU // BRANCH_W, C_SGU // BRANCH_W + 1
    vec = pl.BlockSpec((1, BRANCH_W), lambda i: (0, 0))
    mat = pl.BlockSpec((G, CHUNK, CHUNK), lambda i: (0, 0, 0))
    blk = pl.BlockSpec((tb, BRANCH_W), lambda i: (i, 0))
    msh = jax.ShapeDtypeStruct((G, CHUNK, CHUNK), F32)
    vsh = jax.ShapeDtypeStruct((1, BRANCH_W), F32)
    return pl.pallas_call(
        body, name=name, grid=(T // tb,),
        in_specs=[pl.BlockSpec((tb, BRANCH_W), lambda i: (i, cu)), pl.BlockSpec((tb, BRANCH_W), lambda i: (i, cv)),
                  vec, vec, mat, mat, blk],
        out_specs=[pl.BlockSpec((tb, 2 * BRANCH_W), lambda i: (i, 0)), mat, mat, vec, vec],
        out_shape=[jax.ShapeDtypeStruct((T, 2 * BRANCH_W), BF16), msh, msh, vsh, vsh],
        scratch_shapes=[pltpu.VMEM((tb, BRANCH_W), F32)],
        compiler_params=_params(("arbitrary",)),
    )(proj, proj, ln_g, ln_b, w, bias, dout)


def merge_fwd(a1, a2, a3, p1, p2, p3, proj, *, name):
    T = a1.shape[0]
    tm, tn = min(1024, T), 512
    gb = C_GATE // tn

    def body(a1_ref, a2_ref, a3_ref, p1_ref, p2_ref, p3_ref, g1_ref, g2_ref, g3_ref, m_ref, r1_ref, r2_ref, r3_ref):
        m = None
        for a_ref, p_ref, g_ref, r_ref in ((a1_ref, p1_ref, g1_ref, r1_ref), (a2_ref, p2_ref, g2_ref, r2_ref),
                                           (a3_ref, p3_ref, g3_ref, r3_ref)):
            r = _dot(a_ref[...].astype(BF16), p_ref[...], NN)
            r_ref[...] = r.astype(r_ref.dtype)
            t = jax.nn.sigmoid(g_ref[...]) * r
            m = t if m is None else m + t
        m_ref[...] = m.astype(m_ref.dtype)

    a_spec = pl.BlockSpec((tm, BRANCH_W), lambda i, j: (i, 0))
    p_spec = pl.BlockSpec((BRANCH_W, tn), lambda i, j: (0, j))
    o_spec = pl.BlockSpec((tm, tn), lambda i, j: (i, j))
    osh = jax.ShapeDtypeStruct((T, D_MODEL), F32)
    gates = [pl.BlockSpec((tm, tn), functools.partial(lambda i, j, o: (i, o + j), o=gb + 2 * n)) for n in range(3)]
    return pl.pallas_call(
        body, name=name, grid=(T // tm, D_MODEL // tn),
        in_specs=[a_spec, a_spec, a_spec, p_spec, p_spec, p_spec, *gates],
        out_specs=[o_spec] * 4, out_shape=[jax.ShapeDtypeStruct((T, D_MODEL), BF16)] * 4,
        compiler_params=_params(("parallel", "parallel")),
    )(a1, a2, a3, p1, p2, p3, proj, proj, proj)


def _merge_bwd_epi(dm, r1, r2, r3, g1, g2, g3):
    d_r, d_g = [], []
    for r, g in ((r1, g1), (r2, g2), (r3, g3)):
        s = jax.nn.sigmoid(g)
        d_r.append(dm * s)
        d_g.append(dm * r.astype(F32) * (s * (1.0 - s)))
    return (*d_r, *d_g)


def _rows_call(fn, ins, out_dtypes, *, name, tr=256, job=None):
    first = ins[0][0] if isinstance(ins[0], tuple) else ins[0]
    R, C = first.shape[-2:]
    tr = min(tr, R)
    assert R % tr == 0, (name, R, tr)
    arrs, specs = [], []
    for x in ins:
        if isinstance(x, tuple):
            arrs.append(x[0])
            specs.append(pl.BlockSpec((None, tr, C), functools.partial(lambda i, n: (n, i, 0), n=x[1])))
        else:
            arrs.append(x)
            specs.append(pl.BlockSpec((tr, C), lambda i: (i, 0)))
    ni = len(arrs)

    def body(*refs):
        vals = fn(*[r[...] for r in refs[:ni]])
        for o_ref, v in zip(refs[ni:], vals):
            o_ref[...] = v.astype(o_ref.dtype)

    no = len(out_dtypes)
    j = _job_args(job, ni, no)
    res = pl.pallas_call(
        _hosting(body, job, ni, no, 0, R // tr), name=name, grid=(R // tr,), in_specs=specs + j["in_specs"],
        out_specs=[pl.BlockSpec((tr, C), lambda i: (i, 0)) for _ in out_dtypes] + j["out_specs"],
        out_shape=[jax.ShapeDtypeStruct((R, C), dt) for dt in out_dtypes] + j["out_shape"],
        scratch_shapes=j["scratch"], input_output_aliases=j["aliases"],
        compiler_params=_params(("parallel",) if job is None else ("arbitrary",)),
    )(*arrs, *j["ins"])
    return list(res) if job is None else (list(res[:no]), list(res[no:]))


def _tile_rows(rows, cols):
    t = 256
    while t > 8 and (t * cols > 512 * 1024 or rows % t):
        t //= 2
    return t


def _rows_at(fn, pos, ins, outs, steps, *, name, aliases=None):
    read = [n for n, (_, s) in enumerate(ins) if s is not ANY]
    ni = len(ins)

    def body(pos_ref, *refs):
        vals = fn(*[refs[n][...] for n in read])
        for o_ref, v in zip(refs[ni:], vals):
            o_ref[...] = v.astype(o_ref.dtype)

    return pl.pallas_call(
        body, name=name,
        grid_spec=pltpu.PrefetchScalarGridSpec(num_scalar_prefetch=1, grid=(steps,), in_specs=[s for _, s in ins],
                                               out_specs=[s for _, s in outs]),
        out_shape=[sh for sh, _ in outs],
        input_output_aliases={1 + i: o for i, o in (aliases or {}).items()},
        compiler_params=_params(("parallel",)),
    )(pos, *[a for a, _ in ins])


def cast_into_whole(pos, w, l, axis, *, name):
    _, r, n = w.shape
    tr = _tile_rows(r, n)
    if axis == 1:
        shape, spec = (r, n * N_CHIPS), pl.BlockSpec((tr, n), lambda i, p: (i, p[3]))
    else:
        shape, spec = (r * N_CHIPS, n), pl.BlockSpec((tr, n), lambda i, p: (p[3] * (r // tr) + i, 0))
    return _rows_at(lambda a: (a,), pos, [(w, pl.BlockSpec((None, tr, n), lambda i, p: (l, i, 0)))],
                    [(jax.ShapeDtypeStruct(shape, BF16), spec)], r // tr, name=name)[0]


def pair_sum(pos, theirs, g32, axis, *, name):
    rows2, cols = theirs.shape
    h = rows2 // (N_CHIPS if axis == 0 else 1)
    tr = _tile_rows(h, cols)
    hb = h // tr
    if axis == 1:
        own = pl.BlockSpec((tr, cols), lambda i, p: (p[2] * hb + i, 0))
    else:
        own = pl.BlockSpec((tr, cols), lambda i, p: ((2 * (i // hb) + p[2]) * hb + i % hb, 0))
    row = pl.BlockSpec((tr, cols), lambda i, p: (i, 0))
    return _rows_at(lambda t, m: (m + t.astype(F32),) * 2, pos, [(theirs, row), (g32, own)],
                    [(jax.ShapeDtypeStruct((rows2, cols), F32), row), (jax.ShapeDtypeStruct((rows2, cols), BF16), row)],
                    rows2 // tr, name=name)


def chip_sum(pos, h32, recv, l, axis, whole, *, name):
    _, depth, h, n = recv.shape
    tr = _tile_rows(h, n)
    hb = h // tr
    if axis == 1:
        mine = pl.BlockSpec((tr, n), lambda i, p: (i, p[3]))
    else:
        mine = pl.BlockSpec((tr, n), lambda i, p: (p[3] * hb + i, 0))
    ins = [(h32, mine)] + [(recv, pl.BlockSpec((None, None, tr, n), functools.partial(lambda i, p, j: (j, l, i, 0), j=j)))
                           for j in range(3)]
    if whole is not None:
        ins.append((whole, ANY))
    return _rows_at(lambda o, a, b, c: (((o + a.astype(F32)) + b.astype(F32)) + c.astype(F32),), pos, ins,
                    [(jax.ShapeDtypeStruct((depth, 2, h, n), F32), pl.BlockSpec((None, None, tr, n), lambda i, p: (l, p[2], i, 0)))],
                    hb, name=name, aliases=None if whole is None else {4: 0})[0]


def _adamw(w, g, m, v):
    m2 = ADAM_B1 * m + (1.0 - ADAM_B1) * g
    v2 = ADAM_B2 * v + (1.0 - ADAM_B2) * (g * g)
    m_hat = m2 / (1.0 - ADAM_B1 ** ADAM_STEP)
    v_hat = v2 / (1.0 - ADAM_B2 ** ADAM_STEP)
    delta = -ADAM_LR * (m_hat / (jnp.sqrt(v_hat) + ADAM_EPS) + ADAM_WD * w)
    return delta, m2, v2


def _place():
    return lax.axis_index("x"), lax.axis_index("y"), lax.axis_index("c")


def _chip_peers(x, y, c):
    return [((1 - x, y, c), 2 * (1 - x) + y), ((x, 1 - y, c), 2 * x + 1 - y), ((1 - x, 1 - y, c), 2 * (1 - x) + 1 - y)]


def _shard_of(ref, axis, k, n):
    start = pl.multiple_of(k * n, 128)
    return ref.at[pl.ds(start, n), :] if axis == 0 else ref.at[:, pl.ds(start, n)]


ANY = pl.BlockSpec(memory_space=pl.ANY)


class CopyJob:
    def __init__(self, ins, out_shape, scratch, copies, aliases=None):
        self.ins, self.out_shape, self.scratch, self.copies = list(ins), list(out_shape), list(scratch), copies
        self.aliases = dict(aliases or {})

    def start(self, ins, outs, sems):
        local, remote, _, _ = self.copies(ins, outs, sems)
        for d in local + remote:
            d.start()

    def finish(self, ins, outs, sems):
        local, remote, arrivals, relays = self.copies(ins, outs, sems)
        for needs, sends, _ in relays:
            for d in needs:
                d.wait_recv()
            for d in sends:
                d.start()
        for d in arrivals + [d for _, _, arrives in relays for d in arrives]:
            d.wait_recv()
        for d in remote + [d for _, sends, _ in relays for d in sends]:
            d.wait_send()
        for d in local:
            d.wait()


def run_job(job, *, name):
    ni, no = len(job.ins), len(job.out_shape)

    def body(*refs):
        parts = refs[:ni], refs[ni:ni + no], refs[ni + no:]
        job.start(*parts)
        job.finish(*parts)

    return pl.pallas_call(
        body, name=name, in_specs=[ANY] * ni, out_specs=[ANY] * no, out_shape=job.out_shape,
        scratch_shapes=job.scratch, input_output_aliases=job.aliases,
    )(*job.ins)


def _job_args(job, n_in, n_out):
    if job is None:
        return dict(ins=[], in_specs=[], out_specs=[], out_shape=[], scratch=[], aliases={})
    return dict(ins=job.ins, in_specs=[ANY] * len(job.ins), out_specs=[ANY] * len(job.out_shape),
                out_shape=job.out_shape, scratch=job.scratch,
                aliases={n_in + i: n_out + o for i, o in job.aliases.items()})


def _hosting(body, job, n_in, n_out, n_scratch, grid):
    if job is None:
        return body
    ji, jo = len(job.ins), len(job.out_shape)
    grid = (grid,) if isinstance(grid, int) else tuple(grid)

    def at(ends):
        hit = None
        for ax, e in enumerate(ends):
            here = pl.program_id(ax) == e
            hit = here if hit is None else jnp.logical_and(hit, here)
        return hit

    def hosted(*refs):
        o = n_in + ji
        s = o + n_out + jo
        parts = refs[n_in:o], refs[o + n_out:s], refs[s + n_scratch:]

        @pl.when(at([0] * len(grid)))
        def _():
            job.start(*parts)

        body(*refs[:n_in], *refs[o:o + n_out], *refs[s:s + n_scratch])

        @pl.when(at([g - 1 for g in grid]))
        def _():
            job.finish(*parts)

    return hosted


def _job_sems(n_remote, n_local):
    return [pltpu.SemaphoreType.DMA((n_remote,)), pltpu.SemaphoreType.DMA((n_remote,)), pltpu.SemaphoreType.DMA((n_local,))]


def gather_job(shards, axes, chips=(0, 1, 2)):
    na = len(shards)

    def copies(ins, outs, sems):
        send, recv, _ = sems
        x, y, c = _place()
        k = 2 * x + y
        remote, relays = [], []
        for a in range(na):
            r = outs[a].shape[0] // (N_CHIPS if axes[a] == 0 else 1)
            n = outs[a].shape[axes[a]] // N_CHIPS
            half = r // 2

            def part(kk, cc, a=a, n=n, half=half):
                rows = pl.ds(pl.multiple_of(cc * half + (kk * n if axes[a] == 0 else 0), 8), half)
                return outs[a].at[rows, :] if axes[a] == 0 else outs[a].at[rows, pl.ds(pl.multiple_of(kk * n, 128), n)]

            needs, passes, lands = [], [], []
            for j, (peer, kp) in enumerate(_chip_peers(x, y, c)):
                if j not in chips:
                    continue
                s = 6 * a + j
                remote.append(pltpu.make_async_remote_copy(part(k, c), part(k, c), send.at[s], recv.at[s],
                                                           device_id=peer, device_id_type=MESH))
                needs.append(pltpu.make_async_remote_copy(part(kp, c), part(kp, c), send.at[s], recv.at[s],
                                                          device_id=peer, device_id_type=MESH))
                passes.append(pltpu.make_async_remote_copy(part(kp, c), part(kp, c), send.at[s + 3], recv.at[s + 3],
                                                           device_id=(x, y, 1 - c), device_id_type=MESH))
                lands.append(pltpu.make_async_remote_copy(part(kp, 1 - c), part(kp, 1 - c), send.at[s + 3], recv.at[s + 3],
                                                          device_id=(x, y, 1 - c), device_id_type=MESH))
            relays.append((needs, passes, lands))
        return [], remote, [], relays

    out_shape = [jax.ShapeDtypeStruct(w.shape, BF16) for w in shards]
    return CopyJob(shards, out_shape, _job_sems(6 * na, 1), copies, {a: a for a in range(na)})


def scatter_job(layers, g16, axes, filled, chips=(0, 1, 2)):
    na = len(axes)

    def shard_shape(a):
        r, c = g16[a].shape
        return (r // N_CHIPS, c) if axes[a] == 0 else (r, c // N_CHIPS)

    def copies(ins, outs, sems):
        send, recv_sems, _ = sems
        x, y, c = _place()
        remote = []
        for a in range(na):
            n = shard_shape(a)[axes[a]]
            for r, (peer, kp) in enumerate(_chip_peers(x, y, c)):
                if r not in chips:
                    continue
                remote.append(pltpu.make_async_remote_copy(_shard_of(ins[a], axes[a], kp, n), outs[a].at[r, layers[a]],
                                                           send.at[3 * a + r], recv_sems.at[3 * a + r],
                                                           device_id=peer, device_id_type=MESH))
        return [], remote, remote, []

    out_shape = [jax.ShapeDtypeStruct((3, DEPTH) + shard_shape(a), BF16) for a in range(na)]
    ins = list(g16)
    aliases = {}
    for a in range(na):
        if filled[a] is not None:
            aliases[len(ins)] = a
            ins.append(filled[a])
    return CopyJob(ins, out_shape, _job_sems(3 * na, 1), copies, aliases)


def pair_job(g16, axes):
    na = len(axes)
    pieces = [1 if ax == 1 else N_CHIPS for ax in axes]

    def copies(ins, outs, sems):
        send, recv, _ = sems
        x, y, c = _place()
        remote = []
        s = 0
        for a in range(na):
            rows = g16[a].shape[0] // (2 * pieces[a])
            for kk in range(pieces[a]):
                src = ins[a].at[pl.ds(pl.multiple_of((2 * kk + 1 - c) * rows, 8), rows), :]
                remote.append(pltpu.make_async_remote_copy(src, outs[a].at[pl.ds(kk * rows, rows), :], send.at[s], recv.at[s],
                                                           device_id=(x, y, 1 - c), device_id_type=MESH))
                s += 1
        return [], remote, remote, []

    out_shape = [jax.ShapeDtypeStruct((g.shape[0] // 2, g.shape[1]), BF16) for g in g16]
    return CopyJob(g16, out_shape, _job_sems(sum(pieces), 1), copies)


def join_job(shards):
    na = len(shards)

    def copies(ins, outs, sems):
        send, recv, _ = sems
        x, y, c = _place()
        remote = [pltpu.make_async_remote_copy(outs[a].at[:, c], outs[a].at[:, c], send.at[a], recv.at[a],
                                               device_id=(x, y, 1 - c), device_id_type=MESH) for a in range(na)]
        lands = [pltpu.make_async_remote_copy(outs[a].at[:, 1 - c], outs[a].at[:, 1 - c], send.at[a], recv.at[a],
                                              device_id=(x, y, 1 - c), device_id_type=MESH) for a in range(na)]
        return [], remote, lands, []

    out_shape = [jax.ShapeDtypeStruct(s.shape, F32) for s in shards]
    return CopyJob(shards, out_shape, _job_sems(na, 1), copies, {a: a for a in range(na)})


def small_job(p):
    def copies(ins, outs, sems):
        send, recv, loc = sems
        x, y, c = _place()
        me = 4 * x + 2 * y + c
        remote, lands = [], []
        for rel in range(1, 8):
            dx, dy, dc = rel >> 2, (rel >> 1) & 1, rel & 1
            peer = (1 - x if dx else x, 1 - y if dy else y, 1 - c if dc else c)
            who = 4 * peer[0] + 2 * peer[1] + peer[2]
            remote.append(pltpu.make_async_remote_copy(ins[0], outs[0].at[me], send.at[rel - 1], recv.at[rel - 1],
                                                       device_id=peer, device_id_type=MESH))
            lands.append(pltpu.make_async_remote_copy(ins[0], outs[0].at[who], send.at[rel - 1], recv.at[rel - 1],
                                                      device_id=peer, device_id_type=MESH))
        return [pltpu.make_async_copy(ins[0], outs[0].at[me], loc.at[0])], remote, lands, []

    return CopyJob([p], [jax.ShapeDtypeStruct((8,) + p.shape, F32)], _job_sems(7, 1), copies)


def small_sum(slots):
    def add(*terms):
        acc = terms[0]
        for t in terms[1:]:
            acc = acc + t
        return (acc,)

    return _rows_call(add, [(slots, d) for d in range(8)], [F32], name="small_sum", tr=8 * 47)[0]


BIG = ("w_in", "p_ret", "p_sb", "p_sgu", "w_out", "w_up", "w_down")
BIG_AXIS = {"w_in": 1, "p_ret": 1, "p_sb": 1, "p_sgu": 1, "w_out": 0, "w_up": 1, "w_down": 0}
SMALL = ("ret_gn_g", "ret_gn_b", "sgu_ln_g", "sgu_ln_b", "sgu_w", "sgu_b", "ln1_g", "ln1_b", "ln2_g", "ln2_b")


def layer_forward(l, x0, x0h, W, sm, rope, rconsts, hooks):
    n = f"l{l}_"
    job = hooks.fwd_job(l, "proj")
    proj = matmul(x0h, W["w_in"], mode="nn", tm=2048, tn=768, tk=1024, name=n + "proj", job=job)
    if job is not None:
        proj, job_out = proj
        hooks.done(job, job_out)
    retg, raw, states = ret_fwd(proj, *rope, rconsts, sm["ret_gn_g"], sm["ret_gn_b"], name=n + "ret_fwd")
    job = hooks.fwd_job(l, "sb")
    sb, job_out = sb_fwd(proj, name=n + "sb_fwd", job=job)
    if job is not None:
        hooks.done(job, job_out)
    sg = sgu_fwd(proj, sm["sgu_ln_g"], sm["sgu_ln_b"], sm["sgu_w"], sm["sgu_bias"], name=n + "sgu_fwd")
    merged, r1, r2, r3 = merge_fwd(retg, sb, sg, W["p_ret"], W["p_sb"], W["p_sgu"], proj, name=n + "merge_fwd")
    x1, xh1, rs1, x1h = matmul_ln(merged, W["w_out"], x0, sm["ln1_g"], sm["ln1_b"], tk=1024, name=n + "out_ln1")
    job = hooks.fwd_job(l, "up")
    h1 = matmul(x1h, W["w_up"], mode="nn", tm=1024, tn=1024, tk=1024, outs=((BF16, None),), name=n + "up", job=job)
    if job is not None:
        h1, job_out = h1
        hooks.done(job, job_out)
    job = hooks.fwd_job(l, "down")
    res = matmul_ln(h1, W["w_down"], x1, sm["ln2_g"], sm["ln2_b"], pro=_relu2, tk=1024, name=n + "down_ln2", job=job)
    if job is not None:
        res, job_out = res
        hooks.done(job, job_out)
    x2, xh2, rs2, x2h = res
    saved = dict(x0h=x0h, proj=proj, retg=retg, raw=raw, states=states, sb=sb, sg=sg, merged=merged, r=(r1, r2, r3),
                 x1h=x1h, xh1=xh1, rs1=rs1, h1=h1, xh2=xh2, rs2=rs2)
    return x2, x2h, saved


def layer_backward(l, dx2, s, W, sm, rope, rconsts, hooks):
    n = f"l{l}_"
    two = ((F32, None), (BF16, None))
    gw, gs = {}, {}
    job = hooks.bwd_job(l, "ln2")
    res = ln_bwd(dx2, s["xh2"], s["rs2"], sm["ln2_g"], name=n + "ln2_bwd", job=job)
    if job is not None:
        res, job_out = res
        hooks.done(job, job_out)
    du2, du2h, gs["ln2_g"], gs["ln2_b"] = res
    job = hooks.bwd_job(l, "g_down")
    gw["w_down"] = matmul(s["h1"], du2h, mode="tn", tm=1024, tn=1024, tk=2048, pro=_relu2, outs=two, name=n + "g_down", job=job)
    if job is not None:
        gw["w_down"], job_out = gw["w_down"]
        hooks.done(job, job_out)
    dh1 = matmul(du2h, W["w_down"], mode="nt", tm=1024, tn=1024, tk=1024, outs=((BF16, None),),
                 epi=lambda acc, h: (acc * (2.0 * jnp.maximum(h.astype(F32), 0.0)),), tiles=(s["h1"],), name=n + "d_h1")
    job = hooks.bwd_job(l, "g_up")
    gw["w_up"] = matmul(s["x1h"], dh1, mode="tn", tm=1024, tn=1024, tk=2048, outs=two, name=n + "g_up", job=job)
    if job is not None:
        gw["w_up"], job_out = gw["w_up"]
        hooks.done(job, job_out)
    dx1 = matmul(dh1, W["w_up"], mode="nt", tm=1024, tn=1024, tk=2048,
                 epi=lambda acc, d: (acc + ALPHA * d,), tiles=(du2,), name=n + "d_x1")
    du1, du1h, gs["ln1_g"], gs["ln1_b"] = ln_bwd(dx1, s["xh1"], s["rs1"], sm["ln1_g"], name=n + "ln1_bwd")
    gw["w_out"] = matmul(s["merged"], du1h, mode="tn", tm=1024, tn=1024, tk=2048, outs=two, name=n + "g_out")
    gate0 = C_GATE // 512
    dr1, dr2, dr3, dg1, dg2, dg3 = matmul(
        du1h, W["w_out"], mode="nt", tm=1024, tn=512, tk=1024, outs=((BF16, None),) * 6, epi=_merge_bwd_epi,
        tiles=(*s["r"], (s["proj"], gate0), (s["proj"], gate0 + 2), (s["proj"], gate0 + 4)), name=n + "d_merged")
    d_branch = {}
    for nm, a, dr in (("p_ret", s["retg"], dr1), ("p_sb", s["sb"], dr2), ("p_sgu", s["sg"], dr3)):
        gw[nm] = matmul(a, dr, mode="tn", tm=512, tn=1024, tk=2048, outs=two, name=n + "g_" + nm)
        d_branch[nm] = matmul(dr, W[nm], mode="nt", tm=1024, tn=512, tk=1024, name=n + "d_" + nm)
    job = hooks.pair(l, gw)
    dret, gs["ret_gn_g"], gs["ret_gn_b"], job_out = ret_bwd(s["proj"], *rope, rconsts, sm["ret_gn_g"], sm["ret_gn_b"],
                                                             s["raw"], s["states"], d_branch["p_ret"], name=n + "ret_bwd", job=job)
    if job is not None:
        hooks.done(job, job_out)
    job = hooks.scatter(l) if job is not None else None
    dsq, dsk, dsv, job_out = sb_bwd(s["proj"], s["sb"], d_branch["p_sb"], name=n + "sb_bwd", job=job)
    if job is not None:
        hooks.done(job, job_out)
    dsgu, gs["sgu_w"], dbias, gs["sgu_ln_g"], gs["sgu_ln_b"] = sgu_bwd(
        s["proj"], sm["sgu_ln_g"], sm["sgu_ln_b"], sm["sgu_w"], sm["sgu_bias"], d_branch["p_sgu"], name=n + "sgu_bwd")
    gs["sgu_b"] = dbias[:, :, 0]
    dproj = jnp.concatenate([dret, dsq, dsk, dsv, dsgu, dg1, dg2, dg3], axis=1)
    job = hooks.small(l, gs)
    gw["w_in"] = matmul(s["x0h"], dproj, mode="tn", tm=1024, tn=1536, tk=1024, outs=two, name=n + "g_in", job=job)
    if job is not None:
        gw["w_in"], job_out = gw["w_in"]
        hooks.done(job, job_out)
    job = hooks.tail(l, gw["w_in"])
    dx0 = matmul(dproj, W["w_in"], mode="nt", tm=1024, tn=1024, tk=2560,
                 epi=lambda acc, d: (acc + ALPHA * d,), tiles=(du1,), name=n + "d_x0", job=job)
    if job is not None:
        dx0, job_out = dx0
        hooks.done(job, job_out)
    return dx0, gw, gs


def local_step(x, target, small, plan):
    T = x.shape[0]
    rope = _rope_tables(T)
    rconsts = _ret_consts()
    sms = []
    for l in range(DEPTH):
        sm = {k: small[k][l][None, :] for k in SMALL if k not in ("sgu_w", "sgu_b")}
        sm["sgu_w"] = small["sgu_w"][l]
        sm["sgu_bias"] = jnp.broadcast_to(small["sgu_b"][l][:, :, None], (4, CHUNK, CHUNK))
        sms.append(sm)
    h, saved = x, []
    job = plan.first_job()
    hh = _rows_call(lambda a: (a,), [x], [BF16], name="cast_x", job=job)
    if job is not None:
        hh, job_out = hh
        plan.done(job, job_out)
    hh = hh[0]
    for l in range(DEPTH):
        h, hh, s = layer_forward(l, h, hh, plan.weights(l), sms[l], rope, rconsts, plan)
        saved.append(s)
    dy, sq = loss_head(h, target)
    gs = {k: [None] * DEPTH for k in SMALL}
    for l in reversed(range(DEPTH)):
        dy, gwl, gsl = layer_backward(l, dy, saved[l], plan.weights(l), sms[l], rope, rconsts, plan)
        plan.grads(l, gwl)
        for k in SMALL:
            gs[k][l] = gsl[k].reshape(small[k].shape[1:])
    return sq[0, 0], dy, {k: jnp.stack(v) for k, v in gs.items()}


EARLY_GRADS = ("p_ret", "p_sb", "p_sgu", "w_out", "w_up", "w_down")


class _StepPlan:
    def __init__(self, pos, shards16):
        self.pos = pos
        self.shards16 = shards16
        self.full = [dict() for _ in range(DEPTH)]
        self.gw = [None] * DEPTH
        self.bufs = {}
        self.sums = {}
        self.gs = [None] * DEPTH

    def first_job(self):
        return self._gather([(0, "w_in")])

    def weights(self, l):
        return self.full[l]

    def grads(self, l, gw):
        self.gw[l] = gw

    def _gather(self, items, chips=(0, 1, 2)):
        job = gather_job([self.shards16[l][k] for l, k in items], [BIG_AXIS[k] for _, k in items], chips)
        job.note = ("gather" if 2 in chips else "gather_part", items)
        return job

    def _pair(self, items):
        job = pair_job([g[1] for _, _, g in items], [BIG_AXIS[k] for _, k, _ in items])
        job.note = ("pair", items)
        return job

    def fwd_job(self, l, host):
        if host == "proj":
            return None
        if host == "sb":
            return self._gather([(l, k) for k in BIG[1:]])
        if l + 1 == DEPTH:
            return None
        return self._gather([(l + 1, "w_in")], (0, 1) if host == "up" else (2,))

    def bwd_job(self, l, host):
        if l + 1 == DEPTH:
            return None
        if host == "ln2":
            job = self._pair([(l + 1, "w_in", self.gw[l + 1]["w_in"])])
            job.note = ("pair_w_in", job.note[1])
            return job
        items, sums16 = self.summed_w_in
        job = scatter_job([l_ for l_, _, _ in items], sums16, [BIG_AXIS[k] for _, k, _ in items],
                          [self.bufs.get(k) for _, k, _ in items], (0, 1) if host == "g_down" else (2,))
        job.note = ("scatter", items)
        return job

    def pair(self, l, ready):
        return self._pair([(l, k, ready[k]) for k in EARLY_GRADS])

    def scatter(self, l):
        items, sums16 = self.summed
        job = scatter_job([l_ for l_, _, _ in items], sums16, [BIG_AXIS[k] for _, k, _ in items],
                          [self.bufs.get(k) for _, k, _ in items])
        job.note = ("scatter", items)
        return job

    def small(self, l, gs):
        self.gs[l] = {k: gs[k].reshape(-1) for k in SMALL}
        if l != 0:
            return None
        job = small_job(_pack_small({k: jnp.stack([self.gs[l_][k] for l_ in range(DEPTH)]) for k in SMALL}))
        job.note = ("small", [])
        return job

    def tail(self, l, g):
        if l != 0:
            return None
        last = self._pair([(0, "w_in", g)])
        self.done(last, run_job(last, name="pair_last"))
        return self.scatter(0)

    def done(self, job, outs):
        kind, items = job.note
        if kind == "small":
            self.small_slots = outs[0]
        if kind in ("pair", "pair_w_in"):
            sums16 = []
            for a, (l, k, g) in enumerate(items):
                self.sums[(l, k)], s16 = pair_sum(self.pos, outs[a], g[0], BIG_AXIS[k], name=f"pair_sum_{k}_{l}")
                sums16.append(s16)
            if kind == "pair":
                self.summed = (items, sums16)
            else:
                self.summed_w_in = (items, sums16)
        for a, item in enumerate(items):
            if kind == "gather_part":
                self.shards16[item[0]][item[1]] = outs[a]
            elif kind == "gather":
                self.full[item[0]][item[1]] = outs[a]
            elif kind == "scatter":
                self.bufs[item[1]] = outs[a]

    def finish(self):
        return self.bufs, self.sums


def _flat2(a):
    return a.reshape(-1, a.shape[-1])


def _pack_small(d, pre=""):
    return jnp.concatenate([d[pre + k].reshape(-1) for k in SMALL]).reshape(-1, 128)


def kernel(x, w_in, ret_gn_g, ret_gn_b, sgu_ln_g, sgu_ln_b, sgu_w, sgu_b, p_ret, p_sb, p_sgu, w_out, ln1_g, ln1_b, w_up, w_down, ln2_g, ln2_b, loss_target, m_w_in, m_ret_gn_g, m_ret_gn_b, m_sgu_ln_g, m_sgu_ln_b, m_sgu_w, m_sgu_b, m_p_ret, m_p_sb, m_p_sgu, m_w_out, m_ln1_g, m_ln1_b, m_w_up, m_w_down, m_ln2_g, m_ln2_b, v_w_in, v_ret_gn_g, v_ret_gn_b, v_sgu_ln_g, v_sgu_ln_b, v_sgu_w, v_sgu_b, v_p_ret, v_p_sb, v_p_sgu, v_w_out, v_ln1_g, v_ln1_b, v_w_up, v_w_down, v_ln2_g, v_ln2_b):
    given = dict(locals())
    order = BIG[:1] + SMALL[:6] + BIG[1:5] + SMALL[6:8] + BIG[5:7] + SMALL[8:10]
    L = DEPTH

    px, py, pc = _place()
    pos = jnp.stack([px, py, pc, 2 * px + py]).astype(jnp.int32)

    shards16 = [{k: cast_into_whole(pos, given[k], l, BIG_AXIS[k], name=f"cast_{k}_{l}") for k in BIG} for l in range(L)]
    plan = _StepPlan(pos, shards16)
    sq, dx, gs = local_step(x[0], loss_target[0], {k: given[k] for k in SMALL}, plan)
    loss = 0.5 * lax.psum(sq, ("x", "y", "c"))

    bufs, sums = plan.finish()
    shards = []
    for k in BIG:
        whole = None
        for l in range(L):
            whole = chip_sum(pos, sums[(l, k)], bufs[k], l, BIG_AXIS[k], whole, name=f"chip_sum_{k}_{l}")
        shards.append(whole)
    joined = run_job(join_job(shards), name="join_halves")
    out = {}
    for a, k in enumerate(BIG):
        shp = given[k].shape
        res = _rows_call(lambda g_, w_, m_, v_: (g_,) + _adamw(w_, g_, m_, v_),
                         [joined[a].reshape(-1, shp[-1]), _flat2(given[k]), _flat2(given["m_" + k]), _flat2(given["v_" + k])],
                         [F32] * 4, name="adamw_" + k)
        out[k] = [r.reshape(shp) for r in res]

    pack = _pack_small
    res = _rows_call(lambda g_, w_, m_, v_: (g_,) + _adamw(w_, g_, m_, v_),
                     [small_sum(plan.small_slots), pack(given), pack(given, "m_"), pack(given, "v_")], [F32] * 4,
                     name="adamw_small", tr=8 * 47)
    off = 0
    for k in SMALL:
        sz = given[k].size
        out[k] = [r.reshape(-1)[off:off + sz].reshape(given[k].shape) for r in res]
        off += sz

    grads = [out[k][0] for k in order]
    deltas = [out[k][1] for k in order]
    new_m = [out[k][2] for k in order]
    new_v = [out[k][3] for k in order]
    return (loss, dx[None], *grads, *deltas, *new_m, *new_v)
```

```python
import functools
import math

import jax
import jax.numpy as jnp
from jax import lax
from jax.experimental import pallas as pl
from jax.experimental.pallas import tpu as pltpu

F32 = jnp.float32
BF16 = jnp.bfloat16

D_MODEL = 1024
SEQ = 4096
DEPTH = 2
CHUNK = 128
RET_HEADS = 4
BRANCH_W = 512
N_IN = 7680
D_FF = 4096
LN_EPS = 1e-5
ROPE_BASE = 10000.0
ALPHA = (2 * DEPTH) ** 0.25
RET_SCALE = 128 ** -0.5
SB_SCALE = 64 ** -0.5
C_RET, C_SB, C_SGU, C_GATE = 0, 2048, 3584, 4608

ADAM_LR, ADAM_B1, ADAM_B2, ADAM_EPS, ADAM_WD, ADAM_STEP = 0.001, 0.9, 0.999, 1e-08, 0.01, 10

N_CHIPS = 4
VMEM_LIMIT = 56 * 1024 * 1024
MESH = pl.DeviceIdType.MESH

NN = ((1,), (0,))
NT = ((1,), (1,))
TN = ((0,), (0,))


def _dot(a, b, dims):
    return lax.dot_general(a, b, (dims, ((), ())), preferred_element_type=F32)


def _params(sem):
    return pltpu.CompilerParams(dimension_semantics=sem, vmem_limit_bytes=VMEM_LIMIT)


def _relu2(h):
    r = jnp.maximum(h.astype(F32), 0.0)
    return r * r


def matmul(a, b, *, mode, tm, tn, tk, outs=((F32, None),), pro=None, epi=None, tiles=(), rows=(), name, job=None):
    if mode == "nn":
        (M, K), N = a.shape, b.shape[1]
    elif mode == "nt":
        (M, K), N = a.shape, b.shape[0]
    else:
        (K, M), N = a.shape, b.shape[1]
    tm, tn, tk = min(tm, M), min(tn, N), min(tk, K)
    assert M % tm == 0 and N % tn == 0 and K % tk == 0, (name, M, N, K, tm, tn, tk)
    if mode == "nn":
        a_spec = pl.BlockSpec((tm, tk), lambda i, j, k: (i, k))
        b_spec = pl.BlockSpec((tk, tn), lambda i, j, k: (k, j))
        dims = NN
    elif mode == "nt":
        a_spec = pl.BlockSpec((tm, tk), lambda i, j, k: (i, k))
        b_spec = pl.BlockSpec((tn, tk), lambda i, j, k: (j, k))
        dims = NT
    else:
        a_spec = pl.BlockSpec((tk, tm), lambda i, j, k: (k, i))
        b_spec = pl.BlockSpec((tk, tn), lambda i, j, k: (k, j))
        dims = TN
    nk = K // tk
    nt_, nr, no = len(tiles), len(rows), len(outs)

    def body(a_ref, b_ref, *rest):
        tile_refs = rest[:nt_]
        row_refs = rest[nt_:nt_ + nr]
        out_refs = rest[nt_ + nr:nt_ + nr + no]
        av = a_ref[...]
        if pro is not None:
            av = pro(av)
        p = _dot(av.astype(BF16), b_ref[...].astype(BF16), dims)

        def finish(acc):
            vals = (acc,) * no if epi is None else epi(acc, *[r[...] for r in tile_refs], *[r[...] for r in row_refs])
            for o_ref, v in zip(out_refs, vals):
                o_ref[...] = v.astype(o_ref.dtype)

        if nk == 1:
            finish(p)
        else:
            acc_ref = rest[-1]
            k = pl.program_id(2)

            @pl.when(k == 0)
            def _():
                acc_ref[...] = p

            @pl.when(k > 0)
            def _():
                acc_ref[...] += p

            @pl.when(k == nk - 1)
            def _():
                finish(acc_ref[...])

    out_shape, out_specs = [], []
    for dt, width in outs:
        if width is None:
            out_shape.append(jax.ShapeDtypeStruct((M, N), dt))
            out_specs.append(pl.BlockSpec((tm, tn), lambda i, j, k: (i, j)))
        else:
            assert N == tn
            out_shape.append(jax.ShapeDtypeStruct((M, width), dt))
            out_specs.append(pl.BlockSpec((tm, width), lambda i, j, k: (i, 0)))
    in_specs = [a_spec, b_spec]
    offs = [t[1] if isinstance(t, tuple) else 0 for t in tiles]
    tiles = [t[0] if isinstance(t, tuple) else t for t in tiles]
    in_specs += [pl.BlockSpec((tm, tn), functools.partial(lambda i, j, k, o: (i, j + o), o=o)) for o in offs]
    in_specs += [pl.BlockSpec((1, tn), lambda i, j, k: (0, j)) for _ in rows]
    grid = (M // tm, N // tn, nk)
    scratch = [pltpu.VMEM((tm, tn), F32)] if nk > 1 else []
    j = _job_args(job, len(in_specs), no)
    res = pl.pallas_call(
        _hosting(body, job, len(in_specs), no, len(scratch), grid), name=name, grid=grid,
        in_specs=in_specs + j["in_specs"], out_specs=out_specs + j["out_specs"], out_shape=out_shape + j["out_shape"],
        scratch_shapes=scratch + j["scratch"], input_output_aliases=j["aliases"],
        compiler_params=_params(("parallel", "parallel", "arbitrary") if job is None else ("arbitrary",) * 3),
    )(a, b, *tiles, *rows, *j["ins"])
    mine = res[0] if no == 1 else list(res[:no])
    return mine if job is None else (mine, list(res[no:]))


def _ln_epi(acc, res, g, b):
    u = ALPHA * res + acc
    mu = jnp.mean(u, axis=-1, keepdims=True)
    xc = u - mu
    var = jnp.mean(xc * xc, axis=-1, keepdims=True)
    rstd = lax.rsqrt(var + LN_EPS)
    xhat = xc * rstd
    y = xhat * g + b
    return y, xhat, jnp.broadcast_to(rstd, (u.shape[0], 128)), y


def matmul_ln(a, w, res, g, b, *, pro=None, tk, name, job=None):
    n = w.shape[1]
    return matmul(a, w, mode="nn", tm=1024, tn=n, tk=tk, pro=pro, epi=_ln_epi, tiles=(res,), rows=(g, b),
                  outs=((F32, None), (F32, None), (F32, 128), (BF16, None)), name=name, job=job)


def ln_bwd(dy, xhat, rstd, g, *, name, job=None):
    T, D = dy.shape
    tm = min(512, T)

    def body(dy_ref, xh_ref, rs_ref, g_ref, du_ref, du16_ref, dg_ref, db_ref):
        dyv, xh = dy_ref[...], xh_ref[...]
        r = rs_ref[:, 0:1]
        dxh = dyv * g_ref[...]
        m1 = jnp.mean(dxh, axis=-1, keepdims=True)
        m2 = jnp.mean(dxh * xh, axis=-1, keepdims=True)
        du = r * (dxh - m1 - xh * m2)
        du_ref[...] = du
        du16_ref[...] = du.astype(BF16)

        @pl.when(pl.program_id(0) == 0)
        def _():
            dg_ref[...] = jnp.zeros_like(dg_ref)
            db_ref[...] = jnp.zeros_like(db_ref)

        dg_ref[...] += jnp.sum(dyv * xh, axis=0, keepdims=True)
        db_ref[...] += jnp.sum(dyv, axis=0, keepdims=True)

    row = pl.BlockSpec((tm, D), lambda i: (i, 0))
    vec = pl.BlockSpec((1, D), lambda i: (0, 0))
    j = _job_args(job, 4, 4)
    res = pl.pallas_call(
        _hosting(body, job, 4, 4, 0, T // tm), name=name, grid=(T // tm,),
        in_specs=[row, row, pl.BlockSpec((tm, 128), lambda i: (i, 0)), vec] + j["in_specs"],
        out_specs=[row, row, vec, vec] + j["out_specs"],
        out_shape=[jax.ShapeDtypeStruct((T, D), F32), jax.ShapeDtypeStruct((T, D), BF16),
                   jax.ShapeDtypeStruct((1, D), F32), jax.ShapeDtypeStruct((1, D), F32)] + j["out_shape"],
        scratch_shapes=j["scratch"], input_output_aliases=j["aliases"],
        compiler_params=_params(("arbitrary",)),
    )(dy, xhat, rstd, g, *j["ins"])
    return list(res[:4]) if job is None else (list(res[:4]), list(res[4:]))


def loss_head(y, target):
    T, D = y.shape
    tm = min(512, T)

    def body(y_ref, t_ref, dy_ref, s_ref):
        e = y_ref[...] - t_ref[...]
        dy_ref[...] = e * (1.0 / D)

        @pl.when(pl.program_id(0) == 0)
        def _():
            s_ref[...] = jnp.zeros_like(s_ref)

        s_ref[...] += jnp.sum(jnp.mean(e * e, axis=-1, keepdims=True))

    row = pl.BlockSpec((tm, D), lambda i: (i, 0))
    return pl.pallas_call(
        body, name="loss_head", grid=(T // tm,),
        in_specs=[row, row], out_specs=[row, pl.BlockSpec((8, 128), lambda i: (0, 0))],
        out_shape=[jax.ShapeDtypeStruct((T, D), F32), jax.ShapeDtypeStruct((8, 128), F32)],
        compiler_params=_params(("arbitrary",)),
    )(y, target)


def _rope_tables(T):
    half = 64
    inv_freq = ROPE_BASE ** (-jnp.arange(half, dtype=F32) / half)
    ang = jnp.arange(T, dtype=jnp.int32).astype(F32)[:, None] * inv_freq[None, :]
    cos, sin = jnp.cos(ang), jnp.sin(ang)
    return jnp.concatenate([cos, cos], axis=1), jnp.concatenate([-sin, sin], axis=1)


def _ret_consts():
    H = RET_HEADS
    log_g = jnp.log(1.0 - 2.0 ** (-5.0 - jnp.arange(H, dtype=F32)))
    idx = jnp.arange(CHUNK, dtype=F32)
    diff = idx[:, None] - idx[None, :]
    dmat = jnp.where(diff[None] >= 0, jnp.exp(log_g[:, None, None] * diff[None]), 0.0)
    kd = jnp.exp(log_g[:, None] * (CHUNK - 1 - idx)[None, :])
    qd = jnp.exp(log_g[:, None] * (idx + 1.0)[None, :])
    cd = jnp.exp(log_g * CHUNK)
    full = (H, CHUNK, CHUNK)
    return (dmat.astype(F32), jnp.broadcast_to(kd[:, :, None], full), jnp.broadcast_to(qd[:, :, None], full),
            jnp.broadcast_to(cd[:, None, None], full))


def _swap_halves(v):
    return pltpu.roll(v, 64, 1)


def _group_norm(o):
    mu = jnp.mean(o, axis=-1, keepdims=True)
    xc = o - mu
    var = jnp.mean(xc * xc, axis=-1, keepdims=True)
    rstd = lax.rsqrt(var + LN_EPS)
    return xc * rstd, rstd


def ret_fwd(proj, cosf, sinf, consts, gn_g, gn_b, *, name):
    T = proj.shape[0]
    tb = min(512, T)
    nch = tb // CHUNK
    H = RET_HEADS

    def body(p_ref, cos_ref, sin_ref, dm_ref, kd_ref, qd_ref, cd_ref, g_ref, b_ref, out_ref, raw_ref, st_ref, s_ref):
        @pl.when(pl.program_id(0) == 0)
        def _():
            s_ref[...] = jnp.zeros_like(s_ref)

        for c in range(nch):
            r = slice(c * CHUNK, (c + 1) * CHUNK)
            cs, sn = cos_ref[r, :], sin_ref[r, :]
            for h in range(H):
                hc = slice(h * 128, (h + 1) * 128)
                q = p_ref[r, h * 128:(h + 1) * 128]
                k = p_ref[r, 512 + h * 128:512 + (h + 1) * 128]
                v = p_ref[r, 1024 + h * 128:1024 + (h + 1) * 128]
                gt = p_ref[r, 1536 + h * 128:1536 + (h + 1) * 128]
                qr = q * cs + _swap_halves(q) * sn
                kr = (k * cs + _swap_halves(k) * sn) * RET_SCALE
                sprev = s_ref[h]
                st_ref[c, h] = sprev
                qb, kb, vb = qr.astype(BF16), kr.astype(BF16), v.astype(BF16)
                s = _dot(qb, kb, NT) * dm_ref[h]
                o = _dot(s.astype(BF16), vb, NN) + _dot((qr * qd_ref[h]).astype(BF16), sprev.astype(BF16), NN)
                s_ref[h] = sprev * cd_ref[h] + _dot((kr * kd_ref[h]).astype(BF16), vb, TN)
                raw_ref[r, hc] = o
                y, _ = _group_norm(o)
                out_ref[r, hc] = (gt * jax.nn.sigmoid(gt)) * (y * g_ref[:, hc] + b_ref[:, hc])

    cmat = pl.BlockSpec((H, CHUNK, CHUNK), lambda i: (0, 0, 0))
    vec = pl.BlockSpec((1, BRANCH_W), lambda i: (0, 0))
    rope = pl.BlockSpec((tb, 128), lambda i: (i, 0))
    blk = pl.BlockSpec((tb, BRANCH_W), lambda i: (i, 0))
    return pl.pallas_call(
        body, name=name, grid=(T // tb,),
        in_specs=[pl.BlockSpec((tb, 2048), lambda i: (i, 0)), rope, rope, cmat, cmat, cmat, cmat, vec, vec],
        out_specs=[blk, blk, pl.BlockSpec((nch, H, CHUNK, CHUNK), lambda i: (i, 0, 0, 0))],
        out_shape=[jax.ShapeDtypeStruct((T, BRANCH_W), F32), jax.ShapeDtypeStruct((T, BRANCH_W), F32),
                   jax.ShapeDtypeStruct((T // CHUNK, H, CHUNK, CHUNK), F32)],
        scratch_shapes=[pltpu.VMEM((H, CHUNK, CHUNK), F32)],
        compiler_params=_params(("arbitrary",)),
    )(proj, cosf, sinf, *consts, gn_g, gn_b)


def ret_bwd(proj, cosf, sinf, consts, gn_g, gn_b, raw, states, dout, *, name, job=None):
    T = proj.shape[0]
    tb = min(512, T)
    nch = tb // CHUNK
    nb = T // tb
    H = RET_HEADS

    def body(p_ref, cos_ref, sin_ref, dm_ref, kd_ref, qd_ref, cd_ref, g_ref, b_ref, raw_ref, st_ref, do_ref,
             dp_ref, dg_ref, db_ref, ds_ref):
        @pl.when(pl.program_id(0) == 0)
        def _():
            ds_ref[...] = jnp.zeros_like(ds_ref)
            dg_ref[...] = jnp.zeros_like(dg_ref)
            db_ref[...] = jnp.zeros_like(db_ref)

        for c in reversed(range(nch)):
            r = slice(c * CHUNK, (c + 1) * CHUNK)
            cs, sn = cos_ref[r, :], sin_ref[r, :]
            for h in range(H):
                hc = slice(h * 128, (h + 1) * 128)
                q = p_ref[r, h * 128:(h + 1) * 128]
                k = p_ref[r, 512 + h * 128:512 + (h + 1) * 128]
                v = p_ref[r, 1024 + h * 128:1024 + (h + 1) * 128]
                gt = p_ref[r, 1536 + h * 128:1536 + (h + 1) * 128]
                qr = q * cs + _swap_halves(q) * sn
                kr = (k * cs + _swap_halves(k) * sn) * RET_SCALE
                sprev = st_ref[c, h]
                gv = g_ref[:, hc]
                y, rstd = _group_norm(raw_ref[r, hc])
                d_out = do_ref[r, hc]
                sg = jax.nn.sigmoid(gt)
                d_gate = d_out * (y * gv + b_ref[:, hc]) * (sg * (1.0 + gt * (1.0 - sg)))
                d_aff = d_out * (gt * sg)
                dg_ref[:, hc] += jnp.sum(d_aff * y, axis=0, keepdims=True)
                db_ref[:, hc] += jnp.sum(d_aff, axis=0, keepdims=True)
                dxh = d_aff * gv
                m1 = jnp.mean(dxh, axis=-1, keepdims=True)
                m2 = jnp.mean(dxh * y, axis=-1, keepdims=True)
                d_o = (rstd * (dxh - m1 - y * m2)).astype(BF16)
                qb, kb, vb = qr.astype(BF16), kr.astype(BF16), v.astype(BF16)
                dm, kd, qd = dm_ref[h], kd_ref[h], qd_ref[h]
                p = (_dot(qb, kb, NT) * dm).astype(BF16)
                dp = (_dot(d_o, vb, NT) * dm).astype(BF16)
                dsn = ds_ref[h]
                dsb = dsn.astype(BF16)
                dq_r = _dot(dp, kb, NN) + _dot(d_o, sprev.astype(BF16), NT) * qd
                dk_r = (_dot(dp, qb, TN) + _dot(vb, dsb, NT) * kd) * RET_SCALE
                d_v = _dot(p, d_o, TN) + _dot((kr * kd).astype(BF16), dsb, NN)
                ds_ref[h] = dsn * cd_ref[h] + _dot((qr * qd).astype(BF16), d_o, TN)
                dp_ref[r, h * 128:(h + 1) * 128] = (dq_r * cs - _swap_halves(dq_r) * sn).astype(BF16)
                dp_ref[r, 512 + h * 128:512 + (h + 1) * 128] = (dk_r * cs - _swap_halves(dk_r) * sn).astype(BF16)
                dp_ref[r, 1024 + h * 128:1024 + (h + 1) * 128] = d_v.astype(BF16)
                dp_ref[r, 1536 + h * 128:1536 + (h + 1) * 128] = d_gate.astype(BF16)

    cmat = pl.BlockSpec((H, CHUNK, CHUNK), lambda i: (0, 0, 0))
    vec = pl.BlockSpec((1, BRANCH_W), lambda i: (0, 0))
    rope = pl.BlockSpec((tb, 128), lambda i: (nb - 1 - i, 0))
    blk = pl.BlockSpec((tb, BRANCH_W), lambda i: (nb - 1 - i, 0))
    wide = pl.BlockSpec((tb, 2048), lambda i: (nb - 1 - i, 0))
    j = _job_args(job, 12, 3)
    res = pl.pallas_call(
        _hosting(body, job, 12, 3, 1, nb), name=name, grid=(nb,),
        in_specs=[wide, rope, rope, cmat, cmat, cmat, cmat, vec, vec, blk,
                  pl.BlockSpec((nch, H, CHUNK, CHUNK), lambda i: (nb - 1 - i, 0, 0, 0)), blk] + j["in_specs"],
        out_specs=[wide, vec, vec] + j["out_specs"],
        out_shape=[jax.ShapeDtypeStruct((T, 2048), BF16), jax.ShapeDtypeStruct((1, BRANCH_W), F32),
                   jax.ShapeDtypeStruct((1, BRANCH_W), F32)] + j["out_shape"],
        scratch_shapes=[pltpu.VMEM((H, CHUNK, CHUNK), F32)] + j["scratch"], input_output_aliases=j["aliases"],
        compiler_params=_params(("arbitrary",)),
    )(proj, cosf, sinf, *consts, gn_g, gn_b, raw, states, dout, *j["ins"])
    return res[0], res[1], res[2], list(res[3:])


def _sb_masks():
    row = lax.broadcasted_iota(jnp.int32, (CHUNK, CHUNK), 0)
    lane = lax.broadcasted_iota(jnp.int32, (CHUNK, CHUNK), 1)
    return row, lane


SB_QT = 256
SB_DEAD = -105.0


def _pair(v):
    hi = v.astype(BF16)
    return jnp.concatenate([hi, (v - hi.astype(F32)).astype(BF16)], axis=1)


def _sb_consts():
    r = lax.broadcasted_iota(jnp.int32, (256, 256), 0) & 127
    c = lax.broadcasted_iota(jnp.int32, (256, 256), 1)
    ones = c >= 128
    lane = lax.broadcasted_iota(jnp.int32, (CHUNK, CHUNK), 1)
    return (ones | (r > c)).astype(BF16), (ones | (r >= c)).astype(BF16), (lane < 64, lane >= 64)


def _per_head(x, hms):
    return jnp.concatenate([jnp.where(hm, x, 0.0) for hm in hms], axis=0).astype(BF16)


def _sb_logits(qb, kb2, mask2):
    z = _dot(qb, kb2, NT)
    l1p = jnp.log(1.0 + jnp.exp(-jnp.abs(z)))
    lsp = jnp.minimum(z, 0.0) - l1p
    lsn = lsp - z
    if mask2 is not None:
        lsn = jnp.where(mask2, lsn, 0.0)
    return lsp, lsn


def _sb_tile_mask(qt):
    trow = lax.broadcasted_iota(jnp.int32, (qt, 256), 0)
    tlane = lax.broadcasted_iota(jnp.int32, (qt, 256), 1) & 127
    return lambda m: (tlane + m * CHUNK) < trow


def sb_fwd(proj, *, name, job=None):
    T = proj.shape[0]
    qt = min(SB_QT, T)
    nsub = qt // CHUNK
    cb = C_SB // 128

    def body(q_ref, k_ref, v_ref, o_ref):
        u_gt, _, hms = _sb_consts()
        tile_mask = _sb_tile_mask(qt)

        def qtile(i, _):
            rq = pl.ds(pl.multiple_of(i * qt, qt), qt)
            qb = (q_ref[rq, :] * SB_SCALE).astype(BF16)

            def group(js, masks, state):
                carry, acc = list(state[:2]), state[2]
                rows = [pl.ds(pl.multiple_of(j * CHUNK, CHUNK), CHUNK) for j in js]
                logits = [_sb_logits(qb, _per_head(k_ref[rk, :], hms), m) for rk, m in zip(rows, masks)]
                sums = [[_dot(_pair(lsn[:, h * 128:(h + 1) * 128]), u_gt, NN) for h in range(2)] for _, lsn in logits]
                weights = []
                for (lsp, _), r, m in zip(logits, sums, masks):
                    a_b = []
                    for h in range(2):
                        hc = slice(h * 128, (h + 1) * 128)
                        a = jnp.exp(lsp[:, hc] + r[h][:, :128] + carry[h])
                        if m is not None:
                            a = jnp.where(m[:, hc], a, 0.0)
                        carry[h] = carry[h] + r[h][:, 128:]
                        a_b.append(a.astype(BF16))
                    weights.append(jnp.concatenate(a_b, axis=1))
                for rk, a in zip(rows, weights):
                    acc = acc + _dot(a, _per_head(v_ref[rk, :], hms), NN)
                return carry[0], carry[1], acc

            zero = jnp.zeros((qt, 128), F32)
            diag = list(reversed(range(nsub)))
            state = group([i * nsub + m for m in diag], [tile_mask(m) for m in diag], (zero, zero, zero))

            def live(c):
                return jnp.logical_and(c[0] < i, jnp.maximum(jnp.max(c[1][0]), jnp.max(c[1][1])) > SB_DEAD)

            def blocks(c):
                jj, st = c
                return jj + 1, group([(i - jj) * nsub - 1 - u for u in range(nsub)], [None] * nsub, st)

            _, state = lax.while_loop(live, blocks, (jnp.int32(0), state))
            o_ref[rq, :] = state[2]
            return 0

        lax.fori_loop(0, T // qt, qtile, 0)

    def col(off):
        return pl.BlockSpec((T, 128), lambda hp: (0, off + hp))

    steps = BRANCH_W // 128
    j = _job_args(job, 3, 1)
    res = pl.pallas_call(
        _hosting(body, job, 3, 1, 0, steps), name=name, grid=(steps,),
        in_specs=[col(cb), col(cb + 4), col(cb + 8)] + j["in_specs"], out_specs=[col(0)] + j["out_specs"],
        out_shape=[jax.ShapeDtypeStruct((T, BRANCH_W), F32)] + j["out_shape"],
        scratch_shapes=j["scratch"], input_output_aliases=j["aliases"],
        compiler_params=_params(("parallel",) if job is None else ("arbitrary",)),
    )(proj, proj, proj, *j["ins"])
    return res[0], list(res[1:])


def sb_bwd(proj, out, dout, *, name, job=None):
    T = proj.shape[0]
    qt = min(SB_QT, T)
    nsub = qt // CHUNK
    cb = C_SB // 128

    def body(q_ref, k_ref, v_ref, o_ref, do_ref, dq_ref, dk_ref, dv_ref, dkt_ref, dvt_ref):
        u_gt, u_ge, hms = _sb_consts()
        tile_mask = _sb_tile_mask(qt)
        tall_lane = lax.broadcasted_iota(jnp.int32, (qt, 128), 1)
        top = lax.broadcasted_iota(jnp.int32, (CHUNK, CHUNK), 0) < 64
        dkt_ref[...] = jnp.zeros_like(dkt_ref)
        dvt_ref[...] = jnp.zeros_like(dvt_ref)

        def qtile(i, _):
            rq = pl.ds(pl.multiple_of(i * qt, qt), qt)
            qs = q_ref[rq, :] * SB_SCALE
            qb, q_t = qs.astype(BF16), qs.T.astype(BF16)
            dov = do_ref[rq, :]
            dob, do_t = dov.astype(BF16), dov.T.astype(BF16)
            prod = dob.astype(F32) * o_ref[rq, :]
            total = [jnp.broadcast_to(jnp.sum(jnp.where(hm, prod, 0.0), axis=1, keepdims=True), (qt, 128))
                     for hm in (tall_lane < 64, tall_lane >= 64)]

            def group(js, masks, state):
                c_l, c_w, dq = list(state[:2]), list(state[2:4]), state[4]
                heads = [slice(h * 128, (h + 1) * 128) for h in range(2)]
                rows = [pl.ds(pl.multiple_of(j * CHUNK, CHUNK), CHUNK) for j in js]
                kb2 = [_per_head(k_ref[rk, :], hms) for rk in rows]
                logits = [_sb_logits(qb, kb, m) for kb, m in zip(kb2, masks)]
                da = [_dot(dob, _per_head(v_ref[rk, :], hms), NT) for rk in rows]
                sums = [[_dot(_pair(lsn[:, hc]), u_gt, NN) for hc in heads] for _, lsn in logits]
                a_b, w_all = [], []
                for (lsp, _), r, d, m in zip(logits, sums, da, masks):
                    a_h, w_h = [], []
                    for h, hc in enumerate(heads):
                        a = jnp.exp(lsp[:, hc] + r[h][:, :128] + c_l[h])
                        if m is not None:
                            a = jnp.where(m[:, hc], a, 0.0)
                        c_l[h] = c_l[h] + r[h][:, 128:]
                        a = a.astype(BF16)
                        a_h.append(a)
                        w_h.append(a.astype(F32) * d[:, hc])
                    a_b.append(jnp.concatenate(a_h, axis=1))
                    w_all.append(w_h)
                sums_w = [[_dot(_pair(w), u_ge, NN) for w in w_h] for w_h in w_all]
                dz_b = []
                for (lsp, _), w_h, r, m in zip(logits, w_all, sums_w, masks):
                    sp = jnp.exp(lsp)
                    dz_h = []
                    for h, hc in enumerate(heads):
                        later_w = r[h][:, :128] + c_w[h]
                        c_w[h] = c_w[h] + r[h][:, 128:]
                        dz = w_h[h] * (1.0 - sp[:, hc]) - sp[:, hc] * (total[h] - later_w)
                        if m is not None:
                            dz = jnp.where(m[:, hc], dz, 0.0)
                        dz_h.append(dz.astype(BF16))
                    dz_b.append(jnp.concatenate(dz_h, axis=1))
                for j, kb, a, dz in zip(js, kb2, a_b, dz_b):
                    dkt = _dot(q_t, dz, NN)
                    dvt = _dot(do_t, a, NN)
                    dkt_ref[j] += jnp.where(top, dkt[:, :128], dkt[:, 128:])
                    dvt_ref[j] += jnp.where(top, dvt[:, :128], dvt[:, 128:])
                    dq = dq + _dot(dz, kb, NN)
                return c_l[0], c_l[1], c_w[0], c_w[1], dq

            zero = jnp.zeros((qt, 128), F32)
            diag = list(reversed(range(nsub)))
            state = group([i * nsub + m for m in diag], [tile_mask(m) for m in diag], (zero,) * 5)

            def live(c):
                return jnp.logical_and(c[0] < i, jnp.maximum(jnp.max(c[1][0]), jnp.max(c[1][1])) > SB_DEAD)

            def blocks(c):
                jj, st = c
                return jj + 1, group([(i - jj) * nsub - 1 - u for u in range(nsub)], [None] * nsub, st)

            _, state = lax.while_loop(live, blocks, (jnp.int32(0), state))
            dq_ref[rq, :] = (state[4] * SB_SCALE).astype(BF16)
            return 0

        lax.fori_loop(0, T // qt, qtile, 0)

        def untranspose(jb, _):
            rk = pl.ds(pl.multiple_of(jb * CHUNK, CHUNK), CHUNK)
            dk_ref[rk, :] = dkt_ref[jb].T.astype(BF16)
            dv_ref[rk, :] = dvt_ref[jb].T.astype(BF16)
            return 0

        lax.fori_loop(0, T // CHUNK, untranspose, 0)

    def col(off):
        return pl.BlockSpec((T, 128), lambda hp: (0, off + hp))

    o16 = jax.ShapeDtypeStruct((T, BRANCH_W), BF16)
    steps = BRANCH_W // 128
    j = _job_args(job, 5, 3)
    acc = pltpu.VMEM((T // CHUNK, CHUNK, CHUNK), F32)
    res = pl.pallas_call(
        _hosting(body, job, 5, 3, 2, steps), name=name, grid=(steps,),
        in_specs=[col(cb), col(cb + 4), col(cb + 8), col(0), col(0)] + j["in_specs"],
        out_specs=[col(0), col(0), col(0)] + j["out_specs"], out_shape=[o16, o16, o16] + j["out_shape"],
        scratch_shapes=[acc, acc] + j["scratch"], input_output_aliases=j["aliases"],
        compiler_params=_params(("parallel",) if job is None else ("arbitrary",)),
    )(proj, proj, proj, out, dout, *j["ins"])
    return res[0], res[1], res[2], list(res[3:])


_G0 = math.sqrt(2.0 / math.pi)
_G1 = 0.044715


def _gelu(x):
    return 0.5 * x * (1.0 + jnp.tanh(_G0 * (x + _G1 * x * x * x)))


def _gelu_grad(x):
    t = jnp.tanh(_G0 * (x + _G1 * x * x * x))
    return 0.5 * (1.0 + t) + 0.5 * x * (1.0 - t * t) * (_G0 * (1.0 + 3.0 * _G1 * x * x))


def _tril():
    row, lane = _sb_masks()
    return row >= lane


def sgu_fwd(proj, ln_g, ln_b, w, bias, *, name):
    T = proj.shape[0]
    tb = min(512, T)
    G = BRANCH_W // 128

    def body(u_ref, v_ref, g_ref, b_ref, w_ref, bias_ref, o_ref):
        vv = _gelu(v_ref[...])
        xh, _ = _group_norm(vv)
        vn = (xh * g_ref[...] + b_ref[...]).astype(BF16)
        tril = _tril()
        for g in range(G):
            wg = jnp.where(tril, w_ref[g], 0.0).astype(BF16)
            gc = slice(g * 128, (g + 1) * 128)
            for c in range(tb // CHUNK):
                r = slice(c * CHUNK, (c + 1) * CHUNK)
                sv = _dot(wg, vn[r, gc], NN) + bias_ref[g]
                o_ref[r, gc] = _gelu(u_ref[r, gc]) * sv

    cu, cv = C_SGU // BRANCH_W, C_SGU // BRANCH_W + 1
    vec = pl.BlockSpec((1, BRANCH_W), lambda i: (0, 0))
    mat = pl.BlockSpec((G, CHUNK, CHUNK), lambda i: (0, 0, 0))
    return pl.pallas_call(
        body, name=name, grid=(T // tb,),
        in_specs=[pl.BlockSpec((tb, BRANCH_W), lambda i: (i, cu)), pl.BlockSpec((tb, BRANCH_W), lambda i: (i, cv)),
                  vec, vec, mat, mat],
        out_specs=pl.BlockSpec((tb, BRANCH_W), lambda i: (i, 0)),
        out_shape=jax.ShapeDtypeStruct((T, BRANCH_W), F32),
        compiler_params=_params(("parallel",)),
    )(proj, proj, ln_g, ln_b, w, bias)


def sgu_bwd(proj, ln_g, ln_b, w, bias, dout, *, name):
    T = proj.shape[0]
    tb = min(512, T)
    G = BRANCH_W // 128

    def body(u_ref, v_ref, g_ref, b_ref, w_ref, bias_ref, do_ref, dp_ref, dw_ref, dbias_ref, dg_ref, db_ref, dvn_ref):
        @pl.when(pl.program_id(0) == 0)
        def _():
            dw_ref[...] = jnp.zeros_like(dw_ref)
            dbias_ref[...] = jnp.zeros_like(dbias_ref)
            dg_ref[...] = jnp.zeros_like(dg_ref)
            db_ref[...] = jnp.zeros_like(db_ref)

        gv = v_ref[...]
        vv = _gelu(gv)
        xh, rstd = _group_norm(vv)
        vn = (xh * g_ref[...] + b_ref[...]).astype(BF16)
        tril = _tril()
        for g in range(G):
            wg = jnp.where(tril, w_ref[g], 0.0).astype(BF16)
            gc = slice(g * 128, (g + 1) * 128)
            for c in range(tb // CHUNK):
                r = slice(c * CHUNK, (c + 1) * CHUNK)
                vn_c = vn[r, gc]
                sv = _dot(wg, vn_c, NN) + bias_ref[g]
                gu = u_ref[r, gc]
                d_o = do_ref[r, gc]
                dp_ref[r, gc] = (d_o * sv * _gelu_grad(gu)).astype(BF16)
                dsv = d_o * _gelu(gu)
                dsv_b = dsv.astype(BF16)
                dvn_ref[r, gc] = _dot(wg, dsv_b, TN)
                dw_ref[g] += jnp.where(tril, _dot(dsv_b, vn_c, NT), 0.0)
                dbias_ref[g] += jnp.broadcast_to(jnp.sum(dsv, axis=1, keepdims=True), (CHUNK, CHUNK))
        dvn = dvn_ref[...]
        dg_ref[...] += jnp.sum(dvn * xh, axis=0, keepdims=True)
        db_ref[...] += jnp.sum(dvn, axis=0, keepdims=True)
        dxh = dvn * g_ref[...]
        m1 = jnp.mean(dxh, axis=-1, keepdims=True)
        m2 = jnp.mean(dxh * xh, axis=-1, keepdims=True)
        dp_ref[:, BRANCH_W:2 * BRANCH_W] = (rstd * (dxh - m1 - xh * m2) * _gelu_grad(gv)).astype(BF16)

    cu, cv = C_SGU // BRANCH_W, C_SGU // BRANCH_W + 1
    vec = pl.BlockSpec((1, BRANCH_W), lambda i: (0, 0))
    mat = pl.BlockSpec((G, CHUNK, CHUNK), lambda i: (0, 0, 0))
    blk = pl.BlockSpec((tb, BRANCH_W), lambda i: (i, 0))
    msh = jax.ShapeDtypeStruct((G, CHUNK, CHUNK), F32)
    vsh = jax.ShapeDtypeStruct((1, BRANCH_W), F32)
    return pl.pallas_call(
        body, name=name, grid=(T // tb,),
        in_specs=[pl.BlockSpec((tb, BRANCH_W), lambda i: (i, cu)), pl.BlockSpec((tb, BRANCH_W), lambda i: (i, cv)),
                  vec, vec, mat, mat, blk],
        out_specs=[pl.BlockSpec((tb, 2 * BRANCH_W), lambda i: (i, 0)), mat, mat, vec, vec],
        out_shape=[jax.ShapeDtypeStruct((T, 2 * BRANCH_W), BF16), msh, msh, vsh, vsh],
        scratch_shapes=[pltpu.VMEM((tb, BRANCH_W), F32)],
        compiler_params=_params(("arbitrary",)),
    )(proj, proj, ln_g, ln_b, w, bias, dout)


def merge_fwd(a1, a2, a3, p1, p2, p3, proj, *, name):
    T = a1.shape[0]
    tm, tn = min(1024, T), 512
    gb = C_GATE // tn

    def body(a1_ref, a2_ref, a3_ref, p1_ref, p2_ref, p3_ref, g1_ref, g2_ref, g3_ref, m_ref, r1_ref, r2_ref, r3_ref):
        m = None
        for a_ref, p_ref, g_ref, r_ref in ((a1_ref, p1_ref, g1_ref, r1_ref), (a2_ref, p2_ref, g2_ref, r2_ref),
                                           (a3_ref, p3_ref, g3_ref, r3_ref)):
            r = _dot(a_ref[...].astype(BF16), p_ref[...], NN)
            r_ref[...] = r.astype(r_ref.dtype)
            t = jax.nn.sigmoid(g_ref[...]) * r
            m = t if m is None else m + t
        m_ref[...] = m.astype(m_ref.dtype)

    a_spec = pl.BlockSpec((tm, BRANCH_W), lambda i, j: (i, 0))
    p_spec = pl.BlockSpec((BRANCH_W, tn), lambda i, j: (0, j))
    o_spec = pl.BlockSpec((tm, tn), lambda i, j: (i, j))
    gates = [pl.BlockSpec((tm, tn), functools.partial(lambda i, j, o: (i, o + j), o=gb + 2 * n)) for n in range(3)]
    return pl.pallas_call(
        body, name=name, grid=(T // tm, D_MODEL // tn),
        in_specs=[a_spec, a_spec, a_spec, p_spec, p_spec, p_spec, *gates],
        out_specs=[o_spec] * 4, out_shape=[jax.ShapeDtypeStruct((T, D_MODEL), BF16)] * 4,
        compiler_params=_params(("parallel", "parallel")),
    )(a1, a2, a3, p1, p2, p3, proj, proj, proj)


def _merge_bwd_epi(dm, r1, r2, r3, g1, g2, g3):
    d_r, d_g = [], []
    for r, g in ((r1, g1), (r2, g2), (r3, g3)):
        s = jax.nn.sigmoid(g)
        d_r.append(dm * s)
        d_g.append(dm * r.astype(F32) * (s * (1.0 - s)))
    return (*d_r, *d_g)


def _rows_call(fn, ins, out_dtypes, *, name, tr=256, job=None):
    first = ins[0][0] if isinstance(ins[0], tuple) else ins[0]
    R, C = first.shape[-2:]
    tr = min(tr, R)
    assert R % tr == 0, (name, R, tr)
    arrs, specs = [], []
    for x in ins:
        if isinstance(x, tuple):
            arrs.append(x[0])
            specs.append(pl.BlockSpec((None, tr, C), functools.partial(lambda i, n: (n, i, 0), n=x[1])))
        else:
            arrs.append(x)
            specs.append(pl.BlockSpec((tr, C), lambda i: (i, 0)))
    ni = len(arrs)

    def body(*refs):
        vals = fn(*[r[...] for r in refs[:ni]])
        for o_ref, v in zip(refs[ni:], vals):
            o_ref[...] = v.astype(o_ref.dtype)

    no = len(out_dtypes)
    j = _job_args(job, ni, no)
    res = pl.pallas_call(
        _hosting(body, job, ni, no, 0, R // tr), name=name, grid=(R // tr,), in_specs=specs + j["in_specs"],
        out_specs=[pl.BlockSpec((tr, C), lambda i: (i, 0)) for _ in out_dtypes] + j["out_specs"],
        out_shape=[jax.ShapeDtypeStruct((R, C), dt) for dt in out_dtypes] + j["out_shape"],
        scratch_shapes=j["scratch"], input_output_aliases=j["aliases"],
        compiler_params=_params(("parallel",) if job is None else ("arbitrary",)),
    )(*arrs, *j["ins"])
    return list(res) if job is None else (list(res[:no]), list(res[no:]))


def _tile_rows(rows, cols):
    t = 256
    while t > 8 and (t * cols > 512 * 1024 or rows % t):
        t //= 2
    return t


def _rows_at(fn, pos, ins, outs, steps, *, name, aliases=None):
    read = [n for n, (_, s) in enumerate(ins) if s is not ANY]
    ni = len(ins)

    def body(pos_ref, *refs):
        vals = fn(*[refs[n][...] for n in read])
        for o_ref, v in zip(refs[ni:], vals):
            o_ref[...] = v.astype(o_ref.dtype)

    return pl.pallas_call(
        body, name=name,
        grid_spec=pltpu.PrefetchScalarGridSpec(num_scalar_prefetch=1, grid=(steps,), in_specs=[s for _, s in ins],
                                               out_specs=[s for _, s in outs]),
        out_shape=[sh for sh, _ in outs],
        input_output_aliases={1 + i: o for i, o in (aliases or {}).items()},
        compiler_params=_params(("parallel",)),
    )(pos, *[a for a, _ in ins])


def cast_into_whole(pos, w, l, axis, *, name):
    _, r, n = w.shape
    tr = _tile_rows(r, n)
    if axis == 1:
        shape, spec = (r, n * N_CHIPS), pl.BlockSpec((tr, n), lambda i, p: (i, p[3]))
    else:
        shape, spec = (r * N_CHIPS, n), pl.BlockSpec((tr, n), lambda i, p: (p[3] * (r // tr) + i, 0))
    return _rows_at(lambda a: (a,), pos, [(w, pl.BlockSpec((None, tr, n), lambda i, p: (l, i, 0)))],
                    [(jax.ShapeDtypeStruct(shape, BF16), spec)], r // tr, name=name)[0]


def pair_sum(pos, theirs, g32, axis, *, name):
    rows2, cols = theirs.shape
    h = rows2 // (N_CHIPS if axis == 0 else 1)
    tr = _tile_rows(h, cols)
    hb = h // tr
    if axis == 1:
        own = pl.BlockSpec((tr, cols), lambda i, p: (p[2] * hb + i, 0))
    else:
        own = pl.BlockSpec((tr, cols), lambda i, p: ((2 * (i // hb) + p[2]) * hb + i % hb, 0))
    row = pl.BlockSpec((tr, cols), lambda i, p: (i, 0))
    return _rows_at(lambda t, m: (m + t.astype(F32),) * 2, pos, [(theirs, row), (g32, own)],
                    [(jax.ShapeDtypeStruct((rows2, cols), F32), row), (jax.ShapeDtypeStruct((rows2, cols), BF16), row)],
                    rows2 // tr, name=name)


def chip_sum(pos, h32, recv, l, axis, whole, *, name):
    _, depth, h, n = recv.shape
    tr = _tile_rows(h, n)
    hb = h // tr
    if axis == 1:
        mine = pl.BlockSpec((tr, n), lambda i, p: (i, p[3]))
    else:
        mine = pl.BlockSpec((tr, n), lambda i, p: (p[3] * hb + i, 0))
    ins = [(h32, mine)] + [(recv, pl.BlockSpec((None, None, tr, n), functools.partial(lambda i, p, j: (j, l, i, 0), j=j)))
                           for j in range(3)]
    if whole is not None:
        ins.append((whole, ANY))
    return _rows_at(lambda o, a, b, c: (((o + a.astype(F32)) + b.astype(F32)) + c.astype(F32),), pos, ins,
                    [(jax.ShapeDtypeStruct((depth, 2, h, n), F32), pl.BlockSpec((None, None, tr, n), lambda i, p: (l, p[2], i, 0)))],
                    hb, name=name, aliases=None if whole is None else {4: 0})[0]


def _adamw(w, g, m, v):
    m2 = ADAM_B1 * m + (1.0 - ADAM_B1) * g
    v2 = ADAM_B2 * v + (1.0 - ADAM_B2) * (g * g)
    m_hat = m2 / (1.0 - ADAM_B1 ** ADAM_STEP)
    v_hat = v2 / (1.0 - ADAM_B2 ** ADAM_STEP)
    delta = -ADAM_LR * (m_hat / (jnp.sqrt(v_hat) + ADAM_EPS) + ADAM_WD * w)
    return delta, m2, v2


def _place():
    return lax.axis_index("x"), lax.axis_index("y"), lax.axis_index("c")


def _chip_peers(x, y, c):
    return [((1 - x, y, c), 2 * (1 - x) + y), ((x, 1 - y, c), 2 * x + 1 - y), ((1 - x, 1 - y, c), 2 * (1 - x) + 1 - y)]


def _shard_of(ref, axis, k, n):
    start = pl.multiple_of(k * n, 128)
    return ref.at[pl.ds(start, n), :] if axis == 0 else ref.at[:, pl.ds(start, n)]


ANY = pl.BlockSpec(memory_space=pl.ANY)


class CopyJob:
    def __init__(self, ins, out_shape, scratch, copies, aliases=None):
        self.ins, self.out_shape, self.scratch, self.copies = list(ins), list(out_shape), list(scratch), copies
        self.aliases = dict(aliases or {})

    def start(self, ins, outs, sems):
        local, remote, _, _ = self.copies(ins, outs, sems)
        for d in local + remote:
            d.start()

    def finish(self, ins, outs, sems):
        local, remote, arrivals, relays = self.copies(ins, outs, sems)
        for needs, sends, _ in relays:
            for d in needs:
                d.wait_recv()
            for d in sends:
                d.start()
        for d in arrivals + [d for _, _, arrives in relays for d in arrives]:
            d.wait_recv()
        for d in remote + [d for _, sends, _ in relays for d in sends]:
            d.wait_send()
        for d in local:
            d.wait()


def run_job(job, *, name):
    ni, no = len(job.ins), len(job.out_shape)

    def body(*refs):
        parts = refs[:ni], refs[ni:ni + no], refs[ni + no:]
        job.start(*parts)
        job.finish(*parts)

    return pl.pallas_call(
        body, name=name, in_specs=[ANY] * ni, out_specs=[ANY] * no, out_shape=job.out_shape,
        scratch_shapes=job.scratch, input_output_aliases=job.aliases,
    )(*job.ins)


def _job_args(job, n_in, n_out):
    if job is None:
        return dict(ins=[], in_specs=[], out_specs=[], out_shape=[], scratch=[], aliases={})
    return dict(ins=job.ins, in_specs=[ANY] * len(job.ins), out_specs=[ANY] * len(job.out_shape),
                out_shape=job.out_shape, scratch=job.scratch,
                aliases={n_in + i: n_out + o for i, o in job.aliases.items()})


def _hosting(body, job, n_in, n_out, n_scratch, grid):
    if job is None:
        return body
    ji, jo = len(job.ins), len(job.out_shape)
    grid = (grid,) if isinstance(grid, int) else tuple(grid)

    def at(ends):
        hit = None
        for ax, e in enumerate(ends):
            here = pl.program_id(ax) == e
            hit = here if hit is None else jnp.logical_and(hit, here)
        return hit

    def hosted(*refs):
        o = n_in + ji
        s = o + n_out + jo
        parts = refs[n_in:o], refs[o + n_out:s], refs[s + n_scratch:]

        @pl.when(at([0] * len(grid)))
        def _():
            job.start(*parts)

        body(*refs[:n_in], *refs[o:o + n_out], *refs[s:s + n_scratch])

        @pl.when(at([g - 1 for g in grid]))
        def _():
            job.finish(*parts)

    return hosted


def _job_sems(n_remote, n_local):
    return [pltpu.SemaphoreType.DMA((n_remote,)), pltpu.SemaphoreType.DMA((n_remote,)), pltpu.SemaphoreType.DMA((n_local,))]


def gather_job(shards, axes, chips=(0, 1, 2)):
    na = len(shards)

    def copies(ins, outs, sems):
        send, recv, _ = sems
        x, y, c = _place()
        k = 2 * x + y
        remote, relays = [], []
        for a in range(na):
            r = outs[a].shape[0] // (N_CHIPS if axes[a] == 0 else 1)
            n = outs[a].shape[axes[a]] // N_CHIPS
            half = r // 2

            def part(kk, cc, a=a, n=n, half=half):
                rows = pl.ds(pl.multiple_of(cc * half + (kk * n if axes[a] == 0 else 0), 8), half)
                return outs[a].at[rows, :] if axes[a] == 0 else outs[a].at[rows, pl.ds(pl.multiple_of(kk * n, 128), n)]

            needs, passes, lands = [], [], []
            for j, (peer, kp) in enumerate(_chip_peers(x, y, c)):
                if j not in chips:
                    continue
                s = 6 * a + j
                remote.append(pltpu.make_async_remote_copy(part(k, c), part(k, c), send.at[s], recv.at[s],
                                                           device_id=peer, device_id_type=MESH))
                needs.append(pltpu.make_async_remote_copy(part(kp, c), part(kp, c), send.at[s], recv.at[s],
                                                          device_id=peer, device_id_type=MESH))
                passes.append(pltpu.make_async_remote_copy(part(kp, c), part(kp, c), send.at[s + 3], recv.at[s + 3],
                                                           device_id=(x, y, 1 - c), device_id_type=MESH))
                lands.append(pltpu.make_async_remote_copy(part(kp, 1 - c), part(kp, 1 - c), send.at[s + 3], recv.at[s + 3],
                                                          device_id=(x, y, 1 - c), device_id_type=MESH))
            relays.append((needs, passes, lands))
        return [], remote, [], relays

    out_shape = [jax.ShapeDtypeStruct(w.shape, BF16) for w in shards]
    return CopyJob(shards, out_shape, _job_sems(6 * na, 1), copies, {a: a for a in range(na)})


def scatter_job(layers, g16, axes, filled, chips=(0, 1, 2)):
    na = len(axes)

    def shard_shape(a):
        r, c = g16[a].shape
        return (r // N_CHIPS, c) if axes[a] == 0 else (r, c // N_CHIPS)

    def copies(ins, outs, sems):
        send, recv_sems, _ = sems
        x, y, c = _place()
        remote = []
        for a in range(na):
            n = shard_shape(a)[axes[a]]
            for r, (peer, kp) in enumerate(_chip_peers(x, y, c)):
                if r not in chips:
                    continue
                remote.append(pltpu.make_async_remote_copy(_shard_of(ins[a], axes[a], kp, n), outs[a].at[r, layers[a]],
                                                           send.at[3 * a + r], recv_sems.at[3 * a + r],
                                                           device_id=peer, device_id_type=MESH))
        return [], remote, remote, []

    out_shape = [jax.ShapeDtypeStruct((3, DEPTH) + shard_shape(a), BF16) for a in range(na)]
    ins = list(g16)
    aliases = {}
    for a in range(na):
        if filled[a] is not None:
            aliases[len(ins)] = a
            ins.append(filled[a])
    return CopyJob(ins, out_shape, _job_sems(3 * na, 1), copies, aliases)


def pair_job(g16, axes):
    na = len(axes)
    pieces = [1 if ax == 1 else N_CHIPS for ax in axes]

    def copies(ins, outs, sems):
        send, recv, _ = sems
        x, y, c = _place()
        remote = []
        s = 0
        for a in range(na):
            rows = g16[a].shape[0] // (2 * pieces[a])
            for kk in range(pieces[a]):
                src = ins[a].at[pl.ds(pl.multiple_of((2 * kk + 1 - c) * rows, 8), rows), :]
                remote.append(pltpu.make_async_remote_copy(src, outs[a].at[pl.ds(kk * rows, rows), :], send.at[s], recv.at[s],
                                                           device_id=(x, y, 1 - c), device_id_type=MESH))
                s += 1
        return [], remote, remote, []

    out_shape = [jax.ShapeDtypeStruct((g.shape[0] // 2, g.shape[1]), BF16) for g in g16]
    return CopyJob(g16, out_shape, _job_sems(sum(pieces), 1), copies)


def join_job(shards):
    na = len(shards)

    def copies(ins, outs, sems):
        send, recv, _ = sems
        x, y, c = _place()
        remote = [pltpu.make_async_remote_copy(outs[a].at[:, c], outs[a].at[:, c], send.at[a], recv.at[a],
                                               device_id=(x, y, 1 - c), device_id_type=MESH) for a in range(na)]
        lands = [pltpu.make_async_remote_copy(outs[a].at[:, 1 - c], outs[a].at[:, 1 - c], send.at[a], recv.at[a],
                                              device_id=(x, y, 1 - c), device_id_type=MESH) for a in range(na)]
        return [], remote, lands, []

    out_shape = [jax.ShapeDtypeStruct(s.shape, F32) for s in shards]
    return CopyJob(shards, out_shape, _job_sems(na, 1), copies, {a: a for a in range(na)})


def small_job(p):
    def copies(ins, outs, sems):
        send, recv, loc = sems
        x, y, c = _place()
        me = 4 * x + 2 * y + c
        remote, lands = [], []
        for rel in range(1, 8):
            dx, dy, dc = rel >> 2, (rel >> 1) & 1, rel & 1
            peer = (1 - x if dx else x, 1 - y if dy else y, 1 - c if dc else c)
            who = 4 * peer[0] + 2 * peer[1] + peer[2]
            remote.append(pltpu.make_async_remote_copy(ins[0], outs[0].at[me], send.at[rel - 1], recv.at[rel - 1],
                                                       device_id=peer, device_id_type=MESH))
            lands.append(pltpu.make_async_remote_copy(ins[0], outs[0].at[who], send.at[rel - 1], recv.at[rel - 1],
                                                      device_id=peer, device_id_type=MESH))
        return [pltpu.make_async_copy(ins[0], outs[0].at[me], loc.at[0])], remote, lands, []

    return CopyJob([p], [jax.ShapeDtypeStruct((8,) + p.shape, F32)], _job_sems(7, 1), copies)


def small_sum(slots):
    def add(*terms):
        acc = terms[0]
        for t in terms[1:]:
            acc = acc + t
        return (acc,)

    return _rows_call(add, [(slots, d) for d in range(8)], [F32], name="small_sum", tr=8 * 47)[0]


BIG = ("w_in", "p_ret", "p_sb", "p_sgu", "w_out", "w_up", "w_down")
BIG_AXIS = {"w_in": 1, "p_ret": 1, "p_sb": 1, "p_sgu": 1, "w_out": 0, "w_up": 1, "w_down": 0}
SMALL = ("ret_gn_g", "ret_gn_b", "sgu_ln_g", "sgu_ln_b", "sgu_w", "sgu_b", "ln1_g", "ln1_b", "ln2_g", "ln2_b")


def layer_forward(l, x0, x0h, W, sm, rope, rconsts, hooks):
    n = f"l{l}_"
    job = hooks.fwd_job(l, "proj")
    proj = matmul(x0h, W["w_in"], mode="nn", tm=4096, tn=768, tk=1024, name=n + "proj", job=job)
    if job is not None:
        proj, job_out = proj
        hooks.done(job, job_out)
    retg, raw, states = ret_fwd(proj, *rope, rconsts, sm["ret_gn_g"], sm["ret_gn_b"], name=n + "ret_fwd")
    job = hooks.fwd_job(l, "sb")
    sb, job_out = sb_fwd(proj, name=n + "sb_fwd", job=job)
    if job is not None:
        hooks.done(job, job_out)
    sg = sgu_fwd(proj, sm["sgu_ln_g"], sm["sgu_ln_b"], sm["sgu_w"], sm["sgu_bias"], name=n + "sgu_fwd")
    merged, r1, r2, r3 = merge_fwd(retg, sb, sg, W["p_ret"], W["p_sb"], W["p_sgu"], proj, name=n + "merge_fwd")
    x1, xh1, rs1, x1h = matmul_ln(merged, W["w_out"], x0, sm["ln1_g"], sm["ln1_b"], tk=1024, name=n + "out_ln1")
    job = hooks.fwd_job(l, "up")
    h1 = matmul(x1h, W["w_up"], mode="nn", tm=1024, tn=1024, tk=1024, outs=((BF16, None),), name=n + "up", job=job)
    if job is not None:
        h1, job_out = h1
        hooks.done(job, job_out)
    job = hooks.fwd_job(l, "down")
    res = matmul_ln(h1, W["w_down"], x1, sm["ln2_g"], sm["ln2_b"], pro=_relu2, tk=1024, name=n + "down_ln2", job=job)
    if job is not None:
        res, job_out = res
        hooks.done(job, job_out)
    x2, xh2, rs2, x2h = res
    saved = dict(x0h=x0h, proj=proj, retg=retg, raw=raw, states=states, sb=sb, sg=sg, merged=merged, r=(r1, r2, r3),
                 x1h=x1h, xh1=xh1, rs1=rs1, h1=h1, xh2=xh2, rs2=rs2)
    return x2, x2h, saved


def layer_backward(l, dx2, s, W, sm, rope, rconsts, hooks):
    n = f"l{l}_"
    two = ((F32, None), (BF16, None))
    gw, gs = {}, {}
    job = hooks.bwd_job(l, "ln2")
    res = ln_bwd(dx2, s["xh2"], s["rs2"], sm["ln2_g"], name=n + "ln2_bwd", job=job)
    if job is not None:
        res, job_out = res
        hooks.done(job, job_out)
    du2, du2h, gs["ln2_g"], gs["ln2_b"] = res
    job = hooks.bwd_job(l, "g_down")
    gw["w_down"] = matmul(s["h1"], du2h, mode="tn", tm=1024, tn=1024, tk=2048, pro=_relu2, outs=two, name=n + "g_down", job=job)
    if job is not None:
        gw["w_down"], job_out = gw["w_down"]
        hooks.done(job, job_out)
    dh1 = matmul(du2h, W["w_down"], mode="nt", tm=1024, tn=1024, tk=1024, outs=((BF16, None),),
                 epi=lambda acc, h: (acc * (2.0 * jnp.maximum(h.astype(F32), 0.0)),), tiles=(s["h1"],), name=n + "d_h1")
    job = hooks.bwd_job(l, "g_up")
    gw["w_up"] = matmul(s["x1h"], dh1, mode="tn", tm=1024, tn=1024, tk=2048, outs=two, name=n + "g_up", job=job)
    if job is not None:
        gw["w_up"], job_out = gw["w_up"]
        hooks.done(job, job_out)
    dx1 = matmul(dh1, W["w_up"], mode="nt", tm=1024, tn=1024, tk=2048,
                 epi=lambda acc, d: (acc + ALPHA * d,), tiles=(du2,), name=n + "d_x1")
    du1, du1h, gs["ln1_g"], gs["ln1_b"] = ln_bwd(dx1, s["xh1"], s["rs1"], sm["ln1_g"], name=n + "ln1_bwd")
    gw["w_out"] = matmul(s["merged"], du1h, mode="tn", tm=1024, tn=1024, tk=2048, outs=two, name=n + "g_out")
    gate0 = C_GATE // 512
    dr1, dr2, dr3, dg1, dg2, dg3 = matmul(
        du1h, W["w_out"], mode="nt", tm=1024, tn=512, tk=1024, outs=((BF16, None),) * 6, epi=_merge_bwd_epi,
        tiles=(*s["r"], (s["proj"], gate0), (s["proj"], gate0 + 2), (s["proj"], gate0 + 4)), name=n + "d_merged")
    d_branch = {}
    for nm, a, dr in (("p_ret", s["retg"], dr1), ("p_sb", s["sb"], dr2), ("p_sgu", s["sg"], dr3)):
        gw[nm] = matmul(a, dr, mode="tn", tm=512, tn=1024, tk=2048, outs=two, name=n + "g_" + nm)
        d_branch[nm] = matmul(dr, W[nm], mode="nt", tm=1024, tn=512, tk=1024, name=n + "d_" + nm)
    job = hooks.pair(l, gw)
    dret, gs["ret_gn_g"], gs["ret_gn_b"], job_out = ret_bwd(s["proj"], *rope, rconsts, sm["ret_gn_g"], sm["ret_gn_b"],
                                                             s["raw"], s["states"], d_branch["p_ret"], name=n + "ret_bwd", job=job)
    if job is not None:
        hooks.done(job, job_out)
    job = hooks.scatter(l) if job is not None else None
    dsq, dsk, dsv, job_out = sb_bwd(s["proj"], s["sb"], d_branch["p_sb"], name=n + "sb_bwd", job=job)
    if job is not None:
        hooks.done(job, job_out)
    dsgu, gs["sgu_w"], dbias, gs["sgu_ln_g"], gs["sgu_ln_b"] = sgu_bwd(
        s["proj"], sm["sgu_ln_g"], sm["sgu_ln_b"], sm["sgu_w"], sm["sgu_bias"], d_branch["p_sgu"], name=n + "sgu_bwd")
    gs["sgu_b"] = dbias[:, :, 0]
    dproj = jnp.concatenate([dret, dsq, dsk, dsv, dsgu, dg1, dg2, dg3], axis=1)
    job = hooks.small(l, gs)
    gw["w_in"] = matmul(s["x0h"], dproj, mode="tn", tm=1024, tn=1536, tk=2048, outs=two, name=n + "g_in", job=job)
    if job is not None:
        gw["w_in"], job_out = gw["w_in"]
        hooks.done(job, job_out)
    job = hooks.tail(l, gw["w_in"])
    dx0 = matmul(dproj, W["w_in"], mode="nt", tm=1024, tn=1024, tk=2560,
                 epi=lambda acc, d: (acc + ALPHA * d,), tiles=(du1,), name=n + "d_x0", job=job)
    if job is not None:
        dx0, job_out = dx0
        hooks.done(job, job_out)
    return dx0, gw, gs


def local_step(x, target, small, plan):
    T = x.shape[0]
    rope = _rope_tables(T)
    rconsts = _ret_consts()
    sms = []
    for l in range(DEPTH):
        sm = {k: small[k][l][None, :] for k in SMALL if k not in ("sgu_w", "sgu_b")}
        sm["sgu_w"] = small["sgu_w"][l]
        sm["sgu_bias"] = jnp.broadcast_to(small["sgu_b"][l][:, :, None], (4, CHUNK, CHUNK))
        sms.append(sm)
    h, saved = x, []
    job = plan.first_job()
    hh = _rows_call(lambda a: (a,), [x], [BF16], name="cast_x", job=job)
    if job is not None:
        hh, job_out = hh
        plan.done(job, job_out)
    hh = hh[0]
    for l in range(DEPTH):
        h, hh, s = layer_forward(l, h, hh, plan.weights(l), sms[l], rope, rconsts, plan)
        saved.append(s)
    dy, sq = loss_head(h, target)
    gs = {k: [None] * DEPTH for k in SMALL}
    for l in reversed(range(DEPTH)):
        dy, gwl, gsl = layer_backward(l, dy, saved[l], plan.weights(l), sms[l], rope, rconsts, plan)
        plan.grads(l, gwl)
        for k in SMALL:
            gs[k][l] = gsl[k].reshape(small[k].shape[1:])
    return sq[0, 0], dy, {k: jnp.stack(v) for k, v in gs.items()}


EARLY_GRADS = ("p_ret", "p_sb", "p_sgu", "w_out", "w_up", "w_down")


class _StepPlan:
    def __init__(self, pos, shards16):
        self.pos = pos
        self.shards16 = shards16
        self.full = [dict() for _ in range(DEPTH)]
        self.gw = [None] * DEPTH
        self.bufs = {}
        self.sums = {}
        self.gs = [None] * DEPTH

    def first_job(self):
        return self._gather([(0, "w_in")])

    def weights(self, l):
        return self.full[l]

    def grads(self, l, gw):
        self.gw[l] = gw

    def _gather(self, items, chips=(0, 1, 2)):
        job = gather_job([self.shards16[l][k] for l, k in items], [BIG_AXIS[k] for _, k in items], chips)
        job.note = ("gather" if 2 in chips else "gather_part", items)
        return job

    def _pair(self, items):
        job = pair_job([g[1] for _, _, g in items], [BIG_AXIS[k] for _, k, _ in items])
        job.note = ("pair", items)
        return job

    def fwd_job(self, l, host):
        if host == "proj":
            return None
        if host == "sb":
            return self._gather([(l, k) for k in BIG[1:]])
        if l + 1 == DEPTH:
            return None
        return self._gather([(l + 1, "w_in")], (0, 1) if host == "up" else (2,))

    def bwd_job(self, l, host):
        if l + 1 == DEPTH:
            return None
        if host == "ln2":
            job = self._pair([(l + 1, "w_in", self.gw[l + 1]["w_in"])])
            job.note = ("pair_w_in", job.note[1])
            return job
        items, sums16 = self.summed_w_in
        job = scatter_job([l_ for l_, _, _ in items], sums16, [BIG_AXIS[k] for _, k, _ in items],
                          [self.bufs.get(k) for _, k, _ in items], (0, 1) if host == "g_down" else (2,))
        job.note = ("scatter", items)
        return job

    def pair(self, l, ready):
        return self._pair([(l, k, ready[k]) for k in EARLY_GRADS])

    def scatter(self, l):
        items, sums16 = self.summed
        job = scatter_job([l_ for l_, _, _ in items], sums16, [BIG_AXIS[k] for _, k, _ in items],
                          [self.bufs.get(k) for _, k, _ in items])
        job.note = ("scatter", items)
        return job

    def small(self, l, gs):
        self.gs[l] = {k: gs[k].reshape(-1) for k in SMALL}
        if l != 0:
            return None
        job = small_job(_pack_small({k: jnp.stack([self.gs[l_][k] for l_ in range(DEPTH)]) for k in SMALL}))
        job.note = ("small", [])
        return job

    def tail(self, l, g):
        if l != 0:
            return None
        last = self._pair([(0, "w_in", g)])
        self.done(last, run_job(last, name="pair_last"))
        return self.scatter(0)

    def done(self, job, outs):
        kind, items = job.note
        if kind == "small":
            self.small_slots = outs[0]
        if kind in ("pair", "pair_w_in"):
            sums16 = []
            for a, (l, k, g) in enumerate(items):
                self.sums[(l, k)], s16 = pair_sum(self.pos, outs[a], g[0], BIG_AXIS[k], name=f"pair_sum_{k}_{l}")
                sums16.append(s16)
            if kind == "pair":
                self.summed = (items, sums16)
            else:
                self.summed_w_in = (items, sums16)
        for a, item in enumerate(items):
            if kind == "gather_part":
                self.shards16[item[0]][item[1]] = outs[a]
            elif kind == "gather":
                self.full[item[0]][item[1]] = outs[a]
            elif kind == "scatter":
                self.bufs[item[1]] = outs[a]

    def finish(self):
        return self.bufs, self.sums


def _flat2(a):
    return a.reshape(-1, a.shape[-1])


def _pack_small(d, pre=""):
    return jnp.concatenate([d[pre + k].reshape(-1) for k in SMALL]).reshape(-1, 128)


def kernel(x, w_in, ret_gn_g, ret_gn_b, sgu_ln_g, sgu_ln_b, sgu_w, sgu_b, p_ret, p_sb, p_sgu, w_out, ln1_g, ln1_b, w_up, w_down, ln2_g, ln2_b, loss_target, m_w_in, m_ret_gn_g, m_ret_gn_b, m_sgu_ln_g, m_sgu_ln_b, m_sgu_w, m_sgu_b, m_p_ret, m_p_sb, m_p_sgu, m_w_out, m_ln1_g, m_ln1_b, m_w_up, m_w_down, m_ln2_g, m_ln2_b, v_w_in, v_ret_gn_g, v_ret_gn_b, v_sgu_ln_g, v_sgu_ln_b, v_sgu_w, v_sgu_b, v_p_ret, v_p_sb, v_p_sgu, v_w_out, v_ln1_g, v_ln1_b, v_w_up, v_w_down, v_ln2_g, v_ln2_b):
    given = dict(locals())
    order = BIG[:1] + SMALL[:6] + BIG[1:5] + SMALL[6:8] + BIG[5:7] + SMALL[8:10]
    L = DEPTH

    px, py, pc = _place()
    pos = jnp.stack([px, py, pc, 2 * px + py]).astype(jnp.int32)

    shards16 = [{k: cast_into_whole(pos, given[k], l, BIG_AXIS[k], name=f"cast_{k}_{l}") for k in BIG} for l in range(L)]
    plan = _StepPlan(pos, shards16)
    sq, dx, _ = local_step(x[0], loss_target[0], {k: given[k] for k in SMALL}, plan)
    loss = 0.5 * lax.psum(sq, ("x", "y", "c"))

    bufs, sums = plan.finish()
    shards = []
    for k in BIG:
        whole = None
        for l in range(L):
            whole = chip_sum(pos, sums[(l, k)], bufs[k], l, BIG_AXIS[k], whole, name=f"chip_sum_{k}_{l}")
        shards.append(whole)
    joined = run_job(join_job(shards), name="join_halves")
    out = {}
    for a, k in enumerate(BIG):
        shp = given[k].shape
        res = _rows_call(lambda g_, w_, m_, v_: (g_,) + _adamw(w_, g_, m_, v_),
                         [joined[a].reshape(-1, shp[-1]), _flat2(given[k]), _flat2(given["m_" + k]), _flat2(given["v_" + k])],
                         [F32] * 4, name="adamw_" + k)
        out[k] = [r.reshape(shp) for r in res]

    pack = _pack_small
    res = _rows_call(lambda g_, w_, m_, v_: (g_,) + _adamw(w_, g_, m_, v_),
                     [small_sum(plan.small_slots), pack(given), pack(given, "m_"), pack(given, "v_")], [F32] * 4,
                     name="adamw_small", tr=8 * 47)
    off = 0
    for k in SMALL:
        sz = given[k].size
        out[k] = [r.reshape(-1)[off:off + sz].reshape(given[k].shape) for r in res]
        off += sz

    grads = [out[k][0] for k in order]
    deltas = [out[k][1] for k in order]
    new_m = [out[k][2] for k in order]
    new_v = [out[k][3] for k in order]
    return (loss, dx[None], *grads, *deltas, *new_m, *new_v)
```

```python
import functools
import math

import jax
import jax.numpy as jnp
from jax import lax
from jax.experimental import pallas as pl
from jax.experimental.pallas import tpu as pltpu

F32 = jnp.float32
BF16 = jnp.bfloat16

D_MODEL = 1024
SEQ = 4096
DEPTH = 2
CHUNK = 128
RET_HEADS = 4
BRANCH_W = 512
N_IN = 7680
D_FF = 4096
LN_EPS = 1e-5
ROPE_BASE = 10000.0
ALPHA = (2 * DEPTH) ** 0.25
RET_SCALE = 128 ** -0.5
SB_SCALE = 64 ** -0.5
C_RET, C_SB, C_SGU, C_GATE = 0, 2048, 3584, 4608

ADAM_LR, ADAM_B1, ADAM_B2, ADAM_EPS, ADAM_WD, ADAM_STEP = 0.001, 0.9, 0.999, 1e-08, 0.01, 10

N_CHIPS = 4
VMEM_LIMIT = 56 * 1024 * 1024
MESH = pl.DeviceIdType.MESH

NN = ((1,), (0,))
NT = ((1,), (1,))
TN = ((0,), (0,))


def _dot(a, b, dims):
    return lax.dot_general(a, b, (dims, ((), ())), preferred_element_type=F32)


def _params(sem):
    return pltpu.CompilerParams(dimension_semantics=sem, vmem_limit_bytes=VMEM_LIMIT)


def _relu2(h):
    r = jnp.maximum(h.astype(F32), 0.0)
    return r * r


def matmul(a, b, *, mode, tm, tn, tk, outs=((F32, None),), pro=None, epi=None, tiles=(), rows=(), name, job=None):
    if mode == "nn":
        (M, K), N = a.shape, b.shape[1]
    elif mode == "nt":
        (M, K), N = a.shape, b.shape[0]
    else:
        (K, M), N = a.shape, b.shape[1]
    tm, tn, tk = min(tm, M), min(tn, N), min(tk, K)
    assert M % tm == 0 and N % tn == 0 and K % tk == 0, (name, M, N, K, tm, tn, tk)
    if mode == "nn":
        a_spec = pl.BlockSpec((tm, tk), lambda i, j, k: (i, k))
        b_spec = pl.BlockSpec((tk, tn), lambda i, j, k: (k, j))
        dims = NN
    elif mode == "nt":
        a_spec = pl.BlockSpec((tm, tk), lambda i, j, k: (i, k))
        b_spec = pl.BlockSpec((tn, tk), lambda i, j, k: (j, k))
        dims = NT
    else:
        a_spec = pl.BlockSpec((tk, tm), lambda i, j, k: (k, i))
        b_spec = pl.BlockSpec((tk, tn), lambda i, j, k: (k, j))
        dims = TN
    nk = K // tk
    nt_, nr, no = len(tiles), len(rows), len(outs)

    def body(a_ref, b_ref, *rest):
        tile_refs = rest[:nt_]
        row_refs = rest[nt_:nt_ + nr]
        out_refs = rest[nt_ + nr:nt_ + nr + no]
        av = a_ref[...]
        if pro is not None:
            av = pro(av)
        p = _dot(av.astype(BF16), b_ref[...].astype(BF16), dims)

        def finish(acc):
            vals = (acc,) * no if epi is None else epi(acc, *[r[...] for r in tile_refs], *[r[...] for r in row_refs])
            for o_ref, v in zip(out_refs, vals):
                o_ref[...] = v.astype(o_ref.dtype)

        if nk == 1:
            finish(p)
        else:
            acc_ref = rest[-1]
            k = pl.program_id(2)

            @pl.when(k == 0)
            def _():
                acc_ref[...] = p

            @pl.when(k > 0)
            def _():
                acc_ref[...] += p

            @pl.when(k == nk - 1)
            def _():
                finish(acc_ref[...])

    out_shape, out_specs = [], []
    for dt, width in outs:
        if width is None:
            out_shape.append(jax.ShapeDtypeStruct((M, N), dt))
            out_specs.append(pl.BlockSpec((tm, tn), lambda i, j, k: (i, j)))
        else:
            assert N == tn
            out_shape.append(jax.ShapeDtypeStruct((M, width), dt))
            out_specs.append(pl.BlockSpec((tm, width), lambda i, j, k: (i, 0)))
    in_specs = [a_spec, b_spec]
    offs = [t[1] if isinstance(t, tuple) else 0 for t in tiles]
    tiles = [t[0] if isinstance(t, tuple) else t for t in tiles]
    in_specs += [pl.BlockSpec((tm, tn), functools.partial(lambda i, j, k, o: (i, j + o), o=o)) for o in offs]
    in_specs += [pl.BlockSpec((1, tn), lambda i, j, k: (0, j)) for _ in rows]
    grid = (M // tm, N // tn, nk)
    scratch = [pltpu.VMEM((tm, tn), F32)] if nk > 1 else []
    j = _job_args(job, len(in_specs), no)
    res = pl.pallas_call(
        _hosting(body, job, len(in_specs), no, len(scratch), grid), name=name, grid=grid,
        in_specs=in_specs + j["in_specs"], out_specs=out_specs + j["out_specs"], out_shape=out_shape + j["out_shape"],
        scratch_shapes=scratch + j["scratch"], input_output_aliases=j["aliases"],
        compiler_params=_params(("parallel", "parallel", "arbitrary") if job is None else ("arbitrary",) * 3),
    )(a, b, *tiles, *rows, *j["ins"])
    mine = res[0] if no == 1 else list(res[:no])
    return mine if job is None else (mine, list(res[no:]))


def _ln_epi(acc, res, g, b):
    u = ALPHA * res + acc
    mu = jnp.mean(u, axis=-1, keepdims=True)
    xc = u - mu
    var = jnp.mean(xc * xc, axis=-1, keepdims=True)
    rstd = lax.rsqrt(var + LN_EPS)
    xhat = xc * rstd
    y = xhat * g + b
    return y, xhat, jnp.broadcast_to(rstd, (u.shape[0], 128)), y


def matmul_ln(a, w, res, g, b, *, pro=None, tk, name, job=None):
    n = w.shape[1]
    return matmul(a, w, mode="nn", tm=1024, tn=n, tk=tk, pro=pro, epi=_ln_epi, tiles=(res,), rows=(g, b),
                  outs=((F32, None), (F32, None), (F32, 128), (BF16, None)), name=name, job=job)


def ln_bwd(dy, xhat, rstd, g, *, name, job=None):
    T, D = dy.shape
    tm = min(512, T)

    def body(dy_ref, xh_ref, rs_ref, g_ref, du_ref, du16_ref, dg_ref, db_ref):
        dyv, xh = dy_ref[...], xh_ref[...]
        r = rs_ref[:, 0:1]
        dxh = dyv * g_ref[...]
        m1 = jnp.mean(dxh, axis=-1, keepdims=True)
        m2 = jnp.mean(dxh * xh, axis=-1, keepdims=True)
        du = r * (dxh - m1 - xh * m2)
        du_ref[...] = du
        du16_ref[...] = du.astype(BF16)

        @pl.when(pl.program_id(0) == 0)
        def _():
            dg_ref[...] = jnp.zeros_like(dg_ref)
            db_ref[...] = jnp.zeros_like(db_ref)

        dg_ref[...] += jnp.sum(dyv * xh, axis=0, keepdims=True)
        db_ref[...] += jnp.sum(dyv, axis=0, keepdims=True)

    row = pl.BlockSpec((tm, D), lambda i: (i, 0))
    vec = pl.BlockSpec((1, D), lambda i: (0, 0))
    j = _job_args(job, 4, 4)
    res = pl.pallas_call(
        _hosting(body, job, 4, 4, 0, T // tm), name=name, grid=(T // tm,),
        in_specs=[row, row, pl.BlockSpec((tm, 128), lambda i: (i, 0)), vec] + j["in_specs"],
        out_specs=[row, row, vec, vec] + j["out_specs"],
        out_shape=[jax.ShapeDtypeStruct((T, D), F32), jax.ShapeDtypeStruct((T, D), BF16),
                   jax.ShapeDtypeStruct((1, D), F32), jax.ShapeDtypeStruct((1, D), F32)] + j["out_shape"],
        scratch_shapes=j["scratch"], input_output_aliases=j["aliases"],
        compiler_params=_params(("arbitrary",)),
    )(dy, xhat, rstd, g, *j["ins"])
    return list(res[:4]) if job is None else (list(res[:4]), list(res[4:]))


def loss_head(y, target):
    T, D = y.shape
    tm = min(512, T)

    def body(y_ref, t_ref, dy_ref, s_ref):
        e = y_ref[...] - t_ref[...]
        dy_ref[...] = e * (1.0 / D)

        @pl.when(pl.program_id(0) == 0)
        def _():
            s_ref[...] = jnp.zeros_like(s_ref)

        s_ref[...] += jnp.sum(jnp.mean(e * e, axis=-1, keepdims=True))

    row = pl.BlockSpec((tm, D), lambda i: (i, 0))
    return pl.pallas_call(
        body, name="loss_head", grid=(T // tm,),
        in_specs=[row, row], out_specs=[row, pl.BlockSpec((8, 128), lambda i: (0, 0))],
        out_shape=[jax.ShapeDtypeStruct((T, D), F32), jax.ShapeDtypeStruct((8, 128), F32)],
        compiler_params=_params(("arbitrary",)),
    )(y, target)


def _rope_tables(T):
    half = 64
    inv_freq = ROPE_BASE ** (-jnp.arange(half, dtype=F32) / half)
    ang = jnp.arange(T, dtype=jnp.int32).astype(F32)[:, None] * inv_freq[None, :]
    cos, sin = jnp.cos(ang), jnp.sin(ang)
    return jnp.concatenate([cos, cos], axis=1), jnp.concatenate([-sin, sin], axis=1)


def _ret_consts():
    H = RET_HEADS
    log_g = jnp.log(1.0 - 2.0 ** (-5.0 - jnp.arange(H, dtype=F32)))
    idx = jnp.arange(CHUNK, dtype=F32)
    diff = idx[:, None] - idx[None, :]
    dmat = jnp.where(diff[None] >= 0, jnp.exp(log_g[:, None, None] * diff[None]), 0.0)
    kd = jnp.exp(log_g[:, None] * (CHUNK - 1 - idx)[None, :])
    qd = jnp.exp(log_g[:, None] * (idx + 1.0)[None, :])
    cd = jnp.exp(log_g * CHUNK)
    full = (H, CHUNK, CHUNK)
    return (dmat.astype(F32), jnp.broadcast_to(kd[:, :, None], full), jnp.broadcast_to(qd[:, :, None], full),
            jnp.broadcast_to(cd[:, None, None], full))


def _swap_halves(v):
    return pltpu.roll(v, 64, 1)


def _group_norm(o):
    mu = jnp.mean(o, axis=-1, keepdims=True)
    xc = o - mu
    var = jnp.mean(xc * xc, axis=-1, keepdims=True)
    rstd = lax.rsqrt(var + LN_EPS)
    return xc * rstd, rstd


def ret_fwd(proj, cosf, sinf, consts, gn_g, gn_b, *, name):
    T = proj.shape[0]
    tb = min(512, T)
    nch = tb // CHUNK
    H = RET_HEADS

    def body(p_ref, cos_ref, sin_ref, dm_ref, kd_ref, qd_ref, cd_ref, g_ref, b_ref, out_ref, raw_ref, st_ref, s_ref):
        @pl.when(pl.program_id(0) == 0)
        def _():
            s_ref[...] = jnp.zeros_like(s_ref)

        for c in range(nch):
            r = slice(c * CHUNK, (c + 1) * CHUNK)
            cs, sn = cos_ref[r, :], sin_ref[r, :]
            for h in range(H):
                hc = slice(h * 128, (h + 1) * 128)
                q = p_ref[r, h * 128:(h + 1) * 128]
                k = p_ref[r, 512 + h * 128:512 + (h + 1) * 128]
                v = p_ref[r, 1024 + h * 128:1024 + (h + 1) * 128]
                gt = p_ref[r, 1536 + h * 128:1536 + (h + 1) * 128]
                qr = q * cs + _swap_halves(q) * sn
                kr = (k * cs + _swap_halves(k) * sn) * RET_SCALE
                sprev = s_ref[h]
                st_ref[c, h] = sprev
                qb, kb, vb = qr.astype(BF16), kr.astype(BF16), v.astype(BF16)
                s = _dot(qb, kb, NT) * dm_ref[h]
                o = _dot(s.astype(BF16), vb, NN) + _dot((qr * qd_ref[h]).astype(BF16), sprev.astype(BF16), NN)
                s_ref[h] = sprev * cd_ref[h] + _dot((kr * kd_ref[h]).astype(BF16), vb, TN)
                raw_ref[r, hc] = o
                y, _ = _group_norm(o)
                out_ref[r, hc] = (gt * jax.nn.sigmoid(gt)) * (y * g_ref[:, hc] + b_ref[:, hc])

    cmat = pl.BlockSpec((H, CHUNK, CHUNK), lambda i: (0, 0, 0))
    vec = pl.BlockSpec((1, BRANCH_W), lambda i: (0, 0))
    rope = pl.BlockSpec((tb, 128), lambda i: (i, 0))
    blk = pl.BlockSpec((tb, BRANCH_W), lambda i: (i, 0))
    return pl.pallas_call(
        body, name=name, grid=(T // tb,),
        in_specs=[pl.BlockSpec((tb, 2048), lambda i: (i, 0)), rope, rope, cmat, cmat, cmat, cmat, vec, vec],
        out_specs=[blk, blk, pl.BlockSpec((nch, H, CHUNK, CHUNK), lambda i: (i, 0, 0, 0))],
        out_shape=[jax.ShapeDtypeStruct((T, BRANCH_W), F32), jax.ShapeDtypeStruct((T, BRANCH_W), F32),
                   jax.ShapeDtypeStruct((T // CHUNK, H, CHUNK, CHUNK), F32)],
        scratch_shapes=[pltpu.VMEM((H, CHUNK, CHUNK), F32)],
        compiler_params=_params(("arbitrary",)),
    )(proj, cosf, sinf, *consts, gn_g, gn_b)


def ret_bwd(proj, cosf, sinf, consts, gn_g, gn_b, raw, states, dout, *, name, job=None):
    T = proj.shape[0]
    tb = min(512, T)
    nch = tb // CHUNK
    nb = T // tb
    H = RET_HEADS

    def body(p_ref, cos_ref, sin_ref, dm_ref, kd_ref, qd_ref, cd_ref, g_ref, b_ref, raw_ref, st_ref, do_ref,
             dp_ref, dg_ref, db_ref, ds_ref):
        @pl.when(pl.program_id(0) == 0)
        def _():
            ds_ref[...] = jnp.zeros_like(ds_ref)
            dg_ref[...] = jnp.zeros_like(dg_ref)
            db_ref[...] = jnp.zeros_like(db_ref)

        for c in reversed(range(nch)):
            r = slice(c * CHUNK, (c + 1) * CHUNK)
            cs, sn = cos_ref[r, :], sin_ref[r, :]
            for h in range(H):
                hc = slice(h * 128, (h + 1) * 128)
                q = p_ref[r, h * 128:(h + 1) * 128]
                k = p_ref[r, 512 + h * 128:512 + (h + 1) * 128]
                v = p_ref[r, 1024 + h * 128:1024 + (h + 1) * 128]
                gt = p_ref[r, 1536 + h * 128:1536 + (h + 1) * 128]
                qr = q * cs + _swap_halves(q) * sn
                kr = (k * cs + _swap_halves(k) * sn) * RET_SCALE
                sprev = st_ref[c, h]
                gv = g_ref[:, hc]
                y, rstd = _group_norm(raw_ref[r, hc])
                d_out = do_ref[r, hc]
                sg = jax.nn.sigmoid(gt)
                d_gate = d_out * (y * gv + b_ref[:, hc]) * (sg * (1.0 + gt * (1.0 - sg)))
                d_aff = d_out * (gt * sg)
                dg_ref[:, hc] += jnp.sum(d_aff * y, axis=0, keepdims=True)
                db_ref[:, hc] += jnp.sum(d_aff, axis=0, keepdims=True)
                dxh = d_aff * gv
                m1 = jnp.mean(dxh, axis=-1, keepdims=True)
                m2 = jnp.mean(dxh * y, axis=-1, keepdims=True)
                d_o = (rstd * (dxh - m1 - y * m2)).astype(BF16)
                qb, kb, vb = qr.astype(BF16), kr.astype(BF16), v.astype(BF16)
                dm, kd, qd = dm_ref[h], kd_ref[h], qd_ref[h]
                p = (_dot(qb, kb, NT) * dm).astype(BF16)
                dp = (_dot(d_o, vb, NT) * dm).astype(BF16)
                dsn = ds_ref[h]
                dsb = dsn.astype(BF16)
                dq_r = _dot(dp, kb, NN) + _dot(d_o, sprev.astype(BF16), NT) * qd
                dk_r = (_dot(dp, qb, TN) + _dot(vb, dsb, NT) * kd) * RET_SCALE
                d_v = _dot(p, d_o, TN) + _dot((kr * kd).astype(BF16), dsb, NN)
                ds_ref[h] = dsn * cd_ref[h] + _dot((qr * qd).astype(BF16), d_o, TN)
                dp_ref[r, h * 128:(h + 1) * 128] = (dq_r * cs - _swap_halves(dq_r) * sn).astype(BF16)
                dp_ref[r, 512 + h * 128:512 + (h + 1) * 128] = (dk_r * cs - _swap_halves(dk_r) * sn).astype(BF16)
                dp_ref[r, 1024 + h * 128:1024 + (h + 1) * 128] = d_v.astype(BF16)
                dp_ref[r, 1536 + h * 128:1536 + (h + 1) * 128] = d_gate.astype(BF16)

    cmat = pl.BlockSpec((H, CHUNK, CHUNK), lambda i: (0, 0, 0))
    vec = pl.BlockSpec((1, BRANCH_W), lambda i: (0, 0))
    rope = pl.BlockSpec((tb, 128), lambda i: (nb - 1 - i, 0))
    blk = pl.BlockSpec((tb, BRANCH_W), lambda i: (nb - 1 - i, 0))
    wide = pl.BlockSpec((tb, 2048), lambda i: (nb - 1 - i, 0))
    j = _job_args(job, 12, 3)
    res = pl.pallas_call(
        _hosting(body, job, 12, 3, 1, nb), name=name, grid=(nb,),
        in_specs=[wide, rope, rope, cmat, cmat, cmat, cmat, vec, vec, blk,
                  pl.BlockSpec((nch, H, CHUNK, CHUNK), lambda i: (nb - 1 - i, 0, 0, 0)), blk] + j["in_specs"],
        out_specs=[wide, vec, vec] + j["out_specs"],
        out_shape=[jax.ShapeDtypeStruct((T, 2048), BF16), jax.ShapeDtypeStruct((1, BRANCH_W), F32),
                   jax.ShapeDtypeStruct((1, BRANCH_W), F32)] + j["out_shape"],
        scratch_shapes=[pltpu.VMEM((H, CHUNK, CHUNK), F32)] + j["scratch"], input_output_aliases=j["aliases"],
        compiler_params=_params(("arbitrary",)),
    )(proj, cosf, sinf, *consts, gn_g, gn_b, raw, states, dout, *j["ins"])
    return res[0], res[1], res[2], list(res[3:])


def _sb_masks():
    row = lax.broadcasted_iota(jnp.int32, (CHUNK, CHUNK), 0)
    lane = lax.broadcasted_iota(jnp.int32, (CHUNK, CHUNK), 1)
    return row, lane


SB_QT = 256
SB_DEAD = -105.0


def _pair(v):
    hi = v.astype(BF16)
    return jnp.concatenate([hi, (v - hi.astype(F32)).astype(BF16)], axis=1)


def _sb_consts():
    r = lax.broadcasted_iota(jnp.int32, (256, 256), 0) & 127
    c = lax.broadcasted_iota(jnp.int32, (256, 256), 1)
    ones = c >= 128
    lane = lax.broadcasted_iota(jnp.int32, (CHUNK, CHUNK), 1)
    return (ones | (r > c)).astype(BF16), (ones | (r >= c)).astype(BF16), (lane < 64, lane >= 64)


def _per_head(x, hms):
    return jnp.concatenate([jnp.where(hm, x, 0.0) for hm in hms], axis=0).astype(BF16)


def _sb_logits(qb, kb2, mask2):
    z = _dot(qb, kb2, NT)
    l1p = jnp.log(1.0 + jnp.exp(-jnp.abs(z)))
    lsp = jnp.minimum(z, 0.0) - l1p
    lsn = lsp - z
    if mask2 is not None:
        lsn = jnp.where(mask2, lsn, 0.0)
    return lsp, lsn


def _sb_tile_mask(qt):
    trow = lax.broadcasted_iota(jnp.int32, (qt, 256), 0)
    tlane = lax.broadcasted_iota(jnp.int32, (qt, 256), 1) & 127
    return lambda m: (tlane + m * CHUNK) < trow


def sb_fwd(proj, *, name, job=None):
    T = proj.shape[0]
    qt = min(SB_QT, T)
    nsub = qt // CHUNK
    cb = C_SB // 128

    def body(q_ref, k_ref, v_ref, o_ref):
        u_gt, _, hms = _sb_consts()
        tile_mask = _sb_tile_mask(qt)

        def qtile(i, _):
            rq = pl.ds(pl.multiple_of(i * qt, qt), qt)
            qb = (q_ref[rq, :] * SB_SCALE).astype(BF16)

            def group(js, masks, state):
                carry, acc = list(state[:2]), state[2]
                rows = [pl.ds(pl.multiple_of(j * CHUNK, CHUNK), CHUNK) for j in js]
                logits = [_sb_logits(qb, _per_head(k_ref[rk, :], hms), m) for rk, m in zip(rows, masks)]
                sums = [[_dot(_pair(lsn[:, h * 128:(h + 1) * 128]), u_gt, NN) for h in range(2)] for _, lsn in logits]
                weights = []
                for (lsp, _), r, m in zip(logits, sums, masks):
                    a_b = []
                    for h in range(2):
                        hc = slice(h * 128, (h + 1) * 128)
                        a = jnp.exp(lsp[:, hc] + r[h][:, :128] + carry[h])
                        if m is not None:
                            a = jnp.where(m[:, hc], a, 0.0)
                        carry[h] = carry[h] + r[h][:, 128:]
                        a_b.append(a.astype(BF16))
                    weights.append(jnp.concatenate(a_b, axis=1))
                for rk, a in zip(rows, weights):
                    acc = acc + _dot(a, _per_head(v_ref[rk, :], hms), NN)
                return carry[0], carry[1], acc

            zero = jnp.zeros((qt, 128), F32)
            diag = list(reversed(range(nsub)))
            state = group([i * nsub + m for m in diag], [tile_mask(m) for m in diag], (zero, zero, zero))

            def live(c):
                return jnp.logical_and(c[0] < i, jnp.maximum(jnp.max(c[1][0]), jnp.max(c[1][1])) > SB_DEAD)

            def blocks(c):
                jj, st = c
                return jj + 1, group([(i - jj) * nsub - 1 - u for u in range(nsub)], [None] * nsub, st)

            _, state = lax.while_loop(live, blocks, (jnp.int32(0), state))
            o_ref[rq, :] = state[2]
            return 0

        lax.fori_loop(0, T // qt, qtile, 0)

    def col(off):
        return pl.BlockSpec((T, 128), lambda hp: (0, off + hp))

    steps = BRANCH_W // 128
    j = _job_args(job, 3, 1)
    res = pl.pallas_call(
        _hosting(body, job, 3, 1, 0, steps), name=name, grid=(steps,),
        in_specs=[col(cb), col(cb + 4), col(cb + 8)] + j["in_specs"], out_specs=[col(0)] + j["out_specs"],
        out_shape=[jax.ShapeDtypeStruct((T, BRANCH_W), F32)] + j["out_shape"],
        scratch_shapes=j["scratch"], input_output_aliases=j["aliases"],
        compiler_params=_params(("parallel",) if job is None else ("arbitrary",)),
    )(proj, proj, proj, *j["ins"])
    return res[0], list(res[1:])


def sb_bwd(proj, out, dout, *, name, job=None):
    T = proj.shape[0]
    qt = min(SB_QT, T)
    nsub = qt // CHUNK
    cb = C_SB // 128

    def body(q_ref, k_ref, v_ref, o_ref, do_ref, dq_ref, dk_ref, dv_ref, dkt_ref, dvt_ref):
        u_gt, u_ge, hms = _sb_consts()
        tile_mask = _sb_tile_mask(qt)
        tall_lane = lax.broadcasted_iota(jnp.int32, (qt, 128), 1)
        top = lax.broadcasted_iota(jnp.int32, (CHUNK, CHUNK), 0) < 64
        dkt_ref[...] = jnp.zeros_like(dkt_ref)
        dvt_ref[...] = jnp.zeros_like(dvt_ref)

        def qtile(i, _):
            rq = pl.ds(pl.multiple_of(i * qt, qt), qt)
            qs = q_ref[rq, :] * SB_SCALE
            qb, q_t = qs.astype(BF16), qs.T.astype(BF16)
            dov = do_ref[rq, :]
            dob, do_t = dov.astype(BF16), dov.T.astype(BF16)
            prod = dob.astype(F32) * o_ref[rq, :]
            total = [jnp.broadcast_to(jnp.sum(jnp.where(hm, prod, 0.0), axis=1, keepdims=True), (qt, 128))
                     for hm in (tall_lane < 64, tall_lane >= 64)]

            def group(js, masks, state):
                c_l, c_w, dq = list(state[:2]), list(state[2:4]), state[4]
                heads = [slice(h * 128, (h + 1) * 128) for h in range(2)]
                rows = [pl.ds(pl.multiple_of(j * CHUNK, CHUNK), CHUNK) for j in js]
                kb2 = [_per_head(k_ref[rk, :], hms) for rk in rows]
                logits = [_sb_logits(qb, kb, m) for kb, m in zip(kb2, masks)]
                da = [_dot(dob, _per_head(v_ref[rk, :], hms), NT) for rk in rows]
                sums = [[_dot(_pair(lsn[:, hc]), u_gt, NN) for hc in heads] for _, lsn in logits]
                a_b, w_all = [], []
                for (lsp, _), r, d, m in zip(logits, sums, da, masks):
                    a_h, w_h = [], []
                    for h, hc in enumerate(heads):
                        a = jnp.exp(lsp[:, hc] + r[h][:, :128] + c_l[h])
                        if m is not None:
                            a = jnp.where(m[:, hc], a, 0.0)
                        c_l[h] = c_l[h] + r[h][:, 128:]
                        a = a.astype(BF16)
                        a_h.append(a)
                        w_h.append(a.astype(F32) * d[:, hc])
                    a_b.append(jnp.concatenate(a_h, axis=1))
                    w_all.append(w_h)
                sums_w = [[_dot(_pair(w), u_ge, NN) for w in w_h] for w_h in w_all]
                dz_b = []
                for (lsp, _), w_h, r, m in zip(logits, w_all, sums_w, masks):
                    sp = jnp.exp(lsp)
                    dz_h = []
                    for h, hc in enumerate(heads):
                        later_w = r[h][:, :128] + c_w[h]
                        c_w[h] = c_w[h] + r[h][:, 128:]
                        dz = w_h[h] * (1.0 - sp[:, hc]) - sp[:, hc] * (total[h] - later_w)
                        if m is not None:
                            dz = jnp.where(m[:, hc], dz, 0.0)
                        dz_h.append(dz.astype(BF16))
                    dz_b.append(jnp.concatenate(dz_h, axis=1))
                for j, kb, a, dz in zip(js, kb2, a_b, dz_b):
                    dkt = _dot(q_t, dz, NN)
                    dvt = _dot(do_t, a, NN)
                    dkt_ref[j] += jnp.where(top, dkt[:, :128], dkt[:, 128:])
                    dvt_ref[j] += jnp.where(top, dvt[:, :128], dvt[:, 128:])
                    dq = dq + _dot(dz, kb, NN)
                return c_l[0], c_l[1], c_w[0], c_w[1], dq

            zero = jnp.zeros((qt, 128), F32)
            diag = list(reversed(range(nsub)))
            state = group([i * nsub + m for m in diag], [tile_mask(m) for m in diag], (zero,) * 5)

            def live(c):
                return jnp.logical_and(c[0] < i, jnp.maximum(jnp.max(c[1][0]), jnp.max(c[1][1])) > SB_DEAD)

            def blocks(c):
                jj, st = c
                return jj + 1, group([(i - jj) * nsub - 1 - u for u in range(nsub)], [None] * nsub, st)

            _, state = lax.while_loop(live, blocks, (jnp.int32(0), state))
            dq_ref[rq, :] = (state[4] * SB_SCALE).astype(BF16)
            return 0

        lax.fori_loop(0, T // qt, qtile, 0)

        def untranspose(jb, _):
            rk = pl.ds(pl.multiple_of(jb * CHUNK, CHUNK), CHUNK)
            dk_ref[rk, :] = dkt_ref[jb].T.astype(BF16)
            dv_ref[rk, :] = dvt_ref[jb].T.astype(BF16)
            return 0

        lax.fori_loop(0, T // CHUNK, untranspose, 0)

    def col(off):
        return pl.BlockSpec((T, 128), lambda hp: (0, off + hp))

    o16 = jax.ShapeDtypeStruct((T, BRANCH_W), BF16)
    steps = BRANCH_W // 128
    j = _job_args(job, 5, 3)
    acc = pltpu.VMEM((T // CHUNK, CHUNK, CHUNK), F32)
    res = pl.pallas_call(
        _hosting(body, job, 5, 3, 2, steps), name=name, grid=(steps,),
        in_specs=[col(cb), col(cb + 4), col(cb + 8), col(0), col(0)] + j["in_specs"],
        out_specs=[col(0), col(0), col(0)] + j["out_specs"], out_shape=[o16, o16, o16] + j["out_shape"],
        scratch_shapes=[acc, acc] + j["scratch"], input_output_aliases=j["aliases"],
        compiler_params=_params(("parallel",) if job is None else ("arbitrary",)),
    )(proj, proj, proj, out, dout, *j["ins"])
    return res[0], res[1], res[2], list(res[3:])


_G0 = math.sqrt(2.0 / math.pi)
_G1 = 0.044715


def _gelu(x):
    return 0.5 * x * (1.0 + jnp.tanh(_G0 * (x + _G1 * x * x * x)))


def _gelu_grad(x):
    t = jnp.tanh(_G0 * (x + _G1 * x * x * x))
    return 0.5 * (1.0 + t) + 0.5 * x * (1.0 - t * t) * (_G0 * (1.0 + 3.0 * _G1 * x * x))


def _tril():
    row, lane = _sb_masks()
    return row >= lane


def sgu_fwd(proj, ln_g, ln_b, w, bias, *, name):
    T = proj.shape[0]
    tb = min(512, T)
    G = BRANCH_W // 128

    def body(u_ref, v_ref, g_ref, b_ref, w_ref, bias_ref, o_ref):
        vv = _gelu(v_ref[...])
        xh, _ = _group_norm(vv)
        vn = (xh * g_ref[...] + b_ref[...]).astype(BF16)
        tril = _tril()
        for g in range(G):
            wg = jnp.where(tril, w_ref[g], 0.0).astype(BF16)
            gc = slice(g * 128, (g + 1) * 128)
            for c in range(tb // CHUNK):
                r = slice(c * CHUNK, (c + 1) * CHUNK)
                sv = _dot(wg, vn[r, gc], NN) + bias_ref[g]
                o_ref[r, gc] = _gelu(u_ref[r, gc]) * sv

    cu, cv = C_SGU // BRANCH_W, C_SGU // BRANCH_W + 1
    vec = pl.BlockSpec((1, BRANCH_W), lambda i: (0, 0))
    mat = pl.BlockSpec((G, CHUNK, CHUNK), lambda i: (0, 0, 0))
    return pl.pallas_call(
        body, name=name, grid=(T // tb,),
        in_specs=[pl.BlockSpec((tb, BRANCH_W), lambda i: (i, cu)), pl.BlockSpec((tb, BRANCH_W), lambda i: (i, cv)),
                  vec, vec, mat, mat],
        out_specs=pl.BlockSpec((tb, BRANCH_W), lambda i: (i, 0)),
        out_shape=jax.ShapeDtypeStruct((T, BRANCH_W), F32),
        compiler_params=_params(("parallel",)),
    )(proj, proj, ln_g, ln_b, w, bias)


def sgu_bwd(proj, ln_g, ln_b, w, bias, dout, *, name):
    T = proj.shape[0]
    tb = min(512, T)
    G = BRANCH_W // 128

    def body(u_ref, v_ref, g_ref, b_ref, w_ref, bias_ref, do_ref, dp_ref, dw_ref, dbias_ref, dg_ref, db_ref, dvn_ref):
        @pl.when(pl.program_id(0) == 0)
        def _():
            dw_ref[...] = jnp.zeros_like(dw_ref)
            dbias_ref[...] = jnp.zeros_like(dbias_ref)
            dg_ref[...] = jnp.zeros_like(dg_ref)
            db_ref[...] = jnp.zeros_like(db_ref)

        gv = v_ref[...]
        vv = _gelu(gv)
        xh, rstd = _group_norm(vv)
        vn = (xh * g_ref[...] + b_ref[...]).astype(BF16)
        tril = _tril()
        for g in range(G):
            wg = jnp.where(tril, w_ref[g], 0.0).astype(BF16)
            gc = slice(g * 128, (g + 1) * 128)
            for c in range(tb // CHUNK):
                r = slice(c * CHUNK, (c + 1) * CHUNK)
                vn_c = vn[r, gc]
                sv = _dot(wg, vn_c, NN) + bias_ref[g]
                gu = u_ref[r, gc]
                d_o = do_ref[r, gc]
                dp_ref[r, gc] = (d_o * sv * _gelu_grad(gu)).astype(BF16)
                dsv = d_o * _gelu(gu)
                dsv_b = dsv.astype(BF16)
                dvn_ref[r, gc] = _dot(wg, dsv_b, TN)
                dw_ref[g] += jnp.where(tril, _dot(dsv_b, vn_c, NT), 0.0)
                dbias_ref[g] += jnp.broadcast_to(jnp.sum(dsv, axis=1, keepdims=True), (CHUNK, CHUNK))
        dvn = dvn_ref[...]
        dg_ref[...] += jnp.sum(dvn * xh, axis=0, keepdims=True)
        db_ref[...] += jnp.sum(dvn, axis=0, keepdims=True)
        dxh = dvn * g_ref[...]
        m1 = jnp.mean(dxh, axis=-1, keepdims=True)
        m2 = jnp.mean(dxh * xh, axis=-1, keepdims=True)
        dp_ref[:, BRANCH_W:2 * BRANCH_W] = (rstd * (dxh - m1 - xh * m2) * _gelu_grad(gv)).astype(BF16)

    cu, cv = C_SGU // BRANCH_W, C_SGU // BRANCH_W + 1
    vec = pl.BlockSpec((1, BRANCH_W), lambda i: (0, 0))
    mat = pl.BlockSpec((G, CHUNK, CHUNK), lambda i: (0, 0, 0))
    blk = pl.BlockSpec((tb, BRANCH_W), lambda i: (i, 0))
    msh = jax.ShapeDtypeStruct((G, CHUNK, CHUNK), F32)
    vsh = jax.ShapeDtypeStruct((1, BRANCH_W), F32)
    return pl.pallas_call(
        body, name=name, grid=(T // tb,),
        in_specs=[pl.BlockSpec((tb, BRANCH_W), lambda i: (i, cu)), pl.BlockSpec((tb, BRANCH_W), lambda i: (i, cv)),
                  vec, vec, mat, mat, blk],
        out_specs=[pl.BlockSpec((tb, 2 * BRANCH_W), lambda i: (i, 0)), mat, mat, vec, vec],
        out_shape=[jax.ShapeDtypeStruct((T, 2 * BRANCH_W), BF16), msh, msh, vsh, vsh],
        scratch_shapes=[pltpu.VMEM((tb, BRANCH_W), F32)],
        compiler_params=_params(("arbitrary",)),
    )(proj, proj, ln_g, ln_b, w, bias, dout)


def merge_fwd(a1, a2, a3, p1, p2, p3, proj, *, name):
    T = a1.shape[0]
    tm, tn = min(1024, T), 512
    gb = C_GATE // tn

    def body(a1_ref, a2_ref, a3_ref, p1_ref, p2_ref, p3_ref, g1_ref, g2_ref, g3_ref, m_ref, r1_ref, r2_ref, r3_ref):
        m = None
        for a_ref, p_ref, g_ref, r_ref in ((a1_ref, p1_ref, g1_ref, r1_ref), (a2_ref, p2_ref, g2_ref, r2_ref),
                                           (a3_ref, p3_ref, g3_ref, r3_ref)):
            r = _dot(a_ref[...].astype(BF16), p_ref[...], NN)
            r_ref[...] = r.astype(r_ref.dtype)
            t = jax.nn.sigmoid(g_ref[...]) * r
            m = t if m is None else m + t
        m_ref[...] = m.astype(m_ref.dtype)

    a_spec = pl.BlockSpec((tm, BRANCH_W), lambda i, j: (i, 0))
    p_spec = pl.BlockSpec((BRANCH_W, tn), lambda i, j: (0, j))
    o_spec = pl.BlockSpec((tm, tn), lambda i, j: (i, j))
    gates = [pl.BlockSpec((tm, tn), functools.partial(lambda i, j, o: (i, o + j), o=gb + 2 * n)) for n in range(3)]
    return pl.pallas_call(
        body, name=name, grid=(T // tm, D_MODEL // tn),
        in_specs=[a_spec, a_spec, a_spec, p_spec, p_spec, p_spec, *gates],
        out_specs=[o_spec] * 4, out_shape=[jax.ShapeDtypeStruct((T, D_MODEL), BF16)] * 4,
        compiler_params=_params(("parallel", "parallel")),
    )(a1, a2, a3, p1, p2, p3, proj, proj, proj)


def _merge_bwd_epi(dm, r1, r2, r3, g1, g2, g3):
    d_r, d_g = [], []
    for r, g in ((r1, g1), (r2, g2), (r3, g3)):
        s = jax.nn.sigmoid(g)
        d_r.append(dm * s)
        d_g.append(dm * r.astype(F32) * (s * (1.0 - s)))
    return (*d_r, *d_g)


def _rows_call(fn, ins, out_dtypes, *, name, tr=256, job=None):
    first = ins[0][0] if isinstance(ins[0], tuple) else ins[0]
    R, C = first.shape[-2:]
    tr = min(tr, R)
    assert R % tr == 0, (name, R, tr)
    arrs, specs = [], []
    for x in ins:
        if isinstance(x, tuple):
            arrs.append(x[0])
            specs.append(pl.BlockSpec((None, tr, C), functools.partial(lambda i, n: (n, i, 0), n=x[1])))
        else:
            arrs.append(x)
            specs.append(pl.BlockSpec((tr, C), lambda i: (i, 0)))
    ni = len(arrs)

    def body(*refs):
        vals = fn(*[r[...] for r in refs[:ni]])
        for o_ref, v in zip(refs[ni:], vals):
            o_ref[...] = v.astype(o_ref.dtype)

    no = len(out_dtypes)
    j = _job_args(job, ni, no)
    res = pl.pallas_call(
        _hosting(body, job, ni, no, 0, R // tr), name=name, grid=(R // tr,), in_specs=specs + j["in_specs"],
        out_specs=[pl.BlockSpec((tr, C), lambda i: (i, 0)) for _ in out_dtypes] + j["out_specs"],
        out_shape=[jax.ShapeDtypeStruct((R, C), dt) for dt in out_dtypes] + j["out_shape"],
        scratch_shapes=j["scratch"], input_output_aliases=j["aliases"],
        compiler_params=_params(("parallel",) if job is None else ("arbitrary",)),
    )(*arrs, *j["ins"])
    return list(res) if job is None else (list(res[:no]), list(res[no:]))


def _tile_rows(rows, cols):
    t = 256
    while t > 8 and (t * cols > 512 * 1024 or rows % t):
        t //= 2
    return t


def _rows_at(fn, pos, ins, outs, steps, *, name, aliases=None):
    read = [n for n, (_, s) in enumerate(ins) if s is not ANY]
    ni = len(ins)

    def body(pos_ref, *refs):
        vals = fn(*[refs[n][...] for n in read])
        for o_ref, v in zip(refs[ni:], vals):
            o_ref[...] = v.astype(o_ref.dtype)

    return pl.pallas_call(
        body, name=name,
        grid_spec=pltpu.PrefetchScalarGridSpec(num_scalar_prefetch=1, grid=(steps,), in_specs=[s for _, s in ins],
                                               out_specs=[s for _, s in outs]),
        out_shape=[sh for sh, _ in outs],
        input_output_aliases={1 + i: o for i, o in (aliases or {}).items()},
        compiler_params=_params(("parallel",)),
    )(pos, *[a for a, _ in ins])


def cast_into_whole(pos, w, l, axis, *, name):
    _, r, n = w.shape
    tr = _tile_rows(r, n)
    if axis == 1:
        shape, spec = (r, n * N_CHIPS), pl.BlockSpec((tr, n), lambda i, p: (i, p[3]))
    else:
        shape, spec = (r * N_CHIPS, n), pl.BlockSpec((tr, n), lambda i, p: (p[3] * (r // tr) + i, 0))
    return _rows_at(lambda a: (a,), pos, [(w, pl.BlockSpec((None, tr, n), lambda i, p: (l, i, 0)))],
                    [(jax.ShapeDtypeStruct(shape, BF16), spec)], r // tr, name=name)[0]


def pair_sum(pos, theirs, g32, axis, *, name):
    rows2, cols = theirs.shape
    h = rows2 // (N_CHIPS if axis == 0 else 1)
    tr = _tile_rows(h, cols)
    hb = h // tr
    if axis == 1:
        own = pl.BlockSpec((tr, cols), lambda i, p: (p[2] * hb + i, 0))
    else:
        own = pl.BlockSpec((tr, cols), lambda i, p: ((2 * (i // hb) + p[2]) * hb + i % hb, 0))
    row = pl.BlockSpec((tr, cols), lambda i, p: (i, 0))
    return _rows_at(lambda t, m: (m + t.astype(F32),) * 2, pos, [(theirs, row), (g32, own)],
                    [(jax.ShapeDtypeStruct((rows2, cols), F32), row), (jax.ShapeDtypeStruct((rows2, cols), BF16), row)],
                    rows2 // tr, name=name)


def chip_sum(pos, h32, recv, l, axis, whole, *, name):
    _, depth, h, n = recv.shape
    tr = _tile_rows(h, n)
    hb = h // tr
    if axis == 1:
        mine = pl.BlockSpec((tr, n), lambda i, p: (i, p[3]))
    else:
        mine = pl.BlockSpec((tr, n), lambda i, p: (p[3] * hb + i, 0))
    ins = [(h32, mine)] + [(recv, pl.BlockSpec((None, None, tr, n), functools.partial(lambda i, p, j: (j, l, i, 0), j=j)))
                           for j in range(3)]
    if whole is not None:
        ins.append((whole, ANY))
    return _rows_at(lambda o, a, b, c: (((o + a.astype(F32)) + b.astype(F32)) + c.astype(F32),), pos, ins,
                    [(jax.ShapeDtypeStruct((depth, 2, h, n), F32), pl.BlockSpec((None, None, tr, n), lambda i, p: (l, p[2], i, 0)))],
                    hb, name=name, aliases=None if whole is None else {4: 0})[0]


def _adamw(w, g, m, v):
    m2 = ADAM_B1 * m + (1.0 - ADAM_B1) * g
    v2 = ADAM_B2 * v + (1.0 - ADAM_B2) * (g * g)
    m_hat = m2 / (1.0 - ADAM_B1 ** ADAM_STEP)
    v_hat = v2 / (1.0 - ADAM_B2 ** ADAM_STEP)
    delta = -ADAM_LR * (m_hat / (jnp.sqrt(v_hat) + ADAM_EPS) + ADAM_WD * w)
    return delta, m2, v2


def _place():
    return lax.axis_index("x"), lax.axis_index("y"), lax.axis_index("c")


def _chip_peers(x, y, c):
    return [((1 - x, y, c), 2 * (1 - x) + y), ((x, 1 - y, c), 2 * x + 1 - y), ((1 - x, 1 - y, c), 2 * (1 - x) + 1 - y)]


def _shard_of(ref, axis, k, n):
    start = pl.multiple_of(k * n, 128)
    return ref.at[pl.ds(start, n), :] if axis == 0 else ref.at[:, pl.ds(start, n)]


ANY = pl.BlockSpec(memory_space=pl.ANY)


class CopyJob:
    def __init__(self, ins, out_shape, scratch, copies, aliases=None):
        self.ins, self.out_shape, self.scratch, self.copies = list(ins), list(out_shape), list(scratch), copies
        self.aliases = dict(aliases or {})

    def start(self, ins, outs, sems):
        local, remote, _, _ = self.copies(ins, outs, sems)
        for d in local + remote:
            d.start()

    def finish(self, ins, outs, sems):
        local, remote, arrivals, relays = self.copies(ins, outs, sems)
        for needs, sends, _ in relays:
            for d in needs:
                d.wait_recv()
            for d in sends:
                d.start()
        for d in arrivals + [d for _, _, arrives in relays for d in arrives]:
            d.wait_recv()
        for d in remote + [d for _, sends, _ in relays for d in sends]:
            d.wait_send()
        for d in local:
            d.wait()


def run_job(job, *, name):
    ni, no = len(job.ins), len(job.out_shape)

    def body(*refs):
        parts = refs[:ni], refs[ni:ni + no], refs[ni + no:]
        job.start(*parts)
        job.finish(*parts)

    return pl.pallas_call(
        body, name=name, in_specs=[ANY] * ni, out_specs=[ANY] * no, out_shape=job.out_shape,
        scratch_shapes=job.scratch, input_output_aliases=job.aliases,
    )(*job.ins)


def _job_args(job, n_in, n_out):
    if job is None:
        return dict(ins=[], in_specs=[], out_specs=[], out_shape=[], scratch=[], aliases={})
    return dict(ins=job.ins, in_specs=[ANY] * len(job.ins), out_specs=[ANY] * len(job.out_shape),
                out_shape=job.out_shape, scratch=job.scratch,
                aliases={n_in + i: n_out + o for i, o in job.aliases.items()})


def _hosting(body, job, n_in, n_out, n_scratch, grid):
    if job is None:
        return body
    ji, jo = len(job.ins), len(job.out_shape)
    grid = (grid,) if isinstance(grid, int) else tuple(grid)

    def at(ends):
        hit = None
        for ax, e in enumerate(ends):
            here = pl.program_id(ax) == e
            hit = here if hit is None else jnp.logical_and(hit, here)
        return hit

    def hosted(*refs):
        o = n_in + ji
        s = o + n_out + jo
        parts = refs[n_in:o], refs[o + n_out:s], refs[s + n_scratch:]

        @pl.when(at([0] * len(grid)))
        def _():
            job.start(*parts)

        body(*refs[:n_in], *refs[o:o + n_out], *refs[s:s + n_scratch])

        @pl.when(at([g - 1 for g in grid]))
        def _():
            job.finish(*parts)

    return hosted


def _job_sems(n_remote, n_local):
    return [pltpu.SemaphoreType.DMA((n_remote,)), pltpu.SemaphoreType.DMA((n_remote,)), pltpu.SemaphoreType.DMA((n_local,))]


def gather_job(shards, axes, chips=(0, 1, 2)):
    na = len(shards)

    def copies(ins, outs, sems):
        send, recv, _ = sems
        x, y, c = _place()
        k = 2 * x + y
        remote, relays = [], []
        for a in range(na):
            r = outs[a].shape[0] // (N_CHIPS if axes[a] == 0 else 1)
            n = outs[a].shape[axes[a]] // N_CHIPS
            half = r // 2

            def part(kk, cc, a=a, n=n, half=half):
                rows = pl.ds(pl.multiple_of(cc * half + (kk * n if axes[a] == 0 else 0), 8), half)
                return outs[a].at[rows, :] if axes[a] == 0 else outs[a].at[rows, pl.ds(pl.multiple_of(kk * n, 128), n)]

            needs, passes, lands = [], [], []
            for j, (peer, kp) in enumerate(_chip_peers(x, y, c)):
                if j not in chips:
                    continue
                s = 6 * a + j
                remote.append(pltpu.make_async_remote_copy(part(k, c), part(k, c), send.at[s], recv.at[s],
                                                           device_id=peer, device_id_type=MESH))
                needs.append(pltpu.make_async_remote_copy(part(kp, c), part(kp, c), send.at[s], recv.at[s],
                                                          device_id=peer, device_id_type=MESH))
                passes.append(pltpu.make_async_remote_copy(part(kp, c), part(kp, c), send.at[s + 3], recv.at[s + 3],
                                                           device_id=(x, y, 1 - c), device_id_type=MESH))
                lands.append(pltpu.make_async_remote_copy(part(kp, 1 - c), part(kp, 1 - c), send.at[s + 3], recv.at[s + 3],
                                                          device_id=(x, y, 1 - c), device_id_type=MESH))
            relays.append((needs, passes, lands))
        return [], remote, [], relays

    out_shape = [jax.ShapeDtypeStruct(w.shape, BF16) for w in shards]
    return CopyJob(shards, out_shape, _job_sems(6 * na, 1), copies, {a: a for a in range(na)})


def scatter_job(layers, g16, axes, filled, chips=(0, 1, 2)):
    na = len(axes)

    def shard_shape(a):
        r, c = g16[a].shape
        return (r // N_CHIPS, c) if axes[a] == 0 else (r, c // N_CHIPS)

    def copies(ins, outs, sems):
        send, recv_sems, _ = sems
        x, y, c = _place()
        remote = []
        for a in range(na):
            n = shard_shape(a)[axes[a]]
            for r, (peer, kp) in enumerate(_chip_peers(x, y, c)):
                if r not in chips:
                    continue
                remote.append(pltpu.make_async_remote_copy(_shard_of(ins[a], axes[a], kp, n), outs[a].at[r, layers[a]],
                                                           send.at[3 * a + r], recv_sems.at[3 * a + r],
                                                           device_id=peer, device_id_type=MESH))
        return [], remote, remote, []

    out_shape = [jax.ShapeDtypeStruct((3, DEPTH) + shard_shape(a), BF16) for a in range(na)]
    ins = list(g16)
    aliases = {}
    for a in range(na):
        if filled[a] is not None:
            aliases[len(ins)] = a
            ins.append(filled[a])
    return CopyJob(ins, out_shape, _job_sems(3 * na, 1), copies, aliases)


def pair_job(g16, axes):
    na = len(axes)
    pieces = [1 if ax == 1 else N_CHIPS for ax in axes]

    def copies(ins, outs, sems):
        send, recv, _ = sems
        x, y, c = _place()
        remote = []
        s = 0
        for a in range(na):
            rows = g16[a].shape[0] // (2 * pieces[a])
            for kk in range(pieces[a]):
                src = ins[a].at[pl.ds(pl.multiple_of((2 * kk + 1 - c) * rows, 8), rows), :]
                remote.append(pltpu.make_async_remote_copy(src, outs[a].at[pl.ds(kk * rows, rows), :], send.at[s], recv.at[s],
                                                           device_id=(x, y, 1 - c), device_id_type=MESH))
                s += 1
        return [], remote, remote, []

    out_shape = [jax.ShapeDtypeStruct((g.shape[0] // 2, g.shape[1]), BF16) for g in g16]
    return CopyJob(g16, out_shape, _job_sems(sum(pieces), 1), copies)


def join_job(shards):
    na = len(shards)

    def copies(ins, outs, sems):
        send, recv, _ = sems
        x, y, c = _place()
        remote = [pltpu.make_async_remote_copy(outs[a].at[:, c], outs[a].at[:, c], send.at[a], recv.at[a],
                                               device_id=(x, y, 1 - c), device_id_type=MESH) for a in range(na)]
        lands = [pltpu.make_async_remote_copy(outs[a].at[:, 1 - c], outs[a].at[:, 1 - c], send.at[a], recv.at[a],
                                              device_id=(x, y, 1 - c), device_id_type=MESH) for a in range(na)]
        return [], remote, lands, []

    out_shape = [jax.ShapeDtypeStruct(s.shape, F32) for s in shards]
    return CopyJob(shards, out_shape, _job_sems(na, 1), copies, {a: a for a in range(na)})


def small_job(p):
    def copies(ins, outs, sems):
        send, recv, loc = sems
        x, y, c = _place()
        me = 4 * x + 2 * y + c
        remote, lands = [], []
        for rel in range(1, 8):
            dx, dy, dc = rel >> 2, (rel >> 1) & 1, rel & 1
            peer = (1 - x if dx else x, 1 - y if dy else y, 1 - c if dc else c)
            who = 4 * peer[0] + 2 * peer[1] + peer[2]
            remote.append(pltpu.make_async_remote_copy(ins[0], outs[0].at[me], send.at[rel - 1], recv.at[rel - 1],
                                                       device_id=peer, device_id_type=MESH))
            lands.append(pltpu.make_async_remote_copy(ins[0], outs[0].at[who], send.at[rel - 1], recv.at[rel - 1],
                                                      device_id=peer, device_id_type=MESH))
        return [pltpu.make_async_copy(ins[0], outs[0].at[me], loc.at[0])], remote, lands, []

    return CopyJob([p], [jax.ShapeDtypeStruct((8,) + p.shape, F32)], _job_sems(7, 1), copies)


def small_sum(slots):
    def add(*terms):
        acc = terms[0]
        for t in terms[1:]:
            acc = acc + t
        return (acc,)

    return _rows_call(add, [(slots, d) for d in range(8)], [F32], name="small_sum", tr=8 * 47)[0]


BIG = ("w_in", "p_ret", "p_sb", "p_sgu", "w_out", "w_up", "w_down")
BIG_AXIS = {"w_in": 1, "p_ret": 1, "p_sb": 1, "p_sgu": 1, "w_out": 0, "w_up": 1, "w_down": 0}
SMALL = ("ret_gn_g", "ret_gn_b", "sgu_ln_g", "sgu_ln_b", "sgu_w", "sgu_b", "ln1_g", "ln1_b", "ln2_g", "ln2_b")


def layer_forward(l, x0, x0h, W, sm, rope, rconsts, hooks):
    n = f"l{l}_"
    job = hooks.fwd_job(l, "proj")
    proj = matmul(x0h, W["w_in"], mode="nn", tm=4096, tn=768, tk=1024, name=n + "proj", job=job)
    if job is not None:
        proj, job_out = proj
        hooks.done(job, job_out)
    retg, raw, states = ret_fwd(proj, *rope, rconsts, sm["ret_gn_g"], sm["ret_gn_b"], name=n + "ret_fwd")
    job = hooks.fwd_job(l, "sb")
    sb, job_out = sb_fwd(proj, name=n + "sb_fwd", job=job)
    if job is not None:
        hooks.done(job, job_out)
    sg = sgu_fwd(proj, sm["sgu_ln_g"], sm["sgu_ln_b"], sm["sgu_w"], sm["sgu_bias"], name=n + "sgu_fwd")
    merged, r1, r2, r3 = merge_fwd(retg, sb, sg, W["p_ret"], W["p_sb"], W["p_sgu"], proj, name=n + "merge_fwd")
    x1, xh1, rs1, x1h = matmul_ln(merged, W["w_out"], x0, sm["ln1_g"], sm["ln1_b"], tk=1024, name=n + "out_ln1")
    job = hooks.fwd_job(l, "up")
    h1 = matmul(x1h, W["w_up"], mode="nn", tm=1024, tn=1024, tk=1024, outs=((BF16, None),), name=n + "up", job=job)
    if job is not None:
        h1, job_out = h1
        hooks.done(job, job_out)
    job = hooks.fwd_job(l, "down")
    res = matmul_ln(h1, W["w_down"], x1, sm["ln2_g"], sm["ln2_b"], pro=_relu2, tk=1024, name=n + "down_ln2", job=job)
    if job is not None:
        res, job_out = res
        hooks.done(job, job_out)
    x2, xh2, rs2, x2h = res
    saved = dict(x0h=x0h, proj=proj, retg=retg, raw=raw, states=states, sb=sb, sg=sg, merged=merged, r=(r1, r2, r3),
                 x1h=x1h, xh1=xh1, rs1=rs1, h1=h1, xh2=xh2, rs2=rs2)
    return x2, x2h, saved


def layer_backward(l, dx2, s, W, sm, rope, rconsts, hooks):
    n = f"l{l}_"
    two = ((F32, None), (BF16, None))
    gw, gs = {}, {}
    job = hooks.bwd_job(l, "ln2")
    res = ln_bwd(dx2, s["xh2"], s["rs2"], sm["ln2_g"], name=n + "ln2_bwd", job=job)
    if job is not None:
        res, job_out = res
        hooks.done(job, job_out)
    du2, du2h, gs["ln2_g"], gs["ln2_b"] = res
    job = hooks.bwd_job(l, "g_down")
    gw["w_down"] = matmul(s["h1"], du2h, mode="tn", tm=1024, tn=1024, tk=4096, pro=_relu2, outs=two, name=n + "g_down", job=job)
    if job is not None:
        gw["w_down"], job_out = gw["w_down"]
        hooks.done(job, job_out)
    dh1 = matmul(du2h, W["w_down"], mode="nt", tm=1024, tn=1024, tk=1024, outs=((BF16, None),),
                 epi=lambda acc, h: (acc * (2.0 * jnp.maximum(h.astype(F32), 0.0)),), tiles=(s["h1"],), name=n + "d_h1")
    job = hooks.bwd_job(l, "g_up")
    gw["w_up"] = matmul(s["x1h"], dh1, mode="tn", tm=1024, tn=1024, tk=4096, outs=two, name=n + "g_up", job=job)
    if job is not None:
        gw["w_up"], job_out = gw["w_up"]
        hooks.done(job, job_out)
    dx1 = matmul(dh1, W["w_up"], mode="nt", tm=1024, tn=1024, tk=2048,
                 epi=lambda acc, d: (acc + ALPHA * d,), tiles=(du2,), name=n + "d_x1")
    du1, du1h, gs["ln1_g"], gs["ln1_b"] = ln_bwd(dx1, s["xh1"], s["rs1"], sm["ln1_g"], name=n + "ln1_bwd")
    gw["w_out"] = matmul(s["merged"], du1h, mode="tn", tm=1024, tn=1024, tk=4096, outs=two, name=n + "g_out")
    gate0 = C_GATE // 512
    dr1, dr2, dr3, dg1, dg2, dg3 = matmul(
        du1h, W["w_out"], mode="nt", tm=1024, tn=512, tk=1024, outs=((BF16, None),) * 6, epi=_merge_bwd_epi,
        tiles=(*s["r"], (s["proj"], gate0), (s["proj"], gate0 + 2), (s["proj"], gate0 + 4)), name=n + "d_merged")
    d_branch = {}
    for nm, a, dr in (("p_ret", s["retg"], dr1), ("p_sb", s["sb"], dr2), ("p_sgu", s["sg"], dr3)):
        gw[nm] = matmul(a, dr, mode="tn", tm=512, tn=1024, tk=4096, outs=two, name=n + "g_" + nm)
        d_branch[nm] = matmul(dr, W[nm], mode="nt", tm=1024, tn=512, tk=1024, name=n + "d_" + nm)
    job = hooks.pair(l, gw)
    dret, gs["ret_gn_g"], gs["ret_gn_b"], job_out = ret_bwd(s["proj"], *rope, rconsts, sm["ret_gn_g"], sm["ret_gn_b"],
                                                             s["raw"], s["states"], d_branch["p_ret"], name=n + "ret_bwd", job=job)
    if job is not None:
        hooks.done(job, job_out)
    job = hooks.scatter(l) if job is not None else None
    dsq, dsk, dsv, job_out = sb_bwd(s["proj"], s["sb"], d_branch["p_sb"], name=n + "sb_bwd", job=job)
    if job is not None:
        hooks.done(job, job_out)
    dsgu, gs["sgu_w"], dbias, gs["sgu_ln_g"], gs["sgu_ln_b"] = sgu_bwd(
        s["proj"], sm["sgu_ln_g"], sm["sgu_ln_b"], sm["sgu_w"], sm["sgu_bias"], d_branch["p_sgu"], name=n + "sgu_bwd")
    gs["sgu_b"] = dbias[:, :, 0]
    dproj = jnp.concatenate([dret, dsq, dsk, dsv, dsgu, dg1, dg2, dg3], axis=1)
    job = hooks.small(l, gs)
    gw["w_in"] = matmul(s["x0h"], dproj, mode="tn", tm=1024, tn=1536, tk=2048, outs=two, name=n + "g_in", job=job)
    if job is not None:
        gw["w_in"], job_out = gw["w_in"]
        hooks.done(job, job_out)
    job = hooks.tail(l, gw["w_in"])
    dx0 = matmul(dproj, W["w_in"], mode="nt", tm=1024, tn=1024, tk=2560,
                 epi=lambda acc, d: (acc + ALPHA * d,), tiles=(du1,), name=n + "d_x0", job=job)
    if job is not None:
        dx0, job_out = dx0
        hooks.done(job, job_out)
    return dx0, gw, gs


def local_step(x, target, small, plan):
    T = x.shape[0]
    rope = _rope_tables(T)
    rconsts = _ret_consts()
    sms = []
    for l in range(DEPTH):
        sm = {k: small[k][l][None, :] for k in SMALL if k not in ("sgu_w", "sgu_b")}
        sm["sgu_w"] = small["sgu_w"][l]
        sm["sgu_bias"] = jnp.broadcast_to(small["sgu_b"][l][:, :, None], (4, CHUNK, CHUNK))
        sms.append(sm)
    h, saved = x, []
    job = plan.first_job()
    hh = _rows_call(lambda a: (a,), [x], [BF16], name="cast_x", job=job)
    if job is not None:
        hh, job_out = hh
        plan.done(job, job_out)
    hh = hh[0]
    for l in range(DEPTH):
        h, hh, s = layer_forward(l, h, hh, plan.weights(l), sms[l], rope, rconsts, plan)
        saved.append(s)
    dy, sq = loss_head(h, target)
    gs = {k: [None] * DEPTH for k in SMALL}
    for l in reversed(range(DEPTH)):
        dy, gwl, gsl = layer_backward(l, dy, saved[l], plan.weights(l), sms[l], rope, rconsts, plan)
        plan.grads(l, gwl)
        for k in SMALL:
            gs[k][l] = gsl[k].reshape(small[k].shape[1:])
    return sq[0, 0], dy, {k: jnp.stack(v) for k, v in gs.items()}


EARLY_GRADS = ("p_ret", "p_sb", "p_sgu", "w_out", "w_up", "w_down")


class _StepPlan:
    def __init__(self, pos, shards16):
        self.pos = pos
        self.shards16 = shards16
        self.full = [dict() for _ in range(DEPTH)]
        self.gw = [None] * DEPTH
        self.bufs = {}
        self.sums = {}
        self.gs = [None] * DEPTH

    def first_job(self):
        return self._gather([(0, "w_in")])

    def weights(self, l):
        return self.full[l]

    def grads(self, l, gw):
        self.gw[l] = gw

    def _gather(self, items, chips=(0, 1, 2)):
        job = gather_job([self.shards16[l][k] for l, k in items], [BIG_AXIS[k] for _, k in items], chips)
        job.note = ("gather" if 2 in chips else "gather_part", items)
        return job

    def _pair(self, items):
        job = pair_job([g[1] for _, _, g in items], [BIG_AXIS[k] for _, k, _ in items])
        job.note = ("pair", items)
        return job

    def fwd_job(self, l, host):
        if host == "proj":
            return None
        if host == "sb":
            return self._gather([(l, k) for k in BIG[1:]])
        if l + 1 == DEPTH:
            return None
        return self._gather([(l + 1, "w_in")], (0, 1) if host == "up" else (2,))

    def bwd_job(self, l, host):
        if l + 1 == DEPTH:
            return None
        if host == "ln2":
            job = self._pair([(l + 1, "w_in", self.gw[l + 1]["w_in"])])
            job.note = ("pair_w_in", job.note[1])
            return job
        items, sums16 = self.summed_w_in
        job = scatter_job([l_ for l_, _, _ in items], sums16, [BIG_AXIS[k] for _, k, _ in items],
                          [self.bufs.get(k) for _, k, _ in items], (0, 1) if host == "g_down" else (2,))
        job.note = ("scatter", items)
        return job

    def pair(self, l, ready):
        return self._pair([(l, k, ready[k]) for k in EARLY_GRADS])

    def scatter(self, l):
        items, sums16 = self.summed
        job = scatter_job([l_ for l_, _, _ in items], sums16, [BIG_AXIS[k] for _, k, _ in items],
                          [self.bufs.get(k) for _, k, _ in items])
        job.note = ("scatter", items)
        return job

    def small(self, l, gs):
        self.gs[l] = {k: gs[k].reshape(-1) for k in SMALL}
        if l != 0:
            return None
        job = small_job(_pack_small({k: jnp.stack([self.gs[l_][k] for l_ in range(DEPTH)]) for k in SMALL}))
        job.note = ("small", [])
        return job

    def tail(self, l, g):
        if l != 0:
            return None
        last = self._pair([(0, "w_in", g)])
        self.done(last, run_job(last, name="pair_last"))
        return self.scatter(0)

    def done(self, job, outs):
        kind, items = job.note
        if kind == "small":
            self.small_slots = outs[0]
        if kind in ("pair", "pair_w_in"):
            sums16 = []
            for a, (l, k, g) in enumerate(items):
                self.sums[(l, k)], s16 = pair_sum(self.pos, outs[a], g[0], BIG_AXIS[k], name=f"pair_sum_{k}_{l}")
                sums16.append(s16)
            if kind == "pair":
                self.summed = (items, sums16)
            else:
                self.summed_w_in = (items, sums16)
        for a, item in enumerate(items):
            if kind == "gather_part":
                self.shards16[item[0]][item[1]] = outs[a]
            elif kind == "gather":
                self.full[item[0]][item[1]] = outs[a]
            elif kind == "scatter":
                self.bufs[item[1]] = outs[a]

    def finish(self):
        return self.bufs, self.sums


def _flat2(a):
    return a.reshape(-1, a.shape[-1])


def _pack_small(d, pre=""):
    return jnp.concatenate([d[pre + k].reshape(-1) for k in SMALL]).reshape(-1, 128)


def kernel(x, w_in, ret_gn_g, ret_gn_b, sgu_ln_g, sgu_ln_b, sgu_w, sgu_b, p_ret, p_sb, p_sgu, w_out, ln1_g, ln1_b, w_up, w_down, ln2_g, ln2_b, loss_target, m_w_in, m_ret_gn_g, m_ret_gn_b, m_sgu_ln_g, m_sgu_ln_b, m_sgu_w, m_sgu_b, m_p_ret, m_p_sb, m_p_sgu, m_w_out, m_ln1_g, m_ln1_b, m_w_up, m_w_down, m_ln2_g, m_ln2_b, v_w_in, v_ret_gn_g, v_ret_gn_b, v_sgu_ln_g, v_sgu_ln_b, v_sgu_w, v_sgu_b, v_p_ret, v_p_sb, v_p_sgu, v_w_out, v_ln1_g, v_ln1_b, v_w_up, v_w_down, v_ln2_g, v_ln2_b):
    given = dict(locals())
    order = BIG[:1] + SMALL[:6] + BIG[1:5] + SMALL[6:8] + BIG[5:7] + SMALL[8:10]
    L = DEPTH

    px, py, pc = _place()
    pos = jnp.stack([px, py, pc, 2 * px + py]).astype(jnp.int32)

    shards16 = [{k: cast_into_whole(pos, given[k], l, BIG_AXIS[k], name=f"cast_{k}_{l}") for k in BIG} for l in range(L)]
    plan = _StepPlan(pos, shards16)
    sq, dx, _ = local_step(x[0], loss_target[0], {k: given[k] for k in SMALL}, plan)
    loss = 0.5 * lax.psum(sq, ("x", "y", "c"))

    bufs, sums = plan.finish()
    shards = []
    for k in BIG:
        whole = None
        for l in range(L):
            whole = chip_sum(pos, sums[(l, k)], bufs[k], l, BIG_AXIS[k], whole, name=f"chip_sum_{k}_{l}")
        shards.append(whole)
    joined = run_job(join_job(shards), name="join_halves")
    out = {}
    for a, k in enumerate(BIG):
        shp = given[k].shape
        res = _rows_call(lambda g_, w_, m_, v_: (g_,) + _adamw(w_, g_, m_, v_),
                         [joined[a].reshape(-1, shp[-1]), _flat2(given[k]), _flat2(given["m_" + k]), _flat2(given["v_" + k])],
                         [F32] * 4, name="adamw_" + k)
        out[k] = [r.reshape(shp) for r in res]

    pack = _pack_small
    res = _rows_call(lambda g_, w_, m_, v_: (g_,) + _adamw(w_, g_, m_, v_),
                     [small_sum(plan.small_slots), pack(given), pack(given, "m_"), pack(given, "v_")], [F32] * 4,
                     name="adamw_small", tr=8 * 47)
    off = 0
    for k in SMALL:
        sz = given[k].size
        out[k] = [r.reshape(-1)[off:off + sz].reshape(given[k].shape) for r in res]
        off += sz

    grads = [out[k][0] for k in order]
    deltas = [out[k][1] for k in order]
    new_m = [out[k][2] for k in order]
    new_v = [out[k][3] for k in order]
    return (loss, dx[None], *grads, *deltas, *new_m, *new_v)
```

```python
import functools
import math

import jax
import jax.numpy as jnp
from jax import lax
from jax.experimental import pallas as pl
from jax.experimental.pallas import tpu as pltpu

F32 = jnp.float32
BF16 = jnp.bfloat16

D_MODEL = 1024
SEQ = 4096
DEPTH = 2
CHUNK = 128
RET_HEADS = 4
BRANCH_W = 512
N_IN = 7680
D_FF = 4096
LN_EPS = 1e-5
ROPE_BASE = 10000.0
ALPHA = (2 * DEPTH) ** 0.25
RET_SCALE = 128 ** -0.5
SB_SCALE = 64 ** -0.5
C_RET, C_SB, C_SGU, C_GATE = 0, 2048, 3584, 4608

ADAM_LR, ADAM_B1, ADAM_B2, ADAM_EPS, ADAM_WD, ADAM_STEP = 0.001, 0.9, 0.999, 1e-08, 0.01, 10

N_CHIPS = 4
VMEM_LIMIT = 56 * 1024 * 1024
MESH = pl.DeviceIdType.MESH

NN = ((1,), (0,))
NT = ((1,), (1,))
TN = ((0,), (0,))


def _dot(a, b, dims):
    return lax.dot_general(a, b, (dims, ((), ())), preferred_element_type=F32)


def _params(sem):
    return pltpu.CompilerParams(dimension_semantics=sem, vmem_limit_bytes=VMEM_LIMIT)


def _relu2(h):
    r = jnp.maximum(h.astype(F32), 0.0)
    return r * r


def matmul(a, b, *, mode, tm, tn, tk, outs=((F32, None),), pro=None, epi=None, tiles=(), rows=(), name, job=None):
    if mode == "nn":
        (M, K), N = a.shape, b.shape[1]
    elif mode == "nt":
        (M, K), N = a.shape, b.shape[0]
    else:
        (K, M), N = a.shape, b.shape[1]
    tm, tn, tk = min(tm, M), min(tn, N), min(tk, K)
    assert M % tm == 0 and N % tn == 0 and K % tk == 0, (name, M, N, K, tm, tn, tk)
    if mode == "nn":
        a_spec = pl.BlockSpec((tm, tk), lambda i, j, k: (i, k))
        b_spec = pl.BlockSpec((tk, tn), lambda i, j, k: (k, j))
        dims = NN
    elif mode == "nt":
        a_spec = pl.BlockSpec((tm, tk), lambda i, j, k: (i, k))
        b_spec = pl.BlockSpec((tn, tk), lambda i, j, k: (j, k))
        dims = NT
    else:
        a_spec = pl.BlockSpec((tk, tm), lambda i, j, k: (k, i))
        b_spec = pl.BlockSpec((tk, tn), lambda i, j, k: (k, j))
        dims = TN
    nk = K // tk
    nt_, nr, no = len(tiles), len(rows), len(outs)

    def body(a_ref, b_ref, *rest):
        tile_refs = rest[:nt_]
        row_refs = rest[nt_:nt_ + nr]
        out_refs = rest[nt_ + nr:nt_ + nr + no]
        av = a_ref[...]
        if pro is not None:
            av = pro(av)
        p = _dot(av.astype(BF16), b_ref[...].astype(BF16), dims)

        def finish(acc):
            vals = (acc,) * no if epi is None else epi(acc, *[r[...] for r in tile_refs], *[r[...] for r in row_refs])
            for o_ref, v in zip(out_refs, vals):
                o_ref[...] = v.astype(o_ref.dtype)

        if nk == 1:
            finish(p)
        else:
            acc_ref = rest[-1]
            k = pl.program_id(2)

            @pl.when(k == 0)
            def _():
                acc_ref[...] = p

            @pl.when(k > 0)
            def _():
                acc_ref[...] += p

            @pl.when(k == nk - 1)
            def _():
                finish(acc_ref[...])

    out_shape, out_specs = [], []
    for dt, width in outs:
        if width is None:
            out_shape.append(jax.ShapeDtypeStruct((M, N), dt))
            out_specs.append(pl.BlockSpec((tm, tn), lambda i, j, k: (i, j)))
        else:
            assert N == tn
            out_shape.append(jax.ShapeDtypeStruct((M, width), dt))
            out_specs.append(pl.BlockSpec((tm, width), lambda i, j, k: (i, 0)))
    in_specs = [a_spec, b_spec]
    offs = [t[1] if isinstance(t, tuple) else 0 for t in tiles]
    tiles = [t[0] if isinstance(t, tuple) else t for t in tiles]
    in_specs += [pl.BlockSpec((tm, tn), functools.partial(lambda i, j, k, o: (i, j + o), o=o)) for o in offs]
    in_specs += [pl.BlockSpec((1, tn), lambda i, j, k: (0, j)) for _ in rows]
    grid = (M // tm, N // tn, nk)
    scratch = [pltpu.VMEM((tm, tn), F32)] if nk > 1 else []
    j = _job_args(job, len(in_specs), no)
    res = pl.pallas_call(
        _hosting(body, job, len(in_specs), no, len(scratch), grid), name=name, grid=grid,
        in_specs=in_specs + j["in_specs"], out_specs=out_specs + j["out_specs"], out_shape=out_shape + j["out_shape"],
        scratch_shapes=scratch + j["scratch"], input_output_aliases=j["aliases"],
        compiler_params=_params(("parallel", "parallel", "arbitrary") if job is None else ("arbitrary",) * 3),
    )(a, b, *tiles, *rows, *j["ins"])
    mine = res[0] if no == 1 else list(res[:no])
    return mine if job is None else (mine, list(res[no:]))


def _ln_epi(acc, res, g, b):
    u = ALPHA * res + acc
    mu = jnp.mean(u, axis=-1, keepdims=True)
    xc = u - mu
    var = jnp.mean(xc * xc, axis=-1, keepdims=True)
    rstd = lax.rsqrt(var + LN_EPS)
    xhat = xc * rstd
    y = xhat * g + b
    return y, xhat, jnp.broadcast_to(rstd, (u.shape[0], 128)), y


def matmul_ln(a, w, res, g, b, *, pro=None, tk, name, job=None):
    n = w.shape[1]
    return matmul(a, w, mode="nn", tm=1024, tn=n, tk=tk, pro=pro, epi=_ln_epi, tiles=(res,), rows=(g, b),
                  outs=((F32, None), (F32, None), (F32, 128), (BF16, None)), name=name, job=job)


def ln_bwd(dy, xhat, rstd, g, *, name, job=None):
    T, D = dy.shape
    tm = min(512, T)

    def body(dy_ref, xh_ref, rs_ref, g_ref, du_ref, du16_ref, dg_ref, db_ref):
        dyv, xh = dy_ref[...], xh_ref[...]
        r = rs_ref[:, 0:1]
        dxh = dyv * g_ref[...]
        m1 = jnp.mean(dxh, axis=-1, keepdims=True)
        m2 = jnp.mean(dxh * xh, axis=-1, keepdims=True)
        du = r * (dxh - m1 - xh * m2)
        du_ref[...] = du
        du16_ref[...] = du.astype(BF16)

        @pl.when(pl.program_id(0) == 0)
        def _():
            dg_ref[...] = jnp.zeros_like(dg_ref)
            db_ref[...] = jnp.zeros_like(db_ref)

        dg_ref[...] += jnp.sum(dyv * xh, axis=0, keepdims=True)
        db_ref[...] += jnp.sum(dyv, axis=0, keepdims=True)

    row = pl.BlockSpec((tm, D), lambda i: (i, 0))
    vec = pl.BlockSpec((1, D), lambda i: (0, 0))
    j = _job_args(job, 4, 4)
    res = pl.pallas_call(
        _hosting(body, job, 4, 4, 0, T // tm), name=name, grid=(T // tm,),
        in_specs=[row, row, pl.BlockSpec((tm, 128), lambda i: (i, 0)), vec] + j["in_specs"],
        out_specs=[row, row, vec, vec] + j["out_specs"],
        out_shape=[jax.ShapeDtypeStruct((T, D), F32), jax.ShapeDtypeStruct((T, D), BF16),
                   jax.ShapeDtypeStruct((1, D), F32), jax.ShapeDtypeStruct((1, D), F32)] + j["out_shape"],
        scratch_shapes=j["scratch"], input_output_aliases=j["aliases"],
        compiler_params=_params(("arbitrary",)),
    )(dy, xhat, rstd, g, *j["ins"])
    return list(res[:4]) if job is None else (list(res[:4]), list(res[4:]))


def loss_head(y, target):
    T, D = y.shape
    tm = min(512, T)

    def body(y_ref, t_ref, dy_ref, s_ref):
        e = y_ref[...] - t_ref[...]
        dy_ref[...] = e * (1.0 / D)

        @pl.when(pl.program_id(0) == 0)
        def _():
            s_ref[...] = jnp.zeros_like(s_ref)

        s_ref[...] += jnp.sum(jnp.mean(e * e, axis=-1, keepdims=True))

    row = pl.BlockSpec((tm, D), lambda i: (i, 0))
    return pl.pallas_call(
        body, name="loss_head", grid=(T // tm,),
        in_specs=[row, row], out_specs=[row, pl.BlockSpec((8, 128), lambda i: (0, 0))],
        out_shape=[jax.ShapeDtypeStruct((T, D), F32), jax.ShapeDtypeStruct((8, 128), F32)],
        compiler_params=_params(("arbitrary",)),
    )(y, target)


def _rope_tables(T):
    half = 64
    inv_freq = ROPE_BASE ** (-jnp.arange(half, dtype=F32) / half)
    ang = jnp.arange(T, dtype=jnp.int32).astype(F32)[:, None] * inv_freq[None, :]
    cos, sin = jnp.cos(ang), jnp.sin(ang)
    return jnp.concatenate([cos, cos], axis=1), jnp.concatenate([-sin, sin], axis=1)


def _ret_consts():
    H = RET_HEADS
    log_g = jnp.log(1.0 - 2.0 ** (-5.0 - jnp.arange(H, dtype=F32)))
    idx = jnp.arange(CHUNK, dtype=F32)
    diff = idx[:, None] - idx[None, :]
    dmat = jnp.where(diff[None] >= 0, jnp.exp(log_g[:, None, None] * diff[None]), 0.0)
    kd = jnp.exp(log_g[:, None] * (CHUNK - 1 - idx)[None, :])
    qd = jnp.exp(log_g[:, None] * (idx + 1.0)[None, :])
    cd = jnp.exp(log_g * CHUNK)
    full = (H, CHUNK, CHUNK)
    return (dmat.astype(F32), jnp.broadcast_to(kd[:, :, None], full), jnp.broadcast_to(qd[:, :, None], full),
            jnp.broadcast_to(cd[:, None, None], full))


def _swap_halves(v):
    return pltpu.roll(v, 64, 1)


def _group_norm(o):
    mu = jnp.mean(o, axis=-1, keepdims=True)
    xc = o - mu
    var = jnp.mean(xc * xc, axis=-1, keepdims=True)
    rstd = lax.rsqrt(var + LN_EPS)
    return xc * rstd, rstd


def ret_fwd(proj, cosf, sinf, consts, gn_g, gn_b, *, name):
    T = proj.shape[0]
    tb = min(512, T)
    nch = tb // CHUNK
    H = RET_HEADS

    def body(p_ref, cos_ref, sin_ref, dm_ref, kd_ref, qd_ref, cd_ref, g_ref, b_ref, out_ref, raw_ref, st_ref, s_ref):
        @pl.when(pl.program_id(0) == 0)
        def _():
            s_ref[...] = jnp.zeros_like(s_ref)

        for c in range(nch):
            r = slice(c * CHUNK, (c + 1) * CHUNK)
            cs, sn = cos_ref[r, :], sin_ref[r, :]
            for h in range(H):
                hc = slice(h * 128, (h + 1) * 128)
                q = p_ref[r, h * 128:(h + 1) * 128]
                k = p_ref[r, 512 + h * 128:512 + (h + 1) * 128]
                v = p_ref[r, 1024 + h * 128:1024 + (h + 1) * 128]
                gt = p_ref[r, 1536 + h * 128:1536 + (h + 1) * 128]
                qr = q * cs + _swap_halves(q) * sn
                kr = (k * cs + _swap_halves(k) * sn) * RET_SCALE
                sprev = s_ref[h]
                st_ref[c, h] = sprev
                qb, kb, vb = qr.astype(BF16), kr.astype(BF16), v.astype(BF16)
                s = _dot(qb, kb, NT) * dm_ref[h]
                o = _dot(s.astype(BF16), vb, NN) + _dot((qr * qd_ref[h]).astype(BF16), sprev.astype(BF16), NN)
                s_ref[h] = sprev * cd_ref[h] + _dot((kr * kd_ref[h]).astype(BF16), vb, TN)
                raw_ref[r, hc] = o
                y, _ = _group_norm(o)
                out_ref[r, hc] = (gt * jax.nn.sigmoid(gt)) * (y * g_ref[:, hc] + b_ref[:, hc])

    cmat = pl.BlockSpec((H, CHUNK, CHUNK), lambda i: (0, 0, 0))
    vec = pl.BlockSpec((1, BRANCH_W), lambda i: (0, 0))
    rope = pl.BlockSpec((tb, 128), lambda i: (i, 0))
    blk = pl.BlockSpec((tb, BRANCH_W), lambda i: (i, 0))
    return pl.pallas_call(
        body, name=name, grid=(T // tb,),
        in_specs=[pl.BlockSpec((tb, 2048), lambda i: (i, 0)), rope, rope, cmat, cmat, cmat, cmat, vec, vec],
        out_specs=[blk, blk, pl.BlockSpec((nch, H, CHUNK, CHUNK), lambda i: (i, 0, 0, 0))],
        out_shape=[jax.ShapeDtypeStruct((T, BRANCH_W), F32), jax.ShapeDtypeStruct((T, BRANCH_W), F32),
                   jax.ShapeDtypeStruct((T // CHUNK, H, CHUNK, CHUNK), F32)],
        scratch_shapes=[pltpu.VMEM((H, CHUNK, CHUNK), F32)],
        compiler_params=_params(("arbitrary",)),
    )(proj, cosf, sinf, *consts, gn_g, gn_b)


def ret_bwd(proj, cosf, sinf, consts, gn_g, gn_b, raw, states, dout, *, name, job=None):
    T = proj.shape[0]
    tb = min(512, T)
    nch = tb // CHUNK
    nb = T // tb
    H = RET_HEADS

    def body(p_ref, cos_ref, sin_ref, dm_ref, kd_ref, qd_ref, cd_ref, g_ref, b_ref, raw_ref, st_ref, do_ref,
             dp_ref, dg_ref, db_ref, ds_ref):
        @pl.when(pl.program_id(0) == 0)
        def _():
            ds_ref[...] = jnp.zeros_like(ds_ref)
            dg_ref[...] = jnp.zeros_like(dg_ref)
            db_ref[...] = jnp.zeros_like(db_ref)

        for c in reversed(range(nch)):
            r = slice(c * CHUNK, (c + 1) * CHUNK)
            cs, sn = cos_ref[r, :], sin_ref[r, :]
            for h in range(H):
                hc = slice(h * 128, (h + 1) * 128)
                q = p_ref[r, h * 128:(h + 1) * 128]
                k = p_ref[r, 512 + h * 128:512 + (h + 1) * 128]
                v = p_ref[r, 1024 + h * 128:1024 + (h + 1) * 128]
                gt = p_ref[r, 1536 + h * 128:1536 + (h + 1) * 128]
                qr = q * cs + _swap_halves(q) * sn
                kr = (k * cs + _swap_halves(k) * sn) * RET_SCALE
                sprev = st_ref[c, h]
                gv = g_ref[:, hc]
                y, rstd = _group_norm(raw_ref[r, hc])
                d_out = do_ref[r, hc]
                sg = jax.nn.sigmoid(gt)
                d_gate = d_out * (y * gv + b_ref[:, hc]) * (sg * (1.0 + gt * (1.0 - sg)))
                d_aff = d_out * (gt * sg)
                dg_ref[:, hc] += jnp.sum(d_aff * y, axis=0, keepdims=True)
                db_ref[:, hc] += jnp.sum(d_aff, axis=0, keepdims=True)
                dxh = d_aff * gv
                m1 = jnp.mean(dxh, axis=-1, keepdims=True)
                m2 = jnp.mean(dxh * y, axis=-1, keepdims=True)
                d_o = (rstd * (dxh - m1 - y * m2)).astype(BF16)
                qb, kb, vb = qr.astype(BF16), kr.astype(BF16), v.astype(BF16)
                dm, kd, qd = dm_ref[h], kd_ref[h], qd_ref[h]
                p = (_dot(qb, kb, NT) * dm).astype(BF16)
                dp = (_dot(d_o, vb, NT) * dm).astype(BF16)
                dsn = ds_ref[h]
                dsb = dsn.astype(BF16)
                dq_r = _dot(dp, kb, NN) + _dot(d_o, sprev.astype(BF16), NT) * qd
                dk_r = (_dot(dp, qb, TN) + _dot(vb, dsb, NT) * kd) * RET_SCALE
                d_v = _dot(p, d_o, TN) + _dot((kr * kd).astype(BF16), dsb, NN)
                ds_ref[h] = dsn * cd_ref[h] + _dot((qr * qd).astype(BF16), d_o, TN)
                dp_ref[r, h * 128:(h + 1) * 128] = (dq_r * cs - _swap_halves(dq_r) * sn).astype(BF16)
                dp_ref[r, 512 + h * 128:512 + (h + 1) * 128] = (dk_r * cs - _swap_halves(dk_r) * sn).astype(BF16)
                dp_ref[r, 1024 + h * 128:1024 + (h + 1) * 128] = d_v.astype(BF16)
                dp_ref[r, 1536 + h * 128:1536 + (h + 1) * 128] = d_gate.astype(BF16)

    cmat = pl.BlockSpec((H, CHUNK, CHUNK), lambda i: (0, 0, 0))
    vec = pl.BlockSpec((1, BRANCH_W), lambda i: (0, 0))
    rope = pl.BlockSpec((tb, 128), lambda i: (nb - 1 - i, 0))
    blk = pl.BlockSpec((tb, BRANCH_W), lambda i: (nb - 1 - i, 0))
    wide = pl.BlockSpec((tb, 2048), lambda i: (nb - 1 - i, 0))
    j = _job_args(job, 12, 3)
    res = pl.pallas_call(
        _hosting(body, job, 12, 3, 1, nb), name=name, grid=(nb,),
        in_specs=[wide, rope, rope, cmat, cmat, cmat, cmat, vec, vec, blk,
                  pl.BlockSpec((nch, H, CHUNK, CHUNK), lambda i: (nb - 1 - i, 0, 0, 0)), blk] + j["in_specs"],
        out_specs=[wide, vec, vec] + j["out_specs"],
        out_shape=[jax.ShapeDtypeStruct((T, 2048), BF16), jax.ShapeDtypeStruct((1, BRANCH_W), F32),
                   jax.ShapeDtypeStruct((1, BRANCH_W), F32)] + j["out_shape"],
        scratch_shapes=[pltpu.VMEM((H, CHUNK, CHUNK), F32)] + j["scratch"], input_output_aliases=j["aliases"],
        compiler_params=_params(("arbitrary",)),
    )(proj, cosf, sinf, *consts, gn_g, gn_b, raw, states, dout, *j["ins"])
    return res[0], res[1], res[2], list(res[3:])


def _sb_masks():
    row = lax.broadcasted_iota(jnp.int32, (CHUNK, CHUNK), 0)
    lane = lax.broadcasted_iota(jnp.int32, (CHUNK, CHUNK), 1)
    return row, lane


SB_QT = 256
SB_DEAD = -105.0


def _pair(v):
    hi = v.astype(BF16)
    return jnp.concatenate([hi, (v - hi.astype(F32)).astype(BF16)], axis=1)


def _sb_consts():
    r = lax.broadcasted_iota(jnp.int32, (256, 256), 0) & 127
    c = lax.broadcasted_iota(jnp.int32, (256, 256), 1)
    ones = c >= 128
    lane = lax.broadcasted_iota(jnp.int32, (CHUNK, CHUNK), 1)
    return (ones | (r > c)).astype(BF16), (ones | (r >= c)).astype(BF16), (lane < 64, lane >= 64)


def _per_head(x, hms):
    return jnp.concatenate([jnp.where(hm, x, 0.0) for hm in hms], axis=0).astype(BF16)


def _sb_logits(qb, kb2, mask2):
    z = _dot(qb, kb2, NT)
    l1p = jnp.log(1.0 + jnp.exp(-jnp.abs(z)))
    lsp = jnp.minimum(z, 0.0) - l1p
    lsn = lsp - z
    if mask2 is not None:
        lsn = jnp.where(mask2, lsn, 0.0)
    return lsp, lsn


def _sb_tile_mask(qt):
    trow = lax.broadcasted_iota(jnp.int32, (qt, 256), 0)
    tlane = lax.broadcasted_iota(jnp.int32, (qt, 256), 1) & 127
    return lambda m: (tlane + m * CHUNK) < trow


def sb_fwd(proj, *, name, job=None):
    T = proj.shape[0]
    qt = min(SB_QT, T)
    nsub = qt // CHUNK
    cb = C_SB // 128

    def body(q_ref, k_ref, v_ref, o_ref):
        u_gt, _, hms = _sb_consts()
        tile_mask = _sb_tile_mask(qt)

        def qtile(i, _):
            rq = pl.ds(pl.multiple_of(i * qt, qt), qt)
            qb = (q_ref[rq, :] * SB_SCALE).astype(BF16)

            def group(js, masks, state):
                carry, acc = list(state[:2]), state[2]
                rows = [pl.ds(pl.multiple_of(j * CHUNK, CHUNK), CHUNK) for j in js]
                logits = [_sb_logits(qb, _per_head(k_ref[rk, :], hms), m) for rk, m in zip(rows, masks)]
                sums = [[_dot(_pair(lsn[:, h * 128:(h + 1) * 128]), u_gt, NN) for h in range(2)] for _, lsn in logits]
                weights = []
                for (lsp, _), r, m in zip(logits, sums, masks):
                    a_b = []
                    for h in range(2):
                        hc = slice(h * 128, (h + 1) * 128)
                        a = jnp.exp(lsp[:, hc] + r[h][:, :128] + carry[h])
                        if m is not None:
                            a = jnp.where(m[:, hc], a, 0.0)
                        carry[h] = carry[h] + r[h][:, 128:]
                        a_b.append(a.astype(BF16))
                    weights.append(jnp.concatenate(a_b, axis=1))
                for rk, a in zip(rows, weights):
                    acc = acc + _dot(a, _per_head(v_ref[rk, :], hms), NN)
                return carry[0], carry[1], acc

            zero = jnp.zeros((qt, 128), F32)
            diag = list(reversed(range(nsub)))
            state = group([i * nsub + m for m in diag], [tile_mask(m) for m in diag], (zero, zero, zero))

            def live(c):
                return jnp.logical_and(c[0] < i, jnp.maximum(jnp.max(c[1][0]), jnp.max(c[1][1])) > SB_DEAD)

            def blocks(c):
                jj, st = c
                return jj + 1, group([(i - jj) * nsub - 1 - u for u in range(nsub)], [None] * nsub, st)

            _, state = lax.while_loop(live, blocks, (jnp.int32(0), state))
            o_ref[rq, :] = state[2]
            return 0

        lax.fori_loop(0, T // qt, qtile, 0)

    def col(off):
        return pl.BlockSpec((T, 128), lambda hp: (0, off + hp))

    steps = BRANCH_W // 128
    j = _job_args(job, 3, 1)
    res = pl.pallas_call(
        _hosting(body, job, 3, 1, 0, steps), name=name, grid=(steps,),
        in_specs=[col(cb), col(cb + 4), col(cb + 8)] + j["in_specs"], out_specs=[col(0)] + j["out_specs"],
        out_shape=[jax.ShapeDtypeStruct((T, BRANCH_W), F32)] + j["out_shape"],
        scratch_shapes=j["scratch"], input_output_aliases=j["aliases"],
        compiler_params=_params(("parallel",) if job is None else ("arbitrary",)),
    )(proj, proj, proj, *j["ins"])
    return res[0], list(res[1:])


def sb_bwd(proj, out, dout, *, name, job=None):
    T = proj.shape[0]
    qt = min(SB_QT, T)
    nsub = qt // CHUNK
    cb = C_SB // 128

    def body(q_ref, k_ref, v_ref, o_ref, do_ref, dq_ref, dk_ref, dv_ref, dkt_ref, dvt_ref):
        u_gt, u_ge, hms = _sb_consts()
        tile_mask = _sb_tile_mask(qt)
        tall_lane = lax.broadcasted_iota(jnp.int32, (qt, 128), 1)
        top = lax.broadcasted_iota(jnp.int32, (CHUNK, CHUNK), 0) < 64
        dkt_ref[...] = jnp.zeros_like(dkt_ref)
        dvt_ref[...] = jnp.zeros_like(dvt_ref)

        def qtile(i, _):
            rq = pl.ds(pl.multiple_of(i * qt, qt), qt)
            qs = q_ref[rq, :] * SB_SCALE
            qb, q_t = qs.astype(BF16), qs.T.astype(BF16)
            dov = do_ref[rq, :]
            dob, do_t = dov.astype(BF16), dov.T.astype(BF16)
            prod = dob.astype(F32) * o_ref[rq, :]
            total = [jnp.broadcast_to(jnp.sum(jnp.where(hm, prod, 0.0), axis=1, keepdims=True), (qt, 128))
                     for hm in (tall_lane < 64, tall_lane >= 64)]

            def group(js, masks, state):
                c_l, c_w, dq = list(state[:2]), list(state[2:4]), state[4]
                heads = [slice(h * 128, (h + 1) * 128) for h in range(2)]
                rows = [pl.ds(pl.multiple_of(j * CHUNK, CHUNK), CHUNK) for j in js]
                kb2 = [_per_head(k_ref[rk, :], hms) for rk in rows]
                logits = [_sb_logits(qb, kb, m) for kb, m in zip(kb2, masks)]
                da = [_dot(dob, _per_head(v_ref[rk, :], hms), NT) for rk in rows]
                sums = [[_dot(_pair(lsn[:, hc]), u_gt, NN) for hc in heads] for _, lsn in logits]
                a_b, w_all = [], []
                for (lsp, _), r, d, m in zip(logits, sums, da, masks):
                    a_h, w_h = [], []
                    for h, hc in enumerate(heads):
                        a = jnp.exp(lsp[:, hc] + r[h][:, :128] + c_l[h])
                        if m is not None:
                            a = jnp.where(m[:, hc], a, 0.0)
                        c_l[h] = c_l[h] + r[h][:, 128:]
                        a = a.astype(BF16)
                        a_h.append(a)
                        w_h.append(a.astype(F32) * d[:, hc])
                    a_b.append(jnp.concatenate(a_h, axis=1))
                    w_all.append(w_h)
                sums_w = [[_dot(_pair(w), u_ge, NN) for w in w_h] for w_h in w_all]
                dz_b = []
                for (lsp, _), w_h, r, m in zip(logits, w_all, sums_w, masks):
                    sp = jnp.exp(lsp)
                    dz_h = []
                    for h, hc in enumerate(heads):
                        later_w = r[h][:, :128] + c_w[h]
                        c_w[h] = c_w[h] + r[h][:, 128:]
                        dz = w_h[h] * (1.0 - sp[:, hc]) - sp[:, hc] * (total[h] - later_w)
                        if m is not None:
                            dz = jnp.where(m[:, hc], dz, 0.0)
                        dz_h.append(dz.astype(BF16))
                    dz_b.append(jnp.concatenate(dz_h, axis=1))
                for j, kb, a, dz in zip(js, kb2, a_b, dz_b):
                    dkt = _dot(q_t, dz, NN)
                    dvt = _dot(do_t, a, NN)
                    dkt_ref[j] += jnp.where(top, dkt[:, :128], dkt[:, 128:])
                    dvt_ref[j] += jnp.where(top, dvt[:, :128], dvt[:, 128:])
                    dq = dq + _dot(dz, kb, NN)
                return c_l[0], c_l[1], c_w[0], c_w[1], dq

            zero = jnp.zeros((qt, 128), F32)
            diag = list(reversed(range(nsub)))
            state = group([i * nsub + m for m in diag], [tile_mask(m) for m in diag], (zero,) * 5)

            def live(c):
                return jnp.logical_and(c[0] < i, jnp.maximum(jnp.max(c[1][0]), jnp.max(c[1][1])) > SB_DEAD)

            def blocks(c):
                jj, st = c
                return jj + 1, group([(i - jj) * nsub - 1 - u for u in range(nsub)], [None] * nsub, st)

            _, state = lax.while_loop(live, blocks, (jnp.int32(0), state))
            dq_ref[rq, :] = (state[4] * SB_SCALE).astype(BF16)
            return 0

        lax.fori_loop(0, T // qt, qtile, 0)

        def untranspose(jb, _):
            rk = pl.ds(pl.multiple_of(jb * CHUNK, CHUNK), CHUNK)
            dk_ref[rk, :] = dkt_ref[jb].T.astype(BF16)
            dv_ref[rk, :] = dvt_ref[jb].T.astype(BF16)
            return 0

        lax.fori_loop(0, T // CHUNK, untranspose, 0)

    def col(off):
        return pl.BlockSpec((T, 128), lambda hp: (0, off + hp))

    o16 = jax.ShapeDtypeStruct((T, BRANCH_W), BF16)
    steps = BRANCH_W // 128
    j = _job_args(job, 5, 3)
    acc = pltpu.VMEM((T // CHUNK, CHUNK, CHUNK), F32)
    res = pl.pallas_call(
        _hosting(body, job, 5, 3, 2, steps), name=name, grid=(steps,),
        in_specs=[col(cb), col(cb + 4), col(cb + 8), col(0), col(0)] + j["in_specs"],
        out_specs=[col(0), col(0), col(0)] + j["out_specs"], out_shape=[o16, o16, o16] + j["out_shape"],
        scratch_shapes=[acc, acc] + j["scratch"], input_output_aliases=j["aliases"],
        compiler_params=_params(("parallel",) if job is None else ("arbitrary",)),
    )(proj, proj, proj, out, dout, *j["ins"])
    return res[0], res[1], res[2], list(res[3:])


_G0 = math.sqrt(2.0 / math.pi)
_G1 = 0.044715


def _gelu(x):
    return 0.5 * x * (1.0 + jnp.tanh(_G0 * (x + _G1 * x * x * x)))


def _gelu_grad(x):
    t = jnp.tanh(_G0 * (x + _G1 * x * x * x))
    return 0.5 * (1.0 + t) + 0.5 * x * (1.0 - t * t) * (_G0 * (1.0 + 3.0 * _G1 * x * x))


def _tril():
    row, lane = _sb_masks()
    return row >= lane


def sgu_fwd(proj, ln_g, ln_b, w, bias, *, name):
    T = proj.shape[0]
    tb = min(512, T)
    G = BRANCH_W // 128

    def body(u_ref, v_ref, g_ref, b_ref, w_ref, bias_ref, o_ref):
        vv = _gelu(v_ref[...])
        xh, _ = _group_norm(vv)
        vn = (xh * g_ref[...] + b_ref[...]).astype(BF16)
        tril = _tril()
        for g in range(G):
            wg = jnp.where(tril, w_ref[g], 0.0).astype(BF16)
            gc = slice(g * 128, (g + 1) * 128)
            for c in range(tb // CHUNK):
                r = slice(c * CHUNK, (c + 1) * CHUNK)
                sv = _dot(wg, vn[r, gc], NN) + bias_ref[g]
                o_ref[r, gc] = _gelu(u_ref[r, gc]) * sv

    cu, cv = C_SGU // BRANCH_W, C_SGU // BRANCH_W + 1
    vec = pl.BlockSpec((1, BRANCH_W), lambda i: (0, 0))
    mat = pl.BlockSpec((G, CHUNK, CHUNK), lambda i: (0, 0, 0))
    return pl.pallas_call(
        body, name=name, grid=(T // tb,),
        in_specs=[pl.BlockSpec((tb, BRANCH_W), lambda i: (i, cu)), pl.BlockSpec((tb, BRANCH_W), lambda i: (i, cv)),
                  vec, vec, mat, mat],
        out_specs=pl.BlockSpec((tb, BRANCH_W), lambda i: (i, 0)),
        out_shape=jax.ShapeDtypeStruct((T, BRANCH_W), F32),
        compiler_params=_params(("parallel",)),
    )(proj, proj, ln_g, ln_b, w, bias)


def sgu_bwd(proj, ln_g, ln_b, w, bias, dout, *, name):
    T = proj.shape[0]
    tb = min(512, T)
    G = BRANCH_W // 128

    def body(u_ref, v_ref, g_ref, b_ref, w_ref, bias_ref, do_ref, dp_ref, dw_ref, dbias_ref, dg_ref, db_ref, dvn_ref):
        @pl.when(pl.program_id(0) == 0)
        def _():
            dw_ref[...] = jnp.zeros_like(dw_ref)
            dbias_ref[...] = jnp.zeros_like(dbias_ref)
            dg_ref[...] = jnp.zeros_like(dg_ref)
            db_ref[...] = jnp.zeros_like(db_ref)

        gv = v_ref[...]
        vv = _gelu(gv)
        xh, rstd = _group_norm(vv)
        vn = (xh * g_ref[...] + b_ref[...]).astype(BF16)
        tril = _tril()
        for g in range(G):
            wg = jnp.where(tril, w_ref[g], 0.0).astype(BF16)
            gc = slice(g * 128, (g + 1) * 128)
            for c in range(tb // CHUNK):
                r = slice(c * CHUNK, (c + 1) * CHUNK)
                vn_c = vn[r, gc]
                sv = _dot(wg, vn_c, NN) + bias_ref[g]
                gu = u_ref[r, gc]
                d_o = do_ref[r, gc]
                dp_ref[r, gc] = (d_o * sv * _gelu_grad(gu)).astype(BF16)
                dsv = d_o * _gelu(gu)
                dsv_b = dsv.astype(BF16)
                dvn_ref[r, gc] = _dot(wg, dsv_b, TN)
                dw_ref[g] += jnp.where(tril, _dot(dsv_b, vn_c, NT), 0.0)
                dbias_ref[g] += jnp.broadcast_to(jnp.sum(dsv, axis=1, keepdims=True), (CHUNK, CHUNK))
        dvn = dvn_ref[...]
        dg_ref[...] += jnp.sum(dvn * xh, axis=0, keepdims=True)
        db_ref[...] += jnp.sum(dvn, axis=0, keepdims=True)
        dxh = dvn * g_ref[...]
        m1 = jnp.mean(dxh, axis=-1, keepdims=True)
        m2 = jnp.mean(dxh * xh, axis=-1, keepdims=True)
        dp_ref[:, BRANCH_W:2 * BRANCH_W] = (rstd * (dxh - m1 - xh * m2) * _gelu_grad(gv)).astype(BF16)

    cu, cv = C_SGU // BRANCH_W, C_SGU // BRANCH_W + 1
    vec = pl.BlockSpec((1, BRANCH_W), lambda i: (0, 0))
    mat = pl.BlockSpec((G, CHUNK, CHUNK), lambda i: (0, 0, 0))
    blk = pl.BlockSpec((tb, BRANCH_W), lambda i: (i, 0))
    msh = jax.ShapeDtypeStruct((G, CHUNK, CHUNK), F32)
    vsh = jax.ShapeDtypeStruct((1, BRANCH_W), F32)
    return pl.pallas_call(
        body, name=name, grid=(T // tb,),
        in_specs=[pl.BlockSpec((tb, BRANCH_W), lambda i: (i, cu)), pl.BlockSpec((tb, BRANCH_W), lambda i: (i, cv)),
                  vec, vec, mat, mat, blk],
        out_specs=[pl.BlockSpec((tb, 2 * BRANCH_W), lambda i: (i, 0)), mat, mat, vec, vec],
        out_shape=[jax.ShapeDtypeStruct((T, 2 * BRANCH_W), BF16), msh, msh, vsh, vsh],
        scratch_shapes=[pltpu.VMEM((tb, BRANCH_W), F32)],
        compiler_params=_params(("arbitrary",)),
    )(proj, proj, ln_g, ln_b, w, bias, dout)


def merge_fwd(a1, a2, a3, p1, p2, p3, proj, *, name):
    T = a1.shape[0]
    tm, tn = min(1024, T), 512
    gb = C_GATE // tn

    def body(a1_ref, a2_ref, a3_ref, p1_ref, p2_ref, p3_ref, g1_ref, g2_ref, g3_ref, m_ref, r1_ref, r2_ref, r3_ref):
        m = None
        for a_ref, p_ref, g_ref, r_ref in ((a1_ref, p1_ref, g1_ref, r1_ref), (a2_ref, p2_ref, g2_ref, r2_ref),
                                           (a3_ref, p3_ref, g3_ref, r3_ref)):
            r = _dot(a_ref[...].astype(BF16), p_ref[...], NN)
            r_ref[...] = r.astype(r_ref.dtype)
            t = jax.nn.sigmoid(g_ref[...]) * r
            m = t if m is None else m + t
        m_ref[...] = m.astype(m_ref.dtype)

    a_spec = pl.BlockSpec((tm, BRANCH_W), lambda i, j: (i, 0))
    p_spec = pl.BlockSpec((BRANCH_W, tn), lambda i, j: (0, j))
    o_spec = pl.BlockSpec((tm, tn), lambda i, j: (i, j))
    gates = [pl.BlockSpec((tm, tn), functools.partial(lambda i, j, o: (i, o + j), o=gb + 2 * n)) for n in range(3)]
    return pl.pallas_call(
        body, name=name, grid=(T // tm, D_MODEL // tn),
        in_specs=[a_spec, a_spec, a_spec, p_spec, p_spec, p_spec, *gates],
        out_specs=[o_spec] * 4, out_shape=[jax.ShapeDtypeStruct((T, D_MODEL), BF16)] * 4,
        compiler_params=_params(("parallel", "parallel")),
    )(a1, a2, a3, p1, p2, p3, proj, proj, proj)


def _merge_bwd_epi(dm, r1, r2, r3, g1, g2, g3):
    d_r, d_g = [], []
    for r, g in ((r1, g1), (r2, g2), (r3, g3)):
        s = jax.nn.sigmoid(g)
        d_r.append(dm * s)
        d_g.append(dm * r.astype(F32) * (s * (1.0 - s)))
    return (*d_r, *d_g)


def _rows_call(fn, ins, out_dtypes, *, name, tr=256, job=None):
    first = ins[0][0] if isinstance(ins[0], tuple) else ins[0]
    R, C = first.shape[-2:]
    tr = min(tr, R)
    assert R % tr == 0, (name, R, tr)
    arrs, specs = [], []
    for x in ins:
        if isinstance(x, tuple):
            arrs.append(x[0])
            specs.append(pl.BlockSpec((None, tr, C), functools.partial(lambda i, n: (n, i, 0), n=x[1])))
        else:
            arrs.append(x)
            specs.append(pl.BlockSpec((tr, C), lambda i: (i, 0)))
    ni = len(arrs)

    def body(*refs):
        vals = fn(*[r[...] for r in refs[:ni]])
        for o_ref, v in zip(refs[ni:], vals):
            o_ref[...] = v.astype(o_ref.dtype)

    no = len(out_dtypes)
    j = _job_args(job, ni, no)
    res = pl.pallas_call(
        _hosting(body, job, ni, no, 0, R // tr), name=name, grid=(R // tr,), in_specs=specs + j["in_specs"],
        out_specs=[pl.BlockSpec((tr, C), lambda i: (i, 0)) for _ in out_dtypes] + j["out_specs"],
        out_shape=[jax.ShapeDtypeStruct((R, C), dt) for dt in out_dtypes] + j["out_shape"],
        scratch_shapes=j["scratch"], input_output_aliases=j["aliases"],
        compiler_params=_params(("parallel",) if job is None else ("arbitrary",)),
    )(*arrs, *j["ins"])
    return list(res) if job is None else (list(res[:no]), list(res[no:]))


def _tile_rows(rows, cols):
    t = 256
    while t > 8 and (t * cols > 512 * 1024 or rows % t):
        t //= 2
    return t


def _rows_at(fn, pos, ins, outs, steps, *, name, aliases=None):
    read = [n for n, (_, s) in enumerate(ins) if s is not ANY]
    ni = len(ins)

    def body(pos_ref, *refs):
        vals = fn(*[refs[n][...] for n in read])
        for o_ref, v in zip(refs[ni:], vals):
            o_ref[...] = v.astype(o_ref.dtype)

    return pl.pallas_call(
        body, name=name,
        grid_spec=pltpu.PrefetchScalarGridSpec(num_scalar_prefetch=1, grid=(steps,), in_specs=[s for _, s in ins],
                                               out_specs=[s for _, s in outs]),
        out_shape=[sh for sh, _ in outs],
        input_output_aliases={1 + i: o for i, o in (aliases or {}).items()},
        compiler_params=_params(("parallel",)),
    )(pos, *[a for a, _ in ins])


def cast_into_whole(pos, w, l, axis, *, name):
    _, r, n = w.shape
    tr = _tile_rows(r, n)
    if axis == 1:
        shape, spec = (r, n * N_CHIPS), pl.BlockSpec((tr, n), lambda i, p: (i, p[3]))
    else:
        shape, spec = (r * N_CHIPS, n), pl.BlockSpec((tr, n), lambda i, p: (p[3] * (r // tr) + i, 0))
    return _rows_at(lambda a: (a,), pos, [(w, pl.BlockSpec((None, tr, n), lambda i, p: (l, i, 0)))],
                    [(jax.ShapeDtypeStruct(shape, BF16), spec)], r // tr, name=name)[0]


def pair_sum(pos, theirs, g32, axis, *, name):
    rows2, cols = theirs.shape
    h = rows2 // (N_CHIPS if axis == 0 else 1)
    tr = _tile_rows(h, cols)
    hb = h // tr
    if axis == 1:
        own = pl.BlockSpec((tr, cols), lambda i, p: (p[2] * hb + i, 0))
    else:
        own = pl.BlockSpec((tr, cols), lambda i, p: ((2 * (i // hb) + p[2]) * hb + i % hb, 0))
    row = pl.BlockSpec((tr, cols), lambda i, p: (i, 0))
    return _rows_at(lambda t, m: (m + t.astype(F32),) * 2, pos, [(theirs, row), (g32, own)],
                    [(jax.ShapeDtypeStruct((rows2, cols), F32), row), (jax.ShapeDtypeStruct((rows2, cols), BF16), row)],
                    rows2 // tr, name=name)


def chip_sum(pos, h32, recv, l, axis, whole, *, name):
    _, depth, h, n = recv.shape
    tr = _tile_rows(h, n)
    hb = h // tr
    if axis == 1:
        mine = pl.BlockSpec((tr, n), lambda i, p: (i, p[3]))
    else:
        mine = pl.BlockSpec((tr, n), lambda i, p: (p[3] * hb + i, 0))
    ins = [(h32, mine)] + [(recv, pl.BlockSpec((None, None, tr, n), functools.partial(lambda i, p, j: (j, l, i, 0), j=j)))
                           for j in range(3)]
    if whole is not None:
        ins.append((whole, ANY))
    return _rows_at(lambda o, a, b, c: (((o + a.astype(F32)) + b.astype(F32)) + c.astype(F32),), pos, ins,
                    [(jax.ShapeDtypeStruct((depth, 2, h, n), F32), pl.BlockSpec((None, None, tr, n), lambda i, p: (l, p[2], i, 0)))],
                    hb, name=name, aliases=None if whole is None else {4: 0})[0]


def _adamw(w, g, m, v):
    m2 = ADAM_B1 * m + (1.0 - ADAM_B1) * g
    v2 = ADAM_B2 * v + (1.0 - ADAM_B2) * (g * g)
    m_hat = m2 / (1.0 - ADAM_B1 ** ADAM_STEP)
    v_hat = v2 / (1.0 - ADAM_B2 ** ADAM_STEP)
    delta = -ADAM_LR * (m_hat / (jnp.sqrt(v_hat) + ADAM_EPS) + ADAM_WD * w)
    return delta, m2, v2


def _place():
    return lax.axis_index("x"), lax.axis_index("y"), lax.axis_index("c")


def _chip_peers(x, y, c):
    return [((1 - x, y, c), 2 * (1 - x) + y), ((x, 1 - y, c), 2 * x + 1 - y), ((1 - x, 1 - y, c), 2 * (1 - x) + 1 - y)]


def _shard_of(ref, axis, k, n):
    start = pl.multiple_of(k * n, 128)
    return ref.at[pl.ds(start, n), :] if axis == 0 else ref.at[:, pl.ds(start, n)]


ANY = pl.BlockSpec(memory_space=pl.ANY)


class CopyJob:
    def __init__(self, ins, out_shape, scratch, copies, aliases=None):
        self.ins, self.out_shape, self.scratch, self.copies = list(ins), list(out_shape), list(scratch), copies
        self.aliases = dict(aliases or {})

    def start(self, ins, outs, sems):
        local, remote, _, _ = self.copies(ins, outs, sems)
        for d in local + remote:
            d.start()

    def finish(self, ins, outs, sems):
        local, remote, arrivals, relays = self.copies(ins, outs, sems)
        for needs, sends, _ in relays:
            for d in needs:
                d.wait_recv()
            for d in sends:
                d.start()
        for d in arrivals + [d for _, _, arrives in relays for d in arrives]:
            d.wait_recv()
        for d in remote + [d for _, sends, _ in relays for d in sends]:
            d.wait_send()
        for d in local:
            d.wait()


def run_job(job, *, name):
    ni, no = len(job.ins), len(job.out_shape)

    def body(*refs):
        parts = refs[:ni], refs[ni:ni + no], refs[ni + no:]
        job.start(*parts)
        job.finish(*parts)

    return pl.pallas_call(
        body, name=name, in_specs=[ANY] * ni, out_specs=[ANY] * no, out_shape=job.out_shape,
        scratch_shapes=job.scratch, input_output_aliases=job.aliases,
    )(*job.ins)


def _job_args(job, n_in, n_out):
    if job is None:
        return dict(ins=[], in_specs=[], out_specs=[], out_shape=[], scratch=[], aliases={})
    return dict(ins=job.ins, in_specs=[ANY] * len(job.ins), out_specs=[ANY] * len(job.out_shape),
                out_shape=job.out_shape, scratch=job.scratch,
                aliases={n_in + i: n_out + o for i, o in job.aliases.items()})


def _hosting(body, job, n_in, n_out, n_scratch, grid):
    if job is None:
        return body
    ji, jo = len(job.ins), len(job.out_shape)
    grid = (grid,) if isinstance(grid, int) else tuple(grid)

    def at(ends):
        hit = None
        for ax, e in enumerate(ends):
            here = pl.program_id(ax) == e
            hit = here if hit is None else jnp.logical_and(hit, here)
        return hit

    def hosted(*refs):
        o = n_in + ji
        s = o + n_out + jo
        parts = refs[n_in:o], refs[o + n_out:s], refs[s + n_scratch:]

        @pl.when(at([0] * len(grid)))
        def _():
            job.start(*parts)

        body(*refs[:n_in], *refs[o:o + n_out], *refs[s:s + n_scratch])

        @pl.when(at([g - 1 for g in grid]))
        def _():
            job.finish(*parts)

    return hosted


def _job_sems(n_remote, n_local):
    return [pltpu.SemaphoreType.DMA((n_remote,)), pltpu.SemaphoreType.DMA((n_remote,)), pltpu.SemaphoreType.DMA((n_local,))]


def gather_job(shards, axes, chips=(0, 1, 2)):
    na = len(shards)

    def copies(ins, outs, sems):
        send, recv, _ = sems
        x, y, c = _place()
        k = 2 * x + y
        remote, relays = [], []
        for a in range(na):
            r = outs[a].shape[0] // (N_CHIPS if axes[a] == 0 else 1)
            n = outs[a].shape[axes[a]] // N_CHIPS
            half = r // 2

            def part(kk, cc, a=a, n=n, half=half):
                rows = pl.ds(pl.multiple_of(cc * half + (kk * n if axes[a] == 0 else 0), 8), half)
                return outs[a].at[rows, :] if axes[a] == 0 else outs[a].at[rows, pl.ds(pl.multiple_of(kk * n, 128), n)]

            needs, passes, lands = [], [], []
            for j, (peer, kp) in enumerate(_chip_peers(x, y, c)):
                if j not in chips:
                    continue
                s = 6 * a + j
                remote.append(pltpu.make_async_remote_copy(part(k, c), part(k, c), send.at[s], recv.at[s],
                                                           device_id=peer, device_id_type=MESH))
                needs.append(pltpu.make_async_remote_copy(part(kp, c), part(kp, c), send.at[s], recv.at[s],
                                                          device_id=peer, device_id_type=MESH))
                passes.append(pltpu.make_async_remote_copy(part(kp, c), part(kp, c), send.at[s + 3], recv.at[s + 3],
                                                           device_id=(x, y, 1 - c), device_id_type=MESH))
                lands.append(pltpu.make_async_remote_copy(part(kp, 1 - c), part(kp, 1 - c), send.at[s + 3], recv.at[s + 3],
                                                          device_id=(x, y, 1 - c), device_id_type=MESH))
            relays.append((needs, passes, lands))
        return [], remote, [], relays

    out_shape = [jax.ShapeDtypeStruct(w.shape, BF16) for w in shards]
    return CopyJob(shards, out_shape, _job_sems(6 * na, 1), copies, {a: a for a in range(na)})


def scatter_job(layers, g16, axes, filled, chips=(0, 1, 2)):
    na = len(axes)

    def shard_shape(a):
        r, c = g16[a].shape
        return (r // N_CHIPS, c) if axes[a] == 0 else (r, c // N_CHIPS)

    def copies(ins, outs, sems):
        send, recv_sems, _ = sems
        x, y, c = _place()
        remote = []
        for a in range(na):
            n = shard_shape(a)[axes[a]]
            for r, (peer, kp) in enumerate(_chip_peers(x, y, c)):
                if r not in chips:
                    continue
                remote.append(pltpu.make_async_remote_copy(_shard_of(ins[a], axes[a], kp, n), outs[a].at[r, layers[a]],
                                                           send.at[3 * a + r], recv_sems.at[3 * a + r],
                                                           device_id=peer, device_id_type=MESH))
        return [], remote, remote, []

    out_shape = [jax.ShapeDtypeStruct((3, DEPTH) + shard_shape(a), BF16) for a in range(na)]
    ins = list(g16)
    aliases = {}
    for a in range(na):
        if filled[a] is not None:
            aliases[len(ins)] = a
            ins.append(filled[a])
    return CopyJob(ins, out_shape, _job_sems(3 * na, 1), copies, aliases)


def pair_job(g16, axes):
    na = len(axes)
    pieces = [1 if ax == 1 else N_CHIPS for ax in axes]

    def copies(ins, outs, sems):
        send, recv, _ = sems
        x, y, c = _place()
        remote = []
        s = 0
        for a in range(na):
            rows = g16[a].shape[0] // (2 * pieces[a])
            for kk in range(pieces[a]):
                src = ins[a].at[pl.ds(pl.multiple_of((2 * kk + 1 - c) * rows, 8), rows), :]
                remote.append(pltpu.make_async_remote_copy(src, outs[a].at[pl.ds(kk * rows, rows), :], send.at[s], recv.at[s],
                                                           device_id=(x, y, 1 - c), device_id_type=MESH))
                s += 1
        return [], remote, remote, []

    out_shape = [jax.ShapeDtypeStruct((g.shape[0] // 2, g.shape[1]), BF16) for g in g16]
    return CopyJob(g16, out_shape, _job_sems(sum(pieces), 1), copies)


def join_job(shards):
    na = len(shards)

    def copies(ins, outs, sems):
        send, recv, _ = sems
        x, y, c = _place()
        remote = [pltpu.make_async_remote_copy(outs[a].at[:, c], outs[a].at[:, c], send.at[a], recv.at[a],
                                               device_id=(x, y, 1 - c), device_id_type=MESH) for a in range(na)]
        lands = [pltpu.make_async_remote_copy(outs[a].at[:, 1 - c], outs[a].at[:, 1 - c], send.at[a], recv.at[a],
                                              device_id=(x, y, 1 - c), device_id_type=MESH) for a in range(na)]
        return [], remote, lands, []

    out_shape = [jax.ShapeDtypeStruct(s.shape, F32) for s in shards]
    return CopyJob(shards, out_shape, _job_sems(na, 1), copies, {a: a for a in range(na)})


def small_job(p):
    def copies(ins, outs, sems):
        send, recv, loc = sems
        x, y, c = _place()
        me = 4 * x + 2 * y + c
        remote, lands = [], []
        for rel in range(1, 8):
            dx, dy, dc = rel >> 2, (rel >> 1) & 1, rel & 1
            peer = (1 - x if dx else x, 1 - y if dy else y, 1 - c if dc else c)
            who = 4 * peer[0] + 2 * peer[1] + peer[2]
            remote.append(pltpu.make_async_remote_copy(ins[0], outs[0].at[me], send.at[rel - 1], recv.at[rel - 1],
                                                       device_id=peer, device_id_type=MESH))
            lands.append(pltpu.make_async_remote_copy(ins[0], outs[0].at[who], send.at[rel - 1], recv.at[rel - 1],
                                                      device_id=peer, device_id_type=MESH))
        return [pltpu.make_async_copy(ins[0], outs[0].at[me], loc.at[0])], remote, lands, []

    return CopyJob([p], [jax.ShapeDtypeStruct((8,) + p.shape, F32)], _job_sems(7, 1), copies)


def small_sum(slots):
    def add(*terms):
        acc = terms[0]
        for t in terms[1:]:
            acc = acc + t
        return (acc,)

    return _rows_call(add, [(slots, d) for d in range(8)], [F32], name="small_sum", tr=8 * 47)[0]


BIG = ("w_in", "p_ret", "p_sb", "p_sgu", "w_out", "w_up", "w_down")
BIG_AXIS = {"w_in": 1, "p_ret": 1, "p_sb": 1, "p_sgu": 1, "w_out": 0, "w_up": 1, "w_down": 0}
SMALL = ("ret_gn_g", "ret_gn_b", "sgu_ln_g", "sgu_ln_b", "sgu_w", "sgu_b", "ln1_g", "ln1_b", "ln2_g", "ln2_b")


def layer_forward(l, x0, x0h, W, sm, rope, rconsts, hooks):
    n = f"l{l}_"
    job = hooks.fwd_job(l, "proj")
    proj = matmul(x0h, W["w_in"], mode="nn", tm=4096, tn=768, tk=1024, name=n + "proj", job=job)
    if job is not None:
        proj, job_out = proj
        hooks.done(job, job_out)
    retg, raw, states = ret_fwd(proj, *rope, rconsts, sm["ret_gn_g"], sm["ret_gn_b"], name=n + "ret_fwd")
    job = hooks.fwd_job(l, "sb")
    sb, job_out = sb_fwd(proj, name=n + "sb_fwd", job=job)
    if job is not None:
        hooks.done(job, job_out)
    sg = sgu_fwd(proj, sm["sgu_ln_g"], sm["sgu_ln_b"], sm["sgu_w"], sm["sgu_bias"], name=n + "sgu_fwd")
    merged, r1, r2, r3 = merge_fwd(retg, sb, sg, W["p_ret"], W["p_sb"], W["p_sgu"], proj, name=n + "merge_fwd")
    x1, xh1, rs1, x1h = matmul_ln(merged, W["w_out"], x0, sm["ln1_g"], sm["ln1_b"], tk=1024, name=n + "out_ln1")
    job = hooks.fwd_job(l, "up")
    h1 = matmul(x1h, W["w_up"], mode="nn", tm=1024, tn=1024, tk=1024, outs=((BF16, None),), name=n + "up", job=job)
    if job is not None:
        h1, job_out = h1
        hooks.done(job, job_out)
    job = hooks.fwd_job(l, "down")
    res = matmul_ln(h1, W["w_down"], x1, sm["ln2_g"], sm["ln2_b"], pro=_relu2, tk=1024, name=n + "down_ln2", job=job)
    if job is not None:
        res, job_out = res
        hooks.done(job, job_out)
    x2, xh2, rs2, x2h = res
    saved = dict(x0h=x0h, proj=proj, retg=retg, raw=raw, states=states, sb=sb, sg=sg, merged=merged, r=(r1, r2, r3),
                 x1h=x1h, xh1=xh1, rs1=rs1, h1=h1, xh2=xh2, rs2=rs2)
    return x2, x2h, saved


def layer_backward(l, dx2, s, W, sm, rope, rconsts, hooks):
    n = f"l{l}_"
    two = ((F32, None), (BF16, None))
    gw, gs = {}, {}
    job = hooks.bwd_job(l, "ln2")
    res = ln_bwd(dx2, s["xh2"], s["rs2"], sm["ln2_g"], name=n + "ln2_bwd", job=job)
    if job is not None:
        res, job_out = res
        hooks.done(job, job_out)
    du2, du2h, gs["ln2_g"], gs["ln2_b"] = res
    job = hooks.bwd_job(l, "g_down")
    gw["w_down"] = matmul(s["h1"], du2h, mode="tn", tm=1024, tn=1024, tk=4096, pro=_relu2, outs=two, name=n + "g_down", job=job)
    if job is not None:
        gw["w_down"], job_out = gw["w_down"]
        hooks.done(job, job_out)
    dh1 = matmul(du2h, W["w_down"], mode="nt", tm=1024, tn=1024, tk=1024, outs=((BF16, None),),
                 epi=lambda acc, h: (acc * (2.0 * jnp.maximum(h.astype(F32), 0.0)),), tiles=(s["h1"],), name=n + "d_h1")
    job = hooks.bwd_job(l, "g_up")
    gw["w_up"] = matmul(s["x1h"], dh1, mode="tn", tm=1024, tn=1024, tk=4096, outs=two, name=n + "g_up", job=job)
    if job is not None:
        gw["w_up"], job_out = gw["w_up"]
        hooks.done(job, job_out)
    dx1 = matmul(dh1, W["w_up"], mode="nt", tm=1024, tn=1024, tk=4096,
                 epi=lambda acc, d: (acc + ALPHA * d,), tiles=(du2,), name=n + "d_x1")
    du1, du1h, gs["ln1_g"], gs["ln1_b"] = ln_bwd(dx1, s["xh1"], s["rs1"], sm["ln1_g"], name=n + "ln1_bwd")
    gw["w_out"] = matmul(s["merged"], du1h, mode="tn", tm=1024, tn=1024, tk=4096, outs=two, name=n + "g_out")
    gate0 = C_GATE // 512
    dr1, dr2, dr3, dg1, dg2, dg3 = matmul(
        du1h, W["w_out"], mode="nt", tm=1024, tn=512, tk=1024, outs=((BF16, None),) * 6, epi=_merge_bwd_epi,
        tiles=(*s["r"], (s["proj"], gate0), (s["proj"], gate0 + 2), (s["proj"], gate0 + 4)), name=n + "d_merged")
    d_branch = {}
    for nm, a, dr in (("p_ret", s["retg"], dr1), ("p_sb", s["sb"], dr2), ("p_sgu", s["sg"], dr3)):
        gw[nm] = matmul(a, dr, mode="tn", tm=512, tn=1024, tk=2048, outs=two, name=n + "g_" + nm)
        d_branch[nm] = matmul(dr, W[nm], mode="nt", tm=1024, tn=512, tk=1024, name=n + "d_" + nm)
    job = hooks.pair(l, gw)
    dret, gs["ret_gn_g"], gs["ret_gn_b"], job_out = ret_bwd(s["proj"], *rope, rconsts, sm["ret_gn_g"], sm["ret_gn_b"],
                                                             s["raw"], s["states"], d_branch["p_ret"], name=n + "ret_bwd", job=job)
    if job is not None:
        hooks.done(job, job_out)
    job = hooks.scatter(l) if job is not None else None
    dsq, dsk, dsv, job_out = sb_bwd(s["proj"], s["sb"], d_branch["p_sb"], name=n + "sb_bwd", job=job)
    if job is not None:
        hooks.done(job, job_out)
    dsgu, gs["sgu_w"], dbias, gs["sgu_ln_g"], gs["sgu_ln_b"] = sgu_bwd(
        s["proj"], sm["sgu_ln_g"], sm["sgu_ln_b"], sm["sgu_w"], sm["sgu_bias"], d_branch["p_sgu"], name=n + "sgu_bwd")
    gs["sgu_b"] = dbias[:, :, 0]
    dproj = jnp.concatenate([dret, dsq, dsk, dsv, dsgu, dg1, dg2, dg3], axis=1)
    job = hooks.small(l, gs)
    gw["w_in"] = matmul(s["x0h"], dproj, mode="tn", tm=1024, tn=1536, tk=2048, outs=two, name=n + "g_in", job=job)
    if job is not None:
        gw["w_in"], job_out = gw["w_in"]
        hooks.done(job, job_out)
    job = hooks.tail(l, gw["w_in"])
    dx0 = matmul(dproj, W["w_in"], mode="nt", tm=1024, tn=1024, tk=2560,
                 epi=lambda acc, d: (acc + ALPHA * d,), tiles=(du1,), name=n + "d_x0", job=job)
    if job is not None:
        dx0, job_out = dx0
        hooks.done(job, job_out)
    return dx0, gw, gs


def local_step(x, target, small, plan):
    T = x.shape[0]
    rope = _rope_tables(T)
    rconsts = _ret_consts()
    sms = []
    for l in range(DEPTH):
        sm = {k: small[k][l][None, :] for k in SMALL if k not in ("sgu_w", "sgu_b")}
        sm["sgu_w"] = small["sgu_w"][l]
        sm["sgu_bias"] = jnp.broadcast_to(small["sgu_b"][l][:, :, None], (4, CHUNK, CHUNK))
        sms.append(sm)
    h, saved = x, []
    job = plan.first_job()
    hh = _rows_call(lambda a: (a,), [x], [BF16], name="cast_x", job=job)
    if job is not None:
        hh, job_out = hh
        plan.done(job, job_out)
    hh = hh[0]
    for l in range(DEPTH):
        h, hh, s = layer_forward(l, h, hh, plan.weights(l), sms[l], rope, rconsts, plan)
        saved.append(s)
    dy, sq = loss_head(h, target)
    gs = {k: [None] * DEPTH for k in SMALL}
    for l in reversed(range(DEPTH)):
        dy, gwl, gsl = layer_backward(l, dy, saved[l], plan.weights(l), sms[l], rope, rconsts, plan)
        plan.grads(l, gwl)
        for k in SMALL:
            gs[k][l] = gsl[k].reshape(small[k].shape[1:])
    return sq[0, 0], dy, {k: jnp.stack(v) for k, v in gs.items()}


EARLY_GRADS = ("p_ret", "p_sb", "p_sgu", "w_out", "w_up", "w_down")


class _StepPlan:
    def __init__(self, pos, shards16):
        self.pos = pos
        self.shards16 = shards16
        self.full = [dict() for _ in range(DEPTH)]
        self.gw = [None] * DEPTH
        self.bufs = {}
        self.sums = {}
        self.gs = [None] * DEPTH

    def first_job(self):
        return self._gather([(0, "w_in")])

    def weights(self, l):
        return self.full[l]

    def grads(self, l, gw):
        self.gw[l] = gw

    def _gather(self, items, chips=(0, 1, 2)):
        job = gather_job([self.shards16[l][k] for l, k in items], [BIG_AXIS[k] for _, k in items], chips)
        job.note = ("gather" if 2 in chips else "gather_part", items)
        return job

    def _pair(self, items):
        job = pair_job([g[1] for _, _, g in items], [BIG_AXIS[k] for _, k, _ in items])
        job.note = ("pair", items)
        return job

    def fwd_job(self, l, host):
        if host == "proj":
            return None
        if host == "sb":
            return self._gather([(l, k) for k in BIG[1:]])
        if l + 1 == DEPTH:
            return None
        return self._gather([(l + 1, "w_in")], (0, 1) if host == "up" else (2,))

    def bwd_job(self, l, host):
        if l + 1 == DEPTH:
            return None
        if host == "ln2":
            job = self._pair([(l + 1, "w_in", self.gw[l + 1]["w_in"])])
            job.note = ("pair_w_in", job.note[1])
            return job
        items, sums16 = self.summed_w_in
        job = scatter_job([l_ for l_, _, _ in items], sums16, [BIG_AXIS[k] for _, k, _ in items],
                          [self.bufs.get(k) for _, k, _ in items], (0, 1) if host == "g_down" else (2,))
        job.note = ("scatter", items)
        return job

    def pair(self, l, ready):
        return self._pair([(l, k, ready[k]) for k in EARLY_GRADS])

    def scatter(self, l):
        items, sums16 = self.summed
        job = scatter_job([l_ for l_, _, _ in items], sums16, [BIG_AXIS[k] for _, k, _ in items],
                          [self.bufs.get(k) for _, k, _ in items])
        job.note = ("scatter", items)
        return job

    def small(self, l, gs):
        self.gs[l] = {k: gs[k].reshape(-1) for k in SMALL}
        if l != 0:
            return None
        job = small_job(_pack_small({k: jnp.stack([self.gs[l_][k] for l_ in range(DEPTH)]) for k in SMALL}))
        job.note = ("small", [])
        return job

    def tail(self, l, g):
        if l != 0:
            return None
        last = self._pair([(0, "w_in", g)])
        self.done(last, run_job(last, name="pair_last"))
        return self.scatter(0)

    def done(self, job, outs):
        kind, items = job.note
        if kind == "small":
            self.small_slots = outs[0]
        if kind in ("pair", "pair_w_in"):
            sums16 = []
            for a, (l, k, g) in enumerate(items):
                self.sums[(l, k)], s16 = pair_sum(self.pos, outs[a], g[0], BIG_AXIS[k], name=f"pair_sum_{k}_{l}")
                sums16.append(s16)
            if kind == "pair":
                self.summed = (items, sums16)
            else:
                self.summed_w_in = (items, sums16)
        for a, item in enumerate(items):
            if kind == "gather_part":
                self.shards16[item[0]][item[1]] = outs[a]
            elif kind == "gather":
                self.full[item[0]][item[1]] = outs[a]
            elif kind == "scatter":
                self.bufs[item[1]] = outs[a]

    def finish(self):
        return self.bufs, self.sums


def _flat2(a):
    return a.reshape(-1, a.shape[-1])


def _pack_small(d, pre=""):
    return jnp.concatenate([d[pre + k].reshape(-1) for k in SMALL]).reshape(-1, 128)


def kernel(x, w_in, ret_gn_g, ret_gn_b, sgu_ln_g, sgu_ln_b, sgu_w, sgu_b, p_ret, p_sb, p_sgu, w_out, ln1_g, ln1_b, w_up, w_down, ln2_g, ln2_b, loss_target, m_w_in, m_ret_gn_g, m_ret_gn_b, m_sgu_ln_g, m_sgu_ln_b, m_sgu_w, m_sgu_b, m_p_ret, m_p_sb, m_p_sgu, m_w_out, m_ln1_g, m_ln1_b, m_w_up, m_w_down, m_ln2_g, m_ln2_b, v_w_in, v_ret_gn_g, v_ret_gn_b, v_sgu_ln_g, v_sgu_ln_b, v_sgu_w, v_sgu_b, v_p_ret, v_p_sb, v_p_sgu, v_w_out, v_ln1_g, v_ln1_b, v_w_up, v_w_down, v_ln2_g, v_ln2_b):
    given = dict(locals())
    order = BIG[:1] + SMALL[:6] + BIG[1:5] + SMALL[6:8] + BIG[5:7] + SMALL[8:10]
    L = DEPTH

    px, py, pc = _place()
    pos = jnp.stack([px, py, pc, 2 * px + py]).astype(jnp.int32)

    shards16 = [{k: cast_into_whole(pos, given[k], l, BIG_AXIS[k], name=f"cast_{k}_{l}") for k in BIG} for l in range(L)]
    plan = _StepPlan(pos, shards16)
    sq, dx, _ = local_step(x[0], loss_target[0], {k: given[k] for k in SMALL}, plan)
    loss = 0.5 * lax.psum(sq, ("x", "y", "c"))

    bufs, sums = plan.finish()
    shards = []
    for k in BIG:
        whole = None
        for l in range(L):
            whole = chip_sum(pos, sums[(l, k)], bufs[k], l, BIG_AXIS[k], whole, name=f"chip_sum_{k}_{l}")
        shards.append(whole)
    joined = run_job(join_job(shards), name="join_halves")
    out = {}
    for a, k in enumerate(BIG):
        shp = given[k].shape
        res = _rows_call(lambda g_, w_, m_, v_: (g_,) + _adamw(w_, g_, m_, v_),
                         [joined[a].reshape(-1, shp[-1]), _flat2(given[k]), _flat2(given["m_" + k]), _flat2(given["v_" + k])],
                         [F32] * 4, name="adamw_" + k)
        out[k] = [r.reshape(shp) for r in res]

    pack = _pack_small
    res = _rows_call(lambda g_, w_, m_, v_: (g_,) + _adamw(w_, g_, m_, v_),
                     [small_sum(plan.small_slots), pack(given), pack(given, "m_"), pack(given, "v_")], [F32] * 4,
                     name="adamw_small", tr=8 * 47)
    off = 0
    for k in SMALL:
        sz = given[k].size
        out[k] = [r.reshape(-1)[off:off + sz].reshape(given[k].shape) for r in res]
        off += sz

    grads = [out[k][0] for k in order]
    deltas = [out[k][1] for k in order]
    new_m = [out[k][2] for k in order]
    new_v = [out[k][3] for k in order]
    return (loss, dx[None], *grads, *deltas, *new_m, *new_v)
```

```python
import functools
import math

import jax
import jax.numpy as jnp
from jax import lax
from jax.experimental import pallas as pl
from jax.experimental.pallas import tpu as pltpu

F32 = jnp.float32
BF16 = jnp.bfloat16

D_MODEL = 1024
SEQ = 4096
DEPTH = 2
CHUNK = 128
RET_HEADS = 4
BRANCH_W = 512
N_IN = 7680
D_FF = 4096
LN_EPS = 1e-5
ROPE_BASE = 10000.0
ALPHA = (2 * DEPTH) ** 0.25
RET_SCALE = 128 ** -0.5
SB_SCALE = 64 ** -0.5
C_RET, C_SB, C_SGU, C_GATE = 0, 2048, 3584, 4608

ADAM_LR, ADAM_B1, ADAM_B2, ADAM_EPS, ADAM_WD, ADAM_STEP = 0.001, 0.9, 0.999, 1e-08, 0.01, 10

N_CHIPS = 4
VMEM_LIMIT = 56 * 1024 * 1024
MESH = pl.DeviceIdType.MESH

NN = ((1,), (0,))
NT = ((1,), (1,))
TN = ((0,), (0,))


def _dot(a, b, dims):
    return lax.dot_general(a, b, (dims, ((), ())), preferred_element_type=F32)


def _params(sem):
    return pltpu.CompilerParams(dimension_semantics=sem, vmem_limit_bytes=VMEM_LIMIT)


def _relu2(h):
    r = jnp.maximum(h.astype(F32), 0.0)
    return r * r


def matmul(a, b, *, mode, tm, tn, tk, outs=((F32, None),), pro=None, epi=None, tiles=(), rows=(), name, job=None):
    if mode == "nn":
        (M, K), N = a.shape, b.shape[1]
    elif mode == "nt":
        (M, K), N = a.shape, b.shape[0]
    else:
        (K, M), N = a.shape, b.shape[1]
    tm, tn, tk = min(tm, M), min(tn, N), min(tk, K)
    assert M % tm == 0 and N % tn == 0 and K % tk == 0, (name, M, N, K, tm, tn, tk)
    if mode == "nn":
        a_spec = pl.BlockSpec((tm, tk), lambda i, j, k: (i, k))
        b_spec = pl.BlockSpec((tk, tn), lambda i, j, k: (k, j))
        dims = NN
    elif mode == "nt":
        a_spec = pl.BlockSpec((tm, tk), lambda i, j, k: (i, k))
        b_spec = pl.BlockSpec((tn, tk), lambda i, j, k: (j, k))
        dims = NT
    else:
        a_spec = pl.BlockSpec((tk, tm), lambda i, j, k: (k, i))
        b_spec = pl.BlockSpec((tk, tn), lambda i, j, k: (k, j))
        dims = TN
    nk = K // tk
    nt_, nr, no = len(tiles), len(rows), len(outs)

    def body(a_ref, b_ref, *rest):
        tile_refs = rest[:nt_]
        row_refs = rest[nt_:nt_ + nr]
        out_refs = rest[nt_ + nr:nt_ + nr + no]
        av = a_ref[...]
        if pro is not None:
            av = pro(av)
        p = _dot(av.astype(BF16), b_ref[...].astype(BF16), dims)

        def finish(acc):
            vals = (acc,) * no if epi is None else epi(acc, *[r[...] for r in tile_refs], *[r[...] for r in row_refs])
            for o_ref, v in zip(out_refs, vals):
                o_ref[...] = v.astype(o_ref.dtype)

        if nk == 1:
            finish(p)
        else:
            acc_ref = rest[-1]
            k = pl.program_id(2)

            @pl.when(k == 0)
            def _():
                acc_ref[...] = p

            @pl.when(k > 0)
            def _():
                acc_ref[...] += p

            @pl.when(k == nk - 1)
            def _():
                finish(acc_ref[...])

    out_shape, out_specs = [], []
    for dt, width in outs:
        if width is None:
            out_shape.append(jax.ShapeDtypeStruct((M, N), dt))
            out_specs.append(pl.BlockSpec((tm, tn), lambda i, j, k: (i, j)))
        else:
            assert N == tn
            out_shape.append(jax.ShapeDtypeStruct((M, width), dt))
            out_specs.append(pl.BlockSpec((tm, width), lambda i, j, k: (i, 0)))
    in_specs = [a_spec, b_spec]
    offs = [t[1] if isinstance(t, tuple) else 0 for t in tiles]
    tiles = [t[0] if isinstance(t, tuple) else t for t in tiles]
    in_specs += [pl.BlockSpec((tm, tn), functools.partial(lambda i, j, k, o: (i, j + o), o=o)) for o in offs]
    in_specs += [pl.BlockSpec((1, tn), lambda i, j, k: (0, j)) for _ in rows]
    grid = (M // tm, N // tn, nk)
    scratch = [pltpu.VMEM((tm, tn), F32)] if nk > 1 else []
    j = _job_args(job, len(in_specs), no)
    res = pl.pallas_call(
        _hosting(body, job, len(in_specs), no, len(scratch), grid), name=name, grid=grid,
        in_specs=in_specs + j["in_specs"], out_specs=out_specs + j["out_specs"], out_shape=out_shape + j["out_shape"],
        scratch_shapes=scratch + j["scratch"], input_output_aliases=j["aliases"],
        compiler_params=_params(("parallel", "parallel", "arbitrary") if job is None else ("arbitrary",) * 3),
    )(a, b, *tiles, *rows, *j["ins"])
    mine = res[0] if no == 1 else list(res[:no])
    return mine if job is None else (mine, list(res[no:]))


def _ln_epi(acc, res, g, b):
    u = ALPHA * res + acc
    mu = jnp.mean(u, axis=-1, keepdims=True)
    xc = u - mu
    var = jnp.mean(xc * xc, axis=-1, keepdims=True)
    rstd = lax.rsqrt(var + LN_EPS)
    xhat = xc * rstd
    y = xhat * g + b
    return y, xhat, jnp.broadcast_to(rstd, (u.shape[0], 128)), y


def matmul_ln(a, w, res, g, b, *, pro=None, tk, name, job=None):
    n = w.shape[1]
    return matmul(a, w, mode="nn", tm=1024, tn=n, tk=tk, pro=pro, epi=_ln_epi, tiles=(res,), rows=(g, b),
                  outs=((F32, None), (F32, None), (F32, 128), (BF16, None)), name=name, job=job)


def ln_bwd(dy, xhat, rstd, g, *, name, job=None):
    T, D = dy.shape
    tm = min(512, T)

    def body(dy_ref, xh_ref, rs_ref, g_ref, du_ref, du16_ref, dg_ref, db_ref):
        dyv, xh = dy_ref[...], xh_ref[...]
        r = rs_ref[:, 0:1]
        dxh = dyv * g_ref[...]
        m1 = jnp.mean(dxh, axis=-1, keepdims=True)
        m2 = jnp.mean(dxh * xh, axis=-1, keepdims=True)
        du = r * (dxh - m1 - xh * m2)
        du_ref[...] = du
        du16_ref[...] = du.astype(BF16)

        @pl.when(pl.program_id(0) == 0)
        def _():
            dg_ref[...] = jnp.zeros_like(dg_ref)
            db_ref[...] = jnp.zeros_like(db_ref)

        dg_ref[...] += jnp.sum(dyv * xh, axis=0, keepdims=True)
        db_ref[...] += jnp.sum(dyv, axis=0, keepdims=True)

    row = pl.BlockSpec((tm, D), lambda i: (i, 0))
    vec = pl.BlockSpec((1, D), lambda i: (0, 0))
    j = _job_args(job, 4, 4)
    res = pl.pallas_call(
        _hosting(body, job, 4, 4, 0, T // tm), name=name, grid=(T // tm,),
        in_specs=[row, row, pl.BlockSpec((tm, 128), lambda i: (i, 0)), vec] + j["in_specs"],
        out_specs=[row, row, vec, vec] + j["out_specs"],
        out_shape=[jax.ShapeDtypeStruct((T, D), F32), jax.ShapeDtypeStruct((T, D), BF16),
                   jax.ShapeDtypeStruct((1, D), F32), jax.ShapeDtypeStruct((1, D), F32)] + j["out_shape"],
        scratch_shapes=j["scratch"], input_output_aliases=j["aliases"],
        compiler_params=_params(("arbitrary",)),
    )(dy, xhat, rstd, g, *j["ins"])
    return list(res[:4]) if job is None else (list(res[:4]), list(res[4:]))


def loss_head(y, target):
    T, D = y.shape
    tm = min(512, T)

    def body(y_ref, t_ref, dy_ref, s_ref):
        e = y_ref[...] - t_ref[...]
        dy_ref[...] = e * (1.0 / D)

        @pl.when(pl.program_id(0) == 0)
        def _():
            s_ref[...] = jnp.zeros_like(s_ref)

        s_ref[...] += jnp.sum(jnp.mean(e * e, axis=-1, keepdims=True))

    row = pl.BlockSpec((tm, D), lambda i: (i, 0))
    return pl.pallas_call(
        body, name="loss_head", grid=(T // tm,),
        in_specs=[row, row], out_specs=[row, pl.BlockSpec((8, 128), lambda i: (0, 0))],
        out_shape=[jax.ShapeDtypeStruct((T, D), F32), jax.ShapeDtypeStruct((8, 128), F32)],
        compiler_params=_params(("arbitrary",)),
    )(y, target)


def _rope_tables(T):
    half = 64
    inv_freq = ROPE_BASE ** (-jnp.arange(half, dtype=F32) / half)
    ang = jnp.arange(T, dtype=jnp.int32).astype(F32)[:, None] * inv_freq[None, :]
    cos, sin = jnp.cos(ang), jnp.sin(ang)
    return jnp.concatenate([cos, cos], axis=1), jnp.concatenate([-sin, sin], axis=1)


def _ret_consts():
    H = RET_HEADS
    log_g = jnp.log(1.0 - 2.0 ** (-5.0 - jnp.arange(H, dtype=F32)))
    idx = jnp.arange(CHUNK, dtype=F32)
    diff = idx[:, None] - idx[None, :]
    dmat = jnp.where(diff[None] >= 0, jnp.exp(log_g[:, None, None] * diff[None]), 0.0)
    kd = jnp.exp(log_g[:, None] * (CHUNK - 1 - idx)[None, :])
    qd = jnp.exp(log_g[:, None] * (idx + 1.0)[None, :])
    cd = jnp.exp(log_g * CHUNK)
    full = (H, CHUNK, CHUNK)
    return (dmat.astype(F32), jnp.broadcast_to(kd[:, :, None], full), jnp.broadcast_to(qd[:, :, None], full),
            jnp.broadcast_to(cd[:, None, None], full))


def _swap_halves(v):
    return pltpu.roll(v, 64, 1)


def _group_norm(o):
    mu = jnp.mean(o, axis=-1, keepdims=True)
    xc = o - mu
    var = jnp.mean(xc * xc, axis=-1, keepdims=True)
    rstd = lax.rsqrt(var + LN_EPS)
    return xc * rstd, rstd


def ret_fwd(proj, cosf, sinf, consts, gn_g, gn_b, *, name):
    T = proj.shape[0]
    tb = min(512, T)
    nch = tb // CHUNK
    H = RET_HEADS

    def body(p_ref, cos_ref, sin_ref, dm_ref, kd_ref, qd_ref, cd_ref, g_ref, b_ref, out_ref, raw_ref, st_ref, s_ref):
        @pl.when(pl.program_id(0) == 0)
        def _():
            s_ref[...] = jnp.zeros_like(s_ref)

        for c in range(nch):
            r = slice(c * CHUNK, (c + 1) * CHUNK)
            cs, sn = cos_ref[r, :], sin_ref[r, :]
            for h in range(H):
                hc = slice(h * 128, (h + 1) * 128)
                q = p_ref[r, h * 128:(h + 1) * 128]
                k = p_ref[r, 512 + h * 128:512 + (h + 1) * 128]
                v = p_ref[r, 1024 + h * 128:1024 + (h + 1) * 128]
                gt = p_ref[r, 1536 + h * 128:1536 + (h + 1) * 128]
                qr = q * cs + _swap_halves(q) * sn
                kr = (k * cs + _swap_halves(k) * sn) * RET_SCALE
                sprev = s_ref[h]
                st_ref[c, h] = sprev
                qb, kb, vb = qr.astype(BF16), kr.astype(BF16), v.astype(BF16)
                s = _dot(qb, kb, NT) * dm_ref[h]
                o = _dot(s.astype(BF16), vb, NN) + _dot((qr * qd_ref[h]).astype(BF16), sprev.astype(BF16), NN)
                s_ref[h] = sprev * cd_ref[h] + _dot((kr * kd_ref[h]).astype(BF16), vb, TN)
                raw_ref[r, hc] = o
                y, _ = _group_norm(o)
                out_ref[r, hc] = (gt * jax.nn.sigmoid(gt)) * (y * g_ref[:, hc] + b_ref[:, hc])

    cmat = pl.BlockSpec((H, CHUNK, CHUNK), lambda i: (0, 0, 0))
    vec = pl.BlockSpec((1, BRANCH_W), lambda i: (0, 0))
    rope = pl.BlockSpec((tb, 128), lambda i: (i, 0))
    blk = pl.BlockSpec((tb, BRANCH_W), lambda i: (i, 0))
    return pl.pallas_call(
        body, name=name, grid=(T // tb,),
        in_specs=[pl.BlockSpec((tb, 2048), lambda i: (i, 0)), rope, rope, cmat, cmat, cmat, cmat, vec, vec],
        out_specs=[blk, blk, pl.BlockSpec((nch, H, CHUNK, CHUNK), lambda i: (i, 0, 0, 0))],
        out_shape=[jax.ShapeDtypeStruct((T, BRANCH_W), F32), jax.ShapeDtypeStruct((T, BRANCH_W), F32),
                   jax.ShapeDtypeStruct((T // CHUNK, H, CHUNK, CHUNK), F32)],
        scratch_shapes=[pltpu.VMEM((H, CHUNK, CHUNK), F32)],
        compiler_params=_params(("arbitrary",)),
    )(proj, cosf, sinf, *consts, gn_g, gn_b)


def ret_bwd(proj, cosf, sinf, consts, gn_g, gn_b, raw, states, dout, *, name, job=None):
    T = proj.shape[0]
    tb = min(512, T)
    nch = tb // CHUNK
    nb = T // tb
    H = RET_HEADS

    def body(p_ref, cos_ref, sin_ref, dm_ref, kd_ref, qd_ref, cd_ref, g_ref, b_ref, raw_ref, st_ref, do_ref,
             dp_ref, dg_ref, db_ref, ds_ref):
        @pl.when(pl.program_id(0) == 0)
        def _():
            ds_ref[...] = jnp.zeros_like(ds_ref)
            dg_ref[...] = jnp.zeros_like(dg_ref)
            db_ref[...] = jnp.zeros_like(db_ref)

        for c in reversed(range(nch)):
            r = slice(c * CHUNK, (c + 1) * CHUNK)
            cs, sn = cos_ref[r, :], sin_ref[r, :]
            for h in range(H):
                hc = slice(h * 128, (h + 1) * 128)
                q = p_ref[r, h * 128:(h + 1) * 128]
                k = p_ref[r, 512 + h * 128:512 + (h + 1) * 128]
                v = p_ref[r, 1024 + h * 128:1024 + (h + 1) * 128]
                gt = p_ref[r, 1536 + h * 128:1536 + (h + 1) * 128]
                qr = q * cs + _swap_halves(q) * sn
                kr = (k * cs + _swap_halves(k) * sn) * RET_SCALE
                sprev = st_ref[c, h]
                gv = g_ref[:, hc]
                y, rstd = _group_norm(raw_ref[r, hc])
                d_out = do_ref[r, hc]
                sg = jax.nn.sigmoid(gt)
                d_gate = d_out * (y * gv + b_ref[:, hc]) * (sg * (1.0 + gt * (1.0 - sg)))
                d_aff = d_out * (gt * sg)
                dg_ref[:, hc] += jnp.sum(d_aff * y, axis=0, keepdims=True)
                db_ref[:, hc] += jnp.sum(d_aff, axis=0, keepdims=True)
                dxh = d_aff * gv
                m1 = jnp.mean(dxh, axis=-1, keepdims=True)
                m2 = jnp.mean(dxh * y, axis=-1, keepdims=True)
                d_o = (rstd * (dxh - m1 - y * m2)).astype(BF16)
                qb, kb, vb = qr.astype(BF16), kr.astype(BF16), v.astype(BF16)
                dm, kd, qd = dm_ref[h], kd_ref[h], qd_ref[h]
                p = (_dot(qb, kb, NT) * dm).astype(BF16)
                dp = (_dot(d_o, vb, NT) * dm).astype(BF16)
                dsn = ds_ref[h]
                dsb = dsn.astype(BF16)
                dq_r = _dot(dp, kb, NN) + _dot(d_o, sprev.astype(BF16), NT) * qd
                dk_r = (_dot(dp, qb, TN) + _dot(vb, dsb, NT) * kd) * RET_SCALE
                d_v = _dot(p, d_o, TN) + _dot((kr * kd).astype(BF16), dsb, NN)
                ds_ref[h] = dsn * cd_ref[h] + _dot((qr * qd).astype(BF16), d_o, TN)
                dp_ref[r, h * 128:(h + 1) * 128] = (dq_r * cs - _swap_halves(dq_r) * sn).astype(BF16)
                dp_ref[r, 512 + h * 128:512 + (h + 1) * 128] = (dk_r * cs - _swap_halves(dk_r) * sn).astype(BF16)
                dp_ref[r, 1024 + h * 128:1024 + (h + 1) * 128] = d_v.astype(BF16)
                dp_ref[r, 1536 + h * 128:1536 + (h + 1) * 128] = d_gate.astype(BF16)

    cmat = pl.BlockSpec((H, CHUNK, CHUNK), lambda i: (0, 0, 0))
    vec = pl.BlockSpec((1, BRANCH_W), lambda i: (0, 0))
    rope = pl.BlockSpec((tb, 128), lambda i: (nb - 1 - i, 0))
    blk = pl.BlockSpec((tb, BRANCH_W), lambda i: (nb - 1 - i, 0))
    wide = pl.BlockSpec((tb, 2048), lambda i: (nb - 1 - i, 0))
    j = _job_args(job, 12, 3)
    res = pl.pallas_call(
        _hosting(body, job, 12, 3, 1, nb), name=name, grid=(nb,),
        in_specs=[wide, rope, rope, cmat, cmat, cmat, cmat, vec, vec, blk,
                  pl.BlockSpec((nch, H, CHUNK, CHUNK), lambda i: (nb - 1 - i, 0, 0, 0)), blk] + j["in_specs"],
        out_specs=[wide, vec, vec] + j["out_specs"],
        out_shape=[jax.ShapeDtypeStruct((T, 2048), BF16), jax.ShapeDtypeStruct((1, BRANCH_W), F32),
                   jax.ShapeDtypeStruct((1, BRANCH_W), F32)] + j["out_shape"],
        scratch_shapes=[pltpu.VMEM((H, CHUNK, CHUNK), F32)] + j["scratch"], input_output_aliases=j["aliases"],
        compiler_params=_params(("arbitrary",)),
    )(proj, cosf, sinf, *consts, gn_g, gn_b, raw, states, dout, *j["ins"])
    return res[0], res[1], res[2], list(res[3:])


def _sb_masks():
    row = lax.broadcasted_iota(jnp.int32, (CHUNK, CHUNK), 0)
    lane = lax.broadcasted_iota(jnp.int32, (CHUNK, CHUNK), 1)
    return row, lane


SB_QT = 256
SB_DEAD = -105.0


def _pair(v):
    hi = v.astype(BF16)
    return jnp.concatenate([hi, (v - hi.astype(F32)).astype(BF16)], axis=1)


def _sb_consts():
    r = lax.broadcasted_iota(jnp.int32, (256, 256), 0) & 127
    c = lax.broadcasted_iota(jnp.int32, (256, 256), 1)
    ones = c >= 128
    lane = lax.broadcasted_iota(jnp.int32, (CHUNK, CHUNK), 1)
    return (ones | (r > c)).astype(BF16), (ones | (r >= c)).astype(BF16), (lane < 64, lane >= 64)


def _per_head(x, hms):
    return jnp.concatenate([jnp.where(hm, x, 0.0) for hm in hms], axis=0).astype(BF16)


def _sb_logits(qb, kb2, mask2):
    z = _dot(qb, kb2, NT)
    l1p = jnp.log(1.0 + jnp.exp(-jnp.abs(z)))
    lsp = jnp.minimum(z, 0.0) - l1p
    lsn = lsp - z
    if mask2 is not None:
        lsn = jnp.where(mask2, lsn, 0.0)
    return lsp, lsn


def _below(x, top, new):
    return new if top == 0 else jnp.concatenate([x[:top], new], axis=0)


def _sb_tile_mask(qt):
    trow = lax.broadcasted_iota(jnp.int32, (qt, 256), 0)
    tlane = lax.broadcasted_iota(jnp.int32, (qt, 256), 1) & 127
    return lambda m: (tlane + m * CHUNK) < trow


def sb_fwd(proj, *, name, job=None):
    T = proj.shape[0]
    qt = min(SB_QT, T)
    nsub = qt // CHUNK
    cb = C_SB // 128

    def body(q_ref, k_ref, v_ref, o_ref):
        u_gt, _, hms = _sb_consts()
        tile_mask = _sb_tile_mask(qt)

        def qtile(i, _):
            rq = pl.ds(pl.multiple_of(i * qt, qt), qt)
            qb = (q_ref[rq, :] * SB_SCALE).astype(BF16)

            def group(js, masks, tops, state):
                carry, acc = list(state[:2]), state[2]
                rows = [pl.ds(pl.multiple_of(j * CHUNK, CHUNK), CHUNK) for j in js]
                masks = [None if m is None else m[t:] for m, t in zip(masks, tops)]
                logits = [_sb_logits(qb[t:], _per_head(k_ref[rk, :], hms), m) for rk, m, t in zip(rows, masks, tops)]
                sums = [[_dot(_pair(lsn[:, h * 128:(h + 1) * 128]), u_gt, NN) for h in range(2)] for _, lsn in logits]
                weights = []
                for (lsp, _), r, m, t in zip(logits, sums, masks, tops):
                    a_b = []
                    for h in range(2):
                        hc = slice(h * 128, (h + 1) * 128)
                        a = jnp.exp(lsp[:, hc] + r[h][:, :128] + carry[h][t:])
                        if m is not None:
                            a = jnp.where(m[:, hc], a, 0.0)
                        carry[h] = _below(carry[h], t, carry[h][t:] + r[h][:, 128:])
                        a_b.append(a.astype(BF16))
                    weights.append(jnp.concatenate(a_b, axis=1))
                for rk, a, t in zip(rows, weights, tops):
                    acc = _below(acc, t, acc[t:] + _dot(a, _per_head(v_ref[rk, :], hms), NN))
                return carry[0], carry[1], acc

            zero = jnp.zeros((qt, 128), F32)
            diag = list(reversed(range(nsub)))
            state = group([i * nsub + m for m in diag], [tile_mask(m) for m in diag], [m * CHUNK for m in diag],
                          (zero, zero, zero))

            def live(c):
                return jnp.logical_and(c[0] < i, jnp.maximum(jnp.max(c[1][0]), jnp.max(c[1][1])) > SB_DEAD)

            def blocks(c):
                jj, st = c
                return jj + 1, group([(i - jj) * nsub - 1 - u for u in range(nsub)], [None] * nsub, [0] * nsub, st)

            _, state = lax.while_loop(live, blocks, (jnp.int32(0), state))
            o_ref[rq, :] = state[2]
            return 0

        lax.fori_loop(0, T // qt, qtile, 0)

    def col(off):
        return pl.BlockSpec((T, 128), lambda hp: (0, off + hp))

    steps = BRANCH_W // 128
    j = _job_args(job, 3, 1)
    res = pl.pallas_call(
        _hosting(body, job, 3, 1, 0, steps), name=name, grid=(steps,),
        in_specs=[col(cb), col(cb + 4), col(cb + 8)] + j["in_specs"], out_specs=[col(0)] + j["out_specs"],
        out_shape=[jax.ShapeDtypeStruct((T, BRANCH_W), F32)] + j["out_shape"],
        scratch_shapes=j["scratch"], input_output_aliases=j["aliases"],
        compiler_params=_params(("parallel",) if job is None else ("arbitrary",)),
    )(proj, proj, proj, *j["ins"])
    return res[0], list(res[1:])


def sb_bwd(proj, out, dout, *, name, job=None):
    T = proj.shape[0]
    qt = min(SB_QT, T)
    nsub = qt // CHUNK
    cb = C_SB // 128

    def body(q_ref, k_ref, v_ref, o_ref, do_ref, dq_ref, dk_ref, dv_ref, dkt_ref, dvt_ref):
        u_gt, u_ge, hms = _sb_consts()
        tile_mask = _sb_tile_mask(qt)
        tall_lane = lax.broadcasted_iota(jnp.int32, (qt, 128), 1)
        top = lax.broadcasted_iota(jnp.int32, (CHUNK, CHUNK), 0) < 64
        dkt_ref[...] = jnp.zeros_like(dkt_ref)
        dvt_ref[...] = jnp.zeros_like(dvt_ref)

        def qtile(i, _):
            rq = pl.ds(pl.multiple_of(i * qt, qt), qt)
            qs = q_ref[rq, :] * SB_SCALE
            qb, q_t = qs.astype(BF16), qs.T.astype(BF16)
            dov = do_ref[rq, :]
            dob, do_t = dov.astype(BF16), dov.T.astype(BF16)
            prod = dob.astype(F32) * o_ref[rq, :]
            total = [jnp.broadcast_to(jnp.sum(jnp.where(hm, prod, 0.0), axis=1, keepdims=True), (qt, 128))
                     for hm in (tall_lane < 64, tall_lane >= 64)]

            def group(js, masks, tops, state):
                c_l, c_w, dq = list(state[:2]), list(state[2:4]), state[4]
                heads = [slice(h * 128, (h + 1) * 128) for h in range(2)]
                rows = [pl.ds(pl.multiple_of(j * CHUNK, CHUNK), CHUNK) for j in js]
                masks = [None if m is None else m[t:] for m, t in zip(masks, tops)]
                kb2 = [_per_head(k_ref[rk, :], hms) for rk in rows]
                logits = [_sb_logits(qb[t:], kb, m) for kb, m, t in zip(kb2, masks, tops)]
                da = [_dot(dob[t:], _per_head(v_ref[rk, :], hms), NT) for rk, t in zip(rows, tops)]
                sums = [[_dot(_pair(lsn[:, hc]), u_gt, NN) for hc in heads] for _, lsn in logits]
                a_b, w_all = [], []
                for (lsp, _), r, d, m, t in zip(logits, sums, da, masks, tops):
                    a_h, w_h = [], []
                    for h, hc in enumerate(heads):
                        a = jnp.exp(lsp[:, hc] + r[h][:, :128] + c_l[h][t:])
                        if m is not None:
                            a = jnp.where(m[:, hc], a, 0.0)
                        c_l[h] = _below(c_l[h], t, c_l[h][t:] + r[h][:, 128:])
                        a = a.astype(BF16)
                        a_h.append(a)
                        w_h.append(a.astype(F32) * d[:, hc])
                    a_b.append(jnp.concatenate(a_h, axis=1))
                    w_all.append(w_h)
                sums_w = [[_dot(_pair(w), u_ge, NN) for w in w_h] for w_h in w_all]
                dz_b = []
                for (lsp, _), w_h, r, m, t in zip(logits, w_all, sums_w, masks, tops):
                    sp = jnp.exp(lsp)
                    dz_h = []
                    for h, hc in enumerate(heads):
                        later_w = r[h][:, :128] + c_w[h][t:]
                        c_w[h] = _below(c_w[h], t, c_w[h][t:] + r[h][:, 128:])
                        dz = w_h[h] * (1.0 - sp[:, hc]) - sp[:, hc] * (total[h][t:] - later_w)
                        if m is not None:
                            dz = jnp.where(m[:, hc], dz, 0.0)
                        dz_h.append(dz.astype(BF16))
                    dz_b.append(jnp.concatenate(dz_h, axis=1))
                for j, kb, a, dz, t in zip(js, kb2, a_b, dz_b, tops):
                    dkt = _dot(q_t[:, t:], dz, NN)
                    dvt = _dot(do_t[:, t:], a, NN)
                    dkt_ref[j] += jnp.where(top, dkt[:, :128], dkt[:, 128:])
                    dvt_ref[j] += jnp.where(top, dvt[:, :128], dvt[:, 128:])
                    dq = _below(dq, t, dq[t:] + _dot(dz, kb, NN))
                return c_l[0], c_l[1], c_w[0], c_w[1], dq

            zero = jnp.zeros((qt, 128), F32)
            diag = list(reversed(range(nsub)))
            state = group([i * nsub + m for m in diag], [tile_mask(m) for m in diag], [m * CHUNK for m in diag],
                          (zero,) * 5)

            def live(c):
                return jnp.logical_and(c[0] < i, jnp.maximum(jnp.max(c[1][0]), jnp.max(c[1][1])) > SB_DEAD)

            def blocks(c):
                jj, st = c
                return jj + 1, group([(i - jj) * nsub - 1 - u for u in range(nsub)], [None] * nsub, [0] * nsub, st)

            _, state = lax.while_loop(live, blocks, (jnp.int32(0), state))
            dq_ref[rq, :] = (state[4] * SB_SCALE).astype(BF16)
            return 0

        lax.fori_loop(0, T // qt, qtile, 0)

        def untranspose(jb, _):
            rk = pl.ds(pl.multiple_of(jb * CHUNK, CHUNK), CHUNK)
            dk_ref[rk, :] = dkt_ref[jb].T.astype(BF16)
            dv_ref[rk, :] = dvt_ref[jb].T.astype(BF16)
            return 0

        lax.fori_loop(0, T // CHUNK, untranspose, 0)

    def col(off):
        return pl.BlockSpec((T, 128), lambda hp: (0, off + hp))

    o16 = jax.ShapeDtypeStruct((T, BRANCH_W), BF16)
    steps = BRANCH_W // 128
    j = _job_args(job, 5, 3)
    acc = pltpu.VMEM((T // CHUNK, CHUNK, CHUNK), F32)
    res = pl.pallas_call(
        _hosting(body, job, 5, 3, 2, steps), name=name, grid=(steps,),
        in_specs=[col(cb), col(cb + 4), col(cb + 8), col(0), col(0)] + j["in_specs"],
        out_specs=[col(0), col(0), col(0)] + j["out_specs"], out_shape=[o16, o16, o16] + j["out_shape"],
        scratch_shapes=[acc, acc] + j["scratch"], input_output_aliases=j["aliases"],
        compiler_params=_params(("parallel",) if job is None else ("arbitrary",)),
    )(proj, proj, proj, out, dout, *j["ins"])
    return res[0], res[1], res[2], list(res[3:])


_G0 = math.sqrt(2.0 / math.pi)
_G1 = 0.044715


def _gelu(x):
    return 0.5 * x * (1.0 + jnp.tanh(_G0 * (x + _G1 * x * x * x)))


def _gelu_grad(x):
    t = jnp.tanh(_G0 * (x + _G1 * x * x * x))
    return 0.5 * (1.0 + t) + 0.5 * x * (1.0 - t * t) * (_G0 * (1.0 + 3.0 * _G1 * x * x))


def _tril():
    row, lane = _sb_masks()
    return row >= lane


def sgu_fwd(proj, ln_g, ln_b, w, bias, *, name):
    T = proj.shape[0]
    tb = min(512, T)
    G = BRANCH_W // 128

    def body(u_ref, v_ref, g_ref, b_ref, w_ref, bias_ref, o_ref):
        vv = _gelu(v_ref[...])
        xh, _ = _group_norm(vv)
        vn = (xh * g_ref[...] + b_ref[...]).astype(BF16)
        tril = _tril()
        for g in range(G):
            wg = jnp.where(tril, w_ref[g], 0.0).astype(BF16)
            gc = slice(g * 128, (g + 1) * 128)
            for c in range(tb // CHUNK):
                r = slice(c * CHUNK, (c + 1) * CHUNK)
                sv = _dot(wg, vn[r, gc], NN) + bias_ref[g]
                o_ref[r, gc] = _gelu(u_ref[r, gc]) * sv

    cu, cv = C_SGU // BRANCH_W, C_SGU // BRANCH_W + 1
    vec = pl.BlockSpec((1, BRANCH_W), lambda i: (0, 0))
    mat = pl.BlockSpec((G, CHUNK, CHUNK), lambda i: (0, 0, 0))
    return pl.pallas_call(
        body, name=name, grid=(T // tb,),
        in_specs=[pl.BlockSpec((tb, BRANCH_W), lambda i: (i, cu)), pl.BlockSpec((tb, BRANCH_W), lambda i: (i, cv)),
                  vec, vec, mat, mat],
        out_specs=pl.BlockSpec((tb, BRANCH_W), lambda i: (i, 0)),
        out_shape=jax.ShapeDtypeStruct((T, BRANCH_W), F32),
        compiler_params=_params(("parallel",)),
    )(proj, proj, ln_g, ln_b, w, bias)


def sgu_bwd(proj, ln_g, ln_b, w, bias, dout, *, name):
    T = proj.shape[0]
    tb = min(512, T)
    G = BRANCH_W // 128

    def body(u_ref, v_ref, g_ref, b_ref, w_ref, bias_ref, do_ref, dp_ref, dw_ref, dbias_ref, dg_ref, db_ref, dvn_ref):
        @pl.when(pl.program_id(0) == 0)
        def _():
            dw_ref[...] = jnp.zeros_like(dw_ref)
            dbias_ref[...] = jnp.zeros_like(dbias_ref)
            dg_ref[...] = jnp.zeros_like(dg_ref)
            db_ref[...] = jnp.zeros_like(db_ref)

        gv = v_ref[...]
        vv = _gelu(gv)
        xh, rstd = _group_norm(vv)
        vn = (xh * g_ref[...] + b_ref[...]).astype(BF16)
        tril = _tril()
        for g in range(G):
            wg = jnp.where(tril, w_ref[g], 0.0).astype(BF16)
            gc = slice(g * 128, (g + 1) * 128)
            for c in range(tb // CHUNK):
                r = slice(c * CHUNK, (c + 1) * CHUNK)
                vn_c = vn[r, gc]
                sv = _dot(wg, vn_c, NN) + bias_ref[g]
                gu = u_ref[r, gc]
                d_o = do_ref[r, gc]
                dp_ref[r, gc] = (d_o * sv * _gelu_grad(gu)).astype(BF16)
                dsv = d_o * _gelu(gu)
                dsv_b = dsv.astype(BF16)
                dvn_ref[r, gc] = _dot(wg, dsv_b, TN)
                dw_ref[g] += jnp.where(tril, _dot(dsv_b, vn_c, NT), 0.0)
                dbias_ref[g] += jnp.broadcast_to(jnp.sum(dsv, axis=1, keepdims=True), (CHUNK, CHUNK))
        dvn = dvn_ref[...]
        dg_ref[...] += jnp.sum(dvn * xh, axis=0, keepdims=True)
        db_ref[...] += jnp.sum(dvn, axis=0, keepdims=True)
        dxh = dvn * g_ref[...]
        m1 = jnp.mean(dxh, axis=-1, keepdims=True)
        m2 = jnp.mean(dxh * xh, axis=-1, keepdims=True)
        dp_ref[:, BRANCH_W:2 * BRANCH_W] = (rstd * (dxh - m1 - xh * m2) * _gelu_grad(gv)).astype(BF16)

    cu, cv = C_SGU // BRANCH_W, C_SGU // BRANCH_W + 1
    vec = pl.BlockSpec((1, BRANCH_W), lambda i: (0, 0))
    mat = pl.BlockSpec((G, CHUNK, CHUNK), lambda i: (0, 0, 0))
    blk = pl.BlockSpec((tb, BRANCH_W), lambda i: (i, 0))
    msh = jax.ShapeDtypeStruct((G, CHUNK, CHUNK), F32)
    vsh = jax.ShapeDtypeStruct((1, BRANCH_W), F32)
    return pl.pallas_call(
        body, name=name, grid=(T // tb,),
        in_specs=[pl.BlockSpec((tb, BRANCH_W), lambda i: (i, cu)), pl.BlockSpec((tb, BRANCH_W), lambda i: (i, cv)),
                  vec, vec, mat, mat, blk],
        out_specs=[pl.BlockSpec((tb, 2 * BRANCH_W), lambda i: (i, 0)), mat, mat, vec, vec],
        out_shape=[jax.ShapeDtypeStruct((T, 2 * BRANCH_W), BF16), msh, msh, vsh, vsh],
        scratch_shapes=[pltpu.VMEM((tb, BRANCH_W), F32)],
        compiler_params=_params(("arbitrary",)),
    )(proj, proj, ln_g, ln_b, w, bias, dout)


def merge_fwd(a1, a2, a3, p1, p2, p3, proj, *, name):
    T = a1.shape[0]
    tm, tn = min(1024, T), 512
    gb = C_GATE // tn

    def body(a1_ref, a2_ref, a3_ref, p1_ref, p2_ref, p3_ref, g1_ref, g2_ref, g3_ref, m_ref, r1_ref, r2_ref, r3_ref):
        m = None
        for a_ref, p_ref, g_ref, r_ref in ((a1_ref, p1_ref, g1_ref, r1_ref), (a2_ref, p2_ref, g2_ref, r2_ref),
                                           (a3_ref, p3_ref, g3_ref, r3_ref)):
            r = _dot(a_ref[...].astype(BF16), p_ref[...], NN)
            r_ref[...] = r.astype(r_ref.dtype)
            t = jax.nn.sigmoid(g_ref[...]) * r
            m = t if m is None else m + t
        m_ref[...] = m.astype(m_ref.dtype)

    a_spec = pl.BlockSpec((tm, BRANCH_W), lambda i, j: (i, 0))
    p_spec = pl.BlockSpec((BRANCH_W, tn), lambda i, j: (0, j))
    o_spec = pl.BlockSpec((tm, tn), lambda i, j: (i, j))
    gates = [pl.BlockSpec((tm, tn), functools.partial(lambda i, j, o: (i, o + j), o=gb + 2 * n)) for n in range(3)]
    return pl.pallas_call(
        body, name=name, grid=(T // tm, D_MODEL // tn),
        in_specs=[a_spec, a_spec, a_spec, p_spec, p_spec, p_spec, *gates],
        out_specs=[o_spec] * 4, out_shape=[jax.ShapeDtypeStruct((T, D_MODEL), BF16)] * 4,
        compiler_params=_params(("parallel", "parallel")),
    )(a1, a2, a3, p1, p2, p3, proj, proj, proj)


def _merge_bwd_epi(dm, r1, r2, r3, g1, g2, g3):
    d_r, d_g = [], []
    for r, g in ((r1, g1), (r2, g2), (r3, g3)):
        s = jax.nn.sigmoid(g)
        d_r.append(dm * s)
        d_g.append(dm * r.astype(F32) * (s * (1.0 - s)))
    return (*d_r, *d_g)


def _rows_call(fn, ins, out_dtypes, *, name, tr=256, job=None):
    first = ins[0][0] if isinstance(ins[0], tuple) else ins[0]
    R, C = first.shape[-2:]
    tr = min(tr, R)
    assert R % tr == 0, (name, R, tr)
    arrs, specs = [], []
    for x in ins:
        if isinstance(x, tuple):
            arrs.append(x[0])
            specs.append(pl.BlockSpec((None, tr, C), functools.partial(lambda i, n: (n, i, 0), n=x[1])))
        else:
            arrs.append(x)
            specs.append(pl.BlockSpec((tr, C), lambda i: (i, 0)))
    ni = len(arrs)

    def body(*refs):
        vals = fn(*[r[...] for r in refs[:ni]])
        for o_ref, v in zip(refs[ni:], vals):
            o_ref[...] = v.astype(o_ref.dtype)

    no = len(out_dtypes)
    j = _job_args(job, ni, no)
    res = pl.pallas_call(
        _hosting(body, job, ni, no, 0, R // tr), name=name, grid=(R // tr,), in_specs=specs + j["in_specs"],
        out_specs=[pl.BlockSpec((tr, C), lambda i: (i, 0)) for _ in out_dtypes] + j["out_specs"],
        out_shape=[jax.ShapeDtypeStruct((R, C), dt) for dt in out_dtypes] + j["out_shape"],
        scratch_shapes=j["scratch"], input_output_aliases=j["aliases"],
        compiler_params=_params(("parallel",) if job is None else ("arbitrary",)),
    )(*arrs, *j["ins"])
    return list(res) if job is None else (list(res[:no]), list(res[no:]))


def _tile_rows(rows, cols):
    t = 256
    while t > 8 and (t * cols > 512 * 1024 or rows % t):
        t //= 2
    return t


def _rows_at(fn, pos, ins, outs, steps, *, name, aliases=None):
    read = [n for n, (_, s) in enumerate(ins) if s is not ANY]
    ni = len(ins)

    def body(pos_ref, *refs):
        vals = fn(*[refs[n][...] for n in read])
        for o_ref, v in zip(refs[ni:], vals):
            o_ref[...] = v.astype(o_ref.dtype)

    return pl.pallas_call(
        body, name=name,
        grid_spec=pltpu.PrefetchScalarGridSpec(num_scalar_prefetch=1, grid=(steps,), in_specs=[s for _, s in ins],
                                               out_specs=[s for _, s in outs]),
        out_shape=[sh for sh, _ in outs],
        input_output_aliases={1 + i: o for i, o in (aliases or {}).items()},
        compiler_params=_params(("parallel",)),
    )(pos, *[a for a, _ in ins])


def cast_into_whole(pos, w, l, axis, *, name):
    _, r, n = w.shape
    tr = _tile_rows(r, n)
    if axis == 1:
        shape, spec = (r, n * N_CHIPS), pl.BlockSpec((tr, n), lambda i, p: (i, p[3]))
    else:
        shape, spec = (r * N_CHIPS, n), pl.BlockSpec((tr, n), lambda i, p: (p[3] * (r // tr) + i, 0))
    return _rows_at(lambda a: (a,), pos, [(w, pl.BlockSpec((None, tr, n), lambda i, p: (l, i, 0)))],
                    [(jax.ShapeDtypeStruct(shape, BF16), spec)], r // tr, name=name)[0]


def pair_sum(pos, theirs, g32, axis, *, name):
    rows2, cols = theirs.shape
    h = rows2 // (N_CHIPS if axis == 0 else 1)
    tr = _tile_rows(h, cols)
    hb = h // tr
    if axis == 1:
        own = pl.BlockSpec((tr, cols), lambda i, p: (p[2] * hb + i, 0))
    else:
        own = pl.BlockSpec((tr, cols), lambda i, p: ((2 * (i // hb) + p[2]) * hb + i % hb, 0))
    row = pl.BlockSpec((tr, cols), lambda i, p: (i, 0))
    return _rows_at(lambda t, m: (m + t.astype(F32),) * 2, pos, [(theirs, row), (g32, own)],
                    [(jax.ShapeDtypeStruct((rows2, cols), F32), row), (jax.ShapeDtypeStruct((rows2, cols), BF16), row)],
                    rows2 // tr, name=name)


def chip_sum(pos, h32, recv, l, axis, whole, *, name):
    _, depth, h, n = recv.shape
    tr = _tile_rows(h, n)
    hb = h // tr
    if axis == 1:
        mine = pl.BlockSpec((tr, n), lambda i, p: (i, p[3]))
    else:
        mine = pl.BlockSpec((tr, n), lambda i, p: (p[3] * hb + i, 0))
    ins = [(h32, mine)] + [(recv, pl.BlockSpec((None, None, tr, n), functools.partial(lambda i, p, j: (j, l, i, 0), j=j)))
                           for j in range(3)]
    if whole is not None:
        ins.append((whole, ANY))
    return _rows_at(lambda o, a, b, c: (((o + a.astype(F32)) + b.astype(F32)) + c.astype(F32),), pos, ins,
                    [(jax.ShapeDtypeStruct((depth, 2, h, n), F32), pl.BlockSpec((None, None, tr, n), lambda i, p: (l, p[2], i, 0)))],
                    hb, name=name, aliases=None if whole is None else {4: 0})[0]


def _adamw(w, g, m, v):
    m2 = ADAM_B1 * m + (1.0 - ADAM_B1) * g
    v2 = ADAM_B2 * v + (1.0 - ADAM_B2) * (g * g)
    m_hat = m2 / (1.0 - ADAM_B1 ** ADAM_STEP)
    v_hat = v2 / (1.0 - ADAM_B2 ** ADAM_STEP)
    delta = -ADAM_LR * (m_hat / (jnp.sqrt(v_hat) + ADAM_EPS) + ADAM_WD * w)
    return delta, m2, v2


def _place():
    return lax.axis_index("x"), lax.axis_index("y"), lax.axis_index("c")


def _chip_peers(x, y, c):
    return [((1 - x, y, c), 2 * (1 - x) + y), ((x, 1 - y, c), 2 * x + 1 - y), ((1 - x, 1 - y, c), 2 * (1 - x) + 1 - y)]


def _shard_of(ref, axis, k, n):
    start = pl.multiple_of(k * n, 128)
    return ref.at[pl.ds(start, n), :] if axis == 0 else ref.at[:, pl.ds(start, n)]


ANY = pl.BlockSpec(memory_space=pl.ANY)


class CopyJob:
    def __init__(self, ins, out_shape, scratch, copies, aliases=None):
        self.ins, self.out_shape, self.scratch, self.copies = list(ins), list(out_shape), list(scratch), copies
        self.aliases = dict(aliases or {})

    def start(self, ins, outs, sems):
        local, remote, _, _ = self.copies(ins, outs, sems)
        for d in local + remote:
            d.start()

    def finish(self, ins, outs, sems):
        local, remote, arrivals, relays = self.copies(ins, outs, sems)
        for needs, sends, _ in relays:
            for d in needs:
                d.wait_recv()
            for d in sends:
                d.start()
        for d in arrivals + [d for _, _, arrives in relays for d in arrives]:
            d.wait_recv()
        for d in remote + [d for _, sends, _ in relays for d in sends]:
            d.wait_send()
        for d in local:
            d.wait()


def run_job(job, *, name):
    ni, no = len(job.ins), len(job.out_shape)

    def body(*refs):
        parts = refs[:ni], refs[ni:ni + no], refs[ni + no:]
        job.start(*parts)
        job.finish(*parts)

    return pl.pallas_call(
        body, name=name, in_specs=[ANY] * ni, out_specs=[ANY] * no, out_shape=job.out_shape,
        scratch_shapes=job.scratch, input_output_aliases=job.aliases,
    )(*job.ins)


def _job_args(job, n_in, n_out):
    if job is None:
        return dict(ins=[], in_specs=[], out_specs=[], out_shape=[], scratch=[], aliases={})
    return dict(ins=job.ins, in_specs=[ANY] * len(job.ins), out_specs=[ANY] * len(job.out_shape),
                out_shape=job.out_shape, scratch=job.scratch,
                aliases={n_in + i: n_out + o for i, o in job.aliases.items()})


def _hosting(body, job, n_in, n_out, n_scratch, grid):
    if job is None:
        return body
    ji, jo = len(job.ins), len(job.out_shape)
    grid = (grid,) if isinstance(grid, int) else tuple(grid)

    def at(ends):
        hit = None
        for ax, e in enumerate(ends):
            here = pl.program_id(ax) == e
            hit = here if hit is None else jnp.logical_and(hit, here)
        return hit

    def hosted(*refs):
        o = n_in + ji
        s = o + n_out + jo
        parts = refs[n_in:o], refs[o + n_out:s], refs[s + n_scratch:]

        @pl.when(at([0] * len(grid)))
        def _():
            job.start(*parts)

        body(*refs[:n_in], *refs[o:o + n_out], *refs[s:s + n_scratch])

        @pl.when(at([g - 1 for g in grid]))
        def _():
            job.finish(*parts)

    return hosted


def _job_sems(n_remote, n_local):
    return [pltpu.SemaphoreType.DMA((n_remote,)), pltpu.SemaphoreType.DMA((n_remote,)), pltpu.SemaphoreType.DMA((n_local,))]


def gather_job(shards, axes, chips=(0, 1, 2)):
    na = len(shards)

    def copies(ins, outs, sems):
        send, recv, _ = sems
        x, y, c = _place()
        k = 2 * x + y
        remote, relays = [], []
        for a in range(na):
            r = outs[a].shape[0] // (N_CHIPS if axes[a] == 0 else 1)
            n = outs[a].shape[axes[a]] // N_CHIPS
            half = r // 2

            def part(kk, cc, a=a, n=n, half=half):
                rows = pl.ds(pl.multiple_of(cc * half + (kk * n if axes[a] == 0 else 0), 8), half)
                return outs[a].at[rows, :] if axes[a] == 0 else outs[a].at[rows, pl.ds(pl.multiple_of(kk * n, 128), n)]

            needs, passes, lands = [], [], []
            for j, (peer, kp) in enumerate(_chip_peers(x, y, c)):
                if j not in chips:
                    continue
                s = 6 * a + j
                remote.append(pltpu.make_async_remote_copy(part(k, c), part(k, c), send.at[s], recv.at[s],
                                                           device_id=peer, device_id_type=MESH))
                needs.append(pltpu.make_async_remote_copy(part(kp, c), part(kp, c), send.at[s], recv.at[s],
                                                          device_id=peer, device_id_type=MESH))
                passes.append(pltpu.make_async_remote_copy(part(kp, c), part(kp, c), send.at[s + 3], recv.at[s + 3],
                                                           device_id=(x, y, 1 - c), device_id_type=MESH))
                lands.append(pltpu.make_async_remote_copy(part(kp, 1 - c), part(kp, 1 - c), send.at[s + 3], recv.at[s + 3],
                                                          device_id=(x, y, 1 - c), device_id_type=MESH))
            relays.append((needs, passes, lands))
        return [], remote, [], relays

    out_shape = [jax.ShapeDtypeStruct(w.shape, BF16) for w in shards]
    return CopyJob(shards, out_shape, _job_sems(6 * na, 1), copies, {a: a for a in range(na)})


def scatter_job(layers, g16, axes, filled, chips=(0, 1, 2)):
    na = len(axes)

    def shard_shape(a):
        r, c = g16[a].shape
        return (r // N_CHIPS, c) if axes[a] == 0 else (r, c // N_CHIPS)

    def copies(ins, outs, sems):
        send, recv_sems, _ = sems
        x, y, c = _place()
        remote = []
        for a in range(na):
            n = shard_shape(a)[axes[a]]
            for r, (peer, kp) in enumerate(_chip_peers(x, y, c)):
                if r not in chips:
                    continue
                remote.append(pltpu.make_async_remote_copy(_shard_of(ins[a], axes[a], kp, n), outs[a].at[r, layers[a]],
                                                           send.at[3 * a + r], recv_sems.at[3 * a + r],
                                                           device_id=peer, device_id_type=MESH))
        return [], remote, remote, []

    out_shape = [jax.ShapeDtypeStruct((3, DEPTH) + shard_shape(a), BF16) for a in range(na)]
    ins = list(g16)
    aliases = {}
    for a in range(na):
        if filled[a] is not None:
            aliases[len(ins)] = a
            ins.append(filled[a])
    return CopyJob(ins, out_shape, _job_sems(3 * na, 1), copies, aliases)


def pair_job(g16, axes):
    na = len(axes)
    pieces = [1 if ax == 1 else N_CHIPS for ax in axes]

    def copies(ins, outs, sems):
        send, recv, _ = sems
        x, y, c = _place()
        remote = []
        s = 0
        for a in range(na):
            rows = g16[a].shape[0] // (2 * pieces[a])
            for kk in range(pieces[a]):
                src = ins[a].at[pl.ds(pl.multiple_of((2 * kk + 1 - c) * rows, 8), rows), :]
                remote.append(pltpu.make_async_remote_copy(src, outs[a].at[pl.ds(kk * rows, rows), :], send.at[s], recv.at[s],
                                                           device_id=(x, y, 1 - c), device_id_type=MESH))
                s += 1
        return [], remote, remote, []

    out_shape = [jax.ShapeDtypeStruct((g.shape[0] // 2, g.shape[1]), BF16) for g in g16]
    return CopyJob(g16, out_shape, _job_sems(sum(pieces), 1), copies)


def join_job(shards):
    na = len(shards)

    def copies(ins, outs, sems):
        send, recv, _ = sems
        x, y, c = _place()
        remote = [pltpu.make_async_remote_copy(outs[a].at[:, c], outs[a].at[:, c], send.at[a], recv.at[a],
                                               device_id=(x, y, 1 - c), device_id_type=MESH) for a in range(na)]
        lands = [pltpu.make_async_remote_copy(outs[a].at[:, 1 - c], outs[a].at[:, 1 - c], send.at[a], recv.at[a],
                                              device_id=(x, y, 1 - c), device_id_type=MESH) for a in range(na)]
        return [], remote, lands, []

    out_shape = [jax.ShapeDtypeStruct(s.shape, F32) for s in shards]
    return CopyJob(shards, out_shape, _job_sems(na, 1), copies, {a: a for a in range(na)})


def small_job(p):
    def copies(ins, outs, sems):
        send, recv, loc = sems
        x, y, c = _place()
        me = 4 * x + 2 * y + c
        remote, lands = [], []
        for rel in range(1, 8):
            dx, dy, dc = rel >> 2, (rel >> 1) & 1, rel & 1
            peer = (1 - x if dx else x, 1 - y if dy else y, 1 - c if dc else c)
            who = 4 * peer[0] + 2 * peer[1] + peer[2]
            remote.append(pltpu.make_async_remote_copy(ins[0], outs[0].at[me], send.at[rel - 1], recv.at[rel - 1],
                                                       device_id=peer, device_id_type=MESH))
            lands.append(pltpu.make_async_remote_copy(ins[0], outs[0].at[who], send.at[rel - 1], recv.at[rel - 1],
                                                      device_id=peer, device_id_type=MESH))
        return [pltpu.make_async_copy(ins[0], outs[0].at[me], loc.at[0])], remote, lands, []

    return CopyJob([p], [jax.ShapeDtypeStruct((8,) + p.shape, F32)], _job_sems(7, 1), copies)


def small_sum(slots):
    def add(*terms):
        acc = terms[0]
        for t in terms[1:]:
            acc = acc + t
        return (acc,)

    return _rows_call(add, [(slots, d) for d in range(8)], [F32], name="small_sum", tr=8 * 47)[0]


BIG = ("w_in", "p_ret", "p_sb", "p_sgu", "w_out", "w_up", "w_down")
BIG_AXIS = {"w_in": 1, "p_ret": 1, "p_sb": 1, "p_sgu": 1, "w_out": 0, "w_up": 1, "w_down": 0}
SMALL = ("ret_gn_g", "ret_gn_b", "sgu_ln_g", "sgu_ln_b", "sgu_w", "sgu_b", "ln1_g", "ln1_b", "ln2_g", "ln2_b")


def layer_forward(l, x0, x0h, W, sm, rope, rconsts, hooks):
    n = f"l{l}_"
    job = hooks.fwd_job(l, "proj")
    proj = matmul(x0h, W["w_in"], mode="nn", tm=4096, tn=768, tk=1024, name=n + "proj", job=job)
    if job is not None:
        proj, job_out = proj
        hooks.done(job, job_out)
    retg, raw, states = ret_fwd(proj, *rope, rconsts, sm["ret_gn_g"], sm["ret_gn_b"], name=n + "ret_fwd")
    job = hooks.fwd_job(l, "sb")
    sb, job_out = sb_fwd(proj, name=n + "sb_fwd", job=job)
    if job is not None:
        hooks.done(job, job_out)
    sg = sgu_fwd(proj, sm["sgu_ln_g"], sm["sgu_ln_b"], sm["sgu_w"], sm["sgu_bias"], name=n + "sgu_fwd")
    merged, r1, r2, r3 = merge_fwd(retg, sb, sg, W["p_ret"], W["p_sb"], W["p_sgu"], proj, name=n + "merge_fwd")
    x1, xh1, rs1, x1h = matmul_ln(merged, W["w_out"], x0, sm["ln1_g"], sm["ln1_b"], tk=1024, name=n + "out_ln1")
    job = hooks.fwd_job(l, "up")
    h1 = matmul(x1h, W["w_up"], mode="nn", tm=1024, tn=1024, tk=1024, outs=((BF16, None),), name=n + "up", job=job)
    if job is not None:
        h1, job_out = h1
        hooks.done(job, job_out)
    job = hooks.fwd_job(l, "down")
    res = matmul_ln(h1, W["w_down"], x1, sm["ln2_g"], sm["ln2_b"], pro=_relu2, tk=1024, name=n + "down_ln2", job=job)
    if job is not None:
        res, job_out = res
        hooks.done(job, job_out)
    x2, xh2, rs2, x2h = res
    saved = dict(x0h=x0h, proj=proj, retg=retg, raw=raw, states=states, sb=sb, sg=sg, merged=merged, r=(r1, r2, r3),
                 x1h=x1h, xh1=xh1, rs1=rs1, h1=h1, xh2=xh2, rs2=rs2)
    return x2, x2h, saved


def layer_backward(l, dx2, s, W, sm, rope, rconsts, hooks):
    n = f"l{l}_"
    two = ((F32, None), (BF16, None))
    gw, gs = {}, {}
    job = hooks.bwd_job(l, "ln2")
    res = ln_bwd(dx2, s["xh2"], s["rs2"], sm["ln2_g"], name=n + "ln2_bwd", job=job)
    if job is not None:
        res, job_out = res
        hooks.done(job, job_out)
    du2, du2h, gs["ln2_g"], gs["ln2_b"] = res
    job = hooks.bwd_job(l, "g_down")
    gw["w_down"] = matmul(s["h1"], du2h, mode="tn", tm=1024, tn=1024, tk=4096, pro=_relu2, outs=two, name=n + "g_down", job=job)
    if job is not None:
        gw["w_down"], job_out = gw["w_down"]
        hooks.done(job, job_out)
    dh1 = matmul(du2h, W["w_down"], mode="nt", tm=1024, tn=1024, tk=1024, outs=((BF16, None),),
                 epi=lambda acc, h: (acc * (2.0 * jnp.maximum(h.astype(F32), 0.0)),), tiles=(s["h1"],), name=n + "d_h1")
    job = hooks.bwd_job(l, "g_up")
    gw["w_up"] = matmul(s["x1h"], dh1, mode="tn", tm=1024, tn=1024, tk=4096, outs=two, name=n + "g_up", job=job)
    if job is not None:
        gw["w_up"], job_out = gw["w_up"]
        hooks.done(job, job_out)
    dx1 = matmul(dh1, W["w_up"], mode="nt", tm=1024, tn=1024, tk=4096,
                 epi=lambda acc, d: (acc + ALPHA * d,), tiles=(du2,), name=n + "d_x1")
    du1, du1h, gs["ln1_g"], gs["ln1_b"] = ln_bwd(dx1, s["xh1"], s["rs1"], sm["ln1_g"], name=n + "ln1_bwd")
    gw["w_out"] = matmul(s["merged"], du1h, mode="tn", tm=1024, tn=1024, tk=4096, outs=two, name=n + "g_out")
    gate0 = C_GATE // 512
    dr1, dr2, dr3, dg1, dg2, dg3 = matmul(
        du1h, W["w_out"], mode="nt", tm=1024, tn=512, tk=1024, outs=((BF16, None),) * 6, epi=_merge_bwd_epi,
        tiles=(*s["r"], (s["proj"], gate0), (s["proj"], gate0 + 2), (s["proj"], gate0 + 4)), name=n + "d_merged")
    d_branch = {}
    for nm, a, dr in (("p_ret", s["retg"], dr1), ("p_sb", s["sb"], dr2), ("p_sgu", s["sg"], dr3)):
        gw[nm] = matmul(a, dr, mode="tn", tm=512, tn=1024, tk=2048, outs=two, name=n + "g_" + nm)
        d_branch[nm] = matmul(dr, W[nm], mode="nt", tm=1024, tn=512, tk=1024, name=n + "d_" + nm)
    job = hooks.pair(l, gw)
    dret, gs["ret_gn_g"], gs["ret_gn_b"], job_out = ret_bwd(s["proj"], *rope, rconsts, sm["ret_gn_g"], sm["ret_gn_b"],
                                                             s["raw"], s["states"], d_branch["p_ret"], name=n + "ret_bwd", job=job)
    if job is not None:
        hooks.done(job, job_out)
    job = hooks.scatter(l) if job is not None else None
    dsq, dsk, dsv, job_out = sb_bwd(s["proj"], s["sb"], d_branch["p_sb"], name=n + "sb_bwd", job=job)
    if job is not None:
        hooks.done(job, job_out)
    dsgu, gs["sgu_w"], dbias, gs["sgu_ln_g"], gs["sgu_ln_b"] = sgu_bwd(
        s["proj"], sm["sgu_ln_g"], sm["sgu_ln_b"], sm["sgu_w"], sm["sgu_bias"], d_branch["p_sgu"], name=n + "sgu_bwd")
    gs["sgu_b"] = dbias[:, :, 0]
    dproj = jnp.concatenate([dret, dsq, dsk, dsv, dsgu, dg1, dg2, dg3], axis=1)
    job = hooks.small(l, gs)
    gw["w_in"] = matmul(s["x0h"], dproj, mode="tn", tm=1024, tn=1536, tk=2048, outs=two, name=n + "g_in", job=job)
    if job is not None:
        gw["w_in"], job_out = gw["w_in"]
        hooks.done(job, job_out)
    job = hooks.tail(l, gw["w_in"])
    dx0 = matmul(dproj, W["w_in"], mode="nt", tm=1024, tn=1024, tk=2560,
                 epi=lambda acc, d: (acc + ALPHA * d,), tiles=(du1,), name=n + "d_x0", job=job)
    if job is not None:
        dx0, job_out = dx0
        hooks.done(job, job_out)
    return dx0, gw, gs


def local_step(x, target, small, plan):
    T = x.shape[0]
    rope = _rope_tables(T)
    rconsts = _ret_consts()
    sms = []
    for l in range(DEPTH):
        sm = {k: small[k][l][None, :] for k in SMALL if k not in ("sgu_w", "sgu_b")}
        sm["sgu_w"] = small["sgu_w"][l]
        sm["sgu_bias"] = jnp.broadcast_to(small["sgu_b"][l][:, :, None], (4, CHUNK, CHUNK))
        sms.append(sm)
    h, saved = x, []
    job = plan.first_job()
    hh = _rows_call(lambda a: (a,), [x], [BF16], name="cast_x", job=job)
    if job is not None:
        hh, job_out = hh
        plan.done(job, job_out)
    hh = hh[0]
    for l in range(DEPTH):
        h, hh, s = layer_forward(l, h, hh, plan.weights(l), sms[l], rope, rconsts, plan)
        saved.append(s)
    dy, sq = loss_head(h, target)
    gs = {k: [None] * DEPTH for k in SMALL}
    for l in reversed(range(DEPTH)):
        dy, gwl, gsl = layer_backward(l, dy, saved[l], plan.weights(l), sms[l], rope, rconsts, plan)
        plan.grads(l, gwl)
        for k in SMALL:
            gs[k][l] = gsl[k].reshape(small[k].shape[1:])
    return sq[0, 0], dy, {k: jnp.stack(v) for k, v in gs.items()}


EARLY_GRADS = ("p_ret", "p_sb", "p_sgu", "w_out", "w_up", "w_down")


class _StepPlan:
    def __init__(self, pos, shards16):
        self.pos = pos
        self.shards16 = shards16
        self.full = [dict() for _ in range(DEPTH)]
        self.gw = [None] * DEPTH
        self.bufs = {}
        self.sums = {}
        self.gs = [None] * DEPTH

    def first_job(self):
        return self._gather([(0, "w_in")])

    def weights(self, l):
        return self.full[l]

    def grads(self, l, gw):
        self.gw[l] = gw

    def _gather(self, items, chips=(0, 1, 2)):
        job = gather_job([self.shards16[l][k] for l, k in items], [BIG_AXIS[k] for _, k in items], chips)
        job.note = ("gather" if 2 in chips else "gather_part", items)
        return job

    def _pair(self, items):
        job = pair_job([g[1] for _, _, g in items], [BIG_AXIS[k] for _, k, _ in items])
        job.note = ("pair", items)
        return job

    def fwd_job(self, l, host):
        if host == "proj":
            return None
        if host == "sb":
            return self._gather([(l, k) for k in BIG[1:]])
        if l + 1 == DEPTH:
            return None
        return self._gather([(l + 1, "w_in")], (0, 1) if host == "up" else (2,))

    def bwd_job(self, l, host):
        if l + 1 == DEPTH:
            return None
        if host == "ln2":
            job = self._pair([(l + 1, "w_in", self.gw[l + 1]["w_in"])])
            job.note = ("pair_w_in", job.note[1])
            return job
        items, sums16 = self.summed_w_in
        job = scatter_job([l_ for l_, _, _ in items], sums16, [BIG_AXIS[k] for _, k, _ in items],
                          [self.bufs.get(k) for _, k, _ in items], (0, 1) if host == "g_down" else (2,))
        job.note = ("scatter", items)
        return job

    def pair(self, l, ready):
        return self._pair([(l, k, ready[k]) for k in EARLY_GRADS])

    def scatter(self, l):
        items, sums16 = self.summed
        job = scatter_job([l_ for l_, _, _ in items], sums16, [BIG_AXIS[k] for _, k, _ in items],
                          [self.bufs.get(k) for _, k, _ in items])
        job.note = ("scatter", items)
        return job

    def small(self, l, gs):
        self.gs[l] = {k: gs[k].reshape(-1) for k in SMALL}
        if l != 0:
            return None
        job = small_job(_pack_small({k: jnp.stack([self.gs[l_][k] for l_ in range(DEPTH)]) for k in SMALL}))
        job.note = ("small", [])
        return job

    def tail(self, l, g):
        if l != 0:
            return None
        last = self._pair([(0, "w_in", g)])
        self.done(last, run_job(last, name="pair_last"))
        return self.scatter(0)

    def done(self, job, outs):
        kind, items = job.note
        if kind == "small":
            self.small_slots = outs[0]
        if kind in ("pair", "pair_w_in"):
            sums16 = []
            for a, (l, k, g) in enumerate(items):
                self.sums[(l, k)], s16 = pair_sum(self.pos, outs[a], g[0], BIG_AXIS[k], name=f"pair_sum_{k}_{l}")
                sums16.append(s16)
            if kind == "pair":
                self.summed = (items, sums16)
            else:
                self.summed_w_in = (items, sums16)
        for a, item in enumerate(items):
            if kind == "gather_part":
                self.shards16[item[0]][item[1]] = outs[a]
            elif kind == "gather":
                self.full[item[0]][item[1]] = outs[a]
            elif kind == "scatter":
                self.bufs[item[1]] = outs[a]

    def finish(self):
        return self.bufs, self.sums


def _flat2(a):
    return a.reshape(-1, a.shape[-1])


def _pack_small(d, pre=""):
    return jnp.concatenate([d[pre + k].reshape(-1) for k in SMALL]).reshape(-1, 128)


def kernel(x, w_in, ret_gn_g, ret_gn_b, sgu_ln_g, sgu_ln_b, sgu_w, sgu_b, p_ret, p_sb, p_sgu, w_out, ln1_g, ln1_b, w_up, w_down, ln2_g, ln2_b, loss_target, m_w_in, m_ret_gn_g, m_ret_gn_b, m_sgu_ln_g, m_sgu_ln_b, m_sgu_w, m_sgu_b, m_p_ret, m_p_sb, m_p_sgu, m_w_out, m_ln1_g, m_ln1_b, m_w_up, m_w_down, m_ln2_g, m_ln2_b, v_w_in, v_ret_gn_g, v_ret_gn_b, v_sgu_ln_g, v_sgu_ln_b, v_sgu_w, v_sgu_b, v_p_ret, v_p_sb, v_p_sgu, v_w_out, v_ln1_g, v_ln1_b, v_w_up, v_w_down, v_ln2_g, v_ln2_b):
    given = dict(locals())
    order = BIG[:1] + SMALL[:6] + BIG[1:5] + SMALL[6:8] + BIG[5:7] + SMALL[8:10]
    L = DEPTH

    px, py, pc = _place()
    pos = jnp.stack([px, py, pc, 2 * px + py]).astype(jnp.int32)

    shards16 = [{k: cast_into_whole(pos, given[k], l, BIG_AXIS[k], name=f"cast_{k}_{l}") for k in BIG} for l in range(L)]
    plan = _StepPlan(pos, shards16)
    sq, dx, _ = local_step(x[0], loss_target[0], {k: given[k] for k in SMALL}, plan)
    loss = 0.5 * lax.psum(sq, ("x", "y", "c"))

    bufs, sums = plan.finish()
    shards = []
    for k in BIG:
        whole = None
        for l in range(L):
            whole = chip_sum(pos, sums[(l, k)], bufs[k], l, BIG_AXIS[k], whole, name=f"chip_sum_{k}_{l}")
        shards.append(whole)
    joined = run_job(join_job(shards), name="join_halves")
    out = {}
    for a, k in enumerate(BIG):
        shp = given[k].shape
        res = _rows_call(lambda g_, w_, m_, v_: (g_,) + _adamw(w_, g_, m_, v_),
                         [joined[a].reshape(-1, shp[-1]), _flat2(given[k]), _flat2(given["m_" + k]), _flat2(given["v_" + k])],
                         [F32] * 4, name="adamw_" + k)
        out[k] = [r.reshape(shp) for r in res]

    pack = _pack_small
    res = _rows_call(lambda g_, w_, m_, v_: (g_,) + _adamw(w_, g_, m_, v_),
                     [small_sum(plan.small_slots), pack(given), pack(given, "m_"), pack(given, "v_")], [F32] * 4,
                     name="adamw_small", tr=8 * 47)
    off = 0
    for k in SMALL:
        sz = given[k].size
        out[k] = [r.reshape(-1)[off:off + sz].reshape(given[k].shape) for r in res]
        off += sz

    grads = [out[k][0] for k in order]
    deltas = [out[k][1] for k in order]
    new_m = [out[k][2] for k in order]
    new_v = [out[k][3] for k in order]
    return (loss, dx[None], *grads, *deltas, *new_m, *new_v)
```

```python
import functools
import math

import jax
import jax.numpy as jnp
from jax import lax
from jax.experimental import pallas as pl
from jax.experimental.pallas import tpu as pltpu

F32 = jnp.float32
BF16 = jnp.bfloat16

D_MODEL = 1024
SEQ = 4096
DEPTH = 2
CHUNK = 128
RET_HEADS = 4
BRANCH_W = 512
N_IN = 7680
D_FF = 4096
LN_EPS = 1e-5
ROPE_BASE = 10000.0
ALPHA = (2 * DEPTH) ** 0.25
RET_SCALE = 128 ** -0.5
SB_SCALE = 64 ** -0.5
C_RET, C_SB, C_SGU, C_GATE = 0, 2048, 3584, 4608

ADAM_LR, ADAM_B1, ADAM_B2, ADAM_EPS, ADAM_WD, ADAM_STEP = 0.001, 0.9, 0.999, 1e-08, 0.01, 10

N_CHIPS = 4
VMEM_LIMIT = 56 * 1024 * 1024
MESH = pl.DeviceIdType.MESH

NN = ((1,), (0,))
NT = ((1,), (1,))
TN = ((0,), (0,))


def _dot(a, b, dims):
    return lax.dot_general(a, b, (dims, ((), ())), preferred_element_type=F32)


def _params(sem):
    return pltpu.CompilerParams(dimension_semantics=sem, vmem_limit_bytes=VMEM_LIMIT)


def _relu2(h):
    r = jnp.maximum(h.astype(F32), 0.0)
    return r * r


def matmul(a, b, *, mode, tm, tn, tk, outs=((F32, None),), pro=None, epi=None, tiles=(), rows=(), name, job=None):
    if mode == "nn":
        (M, K), N = a.shape, b.shape[1]
    elif mode == "nt":
        (M, K), N = a.shape, b.shape[0]
    else:
        (K, M), N = a.shape, b.shape[1]
    tm, tn, tk = min(tm, M), min(tn, N), min(tk, K)
    assert M % tm == 0 and N % tn == 0 and K % tk == 0, (name, M, N, K, tm, tn, tk)
    if mode == "nn":
        a_spec = pl.BlockSpec((tm, tk), lambda i, j, k: (i, k))
        b_spec = pl.BlockSpec((tk, tn), lambda i, j, k: (k, j))
        dims = NN
    elif mode == "nt":
        a_spec = pl.BlockSpec((tm, tk), lambda i, j, k: (i, k))
        b_spec = pl.BlockSpec((tn, tk), lambda i, j, k: (j, k))
        dims = NT
    else:
        a_spec = pl.BlockSpec((tk, tm), lambda i, j, k: (k, i))
        b_spec = pl.BlockSpec((tk, tn), lambda i, j, k: (k, j))
        dims = TN
    nk = K // tk
    nt_, nr, no = len(tiles), len(rows), len(outs)

    def body(a_ref, b_ref, *rest):
        tile_refs = rest[:nt_]
        row_refs = rest[nt_:nt_ + nr]
        out_refs = rest[nt_ + nr:nt_ + nr + no]
        av = a_ref[...]
        if pro is not None:
            av = pro(av)
        p = _dot(av.astype(BF16), b_ref[...].astype(BF16), dims)

        def finish(acc):
            vals = (acc,) * no if epi is None else epi(acc, *[r[...] for r in tile_refs], *[r[...] for r in row_refs])
            for o_ref, v in zip(out_refs, vals):
                o_ref[...] = v.astype(o_ref.dtype)

        if nk == 1:
            finish(p)
        else:
            acc_ref = rest[-1]
            k = pl.program_id(2)

            @pl.when(k == 0)
            def _():
                acc_ref[...] = p

            @pl.when(k > 0)
            def _():
                acc_ref[...] += p

            @pl.when(k == nk - 1)
            def _():
                finish(acc_ref[...])

    out_shape, out_specs = [], []
    for dt, width in outs:
        if width is None:
            out_shape.append(jax.ShapeDtypeStruct((M, N), dt))
            out_specs.append(pl.BlockSpec((tm, tn), lambda i, j, k: (i, j)))
        else:
            assert N == tn
            out_shape.append(jax.ShapeDtypeStruct((M, width), dt))
            out_specs.append(pl.BlockSpec((tm, width), lambda i, j, k: (i, 0)))
    in_specs = [a_spec, b_spec]
    offs = [t[1] if isinstance(t, tuple) else 0 for t in tiles]
    tiles = [t[0] if isinstance(t, tuple) else t for t in tiles]
    in_specs += [pl.BlockSpec((tm, tn), functools.partial(lambda i, j, k, o: (i, j + o), o=o)) for o in offs]
    in_specs += [pl.BlockSpec((1, tn), lambda i, j, k: (0, j)) for _ in rows]
    grid = (M // tm, N // tn, nk)
    scratch = [pltpu.VMEM((tm, tn), F32)] if nk > 1 else []
    j = _job_args(job, len(in_specs), no)
    res = pl.pallas_call(
        _hosting(body, job, len(in_specs), no, len(scratch), grid), name=name, grid=grid,
        in_specs=in_specs + j["in_specs"], out_specs=out_specs + j["out_specs"], out_shape=out_shape + j["out_shape"],
        scratch_shapes=scratch + j["scratch"], input_output_aliases=j["aliases"],
        compiler_params=_params(("parallel", "parallel", "arbitrary") if job is None else ("arbitrary",) * 3),
    )(a, b, *tiles, *rows, *j["ins"])
    mine = res[0] if no == 1 else list(res[:no])
    return mine if job is None else (mine, list(res[no:]))


def _ln_epi(acc, res, g, b):
    u = ALPHA * res + acc
    mu = jnp.mean(u, axis=-1, keepdims=True)
    xc = u - mu
    var = jnp.mean(xc * xc, axis=-1, keepdims=True)
    rstd = lax.rsqrt(var + LN_EPS)
    xhat = xc * rstd
    y = xhat * g + b
    return y, xhat, jnp.broadcast_to(rstd, (u.shape[0], 128)), y


def matmul_ln(a, w, res, g, b, *, pro=None, tk, name, job=None):
    n = w.shape[1]
    return matmul(a, w, mode="nn", tm=1024, tn=n, tk=tk, pro=pro, epi=_ln_epi, tiles=(res,), rows=(g, b),
                  outs=((F32, None), (F32, None), (F32, 128), (BF16, None)), name=name, job=job)


def ln_bwd(dy, xhat, rstd, g, *, name, job=None):
    T, D = dy.shape
    tm = min(512, T)

    def body(dy_ref, xh_ref, rs_ref, g_ref, du_ref, du16_ref, dg_ref, db_ref):
        dyv, xh = dy_ref[...], xh_ref[...]
        r = rs_ref[:, 0:1]
        dxh = dyv * g_ref[...]
        m1 = jnp.mean(dxh, axis=-1, keepdims=True)
        m2 = jnp.mean(dxh * xh, axis=-1, keepdims=True)
        du = r * (dxh - m1 - xh * m2)
        du_ref[...] = du
        du16_ref[...] = du.astype(BF16)

        @pl.when(pl.program_id(0) == 0)
        def _():
            dg_ref[...] = jnp.zeros_like(dg_ref)
            db_ref[...] = jnp.zeros_like(db_ref)

        dg_ref[...] += jnp.sum(dyv * xh, axis=0, keepdims=True)
        db_ref[...] += jnp.sum(dyv, axis=0, keepdims=True)

    row = pl.BlockSpec((tm, D), lambda i: (i, 0))
    vec = pl.BlockSpec((1, D), lambda i: (0, 0))
    j = _job_args(job, 4, 4)
    res = pl.pallas_call(
        _hosting(body, job, 4, 4, 0, T // tm), name=name, grid=(T // tm,),
        in_specs=[row, row, pl.BlockSpec((tm, 128), lambda i: (i, 0)), vec] + j["in_specs"],
        out_specs=[row, row, vec, vec] + j["out_specs"],
        out_shape=[jax.ShapeDtypeStruct((T, D), F32), jax.ShapeDtypeStruct((T, D), BF16),
                   jax.ShapeDtypeStruct((1, D), F32), jax.ShapeDtypeStruct((1, D), F32)] + j["out_shape"],
        scratch_shapes=j["scratch"], input_output_aliases=j["aliases"],
        compiler_params=_params(("arbitrary",)),
    )(dy, xhat, rstd, g, *j["ins"])
    return list(res[:4]) if job is None else (list(res[:4]), list(res[4:]))


def loss_head(y, target):
    T, D = y.shape
    tm = min(512, T)

    def body(y_ref, t_ref, dy_ref, s_ref):
        e = y_ref[...] - t_ref[...]
        dy_ref[...] = e * (1.0 / D)

        @pl.when(pl.program_id(0) == 0)
        def _():
            s_ref[...] = jnp.zeros_like(s_ref)

        s_ref[...] += jnp.sum(jnp.mean(e * e, axis=-1, keepdims=True))

    row = pl.BlockSpec((tm, D), lambda i: (i, 0))
    return pl.pallas_call(
        body, name="loss_head", grid=(T // tm,),
        in_specs=[row, row], out_specs=[row, pl.BlockSpec((8, 128), lambda i: (0, 0))],
        out_shape=[jax.ShapeDtypeStruct((T, D), F32), jax.ShapeDtypeStruct((8, 128), F32)],
        compiler_params=_params(("arbitrary",)),
    )(y, target)


def _rope_tables(T):
    half = 64
    inv_freq = ROPE_BASE ** (-jnp.arange(half, dtype=F32) / half)
    ang = jnp.arange(T, dtype=jnp.int32).astype(F32)[:, None] * inv_freq[None, :]
    cos, sin = jnp.cos(ang), jnp.sin(ang)
    return jnp.concatenate([cos, cos], axis=1), jnp.concatenate([-sin, sin], axis=1)


def _ret_consts():
    H = RET_HEADS
    log_g = jnp.log(1.0 - 2.0 ** (-5.0 - jnp.arange(H, dtype=F32)))
    idx = jnp.arange(CHUNK, dtype=F32)
    diff = idx[:, None] - idx[None, :]
    dmat = jnp.where(diff[None] >= 0, jnp.exp(log_g[:, None, None] * diff[None]), 0.0)
    kd = jnp.exp(log_g[:, None] * (CHUNK - 1 - idx)[None, :])
    qd = jnp.exp(log_g[:, None] * (idx + 1.0)[None, :])
    cd = jnp.exp(log_g * CHUNK)
    full = (H, CHUNK, CHUNK)
    return (dmat.astype(F32), jnp.broadcast_to(kd[:, :, None], full), jnp.broadcast_to(qd[:, :, None], full),
            jnp.broadcast_to(cd[:, None, None], full))


def _swap_halves(v):
    return pltpu.roll(v, 64, 1)


def _group_norm(o):
    mu = jnp.mean(o, axis=-1, keepdims=True)
    xc = o - mu
    var = jnp.mean(xc * xc, axis=-1, keepdims=True)
    rstd = lax.rsqrt(var + LN_EPS)
    return xc * rstd, rstd


def ret_fwd(proj, cosf, sinf, consts, gn_g, gn_b, *, name):
    T = proj.shape[0]
    tb = min(512, T)
    nch = tb // CHUNK
    H = RET_HEADS

    def body(p_ref, cos_ref, sin_ref, dm_ref, kd_ref, qd_ref, cd_ref, g_ref, b_ref, out_ref, raw_ref, st_ref, s_ref):
        @pl.when(pl.program_id(0) == 0)
        def _():
            s_ref[...] = jnp.zeros_like(s_ref)

        for c in range(nch):
            r = slice(c * CHUNK, (c + 1) * CHUNK)
            cs, sn = cos_ref[r, :], sin_ref[r, :]
            for h in range(H):
                hc = slice(h * 128, (h + 1) * 128)
                q = p_ref[r, h * 128:(h + 1) * 128]
                k = p_ref[r, 512 + h * 128:512 + (h + 1) * 128]
                v = p_ref[r, 1024 + h * 128:1024 + (h + 1) * 128]
                gt = p_ref[r, 1536 + h * 128:1536 + (h + 1) * 128]
                qr = q * cs + _swap_halves(q) * sn
                kr = (k * cs + _swap_halves(k) * sn) * RET_SCALE
                sprev = s_ref[h]
                st_ref[c, h] = sprev
                qb, kb, vb = qr.astype(BF16), kr.astype(BF16), v.astype(BF16)
                s = _dot(qb, kb, NT) * dm_ref[h]
                o = _dot(s.astype(BF16), vb, NN) + _dot((qr * qd_ref[h]).astype(BF16), sprev.astype(BF16), NN)
                s_ref[h] = sprev * cd_ref[h] + _dot((kr * kd_ref[h]).astype(BF16), vb, TN)
                raw_ref[r, hc] = o
                y, _ = _group_norm(o)
                out_ref[r, hc] = (gt * jax.nn.sigmoid(gt)) * (y * g_ref[:, hc] + b_ref[:, hc])

    cmat = pl.BlockSpec((H, CHUNK, CHUNK), lambda i: (0, 0, 0))
    vec = pl.BlockSpec((1, BRANCH_W), lambda i: (0, 0))
    rope = pl.BlockSpec((tb, 128), lambda i: (i, 0))
    blk = pl.BlockSpec((tb, BRANCH_W), lambda i: (i, 0))
    return pl.pallas_call(
        body, name=name, grid=(T // tb,),
        in_specs=[pl.BlockSpec((tb, 2048), lambda i: (i, 0)), rope, rope, cmat, cmat, cmat, cmat, vec, vec],
        out_specs=[blk, blk, pl.BlockSpec((nch, H, CHUNK, CHUNK), lambda i: (i, 0, 0, 0))],
        out_shape=[jax.ShapeDtypeStruct((T, BRANCH_W), F32), jax.ShapeDtypeStruct((T, BRANCH_W), F32),
                   jax.ShapeDtypeStruct((T // CHUNK, H, CHUNK, CHUNK), F32)],
        scratch_shapes=[pltpu.VMEM((H, CHUNK, CHUNK), F32)],
        compiler_params=_params(("arbitrary",)),
    )(proj, cosf, sinf, *consts, gn_g, gn_b)


def ret_bwd(proj, cosf, sinf, consts, gn_g, gn_b, raw, states, dout, *, name, job=None):
    T = proj.shape[0]
    tb = min(512, T)
    nch = tb // CHUNK
    nb = T // tb
    H = RET_HEADS

    def body(p_ref, cos_ref, sin_ref, dm_ref, kd_ref, qd_ref, cd_ref, g_ref, b_ref, raw_ref, st_ref, do_ref,
             dp_ref, dg_ref, db_ref, ds_ref):
        @pl.when(pl.program_id(0) == 0)
        def _():
            ds_ref[...] = jnp.zeros_like(ds_ref)
            dg_ref[...] = jnp.zeros_like(dg_ref)
            db_ref[...] = jnp.zeros_like(db_ref)

        for c in reversed(range(nch)):
            r = slice(c * CHUNK, (c + 1) * CHUNK)
            cs, sn = cos_ref[r, :], sin_ref[r, :]
            for h in range(H):
                hc = slice(h * 128, (h + 1) * 128)
                q = p_ref[r, h * 128:(h + 1) * 128]
                k = p_ref[r, 512 + h * 128:512 + (h + 1) * 128]
                v = p_ref[r, 1024 + h * 128:1024 + (h + 1) * 128]
                gt = p_ref[r, 1536 + h * 128:1536 + (h + 1) * 128]
                qr = q * cs + _swap_halves(q) * sn
                kr = (k * cs + _swap_halves(k) * sn) * RET_SCALE
                sprev = st_ref[c, h]
                gv = g_ref[:, hc]
                y, rstd = _group_norm(raw_ref[r, hc])
                d_out = do_ref[r, hc]
                sg = jax.nn.sigmoid(gt)
                d_gate = d_out * (y * gv + b_ref[:, hc]) * (sg * (1.0 + gt * (1.0 - sg)))
                d_aff = d_out * (gt * sg)
                dg_ref[:, hc] += jnp.sum(d_aff * y, axis=0, keepdims=True)
                db_ref[:, hc] += jnp.sum(d_aff, axis=0, keepdims=True)
                dxh = d_aff * gv
                m1 = jnp.mean(dxh, axis=-1, keepdims=True)
                m2 = jnp.mean(dxh * y, axis=-1, keepdims=True)
                d_o = (rstd * (dxh - m1 - y * m2)).astype(BF16)
                qb, kb, vb = qr.astype(BF16), kr.astype(BF16), v.astype(BF16)
                dm, kd, qd = dm_ref[h], kd_ref[h], qd_ref[h]
                p = (_dot(qb, kb, NT) * dm).astype(BF16)
                dp = (_dot(d_o, vb, NT) * dm).astype(BF16)
                dsn = ds_ref[h]
                dsb = dsn.astype(BF16)
                dq_r = _dot(dp, kb, NN) + _dot(d_o, sprev.astype(BF16), NT) * qd
                dk_r = (_dot(dp, qb, TN) + _dot(vb, dsb, NT) * kd) * RET_SCALE
                d_v = _dot(p, d_o, TN) + _dot((kr * kd).astype(BF16), dsb, NN)
                ds_ref[h] = dsn * cd_ref[h] + _dot((qr * qd).astype(BF16), d_o, TN)
                dp_ref[r, h * 128:(h + 1) * 128] = (dq_r * cs - _swap_halves(dq_r) * sn).astype(BF16)
                dp_ref[r, 512 + h * 128:512 + (h + 1) * 128] = (dk_r * cs - _swap_halves(dk_r) * sn).astype(BF16)
                dp_ref[r, 1024 + h * 128:1024 + (h + 1) * 128] = d_v.astype(BF16)
                dp_ref[r, 1536 + h * 128:1536 + (h + 1) * 128] = d_gate.astype(BF16)

    cmat = pl.BlockSpec((H, CHUNK, CHUNK), lambda i: (0, 0, 0))
    vec = pl.BlockSpec((1, BRANCH_W), lambda i: (0, 0))
    rope = pl.BlockSpec((tb, 128), lambda i: (nb - 1 - i, 0))
    blk = pl.BlockSpec((tb, BRANCH_W), lambda i: (nb - 1 - i, 0))
    wide = pl.BlockSpec((tb, 2048), lambda i: (nb - 1 - i, 0))
    j = _job_args(job, 12, 3)
    res = pl.pallas_call(
        _hosting(body, job, 12, 3, 1, nb), name=name, grid=(nb,),
        in_specs=[wide, rope, rope, cmat, cmat, cmat, cmat, vec, vec, blk,
                  pl.BlockSpec((nch, H, CHUNK, CHUNK), lambda i: (nb - 1 - i, 0, 0, 0)), blk] + j["in_specs"],
        out_specs=[wide, vec, vec] + j["out_specs"],
        out_shape=[jax.ShapeDtypeStruct((T, 2048), BF16), jax.ShapeDtypeStruct((1, BRANCH_W), F32),
                   jax.ShapeDtypeStruct((1, BRANCH_W), F32)] + j["out_shape"],
        scratch_shapes=[pltpu.VMEM((H, CHUNK, CHUNK), F32)] + j["scratch"], input_output_aliases=j["aliases"],
        compiler_params=_params(("arbitrary",)),
    )(proj, cosf, sinf, *consts, gn_g, gn_b, raw, states, dout, *j["ins"])
    return res[0], res[1], res[2], list(res[3:])


def _sb_masks():
    row = lax.broadcasted_iota(jnp.int32, (CHUNK, CHUNK), 0)
    lane = lax.broadcasted_iota(jnp.int32, (CHUNK, CHUNK), 1)
    return row, lane


SB_QT = 256
SB_DEAD = -105.0


def _pair(v):
    hi = v.astype(BF16)
    return jnp.concatenate([hi, (v - hi.astype(F32)).astype(BF16)], axis=1)


def _sb_consts():
    r = lax.broadcasted_iota(jnp.int32, (256, 256), 0) & 127
    c = lax.broadcasted_iota(jnp.int32, (256, 256), 1)
    ones = c >= 128
    lane = lax.broadcasted_iota(jnp.int32, (CHUNK, CHUNK), 1)
    return (ones | (r > c)).astype(BF16), (ones | (r >= c)).astype(BF16), (lane < 64, lane >= 64)


def _per_head(x, hms):
    return jnp.concatenate([jnp.where(hm, x, 0.0) for hm in hms], axis=0).astype(BF16)


def _sb_logits(qb, kb2, mask2):
    z = _dot(qb, kb2, NT)
    l1p = jnp.log(1.0 + jnp.exp(-jnp.abs(z)))
    lsp = jnp.minimum(z, 0.0) - l1p
    lsn = lsp - z
    if mask2 is not None:
        lsn = jnp.where(mask2, lsn, 0.0)
    return lsp, lsn


def _sb_tile_mask(qt):
    trow = lax.broadcasted_iota(jnp.int32, (qt, 256), 0)
    tlane = lax.broadcasted_iota(jnp.int32, (qt, 256), 1) & 127
    return lambda m: (tlane + m * CHUNK) < trow


def sb_fwd(proj, *, name, job=None):
    T = proj.shape[0]
    qt = min(SB_QT, T)
    nsub = qt // CHUNK
    cb = C_SB // 128

    def body(q_ref, k_ref, v_ref, o_ref):
        u_gt, _, hms = _sb_consts()
        tile_mask = _sb_tile_mask(qt)

        def qtile(i, _):
            rq = pl.ds(pl.multiple_of(i * qt, qt), qt)
            qb = (q_ref[rq, :] * SB_SCALE).astype(BF16)

            def group(js, masks, state):
                carry, acc = list(state[:2]), state[2]
                rows = [pl.ds(pl.multiple_of(j * CHUNK, CHUNK), CHUNK) for j in js]
                logits = [_sb_logits(qb, _per_head(k_ref[rk, :], hms), m) for rk, m in zip(rows, masks)]
                sums = [[_dot(_pair(lsn[:, h * 128:(h + 1) * 128]), u_gt, NN) for h in range(2)] for _, lsn in logits]
                weights = []
                for (lsp, _), r, m in zip(logits, sums, masks):
                    a_b = []
                    for h in range(2):
                        hc = slice(h * 128, (h + 1) * 128)
                        a = jnp.exp(lsp[:, hc] + r[h][:, :128] + carry[h])
                        if m is not None:
                            a = jnp.where(m[:, hc], a, 0.0)
                        carry[h] = carry[h] + r[h][:, 128:]
                        a_b.append(a.astype(BF16))
                    weights.append(jnp.concatenate(a_b, axis=1))
                for rk, a in zip(rows, weights):
                    acc = acc + _dot(a, _per_head(v_ref[rk, :], hms), NN)
                return carry[0], carry[1], acc

            zero = jnp.zeros((qt, 128), F32)
            diag = list(reversed(range(nsub)))
            state = group([i * nsub + m for m in diag], [tile_mask(m) for m in diag], (zero, zero, zero))

            def live(c):
                return jnp.logical_and(c[0] < i, jnp.maximum(jnp.max(c[1][0]), jnp.max(c[1][1])) > SB_DEAD)

            def blocks(c):
                jj, st = c
                return jj + 1, group([(i - jj) * nsub - 1 - u for u in range(nsub)], [None] * nsub, st)

            _, state = lax.while_loop(live, blocks, (jnp.int32(0), state))
            o_ref[rq, :] = state[2]
            return 0

        lax.fori_loop(0, T // qt, qtile, 0)

    def col(off):
        return pl.BlockSpec((T, 128), lambda hp: (0, off + hp))

    steps = BRANCH_W // 128
    j = _job_args(job, 3, 1)
    res = pl.pallas_call(
        _hosting(body, job, 3, 1, 0, steps), name=name, grid=(steps,),
        in_specs=[col(cb), col(cb + 4), col(cb + 8)] + j["in_specs"], out_specs=[col(0)] + j["out_specs"],
        out_shape=[jax.ShapeDtypeStruct((T, BRANCH_W), F32)] + j["out_shape"],
        scratch_shapes=j["scratch"], input_output_aliases=j["aliases"],
        compiler_params=_params(("parallel",) if job is None else ("arbitrary",)),
    )(proj, proj, proj, *j["ins"])
    return res[0], list(res[1:])


def sb_bwd(proj, out, dout, *, name, job=None):
    T = proj.shape[0]
    qt = min(SB_QT, T)
    nsub = qt // CHUNK
    cb = C_SB // 128

    def body(q_ref, k_ref, v_ref, o_ref, do_ref, dq_ref, dk_ref, dv_ref, dkt_ref, dvt_ref):
        u_gt, u_ge, hms = _sb_consts()
        tile_mask = _sb_tile_mask(qt)
        tall_lane = lax.broadcasted_iota(jnp.int32, (qt, 128), 1)
        top = lax.broadcasted_iota(jnp.int32, (CHUNK, CHUNK), 0) < 64
        dkt_ref[...] = jnp.zeros_like(dkt_ref)
        dvt_ref[...] = jnp.zeros_like(dvt_ref)

        def qtile(i, _):
            rq = pl.ds(pl.multiple_of(i * qt, qt), qt)
            qs = q_ref[rq, :] * SB_SCALE
            qb, q_t = qs.astype(BF16), qs.T.astype(BF16)
            dov = do_ref[rq, :]
            dob, do_t = dov.astype(BF16), dov.T.astype(BF16)
            prod = dob.astype(F32) * o_ref[rq, :]
            total = [jnp.broadcast_to(jnp.sum(jnp.where(hm, prod, 0.0), axis=1, keepdims=True), (qt, 128))
                     for hm in (tall_lane < 64, tall_lane >= 64)]

            def group(js, masks, state):
                c_l, c_w, dq = list(state[:2]), list(state[2:4]), state[4]
                heads = [slice(h * 128, (h + 1) * 128) for h in range(2)]
                rows = [pl.ds(pl.multiple_of(j * CHUNK, CHUNK), CHUNK) for j in js]
                kb2 = [_per_head(k_ref[rk, :], hms) for rk in rows]
                logits = [_sb_logits(qb, kb, m) for kb, m in zip(kb2, masks)]
                da = [_dot(dob, _per_head(v_ref[rk, :], hms), NT) for rk in rows]
                sums = [[_dot(_pair(lsn[:, hc]), u_gt, NN) for hc in heads] for _, lsn in logits]
                a_b, w_all = [], []
                for (lsp, _), r, d, m in zip(logits, sums, da, masks):
                    a_h, w_h = [], []
                    for h, hc in enumerate(heads):
                        a = jnp.exp(lsp[:, hc] + r[h][:, :128] + c_l[h])
                        if m is not None:
                            a = jnp.where(m[:, hc], a, 0.0)
                        c_l[h] = c_l[h] + r[h][:, 128:]
                        a = a.astype(BF16)
                        a_h.append(a)
                        w_h.append(a.astype(F32) * d[:, hc])
                    a_b.append(jnp.concatenate(a_h, axis=1))
                    w_all.append(w_h)
                sums_w = [[_dot(_pair(w), u_ge, NN) for w in w_h] for w_h in w_all]
                dz_b = []
                for (lsp, _), w_h, r, m in zip(logits, w_all, sums_w, masks):
                    sp = jnp.exp(lsp)
                    dz_h = []
                    for h, hc in enumerate(heads):
                        later_w = r[h][:, :128] + c_w[h]
                        c_w[h] = c_w[h] + r[h][:, 128:]
                        dz = w_h[h] * (1.0 - sp[:, hc]) - sp[:, hc] * (total[h] - later_w)
                        if m is not None:
                            dz = jnp.where(m[:, hc], dz, 0.0)
                        dz_h.append(dz.astype(BF16))
                    dz_b.append(jnp.concatenate(dz_h, axis=1))
                for j, kb, a, dz in zip(js, kb2, a_b, dz_b):
                    dkt = _dot(q_t, dz, NN)
                    dvt = _dot(do_t, a, NN)
                    dkt_ref[j] += jnp.where(top, dkt[:, :128], dkt[:, 128:])
                    dvt_ref[j] += jnp.where(top, dvt[:, :128], dvt[:, 128:])
                    dq = dq + _dot(dz, kb, NN)
                return c_l[0], c_l[1], c_w[0], c_w[1], dq

            zero = jnp.zeros((qt, 128), F32)
            diag = list(reversed(range(nsub)))
            state = group([i * nsub + m for m in diag], [tile_mask(m) for m in diag], (zero,) * 5)

            def live(c):
                return jnp.logical_and(c[0] < i, jnp.maximum(jnp.max(c[1][0]), jnp.max(c[1][1])) > SB_DEAD)

            def blocks(c):
                jj, st = c
                return jj + 1, group([(i - jj) * nsub - 1 - u for u in range(nsub)], [None] * nsub, st)

            _, state = lax.while_loop(live, blocks, (jnp.int32(0), state))
            dq_ref[rq, :] = (state[4] * SB_SCALE).astype(BF16)
            return 0

        lax.fori_loop(0, T // qt, qtile, 0)

        def untranspose(jb, _):
            rk = pl.ds(pl.multiple_of(jb * CHUNK, CHUNK), CHUNK)
            dk_ref[rk, :] = dkt_ref[jb].T.astype(BF16)
            dv_ref[rk, :] = dvt_ref[jb].T.astype(BF16)
            return 0

        lax.fori_loop(0, T // CHUNK, untranspose, 0)

    def col(off):
        return pl.BlockSpec((T, 128), lambda hp: (0, off + hp))

    o16 = jax.ShapeDtypeStruct((T, BRANCH_W), BF16)
    steps = BRANCH_W // 128
    j = _job_args(job, 5, 3)
    acc = pltpu.VMEM((T // CHUNK, CHUNK, CHUNK), F32)
    res = pl.pallas_call(
        _hosting(body, job, 5, 3, 2, steps), name=name, grid=(steps,),
        in_specs=[col(cb), col(cb + 4), col(cb + 8), col(0), col(0)] + j["in_specs"],
        out_specs=[col(0), col(0), col(0)] + j["out_specs"], out_shape=[o16, o16, o16] + j["out_shape"],
        scratch_shapes=[acc, acc] + j["scratch"], input_output_aliases=j["aliases"],
        compiler_params=_params(("parallel",) if job is None else ("arbitrary",)),
    )(proj, proj, proj, out, dout, *j["ins"])
    return res[0], res[1], res[2], list(res[3:])


_G0 = math.sqrt(2.0 / math.pi)
_G1 = 0.044715


def _gelu(x):
    return 0.5 * x * (1.0 + jnp.tanh(_G0 * (x + _G1 * x * x * x)))


def _gelu_grad(x):
    t = jnp.tanh(_G0 * (x + _G1 * x * x * x))
    return 0.5 * (1.0 + t) + 0.5 * x * (1.0 - t * t) * (_G0 * (1.0 + 3.0 * _G1 * x * x))


def _tril():
    row, lane = _sb_masks()
    return row >= lane


def sgu_fwd(proj, ln_g, ln_b, w, bias, *, name):
    T = proj.shape[0]
    tb = min(512, T)
    G = BRANCH_W // 128

    def body(u_ref, v_ref, g_ref, b_ref, w_ref, bias_ref, o_ref):
        vv = _gelu(v_ref[...])
        xh, _ = _group_norm(vv)
        vn = (xh * g_ref[...] + b_ref[...]).astype(BF16)
        tril = _tril()
        for g in range(G):
            wg = jnp.where(tril, w_ref[g], 0.0).astype(BF16)
            gc = slice(g * 128, (g + 1) * 128)
            for c in range(tb // CHUNK):
                r = slice(c * CHUNK, (c + 1) * CHUNK)
                sv = _dot(wg, vn[r, gc], NN) + bias_ref[g]
                o_ref[r, gc] = _gelu(u_ref[r, gc]) * sv

    cu, cv = C_SGU // BRANCH_W, C_SGU // BRANCH_W + 1
    vec = pl.BlockSpec((1, BRANCH_W), lambda i: (0, 0))
    mat = pl.BlockSpec((G, CHUNK, CHUNK), lambda i: (0, 0, 0))
    return pl.pallas_call(
        body, name=name, grid=(T // tb,),
        in_specs=[pl.BlockSpec((tb, BRANCH_W), lambda i: (i, cu)), pl.BlockSpec((tb, BRANCH_W), lambda i: (i, cv)),
                  vec, vec, mat, mat],
        out_specs=pl.BlockSpec((tb, BRANCH_W), lambda i: (i, 0)),
        out_shape=jax.ShapeDtypeStruct((T, BRANCH_W), F32),
        compiler_params=_params(("parallel",)),
    )(proj, proj, ln_g, ln_b, w, bias)


def sgu_bwd(proj, ln_g, ln_b, w, bias, dout, *, name):
    T = proj.shape[0]
    tb = min(512, T)
    G = BRANCH_W // 128

    def body(u_ref, v_ref, g_ref, b_ref, w_ref, bias_ref, do_ref, dp_ref, dw_ref, dbias_ref, dg_ref, db_ref, dvn_ref):
        @pl.when(pl.program_id(0) == 0)
        def _():
            dw_ref[...] = jnp.zeros_like(dw_ref)
            dbias_ref[...] = jnp.zeros_like(dbias_ref)
            dg_ref[...] = jnp.zeros_like(dg_ref)
            db_ref[...] = jnp.zeros_like(db_ref)

        gv = v_ref[...]
        vv = _gelu(gv)
        xh, rstd = _group_norm(vv)
        vn = (xh * g_ref[...] + b_ref[...]).astype(BF16)
        tril = _tril()
        for g in range(G):
            wg = jnp.where(tril, w_ref[g], 0.0).astype(BF16)
            gc = slice(g * 128, (g + 1) * 128)
            for c in range(tb // CHUNK):
                r = slice(c * CHUNK, (c + 1) * CHUNK)
                vn_c = vn[r, gc]
                sv = _dot(wg, vn_c, NN) + bias_ref[g]
                gu = u_ref[r, gc]
                d_o = do_ref[r, gc]
                dp_ref[r, gc] = (d_o * sv * _gelu_grad(gu)).astype(BF16)
                dsv = d_o * _gelu(gu)
                dsv_b = dsv.astype(BF16)
                dvn_ref[r, gc] = _dot(wg, dsv_b, TN)
                dw_ref[g] += jnp.where(tril, _dot(dsv_b, vn_c, NT), 0.0)
                dbias_ref[g] += jnp.broadcast_to(jnp.sum(dsv, axis=1, keepdims=True), (CHUNK, CHUNK))
        dvn = dvn_ref[...]
        dg_ref[...] += jnp.sum(dvn * xh, axis=0, keepdims=True)
        db_ref[...] += jnp.sum(dvn, axis=0, keepdims=True)
        dxh = dvn * g_ref[...]
        m1 = jnp.mean(dxh, axis=-1, keepdims=True)
        m2 = jnp.mean(dxh * xh, axis=-1, keepdims=True)
        dp_ref[:, BRANCH_W:2 * BRANCH_W] = (rstd * (dxh - m1 - xh * m2) * _gelu_grad(gv)).astype(BF16)

    cu, cv = C_SGU // BRANCH_W, C_SGU // BRANCH_W + 1
    vec = pl.BlockSpec((1, BRANCH_W), lambda i: (0, 0))
    mat = pl.BlockSpec((G, CHUNK, CHUNK), lambda i: (0, 0, 0))
    blk = pl.BlockSpec((tb, BRANCH_W), lambda i: (i, 0))
    msh = jax.ShapeDtypeStruct((G, CHUNK, CHUNK), F32)
    vsh = jax.ShapeDtypeStruct((1, BRANCH_W), F32)
    return pl.pallas_call(
        body, name=name, grid=(T // tb,),
        in_specs=[pl.BlockSpec((tb, BRANCH_W), lambda i: (i, cu)), pl.BlockSpec((tb, BRANCH_W), lambda i: (i, cv)),
                  vec, vec, mat, mat, blk],
        out_specs=[pl.BlockSpec((tb, 2 * BRANCH_W), lambda i: (i, 0)), mat, mat, vec, vec],
        out_shape=[jax.ShapeDtypeStruct((T, 2 * BRANCH_W), BF16), msh, msh, vsh, vsh],
        scratch_shapes=[pltpu.VMEM((tb, BRANCH_W), F32)],
        compiler_params=_params(("arbitrary",)),
    )(proj, proj, ln_g, ln_b, w, bias, dout)


def merge_fwd(a1, a2, a3, p1, p2, p3, proj, *, name):
    T = a1.shape[0]
    tm, tn = min(1024, T), 512
    gb = C_GATE // tn

    def body(a1_ref, a2_ref, a3_ref, p1_ref, p2_ref, p3_ref, g1_ref, g2_ref, g3_ref, m_ref, r1_ref, r2_ref, r3_ref):
        m = None
        for a_ref, p_ref, g_ref, r_ref in ((a1_ref, p1_ref, g1_ref, r1_ref), (a2_ref, p2_ref, g2_ref, r2_ref),
                                           (a3_ref, p3_ref, g3_ref, r3_ref)):
            r = _dot(a_ref[...].astype(BF16), p_ref[...], NN)
            r_ref[...] = r.astype(r_ref.dtype)
            t = jax.nn.sigmoid(g_ref[...]) * r
            m = t if m is None else m + t
        m_ref[...] = m.astype(m_ref.dtype)

    a_spec = pl.BlockSpec((tm, BRANCH_W), lambda i, j: (i, 0))
    p_spec = pl.BlockSpec((BRANCH_W, tn), lambda i, j: (0, j))
    o_spec = pl.BlockSpec((tm, tn), lambda i, j: (i, j))
    gates = [pl.BlockSpec((tm, tn), functools.partial(lambda i, j, o: (i, o + j), o=gb + 2 * n)) for n in range(3)]
    return pl.pallas_call(
        body, name=name, grid=(T // tm, D_MODEL // tn),
        in_specs=[a_spec, a_spec, a_spec, p_spec, p_spec, p_spec, *gates],
        out_specs=[o_spec] * 4, out_shape=[jax.ShapeDtypeStruct((T, D_MODEL), BF16)] * 4,
        compiler_params=_params(("parallel", "parallel")),
    )(a1, a2, a3, p1, p2, p3, proj, proj, proj)


def _merge_bwd_epi(dm, r1, r2, r3, g1, g2, g3):
    d_r, d_g = [], []
    for r, g in ((r1, g1), (r2, g2), (r3, g3)):
        s = jax.nn.sigmoid(g)
        d_r.append(dm * s)
        d_g.append(dm * r.astype(F32) * (s * (1.0 - s)))
    return (*d_r, *d_g)


def _rows_call(fn, ins, out_dtypes, *, name, tr=256, job=None):
    first = ins[0][0] if isinstance(ins[0], tuple) else ins[0]
    R, C = first.shape[-2:]
    tr = min(tr, R)
    assert R % tr == 0, (name, R, tr)
    arrs, specs = [], []
    for x in ins:
        if isinstance(x, tuple):
            arrs.append(x[0])
            specs.append(pl.BlockSpec((None, tr, C), functools.partial(lambda i, n: (n, i, 0), n=x[1])))
        else:
            arrs.append(x)
            specs.append(pl.BlockSpec((tr, C), lambda i: (i, 0)))
    ni = len(arrs)

    def body(*refs):
        vals = fn(*[r[...] for r in refs[:ni]])
        for o_ref, v in zip(refs[ni:], vals):
            o_ref[...] = v.astype(o_ref.dtype)

    no = len(out_dtypes)
    j = _job_args(job, ni, no)
    res = pl.pallas_call(
        _hosting(body, job, ni, no, 0, R // tr), name=name, grid=(R // tr,), in_specs=specs + j["in_specs"],
        out_specs=[pl.BlockSpec((tr, C), lambda i: (i, 0)) for _ in out_dtypes] + j["out_specs"],
        out_shape=[jax.ShapeDtypeStruct((R, C), dt) for dt in out_dtypes] + j["out_shape"],
        scratch_shapes=j["scratch"], input_output_aliases=j["aliases"],
        compiler_params=_params(("parallel",) if job is None else ("arbitrary",)),
    )(*arrs, *j["ins"])
    return list(res) if job is None else (list(res[:no]), list(res[no:]))


def _tile_rows(rows, cols):
    t = 256
    while t > 8 and (t * cols > 512 * 1024 or rows % t):
        t //= 2
    return t


def _rows_at(fn, pos, ins, outs, steps, *, name, aliases=None):
    read = [n for n, (_, s) in enumerate(ins) if s is not ANY]
    ni = len(ins)

    def body(pos_ref, *refs):
        vals = fn(*[refs[n][...] for n in read])
        for o_ref, v in zip(refs[ni:], vals):
            o_ref[...] = v.astype(o_ref.dtype)

    return pl.pallas_call(
        body, name=name,
        grid_spec=pltpu.PrefetchScalarGridSpec(num_scalar_prefetch=1, grid=(steps,), in_specs=[s for _, s in ins],
                                               out_specs=[s for _, s in outs]),
        out_shape=[sh for sh, _ in outs],
        input_output_aliases={1 + i: o for i, o in (aliases or {}).items()},
        compiler_params=_params(("parallel",)),
    )(pos, *[a for a, _ in ins])


def cast_into_whole(pos, w, l, axis, *, name):
    _, r, n = w.shape
    tr = _tile_rows(r, n)
    if axis == 1:
        shape, spec = (r, n * N_CHIPS), pl.BlockSpec((tr, n), lambda i, p: (i, p[3]))
    else:
        shape, spec = (r * N_CHIPS, n), pl.BlockSpec((tr, n), lambda i, p: (p[3] * (r // tr) + i, 0))
    return _rows_at(lambda a: (a,), pos, [(w, pl.BlockSpec((None, tr, n), lambda i, p: (l, i, 0)))],
                    [(jax.ShapeDtypeStruct(shape, BF16), spec)], r // tr, name=name)[0]


def pair_sum(pos, theirs, g32, axis, *, name):
    rows2, cols = theirs.shape
    h = rows2 // (N_CHIPS if axis == 0 else 1)
    tr = _tile_rows(h, cols)
    hb = h // tr
    if axis == 1:
        own = pl.BlockSpec((tr, cols), lambda i, p: (p[2] * hb + i, 0))
    else:
        own = pl.BlockSpec((tr, cols), lambda i, p: ((2 * (i // hb) + p[2]) * hb + i % hb, 0))
    row = pl.BlockSpec((tr, cols), lambda i, p: (i, 0))
    return _rows_at(lambda t, m: (m + t.astype(F32),) * 2, pos, [(theirs, row), (g32, own)],
                    [(jax.ShapeDtypeStruct((rows2, cols), F32), row), (jax.ShapeDtypeStruct((rows2, cols), BF16), row)],
                    rows2 // tr, name=name)


def chip_sum(pos, h32, recv, l, axis, whole, *, name):
    _, depth, h, n = recv.shape
    tr = _tile_rows(h, n)
    hb = h // tr
    if axis == 1:
        mine = pl.BlockSpec((tr, n), lambda i, p: (i, p[3]))
    else:
        mine = pl.BlockSpec((tr, n), lambda i, p: (p[3] * hb + i, 0))
    ins = [(h32, mine)] + [(recv, pl.BlockSpec((None, None, tr, n), functools.partial(lambda i, p, j: (j, l, i, 0), j=j)))
                           for j in range(3)]
    if whole is not None:
        ins.append((whole, ANY))
    return _rows_at(lambda o, a, b, c: (((o + a.astype(F32)) + b.astype(F32)) + c.astype(F32),), pos, ins,
                    [(jax.ShapeDtypeStruct((depth, 2, h, n), F32), pl.BlockSpec((None, None, tr, n), lambda i, p: (l, p[2], i, 0)))],
                    hb, name=name, aliases=None if whole is None else {4: 0})[0]


def _adamw(w, g, m, v):
    m2 = ADAM_B1 * m + (1.0 - ADAM_B1) * g
    v2 = ADAM_B2 * v + (1.0 - ADAM_B2) * (g * g)
    m_hat = m2 / (1.0 - ADAM_B1 ** ADAM_STEP)
    v_hat = v2 / (1.0 - ADAM_B2 ** ADAM_STEP)
    delta = -ADAM_LR * (m_hat / (jnp.sqrt(v_hat) + ADAM_EPS) + ADAM_WD * w)
    return delta, m2, v2


def _place():
    return lax.axis_index("x"), lax.axis_index("y"), lax.axis_index("c")


def _chip_peers(x, y, c):
    return [((1 - x, y, c), 2 * (1 - x) + y), ((x, 1 - y, c), 2 * x + 1 - y), ((1 - x, 1 - y, c), 2 * (1 - x) + 1 - y)]


def _shard_of(ref, axis, k, n):
    start = pl.multiple_of(k * n, 128)
    return ref.at[pl.ds(start, n), :] if axis == 0 else ref.at[:, pl.ds(start, n)]


ANY = pl.BlockSpec(memory_space=pl.ANY)


class CopyJob:
    def __init__(self, ins, out_shape, scratch, copies, aliases=None):
        self.ins, self.out_shape, self.scratch, self.copies = list(ins), list(out_shape), list(scratch), copies
        self.aliases = dict(aliases or {})

    def start(self, ins, outs, sems):
        local, remote, _, _ = self.copies(ins, outs, sems)
        for d in local + remote:
            d.start()

    def finish(self, ins, outs, sems):
        local, remote, arrivals, relays = self.copies(ins, outs, sems)
        for needs, sends, _ in relays:
            for d in needs:
                d.wait_recv()
            for d in sends:
                d.start()
        for d in arrivals + [d for _, _, arrives in relays for d in arrives]:
            d.wait_recv()
        for d in remote + [d for _, sends, _ in relays for d in sends]:
            d.wait_send()
        for d in local:
            d.wait()


def run_job(job, *, name):
    ni, no = len(job.ins), len(job.out_shape)

    def body(*refs):
        parts = refs[:ni], refs[ni:ni + no], refs[ni + no:]
        job.start(*parts)
        job.finish(*parts)

    return pl.pallas_call(
        body, name=name, in_specs=[ANY] * ni, out_specs=[ANY] * no, out_shape=job.out_shape,
        scratch_shapes=job.scratch, input_output_aliases=job.aliases,
    )(*job.ins)


def _job_args(job, n_in, n_out):
    if job is None:
        return dict(ins=[], in_specs=[], out_specs=[], out_shape=[], scratch=[], aliases={})
    return dict(ins=job.ins, in_specs=[ANY] * len(job.ins), out_specs=[ANY] * len(job.out_shape),
                out_shape=job.out_shape, scratch=job.scratch,
                aliases={n_in + i: n_out + o for i, o in job.aliases.items()})


def _hosting(body, job, n_in, n_out, n_scratch, grid):
    if job is None:
        return body
    ji, jo = len(job.ins), len(job.out_shape)
    grid = (grid,) if isinstance(grid, int) else tuple(grid)

    def at(ends):
        hit = None
        for ax, e in enumerate(ends):
            here = pl.program_id(ax) == e
            hit = here if hit is None else jnp.logical_and(hit, here)
        return hit

    def hosted(*refs):
        o = n_in + ji
        s = o + n_out + jo
        parts = refs[n_in:o], refs[o + n_out:s], refs[s + n_scratch:]

        @pl.when(at([0] * len(grid)))
        def _():
            job.start(*parts)

        body(*refs[:n_in], *refs[o:o + n_out], *refs[s:s + n_scratch])

        @pl.when(at([g - 1 for g in grid]))
        def _():
            job.finish(*parts)

    return hosted


def _job_sems(n_remote, n_local):
    return [pltpu.SemaphoreType.DMA((n_remote,)), pltpu.SemaphoreType.DMA((n_remote,)), pltpu.SemaphoreType.DMA((n_local,))]


def gather_job(shards, axes, chips=(0, 1, 2)):
    na = len(shards)

    def copies(ins, outs, sems):
        send, recv, _ = sems
        x, y, c = _place()
        k = 2 * x + y
        remote, relays = [], []
        for a in range(na):
            r = outs[a].shape[0] // (N_CHIPS if axes[a] == 0 else 1)
            n = outs[a].shape[axes[a]] // N_CHIPS
            half = r // 2

            def part(kk, cc, a=a, n=n, half=half):
                rows = pl.ds(pl.multiple_of(cc * half + (kk * n if axes[a] == 0 else 0), 8), half)
                return outs[a].at[rows, :] if axes[a] == 0 else outs[a].at[rows, pl.ds(pl.multiple_of(kk * n, 128), n)]

            needs, passes, lands = [], [], []
            for j, (peer, kp) in enumerate(_chip_peers(x, y, c)):
                if j not in chips:
                    continue
                s = 6 * a + j
                remote.append(pltpu.make_async_remote_copy(part(k, c), part(k, c), send.at[s], recv.at[s],
                                                           device_id=peer, device_id_type=MESH))
                needs.append(pltpu.make_async_remote_copy(part(kp, c), part(kp, c), send.at[s], recv.at[s],
                                                          device_id=peer, device_id_type=MESH))
                passes.append(pltpu.make_async_remote_copy(part(kp, c), part(kp, c), send.at[s + 3], recv.at[s + 3],
                                                           device_id=(x, y, 1 - c), device_id_type=MESH))
                lands.append(pltpu.make_async_remote_copy(part(kp, 1 - c), part(kp, 1 - c), send.at[s + 3], recv.at[s + 3],
                                                          device_id=(x, y, 1 - c), device_id_type=MESH))
            relays.append((needs, passes, lands))
        return [], remote, [], relays

    out_shape = [jax.ShapeDtypeStruct(w.shape, BF16) for w in shards]
    return CopyJob(shards, out_shape, _job_sems(6 * na, 1), copies, {a: a for a in range(na)})


def scatter_job(layers, g16, axes, filled, chips=(0, 1, 2)):
    na = len(axes)

    def shard_shape(a):
        r, c = g16[a].shape
        return (r // N_CHIPS, c) if axes[a] == 0 else (r, c // N_CHIPS)

    def copies(ins, outs, sems):
        send, recv_sems, _ = sems
        x, y, c = _place()
        remote = []
        for a in range(na):
            n = shard_shape(a)[axes[a]]
            for r, (peer, kp) in enumerate(_chip_peers(x, y, c)):
                if r not in chips:
                    continue
                remote.append(pltpu.make_async_remote_copy(_shard_of(ins[a], axes[a], kp, n), outs[a].at[r, layers[a]],
                                                           send.at[3 * a + r], recv_sems.at[3 * a + r],
                                                           device_id=peer, device_id_type=MESH))
        return [], remote, remote, []

    out_shape = [jax.ShapeDtypeStruct((3, DEPTH) + shard_shape(a), BF16) for a in range(na)]
    ins = list(g16)
    aliases = {}
    for a in range(na):
        if filled[a] is not None:
            aliases[len(ins)] = a
            ins.append(filled[a])
    return CopyJob(ins, out_shape, _job_sems(3 * na, 1), copies, aliases)


def pair_job(g16, axes):
    na = len(axes)
    pieces = [1 if ax == 1 else N_CHIPS for ax in axes]

    def copies(ins, outs, sems):
        send, recv, _ = sems
        x, y, c = _place()
        remote = []
        s = 0
        for a in range(na):
            rows = g16[a].shape[0] // (2 * pieces[a])
            for kk in range(pieces[a]):
                src = ins[a].at[pl.ds(pl.multiple_of((2 * kk + 1 - c) * rows, 8), rows), :]
                remote.append(pltpu.make_async_remote_copy(src, outs[a].at[pl.ds(kk * rows, rows), :], send.at[s], recv.at[s],
                                                           device_id=(x, y, 1 - c), device_id_type=MESH))
                s += 1
        return [], remote, remote, []

    out_shape = [jax.ShapeDtypeStruct((g.shape[0] // 2, g.shape[1]), BF16) for g in g16]
    return CopyJob(g16, out_shape, _job_sems(sum(pieces), 1), copies)


def join_job(shards):
    na = len(shards)

    def copies(ins, outs, sems):
        send, recv, _ = sems
        x, y, c = _place()
        remote = [pltpu.make_async_remote_copy(outs[a].at[:, c], outs[a].at[:, c], send.at[a], recv.at[a],
                                               device_id=(x, y, 1 - c), device_id_type=MESH) for a in range(na)]
        lands = [pltpu.make_async_remote_copy(outs[a].at[:, 1 - c], outs[a].at[:, 1 - c], send.at[a], recv.at[a],
                                              device_id=(x, y, 1 - c), device_id_type=MESH) for a in range(na)]
        return [], remote, lands, []

    out_shape = [jax.ShapeDtypeStruct(s.shape, F32) for s in shards]
    return CopyJob(shards, out_shape, _job_sems(na, 1), copies, {a: a for a in range(na)})


def small_job(p):
    def copies(ins, outs, sems):
        send, recv, loc = sems
        x, y, c = _place()
        me = 4 * x + 2 * y + c
        remote, lands = [], []
        for rel in range(1, 8):
            dx, dy, dc = rel >> 2, (rel >> 1) & 1, rel & 1
            peer = (1 - x if dx else x, 1 - y if dy else y, 1 - c if dc else c)
            who = 4 * peer[0] + 2 * peer[1] + peer[2]
            remote.append(pltpu.make_async_remote_copy(ins[0], outs[0].at[me], send.at[rel - 1], recv.at[rel - 1],
                                                       device_id=peer, device_id_type=MESH))
            lands.append(pltpu.make_async_remote_copy(ins[0], outs[0].at[who], send.at[rel - 1], recv.at[rel - 1],
                                                      device_id=peer, device_id_type=MESH))
        return [pltpu.make_async_copy(ins[0], outs[0].at[me], loc.at[0])], remote, lands, []

    return CopyJob([p], [jax.ShapeDtypeStruct((8,) + p.shape, F32)], _job_sems(7, 1), copies)


def small_sum(slots):
    def add(*terms):
        acc = terms[0]
        for t in terms[1:]:
            acc = acc + t
        return (acc,)

    return _rows_call(add, [(slots, d) for d in range(8)], [F32], name="small_sum", tr=8 * 47)[0]


BIG = ("w_in", "p_ret", "p_sb", "p_sgu", "w_out", "w_up", "w_down")
BIG_AXIS = {"w_in": 1, "p_ret": 1, "p_sb": 1, "p_sgu": 1, "w_out": 0, "w_up": 1, "w_down": 0}
SMALL = ("ret_gn_g", "ret_gn_b", "sgu_ln_g", "sgu_ln_b", "sgu_w", "sgu_b", "ln1_g", "ln1_b", "ln2_g", "ln2_b")


def layer_forward(l, x0, x0h, W, sm, rope, rconsts, hooks):
    n = f"l{l}_"
    job = hooks.fwd_job(l, "proj")
    proj = matmul(x0h, W["w_in"], mode="nn", tm=4096, tn=768, tk=1024, name=n + "proj", job=job)
    if job is not None:
        proj, job_out = proj
        hooks.done(job, job_out)
    retg, raw, states = ret_fwd(proj, *rope, rconsts, sm["ret_gn_g"], sm["ret_gn_b"], name=n + "ret_fwd")
    job = hooks.fwd_job(l, "sb")
    sb, job_out = sb_fwd(proj, name=n + "sb_fwd", job=job)
    if job is not None:
        hooks.done(job, job_out)
    sg = sgu_fwd(proj, sm["sgu_ln_g"], sm["sgu_ln_b"], sm["sgu_w"], sm["sgu_bias"], name=n + "sgu_fwd")
    merged, r1, r2, r3 = merge_fwd(retg, sb, sg, W["p_ret"], W["p_sb"], W["p_sgu"], proj, name=n + "merge_fwd")
    x1, xh1, rs1, x1h = matmul_ln(merged, W["w_out"], x0, sm["ln1_g"], sm["ln1_b"], tk=1024, name=n + "out_ln1")
    job = hooks.fwd_job(l, "up")
    h1 = matmul(x1h, W["w_up"], mode="nn", tm=2048, tn=1024, tk=1024, outs=((BF16, None),), name=n + "up", job=job)
    if job is not None:
        h1, job_out = h1
        hooks.done(job, job_out)
    job = hooks.fwd_job(l, "down")
    res = matmul_ln(h1, W["w_down"], x1, sm["ln2_g"], sm["ln2_b"], pro=_relu2, tk=1024, name=n + "down_ln2", job=job)
    if job is not None:
        res, job_out = res
        hooks.done(job, job_out)
    x2, xh2, rs2, x2h = res
    saved = dict(x0h=x0h, proj=proj, retg=retg, raw=raw, states=states, sb=sb, sg=sg, merged=merged, r=(r1, r2, r3),
                 x1h=x1h, xh1=xh1, rs1=rs1, h1=h1, xh2=xh2, rs2=rs2)
    return x2, x2h, saved


def layer_backward(l, dx2, s, W, sm, rope, rconsts, hooks):
    n = f"l{l}_"
    two = ((F32, None), (BF16, None))
    gw, gs = {}, {}
    job = hooks.bwd_job(l, "ln2")
    res = ln_bwd(dx2, s["xh2"], s["rs2"], sm["ln2_g"], name=n + "ln2_bwd", job=job)
    if job is not None:
        res, job_out = res
        hooks.done(job, job_out)
    du2, du2h, gs["ln2_g"], gs["ln2_b"] = res
    job = hooks.bwd_job(l, "g_down")
    gw["w_down"] = matmul(s["h1"], du2h, mode="tn", tm=1024, tn=1024, tk=4096, pro=_relu2, outs=two, name=n + "g_down", job=job)
    if job is not None:
        gw["w_down"], job_out = gw["w_down"]
        hooks.done(job, job_out)
    dh1 = matmul(du2h, W["w_down"], mode="nt", tm=2048, tn=1024, tk=1024, outs=((BF16, None),),
                 epi=lambda acc, h: (acc * (2.0 * jnp.maximum(h.astype(F32), 0.0)),), tiles=(s["h1"],), name=n + "d_h1")
    job = hooks.bwd_job(l, "g_up")
    gw["w_up"] = matmul(s["x1h"], dh1, mode="tn", tm=1024, tn=1024, tk=4096, outs=two, name=n + "g_up", job=job)
    if job is not None:
        gw["w_up"], job_out = gw["w_up"]
        hooks.done(job, job_out)
    dx1 = matmul(dh1, W["w_up"], mode="nt", tm=1024, tn=1024, tk=4096,
                 epi=lambda acc, d: (acc + ALPHA * d,), tiles=(du2,), name=n + "d_x1")
    du1, du1h, gs["ln1_g"], gs["ln1_b"] = ln_bwd(dx1, s["xh1"], s["rs1"], sm["ln1_g"], name=n + "ln1_bwd")
    gw["w_out"] = matmul(s["merged"], du1h, mode="tn", tm=1024, tn=1024, tk=4096, outs=two, name=n + "g_out")
    gate0 = C_GATE // 512
    dr1, dr2, dr3, dg1, dg2, dg3 = matmul(
        du1h, W["w_out"], mode="nt", tm=1024, tn=512, tk=1024, outs=((BF16, None),) * 6, epi=_merge_bwd_epi,
        tiles=(*s["r"], (s["proj"], gate0), (s["proj"], gate0 + 2), (s["proj"], gate0 + 4)), name=n + "d_merged")
    d_branch = {}
    for nm, a, dr in (("p_ret", s["retg"], dr1), ("p_sb", s["sb"], dr2), ("p_sgu", s["sg"], dr3)):
        gw[nm] = matmul(a, dr, mode="tn", tm=512, tn=1024, tk=2048, outs=two, name=n + "g_" + nm)
        d_branch[nm] = matmul(dr, W[nm], mode="nt", tm=1024, tn=512, tk=1024, name=n + "d_" + nm)
    job = hooks.pair(l, gw)
    dret, gs["ret_gn_g"], gs["ret_gn_b"], job_out = ret_bwd(s["proj"], *rope, rconsts, sm["ret_gn_g"], sm["ret_gn_b"],
                                                             s["raw"], s["states"], d_branch["p_ret"], name=n + "ret_bwd", job=job)
    if job is not None:
        hooks.done(job, job_out)
    job = hooks.scatter(l) if job is not None else None
    dsq, dsk, dsv, job_out = sb_bwd(s["proj"], s["sb"], d_branch["p_sb"], name=n + "sb_bwd", job=job)
    if job is not None:
        hooks.done(job, job_out)
    dsgu, gs["sgu_w"], dbias, gs["sgu_ln_g"], gs["sgu_ln_b"] = sgu_bwd(
        s["proj"], sm["sgu_ln_g"], sm["sgu_ln_b"], sm["sgu_w"], sm["sgu_bias"], d_branch["p_sgu"], name=n + "sgu_bwd")
    gs["sgu_b"] = dbias[:, :, 0]
    dproj = jnp.concatenate([dret, dsq, dsk, dsv, dsgu, dg1, dg2, dg3], axis=1)
    job = hooks.small(l, gs)
    gw["w_in"] = matmul(s["x0h"], dproj, mode="tn", tm=1024, tn=1536, tk=2048, outs=two, name=n + "g_in", job=job)
    if job is not None:
        gw["w_in"], job_out = gw["w_in"]
        hooks.done(job, job_out)
    job = hooks.tail(l, gw["w_in"])
    dx0 = matmul(dproj, W["w_in"], mode="nt", tm=1024, tn=1024, tk=2560,
                 epi=lambda acc, d: (acc + ALPHA * d,), tiles=(du1,), name=n + "d_x0", job=job)
    if job is not None:
        dx0, job_out = dx0
        hooks.done(job, job_out)
    return dx0, gw, gs


def local_step(x, target, small, plan):
    T = x.shape[0]
    rope = _rope_tables(T)
    rconsts = _ret_consts()
    sms = []
    for l in range(DEPTH):
        sm = {k: small[k][l][None, :] for k in SMALL if k not in ("sgu_w", "sgu_b")}
        sm["sgu_w"] = small["sgu_w"][l]
        sm["sgu_bias"] = jnp.broadcast_to(small["sgu_b"][l][:, :, None], (4, CHUNK, CHUNK))
        sms.append(sm)
    h, saved = x, []
    job = plan.first_job()
    hh = _rows_call(lambda a: (a,), [x], [BF16], name="cast_x", job=job)
    if job is not None:
        hh, job_out = hh
        plan.done(job, job_out)
    hh = hh[0]
    for l in range(DEPTH):
        h, hh, s = layer_forward(l, h, hh, plan.weights(l), sms[l], rope, rconsts, plan)
        saved.append(s)
    dy, sq = loss_head(h, target)
    gs = {k: [None] * DEPTH for k in SMALL}
    for l in reversed(range(DEPTH)):
        dy, gwl, gsl = layer_backward(l, dy, saved[l], plan.weights(l), sms[l], rope, rconsts, plan)
        plan.grads(l, gwl)
        for k in SMALL:
            gs[k][l] = gsl[k].reshape(small[k].shape[1:])
    return sq[0, 0], dy, {k: jnp.stack(v) for k, v in gs.items()}


EARLY_GRADS = ("p_ret", "p_sb", "p_sgu", "w_out", "w_up", "w_down")


class _StepPlan:
    def __init__(self, pos, shards16):
        self.pos = pos
        self.shards16 = shards16
        self.full = [dict() for _ in range(DEPTH)]
        self.gw = [None] * DEPTH
        self.bufs = {}
        self.sums = {}
        self.gs = [None] * DEPTH

    def first_job(self):
        return self._gather([(0, "w_in")])

    def weights(self, l):
        return self.full[l]

    def grads(self, l, gw):
        self.gw[l] = gw

    def _gather(self, items, chips=(0, 1, 2)):
        job = gather_job([self.shards16[l][k] for l, k in items], [BIG_AXIS[k] for _, k in items], chips)
        job.note = ("gather" if 2 in chips else "gather_part", items)
        return job

    def _pair(self, items):
        job = pair_job([g[1] for _, _, g in items], [BIG_AXIS[k] for _, k, _ in items])
        job.note = ("pair", items)
        return job

    def fwd_job(self, l, host):
        if host == "proj":
            return None
        if host == "sb":
            return self._gather([(l, k) for k in BIG[1:]])
        if l + 1 == DEPTH:
            return None
        return self._gather([(l + 1, "w_in")], (0, 1) if host == "up" else (2,))

    def bwd_job(self, l, host):
        if l + 1 == DEPTH:
            return None
        if host == "ln2":
            job = self._pair([(l + 1, "w_in", self.gw[l + 1]["w_in"])])
            job.note = ("pair_w_in", job.note[1])
            return job
        items, sums16 = self.summed_w_in
        job = scatter_job([l_ for l_, _, _ in items], sums16, [BIG_AXIS[k] for _, k, _ in items],
                          [self.bufs.get(k) for _, k, _ in items], (0, 1) if host == "g_down" else (2,))
        job.note = ("scatter", items)
        return job

    def pair(self, l, ready):
        return self._pair([(l, k, ready[k]) for k in EARLY_GRADS])

    def scatter(self, l):
        items, sums16 = self.summed
        job = scatter_job([l_ for l_, _, _ in items], sums16, [BIG_AXIS[k] for _, k, _ in items],
                          [self.bufs.get(k) for _, k, _ in items])
        job.note = ("scatter", items)
        return job

    def small(self, l, gs):
        self.gs[l] = {k: gs[k].reshape(-1) for k in SMALL}
        if l != 0:
            return None
        job = small_job(_pack_small({k: jnp.stack([self.gs[l_][k] for l_ in range(DEPTH)]) for k in SMALL}))
        job.note = ("small", [])
        return job

    def tail(self, l, g):
        if l != 0:
            return None
        last = self._pair([(0, "w_in", g)])
        self.done(last, run_job(last, name="pair_last"))
        return self.scatter(0)

    def done(self, job, outs):
        kind, items = job.note
        if kind == "small":
            self.small_slots = outs[0]
        if kind in ("pair", "pair_w_in"):
            sums16 = []
            for a, (l, k, g) in enumerate(items):
                self.sums[(l, k)], s16 = pair_sum(self.pos, outs[a], g[0], BIG_AXIS[k], name=f"pair_sum_{k}_{l}")
                sums16.append(s16)
            if kind == "pair":
                self.summed = (items, sums16)
            else:
                self.summed_w_in = (items, sums16)
        for a, item in enumerate(items):
            if kind == "gather_part":
                self.shards16[item[0]][item[1]] = outs[a]
            elif kind == "gather":
                self.full[item[0]][item[1]] = outs[a]
            elif kind == "scatter":
                self.bufs[item[1]] = outs[a]

    def finish(self):
        return self.bufs, self.sums


def _flat2(a):
    return a.reshape(-1, a.shape[-1])


def _pack_small(d, pre=""):
    return jnp.concatenate([d[pre + k].reshape(-1) for k in SMALL]).reshape(-1, 128)


def kernel(x, w_in, ret_gn_g, ret_gn_b, sgu_ln_g, sgu_ln_b, sgu_w, sgu_b, p_ret, p_sb, p_sgu, w_out, ln1_g, ln1_b, w_up, w_down, ln2_g, ln2_b, loss_target, m_w_in, m_ret_gn_g, m_ret_gn_b, m_sgu_ln_g, m_sgu_ln_b, m_sgu_w, m_sgu_b, m_p_ret, m_p_sb, m_p_sgu, m_w_out, m_ln1_g, m_ln1_b, m_w_up, m_w_down, m_ln2_g, m_ln2_b, v_w_in, v_ret_gn_g, v_ret_gn_b, v_sgu_ln_g, v_sgu_ln_b, v_sgu_w, v_sgu_b, v_p_ret, v_p_sb, v_p_sgu, v_w_out, v_ln1_g, v_ln1_b, v_w_up, v_w_down, v_ln2_g, v_ln2_b):
    given = dict(locals())
    order = BIG[:1] + SMALL[:6] + BIG[1:5] + SMALL[6:8] + BIG[5:7] + SMALL[8:10]
    L = DEPTH

    px, py, pc = _place()
    pos = jnp.stack([px, py, pc, 2 * px + py]).astype(jnp.int32)

    shards16 = [{k: cast_into_whole(pos, given[k], l, BIG_AXIS[k], name=f"cast_{k}_{l}") for k in BIG} for l in range(L)]
    plan = _StepPlan(pos, shards16)
    sq, dx, _ = local_step(x[0], loss_target[0], {k: given[k] for k in SMALL}, plan)
    loss = 0.5 * lax.psum(sq, ("x", "y", "c"))

    bufs, sums = plan.finish()
    shards = []
    for k in BIG:
        whole = None
        for l in range(L):
            whole = chip_sum(pos, sums[(l, k)], bufs[k], l, BIG_AXIS[k], whole, name=f"chip_sum_{k}_{l}")
        shards.append(whole)
    joined = run_job(join_job(shards), name="join_halves")
    out = {}
    for a, k in enumerate(BIG):
        shp = given[k].shape
        res = _rows_call(lambda g_, w_, m_, v_: (g_,) + _adamw(w_, g_, m_, v_),
                         [joined[a].reshape(-1, shp[-1]), _flat2(given[k]), _flat2(given["m_" + k]), _flat2(given["v_" + k])],
                         [F32] * 4, name="adamw_" + k)
        out[k] = [r.reshape(shp) for r in res]

    pack = _pack_small
    res = _rows_call(lambda g_, w_, m_, v_: (g_,) + _adamw(w_, g_, m_, v_),
                     [small_sum(plan.small_slots), pack(given), pack(given, "m_"), pack(given, "v_")], [F32] * 4,
                     name="adamw_small", tr=8 * 47)
    off = 0
    for k in SMALL:
        sz = given[k].size
        out[k] = [r.reshape(-1)[off:off + sz].reshape(given[k].shape) for r in res]
        off += sz

    grads = [out[k][0] for k in order]
    deltas = [out[k][1] for k in order]
    new_m = [out[k][2] for k in order]
    new_v = [out[k][3] for k in order]
    return (loss, dx[None], *grads, *deltas, *new_m, *new_v)
```

```python
import functools
import math

import jax
import jax.numpy as jnp
from jax import lax
from jax.experimental import pallas as pl
from jax.experimental.pallas import tpu as pltpu

F32 = jnp.float32
BF16 = jnp.bfloat16

D_MODEL = 1024
SEQ = 4096
DEPTH = 2
CHUNK = 128
RET_HEADS = 4
BRANCH_W = 512
N_IN = 7680
D_FF = 4096
LN_EPS = 1e-5
ROPE_BASE = 10000.0
ALPHA = (2 * DEPTH) ** 0.25
RET_SCALE = 128 ** -0.5
SB_SCALE = 64 ** -0.5
C_RET, C_SB, C_SGU, C_GATE = 0, 2048, 3584, 4608

ADAM_LR, ADAM_B1, ADAM_B2, ADAM_EPS, ADAM_WD, ADAM_STEP = 0.001, 0.9, 0.999, 1e-08, 0.01, 10

N_CHIPS = 4
VMEM_LIMIT = 56 * 1024 * 1024
MESH = pl.DeviceIdType.MESH

NN = ((1,), (0,))
NT = ((1,), (1,))
TN = ((0,), (0,))


def _dot(a, b, dims):
    return lax.dot_general(a, b, (dims, ((), ())), preferred_element_type=F32)


def _params(sem):
    return pltpu.CompilerParams(dimension_semantics=sem, vmem_limit_bytes=VMEM_LIMIT)


def _relu2(h):
    r = jnp.maximum(h.astype(F32), 0.0)
    return r * r


def matmul(a, b, *, mode, tm, tn, tk, outs=((F32, None),), pro=None, epi=None, tiles=(), rows=(), name, job=None):
    if mode == "nn":
        (M, K), N = a.shape, b.shape[1]
    elif mode == "nt":
        (M, K), N = a.shape, b.shape[0]
    else:
        (K, M), N = a.shape, b.shape[1]
    tm, tn, tk = min(tm, M), min(tn, N), min(tk, K)
    assert M % tm == 0 and N % tn == 0 and K % tk == 0, (name, M, N, K, tm, tn, tk)
    if mode == "nn":
        a_spec = pl.BlockSpec((tm, tk), lambda i, j, k: (i, k))
        b_spec = pl.BlockSpec((tk, tn), lambda i, j, k: (k, j))
        dims = NN
    elif mode == "nt":
        a_spec = pl.BlockSpec((tm, tk), lambda i, j, k: (i, k))
        b_spec = pl.BlockSpec((tn, tk), lambda i, j, k: (j, k))
        dims = NT
    else:
        a_spec = pl.BlockSpec((tk, tm), lambda i, j, k: (k, i))
        b_spec = pl.BlockSpec((tk, tn), lambda i, j, k: (k, j))
        dims = TN
    nk = K // tk
    nt_, nr, no = len(tiles), len(rows), len(outs)

    def body(a_ref, b_ref, *rest):
        tile_refs = rest[:nt_]
        row_refs = rest[nt_:nt_ + nr]
        out_refs = rest[nt_ + nr:nt_ + nr + no]
        av = a_ref[...]
        if pro is not None:
            av = pro(av)
        p = _dot(av.astype(BF16), b_ref[...].astype(BF16), dims)

        def finish(acc):
            vals = (acc,) * no if epi is None else epi(acc, *[r[...] for r in tile_refs], *[r[...] for r in row_refs])
            for o_ref, v in zip(out_refs, vals):
                o_ref[...] = v.astype(o_ref.dtype)

        if nk == 1:
            finish(p)
        else:
            acc_ref = rest[-1]
            k = pl.program_id(2)

            @pl.when(k == 0)
            def _():
                acc_ref[...] = p

            @pl.when(k > 0)
            def _():
                acc_ref[...] += p

            @pl.when(k == nk - 1)
            def _():
                finish(acc_ref[...])

    out_shape, out_specs = [], []
    for dt, width in outs:
        if width is None:
            out_shape.append(jax.ShapeDtypeStruct((M, N), dt))
            out_specs.append(pl.BlockSpec((tm, tn), lambda i, j, k: (i, j)))
        else:
            assert N == tn
            out_shape.append(jax.ShapeDtypeStruct((M, width), dt))
            out_specs.append(pl.BlockSpec((tm, width), lambda i, j, k: (i, 0)))
    in_specs = [a_spec, b_spec]
    offs = [t[1] if isinstance(t, tuple) else 0 for t in tiles]
    tiles = [t[0] if isinstance(t, tuple) else t for t in tiles]
    in_specs += [pl.BlockSpec((tm, tn), functools.partial(lambda i, j, k, o: (i, j + o), o=o)) for o in offs]
    in_specs += [pl.BlockSpec((1, tn), lambda i, j, k: (0, j)) for _ in rows]
    grid = (M // tm, N // tn, nk)
    scratch = [pltpu.VMEM((tm, tn), F32)] if nk > 1 else []
    j = _job_args(job, len(in_specs), no)
    res = pl.pallas_call(
        _hosting(body, job, len(in_specs), no, len(scratch), grid), name=name, grid=grid,
        in_specs=in_specs + j["in_specs"], out_specs=out_specs + j["out_specs"], out_shape=out_shape + j["out_shape"],
        scratch_shapes=scratch + j["scratch"], input_output_aliases=j["aliases"],
        compiler_params=_params(("parallel", "parallel", "arbitrary") if job is None else ("arbitrary",) * 3),
    )(a, b, *tiles, *rows, *j["ins"])
    mine = res[0] if no == 1 else list(res[:no])
    return mine if job is None else (mine, list(res[no:]))


def _ln_epi(acc, res, g, b):
    u = ALPHA * res + acc
    mu = jnp.mean(u, axis=-1, keepdims=True)
    xc = u - mu
    var = jnp.mean(xc * xc, axis=-1, keepdims=True)
    rstd = lax.rsqrt(var + LN_EPS)
    xhat = xc * rstd
    y = xhat * g + b
    return y, xhat, jnp.broadcast_to(rstd, (u.shape[0], 128)), y


def matmul_ln(a, w, res, g, b, *, pro=None, tk, name, job=None):
    n = w.shape[1]
    return matmul(a, w, mode="nn", tm=1024, tn=n, tk=tk, pro=pro, epi=_ln_epi, tiles=(res,), rows=(g, b),
                  outs=((F32, None), (F32, None), (F32, 128), (BF16, None)), name=name, job=job)


def ln_bwd(dy, xhat, rstd, g, *, name, job=None):
    T, D = dy.shape
    tm = min(512, T)

    def body(dy_ref, xh_ref, rs_ref, g_ref, du_ref, du16_ref, dg_ref, db_ref):
        dyv, xh = dy_ref[...], xh_ref[...]
        r = rs_ref[:, 0:1]
        dxh = dyv * g_ref[...]
        m1 = jnp.mean(dxh, axis=-1, keepdims=True)
        m2 = jnp.mean(dxh * xh, axis=-1, keepdims=True)
        du = r * (dxh - m1 - xh * m2)
        du_ref[...] = du
        du16_ref[...] = du.astype(BF16)

        @pl.when(pl.program_id(0) == 0)
        def _():
            dg_ref[...] = jnp.zeros_like(dg_ref)
            db_ref[...] = jnp.zeros_like(db_ref)

        dg_ref[...] += jnp.sum(dyv * xh, axis=0, keepdims=True)
        db_ref[...] += jnp.sum(dyv, axis=0, keepdims=True)

    row = pl.BlockSpec((tm, D), lambda i: (i, 0))
    vec = pl.BlockSpec((1, D), lambda i: (0, 0))
    j = _job_args(job, 4, 4)
    res = pl.pallas_call(
        _hosting(body, job, 4, 4, 0, T // tm), name=name, grid=(T // tm,),
        in_specs=[row, row, pl.BlockSpec((tm, 128), lambda i: (i, 0)), vec] + j["in_specs"],
        out_specs=[row, row, vec, vec] + j["out_specs"],
        out_shape=[jax.ShapeDtypeStruct((T, D), F32), jax.ShapeDtypeStruct((T, D), BF16),
                   jax.ShapeDtypeStruct((1, D), F32), jax.ShapeDtypeStruct((1, D), F32)] + j["out_shape"],
        scratch_shapes=j["scratch"], input_output_aliases=j["aliases"],
        compiler_params=_params(("arbitrary",)),
    )(dy, xhat, rstd, g, *j["ins"])
    return list(res[:4]) if job is None else (list(res[:4]), list(res[4:]))


def loss_head(y, target):
    T, D = y.shape
    tm = min(512, T)

    def body(y_ref, t_ref, dy_ref, s_ref):
        e = y_ref[...] - t_ref[...]
        dy_ref[...] = e * (1.0 / D)

        @pl.when(pl.program_id(0) == 0)
        def _():
            s_ref[...] = jnp.zeros_like(s_ref)

        s_ref[...] += jnp.sum(jnp.mean(e * e, axis=-1, keepdims=True))

    row = pl.BlockSpec((tm, D), lambda i: (i, 0))
    return pl.pallas_call(
        body, name="loss_head", grid=(T // tm,),
        in_specs=[row, row], out_specs=[row, pl.BlockSpec((8, 128), lambda i: (0, 0))],
        out_shape=[jax.ShapeDtypeStruct((T, D), F32), jax.ShapeDtypeStruct((8, 128), F32)],
        compiler_params=_params(("arbitrary",)),
    )(y, target)


def _rope_tables(T):
    half = 64
    inv_freq = ROPE_BASE ** (-jnp.arange(half, dtype=F32) / half)
    ang = jnp.arange(T, dtype=jnp.int32).astype(F32)[:, None] * inv_freq[None, :]
    cos, sin = jnp.cos(ang), jnp.sin(ang)
    return jnp.concatenate([cos, cos], axis=1), jnp.concatenate([-sin, sin], axis=1)


def _ret_consts():
    H = RET_HEADS
    log_g = jnp.log(1.0 - 2.0 ** (-5.0 - jnp.arange(H, dtype=F32)))
    idx = jnp.arange(CHUNK, dtype=F32)
    diff = idx[:, None] - idx[None, :]
    dmat = jnp.where(diff[None] >= 0, jnp.exp(log_g[:, None, None] * diff[None]), 0.0)
    kd = jnp.exp(log_g[:, None] * (CHUNK - 1 - idx)[None, :])
    qd = jnp.exp(log_g[:, None] * (idx + 1.0)[None, :])
    cd = jnp.exp(log_g * CHUNK)
    full = (H, CHUNK, CHUNK)
    return (dmat.astype(F32), jnp.broadcast_to(kd[:, :, None], full), jnp.broadcast_to(qd[:, :, None], full),
            jnp.broadcast_to(cd[:, None, None], full))


def _swap_halves(v):
    return pltpu.roll(v, 64, 1)


def _group_norm(o):
    mu = jnp.mean(o, axis=-1, keepdims=True)
    xc = o - mu
    var = jnp.mean(xc * xc, axis=-1, keepdims=True)
    rstd = lax.rsqrt(var + LN_EPS)
    return xc * rstd, rstd


def ret_fwd(proj, cosf, sinf, consts, gn_g, gn_b, *, name):
    T = proj.shape[0]
    tb = min(512, T)
    nch = tb // CHUNK
    H = RET_HEADS

    def body(p_ref, cos_ref, sin_ref, dm_ref, kd_ref, qd_ref, cd_ref, g_ref, b_ref, out_ref, raw_ref, st_ref, s_ref):
        @pl.when(pl.program_id(0) == 0)
        def _():
            s_ref[...] = jnp.zeros_like(s_ref)

        for c in range(nch):
            r = slice(c * CHUNK, (c + 1) * CHUNK)
            cs, sn = cos_ref[r, :], sin_ref[r, :]
            for h in range(H):
                hc = slice(h * 128, (h + 1) * 128)
                q = p_ref[r, h * 128:(h + 1) * 128]
                k = p_ref[r, 512 + h * 128:512 + (h + 1) * 128]
                v = p_ref[r, 1024 + h * 128:1024 + (h + 1) * 128]
                gt = p_ref[r, 1536 + h * 128:1536 + (h + 1) * 128]
                qr = q * cs + _swap_halves(q) * sn
                kr = (k * cs + _swap_halves(k) * sn) * RET_SCALE
                sprev = s_ref[h]
                st_ref[c, h] = sprev
                qb, kb, vb = qr.astype(BF16), kr.astype(BF16), v.astype(BF16)
                s = _dot(qb, kb, NT) * dm_ref[h]
                o = _dot(s.astype(BF16), vb, NN) + _dot((qr * qd_ref[h]).astype(BF16), sprev.astype(BF16), NN)
                s_ref[h] = sprev * cd_ref[h] + _dot((kr * kd_ref[h]).astype(BF16), vb, TN)
                raw_ref[r, hc] = o
                y, _ = _group_norm(o)
                out_ref[r, hc] = (gt * jax.nn.sigmoid(gt)) * (y * g_ref[:, hc] + b_ref[:, hc])

    cmat = pl.BlockSpec((H, CHUNK, CHUNK), lambda i: (0, 0, 0))
    vec = pl.BlockSpec((1, BRANCH_W), lambda i: (0, 0))
    rope = pl.BlockSpec((tb, 128), lambda i: (i, 0))
    blk = pl.BlockSpec((tb, BRANCH_W), lambda i: (i, 0))
    return pl.pallas_call(
        body, name=name, grid=(T // tb,),
        in_specs=[pl.BlockSpec((tb, 2048), lambda i: (i, 0)), rope, rope, cmat, cmat, cmat, cmat, vec, vec],
        out_specs=[blk, blk, pl.BlockSpec((nch, H, CHUNK, CHUNK), lambda i: (i, 0, 0, 0))],
        out_shape=[jax.ShapeDtypeStruct((T, BRANCH_W), F32), jax.ShapeDtypeStruct((T, BRANCH_W), F32),
                   jax.ShapeDtypeStruct((T // CHUNK, H, CHUNK, CHUNK), F32)],
        scratch_shapes=[pltpu.VMEM((H, CHUNK, CHUNK), F32)],
        compiler_params=_params(("arbitrary",)),
    )(proj, cosf, sinf, *consts, gn_g, gn_b)


def ret_bwd(proj, cosf, sinf, consts, gn_g, gn_b, raw, states, dout, *, name, job=None):
    T = proj.shape[0]
    tb = min(512, T)
    nch = tb // CHUNK
    nb = T // tb
    H = RET_HEADS

    def body(p_ref, cos_ref, sin_ref, dm_ref, kd_ref, qd_ref, cd_ref, g_ref, b_ref, raw_ref, st_ref, do_ref,
             dp_ref, dg_ref, db_ref, ds_ref):
        @pl.when(pl.program_id(0) == 0)
        def _():
            ds_ref[...] = jnp.zeros_like(ds_ref)
            dg_ref[...] = jnp.zeros_like(dg_ref)
            db_ref[...] = jnp.zeros_like(db_ref)

        for c in reversed(range(nch)):
            r = slice(c * CHUNK, (c + 1) * CHUNK)
            cs, sn = cos_ref[r, :], sin_ref[r, :]
            for h in range(H):
                hc = slice(h * 128, (h + 1) * 128)
                q = p_ref[r, h * 128:(h + 1) * 128]
                k = p_ref[r, 512 + h * 128:512 + (h + 1) * 128]
                v = p_ref[r, 1024 + h * 128:1024 + (h + 1) * 128]
                gt = p_ref[r, 1536 + h * 128:1536 + (h + 1) * 128]
                qr = q * cs + _swap_halves(q) * sn
                kr = (k * cs + _swap_halves(k) * sn) * RET_SCALE
                sprev = st_ref[c, h]
                gv = g_ref[:, hc]
                y, rstd = _group_norm(raw_ref[r, hc])
                d_out = do_ref[r, hc]
                sg = jax.nn.sigmoid(gt)
                d_gate = d_out * (y * gv + b_ref[:, hc]) * (sg * (1.0 + gt * (1.0 - sg)))
                d_aff = d_out * (gt * sg)
                dg_ref[:, hc] += jnp.sum(d_aff * y, axis=0, keepdims=True)
                db_ref[:, hc] += jnp.sum(d_aff, axis=0, keepdims=True)
                dxh = d_aff * gv
                m1 = jnp.mean(dxh, axis=-1, keepdims=True)
                m2 = jnp.mean(dxh * y, axis=-1, keepdims=True)
                d_o = (rstd * (dxh - m1 - y * m2)).astype(BF16)
                qb, kb, vb = qr.astype(BF16), kr.astype(BF16), v.astype(BF16)
                dm, kd, qd = dm_ref[h], kd_ref[h], qd_ref[h]
                p = (_dot(qb, kb, NT) * dm).astype(BF16)
                dp = (_dot(d_o, vb, NT) * dm).astype(BF16)
                dsn = ds_ref[h]
                dsb = dsn.astype(BF16)
                dq_r = _dot(dp, kb, NN) + _dot(d_o, sprev.astype(BF16), NT) * qd
                dk_r = (_dot(dp, qb, TN) + _dot(vb, dsb, NT) * kd) * RET_SCALE
                d_v = _dot(p, d_o, TN) + _dot((kr * kd).astype(BF16), dsb, NN)
                ds_ref[h] = dsn * cd_ref[h] + _dot((qr * qd).astype(BF16), d_o, TN)
                dp_ref[r, h * 128:(h + 1) * 128] = (dq_r * cs - _swap_halves(dq_r) * sn).astype(BF16)
                dp_ref[r, 512 + h * 128:512 + (h + 1) * 128] = (dk_r * cs - _swap_halves(dk_r) * sn).astype(BF16)
                dp_ref[r, 1024 + h * 128:1024 + (h + 1) * 128] = d_v.astype(BF16)
                dp_ref[r, 1536 + h * 128:1536 + (h + 1) * 128] = d_gate.astype(BF16)

    cmat = pl.BlockSpec((H, CHUNK, CHUNK), lambda i: (0, 0, 0))
    vec = pl.BlockSpec((1, BRANCH_W), lambda i: (0, 0))
    rope = pl.BlockSpec((tb, 128), lambda i: (nb - 1 - i, 0))
    blk = pl.BlockSpec((tb, BRANCH_W), lambda i: (nb - 1 - i, 0))
    wide = pl.BlockSpec((tb, 2048), lambda i: (nb - 1 - i, 0))
    j = _job_args(job, 12, 3)
    res = pl.pallas_call(
        _hosting(body, job, 12, 3, 1, nb), name=name, grid=(nb,),
        in_specs=[wide, rope, rope, cmat, cmat, cmat, cmat, vec, vec, blk,
                  pl.BlockSpec((nch, H, CHUNK, CHUNK), lambda i: (nb - 1 - i, 0, 0, 0)), blk] + j["in_specs"],
        out_specs=[wide, vec, vec] + j["out_specs"],
        out_shape=[jax.ShapeDtypeStruct((T, 2048), BF16), jax.ShapeDtypeStruct((1, BRANCH_W), F32),
                   jax.ShapeDtypeStruct((1, BRANCH_W), F32)] + j["out_shape"],
        scratch_shapes=[pltpu.VMEM((H, CHUNK, CHUNK), F32)] + j["scratch"], input_output_aliases=j["aliases"],
        compiler_params=_params(("arbitrary",)),
    )(proj, cosf, sinf, *consts, gn_g, gn_b, raw, states, dout, *j["ins"])
    return res[0], res[1], res[2], list(res[3:])


def _sb_masks():
    row = lax.broadcasted_iota(jnp.int32, (CHUNK, CHUNK), 0)
    lane = lax.broadcasted_iota(jnp.int32, (CHUNK, CHUNK), 1)
    return row, lane


SB_QT = 256
SB_DEAD = -105.0


def _pair(v):
    hi = v.astype(BF16)
    return jnp.concatenate([hi, (v - hi.astype(F32)).astype(BF16)], axis=1)


def _sb_consts():
    r = lax.broadcasted_iota(jnp.int32, (256, 256), 0) & 127
    c = lax.broadcasted_iota(jnp.int32, (256, 256), 1)
    ones = c >= 128
    lane = lax.broadcasted_iota(jnp.int32, (CHUNK, CHUNK), 1)
    return (ones | (r > c)).astype(BF16), (ones | (r >= c)).astype(BF16), (lane < 64, lane >= 64)


def _per_head(x, hms):
    return jnp.concatenate([jnp.where(hm, x, 0.0) for hm in hms], axis=0).astype(BF16)


def _sb_logits(qb, kb2, mask2):
    z = _dot(qb, kb2, NT)
    l1p = jnp.log(1.0 + jnp.exp(-jnp.abs(z)))
    lsp = jnp.minimum(z, 0.0) - l1p
    lsn = lsp - z
    if mask2 is not None:
        lsn = jnp.where(mask2, lsn, 0.0)
    return lsp, lsn


def _sb_tile_mask(qt):
    trow = lax.broadcasted_iota(jnp.int32, (qt, 256), 0)
    tlane = lax.broadcasted_iota(jnp.int32, (qt, 256), 1) & 127
    return lambda m: (tlane + m * CHUNK) < trow


def sb_fwd(proj, *, name, job=None):
    T = proj.shape[0]
    qt = min(SB_QT, T)
    nsub = qt // CHUNK
    cb = C_SB // 128

    def body(q_ref, k_ref, v_ref, o_ref):
        u_gt, _, hms = _sb_consts()
        tile_mask = _sb_tile_mask(qt)

        def qtile(i, _):
            rq = pl.ds(pl.multiple_of(i * qt, qt), qt)
            qb = (q_ref[rq, :] * SB_SCALE).astype(BF16)

            def group(js, masks, state):
                carry, acc = list(state[:2]), state[2]
                rows = [pl.ds(pl.multiple_of(j * CHUNK, CHUNK), CHUNK) for j in js]
                logits = [_sb_logits(qb, _per_head(k_ref[rk, :], hms), m) for rk, m in zip(rows, masks)]
                sums = [[_dot(_pair(lsn[:, h * 128:(h + 1) * 128]), u_gt, NN) for h in range(2)] for _, lsn in logits]
                weights = []
                for (lsp, _), r, m in zip(logits, sums, masks):
                    a_b = []
                    for h in range(2):
                        hc = slice(h * 128, (h + 1) * 128)
                        a = jnp.exp(lsp[:, hc] + r[h][:, :128] + carry[h])
                        if m is not None:
                            a = jnp.where(m[:, hc], a, 0.0)
                        carry[h] = carry[h] + r[h][:, 128:]
                        a_b.append(a.astype(BF16))
                    weights.append(jnp.concatenate(a_b, axis=1))
                for rk, a in zip(rows, weights):
                    acc = acc + _dot(a, _per_head(v_ref[rk, :], hms), NN)
                return carry[0], carry[1], acc

            zero = jnp.zeros((qt, 128), F32)
            diag = list(reversed(range(nsub)))
            state = group([i * nsub + m for m in diag], [tile_mask(m) for m in diag], (zero, zero, zero))

            def live(c):
                return jnp.logical_and(c[0] < i, jnp.maximum(jnp.max(c[1][0]), jnp.max(c[1][1])) > SB_DEAD)

            def blocks(c):
                jj, st = c
                return jj + 1, group([(i - jj) * nsub - 1 - u for u in range(nsub)], [None] * nsub, st)

            _, state = lax.while_loop(live, blocks, (jnp.int32(0), state))
            o_ref[rq, :] = state[2]
            return 0

        lax.fori_loop(0, T // qt, qtile, 0)

    def col(off):
        return pl.BlockSpec((T, 128), lambda hp: (0, off + hp))

    steps = BRANCH_W // 128
    j = _job_args(job, 3, 1)
    res = pl.pallas_call(
        _hosting(body, job, 3, 1, 0, steps), name=name, grid=(steps,),
        in_specs=[col(cb), col(cb + 4), col(cb + 8)] + j["in_specs"], out_specs=[col(0)] + j["out_specs"],
        out_shape=[jax.ShapeDtypeStruct((T, BRANCH_W), F32)] + j["out_shape"],
        scratch_shapes=j["scratch"], input_output_aliases=j["aliases"],
        compiler_params=_params(("parallel",) if job is None else ("arbitrary",)),
    )(proj, proj, proj, *j["ins"])
    return res[0], list(res[1:])


def sb_bwd(proj, out, dout, *, name, job=None):
    T = proj.shape[0]
    qt = min(SB_QT, T)
    nsub = qt // CHUNK
    cb = C_SB // 128

    def body(q_ref, k_ref, v_ref, o_ref, do_ref, dq_ref, dk_ref, dv_ref, dkt_ref, dvt_ref):
        u_gt, u_ge, hms = _sb_consts()
        tile_mask = _sb_tile_mask(qt)
        tall_lane = lax.broadcasted_iota(jnp.int32, (qt, 128), 1)
        top = lax.broadcasted_iota(jnp.int32, (CHUNK, CHUNK), 0) < 64
        dkt_ref[...] = jnp.zeros_like(dkt_ref)
        dvt_ref[...] = jnp.zeros_like(dvt_ref)

        def qtile(i, _):
            rq = pl.ds(pl.multiple_of(i * qt, qt), qt)
            qs = q_ref[rq, :] * SB_SCALE
            qb, q_t = qs.astype(BF16), qs.T.astype(BF16)
            dov = do_ref[rq, :]
            dob, do_t = dov.astype(BF16), dov.T.astype(BF16)
            prod = dob.astype(F32) * o_ref[rq, :]
            total = [jnp.broadcast_to(jnp.sum(jnp.where(hm, prod, 0.0), axis=1, keepdims=True), (qt, 128))
                     for hm in (tall_lane < 64, tall_lane >= 64)]

            def group(js, masks, state):
                c_l, c_w, dq = list(state[:2]), list(state[2:4]), state[4]
                heads = [slice(h * 128, (h + 1) * 128) for h in range(2)]
                rows = [pl.ds(pl.multiple_of(j * CHUNK, CHUNK), CHUNK) for j in js]
                kb2 = [_per_head(k_ref[rk, :], hms) for rk in rows]
                logits = [_sb_logits(qb, kb, m) for kb, m in zip(kb2, masks)]
                da = [_dot(dob, _per_head(v_ref[rk, :], hms), NT) for rk in rows]
                sums = [[_dot(_pair(lsn[:, hc]), u_gt, NN) for hc in heads] for _, lsn in logits]
                a_b, w_all = [], []
                for (lsp, _), r, d, m in zip(logits, sums, da, masks):
                    a_h, w_h = [], []
                    for h, hc in enumerate(heads):
                        a = jnp.exp(lsp[:, hc] + r[h][:, :128] + c_l[h])
                        if m is not None:
                            a = jnp.where(m[:, hc], a, 0.0)
                        c_l[h] = c_l[h] + r[h][:, 128:]
                        a = a.astype(BF16)
                        a_h.append(a)
                        w_h.append(a.astype(F32) * d[:, hc])
                    a_b.append(jnp.concatenate(a_h, axis=1))
                    w_all.append(w_h)
                sums_w = [[_dot(_pair(w), u_ge, NN) for w in w_h] for w_h in w_all]
                dz_b = []
                for (lsp, _), w_h, r, m in zip(logits, w_all, sums_w, masks):
                    sp = jnp.exp(lsp)
                    dz_h = []
                    for h, hc in enumerate(heads):
                        later_w = r[h][:, :128] + c_w[h]
                        c_w[h] = c_w[h] + r[h][:, 128:]
                        dz = w_h[h] * (1.0 - sp[:, hc]) - sp[:, hc] * (total[h] - later_w)
                        if m is not None:
                            dz = jnp.where(m[:, hc], dz, 0.0)
                        dz_h.append(dz.astype(BF16))
                    dz_b.append(jnp.concatenate(dz_h, axis=1))
                for j, kb, a, dz in zip(js, kb2, a_b, dz_b):
                    dkt = _dot(q_t, dz, NN)
                    dvt = _dot(do_t, a, NN)
                    dkt_ref[j] += jnp.where(top, dkt[:, :128], dkt[:, 128:])
                    dvt_ref[j] += jnp.where(top, dvt[:, :128], dvt[:, 128:])
                    dq = dq + _dot(dz, kb, NN)
                return c_l[0], c_l[1], c_w[0], c_w[1], dq

            zero = jnp.zeros((qt, 128), F32)
            diag = list(reversed(range(nsub)))
            state = group([i * nsub + m for m in diag], [tile_mask(m) for m in diag], (zero,) * 5)

            def live(c):
                return jnp.logical_and(c[0] < i, jnp.maximum(jnp.max(c[1][0]), jnp.max(c[1][1])) > SB_DEAD)

            def blocks(c):
                jj, st = c
                return jj + 1, group([(i - jj) * nsub - 1 - u for u in range(nsub)], [None] * nsub, st)

            _, state = lax.while_loop(live, blocks, (jnp.int32(0), state))
            dq_ref[rq, :] = (state[4] * SB_SCALE).astype(BF16)
            return 0

        lax.fori_loop(0, T // qt, qtile, 0)

        def untranspose(jb, _):
            rk = pl.ds(pl.multiple_of(jb * CHUNK, CHUNK), CHUNK)
            dk_ref[rk, :] = dkt_ref[jb].T.astype(BF16)
            dv_ref[rk, :] = dvt_ref[jb].T.astype(BF16)
            return 0

        lax.fori_loop(0, T // CHUNK, untranspose, 0)

    def col(off):
        return pl.BlockSpec((T, 128), lambda hp: (0, off + hp))

    o16 = jax.ShapeDtypeStruct((T, BRANCH_W), BF16)
    steps = BRANCH_W // 128
    j = _job_args(job, 5, 3)
    acc = pltpu.VMEM((T // CHUNK, CHUNK, CHUNK), F32)
    res = pl.pallas_call(
        _hosting(body, job, 5, 3, 2, steps), name=name, grid=(steps,),
        in_specs=[col(cb), col(cb + 4), col(cb + 8), col(0), col(0)] + j["in_specs"],
        out_specs=[col(0), col(0), col(0)] + j["out_specs"], out_shape=[o16, o16, o16] + j["out_shape"],
        scratch_shapes=[acc, acc] + j["scratch"], input_output_aliases=j["aliases"],
        compiler_params=_params(("parallel",) if job is None else ("arbitrary",)),
    )(proj, proj, proj, out, dout, *j["ins"])
    return res[0], res[1], res[2], list(res[3:])


_G0 = math.sqrt(2.0 / math.pi)
_G1 = 0.044715


def _gelu(x):
    return 0.5 * x * (1.0 + jnp.tanh(_G0 * (x + _G1 * x * x * x)))


def _gelu_grad(x):
    t = jnp.tanh(_G0 * (x + _G1 * x * x * x))
    return 0.5 * (1.0 + t) + 0.5 * x * (1.0 - t * t) * (_G0 * (1.0 + 3.0 * _G1 * x * x))


def _tril():
    row, lane = _sb_masks()
    return row >= lane


def sgu_fwd(proj, ln_g, ln_b, w, bias, *, name):
    T = proj.shape[0]
    tb = min(512, T)
    G = BRANCH_W // 128

    def body(u_ref, v_ref, g_ref, b_ref, w_ref, bias_ref, o_ref):
        vv = _gelu(v_ref[...])
        xh, _ = _group_norm(vv)
        vn = (xh * g_ref[...] + b_ref[...]).astype(BF16)
        tril = _tril()
        for g in range(G):
            wg = jnp.where(tril, w_ref[g], 0.0).astype(BF16)
            gc = slice(g * 128, (g + 1) * 128)
            for c in range(tb // CHUNK):
                r = slice(c * CHUNK, (c + 1) * CHUNK)
                sv = _dot(wg, vn[r, gc], NN) + bias_ref[g]
                o_ref[r, gc] = _gelu(u_ref[r, gc]) * sv

    cu, cv = C_SGU // BRANCH_W, C_SGU // BRANCH_W + 1
    vec = pl.BlockSpec((1, BRANCH_W), lambda i: (0, 0))
    mat = pl.BlockSpec((G, CHUNK, CHUNK), lambda i: (0, 0, 0))
    return pl.pallas_call(
        body, name=name, grid=(T // tb,),
        in_specs=[pl.BlockSpec((tb, BRANCH_W), lambda i: (i, cu)), pl.BlockSpec((tb, BRANCH_W), lambda i: (i, cv)),
                  vec, vec, mat, mat],
        out_specs=pl.BlockSpec((tb, BRANCH_W), lambda i: (i, 0)),
        out_shape=jax.ShapeDtypeStruct((T, BRANCH_W), F32),
        compiler_params=_params(("parallel",)),
    )(proj, proj, ln_g, ln_b, w, bias)


def sgu_bwd(proj, ln_g, ln_b, w, bias, dout, *, name):
    T = proj.shape[0]
    tb = min(512, T)
    G = BRANCH_W // 128

    def body(u_ref, v_ref, g_ref, b_ref, w_ref, bias_ref, do_ref, dp_ref, dw_ref, dbias_ref, dg_ref, db_ref, dvn_ref):
        @pl.when(pl.program_id(0) == 0)
        def _():
            dw_ref[...] = jnp.zeros_like(dw_ref)
            dbias_ref[...] = jnp.zeros_like(dbias_ref)
            dg_ref[...] = jnp.zeros_like(dg_ref)
            db_ref[...] = jnp.zeros_like(db_ref)

        gv = v_ref[...]
        vv = _gelu(gv)
        xh, rstd = _group_norm(vv)
        vn = (xh * g_ref[...] + b_ref[...]).astype(BF16)
        tril = _tril()
        for g in range(G):
            wg = jnp.where(tril, w_ref[g], 0.0).astype(BF16)
            gc = slice(g * 128, (g + 1) * 128)
            for c in range(tb // CHUNK):
                r = slice(c * CHUNK, (c + 1) * CHUNK)
                vn_c = vn[r, gc]
                sv = _dot(wg, vn_c, NN) + bias_ref[g]
                gu = u_ref[r, gc]
                d_o = do_ref[r, gc]
                dp_ref[r, gc] = (d_o * sv * _gelu_grad(gu)).astype(BF16)
                dsv = d_o * _gelu(gu)
                dsv_b = dsv.astype(BF16)
                dvn_ref[r, gc] = _dot(wg, dsv_b, TN)
                dw_ref[g] += jnp.where(tril, _dot(dsv_b, vn_c, NT), 0.0)
                dbias_ref[g] += jnp.broadcast_to(jnp.sum(dsv, axis=1, keepdims=True), (CHUNK, CHUNK))
        dvn = dvn_ref[...]
        dg_ref[...] += jnp.sum(dvn * xh, axis=0, keepdims=True)
        db_ref[...] += jnp.sum(dvn, axis=0, keepdims=True)
        dxh = dvn * g_ref[...]
        m1 = jnp.mean(dxh, axis=-1, keepdims=True)
        m2 = jnp.mean(dxh * xh, axis=-1, keepdims=True)
        dp_ref[:, BRANCH_W:2 * BRANCH_W] = (rstd * (dxh - m1 - xh * m2) * _gelu_grad(gv)).astype(BF16)

    cu, cv = C_SGU // BRANCH_W, C_SGU // BRANCH_W + 1
    vec = pl.BlockSpec((1, BRANCH_W), lambda i: (0, 0))
    mat = pl.BlockSpec((G, CHUNK, CHUNK), lambda i: (0, 0, 0))
    blk = pl.BlockSpec((tb, BRANCH_W), lambda i: (i, 0))
    msh = jax.ShapeDtypeStruct((G, CHUNK, CHUNK), F32)
    vsh = jax.ShapeDtypeStruct((1, BRANCH_W), F32)
    return pl.pallas_call(
        body, name=name, grid=(T // tb,),
        in_specs=[pl.BlockSpec((tb, BRANCH_W), lambda i: (i, cu)), pl.BlockSpec((tb, BRANCH_W), lambda i: (i, cv)),
                  vec, vec, mat, mat, blk],
        out_specs=[pl.BlockSpec((tb, 2 * BRANCH_W), lambda i: (i, 0)), mat, mat, vec, vec],
        out_shape=[jax.ShapeDtypeStruct((T, 2 * BRANCH_W), BF16), msh, msh, vsh, vsh],
        scratch_shapes=[pltpu.VMEM((tb, BRANCH_W), F32)],
        compiler_params=_params(("arbitrary",)),
    )(proj, proj, ln_g, ln_b, w, bias, dout)


def merge_fwd(a1, a2, a3, p1, p2, p3, proj, *, name):
    T = a1.shape[0]
    tm, tn = min(1024, T), 512
    gb = C_GATE // tn

    def body(a1_ref, a2_ref, a3_ref, p1_ref, p2_ref, p3_ref, g1_ref, g2_ref, g3_ref, m_ref, r1_ref, r2_ref, r3_ref):
        m = None
        for a_ref, p_ref, g_ref, r_ref in ((a1_ref, p1_ref, g1_ref, r1_ref), (a2_ref, p2_ref, g2_ref, r2_ref),
                                           (a3_ref, p3_ref, g3_ref, r3_ref)):
            r = _dot(a_ref[...].astype(BF16), p_ref[...], NN)
            r_ref[...] = r.astype(r_ref.dtype)
            t = jax.nn.sigmoid(g_ref[...]) * r
            m = t if m is None else m + t
        m_ref[...] = m.astype(m_ref.dtype)

    a_spec = pl.BlockSpec((tm, BRANCH_W), lambda i, j: (i, 0))
    p_spec = pl.BlockSpec((BRANCH_W, tn), lambda i, j: (0, j))
    o_spec = pl.BlockSpec((tm, tn), lambda i, j: (i, j))
    gates = [pl.BlockSpec((tm, tn), functools.partial(lambda i, j, o: (i, o + j), o=gb + 2 * n)) for n in range(3)]
    return pl.pallas_call(
        body, name=name, grid=(T // tm, D_MODEL // tn),
        in_specs=[a_spec, a_spec, a_spec, p_spec, p_spec, p_spec, *gates],
        out_specs=[o_spec] * 4, out_shape=[jax.ShapeDtypeStruct((T, D_MODEL), BF16)] * 4,
        compiler_params=_params(("parallel", "parallel")),
    )(a1, a2, a3, p1, p2, p3, proj, proj, proj)


def _merge_bwd_epi(dm, r1, r2, r3, g1, g2, g3):
    d_r, d_g = [], []
    for r, g in ((r1, g1), (r2, g2), (r3, g3)):
        s = jax.nn.sigmoid(g)
        d_r.append(dm * s)
        d_g.append(dm * r.astype(F32) * (s * (1.0 - s)))
    return (*d_r, *d_g)


def _rows_call(fn, ins, out_dtypes, *, name, tr=256, job=None):
    first = ins[0][0] if isinstance(ins[0], tuple) else ins[0]
    R, C = first.shape[-2:]
    tr = min(tr, R)
    assert R % tr == 0, (name, R, tr)
    arrs, specs = [], []
    for x in ins:
        if isinstance(x, tuple):
            arrs.append(x[0])
            specs.append(pl.BlockSpec((None, tr, C), functools.partial(lambda i, n: (n, i, 0), n=x[1])))
        else:
            arrs.append(x)
            specs.append(pl.BlockSpec((tr, C), lambda i: (i, 0)))
    ni = len(arrs)

    def body(*refs):
        vals = fn(*[r[...] for r in refs[:ni]])
        for o_ref, v in zip(refs[ni:], vals):
            o_ref[...] = v.astype(o_ref.dtype)

    no = len(out_dtypes)
    j = _job_args(job, ni, no)
    res = pl.pallas_call(
        _hosting(body, job, ni, no, 0, R // tr), name=name, grid=(R // tr,), in_specs=specs + j["in_specs"],
        out_specs=[pl.BlockSpec((tr, C), lambda i: (i, 0)) for _ in out_dtypes] + j["out_specs"],
        out_shape=[jax.ShapeDtypeStruct((R, C), dt) for dt in out_dtypes] + j["out_shape"],
        scratch_shapes=j["scratch"], input_output_aliases=j["aliases"],
        compiler_params=_params(("parallel",) if job is None else ("arbitrary",)),
    )(*arrs, *j["ins"])
    return list(res) if job is None else (list(res[:no]), list(res[no:]))


def _tile_rows(rows, cols):
    t = 256
    while t > 8 and (t * cols > 512 * 1024 or rows % t):
        t //= 2
    return t


def _rows_at(fn, pos, ins, outs, steps, *, name, aliases=None):
    read = [n for n, (_, s) in enumerate(ins) if s is not ANY]
    ni = len(ins)

    def body(pos_ref, *refs):
        vals = fn(*[refs[n][...] for n in read])
        for o_ref, v in zip(refs[ni:], vals):
            o_ref[...] = v.astype(o_ref.dtype)

    return pl.pallas_call(
        body, name=name,
        grid_spec=pltpu.PrefetchScalarGridSpec(num_scalar_prefetch=1, grid=(steps,), in_specs=[s for _, s in ins],
                                               out_specs=[s for _, s in outs]),
        out_shape=[sh for sh, _ in outs],
        input_output_aliases={1 + i: o for i, o in (aliases or {}).items()},
        compiler_params=_params(("parallel",)),
    )(pos, *[a for a, _ in ins])


def cast_into_whole(pos, w, l, axis, *, name):
    _, r, n = w.shape
    tr = _tile_rows(r, n)
    if axis == 1:
        shape, spec = (r, n * N_CHIPS), pl.BlockSpec((tr, n), lambda i, p: (i, p[3]))
    else:
        shape, spec = (r * N_CHIPS, n), pl.BlockSpec((tr, n), lambda i, p: (p[3] * (r // tr) + i, 0))
    return _rows_at(lambda a: (a,), pos, [(w, pl.BlockSpec((None, tr, n), lambda i, p: (l, i, 0)))],
                    [(jax.ShapeDtypeStruct(shape, BF16), spec)], r // tr, name=name)[0]


def pair_sum(pos, theirs, g32, axis, *, name):
    rows2, cols = theirs.shape
    h = rows2 // (N_CHIPS if axis == 0 else 1)
    tr = _tile_rows(h, cols)
    hb = h // tr
    if axis == 1:
        own = pl.BlockSpec((tr, cols), lambda i, p: (p[2] * hb + i, 0))
    else:
        own = pl.BlockSpec((tr, cols), lambda i, p: ((2 * (i // hb) + p[2]) * hb + i % hb, 0))
    row = pl.BlockSpec((tr, cols), lambda i, p: (i, 0))
    return _rows_at(lambda t, m: (m + t.astype(F32),) * 2, pos, [(theirs, row), (g32, own)],
                    [(jax.ShapeDtypeStruct((rows2, cols), F32), row), (jax.ShapeDtypeStruct((rows2, cols), BF16), row)],
                    rows2 // tr, name=name)


def chip_sum(pos, h32, recv, l, axis, whole, *, name):
    _, depth, h, n = recv.shape
    tr = _tile_rows(h, n)
    hb = h // tr
    if axis == 1:
        mine = pl.BlockSpec((tr, n), lambda i, p: (i, p[3]))
    else:
        mine = pl.BlockSpec((tr, n), lambda i, p: (p[3] * hb + i, 0))
    ins = [(h32, mine)] + [(recv, pl.BlockSpec((None, None, tr, n), functools.partial(lambda i, p, j: (j, l, i, 0), j=j)))
                           for j in range(3)]
    if whole is not None:
        ins.append((whole, ANY))
    return _rows_at(lambda o, a, b, c: (((o + a.astype(F32)) + b.astype(F32)) + c.astype(F32),), pos, ins,
                    [(jax.ShapeDtypeStruct((depth, 2, h, n), F32), pl.BlockSpec((None, None, tr, n), lambda i, p: (l, p[2], i, 0)))],
                    hb, name=name, aliases=None if whole is None else {4: 0})[0]


def _adamw(w, g, m, v):
    m2 = ADAM_B1 * m + (1.0 - ADAM_B1) * g
    v2 = ADAM_B2 * v + (1.0 - ADAM_B2) * (g * g)
    m_hat = m2 / (1.0 - ADAM_B1 ** ADAM_STEP)
    v_hat = v2 / (1.0 - ADAM_B2 ** ADAM_STEP)
    delta = -ADAM_LR * (m_hat / (jnp.sqrt(v_hat) + ADAM_EPS) + ADAM_WD * w)
    return delta, m2, v2


def _place():
    return lax.axis_index("x"), lax.axis_index("y"), lax.axis_index("c")


def _chip_peers(x, y, c):
    return [((1 - x, y, c), 2 * (1 - x) + y), ((x, 1 - y, c), 2 * x + 1 - y), ((1 - x, 1 - y, c), 2 * (1 - x) + 1 - y)]


def _shard_of(ref, axis, k, n):
    start = pl.multiple_of(k * n, 128)
    return ref.at[pl.ds(start, n), :] if axis == 0 else ref.at[:, pl.ds(start, n)]


ANY = pl.BlockSpec(memory_space=pl.ANY)


class CopyJob:
    def __init__(self, ins, out_shape, scratch, copies, aliases=None):
        self.ins, self.out_shape, self.scratch, self.copies = list(ins), list(out_shape), list(scratch), copies
        self.aliases = dict(aliases or {})

    def start(self, ins, outs, sems):
        local, remote, _, _ = self.copies(ins, outs, sems)
        for d in local + remote:
            d.start()

    def finish(self, ins, outs, sems):
        local, remote, arrivals, relays = self.copies(ins, outs, sems)
        for needs, sends, _ in relays:
            for d in needs:
                d.wait_recv()
            for d in sends:
                d.start()
        for d in arrivals + [d for _, _, arrives in relays for d in arrives]:
            d.wait_recv()
        for d in remote + [d for _, sends, _ in relays for d in sends]:
            d.wait_send()
        for d in local:
            d.wait()


def merge_jobs(a, b):
    ai, ao, asc = len(a.ins), len(a.out_shape), len(a.scratch)

    def copies(ins, outs, sems):
        ra = a.copies(ins[:ai], outs[:ao], sems[:asc])
        rb = b.copies(ins[ai:], outs[ao:], sems[asc:])
        return tuple(x + y for x, y in zip(ra, rb))

    aliases = dict(a.aliases)
    aliases.update({ai + i: ao + o for i, o in b.aliases.items()})
    return CopyJob(a.ins + b.ins, a.out_shape + b.out_shape, a.scratch + b.scratch, copies, aliases)


def run_job(job, *, name):
    ni, no = len(job.ins), len(job.out_shape)

    def body(*refs):
        parts = refs[:ni], refs[ni:ni + no], refs[ni + no:]
        job.start(*parts)
        job.finish(*parts)

    return pl.pallas_call(
        body, name=name, in_specs=[ANY] * ni, out_specs=[ANY] * no, out_shape=job.out_shape,
        scratch_shapes=job.scratch, input_output_aliases=job.aliases,
    )(*job.ins)


def _job_args(job, n_in, n_out):
    if job is None:
        return dict(ins=[], in_specs=[], out_specs=[], out_shape=[], scratch=[], aliases={})
    return dict(ins=job.ins, in_specs=[ANY] * len(job.ins), out_specs=[ANY] * len(job.out_shape),
                out_shape=job.out_shape, scratch=job.scratch,
                aliases={n_in + i: n_out + o for i, o in job.aliases.items()})


def _hosting(body, job, n_in, n_out, n_scratch, grid):
    if job is None:
        return body
    ji, jo = len(job.ins), len(job.out_shape)
    grid = (grid,) if isinstance(grid, int) else tuple(grid)

    def at(ends):
        hit = None
        for ax, e in enumerate(ends):
            here = pl.program_id(ax) == e
            hit = here if hit is None else jnp.logical_and(hit, here)
        return hit

    def hosted(*refs):
        o = n_in + ji
        s = o + n_out + jo
        parts = refs[n_in:o], refs[o + n_out:s], refs[s + n_scratch:]

        @pl.when(at([0] * len(grid)))
        def _():
            job.start(*parts)

        body(*refs[:n_in], *refs[o:o + n_out], *refs[s:s + n_scratch])

        @pl.when(at([g - 1 for g in grid]))
        def _():
            job.finish(*parts)

    return hosted


def _job_sems(n_remote, n_local):
    return [pltpu.SemaphoreType.DMA((n_remote,)), pltpu.SemaphoreType.DMA((n_remote,)), pltpu.SemaphoreType.DMA((n_local,))]


def gather_job(shards, axes, chips=(0, 1, 2)):
    na = len(shards)

    def copies(ins, outs, sems):
        send, recv, _ = sems
        x, y, c = _place()
        k = 2 * x + y
        remote, relays = [], []
        for a in range(na):
            r = outs[a].shape[0] // (N_CHIPS if axes[a] == 0 else 1)
            n = outs[a].shape[axes[a]] // N_CHIPS
            half = r // 2

            def part(kk, cc, a=a, n=n, half=half):
                rows = pl.ds(pl.multiple_of(cc * half + (kk * n if axes[a] == 0 else 0), 8), half)
                return outs[a].at[rows, :] if axes[a] == 0 else outs[a].at[rows, pl.ds(pl.multiple_of(kk * n, 128), n)]

            needs, passes, lands = [], [], []
            for j, (peer, kp) in enumerate(_chip_peers(x, y, c)):
                if j not in chips:
                    continue
                s = 6 * a + j
                remote.append(pltpu.make_async_remote_copy(part(k, c), part(k, c), send.at[s], recv.at[s],
                                                           device_id=peer, device_id_type=MESH))
                needs.append(pltpu.make_async_remote_copy(part(kp, c), part(kp, c), send.at[s], recv.at[s],
                                                          device_id=peer, device_id_type=MESH))
                passes.append(pltpu.make_async_remote_copy(part(kp, c), part(kp, c), send.at[s + 3], recv.at[s + 3],
                                                           device_id=(x, y, 1 - c), device_id_type=MESH))
                lands.append(pltpu.make_async_remote_copy(part(kp, 1 - c), part(kp, 1 - c), send.at[s + 3], recv.at[s + 3],
                                                          device_id=(x, y, 1 - c), device_id_type=MESH))
            relays.append((needs, passes, lands))
        return [], remote, [], relays

    out_shape = [jax.ShapeDtypeStruct(w.shape, BF16) for w in shards]
    return CopyJob(shards, out_shape, _job_sems(6 * na, 1), copies, {a: a for a in range(na)})


def scatter_job(layers, g16, axes, filled, chips=(0, 1, 2)):
    na = len(axes)

    def shard_shape(a):
        r, c = g16[a].shape
        return (r // N_CHIPS, c) if axes[a] == 0 else (r, c // N_CHIPS)

    def copies(ins, outs, sems):
        send, recv_sems, _ = sems
        x, y, c = _place()
        remote = []
        for a in range(na):
            n = shard_shape(a)[axes[a]]
            for r, (peer, kp) in enumerate(_chip_peers(x, y, c)):
                if r not in chips:
                    continue
                remote.append(pltpu.make_async_remote_copy(_shard_of(ins[a], axes[a], kp, n), outs[a].at[r, layers[a]],
                                                           send.at[3 * a + r], recv_sems.at[3 * a + r],
                                                           device_id=peer, device_id_type=MESH))
        return [], remote, remote, []

    out_shape = [jax.ShapeDtypeStruct((3, DEPTH) + shard_shape(a), BF16) for a in range(na)]
    ins = list(g16)
    aliases = {}
    for a in range(na):
        if filled[a] is not None:
            aliases[len(ins)] = a
            ins.append(filled[a])
    return CopyJob(ins, out_shape, _job_sems(3 * na, 1), copies, aliases)


def pair_job(g16, axes):
    na = len(axes)
    pieces = [1 if ax == 1 else N_CHIPS for ax in axes]

    def copies(ins, outs, sems):
        send, recv, _ = sems
        x, y, c = _place()
        remote = []
        s = 0
        for a in range(na):
            rows = g16[a].shape[0] // (2 * pieces[a])
            for kk in range(pieces[a]):
                src = ins[a].at[pl.ds(pl.multiple_of((2 * kk + 1 - c) * rows, 8), rows), :]
                remote.append(pltpu.make_async_remote_copy(src, outs[a].at[pl.ds(kk * rows, rows), :], send.at[s], recv.at[s],
                                                           device_id=(x, y, 1 - c), device_id_type=MESH))
                s += 1
        return [], remote, remote, []

    out_shape = [jax.ShapeDtypeStruct((g.shape[0] // 2, g.shape[1]), BF16) for g in g16]
    return CopyJob(g16, out_shape, _job_sems(sum(pieces), 1), copies)


def join_job(shards):
    na = len(shards)

    def copies(ins, outs, sems):
        send, recv, _ = sems
        x, y, c = _place()
        remote = [pltpu.make_async_remote_copy(outs[a].at[:, c], outs[a].at[:, c], send.at[a], recv.at[a],
                                               device_id=(x, y, 1 - c), device_id_type=MESH) for a in range(na)]
        lands = [pltpu.make_async_remote_copy(outs[a].at[:, 1 - c], outs[a].at[:, 1 - c], send.at[a], recv.at[a],
                                              device_id=(x, y, 1 - c), device_id_type=MESH) for a in range(na)]
        return [], remote, lands, []

    out_shape = [jax.ShapeDtypeStruct(s.shape, F32) for s in shards]
    return CopyJob(shards, out_shape, _job_sems(na, 1), copies, {a: a for a in range(na)})


def small_job(p):
    def copies(ins, outs, sems):
        send, recv, loc = sems
        x, y, c = _place()
        me = 4 * x + 2 * y + c
        remote, lands = [], []
        for rel in range(1, 8):
            dx, dy, dc = rel >> 2, (rel >> 1) & 1, rel & 1
            peer = (1 - x if dx else x, 1 - y if dy else y, 1 - c if dc else c)
            who = 4 * peer[0] + 2 * peer[1] + peer[2]
            remote.append(pltpu.make_async_remote_copy(ins[0], outs[0].at[me], send.at[rel - 1], recv.at[rel - 1],
                                                       device_id=peer, device_id_type=MESH))
            lands.append(pltpu.make_async_remote_copy(ins[0], outs[0].at[who], send.at[rel - 1], recv.at[rel - 1],
                                                      device_id=peer, device_id_type=MESH))
        return [pltpu.make_async_copy(ins[0], outs[0].at[me], loc.at[0])], remote, lands, []

    return CopyJob([p], [jax.ShapeDtypeStruct((8,) + p.shape, F32)], _job_sems(7, 1), copies)


def small_sum(slots):
    def add(*terms):
        acc = terms[0]
        for t in terms[1:]:
            acc = acc + t
        return (acc,)

    return _rows_call(add, [(slots, d) for d in range(8)], [F32], name="small_sum", tr=8 * 47)[0]


BIG = ("w_in", "p_ret", "p_sb", "p_sgu", "w_out", "w_up", "w_down")
BIG_AXIS = {"w_in": 1, "p_ret": 1, "p_sb": 1, "p_sgu": 1, "w_out": 0, "w_up": 1, "w_down": 0}
SMALL = ("ret_gn_g", "ret_gn_b", "sgu_ln_g", "sgu_ln_b", "sgu_w", "sgu_b", "ln1_g", "ln1_b", "ln2_g", "ln2_b")


def layer_forward(l, x0, x0h, W, sm, rope, rconsts, hooks):
    n = f"l{l}_"
    job = hooks.fwd_job(l, "proj")
    proj = matmul(x0h, W["w_in"], mode="nn", tm=4096, tn=768, tk=1024, name=n + "proj", job=job)
    if job is not None:
        proj, job_out = proj
        hooks.done(job, job_out)
    retg, raw, states = ret_fwd(proj, *rope, rconsts, sm["ret_gn_g"], sm["ret_gn_b"], name=n + "ret_fwd")
    job = hooks.fwd_job(l, "sb")
    sb, job_out = sb_fwd(proj, name=n + "sb_fwd", job=job)
    if job is not None:
        hooks.done(job, job_out)
    sg = sgu_fwd(proj, sm["sgu_ln_g"], sm["sgu_ln_b"], sm["sgu_w"], sm["sgu_bias"], name=n + "sgu_fwd")
    merged, r1, r2, r3 = merge_fwd(retg, sb, sg, W["p_ret"], W["p_sb"], W["p_sgu"], proj, name=n + "merge_fwd")
    x1, xh1, rs1, x1h = matmul_ln(merged, W["w_out"], x0, sm["ln1_g"], sm["ln1_b"], tk=1024, name=n + "out_ln1")
    job = hooks.fwd_job(l, "up")
    h1 = matmul(x1h, W["w_up"], mode="nn", tm=2048, tn=1024, tk=1024, outs=((BF16, None),), name=n + "up", job=job)
    if job is not None:
        h1, job_out = h1
        hooks.done(job, job_out)
    job = hooks.fwd_job(l, "down")
    res = matmul_ln(h1, W["w_down"], x1, sm["ln2_g"], sm["ln2_b"], pro=_relu2, tk=1024, name=n + "down_ln2", job=job)
    if job is not None:
        res, job_out = res
        hooks.done(job, job_out)
    x2, xh2, rs2, x2h = res
    saved = dict(x0h=x0h, proj=proj, retg=retg, raw=raw, states=states, sb=sb, sg=sg, merged=merged, r=(r1, r2, r3),
                 x1h=x1h, xh1=xh1, rs1=rs1, h1=h1, xh2=xh2, rs2=rs2)
    return x2, x2h, saved


def layer_backward(l, dx2, s, W, sm, rope, rconsts, hooks):
    n = f"l{l}_"
    two = ((F32, None), (BF16, None))
    gw, gs = {}, {}
    job = hooks.bwd_job(l, "ln2")
    res = ln_bwd(dx2, s["xh2"], s["rs2"], sm["ln2_g"], name=n + "ln2_bwd", job=job)
    if job is not None:
        res, job_out = res
        hooks.done(job, job_out)
    du2, du2h, gs["ln2_g"], gs["ln2_b"] = res
    job = hooks.bwd_job(l, "g_down")
    gw["w_down"] = matmul(s["h1"], du2h, mode="tn", tm=1024, tn=1024, tk=4096, pro=_relu2, outs=two, name=n + "g_down", job=job)
    if job is not None:
        gw["w_down"], job_out = gw["w_down"]
        hooks.done(job, job_out)
    dh1 = matmul(du2h, W["w_down"], mode="nt", tm=2048, tn=1024, tk=1024, outs=((BF16, None),),
                 epi=lambda acc, h: (acc * (2.0 * jnp.maximum(h.astype(F32), 0.0)),), tiles=(s["h1"],), name=n + "d_h1")
    job = hooks.bwd_job(l, "g_up")
    gw["w_up"] = matmul(s["x1h"], dh1, mode="tn", tm=1024, tn=1024, tk=4096, outs=two, name=n + "g_up", job=job)
    if job is not None:
        gw["w_up"], job_out = gw["w_up"]
        hooks.done(job, job_out)
    dx1 = matmul(dh1, W["w_up"], mode="nt", tm=1024, tn=1024, tk=4096,
                 epi=lambda acc, d: (acc + ALPHA * d,), tiles=(du2,), name=n + "d_x1")
    du1, du1h, gs["ln1_g"], gs["ln1_b"] = ln_bwd(dx1, s["xh1"], s["rs1"], sm["ln1_g"], name=n + "ln1_bwd")
    gw["w_out"] = matmul(s["merged"], du1h, mode="tn", tm=1024, tn=1024, tk=4096, outs=two, name=n + "g_out")
    gate0 = C_GATE // 512
    dr1, dr2, dr3, dg1, dg2, dg3 = matmul(
        du1h, W["w_out"], mode="nt", tm=1024, tn=512, tk=1024, outs=((BF16, None),) * 6, epi=_merge_bwd_epi,
        tiles=(*s["r"], (s["proj"], gate0), (s["proj"], gate0 + 2), (s["proj"], gate0 + 4)), name=n + "d_merged")
    d_branch = {}
    for nm, a, dr in (("p_ret", s["retg"], dr1), ("p_sb", s["sb"], dr2), ("p_sgu", s["sg"], dr3)):
        gw[nm] = matmul(a, dr, mode="tn", tm=512, tn=1024, tk=2048, outs=two, name=n + "g_" + nm)
        d_branch[nm] = matmul(dr, W[nm], mode="nt", tm=1024, tn=512, tk=1024, name=n + "d_" + nm)
    job = hooks.pair(l, gw)
    dret, gs["ret_gn_g"], gs["ret_gn_b"], job_out = ret_bwd(s["proj"], *rope, rconsts, sm["ret_gn_g"], sm["ret_gn_b"],
                                                             s["raw"], s["states"], d_branch["p_ret"], name=n + "ret_bwd", job=job)
    if job is not None:
        hooks.done(job, job_out)
    job = hooks.scatter(l) if job is not None else None
    dsq, dsk, dsv, job_out = sb_bwd(s["proj"], s["sb"], d_branch["p_sb"], name=n + "sb_bwd", job=job)
    if job is not None:
        hooks.done(job, job_out)
    dsgu, gs["sgu_w"], dbias, gs["sgu_ln_g"], gs["sgu_ln_b"] = sgu_bwd(
        s["proj"], sm["sgu_ln_g"], sm["sgu_ln_b"], sm["sgu_w"], sm["sgu_bias"], d_branch["p_sgu"], name=n + "sgu_bwd")
    gs["sgu_b"] = dbias[:, :, 0]
    dproj = jnp.concatenate([dret, dsq, dsk, dsv, dsgu, dg1, dg2, dg3], axis=1)
    job = hooks.small(l, gs)
    gw["w_in"] = matmul(s["x0h"], dproj, mode="tn", tm=1024, tn=1536, tk=2048, outs=two, name=n + "g_in", job=job)
    if job is not None:
        gw["w_in"], job_out = gw["w_in"]
        hooks.done(job, job_out)
    job = hooks.tail(l, gw["w_in"])
    dx0 = matmul(dproj, W["w_in"], mode="nt", tm=1024, tn=1024, tk=2560,
                 epi=lambda acc, d: (acc + ALPHA * d,), tiles=(du1,), name=n + "d_x0", job=job)
    if job is not None:
        dx0, job_out = dx0
        hooks.done(job, job_out)
    return dx0, gw, gs


def local_step(x, target, small, plan):
    T = x.shape[0]
    rope = _rope_tables(T)
    rconsts = _ret_consts()
    sms = []
    for l in range(DEPTH):
        sm = {k: small[k][l][None, :] for k in SMALL if k not in ("sgu_w", "sgu_b")}
        sm["sgu_w"] = small["sgu_w"][l]
        sm["sgu_bias"] = jnp.broadcast_to(small["sgu_b"][l][:, :, None], (4, CHUNK, CHUNK))
        sms.append(sm)
    h, saved = x, []
    job = plan.first_job()
    hh = _rows_call(lambda a: (a,), [x], [BF16], name="cast_x", job=job)
    if job is not None:
        hh, job_out = hh
        plan.done(job, job_out)
    hh = hh[0]
    for l in range(DEPTH):
        h, hh, s = layer_forward(l, h, hh, plan.weights(l), sms[l], rope, rconsts, plan)
        saved.append(s)
    dy, sq = loss_head(h, target)
    gs = {k: [None] * DEPTH for k in SMALL}
    for l in reversed(range(DEPTH)):
        dy, gwl, gsl = layer_backward(l, dy, saved[l], plan.weights(l), sms[l], rope, rconsts, plan)
        plan.grads(l, gwl)
        for k in SMALL:
            gs[k][l] = gsl[k].reshape(small[k].shape[1:])
    return sq[0, 0], dy, {k: jnp.stack(v) for k, v in gs.items()}


EARLY_GRADS = ("p_ret", "p_sb", "p_sgu", "w_out", "w_up", "w_down")


class _StepPlan:
    def __init__(self, pos, shards16):
        self.pos = pos
        self.shards16 = shards16
        self.full = [dict() for _ in range(DEPTH)]
        self.gw = [None] * DEPTH
        self.bufs = {}
        self.sums = {}
        self.gs = [None] * DEPTH

    def first_job(self):
        return self._gather([(0, "w_in")])

    def weights(self, l):
        return self.full[l]

    def grads(self, l, gw):
        self.gw[l] = gw

    def _gather(self, items, chips=(0, 1, 2)):
        job = gather_job([self.shards16[l][k] for l, k in items], [BIG_AXIS[k] for _, k in items], chips)
        job.note = ("gather" if 2 in chips else "gather_part", items)
        return job

    def _pair(self, items):
        job = pair_job([g[1] for _, _, g in items], [BIG_AXIS[k] for _, k, _ in items])
        job.note = ("pair", items)
        return job

    def fwd_job(self, l, host):
        if host == "proj":
            return None
        if host == "sb":
            return self._gather([(l, k) for k in BIG[1:]])
        if l + 1 == DEPTH:
            return None
        return self._gather([(l + 1, "w_in")], (0, 1) if host == "up" else (2,))

    def bwd_job(self, l, host):
        if l + 1 == DEPTH:
            return None
        if host == "ln2":
            job = self._pair([(l + 1, "w_in", self.gw[l + 1]["w_in"])])
            job.note = ("pair_w_in", job.note[1])
            return job
        items, sums16 = self.summed_w_in
        job = scatter_job([l_ for l_, _, _ in items], sums16, [BIG_AXIS[k] for _, k, _ in items],
                          [self.bufs.get(k) for _, k, _ in items], (0, 1) if host == "g_down" else (2,))
        job.note = ("scatter", items)
        return job

    def pair(self, l, ready):
        return self._pair([(l, k, ready[k]) for k in EARLY_GRADS])

    def scatter(self, l):
        items, sums16 = self.summed
        job = scatter_job([l_ for l_, _, _ in items], sums16, [BIG_AXIS[k] for _, k, _ in items],
                          [self.bufs.get(k) for _, k, _ in items])
        job.note = ("scatter", items)
        return job

    def small(self, l, gs):
        self.gs[l] = {k: gs[k].reshape(-1) for k in SMALL}
        if l != 0:
            return None
        job = small_job(_pack_small({k: jnp.stack([self.gs[l_][k] for l_ in range(DEPTH)]) for k in SMALL}))
        shards = []
        for k in EARLY_GRADS:
            whole = None
            for l_ in range(DEPTH):
                whole = chip_sum(self.pos, self.sums[(l_, k)], self.bufs[k], l_, BIG_AXIS[k], whole, name=f"chip_sum_{k}_{l_}")
            shards.append(whole)
        job = merge_jobs(job, join_job(shards))
        job.note = ("small", [])
        return job

    def tail(self, l, g):
        if l != 0:
            return None
        last = self._pair([(0, "w_in", g)])
        self.done(last, run_job(last, name="pair_last"))
        return self.scatter(0)

    def done(self, job, outs):
        kind, items = job.note
        if kind == "small":
            self.small_slots = outs[0]
            self.joined = dict(zip(EARLY_GRADS, outs[1:]))
        if kind in ("pair", "pair_w_in"):
            sums16 = []
            for a, (l, k, g) in enumerate(items):
                self.sums[(l, k)], s16 = pair_sum(self.pos, outs[a], g[0], BIG_AXIS[k], name=f"pair_sum_{k}_{l}")
                sums16.append(s16)
            if kind == "pair":
                self.summed = (items, sums16)
            else:
                self.summed_w_in = (items, sums16)
        for a, item in enumerate(items):
            if kind == "gather_part":
                self.shards16[item[0]][item[1]] = outs[a]
            elif kind == "gather":
                self.full[item[0]][item[1]] = outs[a]
            elif kind == "scatter":
                self.bufs[item[1]] = outs[a]

    def finish(self):
        return self.bufs, self.sums


def _flat2(a):
    return a.reshape(-1, a.shape[-1])


def _pack_small(d, pre=""):
    return jnp.concatenate([d[pre + k].reshape(-1) for k in SMALL]).reshape(-1, 128)


def kernel(x, w_in, ret_gn_g, ret_gn_b, sgu_ln_g, sgu_ln_b, sgu_w, sgu_b, p_ret, p_sb, p_sgu, w_out, ln1_g, ln1_b, w_up, w_down, ln2_g, ln2_b, loss_target, m_w_in, m_ret_gn_g, m_ret_gn_b, m_sgu_ln_g, m_sgu_ln_b, m_sgu_w, m_sgu_b, m_p_ret, m_p_sb, m_p_sgu, m_w_out, m_ln1_g, m_ln1_b, m_w_up, m_w_down, m_ln2_g, m_ln2_b, v_w_in, v_ret_gn_g, v_ret_gn_b, v_sgu_ln_g, v_sgu_ln_b, v_sgu_w, v_sgu_b, v_p_ret, v_p_sb, v_p_sgu, v_w_out, v_ln1_g, v_ln1_b, v_w_up, v_w_down, v_ln2_g, v_ln2_b):
    given = dict(locals())
    order = BIG[:1] + SMALL[:6] + BIG[1:5] + SMALL[6:8] + BIG[5:7] + SMALL[8:10]
    L = DEPTH

    px, py, pc = _place()
    pos = jnp.stack([px, py, pc, 2 * px + py]).astype(jnp.int32)

    shards16 = [{k: cast_into_whole(pos, given[k], l, BIG_AXIS[k], name=f"cast_{k}_{l}") for k in BIG} for l in range(L)]
    plan = _StepPlan(pos, shards16)
    sq, dx, _ = local_step(x[0], loss_target[0], {k: given[k] for k in SMALL}, plan)
    loss = 0.5 * lax.psum(sq, ("x", "y", "c"))

    bufs, sums = plan.finish()
    whole = None
    for l in range(L):
        whole = chip_sum(pos, sums[(l, "w_in")], bufs["w_in"], l, BIG_AXIS["w_in"], whole, name=f"chip_sum_w_in_{l}")
    joined = dict(plan.joined, w_in=run_job(join_job([whole]), name="join_halves")[0])
    out = {}
    for k in BIG:
        shp = given[k].shape
        res = _rows_call(lambda g_, w_, m_, v_: (g_,) + _adamw(w_, g_, m_, v_),
                         [joined[k].reshape(-1, shp[-1]), _flat2(given[k]), _flat2(given["m_" + k]), _flat2(given["v_" + k])],
                         [F32] * 4, name="adamw_" + k)
        out[k] = [r.reshape(shp) for r in res]

    pack = _pack_small
    res = _rows_call(lambda g_, w_, m_, v_: (g_,) + _adamw(w_, g_, m_, v_),
                     [small_sum(plan.small_slots), pack(given), pack(given, "m_"), pack(given, "v_")], [F32] * 4,
                     name="adamw_small", tr=8 * 47)
    off = 0
    for k in SMALL:
        sz = given[k].size
        out[k] = [r.reshape(-1)[off:off + sz].reshape(given[k].shape) for r in res]
        off += sz

    grads = [out[k][0] for k in order]
    deltas = [out[k][1] for k in order]
    new_m = [out[k][2] for k in order]
    new_v = [out[k][3] for k in order]
    return (loss, dx[None], *grads, *deltas, *new_m, *new_v)
```

```python
import functools
import math

import jax
import jax.numpy as jnp
from jax import lax
from jax.experimental import pallas as pl
from jax.experimental.pallas import tpu as pltpu

F32 = jnp.float32
BF16 = jnp.bfloat16

D_MODEL = 1024
SEQ = 4096
DEPTH = 2
CHUNK = 128
RET_HEADS = 4
BRANCH_W = 512
N_IN = 7680
D_FF = 4096
LN_EPS = 1e-5
ROPE_BASE = 10000.0
ALPHA = (2 * DEPTH) ** 0.25
RET_SCALE = 128 ** -0.5
SB_SCALE = 64 ** -0.5
C_RET, C_SB, C_SGU, C_GATE = 0, 2048, 3584, 4608

ADAM_LR, ADAM_B1, ADAM_B2, ADAM_EPS, ADAM_WD, ADAM_STEP = 0.001, 0.9, 0.999, 1e-08, 0.01, 10

N_CHIPS = 4
VMEM_LIMIT = 56 * 1024 * 1024
MESH = pl.DeviceIdType.MESH

NN = ((1,), (0,))
NT = ((1,), (1,))
TN = ((0,), (0,))


def _dot(a, b, dims):
    return lax.dot_general(a, b, (dims, ((), ())), preferred_element_type=F32)


def _params(sem):
    return pltpu.CompilerParams(dimension_semantics=sem, vmem_limit_bytes=VMEM_LIMIT)


def _relu2(h):
    r = jnp.maximum(h.astype(F32), 0.0)
    return r * r


def matmul(a, b, *, mode, tm, tn, tk, outs=((F32, None),), pro=None, epi=None, tiles=(), rows=(), name, job=None):
    if mode == "nn":
        (M, K), N = a.shape, b.shape[1]
    elif mode == "nt":
        (M, K), N = a.shape, b.shape[0]
    else:
        (K, M), N = a.shape, b.shape[1]
    tm, tn, tk = min(tm, M), min(tn, N), min(tk, K)
    assert M % tm == 0 and N % tn == 0 and K % tk == 0, (name, M, N, K, tm, tn, tk)
    if mode == "nn":
        a_spec = pl.BlockSpec((tm, tk), lambda i, j, k: (i, k))
        b_spec = pl.BlockSpec((tk, tn), lambda i, j, k: (k, j))
        dims = NN
    elif mode == "nt":
        a_spec = pl.BlockSpec((tm, tk), lambda i, j, k: (i, k))
        b_spec = pl.BlockSpec((tn, tk), lambda i, j, k: (j, k))
        dims = NT
    else:
        a_spec = pl.BlockSpec((tk, tm), lambda i, j, k: (k, i))
        b_spec = pl.BlockSpec((tk, tn), lambda i, j, k: (k, j))
        dims = TN
    nk = K // tk
    nt_, nr, no = len(tiles), len(rows), len(outs)

    def body(a_ref, b_ref, *rest):
        tile_refs = rest[:nt_]
        row_refs = rest[nt_:nt_ + nr]
        out_refs = rest[nt_ + nr:nt_ + nr + no]
        av = a_ref[...]
        if pro is not None:
            av = pro(av)
        p = _dot(av.astype(BF16), b_ref[...].astype(BF16), dims)

        def finish(acc):
            vals = (acc,) * no if epi is None else epi(acc, *[r[...] for r in tile_refs], *[r[...] for r in row_refs])
            for o_ref, v in zip(out_refs, vals):
                o_ref[...] = v.astype(o_ref.dtype)

        if nk == 1:
            finish(p)
        else:
            acc_ref = rest[-1]
            k = pl.program_id(2)

            @pl.when(k == 0)
            def _():
                acc_ref[...] = p

            @pl.when(k > 0)
            def _():
                acc_ref[...] += p

            @pl.when(k == nk - 1)
            def _():
                finish(acc_ref[...])

    out_shape, out_specs = [], []
    for dt, width in outs:
        if width is None:
            out_shape.append(jax.ShapeDtypeStruct((M, N), dt))
            out_specs.append(pl.BlockSpec((tm, tn), lambda i, j, k: (i, j)))
        else:
            assert N == tn
            out_shape.append(jax.ShapeDtypeStruct((M, width), dt))
            out_specs.append(pl.BlockSpec((tm, width), lambda i, j, k: (i, 0)))
    in_specs = [a_spec, b_spec]
    offs = [t[1] if isinstance(t, tuple) else 0 for t in tiles]
    tiles = [t[0] if isinstance(t, tuple) else t for t in tiles]
    in_specs += [pl.BlockSpec((tm, tn), functools.partial(lambda i, j, k, o: (i, j + o), o=o)) for o in offs]
    in_specs += [pl.BlockSpec((1, tn), lambda i, j, k: (0, j)) for _ in rows]
    grid = (M // tm, N // tn, nk)
    scratch = [pltpu.VMEM((tm, tn), F32)] if nk > 1 else []
    j = _job_args(job, len(in_specs), no)
    res = pl.pallas_call(
        _hosting(body, job, len(in_specs), no, len(scratch), grid), name=name, grid=grid,
        in_specs=in_specs + j["in_specs"], out_specs=out_specs + j["out_specs"], out_shape=out_shape + j["out_shape"],
        scratch_shapes=scratch + j["scratch"], input_output_aliases=j["aliases"],
        compiler_params=_params(("parallel", "parallel", "arbitrary") if job is None else ("arbitrary",) * 3),
    )(a, b, *tiles, *rows, *j["ins"])
    mine = res[0] if no == 1 else list(res[:no])
    return mine if job is None else (mine, list(res[no:]))


def _ln_epi(acc, res, g, b):
    u = ALPHA * res + acc
    mu = jnp.mean(u, axis=-1, keepdims=True)
    xc = u - mu
    var = jnp.mean(xc * xc, axis=-1, keepdims=True)
    rstd = lax.rsqrt(var + LN_EPS)
    xhat = xc * rstd
    y = xhat * g + b
    return y, xhat, jnp.broadcast_to(rstd, (u.shape[0], 128)), y


def matmul_ln(a, w, res, g, b, *, pro=None, tk, name, job=None):
    n = w.shape[1]
    return matmul(a, w, mode="nn", tm=1024, tn=n, tk=tk, pro=pro, epi=_ln_epi, tiles=(res,), rows=(g, b),
                  outs=((F32, None), (F32, None), (F32, 128), (BF16, None)), name=name, job=job)


def ln_bwd(dy, xhat, rstd, g, *, name, job=None):
    T, D = dy.shape
    tm = min(512, T)

    def body(dy_ref, xh_ref, rs_ref, g_ref, du_ref, du16_ref, dg_ref, db_ref):
        dyv, xh = dy_ref[...], xh_ref[...]
        r = rs_ref[:, 0:1]
        dxh = dyv * g_ref[...]
        m1 = jnp.mean(dxh, axis=-1, keepdims=True)
        m2 = jnp.mean(dxh * xh, axis=-1, keepdims=True)
        du = r * (dxh - m1 - xh * m2)
        du_ref[...] = du
        du16_ref[...] = du.astype(BF16)

        @pl.when(pl.program_id(0) == 0)
        def _():
            dg_ref[...] = jnp.zeros_like(dg_ref)
            db_ref[...] = jnp.zeros_like(db_ref)

        dg_ref[...] += jnp.sum(dyv * xh, axis=0, keepdims=True)
        db_ref[...] += jnp.sum(dyv, axis=0, keepdims=True)

    row = pl.BlockSpec((tm, D), lambda i: (i, 0))
    vec = pl.BlockSpec((1, D), lambda i: (0, 0))
    j = _job_args(job, 4, 4)
    res = pl.pallas_call(
        _hosting(body, job, 4, 4, 0, T // tm), name=name, grid=(T // tm,),
        in_specs=[row, row, pl.BlockSpec((tm, 128), lambda i: (i, 0)), vec] + j["in_specs"],
        out_specs=[row, row, vec, vec] + j["out_specs"],
        out_shape=[jax.ShapeDtypeStruct((T, D), F32), jax.ShapeDtypeStruct((T, D), BF16),
                   jax.ShapeDtypeStruct((1, D), F32), jax.ShapeDtypeStruct((1, D), F32)] + j["out_shape"],
        scratch_shapes=j["scratch"], input_output_aliases=j["aliases"],
        compiler_params=_params(("arbitrary",)),
    )(dy, xhat, rstd, g, *j["ins"])
    return list(res[:4]) if job is None else (list(res[:4]), list(res[4:]))


def loss_head(y, target):
    T, D = y.shape
    tm = min(512, T)

    def body(y_ref, t_ref, dy_ref, s_ref):
        e = y_ref[...] - t_ref[...]
        dy_ref[...] = e * (1.0 / D)

        @pl.when(pl.program_id(0) == 0)
        def _():
            s_ref[...] = jnp.zeros_like(s_ref)

        s_ref[...] += jnp.sum(jnp.mean(e * e, axis=-1, keepdims=True))

    row = pl.BlockSpec((tm, D), lambda i: (i, 0))
    return pl.pallas_call(
        body, name="loss_head", grid=(T // tm,),
        in_specs=[row, row], out_specs=[row, pl.BlockSpec((8, 128), lambda i: (0, 0))],
        out_shape=[jax.ShapeDtypeStruct((T, D), F32), jax.ShapeDtypeStruct((8, 128), F32)],
        compiler_params=_params(("arbitrary",)),
    )(y, target)


def _rope_tables(T):
    half = 64
    inv_freq = ROPE_BASE ** (-jnp.arange(half, dtype=F32) / half)
    ang = jnp.arange(T, dtype=jnp.int32).astype(F32)[:, None] * inv_freq[None, :]
    cos, sin = jnp.cos(ang), jnp.sin(ang)
    return jnp.concatenate([cos, cos], axis=1), jnp.concatenate([-sin, sin], axis=1)


def _ret_consts():
    H = RET_HEADS
    log_g = jnp.log(1.0 - 2.0 ** (-5.0 - jnp.arange(H, dtype=F32)))
    idx = jnp.arange(CHUNK, dtype=F32)
    diff = idx[:, None] - idx[None, :]
    dmat = jnp.where(diff[None] >= 0, jnp.exp(log_g[:, None, None] * diff[None]), 0.0)
    kd = jnp.exp(log_g[:, None] * (CHUNK - 1 - idx)[None, :])
    qd = jnp.exp(log_g[:, None] * (idx + 1.0)[None, :])
    cd = jnp.exp(log_g * CHUNK)
    full = (H, CHUNK, CHUNK)
    return (dmat.astype(F32), jnp.broadcast_to(kd[:, :, None], full), jnp.broadcast_to(qd[:, :, None], full),
            jnp.broadcast_to(cd[:, None, None], full))


def _swap_halves(v):
    return pltpu.roll(v, 64, 1)


def _group_norm(o):
    mu = jnp.mean(o, axis=-1, keepdims=True)
    xc = o - mu
    var = jnp.mean(xc * xc, axis=-1, keepdims=True)
    rstd = lax.rsqrt(var + LN_EPS)
    return xc * rstd, rstd


def ret_fwd(proj, cosf, sinf, consts, gn_g, gn_b, *, name):
    T = proj.shape[0]
    tb = min(512, T)
    nch = tb // CHUNK
    H = RET_HEADS

    def body(p_ref, cos_ref, sin_ref, dm_ref, kd_ref, qd_ref, cd_ref, g_ref, b_ref, out_ref, raw_ref, st_ref, s_ref):
        @pl.when(pl.program_id(0) == 0)
        def _():
            s_ref[...] = jnp.zeros_like(s_ref)

        for c in range(nch):
            r = slice(c * CHUNK, (c + 1) * CHUNK)
            cs, sn = cos_ref[r, :], sin_ref[r, :]
            for h in range(H):
                hc = slice(h * 128, (h + 1) * 128)
                q = p_ref[r, h * 128:(h + 1) * 128]
                k = p_ref[r, 512 + h * 128:512 + (h + 1) * 128]
                v = p_ref[r, 1024 + h * 128:1024 + (h + 1) * 128]
                gt = p_ref[r, 1536 + h * 128:1536 + (h + 1) * 128]
                qr = q * cs + _swap_halves(q) * sn
                kr = (k * cs + _swap_halves(k) * sn) * RET_SCALE
                sprev = s_ref[h]
                st_ref[c, h] = sprev
                qb, kb, vb = qr.astype(BF16), kr.astype(BF16), v.astype(BF16)
                s = _dot(qb, kb, NT) * dm_ref[h]
                o = _dot(s.astype(BF16), vb, NN) + _dot((qr * qd_ref[h]).astype(BF16), sprev.astype(BF16), NN)
                s_ref[h] = sprev * cd_ref[h] + _dot((kr * kd_ref[h]).astype(BF16), vb, TN)
                raw_ref[r, hc] = o
                y, _ = _group_norm(o)
                out_ref[r, hc] = (gt * jax.nn.sigmoid(gt)) * (y * g_ref[:, hc] + b_ref[:, hc])

    cmat = pl.BlockSpec((H, CHUNK, CHUNK), lambda i: (0, 0, 0))
    vec = pl.BlockSpec((1, BRANCH_W), lambda i: (0, 0))
    rope = pl.BlockSpec((tb, 128), lambda i: (i, 0))
    blk = pl.BlockSpec((tb, BRANCH_W), lambda i: (i, 0))
    return pl.pallas_call(
        body, name=name, grid=(T // tb,),
        in_specs=[pl.BlockSpec((tb, 2048), lambda i: (i, 0)), rope, rope, cmat, cmat, cmat, cmat, vec, vec],
        out_specs=[blk, blk, pl.BlockSpec((nch, H, CHUNK, CHUNK), lambda i: (i, 0, 0, 0))],
        out_shape=[jax.ShapeDtypeStruct((T, BRANCH_W), F32), jax.ShapeDtypeStruct((T, BRANCH_W), F32),
                   jax.ShapeDtypeStruct((T // CHUNK, H, CHUNK, CHUNK), F32)],
        scratch_shapes=[pltpu.VMEM((H, CHUNK, CHUNK), F32)],
        compiler_params=_params(("arbitrary",)),
    )(proj, cosf, sinf, *consts, gn_g, gn_b)


def ret_bwd(proj, cosf, sinf, consts, gn_g, gn_b, raw, states, dout, *, name, job=None):
    T = proj.shape[0]
    tb = min(512, T)
    nch = tb // CHUNK
    nb = T // tb
    H = RET_HEADS

    def body(p_ref, cos_ref, sin_ref, dm_ref, kd_ref, qd_ref, cd_ref, g_ref, b_ref, raw_ref, st_ref, do_ref,
             dp_ref, dg_ref, db_ref, ds_ref):
        @pl.when(pl.program_id(0) == 0)
        def _():
            ds_ref[...] = jnp.zeros_like(ds_ref)
            dg_ref[...] = jnp.zeros_like(dg_ref)
            db_ref[...] = jnp.zeros_like(db_ref)

        for c in reversed(range(nch)):
            r = slice(c * CHUNK, (c + 1) * CHUNK)
            cs, sn = cos_ref[r, :], sin_ref[r, :]
            for h in range(H):
                hc = slice(h * 128, (h + 1) * 128)
                q = p_ref[r, h * 128:(h + 1) * 128]
                k = p_ref[r, 512 + h * 128:512 + (h + 1) * 128]
                v = p_ref[r, 1024 + h * 128:1024 + (h + 1) * 128]
                gt = p_ref[r, 1536 + h * 128:1536 + (h + 1) * 128]
                qr = q * cs + _swap_halves(q) * sn
                kr = (k * cs + _swap_halves(k) * sn) * RET_SCALE
                sprev = st_ref[c, h]
                gv = g_ref[:, hc]
                y, rstd = _group_norm(raw_ref[r, hc])
                d_out = do_ref[r, hc]
                sg = jax.nn.sigmoid(gt)
                d_gate = d_out * (y * gv + b_ref[:, hc]) * (sg * (1.0 + gt * (1.0 - sg)))
                d_aff = d_out * (gt * sg)
                dg_ref[:, hc] += jnp.sum(d_aff * y, axis=0, keepdims=True)
                db_ref[:, hc] += jnp.sum(d_aff, axis=0, keepdims=True)
                dxh = d_aff * gv
                m1 = jnp.mean(dxh, axis=-1, keepdims=True)
                m2 = jnp.mean(dxh * y, axis=-1, keepdims=True)
                d_o = (rstd * (dxh - m1 - y * m2)).astype(BF16)
                qb, kb, vb = qr.astype(BF16), kr.astype(BF16), v.astype(BF16)
                dm, kd, qd = dm_ref[h], kd_ref[h], qd_ref[h]
                p = (_dot(qb, kb, NT) * dm).astype(BF16)
                dp = (_dot(d_o, vb, NT) * dm).astype(BF16)
                dsn = ds_ref[h]
                dsb = dsn.astype(BF16)
                dq_r = _dot(dp, kb, NN) + _dot(d_o, sprev.astype(BF16), NT) * qd
                dk_r = (_dot(dp, qb, TN) + _dot(vb, dsb, NT) * kd) * RET_SCALE
                d_v = _dot(p, d_o, TN) + _dot((kr * kd).astype(BF16), dsb, NN)
                ds_ref[h] = dsn * cd_ref[h] + _dot((qr * qd).astype(BF16), d_o, TN)
                dp_ref[r, h * 128:(h + 1) * 128] = (dq_r * cs - _swap_halves(dq_r) * sn).astype(BF16)
                dp_ref[r, 512 + h * 128:512 + (h + 1) * 128] = (dk_r * cs - _swap_halves(dk_r) * sn).astype(BF16)
                dp_ref[r, 1024 + h * 128:1024 + (h + 1) * 128] = d_v.astype(BF16)
                dp_ref[r, 1536 + h * 128:1536 + (h + 1) * 128] = d_gate.astype(BF16)

    cmat = pl.BlockSpec((H, CHUNK, CHUNK), lambda i: (0, 0, 0))
    vec = pl.BlockSpec((1, BRANCH_W), lambda i: (0, 0))
    rope = pl.BlockSpec((tb, 128), lambda i: (nb - 1 - i, 0))
    blk = pl.BlockSpec((tb, BRANCH_W), lambda i: (nb - 1 - i, 0))
    wide = pl.BlockSpec((tb, 2048), lambda i: (nb - 1 - i, 0))
    j = _job_args(job, 12, 3)
    res = pl.pallas_call(
        _hosting(body, job, 12, 3, 1, nb), name=name, grid=(nb,),
        in_specs=[wide, rope, rope, cmat, cmat, cmat, cmat, vec, vec, blk,
                  pl.BlockSpec((nch, H, CHUNK, CHUNK), lambda i: (nb - 1 - i, 0, 0, 0)), blk] + j["in_specs"],
        out_specs=[wide, vec, vec] + j["out_specs"],
        out_shape=[jax.ShapeDtypeStruct((T, 2048), BF16), jax.ShapeDtypeStruct((1, BRANCH_W), F32),
                   jax.ShapeDtypeStruct((1, BRANCH_W), F32)] + j["out_shape"],
        scratch_shapes=[pltpu.VMEM((H, CHUNK, CHUNK), F32)] + j["scratch"], input_output_aliases=j["aliases"],
        compiler_params=_params(("arbitrary",)),
    )(proj, cosf, sinf, *consts, gn_g, gn_b, raw, states, dout, *j["ins"])
    return res[0], res[1], res[2], list(res[3:])


def _sb_masks():
    row = lax.broadcasted_iota(jnp.int32, (CHUNK, CHUNK), 0)
    lane = lax.broadcasted_iota(jnp.int32, (CHUNK, CHUNK), 1)
    return row, lane


SB_QT = 256
SB_DEAD = -105.0


def _pair(v):
    hi = v.astype(BF16)
    return jnp.concatenate([hi, (v - hi.astype(F32)).astype(BF16)], axis=1)


def _sb_consts():
    r = lax.broadcasted_iota(jnp.int32, (256, 256), 0) & 127
    c = lax.broadcasted_iota(jnp.int32, (256, 256), 1)
    ones = c >= 128
    lane = lax.broadcasted_iota(jnp.int32, (CHUNK, CHUNK), 1)
    return (ones | (r > c)).astype(BF16), (ones | (r >= c)).astype(BF16), (lane < 64, lane >= 64)


def _per_head(x, hms):
    return jnp.concatenate([jnp.where(hm, x, 0.0) for hm in hms], axis=0).astype(BF16)


def _sb_logits(qb, kb2, mask2):
    z = _dot(qb, kb2, NT)
    l1p = jnp.log(1.0 + jnp.exp(-jnp.abs(z)))
    lsp = jnp.minimum(z, 0.0) - l1p
    lsn = lsp - z
    if mask2 is not None:
        lsn = jnp.where(mask2, lsn, 0.0)
    return lsp, lsn


def _sb_tile_mask(qt):
    trow = lax.broadcasted_iota(jnp.int32, (qt, 256), 0)
    tlane = lax.broadcasted_iota(jnp.int32, (qt, 256), 1) & 127
    return lambda m: (tlane + m * CHUNK) < trow


def sb_fwd(proj, *, name, job=None):
    T = proj.shape[0]
    qt = min(SB_QT, T)
    nsub = qt // CHUNK
    cb = C_SB // 128

    def body(q_ref, k_ref, v_ref, o_ref):
        u_gt, _, hms = _sb_consts()
        tile_mask = _sb_tile_mask(qt)

        def qtile(i, _):
            rq = pl.ds(pl.multiple_of(i * qt, qt), qt)
            qb = (q_ref[rq, :] * SB_SCALE).astype(BF16)

            def group(js, masks, state):
                carry, acc = list(state[:2]), state[2]
                rows = [pl.ds(pl.multiple_of(j * CHUNK, CHUNK), CHUNK) for j in js]
                logits = [_sb_logits(qb, _per_head(k_ref[rk, :], hms), m) for rk, m in zip(rows, masks)]
                sums = [[_dot(_pair(lsn[:, h * 128:(h + 1) * 128]), u_gt, NN) for h in range(2)] for _, lsn in logits]
                weights = []
                for (lsp, _), r, m in zip(logits, sums, masks):
                    a_b = []
                    for h in range(2):
                        hc = slice(h * 128, (h + 1) * 128)
                        a = jnp.exp(lsp[:, hc] + r[h][:, :128] + carry[h])
                        if m is not None:
                            a = jnp.where(m[:, hc], a, 0.0)
                        carry[h] = carry[h] + r[h][:, 128:]
                        a_b.append(a.astype(BF16))
                    weights.append(jnp.concatenate(a_b, axis=1))
                for rk, a in zip(rows, weights):
                    acc = acc + _dot(a, _per_head(v_ref[rk, :], hms), NN)
                return carry[0], carry[1], acc

            zero = jnp.zeros((qt, 128), F32)
            diag = list(reversed(range(nsub)))
            state = group([i * nsub + m for m in diag], [tile_mask(m) for m in diag], (zero, zero, zero))

            def live(c):
                return jnp.logical_and(c[0] < i, jnp.maximum(jnp.max(c[1][0]), jnp.max(c[1][1])) > SB_DEAD)

            def blocks(c):
                jj, st = c
                return jj + 1, group([(i - jj) * nsub - 1 - u for u in range(nsub)], [None] * nsub, st)

            _, state = lax.while_loop(live, blocks, (jnp.int32(0), state))
            o_ref[rq, :] = state[2]
            return 0

        lax.fori_loop(0, T // qt, qtile, 0)

    def col(off):
        return pl.BlockSpec((T, 128), lambda hp: (0, off + hp))

    steps = BRANCH_W // 128
    j = _job_args(job, 3, 1)
    res = pl.pallas_call(
        _hosting(body, job, 3, 1, 0, steps), name=name, grid=(steps,),
        in_specs=[col(cb), col(cb + 4), col(cb + 8)] + j["in_specs"], out_specs=[col(0)] + j["out_specs"],
        out_shape=[jax.ShapeDtypeStruct((T, BRANCH_W), F32)] + j["out_shape"],
        scratch_shapes=j["scratch"], input_output_aliases=j["aliases"],
        compiler_params=_params(("parallel",) if job is None else ("arbitrary",)),
    )(proj, proj, proj, *j["ins"])
    return res[0], list(res[1:])


def sb_bwd(proj, out, dout, *, name, job=None):
    T = proj.shape[0]
    qt = min(SB_QT, T)
    nsub = qt // CHUNK
    cb = C_SB // 128

    def body(q_ref, k_ref, v_ref, o_ref, do_ref, dq_ref, dk_ref, dv_ref, dkt_ref, dvt_ref):
        u_gt, u_ge, hms = _sb_consts()
        tile_mask = _sb_tile_mask(qt)
        tall_lane = lax.broadcasted_iota(jnp.int32, (qt, 128), 1)
        top = lax.broadcasted_iota(jnp.int32, (CHUNK, CHUNK), 0) < 64
        dkt_ref[...] = jnp.zeros_like(dkt_ref)
        dvt_ref[...] = jnp.zeros_like(dvt_ref)

        def qtile(i, _):
            rq = pl.ds(pl.multiple_of(i * qt, qt), qt)
            qs = q_ref[rq, :] * SB_SCALE
            qb, q_t = qs.astype(BF16), qs.T.astype(BF16)
            dov = do_ref[rq, :]
            dob, do_t = dov.astype(BF16), dov.T.astype(BF16)
            prod = dob.astype(F32) * o_ref[rq, :]
            total = [jnp.broadcast_to(jnp.sum(jnp.where(hm, prod, 0.0), axis=1, keepdims=True), (qt, 128))
                     for hm in (tall_lane < 64, tall_lane >= 64)]

            def group(js, masks, state):
                c_l, c_w, dq = list(state[:2]), list(state[2:4]), state[4]
                heads = [slice(h * 128, (h + 1) * 128) for h in range(2)]
                rows = [pl.ds(pl.multiple_of(j * CHUNK, CHUNK), CHUNK) for j in js]
                kb2 = [_per_head(k_ref[rk, :], hms) for rk in rows]
                logits = [_sb_logits(qb, kb, m) for kb, m in zip(kb2, masks)]
                da = [_dot(dob, _per_head(v_ref[rk, :], hms), NT) for rk in rows]
                sums = [[_dot(_pair(lsn[:, hc]), u_gt, NN) for hc in heads] for _, lsn in logits]
                a_b, w_all = [], []
                for (lsp, _), r, d, m in zip(logits, sums, da, masks):
                    a_h, w_h = [], []
                    for h, hc in enumerate(heads):
                        a = jnp.exp(lsp[:, hc] + r[h][:, :128] + c_l[h])
                        if m is not None:
                            a = jnp.where(m[:, hc], a, 0.0)
                        c_l[h] = c_l[h] + r[h][:, 128:]
                        a = a.astype(BF16)
                        a_h.append(a)
                        w_h.append(a.astype(F32) * d[:, hc])
                    a_b.append(jnp.concatenate(a_h, axis=1))
                    w_all.append(w_h)
                sums_w = [[_dot(_pair(w), u_ge, NN) for w in w_h] for w_h in w_all]
                dz_b = []
                for (lsp, _), w_h, r, m in zip(logits, w_all, sums_w, masks):
                    sp = jnp.exp(lsp)
                    dz_h = []
                    for h, hc in enumerate(heads):
                        later_w = r[h][:, :128] + c_w[h]
                        c_w[h] = c_w[h] + r[h][:, 128:]
                        dz = w_h[h] * (1.0 - sp[:, hc]) - sp[:, hc] * (total[h] - later_w)
                        if m is not None:
                            dz = jnp.where(m[:, hc], dz, 0.0)
                        dz_h.append(dz.astype(BF16))
                    dz_b.append(jnp.concatenate(dz_h, axis=1))
                for j, kb, a, dz in zip(js, kb2, a_b, dz_b):
                    dkt = _dot(q_t, dz, NN)
                    dvt = _dot(do_t, a, NN)
                    dkt_ref[j] += jnp.where(top, dkt[:, :128], dkt[:, 128:])
                    dvt_ref[j] += jnp.where(top, dvt[:, :128], dvt[:, 128:])
                    dq = dq + _dot(dz, kb, NN)
                return c_l[0], c_l[1], c_w[0], c_w[1], dq

            zero = jnp.zeros((qt, 128), F32)
            diag = list(reversed(range(nsub)))
            state = group([i * nsub + m for m in diag], [tile_mask(m) for m in diag], (zero,) * 5)

            def live(c):
                return jnp.logical_and(c[0] < i, jnp.maximum(jnp.max(c[1][0]), jnp.max(c[1][1])) > SB_DEAD)

            def blocks(c):
                jj, st = c
                return jj + 1, group([(i - jj) * nsub - 1 - u for u in range(nsub)], [None] * nsub, st)

            _, state = lax.while_loop(live, blocks, (jnp.int32(0), state))
            dq_ref[rq, :] = (state[4] * SB_SCALE).astype(BF16)
            return 0

        lax.fori_loop(0, T // qt, qtile, 0)

        def untranspose(jb, _):
            rk = pl.ds(pl.multiple_of(jb * CHUNK, CHUNK), CHUNK)
            dk_ref[rk, :] = dkt_ref[jb].T.astype(BF16)
            dv_ref[rk, :] = dvt_ref[jb].T.astype(BF16)
            return 0

        lax.fori_loop(0, T // CHUNK, untranspose, 0, unroll=4)

    def col(off):
        return pl.BlockSpec((T, 128), lambda hp: (0, off + hp))

    o16 = jax.ShapeDtypeStruct((T, BRANCH_W), BF16)
    steps = BRANCH_W // 128
    j = _job_args(job, 5, 3)
    acc = pltpu.VMEM((T // CHUNK, CHUNK, CHUNK), F32)
    res = pl.pallas_call(
        _hosting(body, job, 5, 3, 2, steps), name=name, grid=(steps,),
        in_specs=[col(cb), col(cb + 4), col(cb + 8), col(0), col(0)] + j["in_specs"],
        out_specs=[col(0), col(0), col(0)] + j["out_specs"], out_shape=[o16, o16, o16] + j["out_shape"],
        scratch_shapes=[acc, acc] + j["scratch"], input_output_aliases=j["aliases"],
        compiler_params=_params(("parallel",) if job is None else ("arbitrary",)),
    )(proj, proj, proj, out, dout, *j["ins"])
    return res[0], res[1], res[2], list(res[3:])


_G0 = math.sqrt(2.0 / math.pi)
_G1 = 0.044715


def _gelu(x):
    return 0.5 * x * (1.0 + jnp.tanh(_G0 * (x + _G1 * x * x * x)))


def _gelu_grad(x):
    t = jnp.tanh(_G0 * (x + _G1 * x * x * x))
    return 0.5 * (1.0 + t) + 0.5 * x * (1.0 - t * t) * (_G0 * (1.0 + 3.0 * _G1 * x * x))


def _tril():
    row, lane = _sb_masks()
    return row >= lane


def sgu_fwd(proj, ln_g, ln_b, w, bias, *, name):
    T = proj.shape[0]
    tb = min(512, T)
    G = BRANCH_W // 128

    def body(u_ref, v_ref, g_ref, b_ref, w_ref, bias_ref, o_ref):
        vv = _gelu(v_ref[...])
        xh, _ = _group_norm(vv)
        vn = (xh * g_ref[...] + b_ref[...]).astype(BF16)
        tril = _tril()
        for g in range(G):
            wg = jnp.where(tril, w_ref[g], 0.0).astype(BF16)
            gc = slice(g * 128, (g + 1) * 128)
            for c in range(tb // CHUNK):
                r = slice(c * CHUNK, (c + 1) * CHUNK)
                sv = _dot(wg, vn[r, gc], NN) + bias_ref[g]
                o_ref[r, gc] = _gelu(u_ref[r, gc]) * sv

    cu, cv = C_SGU // BRANCH_W, C_SGU // BRANCH_W + 1
    vec = pl.BlockSpec((1, BRANCH_W), lambda i: (0, 0))
    mat = pl.BlockSpec((G, CHUNK, CHUNK), lambda i: (0, 0, 0))
    return pl.pallas_call(
        body, name=name, grid=(T // tb,),
        in_specs=[pl.BlockSpec((tb, BRANCH_W), lambda i: (i, cu)), pl.BlockSpec((tb, BRANCH_W), lambda i: (i, cv)),
                  vec, vec, mat, mat],
        out_specs=pl.BlockSpec((tb, BRANCH_W), lambda i: (i, 0)),
        out_shape=jax.ShapeDtypeStruct((T, BRANCH_W), F32),
        compiler_params=_params(("parallel",)),
    )(proj, proj, ln_g, ln_b, w, bias)


def sgu_bwd(proj, ln_g, ln_b, w, bias, dout, *, name):
    T = proj.shape[0]
    tb = min(512, T)
    G = BRANCH_W // 128

    def body(u_ref, v_ref, g_ref, b_ref, w_ref, bias_ref, do_ref, dp_ref, dw_ref, dbias_ref, dg_ref, db_ref, dvn_ref):
        @pl.when(pl.program_id(0) == 0)
        def _():
            dw_ref[...] = jnp.zeros_like(dw_ref)
            dbias_ref[...] = jnp.zeros_like(dbias_ref)
            dg_ref[...] = jnp.zeros_like(dg_ref)
            db_ref[...] = jnp.zeros_like(db_ref)

        gv = v_ref[...]
        vv = _gelu(gv)
        xh, rstd = _group_norm(vv)
        vn = (xh * g_ref[...] + b_ref[...]).astype(BF16)
        tril = _tril()
        for g in range(G):
            wg = jnp.where(tril, w_ref[g], 0.0).astype(BF16)
            gc = slice(g * 128, (g + 1) * 128)
            for c in range(tb // CHUNK):
                r = slice(c * CHUNK, (c + 1) * CHUNK)
                vn_c = vn[r, gc]
                sv = _dot(wg, vn_c, NN) + bias_ref[g]
                gu = u_ref[r, gc]
                d_o = do_ref[r, gc]
                dp_ref[r, gc] = (d_o * sv * _gelu_grad(gu)).astype(BF16)
                dsv = d_o * _gelu(gu)
                dsv_b = dsv.astype(BF16)
                dvn_ref[r, gc] = _dot(wg, dsv_b, TN)
                dw_ref[g] += jnp.where(tril, _dot(dsv_b, vn_c, NT), 0.0)
                dbias_ref[g] += jnp.broadcast_to(jnp.sum(dsv, axis=1, keepdims=True), (CHUNK, CHUNK))
        dvn = dvn_ref[...]
        dg_ref[...] += jnp.sum(dvn * xh, axis=0, keepdims=True)
        db_ref[...] += jnp.sum(dvn, axis=0, keepdims=True)
        dxh = dvn * g_ref[...]
        m1 = jnp.mean(dxh, axis=-1, keepdims=True)
        m2 = jnp.mean(dxh * xh, axis=-1, keepdims=True)
        dp_ref[:, BRANCH_W:2 * BRANCH_W] = (rstd * (dxh - m1 - xh * m2) * _gelu_grad(gv)).astype(BF16)

    cu, cv = C_SGU // BRANCH_W, C_SGU // BRANCH_W + 1
    vec = pl.BlockSpec((1, BRANCH_W), lambda i: (0, 0))
    mat = pl.BlockSpec((G, CHUNK, CHUNK), lambda i: (0, 0, 0))
    blk = pl.BlockSpec((tb, BRANCH_W), lambda i: (i, 0))
    msh = jax.ShapeDtypeStruct((G, CHUNK, CHUNK), F32)
    vsh = jax.ShapeDtypeStruct((1, BRANCH_W), F32)
    return pl.pallas_call(
        body, name=name, grid=(T // tb,),
        in_specs=[pl.BlockSpec((tb, BRANCH_W), lambda i: (i, cu)), pl.BlockSpec((tb, BRANCH_W), lambda i: (i, cv)),
                  vec, vec, mat, mat, blk],
        out_specs=[pl.BlockSpec((tb, 2 * BRANCH_W), lambda i: (i, 0)), mat, mat, vec, vec],
        out_shape=[jax.ShapeDtypeStruct((T, 2 * BRANCH_W), BF16), msh, msh, vsh, vsh],
        scratch_shapes=[pltpu.VMEM((tb, BRANCH_W), F32)],
        compiler_params=_params(("arbitrary",)),
    )(proj, proj, ln_g, ln_b, w, bias, dout)


def merge_fwd(a1, a2, a3, p1, p2, p3, proj, *, name):
    T = a1.shape[0]
    tm, tn = min(1024, T), 512
    gb = C_GATE // tn

    def body(a1_ref, a2_ref, a3_ref, p1_ref, p2_ref, p3_ref, g1_ref, g2_ref, g3_ref, m_ref, r1_ref, r2_ref, r3_ref):
        m = None
        for a_ref, p_ref, g_ref, r_ref in ((a1_ref, p1_ref, g1_ref, r1_ref), (a2_ref, p2_ref, g2_ref, r2_ref),
                                           (a3_ref, p3_ref, g3_ref, r3_ref)):
            r = _dot(a_ref[...].astype(BF16), p_ref[...], NN)
            r_ref[...] = r.astype(r_ref.dtype)
            t = jax.nn.sigmoid(g_ref[...]) * r
            m = t if m is None else m + t
        m_ref[...] = m.astype(m_ref.dtype)

    a_spec = pl.BlockSpec((tm, BRANCH_W), lambda i, j: (i, 0))
    p_spec = pl.BlockSpec((BRANCH_W, tn), lambda i, j: (0, j))
    o_spec = pl.BlockSpec((tm, tn), lambda i, j: (i, j))
    gates = [pl.BlockSpec((tm, tn), functools.partial(lambda i, j, o: (i, o + j), o=gb + 2 * n)) for n in range(3)]
    return pl.pallas_call(
        body, name=name, grid=(T // tm, D_MODEL // tn),
        in_specs=[a_spec, a_spec, a_spec, p_spec, p_spec, p_spec, *gates],
        out_specs=[o_spec] * 4, out_shape=[jax.ShapeDtypeStruct((T, D_MODEL), BF16)] * 4,
        compiler_params=_params(("parallel", "parallel")),
    )(a1, a2, a3, p1, p2, p3, proj, proj, proj)


def _merge_bwd_epi(dm, r1, r2, r3, g1, g2, g3):
    d_r, d_g = [], []
    for r, g in ((r1, g1), (r2, g2), (r3, g3)):
        s = jax.nn.sigmoid(g)
        d_r.append(dm * s)
        d_g.append(dm * r.astype(F32) * (s * (1.0 - s)))
    return (*d_r, *d_g)


def _rows_call(fn, ins, out_dtypes, *, name, tr=256, job=None):
    first = ins[0][0] if isinstance(ins[0], tuple) else ins[0]
    R, C = first.shape[-2:]
    tr = min(tr, R)
    assert R % tr == 0, (name, R, tr)
    arrs, specs = [], []
    for x in ins:
        if isinstance(x, tuple):
            arrs.append(x[0])
            specs.append(pl.BlockSpec((None, tr, C), functools.partial(lambda i, n: (n, i, 0), n=x[1])))
        else:
            arrs.append(x)
            specs.append(pl.BlockSpec((tr, C), lambda i: (i, 0)))
    ni = len(arrs)

    def body(*refs):
        vals = fn(*[r[...] for r in refs[:ni]])
        for o_ref, v in zip(refs[ni:], vals):
            o_ref[...] = v.astype(o_ref.dtype)

    no = len(out_dtypes)
    j = _job_args(job, ni, no)
    res = pl.pallas_call(
        _hosting(body, job, ni, no, 0, R // tr), name=name, grid=(R // tr,), in_specs=specs + j["in_specs"],
        out_specs=[pl.BlockSpec((tr, C), lambda i: (i, 0)) for _ in out_dtypes] + j["out_specs"],
        out_shape=[jax.ShapeDtypeStruct((R, C), dt) for dt in out_dtypes] + j["out_shape"],
        scratch_shapes=j["scratch"], input_output_aliases=j["aliases"],
        compiler_params=_params(("parallel",) if job is None else ("arbitrary",)),
    )(*arrs, *j["ins"])
    return list(res) if job is None else (list(res[:no]), list(res[no:]))


def _tile_rows(rows, cols):
    t = 256
    while t > 8 and (t * cols > 512 * 1024 or rows % t):
        t //= 2
    return t


def _rows_at(fn, pos, ins, outs, steps, *, name, aliases=None):
    read = [n for n, (_, s) in enumerate(ins) if s is not ANY]
    ni = len(ins)

    def body(pos_ref, *refs):
        vals = fn(*[refs[n][...] for n in read])
        for o_ref, v in zip(refs[ni:], vals):
            o_ref[...] = v.astype(o_ref.dtype)

    return pl.pallas_call(
        body, name=name,
        grid_spec=pltpu.PrefetchScalarGridSpec(num_scalar_prefetch=1, grid=(steps,), in_specs=[s for _, s in ins],
                                               out_specs=[s for _, s in outs]),
        out_shape=[sh for sh, _ in outs],
        input_output_aliases={1 + i: o for i, o in (aliases or {}).items()},
        compiler_params=_params(("parallel",)),
    )(pos, *[a for a, _ in ins])


def cast_into_whole(pos, w, l, axis, *, name):
    _, r, n = w.shape
    tr = _tile_rows(r, n)
    if axis == 1:
        shape, spec = (r, n * N_CHIPS), pl.BlockSpec((tr, n), lambda i, p: (i, p[3]))
    else:
        shape, spec = (r * N_CHIPS, n), pl.BlockSpec((tr, n), lambda i, p: (p[3] * (r // tr) + i, 0))
    return _rows_at(lambda a: (a,), pos, [(w, pl.BlockSpec((None, tr, n), lambda i, p: (l, i, 0)))],
                    [(jax.ShapeDtypeStruct(shape, BF16), spec)], r // tr, name=name)[0]


def pair_sum(pos, theirs, g32, axis, *, name):
    rows2, cols = theirs.shape
    h = rows2 // (N_CHIPS if axis == 0 else 1)
    tr = _tile_rows(h, cols)
    hb = h // tr
    if axis == 1:
        own = pl.BlockSpec((tr, cols), lambda i, p: (p[2] * hb + i, 0))
    else:
        own = pl.BlockSpec((tr, cols), lambda i, p: ((2 * (i // hb) + p[2]) * hb + i % hb, 0))
    row = pl.BlockSpec((tr, cols), lambda i, p: (i, 0))
    return _rows_at(lambda t, m: (m + t.astype(F32),) * 2, pos, [(theirs, row), (g32, own)],
                    [(jax.ShapeDtypeStruct((rows2, cols), F32), row), (jax.ShapeDtypeStruct((rows2, cols), BF16), row)],
                    rows2 // tr, name=name)


def chip_sum(pos, h32, recv, l, axis, whole, *, name):
    _, depth, h, n = recv.shape
    tr = _tile_rows(h, n)
    hb = h // tr
    if axis == 1:
        mine = pl.BlockSpec((tr, n), lambda i, p: (i, p[3]))
    else:
        mine = pl.BlockSpec((tr, n), lambda i, p: (p[3] * hb + i, 0))
    ins = [(h32, mine)] + [(recv, pl.BlockSpec((None, None, tr, n), functools.partial(lambda i, p, j: (j, l, i, 0), j=j)))
                           for j in range(3)]
    if whole is not None:
        ins.append((whole, ANY))
    return _rows_at(lambda o, a, b, c: (((o + a.astype(F32)) + b.astype(F32)) + c.astype(F32),), pos, ins,
                    [(jax.ShapeDtypeStruct((depth, 2, h, n), F32), pl.BlockSpec((None, None, tr, n), lambda i, p: (l, p[2], i, 0)))],
                    hb, name=name, aliases=None if whole is None else {4: 0})[0]


def _adamw(w, g, m, v):
    m2 = ADAM_B1 * m + (1.0 - ADAM_B1) * g
    v2 = ADAM_B2 * v + (1.0 - ADAM_B2) * (g * g)
    m_hat = m2 / (1.0 - ADAM_B1 ** ADAM_STEP)
    v_hat = v2 / (1.0 - ADAM_B2 ** ADAM_STEP)
    delta = -ADAM_LR * (m_hat / (jnp.sqrt(v_hat) + ADAM_EPS) + ADAM_WD * w)
    return delta, m2, v2


def _place():
    return lax.axis_index("x"), lax.axis_index("y"), lax.axis_index("c")


def _chip_peers(x, y, c):
    return [((1 - x, y, c), 2 * (1 - x) + y), ((x, 1 - y, c), 2 * x + 1 - y), ((1 - x, 1 - y, c), 2 * (1 - x) + 1 - y)]


def _shard_of(ref, axis, k, n):
    start = pl.multiple_of(k * n, 128)
    return ref.at[pl.ds(start, n), :] if axis == 0 else ref.at[:, pl.ds(start, n)]


ANY = pl.BlockSpec(memory_space=pl.ANY)


class CopyJob:
    def __init__(self, ins, out_shape, scratch, copies, aliases=None):
        self.ins, self.out_shape, self.scratch, self.copies = list(ins), list(out_shape), list(scratch), copies
        self.aliases = dict(aliases or {})

    def start(self, ins, outs, sems):
        local, remote, _, _ = self.copies(ins, outs, sems)
        for d in local + remote:
            d.start()

    def finish(self, ins, outs, sems):
        local, remote, arrivals, relays = self.copies(ins, outs, sems)
        for needs, sends, _ in relays:
            for d in needs:
                d.wait_recv()
            for d in sends:
                d.start()
        for d in arrivals + [d for _, _, arrives in relays for d in arrives]:
            d.wait_recv()
        for d in remote + [d for _, sends, _ in relays for d in sends]:
            d.wait_send()
        for d in local:
            d.wait()


def merge_jobs(a, b):
    ai, ao, asc = len(a.ins), len(a.out_shape), len(a.scratch)

    def copies(ins, outs, sems):
        ra = a.copies(ins[:ai], outs[:ao], sems[:asc])
        rb = b.copies(ins[ai:], outs[ao:], sems[asc:])
        return tuple(x + y for x, y in zip(ra, rb))

    aliases = dict(a.aliases)
    aliases.update({ai + i: ao + o for i, o in b.aliases.items()})
    return CopyJob(a.ins + b.ins, a.out_shape + b.out_shape, a.scratch + b.scratch, copies, aliases)


def run_job(job, *, name):
    ni, no = len(job.ins), len(job.out_shape)

    def body(*refs):
        parts = refs[:ni], refs[ni:ni + no], refs[ni + no:]
        job.start(*parts)
        job.finish(*parts)

    return pl.pallas_call(
        body, name=name, in_specs=[ANY] * ni, out_specs=[ANY] * no, out_shape=job.out_shape,
        scratch_shapes=job.scratch, input_output_aliases=job.aliases,
    )(*job.ins)


def _job_args(job, n_in, n_out):
    if job is None:
        return dict(ins=[], in_specs=[], out_specs=[], out_shape=[], scratch=[], aliases={})
    return dict(ins=job.ins, in_specs=[ANY] * len(job.ins), out_specs=[ANY] * len(job.out_shape),
                out_shape=job.out_shape, scratch=job.scratch,
                aliases={n_in + i: n_out + o for i, o in job.aliases.items()})


def _hosting(body, job, n_in, n_out, n_scratch, grid):
    if job is None:
        return body
    ji, jo = len(job.ins), len(job.out_shape)
    grid = (grid,) if isinstance(grid, int) else tuple(grid)

    def at(ends):
        hit = None
        for ax, e in enumerate(ends):
            here = pl.program_id(ax) == e
            hit = here if hit is None else jnp.logical_and(hit, here)
        return hit

    def hosted(*refs):
        o = n_in + ji
        s = o + n_out + jo
        parts = refs[n_in:o], refs[o + n_out:s], refs[s + n_scratch:]

        @pl.when(at([0] * len(grid)))
        def _():
            job.start(*parts)

        body(*refs[:n_in], *refs[o:o + n_out], *refs[s:s + n_scratch])

        @pl.when(at([g - 1 for g in grid]))
        def _():
            job.finish(*parts)

    return hosted


def _job_sems(n_remote, n_local):
    return [pltpu.SemaphoreType.DMA((n_remote,)), pltpu.SemaphoreType.DMA((n_remote,)), pltpu.SemaphoreType.DMA((n_local,))]


def gather_job(shards, axes, chips=(0, 1, 2)):
    na = len(shards)

    def copies(ins, outs, sems):
        send, recv, _ = sems
        x, y, c = _place()
        k = 2 * x + y
        remote, relays = [], []
        for a in range(na):
            r = outs[a].shape[0] // (N_CHIPS if axes[a] == 0 else 1)
            n = outs[a].shape[axes[a]] // N_CHIPS
            half = r // 2

            def part(kk, cc, a=a, n=n, half=half):
                rows = pl.ds(pl.multiple_of(cc * half + (kk * n if axes[a] == 0 else 0), 8), half)
                return outs[a].at[rows, :] if axes[a] == 0 else outs[a].at[rows, pl.ds(pl.multiple_of(kk * n, 128), n)]

            needs, passes, lands = [], [], []
            for j, (peer, kp) in enumerate(_chip_peers(x, y, c)):
                if j not in chips:
                    continue
                s = 6 * a + j
                remote.append(pltpu.make_async_remote_copy(part(k, c), part(k, c), send.at[s], recv.at[s],
                                                           device_id=peer, device_id_type=MESH))
                needs.append(pltpu.make_async_remote_copy(part(kp, c), part(kp, c), send.at[s], recv.at[s],
                                                          device_id=peer, device_id_type=MESH))
                passes.append(pltpu.make_async_remote_copy(part(kp, c), part(kp, c), send.at[s + 3], recv.at[s + 3],
                                                           device_id=(x, y, 1 - c), device_id_type=MESH))
                lands.append(pltpu.make_async_remote_copy(part(kp, 1 - c), part(kp, 1 - c), send.at[s + 3], recv.at[s + 3],
                                                          device_id=(x, y, 1 - c), device_id_type=MESH))
            relays.append((needs, passes, lands))
        return [], remote, [], relays

    out_shape = [jax.ShapeDtypeStruct(w.shape, BF16) for w in shards]
    return CopyJob(shards, out_shape, _job_sems(6 * na, 1), copies, {a: a for a in range(na)})


def scatter_job(layers, g16, axes, filled, chips=(0, 1, 2)):
    na = len(axes)

    def shard_shape(a):
        r, c = g16[a].shape
        return (r // N_CHIPS, c) if axes[a] == 0 else (r, c // N_CHIPS)

    def copies(ins, outs, sems):
        send, recv_sems, _ = sems
        x, y, c = _place()
        remote = []
        for a in range(na):
            n = shard_shape(a)[axes[a]]
            for r, (peer, kp) in enumerate(_chip_peers(x, y, c)):
                if r not in chips:
                    continue
                remote.append(pltpu.make_async_remote_copy(_shard_of(ins[a], axes[a], kp, n), outs[a].at[r, layers[a]],
                                                           send.at[3 * a + r], recv_sems.at[3 * a + r],
                                                           device_id=peer, device_id_type=MESH))
        return [], remote, remote, []

    out_shape = [jax.ShapeDtypeStruct((3, DEPTH) + shard_shape(a), BF16) for a in range(na)]
    ins = list(g16)
    aliases = {}
    for a in range(na):
        if filled[a] is not None:
            aliases[len(ins)] = a
            ins.append(filled[a])
    return CopyJob(ins, out_shape, _job_sems(3 * na, 1), copies, aliases)


def pair_job(g16, axes):
    na = len(axes)
    pieces = [1 if ax == 1 else N_CHIPS for ax in axes]

    def copies(ins, outs, sems):
        send, recv, _ = sems
        x, y, c = _place()
        remote = []
        s = 0
        for a in range(na):
            rows = g16[a].shape[0] // (2 * pieces[a])
            for kk in range(pieces[a]):
                src = ins[a].at[pl.ds(pl.multiple_of((2 * kk + 1 - c) * rows, 8), rows), :]
                remote.append(pltpu.make_async_remote_copy(src, outs[a].at[pl.ds(kk * rows, rows), :], send.at[s], recv.at[s],
                                                           device_id=(x, y, 1 - c), device_id_type=MESH))
                s += 1
        return [], remote, remote, []

    out_shape = [jax.ShapeDtypeStruct((g.shape[0] // 2, g.shape[1]), BF16) for g in g16]
    return CopyJob(g16, out_shape, _job_sems(sum(pieces), 1), copies)


def join_job(shards):
    na = len(shards)

    def copies(ins, outs, sems):
        send, recv, _ = sems
        x, y, c = _place()
        remote = [pltpu.make_async_remote_copy(outs[a].at[:, c], outs[a].at[:, c], send.at[a], recv.at[a],
                                               device_id=(x, y, 1 - c), device_id_type=MESH) for a in range(na)]
        lands = [pltpu.make_async_remote_copy(outs[a].at[:, 1 - c], outs[a].at[:, 1 - c], send.at[a], recv.at[a],
                                              device_id=(x, y, 1 - c), device_id_type=MESH) for a in range(na)]
        return [], remote, lands, []

    out_shape = [jax.ShapeDtypeStruct(s.shape, F32) for s in shards]
    return CopyJob(shards, out_shape, _job_sems(na, 1), copies, {a: a for a in range(na)})


def small_job(p):
    def copies(ins, outs, sems):
        send, recv, loc = sems
        x, y, c = _place()
        me = 4 * x + 2 * y + c
        remote, lands = [], []
        for rel in range(1, 8):
            dx, dy, dc = rel >> 2, (rel >> 1) & 1, rel & 1
            peer = (1 - x if dx else x, 1 - y if dy else y, 1 - c if dc else c)
            who = 4 * peer[0] + 2 * peer[1] + peer[2]
            remote.append(pltpu.make_async_remote_copy(ins[0], outs[0].at[me], send.at[rel - 1], recv.at[rel - 1],
                                                       device_id=peer, device_id_type=MESH))
            lands.append(pltpu.make_async_remote_copy(ins[0], outs[0].at[who], send.at[rel - 1], recv.at[rel - 1],
                                                      device_id=peer, device_id_type=MESH))
        return [pltpu.make_async_copy(ins[0], outs[0].at[me], loc.at[0])], remote, lands, []

    return CopyJob([p], [jax.ShapeDtypeStruct((8,) + p.shape, F32)], _job_sems(7, 1), copies)


def small_sum(slots):
    def add(*terms):
        acc = terms[0]
        for t in terms[1:]:
            acc = acc + t
        return (acc,)

    return _rows_call(add, [(slots, d) for d in range(8)], [F32], name="small_sum", tr=8 * 47)[0]


BIG = ("w_in", "p_ret", "p_sb", "p_sgu", "w_out", "w_up", "w_down")
BIG_AXIS = {"w_in": 1, "p_ret": 1, "p_sb": 1, "p_sgu": 1, "w_out": 0, "w_up": 1, "w_down": 0}
SMALL = ("ret_gn_g", "ret_gn_b", "sgu_ln_g", "sgu_ln_b", "sgu_w", "sgu_b", "ln1_g", "ln1_b", "ln2_g", "ln2_b")


def layer_forward(l, x0, x0h, W, sm, rope, rconsts, hooks):
    n = f"l{l}_"
    job = hooks.fwd_job(l, "proj")
    proj = matmul(x0h, W["w_in"], mode="nn", tm=4096, tn=768, tk=1024, name=n + "proj", job=job)
    if job is not None:
        proj, job_out = proj
        hooks.done(job, job_out)
    retg, raw, states = ret_fwd(proj, *rope, rconsts, sm["ret_gn_g"], sm["ret_gn_b"], name=n + "ret_fwd")
    job = hooks.fwd_job(l, "sb")
    sb, job_out = sb_fwd(proj, name=n + "sb_fwd", job=job)
    if job is not None:
        hooks.done(job, job_out)
    sg = sgu_fwd(proj, sm["sgu_ln_g"], sm["sgu_ln_b"], sm["sgu_w"], sm["sgu_bias"], name=n + "sgu_fwd")
    merged, r1, r2, r3 = merge_fwd(retg, sb, sg, W["p_ret"], W["p_sb"], W["p_sgu"], proj, name=n + "merge_fwd")
    x1, xh1, rs1, x1h = matmul_ln(merged, W["w_out"], x0, sm["ln1_g"], sm["ln1_b"], tk=1024, name=n + "out_ln1")
    job = hooks.fwd_job(l, "up")
    h1 = matmul(x1h, W["w_up"], mode="nn", tm=2048, tn=1024, tk=1024, outs=((BF16, None),), name=n + "up", job=job)
    if job is not None:
        h1, job_out = h1
        hooks.done(job, job_out)
    job = hooks.fwd_job(l, "down")
    res = matmul_ln(h1, W["w_down"], x1, sm["ln2_g"], sm["ln2_b"], pro=_relu2, tk=1024, name=n + "down_ln2", job=job)
    if job is not None:
        res, job_out = res
        hooks.done(job, job_out)
    x2, xh2, rs2, x2h = res
    saved = dict(x0h=x0h, proj=proj, retg=retg, raw=raw, states=states, sb=sb, sg=sg, merged=merged, r=(r1, r2, r3),
                 x1h=x1h, xh1=xh1, rs1=rs1, h1=h1, xh2=xh2, rs2=rs2)
    return x2, x2h, saved


def layer_backward(l, dx2, s, W, sm, rope, rconsts, hooks):
    n = f"l{l}_"
    two = ((F32, None), (BF16, None))
    gw, gs = {}, {}
    job = hooks.bwd_job(l, "ln2")
    res = ln_bwd(dx2, s["xh2"], s["rs2"], sm["ln2_g"], name=n + "ln2_bwd", job=job)
    if job is not None:
        res, job_out = res
        hooks.done(job, job_out)
    du2, du2h, gs["ln2_g"], gs["ln2_b"] = res
    job = hooks.bwd_job(l, "g_down")
    gw["w_down"] = matmul(s["h1"], du2h, mode="tn", tm=1024, tn=1024, tk=4096, pro=_relu2, outs=two, name=n + "g_down", job=job)
    if job is not None:
        gw["w_down"], job_out = gw["w_down"]
        hooks.done(job, job_out)
    dh1 = matmul(du2h, W["w_down"], mode="nt", tm=2048, tn=1024, tk=1024, outs=((BF16, None),),
                 epi=lambda acc, h: (acc * (2.0 * jnp.maximum(h.astype(F32), 0.0)),), tiles=(s["h1"],), name=n + "d_h1")
    job = hooks.bwd_job(l, "g_up")
    gw["w_up"] = matmul(s["x1h"], dh1, mode="tn", tm=1024, tn=1024, tk=4096, outs=two, name=n + "g_up", job=job)
    if job is not None:
        gw["w_up"], job_out = gw["w_up"]
        hooks.done(job, job_out)
    dx1 = matmul(dh1, W["w_up"], mode="nt", tm=1024, tn=1024, tk=4096,
                 epi=lambda acc, d: (acc + ALPHA * d,), tiles=(du2,), name=n + "d_x1")
    du1, du1h, gs["ln1_g"], gs["ln1_b"] = ln_bwd(dx1, s["xh1"], s["rs1"], sm["ln1_g"], name=n + "ln1_bwd")
    gw["w_out"] = matmul(s["merged"], du1h, mode="tn", tm=1024, tn=1024, tk=4096, outs=two, name=n + "g_out")
    gate0 = C_GATE // 512
    dr1, dr2, dr3, dg1, dg2, dg3 = matmul(
        du1h, W["w_out"], mode="nt", tm=1024, tn=512, tk=1024, outs=((BF16, None),) * 6, epi=_merge_bwd_epi,
        tiles=(*s["r"], (s["proj"], gate0), (s["proj"], gate0 + 2), (s["proj"], gate0 + 4)), name=n + "d_merged")
    d_branch = {}
    for nm, a, dr in (("p_ret", s["retg"], dr1), ("p_sb", s["sb"], dr2), ("p_sgu", s["sg"], dr3)):
        gw[nm] = matmul(a, dr, mode="tn", tm=512, tn=1024, tk=2048, outs=two, name=n + "g_" + nm)
        d_branch[nm] = matmul(dr, W[nm], mode="nt", tm=1024, tn=512, tk=1024, name=n + "d_" + nm)
    job = hooks.pair(l, gw)
    dret, gs["ret_gn_g"], gs["ret_gn_b"], job_out = ret_bwd(s["proj"], *rope, rconsts, sm["ret_gn_g"], sm["ret_gn_b"],
                                                             s["raw"], s["states"], d_branch["p_ret"], name=n + "ret_bwd", job=job)
    if job is not None:
        hooks.done(job, job_out)
    job = hooks.scatter(l) if job is not None else None
    dsq, dsk, dsv, job_out = sb_bwd(s["proj"], s["sb"], d_branch["p_sb"], name=n + "sb_bwd", job=job)
    if job is not None:
        hooks.done(job, job_out)
    dsgu, gs["sgu_w"], dbias, gs["sgu_ln_g"], gs["sgu_ln_b"] = sgu_bwd(
        s["proj"], sm["sgu_ln_g"], sm["sgu_ln_b"], sm["sgu_w"], sm["sgu_bias"], d_branch["p_sgu"], name=n + "sgu_bwd")
    gs["sgu_b"] = dbias[:, :, 0]
    dproj = jnp.concatenate([dret, dsq, dsk, dsv, dsgu, dg1, dg2, dg3], axis=1)
    job = hooks.small(l, gs)
    gw["w_in"] = matmul(s["x0h"], dproj, mode="tn", tm=1024, tn=1536, tk=2048, outs=two, name=n + "g_in", job=job)
    if job is not None:
        gw["w_in"], job_out = gw["w_in"]
        hooks.done(job, job_out)
    job = hooks.tail(l, gw["w_in"])
    dx0 = matmul(dproj, W["w_in"], mode="nt", tm=1024, tn=1024, tk=2560,
                 epi=lambda acc, d: (acc + ALPHA * d,), tiles=(du1,), name=n + "d_x0", job=job)
    if job is not None:
        dx0, job_out = dx0
        hooks.done(job, job_out)
    return dx0, gw, gs


def local_step(x, target, small, plan):
    T = x.shape[0]
    rope = _rope_tables(T)
    rconsts = _ret_consts()
    sms = []
    for l in range(DEPTH):
        sm = {k: small[k][l][None, :] for k in SMALL if k not in ("sgu_w", "sgu_b")}
        sm["sgu_w"] = small["sgu_w"][l]
        sm["sgu_bias"] = jnp.broadcast_to(small["sgu_b"][l][:, :, None], (4, CHUNK, CHUNK))
        sms.append(sm)
    h, saved = x, []
    job = plan.first_job()
    hh = _rows_call(lambda a: (a,), [x], [BF16], name="cast_x", job=job)
    if job is not None:
        hh, job_out = hh
        plan.done(job, job_out)
    hh = hh[0]
    for l in range(DEPTH):
        h, hh, s = layer_forward(l, h, hh, plan.weights(l), sms[l], rope, rconsts, plan)
        saved.append(s)
    dy, sq = loss_head(h, target)
    gs = {k: [None] * DEPTH for k in SMALL}
    for l in reversed(range(DEPTH)):
        dy, gwl, gsl = layer_backward(l, dy, saved[l], plan.weights(l), sms[l], rope, rconsts, plan)
        plan.grads(l, gwl)
        for k in SMALL:
            gs[k][l] = gsl[k].reshape(small[k].shape[1:])
    return sq[0, 0], dy, {k: jnp.stack(v) for k, v in gs.items()}


EARLY_GRADS = ("p_ret", "p_sb", "p_sgu", "w_out", "w_up", "w_down")


class _StepPlan:
    def __init__(self, pos, shards16):
        self.pos = pos
        self.shards16 = shards16
        self.full = [dict() for _ in range(DEPTH)]
        self.gw = [None] * DEPTH
        self.bufs = {}
        self.sums = {}
        self.gs = [None] * DEPTH

    def first_job(self):
        return self._gather([(0, "w_in")])

    def weights(self, l):
        return self.full[l]

    def grads(self, l, gw):
        self.gw[l] = gw

    def _gather(self, items, chips=(0, 1, 2)):
        job = gather_job([self.shards16[l][k] for l, k in items], [BIG_AXIS[k] for _, k in items], chips)
        job.note = ("gather" if 2 in chips else "gather_part", items)
        return job

    def _pair(self, items):
        job = pair_job([g[1] for _, _, g in items], [BIG_AXIS[k] for _, k, _ in items])
        job.note = ("pair", items)
        return job

    def fwd_job(self, l, host):
        if host == "proj":
            return None
        if host == "sb":
            return self._gather([(l, k) for k in BIG[1:]])
        if l + 1 == DEPTH:
            return None
        return self._gather([(l + 1, "w_in")], (0, 1) if host == "up" else (2,))

    def bwd_job(self, l, host):
        if l + 1 == DEPTH:
            return None
        if host == "ln2":
            job = self._pair([(l + 1, "w_in", self.gw[l + 1]["w_in"])])
            job.note = ("pair_w_in", job.note[1])
            return job
        items, sums16 = self.summed_w_in
        job = scatter_job([l_ for l_, _, _ in items], sums16, [BIG_AXIS[k] for _, k, _ in items],
                          [self.bufs.get(k) for _, k, _ in items], (0, 1) if host == "g_down" else (2,))
        job.note = ("scatter", items)
        return job

    def pair(self, l, ready):
        return self._pair([(l, k, ready[k]) for k in EARLY_GRADS])

    def scatter(self, l):
        items, sums16 = self.summed
        job = scatter_job([l_ for l_, _, _ in items], sums16, [BIG_AXIS[k] for _, k, _ in items],
                          [self.bufs.get(k) for _, k, _ in items])
        job.note = ("scatter", items)
        return job

    def small(self, l, gs):
        self.gs[l] = {k: gs[k].reshape(-1) for k in SMALL}
        if l != 0:
            return None
        job = small_job(_pack_small({k: jnp.stack([self.gs[l_][k] for l_ in range(DEPTH)]) for k in SMALL}))
        shards = []
        for k in EARLY_GRADS:
            whole = None
            for l_ in range(DEPTH):
                whole = chip_sum(self.pos, self.sums[(l_, k)], self.bufs[k], l_, BIG_AXIS[k], whole, name=f"chip_sum_{k}_{l_}")
            shards.append(whole)
        job = merge_jobs(job, join_job(shards))
        job.note = ("small", [])
        return job

    def tail(self, l, g):
        if l != 0:
            return None
        last = self._pair([(0, "w_in", g)])
        self.done(last, run_job(last, name="pair_last"))
        return self.scatter(0)

    def done(self, job, outs):
        kind, items = job.note
        if kind == "small":
            self.small_slots = outs[0]
            self.joined = dict(zip(EARLY_GRADS, outs[1:]))
        if kind in ("pair", "pair_w_in"):
            sums16 = []
            for a, (l, k, g) in enumerate(items):
                self.sums[(l, k)], s16 = pair_sum(self.pos, outs[a], g[0], BIG_AXIS[k], name=f"pair_sum_{k}_{l}")
                sums16.append(s16)
            if kind == "pair":
                self.summed = (items, sums16)
            else:
                self.summed_w_in = (items, sums16)
        for a, item in enumerate(items):
            if kind == "gather_part":
                self.shards16[item[0]][item[1]] = outs[a]
            elif kind == "gather":
                self.full[item[0]][item[1]] = outs[a]
            elif kind == "scatter":
                self.bufs[item[1]] = outs[a]

    def finish(self):
        return self.bufs, self.sums


def _flat2(a):
    return a.reshape(-1, a.shape[-1])


def _pack_small(d, pre=""):
    return jnp.concatenate([d[pre + k].reshape(-1) for k in SMALL]).reshape(-1, 128)


def kernel(x, w_in, ret_gn_g, ret_gn_b, sgu_ln_g, sgu_ln_b, sgu_w, sgu_b, p_ret, p_sb, p_sgu, w_out, ln1_g, ln1_b, w_up, w_down, ln2_g, ln2_b, loss_target, m_w_in, m_ret_gn_g, m_ret_gn_b, m_sgu_ln_g, m_sgu_ln_b, m_sgu_w, m_sgu_b, m_p_ret, m_p_sb, m_p_sgu, m_w_out, m_ln1_g, m_ln1_b, m_w_up, m_w_down, m_ln2_g, m_ln2_b, v_w_in, v_ret_gn_g, v_ret_gn_b, v_sgu_ln_g, v_sgu_ln_b, v_sgu_w, v_sgu_b, v_p_ret, v_p_sb, v_p_sgu, v_w_out, v_ln1_g, v_ln1_b, v_w_up, v_w_down, v_ln2_g, v_ln2_b):
    given = dict(locals())
    order = BIG[:1] + SMALL[:6] + BIG[1:5] + SMALL[6:8] + BIG[5:7] + SMALL[8:10]
    L = DEPTH

    px, py, pc = _place()
    pos = jnp.stack([px, py, pc, 2 * px + py]).astype(jnp.int32)

    shards16 = [{k: cast_into_whole(pos, given[k], l, BIG_AXIS[k], name=f"cast_{k}_{l}") for k in BIG} for l in range(L)]
    plan = _StepPlan(pos, shards16)
    sq, dx, _ = local_step(x[0], loss_target[0], {k: given[k] for k in SMALL}, plan)
    loss = 0.5 * lax.psum(sq, ("x", "y", "c"))

    bufs, sums = plan.finish()
    whole = None
    for l in range(L):
        whole = chip_sum(pos, sums[(l, "w_in")], bufs["w_in"], l, BIG_AXIS["w_in"], whole, name=f"chip_sum_w_in_{l}")
    joined = dict(plan.joined, w_in=run_job(join_job([whole]), name="join_halves")[0])
    out = {}
    for k in BIG:
        shp = given[k].shape
        res = _rows_call(lambda g_, w_, m_, v_: (g_,) + _adamw(w_, g_, m_, v_),
                         [joined[k].reshape(-1, shp[-1]), _flat2(given[k]), _flat2(given["m_" + k]), _flat2(given["v_" + k])],
                         [F32] * 4, name="adamw_" + k)
        out[k] = [r.reshape(shp) for r in res]

    pack = _pack_small
    res = _rows_call(lambda g_, w_, m_, v_: (g_,) + _adamw(w_, g_, m_, v_),
                     [small_sum(plan.small_slots), pack(given), pack(given, "m_"), pack(given, "v_")], [F32] * 4,
                     name="adamw_small", tr=8 * 47)
    off = 0
    for k in SMALL:
        sz = given[k].size
        out[k] = [r.reshape(-1)[off:off + sz].reshape(given[k].shape) for r in res]
        off += sz

    grads = [out[k][0] for k in order]
    deltas = [out[k][1] for k in order]
    new_m = [out[k][2] for k in order]
    new_v = [out[k][3] for k in order]
    return (loss, dx[None], *grads, *deltas, *new_m, *new_v)
```

```python
import functools
import math

import jax
import jax.numpy as jnp
from jax import lax
from jax.experimental import pallas as pl
from jax.experimental.pallas import tpu as pltpu

F32 = jnp.float32
BF16 = jnp.bfloat16

D_MODEL = 1024
SEQ = 4096
DEPTH = 2
CHUNK = 128
RET_HEADS = 4
BRANCH_W = 512
N_IN = 7680
D_FF = 4096
LN_EPS = 1e-5
ROPE_BASE = 10000.0
ALPHA = (2 * DEPTH) ** 0.25
RET_SCALE = 128 ** -0.5
SB_SCALE = 64 ** -0.5
C_RET, C_SB, C_SGU, C_GATE = 0, 2048, 3584, 4608

ADAM_LR, ADAM_B1, ADAM_B2, ADAM_EPS, ADAM_WD, ADAM_STEP = 0.001, 0.9, 0.999, 1e-08, 0.01, 10

N_CHIPS = 4
VMEM_LIMIT = 56 * 1024 * 1024
MESH = pl.DeviceIdType.MESH

NN = ((1,), (0,))
NT = ((1,), (1,))
TN = ((0,), (0,))


def _dot(a, b, dims):
    return lax.dot_general(a, b, (dims, ((), ())), preferred_element_type=F32)


def _params(sem):
    return pltpu.CompilerParams(dimension_semantics=sem, vmem_limit_bytes=VMEM_LIMIT)


def _relu2(h):
    r = jnp.maximum(h.astype(F32), 0.0)
    return r * r


def matmul(a, b, *, mode, tm, tn, tk, outs=((F32, None),), pro=None, epi=None, tiles=(), rows=(), name, job=None):
    if mode == "nn":
        (M, K), N = a.shape, b.shape[1]
    elif mode == "nt":
        (M, K), N = a.shape, b.shape[0]
    else:
        (K, M), N = a.shape, b.shape[1]
    tm, tn, tk = min(tm, M), min(tn, N), min(tk, K)
    assert M % tm == 0 and N % tn == 0 and K % tk == 0, (name, M, N, K, tm, tn, tk)
    if mode == "nn":
        a_spec = pl.BlockSpec((tm, tk), lambda i, j, k: (i, k))
        b_spec = pl.BlockSpec((tk, tn), lambda i, j, k: (k, j))
        dims = NN
    elif mode == "nt":
        a_spec = pl.BlockSpec((tm, tk), lambda i, j, k: (i, k))
        b_spec = pl.BlockSpec((tn, tk), lambda i, j, k: (j, k))
        dims = NT
    else:
        a_spec = pl.BlockSpec((tk, tm), lambda i, j, k: (k, i))
        b_spec = pl.BlockSpec((tk, tn), lambda i, j, k: (k, j))
        dims = TN
    nk = K // tk
    nt_, nr, no = len(tiles), len(rows), len(outs)

    def body(a_ref, b_ref, *rest):
        tile_refs = rest[:nt_]
        row_refs = rest[nt_:nt_ + nr]
        out_refs = rest[nt_ + nr:nt_ + nr + no]
        av = a_ref[...]
        if pro is not None:
            av = pro(av)
        p = _dot(av.astype(BF16), b_ref[...].astype(BF16), dims)

        def finish(acc):
            vals = (acc,) * no if epi is None else epi(acc, *[r[...] for r in tile_refs], *[r[...] for r in row_refs])
            for o_ref, v in zip(out_refs, vals):
                o_ref[...] = v.astype(o_ref.dtype)

        if nk == 1:
            finish(p)
        else:
            acc_ref = rest[-1]
            k = pl.program_id(2)

            @pl.when(k == 0)
            def _():
                acc_ref[...] = p

            @pl.when(k > 0)
            def _():
                acc_ref[...] += p

            @pl.when(k == nk - 1)
            def _():
                finish(acc_ref[...])

    out_shape, out_specs = [], []
    for dt, width in outs:
        if width is None:
            out_shape.append(jax.ShapeDtypeStruct((M, N), dt))
            out_specs.append(pl.BlockSpec((tm, tn), lambda i, j, k: (i, j)))
        else:
            assert N == tn
            out_shape.append(jax.ShapeDtypeStruct((M, width), dt))
            out_specs.append(pl.BlockSpec((tm, width), lambda i, j, k: (i, 0)))
    in_specs = [a_spec, b_spec]
    offs = [t[1] if isinstance(t, tuple) else 0 for t in tiles]
    tiles = [t[0] if isinstance(t, tuple) else t for t in tiles]
    in_specs += [pl.BlockSpec((tm, tn), functools.partial(lambda i, j, k, o: (i, j + o), o=o)) for o in offs]
    in_specs += [pl.BlockSpec((1, tn), lambda i, j, k: (0, j)) for _ in rows]
    grid = (M // tm, N // tn, nk)
    scratch = [pltpu.VMEM((tm, tn), F32)] if nk > 1 else []
    j = _job_args(job, len(in_specs), no)
    res = pl.pallas_call(
        _hosting(body, job, len(in_specs), no, len(scratch), grid), name=name, grid=grid,
        in_specs=in_specs + j["in_specs"], out_specs=out_specs + j["out_specs"], out_shape=out_shape + j["out_shape"],
        scratch_shapes=scratch + j["scratch"], input_output_aliases=j["aliases"],
        compiler_params=_params(("parallel", "parallel", "arbitrary") if job is None else ("arbitrary",) * 3),
    )(a, b, *tiles, *rows, *j["ins"])
    mine = res[0] if no == 1 else list(res[:no])
    return mine if job is None else (mine, list(res[no:]))


def _ln_epi(acc, res, g, b):
    u = ALPHA * res + acc
    mu = jnp.mean(u, axis=-1, keepdims=True)
    xc = u - mu
    var = jnp.mean(xc * xc, axis=-1, keepdims=True)
    rstd = lax.rsqrt(var + LN_EPS)
    xhat = xc * rstd
    y = xhat * g + b
    return y, xhat, jnp.broadcast_to(rstd, (u.shape[0], 128)), y


def matmul_ln(a, w, res, g, b, *, pro=None, tk, name, job=None):
    n = w.shape[1]
    return matmul(a, w, mode="nn", tm=1024, tn=n, tk=tk, pro=pro, epi=_ln_epi, tiles=(res,), rows=(g, b),
                  outs=((F32, None), (F32, None), (F32, 128), (BF16, None)), name=name, job=job)


def ln_bwd(dy, xhat, rstd, g, *, name, job=None):
    T, D = dy.shape
    tm = min(512, T)

    def body(dy_ref, xh_ref, rs_ref, g_ref, du_ref, du16_ref, dg_ref, db_ref):
        dyv, xh = dy_ref[...], xh_ref[...]
        r = rs_ref[:, 0:1]
        dxh = dyv * g_ref[...]
        m1 = jnp.mean(dxh, axis=-1, keepdims=True)
        m2 = jnp.mean(dxh * xh, axis=-1, keepdims=True)
        du = r * (dxh - m1 - xh * m2)
        du_ref[...] = du
        du16_ref[...] = du.astype(BF16)

        @pl.when(pl.program_id(0) == 0)
        def _():
            dg_ref[...] = jnp.zeros_like(dg_ref)
            db_ref[...] = jnp.zeros_like(db_ref)

        dg_ref[...] += jnp.sum(dyv * xh, axis=0, keepdims=True)
        db_ref[...] += jnp.sum(dyv, axis=0, keepdims=True)

    row = pl.BlockSpec((tm, D), lambda i: (i, 0))
    vec = pl.BlockSpec((1, D), lambda i: (0, 0))
    j = _job_args(job, 4, 4)
    res = pl.pallas_call(
        _hosting(body, job, 4, 4, 0, T // tm), name=name, grid=(T // tm,),
        in_specs=[row, row, pl.BlockSpec((tm, 128), lambda i: (i, 0)), vec] + j["in_specs"],
        out_specs=[row, row, vec, vec] + j["out_specs"],
        out_shape=[jax.ShapeDtypeStruct((T, D), F32), jax.ShapeDtypeStruct((T, D), BF16),
                   jax.ShapeDtypeStruct((1, D), F32), jax.ShapeDtypeStruct((1, D), F32)] + j["out_shape"],
        scratch_shapes=j["scratch"], input_output_aliases=j["aliases"],
        compiler_params=_params(("arbitrary",)),
    )(dy, xhat, rstd, g, *j["ins"])
    return list(res[:4]) if job is None else (list(res[:4]), list(res[4:]))


def loss_head(y, target):
    T, D = y.shape
    tm = min(512, T)

    def body(y_ref, t_ref, dy_ref, s_ref):
        e = y_ref[...] - t_ref[...]
        dy_ref[...] = e * (1.0 / D)

        @pl.when(pl.program_id(0) == 0)
        def _():
            s_ref[...] = jnp.zeros_like(s_ref)

        s_ref[...] += jnp.sum(jnp.mean(e * e, axis=-1, keepdims=True))

    row = pl.BlockSpec((tm, D), lambda i: (i, 0))
    return pl.pallas_call(
        body, name="loss_head", grid=(T // tm,),
        in_specs=[row, row], out_specs=[row, pl.BlockSpec((8, 128), lambda i: (0, 0))],
        out_shape=[jax.ShapeDtypeStruct((T, D), F32), jax.ShapeDtypeStruct((8, 128), F32)],
        compiler_params=_params(("arbitrary",)),
    )(y, target)


def _rope_tables(T):
    half = 64
    inv_freq = ROPE_BASE ** (-jnp.arange(half, dtype=F32) / half)
    ang = jnp.arange(T, dtype=jnp.int32).astype(F32)[:, None] * inv_freq[None, :]
    cos, sin = jnp.cos(ang), jnp.sin(ang)
    return jnp.concatenate([cos, cos], axis=1), jnp.concatenate([-sin, sin], axis=1)


def _ret_consts():
    H = RET_HEADS
    log_g = jnp.log(1.0 - 2.0 ** (-5.0 - jnp.arange(H, dtype=F32)))
    idx = jnp.arange(CHUNK, dtype=F32)
    diff = idx[:, None] - idx[None, :]
    dmat = jnp.where(diff[None] >= 0, jnp.exp(log_g[:, None, None] * diff[None]), 0.0)
    kd = jnp.exp(log_g[:, None] * (CHUNK - 1 - idx)[None, :])
    qd = jnp.exp(log_g[:, None] * (idx + 1.0)[None, :])
    cd = jnp.exp(log_g * CHUNK)
    full = (H, CHUNK, CHUNK)
    return (dmat.astype(F32), jnp.broadcast_to(kd[:, :, None], full), jnp.broadcast_to(qd[:, :, None], full),
            jnp.broadcast_to(cd[:, None, None], full))


def _swap_halves(v):
    return pltpu.roll(v, 64, 1)


def _group_norm(o):
    mu = jnp.mean(o, axis=-1, keepdims=True)
    xc = o - mu
    var = jnp.mean(xc * xc, axis=-1, keepdims=True)
    rstd = lax.rsqrt(var + LN_EPS)
    return xc * rstd, rstd


def ret_fwd(proj, cosf, sinf, consts, gn_g, gn_b, *, name):
    T = proj.shape[0]
    tb = min(512, T)
    nch = tb // CHUNK
    H = RET_HEADS

    def body(p_ref, cos_ref, sin_ref, dm_ref, kd_ref, qd_ref, cd_ref, g_ref, b_ref, out_ref, raw_ref, st_ref, s_ref):
        @pl.when(pl.program_id(0) == 0)
        def _():
            s_ref[...] = jnp.zeros_like(s_ref)

        for c in range(nch):
            r = slice(c * CHUNK, (c + 1) * CHUNK)
            cs, sn = cos_ref[r, :], sin_ref[r, :]
            for h in range(H):
                hc = slice(h * 128, (h + 1) * 128)
                q = p_ref[r, h * 128:(h + 1) * 128]
                k = p_ref[r, 512 + h * 128:512 + (h + 1) * 128]
                v = p_ref[r, 1024 + h * 128:1024 + (h + 1) * 128]
                gt = p_ref[r, 1536 + h * 128:1536 + (h + 1) * 128]
                qr = q * cs + _swap_halves(q) * sn
                kr = (k * cs + _swap_halves(k) * sn) * RET_SCALE
                sprev = s_ref[h]
                st_ref[c, h] = sprev
                qb, kb, vb = qr.astype(BF16), kr.astype(BF16), v.astype(BF16)
                s = _dot(qb, kb, NT) * dm_ref[h]
                o = _dot(s.astype(BF16), vb, NN) + _dot((qr * qd_ref[h]).astype(BF16), sprev.astype(BF16), NN)
                s_ref[h] = sprev * cd_ref[h] + _dot((kr * kd_ref[h]).astype(BF16), vb, TN)
                raw_ref[r, hc] = o
                y, _ = _group_norm(o)
                out_ref[r, hc] = (gt * jax.nn.sigmoid(gt)) * (y * g_ref[:, hc] + b_ref[:, hc])

    cmat = pl.BlockSpec((H, CHUNK, CHUNK), lambda i: (0, 0, 0))
    vec = pl.BlockSpec((1, BRANCH_W), lambda i: (0, 0))
    rope = pl.BlockSpec((tb, 128), lambda i: (i, 0))
    blk = pl.BlockSpec((tb, BRANCH_W), lambda i: (i, 0))
    return pl.pallas_call(
        body, name=name, grid=(T // tb,),
        in_specs=[pl.BlockSpec((tb, 2048), lambda i: (i, 0)), rope, rope, cmat, cmat, cmat, cmat, vec, vec],
        out_specs=[blk, blk, pl.BlockSpec((nch, H, CHUNK, CHUNK), lambda i: (i, 0, 0, 0))],
        out_shape=[jax.ShapeDtypeStruct((T, BRANCH_W), F32), jax.ShapeDtypeStruct((T, BRANCH_W), F32),
                   jax.ShapeDtypeStruct((T // CHUNK, H, CHUNK, CHUNK), F32)],
        scratch_shapes=[pltpu.VMEM((H, CHUNK, CHUNK), F32)],
        compiler_params=_params(("arbitrary",)),
    )(proj, cosf, sinf, *consts, gn_g, gn_b)


def ret_bwd(proj, cosf, sinf, consts, gn_g, gn_b, raw, states, dout, *, name, job=None):
    T = proj.shape[0]
    tb = min(512, T)
    nch = tb // CHUNK
    nb = T // tb
    H = RET_HEADS

    def body(p_ref, cos_ref, sin_ref, dm_ref, kd_ref, qd_ref, cd_ref, g_ref, b_ref, raw_ref, st_ref, do_ref,
             dp_ref, dg_ref, db_ref, ds_ref):
        @pl.when(pl.program_id(0) == 0)
        def _():
            ds_ref[...] = jnp.zeros_like(ds_ref)
            dg_ref[...] = jnp.zeros_like(dg_ref)
            db_ref[...] = jnp.zeros_like(db_ref)

        for c in reversed(range(nch)):
            r = slice(c * CHUNK, (c + 1) * CHUNK)
            cs, sn = cos_ref[r, :], sin_ref[r, :]
            for h in range(H):
                hc = slice(h * 128, (h + 1) * 128)
                q = p_ref[r, h * 128:(h + 1) * 128]
                k = p_ref[r, 512 + h * 128:512 + (h + 1) * 128]
                v = p_ref[r, 1024 + h * 128:1024 + (h + 1) * 128]
                gt = p_ref[r, 1536 + h * 128:1536 + (h + 1) * 128]
                qr = q * cs + _swap_halves(q) * sn
                kr = (k * cs + _swap_halves(k) * sn) * RET_SCALE
                sprev = st_ref[c, h]
                gv = g_ref[:, hc]
                y, rstd = _group_norm(raw_ref[r, hc])
                d_out = do_ref[r, hc]
                sg = jax.nn.sigmoid(gt)
                d_gate = d_out * (y * gv + b_ref[:, hc]) * (sg * (1.0 + gt * (1.0 - sg)))
                d_aff = d_out * (gt * sg)
                dg_ref[:, hc] += jnp.sum(d_aff * y, axis=0, keepdims=True)
                db_ref[:, hc] += jnp.sum(d_aff, axis=0, keepdims=True)
                dxh = d_aff * gv
                m1 = jnp.mean(dxh, axis=-1, keepdims=True)
                m2 = jnp.mean(dxh * y, axis=-1, keepdims=True)
                d_o = (rstd * (dxh - m1 - y * m2)).astype(BF16)
                qb, kb, vb = qr.astype(BF16), kr.astype(BF16), v.astype(BF16)
                dm, kd, qd = dm_ref[h], kd_ref[h], qd_ref[h]
                p = (_dot(qb, kb, NT) * dm).astype(BF16)
                dp = (_dot(d_o, vb, NT) * dm).astype(BF16)
                dsn = ds_ref[h]
                dsb = dsn.astype(BF16)
                dq_r = _dot(dp, kb, NN) + _dot(d_o, sprev.astype(BF16), NT) * qd
                dk_r = (_dot(dp, qb, TN) + _dot(vb, dsb, NT) * kd) * RET_SCALE
                d_v = _dot(p, d_o, TN) + _dot((kr * kd).astype(BF16), dsb, NN)
                ds_ref[h] = dsn * cd_ref[h] + _dot((qr * qd).astype(BF16), d_o, TN)
                dp_ref[r, h * 128:(h + 1) * 128] = (dq_r * cs - _swap_halves(dq_r) * sn).astype(BF16)
                dp_ref[r, 512 + h * 128:512 + (h + 1) * 128] = (dk_r * cs - _swap_halves(dk_r) * sn).astype(BF16)
                dp_ref[r, 1024 + h * 128:1024 + (h + 1) * 128] = d_v.astype(BF16)
                dp_ref[r, 1536 + h * 128:1536 + (h + 1) * 128] = d_gate.astype(BF16)

    cmat = pl.BlockSpec((H, CHUNK, CHUNK), lambda i: (0, 0, 0))
    vec = pl.BlockSpec((1, BRANCH_W), lambda i: (0, 0))
    rope = pl.BlockSpec((tb, 128), lambda i: (nb - 1 - i, 0))
    blk = pl.BlockSpec((tb, BRANCH_W), lambda i: (nb - 1 - i, 0))
    wide = pl.BlockSpec((tb, 2048), lambda i: (nb - 1 - i, 0))
    j = _job_args(job, 12, 3)
    res = pl.pallas_call(
        _hosting(body, job, 12, 3, 1, nb), name=name, grid=(nb,),
        in_specs=[wide, rope, rope, cmat, cmat, cmat, cmat, vec, vec, blk,
                  pl.BlockSpec((nch, H, CHUNK, CHUNK), lambda i: (nb - 1 - i, 0, 0, 0)), blk] + j["in_specs"],
        out_specs=[wide, vec, vec] + j["out_specs"],
        out_shape=[jax.ShapeDtypeStruct((T, 2048), BF16), jax.ShapeDtypeStruct((1, BRANCH_W), F32),
                   jax.ShapeDtypeStruct((1, BRANCH_W), F32)] + j["out_shape"],
        scratch_shapes=[pltpu.VMEM((H, CHUNK, CHUNK), F32)] + j["scratch"], input_output_aliases=j["aliases"],
        compiler_params=_params(("arbitrary",)),
    )(proj, cosf, sinf, *consts, gn_g, gn_b, raw, states, dout, *j["ins"])
    return res[0], res[1], res[2], list(res[3:])


def _sb_masks():
    row = lax.broadcasted_iota(jnp.int32, (CHUNK, CHUNK), 0)
    lane = lax.broadcasted_iota(jnp.int32, (CHUNK, CHUNK), 1)
    return row, lane


SB_QT = 256
SB_DEAD = -105.0


def _pair(v):
    hi = v.astype(BF16)
    return jnp.concatenate([hi, (v - hi.astype(F32)).astype(BF16)], axis=1)


def _sb_consts():
    r = lax.broadcasted_iota(jnp.int32, (256, 256), 0) & 127
    c = lax.broadcasted_iota(jnp.int32, (256, 256), 1)
    ones = c >= 128
    lane = lax.broadcasted_iota(jnp.int32, (CHUNK, CHUNK), 1)
    return (ones | (r > c)).astype(BF16), (ones | (r >= c)).astype(BF16), (lane < 64, lane >= 64)


def _per_head(x, hms):
    return jnp.concatenate([jnp.where(hm, x, 0.0) for hm in hms], axis=0).astype(BF16)


def _sb_logits(qb, kb2, mask2):
    z = _dot(qb, kb2, NT)
    l1p = jnp.log(1.0 + jnp.exp(-jnp.abs(z)))
    lsp = jnp.minimum(z, 0.0) - l1p
    lsn = lsp - z
    if mask2 is not None:
        lsn = jnp.where(mask2, lsn, 0.0)
    return lsp, lsn


def _sb_tile_mask(qt):
    trow = lax.broadcasted_iota(jnp.int32, (qt, 256), 0)
    tlane = lax.broadcasted_iota(jnp.int32, (qt, 256), 1) & 127
    return lambda m: (tlane + m * CHUNK) < trow


def sb_fwd(proj, *, name, job=None):
    T = proj.shape[0]
    qt = min(SB_QT, T)
    nsub = qt // CHUNK
    cb = C_SB // 128

    def body(q_ref, k_ref, v_ref, o_ref):
        u_gt, _, hms = _sb_consts()
        tile_mask = _sb_tile_mask(qt)

        def qtile(i, _):
            rq = pl.ds(pl.multiple_of(i * qt, qt), qt)
            qb = (q_ref[rq, :] * SB_SCALE).astype(BF16)

            def group(js, masks, state):
                carry, acc = list(state[:2]), state[2]
                rows = [pl.ds(pl.multiple_of(j * CHUNK, CHUNK), CHUNK) for j in js]
                logits = [_sb_logits(qb, _per_head(k_ref[rk, :], hms), m) for rk, m in zip(rows, masks)]
                sums = [[_dot(_pair(lsn[:, h * 128:(h + 1) * 128]), u_gt, NN) for h in range(2)] for _, lsn in logits]
                weights = []
                for (lsp, _), r, m in zip(logits, sums, masks):
                    a_b = []
                    for h in range(2):
                        hc = slice(h * 128, (h + 1) * 128)
                        a = jnp.exp(lsp[:, hc] + r[h][:, :128] + carry[h])
                        if m is not None:
                            a = jnp.where(m[:, hc], a, 0.0)
                        carry[h] = carry[h] + r[h][:, 128:]
                        a_b.append(a.astype(BF16))
                    weights.append(jnp.concatenate(a_b, axis=1))
                for rk, a in zip(rows, weights):
                    acc = acc + _dot(a, _per_head(v_ref[rk, :], hms), NN)
                return carry[0], carry[1], acc

            zero = jnp.zeros((qt, 128), F32)
            diag = list(reversed(range(nsub)))
            state = group([i * nsub + m for m in diag], [tile_mask(m) for m in diag], (zero, zero, zero))

            def live(c):
                return jnp.logical_and(c[0] < i, jnp.maximum(jnp.max(c[1][0]), jnp.max(c[1][1])) > SB_DEAD)

            def blocks(c):
                jj, st = c
                return jj + 1, group([(i - jj) * nsub - 1 - u for u in range(nsub)], [None] * nsub, st)

            _, state = lax.while_loop(live, blocks, (jnp.int32(0), state))
            o_ref[rq, :] = state[2]
            return 0

        lax.fori_loop(0, T // qt, qtile, 0)

    def col(off):
        return pl.BlockSpec((T, 128), lambda hp: (0, off + hp))

    steps = BRANCH_W // 128
    j = _job_args(job, 3, 1)
    res = pl.pallas_call(
        _hosting(body, job, 3, 1, 0, steps), name=name, grid=(steps,),
        in_specs=[col(cb), col(cb + 4), col(cb + 8)] + j["in_specs"], out_specs=[col(0)] + j["out_specs"],
        out_shape=[jax.ShapeDtypeStruct((T, BRANCH_W), F32)] + j["out_shape"],
        scratch_shapes=j["scratch"], input_output_aliases=j["aliases"],
        compiler_params=_params(("parallel",) if job is None else ("arbitrary",)),
    )(proj, proj, proj, *j["ins"])
    return res[0], list(res[1:])


def sb_bwd(proj, out, dout, *, name, job=None):
    T = proj.shape[0]
    qt = min(SB_QT, T)
    nsub = qt // CHUNK
    cb = C_SB // 128

    def body(q_ref, k_ref, v_ref, o_ref, do_ref, dq_ref, dk_ref, dv_ref, dkt_ref, dvt_ref):
        u_gt, u_ge, hms = _sb_consts()
        tile_mask = _sb_tile_mask(qt)
        tall_lane = lax.broadcasted_iota(jnp.int32, (qt, 128), 1)
        top = lax.broadcasted_iota(jnp.int32, (CHUNK, CHUNK), 0) < 64
        dkt_ref[...] = jnp.zeros_like(dkt_ref)
        dvt_ref[...] = jnp.zeros_like(dvt_ref)

        def qtile(i, _):
            rq = pl.ds(pl.multiple_of(i * qt, qt), qt)
            qs = q_ref[rq, :] * SB_SCALE
            qb, q_t = qs.astype(BF16), qs.T.astype(BF16)
            dov = do_ref[rq, :]
            dob, do_t = dov.astype(BF16), dov.T.astype(BF16)
            prod = dob.astype(F32) * o_ref[rq, :]
            total = [jnp.broadcast_to(jnp.sum(jnp.where(hm, prod, 0.0), axis=1, keepdims=True), (qt, 128))
                     for hm in (tall_lane < 64, tall_lane >= 64)]

            def group(js, masks, state):
                c_l, c_w, dq = list(state[:2]), list(state[2:4]), state[4]
                heads = [slice(h * 128, (h + 1) * 128) for h in range(2)]
                rows = [pl.ds(pl.multiple_of(j * CHUNK, CHUNK), CHUNK) for j in js]
                kb2 = [_per_head(k_ref[rk, :], hms) for rk in rows]
                logits = [_sb_logits(qb, kb, m) for kb, m in zip(kb2, masks)]
                da = [_dot(dob, _per_head(v_ref[rk, :], hms), NT) for rk in rows]
                sums = [[_dot(_pair(lsn[:, hc]), u_gt, NN) for hc in heads] for _, lsn in logits]
                a_b, w_all = [], []
                for (lsp, _), r, d, m in zip(logits, sums, da, masks):
                    a_h, w_h = [], []
                    for h, hc in enumerate(heads):
                        a = jnp.exp(lsp[:, hc] + r[h][:, :128] + c_l[h])
                        if m is not None:
                            a = jnp.where(m[:, hc], a, 0.0)
                        c_l[h] = c_l[h] + r[h][:, 128:]
                        a = a.astype(BF16)
                        a_h.append(a)
                        w_h.append(a.astype(F32) * d[:, hc])
                    a_b.append(jnp.concatenate(a_h, axis=1))
                    w_all.append(w_h)
                sums_w = [[_dot(_pair(w), u_ge, NN) for w in w_h] for w_h in w_all]
                dz_b = []
                for (lsp, _), w_h, r, m in zip(logits, w_all, sums_w, masks):
                    sp = jnp.exp(lsp)
                    dz_h = []
                    for h, hc in enumerate(heads):
                        later_w = r[h][:, :128] + c_w[h]
                        c_w[h] = c_w[h] + r[h][:, 128:]
                        dz = w_h[h] * (1.0 - sp[:, hc]) - sp[:, hc] * (total[h] - later_w)
                        if m is not None:
                            dz = jnp.where(m[:, hc], dz, 0.0)
                        dz_h.append(dz.astype(BF16))
                    dz_b.append(jnp.concatenate(dz_h, axis=1))
                for j, kb, a, dz in zip(js, kb2, a_b, dz_b):
                    dkt = _dot(q_t, dz, NN)
                    dvt = _dot(do_t, a, NN)
                    dkt_ref[j] += jnp.where(top, dkt[:, :128], dkt[:, 128:])
                    dvt_ref[j] += jnp.where(top, dvt[:, :128], dvt[:, 128:])
                    dq = dq + _dot(dz, kb, NN)
                return c_l[0], c_l[1], c_w[0], c_w[1], dq

            zero = jnp.zeros((qt, 128), F32)
            diag = list(reversed(range(nsub)))
            state = group([i * nsub + m for m in diag], [tile_mask(m) for m in diag], (zero,) * 5)

            def live(c):
                return jnp.logical_and(c[0] < i, jnp.maximum(jnp.max(c[1][0]), jnp.max(c[1][1])) > SB_DEAD)

            def blocks(c):
                jj, st = c
                return jj + 1, group([(i - jj) * nsub - 1 - u for u in range(nsub)], [None] * nsub, st)

            _, state = lax.while_loop(live, blocks, (jnp.int32(0), state))
            dq_ref[rq, :] = (state[4] * SB_SCALE).astype(BF16)
            return 0

        lax.fori_loop(0, T // qt, qtile, 0)

        def untranspose(jb, _):
            rk = pl.ds(pl.multiple_of(jb * CHUNK, CHUNK), CHUNK)
            dk_ref[rk, :] = dkt_ref[jb].T.astype(BF16)
            dv_ref[rk, :] = dvt_ref[jb].T.astype(BF16)
            return 0

        lax.fori_loop(0, T // CHUNK, untranspose, 0, unroll=8)

    def col(off):
        return pl.BlockSpec((T, 128), lambda hp: (0, off + hp))

    o16 = jax.ShapeDtypeStruct((T, BRANCH_W), BF16)
    steps = BRANCH_W // 128
    j = _job_args(job, 5, 3)
    acc = pltpu.VMEM((T // CHUNK, CHUNK, CHUNK), F32)
    res = pl.pallas_call(
        _hosting(body, job, 5, 3, 2, steps), name=name, grid=(steps,),
        in_specs=[col(cb), col(cb + 4), col(cb + 8), col(0), col(0)] + j["in_specs"],
        out_specs=[col(0), col(0), col(0)] + j["out_specs"], out_shape=[o16, o16, o16] + j["out_shape"],
        scratch_shapes=[acc, acc] + j["scratch"], input_output_aliases=j["aliases"],
        compiler_params=_params(("parallel",) if job is None else ("arbitrary",)),
    )(proj, proj, proj, out, dout, *j["ins"])
    return res[0], res[1], res[2], list(res[3:])


_G0 = math.sqrt(2.0 / math.pi)
_G1 = 0.044715


def _gelu(x):
    return 0.5 * x * (1.0 + jnp.tanh(_G0 * (x + _G1 * x * x * x)))


def _gelu_grad(x):
    t = jnp.tanh(_G0 * (x + _G1 * x * x * x))
    return 0.5 * (1.0 + t) + 0.5 * x * (1.0 - t * t) * (_G0 * (1.0 + 3.0 * _G1 * x * x))


def _tril():
    row, lane = _sb_masks()
    return row >= lane


def sgu_fwd(proj, ln_g, ln_b, w, bias, *, name):
    T = proj.shape[0]
    tb = min(512, T)
    G = BRANCH_W // 128

    def body(u_ref, v_ref, g_ref, b_ref, w_ref, bias_ref, o_ref):
        vv = _gelu(v_ref[...])
        xh, _ = _group_norm(vv)
        vn = (xh * g_ref[...] + b_ref[...]).astype(BF16)
        tril = _tril()
        for g in range(G):
            wg = jnp.where(tril, w_ref[g], 0.0).astype(BF16)
            gc = slice(g * 128, (g + 1) * 128)
            for c in range(tb // CHUNK):
                r = slice(c * CHUNK, (c + 1) * CHUNK)
                sv = _dot(wg, vn[r, gc], NN) + bias_ref[g]
                o_ref[r, gc] = _gelu(u_ref[r, gc]) * sv

    cu, cv = C_SGU // BRANCH_W, C_SGU // BRANCH_W + 1
    vec = pl.BlockSpec((1, BRANCH_W), lambda i: (0, 0))
    mat = pl.BlockSpec((G, CHUNK, CHUNK), lambda i: (0, 0, 0))
    return pl.pallas_call(
        body, name=name, grid=(T // tb,),
        in_specs=[pl.BlockSpec((tb, BRANCH_W), lambda i: (i, cu)), pl.BlockSpec((tb, BRANCH_W), lambda i: (i, cv)),
                  vec, vec, mat, mat],
        out_specs=pl.BlockSpec((tb, BRANCH_W), lambda i: (i, 0)),
        out_shape=jax.ShapeDtypeStruct((T, BRANCH_W), F32),
        compiler_params=_params(("parallel",)),
    )(proj, proj, ln_g, ln_b, w, bias)


def sgu_bwd(proj, ln_g, ln_b, w, bias, dout, *, name):
    T = proj.shape[0]
    tb = min(512, T)
    G = BRANCH_W // 128

    def body(u_ref, v_ref, g_ref, b_ref, w_ref, bias_ref, do_ref, dp_ref, dw_ref, dbias_ref, dg_ref, db_ref, dvn_ref):
        @pl.when(pl.program_id(0) == 0)
        def _():
            dw_ref[...] = jnp.zeros_like(dw_ref)
            dbias_ref[...] = jnp.zeros_like(dbias_ref)
            dg_ref[...] = jnp.zeros_like(dg_ref)
            db_ref[...] = jnp.zeros_like(db_ref)

        gv = v_ref[...]
        vv = _gelu(gv)
        xh, rstd = _group_norm(vv)
        vn = (xh * g_ref[...] + b_ref[...]).astype(BF16)
        tril = _tril()
        for g in range(G):
            wg = jnp.where(tril, w_ref[g], 0.0).astype(BF16)
            gc = slice(g * 128, (g + 1) * 128)
            for c in range(tb // CHUNK):
                r = slice(c * CHUNK, (c + 1) * CHUNK)
                vn_c = vn[r, gc]
                sv = _dot(wg, vn_c, NN) + bias_ref[g]
                gu = u_ref[r, gc]
                d_o = do_ref[r, gc]
                dp_ref[r, gc] = (d_o * sv * _gelu_grad(gu)).astype(BF16)
                dsv = d_o * _gelu(gu)
                dsv_b = dsv.astype(BF16)
                dvn_ref[r, gc] = _dot(wg, dsv_b, TN)
                dw_ref[g] += jnp.where(tril, _dot(dsv_b, vn_c, NT), 0.0)
                dbias_ref[g] += jnp.broadcast_to(jnp.sum(dsv, axis=1, keepdims=True), (CHUNK, CHUNK))
        dvn = dvn_ref[...]
        dg_ref[...] += jnp.sum(dvn * xh, axis=0, keepdims=True)
        db_ref[...] += jnp.sum(dvn, axis=0, keepdims=True)
        dxh = dvn * g_ref[...]
        m1 = jnp.mean(dxh, axis=-1, keepdims=True)
        m2 = jnp.mean(dxh * xh, axis=-1, keepdims=True)
        dp_ref[:, BRANCH_W:2 * BRANCH_W] = (rstd * (dxh - m1 - xh * m2) * _gelu_grad(gv)).astype(BF16)

    cu, cv = C_SGU // BRANCH_W, C_SGU // BRANCH_W + 1
    vec = pl.BlockSpec((1, BRANCH_W), lambda i: (0, 0))
    mat = pl.BlockSpec((G, CHUNK, CHUNK), lambda i: (0, 0, 0))
    blk = pl.BlockSpec((tb, BRANCH_W), lambda i: (i, 0))
    msh = jax.ShapeDtypeStruct((G, CHUNK, CHUNK), F32)
    vsh = jax.ShapeDtypeStruct((1, BRANCH_W), F32)
    return pl.pallas_call(
        body, name=name, grid=(T // tb,),
        in_specs=[pl.BlockSpec((tb, BRANCH_W), lambda i: (i, cu)), pl.BlockSpec((tb, BRANCH_W), lambda i: (i, cv)),
                  vec, vec, mat, mat, blk],
        out_specs=[pl.BlockSpec((tb, 2 * BRANCH_W), lambda i: (i, 0)), mat, mat, vec, vec],
        out_shape=[jax.ShapeDtypeStruct((T, 2 * BRANCH_W), BF16), msh, msh, vsh, vsh],
        scratch_shapes=[pltpu.VMEM((tb, BRANCH_W), F32)],
        compiler_params=_params(("arbitrary",)),
    )(proj, proj, ln_g, ln_b, w, bias, dout)


def merge_fwd(a1, a2, a3, p1, p2, p3, proj, *, name):
    T = a1.shape[0]
    tm, tn = min(1024, T), 512
    gb = C_GATE // tn

    def body(a1_ref, a2_ref, a3_ref, p1_ref, p2_ref, p3_ref, g1_ref, g2_ref, g3_ref, m_ref, r1_ref, r2_ref, r3_ref):
        m = None
        for a_ref, p_ref, g_ref, r_ref in ((a1_ref, p1_ref, g1_ref, r1_ref), (a2_ref, p2_ref, g2_ref, r2_ref),
                                           (a3_ref, p3_ref, g3_ref, r3_ref)):
            r = _dot(a_ref[...].astype(BF16), p_ref[...], NN)
            r_ref[...] = r.astype(r_ref.dtype)
            t = jax.nn.sigmoid(g_ref[...]) * r
            m = t if m is None else m + t
        m_ref[...] = m.astype(m_ref.dtype)

    a_spec = pl.BlockSpec((tm, BRANCH_W), lambda i, j: (i, 0))
    p_spec = pl.BlockSpec((BRANCH_W, tn), lambda i, j: (0, j))
    o_spec = pl.BlockSpec((tm, tn), lambda i, j: (i, j))
    gates = [pl.BlockSpec((tm, tn), functools.partial(lambda i, j, o: (i, o + j), o=gb + 2 * n)) for n in range(3)]
    return pl.pallas_call(
        body, name=name, grid=(T // tm, D_MODEL // tn),
        in_specs=[a_spec, a_spec, a_spec, p_spec, p_spec, p_spec, *gates],
        out_specs=[o_spec] * 4, out_shape=[jax.ShapeDtypeStruct((T, D_MODEL), BF16)] * 4,
        compiler_params=_params(("parallel", "parallel")),
    )(a1, a2, a3, p1, p2, p3, proj, proj, proj)


def _merge_bwd_epi(dm, r1, r2, r3, g1, g2, g3):
    d_r, d_g = [], []
    for r, g in ((r1, g1), (r2, g2), (r3, g3)):
        s = jax.nn.sigmoid(g)
        d_r.append(dm * s)
        d_g.append(dm * r.astype(F32) * (s * (1.0 - s)))
    return (*d_r, *d_g)


def _rows_call(fn, ins, out_dtypes, *, name, tr=256, job=None):
    first = ins[0][0] if isinstance(ins[0], tuple) else ins[0]
    R, C = first.shape[-2:]
    tr = min(tr, R)
    assert R % tr == 0, (name, R, tr)
    arrs, specs = [], []
    for x in ins:
        if isinstance(x, tuple):
            arrs.append(x[0])
            specs.append(pl.BlockSpec((None, tr, C), functools.partial(lambda i, n: (n, i, 0), n=x[1])))
        else:
            arrs.append(x)
            specs.append(pl.BlockSpec((tr, C), lambda i: (i, 0)))
    ni = len(arrs)

    def body(*refs):
        vals = fn(*[r[...] for r in refs[:ni]])
        for o_ref, v in zip(refs[ni:], vals):
            o_ref[...] = v.astype(o_ref.dtype)

    no = len(out_dtypes)
    j = _job_args(job, ni, no)
    res = pl.pallas_call(
        _hosting(body, job, ni, no, 0, R // tr), name=name, grid=(R // tr,), in_specs=specs + j["in_specs"],
        out_specs=[pl.BlockSpec((tr, C), lambda i: (i, 0)) for _ in out_dtypes] + j["out_specs"],
        out_shape=[jax.ShapeDtypeStruct((R, C), dt) for dt in out_dtypes] + j["out_shape"],
        scratch_shapes=j["scratch"], input_output_aliases=j["aliases"],
        compiler_params=_params(("parallel",) if job is None else ("arbitrary",)),
    )(*arrs, *j["ins"])
    return list(res) if job is None else (list(res[:no]), list(res[no:]))


def _tile_rows(rows, cols):
    t = 256
    while t > 8 and (t * cols > 512 * 1024 or rows % t):
        t //= 2
    return t


def _rows_at(fn, pos, ins, outs, steps, *, name, aliases=None):
    read = [n for n, (_, s) in enumerate(ins) if s is not ANY]
    ni = len(ins)

    def body(pos_ref, *refs):
        vals = fn(*[refs[n][...] for n in read])
        for o_ref, v in zip(refs[ni:], vals):
            o_ref[...] = v.astype(o_ref.dtype)

    return pl.pallas_call(
        body, name=name,
        grid_spec=pltpu.PrefetchScalarGridSpec(num_scalar_prefetch=1, grid=(steps,), in_specs=[s for _, s in ins],
                                               out_specs=[s for _, s in outs]),
        out_shape=[sh for sh, _ in outs],
        input_output_aliases={1 + i: o for i, o in (aliases or {}).items()},
        compiler_params=_params(("parallel",)),
    )(pos, *[a for a, _ in ins])


def cast_into_whole(pos, w, l, axis, *, name):
    _, r, n = w.shape
    tr = _tile_rows(r, n)
    if axis == 1:
        shape, spec = (r, n * N_CHIPS), pl.BlockSpec((tr, n), lambda i, p: (i, p[3]))
    else:
        shape, spec = (r * N_CHIPS, n), pl.BlockSpec((tr, n), lambda i, p: (p[3] * (r // tr) + i, 0))
    return _rows_at(lambda a: (a,), pos, [(w, pl.BlockSpec((None, tr, n), lambda i, p: (l, i, 0)))],
                    [(jax.ShapeDtypeStruct(shape, BF16), spec)], r // tr, name=name)[0]


def pair_sum(pos, theirs, g32, axis, *, name):
    rows2, cols = theirs.shape
    h = rows2 // (N_CHIPS if axis == 0 else 1)
    tr = _tile_rows(h, cols)
    hb = h // tr
    if axis == 1:
        own = pl.BlockSpec((tr, cols), lambda i, p: (p[2] * hb + i, 0))
    else:
        own = pl.BlockSpec((tr, cols), lambda i, p: ((2 * (i // hb) + p[2]) * hb + i % hb, 0))
    row = pl.BlockSpec((tr, cols), lambda i, p: (i, 0))
    return _rows_at(lambda t, m: (m + t.astype(F32),) * 2, pos, [(theirs, row), (g32, own)],
                    [(jax.ShapeDtypeStruct((rows2, cols), F32), row), (jax.ShapeDtypeStruct((rows2, cols), BF16), row)],
                    rows2 // tr, name=name)


def chip_sum(pos, h32, recv, l, axis, whole, *, name):
    _, depth, h, n = recv.shape
    tr = _tile_rows(h, n)
    hb = h // tr
    if axis == 1:
        mine = pl.BlockSpec((tr, n), lambda i, p: (i, p[3]))
    else:
        mine = pl.BlockSpec((tr, n), lambda i, p: (p[3] * hb + i, 0))
    ins = [(h32, mine)] + [(recv, pl.BlockSpec((None, None, tr, n), functools.partial(lambda i, p, j: (j, l, i, 0), j=j)))
                           for j in range(3)]
    if whole is not None:
        ins.append((whole, ANY))
    return _rows_at(lambda o, a, b, c: (((o + a.astype(F32)) + b.astype(F32)) + c.astype(F32),), pos, ins,
                    [(jax.ShapeDtypeStruct((depth, 2, h, n), F32), pl.BlockSpec((None, None, tr, n), lambda i, p: (l, p[2], i, 0)))],
                    hb, name=name, aliases=None if whole is None else {4: 0})[0]


def _adamw(w, g, m, v):
    m2 = ADAM_B1 * m + (1.0 - ADAM_B1) * g
    v2 = ADAM_B2 * v + (1.0 - ADAM_B2) * (g * g)
    m_hat = m2 / (1.0 - ADAM_B1 ** ADAM_STEP)
    v_hat = v2 / (1.0 - ADAM_B2 ** ADAM_STEP)
    delta = -ADAM_LR * (m_hat / (jnp.sqrt(v_hat) + ADAM_EPS) + ADAM_WD * w)
    return delta, m2, v2


def _place():
    return lax.axis_index("x"), lax.axis_index("y"), lax.axis_index("c")


def _chip_peers(x, y, c):
    return [((1 - x, y, c), 2 * (1 - x) + y), ((x, 1 - y, c), 2 * x + 1 - y), ((1 - x, 1 - y, c), 2 * (1 - x) + 1 - y)]


def _shard_of(ref, axis, k, n):
    start = pl.multiple_of(k * n, 128)
    return ref.at[pl.ds(start, n), :] if axis == 0 else ref.at[:, pl.ds(start, n)]


ANY = pl.BlockSpec(memory_space=pl.ANY)


class CopyJob:
    def __init__(self, ins, out_shape, scratch, copies, aliases=None):
        self.ins, self.out_shape, self.scratch, self.copies = list(ins), list(out_shape), list(scratch), copies
        self.aliases = dict(aliases or {})

    def start(self, ins, outs, sems):
        local, remote, _, _ = self.copies(ins, outs, sems)
        for d in local + remote:
            d.start()

    def finish(self, ins, outs, sems):
        local, remote, arrivals, relays = self.copies(ins, outs, sems)
        for needs, sends, _ in relays:
            for d in needs:
                d.wait_recv()
            for d in sends:
                d.start()
        for d in arrivals + [d for _, _, arrives in relays for d in arrives]:
            d.wait_recv()
        for d in remote + [d for _, sends, _ in relays for d in sends]:
            d.wait_send()
        for d in local:
            d.wait()


def merge_jobs(a, b):
    ai, ao, asc = len(a.ins), len(a.out_shape), len(a.scratch)

    def copies(ins, outs, sems):
        ra = a.copies(ins[:ai], outs[:ao], sems[:asc])
        rb = b.copies(ins[ai:], outs[ao:], sems[asc:])
        return tuple(x + y for x, y in zip(ra, rb))

    aliases = dict(a.aliases)
    aliases.update({ai + i: ao + o for i, o in b.aliases.items()})
    return CopyJob(a.ins + b.ins, a.out_shape + b.out_shape, a.scratch + b.scratch, copies, aliases)


def run_job(job, *, name):
    ni, no = len(job.ins), len(job.out_shape)

    def body(*refs):
        parts = refs[:ni], refs[ni:ni + no], refs[ni + no:]
        job.start(*parts)
        job.finish(*parts)

    return pl.pallas_call(
        body, name=name, in_specs=[ANY] * ni, out_specs=[ANY] * no, out_shape=job.out_shape,
        scratch_shapes=job.scratch, input_output_aliases=job.aliases,
    )(*job.ins)


def _job_args(job, n_in, n_out):
    if job is None:
        return dict(ins=[], in_specs=[], out_specs=[], out_shape=[], scratch=[], aliases={})
    return dict(ins=job.ins, in_specs=[ANY] * len(job.ins), out_specs=[ANY] * len(job.out_shape),
                out_shape=job.out_shape, scratch=job.scratch,
                aliases={n_in + i: n_out + o for i, o in job.aliases.items()})


def _hosting(body, job, n_in, n_out, n_scratch, grid):
    if job is None:
        return body
    ji, jo = len(job.ins), len(job.out_shape)
    grid = (grid,) if isinstance(grid, int) else tuple(grid)

    def at(ends):
        hit = None
        for ax, e in enumerate(ends):
            here = pl.program_id(ax) == e
            hit = here if hit is None else jnp.logical_and(hit, here)
        return hit

    def hosted(*refs):
        o = n_in + ji
        s = o + n_out + jo
        parts = refs[n_in:o], refs[o + n_out:s], refs[s + n_scratch:]

        @pl.when(at([0] * len(grid)))
        def _():
            job.start(*parts)

        body(*refs[:n_in], *refs[o:o + n_out], *refs[s:s + n_scratch])

        @pl.when(at([g - 1 for g in grid]))
        def _():
            job.finish(*parts)

    return hosted


def _job_sems(n_remote, n_local):
    return [pltpu.SemaphoreType.DMA((n_remote,)), pltpu.SemaphoreType.DMA((n_remote,)), pltpu.SemaphoreType.DMA((n_local,))]


def gather_job(shards, axes, chips=(0, 1, 2)):
    na = len(shards)

    def copies(ins, outs, sems):
        send, recv, _ = sems
        x, y, c = _place()
        k = 2 * x + y
        remote, relays = [], []
        for a in range(na):
            r = outs[a].shape[0] // (N_CHIPS if axes[a] == 0 else 1)
            n = outs[a].shape[axes[a]] // N_CHIPS
            half = r // 2

            def part(kk, cc, a=a, n=n, half=half):
                rows = pl.ds(pl.multiple_of(cc * half + (kk * n if axes[a] == 0 else 0), 8), half)
                return outs[a].at[rows, :] if axes[a] == 0 else outs[a].at[rows, pl.ds(pl.multiple_of(kk * n, 128), n)]

            needs, passes, lands = [], [], []
            for j, (peer, kp) in enumerate(_chip_peers(x, y, c)):
                if j not in chips:
                    continue
                s = 6 * a + j
                remote.append(pltpu.make_async_remote_copy(part(k, c), part(k, c), send.at[s], recv.at[s],
                                                           device_id=peer, device_id_type=MESH))
                needs.append(pltpu.make_async_remote_copy(part(kp, c), part(kp, c), send.at[s], recv.at[s],
                                                          device_id=peer, device_id_type=MESH))
                passes.append(pltpu.make_async_remote_copy(part(kp, c), part(kp, c), send.at[s + 3], recv.at[s + 3],
                                                           device_id=(x, y, 1 - c), device_id_type=MESH))
                lands.append(pltpu.make_async_remote_copy(part(kp, 1 - c), part(kp, 1 - c), send.at[s + 3], recv.at[s + 3],
                                                          device_id=(x, y, 1 - c), device_id_type=MESH))
            relays.append((needs, passes, lands))
        return [], remote, [], relays

    out_shape = [jax.ShapeDtypeStruct(w.shape, BF16) for w in shards]
    return CopyJob(shards, out_shape, _job_sems(6 * na, 1), copies, {a: a for a in range(na)})


def scatter_job(layers, g16, axes, filled, chips=(0, 1, 2)):
    na = len(axes)

    def shard_shape(a):
        r, c = g16[a].shape
        return (r // N_CHIPS, c) if axes[a] == 0 else (r, c // N_CHIPS)

    def copies(ins, outs, sems):
        send, recv_sems, _ = sems
        x, y, c = _place()
        remote = []
        for a in range(na):
            n = shard_shape(a)[axes[a]]
            for r, (peer, kp) in enumerate(_chip_peers(x, y, c)):
                if r not in chips:
                    continue
                remote.append(pltpu.make_async_remote_copy(_shard_of(ins[a], axes[a], kp, n), outs[a].at[r, layers[a]],
                                                           send.at[3 * a + r], recv_sems.at[3 * a + r],
                                                           device_id=peer, device_id_type=MESH))
        return [], remote, remote, []

    out_shape = [jax.ShapeDtypeStruct((3, DEPTH) + shard_shape(a), BF16) for a in range(na)]
    ins = list(g16)
    aliases = {}
    for a in range(na):
        if filled[a] is not None:
            aliases[len(ins)] = a
            ins.append(filled[a])
    return CopyJob(ins, out_shape, _job_sems(3 * na, 1), copies, aliases)


def pair_job(g16, axes):
    na = len(axes)
    pieces = [1 if ax == 1 else N_CHIPS for ax in axes]

    def copies(ins, outs, sems):
        send, recv, _ = sems
        x, y, c = _place()
        remote = []
        s = 0
        for a in range(na):
            rows = g16[a].shape[0] // (2 * pieces[a])
            for kk in range(pieces[a]):
                src = ins[a].at[pl.ds(pl.multiple_of((2 * kk + 1 - c) * rows, 8), rows), :]
                remote.append(pltpu.make_async_remote_copy(src, outs[a].at[pl.ds(kk * rows, rows), :], send.at[s], recv.at[s],
                                                           device_id=(x, y, 1 - c), device_id_type=MESH))
                s += 1
        return [], remote, remote, []

    out_shape = [jax.ShapeDtypeStruct((g.shape[0] // 2, g.shape[1]), BF16) for g in g16]
    return CopyJob(g16, out_shape, _job_sems(sum(pieces), 1), copies)


def join_job(shards):
    na = len(shards)

    def copies(ins, outs, sems):
        send, recv, _ = sems
        x, y, c = _place()
        remote = [pltpu.make_async_remote_copy(outs[a].at[:, c], outs[a].at[:, c], send.at[a], recv.at[a],
                                               device_id=(x, y, 1 - c), device_id_type=MESH) for a in range(na)]
        lands = [pltpu.make_async_remote_copy(outs[a].at[:, 1 - c], outs[a].at[:, 1 - c], send.at[a], recv.at[a],
                                              device_id=(x, y, 1 - c), device_id_type=MESH) for a in range(na)]
        return [], remote, lands, []

    out_shape = [jax.ShapeDtypeStruct(s.shape, F32) for s in shards]
    return CopyJob(shards, out_shape, _job_sems(na, 1), copies, {a: a for a in range(na)})


def small_job(p):
    def copies(ins, outs, sems):
        send, recv, loc = sems
        x, y, c = _place()
        me = 4 * x + 2 * y + c
        remote, lands = [], []
        for rel in range(1, 8):
            dx, dy, dc = rel >> 2, (rel >> 1) & 1, rel & 1
            peer = (1 - x if dx else x, 1 - y if dy else y, 1 - c if dc else c)
            who = 4 * peer[0] + 2 * peer[1] + peer[2]
            remote.append(pltpu.make_async_remote_copy(ins[0], outs[0].at[me], send.at[rel - 1], recv.at[rel - 1],
                                                       device_id=peer, device_id_type=MESH))
            lands.append(pltpu.make_async_remote_copy(ins[0], outs[0].at[who], send.at[rel - 1], recv.at[rel - 1],
                                                      device_id=peer, device_id_type=MESH))
        return [pltpu.make_async_copy(ins[0], outs[0].at[me], loc.at[0])], remote, lands, []

    return CopyJob([p], [jax.ShapeDtypeStruct((8,) + p.shape, F32)], _job_sems(7, 1), copies)


def small_sum(slots):
    def add(*terms):
        acc = terms[0]
        for t in terms[1:]:
            acc = acc + t
        return (acc,)

    return _rows_call(add, [(slots, d) for d in range(8)], [F32], name="small_sum", tr=8 * 47)[0]


BIG = ("w_in", "p_ret", "p_sb", "p_sgu", "w_out", "w_up", "w_down")
BIG_AXIS = {"w_in": 1, "p_ret": 1, "p_sb": 1, "p_sgu": 1, "w_out": 0, "w_up": 1, "w_down": 0}
SMALL = ("ret_gn_g", "ret_gn_b", "sgu_ln_g", "sgu_ln_b", "sgu_w", "sgu_b", "ln1_g", "ln1_b", "ln2_g", "ln2_b")


def layer_forward(l, x0, x0h, W, sm, rope, rconsts, hooks):
    n = f"l{l}_"
    job = hooks.fwd_job(l, "proj")
    proj = matmul(x0h, W["w_in"], mode="nn", tm=4096, tn=768, tk=1024, name=n + "proj", job=job)
    if job is not None:
        proj, job_out = proj
        hooks.done(job, job_out)
    retg, raw, states = ret_fwd(proj, *rope, rconsts, sm["ret_gn_g"], sm["ret_gn_b"], name=n + "ret_fwd")
    job = hooks.fwd_job(l, "sb")
    sb, job_out = sb_fwd(proj, name=n + "sb_fwd", job=job)
    if job is not None:
        hooks.done(job, job_out)
    sg = sgu_fwd(proj, sm["sgu_ln_g"], sm["sgu_ln_b"], sm["sgu_w"], sm["sgu_bias"], name=n + "sgu_fwd")
    merged, r1, r2, r3 = merge_fwd(retg, sb, sg, W["p_ret"], W["p_sb"], W["p_sgu"], proj, name=n + "merge_fwd")
    x1, xh1, rs1, x1h = matmul_ln(merged, W["w_out"], x0, sm["ln1_g"], sm["ln1_b"], tk=1024, name=n + "out_ln1")
    job = hooks.fwd_job(l, "up")
    h1 = matmul(x1h, W["w_up"], mode="nn", tm=2048, tn=1024, tk=1024, outs=((BF16, None),), name=n + "up", job=job)
    if job is not None:
        h1, job_out = h1
        hooks.done(job, job_out)
    job = hooks.fwd_job(l, "down")
    res = matmul_ln(h1, W["w_down"], x1, sm["ln2_g"], sm["ln2_b"], pro=_relu2, tk=1024, name=n + "down_ln2", job=job)
    if job is not None:
        res, job_out = res
        hooks.done(job, job_out)
    x2, xh2, rs2, x2h = res
    saved = dict(x0h=x0h, proj=proj, retg=retg, raw=raw, states=states, sb=sb, sg=sg, merged=merged, r=(r1, r2, r3),
                 x1h=x1h, xh1=xh1, rs1=rs1, h1=h1, xh2=xh2, rs2=rs2)
    return x2, x2h, saved


def layer_backward(l, dx2, s, W, sm, rope, rconsts, hooks):
    n = f"l{l}_"
    two = ((F32, None), (BF16, None))
    gw, gs = {}, {}
    job = hooks.bwd_job(l, "ln2")
    res = ln_bwd(dx2, s["xh2"], s["rs2"], sm["ln2_g"], name=n + "ln2_bwd", job=job)
    if job is not None:
        res, job_out = res
        hooks.done(job, job_out)
    du2, du2h, gs["ln2_g"], gs["ln2_b"] = res
    job = hooks.bwd_job(l, "g_down")
    gw["w_down"] = matmul(s["h1"], du2h, mode="tn", tm=1024, tn=1024, tk=4096, pro=_relu2, outs=two, name=n + "g_down", job=job)
    if job is not None:
        gw["w_down"], job_out = gw["w_down"]
        hooks.done(job, job_out)
    dh1 = matmul(du2h, W["w_down"], mode="nt", tm=2048, tn=1024, tk=1024, outs=((BF16, None),),
                 epi=lambda acc, h: (acc * (2.0 * jnp.maximum(h.astype(F32), 0.0)),), tiles=(s["h1"],), name=n + "d_h1")
    job = hooks.bwd_job(l, "g_up")
    gw["w_up"] = matmul(s["x1h"], dh1, mode="tn", tm=1024, tn=1024, tk=4096, outs=two, name=n + "g_up", job=job)
    if job is not None:
        gw["w_up"], job_out = gw["w_up"]
        hooks.done(job, job_out)
    dx1 = matmul(dh1, W["w_up"], mode="nt", tm=1024, tn=1024, tk=4096,
                 epi=lambda acc, d: (acc + ALPHA * d,), tiles=(du2,), name=n + "d_x1")
    du1, du1h, gs["ln1_g"], gs["ln1_b"] = ln_bwd(dx1, s["xh1"], s["rs1"], sm["ln1_g"], name=n + "ln1_bwd")
    gw["w_out"] = matmul(s["merged"], du1h, mode="tn", tm=1024, tn=1024, tk=4096, outs=two, name=n + "g_out")
    gate0 = C_GATE // 512
    dr1, dr2, dr3, dg1, dg2, dg3 = matmul(
        du1h, W["w_out"], mode="nt", tm=1024, tn=512, tk=1024, outs=((BF16, None),) * 6, epi=_merge_bwd_epi,
        tiles=(*s["r"], (s["proj"], gate0), (s["proj"], gate0 + 2), (s["proj"], gate0 + 4)), name=n + "d_merged")
    d_branch = {}
    for nm, a, dr in (("p_ret", s["retg"], dr1), ("p_sb", s["sb"], dr2), ("p_sgu", s["sg"], dr3)):
        gw[nm] = matmul(a, dr, mode="tn", tm=512, tn=1024, tk=2048, outs=two, name=n + "g_" + nm)
        d_branch[nm] = matmul(dr, W[nm], mode="nt", tm=1024, tn=512, tk=1024, name=n + "d_" + nm)
    job = hooks.pair(l, gw)
    dret, gs["ret_gn_g"], gs["ret_gn_b"], job_out = ret_bwd(s["proj"], *rope, rconsts, sm["ret_gn_g"], sm["ret_gn_b"],
                                                             s["raw"], s["states"], d_branch["p_ret"], name=n + "ret_bwd", job=job)
    if job is not None:
        hooks.done(job, job_out)
    job = hooks.scatter(l) if job is not None else None
    dsq, dsk, dsv, job_out = sb_bwd(s["proj"], s["sb"], d_branch["p_sb"], name=n + "sb_bwd", job=job)
    if job is not None:
        hooks.done(job, job_out)
    dsgu, gs["sgu_w"], dbias, gs["sgu_ln_g"], gs["sgu_ln_b"] = sgu_bwd(
        s["proj"], sm["sgu_ln_g"], sm["sgu_ln_b"], sm["sgu_w"], sm["sgu_bias"], d_branch["p_sgu"], name=n + "sgu_bwd")
    gs["sgu_b"] = dbias[:, :, 0]
    dproj = jnp.concatenate([dret, dsq, dsk, dsv, dsgu, dg1, dg2, dg3], axis=1)
    job = hooks.small(l, gs)
    gw["w_in"] = matmul(s["x0h"], dproj, mode="tn", tm=1024, tn=1536, tk=2048, outs=two, name=n + "g_in", job=job)
    if job is not None:
        gw["w_in"], job_out = gw["w_in"]
        hooks.done(job, job_out)
    job = hooks.tail(l, gw["w_in"])
    dx0 = matmul(dproj, W["w_in"], mode="nt", tm=1024, tn=1024, tk=2560,
                 epi=lambda acc, d: (acc + ALPHA * d,), tiles=(du1,), name=n + "d_x0", job=job)
    if job is not None:
        dx0, job_out = dx0
        hooks.done(job, job_out)
    return dx0, gw, gs


def local_step(x, target, small, plan):
    T = x.shape[0]
    rope = _rope_tables(T)
    rconsts = _ret_consts()
    sms = []
    for l in range(DEPTH):
        sm = {k: small[k][l][None, :] for k in SMALL if k not in ("sgu_w", "sgu_b")}
        sm["sgu_w"] = small["sgu_w"][l]
        sm["sgu_bias"] = jnp.broadcast_to(small["sgu_b"][l][:, :, None], (4, CHUNK, CHUNK))
        sms.append(sm)
    h, saved = x, []
    job = plan.first_job()
    hh = _rows_call(lambda a: (a,), [x], [BF16], name="cast_x", job=job)
    if job is not None:
        hh, job_out = hh
        plan.done(job, job_out)
    hh = hh[0]
    for l in range(DEPTH):
        h, hh, s = layer_forward(l, h, hh, plan.weights(l), sms[l], rope, rconsts, plan)
        saved.append(s)
    dy, sq = loss_head(h, target)
    gs = {k: [None] * DEPTH for k in SMALL}
    for l in reversed(range(DEPTH)):
        dy, gwl, gsl = layer_backward(l, dy, saved[l], plan.weights(l), sms[l], rope, rconsts, plan)
        plan.grads(l, gwl)
        for k in SMALL:
            gs[k][l] = gsl[k].reshape(small[k].shape[1:])
    return sq[0, 0], dy, {k: jnp.stack(v) for k, v in gs.items()}


EARLY_GRADS = ("p_ret", "p_sb", "p_sgu", "w_out", "w_up", "w_down")


class _StepPlan:
    def __init__(self, pos, shards16):
        self.pos = pos
        self.shards16 = shards16
        self.full = [dict() for _ in range(DEPTH)]
        self.gw = [None] * DEPTH
        self.bufs = {}
        self.sums = {}
        self.gs = [None] * DEPTH

    def first_job(self):
        return self._gather([(0, "w_in")])

    def weights(self, l):
        return self.full[l]

    def grads(self, l, gw):
        self.gw[l] = gw

    def _gather(self, items, chips=(0, 1, 2)):
        job = gather_job([self.shards16[l][k] for l, k in items], [BIG_AXIS[k] for _, k in items], chips)
        job.note = ("gather" if 2 in chips else "gather_part", items)
        return job

    def _pair(self, items):
        job = pair_job([g[1] for _, _, g in items], [BIG_AXIS[k] for _, k, _ in items])
        job.note = ("pair", items)
        return job

    def fwd_job(self, l, host):
        if host == "proj":
            return None
        if host == "sb":
            return self._gather([(l, k) for k in BIG[1:]])
        if l + 1 == DEPTH:
            return None
        return self._gather([(l + 1, "w_in")], (0, 1) if host == "up" else (2,))

    def bwd_job(self, l, host):
        if l + 1 == DEPTH:
            return None
        if host == "ln2":
            job = self._pair([(l + 1, "w_in", self.gw[l + 1]["w_in"])])
            job.note = ("pair_w_in", job.note[1])
            return job
        items, sums16 = self.summed_w_in
        job = scatter_job([l_ for l_, _, _ in items], sums16, [BIG_AXIS[k] for _, k, _ in items],
                          [self.bufs.get(k) for _, k, _ in items], (0, 1) if host == "g_down" else (2,))
        job.note = ("scatter", items)
        return job

    def pair(self, l, ready):
        return self._pair([(l, k, ready[k]) for k in EARLY_GRADS])

    def scatter(self, l):
        items, sums16 = self.summed
        job = scatter_job([l_ for l_, _, _ in items], sums16, [BIG_AXIS[k] for _, k, _ in items],
                          [self.bufs.get(k) for _, k, _ in items])
        job.note = ("scatter", items)
        return job

    def small(self, l, gs):
        self.gs[l] = {k: gs[k].reshape(-1) for k in SMALL}
        if l != 0:
            return None
        job = small_job(_pack_small({k: jnp.stack([self.gs[l_][k] for l_ in range(DEPTH)]) for k in SMALL}))
        shards = []
        for k in EARLY_GRADS:
            whole = None
            for l_ in range(DEPTH):
                whole = chip_sum(self.pos, self.sums[(l_, k)], self.bufs[k], l_, BIG_AXIS[k], whole, name=f"chip_sum_{k}_{l_}")
            shards.append(whole)
        job = merge_jobs(job, join_job(shards))
        job.note = ("small", [])
        return job

    def tail(self, l, g):
        if l != 0:
            return None
        last = self._pair([(0, "w_in", g)])
        self.done(last, run_job(last, name="pair_last"))
        return self.scatter(0)

    def done(self, job, outs):
        kind, items = job.note
        if kind == "small":
            self.small_slots = outs[0]
            self.joined = dict(zip(EARLY_GRADS, outs[1:]))
        if kind in ("pair", "pair_w_in"):
            sums16 = []
            for a, (l, k, g) in enumerate(items):
                self.sums[(l, k)], s16 = pair_sum(self.pos, outs[a], g[0], BIG_AXIS[k], name=f"pair_sum_{k}_{l}")
                sums16.append(s16)
            if kind == "pair":
                self.summed = (items, sums16)
            else:
                self.summed_w_in = (items, sums16)
        for a, item in enumerate(items):
            if kind == "gather_part":
                self.shards16[item[0]][item[1]] = outs[a]
            elif kind == "gather":
                self.full[item[0]][item[1]] = outs[a]
            elif kind == "scatter":
                self.bufs[item[1]] = outs[a]

    def finish(self):
        return self.bufs, self.sums


def _flat2(a):
    return a.reshape(-1, a.shape[-1])


def _pack_small(d, pre=""):
    return jnp.concatenate([d[pre + k].reshape(-1) for k in SMALL]).reshape(-1, 128)


def kernel(x, w_in, ret_gn_g, ret_gn_b, sgu_ln_g, sgu_ln_b, sgu_w, sgu_b, p_ret, p_sb, p_sgu, w_out, ln1_g, ln1_b, w_up, w_down, ln2_g, ln2_b, loss_target, m_w_in, m_ret_gn_g, m_ret_gn_b, m_sgu_ln_g, m_sgu_ln_b, m_sgu_w, m_sgu_b, m_p_ret, m_p_sb, m_p_sgu, m_w_out, m_ln1_g, m_ln1_b, m_w_up, m_w_down, m_ln2_g, m_ln2_b, v_w_in, v_ret_gn_g, v_ret_gn_b, v_sgu_ln_g, v_sgu_ln_b, v_sgu_w, v_sgu_b, v_p_ret, v_p_sb, v_p_sgu, v_w_out, v_ln1_g, v_ln1_b, v_w_up, v_w_down, v_ln2_g, v_ln2_b):
    given = dict(locals())
    order = BIG[:1] + SMALL[:6] + BIG[1:5] + SMALL[6:8] + BIG[5:7] + SMALL[8:10]
    L = DEPTH

    px, py, pc = _place()
    pos = jnp.stack([px, py, pc, 2 * px + py]).astype(jnp.int32)

    shards16 = [{k: cast_into_whole(pos, given[k], l, BIG_AXIS[k], name=f"cast_{k}_{l}") for k in BIG} for l in range(L)]
    plan = _StepPlan(pos, shards16)
    sq, dx, _ = local_step(x[0], loss_target[0], {k: given[k] for k in SMALL}, plan)
    loss = 0.5 * lax.psum(sq, ("x", "y", "c"))

    bufs, sums = plan.finish()
    whole = None
    for l in range(L):
        whole = chip_sum(pos, sums[(l, "w_in")], bufs["w_in"], l, BIG_AXIS["w_in"], whole, name=f"chip_sum_w_in_{l}")
    joined = dict(plan.joined, w_in=run_job(join_job([whole]), name="join_halves")[0])
    out = {}
    for k in BIG:
        shp = given[k].shape
        res = _rows_call(lambda g_, w_, m_, v_: (g_,) + _adamw(w_, g_, m_, v_),
                         [joined[k].reshape(-1, shp[-1]), _flat2(given[k]), _flat2(given["m_" + k]), _flat2(given["v_" + k])],
                         [F32] * 4, name="adamw_" + k)
        out[k] = [r.reshape(shp) for r in res]

    pack = _pack_small
    res = _rows_call(lambda g_, w_, m_, v_: (g_,) + _adamw(w_, g_, m_, v_),
                     [small_sum(plan.small_slots), pack(given), pack(given, "m_"), pack(given, "v_")], [F32] * 4,
                     name="adamw_small", tr=8 * 47)
    off = 0
    for k in SMALL:
        sz = given[k].size
        out[k] = [r.reshape(-1)[off:off + sz].reshape(given[k].shape) for r in res]
        off += sz

    grads = [out[k][0] for k in order]
    deltas = [out[k][1] for k in order]
    new_m = [out[k][2] for k in order]
    new_v = [out[k][3] for k in order]
    return (loss, dx[None], *grads, *deltas, *new_m, *new_v)
```
